```python
import jax, jax.numpy as jnp
from jax import lax
import numpy as np

D_MODEL = 1024
BATCH = 32
SEQ = 2048
DEPTH = 1

HEAD_DIM = 64
N_ATTN_HEADS = 12
D_ATTN = N_ATTN_HEADS * HEAD_DIM
DILATED_BRANCHES = ((128, 1), (512, 4), (2048, 16))
ATTN_BLOCK = 128
ROPE_THETA = 500000.0
ROPE_DIM = HEAD_DIM // 4
N_SSD_HEADS = 12
SSD_HEAD_DIM = 64
D_SSD = N_SSD_HEADS * SSD_HEAD_DIM
SSD_GROUPS = 4
SSD_HEADS_PER_GROUP = N_SSD_HEADS // SSD_GROUPS
SSD_STATE = 128
CONV_WIDTH = 4
SSD_CHUNK = 128
D_CONV = D_SSD + 2 * SSD_GROUPS * SSD_STATE
D_MIX = D_ATTN + D_SSD
D_IN_PROJ = 3 * D_ATTN + D_SSD + D_CONV + N_SSD_HEADS
D_FF = ((8 * D_MODEL // 3 + 255) // 256) * 256
ALPHA = (2.0 * DEPTH) ** 0.25
BETA = (8.0 * DEPTH) ** -0.25
LN_EPS = 1e-5
RMS_EPS = 1e-6

kernel_name = 'hybrid_ssd_dilated_attn_macaron_deepnorm'


def layer_norm(t, g, b):
    tf = t.astype(jnp.float32)
    mu = jnp.mean(tf, axis=-1, keepdims=True)
    var = jnp.mean(jnp.square(tf - mu), axis=-1, keepdims=True)
    return ((tf - mu) * lax.rsqrt(var + LN_EPS) * g + b).astype(t.dtype)


def rms_norm(t, w):
    tf = t.astype(jnp.float32)
    return (tf * lax.rsqrt(jnp.mean(tf * tf, axis=-1, keepdims=True) + RMS_EPS) * w).astype(t.dtype)


def swiglu(t, w_gate, w_up, w_down):
    return (jax.nn.silu(t @ w_gate) * (t @ w_up)) @ w_down


def rotary_tables(positions):
    inv_freq = ROPE_THETA ** (-jnp.arange(0, ROPE_DIM, 2, dtype=jnp.float32) / ROPE_DIM)
    ang = positions.astype(jnp.float32)[..., None] * inv_freq
    return jnp.cos(ang)[:, :, None, :], jnp.sin(ang)[:, :, None, :]


def partial_rope(t, cos, sin):
    half = ROPE_DIM // 2
    cos = cos.astype(t.dtype)
    sin = sin.astype(t.dtype)
    t1 = t[..., :half]
    t2 = t[..., half:ROPE_DIM]
    return jnp.concatenate([t1 * cos - t2 * sin, t2 * cos + t1 * sin, t[..., ROPE_DIM:]], axis=-1)


def banded_causal_attention(q, k, v, wr):
    n, L, h, dh = q.shape
    nblk = -(-L // ATTN_BLOCK)
    lp = nblk * ATTN_BLOCK
    qp = jnp.pad(q, ((0, 0), (0, lp - L), (0, 0), (0, 0)))
    kp = jnp.pad(k, ((0, 0), (wr, lp - L), (0, 0), (0, 0)))
    vp = jnp.pad(v, ((0, 0), (wr, lp - L), (0, 0), (0, 0)))
    scale = dh ** -0.5
    q_off = jnp.arange(ATTN_BLOCK)
    k_off = jnp.arange(ATTN_BLOCK + wr)
    rel = q_off[:, None] + wr - k_off[None, :]
    band = (rel >= 0) & (rel <= wr)

    def one_block(i):
        start = i * ATTN_BLOCK
        qb = lax.dynamic_slice_in_dim(qp, start, ATTN_BLOCK, axis=1)
        kb = lax.dynamic_slice_in_dim(kp, start, ATTN_BLOCK + wr, axis=1)
        vb = lax.dynamic_slice_in_dim(vp, start, ATTN_BLOCK + wr, axis=1)
        m_k = start - wr + k_off
        mask = band & (m_k >= 0)[None, :]
        s = jnp.einsum('nqhd,nkhd->nhqk', qb, kb).astype(jnp.float32) * scale
        s = jnp.where(mask, s, -jnp.inf)
        lse = jax.nn.logsumexp(s, axis=-1)
        p = jnp.exp(s - lse[..., None])
        o = jnp.einsum('nhqk,nkhd->nqhd', p, vb.astype(jnp.float32))
        return o, jnp.transpose(lse, (0, 2, 1))

    o, lse = lax.map(one_block, jnp.arange(nblk))
    o = jnp.transpose(o, (1, 0, 2, 3, 4)).reshape(n, lp, h, dh)[:, :L]
    lse = jnp.transpose(lse, (1, 0, 2, 3)).reshape(n, lp, h)[:, :L]
    return o, lse


def dilated_branch(q, k, v, window, dilation):
    b, s, h, dh = q.shape
    L = s // dilation

    def to_residue(t):
        return jnp.transpose(t.reshape(b, L, dilation, h, dh), (0, 2, 1, 3, 4)).reshape(b * dilation, L, h, dh)

    o, lse = banded_causal_attention(to_residue(q), to_residue(k), to_residue(v), window // dilation)
    o = jnp.transpose(o.reshape(b, dilation, L, h, dh), (0, 2, 1, 3, 4)).reshape(b, s, h, dh)
    lse = jnp.transpose(lse.reshape(b, dilation, L, h), (0, 2, 1, 3)).reshape(b, s, h)
    return o, lse


def dilated_attention_mixture(q, k, v):
    outs, lses = [], []
    for window, dilation in DILATED_BRANCHES:
        o, lse = dilated_branch(q, k, v, window, dilation)
        outs.append(o)
        lses.append(lse)
    w = jax.nn.softmax(jnp.stack(lses, axis=0), axis=0)
    return jnp.einsum('gbsh,gbshd->bshd', w, jnp.stack(outs, axis=0))


def causal_depthwise_conv(u, w, bias):
    c = u.shape[-1]
    out = lax.conv_general_dilated(u, w[:, None, :].astype(u.dtype), window_strides=(1,),
                                   padding=((CONV_WIDTH - 1, 0),),
                                   dimension_numbers=('NWC', 'WIO', 'NWC'),
                                   feature_group_count=c)
    return out + bias


def ssd_chunked(xdt, dA, Bm, Cm):
    b, s, g, j, p = xdt.shape
    n = Bm.shape[-1]
    nc = s // SSD_CHUNK
    cl = SSD_CHUNK
    xdt = xdt.astype(jnp.float32).reshape(b, nc, cl, g, j, p)
    Bc = Bm.astype(jnp.float32).reshape(b, nc, cl, g, n)
    Cc = Cm.astype(jnp.float32).reshape(b, nc, cl, g, n)
    a = jnp.transpose(dA.reshape(b, nc, cl, g, j), (0, 3, 4, 1, 2))
    a_cum = jnp.cumsum(a, axis=-1)
    tri = jnp.tril(jnp.ones((cl, cl), dtype=bool))
    seg = a_cum[..., :, None] - a_cum[..., None, :]
    Lmat = jnp.exp(jnp.where(tri, seg, -jnp.inf))
    CB = jnp.einsum('bclgn,bcsgn->bgcls', Cc, Bc)
    y_diag = jnp.einsum('bgcls,bgjcls,bcsgjp->bclgjp', CB, Lmat, xdt)
    decay_states = jnp.exp(a_cum[..., -1:] - a_cum)
    states = jnp.einsum('bclgn,bgjcl,bclgjp->bcgjpn', Bc, decay_states, xdt)
    chunk_decay = jnp.exp(a_cum[..., -1])

    def step(h, inp):
        st, dec = inp
        return dec[..., None, None] * h + st, h

    h0 = jnp.zeros((b, g, j, p, n), jnp.float32)
    _, prev = lax.scan(step, h0, (jnp.moveaxis(states, 1, 0), jnp.moveaxis(chunk_decay, -1, 0)))
    prev = jnp.moveaxis(prev, 0, 1)
    y_off = jnp.einsum('bclgn,bcgjpn,bgjcl->bclgjp', Cc, prev, jnp.exp(a_cum))
    return (y_diag + y_off).reshape(b, s, g, j, p)


def hybrid_mixer(h, cos, sin, w_in, conv_w, conv_b, dt_bias, a_log, d_skip, attn_norm_w, ssd_norm_w, w_out):
    b, s, _ = h.shape
    proj = h @ w_in
    cuts = [D_ATTN, 2 * D_ATTN, 3 * D_ATTN, 3 * D_ATTN + D_SSD, 3 * D_ATTN + D_SSD + D_CONV]
    q, k, v, z, xbc, dt = jnp.split(proj, cuts, axis=-1)
    q = partial_rope(q.reshape(b, s, N_ATTN_HEADS, HEAD_DIM), cos, sin)
    k = partial_rope(k.reshape(b, s, N_ATTN_HEADS, HEAD_DIM), cos, sin)
    v = v.reshape(b, s, N_ATTN_HEADS, HEAD_DIM)
    attn = dilated_attention_mixture(q, k, v).astype(h.dtype).reshape(b, s, D_ATTN)
    attn = rms_norm(attn, attn_norm_w)
    xbc = jax.nn.silu(causal_depthwise_conv(xbc, conv_w, conv_b))
    xs, Bm, Cm = jnp.split(xbc, [D_SSD, D_SSD + SSD_GROUPS * SSD_STATE], axis=-1)
    xs = xs.reshape(b, s, SSD_GROUPS, SSD_HEADS_PER_GROUP, SSD_HEAD_DIM)
    Bm = Bm.reshape(b, s, SSD_GROUPS, SSD_STATE)
    Cm = Cm.reshape(b, s, SSD_GROUPS, SSD_STATE)
    dt = jax.nn.softplus(dt.astype(jnp.float32) + dt_bias).reshape(b, s, SSD_GROUPS, SSD_HEADS_PER_GROUP)
    A = -jnp.exp(a_log.astype(jnp.float32)).reshape(SSD_GROUPS, SSD_HEADS_PER_GROUP)
    y = ssd_chunked(xs.astype(jnp.float32) * dt[..., None], dt * A, Bm, Cm)
    y = y + d_skip.reshape(SSD_GROUPS, SSD_HEADS_PER_GROUP)[..., None] * xs
    y = y.astype(h.dtype).reshape(b, s, D_SSD)
    y = rms_norm(y * jax.nn.silu(z), ssd_norm_w)
    return jnp.concatenate([attn, y], axis=-1) @ w_out


def _fwd_setup_inputs(seed: int = 0) -> dict:
    key = jax.random.key(seed)
    ks = jax.random.split(key, 32)
    f32 = jnp.float32
    nrm = lambda k, shape, std: jax.random.normal(k, shape, f32) * std
    x = jax.random.normal(ks[0], (BATCH, SEQ, D_MODEL), f32)
    positions = (jnp.arange(SEQ, dtype=jnp.int32)[None, :]
                 + jax.random.randint(ks[1], (BATCH, 1), 0, 4096, dtype=jnp.int32))
    col_scale = jnp.concatenate([jnp.ones((2 * D_ATTN,), f32), jnp.full((D_ATTN,), BETA, f32),
                                 jnp.ones((D_SSD + D_CONV,), f32), jnp.full((N_SSD_HEADS,), 0.1, f32)])
    w_in = nrm(ks[2], (DEPTH, D_MODEL, D_IN_PROJ), D_MODEL ** -0.5) * col_scale
    conv_w = nrm(ks[3], (DEPTH, CONV_WIDTH, D_CONV), CONV_WIDTH ** -0.5)
    conv_b = nrm(ks[4], (DEPTH, D_CONV), 0.01)
    dt0 = jnp.exp(jax.random.uniform(ks[5], (DEPTH, N_SSD_HEADS), f32, np.log(1e-3), np.log(1e-1)))
    dt_bias = dt0 + jnp.log(-jnp.expm1(-dt0))
    a_log = jnp.log(jax.random.uniform(ks[6], (DEPTH, N_SSD_HEADS), f32, 1.0, 16.0))
    d_skip = 1.0 + nrm(ks[7], (DEPTH, N_SSD_HEADS), 0.01)
    attn_norm_w = 1.0 + nrm(ks[8], (DEPTH, D_ATTN), 0.01)
    ssd_norm_w = 1.0 + nrm(ks[9], (DEPTH, D_SSD), 0.01)
    w_out = nrm(ks[10], (DEPTH, D_MIX, D_MODEL), BETA * D_MIX ** -0.5)

    def ffn(k0, k1, k2):
        return (nrm(k0, (DEPTH, D_MODEL, D_FF), D_MODEL ** -0.5),
                nrm(k1, (DEPTH, D_MODEL, D_FF), BETA * D_MODEL ** -0.5),
                nrm(k2, (DEPTH, D_FF, D_MODEL), BETA * D_FF ** -0.5))

    ffn1_gate, ffn1_up, ffn1_down = ffn(ks[11], ks[12], ks[13])
    ffn2_gate, ffn2_up, ffn2_down = ffn(ks[14], ks[15], ks[16])
    gain = lambda k: 1.0 + nrm(k, (DEPTH, D_MODEL), 0.01)
    bias = lambda k: nrm(k, (DEPTH, D_MODEL), 0.01)
    return {'x': x, 'positions': positions,
            'ln1_g': gain(ks[17]), 'ln1_b': bias(ks[18]),
            'ffn1_gate': ffn1_gate, 'ffn1_up': ffn1_up, 'ffn1_down': ffn1_down,
            'w_in': w_in, 'conv_w': conv_w, 'conv_b': conv_b, 'dt_bias': dt_bias, 'a_log': a_log,
            'd_skip': d_skip, 'attn_norm_w': attn_norm_w, 'ssd_norm_w': ssd_norm_w, 'w_out': w_out,
            'ln2_g': gain(ks[19]), 'ln2_b': bias(ks[20]),
            'ffn2_gate': ffn2_gate, 'ffn2_up': ffn2_up, 'ffn2_down': ffn2_down,
            'ln3_g': gain(ks[21]), 'ln3_b': bias(ks[22])}


def _fwd_reference(x, positions, ln1_g, ln1_b, ffn1_gate, ffn1_up, ffn1_down, w_in, conv_w, conv_b,
              dt_bias, a_log, d_skip, attn_norm_w, ssd_norm_w, w_out, ln2_g, ln2_b,
              ffn2_gate, ffn2_up, ffn2_down, ln3_g, ln3_b):
    cos, sin = rotary_tables(positions)
    h = x
    for l in range(DEPTH):
        h = layer_norm(ALPHA * h + 0.5 * swiglu(h, ffn1_gate[l], ffn1_up[l], ffn1_down[l]), ln1_g[l], ln1_b[l])
        mix = hybrid_mixer(h, cos, sin, w_in[l], conv_w[l], conv_b[l], dt_bias[l], a_log[l], d_skip[l],
                           attn_norm_w[l], ssd_norm_w[l], w_out[l])
        h = layer_norm(ALPHA * h + mix, ln2_g[l], ln2_b[l])
        h = layer_norm(ALPHA * h + 0.5 * swiglu(h, ffn2_gate[l], ffn2_up[l], ffn2_down[l]), ln3_g[l], ln3_b[l])
    return h


import jax as _jax
import jax.numpy as _jnp

TWIN_FORMAT = 'train_step'
FWD_PARAMS = ['x', 'positions', 'ln1_g', 'ln1_b', 'ffn1_gate', 'ffn1_up', 'ffn1_down', 'w_in', 'conv_w', 'conv_b', 'dt_bias', 'a_log', 'd_skip', 'attn_norm_w', 'ssd_norm_w', 'w_out', 'ln2_g', 'ln2_b', 'ffn2_gate', 'ffn2_up', 'ffn2_down', 'ln3_g', 'ln3_b']
TWIN_WEIGHTS = ['ln1_g', 'ln1_b', 'ffn1_gate', 'ffn1_up', 'ffn1_down', 'w_in', 'conv_w', 'conv_b', 'dt_bias', 'a_log', 'd_skip', 'attn_norm_w', 'ssd_norm_w', 'w_out', 'ln2_g', 'ln2_b', 'ffn2_gate', 'ffn2_up', 'ffn2_down', 'ln3_g', 'ln3_b']
TWIN_DIFF_INPUT = 'x'
TWIN_INPUTS = ['x', 'positions', 'ln1_g', 'ln1_b', 'ffn1_gate', 'ffn1_up', 'ffn1_down', 'w_in', 'conv_w', 'conv_b', 'dt_bias', 'a_log', 'd_skip', 'attn_norm_w', 'ssd_norm_w', 'w_out', 'ln2_g', 'ln2_b', 'ffn2_gate', 'ffn2_up', 'ffn2_down', 'ln3_g', 'ln3_b', 'loss_target', 'm_ln1_g', 'm_ln1_b', 'm_ffn1_gate', 'm_ffn1_up', 'm_ffn1_down', 'm_w_in', 'm_conv_w', 'm_conv_b', 'm_dt_bias', 'm_a_log', 'm_d_skip', 'm_attn_norm_w', 'm_ssd_norm_w', 'm_w_out', 'm_ln2_g', 'm_ln2_b', 'm_ffn2_gate', 'm_ffn2_up', 'm_ffn2_down', 'm_ln3_g', 'm_ln3_b', 'v_ln1_g', 'v_ln1_b', 'v_ffn1_gate', 'v_ffn1_up', 'v_ffn1_down', 'v_w_in', 'v_conv_w', 'v_conv_b', 'v_dt_bias', 'v_a_log', 'v_d_skip', 'v_attn_norm_w', 'v_ssd_norm_w', 'v_w_out', 'v_ln2_g', 'v_ln2_b', 'v_ffn2_gate', 'v_ffn2_up', 'v_ffn2_down', 'v_ln3_g', 'v_ln3_b']
TWIN_OUTPUTS = ['loss', 'grad_x', 'grad_ln1_g', 'grad_ln1_b', 'grad_ffn1_gate', 'grad_ffn1_up', 'grad_ffn1_down', 'grad_w_in', 'grad_conv_w', 'grad_conv_b', 'grad_dt_bias', 'grad_a_log', 'grad_d_skip', 'grad_attn_norm_w', 'grad_ssd_norm_w', 'grad_w_out', 'grad_ln2_g', 'grad_ln2_b', 'grad_ffn2_gate', 'grad_ffn2_up', 'grad_ffn2_down', 'grad_ln3_g', 'grad_ln3_b', 'delta_ln1_g', 'delta_ln1_b', 'delta_ffn1_gate', 'delta_ffn1_up', 'delta_ffn1_down', 'delta_w_in', 'delta_conv_w', 'delta_conv_b', 'delta_dt_bias', 'delta_a_log', 'delta_d_skip', 'delta_attn_norm_w', 'delta_ssd_norm_w', 'delta_w_out', 'delta_ln2_g', 'delta_ln2_b', 'delta_ffn2_gate', 'delta_ffn2_up', 'delta_ffn2_down', 'delta_ln3_g', 'delta_ln3_b', 'new_m_ln1_g', 'new_m_ln1_b', 'new_m_ffn1_gate', 'new_m_ffn1_up', 'new_m_ffn1_down', 'new_m_w_in', 'new_m_conv_w', 'new_m_conv_b', 'new_m_dt_bias', 'new_m_a_log', 'new_m_d_skip', 'new_m_attn_norm_w', 'new_m_ssd_norm_w', 'new_m_w_out', 'new_m_ln2_g', 'new_m_ln2_b', 'new_m_ffn2_gate', 'new_m_ffn2_up', 'new_m_ffn2_down', 'new_m_ln3_g', 'new_m_ln3_b', 'new_v_ln1_g', 'new_v_ln1_b', 'new_v_ffn1_gate', 'new_v_ffn1_up', 'new_v_ffn1_down', 'new_v_w_in', 'new_v_conv_w', 'new_v_conv_b', 'new_v_dt_bias', 'new_v_a_log', 'new_v_d_skip', 'new_v_attn_norm_w', 'new_v_ssd_norm_w', 'new_v_w_out', 'new_v_ln2_g', 'new_v_ln2_b', 'new_v_ffn2_gate', 'new_v_ffn2_up', 'new_v_ffn2_down', 'new_v_ln3_g', 'new_v_ln3_b']
TWIN_LEAF_KINDS = {'loss': 'loss', 'grad_x': 'grad_x', 'grad_ln1_g': 'grad_w', 'grad_ln1_b': 'grad_w', 'grad_ffn1_gate': 'grad_w', 'grad_ffn1_up': 'grad_w', 'grad_ffn1_down': 'grad_w', 'grad_w_in': 'grad_w', 'grad_conv_w': 'grad_w', 'grad_conv_b': 'grad_w', 'grad_dt_bias': 'grad_w', 'grad_a_log': 'grad_w', 'grad_d_skip': 'grad_w', 'grad_attn_norm_w': 'grad_w', 'grad_ssd_norm_w': 'grad_w', 'grad_w_out': 'grad_w', 'grad_ln2_g': 'grad_w', 'grad_ln2_b': 'grad_w', 'grad_ffn2_gate': 'grad_w', 'grad_ffn2_up': 'grad_w', 'grad_ffn2_down': 'grad_w', 'grad_ln3_g': 'grad_w', 'grad_ln3_b': 'grad_w', 'delta_ln1_g': 'delta_w', 'delta_ln1_b': 'delta_w', 'delta_ffn1_gate': 'delta_w', 'delta_ffn1_up': 'delta_w', 'delta_ffn1_down': 'delta_w', 'delta_w_in': 'delta_w', 'delta_conv_w': 'delta_w', 'delta_conv_b': 'delta_w', 'delta_dt_bias': 'delta_w', 'delta_a_log': 'delta_w', 'delta_d_skip': 'delta_w', 'delta_attn_norm_w': 'delta_w', 'delta_ssd_norm_w': 'delta_w', 'delta_w_out': 'delta_w', 'delta_ln2_g': 'delta_w', 'delta_ln2_b': 'delta_w', 'delta_ffn2_gate': 'delta_w', 'delta_ffn2_up': 'delta_w', 'delta_ffn2_down': 'delta_w', 'delta_ln3_g': 'delta_w', 'delta_ln3_b': 'delta_w', 'new_m_ln1_g': 'new_m', 'new_m_ln1_b': 'new_m', 'new_m_ffn1_gate': 'new_m', 'new_m_ffn1_up': 'new_m', 'new_m_ffn1_down': 'new_m', 'new_m_w_in': 'new_m', 'new_m_conv_w': 'new_m', 'new_m_conv_b': 'new_m', 'new_m_dt_bias': 'new_m', 'new_m_a_log': 'new_m', 'new_m_d_skip': 'new_m', 'new_m_attn_norm_w': 'new_m', 'new_m_ssd_norm_w': 'new_m', 'new_m_w_out': 'new_m', 'new_m_ln2_g': 'new_m', 'new_m_ln2_b': 'new_m', 'new_m_ffn2_gate': 'new_m', 'new_m_ffn2_up': 'new_m', 'new_m_ffn2_down': 'new_m', 'new_m_ln3_g': 'new_m', 'new_m_ln3_b': 'new_m', 'new_v_ln1_g': 'new_v', 'new_v_ln1_b': 'new_v', 'new_v_ffn1_gate': 'new_v', 'new_v_ffn1_up': 'new_v', 'new_v_ffn1_down': 'new_v', 'new_v_w_in': 'new_v', 'new_v_conv_w': 'new_v', 'new_v_conv_b': 'new_v', 'new_v_dt_bias': 'new_v', 'new_v_a_log': 'new_v', 'new_v_d_skip': 'new_v', 'new_v_attn_norm_w': 'new_v', 'new_v_ssd_norm_w': 'new_v', 'new_v_w_out': 'new_v', 'new_v_ln2_g': 'new_v', 'new_v_ln2_b': 'new_v', 'new_v_ffn2_gate': 'new_v', 'new_v_ffn2_up': 'new_v', 'new_v_ffn2_down': 'new_v', 'new_v_ln3_g': 'new_v', 'new_v_ln3_b': 'new_v'}


def _forward(args):
    return _fwd_reference(*[args[k] for k in FWD_PARAMS])


def _output_shape():
    out = _jax.eval_shape(lambda: _forward(_fwd_setup_inputs(0)))
    return out.shape, out.dtype

N_MICROBATCH = 1
ADAM_LR = 0.001
ADAM_B1 = 0.9
ADAM_B2 = 0.999
ADAM_EPS = 1e-08
ADAM_WD = 0.01
ADAM_STEP = 10
PER_EXAMPLE_BATCH_AXIS = {'x': 0, 'positions': 0, 'loss_target': 0}
SHARED_INPUTS = []
_WEIGHT_DTYPES = {'ln1_g': _jnp.float32, 'ln1_b': _jnp.float32, 'ffn1_gate': _jnp.float32, 'ffn1_up': _jnp.float32, 'ffn1_down': _jnp.float32, 'w_in': _jnp.float32, 'conv_w': _jnp.float32, 'conv_b': _jnp.float32, 'dt_bias': _jnp.float32, 'a_log': _jnp.float32, 'd_skip': _jnp.float32, 'attn_norm_w': _jnp.float32, 'ssd_norm_w': _jnp.float32, 'w_out': _jnp.float32, 'ln2_g': _jnp.float32, 'ln2_b': _jnp.float32, 'ffn2_gate': _jnp.float32, 'ffn2_up': _jnp.float32, 'ffn2_down': _jnp.float32, 'ln3_g': _jnp.float32, 'ln3_b': _jnp.float32}
MOMENT_SCALE = {'ln1_g': 5.419799e-01, 'ln1_b': 1.875846e+00, 'ffn1_gate': 1.593024e-02, 'ffn1_up': 2.595454e-02, 'ffn1_down': 4.274518e-02, 'w_in': 9.715231e-02, 'conv_w': 6.639054e-02, 'conv_b': 8.843053e-02, 'dt_bias': 2.548795e-01, 'a_log': 2.844928e-01, 'd_skip': 1.358744e+00, 'attn_norm_w': 9.457569e-02, 'ssd_norm_w': 9.467614e-02, 'w_out': 1.904289e-01, 'ln2_g': 7.402710e-01, 'ln2_b': 3.716784e-01, 'ffn2_gate': 1.379109e-02, 'ffn2_up': 2.241335e-02, 'ffn2_down': 3.722948e-02, 'ln3_g': 6.393834e+01, 'ln3_b': 2.966930e+00}


def _to_microbatches(a, axis):
    t = _jnp.moveaxis(a, axis, 0)
    t = t.reshape((N_MICROBATCH, t.shape[0] // N_MICROBATCH) + t.shape[1:])
    return _jnp.moveaxis(t, 1, axis + 1)


def setup_inputs(seed: int = 0) -> dict:
    inp = _fwd_setup_inputs(seed)
    key = _jax.random.fold_in(_jax.random.key(seed), 7919)
    shape, _ = _output_shape()
    out = dict(inp)
    out["loss_target"] = _jax.random.normal(_jax.random.fold_in(key, 0), shape, _jnp.float32)
    for i, name in enumerate(TWIN_WEIGHTS):
        w = inp[name].astype(_jnp.float32)
        if MOMENT_SCALE is None:
            s = _jnp.sqrt(_jnp.mean(_jnp.square(w)) + 1e-30)
        else:
            s = MOMENT_SCALE[name]
        km, kv = _jax.random.split(_jax.random.fold_in(key, i + 1))
        out[name] = w
        out["m_" + name] = s * _jax.random.normal(km, w.shape, _jnp.float32)
        out["v_" + name] = (s * s) * _jax.random.uniform(kv, w.shape, _jnp.float32, 0.5, 1.5)
    if N_MICROBATCH > 1:
        for name, axis in PER_EXAMPLE_BATCH_AXIS.items():
            out[name] = _to_microbatches(out[name], axis)
    return {'x': out['x'], 'positions': out['positions'], 'ln1_g': out['ln1_g'], 'ln1_b': out['ln1_b'], 'ffn1_gate': out['ffn1_gate'], 'ffn1_up': out['ffn1_up'], 'ffn1_down': out['ffn1_down'], 'w_in': out['w_in'], 'conv_w': out['conv_w'], 'conv_b': out['conv_b'], 'dt_bias': out['dt_bias'], 'a_log': out['a_log'], 'd_skip': out['d_skip'], 'attn_norm_w': out['attn_norm_w'], 'ssd_norm_w': out['ssd_norm_w'], 'w_out': out['w_out'], 'ln2_g': out['ln2_g'], 'ln2_b': out['ln2_b'], 'ffn2_gate': out['ffn2_gate'], 'ffn2_up': out['ffn2_up'], 'ffn2_down': out['ffn2_down'], 'ln3_g': out['ln3_g'], 'ln3_b': out['ln3_b'], 'loss_target': out['loss_target'], 'm_ln1_g': out['m_ln1_g'], 'm_ln1_b': out['m_ln1_b'], 'm_ffn1_gate': out['m_ffn1_gate'], 'm_ffn1_up': out['m_ffn1_up'], 'm_ffn1_down': out['m_ffn1_down'], 'm_w_in': out['m_w_in'], 'm_conv_w': out['m_conv_w'], 'm_conv_b': out['m_conv_b'], 'm_dt_bias': out['m_dt_bias'], 'm_a_log': out['m_a_log'], 'm_d_skip': out['m_d_skip'], 'm_attn_norm_w': out['m_attn_norm_w'], 'm_ssd_norm_w': out['m_ssd_norm_w'], 'm_w_out': out['m_w_out'], 'm_ln2_g': out['m_ln2_g'], 'm_ln2_b': out['m_ln2_b'], 'm_ffn2_gate': out['m_ffn2_gate'], 'm_ffn2_up': out['m_ffn2_up'], 'm_ffn2_down': out['m_ffn2_down'], 'm_ln3_g': out['m_ln3_g'], 'm_ln3_b': out['m_ln3_b'], 'v_ln1_g': out['v_ln1_g'], 'v_ln1_b': out['v_ln1_b'], 'v_ffn1_gate': out['v_ffn1_gate'], 'v_ffn1_up': out['v_ffn1_up'], 'v_ffn1_down': out['v_ffn1_down'], 'v_w_in': out['v_w_in'], 'v_conv_w': out['v_conv_w'], 'v_conv_b': out['v_conv_b'], 'v_dt_bias': out['v_dt_bias'], 'v_a_log': out['v_a_log'], 'v_d_skip': out['v_d_skip'], 'v_attn_norm_w': out['v_attn_norm_w'], 'v_ssd_norm_w': out['v_ssd_norm_w'], 'v_w_out': out['v_w_out'], 'v_ln2_g': out['v_ln2_g'], 'v_ln2_b': out['v_ln2_b'], 'v_ffn2_gate': out['v_ffn2_gate'], 'v_ffn2_up': out['v_ffn2_up'], 'v_ffn2_down': out['v_ffn2_down'], 'v_ln3_g': out['v_ln3_g'], 'v_ln3_b': out['v_ln3_b']}


def _loss(weights, diff, rest, loss_target):
    with _jax.named_scope("forward"):
        args = {**rest, TWIN_DIFF_INPUT: diff, **{k: w.astype(_WEIGHT_DTYPES[k]) for k, w in weights.items()}}
        y = _forward(args)
    with _jax.named_scope("loss_head"):
        err = _jnp.square(y.astype(_jnp.float32) - loss_target)
        return 0.5 * _jnp.sum(_jnp.mean(err, axis=-1)) if err.ndim else 0.5 * err


def _adamw(w, g, m, v):
    m = ADAM_B1 * m + (1.0 - ADAM_B1) * g
    v = ADAM_B2 * v + (1.0 - ADAM_B2) * _jnp.square(g)
    m_hat = m / (1.0 - ADAM_B1 ** ADAM_STEP)
    v_hat = v / (1.0 - ADAM_B2 ** ADAM_STEP)
    delta = -ADAM_LR * (m_hat / (_jnp.sqrt(v_hat) + ADAM_EPS) + ADAM_WD * w)
    return delta, m, v


def reference(x, positions, ln1_g, ln1_b, ffn1_gate, ffn1_up, ffn1_down, w_in, conv_w, conv_b, dt_bias, a_log, d_skip, attn_norm_w, ssd_norm_w, w_out, ln2_g, ln2_b, ffn2_gate, ffn2_up, ffn2_down, ln3_g, ln3_b, loss_target, m_ln1_g, m_ln1_b, m_ffn1_gate, m_ffn1_up, m_ffn1_down, m_w_in, m_conv_w, m_conv_b, m_dt_bias, m_a_log, m_d_skip, m_attn_norm_w, m_ssd_norm_w, m_w_out, m_ln2_g, m_ln2_b, m_ffn2_gate, m_ffn2_up, m_ffn2_down, m_ln3_g, m_ln3_b, v_ln1_g, v_ln1_b, v_ffn1_gate, v_ffn1_up, v_ffn1_down, v_w_in, v_conv_w, v_conv_b, v_dt_bias, v_a_log, v_d_skip, v_attn_norm_w, v_ssd_norm_w, v_w_out, v_ln2_g, v_ln2_b, v_ffn2_gate, v_ffn2_up, v_ffn2_down, v_ln3_g, v_ln3_b):
    given = dict(x=x, positions=positions, ln1_g=ln1_g, ln1_b=ln1_b, ffn1_gate=ffn1_gate, ffn1_up=ffn1_up, ffn1_down=ffn1_down, w_in=w_in, conv_w=conv_w, conv_b=conv_b, dt_bias=dt_bias, a_log=a_log, d_skip=d_skip, attn_norm_w=attn_norm_w, ssd_norm_w=ssd_norm_w, w_out=w_out, ln2_g=ln2_g, ln2_b=ln2_b, ffn2_gate=ffn2_gate, ffn2_up=ffn2_up, ffn2_down=ffn2_down, ln3_g=ln3_g, ln3_b=ln3_b, loss_target=loss_target, m_ln1_g=m_ln1_g, m_ln1_b=m_ln1_b, m_ffn1_gate=m_ffn1_gate, m_ffn1_up=m_ffn1_up, m_ffn1_down=m_ffn1_down, m_w_in=m_w_in, m_conv_w=m_conv_w, m_conv_b=m_conv_b, m_dt_bias=m_dt_bias, m_a_log=m_a_log, m_d_skip=m_d_skip, m_attn_norm_w=m_attn_norm_w, m_ssd_norm_w=m_ssd_norm_w, m_w_out=m_w_out, m_ln2_g=m_ln2_g, m_ln2_b=m_ln2_b, m_ffn2_gate=m_ffn2_gate, m_ffn2_up=m_ffn2_up, m_ffn2_down=m_ffn2_down, m_ln3_g=m_ln3_g, m_ln3_b=m_ln3_b, v_ln1_g=v_ln1_g, v_ln1_b=v_ln1_b, v_ffn1_gate=v_ffn1_gate, v_ffn1_up=v_ffn1_up, v_ffn1_down=v_ffn1_down, v_w_in=v_w_in, v_conv_w=v_conv_w, v_conv_b=v_conv_b, v_dt_bias=v_dt_bias, v_a_log=v_a_log, v_d_skip=v_d_skip, v_attn_norm_w=v_attn_norm_w, v_ssd_norm_w=v_ssd_norm_w, v_w_out=v_w_out, v_ln2_g=v_ln2_g, v_ln2_b=v_ln2_b, v_ffn2_gate=v_ffn2_gate, v_ffn2_up=v_ffn2_up, v_ffn2_down=v_ffn2_down, v_ln3_g=v_ln3_g, v_ln3_b=v_ln3_b)
    weights = {n: given[n] for n in TWIN_WEIGHTS}
    shared = {n: given[n] for n in SHARED_INPUTS}
    per_example = {n: given[n] for n in ['x', 'positions']}
    grad_fn = _jax.value_and_grad(_loss, argnums=(0, 1))

    def one_microbatch(ex, loss_target):
        ex = dict(ex)
        diff = ex.pop(TWIN_DIFF_INPUT)
        return grad_fn(weights, diff, {**shared, **ex}, loss_target)

    if N_MICROBATCH == 1:
        loss, (grad_w, grad_x) = one_microbatch(per_example, given["loss_target"])
    else:
        def body(carry, xs):
            loss_sum, grad_sum = carry
            l_k, (gw_k, gx_k) = one_microbatch(xs[0], xs[1])
            with _jax.named_scope("update"):
                return (loss_sum + l_k, _jax.tree.map(_jnp.add, grad_sum, gw_k)), gx_k

        init = (_jnp.zeros((), _jnp.float32), _jax.tree.map(_jnp.zeros_like, weights))
        (loss, grad_w), grad_x = _jax.lax.scan(body, init, (per_example, given["loss_target"]))
    with _jax.named_scope("update"):
        delta_w, new_m, new_v = {}, {}, {}
        for n in TWIN_WEIGHTS:
            delta_w[n], new_m[n], new_v[n] = _adamw(weights[n], grad_w[n], given["m_" + n], given["v_" + n])
    return (loss, grad_x, *[grad_w[n] for n in TWIN_WEIGHTS], *[delta_w[n] for n in TWIN_WEIGHTS],
            *[new_m[n] for n in TWIN_WEIGHTS], *[new_v[n] for n in TWIN_WEIGHTS])
```

```python
import functools
import math

import jax
import jax.numpy as jnp
import numpy as np
from jax import lax
from jax.experimental import pallas as pl
from jax.experimental.pallas import tpu as pltpu

F32, BF16 = jnp.float32, jnp.bfloat16
HI = lax.Precision.HIGHEST
MESH = pl.DeviceIdType.MESH
AXES = ("x", "y", "c")
N_DEV = 8

D_MODEL = 1024
SEQ = 2048
HEAD_DIM = 64
N_HEADS = 12
D_ATTN = N_HEADS * HEAD_DIM
DILATIONS = (1, 4, 16)
ATTN_BLOCK = 128
ROPE_THETA = 500000.0
ROPE_DIM = 16
D_SSD = 768
SSD_GROUPS = 4
SSD_STATE = 128
SSD_CHUNK = 128
D_BC = SSD_GROUPS * SSD_STATE
D_CONV = D_SSD + 2 * D_BC
CONV_WIDTH = 4
D_QKVZ = 3 * D_ATTN + D_SSD
D_IN_PROJ = D_QKVZ + D_CONV + N_HEADS
D_FF = 2816
ALPHA = 2.0 ** 0.25
LN_EPS = 1e-5
RMS_EPS = 1e-6
ADAM_LR, ADAM_B1, ADAM_B2, ADAM_EPS, ADAM_WD, ADAM_STEP = 0.001, 0.9, 0.999, 1e-08, 0.01, 10

LANES = 128
VMEM_LIMIT = 52 * 1024 * 1024
NEG = -1e30

WEIGHTS = ['ln1_g', 'ln1_b', 'ffn1_gate', 'ffn1_up', 'ffn1_down', 'w_in', 'conv_w', 'conv_b', 'dt_bias', 'a_log',
           'd_skip', 'attn_norm_w', 'ssd_norm_w', 'w_out', 'ln2_g', 'ln2_b', 'ffn2_gate', 'ffn2_up', 'ffn2_down',
           'ln3_g', 'ln3_b']
COL_SHARDED = ('ffn1_gate', 'ffn1_up', 'conv_w', 'ffn2_gate', 'ffn2_up')
ROW_SHARDED = ('ffn1_down', 'w_in', 'w_out', 'ffn2_down')
SHARDED = tuple(n for n in WEIGHTS if n in COL_SHARDED or n in ROW_SHARDED)
REPLICATED = tuple(n for n in WEIGHTS if n not in SHARDED)
PACK_ROWS_MULT = 512


def _cparams(sem=None):
    return pltpu.CompilerParams(dimension_semantics=sem, vmem_limit_bytes=VMEM_LIMIT)


def _tile(n, prefs):
    for p in prefs:
        if n % p == 0:
            return p
    return n


class Op:
    def __init__(self, arr, bw=None, cb=0, ro=0):
        self.arr, self.bw, self.cb, self.ro = arr, (arr.shape[1] if bw is None else bw), cb, ro


def _op(a):
    return a if isinstance(a, Op) else Op(a)


def rowwise(name, fn, ins, consts, outs, accs=(), tm=256):
    ins = [_op(a) for a in ins]
    rows = outs[0][0]
    n_in, n_c, n_o, n_a = len(ins), len(consts), len(outs), len(accs)
    tm = min(tm, rows)
    assert rows % tm == 0, (name, rows, tm)

    def body(*refs):
        vals = [r[...] for r in refs[:n_in + n_c]]
        res = fn(*vals)
        res = res if isinstance(res, (tuple, list)) else (res,)
        o_refs = refs[n_in + n_c:n_in + n_c + n_o]
        a_refs = refs[n_in + n_c + n_o:]
        for r, v in zip(o_refs, res[:n_o]):
            r[...] = v.astype(r.dtype)
        if n_a:
            @pl.when(pl.program_id(0) == 0)
            def _():
                for r in a_refs:
                    r[...] = jnp.zeros(r.shape, r.dtype)
            for r, v in zip(a_refs, res[n_o:]):
                r[...] += v

    in_specs = [pl.BlockSpec((tm, o.bw), functools.partial(lambda i, o: (i + o.ro, o.cb), o=o)) for o in ins]
    in_specs += [pl.BlockSpec(c.shape, functools.partial(lambda i, nd: (0,) * nd, nd=c.ndim)) for c in consts]
    out_specs = [pl.BlockSpec((tm, w), lambda i: (i, 0)) for (_, w, _) in outs]
    out_specs += [pl.BlockSpec(s, functools.partial(lambda i, nd: (0,) * nd, nd=len(s))) for s in accs]
    out_shape = [jax.ShapeDtypeStruct((r, w), dt) for (r, w, dt) in outs]
    out_shape += [jax.ShapeDtypeStruct(s, F32) for s in accs]
    res = pl.pallas_call(
        body, name=name, grid=(rows // tm,), in_specs=in_specs, out_specs=out_specs, out_shape=out_shape,
        compiler_params=_cparams(("arbitrary",) if n_a else ("parallel",)),
    )(*[o.arr for o in ins], *consts)
    return res


def mm(name, pairs, n_out, add=None, add_scale=1.0, tm=256, tn=None):
    ops = [_op(a) for a, _, _ in pairs]
    m = ops[0].arr.shape[0]
    tn = tn or _tile(n_out, (1408, 1024, 896, 768, 512, 256, 128))
    n_p = len(pairs)

    def body(*refs):
        acc = None
        for k in range(n_p):
            a = refs[2 * k][...].astype(BF16)
            b = refs[2 * k + 1][...].astype(BF16)
            dn = (((1,), (0,)), ((), ())) if pairs[k][2] == 'nn' else (((1,), (1,)), ((), ()))
            d = lax.dot_general(a, b, dn, preferred_element_type=F32)
            acc = d if acc is None else acc + d
        if add is not None:
            acc = acc + add_scale * refs[2 * n_p][...]
        refs[-1][...] = acc

    in_specs, args = [], []
    for o, (_, b, mode) in zip(ops, pairs):
        in_specs.append(pl.BlockSpec((tm, o.bw), functools.partial(lambda j, i, o: (i, o.cb), o=o)))
        args.append(o.arr)
        if mode == 'nn':
            assert b.shape == (o.bw, n_out), (name, b.shape, o.bw, n_out)
            in_specs.append(pl.BlockSpec((o.bw, tn), lambda j, i: (0, j)))
        else:
            assert b.shape == (n_out, o.bw), (name, b.shape, o.bw, n_out)
            in_specs.append(pl.BlockSpec((tn, o.bw), lambda j, i: (j, 0)))
        args.append(b)
    if add is not None:
        in_specs.append(pl.BlockSpec((tm, tn), lambda j, i: (i, j)))
        args.append(add)
    return pl.pallas_call(
        body, name=name, grid=(n_out // tn, m // tm), in_specs=in_specs,
        out_specs=pl.BlockSpec((tm, tn), lambda j, i: (i, j)),
        out_shape=jax.ShapeDtypeStruct((m, n_out), F32),
        compiler_params=_cparams(("parallel", "parallel")),
    )(*args)


def mm_tn(name, a, b, tt=512):
    a, b = _op(a), _op(b)
    t = a.arr.shape[0]
    k, n = a.bw, b.bw
    tk = _tile(k, (512, 896, 768, 256, 128))
    tn = _tile(n, (1408, 1024, 896, 768, 512, 256, 128))
    tt = min(tt, t)

    def body(a_ref, b_ref, o_ref):
        @pl.when(pl.program_id(2) == 0)
        def _():
            o_ref[...] = jnp.zeros(o_ref.shape, F32)
        o_ref[...] += lax.dot_general(a_ref[...].astype(BF16), b_ref[...].astype(BF16), (((0,), (0,)), ((), ())),
                                      preferred_element_type=F32)

    return pl.pallas_call(
        body, name=name, grid=(k // tk, n // tn, t // tt),
        in_specs=[pl.BlockSpec((tt, tk), functools.partial(lambda kk, nn, s, o: (s, o.cb * (o.bw // tk) + kk), o=a)),
                  pl.BlockSpec((tt, tn), functools.partial(lambda kk, nn, s, o: (s, o.cb * (o.bw // tn) + nn), o=b))],
        out_specs=pl.BlockSpec((tk, tn), lambda kk, nn, s: (kk, nn)),
        out_shape=jax.ShapeDtypeStruct((k, n), F32),
        compiler_params=_cparams(("parallel", "parallel", "arbitrary")),
    )(a.arr, b.arr)


def _sigmoid(x):
    return 1.0 / (1.0 + jnp.exp(-x))


def _silu(x):
    return x * _sigmoid(x)


def _softplus(x):
    return jnp.maximum(x, 0.0) + jnp.log(1.0 + jnp.exp(-jnp.abs(x)))


def _act(g, u):
    return _silu(g) * u


def _resid_ln(scale, h, branch, g, b):
    r = ALPHA * h + scale * branch
    mu = jnp.mean(r, axis=-1, keepdims=True)
    var = jnp.mean(jnp.square(r - mu), axis=-1, keepdims=True)
    return (r - mu) * lax.rsqrt(var + LN_EPS) * g + b


def _rms(t, w):
    return t * lax.rsqrt(jnp.mean(t * t, axis=-1, keepdims=True) + RMS_EPS) * w


def _branch_weights(l1, l2, l3):
    m = jnp.maximum(jnp.maximum(l1, l2), l3)
    e1, e2, e3 = jnp.exp(l1 - m), jnp.exp(l2 - m), jnp.exp(l3 - m)
    inv = 1.0 / (e1 + e2 + e3)
    return e1 * inv, e2 * inv, e3 * inv


def _gate(y, xs, z, dskip, w):
    return _rms((y + dskip * xs) * _silu(z), w)


def _rot(x):
    d = lax.broadcasted_iota(jnp.int32, x.shape, 1) % HEAD_DIM
    up = pltpu.roll(x, x.shape[1] - ROPE_DIM // 2, 1)
    down = jnp.where(d < ROPE_DIM, pltpu.roll(x, ROPE_DIM // 2, 1), 0.0)
    return jnp.where(d < ROPE_DIM // 2, up, down)


def _tile6(t):
    return jnp.concatenate([t] * (D_ATTN // LANES), axis=1)


def ffn_fwd(tag, h, wg, wu, wd, ln_g, ln_b):
    t = h.shape[0]
    g = mm(f"{tag}_gate", [(h, wg, 'nn')], D_FF)
    u = mm(f"{tag}_up", [(h, wu, 'nn')], D_FF)
    a, = rowwise(f"{tag}_act", _act, [g, u], [], [(t, D_FF, F32)])
    f = mm(f"{tag}_down", [(a, wd, 'nn')], D_MODEL)
    out, = rowwise(f"{tag}_ln", functools.partial(_resid_ln, 0.5), [h, f], [ln_g, ln_b], [(t, D_MODEL, F32)])
    return out, (h, g, u, a, f)


def resid_ln_bwd(name, scale, h, branch, ln_g, ln_b, dout, extra=None):
    t = h.shape[0]

    def fn(h_, br_, do_, *rest):
        g_, b_ = rest[-2], rest[-1]
        _, vjp = jax.vjp(functools.partial(_resid_ln, scale), h_, br_, g_, b_)
        dh, dbr, dg, db = vjp(do_)
        if extra is not None:
            dh = dh + rest[0]
        return dh, dbr, dg, db

    ins = [h, branch, dout] + ([extra] if extra is not None else [])
    return rowwise(name, fn, ins, [ln_g, ln_b], [(t, D_MODEL, F32), (t, D_MODEL, F32)],
                   accs=[(1, D_MODEL), (1, D_MODEL)])


def ffn_bwd(tag, res, wg, wu, wd, df, dh_resid):
    h, g, u, a, _ = res
    t = h.shape[0]
    da = mm(f"{tag}_bwd_da", [(df, wd, 'nt')], D_FF)
    dwd = mm_tn(f"{tag}_bwd_dwd", a, df)

    def act_bwd(g_, u_, da_):
        _, vjp = jax.vjp(_act, g_, u_)
        return vjp(da_)

    dg, du = rowwise(f"{tag}_bwd_act", act_bwd, [g, u, da], [], [(t, D_FF, F32), (t, D_FF, F32)])
    dh = mm(f"{tag}_bwd_dh", [(dg, wg, 'nt'), (du, wu, 'nt')], D_MODEL, add=dh_resid)
    dwg = mm_tn(f"{tag}_bwd_dwg", h, dg)
    dwu = mm_tn(f"{tag}_bwd_dwu", h, du)
    return dh, dwg, dwu, dwd


def rope_tables(positions):
    inv_freq = ROPE_THETA ** (-jnp.arange(0, ROPE_DIM, 2, dtype=F32) / ROPE_DIM)
    ang = positions.reshape(-1, 1).astype(F32) * inv_freq
    c, s = jnp.cos(ang), jnp.sin(ang)
    t = ang.shape[0]
    cosv = jnp.concatenate([c, c, jnp.ones((t, HEAD_DIM - ROPE_DIM), F32)], axis=1)
    sinv = jnp.concatenate([-s, s, jnp.zeros((t, HEAD_DIM - ROPE_DIM), F32)], axis=1)
    return jnp.tile(cosv, (1, 2)), jnp.tile(sinv, (1, 2))


def rope_fwd(qkvz, cosv, sinv):
    t = qkvz.shape[0]

    def fn(q, k, c, s):
        c, s = _tile6(c), _tile6(s)
        return q * c + _rot(q) * s, k * c + _rot(k) * s

    return rowwise("rope_fwd", fn, [Op(qkvz, D_ATTN, 0), Op(qkvz, D_ATTN, 1), cosv, sinv], [],
                   [(t, D_ATTN, F32), (t, D_ATTN, F32)])


def rope_bwd(dqs, dks, dvs, cosv, sinv):
    t = cosv.shape[0]

    def fn(q1, q2, q3, k1, k2, k3, v1, v2, v3, c, s):
        c, s = _tile6(c), _tile6(s)
        dq, dk = q1 + q2 + q3, k1 + k2 + k3
        return dq * c + _rot(dq * s), dk * c + _rot(dk * s), v1 + v2 + v3

    return rowwise("rope_bwd", fn, [*dqs, *dks, *dvs, cosv, sinv], [],
                   [(t, D_ATTN, F32), (t, D_ATTN, F32), (t, D_ATTN, F32)])


def _pair_masks():
    lane = lax.broadcasted_iota(jnp.int32, (1, LANES), 1)
    return (lane < HEAD_DIM, lane >= HEAD_DIM)


def _band_masks(i):
    row = lax.broadcasted_iota(jnp.int32, (ATTN_BLOCK, ATTN_BLOCK), 0)
    col = lax.broadcasted_iota(jnp.int32, (ATTN_BLOCK, ATTN_BLOCK), 1)
    return jnp.logical_and(col >= row, i > 0), col <= row


_NT = (((1,), (1,)), ((), ()))
_NN = (((1,), (0,)), ((), ()))
_TN = (((0,), (0,)), ((), ()))


def _dot(a, b, dn, precision=None):
    return lax.dot_general(a, b, dn, preferred_element_type=F32, precision=precision)


def _attn_views(b, d):
    l = SEQ // d
    nb = l // ATTN_BLOCK
    cur = lambda w, cb: pl.BlockSpec((ATTN_BLOCK, D_ATTN), lambda bb, r, i: (bb * nb + i, r * w + cb))
    prev = lambda w, cb: pl.BlockSpec((ATTN_BLOCK, D_ATTN), lambda bb, r, i: (bb * nb + jnp.maximum(i - 1, 0), r * w + cb))
    return l, nb, cur, prev


def attn_fwd(qr, kr, qkvz, d, b):
    l, nb, cur, prev = _attn_views(b, d)
    scale = HEAD_DIM ** -0.5

    def body(q_ref, kp_ref, kc_ref, vp_ref, vc_ref, o_ref, l_ref):
        prev_ok, cur_ok = _band_masks(pl.program_id(2))
        masks = _pair_masks()
        for hp in range(D_ATTN // LANES):
            sl = slice(hp * LANES, (hp + 1) * LANES)
            q2, vp2, vc2 = q_ref[:, sl], vp_ref[:, sl], vc_ref[:, sl]
            kp2, kc2 = kp_ref[:, sl].astype(BF16), kc_ref[:, sl].astype(BF16)
            o2 = jnp.zeros((ATTN_BLOCK, LANES), F32)
            l2 = jnp.zeros((ATTN_BLOCK, LANES), F32)
            for m in masks:
                qm = jnp.where(m, q2, 0.0).astype(BF16)
                sp = jnp.where(prev_ok, _dot(qm, kp2, _NT) * scale, NEG)
                sc = jnp.where(cur_ok, _dot(qm, kc2, _NT) * scale, NEG)
                mx = jnp.maximum(jnp.max(sp, axis=1, keepdims=True), jnp.max(sc, axis=1, keepdims=True))
                pp, pc = jnp.exp(sp - mx), jnp.exp(sc - mx)
                den = jnp.sum(pp, axis=1, keepdims=True) + jnp.sum(pc, axis=1, keepdims=True)
                oh = _dot(pp.astype(BF16), jnp.where(m, vp2, 0.0).astype(BF16), _NN)
                oh = oh + _dot(pc.astype(BF16), jnp.where(m, vc2, 0.0).astype(BF16), _NN)
                o2 = o2 + oh * (1.0 / den)
                l2 = jnp.where(m, mx + jnp.log(den), l2)
            o_ref[:, sl] = o2
            l_ref[:, sl] = l2

    shp = jax.ShapeDtypeStruct((b * l, d * D_ATTN), F32)
    o, lse = pl.pallas_call(
        body, name=f"attn_fwd_d{d}", grid=(b, d, nb),
        in_specs=[cur(1, 0), prev(1, 0), cur(1, 0), prev(4, 2), cur(4, 2)],
        out_specs=[cur(1, 0), cur(1, 0)], out_shape=[shp, shp],
        compiler_params=_cparams(("parallel", "parallel", "parallel")),
    )(qr.reshape(b * l, d * D_ATTN), kr.reshape(b * l, d * D_ATTN), kr.reshape(b * l, d * D_ATTN),
      qkvz.reshape(b * l, d * D_QKVZ), qkvz.reshape(b * l, d * D_QKVZ))
    return o.reshape(-1, D_ATTN), lse.reshape(-1, D_ATTN)


def attn_bwd(qr, kr, qkvz, do, lse, dd, d, b):
    l, nb, cur, prev = _attn_views(b, d)
    scale = HEAD_DIM ** -0.5

    def body(q_ref, kp_ref, kc_ref, vp_ref, vc_ref, do_ref, l_ref, dd_ref, dq_ref, dka_ref, dkb_ref, dva_ref, dvb_ref):
        prev_ok, cur_ok = _band_masks(pl.program_id(2))
        masks = _pair_masks()
        for hp in range(D_ATTN // LANES):
            sl = slice(hp * LANES, (hp + 1) * LANES)
            q2, do2, l2, dd2 = q_ref[:, sl], do_ref[:, sl], l_ref[:, sl], dd_ref[:, sl]
            kp2, kc2 = kp_ref[:, sl].astype(BF16), kc_ref[:, sl].astype(BF16)
            vp2, vc2 = vp_ref[:, sl].astype(BF16), vc_ref[:, sl].astype(BF16)
            l2s, dd2s = pltpu.roll(l2, HEAD_DIM, 1), pltpu.roll(dd2, HEAD_DIM, 1)
            dq2 = jnp.zeros((ATTN_BLOCK, LANES), F32)
            dka, dkb, dva, dvb = dq2, dq2, dq2, dq2
            for m in masks:
                qm = jnp.where(m, q2, 0.0).astype(BF16)
                dom = jnp.where(m, do2, 0.0).astype(BF16)
                lrep, ddrep = jnp.where(m, l2, l2s), jnp.where(m, dd2, dd2s)
                sp = jnp.where(prev_ok, _dot(qm, kp2, _NT) * scale, NEG)
                sc = jnp.where(cur_ok, _dot(qm, kc2, _NT) * scale, NEG)
                pp, pc = jnp.exp(sp - lrep), jnp.exp(sc - lrep)
                dsp = pp * (_dot(dom, vp2, _NT) - ddrep) * scale
                dsc = pc * (_dot(dom, vc2, _NT) - ddrep) * scale
                dspb, dscb = dsp.astype(BF16), dsc.astype(BF16)
                dq2 = dq2 + _dot(dspb, jnp.where(m, kp2, 0), _NN) + _dot(dscb, jnp.where(m, kc2, 0), _NN)
                dka = dka + _dot(dspb, qm, _TN)
                dkb = dkb + _dot(dscb, qm, _TN)
                dva = dva + _dot(pp.astype(BF16), dom, _TN)
                dvb = dvb + _dot(pc.astype(BF16), dom, _TN)
            dq_ref[:, sl], dka_ref[:, sl], dkb_ref[:, sl], dva_ref[:, sl], dvb_ref[:, sl] = dq2, dka, dkb, dva, dvb

    shp = jax.ShapeDtypeStruct((b * l, d * D_ATTN), F32)
    v1 = lambda a: a.reshape(b * l, d * D_ATTN)
    v4 = qkvz.reshape(b * l, d * D_QKVZ)
    dq, dka, dkb, dva, dvb = pl.pallas_call(
        body, name=f"attn_bwd_d{d}", grid=(b, d, nb),
        in_specs=[cur(1, 0), prev(1, 0), cur(1, 0), prev(4, 2), cur(4, 2), cur(1, 0), cur(1, 0), cur(1, 0)],
        out_specs=[cur(1, 0)] * 5, out_shape=[shp] * 5,
        compiler_params=_cparams(("parallel", "parallel", "parallel")),
    )(v1(qr), v1(kr), v1(kr), v4, v4, v1(do), v1(lse), v1(dd))

    nxt = pl.BlockSpec((ATTN_BLOCK, D_ATTN), lambda bb, r, i: (bb * nb + jnp.minimum(i + 1, nb - 1), r))

    def comb(kb_ref, ka_ref, vb_ref, va_ref, dk_ref, dv_ref):
        has_next = pl.program_id(2) + 1 < nb
        dk_ref[...] = kb_ref[...] + jnp.where(has_next, ka_ref[...], 0.0)
        dv_ref[...] = vb_ref[...] + jnp.where(has_next, va_ref[...], 0.0)

    dk, dv = pl.pallas_call(
        comb, name=f"attn_bwd_comb_d{d}", grid=(b, d, nb),
        in_specs=[cur(1, 0), nxt, cur(1, 0), nxt], out_specs=[cur(1, 0)] * 2, out_shape=[shp] * 2,
        compiler_params=_cparams(("parallel", "parallel", "parallel")),
    )(dkb, dka, dvb, dva)
    return dq.reshape(-1, D_ATTN), dk.reshape(-1, D_ATTN), dv.reshape(-1, D_ATTN)


def _head_ones():
    h = np.arange(D_ATTN) // HEAD_DIM
    return jnp.asarray((h[:, None] == h[None, :]).astype(np.float32))


def attn_merge_fwd(os_, ls_, norm_w):
    t = os_[0].shape[0]

    def fn(o1, o2, o3, l1, l2, l3, w):
        w1, w2, w3 = _branch_weights(l1, l2, l3)
        return _rms(w1 * o1 + w2 * o2 + w3 * o3, w)

    return rowwise("attn_merge", fn, [*os_, *ls_], [norm_w], [(t, D_ATTN, F32)])[0]


def attn_merge_bwd(dout, os_, ls_, norm_w):
    t = dout.shape[0]

    def fn(dy, o1, o2, o3, l1, l2, l3, w, ones):
        w1, w2, w3 = _branch_weights(l1, l2, l3)
        mixed = w1 * o1 + w2 * o2 + w3 * o3
        _, vjp = jax.vjp(_rms, mixed, w)
        dmix, dw = vjp(dy)
        tot = _dot(dmix * mixed, ones, _NN, HI)
        return w1 * dmix, w2 * dmix, w3 * dmix, w1 * tot, w2 * tot, w3 * tot, dw

    res = rowwise("attn_merge_bwd", fn, [dout, *os_, *ls_], [norm_w, _head_ones()], [(t, D_ATTN, F32)] * 6,
                  accs=[(1, D_ATTN)])
    return res[0:3], res[3:6], res[6]


CONV_TM = 256
HALO = 8


def conv_fwd(u, w, bias):
    t = u.shape[0]
    tm, per_seq = CONV_TM, SEQ // CONV_TM

    def body(u_ref, h_ref, w_ref, b_ref, xs_ref, bm_ref, cm_ref, scr):
        first = pl.program_id(0) % per_seq == 0
        scr[0:HALO, :] = jnp.where(first, 0.0, h_ref[...])
        scr[HALO:, :] = u_ref[...]
        acc = b_ref[...]
        for k in range(CONV_WIDTH):
            acc = acc + w_ref[k:k + 1, :] * scr[pl.ds(HALO - CONV_WIDTH + 1 + k, tm), :]
        y = _silu(acc)
        xs_ref[...] = y[:, :D_SSD]
        bm_ref[...] = y[:, D_SSD:D_SSD + D_BC]
        cm_ref[...] = y[:, D_SSD + D_BC:]

    return pl.pallas_call(
        body, name="conv_fwd", grid=(t // tm,),
        in_specs=[pl.BlockSpec((tm, D_CONV), lambda i: (i, 0)),
                  pl.BlockSpec((HALO, D_CONV), lambda i: (jnp.maximum(i * (tm // HALO) - 1, 0), 0)),
                  pl.BlockSpec((CONV_WIDTH, D_CONV), lambda i: (0, 0)), pl.BlockSpec((1, D_CONV), lambda i: (0, 0))],
        out_specs=[pl.BlockSpec((tm, D_SSD), lambda i: (i, 0)), pl.BlockSpec((tm, D_BC), lambda i: (i, 0)),
                   pl.BlockSpec((tm, D_BC), lambda i: (i, 0))],
        out_shape=[jax.ShapeDtypeStruct((t, D_SSD), F32), jax.ShapeDtypeStruct((t, D_BC), F32),
                   jax.ShapeDtypeStruct((t, D_BC), F32)],
        scratch_shapes=[pltpu.VMEM((tm + HALO, D_CONV), F32)],
        compiler_params=_cparams(("parallel",)),
    )(u, u, w, bias)


def conv_bwd(u, w, bias, dxs, dbm, dcm):
    t = u.shape[0]
    tm, per_seq = CONV_TM, SEQ // CONV_TM
    n_tiles = t // tm

    def body1(u_ref, h_ref, dxs_ref, dbm_ref, dcm_ref, w_ref, b_ref, dz_ref, dw_ref, db_ref, scr):
        i = pl.program_id(0)
        first = i % per_seq == 0
        scr[0:HALO, :] = jnp.where(first, 0.0, h_ref[...])
        scr[HALO:, :] = u_ref[...]
        acc = b_ref[...]
        for k in range(CONV_WIDTH):
            acc = acc + w_ref[k:k + 1, :] * scr[pl.ds(HALO - CONV_WIDTH + 1 + k, tm), :]
        sig = _sigmoid(acc)
        dy = jnp.concatenate([dxs_ref[...], dbm_ref[...], dcm_ref[...]], axis=1)
        dz = dy * sig * (1.0 + acc * (1.0 - sig))
        dz_ref[...] = dz

        @pl.when(i == 0)
        def _():
            dw_ref[...] = jnp.zeros(dw_ref.shape, F32)
            db_ref[...] = jnp.zeros(db_ref.shape, F32)
        db_ref[...] += jnp.sum(dz, axis=0, keepdims=True)
        for k in range(CONV_WIDTH):
            dw_ref[k:k + 1, :] += jnp.sum(dz * scr[pl.ds(HALO - CONV_WIDTH + 1 + k, tm), :], axis=0, keepdims=True)

    dz, dw, db = pl.pallas_call(
        body1, name="conv_bwd_dz", grid=(n_tiles,),
        in_specs=[pl.BlockSpec((tm, D_CONV), lambda i: (i, 0)),
                  pl.BlockSpec((HALO, D_CONV), lambda i: (jnp.maximum(i * (tm // HALO) - 1, 0), 0)),
                  pl.BlockSpec((tm, D_SSD), lambda i: (i, 0)), pl.BlockSpec((tm, D_BC), lambda i: (i, 0)),
                  pl.BlockSpec((tm, D_BC), lambda i: (i, 0)),
                  pl.BlockSpec((CONV_WIDTH, D_CONV), lambda i: (0, 0)), pl.BlockSpec((1, D_CONV), lambda i: (0, 0))],
        out_specs=[pl.BlockSpec((tm, D_CONV), lambda i: (i, 0)), pl.BlockSpec((CONV_WIDTH, D_CONV), lambda i: (0, 0)),
                   pl.BlockSpec((1, D_CONV), lambda i: (0, 0))],
        out_shape=[jax.ShapeDtypeStruct((t, D_CONV), F32), jax.ShapeDtypeStruct((CONV_WIDTH, D_CONV), F32),
                   jax.ShapeDtypeStruct((1, D_CONV), F32)],
        scratch_shapes=[pltpu.VMEM((tm + HALO, D_CONV), F32)],
        compiler_params=_cparams(("arbitrary",)),
    )(u, u, dxs, dbm, dcm, w, bias)

    def body2(dz_ref, n_ref, w_ref, du_ref, scr):
        last = pl.program_id(0) % per_seq == per_seq - 1
        scr[0:tm, :] = dz_ref[...]
        scr[tm:, :] = jnp.where(last, 0.0, n_ref[...])
        acc = jnp.zeros((tm, D_CONV), F32)
        for k in range(CONV_WIDTH):
            acc = acc + w_ref[k:k + 1, :] * scr[pl.ds(CONV_WIDTH - 1 - k, tm), :]
        du_ref[...] = acc

    du = pl.pallas_call(
        body2, name="conv_bwd_du", grid=(n_tiles,),
        in_specs=[pl.BlockSpec((tm, D_CONV), lambda i: (i, 0)),
                  pl.BlockSpec((HALO, D_CONV), lambda i: (jnp.minimum((i + 1) * (tm // HALO), t // HALO - 1), 0)),
                  pl.BlockSpec((CONV_WIDTH, D_CONV), lambda i: (0, 0))],
        out_specs=pl.BlockSpec((tm, D_CONV), lambda i: (i, 0)),
        out_shape=jax.ShapeDtypeStruct((t, D_CONV), F32),
        scratch_shapes=[pltpu.VMEM((tm + HALO, D_CONV), F32)],
        compiler_params=_cparams(("parallel",)),
    )(dz, dz, w)
    return du, dw, db


Q = SSD_CHUNK
N_PAIRS = D_SSD // LANES
HEADS_PER_GROUP = N_HEADS // SSD_GROUPS


def _head_expand():
    e = np.zeros((LANES, N_HEADS * LANES), np.float32)
    for j in range(N_HEADS):
        e[j, j * LANES:(j + 1) * LANES] = 1.0
    return jnp.asarray(e)


def _pad_lanes(v, fill=0.0):
    row = jnp.pad(v.reshape(1, -1).astype(F32), ((0, 0), (0, LANES - v.size)), constant_values=fill)
    return row, row.reshape(LANES, 1)


def _ssd_common(dtr_ref, dtrt_ref, bias_r, bias_c, alog_r, alog_c, e_ref):
    row = lax.broadcasted_iota(jnp.int32, (Q, Q), 0)
    col = lax.broadcasted_iota(jnp.int32, (Q, Q), 1)
    tril = row >= col
    lane = lax.broadcasted_iota(jnp.int32, (1, LANES), 1)
    a_r = jnp.where(lane < N_HEADS, -jnp.exp(alog_r[...]), 0.0)
    sub = lax.broadcasted_iota(jnp.int32, (LANES, 1), 0)
    a_c = jnp.where(sub < N_HEADS, -jnp.exp(alog_c[...]), 0.0)
    dt = _softplus(dtr_ref[...] + bias_r[...])
    cs = _dot(tril.astype(F32), dt * a_r, _NN, HI)
    dt_rep = _dot(dt, e_ref[...], _NN, HI)
    cs_rep = _dot(cs, e_ref[...], _NN, HI)
    dtt = _softplus(dtrt_ref[...] + bias_c[...])
    cst = _dot(dtt * a_c, (row <= col).astype(F32), _NN, HI)
    return tril, lane, a_r, dt, dt_rep, cs_rep, cst


def _ssd_specs(b, nc, rev):
    ci = (lambda c: nc - 1 - c) if rev else (lambda c: c)
    rows = lambda w: pl.BlockSpec((Q, w), lambda bb, c: (bb * nc + ci(c), 0))
    dtt = pl.BlockSpec((LANES, Q), lambda bb, c: (0, bb * nc + ci(c)))
    const = lambda s: pl.BlockSpec(s, lambda bb, c: (0,) * len(s))
    state = pl.BlockSpec((None, N_PAIRS, LANES, SSD_STATE), lambda bb, c: (bb * nc + ci(c), 0, 0, 0))
    return rows, dtt, const, state


def ssd_fwd(xs, bm, cm, dtraw, dt_bias, a_log, b):
    t = xs.shape[0]
    nc = SEQ // Q
    rows, dtt_spec, const, state = _ssd_specs(b, nc, False)
    bias_r, bias_c = _pad_lanes(dt_bias)
    alog_r, alog_c = _pad_lanes(a_log)

    def body(xs_ref, b_ref, c_ref, dtr_ref, dtrt_ref, br, bc, ar, ac, e_ref, y_ref, hp_ref, h_scr):
        @pl.when(pl.program_id(1) == 0)
        def _():
            h_scr[...] = jnp.zeros(h_scr.shape, F32)
        tril, lane, _, _, dt_rep, cs_rep, cst = _ssd_common(dtr_ref, dtrt_ref, br, bc, ar, ac, e_ref)
        sub = lax.broadcasted_iota(jnp.int32, (LANES, 1), 0)
        y_acc = [jnp.zeros((Q, LANES), F32) for _ in range(N_PAIRS)]
        h_old = [h_scr[p] for p in range(N_PAIRS)]
        h_new = [jnp.zeros((LANES, SSD_STATE), F32) for _ in range(N_PAIRS)]
        for g in range(SSD_GROUPS):
            bg = b_ref[:, g * SSD_STATE:(g + 1) * SSD_STATE].astype(BF16)
            cg = c_ref[:, g * SSD_STATE:(g + 1) * SSD_STATE].astype(BF16)
            cb = _dot(cg, bg, _NT)
            for j in range(g * HEADS_PER_GROUP, (g + 1) * HEADS_PER_GROUP):
                p, side = j // 2, j % 2
                m = (lane < HEAD_DIM) if side == 0 else (lane >= HEAD_DIM)
                ms = (sub < HEAD_DIM) if side == 0 else (sub >= HEAD_DIM)
                csj = cs_rep[:, j * LANES:(j + 1) * LANES]
                dtj = dt_rep[:, j * LANES:(j + 1) * LANES]
                lmat = jnp.exp(jnp.where(tril, csj - cst[j:j + 1, :], NEG))
                xdt = jnp.where(m, xs_ref[:, p * LANES:(p + 1) * LANES] * dtj, 0.0)
                hm = jnp.where(ms, h_old[p], 0.0)
                ydiag = _dot((cb * lmat).astype(BF16), xdt.astype(BF16), _NN)
                yoff = jnp.exp(csj) * _dot(cg, hm.astype(BF16), _NT)
                y_acc[p] = y_acc[p] + ydiag + yoff
                last = csj[Q - 1:Q, :]
                sj = _dot((xdt * jnp.exp(last - csj)).astype(BF16), bg, _TN)
                h_new[p] = h_new[p] + jnp.exp(last) * hm + sj
        for p in range(N_PAIRS):
            y_ref[:, p * LANES:(p + 1) * LANES] = y_acc[p]
            hp_ref[p] = h_old[p]
            h_scr[p] = h_new[p]

    return pl.pallas_call(
        body, name="ssd_fwd", grid=(b, nc),
        in_specs=[rows(D_SSD), rows(D_BC), rows(D_BC), rows(LANES), dtt_spec, const((1, LANES)), const((LANES, 1)),
                  const((1, LANES)), const((LANES, 1)), const((LANES, N_HEADS * LANES))],
        out_specs=[rows(D_SSD), state],
        out_shape=[jax.ShapeDtypeStruct((t, D_SSD), F32),
                   jax.ShapeDtypeStruct((b * nc, N_PAIRS, LANES, SSD_STATE), F32)],
        scratch_shapes=[pltpu.VMEM((N_PAIRS, LANES, SSD_STATE), F32)],
        compiler_params=_cparams(("parallel", "arbitrary")),
    )(xs, bm, cm, dtraw, dtraw.T, bias_r, bias_c, alog_r, alog_c, _head_expand())


def ssd_bwd(xs, bm, cm, dtraw, dt_bias, a_log, hprev, dy, b):
    t = xs.shape[0]
    nc = SEQ // Q
    rows, dtt_spec, const, state = _ssd_specs(b, nc, True)
    bias_r, bias_c = _pad_lanes(dt_bias)
    alog_r, alog_c = _pad_lanes(a_log)

    def body(xs_ref, b_ref, c_ref, dtr_ref, dtrt_ref, hp_ref, dy_ref, br, bc, ar, ac, e_ref,
             dxs_ref, db_ref, dc_ref, ddt_ref, dbias_ref, dalog_ref, dh_scr):
        first = jnp.logical_and(pl.program_id(0) == 0, pl.program_id(1) == 0)

        @pl.when(pl.program_id(1) == 0)
        def _():
            dh_scr[...] = jnp.zeros(dh_scr.shape, F32)

        @pl.when(first)
        def _():
            dbias_ref[...] = jnp.zeros(dbias_ref.shape, F32)
            dalog_ref[...] = jnp.zeros(dalog_ref.shape, F32)
        tril, lane, a_r, dt, dt_rep, cs_rep, cst = _ssd_common(dtr_ref, dtrt_ref, br, bc, ar, ac, e_ref)
        sub = lax.broadcasted_iota(jnp.int32, (LANES, 1), 0)
        rowq = lax.broadcasted_iota(jnp.int32, (Q, 1), 0)
        ones = jnp.ones((Q, LANES), F32)
        triu = (lax.broadcasted_iota(jnp.int32, (Q, Q), 0) <= lax.broadcasted_iota(jnp.int32, (Q, Q), 1)).astype(F32)
        dxs_acc = [jnp.zeros((Q, LANES), F32) for _ in range(N_PAIRS)]
        dh_in = [dh_scr[p] for p in range(N_PAIRS)]
        h_in = [hp_ref[p] for p in range(N_PAIRS)]
        dh_out = [jnp.zeros((LANES, SSD_STATE), F32) for _ in range(N_PAIRS)]
        ddt = jnp.zeros((Q, LANES), F32)
        dalog = jnp.zeros((1, LANES), F32)
        for g in range(SSD_GROUPS):
            gs = slice(g * SSD_STATE, (g + 1) * SSD_STATE)
            bg, cg = b_ref[:, gs].astype(BF16), c_ref[:, gs].astype(BF16)
            cb = _dot(cg, bg, _NT)
            dcb = jnp.zeros((Q, Q), F32)
            dbg = jnp.zeros((Q, SSD_STATE), F32)
            dcg = jnp.zeros((Q, SSD_STATE), F32)
            for j in range(g * HEADS_PER_GROUP, (g + 1) * HEADS_PER_GROUP):
                p, side = j // 2, j % 2
                m = (lane < HEAD_DIM) if side == 0 else (lane >= HEAD_DIM)
                ms = (sub < HEAD_DIM) if side == 0 else (sub >= HEAD_DIM)
                csj = cs_rep[:, j * LANES:(j + 1) * LANES]
                dtj = dt_rep[:, j * LANES:(j + 1) * LANES]
                lmat = jnp.exp(jnp.where(tril, csj - cst[j:j + 1, :], NEG))
                x2 = jnp.where(m, xs_ref[:, p * LANES:(p + 1) * LANES], 0.0)
                xdt = x2 * dtj
                dym = jnp.where(m, dy_ref[:, p * LANES:(p + 1) * LANES], 0.0)
                hm = jnp.where(ms, h_in[p], 0.0)
                dhm = jnp.where(ms, dh_in[p], 0.0)
                ecs = jnp.exp(csj)
                last = csj[Q - 1:Q, :]
                decay = jnp.exp(last - csj)
                el = jnp.exp(last)
                gmat = cb * lmat
                dymb, xdtb = dym.astype(BF16), xdt.astype(BF16)
                dg = _dot(dymb, xdtb, _NT)
                dxdt = _dot(gmat.astype(BF16), dymb, _TN)
                dcb = dcb + dg * lmat
                ej = dg * gmat
                dcs = _dot(ej, ones, _NN, HI) - _dot(ej, ones, _TN, HI)
                ch = _dot(cg, hm.astype(BF16), _NT)
                dye = dym * ecs
                dcs = dcs + jnp.sum(dye * ch, axis=1, keepdims=True)
                dcg = dcg + _dot(dye.astype(BF16), hm.astype(BF16), _NN)
                dhp = _dot(dye.astype(BF16), cg, _TN)
                wmat = _dot(bg, dhm.astype(BF16), _NT)
                xd = xdt * decay
                dxdt = dxdt + decay * wmat
                ddl = jnp.sum(xd * wmat, axis=1, keepdims=True)
                dlast = jnp.sum(ddl, axis=0, keepdims=True) + el * jnp.sum(jnp.sum(dhm * hm, axis=1, keepdims=True), axis=0, keepdims=True)
                dcs = dcs - ddl + jnp.where(rowq == Q - 1, dlast, 0.0)
                dbg = dbg + _dot(xd.astype(BF16), dhm.astype(BF16), _NN)
                dh_out[p] = dh_out[p] + el * dhm + dhp
                da = _dot(triu, dcs, _NN, HI)
                aj = jnp.sum(jnp.where(lane == j, a_r, 0.0), axis=1, keepdims=True)
                ddtj = da * aj + jnp.sum(dxdt * x2, axis=1, keepdims=True)
                ddt = ddt + jnp.where(lane == j, ddtj, 0.0)
                dalog = dalog + jnp.where(lane == j, jnp.sum(da * dtj, axis=0, keepdims=True) * aj, 0.0)
                dxs_acc[p] = dxs_acc[p] + dxdt * dtj
            dcbb = dcb.astype(BF16)
            dc_ref[:, gs] = dcg + _dot(dcbb, bg, _NN)
            db_ref[:, gs] = dbg + _dot(dcbb, cg, _TN)
        for p in range(N_PAIRS):
            dxs_ref[:, p * LANES:(p + 1) * LANES] = dxs_acc[p]
            dh_scr[p] = dh_out[p]
        ddtraw = ddt * _sigmoid(dtr_ref[...] + br[...])
        ddt_ref[...] = ddtraw
        dbias_ref[...] += jnp.sum(ddtraw, axis=0, keepdims=True)
        dalog_ref[...] += dalog

    return pl.pallas_call(
        body, name="ssd_bwd", grid=(b, nc),
        in_specs=[rows(D_SSD), rows(D_BC), rows(D_BC), rows(LANES), dtt_spec, state, rows(D_SSD), const((1, LANES)),
                  const((LANES, 1)), const((1, LANES)), const((LANES, 1)), const((LANES, N_HEADS * LANES))],
        out_specs=[rows(D_SSD), rows(D_BC), rows(D_BC), rows(LANES), const((1, LANES)), const((1, LANES))],
        out_shape=[jax.ShapeDtypeStruct((t, D_SSD), F32), jax.ShapeDtypeStruct((t, D_BC), F32),
                   jax.ShapeDtypeStruct((t, D_BC), F32), jax.ShapeDtypeStruct((t, LANES), F32),
                   jax.ShapeDtypeStruct((1, LANES), F32), jax.ShapeDtypeStruct((1, LANES), F32)],
        scratch_shapes=[pltpu.VMEM((N_PAIRS, LANES, SSD_STATE), F32)],
        compiler_params=_cparams(("arbitrary", "arbitrary")),
    )(xs, bm, cm, dtraw, dtraw.T, hprev, dy, bias_r, bias_c, alog_r, alog_c, _head_expand())


def _split_w_in(w_in):
    w_dt = jnp.pad(w_in[:, D_QKVZ + D_CONV:], ((0, 0), (0, LANES - N_HEADS)))
    return w_in[:, :D_QKVZ], w_in[:, D_QKVZ:D_QKVZ + D_CONV], w_dt


def mixer_fwd(h, p, cosv, sinv, b):
    t = h.shape[0]
    w_a, w_b, w_c = _split_w_in(p['w_in'])
    qkvz = mm("in_qkvz", [(h, w_a, 'nn')], D_QKVZ)
    xbc = mm("in_xbc", [(h, w_b, 'nn')], D_CONV)
    dtraw = mm("in_dt", [(h, w_c, 'nn')], LANES)
    qr, kr = rope_fwd(qkvz, cosv, sinv)
    os_, ls_ = [], []
    for d in DILATIONS:
        o, l = attn_fwd(qr, kr, qkvz, d, b)
        os_.append(o)
        ls_.append(l)
    attn = attn_merge_fwd(os_, ls_, p['attn_norm_w'])
    xs, bm, cm = conv_fwd(xbc, p['conv_w'], p['conv_b'])
    y, hprev = ssd_fwd(xs, bm, cm, dtraw, p['dt_bias'], p['a_log'], b)
    dskip = jnp.repeat(p['d_skip'].reshape(-1), HEAD_DIM).reshape(1, D_SSD)
    yg, = rowwise("ssd_gate", _gate, [y, xs, Op(qkvz, D_SSD, 3)], [dskip, p['ssd_norm_w']], [(t, D_SSD, F32)])
    mix = mm("out_proj", [(attn, p['w_out'][:D_ATTN], 'nn'), (yg, p['w_out'][D_ATTN:], 'nn')], D_MODEL)
    res = dict(h=h, qkvz=qkvz, xbc=xbc, dtraw=dtraw, qr=qr, kr=kr, os=os_, ls=ls_, attn=attn, xs=xs, bm=bm, cm=cm,
               y=y, hprev=hprev, dskip=dskip, yg=yg, cosv=cosv, sinv=sinv)
    return mix, res


def mixer_bwd(r, p, dmix, dh_resid, b):
    t = dmix.shape[0]
    w_a, w_b, w_c = _split_w_in(p['w_in'])
    w_out = p['w_out']
    dattn = mm("out_bwd_dattn", [(dmix, w_out[:D_ATTN], 'nt')], D_ATTN)
    dyg = mm("out_bwd_dyg", [(dmix, w_out[D_ATTN:], 'nt')], D_SSD)
    dw_out = jnp.concatenate([mm_tn("out_bwd_dw_a", r['attn'], dmix), mm_tn("out_bwd_dw_y", r['yg'], dmix)], axis=0)

    def gate_bwd(dy_, y_, xs_, z_, ds_, w_):
        _, vjp = jax.vjp(_gate, y_, xs_, z_, ds_, w_)
        return vjp(dy_)

    dy, dxs_a, dz, ddskip, dssd_norm = rowwise(
        "ssd_gate_bwd", gate_bwd, [dyg, r['y'], r['xs'], Op(r['qkvz'], D_SSD, 3)], [r['dskip'], p['ssd_norm_w']],
        [(t, D_SSD, F32)] * 3, accs=[(1, D_SSD), (1, D_SSD)])
    dxs_b, dbm, dcm, ddtraw, ddt_bias, da_log = ssd_bwd(r['xs'], r['bm'], r['cm'], r['dtraw'], p['dt_bias'], p['a_log'],
                                                        r['hprev'], dy, b)
    dxs, = rowwise("ssd_dxs_sum", lambda a_, b_: a_ + b_, [dxs_a, dxs_b], [], [(t, D_SSD, F32)])
    dxbc, dconv_w, dconv_b = conv_bwd(r['xbc'], p['conv_w'], p['conv_b'], dxs, dbm, dcm)
    dos, dds, dattn_norm = attn_merge_bwd(dattn, r['os'], r['ls'], p['attn_norm_w'])
    dqs, dks, dvs = [], [], []
    for d, do, l, dd in zip(DILATIONS, dos, r['ls'], dds):
        dq, dk, dv = attn_bwd(r['qr'], r['kr'], r['qkvz'], do, l, dd, d, b)
        dqs.append(dq)
        dks.append(dk)
        dvs.append(dv)
    dq, dk, dv = rope_bwd(dqs, dks, dvs, r['cosv'], r['sinv'])
    wq, wk, wv, wz = (w_a[:, i * D_ATTN:(i + 1) * D_ATTN] for i in range(4))
    dh = mm("in_bwd_dh", [(dq, wq, 'nt'), (dk, wk, 'nt'), (dv, wv, 'nt'), (dz, wz, 'nt'), (dxbc, w_b, 'nt'),
                          (ddtraw, w_c, 'nt')], D_MODEL, add=dh_resid, tn=512)
    h = r['h']
    dw_in = jnp.concatenate([mm_tn("in_bwd_dwq", h, dq), mm_tn("in_bwd_dwk", h, dk), mm_tn("in_bwd_dwv", h, dv),
                             mm_tn("in_bwd_dwz", h, dz), mm_tn("in_bwd_dwx", h, dxbc),
                             mm_tn("in_bwd_dwdt", h, ddtraw)[:, :N_HEADS]], axis=1)
    head_sum = lambda v: v.reshape(N_HEADS, HEAD_DIM).sum(axis=1).reshape(1, N_HEADS)
    grads = dict(w_in=dw_in, w_out=dw_out, conv_w=dconv_w, conv_b=dconv_b, dt_bias=ddt_bias[:, :N_HEADS],
                 a_log=da_log[:, :N_HEADS], d_skip=head_sum(ddskip), attn_norm_w=dattn_norm, ssd_norm_w=dssd_norm)
    return dh, grads


def _shard_len(shapes):
    n = sum(int(np.prod(shapes[k])) for k in SHARDED) + sum(int(np.prod(shapes[k])) for k in REPLICATED)
    unit = PACK_ROWS_MULT * LANES
    return -(-n // unit) * unit


def pack_local(vals):
    flat = jnp.concatenate([vals[k].reshape(-1).astype(F32) for k in SHARDED + REPLICATED])
    n = _shard_len({k: vals[k].shape for k in WEIGHTS})
    return jnp.pad(flat, (0, n - flat.size)).reshape(-1, LANES)


def unpack_local(packed, shapes):
    flat, out, off = packed.reshape(-1), {}, 0
    for k in SHARDED + REPLICATED:
        n = int(np.prod(shapes[k]))
        out[k] = flat[off:off + n].reshape(shapes[k])
        off += n
    return out


def unpack_gathered(gathered, shapes):
    flat, out, off = gathered.reshape(N_DEV, -1), {}, 0
    for k in SHARDED:
        shp = shapes[k][1:]
        n = int(np.prod(shp))
        blk = flat[:, off:off + n].reshape((N_DEV,) + tuple(shp))
        if k in COL_SHARDED:
            out[k] = jnp.transpose(blk, (1, 0, 2)).reshape(shp[0], N_DEV * shp[1])
        else:
            out[k] = blk.reshape(N_DEV * shp[0], shp[1])
        off += n
    return out


def pack_grads(full, small, shapes):
    parts = []
    for k in SHARDED:
        g, shp = full[k], shapes[k][1:]
        if k in COL_SHARDED:
            parts.append(jnp.transpose(g.reshape(shp[0], N_DEV, shp[1]), (1, 0, 2)).reshape(N_DEV, -1))
        else:
            parts.append(g.reshape(N_DEV, -1))
    sm = jnp.concatenate([small[k].reshape(-1) for k in REPLICATED])
    parts.append(jnp.broadcast_to(sm[None], (N_DEV, sm.size)))
    flat = jnp.concatenate(parts, axis=1)
    n = _shard_len(shapes)
    return jnp.pad(flat, ((0, 0), (0, n - flat.shape[1]))).reshape(N_DEV, -1, LANES)


def _flip(v, bit):
    return 1 - v if bit else v


def all_gather_packed(shard):
    def body(x_ref, out_ref, send_sems, recv_sems, local_sem):
        x, y, c = lax.axis_index("x"), lax.axis_index("y"), lax.axis_index("c")
        me, sibling = (x, y, c), (x, y, 1 - c)
        chips = [(1 - x, y), (x, 1 - y), (1 - x, 1 - y)]

        def rows(px, py, pc):
            return out_ref.at[4 * px + 2 * py + pc]

        def copy(k, block, to, src=None):
            return pltpu.make_async_remote_copy(
                src_ref=rows(*block) if src is None else src, dst_ref=rows(*block),
                send_sem=send_sems.at[k], recv_sem=recv_sems.at[k], device_id=to, device_id_type=MESH)

        mine = pltpu.make_async_copy(x_ref, rows(*me), local_sem)
        mine.start()
        first = [copy(0, me, sibling, src=x_ref)]
        first += [copy(1 + j, me, (*chip, c), src=x_ref) for j, chip in enumerate(chips)]
        for cp in first:
            cp.start()
        passed = [copy(4 + j, (*chip, c), sibling) for j, chip in enumerate(chips)]
        for j, chip in enumerate(chips):
            copy(1 + j, (*chip, c), me).wait_recv()
            passed[j].start()
        copy(0, sibling, me).wait_recv()
        for j, chip in enumerate(chips):
            copy(4 + j, (*chip, 1 - c), me).wait_recv()
        for cp in first + passed:
            cp.wait_send()
        mine.wait()

    return pl.pallas_call(
        body, name="all_gather_weights",
        out_shape=jax.ShapeDtypeStruct((N_DEV,) + shard.shape, shard.dtype),
        in_specs=[pl.BlockSpec(memory_space=pl.ANY)], out_specs=pl.BlockSpec(memory_space=pl.ANY),
        scratch_shapes=[pltpu.SemaphoreType.DMA((7,)), pltpu.SemaphoreType.DMA((7,)), pltpu.SemaphoreType.DMA],
    )(shard)


def all_to_all_packed(send):
    def body(s_ref, r_ref, send_sems, recv_sems, local_sem):
        x, y, c = lax.axis_index("x"), lax.axis_index("y"), lax.axis_index("c")
        me = 4 * x + 2 * y + c
        mine = pltpu.make_async_copy(s_ref.at[me], r_ref.at[me], local_sem)
        mine.start()

        def copy(k):
            px, py, pc = _flip(x, k & 4), _flip(y, k & 2), _flip(c, k & 1)
            return pltpu.make_async_remote_copy(
                src_ref=s_ref.at[4 * px + 2 * py + pc], dst_ref=r_ref.at[me],
                send_sem=send_sems.at[k - 1], recv_sem=recv_sems.at[k - 1], device_id=(px, py, pc), device_id_type=MESH)

        def landing(k):
            px, py, pc = _flip(x, k & 4), _flip(y, k & 2), _flip(c, k & 1)
            return pltpu.make_async_remote_copy(
                src_ref=s_ref.at[me], dst_ref=r_ref.at[4 * px + 2 * py + pc],
                send_sem=send_sems.at[k - 1], recv_sem=recv_sems.at[k - 1], device_id=(px, py, pc), device_id_type=MESH)

        sends = [copy(k) for k in range(1, N_DEV)]
        for cp in sends:
            cp.start()
        for k in range(1, N_DEV):
            landing(k).wait_recv()
        for cp in sends:
            cp.wait_send()
        mine.wait()

    return pl.pallas_call(
        body, name="all_to_all_grads",
        out_shape=jax.ShapeDtypeStruct(send.shape, send.dtype),
        in_specs=[pl.BlockSpec(memory_space=pl.ANY)], out_specs=pl.BlockSpec(memory_space=pl.ANY),
        scratch_shapes=[pltpu.SemaphoreType.DMA((7,)), pltpu.SemaphoreType.DMA((7,)), pltpu.SemaphoreType.DMA],
    )(send)


def adamw_packed(recv, w, m, v):
    rows = w.shape[0]
    tm = PACK_ROWS_MULT
    per = rows // tm
    c1 = 1.0 / (1.0 - ADAM_B1 ** ADAM_STEP)
    c2 = 1.0 / (1.0 - ADAM_B2 ** ADAM_STEP)

    def fn(*a):
        g = a[0]
        for s in range(1, N_DEV):
            g = g + a[s]
        w_, m_, v_ = a[N_DEV:]
        m_ = ADAM_B1 * m_ + (1.0 - ADAM_B1) * g
        v_ = ADAM_B2 * v_ + (1.0 - ADAM_B2) * jnp.square(g)
        delta = -ADAM_LR * ((m_ * c1) / (jnp.sqrt(v_ * c2) + ADAM_EPS) + ADAM_WD * w_)
        return g, delta, m_, v_

    flat = recv.reshape(N_DEV * rows, LANES)
    ins = [Op(flat, LANES, 0, s * per) for s in range(N_DEV)] + [w, m, v]
    return rowwise("adamw", fn, ins, [], [(rows, LANES, F32)] * 4, tm=tm)


def kernel(x, positions, ln1_g, ln1_b, ffn1_gate, ffn1_up, ffn1_down, w_in, conv_w, conv_b, dt_bias, a_log, d_skip, attn_norm_w, ssd_norm_w, w_out, ln2_g, ln2_b, ffn2_gate, ffn2_up, ffn2_down, ln3_g, ln3_b, loss_target, m_ln1_g, m_ln1_b, m_ffn1_gate, m_ffn1_up, m_ffn1_down, m_w_in, m_conv_w, m_conv_b, m_dt_bias, m_a_log, m_d_skip, m_attn_norm_w, m_ssd_norm_w, m_w_out, m_ln2_g, m_ln2_b, m_ffn2_gate, m_ffn2_up, m_ffn2_down, m_ln3_g, m_ln3_b, v_ln1_g, v_ln1_b, v_ffn1_gate, v_ffn1_up, v_ffn1_down, v_w_in, v_conv_w, v_conv_b, v_dt_bias, v_a_log, v_d_skip, v_attn_norm_w, v_ssd_norm_w, v_w_out, v_ln2_g, v_ln2_b, v_ffn2_gate, v_ffn2_up, v_ffn2_down, v_ln3_g, v_ln3_b):
    args = dict(locals())
    wl = {k: args[k] for k in WEIGHTS}
    ml = {k: args["m_" + k] for k in WEIGHTS}
    vl = {k: args["v_" + k] for k in WEIGHTS}
    shapes = {k: wl[k].shape for k in WEIGHTS}
    b, s, dm = x.shape
    t = b * s

    w_pack = pack_local(wl)
    cw = conv_w.reshape(-1)
    cw_hi = cw.astype(BF16).astype(F32)
    cw_mid = (cw - cw_hi).astype(BF16)
    cw_lo = (cw - cw_hi - cw_mid.astype(F32)).astype(BF16)
    extra = jnp.concatenate([cw_mid, cw_lo])
    extra = jnp.pad(extra, (0, 16 * LANES - extra.size)).reshape(16, LANES)
    gathered = all_gather_packed(jnp.concatenate([w_pack.astype(BF16), extra], axis=0))
    p = unpack_gathered(gathered[:, :w_pack.shape[0]], shapes)
    for k in REPLICATED:
        p[k] = wl[k].reshape(1, -1)
    ex = gathered[:, w_pack.shape[0]:].reshape(N_DEV, -1).astype(F32)
    cshape = (N_DEV,) + conv_w.shape[1:]
    low = (ex[:, :cw.size] + ex[:, cw.size:2 * cw.size]).reshape(cshape)
    p['conv_w'] = p['conv_w'].astype(F32) + jnp.transpose(low, (1, 0, 2)).reshape(CONV_WIDTH, D_CONV)

    x2 = x.reshape(t, dm)
    cosv, sinv = rope_tables(positions)
    h1, res1 = ffn_fwd("ffn1", x2, p['ffn1_gate'], p['ffn1_up'], p['ffn1_down'], p['ln1_g'], p['ln1_b'])
    mix, resm = mixer_fwd(h1, p, cosv, sinv, b)
    h2, = rowwise("ln2", functools.partial(_resid_ln, 1.0), [h1, mix], [p['ln2_g'], p['ln2_b']], [(t, dm, F32)])
    h3, res3 = ffn_fwd("ffn2", h2, p['ffn2_gate'], p['ffn2_up'], p['ffn2_down'], p['ln3_g'], p['ln3_b'])

    def loss_fn(y, tgt):
        e = y - tgt
        return e * (1.0 / dm), jnp.sum(e * e, axis=0, keepdims=True)

    dh3, sq = rowwise("loss", loss_fn, [h3, loss_target.reshape(t, dm)], [], [(t, dm, F32)], accs=[(1, dm)])
    loss = lax.psum(jnp.sum(sq) * (0.5 / dm), AXES)

    small = {}
    dh2_res, df2, small['ln3_g'], small['ln3_b'] = resid_ln_bwd("ln3_bwd", 0.5, h2, res3[4], p['ln3_g'], p['ln3_b'], dh3)
    full = {}
    dh2, full['ffn2_gate'], full['ffn2_up'], full['ffn2_down'] = ffn_bwd("ffn2", res3, p['ffn2_gate'], p['ffn2_up'],
                                                                       p['ffn2_down'], df2, dh2_res)
    dh1_res, dmix, small['ln2_g'], small['ln2_b'] = resid_ln_bwd("ln2_bwd", 1.0, h1, mix, p['ln2_g'], p['ln2_b'], dh2)
    dh1, gm = mixer_bwd(resm, p, dmix, dh1_res, b)
    for k in ('w_in', 'w_out', 'conv_w'):
        full[k] = gm[k]
    for k in ('conv_b', 'dt_bias', 'a_log', 'd_skip', 'attn_norm_w', 'ssd_norm_w'):
        small[k] = gm[k]
    dx_res, df1, small['ln1_g'], small['ln1_b'] = resid_ln_bwd("ln1_bwd", 0.5, x2, res1[4], p['ln1_g'], p['ln1_b'], dh1)
    dx, full['ffn1_gate'], full['ffn1_up'], full['ffn1_down'] = ffn_bwd("ffn1", res1, p['ffn1_gate'], p['ffn1_up'],
                                                                      p['ffn1_down'], df1, dx_res)

    recv = all_to_all_packed(pack_grads(full, small, shapes))
    g_pack, d_pack, m_pack, v_pack = adamw_packed(recv, w_pack, pack_local(ml), pack_local(vl))
    outs = [unpack_local(a, shapes) for a in (g_pack, d_pack, m_pack, v_pack)]
    return (loss, dx.reshape(b, s, dm), *[o[k] for o in outs for k in WEIGHTS])
```

```python
import functools
import math

import jax
import jax.numpy as jnp
import numpy as np
from jax import lax
from jax.experimental import pallas as pl
from jax.experimental.pallas import tpu as pltpu

F32, BF16 = jnp.float32, jnp.bfloat16
HI = lax.Precision.HIGHEST
MESH = pl.DeviceIdType.MESH
AXES = ("x", "y", "c")
N_DEV = 8

D_MODEL = 1024
SEQ = 2048
HEAD_DIM = 64
N_HEADS = 12
D_ATTN = N_HEADS * HEAD_DIM
DILATIONS = (1, 4, 16)
ATTN_BLOCK = 128
ROPE_THETA = 500000.0
ROPE_DIM = 16
D_SSD = 768
SSD_GROUPS = 4
SSD_STATE = 128
SSD_CHUNK = 128
D_BC = SSD_GROUPS * SSD_STATE
D_CONV = D_SSD + 2 * D_BC
CONV_WIDTH = 4
D_QKVZ = 3 * D_ATTN + D_SSD
D_IN_PROJ = D_QKVZ + D_CONV + N_HEADS
D_FF = 2816
ALPHA = 2.0 ** 0.25
LN_EPS = 1e-5
RMS_EPS = 1e-6
ADAM_LR, ADAM_B1, ADAM_B2, ADAM_EPS, ADAM_WD, ADAM_STEP = 0.001, 0.9, 0.999, 1e-08, 0.01, 10

LANES = 128
VMEM_LIMIT = 52 * 1024 * 1024
NEG = -1e30

WEIGHTS = ['ln1_g', 'ln1_b', 'ffn1_gate', 'ffn1_up', 'ffn1_down', 'w_in', 'conv_w', 'conv_b', 'dt_bias', 'a_log',
           'd_skip', 'attn_norm_w', 'ssd_norm_w', 'w_out', 'ln2_g', 'ln2_b', 'ffn2_gate', 'ffn2_up', 'ffn2_down',
           'ln3_g', 'ln3_b']
COL_SHARDED = ('ffn1_gate', 'ffn1_up', 'conv_w', 'ffn2_gate', 'ffn2_up')
ROW_SHARDED = ('ffn1_down', 'w_in', 'w_out', 'ffn2_down')
SHARDED = tuple(n for n in WEIGHTS if n in COL_SHARDED or n in ROW_SHARDED)
REPLICATED = tuple(n for n in WEIGHTS if n not in SHARDED)
FF_SHARD = D_FF // N_DEV
FF_PAD = -(-FF_SHARD // LANES) * LANES
D_FF_INT = N_DEV * FF_PAD


def _cparams(sem=None):
    return pltpu.CompilerParams(dimension_semantics=sem, vmem_limit_bytes=VMEM_LIMIT)


def _tile(n, prefs):
    for p in prefs:
        if n % p == 0:
            return p
    return n


class Op:
    def __init__(self, arr, bw=None, cb=0, ro=0):
        self.arr, self.bw, self.cb, self.ro = arr, (arr.shape[1] if bw is None else bw), cb, ro


def _op(a):
    return a if isinstance(a, Op) else Op(a)


def rowwise(name, fn, ins, consts, outs, accs=(), tm=256):
    ins = [_op(a) for a in ins]
    rows = outs[0][0]
    n_in, n_c, n_o, n_a = len(ins), len(consts), len(outs), len(accs)
    tm = min(tm, rows)
    assert rows % tm == 0, (name, rows, tm)

    def body(*refs):
        vals = [r[...] for r in refs[:n_in + n_c]]
        res = fn(*vals)
        res = res if isinstance(res, (tuple, list)) else (res,)
        o_refs = refs[n_in + n_c:n_in + n_c + n_o]
        a_refs = refs[n_in + n_c + n_o:]
        for r, v in zip(o_refs, res[:n_o]):
            r[...] = v.astype(r.dtype)
        if n_a:
            @pl.when(pl.program_id(0) == 0)
            def _():
                for r in a_refs:
                    r[...] = jnp.zeros(r.shape, r.dtype)
            for r, v in zip(a_refs, res[n_o:]):
                r[...] += v

    in_specs = [pl.BlockSpec((tm, o.bw), functools.partial(lambda i, o: (i + o.ro, o.cb), o=o)) for o in ins]
    in_specs += [pl.BlockSpec(c.shape, functools.partial(lambda i, nd: (0,) * nd, nd=c.ndim)) for c in consts]
    out_specs = [pl.BlockSpec((tm, w), lambda i: (i, 0)) for (_, w, _) in outs]
    out_specs += [pl.BlockSpec(s, functools.partial(lambda i, nd: (0,) * nd, nd=len(s))) for s in accs]
    out_shape = [jax.ShapeDtypeStruct((r, w), dt) for (r, w, dt) in outs]
    out_shape += [jax.ShapeDtypeStruct(s, F32) for s in accs]
    res = pl.pallas_call(
        body, name=name, grid=(rows // tm,), in_specs=in_specs, out_specs=out_specs, out_shape=out_shape,
        compiler_params=_cparams(("arbitrary",) if n_a else ("parallel",)),
    )(*[o.arr for o in ins], *consts)
    return res


def mm(name, pairs, n_out, add=None, add_scale=1.0, tm=256, tn=None):
    ops = [_op(a) for a, _, _ in pairs]
    m = ops[0].arr.shape[0]
    tn = tn or _tile(n_out, (1408, 1024, 896, 768, 512, 256, 128))
    n_p = len(pairs)

    def body(*refs):
        acc = None
        for k in range(n_p):
            a = refs[2 * k][...].astype(BF16)
            b = refs[2 * k + 1][...].astype(BF16)
            dn = (((1,), (0,)), ((), ())) if pairs[k][2] == 'nn' else (((1,), (1,)), ((), ()))
            d = lax.dot_general(a, b, dn, preferred_element_type=F32)
            acc = d if acc is None else acc + d
        if add is not None:
            acc = acc + add_scale * refs[2 * n_p][...]
        refs[-1][...] = acc

    in_specs, args = [], []
    for o, (_, b, mode) in zip(ops, pairs):
        in_specs.append(pl.BlockSpec((tm, o.bw), functools.partial(lambda j, i, o: (i, o.cb), o=o)))
        args.append(o.arr)
        if mode == 'nn':
            assert b.shape == (o.bw, n_out), (name, b.shape, o.bw, n_out)
            in_specs.append(pl.BlockSpec((o.bw, tn), lambda j, i: (0, j)))
        else:
            assert b.shape == (n_out, o.bw), (name, b.shape, o.bw, n_out)
            in_specs.append(pl.BlockSpec((tn, o.bw), lambda j, i: (j, 0)))
        args.append(b)
    if add is not None:
        in_specs.append(pl.BlockSpec((tm, tn), lambda j, i: (i, j)))
        args.append(add)
    return pl.pallas_call(
        body, name=name, grid=(n_out // tn, m // tm), in_specs=in_specs,
        out_specs=pl.BlockSpec((tm, tn), lambda j, i: (i, j)),
        out_shape=jax.ShapeDtypeStruct((m, n_out), F32),
        compiler_params=_cparams(("parallel", "parallel")),
    )(*args)


def mm_tn(name, a, b, tt=512):
    a, b = _op(a), _op(b)
    t = a.arr.shape[0]
    k, n = a.bw, b.bw
    tk = _tile(k, (512, 896, 768, 256, 128))
    tn = _tile(n, (1408, 1024, 896, 768, 512, 256, 128))
    tt = min(tt, t)

    def body(a_ref, b_ref, o_ref):
        @pl.when(pl.program_id(2) == 0)
        def _():
            o_ref[...] = jnp.zeros(o_ref.shape, F32)
        o_ref[...] += lax.dot_general(a_ref[...].astype(BF16), b_ref[...].astype(BF16), (((0,), (0,)), ((), ())),
                                      preferred_element_type=F32)

    return pl.pallas_call(
        body, name=name, grid=(k // tk, n // tn, t // tt),
        in_specs=[pl.BlockSpec((tt, tk), functools.partial(lambda kk, nn, s, o: (s, o.cb * (o.bw // tk) + kk), o=a)),
                  pl.BlockSpec((tt, tn), functools.partial(lambda kk, nn, s, o: (s, o.cb * (o.bw // tn) + nn), o=b))],
        out_specs=pl.BlockSpec((tk, tn), lambda kk, nn, s: (kk, nn)),
        out_shape=jax.ShapeDtypeStruct((k, n), F32),
        compiler_params=_cparams(("parallel", "parallel", "arbitrary")),
    )(a.arr, b.arr)


def _sigmoid(x):
    return 1.0 / (1.0 + jnp.exp(-x))


def _silu(x):
    return x * _sigmoid(x)


def _softplus(x):
    return jnp.maximum(x, 0.0) + jnp.log(1.0 + jnp.exp(-jnp.abs(x)))


def _act(g, u):
    return _silu(g) * u


def _resid_ln(scale, h, branch, g, b):
    r = ALPHA * h + scale * branch
    mu = jnp.mean(r, axis=-1, keepdims=True)
    var = jnp.mean(jnp.square(r - mu), axis=-1, keepdims=True)
    return (r - mu) * lax.rsqrt(var + LN_EPS) * g + b


def _rms(t, w):
    return t * lax.rsqrt(jnp.mean(t * t, axis=-1, keepdims=True) + RMS_EPS) * w


def _branch_weights(l1, l2, l3):
    m = jnp.maximum(jnp.maximum(l1, l2), l3)
    e1, e2, e3 = jnp.exp(l1 - m), jnp.exp(l2 - m), jnp.exp(l3 - m)
    inv = 1.0 / (e1 + e2 + e3)
    return e1 * inv, e2 * inv, e3 * inv


def _gate(y, xs, z, dskip, w):
    return _rms((y + dskip * xs) * _silu(z), w)


def _rot(x):
    d = lax.broadcasted_iota(jnp.int32, x.shape, 1) % HEAD_DIM
    up = pltpu.roll(x, x.shape[1] - ROPE_DIM // 2, 1)
    down = jnp.where(d < ROPE_DIM, pltpu.roll(x, ROPE_DIM // 2, 1), 0.0)
    return jnp.where(d < ROPE_DIM // 2, up, down)


def ffn_fwd(tag, h, wg, wu, wd, ln_g, ln_b):
    t = h.shape[0]
    nf = wg.shape[1]
    g = mm(f"{tag}_gate", [(h, wg, 'nn')], nf)
    u = mm(f"{tag}_up", [(h, wu, 'nn')], nf)
    a, = rowwise(f"{tag}_act", _act, [g, u], [], [(t, nf, F32)])
    f = mm(f"{tag}_down", [(a, wd, 'nn')], D_MODEL)
    out, = rowwise(f"{tag}_ln", functools.partial(_resid_ln, 0.5), [h, f], [ln_g, ln_b], [(t, D_MODEL, F32)])
    return out, (h, g, u, a, f)


def resid_ln_bwd(name, scale, h, branch, ln_g, ln_b, dout, extra=None):
    t = h.shape[0]

    def fn(h_, br_, do_, *rest):
        g_, b_ = rest[-2], rest[-1]
        _, vjp = jax.vjp(functools.partial(_resid_ln, scale), h_, br_, g_, b_)
        dh, dbr, dg, db = vjp(do_)
        if extra is not None:
            dh = dh + rest[0]
        return dh, dbr, dg, db

    ins = [h, branch, dout] + ([extra] if extra is not None else [])
    return rowwise(name, fn, ins, [ln_g, ln_b], [(t, D_MODEL, F32), (t, D_MODEL, F32)],
                   accs=[(1, D_MODEL), (1, D_MODEL)])


def ffn_bwd(tag, res, wg, wu, wd, df, dh_resid):
    h, g, u, a, _ = res
    t = h.shape[0]
    nf = wg.shape[1]
    da = mm(f"{tag}_bwd_da", [(df, wd, 'nt')], nf)
    dwd = mm_tn(f"{tag}_bwd_dwd", a, df)

    def act_bwd(g_, u_, da_):
        _, vjp = jax.vjp(_act, g_, u_)
        return vjp(da_)

    dg, du = rowwise(f"{tag}_bwd_act", act_bwd, [g, u, da], [], [(t, nf, F32), (t, nf, F32)])
    dh = mm(f"{tag}_bwd_dh", [(dg, wg, 'nt'), (du, wu, 'nt')], D_MODEL, add=dh_resid)
    dwg = mm_tn(f"{tag}_bwd_dwg", h, dg)
    dwu = mm_tn(f"{tag}_bwd_dwu", h, du)
    return dh, dwg, dwu, dwd


def rope_tables(positions):
    inv_freq = ROPE_THETA ** (-jnp.arange(0, ROPE_DIM, 2, dtype=F32) / ROPE_DIM)
    ang = positions.reshape(-1, 1).astype(F32) * inv_freq
    c, s = jnp.cos(ang), jnp.sin(ang)
    t = ang.shape[0]
    cosv = jnp.concatenate([c, c, jnp.ones((t, HEAD_DIM - ROPE_DIM), F32)], axis=1)
    sinv = jnp.concatenate([-s, s, jnp.zeros((t, HEAD_DIM - ROPE_DIM), F32)], axis=1)
    return jnp.tile(cosv, (1, 2)), jnp.tile(sinv, (1, 2))


def _pair_masks():
    lane = lax.broadcasted_iota(jnp.int32, (1, LANES), 1)
    return (lane < HEAD_DIM, lane >= HEAD_DIM)


def _band_masks():
    row = lax.broadcasted_iota(jnp.int32, (ATTN_BLOCK, ATTN_BLOCK), 0)
    col = lax.broadcasted_iota(jnp.int32, (ATTN_BLOCK, ATTN_BLOCK), 1)
    return col >= row, col <= row


def _residue_blocks():
    out = []
    for g, d in enumerate(DILATIONS):
        for r in range(d):
            for i in range(SEQ // d // ATTN_BLOCK):
                rows = lambda j: pl.ds(r + j * ATTN_BLOCK * d, ATTN_BLOCK, stride=d) if d > 1 else pl.ds(j * ATTN_BLOCK, ATTN_BLOCK)
                out.append((g, rows(i), rows(i - 1) if i > 0 else None))
    return out


_NT = (((1,), (1,)), ((), ()))
_NN = (((1,), (0,)), ((), ()))
_TN = (((0,), (0,)), ((), ()))


def _dot(a, b, dn, precision=None):
    return lax.dot_general(a, b, dn, preferred_element_type=F32, precision=precision)


N_HEAD_PAIRS = D_ATTN // LANES
SCALE = HEAD_DIM ** -0.5


def _attn_specs(b):
    col = lambda cb: pl.BlockSpec((SEQ, LANES), lambda bb, hp: (bb, cb + hp))
    tab = pl.BlockSpec((SEQ, LANES), lambda bb, hp: (bb, 0))
    return col, tab


def attn_fwd(qkvz, cosv, sinv, b):
    t = qkvz.shape[0]
    col, tab = _attn_specs(b)
    blocks = _residue_blocks()

    def body(q_ref, k_ref, v_ref, c_ref, s_ref, o_ref, l1_ref, l2_ref, l3_ref, qr, kr, o1, o2, o3):
        l_refs, o_scr = (l1_ref, l2_ref, l3_ref), (o1, o2, o3)
        c, s = c_ref[...], s_ref[...]
        q, k = q_ref[...], k_ref[...]
        qr[...] = q * c + _rot(q) * s
        kr[...] = k * c + _rot(k) * s
        prev_ok, cur_ok = _band_masks()
        masks = _pair_masks()
        for g, cur, prev in blocks:
            q2, kc2, vc2 = qr[cur, :], kr[cur, :].astype(BF16), v_ref[cur, :]
            if prev is not None:
                kp2, vp2 = kr[prev, :].astype(BF16), v_ref[prev, :]
            acc = jnp.zeros((ATTN_BLOCK, LANES), F32)
            lse = jnp.zeros((ATTN_BLOCK, LANES), F32)
            for m in masks:
                qm = jnp.where(m, q2, 0.0).astype(BF16)
                sc = jnp.where(cur_ok, _dot(qm, kc2, _NT) * SCALE, NEG)
                mx = jnp.max(sc, axis=1, keepdims=True)
                if prev is not None:
                    sp = jnp.where(prev_ok, _dot(qm, kp2, _NT) * SCALE, NEG)
                    mx = jnp.maximum(mx, jnp.max(sp, axis=1, keepdims=True))
                pc = jnp.exp(sc - mx)
                den = jnp.sum(pc, axis=1, keepdims=True)
                oh = _dot(pc.astype(BF16), jnp.where(m, vc2, 0.0).astype(BF16), _NN)
                if prev is not None:
                    pp = jnp.exp(sp - mx)
                    den = den + jnp.sum(pp, axis=1, keepdims=True)
                    oh = oh + _dot(pp.astype(BF16), jnp.where(m, vp2, 0.0).astype(BF16), _NN)
                acc = acc + oh * (1.0 / den)
                lse = jnp.where(m, mx + jnp.log(den), lse)
            o_scr[g][cur, :] = acc
            l_refs[g][cur, :] = lse
        w1, w2, w3 = _branch_weights(l1_ref[...], l2_ref[...], l3_ref[...])
        o_ref[...] = w1 * o1[...] + w2 * o2[...] + w3 * o3[...]

    shp = jax.ShapeDtypeStruct((t, D_ATTN), F32)
    return pl.pallas_call(
        body, name="attn_fwd", grid=(b, N_HEAD_PAIRS),
        in_specs=[col(0), col(N_HEAD_PAIRS), col(2 * N_HEAD_PAIRS), tab, tab],
        out_specs=[col(0)] * 4, out_shape=[shp] * 4,
        scratch_shapes=[pltpu.VMEM((SEQ, LANES), F32)] * 5,
        compiler_params=_cparams(("parallel", "parallel")),
    )(qkvz, qkvz, qkvz, cosv, sinv)


def attn_bwd(qkvz, cosv, sinv, dmix, mixed, lses, b):
    t = qkvz.shape[0]
    col, tab = _attn_specs(b)
    blocks = _residue_blocks()
    hd = np.arange(LANES) // HEAD_DIM
    head_ones = jnp.asarray((hd[:, None] == hd[None, :]).astype(np.float32))

    def body(q_ref, k_ref, v_ref, c_ref, s_ref, dm_ref, mx_ref, l1_ref, l2_ref, l3_ref, ones_ref,
             dq_ref, dk_ref, dv_ref, qr, kr, do1, do2, do3, dd1, dd2, dd3):
        l_refs, do_scr, dd_scr = (l1_ref, l2_ref, l3_ref), (do1, do2, do3), (dd1, dd2, dd3)
        c, s = c_ref[...], s_ref[...]
        q, k = q_ref[...], k_ref[...]
        qr[...] = q * c + _rot(q) * s
        kr[...] = k * c + _rot(k) * s
        dm = dm_ref[...]
        tot = _dot(dm * mx_ref[...], ones_ref[...], _NN, HI)
        for w, do_g, dd_g in zip(_branch_weights(l1_ref[...], l2_ref[...], l3_ref[...]), do_scr, dd_scr):
            do_g[...] = w * dm
            dd_g[...] = w * tot
        dq_ref[...] = jnp.zeros((SEQ, LANES), F32)
        dk_ref[...] = jnp.zeros((SEQ, LANES), F32)
        dv_ref[...] = jnp.zeros((SEQ, LANES), F32)
        prev_ok, cur_ok = _band_masks()
        masks = _pair_masks()
        zero = jnp.zeros((ATTN_BLOCK, LANES), F32)
        for g, cur, prev in blocks:
            q2, do2_, l2, dd2_ = qr[cur, :], do_scr[g][cur, :], l_refs[g][cur, :], dd_scr[g][cur, :]
            kc2, vc2 = kr[cur, :].astype(BF16), v_ref[cur, :].astype(BF16)
            if prev is not None:
                kp2, vp2 = kr[prev, :].astype(BF16), v_ref[prev, :].astype(BF16)
            l2s, dd2s = pltpu.roll(l2, HEAD_DIM, 1), pltpu.roll(dd2_, HEAD_DIM, 1)
            dq2, dkc, dvc, dkp, dvp = zero, zero, zero, zero, zero
            for m in masks:
                qm = jnp.where(m, q2, 0.0).astype(BF16)
                dom = jnp.where(m, do2_, 0.0).astype(BF16)
                lrep, ddrep = jnp.where(m, l2, l2s), jnp.where(m, dd2_, dd2s)
                pc = jnp.exp(jnp.where(cur_ok, _dot(qm, kc2, _NT) * SCALE, NEG) - lrep)
                dsc = (pc * (_dot(dom, vc2, _NT) - ddrep) * SCALE).astype(BF16)
                dq2 = dq2 + _dot(dsc, jnp.where(m, kc2, 0), _NN)
                dkc = dkc + _dot(dsc, qm, _TN)
                dvc = dvc + _dot(pc.astype(BF16), dom, _TN)
                if prev is not None:
                    pp = jnp.exp(jnp.where(prev_ok, _dot(qm, kp2, _NT) * SCALE, NEG) - lrep)
                    dsp = (pp * (_dot(dom, vp2, _NT) - ddrep) * SCALE).astype(BF16)
                    dq2 = dq2 + _dot(dsp, jnp.where(m, kp2, 0), _NN)
                    dkp = dkp + _dot(dsp, qm, _TN)
                    dvp = dvp + _dot(pp.astype(BF16), dom, _TN)
            dq_ref[cur, :] += dq2
            dk_ref[cur, :] += dkc
            dv_ref[cur, :] += dvc
            if prev is not None:
                dk_ref[prev, :] += dkp
                dv_ref[prev, :] += dvp
        dq, dk = dq_ref[...], dk_ref[...]
        dq_ref[...] = dq * c + _rot(dq * s)
        dk_ref[...] = dk * c + _rot(dk * s)

    shp = jax.ShapeDtypeStruct((t, D_ATTN), F32)
    return pl.pallas_call(
        body, name="attn_bwd", grid=(b, N_HEAD_PAIRS),
        in_specs=[col(0), col(N_HEAD_PAIRS), col(2 * N_HEAD_PAIRS), tab, tab, col(0), col(0), col(0), col(0), col(0),
                  pl.BlockSpec((LANES, LANES), lambda bb, hp: (0, 0))],
        out_specs=[col(0)] * 3, out_shape=[shp] * 3,
        scratch_shapes=[pltpu.VMEM((SEQ, LANES), F32)] * 8,
        compiler_params=_cparams(("parallel", "parallel")),
    )(qkvz, qkvz, qkvz, cosv, sinv, dmix, mixed, *lses, head_ones)


def attn_norm_fwd(mixed, norm_w):
    return rowwise("attn_norm", _rms, [mixed], [norm_w], [(mixed.shape[0], D_ATTN, F32)])[0]


def attn_norm_bwd(dout, mixed, norm_w):
    def fn(dy, mx, w):
        _, vjp = jax.vjp(_rms, mx, w)
        return vjp(dy)

    return rowwise("attn_norm_bwd", fn, [dout, mixed], [norm_w], [(dout.shape[0], D_ATTN, F32)], accs=[(1, D_ATTN)])


CONV_TM = 256
HALO = 8


def conv_fwd(u, w, bias):
    t = u.shape[0]
    tm, per_seq = CONV_TM, SEQ // CONV_TM

    def body(u_ref, h_ref, w_ref, b_ref, xs_ref, bm_ref, cm_ref, scr):
        first = pl.program_id(0) % per_seq == 0
        scr[0:HALO, :] = jnp.where(first, 0.0, h_ref[...])
        scr[HALO:, :] = u_ref[...]
        acc = b_ref[...]
        for k in range(CONV_WIDTH):
            acc = acc + w_ref[k:k + 1, :] * scr[pl.ds(HALO - CONV_WIDTH + 1 + k, tm), :]
        y = _silu(acc)
        xs_ref[...] = y[:, :D_SSD]
        bm_ref[...] = y[:, D_SSD:D_SSD + D_BC]
        cm_ref[...] = y[:, D_SSD + D_BC:]

    return pl.pallas_call(
        body, name="conv_fwd", grid=(t // tm,),
        in_specs=[pl.BlockSpec((tm, D_CONV), lambda i: (i, 0)),
                  pl.BlockSpec((HALO, D_CONV), lambda i: (jnp.maximum(i * (tm // HALO) - 1, 0), 0)),
                  pl.BlockSpec((CONV_WIDTH, D_CONV), lambda i: (0, 0)), pl.BlockSpec((1, D_CONV), lambda i: (0, 0))],
        out_specs=[pl.BlockSpec((tm, D_SSD), lambda i: (i, 0)), pl.BlockSpec((tm, D_BC), lambda i: (i, 0)),
                   pl.BlockSpec((tm, D_BC), lambda i: (i, 0))],
        out_shape=[jax.ShapeDtypeStruct((t, D_SSD), F32), jax.ShapeDtypeStruct((t, D_BC), F32),
                   jax.ShapeDtypeStruct((t, D_BC), F32)],
        scratch_shapes=[pltpu.VMEM((tm + HALO, D_CONV), F32)],
        compiler_params=_cparams(("parallel",)),
    )(u, u, w, bias)


def conv_bwd(u, w, bias, dxs, dbm, dcm):
    t = u.shape[0]
    tm, per_seq = CONV_TM, SEQ // CONV_TM
    n_tiles = t // tm

    def body1(u_ref, h_ref, dxs_ref, dbm_ref, dcm_ref, w_ref, b_ref, dz_ref, dw_ref, db_ref, scr):
        i = pl.program_id(0)
        first = i % per_seq == 0
        scr[0:HALO, :] = jnp.where(first, 0.0, h_ref[...])
        scr[HALO:, :] = u_ref[...]
        acc = b_ref[...]
        for k in range(CONV_WIDTH):
            acc = acc + w_ref[k:k + 1, :] * scr[pl.ds(HALO - CONV_WIDTH + 1 + k, tm), :]
        sig = _sigmoid(acc)
        dy = jnp.concatenate([dxs_ref[...], dbm_ref[...], dcm_ref[...]], axis=1)
        dz = dy * sig * (1.0 + acc * (1.0 - sig))
        dz_ref[...] = dz

        @pl.when(i == 0)
        def _():
            dw_ref[...] = jnp.zeros(dw_ref.shape, F32)
            db_ref[...] = jnp.zeros(db_ref.shape, F32)
        db_ref[...] += jnp.sum(dz, axis=0, keepdims=True)
        for k in range(CONV_WIDTH):
            dw_ref[k:k + 1, :] += jnp.sum(dz * scr[pl.ds(HALO - CONV_WIDTH + 1 + k, tm), :], axis=0, keepdims=True)

    dz, dw, db = pl.pallas_call(
        body1, name="conv_bwd_dz", grid=(n_tiles,),
        in_specs=[pl.BlockSpec((tm, D_CONV), lambda i: (i, 0)),
                  pl.BlockSpec((HALO, D_CONV), lambda i: (jnp.maximum(i * (tm // HALO) - 1, 0), 0)),
                  pl.BlockSpec((tm, D_SSD), lambda i: (i, 0)), pl.BlockSpec((tm, D_BC), lambda i: (i, 0)),
                  pl.BlockSpec((tm, D_BC), lambda i: (i, 0)),
                  pl.BlockSpec((CONV_WIDTH, D_CONV), lambda i: (0, 0)), pl.BlockSpec((1, D_CONV), lambda i: (0, 0))],
        out_specs=[pl.BlockSpec((tm, D_CONV), lambda i: (i, 0)), pl.BlockSpec((CONV_WIDTH, D_CONV), lambda i: (0, 0)),
                   pl.BlockSpec((1, D_CONV), lambda i: (0, 0))],
        out_shape=[jax.ShapeDtypeStruct((t, D_CONV), F32), jax.ShapeDtypeStruct((CONV_WIDTH, D_CONV), F32),
                   jax.ShapeDtypeStruct((1, D_CONV), F32)],
        scratch_shapes=[pltpu.VMEM((tm + HALO, D_CONV), F32)],
        compiler_params=_cparams(("arbitrary",)),
    )(u, u, dxs, dbm, dcm, w, bias)

    def body2(dz_ref, n_ref, w_ref, du_ref, scr):
        last = pl.program_id(0) % per_seq == per_seq - 1
        scr[0:tm, :] = dz_ref[...]
        scr[tm:, :] = jnp.where(last, 0.0, n_ref[...])
        acc = jnp.zeros((tm, D_CONV), F32)
        for k in range(CONV_WIDTH):
            acc = acc + w_ref[k:k + 1, :] * scr[pl.ds(CONV_WIDTH - 1 - k, tm), :]
        du_ref[...] = acc

    du = pl.pallas_call(
        body2, name="conv_bwd_du", grid=(n_tiles,),
        in_specs=[pl.BlockSpec((tm, D_CONV), lambda i: (i, 0)),
                  pl.BlockSpec((HALO, D_CONV), lambda i: (jnp.minimum((i + 1) * (tm // HALO), t // HALO - 1), 0)),
                  pl.BlockSpec((CONV_WIDTH, D_CONV), lambda i: (0, 0))],
        out_specs=pl.BlockSpec((tm, D_CONV), lambda i: (i, 0)),
        out_shape=jax.ShapeDtypeStruct((t, D_CONV), F32),
        scratch_shapes=[pltpu.VMEM((tm + HALO, D_CONV), F32)],
        compiler_params=_cparams(("parallel",)),
    )(dz, dz, w)
    return du, dw, db


Q = SSD_CHUNK
N_PAIRS = D_SSD // LANES
HEADS_PER_GROUP = N_HEADS // SSD_GROUPS


def _head_expand():
    e = np.zeros((LANES, N_HEADS * LANES), np.float32)
    for j in range(N_HEADS):
        e[j, j * LANES:(j + 1) * LANES] = 1.0
    return jnp.asarray(e)


def _pad_lanes(v, fill=0.0):
    row = jnp.pad(v.reshape(1, -1).astype(F32), ((0, 0), (0, LANES - v.size)), constant_values=fill)
    return row, row.reshape(LANES, 1)


def _ssd_common(dtr_ref, dtrt_ref, bias_r, bias_c, alog_r, alog_c, e_ref):
    row = lax.broadcasted_iota(jnp.int32, (Q, Q), 0)
    col = lax.broadcasted_iota(jnp.int32, (Q, Q), 1)
    tril = row >= col
    lane = lax.broadcasted_iota(jnp.int32, (1, LANES), 1)
    a_r = jnp.where(lane < N_HEADS, -jnp.exp(alog_r[...]), 0.0)
    sub = lax.broadcasted_iota(jnp.int32, (LANES, 1), 0)
    a_c = jnp.where(sub < N_HEADS, -jnp.exp(alog_c[...]), 0.0)
    dt = _softplus(dtr_ref[...] + bias_r[...])
    cs = _dot(tril.astype(F32), dt * a_r, _NN, HI)
    dt_rep = _dot(dt, e_ref[...], _NN, HI)
    cs_rep = _dot(cs, e_ref[...], _NN, HI)
    dtt = _softplus(dtrt_ref[...] + bias_c[...])
    cst = _dot(dtt * a_c, (row <= col).astype(F32), _NN, HI)
    return tril, lane, a_r, dt, dt_rep, cs_rep, cst


def _ssd_specs(b, nc, rev):
    ci = (lambda c: nc - 1 - c) if rev else (lambda c: c)
    rows = lambda w: pl.BlockSpec((Q, w), lambda bb, c: (bb * nc + ci(c), 0))
    dtt = pl.BlockSpec((LANES, Q), lambda bb, c: (0, bb * nc + ci(c)))
    const = lambda s: pl.BlockSpec(s, lambda bb, c: (0,) * len(s))
    state = pl.BlockSpec((None, N_PAIRS, LANES, SSD_STATE), lambda bb, c: (bb * nc + ci(c), 0, 0, 0))
    return rows, dtt, const, state


def ssd_fwd(xs, bm, cm, dtraw, dt_bias, a_log, b):
    t = xs.shape[0]
    nc = SEQ // Q
    rows, dtt_spec, const, state = _ssd_specs(b, nc, False)
    bias_r, bias_c = _pad_lanes(dt_bias)
    alog_r, alog_c = _pad_lanes(a_log)

    def body(xs_ref, b_ref, c_ref, dtr_ref, dtrt_ref, br, bc, ar, ac, e_ref, y_ref, hp_ref, h_scr):
        @pl.when(pl.program_id(1) == 0)
        def _():
            h_scr[...] = jnp.zeros(h_scr.shape, F32)
        tril, lane, _, _, dt_rep, cs_rep, cst = _ssd_common(dtr_ref, dtrt_ref, br, bc, ar, ac, e_ref)
        sub = lax.broadcasted_iota(jnp.int32, (LANES, 1), 0)
        y_acc = [jnp.zeros((Q, LANES), F32) for _ in range(N_PAIRS)]
        h_old = [h_scr[p] for p in range(N_PAIRS)]
        h_new = [jnp.zeros((LANES, SSD_STATE), F32) for _ in range(N_PAIRS)]
        for g in range(SSD_GROUPS):
            bg = b_ref[:, g * SSD_STATE:(g + 1) * SSD_STATE].astype(BF16)
            cg = c_ref[:, g * SSD_STATE:(g + 1) * SSD_STATE].astype(BF16)
            cb = _dot(cg, bg, _NT)
            for j in range(g * HEADS_PER_GROUP, (g + 1) * HEADS_PER_GROUP):
                p, side = j // 2, j % 2
                m = (lane < HEAD_DIM) if side == 0 else (lane >= HEAD_DIM)
                ms = (sub < HEAD_DIM) if side == 0 else (sub >= HEAD_DIM)
                csj = cs_rep[:, j * LANES:(j + 1) * LANES]
                dtj = dt_rep[:, j * LANES:(j + 1) * LANES]
                lmat = jnp.exp(jnp.where(tril, csj - cst[j:j + 1, :], NEG))
                xdt = jnp.where(m, xs_ref[:, p * LANES:(p + 1) * LANES] * dtj, 0.0)
                hm = jnp.where(ms, h_old[p], 0.0)
                ydiag = _dot((cb * lmat).astype(BF16), xdt.astype(BF16), _NN)
                yoff = jnp.exp(csj) * _dot(cg, hm.astype(BF16), _NT)
                y_acc[p] = y_acc[p] + ydiag + yoff
                last = csj[Q - 1:Q, :]
                sj = _dot((xdt * jnp.exp(last - csj)).astype(BF16), bg, _TN)
                h_new[p] = h_new[p] + jnp.exp(last) * hm + sj
        for p in range(N_PAIRS):
            y_ref[:, p * LANES:(p + 1) * LANES] = y_acc[p]
            hp_ref[p] = h_old[p]
            h_scr[p] = h_new[p]

    return pl.pallas_call(
        body, name="ssd_fwd", grid=(b, nc),
        in_specs=[rows(D_SSD), rows(D_BC), rows(D_BC), rows(LANES), dtt_spec, const((1, LANES)), const((LANES, 1)),
                  const((1, LANES)), const((LANES, 1)), const((LANES, N_HEADS * LANES))],
        out_specs=[rows(D_SSD), state],
        out_shape=[jax.ShapeDtypeStruct((t, D_SSD), F32),
                   jax.ShapeDtypeStruct((b * nc, N_PAIRS, LANES, SSD_STATE), F32)],
        scratch_shapes=[pltpu.VMEM((N_PAIRS, LANES, SSD_STATE), F32)],
        compiler_params=_cparams(("parallel", "arbitrary")),
    )(xs, bm, cm, dtraw, dtraw.T, bias_r, bias_c, alog_r, alog_c, _head_expand())


def ssd_bwd(xs, bm, cm, dtraw, dt_bias, a_log, hprev, dy, b):
    t = xs.shape[0]
    nc = SEQ // Q
    rows, dtt_spec, const, state = _ssd_specs(b, nc, True)
    bias_r, bias_c = _pad_lanes(dt_bias)
    alog_r, alog_c = _pad_lanes(a_log)

    def body(xs_ref, b_ref, c_ref, dtr_ref, dtrt_ref, hp_ref, dy_ref, br, bc, ar, ac, e_ref,
             dxs_ref, db_ref, dc_ref, ddt_ref, dbias_ref, dalog_ref, dh_scr):
        first = jnp.logical_and(pl.program_id(0) == 0, pl.program_id(1) == 0)

        @pl.when(pl.program_id(1) == 0)
        def _():
            dh_scr[...] = jnp.zeros(dh_scr.shape, F32)

        @pl.when(first)
        def _():
            dbias_ref[...] = jnp.zeros(dbias_ref.shape, F32)
            dalog_ref[...] = jnp.zeros(dalog_ref.shape, F32)
        tril, lane, a_r, dt, dt_rep, cs_rep, cst = _ssd_common(dtr_ref, dtrt_ref, br, bc, ar, ac, e_ref)
        sub = lax.broadcasted_iota(jnp.int32, (LANES, 1), 0)
        rowq = lax.broadcasted_iota(jnp.int32, (Q, 1), 0)
        ones = jnp.ones((Q, LANES), F32)
        triu = (lax.broadcasted_iota(jnp.int32, (Q, Q), 0) <= lax.broadcasted_iota(jnp.int32, (Q, Q), 1)).astype(F32)
        dxs_acc = [jnp.zeros((Q, LANES), F32) for _ in range(N_PAIRS)]
        dh_in = [dh_scr[p] for p in range(N_PAIRS)]
        h_in = [hp_ref[p] for p in range(N_PAIRS)]
        dh_out = [jnp.zeros((LANES, SSD_STATE), F32) for _ in range(N_PAIRS)]
        ddt = jnp.zeros((Q, LANES), F32)
        dalog = jnp.zeros((1, LANES), F32)
        for g in range(SSD_GROUPS):
            gs = slice(g * SSD_STATE, (g + 1) * SSD_STATE)
            bg, cg = b_ref[:, gs].astype(BF16), c_ref[:, gs].astype(BF16)
            cb = _dot(cg, bg, _NT)
            dcb = jnp.zeros((Q, Q), F32)
            dbg = jnp.zeros((Q, SSD_STATE), F32)
            dcg = jnp.zeros((Q, SSD_STATE), F32)
            for j in range(g * HEADS_PER_GROUP, (g + 1) * HEADS_PER_GROUP):
                p, side = j // 2, j % 2
                m = (lane < HEAD_DIM) if side == 0 else (lane >= HEAD_DIM)
                ms = (sub < HEAD_DIM) if side == 0 else (sub >= HEAD_DIM)
                csj = cs_rep[:, j * LANES:(j + 1) * LANES]
                dtj = dt_rep[:, j * LANES:(j + 1) * LANES]
                lmat = jnp.exp(jnp.where(tril, csj - cst[j:j + 1, :], NEG))
                x2 = jnp.where(m, xs_ref[:, p * LANES:(p + 1) * LANES], 0.0)
                xdt = x2 * dtj
                dym = jnp.where(m, dy_ref[:, p * LANES:(p + 1) * LANES], 0.0)
                hm = jnp.where(ms, h_in[p], 0.0)
                dhm = jnp.where(ms, dh_in[p], 0.0)
                ecs = jnp.exp(csj)
                last = csj[Q - 1:Q, :]
                decay = jnp.exp(last - csj)
                el = jnp.exp(last)
                gmat = cb * lmat
                dymb, xdtb = dym.astype(BF16), xdt.astype(BF16)
                dg = _dot(dymb, xdtb, _NT)
                dxdt = _dot(gmat.astype(BF16), dymb, _TN)
                dcb = dcb + dg * lmat
                ej = dg * gmat
                dcs = _dot(ej, ones, _NN, HI) - _dot(ej, ones, _TN, HI)
                ch = _dot(cg, hm.astype(BF16), _NT)
                dye = dym * ecs
                dcs = dcs + jnp.sum(dye * ch, axis=1, keepdims=True)
                dcg = dcg + _dot(dye.astype(BF16), hm.astype(BF16), _NN)
                dhp = _dot(dye.astype(BF16), cg, _TN)
                wmat = _dot(bg, dhm.astype(BF16), _NT)
                xd = xdt * decay
                dxdt = dxdt + decay * wmat
                ddl = jnp.sum(xd * wmat, axis=1, keepdims=True)
                dlast = jnp.sum(ddl, axis=0, keepdims=True) + el * jnp.sum(jnp.sum(dhm * hm, axis=1, keepdims=True), axis=0, keepdims=True)
                dcs = dcs - ddl + jnp.where(rowq == Q - 1, dlast, 0.0)
                dbg = dbg + _dot(xd.astype(BF16), dhm.astype(BF16), _NN)
                dh_out[p] = dh_out[p] + el * dhm + dhp
                da = _dot(triu, dcs, _NN, HI)
                aj = jnp.sum(jnp.where(lane == j, a_r, 0.0), axis=1, keepdims=True)
                ddtj = da * aj + jnp.sum(dxdt * x2, axis=1, keepdims=True)
                ddt = ddt + jnp.where(lane == j, ddtj, 0.0)
                dalog = dalog + jnp.where(lane == j, jnp.sum(da * dtj, axis=0, keepdims=True) * aj, 0.0)
                dxs_acc[p] = dxs_acc[p] + dxdt * dtj
            dcbb = dcb.astype(BF16)
            dc_ref[:, gs] = dcg + _dot(dcbb, bg, _NN)
            db_ref[:, gs] = dbg + _dot(dcbb, cg, _TN)
        for p in range(N_PAIRS):
            dxs_ref[:, p * LANES:(p + 1) * LANES] = dxs_acc[p]
            dh_scr[p] = dh_out[p]
        ddtraw = ddt * _sigmoid(dtr_ref[...] + br[...])
        ddt_ref[...] = ddtraw
        dbias_ref[...] += jnp.sum(ddtraw, axis=0, keepdims=True)
        dalog_ref[...] += dalog

    return pl.pallas_call(
        body, name="ssd_bwd", grid=(b, nc),
        in_specs=[rows(D_SSD), rows(D_BC), rows(D_BC), rows(LANES), dtt_spec, state, rows(D_SSD), const((1, LANES)),
                  const((LANES, 1)), const((1, LANES)), const((LANES, 1)), const((LANES, N_HEADS * LANES))],
        out_specs=[rows(D_SSD), rows(D_BC), rows(D_BC), rows(LANES), const((1, LANES)), const((1, LANES))],
        out_shape=[jax.ShapeDtypeStruct((t, D_SSD), F32), jax.ShapeDtypeStruct((t, D_BC), F32),
                   jax.ShapeDtypeStruct((t, D_BC), F32), jax.ShapeDtypeStruct((t, LANES), F32),
                   jax.ShapeDtypeStruct((1, LANES), F32), jax.ShapeDtypeStruct((1, LANES), F32)],
        scratch_shapes=[pltpu.VMEM((N_PAIRS, LANES, SSD_STATE), F32)],
        compiler_params=_cparams(("arbitrary", "arbitrary")),
    )(xs, bm, cm, dtraw, dtraw.T, hprev, dy, bias_r, bias_c, alog_r, alog_c, _head_expand())


def _split_w_in(w_in):
    w_dt = jnp.pad(w_in[:, D_QKVZ + D_CONV:], ((0, 0), (0, LANES - N_HEADS)))
    return w_in[:, :D_QKVZ], w_in[:, D_QKVZ:D_QKVZ + D_CONV], w_dt


def mixer_fwd(h, p, cosv, sinv, b):
    t = h.shape[0]
    w_a, w_b, w_c = _split_w_in(p['w_in'])
    qkvz = mm("in_qkvz", [(h, w_a, 'nn')], D_QKVZ)
    xbc = mm("in_xbc", [(h, w_b, 'nn')], D_CONV)
    dtraw = mm("in_dt", [(h, w_c, 'nn')], LANES)
    mixed, *lses = attn_fwd(qkvz, cosv, sinv, b)
    attn = attn_norm_fwd(mixed, p['attn_norm_w'])
    xs, bm, cm = conv_fwd(xbc, p['conv_w'], p['conv_b'])
    y, hprev = ssd_fwd(xs, bm, cm, dtraw, p['dt_bias'], p['a_log'], b)
    dskip = jnp.repeat(p['d_skip'].reshape(-1), HEAD_DIM).reshape(1, D_SSD)
    yg, = rowwise("ssd_gate", _gate, [y, xs, Op(qkvz, D_SSD, 3)], [dskip, p['ssd_norm_w']], [(t, D_SSD, F32)])
    mix = mm("out_proj", [(attn, p['w_out'][:D_ATTN], 'nn'), (yg, p['w_out'][D_ATTN:], 'nn')], D_MODEL)
    res = dict(h=h, qkvz=qkvz, xbc=xbc, dtraw=dtraw, mixed=mixed, lses=lses, attn=attn, xs=xs, bm=bm, cm=cm,
               y=y, hprev=hprev, dskip=dskip, yg=yg, cosv=cosv, sinv=sinv)
    return mix, res


def mixer_bwd(r, p, dmix, dh_resid, b):
    t = dmix.shape[0]
    w_a, w_b, w_c = _split_w_in(p['w_in'])
    w_out = p['w_out']
    dattn = mm("out_bwd_dattn", [(dmix, w_out[:D_ATTN], 'nt')], D_ATTN)
    dyg = mm("out_bwd_dyg", [(dmix, w_out[D_ATTN:], 'nt')], D_SSD)
    dw_out = jnp.concatenate([mm_tn("out_bwd_dw_a", r['attn'], dmix), mm_tn("out_bwd_dw_y", r['yg'], dmix)], axis=0)

    def gate_bwd(dy_, y_, xs_, z_, ds_, w_):
        _, vjp = jax.vjp(_gate, y_, xs_, z_, ds_, w_)
        return vjp(dy_)

    dy, dxs_a, dz, ddskip, dssd_norm = rowwise(
        "ssd_gate_bwd", gate_bwd, [dyg, r['y'], r['xs'], Op(r['qkvz'], D_SSD, 3)], [r['dskip'], p['ssd_norm_w']],
        [(t, D_SSD, F32)] * 3, accs=[(1, D_SSD), (1, D_SSD)])
    dxs_b, dbm, dcm, ddtraw, ddt_bias, da_log = ssd_bwd(r['xs'], r['bm'], r['cm'], r['dtraw'], p['dt_bias'], p['a_log'],
                                                        r['hprev'], dy, b)
    dxs, = rowwise("ssd_dxs_sum", lambda a_, b_: a_ + b_, [dxs_a, dxs_b], [], [(t, D_SSD, F32)])
    dxbc, dconv_w, dconv_b = conv_bwd(r['xbc'], p['conv_w'], p['conv_b'], dxs, dbm, dcm)
    dmixed, dattn_norm = attn_norm_bwd(dattn, r['mixed'], p['attn_norm_w'])
    dq, dk, dv = attn_bwd(r['qkvz'], r['cosv'], r['sinv'], dmixed, r['mixed'], r['lses'], b)
    wq, wk, wv, wz = (w_a[:, i * D_ATTN:(i + 1) * D_ATTN] for i in range(4))
    dh = mm("in_bwd_dh", [(dq, wq, 'nt'), (dk, wk, 'nt'), (dv, wv, 'nt'), (dz, wz, 'nt'), (dxbc, w_b, 'nt'),
                          (ddtraw, w_c, 'nt')], D_MODEL, add=dh_resid, tn=512)
    h = r['h']
    dw_in = jnp.concatenate([mm_tn("in_bwd_dwq", h, dq), mm_tn("in_bwd_dwk", h, dk), mm_tn("in_bwd_dwv", h, dv),
                             mm_tn("in_bwd_dwz", h, dz), mm_tn("in_bwd_dwx", h, dxbc),
                             mm_tn("in_bwd_dwdt", h, ddtraw)[:, :N_HEADS]], axis=1)
    head_sum = lambda v: v.reshape(N_HEADS, HEAD_DIM).sum(axis=1).reshape(1, N_HEADS)
    grads = dict(w_in=dw_in, w_out=dw_out, conv_w=dconv_w, conv_b=dconv_b, dt_bias=ddt_bias[:, :N_HEADS],
                 a_log=da_log[:, :N_HEADS], d_skip=head_sum(ddskip), attn_norm_w=dattn_norm, ssd_norm_w=dssd_norm)
    return dh, grads


FFN_COL = ('ffn1_gate', 'ffn1_up', 'ffn2_gate', 'ffn2_up')
FFN_ROW = ('ffn1_down', 'ffn2_down')
CONV_W_COMM = (8, 2 * LANES)
SMALL = 'small'


def comm_shape(k, shapes):
    if k in FFN_COL:
        return (D_MODEL, FF_PAD)
    if k in FFN_ROW:
        return (FF_PAD, D_MODEL)
    if k == 'conv_w':
        return CONV_W_COMM
    if k == SMALL:
        n = sum(int(np.prod(shapes[r])) for r in REPLICATED)
        return (-(-n // (8 * LANES)) * 8, LANES)
    return tuple(shapes[k][1:])


def to_comm(k, vals, shapes):
    if k == SMALL:
        flat = jnp.concatenate([vals[r].reshape(-1) for r in REPLICATED])
        r_, c_ = comm_shape(k, shapes)
        return jnp.pad(flat, (0, r_ * c_ - flat.size)).reshape(r_, c_)
    a = vals[k].reshape(shapes[k][1:])
    r_, c_ = comm_shape(k, shapes)
    return jnp.pad(a, ((0, r_ - a.shape[0]), (0, c_ - a.shape[1])))


def from_comm(k, a, shapes):
    if k == SMALL:
        flat, out, off = a.reshape(-1), {}, 0
        for r in REPLICATED:
            n = int(np.prod(shapes[r]))
            out[r] = flat[off:off + n].reshape(shapes[r])
            off += n
        return out
    shp = shapes[k][1:]
    return {k: a[:shp[0], :shp[1]].reshape(shapes[k])}


def full_weight(k, g):
    if k in FFN_COL:
        return jnp.concatenate([g[p] for p in range(N_DEV)], axis=1)
    if k == 'conv_w':
        return jnp.transpose(g[:, :CONV_WIDTH, :D_CONV // N_DEV], (1, 0, 2)).reshape(CONV_WIDTH, D_CONV)
    return g.reshape(N_DEV * g.shape[1], g.shape[2])


def grad_shards(k, g):
    if k in FFN_COL:
        return jnp.stack([g[:, p * FF_PAD:(p + 1) * FF_PAD] for p in range(N_DEV)])
    if k == 'conv_w':
        s = jnp.transpose(g.reshape(CONV_WIDTH, N_DEV, D_CONV // N_DEV), (1, 0, 2))
        return jnp.pad(s, ((0, 0), (0, CONV_W_COMM[0] - CONV_WIDTH), (0, CONV_W_COMM[1] - D_CONV // N_DEV)))
    return g.reshape(N_DEV, g.shape[0] // N_DEV, g.shape[1])


def _flip(v, bit):
    return 1 - v if bit else v


N_PEER_COPIES = N_DEV - 1


def _comm_call(name, body, arrs, out_shape):
    n = len(arrs)
    return pl.pallas_call(
        functools.partial(body, n), name=name, out_shape=out_shape,
        in_specs=[pl.BlockSpec(memory_space=pl.ANY)] * n, out_specs=[pl.BlockSpec(memory_space=pl.ANY)] * n,
        scratch_shapes=[pltpu.SemaphoreType.DMA((n * N_PEER_COPIES,)), pltpu.SemaphoreType.DMA((n * N_PEER_COPIES,)),
                        pltpu.SemaphoreType.DMA((n,))],
    )(*arrs)


def all_gather(arrs):
    def body(n, *refs):
        x_refs, out_refs, (send_sems, recv_sems, local_sems) = refs[:n], refs[n:2 * n], refs[2 * n:]
        x, y, c = lax.axis_index("x"), lax.axis_index("y"), lax.axis_index("c")
        me, sibling = (x, y, c), (x, y, 1 - c)
        chips = [(1 - x, y), (x, 1 - y), (1 - x, 1 - y)]

        def copy(a, k, block, to, src=None):
            px, py, pc = block
            dst = out_refs[a].at[4 * px + 2 * py + pc]
            return pltpu.make_async_remote_copy(
                src_ref=dst if src is None else src, dst_ref=dst, send_sem=send_sems.at[a * N_PEER_COPIES + k],
                recv_sem=recv_sems.at[a * N_PEER_COPIES + k], device_id=to, device_id_type=MESH)

        mine = [pltpu.make_async_copy(x_refs[a], out_refs[a].at[4 * x + 2 * y + c], local_sems.at[a]) for a in range(n)]
        started = []
        for a in range(n):
            mine[a].start()
            first = [copy(a, 0, me, sibling, src=x_refs[a])]
            first += [copy(a, 1 + j, me, (*chip, c), src=x_refs[a]) for j, chip in enumerate(chips)]
            for cp in first:
                cp.start()
            started += first
        for j, chip in enumerate(chips):
            for a in range(n):
                copy(a, 1 + j, (*chip, c), me).wait_recv()
                cp = copy(a, 4 + j, (*chip, c), sibling)
                cp.start()
                started.append(cp)
        for a in range(n):
            copy(a, 0, sibling, me).wait_recv()
            for j, chip in enumerate(chips):
                copy(a, 4 + j, (*chip, 1 - c), me).wait_recv()
        for cp in started:
            cp.wait_send()
        for cp in mine:
            cp.wait()

    return _comm_call("all_gather_weights", body, arrs,
                      [jax.ShapeDtypeStruct((N_DEV,) + a.shape, a.dtype) for a in arrs])


def all_to_all(arrs):
    def body(n, *refs):
        s_refs, r_refs, (send_sems, recv_sems, local_sems) = refs[:n], refs[n:2 * n], refs[2 * n:]
        x, y, c = lax.axis_index("x"), lax.axis_index("y"), lax.axis_index("c")
        me = 4 * x + 2 * y + c

        def peer(k):
            return _flip(x, k & 4), _flip(y, k & 2), _flip(c, k & 1)

        def copy(a, k, landing):
            px, py, pc = peer(k)
            p = 4 * px + 2 * py + pc
            src, dst = (s_refs[a].at[me], r_refs[a].at[p]) if landing else (s_refs[a].at[p], r_refs[a].at[me])
            return pltpu.make_async_remote_copy(
                src_ref=src, dst_ref=dst, send_sem=send_sems.at[a * N_PEER_COPIES + k - 1],
                recv_sem=recv_sems.at[a * N_PEER_COPIES + k - 1], device_id=(px, py, pc), device_id_type=MESH)

        mine = [pltpu.make_async_copy(s_refs[a].at[me], r_refs[a].at[me], local_sems.at[a]) for a in range(n)]
        sends = [copy(a, k, False) for a in range(n) for k in range(1, N_DEV)]
        for cp in mine + sends:
            cp.start()
        for a in range(n):
            for k in range(1, N_DEV):
                copy(a, k, True).wait_recv()
        for cp in sends:
            cp.wait_send()
        for cp in mine:
            cp.wait()

    return _comm_call("all_to_all_grads", body, arrs, [jax.ShapeDtypeStruct(a.shape, a.dtype) for a in arrs])


def adamw(name, recv, w, m, v, tm):
    rows, cols = w.shape
    c1 = 1.0 / (1.0 - ADAM_B1 ** ADAM_STEP)
    c2 = 1.0 / (1.0 - ADAM_B2 ** ADAM_STEP)

    def fn(*a):
        g = a[0]
        for s in range(1, N_DEV):
            g = g + a[s]
        w_, m_, v_ = a[N_DEV:]
        m_ = ADAM_B1 * m_ + (1.0 - ADAM_B1) * g
        v_ = ADAM_B2 * v_ + (1.0 - ADAM_B2) * jnp.square(g)
        delta = -ADAM_LR * ((m_ * c1) / (jnp.sqrt(v_ * c2) + ADAM_EPS) + ADAM_WD * w_)
        return g, delta, m_, v_

    flat = recv.reshape(N_DEV * rows, cols)
    ins = [Op(flat, cols, 0, s * (rows // tm)) for s in range(N_DEV)] + [w, m, v]
    return rowwise(name, fn, ins, [], [(rows, cols, F32)] * 4, tm=tm)


ADAMW_TM = {'ffn1_gate': 256, 'ffn1_up': 256, 'ffn1_down': 128, 'w_in': 32, 'conv_w': 8, 'w_out': 64,
            'ffn2_gate': 256, 'ffn2_up': 256, 'ffn2_down': 128}


def kernel(x, positions, ln1_g, ln1_b, ffn1_gate, ffn1_up, ffn1_down, w_in, conv_w, conv_b, dt_bias, a_log, d_skip, attn_norm_w, ssd_norm_w, w_out, ln2_g, ln2_b, ffn2_gate, ffn2_up, ffn2_down, ln3_g, ln3_b, loss_target, m_ln1_g, m_ln1_b, m_ffn1_gate, m_ffn1_up, m_ffn1_down, m_w_in, m_conv_w, m_conv_b, m_dt_bias, m_a_log, m_d_skip, m_attn_norm_w, m_ssd_norm_w, m_w_out, m_ln2_g, m_ln2_b, m_ffn2_gate, m_ffn2_up, m_ffn2_down, m_ln3_g, m_ln3_b, v_ln1_g, v_ln1_b, v_ffn1_gate, v_ffn1_up, v_ffn1_down, v_w_in, v_conv_w, v_conv_b, v_dt_bias, v_a_log, v_d_skip, v_attn_norm_w, v_ssd_norm_w, v_w_out, v_ln2_g, v_ln2_b, v_ffn2_gate, v_ffn2_up, v_ffn2_down, v_ln3_g, v_ln3_b):
    args = dict(locals())
    wl = {k: args[k] for k in WEIGHTS}
    ml = {k: args["m_" + k] for k in WEIGHTS}
    vl = {k: args["v_" + k] for k in WEIGHTS}
    shapes = {k: wl[k].shape for k in WEIGHTS}
    b, s, dm = x.shape
    t = b * s

    w_comm = {k: to_comm(k, wl, shapes) for k in SHARDED + (SMALL,)}
    gathered = all_gather([w_comm[k] if k == 'conv_w' else w_comm[k].astype(BF16) for k in SHARDED])
    p = {k: full_weight(k, g) for k, g in zip(SHARDED, gathered)}
    for k in REPLICATED:
        p[k] = wl[k].reshape(1, -1)

    x2 = x.reshape(t, dm)
    cosv, sinv = rope_tables(positions)
    h1, res1 = ffn_fwd("ffn1", x2, p['ffn1_gate'], p['ffn1_up'], p['ffn1_down'], p['ln1_g'], p['ln1_b'])
    mix, resm = mixer_fwd(h1, p, cosv, sinv, b)
    h2, = rowwise("ln2", functools.partial(_resid_ln, 1.0), [h1, mix], [p['ln2_g'], p['ln2_b']], [(t, dm, F32)])
    h3, res3 = ffn_fwd("ffn2", h2, p['ffn2_gate'], p['ffn2_up'], p['ffn2_down'], p['ln3_g'], p['ln3_b'])

    def loss_fn(y, tgt):
        e = y - tgt
        return e * (1.0 / dm), jnp.sum(e * e, axis=0, keepdims=True)

    dh3, sq = rowwise("loss", loss_fn, [h3, loss_target.reshape(t, dm)], [], [(t, dm, F32)], accs=[(1, dm)])
    loss = lax.psum(jnp.sum(sq) * (0.5 / dm), AXES)

    small = {}
    dh2_res, df2, small['ln3_g'], small['ln3_b'] = resid_ln_bwd("ln3_bwd", 0.5, h2, res3[4], p['ln3_g'], p['ln3_b'], dh3)
    full = {}
    dh2, full['ffn2_gate'], full['ffn2_up'], full['ffn2_down'] = ffn_bwd("ffn2", res3, p['ffn2_gate'], p['ffn2_up'],
                                                                       p['ffn2_down'], df2, dh2_res)
    dh1_res, dmix, small['ln2_g'], small['ln2_b'] = resid_ln_bwd("ln2_bwd", 1.0, h1, mix, p['ln2_g'], p['ln2_b'], dh2)
    dh1, gm = mixer_bwd(resm, p, dmix, dh1_res, b)
    for k in ('w_in', 'w_out', 'conv_w'):
        full[k] = gm[k]
    for k in ('conv_b', 'dt_bias', 'a_log', 'd_skip', 'attn_norm_w', 'ssd_norm_w'):
        small[k] = gm[k]
    dx_res, df1, small['ln1_g'], small['ln1_b'] = resid_ln_bwd("ln1_bwd", 0.5, x2, res1[4], p['ln1_g'], p['ln1_b'], dh1)
    dx, full['ffn1_gate'], full['ffn1_up'], full['ffn1_down'] = ffn_bwd("ffn1", res1, p['ffn1_gate'], p['ffn1_up'],
                                                                      p['ffn1_down'], df1, dx_res)

    keys = SHARDED + (SMALL,)
    small_part = to_comm(SMALL, small, shapes)
    send = [grad_shards(k, full[k]) for k in SHARDED] + [jnp.broadcast_to(small_part[None], (N_DEV,) + small_part.shape)]
    recv = all_to_all(send)
    outs = [{}, {}, {}, {}]
    for k, r in zip(keys, recv):
        tm = ADAMW_TM.get(k, r.shape[1])
        res = adamw(f"adamw_{k}", r, w_comm[k], to_comm(k, ml, shapes), to_comm(k, vl, shapes), tm)
        for o, a in zip(outs, res):
            o.update(from_comm(k, a, shapes))
    return (loss, dx.reshape(b, s, dm), *[o[k] for o in outs for k in WEIGHTS])
```

```python
import functools
import math

import jax
import jax.numpy as jnp
import numpy as np
from jax import lax
from jax.experimental import pallas as pl
from jax.experimental.pallas import tpu as pltpu

F32, BF16 = jnp.float32, jnp.bfloat16
HI = lax.Precision.HIGHEST
MESH = pl.DeviceIdType.MESH
AXES = ("x", "y", "c")
N_DEV = 8

D_MODEL = 1024
SEQ = 2048
HEAD_DIM = 64
N_HEADS = 12
D_ATTN = N_HEADS * HEAD_DIM
DILATIONS = (1, 4, 16)
ATTN_BLOCK = 128
ROPE_THETA = 500000.0
ROPE_DIM = 16
D_SSD = 768
SSD_GROUPS = 4
SSD_STATE = 128
SSD_CHUNK = 128
D_BC = SSD_GROUPS * SSD_STATE
D_CONV = D_SSD + 2 * D_BC
CONV_WIDTH = 4
D_QKVZ = 3 * D_ATTN + D_SSD
D_IN_PROJ = D_QKVZ + D_CONV + N_HEADS
D_FF = 2816
ALPHA = 2.0 ** 0.25
LN_EPS = 1e-5
RMS_EPS = 1e-6
ADAM_LR, ADAM_B1, ADAM_B2, ADAM_EPS, ADAM_WD, ADAM_STEP = 0.001, 0.9, 0.999, 1e-08, 0.01, 10

LANES = 128
VMEM_LIMIT = 52 * 1024 * 1024
NEG = -1e30

WEIGHTS = ['ln1_g', 'ln1_b', 'ffn1_gate', 'ffn1_up', 'ffn1_down', 'w_in', 'conv_w', 'conv_b', 'dt_bias', 'a_log',
           'd_skip', 'attn_norm_w', 'ssd_norm_w', 'w_out', 'ln2_g', 'ln2_b', 'ffn2_gate', 'ffn2_up', 'ffn2_down',
           'ln3_g', 'ln3_b']
COL_SHARDED = ('ffn1_gate', 'ffn1_up', 'conv_w', 'ffn2_gate', 'ffn2_up')
ROW_SHARDED = ('ffn1_down', 'w_in', 'w_out', 'ffn2_down')
SHARDED = tuple(n for n in WEIGHTS if n in COL_SHARDED or n in ROW_SHARDED)
REPLICATED = tuple(n for n in WEIGHTS if n not in SHARDED)
FF_SHARD = D_FF // N_DEV
FF_PAD = -(-FF_SHARD // LANES) * LANES
D_FF_INT = N_DEV * FF_PAD


def _cparams(sem=None):
    return pltpu.CompilerParams(dimension_semantics=sem, vmem_limit_bytes=VMEM_LIMIT)


def _tile(n, prefs):
    for p in prefs:
        if n % p == 0:
            return p
    return n


class Op:
    def __init__(self, arr, bw=None, cb=0, ro=0):
        self.arr, self.bw, self.cb, self.ro = arr, (arr.shape[1] if bw is None else bw), cb, ro


def _op(a):
    return a if isinstance(a, Op) else Op(a)


def rowwise(name, fn, ins, consts, outs, accs=(), tm=256):
    ins = [_op(a) for a in ins]
    rows = outs[0][0]
    n_in, n_c, n_o, n_a = len(ins), len(consts), len(outs), len(accs)
    tm = min(tm, rows)
    assert rows % tm == 0, (name, rows, tm)

    def body(*refs):
        vals = [r[...].astype(F32) for r in refs[:n_in + n_c]]
        res = fn(*vals)
        res = res if isinstance(res, (tuple, list)) else (res,)
        o_refs = refs[n_in + n_c:n_in + n_c + n_o]
        a_refs = refs[n_in + n_c + n_o:]
        for r, v in zip(o_refs, res[:n_o]):
            r[...] = v.astype(r.dtype)
        if n_a:
            @pl.when(pl.program_id(0) == 0)
            def _():
                for r in a_refs:
                    r[...] = jnp.zeros(r.shape, r.dtype)
            for r, v in zip(a_refs, res[n_o:]):
                r[...] += v

    in_specs = [pl.BlockSpec((tm, o.bw), functools.partial(lambda i, o: (i + o.ro, o.cb), o=o)) for o in ins]
    in_specs += [pl.BlockSpec(c.shape, functools.partial(lambda i, nd: (0,) * nd, nd=c.ndim)) for c in consts]
    out_specs = [pl.BlockSpec((tm, w), lambda i: (i, 0)) for (_, w, _) in outs]
    out_specs += [pl.BlockSpec(s, functools.partial(lambda i, nd: (0,) * nd, nd=len(s))) for s in accs]
    out_shape = [jax.ShapeDtypeStruct((r, w), dt) for (r, w, dt) in outs]
    out_shape += [jax.ShapeDtypeStruct(s, F32) for s in accs]
    res = pl.pallas_call(
        body, name=name, grid=(rows // tm,), in_specs=in_specs, out_specs=out_specs, out_shape=out_shape,
        compiler_params=_cparams(("arbitrary",) if n_a else ("parallel",)),
    )(*[o.arr for o in ins], *consts)
    return res


MM_TM = 512
MM_TN = (1024, 896, 768, 512, 256, 128)
_NT = (((1,), (1,)), ((), ()))
_NN = (((1,), (0,)), ((), ()))
_TN = (((0,), (0,)), ((), ()))


def _dot(a, b, dn, precision=None):
    return lax.dot_general(a, b, dn, preferred_element_type=F32, precision=precision)


def _mm_specs(name, pairs, n_out, tm, tn):
    in_specs, args = [], []
    for a, b, mode in pairs:
        o = _op(a)
        in_specs.append(pl.BlockSpec((tm, o.bw), functools.partial(lambda j, i, o: (i, o.cb), o=o)))
        args.append(o.arr)
        if mode == 'nn':
            assert b.shape == (o.bw, n_out), (name, b.shape, o.bw, n_out)
            in_specs.append(pl.BlockSpec((o.bw, tn), lambda j, i: (0, j)))
        else:
            assert b.shape == (n_out, o.bw), (name, b.shape, o.bw, n_out)
            in_specs.append(pl.BlockSpec((tn, o.bw), lambda j, i: (j, 0)))
        args.append(b)
    return in_specs, args


def _mm_acc(refs, pairs):
    acc = None
    for k, (_, _, mode) in enumerate(pairs):
        d = _dot(refs[2 * k][...].astype(BF16), refs[2 * k + 1][...].astype(BF16), _NN if mode == 'nn' else _NT)
        acc = d if acc is None else acc + d
    return acc


def mm(name, pairs, n_out, add=None, out_dtype=F32, tm=MM_TM, tn=None):
    m = _op(pairs[0][0]).arr.shape[0]
    tn = tn or _tile(n_out, MM_TN)
    n_p = len(pairs)

    def body(*refs):
        acc = _mm_acc(refs, pairs)
        if add is not None:
            acc = acc + refs[2 * n_p][...]
        refs[-1][...] = acc.astype(refs[-1].dtype)

    in_specs, args = _mm_specs(name, pairs, n_out, tm, tn)
    tile = pl.BlockSpec((tm, tn), lambda j, i: (i, j))
    if add is not None:
        in_specs.append(tile)
        args.append(add)
    return pl.pallas_call(
        body, name=name, grid=(n_out // tn, m // tm), in_specs=in_specs, out_specs=tile,
        out_shape=jax.ShapeDtypeStruct((m, n_out), out_dtype),
        compiler_params=_cparams(("parallel", "parallel")),
    )(*args)


def mm_tn(name, a, b, out_dtype=F32, tt=1024):
    a, b = _op(a), _op(b)
    t = a.arr.shape[0]
    k, n = a.bw, b.bw
    tk = _tile(k, (512, 896, 768, 256, 128))
    tn = _tile(n, MM_TN)
    tt = min(tt, t)
    n_t = t // tt

    def body(a_ref, b_ref, o_ref, acc_ref):
        s = pl.program_id(2)
        d = _dot(a_ref[...].astype(BF16), b_ref[...].astype(BF16), _TN)

        @pl.when(s == 0)
        def _():
            acc_ref[...] = d

        @pl.when(s > 0)
        def _():
            acc_ref[...] += d

        @pl.when(s == n_t - 1)
        def _():
            o_ref[...] = acc_ref[...].astype(o_ref.dtype)

    return pl.pallas_call(
        body, name=name, grid=(k // tk, n // tn, n_t),
        in_specs=[pl.BlockSpec((tt, tk), functools.partial(lambda kk, nn, s, o: (s, o.cb * (o.bw // tk) + kk), o=a)),
                  pl.BlockSpec((tt, tn), functools.partial(lambda kk, nn, s, o: (s, o.cb * (o.bw // tn) + nn), o=b))],
        out_specs=pl.BlockSpec((tk, tn), lambda kk, nn, s: (kk, nn)),
        out_shape=jax.ShapeDtypeStruct((k, n), out_dtype),
        scratch_shapes=[pltpu.VMEM((tk, tn), F32)],
        compiler_params=_cparams(("parallel", "parallel", "arbitrary")),
    )(a.arr, b.arr)


def _sigmoid(x):
    return 1.0 / (1.0 + jnp.exp(-x))


def _silu(x):
    return x * _sigmoid(x)


def _softplus(x):
    return jnp.maximum(x, 0.0) + jnp.log(1.0 + jnp.exp(-jnp.abs(x)))


def _act(g, u):
    return _silu(g) * u


def _resid_ln(scale, h, branch, g, b):
    r = ALPHA * h + scale * branch
    mu = jnp.mean(r, axis=-1, keepdims=True)
    var = jnp.mean(jnp.square(r - mu), axis=-1, keepdims=True)
    return (r - mu) * lax.rsqrt(var + LN_EPS) * g + b


def _rms(t, w):
    return t * lax.rsqrt(jnp.mean(t * t, axis=-1, keepdims=True) + RMS_EPS) * w


def _branch_weights(l1, l2, l3):
    m = jnp.maximum(jnp.maximum(l1, l2), l3)
    e1, e2, e3 = jnp.exp(l1 - m), jnp.exp(l2 - m), jnp.exp(l3 - m)
    inv = 1.0 / (e1 + e2 + e3)
    return e1 * inv, e2 * inv, e3 * inv


def _gate(y, xs, z, dskip, w):
    return _rms((y + dskip * xs) * _silu(z), w)


def _rot(x):
    d = lax.broadcasted_iota(jnp.int32, x.shape, 1) % HEAD_DIM
    up = pltpu.roll(x, x.shape[1] - ROPE_DIM // 2, 1)
    down = jnp.where(d < ROPE_DIM, pltpu.roll(x, ROPE_DIM // 2, 1), 0.0)
    return jnp.where(d < ROPE_DIM // 2, up, down)


def ffn_gate_up(name, h, wg, wu):
    m, nf = h.shape[0], wg.shape[1]
    tn = _tile(nf, MM_TN)

    def body(h_ref, g_w, u_w, g_ref, u_ref, a_ref):
        hb = h_ref[...].astype(BF16)
        g = _dot(hb, g_w[...].astype(BF16), _NN)
        u = _dot(hb, u_w[...].astype(BF16), _NN)
        g_ref[...] = g.astype(g_ref.dtype)
        u_ref[...] = u.astype(u_ref.dtype)
        a_ref[...] = _act(g, u).astype(a_ref.dtype)

    in_specs, args = _mm_specs(name, [(h, wg, 'nn')], nf, MM_TM, tn)
    in_specs.append(in_specs[1])
    tile = pl.BlockSpec((MM_TM, tn), lambda j, i: (i, j))
    return pl.pallas_call(
        body, name=name, grid=(nf // tn, m // MM_TM), in_specs=in_specs, out_specs=[tile] * 3,
        out_shape=[jax.ShapeDtypeStruct((m, nf), BF16)] * 3, compiler_params=_cparams(("parallel", "parallel")),
    )(*args, wu)


def ffn_da_act(name, df, wd, g, u):
    m, nf = df.shape[0], wd.shape[0]
    tn = _tile(nf, MM_TN)

    def body(df_ref, w_ref, g_ref, u_ref, dg_ref, du_ref):
        da = _dot(df_ref[...].astype(BF16), w_ref[...].astype(BF16), _NT)
        _, vjp = jax.vjp(_act, g_ref[...].astype(F32), u_ref[...].astype(F32))
        dg, du = vjp(da)
        dg_ref[...] = dg.astype(dg_ref.dtype)
        du_ref[...] = du.astype(du_ref.dtype)

    in_specs, args = _mm_specs(name, [(df, wd, 'nt')], nf, MM_TM, tn)
    tile = pl.BlockSpec((MM_TM, tn), lambda j, i: (i, j))
    return pl.pallas_call(
        body, name=name, grid=(nf // tn, m // MM_TM), in_specs=in_specs + [tile, tile], out_specs=[tile] * 2,
        out_shape=[jax.ShapeDtypeStruct((m, nf), BF16)] * 2, compiler_params=_cparams(("parallel", "parallel")),
    )(*args, g, u)


def resid_ln_fwd(name, scale, h, branch, ln_g, ln_b):
    t = h.shape[0]

    def fn(*a):
        y = _resid_ln(scale, *a)
        return y, y

    return rowwise(name, fn, [h, branch], [ln_g, ln_b], [(t, D_MODEL, F32), (t, D_MODEL, BF16)], tm=512)


def ffn_fwd(tag, h, hb, wg, wu, wd, ln_g, ln_b):
    g, u, a = ffn_gate_up(f"{tag}_gate_up", hb, wg, wu)
    f = mm(f"{tag}_down", [(a, wd, 'nn')], D_MODEL)
    out, outb = resid_ln_fwd(f"{tag}_ln", 0.5, h, f, ln_g, ln_b)
    return out, outb, (h, hb, g, u, a, f)


def resid_ln_bwd(name, scale, h, branch, ln_g, ln_b, dout, extra=None):
    t = h.shape[0]

    def fn(h_, br_, do_, *rest):
        g_, b_ = rest[-2], rest[-1]
        _, vjp = jax.vjp(functools.partial(_resid_ln, scale), h_, br_, g_, b_)
        dh, dbr, dg, db = vjp(do_)
        if extra is not None:
            dh = dh + rest[0]
        return dh, dbr, dg, db

    ins = [h, branch, dout] + ([extra] if extra is not None else [])
    return rowwise(name, fn, ins, [ln_g, ln_b], [(t, D_MODEL, F32), (t, D_MODEL, F32)],
                   accs=[(1, D_MODEL), (1, D_MODEL)], tm=512)


def ffn_bwd(tag, res, wg, wu, wd, df, dh_resid):
    _, hb, g, u, a, _ = res
    dg, du = ffn_da_act(f"{tag}_bwd_da_act", df, wd, g, u)
    dwd = mm_tn(f"{tag}_bwd_dwd", a, df, BF16)
    dh = mm(f"{tag}_bwd_dh", [(dg, wg, 'nt'), (du, wu, 'nt')], D_MODEL, add=dh_resid, tn=512)
    dwg = mm_tn(f"{tag}_bwd_dwg", hb, dg, BF16)
    dwu = mm_tn(f"{tag}_bwd_dwu", hb, du, BF16)
    return dh, dwg, dwu, dwd


def rope_tables(positions):
    inv_freq = ROPE_THETA ** (-jnp.arange(0, ROPE_DIM, 2, dtype=F32) / ROPE_DIM)
    ang = positions.reshape(-1, 1).astype(F32) * inv_freq
    c, s = jnp.cos(ang), jnp.sin(ang)
    t = ang.shape[0]
    cosv = jnp.concatenate([c, c, jnp.ones((t, HEAD_DIM - ROPE_DIM), F32)], axis=1)
    sinv = jnp.concatenate([-s, s, jnp.zeros((t, HEAD_DIM - ROPE_DIM), F32)], axis=1)
    return jnp.tile(cosv, (1, 2)), jnp.tile(sinv, (1, 2))


def _pair_masks():
    lane = lax.broadcasted_iota(jnp.int32, (1, LANES), 1)
    return (lane < HEAD_DIM, lane >= HEAD_DIM)


def _band_masks():
    row = lax.broadcasted_iota(jnp.int32, (ATTN_BLOCK, ATTN_BLOCK), 0)
    col = lax.broadcasted_iota(jnp.int32, (ATTN_BLOCK, ATTN_BLOCK), 1)
    return col >= row, col <= row


def _residue_blocks():
    out = []
    for g, d in enumerate(DILATIONS):
        for r in range(d):
            for i in range(SEQ // d // ATTN_BLOCK):
                rows = lambda j: pl.ds(r + j * ATTN_BLOCK * d, ATTN_BLOCK, stride=d) if d > 1 else pl.ds(j * ATTN_BLOCK, ATTN_BLOCK)
                out.append((g, rows(i), rows(i - 1) if i > 0 else None))
    return out


N_HEAD_PAIRS = D_ATTN // LANES
SCALE = HEAD_DIM ** -0.5


def _attn_specs(b):
    col = lambda cb: pl.BlockSpec((SEQ, LANES), lambda bb, hp: (bb, cb + hp))
    tab = pl.BlockSpec((SEQ, LANES), lambda bb, hp: (bb, 0))
    return col, tab


def attn_fwd(qkvz, cosv, sinv, b):
    t = qkvz.shape[0]
    col, tab = _attn_specs(b)
    blocks = _residue_blocks()

    def body(q_ref, k_ref, v_ref, c_ref, s_ref, o_ref, l1_ref, l2_ref, l3_ref, qr, kr, o1, o2, o3):
        l_refs, o_scr = (l1_ref, l2_ref, l3_ref), (o1, o2, o3)
        c, s = c_ref[...], s_ref[...]
        q, k = q_ref[...], k_ref[...]
        qr[...] = q * c + _rot(q) * s
        kr[...] = k * c + _rot(k) * s
        prev_ok, cur_ok = _band_masks()
        masks = _pair_masks()
        for g, cur, prev in blocks:
            q2, kc2, vc2 = qr[cur, :], kr[cur, :].astype(BF16), v_ref[cur, :]
            if prev is not None:
                kp2, vp2 = kr[prev, :].astype(BF16), v_ref[prev, :]
            acc = jnp.zeros((ATTN_BLOCK, LANES), F32)
            lse = jnp.zeros((ATTN_BLOCK, LANES), F32)
            for m in masks:
                qm = jnp.where(m, q2, 0.0).astype(BF16)
                sc = jnp.where(cur_ok, _dot(qm, kc2, _NT) * SCALE, NEG)
                mx = jnp.max(sc, axis=1, keepdims=True)
                if prev is not None:
                    sp = jnp.where(prev_ok, _dot(qm, kp2, _NT) * SCALE, NEG)
                    mx = jnp.maximum(mx, jnp.max(sp, axis=1, keepdims=True))
                pc = jnp.exp(sc - mx)
                den = jnp.sum(pc, axis=1, keepdims=True)
                oh = _dot(pc.astype(BF16), jnp.where(m, vc2, 0.0).astype(BF16), _NN)
                if prev is not None:
                    pp = jnp.exp(sp - mx)
                    den = den + jnp.sum(pp, axis=1, keepdims=True)
                    oh = oh + _dot(pp.astype(BF16), jnp.where(m, vp2, 0.0).astype(BF16), _NN)
                acc = acc + oh * (1.0 / den)
                lse = jnp.where(m, mx + jnp.log(den), lse)
            o_scr[g][cur, :] = acc
            l_refs[g][cur, :] = lse
        w1, w2, w3 = _branch_weights(l1_ref[...], l2_ref[...], l3_ref[...])
        o_ref[...] = w1 * o1[...] + w2 * o2[...] + w3 * o3[...]

    shp = jax.ShapeDtypeStruct((t, D_ATTN), F32)
    return pl.pallas_call(
        body, name="attn_fwd", grid=(b, N_HEAD_PAIRS),
        in_specs=[col(0), col(N_HEAD_PAIRS), col(2 * N_HEAD_PAIRS), tab, tab],
        out_specs=[col(0)] * 4, out_shape=[shp] * 4,
        scratch_shapes=[pltpu.VMEM((SEQ, LANES), F32)] * 5,
        compiler_params=_cparams(("parallel", "parallel")),
    )(qkvz, qkvz, qkvz, cosv, sinv)


def attn_bwd(qkvz, cosv, sinv, dmix, mixed, lses, b):
    t = qkvz.shape[0]
    col, tab = _attn_specs(b)
    blocks = _residue_blocks()
    hd = np.arange(LANES) // HEAD_DIM
    head_ones = jnp.asarray((hd[:, None] == hd[None, :]).astype(np.float32))

    def body(q_ref, k_ref, v_ref, c_ref, s_ref, dm_ref, mx_ref, l1_ref, l2_ref, l3_ref, ones_ref,
             dq_out, dk_out, dv_out, qr, kr, do1, do2, do3, dd1, dd2, dd3, dq_ref, dk_ref, dv_ref):
        l_refs, do_scr, dd_scr = (l1_ref, l2_ref, l3_ref), (do1, do2, do3), (dd1, dd2, dd3)
        c, s = c_ref[...], s_ref[...]
        q, k = q_ref[...], k_ref[...]
        qr[...] = q * c + _rot(q) * s
        kr[...] = k * c + _rot(k) * s
        dm = dm_ref[...]
        tot = _dot(dm * mx_ref[...], ones_ref[...], _NN, HI)
        for w, do_g, dd_g in zip(_branch_weights(l1_ref[...], l2_ref[...], l3_ref[...]), do_scr, dd_scr):
            do_g[...] = w * dm
            dd_g[...] = w * tot
        dq_ref[...] = jnp.zeros((SEQ, LANES), F32)
        dk_ref[...] = jnp.zeros((SEQ, LANES), F32)
        dv_ref[...] = jnp.zeros((SEQ, LANES), F32)
        prev_ok, cur_ok = _band_masks()
        masks = _pair_masks()
        zero = jnp.zeros((ATTN_BLOCK, LANES), F32)
        for g, cur, prev in blocks:
            q2, do2_, l2, dd2_ = qr[cur, :], do_scr[g][cur, :], l_refs[g][cur, :], dd_scr[g][cur, :]
            kc2, vc2 = kr[cur, :].astype(BF16), v_ref[cur, :].astype(BF16)
            if prev is not None:
                kp2, vp2 = kr[prev, :].astype(BF16), v_ref[prev, :].astype(BF16)
            l2s, dd2s = pltpu.roll(l2, HEAD_DIM, 1), pltpu.roll(dd2_, HEAD_DIM, 1)
            dq2, dkc, dvc, dkp, dvp = zero, zero, zero, zero, zero
            for m in masks:
                qm = jnp.where(m, q2, 0.0).astype(BF16)
                dom = jnp.where(m, do2_, 0.0).astype(BF16)
                lrep, ddrep = jnp.where(m, l2, l2s), jnp.where(m, dd2_, dd2s)
                pc = jnp.exp(jnp.where(cur_ok, _dot(qm, kc2, _NT) * SCALE, NEG) - lrep)
                dsc = (pc * (_dot(dom, vc2, _NT) - ddrep) * SCALE).astype(BF16)
                dq2 = dq2 + _dot(dsc, jnp.where(m, kc2, 0), _NN)
                dkc = dkc + _dot(dsc, qm, _TN)
                dvc = dvc + _dot(pc.astype(BF16), dom, _TN)
                if prev is not None:
                    pp = jnp.exp(jnp.where(prev_ok, _dot(qm, kp2, _NT) * SCALE, NEG) - lrep)
                    dsp = (pp * (_dot(dom, vp2, _NT) - ddrep) * SCALE).astype(BF16)
                    dq2 = dq2 + _dot(dsp, jnp.where(m, kp2, 0), _NN)
                    dkp = dkp + _dot(dsp, qm, _TN)
                    dvp = dvp + _dot(pp.astype(BF16), dom, _TN)
            dq_ref[cur, :] += dq2
            dk_ref[cur, :] += dkc
            dv_ref[cur, :] += dvc
            if prev is not None:
                dk_ref[prev, :] += dkp
                dv_ref[prev, :] += dvp
        dq, dk = dq_ref[...], dk_ref[...]
        dq_out[...] = (dq * c + _rot(dq * s)).astype(dq_out.dtype)
        dk_out[...] = (dk * c + _rot(dk * s)).astype(dk_out.dtype)
        dv_out[...] = dv_ref[...].astype(dv_out.dtype)

    shp = jax.ShapeDtypeStruct((t, D_ATTN), BF16)
    return pl.pallas_call(
        body, name="attn_bwd", grid=(b, N_HEAD_PAIRS),
        in_specs=[col(0), col(N_HEAD_PAIRS), col(2 * N_HEAD_PAIRS), tab, tab, col(0), col(0), col(0), col(0), col(0),
                  pl.BlockSpec((LANES, LANES), lambda bb, hp: (0, 0))],
        out_specs=[col(0)] * 3, out_shape=[shp] * 3,
        scratch_shapes=[pltpu.VMEM((SEQ, LANES), F32)] * 11,
        compiler_params=_cparams(("parallel", "parallel")),
    )(qkvz, qkvz, qkvz, cosv, sinv, dmix, mixed, *lses, head_ones)


def attn_norm_fwd(mixed, norm_w):
    return rowwise("attn_norm", _rms, [mixed], [norm_w], [(mixed.shape[0], D_ATTN, BF16)])[0]


def attn_norm_bwd(dout, mixed, norm_w):
    def fn(dy, mx, w):
        _, vjp = jax.vjp(_rms, mx, w)
        return vjp(dy)

    return rowwise("attn_norm_bwd", fn, [dout, mixed], [norm_w], [(dout.shape[0], D_ATTN, F32)], accs=[(1, D_ATTN)])


CONV_TM = 256
HALO = 8


def conv_fwd(u, w, bias):
    t = u.shape[0]
    tm, per_seq = CONV_TM, SEQ // CONV_TM

    def body(u_ref, h_ref, w_ref, b_ref, xs_ref, bm_ref, cm_ref, scr):
        first = pl.program_id(0) % per_seq == 0
        scr[0:HALO, :] = jnp.where(first, 0.0, h_ref[...])
        scr[HALO:, :] = u_ref[...]
        acc = b_ref[...]
        for k in range(CONV_WIDTH):
            acc = acc + w_ref[k:k + 1, :] * scr[pl.ds(HALO - CONV_WIDTH + 1 + k, tm), :]
        y = _silu(acc)
        xs_ref[...] = y[:, :D_SSD]
        bm_ref[...] = y[:, D_SSD:D_SSD + D_BC]
        cm_ref[...] = y[:, D_SSD + D_BC:]

    return pl.pallas_call(
        body, name="conv_fwd", grid=(t // tm,),
        in_specs=[pl.BlockSpec((tm, D_CONV), lambda i: (i, 0)),
                  pl.BlockSpec((HALO, D_CONV), lambda i: (jnp.maximum(i * (tm // HALO) - 1, 0), 0)),
                  pl.BlockSpec((CONV_WIDTH, D_CONV), lambda i: (0, 0)), pl.BlockSpec((1, D_CONV), lambda i: (0, 0))],
        out_specs=[pl.BlockSpec((tm, D_SSD), lambda i: (i, 0)), pl.BlockSpec((tm, D_BC), lambda i: (i, 0)),
                   pl.BlockSpec((tm, D_BC), lambda i: (i, 0))],
        out_shape=[jax.ShapeDtypeStruct((t, D_SSD), F32), jax.ShapeDtypeStruct((t, D_BC), F32),
                   jax.ShapeDtypeStruct((t, D_BC), F32)],
        scratch_shapes=[pltpu.VMEM((tm + HALO, D_CONV), F32)],
        compiler_params=_cparams(("parallel",)),
    )(u, u, w, bias)


def conv_bwd(u, w, bias, dxs, dbm, dcm):
    t = u.shape[0]
    tm, per_seq = CONV_TM, SEQ // CONV_TM
    n_tiles = t // tm

    def body1(u_ref, h_ref, dxs_ref, dbm_ref, dcm_ref, w_ref, b_ref, dz_ref, dw_ref, db_ref, scr):
        i = pl.program_id(0)
        first = i % per_seq == 0
        scr[0:HALO, :] = jnp.where(first, 0.0, h_ref[...])
        scr[HALO:, :] = u_ref[...]
        acc = b_ref[...]
        for k in range(CONV_WIDTH):
            acc = acc + w_ref[k:k + 1, :] * scr[pl.ds(HALO - CONV_WIDTH + 1 + k, tm), :]
        sig = _sigmoid(acc)
        dy = jnp.concatenate([dxs_ref[...], dbm_ref[...], dcm_ref[...]], axis=1)
        dz = dy * sig * (1.0 + acc * (1.0 - sig))
        dz_ref[...] = dz

        @pl.when(i == 0)
        def _():
            dw_ref[...] = jnp.zeros(dw_ref.shape, F32)
            db_ref[...] = jnp.zeros(db_ref.shape, F32)
        db_ref[...] += jnp.sum(dz, axis=0, keepdims=True)
        for k in range(CONV_WIDTH):
            dw_ref[k:k + 1, :] += jnp.sum(dz * scr[pl.ds(HALO - CONV_WIDTH + 1 + k, tm), :], axis=0, keepdims=True)

    dz, dw, db = pl.pallas_call(
        body1, name="conv_bwd_dz", grid=(n_tiles,),
        in_specs=[pl.BlockSpec((tm, D_CONV), lambda i: (i, 0)),
                  pl.BlockSpec((HALO, D_CONV), lambda i: (jnp.maximum(i * (tm // HALO) - 1, 0), 0)),
                  pl.BlockSpec((tm, D_SSD), lambda i: (i, 0)), pl.BlockSpec((tm, D_BC), lambda i: (i, 0)),
                  pl.BlockSpec((tm, D_BC), lambda i: (i, 0)),
                  pl.BlockSpec((CONV_WIDTH, D_CONV), lambda i: (0, 0)), pl.BlockSpec((1, D_CONV), lambda i: (0, 0))],
        out_specs=[pl.BlockSpec((tm, D_CONV), lambda i: (i, 0)), pl.BlockSpec((CONV_WIDTH, D_CONV), lambda i: (0, 0)),
                   pl.BlockSpec((1, D_CONV), lambda i: (0, 0))],
        out_shape=[jax.ShapeDtypeStruct((t, D_CONV), F32), jax.ShapeDtypeStruct((CONV_WIDTH, D_CONV), F32),
                   jax.ShapeDtypeStruct((1, D_CONV), F32)],
        scratch_shapes=[pltpu.VMEM((tm + HALO, D_CONV), F32)],
        compiler_params=_cparams(("arbitrary",)),
    )(u, u, dxs, dbm, dcm, w, bias)

    def body2(dz_ref, n_ref, w_ref, du_ref, scr):
        last = pl.program_id(0) % per_seq == per_seq - 1
        scr[0:tm, :] = dz_ref[...]
        scr[tm:, :] = jnp.where(last, 0.0, n_ref[...])
        acc = jnp.zeros((tm, D_CONV), F32)
        for k in range(CONV_WIDTH):
            acc = acc + w_ref[k:k + 1, :] * scr[pl.ds(CONV_WIDTH - 1 - k, tm), :]
        du_ref[...] = acc.astype(du_ref.dtype)

    du = pl.pallas_call(
        body2, name="conv_bwd_du", grid=(n_tiles,),
        in_specs=[pl.BlockSpec((tm, D_CONV), lambda i: (i, 0)),
                  pl.BlockSpec((HALO, D_CONV), lambda i: (jnp.minimum((i + 1) * (tm // HALO), t // HALO - 1), 0)),
                  pl.BlockSpec((CONV_WIDTH, D_CONV), lambda i: (0, 0))],
        out_specs=pl.BlockSpec((tm, D_CONV), lambda i: (i, 0)),
        out_shape=jax.ShapeDtypeStruct((t, D_CONV), BF16),
        scratch_shapes=[pltpu.VMEM((tm + HALO, D_CONV), F32)],
        compiler_params=_cparams(("parallel",)),
    )(dz, dz, w)
    return du, dw, db


Q = SSD_CHUNK
N_PAIRS = D_SSD // LANES
HEADS_PER_GROUP = N_HEADS // SSD_GROUPS


def _head_expand():
    e = np.zeros((LANES, N_HEADS * LANES), np.float32)
    for j in range(N_HEADS):
        e[j, j * LANES:(j + 1) * LANES] = 1.0
    return jnp.asarray(e)


def _pad_lanes(v, fill=0.0):
    row = jnp.pad(v.reshape(1, -1).astype(F32), ((0, 0), (0, LANES - v.size)), constant_values=fill)
    return row, row.reshape(LANES, 1)


def _ssd_common(dtr_ref, dtrt_ref, bias_r, bias_c, alog_r, alog_c, e_ref):
    row = lax.broadcasted_iota(jnp.int32, (Q, Q), 0)
    col = lax.broadcasted_iota(jnp.int32, (Q, Q), 1)
    tril = row >= col
    lane = lax.broadcasted_iota(jnp.int32, (1, LANES), 1)
    a_r = jnp.where(lane < N_HEADS, -jnp.exp(alog_r[...]), 0.0)
    sub = lax.broadcasted_iota(jnp.int32, (LANES, 1), 0)
    a_c = jnp.where(sub < N_HEADS, -jnp.exp(alog_c[...]), 0.0)
    dt = _softplus(dtr_ref[...] + bias_r[...])
    cs = _dot(tril.astype(F32), dt * a_r, _NN, HI)
    dt_rep = _dot(dt, e_ref[...], _NN, HI)
    cs_rep = _dot(cs, e_ref[...], _NN, HI)
    dtt = _softplus(dtrt_ref[...] + bias_c[...])
    cst = _dot(dtt * a_c, (row <= col).astype(F32), _NN, HI)
    return tril, lane, a_r, dt, dt_rep, cs_rep, cst


def _ssd_specs(b, nc, rev):
    ci = (lambda c: nc - 1 - c) if rev else (lambda c: c)
    rows = lambda w: pl.BlockSpec((Q, w), lambda bb, c: (bb * nc + ci(c), 0))
    dtt = pl.BlockSpec((LANES, Q), lambda bb, c: (0, bb * nc + ci(c)))
    const = lambda s: pl.BlockSpec(s, lambda bb, c: (0,) * len(s))
    state = pl.BlockSpec((None, N_PAIRS, LANES, SSD_STATE), lambda bb, c: (bb * nc + ci(c), 0, 0, 0))
    return rows, dtt, const, state


def ssd_fwd(xs, bm, cm, dtraw, dt_bias, a_log, b):
    t = xs.shape[0]
    nc = SEQ // Q
    rows, dtt_spec, const, state = _ssd_specs(b, nc, False)
    bias_r, bias_c = _pad_lanes(dt_bias)
    alog_r, alog_c = _pad_lanes(a_log)

    def body(xs_ref, b_ref, c_ref, dtr_ref, dtrt_ref, br, bc, ar, ac, e_ref, y_ref, hp_ref, h_scr):
        @pl.when(pl.program_id(1) == 0)
        def _():
            h_scr[...] = jnp.zeros(h_scr.shape, F32)
        tril, lane, _, _, dt_rep, cs_rep, cst = _ssd_common(dtr_ref, dtrt_ref, br, bc, ar, ac, e_ref)
        sub = lax.broadcasted_iota(jnp.int32, (LANES, 1), 0)
        y_acc = [jnp.zeros((Q, LANES), F32) for _ in range(N_PAIRS)]
        h_old = [h_scr[p] for p in range(N_PAIRS)]
        h_new = [jnp.zeros((LANES, SSD_STATE), F32) for _ in range(N_PAIRS)]
        for g in range(SSD_GROUPS):
            bg = b_ref[:, g * SSD_STATE:(g + 1) * SSD_STATE].astype(BF16)
            cg = c_ref[:, g * SSD_STATE:(g + 1) * SSD_STATE].astype(BF16)
            cb = _dot(cg, bg, _NT)
            for j in range(g * HEADS_PER_GROUP, (g + 1) * HEADS_PER_GROUP):
                p, side = j // 2, j % 2
                m = (lane < HEAD_DIM) if side == 0 else (lane >= HEAD_DIM)
                ms = (sub < HEAD_DIM) if side == 0 else (sub >= HEAD_DIM)
                csj = cs_rep[:, j * LANES:(j + 1) * LANES]
                dtj = dt_rep[:, j * LANES:(j + 1) * LANES]
                lmat = jnp.exp(jnp.where(tril, csj - cst[j:j + 1, :], NEG))
                xdt = jnp.where(m, xs_ref[:, p * LANES:(p + 1) * LANES] * dtj, 0.0)
                hm = jnp.where(ms, h_old[p], 0.0)
                ydiag = _dot((cb * lmat).astype(BF16), xdt.astype(BF16), _NN)
                yoff = jnp.exp(csj) * _dot(cg, hm.astype(BF16), _NT)
                y_acc[p] = y_acc[p] + ydiag + yoff
                last = csj[Q - 1:Q, :]
                sj = _dot((xdt * jnp.exp(last - csj)).astype(BF16), bg, _TN)
                h_new[p] = h_new[p] + jnp.exp(last) * hm + sj
        for p in range(N_PAIRS):
            y_ref[:, p * LANES:(p + 1) * LANES] = y_acc[p]
            hp_ref[p] = h_old[p]
            h_scr[p] = h_new[p]

    return pl.pallas_call(
        body, name="ssd_fwd", grid=(b, nc),
        in_specs=[rows(D_SSD), rows(D_BC), rows(D_BC), rows(LANES), dtt_spec, const((1, LANES)), const((LANES, 1)),
                  const((1, LANES)), const((LANES, 1)), const((LANES, N_HEADS * LANES))],
        out_specs=[rows(D_SSD), state],
        out_shape=[jax.ShapeDtypeStruct((t, D_SSD), F32),
                   jax.ShapeDtypeStruct((b * nc, N_PAIRS, LANES, SSD_STATE), F32)],
        scratch_shapes=[pltpu.VMEM((N_PAIRS, LANES, SSD_STATE), F32)],
        compiler_params=_cparams(("parallel", "arbitrary")),
    )(xs, bm, cm, dtraw, dtraw.T, bias_r, bias_c, alog_r, alog_c, _head_expand())


def ssd_bwd(xs, bm, cm, dtraw, dt_bias, a_log, hprev, dy, b):
    t = xs.shape[0]
    nc = SEQ // Q
    rows, dtt_spec, const, state = _ssd_specs(b, nc, True)
    bias_r, bias_c = _pad_lanes(dt_bias)
    alog_r, alog_c = _pad_lanes(a_log)

    def body(xs_ref, b_ref, c_ref, dtr_ref, dtrt_ref, hp_ref, dy_ref, br, bc, ar, ac, e_ref,
             dxs_ref, db_ref, dc_ref, ddt_ref, dbias_ref, dalog_ref, dh_scr):
        first = jnp.logical_and(pl.program_id(0) == 0, pl.program_id(1) == 0)

        @pl.when(pl.program_id(1) == 0)
        def _():
            dh_scr[...] = jnp.zeros(dh_scr.shape, F32)

        @pl.when(first)
        def _():
            dbias_ref[...] = jnp.zeros(dbias_ref.shape, F32)
            dalog_ref[...] = jnp.zeros(dalog_ref.shape, F32)
        tril, lane, a_r, dt, dt_rep, cs_rep, cst = _ssd_common(dtr_ref, dtrt_ref, br, bc, ar, ac, e_ref)
        sub = lax.broadcasted_iota(jnp.int32, (LANES, 1), 0)
        rowq = lax.broadcasted_iota(jnp.int32, (Q, 1), 0)
        ones = jnp.ones((Q, LANES), F32)
        triu = (lax.broadcasted_iota(jnp.int32, (Q, Q), 0) <= lax.broadcasted_iota(jnp.int32, (Q, Q), 1)).astype(F32)
        dxs_acc = [jnp.zeros((Q, LANES), F32) for _ in range(N_PAIRS)]
        dh_in = [dh_scr[p] for p in range(N_PAIRS)]
        h_in = [hp_ref[p] for p in range(N_PAIRS)]
        dh_out = [jnp.zeros((LANES, SSD_STATE), F32) for _ in range(N_PAIRS)]
        ddt = jnp.zeros((Q, LANES), F32)
        dalog = jnp.zeros((1, LANES), F32)
        for g in range(SSD_GROUPS):
            gs = slice(g * SSD_STATE, (g + 1) * SSD_STATE)
            bg, cg = b_ref[:, gs].astype(BF16), c_ref[:, gs].astype(BF16)
            cb = _dot(cg, bg, _NT)
            dcb = jnp.zeros((Q, Q), F32)
            dbg = jnp.zeros((Q, SSD_STATE), F32)
            dcg = jnp.zeros((Q, SSD_STATE), F32)
            for j in range(g * HEADS_PER_GROUP, (g + 1) * HEADS_PER_GROUP):
                p, side = j // 2, j % 2
                m = (lane < HEAD_DIM) if side == 0 else (lane >= HEAD_DIM)
                ms = (sub < HEAD_DIM) if side == 0 else (sub >= HEAD_DIM)
                csj = cs_rep[:, j * LANES:(j + 1) * LANES]
                dtj = dt_rep[:, j * LANES:(j + 1) * LANES]
                lmat = jnp.exp(jnp.where(tril, csj - cst[j:j + 1, :], NEG))
                x2 = jnp.where(m, xs_ref[:, p * LANES:(p + 1) * LANES], 0.0)
                xdt = x2 * dtj
                dym = jnp.where(m, dy_ref[:, p * LANES:(p + 1) * LANES], 0.0)
                hm = jnp.where(ms, h_in[p], 0.0)
                dhm = jnp.where(ms, dh_in[p], 0.0)
                ecs = jnp.exp(csj)
                last = csj[Q - 1:Q, :]
                decay = jnp.exp(last - csj)
                el = jnp.exp(last)
                gmat = cb * lmat
                dymb, xdtb = dym.astype(BF16), xdt.astype(BF16)
                dg = _dot(dymb, xdtb, _NT)
                dxdt = _dot(gmat.astype(BF16), dymb, _TN)
                dcb = dcb + dg * lmat
                ej = dg * gmat
                dcs = _dot(ej, ones, _NN, HI) - _dot(ej, ones, _TN, HI)
                ch = _dot(cg, hm.astype(BF16), _NT)
                dye = dym * ecs
                dcs = dcs + jnp.sum(dye * ch, axis=1, keepdims=True)
                dcg = dcg + _dot(dye.astype(BF16), hm.astype(BF16), _NN)
                dhp = _dot(dye.astype(BF16), cg, _TN)
                wmat = _dot(bg, dhm.astype(BF16), _NT)
                xd = xdt * decay
                dxdt = dxdt + decay * wmat
                ddl = jnp.sum(xd * wmat, axis=1, keepdims=True)
                dlast = jnp.sum(ddl, axis=0, keepdims=True) + el * jnp.sum(jnp.sum(dhm * hm, axis=1, keepdims=True), axis=0, keepdims=True)
                dcs = dcs - ddl + jnp.where(rowq == Q - 1, dlast, 0.0)
                dbg = dbg + _dot(xd.astype(BF16), dhm.astype(BF16), _NN)
                dh_out[p] = dh_out[p] + el * dhm + dhp
                da = _dot(triu, dcs, _NN, HI)
                aj = jnp.sum(jnp.where(lane == j, a_r, 0.0), axis=1, keepdims=True)
                ddtj = da * aj + jnp.sum(dxdt * x2, axis=1, keepdims=True)
                ddt = ddt + jnp.where(lane == j, ddtj, 0.0)
                dalog = dalog + jnp.where(lane == j, jnp.sum(da * dtj, axis=0, keepdims=True) * aj, 0.0)
                dxs_acc[p] = dxs_acc[p] + dxdt * dtj
            dcbb = dcb.astype(BF16)
            dc_ref[:, gs] = dcg + _dot(dcbb, bg, _NN)
            db_ref[:, gs] = dbg + _dot(dcbb, cg, _TN)
        for p in range(N_PAIRS):
            dxs_ref[:, p * LANES:(p + 1) * LANES] = dxs_acc[p]
            dh_scr[p] = dh_out[p]
        ddtraw = ddt * _sigmoid(dtr_ref[...] + br[...])
        ddt_ref[...] = ddtraw
        dbias_ref[...] += jnp.sum(ddtraw, axis=0, keepdims=True)
        dalog_ref[...] += dalog

    return pl.pallas_call(
        body, name="ssd_bwd", grid=(b, nc),
        in_specs=[rows(D_SSD), rows(D_BC), rows(D_BC), rows(LANES), dtt_spec, state, rows(D_SSD), const((1, LANES)),
                  const((LANES, 1)), const((1, LANES)), const((LANES, 1)), const((LANES, N_HEADS * LANES))],
        out_specs=[rows(D_SSD), rows(D_BC), rows(D_BC), rows(LANES), const((1, LANES)), const((1, LANES))],
        out_shape=[jax.ShapeDtypeStruct((t, D_SSD), F32), jax.ShapeDtypeStruct((t, D_BC), F32),
                   jax.ShapeDtypeStruct((t, D_BC), F32), jax.ShapeDtypeStruct((t, LANES), F32),
                   jax.ShapeDtypeStruct((1, LANES), F32), jax.ShapeDtypeStruct((1, LANES), F32)],
        scratch_shapes=[pltpu.VMEM((N_PAIRS, LANES, SSD_STATE), F32)],
        compiler_params=_cparams(("arbitrary", "arbitrary")),
    )(xs, bm, cm, dtraw, dtraw.T, hprev, dy, bias_r, bias_c, alog_r, alog_c, _head_expand())


def _split_w_in(w_in):
    w_dt = jnp.pad(w_in[:, D_QKVZ + D_CONV:], ((0, 0), (0, LANES - N_HEADS)))
    return w_in[:, :D_QKVZ], w_in[:, D_QKVZ:D_QKVZ + D_CONV], w_dt


def mixer_fwd(hb, p, cosv, sinv, b):
    t = hb.shape[0]
    w_a, w_b, w_c = _split_w_in(p['w_in'])
    qkvz = mm("in_qkvz", [(hb, w_a, 'nn')], D_QKVZ)
    xbc = mm("in_xbc", [(hb, w_b, 'nn')], D_CONV)
    dtraw = mm("in_dt", [(hb, w_c, 'nn')], LANES)
    mixed, *lses = attn_fwd(qkvz, cosv, sinv, b)
    attn = attn_norm_fwd(mixed, p['attn_norm_w'])
    xs, bm, cm = conv_fwd(xbc, p['conv_w'], p['conv_b'])
    y, hprev = ssd_fwd(xs, bm, cm, dtraw, p['dt_bias'], p['a_log'], b)
    dskip = jnp.repeat(p['d_skip'].reshape(-1), HEAD_DIM).reshape(1, D_SSD)
    yg, = rowwise("ssd_gate", _gate, [y, xs, Op(qkvz, D_SSD, 3)], [dskip, p['ssd_norm_w']], [(t, D_SSD, BF16)])
    mix = mm("out_proj", [(attn, p['w_out'][:D_ATTN], 'nn'), (yg, p['w_out'][D_ATTN:], 'nn')], D_MODEL)
    res = dict(hb=hb, qkvz=qkvz, xbc=xbc, dtraw=dtraw, mixed=mixed, lses=lses, attn=attn, xs=xs, bm=bm, cm=cm,
               y=y, hprev=hprev, dskip=dskip, yg=yg, cosv=cosv, sinv=sinv)
    return mix, res


def mixer_bwd(r, p, dmix, dh_resid, b):
    t = dmix.shape[0]
    w_a, w_b, w_c = _split_w_in(p['w_in'])
    w_out = p['w_out']
    dattn = mm("out_bwd_dattn", [(dmix, w_out[:D_ATTN], 'nt')], D_ATTN)
    dyg = mm("out_bwd_dyg", [(dmix, w_out[D_ATTN:], 'nt')], D_SSD)
    dw_out = jnp.concatenate([mm_tn("out_bwd_dw_a", r['attn'], dmix, BF16),
                              mm_tn("out_bwd_dw_y", r['yg'], dmix, BF16)], axis=0)

    def gate_bwd(dy_, y_, xs_, z_, ds_, w_):
        _, vjp = jax.vjp(_gate, y_, xs_, z_, ds_, w_)
        return vjp(dy_)

    dy, dxs_a, dz, ddskip, dssd_norm = rowwise(
        "ssd_gate_bwd", gate_bwd, [dyg, r['y'], r['xs'], Op(r['qkvz'], D_SSD, 3)], [r['dskip'], p['ssd_norm_w']],
        [(t, D_SSD, F32), (t, D_SSD, F32), (t, D_SSD, BF16)], accs=[(1, D_SSD), (1, D_SSD)])
    dxs_b, dbm, dcm, ddtraw, ddt_bias, da_log = ssd_bwd(r['xs'], r['bm'], r['cm'], r['dtraw'], p['dt_bias'], p['a_log'],
                                                        r['hprev'], dy, b)
    dxs, = rowwise("ssd_dxs_sum", lambda a_, b_: a_ + b_, [dxs_a, dxs_b], [], [(t, D_SSD, F32)])
    dxbc, dconv_w, dconv_b = conv_bwd(r['xbc'], p['conv_w'], p['conv_b'], dxs, dbm, dcm)
    dmixed, dattn_norm = attn_norm_bwd(dattn, r['mixed'], p['attn_norm_w'])
    dq, dk, dv = attn_bwd(r['qkvz'], r['cosv'], r['sinv'], dmixed, r['mixed'], r['lses'], b)
    wq, wk, wv, wz = (w_a[:, i * D_ATTN:(i + 1) * D_ATTN] for i in range(4))
    dh = mm("in_bwd_dh", [(dq, wq, 'nt'), (dk, wk, 'nt'), (dv, wv, 'nt'), (dz, wz, 'nt'), (dxbc, w_b, 'nt'),
                          (ddtraw, w_c, 'nt')], D_MODEL, add=dh_resid, tn=512)
    h = r['hb']
    dw_in = jnp.concatenate([mm_tn("in_bwd_dwq", h, dq, BF16), mm_tn("in_bwd_dwk", h, dk, BF16),
                             mm_tn("in_bwd_dwv", h, dv, BF16), mm_tn("in_bwd_dwz", h, dz, BF16),
                             mm_tn("in_bwd_dwx", h, dxbc, BF16), mm_tn("in_bwd_dwdt", h, ddtraw, BF16)[:, :N_HEADS]], axis=1)
    head_sum = lambda v: v.reshape(N_HEADS, HEAD_DIM).sum(axis=1).reshape(1, N_HEADS)
    grads = dict(w_in=dw_in, w_out=dw_out, conv_w=dconv_w, conv_b=dconv_b, dt_bias=ddt_bias[:, :N_HEADS],
                 a_log=da_log[:, :N_HEADS], d_skip=head_sum(ddskip), attn_norm_w=dattn_norm, ssd_norm_w=dssd_norm)
    return dh, grads


FFN_COL = ('ffn1_gate', 'ffn1_up', 'ffn2_gate', 'ffn2_up')
FFN_ROW = ('ffn1_down', 'ffn2_down')
CONV_W_COMM = (8, 2 * LANES)
SMALL = 'small'


def comm_shape(k, shapes):
    if k in FFN_COL:
        return (D_MODEL, FF_PAD)
    if k in FFN_ROW:
        return (FF_PAD, D_MODEL)
    if k == 'conv_w':
        return CONV_W_COMM
    if k == SMALL:
        n = sum(int(np.prod(shapes[r])) for r in REPLICATED)
        return (-(-n // (8 * LANES)) * 8, LANES)
    return tuple(shapes[k][1:])


def to_comm(k, vals, shapes):
    if k == SMALL:
        flat = jnp.concatenate([vals[r].reshape(-1) for r in REPLICATED])
        r_, c_ = comm_shape(k, shapes)
        return jnp.pad(flat, (0, r_ * c_ - flat.size)).reshape(r_, c_)
    a = vals[k].reshape(shapes[k][1:])
    r_, c_ = comm_shape(k, shapes)
    return jnp.pad(a, ((0, r_ - a.shape[0]), (0, c_ - a.shape[1])))


def from_comm(k, a, shapes):
    if k == SMALL:
        flat, out, off = a.reshape(-1), {}, 0
        for r in REPLICATED:
            n = int(np.prod(shapes[r]))
            out[r] = flat[off:off + n].reshape(shapes[r])
            off += n
        return out
    shp = shapes[k][1:]
    return {k: a[:shp[0], :shp[1]].reshape(shapes[k])}


def full_weight(k, g):
    if k in FFN_COL:
        return jnp.concatenate([g[p] for p in range(N_DEV)], axis=1)
    if k == 'conv_w':
        return jnp.transpose(g[:, :CONV_WIDTH, :D_CONV // N_DEV], (1, 0, 2)).reshape(CONV_WIDTH, D_CONV)
    return g.reshape(N_DEV * g.shape[1], g.shape[2])


def grad_shards(k, g):
    if k in FFN_COL:
        return jnp.stack([g[:, p * FF_PAD:(p + 1) * FF_PAD] for p in range(N_DEV)])
    if k == 'conv_w':
        s = jnp.transpose(g.reshape(CONV_WIDTH, N_DEV, D_CONV // N_DEV), (1, 0, 2))
        return jnp.pad(s, ((0, 0), (0, CONV_W_COMM[0] - CONV_WIDTH), (0, CONV_W_COMM[1] - D_CONV // N_DEV)))
    return g.reshape(N_DEV, g.shape[0] // N_DEV, g.shape[1])


def _flip(v, bit):
    return 1 - v if bit else v


N_PEER_COPIES = N_DEV - 1


def _comm_call(name, body, arrs, out_shape):
    n = len(arrs)
    return pl.pallas_call(
        functools.partial(body, n), name=name, out_shape=out_shape,
        in_specs=[pl.BlockSpec(memory_space=pl.ANY)] * n, out_specs=[pl.BlockSpec(memory_space=pl.ANY)] * n,
        scratch_shapes=[pltpu.SemaphoreType.DMA((n * N_PEER_COPIES,)), pltpu.SemaphoreType.DMA((n * N_PEER_COPIES,)),
                        pltpu.SemaphoreType.DMA((n,))],
    )(*arrs)


def all_gather(arrs):
    def body(n, *refs):
        x_refs, out_refs, (send_sems, recv_sems, local_sems) = refs[:n], refs[n:2 * n], refs[2 * n:]
        x, y, c = lax.axis_index("x"), lax.axis_index("y"), lax.axis_index("c")
        me, sibling = (x, y, c), (x, y, 1 - c)
        chips = [(1 - x, y), (x, 1 - y), (1 - x, 1 - y)]

        def copy(a, k, block, to, src=None):
            px, py, pc = block
            dst = out_refs[a].at[4 * px + 2 * py + pc]
            return pltpu.make_async_remote_copy(
                src_ref=dst if src is None else src, dst_ref=dst, send_sem=send_sems.at[a * N_PEER_COPIES + k],
                recv_sem=recv_sems.at[a * N_PEER_COPIES + k], device_id=to, device_id_type=MESH)

        mine = [pltpu.make_async_copy(x_refs[a], out_refs[a].at[4 * x + 2 * y + c], local_sems.at[a]) for a in range(n)]
        started = []
        for a in range(n):
            mine[a].start()
            first = [copy(a, 0, me, sibling, src=x_refs[a])]
            first += [copy(a, 1 + j, me, (*chip, c), src=x_refs[a]) for j, chip in enumerate(chips)]
            for cp in first:
                cp.start()
            started += first
        for j, chip in enumerate(chips):
            for a in range(n):
                copy(a, 1 + j, (*chip, c), me).wait_recv()
                cp = copy(a, 4 + j, (*chip, c), sibling)
                cp.start()
                started.append(cp)
        for a in range(n):
            copy(a, 0, sibling, me).wait_recv()
            for j, chip in enumerate(chips):
                copy(a, 4 + j, (*chip, 1 - c), me).wait_recv()
        for cp in started:
            cp.wait_send()
        for cp in mine:
            cp.wait()

    return _comm_call("all_gather_weights", body, arrs,
                      [jax.ShapeDtypeStruct((N_DEV,) + a.shape, a.dtype) for a in arrs])


def all_to_all(arrs):
    def body(n, *refs):
        s_refs, r_refs, (send_sems, recv_sems, local_sems) = refs[:n], refs[n:2 * n], refs[2 * n:]
        x, y, c = lax.axis_index("x"), lax.axis_index("y"), lax.axis_index("c")
        me = 4 * x + 2 * y + c

        def peer(k):
            return _flip(x, k & 4), _flip(y, k & 2), _flip(c, k & 1)

        def copy(a, k, landing):
            px, py, pc = peer(k)
            p = 4 * px + 2 * py + pc
            src, dst = (s_refs[a].at[me], r_refs[a].at[p]) if landing else (s_refs[a].at[p], r_refs[a].at[me])
            return pltpu.make_async_remote_copy(
                src_ref=src, dst_ref=dst, send_sem=send_sems.at[a * N_PEER_COPIES + k - 1],
                recv_sem=recv_sems.at[a * N_PEER_COPIES + k - 1], device_id=(px, py, pc), device_id_type=MESH)

        mine = [pltpu.make_async_copy(s_refs[a].at[me], r_refs[a].at[me], local_sems.at[a]) for a in range(n)]
        sends = [copy(a, k, False) for a in range(n) for k in range(1, N_DEV)]
        for cp in mine + sends:
            cp.start()
        for a in range(n):
            for k in range(1, N_DEV):
                copy(a, k, True).wait_recv()
        for cp in sends:
            cp.wait_send()
        for cp in mine:
            cp.wait()

    return _comm_call("all_to_all_grads", body, arrs, [jax.ShapeDtypeStruct(a.shape, a.dtype) for a in arrs])


def adamw(name, recv, w, m, v, tm):
    rows, cols = w.shape
    c1 = 1.0 / (1.0 - ADAM_B1 ** ADAM_STEP)
    c2 = 1.0 / (1.0 - ADAM_B2 ** ADAM_STEP)

    def fn(*a):
        g = a[0]
        for s in range(1, N_DEV):
            g = g + a[s]
        w_, m_, v_ = a[N_DEV:]
        m_ = ADAM_B1 * m_ + (1.0 - ADAM_B1) * g
        v_ = ADAM_B2 * v_ + (1.0 - ADAM_B2) * jnp.square(g)
        delta = -ADAM_LR * ((m_ * c1) / (jnp.sqrt(v_ * c2) + ADAM_EPS) + ADAM_WD * w_)
        return g, delta, m_, v_

    flat = recv.reshape(N_DEV * rows, cols)
    ins = [Op(flat, cols, 0, s * (rows // tm)) for s in range(N_DEV)] + [w, m, v]
    return rowwise(name, fn, ins, [], [(rows, cols, F32)] * 4, tm=tm)


ADAMW_TM = {'ffn1_gate': 256, 'ffn1_up': 256, 'ffn1_down': 128, 'w_in': 32, 'conv_w': 8, 'w_out': 64,
            'ffn2_gate': 256, 'ffn2_up': 256, 'ffn2_down': 128}


def kernel(x, positions, ln1_g, ln1_b, ffn1_gate, ffn1_up, ffn1_down, w_in, conv_w, conv_b, dt_bias, a_log, d_skip, attn_norm_w, ssd_norm_w, w_out, ln2_g, ln2_b, ffn2_gate, ffn2_up, ffn2_down, ln3_g, ln3_b, loss_target, m_ln1_g, m_ln1_b, m_ffn1_gate, m_ffn1_up, m_ffn1_down, m_w_in, m_conv_w, m_conv_b, m_dt_bias, m_a_log, m_d_skip, m_attn_norm_w, m_ssd_norm_w, m_w_out, m_ln2_g, m_ln2_b, m_ffn2_gate, m_ffn2_up, m_ffn2_down, m_ln3_g, m_ln3_b, v_ln1_g, v_ln1_b, v_ffn1_gate, v_ffn1_up, v_ffn1_down, v_w_in, v_conv_w, v_conv_b, v_dt_bias, v_a_log, v_d_skip, v_attn_norm_w, v_ssd_norm_w, v_w_out, v_ln2_g, v_ln2_b, v_ffn2_gate, v_ffn2_up, v_ffn2_down, v_ln3_g, v_ln3_b):
    args = dict(locals())
    wl = {k: args[k] for k in WEIGHTS}
    ml = {k: args["m_" + k] for k in WEIGHTS}
    vl = {k: args["v_" + k] for k in WEIGHTS}
    shapes = {k: wl[k].shape for k in WEIGHTS}
    b, s, dm = x.shape
    t = b * s

    w_comm = {k: to_comm(k, wl, shapes) for k in SHARDED + (SMALL,)}
    gathered = all_gather([w_comm[k] if k == 'conv_w' else w_comm[k].astype(BF16) for k in SHARDED])
    p = {k: full_weight(k, g) for k, g in zip(SHARDED, gathered)}
    for k in REPLICATED:
        p[k] = wl[k].reshape(1, -1)

    x2 = x.reshape(t, dm)
    cosv, sinv = rope_tables(positions)
    h1, h1b, res1 = ffn_fwd("ffn1", x2, x2, p['ffn1_gate'], p['ffn1_up'], p['ffn1_down'], p['ln1_g'], p['ln1_b'])
    mix, resm = mixer_fwd(h1b, p, cosv, sinv, b)
    h2, h2b = resid_ln_fwd("ln2", 1.0, h1, mix, p['ln2_g'], p['ln2_b'])
    h3, _, res3 = ffn_fwd("ffn2", h2, h2b, p['ffn2_gate'], p['ffn2_up'], p['ffn2_down'], p['ln3_g'], p['ln3_b'])

    def loss_fn(y, tgt):
        e = y - tgt
        return e * (1.0 / dm), jnp.sum(e * e, axis=0, keepdims=True)

    dh3, sq = rowwise("loss", loss_fn, [h3, loss_target.reshape(t, dm)], [], [(t, dm, F32)], accs=[(1, dm)])
    loss = lax.psum(jnp.sum(sq) * (0.5 / dm), AXES)

    small = {}
    dh2_res, df2, small['ln3_g'], small['ln3_b'] = resid_ln_bwd("ln3_bwd", 0.5, h2, res3[5], p['ln3_g'], p['ln3_b'], dh3)
    full = {}
    dh2, full['ffn2_gate'], full['ffn2_up'], full['ffn2_down'] = ffn_bwd("ffn2", res3, p['ffn2_gate'], p['ffn2_up'],
                                                                       p['ffn2_down'], df2, dh2_res)
    dh1_res, dmix, small['ln2_g'], small['ln2_b'] = resid_ln_bwd("ln2_bwd", 1.0, h1, mix, p['ln2_g'], p['ln2_b'], dh2)
    dh1, gm = mixer_bwd(resm, p, dmix, dh1_res, b)
    for k in ('w_in', 'w_out', 'conv_w'):
        full[k] = gm[k]
    for k in ('conv_b', 'dt_bias', 'a_log', 'd_skip', 'attn_norm_w', 'ssd_norm_w'):
        small[k] = gm[k]
    dx_res, df1, small['ln1_g'], small['ln1_b'] = resid_ln_bwd("ln1_bwd", 0.5, x2, res1[5], p['ln1_g'], p['ln1_b'], dh1)
    dx, full['ffn1_gate'], full['ffn1_up'], full['ffn1_down'] = ffn_bwd("ffn1", res1, p['ffn1_gate'], p['ffn1_up'],
                                                                      p['ffn1_down'], df1, dx_res)

    keys = SHARDED + (SMALL,)
    small_part = to_comm(SMALL, small, shapes)
    send = [grad_shards(k, full[k]) for k in SHARDED] + [jnp.broadcast_to(small_part[None], (N_DEV,) + small_part.shape)]
    recv = all_to_all(send)
    outs = [{}, {}, {}, {}]
    for k, r in zip(keys, recv):
        tm = ADAMW_TM.get(k, r.shape[1])
        res = adamw(f"adamw_{k}", r, w_comm[k], to_comm(k, ml, shapes), to_comm(k, vl, shapes), tm)
        for o, a in zip(outs, res):
            o.update(from_comm(k, a, shapes))
    return (loss, dx.reshape(b, s, dm), *[o[k] for o in outs for k in WEIGHTS])
```

```python
import functools
import math

import jax
import jax.numpy as jnp
import numpy as np
from jax import lax
from jax.experimental import pallas as pl
from jax.experimental.pallas import tpu as pltpu

F32, BF16 = jnp.float32, jnp.bfloat16
HI = lax.Precision.HIGHEST
MESH = pl.DeviceIdType.MESH
AXES = ("x", "y", "c")
N_DEV = 8

D_MODEL = 1024
SEQ = 2048
HEAD_DIM = 64
N_HEADS = 12
D_ATTN = N_HEADS * HEAD_DIM
DILATIONS = (1, 4, 16)
ATTN_BLOCK = 128
ROPE_THETA = 500000.0
ROPE_DIM = 16
D_SSD = 768
SSD_GROUPS = 4
SSD_STATE = 128
SSD_CHUNK = 128
D_BC = SSD_GROUPS * SSD_STATE
D_CONV = D_SSD + 2 * D_BC
CONV_WIDTH = 4
D_QKVZ = 3 * D_ATTN + D_SSD
D_IN_PROJ = D_QKVZ + D_CONV + N_HEADS
D_FF = 2816
ALPHA = 2.0 ** 0.25
LN_EPS = 1e-5
RMS_EPS = 1e-6
ADAM_LR, ADAM_B1, ADAM_B2, ADAM_EPS, ADAM_WD, ADAM_STEP = 0.001, 0.9, 0.999, 1e-08, 0.01, 10

LANES = 128
VMEM_LIMIT = 52 * 1024 * 1024
NEG = -1e30

WEIGHTS = ['ln1_g', 'ln1_b', 'ffn1_gate', 'ffn1_up', 'ffn1_down', 'w_in', 'conv_w', 'conv_b', 'dt_bias', 'a_log',
           'd_skip', 'attn_norm_w', 'ssd_norm_w', 'w_out', 'ln2_g', 'ln2_b', 'ffn2_gate', 'ffn2_up', 'ffn2_down',
           'ln3_g', 'ln3_b']
COL_SHARDED = ('ffn1_gate', 'ffn1_up', 'conv_w', 'ffn2_gate', 'ffn2_up')
ROW_SHARDED = ('ffn1_down', 'w_in', 'w_out', 'ffn2_down')
SHARDED = tuple(n for n in WEIGHTS if n in COL_SHARDED or n in ROW_SHARDED)
REPLICATED = tuple(n for n in WEIGHTS if n not in SHARDED)
FF_SHARD = D_FF // N_DEV
FF_PAD = -(-FF_SHARD // LANES) * LANES
D_FF_INT = N_DEV * FF_PAD


def _cparams(sem=None):
    return pltpu.CompilerParams(dimension_semantics=sem, vmem_limit_bytes=VMEM_LIMIT)


def _tile(n, prefs):
    for p in prefs:
        if n % p == 0:
            return p
    return n


class Op:
    def __init__(self, arr, bw=None, cb=0, ro=0):
        self.arr, self.bw, self.cb, self.ro = arr, (arr.shape[1] if bw is None else bw), cb, ro


def _op(a):
    return a if isinstance(a, Op) else Op(a)


def rowwise(name, fn, ins, consts, outs, accs=(), tm=256):
    ins = [_op(a) for a in ins]
    rows = outs[0][0]
    n_in, n_c, n_o, n_a = len(ins), len(consts), len(outs), len(accs)
    tm = min(tm, rows)
    assert rows % tm == 0, (name, rows, tm)

    def body(*refs):
        vals = [r[...].astype(F32) for r in refs[:n_in + n_c]]
        res = fn(*vals)
        res = res if isinstance(res, (tuple, list)) else (res,)
        o_refs = refs[n_in + n_c:n_in + n_c + n_o]
        a_refs = refs[n_in + n_c + n_o:]
        for r, v in zip(o_refs, res[:n_o]):
            r[...] = v.astype(r.dtype)
        if n_a:
            @pl.when(pl.program_id(0) == 0)
            def _():
                for r in a_refs:
                    r[...] = jnp.zeros(r.shape, r.dtype)
            for r, v in zip(a_refs, res[n_o:]):
                r[...] += v

    in_specs = [pl.BlockSpec((tm, o.bw), functools.partial(lambda i, o: (i + o.ro, o.cb), o=o)) for o in ins]
    in_specs += [pl.BlockSpec(c.shape, functools.partial(lambda i, nd: (0,) * nd, nd=c.ndim)) for c in consts]
    out_specs = [pl.BlockSpec((tm, w), lambda i: (i, 0)) for (_, w, _) in outs]
    out_specs += [pl.BlockSpec(s, functools.partial(lambda i, nd: (0,) * nd, nd=len(s))) for s in accs]
    out_shape = [jax.ShapeDtypeStruct((r, w), dt) for (r, w, dt) in outs]
    out_shape += [jax.ShapeDtypeStruct(s, F32) for s in accs]
    res = pl.pallas_call(
        body, name=name, grid=(rows // tm,), in_specs=in_specs, out_specs=out_specs, out_shape=out_shape,
        compiler_params=_cparams(("arbitrary",) if n_a else ("parallel",)),
    )(*[o.arr for o in ins], *consts)
    return res


MM_TM = 512
MM_TN = (1024, 896, 768, 512, 256, 128)
_NT = (((1,), (1,)), ((), ()))
_NN = (((1,), (0,)), ((), ()))
_TN = (((0,), (0,)), ((), ()))


def _dot(a, b, dn, precision=None):
    return lax.dot_general(a, b, dn, preferred_element_type=F32, precision=precision)


def _mm_specs(name, pairs, n_out, tm, tn):
    in_specs, args = [], []
    for a, b, mode in pairs:
        o = _op(a)
        in_specs.append(pl.BlockSpec((tm, o.bw), functools.partial(lambda j, i, o: (i, o.cb), o=o)))
        args.append(o.arr)
        if mode == 'nn':
            assert b.shape == (o.bw, n_out), (name, b.shape, o.bw, n_out)
            in_specs.append(pl.BlockSpec((o.bw, tn), lambda j, i: (0, j)))
        else:
            assert b.shape == (n_out, o.bw), (name, b.shape, o.bw, n_out)
            in_specs.append(pl.BlockSpec((tn, o.bw), lambda j, i: (j, 0)))
        args.append(b)
    return in_specs, args


def _mm_acc(refs, pairs):
    acc = None
    for k, (_, _, mode) in enumerate(pairs):
        d = _dot(refs[2 * k][...].astype(BF16), refs[2 * k + 1][...].astype(BF16), _NN if mode == 'nn' else _NT)
        acc = d if acc is None else acc + d
    return acc


def mm(name, pairs, n_out, add=None, out_dtype=F32, tm=MM_TM, tn=None):
    m = _op(pairs[0][0]).arr.shape[0]
    tn = tn or _tile(n_out, MM_TN)
    n_p = len(pairs)

    def body(*refs):
        acc = _mm_acc(refs, pairs)
        if add is not None:
            acc = acc + refs[2 * n_p][...]
        refs[-1][...] = acc.astype(refs[-1].dtype)

    in_specs, args = _mm_specs(name, pairs, n_out, tm, tn)
    tile = pl.BlockSpec((tm, tn), lambda j, i: (i, j))
    if add is not None:
        in_specs.append(tile)
        args.append(add)
    return pl.pallas_call(
        body, name=name, grid=(n_out // tn, m // tm), in_specs=in_specs, out_specs=tile,
        out_shape=jax.ShapeDtypeStruct((m, n_out), out_dtype),
        compiler_params=_cparams(("parallel", "parallel")),
    )(*args)


def mm_tn(name, a, b, out_dtype=F32, tt=1024):
    a, b = _op(a), _op(b)
    t = a.arr.shape[0]
    k, n = a.bw, b.bw
    tk = _tile(k, (512, 896, 768, 256, 128))
    tn = _tile(n, MM_TN)
    tt = min(tt, t)
    n_t = t // tt

    def body(a_ref, b_ref, o_ref, acc_ref):
        s = pl.program_id(2)
        d = _dot(a_ref[...].astype(BF16), b_ref[...].astype(BF16), _TN)

        @pl.when(s == 0)
        def _():
            acc_ref[...] = d

        @pl.when(s > 0)
        def _():
            acc_ref[...] += d

        @pl.when(s == n_t - 1)
        def _():
            o_ref[...] = acc_ref[...].astype(o_ref.dtype)

    return pl.pallas_call(
        body, name=name, grid=(k // tk, n // tn, n_t),
        in_specs=[pl.BlockSpec((tt, tk), functools.partial(lambda kk, nn, s, o: (s, o.cb * (o.bw // tk) + kk), o=a)),
                  pl.BlockSpec((tt, tn), functools.partial(lambda kk, nn, s, o: (s, o.cb * (o.bw // tn) + nn), o=b))],
        out_specs=pl.BlockSpec((tk, tn), lambda kk, nn, s: (kk, nn)),
        out_shape=jax.ShapeDtypeStruct((k, n), out_dtype),
        scratch_shapes=[pltpu.VMEM((tk, tn), F32)],
        compiler_params=_cparams(("parallel", "parallel", "arbitrary")),
    )(a.arr, b.arr)


def _sigmoid(x):
    return 1.0 / (1.0 + jnp.exp(-x))


def _silu(x):
    return x * _sigmoid(x)


def _softplus(x):
    return jnp.maximum(x, 0.0) + jnp.log(1.0 + jnp.exp(-jnp.abs(x)))


def _act(g, u):
    return _silu(g) * u


def _resid_ln(scale, h, branch, g, b):
    r = ALPHA * h + scale * branch
    mu = jnp.mean(r, axis=-1, keepdims=True)
    var = jnp.mean(jnp.square(r - mu), axis=-1, keepdims=True)
    return (r - mu) * lax.rsqrt(var + LN_EPS) * g + b


def _rms(t, w):
    return t * lax.rsqrt(jnp.mean(t * t, axis=-1, keepdims=True) + RMS_EPS) * w


def _branch_weights(l1, l2, l3):
    m = jnp.maximum(jnp.maximum(l1, l2), l3)
    e1, e2, e3 = jnp.exp(l1 - m), jnp.exp(l2 - m), jnp.exp(l3 - m)
    inv = 1.0 / (e1 + e2 + e3)
    return e1 * inv, e2 * inv, e3 * inv


def _gate(y, xs, z, dskip, w):
    return _rms((y + dskip * xs) * _silu(z), w)


def _rot(x):
    d = lax.broadcasted_iota(jnp.int32, x.shape, 1) % HEAD_DIM
    up = pltpu.roll(x, x.shape[1] - ROPE_DIM // 2, 1)
    down = jnp.where(d < ROPE_DIM, pltpu.roll(x, ROPE_DIM // 2, 1), 0.0)
    return jnp.where(d < ROPE_DIM // 2, up, down)


def ffn_gate_up(name, h, wg, wu):
    m, nf = h.shape[0], wg.shape[1]
    tn = _tile(nf, MM_TN)

    def body(h_ref, g_w, u_w, g_ref, u_ref, a_ref):
        hb = h_ref[...].astype(BF16)
        g = _dot(hb, g_w[...].astype(BF16), _NN)
        u = _dot(hb, u_w[...].astype(BF16), _NN)
        g_ref[...] = g.astype(g_ref.dtype)
        u_ref[...] = u.astype(u_ref.dtype)
        a_ref[...] = _act(g, u).astype(a_ref.dtype)

    in_specs, args = _mm_specs(name, [(h, wg, 'nn')], nf, MM_TM, tn)
    in_specs.append(in_specs[1])
    tile = pl.BlockSpec((MM_TM, tn), lambda j, i: (i, j))
    return pl.pallas_call(
        body, name=name, grid=(nf // tn, m // MM_TM), in_specs=in_specs, out_specs=[tile] * 3,
        out_shape=[jax.ShapeDtypeStruct((m, nf), BF16)] * 3, compiler_params=_cparams(("parallel", "parallel")),
    )(*args, wu)


def ffn_da_act(name, df, wd, g, u):
    m, nf = df.shape[0], wd.shape[0]
    tn = _tile(nf, MM_TN)

    def body(df_ref, w_ref, g_ref, u_ref, dg_ref, du_ref):
        da = _dot(df_ref[...].astype(BF16), w_ref[...].astype(BF16), _NT)
        _, vjp = jax.vjp(_act, g_ref[...].astype(F32), u_ref[...].astype(F32))
        dg, du = vjp(da)
        dg_ref[...] = dg.astype(dg_ref.dtype)
        du_ref[...] = du.astype(du_ref.dtype)

    in_specs, args = _mm_specs(name, [(df, wd, 'nt')], nf, MM_TM, tn)
    tile = pl.BlockSpec((MM_TM, tn), lambda j, i: (i, j))
    return pl.pallas_call(
        body, name=name, grid=(nf // tn, m // MM_TM), in_specs=in_specs + [tile, tile], out_specs=[tile] * 2,
        out_shape=[jax.ShapeDtypeStruct((m, nf), BF16)] * 2, compiler_params=_cparams(("parallel", "parallel")),
    )(*args, g, u)


def resid_ln_fwd(name, scale, h, branch, ln_g, ln_b):
    t = h.shape[0]

    def fn(*a):
        y = _resid_ln(scale, *a)
        return y, y

    return rowwise(name, fn, [h, branch], [ln_g, ln_b], [(t, D_MODEL, F32), (t, D_MODEL, BF16)], tm=512)


def ffn_fwd(tag, hb, wg, wu, wd):
    g, u, a = ffn_gate_up(f"{tag}_gate_up", hb, wg, wu)
    f = mm(f"{tag}_down", [(a, wd, 'nn')], D_MODEL)
    return f, (hb, g, u, a)


def ln_loss_bwd(name, h, branch, target, ln_g, ln_b):
    t, dm = h.shape

    def fn(h_, br_, tgt, g_, b_):
        y, vjp = jax.vjp(functools.partial(_resid_ln, 0.5), h_, br_, g_, b_)
        e = y - tgt
        return (*vjp(e * (1.0 / dm)), jnp.sum(e * e, axis=0, keepdims=True))

    return rowwise(name, fn, [h, branch, target], [ln_g, ln_b], [(t, dm, F32), (t, dm, F32)],
                   accs=[(1, dm), (1, dm), (1, dm)], tm=512)


def resid_ln_bwd(name, scale, h, branch, ln_g, ln_b, dout, extra=None):
    t = h.shape[0]

    def fn(h_, br_, do_, *rest):
        g_, b_ = rest[-2], rest[-1]
        _, vjp = jax.vjp(functools.partial(_resid_ln, scale), h_, br_, g_, b_)
        dh, dbr, dg, db = vjp(do_)
        if extra is not None:
            dh = dh + rest[0]
        return dh, dbr, dg, db

    ins = [h, branch, dout] + ([extra] if extra is not None else [])
    return rowwise(name, fn, ins, [ln_g, ln_b], [(t, D_MODEL, F32), (t, D_MODEL, F32)],
                   accs=[(1, D_MODEL), (1, D_MODEL)], tm=512)


def ffn_bwd(tag, res, wg, wu, wd, df, dh_resid):
    hb, g, u, a = res
    dg, du = ffn_da_act(f"{tag}_bwd_da_act", df, wd, g, u)
    dwd = mm_tn(f"{tag}_bwd_dwd", a, df, BF16)
    dh = mm(f"{tag}_bwd_dh", [(dg, wg, 'nt'), (du, wu, 'nt')], D_MODEL, add=dh_resid, tn=512)
    dwg = mm_tn(f"{tag}_bwd_dwg", hb, dg, BF16)
    dwu = mm_tn(f"{tag}_bwd_dwu", hb, du, BF16)
    return dh, dwg, dwu, dwd


def rope_tables(positions):
    inv_freq = ROPE_THETA ** (-jnp.arange(0, ROPE_DIM, 2, dtype=F32) / ROPE_DIM)
    ang = positions.reshape(-1, 1).astype(F32) * inv_freq
    c, s = jnp.cos(ang), jnp.sin(ang)
    t = ang.shape[0]
    cosv = jnp.concatenate([c, c, jnp.ones((t, HEAD_DIM - ROPE_DIM), F32)], axis=1)
    sinv = jnp.concatenate([-s, s, jnp.zeros((t, HEAD_DIM - ROPE_DIM), F32)], axis=1)
    return jnp.tile(cosv, (1, 2)), jnp.tile(sinv, (1, 2))


def _pair_masks():
    lane = lax.broadcasted_iota(jnp.int32, (1, LANES), 1)
    return (lane < HEAD_DIM, lane >= HEAD_DIM)


def _band_masks():
    row = lax.broadcasted_iota(jnp.int32, (ATTN_BLOCK, ATTN_BLOCK), 0)
    col = lax.broadcasted_iota(jnp.int32, (ATTN_BLOCK, ATTN_BLOCK), 1)
    return col >= row, col <= row


def _residue_blocks():
    out = []
    for g, d in enumerate(DILATIONS):
        for r in range(d):
            for i in range(SEQ // d // ATTN_BLOCK):
                rows = lambda j: pl.ds(r + j * ATTN_BLOCK * d, ATTN_BLOCK, stride=d) if d > 1 else pl.ds(j * ATTN_BLOCK, ATTN_BLOCK)
                out.append((g, rows(i), rows(i - 1) if i > 0 else None))
    return out


N_HEAD_PAIRS = D_ATTN // LANES
SCALE = HEAD_DIM ** -0.5
ATTN_GROUP = 4


def _block_operands(qr, kr, v_ref, cur, prev):
    prev_ok, cur_ok = _band_masks()
    if prev is None:
        return qr[cur, :], kr[cur, :].astype(BF16), v_ref[cur, :], cur_ok
    kcat = jnp.concatenate([kr[prev, :], kr[cur, :]], axis=0).astype(BF16)
    vcat = jnp.concatenate([v_ref[prev, :], v_ref[cur, :]], axis=0)
    return qr[cur, :], kcat, vcat, jnp.concatenate([prev_ok, cur_ok], axis=1)


def _attn_specs(b):
    col = lambda cb: pl.BlockSpec((SEQ, LANES), lambda bb, hp: (bb, cb + hp))
    tab = pl.BlockSpec((SEQ, LANES), lambda bb, hp: (bb, 0))
    return col, tab


def attn_fwd(qkvz, cosv, sinv, b):
    t = qkvz.shape[0]
    col, tab = _attn_specs(b)
    blocks = _residue_blocks()

    def body(q_ref, k_ref, v_ref, c_ref, s_ref, o_ref, l1_ref, l2_ref, l3_ref, qr, kr, o1, o2, o3):
        l_refs, o_scr = (l1_ref, l2_ref, l3_ref), (o1, o2, o3)
        c, s = c_ref[...], s_ref[...]
        q, k = q_ref[...], k_ref[...]
        qr[...] = q * c + _rot(q) * s
        kr[...] = k * c + _rot(k) * s
        masks = _pair_masks()
        for lo in range(0, len(blocks), ATTN_GROUP):
            chains = []
            for g, cur, prev in blocks[lo:lo + ATTN_GROUP]:
                q2, kcat, vcat, ok = _block_operands(qr, kr, v_ref, cur, prev)
                for m in masks:
                    qm = jnp.where(m, q2, 0.0).astype(BF16)
                    chains.append(dict(g=g, cur=cur, m=m, v=jnp.where(m, vcat, 0.0).astype(BF16),
                                       s=jnp.where(ok, _dot(qm, kcat, _NT) * SCALE, NEG)))
            for ch in chains:
                mx = jnp.max(ch['s'], axis=1, keepdims=True)
                p = jnp.exp(ch['s'] - mx)
                den = jnp.sum(p, axis=1, keepdims=True)
                ch.update(p=p.astype(BF16), inv=1.0 / den, lse=mx + jnp.log(den))
            for ch in chains:
                ch['o'] = _dot(ch['p'], ch['v'], _NN) * ch['inv']
            for c0, c1 in zip(chains[0::2], chains[1::2]):
                o_scr[c0['g']][c0['cur'], :] = c0['o'] + c1['o']
                l_refs[c0['g']][c0['cur'], :] = jnp.where(c0['m'], c0['lse'], c1['lse'])
        w1, w2, w3 = _branch_weights(l1_ref[...], l2_ref[...], l3_ref[...])
        o_ref[...] = w1 * o1[...] + w2 * o2[...] + w3 * o3[...]

    shp = jax.ShapeDtypeStruct((t, D_ATTN), F32)
    return pl.pallas_call(
        body, name="attn_fwd", grid=(b, N_HEAD_PAIRS),
        in_specs=[col(0), col(N_HEAD_PAIRS), col(2 * N_HEAD_PAIRS), tab, tab],
        out_specs=[col(0)] * 4, out_shape=[shp] * 4,
        scratch_shapes=[pltpu.VMEM((SEQ, LANES), F32)] * 5,
        compiler_params=_cparams(("parallel", "parallel")),
    )(qkvz, qkvz, qkvz, cosv, sinv)


def attn_bwd(qkvz, cosv, sinv, dmix, mixed, lses, b):
    t = qkvz.shape[0]
    col, tab = _attn_specs(b)
    blocks = _residue_blocks()
    hd = np.arange(LANES) // HEAD_DIM
    head_ones = jnp.asarray((hd[:, None] == hd[None, :]).astype(np.float32))

    def body(q_ref, k_ref, v_ref, c_ref, s_ref, dm_ref, mx_ref, l1_ref, l2_ref, l3_ref, ones_ref,
             dq_out, dk_out, dv_out, qr, kr, do1, do2, do3, dd1, dd2, dd3, dq_ref, dk_ref, dv_ref):
        l_refs, do_scr, dd_scr = (l1_ref, l2_ref, l3_ref), (do1, do2, do3), (dd1, dd2, dd3)
        c, s = c_ref[...], s_ref[...]
        q, k = q_ref[...], k_ref[...]
        qr[...] = q * c + _rot(q) * s
        kr[...] = k * c + _rot(k) * s
        dm = dm_ref[...]
        tot = _dot(dm * mx_ref[...], ones_ref[...], _NN, HI)
        for w, do_g, dd_g in zip(_branch_weights(l1_ref[...], l2_ref[...], l3_ref[...]), do_scr, dd_scr):
            do_g[...] = w * dm
            dd_g[...] = w * tot
        dq_ref[...] = jnp.zeros((SEQ, LANES), F32)
        dk_ref[...] = jnp.zeros((SEQ, LANES), F32)
        dv_ref[...] = jnp.zeros((SEQ, LANES), F32)
        masks = _pair_masks()
        for lo in range(0, len(blocks), ATTN_GROUP):
            chains = []
            for g, cur, prev in blocks[lo:lo + ATTN_GROUP]:
                q2, kcat, vcat, ok = _block_operands(qr, kr, v_ref, cur, prev)
                vcat = vcat.astype(BF16)
                do2_, l2, dd2_ = do_scr[g][cur, :], l_refs[g][cur, :], dd_scr[g][cur, :]
                l2s, dd2s = pltpu.roll(l2, HEAD_DIM, 1), pltpu.roll(dd2_, HEAD_DIM, 1)
                for m in masks:
                    qm = jnp.where(m, q2, 0.0).astype(BF16)
                    dom = jnp.where(m, do2_, 0.0).astype(BF16)
                    lrep, ddrep = jnp.where(m, l2, l2s), jnp.where(m, dd2_, dd2s)
                    if prev is not None:
                        lrep, ddrep = jnp.concatenate([lrep, lrep], axis=1), jnp.concatenate([ddrep, ddrep], axis=1)
                    chains.append(dict(cur=cur, prev=prev, qm=qm, dom=dom, km=jnp.where(m, kcat, 0), lrep=lrep, ddrep=ddrep,
                                       s=jnp.where(ok, _dot(qm, kcat, _NT) * SCALE, NEG), dp=_dot(dom, vcat, _NT)))
            for ch in chains:
                p = jnp.exp(ch['s'] - ch['lrep'])
                ch.update(p=p.astype(BF16), ds=(p * (ch['dp'] - ch['ddrep']) * SCALE).astype(BF16))
            for ch in chains:
                ch.update(dq=_dot(ch['ds'], ch['km'], _NN), dk=_dot(ch['ds'], ch['qm'], _TN), dv=_dot(ch['p'], ch['dom'], _TN))
            for c0, c1 in zip(chains[0::2], chains[1::2]):
                cur, prev = c0['cur'], c0['prev']
                dk, dv = c0['dk'] + c1['dk'], c0['dv'] + c1['dv']
                dq_ref[cur, :] += c0['dq'] + c1['dq']
                if prev is None:
                    dk_ref[cur, :] += dk
                    dv_ref[cur, :] += dv
                else:
                    dk_ref[prev, :] += dk[:ATTN_BLOCK]
                    dv_ref[prev, :] += dv[:ATTN_BLOCK]
                    dk_ref[cur, :] += dk[ATTN_BLOCK:]
                    dv_ref[cur, :] += dv[ATTN_BLOCK:]
        dq, dk = dq_ref[...], dk_ref[...]
        dq_out[...] = (dq * c + _rot(dq * s)).astype(dq_out.dtype)
        dk_out[...] = (dk * c + _rot(dk * s)).astype(dk_out.dtype)
        dv_out[...] = dv_ref[...].astype(dv_out.dtype)

    shp = jax.ShapeDtypeStruct((t, D_ATTN), BF16)
    return pl.pallas_call(
        body, name="attn_bwd", grid=(b, N_HEAD_PAIRS),
        in_specs=[col(0), col(N_HEAD_PAIRS), col(2 * N_HEAD_PAIRS), tab, tab, col(0), col(0), col(0), col(0), col(0),
                  pl.BlockSpec((LANES, LANES), lambda bb, hp: (0, 0))],
        out_specs=[col(0)] * 3, out_shape=[shp] * 3,
        scratch_shapes=[pltpu.VMEM((SEQ, LANES), F32)] * 11,
        compiler_params=_cparams(("parallel", "parallel")),
    )(qkvz, qkvz, qkvz, cosv, sinv, dmix, mixed, *lses, head_ones)


def attn_norm_fwd(mixed, norm_w):
    return rowwise("attn_norm", _rms, [mixed], [norm_w], [(mixed.shape[0], D_ATTN, BF16)])[0]


def attn_norm_bwd(dout, mixed, norm_w):
    def fn(dy, mx, w):
        _, vjp = jax.vjp(_rms, mx, w)
        return vjp(dy)

    return rowwise("attn_norm_bwd", fn, [dout, mixed], [norm_w], [(dout.shape[0], D_ATTN, F32)], accs=[(1, D_ATTN)])


CONV_TM = 256
HALO = 8


def conv_fwd(u, w, bias):
    t = u.shape[0]
    tm, per_seq = CONV_TM, SEQ // CONV_TM

    def body(u_ref, h_ref, w_ref, b_ref, xs_ref, bm_ref, cm_ref, scr):
        first = pl.program_id(0) % per_seq == 0
        scr[0:HALO, :] = jnp.where(first, 0.0, h_ref[...])
        scr[HALO:, :] = u_ref[...]
        acc = b_ref[...]
        for k in range(CONV_WIDTH):
            acc = acc + w_ref[k:k + 1, :] * scr[pl.ds(HALO - CONV_WIDTH + 1 + k, tm), :]
        y = _silu(acc)
        xs_ref[...] = y[:, :D_SSD]
        bm_ref[...] = y[:, D_SSD:D_SSD + D_BC]
        cm_ref[...] = y[:, D_SSD + D_BC:]

    return pl.pallas_call(
        body, name="conv_fwd", grid=(t // tm,),
        in_specs=[pl.BlockSpec((tm, D_CONV), lambda i: (i, 0)),
                  pl.BlockSpec((HALO, D_CONV), lambda i: (jnp.maximum(i * (tm // HALO) - 1, 0), 0)),
                  pl.BlockSpec((CONV_WIDTH, D_CONV), lambda i: (0, 0)), pl.BlockSpec((1, D_CONV), lambda i: (0, 0))],
        out_specs=[pl.BlockSpec((tm, D_SSD), lambda i: (i, 0)), pl.BlockSpec((tm, D_BC), lambda i: (i, 0)),
                   pl.BlockSpec((tm, D_BC), lambda i: (i, 0))],
        out_shape=[jax.ShapeDtypeStruct((t, D_SSD), F32), jax.ShapeDtypeStruct((t, D_BC), F32),
                   jax.ShapeDtypeStruct((t, D_BC), F32)],
        scratch_shapes=[pltpu.VMEM((tm + HALO, D_CONV), F32)],
        compiler_params=_cparams(("parallel",)),
    )(u, u, w, bias)


def conv_bwd(u, w, bias, dxs_a, dxs_b, dbm, dcm):
    t = u.shape[0]
    tm, per_seq = CONV_TM, SEQ // CONV_TM
    n_tiles = t // tm

    def body1(u_ref, h_ref, dxs_ref, dxs2_ref, dbm_ref, dcm_ref, w_ref, b_ref, dz_ref, dw_ref, db_ref, scr):
        i = pl.program_id(0)
        first = i % per_seq == 0
        scr[0:HALO, :] = jnp.where(first, 0.0, h_ref[...])
        scr[HALO:, :] = u_ref[...]
        acc = b_ref[...]
        for k in range(CONV_WIDTH):
            acc = acc + w_ref[k:k + 1, :] * scr[pl.ds(HALO - CONV_WIDTH + 1 + k, tm), :]
        sig = _sigmoid(acc)
        dy = jnp.concatenate([dxs_ref[...] + dxs2_ref[...], dbm_ref[...], dcm_ref[...]], axis=1)
        dz = dy * sig * (1.0 + acc * (1.0 - sig))
        dz_ref[...] = dz

        @pl.when(i == 0)
        def _():
            dw_ref[...] = jnp.zeros(dw_ref.shape, F32)
            db_ref[...] = jnp.zeros(db_ref.shape, F32)
        db_ref[...] += jnp.sum(dz, axis=0, keepdims=True)
        for k in range(CONV_WIDTH):
            dw_ref[k:k + 1, :] += jnp.sum(dz * scr[pl.ds(HALO - CONV_WIDTH + 1 + k, tm), :], axis=0, keepdims=True)

    dz, dw, db = pl.pallas_call(
        body1, name="conv_bwd_dz", grid=(n_tiles,),
        in_specs=[pl.BlockSpec((tm, D_CONV), lambda i: (i, 0)),
                  pl.BlockSpec((HALO, D_CONV), lambda i: (jnp.maximum(i * (tm // HALO) - 1, 0), 0)),
                  pl.BlockSpec((tm, D_SSD), lambda i: (i, 0)), pl.BlockSpec((tm, D_SSD), lambda i: (i, 0)),
                  pl.BlockSpec((tm, D_BC), lambda i: (i, 0)), pl.BlockSpec((tm, D_BC), lambda i: (i, 0)),
                  pl.BlockSpec((CONV_WIDTH, D_CONV), lambda i: (0, 0)), pl.BlockSpec((1, D_CONV), lambda i: (0, 0))],
        out_specs=[pl.BlockSpec((tm, D_CONV), lambda i: (i, 0)), pl.BlockSpec((CONV_WIDTH, D_CONV), lambda i: (0, 0)),
                   pl.BlockSpec((1, D_CONV), lambda i: (0, 0))],
        out_shape=[jax.ShapeDtypeStruct((t, D_CONV), F32), jax.ShapeDtypeStruct((CONV_WIDTH, D_CONV), F32),
                   jax.ShapeDtypeStruct((1, D_CONV), F32)],
        scratch_shapes=[pltpu.VMEM((tm + HALO, D_CONV), F32)],
        compiler_params=_cparams(("arbitrary",)),
    )(u, u, dxs_a, dxs_b, dbm, dcm, w, bias)

    def body2(dz_ref, n_ref, w_ref, du_ref, scr):
        last = pl.program_id(0) % per_seq == per_seq - 1
        scr[0:tm, :] = dz_ref[...]
        scr[tm:, :] = jnp.where(last, 0.0, n_ref[...])
        acc = jnp.zeros((tm, D_CONV), F32)
        for k in range(CONV_WIDTH):
            acc = acc + w_ref[k:k + 1, :] * scr[pl.ds(CONV_WIDTH - 1 - k, tm), :]
        du_ref[...] = acc.astype(du_ref.dtype)

    du = pl.pallas_call(
        body2, name="conv_bwd_du", grid=(n_tiles,),
        in_specs=[pl.BlockSpec((tm, D_CONV), lambda i: (i, 0)),
                  pl.BlockSpec((HALO, D_CONV), lambda i: (jnp.minimum((i + 1) * (tm // HALO), t // HALO - 1), 0)),
                  pl.BlockSpec((CONV_WIDTH, D_CONV), lambda i: (0, 0))],
        out_specs=pl.BlockSpec((tm, D_CONV), lambda i: (i, 0)),
        out_shape=jax.ShapeDtypeStruct((t, D_CONV), BF16),
        scratch_shapes=[pltpu.VMEM((tm + HALO, D_CONV), F32)],
        compiler_params=_cparams(("parallel",)),
    )(dz, dz, w)
    return du, dw, db


Q = SSD_CHUNK
N_PAIRS = D_SSD // LANES
HEADS_PER_GROUP = N_HEADS // SSD_GROUPS


def _head_expand():
    e = np.zeros((LANES, N_HEADS * LANES), np.float32)
    for j in range(N_HEADS):
        e[j, j * LANES:(j + 1) * LANES] = 1.0
    return jnp.asarray(e)


def _pad_lanes(v, fill=0.0):
    row = jnp.pad(v.reshape(1, -1).astype(F32), ((0, 0), (0, LANES - v.size)), constant_values=fill)
    return row, row.reshape(LANES, 1)


def _ssd_common(dtr_ref, dtrt_ref, bias_r, bias_c, alog_r, alog_c, e_ref):
    row = lax.broadcasted_iota(jnp.int32, (Q, Q), 0)
    col = lax.broadcasted_iota(jnp.int32, (Q, Q), 1)
    tril = row >= col
    lane = lax.broadcasted_iota(jnp.int32, (1, LANES), 1)
    a_r = jnp.where(lane < N_HEADS, -jnp.exp(alog_r[...]), 0.0)
    sub = lax.broadcasted_iota(jnp.int32, (LANES, 1), 0)
    a_c = jnp.where(sub < N_HEADS, -jnp.exp(alog_c[...]), 0.0)
    dt = _softplus(dtr_ref[...] + bias_r[...])
    cs = _dot(tril.astype(F32), dt * a_r, _NN, HI)
    dt_rep = _dot(dt, e_ref[...], _NN, HI)
    cs_rep = _dot(cs, e_ref[...], _NN, HI)
    dtt = _softplus(dtrt_ref[...] + bias_c[...])
    cst = _dot(dtt * a_c, (row <= col).astype(F32), _NN, HI)
    return tril, lane, a_r, dt, dt_rep, cs_rep, cst


def _ssd_specs(b, nc, rev):
    ci = (lambda c: nc - 1 - c) if rev else (lambda c: c)
    rows = lambda w: pl.BlockSpec((Q, w), lambda bb, c: (bb * nc + ci(c), 0))
    dtt = pl.BlockSpec((LANES, Q), lambda bb, c: (0, bb * nc + ci(c)))
    const = lambda s: pl.BlockSpec(s, lambda bb, c: (0,) * len(s))
    state = pl.BlockSpec((None, N_PAIRS, LANES, SSD_STATE), lambda bb, c: (bb * nc + ci(c), 0, 0, 0))
    return rows, dtt, const, state


def ssd_fwd(xs, bm, cm, dtraw, dt_bias, a_log, b):
    t = xs.shape[0]
    nc = SEQ // Q
    rows, dtt_spec, const, state = _ssd_specs(b, nc, False)
    bias_r, bias_c = _pad_lanes(dt_bias)
    alog_r, alog_c = _pad_lanes(a_log)

    def body(xs_ref, b_ref, c_ref, dtr_ref, dtrt_ref, br, bc, ar, ac, e_ref, y_ref, hp_ref, h_scr):
        @pl.when(pl.program_id(1) == 0)
        def _():
            h_scr[...] = jnp.zeros(h_scr.shape, F32)
        tril, lane, _, _, dt_rep, cs_rep, cst = _ssd_common(dtr_ref, dtrt_ref, br, bc, ar, ac, e_ref)
        sub = lax.broadcasted_iota(jnp.int32, (LANES, 1), 0)
        y_acc = [jnp.zeros((Q, LANES), F32) for _ in range(N_PAIRS)]
        h_old = [h_scr[p] for p in range(N_PAIRS)]
        h_new = [jnp.zeros((LANES, SSD_STATE), F32) for _ in range(N_PAIRS)]
        for g in range(SSD_GROUPS):
            bg = b_ref[:, g * SSD_STATE:(g + 1) * SSD_STATE].astype(BF16)
            cg = c_ref[:, g * SSD_STATE:(g + 1) * SSD_STATE].astype(BF16)
            cb = _dot(cg, bg, _NT)
            for j in range(g * HEADS_PER_GROUP, (g + 1) * HEADS_PER_GROUP):
                p, side = j // 2, j % 2
                m = (lane < HEAD_DIM) if side == 0 else (lane >= HEAD_DIM)
                ms = (sub < HEAD_DIM) if side == 0 else (sub >= HEAD_DIM)
                csj = cs_rep[:, j * LANES:(j + 1) * LANES]
                dtj = dt_rep[:, j * LANES:(j + 1) * LANES]
                lmat = jnp.exp(jnp.where(tril, csj - cst[j:j + 1, :], NEG))
                xdt = jnp.where(m, xs_ref[:, p * LANES:(p + 1) * LANES] * dtj, 0.0)
                hm = jnp.where(ms, h_old[p], 0.0)
                ydiag = _dot((cb * lmat).astype(BF16), xdt.astype(BF16), _NN)
                yoff = jnp.exp(csj) * _dot(cg, hm.astype(BF16), _NT)
                y_acc[p] = y_acc[p] + ydiag + yoff
                last = csj[Q - 1:Q, :]
                sj = _dot((xdt * jnp.exp(last - csj)).astype(BF16), bg, _TN)
                h_new[p] = h_new[p] + jnp.exp(last) * hm + sj
        for p in range(N_PAIRS):
            y_ref[:, p * LANES:(p + 1) * LANES] = y_acc[p]
            hp_ref[p] = h_old[p]
            h_scr[p] = h_new[p]

    return pl.pallas_call(
        body, name="ssd_fwd", grid=(b, nc),
        in_specs=[rows(D_SSD), rows(D_BC), rows(D_BC), rows(LANES), dtt_spec, const((1, LANES)), const((LANES, 1)),
                  const((1, LANES)), const((LANES, 1)), const((LANES, N_HEADS * LANES))],
        out_specs=[rows(D_SSD), state],
        out_shape=[jax.ShapeDtypeStruct((t, D_SSD), F32),
                   jax.ShapeDtypeStruct((b * nc, N_PAIRS, LANES, SSD_STATE), F32)],
        scratch_shapes=[pltpu.VMEM((N_PAIRS, LANES, SSD_STATE), F32)],
        compiler_params=_cparams(("parallel", "arbitrary")),
    )(xs, bm, cm, dtraw, dtraw.T, bias_r, bias_c, alog_r, alog_c, _head_expand())


def ssd_bwd(xs, bm, cm, dtraw, dt_bias, a_log, hprev, dy, b):
    t = xs.shape[0]
    nc = SEQ // Q
    rows, dtt_spec, const, state = _ssd_specs(b, nc, True)
    bias_r, bias_c = _pad_lanes(dt_bias)
    alog_r, alog_c = _pad_lanes(a_log)

    def body(xs_ref, b_ref, c_ref, dtr_ref, dtrt_ref, hp_ref, dy_ref, br, bc, ar, ac, e_ref,
             dxs_ref, db_ref, dc_ref, ddt_ref, dbias_ref, dalog_ref, dh_scr):
        first = jnp.logical_and(pl.program_id(0) == 0, pl.program_id(1) == 0)

        @pl.when(pl.program_id(1) == 0)
        def _():
            dh_scr[...] = jnp.zeros(dh_scr.shape, F32)

        @pl.when(first)
        def _():
            dbias_ref[...] = jnp.zeros(dbias_ref.shape, F32)
            dalog_ref[...] = jnp.zeros(dalog_ref.shape, F32)
        tril, lane, a_r, dt, dt_rep, cs_rep, cst = _ssd_common(dtr_ref, dtrt_ref, br, bc, ar, ac, e_ref)
        sub = lax.broadcasted_iota(jnp.int32, (LANES, 1), 0)
        rowq = lax.broadcasted_iota(jnp.int32, (Q, 1), 0)
        ones = jnp.ones((Q, LANES), F32)
        triu = (lax.broadcasted_iota(jnp.int32, (Q, Q), 0) <= lax.broadcasted_iota(jnp.int32, (Q, Q), 1)).astype(F32)
        dxs_acc = [jnp.zeros((Q, LANES), F32) for _ in range(N_PAIRS)]
        dh_in = [dh_scr[p] for p in range(N_PAIRS)]
        h_in = [hp_ref[p] for p in range(N_PAIRS)]
        dh_out = [jnp.zeros((LANES, SSD_STATE), F32) for _ in range(N_PAIRS)]
        ddt = jnp.zeros((Q, LANES), F32)
        dalog = jnp.zeros((1, LANES), F32)
        for g in range(SSD_GROUPS):
            gs = slice(g * SSD_STATE, (g + 1) * SSD_STATE)
            bg, cg = b_ref[:, gs].astype(BF16), c_ref[:, gs].astype(BF16)
            cb = _dot(cg, bg, _NT)
            dcb = jnp.zeros((Q, Q), F32)
            dbg = jnp.zeros((Q, SSD_STATE), F32)
            dcg = jnp.zeros((Q, SSD_STATE), F32)
            for j in range(g * HEADS_PER_GROUP, (g + 1) * HEADS_PER_GROUP):
                p, side = j // 2, j % 2
                m = (lane < HEAD_DIM) if side == 0 else (lane >= HEAD_DIM)
                ms = (sub < HEAD_DIM) if side == 0 else (sub >= HEAD_DIM)
                csj = cs_rep[:, j * LANES:(j + 1) * LANES]
                dtj = dt_rep[:, j * LANES:(j + 1) * LANES]
                lmat = jnp.exp(jnp.where(tril, csj - cst[j:j + 1, :], NEG))
                x2 = jnp.where(m, xs_ref[:, p * LANES:(p + 1) * LANES], 0.0)
                xdt = x2 * dtj
                dym = jnp.where(m, dy_ref[:, p * LANES:(p + 1) * LANES], 0.0)
                hm = jnp.where(ms, h_in[p], 0.0)
                dhm = jnp.where(ms, dh_in[p], 0.0)
                ecs = jnp.exp(csj)
                last = csj[Q - 1:Q, :]
                decay = jnp.exp(last - csj)
                el = jnp.exp(last)
                gmat = cb * lmat
                dymb, xdtb = dym.astype(BF16), xdt.astype(BF16)
                dg = _dot(dymb, xdtb, _NT)
                dxdt = _dot(gmat.astype(BF16), dymb, _TN)
                dcb = dcb + dg * lmat
                ej = dg * gmat
                dcs = _dot(ej, ones, _NN, HI) - _dot(ej, ones, _TN, HI)
                ch = _dot(cg, hm.astype(BF16), _NT)
                dye = dym * ecs
                dcs = dcs + jnp.sum(dye * ch, axis=1, keepdims=True)
                dcg = dcg + _dot(dye.astype(BF16), hm.astype(BF16), _NN)
                dhp = _dot(dye.astype(BF16), cg, _TN)
                wmat = _dot(bg, dhm.astype(BF16), _NT)
                xd = xdt * decay
                dxdt = dxdt + decay * wmat
                ddl = jnp.sum(xd * wmat, axis=1, keepdims=True)
                dlast = jnp.sum(ddl, axis=0, keepdims=True) + el * jnp.sum(jnp.sum(dhm * hm, axis=1, keepdims=True), axis=0, keepdims=True)
                dcs = dcs - ddl + jnp.where(rowq == Q - 1, dlast, 0.0)
                dbg = dbg + _dot(xd.astype(BF16), dhm.astype(BF16), _NN)
                dh_out[p] = dh_out[p] + el * dhm + dhp
                da = _dot(triu, dcs, _NN, HI)
                aj = jnp.sum(jnp.where(lane == j, a_r, 0.0), axis=1, keepdims=True)
                ddtj = da * aj + jnp.sum(dxdt * x2, axis=1, keepdims=True)
                ddt = ddt + jnp.where(lane == j, ddtj, 0.0)
                dalog = dalog + jnp.where(lane == j, jnp.sum(da * dtj, axis=0, keepdims=True) * aj, 0.0)
                dxs_acc[p] = dxs_acc[p] + dxdt * dtj
            dcbb = dcb.astype(BF16)
            dc_ref[:, gs] = dcg + _dot(dcbb, bg, _NN)
            db_ref[:, gs] = dbg + _dot(dcbb, cg, _TN)
        for p in range(N_PAIRS):
            dxs_ref[:, p * LANES:(p + 1) * LANES] = dxs_acc[p]
            dh_scr[p] = dh_out[p]
        ddtraw = ddt * _sigmoid(dtr_ref[...] + br[...])
        ddt_ref[...] = ddtraw
        dbias_ref[...] += jnp.sum(ddtraw, axis=0, keepdims=True)
        dalog_ref[...] += dalog

    return pl.pallas_call(
        body, name="ssd_bwd", grid=(b, nc),
        in_specs=[rows(D_SSD), rows(D_BC), rows(D_BC), rows(LANES), dtt_spec, state, rows(D_SSD), const((1, LANES)),
                  const((LANES, 1)), const((1, LANES)), const((LANES, 1)), const((LANES, N_HEADS * LANES))],
        out_specs=[rows(D_SSD), rows(D_BC), rows(D_BC), rows(LANES), const((1, LANES)), const((1, LANES))],
        out_shape=[jax.ShapeDtypeStruct((t, D_SSD), F32), jax.ShapeDtypeStruct((t, D_BC), F32),
                   jax.ShapeDtypeStruct((t, D_BC), F32), jax.ShapeDtypeStruct((t, LANES), F32),
                   jax.ShapeDtypeStruct((1, LANES), F32), jax.ShapeDtypeStruct((1, LANES), F32)],
        scratch_shapes=[pltpu.VMEM((N_PAIRS, LANES, SSD_STATE), F32)],
        compiler_params=_cparams(("arbitrary", "arbitrary")),
    )(xs, bm, cm, dtraw, dtraw.T, hprev, dy, bias_r, bias_c, alog_r, alog_c, _head_expand())


def _split_w_in(w_in):
    w_dt = jnp.pad(w_in[:, D_QKVZ + D_CONV:], ((0, 0), (0, LANES - N_HEADS)))
    return w_in[:, :D_QKVZ], w_in[:, D_QKVZ:D_QKVZ + D_CONV], w_dt


def mixer_fwd(hb, p, cosv, sinv, b):
    t = hb.shape[0]
    w_a, w_b, w_c = _split_w_in(p['w_in'])
    qkvz = mm("in_qkvz", [(hb, w_a, 'nn')], D_QKVZ)
    xbc = mm("in_xbc", [(hb, w_b, 'nn')], D_CONV)
    dtraw = mm("in_dt", [(hb, w_c, 'nn')], LANES)
    mixed, *lses = attn_fwd(qkvz, cosv, sinv, b)
    attn = attn_norm_fwd(mixed, p['attn_norm_w'])
    xs, bm, cm = conv_fwd(xbc, p['conv_w'], p['conv_b'])
    y, hprev = ssd_fwd(xs, bm, cm, dtraw, p['dt_bias'], p['a_log'], b)
    dskip = jnp.repeat(p['d_skip'].reshape(-1), HEAD_DIM).reshape(1, D_SSD)
    yg, = rowwise("ssd_gate", _gate, [y, xs, Op(qkvz, D_SSD, 3)], [dskip, p['ssd_norm_w']], [(t, D_SSD, BF16)])
    mix = mm("out_proj", [(attn, p['w_out'][:D_ATTN], 'nn'), (yg, p['w_out'][D_ATTN:], 'nn')], D_MODEL)
    res = dict(hb=hb, qkvz=qkvz, xbc=xbc, dtraw=dtraw, mixed=mixed, lses=lses, attn=attn, xs=xs, bm=bm, cm=cm,
               y=y, hprev=hprev, dskip=dskip, yg=yg, cosv=cosv, sinv=sinv)
    return mix, res


def mixer_bwd(r, p, dmix, dh_resid, b):
    t = dmix.shape[0]
    w_a, w_b, w_c = _split_w_in(p['w_in'])
    w_out = p['w_out']
    dattn = mm("out_bwd_dattn", [(dmix, w_out[:D_ATTN], 'nt')], D_ATTN)
    dyg = mm("out_bwd_dyg", [(dmix, w_out[D_ATTN:], 'nt')], D_SSD)
    dw_out = jnp.concatenate([mm_tn("out_bwd_dw_a", r['attn'], dmix, BF16),
                              mm_tn("out_bwd_dw_y", r['yg'], dmix, BF16)], axis=0)

    def gate_bwd(dy_, y_, xs_, z_, ds_, w_):
        _, vjp = jax.vjp(_gate, y_, xs_, z_, ds_, w_)
        return vjp(dy_)

    dy, dxs_a, dz, ddskip, dssd_norm = rowwise(
        "ssd_gate_bwd", gate_bwd, [dyg, r['y'], r['xs'], Op(r['qkvz'], D_SSD, 3)], [r['dskip'], p['ssd_norm_w']],
        [(t, D_SSD, F32), (t, D_SSD, F32), (t, D_SSD, BF16)], accs=[(1, D_SSD), (1, D_SSD)])
    dxs_b, dbm, dcm, ddtraw, ddt_bias, da_log = ssd_bwd(r['xs'], r['bm'], r['cm'], r['dtraw'], p['dt_bias'], p['a_log'],
                                                        r['hprev'], dy, b)
    dxbc, dconv_w, dconv_b = conv_bwd(r['xbc'], p['conv_w'], p['conv_b'], dxs_a, dxs_b, dbm, dcm)
    dmixed, dattn_norm = attn_norm_bwd(dattn, r['mixed'], p['attn_norm_w'])
    dq, dk, dv = attn_bwd(r['qkvz'], r['cosv'], r['sinv'], dmixed, r['mixed'], r['lses'], b)
    wq, wk, wv, wz = (w_a[:, i * D_ATTN:(i + 1) * D_ATTN] for i in range(4))
    dh = mm("in_bwd_dh", [(dq, wq, 'nt'), (dk, wk, 'nt'), (dv, wv, 'nt'), (dz, wz, 'nt'), (dxbc, w_b, 'nt'),
                          (ddtraw, w_c, 'nt')], D_MODEL, add=dh_resid, tn=512)
    h = r['hb']
    dw_in = jnp.concatenate([mm_tn("in_bwd_dwq", h, dq, BF16), mm_tn("in_bwd_dwk", h, dk, BF16),
                             mm_tn("in_bwd_dwv", h, dv, BF16), mm_tn("in_bwd_dwz", h, dz, BF16),
                             mm_tn("in_bwd_dwx", h, dxbc, BF16), mm_tn("in_bwd_dwdt", h, ddtraw, BF16)[:, :N_HEADS]], axis=1)
    head_sum = lambda v: v.reshape(N_HEADS, HEAD_DIM).sum(axis=1).reshape(1, N_HEADS)
    grads = dict(w_in=dw_in, w_out=dw_out, conv_w=dconv_w, conv_b=dconv_b, dt_bias=ddt_bias[:, :N_HEADS],
                 a_log=da_log[:, :N_HEADS], d_skip=head_sum(ddskip), attn_norm_w=dattn_norm, ssd_norm_w=dssd_norm)
    return dh, grads


FFN_COL = ('ffn1_gate', 'ffn1_up', 'ffn2_gate', 'ffn2_up')
FFN_ROW = ('ffn1_down', 'ffn2_down')
CONV_W_COMM = (8, 2 * LANES)
SMALL = 'small'


def comm_shape(k, shapes):
    if k in FFN_COL:
        return (D_MODEL, FF_PAD)
    if k in FFN_ROW:
        return (FF_PAD, D_MODEL)
    if k == 'conv_w':
        return CONV_W_COMM
    if k == SMALL:
        n = sum(int(np.prod(shapes[r])) for r in REPLICATED)
        return (-(-n // (8 * LANES)) * 8, LANES)
    return tuple(shapes[k][1:])


def to_comm(k, vals, shapes):
    if k == SMALL:
        flat = jnp.concatenate([vals[r].reshape(-1) for r in REPLICATED])
        r_, c_ = comm_shape(k, shapes)
        return jnp.pad(flat, (0, r_ * c_ - flat.size)).reshape(r_, c_)
    a = vals[k].reshape(shapes[k][1:])
    r_, c_ = comm_shape(k, shapes)
    return jnp.pad(a, ((0, r_ - a.shape[0]), (0, c_ - a.shape[1])))


def from_comm(k, a, shapes):
    if k == SMALL:
        flat, out, off = a.reshape(-1), {}, 0
        for r in REPLICATED:
            n = int(np.prod(shapes[r]))
            out[r] = flat[off:off + n].reshape(shapes[r])
            off += n
        return out
    shp = shapes[k][1:]
    return {k: a[:shp[0], :shp[1]].reshape(shapes[k])}


def full_weight(k, g):
    if k in FFN_COL:
        return jnp.concatenate([g[p] for p in range(N_DEV)], axis=1)
    if k == 'conv_w':
        return jnp.transpose(g[:, :CONV_WIDTH, :D_CONV // N_DEV], (1, 0, 2)).reshape(CONV_WIDTH, D_CONV)
    return g.reshape(N_DEV * g.shape[1], g.shape[2])


def grad_shards(k, g):
    if k in FFN_COL:
        return jnp.stack([g[:, p * FF_PAD:(p + 1) * FF_PAD] for p in range(N_DEV)])
    if k == 'conv_w':
        s = jnp.transpose(g.reshape(CONV_WIDTH, N_DEV, D_CONV // N_DEV), (1, 0, 2))
        return jnp.pad(s, ((0, 0), (0, CONV_W_COMM[0] - CONV_WIDTH), (0, CONV_W_COMM[1] - D_CONV // N_DEV)))
    return g.reshape(N_DEV, g.shape[0] // N_DEV, g.shape[1])


def _flip(v, bit):
    return 1 - v if bit else v


N_PEER_COPIES = N_DEV - 1


def _comm_call(name, body, arrs, out_shape):
    n = len(arrs)
    return pl.pallas_call(
        functools.partial(body, n), name=name, out_shape=out_shape,
        in_specs=[pl.BlockSpec(memory_space=pl.ANY)] * n, out_specs=[pl.BlockSpec(memory_space=pl.ANY)] * n,
        scratch_shapes=[pltpu.SemaphoreType.DMA((n * N_PEER_COPIES,)), pltpu.SemaphoreType.DMA((n * N_PEER_COPIES,)),
                        pltpu.SemaphoreType.DMA((n,))],
    )(*arrs)


def all_gather(arrs):
    def body(n, *refs):
        x_refs, out_refs, (send_sems, recv_sems, local_sems) = refs[:n], refs[n:2 * n], refs[2 * n:]
        x, y, c = lax.axis_index("x"), lax.axis_index("y"), lax.axis_index("c")
        me, sibling = (x, y, c), (x, y, 1 - c)
        chips = [(1 - x, y), (x, 1 - y), (1 - x, 1 - y)]

        def copy(a, k, block, to, src=None):
            px, py, pc = block
            dst = out_refs[a].at[4 * px + 2 * py + pc]
            return pltpu.make_async_remote_copy(
                src_ref=dst if src is None else src, dst_ref=dst, send_sem=send_sems.at[a * N_PEER_COPIES + k],
                recv_sem=recv_sems.at[a * N_PEER_COPIES + k], device_id=to, device_id_type=MESH)

        mine = [pltpu.make_async_copy(x_refs[a], out_refs[a].at[4 * x + 2 * y + c], local_sems.at[a]) for a in range(n)]
        started = []
        for a in range(n):
            mine[a].start()
            first = [copy(a, 0, me, sibling, src=x_refs[a])]
            first += [copy(a, 1 + j, me, (*chip, c), src=x_refs[a]) for j, chip in enumerate(chips)]
            for cp in first:
                cp.start()
            started += first
        for j, chip in enumerate(chips):
            for a in range(n):
                copy(a, 1 + j, (*chip, c), me).wait_recv()
                cp = copy(a, 4 + j, (*chip, c), sibling)
                cp.start()
                started.append(cp)
        for a in range(n):
            copy(a, 0, sibling, me).wait_recv()
            for j, chip in enumerate(chips):
                copy(a, 4 + j, (*chip, 1 - c), me).wait_recv()
        for cp in started:
            cp.wait_send()
        for cp in mine:
            cp.wait()

    return _comm_call("all_gather_weights", body, arrs,
                      [jax.ShapeDtypeStruct((N_DEV,) + a.shape, a.dtype) for a in arrs])


def all_to_all(arrs):
    def body(n, *refs):
        s_refs, r_refs, (send_sems, recv_sems, local_sems) = refs[:n], refs[n:2 * n], refs[2 * n:]
        x, y, c = lax.axis_index("x"), lax.axis_index("y"), lax.axis_index("c")
        me = 4 * x + 2 * y + c

        def peer(k):
            return _flip(x, k & 4), _flip(y, k & 2), _flip(c, k & 1)

        def copy(a, k, landing):
            px, py, pc = peer(k)
            p = 4 * px + 2 * py + pc
            src, dst = (s_refs[a].at[me], r_refs[a].at[p]) if landing else (s_refs[a].at[p], r_refs[a].at[me])
            return pltpu.make_async_remote_copy(
                src_ref=src, dst_ref=dst, send_sem=send_sems.at[a * N_PEER_COPIES + k - 1],
                recv_sem=recv_sems.at[a * N_PEER_COPIES + k - 1], device_id=(px, py, pc), device_id_type=MESH)

        mine = [pltpu.make_async_copy(s_refs[a].at[me], r_refs[a].at[me], local_sems.at[a]) for a in range(n)]
        sends = [copy(a, k, False) for a in range(n) for k in range(1, N_DEV)]
        for cp in mine + sends:
            cp.start()
        for a in range(n):
            for k in range(1, N_DEV):
                copy(a, k, True).wait_recv()
        for cp in sends:
            cp.wait_send()
        for cp in mine:
            cp.wait()

    return _comm_call("all_to_all_grads", body, arrs, [jax.ShapeDtypeStruct(a.shape, a.dtype) for a in arrs])


def adamw(name, recv, w, m, v, tm):
    rows, cols = w.shape
    c1 = 1.0 / (1.0 - ADAM_B1 ** ADAM_STEP)
    c2 = 1.0 / (1.0 - ADAM_B2 ** ADAM_STEP)

    def fn(*a):
        g = a[0]
        for s in range(1, N_DEV):
            g = g + a[s]
        w_, m_, v_ = a[N_DEV:]
        m_ = ADAM_B1 * m_ + (1.0 - ADAM_B1) * g
        v_ = ADAM_B2 * v_ + (1.0 - ADAM_B2) * jnp.square(g)
        delta = -ADAM_LR * ((m_ * c1) / (jnp.sqrt(v_ * c2) + ADAM_EPS) + ADAM_WD * w_)
        return g, delta, m_, v_

    flat = recv.reshape(N_DEV * rows, cols)
    ins = [Op(flat, cols, 0, s * (rows // tm)) for s in range(N_DEV)] + [w, m, v]
    return rowwise(name, fn, ins, [], [(rows, cols, F32)] * 4, tm=tm)


ADAMW_TM = {'ffn1_gate': 256, 'ffn1_up': 256, 'ffn1_down': 128, 'w_in': 32, 'conv_w': 8, 'w_out': 64,
            'ffn2_gate': 256, 'ffn2_up': 256, 'ffn2_down': 128}


def kernel(x, positions, ln1_g, ln1_b, ffn1_gate, ffn1_up, ffn1_down, w_in, conv_w, conv_b, dt_bias, a_log, d_skip, attn_norm_w, ssd_norm_w, w_out, ln2_g, ln2_b, ffn2_gate, ffn2_up, ffn2_down, ln3_g, ln3_b, loss_target, m_ln1_g, m_ln1_b, m_ffn1_gate, m_ffn1_up, m_ffn1_down, m_w_in, m_conv_w, m_conv_b, m_dt_bias, m_a_log, m_d_skip, m_attn_norm_w, m_ssd_norm_w, m_w_out, m_ln2_g, m_ln2_b, m_ffn2_gate, m_ffn2_up, m_ffn2_down, m_ln3_g, m_ln3_b, v_ln1_g, v_ln1_b, v_ffn1_gate, v_ffn1_up, v_ffn1_down, v_w_in, v_conv_w, v_conv_b, v_dt_bias, v_a_log, v_d_skip, v_attn_norm_w, v_ssd_norm_w, v_w_out, v_ln2_g, v_ln2_b, v_ffn2_gate, v_ffn2_up, v_ffn2_down, v_ln3_g, v_ln3_b):
    args = dict(locals())
    wl = {k: args[k] for k in WEIGHTS}
    ml = {k: args["m_" + k] for k in WEIGHTS}
    vl = {k: args["v_" + k] for k in WEIGHTS}
    shapes = {k: wl[k].shape for k in WEIGHTS}
    b, s, dm = x.shape
    t = b * s

    w_comm = {k: to_comm(k, wl, shapes) for k in SHARDED + (SMALL,)}
    gathered = all_gather([w_comm[k] if k == 'conv_w' else w_comm[k].astype(BF16) for k in SHARDED])
    p = {k: full_weight(k, g) for k, g in zip(SHARDED, gathered)}
    for k in REPLICATED:
        p[k] = wl[k].reshape(1, -1)

    x2 = x.reshape(t, dm)
    cosv, sinv = rope_tables(positions)
    f1, res1 = ffn_fwd("ffn1", x2, p['ffn1_gate'], p['ffn1_up'], p['ffn1_down'])
    h1, h1b = resid_ln_fwd("ln1", 0.5, x2, f1, p['ln1_g'], p['ln1_b'])
    mix, resm = mixer_fwd(h1b, p, cosv, sinv, b)
    h2, h2b = resid_ln_fwd("ln2", 1.0, h1, mix, p['ln2_g'], p['ln2_b'])
    f2, res3 = ffn_fwd("ffn2", h2b, p['ffn2_gate'], p['ffn2_up'], p['ffn2_down'])

    small, full = {}, {}
    dh2_res, df2, small['ln3_g'], small['ln3_b'], sq = ln_loss_bwd("ln3_loss_bwd", h2, f2, loss_target.reshape(t, dm),
                                                                   p['ln3_g'], p['ln3_b'])
    loss = lax.psum(jnp.sum(sq) * (0.5 / dm), AXES)

    dh2, full['ffn2_gate'], full['ffn2_up'], full['ffn2_down'] = ffn_bwd("ffn2", res3, p['ffn2_gate'], p['ffn2_up'],
                                                                       p['ffn2_down'], df2, dh2_res)
    dh1_res, dmix, small['ln2_g'], small['ln2_b'] = resid_ln_bwd("ln2_bwd", 1.0, h1, mix, p['ln2_g'], p['ln2_b'], dh2)
    dh1, gm = mixer_bwd(resm, p, dmix, dh1_res, b)
    for k in ('w_in', 'w_out', 'conv_w'):
        full[k] = gm[k]
    for k in ('conv_b', 'dt_bias', 'a_log', 'd_skip', 'attn_norm_w', 'ssd_norm_w'):
        small[k] = gm[k]
    dx_res, df1, small['ln1_g'], small['ln1_b'] = resid_ln_bwd("ln1_bwd", 0.5, x2, f1, p['ln1_g'], p['ln1_b'], dh1)
    dx, full['ffn1_gate'], full['ffn1_up'], full['ffn1_down'] = ffn_bwd("ffn1", res1, p['ffn1_gate'], p['ffn1_up'],
                                                                      p['ffn1_down'], df1, dx_res)

    keys = SHARDED + (SMALL,)
    small_part = to_comm(SMALL, small, shapes)
    send = [grad_shards(k, full[k]) for k in SHARDED] + [jnp.broadcast_to(small_part[None], (N_DEV,) + small_part.shape)]
    recv = all_to_all(send)
    outs = [{}, {}, {}, {}]
    for k, r in zip(keys, recv):
        tm = ADAMW_TM.get(k, r.shape[1])
        res = adamw(f"adamw_{k}", r, w_comm[k], to_comm(k, ml, shapes), to_comm(k, vl, shapes), tm)
        for o, a in zip(outs, res):
            o.update(from_comm(k, a, shapes))
    return (loss, dx.reshape(b, s, dm), *[o[k] for o in outs for k in WEIGHTS])
```

```python
import functools
import math

import jax
import jax.numpy as jnp
import numpy as np
from jax import lax
from jax.experimental import pallas as pl
from jax.experimental.pallas import tpu as pltpu

F32, BF16 = jnp.float32, jnp.bfloat16
HI = lax.Precision.HIGHEST
MESH = pl.DeviceIdType.MESH
AXES = ("x", "y", "c")
N_DEV = 8

D_MODEL = 1024
SEQ = 2048
HEAD_DIM = 64
N_HEADS = 12
D_ATTN = N_HEADS * HEAD_DIM
DILATIONS = (1, 4, 16)
ATTN_BLOCK = 128
ROPE_THETA = 500000.0
ROPE_DIM = 16
D_SSD = 768
SSD_GROUPS = 4
SSD_STATE = 128
SSD_CHUNK = 128
D_BC = SSD_GROUPS * SSD_STATE
D_CONV = D_SSD + 2 * D_BC
CONV_WIDTH = 4
D_QKVZ = 3 * D_ATTN + D_SSD
D_IN_PROJ = D_QKVZ + D_CONV + N_HEADS
D_FF = 2816
ALPHA = 2.0 ** 0.25
LN_EPS = 1e-5
RMS_EPS = 1e-6
ADAM_LR, ADAM_B1, ADAM_B2, ADAM_EPS, ADAM_WD, ADAM_STEP = 0.001, 0.9, 0.999, 1e-08, 0.01, 10

LANES = 128
VMEM_LIMIT = 52 * 1024 * 1024
NEG = -1e30

WEIGHTS = ['ln1_g', 'ln1_b', 'ffn1_gate', 'ffn1_up', 'ffn1_down', 'w_in', 'conv_w', 'conv_b', 'dt_bias', 'a_log',
           'd_skip', 'attn_norm_w', 'ssd_norm_w', 'w_out', 'ln2_g', 'ln2_b', 'ffn2_gate', 'ffn2_up', 'ffn2_down',
           'ln3_g', 'ln3_b']
COL_SHARDED = ('ffn1_gate', 'ffn1_up', 'conv_w', 'ffn2_gate', 'ffn2_up')
ROW_SHARDED = ('ffn1_down', 'w_in', 'w_out', 'ffn2_down')
SHARDED = tuple(n for n in WEIGHTS if n in COL_SHARDED or n in ROW_SHARDED)
REPLICATED = tuple(n for n in WEIGHTS if n not in SHARDED)
FF_SHARD = D_FF // N_DEV
FF_PAD = -(-FF_SHARD // LANES) * LANES
D_FF_INT = N_DEV * FF_PAD


def _cparams(sem=None):
    return pltpu.CompilerParams(dimension_semantics=sem, vmem_limit_bytes=VMEM_LIMIT)


def _tile(n, prefs):
    for p in prefs:
        if n % p == 0:
            return p
    return n


class Op:
    def __init__(self, arr, bw=None, cb=0, ro=0):
        self.arr, self.bw, self.cb, self.ro = arr, (arr.shape[1] if bw is None else bw), cb, ro


def _op(a):
    return a if isinstance(a, Op) else Op(a)


def rowwise(name, fn, ins, consts, outs, accs=(), tm=256):
    ins = [_op(a) for a in ins]
    rows = outs[0][0]
    n_in, n_c, n_o, n_a = len(ins), len(consts), len(outs), len(accs)
    tm = min(tm, rows)
    assert rows % tm == 0, (name, rows, tm)

    def body(*refs):
        vals = [r[...].astype(F32) for r in refs[:n_in + n_c]]
        res = fn(*vals)
        res = res if isinstance(res, (tuple, list)) else (res,)
        o_refs = refs[n_in + n_c:n_in + n_c + n_o]
        a_refs = refs[n_in + n_c + n_o:]
        for r, v in zip(o_refs, res[:n_o]):
            r[...] = v.astype(r.dtype)
        if n_a:
            @pl.when(pl.program_id(0) == 0)
            def _():
                for r in a_refs:
                    r[...] = jnp.zeros(r.shape, r.dtype)
            for r, v in zip(a_refs, res[n_o:]):
                r[...] += v

    in_specs = [pl.BlockSpec((tm, o.bw), functools.partial(lambda i, o: (i + o.ro, o.cb), o=o)) for o in ins]
    in_specs += [pl.BlockSpec(c.shape, functools.partial(lambda i, nd: (0,) * nd, nd=c.ndim)) for c in consts]
    out_specs = [pl.BlockSpec((tm, w), lambda i: (i, 0)) for (_, w, _) in outs]
    out_specs += [pl.BlockSpec(s, functools.partial(lambda i, nd: (0,) * nd, nd=len(s))) for s in accs]
    out_shape = [jax.ShapeDtypeStruct((r, w), dt) for (r, w, dt) in outs]
    out_shape += [jax.ShapeDtypeStruct(s, F32) for s in accs]
    res = pl.pallas_call(
        body, name=name, grid=(rows // tm,), in_specs=in_specs, out_specs=out_specs, out_shape=out_shape,
        compiler_params=_cparams(("arbitrary",) if n_a else ("parallel",)),
    )(*[o.arr for o in ins], *consts)
    return res


MM_TM = 512
MM_TN = (1024, 896, 768, 512, 256, 128)
_NT = (((1,), (1,)), ((), ()))
_NN = (((1,), (0,)), ((), ()))
_TN = (((0,), (0,)), ((), ()))


def _dot(a, b, dn, precision=None):
    return lax.dot_general(a, b, dn, preferred_element_type=F32, precision=precision)


def _mm_specs(name, pairs, n_out, tm, tn):
    in_specs, args = [], []
    for a, b, mode in pairs:
        o = _op(a)
        in_specs.append(pl.BlockSpec((tm, o.bw), functools.partial(lambda j, i, o: (i, o.cb), o=o)))
        args.append(o.arr)
        if mode == 'nn':
            assert b.shape == (o.bw, n_out), (name, b.shape, o.bw, n_out)
            in_specs.append(pl.BlockSpec((o.bw, tn), lambda j, i: (0, j)))
        else:
            assert b.shape == (n_out, o.bw), (name, b.shape, o.bw, n_out)
            in_specs.append(pl.BlockSpec((tn, o.bw), lambda j, i: (j, 0)))
        args.append(b)
    return in_specs, args


def _mm_acc(refs, pairs):
    acc = None
    for k, (_, _, mode) in enumerate(pairs):
        d = _dot(refs[2 * k][...].astype(BF16), refs[2 * k + 1][...].astype(BF16), _NN if mode == 'nn' else _NT)
        acc = d if acc is None else acc + d
    return acc


def mm(name, pairs, n_out, add=None, out_dtype=F32, tm=MM_TM, tn=None):
    m = _op(pairs[0][0]).arr.shape[0]
    tn = tn or _tile(n_out, MM_TN)
    n_p = len(pairs)

    def body(*refs):
        acc = _mm_acc(refs, pairs)
        if add is not None:
            acc = acc + refs[2 * n_p][...]
        refs[-1][...] = acc.astype(refs[-1].dtype)

    in_specs, args = _mm_specs(name, pairs, n_out, tm, tn)
    tile = pl.BlockSpec((tm, tn), lambda j, i: (i, j))
    if add is not None:
        in_specs.append(tile)
        args.append(add)
    return pl.pallas_call(
        body, name=name, grid=(n_out // tn, m // tm), in_specs=in_specs, out_specs=tile,
        out_shape=jax.ShapeDtypeStruct((m, n_out), out_dtype),
        compiler_params=_cparams(("parallel", "parallel")),
    )(*args)


def mm_tn(name, a, b, out_dtype=F32, tt=1024):
    a, b = _op(a), _op(b)
    t = a.arr.shape[0]
    k, n = a.bw, b.bw
    tk = _tile(k, (512, 896, 768, 256, 128))
    tn = _tile(n, MM_TN)
    tt = min(tt, t)
    n_t = t // tt

    def body(a_ref, b_ref, o_ref, acc_ref):
        s = pl.program_id(2)
        d = _dot(a_ref[...].astype(BF16), b_ref[...].astype(BF16), _TN)

        @pl.when(s == 0)
        def _():
            acc_ref[...] = d

        @pl.when(s > 0)
        def _():
            acc_ref[...] += d

        @pl.when(s == n_t - 1)
        def _():
            o_ref[...] = acc_ref[...].astype(o_ref.dtype)

    return pl.pallas_call(
        body, name=name, grid=(k // tk, n // tn, n_t),
        in_specs=[pl.BlockSpec((tt, tk), functools.partial(lambda kk, nn, s, o: (s, o.cb * (o.bw // tk) + kk), o=a)),
                  pl.BlockSpec((tt, tn), functools.partial(lambda kk, nn, s, o: (s, o.cb * (o.bw // tn) + nn), o=b))],
        out_specs=pl.BlockSpec((tk, tn), lambda kk, nn, s: (kk, nn)),
        out_shape=jax.ShapeDtypeStruct((k, n), out_dtype),
        scratch_shapes=[pltpu.VMEM((tk, tn), F32)],
        compiler_params=_cparams(("parallel", "parallel", "arbitrary")),
    )(a.arr, b.arr)


def _sigmoid(x):
    return 1.0 / (1.0 + jnp.exp(-x))


def _silu(x):
    return x * _sigmoid(x)


def _softplus(x):
    return jnp.maximum(x, 0.0) + jnp.log(1.0 + jnp.exp(-jnp.abs(x)))


def _act(g, u):
    return _silu(g) * u


def _resid_ln(scale, h, branch, g, b):
    r = ALPHA * h + scale * branch
    mu = jnp.mean(r, axis=-1, keepdims=True)
    var = jnp.mean(jnp.square(r - mu), axis=-1, keepdims=True)
    return (r - mu) * lax.rsqrt(var + LN_EPS) * g + b


def _rms(t, w):
    return t * lax.rsqrt(jnp.mean(t * t, axis=-1, keepdims=True) + RMS_EPS) * w


def _branch_weights(l1, l2, l3):
    m = jnp.maximum(jnp.maximum(l1, l2), l3)
    e1, e2, e3 = jnp.exp(l1 - m), jnp.exp(l2 - m), jnp.exp(l3 - m)
    inv = 1.0 / (e1 + e2 + e3)
    return e1 * inv, e2 * inv, e3 * inv


def _gate(y, xs, z, dskip, w):
    return _rms((y + dskip * xs) * _silu(z), w)


def _rot(x):
    d = lax.broadcasted_iota(jnp.int32, x.shape, 1) % HEAD_DIM
    up = pltpu.roll(x, x.shape[1] - ROPE_DIM // 2, 1)
    down = jnp.where(d < ROPE_DIM, pltpu.roll(x, ROPE_DIM // 2, 1), 0.0)
    return jnp.where(d < ROPE_DIM // 2, up, down)


def ffn_gate_up(name, h, wg, wu):
    m, nf = h.shape[0], wg.shape[1]
    tn = _tile(nf, MM_TN)

    def body(h_ref, g_w, u_w, g_ref, u_ref, a_ref):
        hb = h_ref[...].astype(BF16)
        g = _dot(hb, g_w[...].astype(BF16), _NN)
        u = _dot(hb, u_w[...].astype(BF16), _NN)
        g_ref[...] = g.astype(g_ref.dtype)
        u_ref[...] = u.astype(u_ref.dtype)
        a_ref[...] = _act(g, u).astype(a_ref.dtype)

    in_specs, args = _mm_specs(name, [(h, wg, 'nn')], nf, MM_TM, tn)
    in_specs.append(in_specs[1])
    tile = pl.BlockSpec((MM_TM, tn), lambda j, i: (i, j))
    return pl.pallas_call(
        body, name=name, grid=(nf // tn, m // MM_TM), in_specs=in_specs, out_specs=[tile] * 3,
        out_shape=[jax.ShapeDtypeStruct((m, nf), BF16)] * 3, compiler_params=_cparams(("parallel", "parallel")),
    )(*args, wu)


def ffn_da_act(name, df, wd, g, u):
    m, nf = df.shape[0], wd.shape[0]
    tn = _tile(nf, MM_TN)

    def body(df_ref, w_ref, g_ref, u_ref, dg_ref, du_ref):
        da = _dot(df_ref[...].astype(BF16), w_ref[...].astype(BF16), _NT)
        _, vjp = jax.vjp(_act, g_ref[...].astype(F32), u_ref[...].astype(F32))
        dg, du = vjp(da)
        dg_ref[...] = dg.astype(dg_ref.dtype)
        du_ref[...] = du.astype(du_ref.dtype)

    in_specs, args = _mm_specs(name, [(df, wd, 'nt')], nf, MM_TM, tn)
    tile = pl.BlockSpec((MM_TM, tn), lambda j, i: (i, j))
    return pl.pallas_call(
        body, name=name, grid=(nf // tn, m // MM_TM), in_specs=in_specs + [tile, tile], out_specs=[tile] * 2,
        out_shape=[jax.ShapeDtypeStruct((m, nf), BF16)] * 2, compiler_params=_cparams(("parallel", "parallel")),
    )(*args, g, u)


def resid_ln_fwd(name, scale, h, branch, ln_g, ln_b):
    t = h.shape[0]

    def fn(*a):
        y = _resid_ln(scale, *a)
        return y, y

    return rowwise(name, fn, [h, branch], [ln_g, ln_b], [(t, D_MODEL, F32), (t, D_MODEL, BF16)], tm=512)


def ffn_fwd(tag, hb, wg, wu, wd):
    g, u, a = ffn_gate_up(f"{tag}_gate_up", hb, wg, wu)
    f = mm(f"{tag}_down", [(a, wd, 'nn')], D_MODEL)
    return f, (hb, g, u, a)


def ln_loss_bwd(name, h, branch, target, ln_g, ln_b):
    t, dm = h.shape

    def fn(h_, br_, tgt, g_, b_):
        y, vjp = jax.vjp(functools.partial(_resid_ln, 0.5), h_, br_, g_, b_)
        e = y - tgt
        return (*vjp(e * (1.0 / dm)), jnp.sum(e * e, axis=0, keepdims=True))

    return rowwise(name, fn, [h, branch, target], [ln_g, ln_b], [(t, dm, F32), (t, dm, F32)],
                   accs=[(1, dm), (1, dm), (1, dm)], tm=512)


def resid_ln_bwd(name, scale, h, branch, ln_g, ln_b, dout, extra=None):
    t = h.shape[0]

    def fn(h_, br_, do_, *rest):
        g_, b_ = rest[-2], rest[-1]
        _, vjp = jax.vjp(functools.partial(_resid_ln, scale), h_, br_, g_, b_)
        dh, dbr, dg, db = vjp(do_)
        if extra is not None:
            dh = dh + rest[0]
        return dh, dbr, dg, db

    ins = [h, branch, dout] + ([extra] if extra is not None else [])
    return rowwise(name, fn, ins, [ln_g, ln_b], [(t, D_MODEL, F32), (t, D_MODEL, F32)],
                   accs=[(1, D_MODEL), (1, D_MODEL)], tm=512)


def ffn_bwd(tag, res, wg, wu, wd, df, dh_resid):
    hb, g, u, a = res
    dg, du = ffn_da_act(f"{tag}_bwd_da_act", df, wd, g, u)
    dwd = mm_tn(f"{tag}_bwd_dwd", a, df, BF16)
    dh = mm(f"{tag}_bwd_dh", [(dg, wg, 'nt'), (du, wu, 'nt')], D_MODEL, add=dh_resid, tn=512)
    dwg = mm_tn(f"{tag}_bwd_dwg", hb, dg, BF16)
    dwu = mm_tn(f"{tag}_bwd_dwu", hb, du, BF16)
    return dh, dwg, dwu, dwd


def rope_tables(positions):
    inv_freq = ROPE_THETA ** (-jnp.arange(0, ROPE_DIM, 2, dtype=F32) / ROPE_DIM)
    ang = positions.reshape(-1, 1).astype(F32) * inv_freq
    c, s = jnp.cos(ang), jnp.sin(ang)
    t = ang.shape[0]
    cosv = jnp.concatenate([c, c, jnp.ones((t, HEAD_DIM - ROPE_DIM), F32)], axis=1)
    sinv = jnp.concatenate([-s, s, jnp.zeros((t, HEAD_DIM - ROPE_DIM), F32)], axis=1)
    return jnp.tile(cosv, (1, 2)), jnp.tile(sinv, (1, 2))


def _pair_masks():
    lane = lax.broadcasted_iota(jnp.int32, (1, LANES), 1)
    return (lane < HEAD_DIM, lane >= HEAD_DIM)


def _band_masks():
    row = lax.broadcasted_iota(jnp.int32, (ATTN_BLOCK, ATTN_BLOCK), 0)
    col = lax.broadcasted_iota(jnp.int32, (ATTN_BLOCK, ATTN_BLOCK), 1)
    return col >= row, col <= row


def _residue_blocks():
    out = []
    for g, d in enumerate(DILATIONS):
        for r in range(d):
            for i in range(SEQ // d // ATTN_BLOCK):
                rows = lambda j: pl.ds(r + j * ATTN_BLOCK * d, ATTN_BLOCK, stride=d) if d > 1 else pl.ds(j * ATTN_BLOCK, ATTN_BLOCK)
                out.append((g, rows(i), rows(i - 1) if i > 0 else None))
    return out


N_HEAD_PAIRS = D_ATTN // LANES
SCALE = HEAD_DIM ** -0.5
ATTN_GROUP = 4


def _block_operands(qr, kr, v_ref, cur, prev):
    prev_ok, cur_ok = _band_masks()
    if prev is None:
        return qr[cur, :], kr[cur, :].astype(BF16), v_ref[cur, :], cur_ok
    kcat = jnp.concatenate([kr[prev, :], kr[cur, :]], axis=0).astype(BF16)
    vcat = jnp.concatenate([v_ref[prev, :], v_ref[cur, :]], axis=0)
    return qr[cur, :], kcat, vcat, jnp.concatenate([prev_ok, cur_ok], axis=1)


def _attn_specs(b):
    col = lambda cb: pl.BlockSpec((SEQ, LANES), lambda bb, hp: (bb, cb + hp))
    tab = pl.BlockSpec((SEQ, LANES), lambda bb, hp: (bb, 0))
    return col, tab


def attn_fwd(qkvz, cosv, sinv, b):
    t = qkvz.shape[0]
    col, tab = _attn_specs(b)
    blocks = _residue_blocks()

    def body(q_ref, k_ref, v_ref, c_ref, s_ref, o_ref, l1_ref, l2_ref, l3_ref, qr, kr, o1, o2, o3):
        l_refs, o_scr = (l1_ref, l2_ref, l3_ref), (o1, o2, o3)
        c, s = c_ref[...], s_ref[...]
        q, k = q_ref[...], k_ref[...]
        qr[...] = q * c + _rot(q) * s
        kr[...] = k * c + _rot(k) * s
        masks = _pair_masks()
        for lo in range(0, len(blocks), ATTN_GROUP):
            chains = []
            for g, cur, prev in blocks[lo:lo + ATTN_GROUP]:
                q2, kcat, vcat, ok = _block_operands(qr, kr, v_ref, cur, prev)
                for m in masks:
                    qm = jnp.where(m, q2, 0.0).astype(BF16)
                    chains.append(dict(g=g, cur=cur, m=m, v=jnp.where(m, vcat, 0.0).astype(BF16),
                                       s=jnp.where(ok, _dot(qm, kcat, _NT) * SCALE, NEG)))
            for ch in chains:
                mx = jnp.max(ch['s'], axis=1, keepdims=True)
                p = jnp.exp(ch['s'] - mx)
                den = jnp.sum(p, axis=1, keepdims=True)
                ch.update(p=p.astype(BF16), inv=1.0 / den, lse=mx + jnp.log(den))
            for ch in chains:
                ch['o'] = _dot(ch['p'], ch['v'], _NN) * ch['inv']
            for c0, c1 in zip(chains[0::2], chains[1::2]):
                o_scr[c0['g']][c0['cur'], :] = c0['o'] + c1['o']
                l_refs[c0['g']][c0['cur'], :] = jnp.where(c0['m'], c0['lse'], c1['lse'])
        w1, w2, w3 = _branch_weights(l1_ref[...], l2_ref[...], l3_ref[...])
        o_ref[...] = w1 * o1[...] + w2 * o2[...] + w3 * o3[...]

    shp = jax.ShapeDtypeStruct((t, D_ATTN), F32)
    return pl.pallas_call(
        body, name="attn_fwd", grid=(b, N_HEAD_PAIRS),
        in_specs=[col(0), col(N_HEAD_PAIRS), col(2 * N_HEAD_PAIRS), tab, tab],
        out_specs=[col(0)] * 4, out_shape=[shp] * 4,
        scratch_shapes=[pltpu.VMEM((SEQ, LANES), F32)] * 5,
        compiler_params=_cparams(("parallel", "parallel")),
    )(qkvz, qkvz, qkvz, cosv, sinv)


def attn_bwd(qkvz, cosv, sinv, dmix, mixed, lses, b):
    t = qkvz.shape[0]
    col, tab = _attn_specs(b)
    blocks = _residue_blocks()
    hd = np.arange(LANES) // HEAD_DIM
    head_ones = jnp.asarray((hd[:, None] == hd[None, :]).astype(np.float32))

    def body(q_ref, k_ref, v_ref, c_ref, s_ref, dm_ref, mx_ref, l1_ref, l2_ref, l3_ref, ones_ref,
             dq_out, dk_out, dv_out, qr, kr, do1, do2, do3, dd1, dd2, dd3, dq_ref, dk_ref, dv_ref):
        l_refs, do_scr, dd_scr = (l1_ref, l2_ref, l3_ref), (do1, do2, do3), (dd1, dd2, dd3)
        c, s = c_ref[...], s_ref[...]
        q, k = q_ref[...], k_ref[...]
        qr[...] = q * c + _rot(q) * s
        kr[...] = k * c + _rot(k) * s
        dm = dm_ref[...]
        tot = _dot(dm * mx_ref[...], ones_ref[...], _NN, HI)
        for w, do_g, dd_g in zip(_branch_weights(l1_ref[...], l2_ref[...], l3_ref[...]), do_scr, dd_scr):
            do_g[...] = w * dm
            dd_g[...] = w * tot
        dq_ref[...] = jnp.zeros((SEQ, LANES), F32)
        dk_ref[...] = jnp.zeros((SEQ, LANES), F32)
        dv_ref[...] = jnp.zeros((SEQ, LANES), F32)
        masks = _pair_masks()
        for lo in range(0, len(blocks), ATTN_GROUP):
            chains = []
            for g, cur, prev in blocks[lo:lo + ATTN_GROUP]:
                q2, kcat, vcat, ok = _block_operands(qr, kr, v_ref, cur, prev)
                vcat = vcat.astype(BF16)
                do2_, l2, dd2_ = do_scr[g][cur, :], l_refs[g][cur, :], dd_scr[g][cur, :]
                l2s, dd2s = pltpu.roll(l2, HEAD_DIM, 1), pltpu.roll(dd2_, HEAD_DIM, 1)
                for m in masks:
                    qm = jnp.where(m, q2, 0.0).astype(BF16)
                    dom = jnp.where(m, do2_, 0.0).astype(BF16)
                    lrep, ddrep = jnp.where(m, l2, l2s), jnp.where(m, dd2_, dd2s)
                    if prev is not None:
                        lrep, ddrep = jnp.concatenate([lrep, lrep], axis=1), jnp.concatenate([ddrep, ddrep], axis=1)
                    chains.append(dict(cur=cur, prev=prev, qm=qm, dom=dom, km=jnp.where(m, kcat, 0), lrep=lrep, ddrep=ddrep,
                                       s=jnp.where(ok, _dot(qm, kcat, _NT) * SCALE, NEG), dp=_dot(dom, vcat, _NT)))
            for ch in chains:
                p = jnp.exp(ch['s'] - ch['lrep'])
                ch.update(p=p.astype(BF16), ds=(p * (ch['dp'] - ch['ddrep']) * SCALE).astype(BF16))
            for ch in chains:
                ch.update(dq=_dot(ch['ds'], ch['km'], _NN), dk=_dot(ch['ds'], ch['qm'], _TN), dv=_dot(ch['p'], ch['dom'], _TN))
            for c0, c1 in zip(chains[0::2], chains[1::2]):
                cur, prev = c0['cur'], c0['prev']
                dk, dv = c0['dk'] + c1['dk'], c0['dv'] + c1['dv']
                dq_ref[cur, :] += c0['dq'] + c1['dq']
                if prev is None:
                    dk_ref[cur, :] += dk
                    dv_ref[cur, :] += dv
                else:
                    dk_ref[prev, :] += dk[:ATTN_BLOCK]
                    dv_ref[prev, :] += dv[:ATTN_BLOCK]
                    dk_ref[cur, :] += dk[ATTN_BLOCK:]
                    dv_ref[cur, :] += dv[ATTN_BLOCK:]
        dq, dk = dq_ref[...], dk_ref[...]
        dq_out[...] = (dq * c + _rot(dq * s)).astype(dq_out.dtype)
        dk_out[...] = (dk * c + _rot(dk * s)).astype(dk_out.dtype)
        dv_out[...] = dv_ref[...].astype(dv_out.dtype)

    shp = jax.ShapeDtypeStruct((t, D_ATTN), BF16)
    return pl.pallas_call(
        body, name="attn_bwd", grid=(b, N_HEAD_PAIRS),
        in_specs=[col(0), col(N_HEAD_PAIRS), col(2 * N_HEAD_PAIRS), tab, tab, col(0), col(0), col(0), col(0), col(0),
                  pl.BlockSpec((LANES, LANES), lambda bb, hp: (0, 0))],
        out_specs=[col(0)] * 3, out_shape=[shp] * 3,
        scratch_shapes=[pltpu.VMEM((SEQ, LANES), F32)] * 11,
        compiler_params=_cparams(("parallel", "parallel")),
    )(qkvz, qkvz, qkvz, cosv, sinv, dmix, mixed, *lses, head_ones)


def attn_norm_fwd(mixed, norm_w):
    return rowwise("attn_norm", _rms, [mixed], [norm_w], [(mixed.shape[0], D_ATTN, BF16)])[0]


def attn_norm_bwd(dout, mixed, norm_w):
    def fn(dy, mx, w):
        _, vjp = jax.vjp(_rms, mx, w)
        return vjp(dy)

    return rowwise("attn_norm_bwd", fn, [dout, mixed], [norm_w], [(dout.shape[0], D_ATTN, F32)], accs=[(1, D_ATTN)])


CONV_TM = 256
HALO = 8


def conv_fwd(u, w, bias):
    t = u.shape[0]
    tm, per_seq = CONV_TM, SEQ // CONV_TM

    def body(u_ref, h_ref, w_ref, b_ref, xs_ref, bm_ref, cm_ref, scr):
        first = pl.program_id(0) % per_seq == 0
        scr[0:HALO, :] = jnp.where(first, 0.0, h_ref[...])
        scr[HALO:, :] = u_ref[...]
        acc = b_ref[...]
        for k in range(CONV_WIDTH):
            acc = acc + w_ref[k:k + 1, :] * scr[pl.ds(HALO - CONV_WIDTH + 1 + k, tm), :]
        y = _silu(acc)
        xs_ref[...] = y[:, :D_SSD]
        bm_ref[...] = y[:, D_SSD:D_SSD + D_BC]
        cm_ref[...] = y[:, D_SSD + D_BC:]

    return pl.pallas_call(
        body, name="conv_fwd", grid=(t // tm,),
        in_specs=[pl.BlockSpec((tm, D_CONV), lambda i: (i, 0)),
                  pl.BlockSpec((HALO, D_CONV), lambda i: (jnp.maximum(i * (tm // HALO) - 1, 0), 0)),
                  pl.BlockSpec((CONV_WIDTH, D_CONV), lambda i: (0, 0)), pl.BlockSpec((1, D_CONV), lambda i: (0, 0))],
        out_specs=[pl.BlockSpec((tm, D_SSD), lambda i: (i, 0)), pl.BlockSpec((tm, D_BC), lambda i: (i, 0)),
                   pl.BlockSpec((tm, D_BC), lambda i: (i, 0))],
        out_shape=[jax.ShapeDtypeStruct((t, D_SSD), F32), jax.ShapeDtypeStruct((t, D_BC), F32),
                   jax.ShapeDtypeStruct((t, D_BC), F32)],
        scratch_shapes=[pltpu.VMEM((tm + HALO, D_CONV), F32)],
        compiler_params=_cparams(("parallel",)),
    )(u, u, w, bias)


def conv_bwd(u, w, bias, dxs_a, dxs_b, dbm, dcm):
    t = u.shape[0]
    tm, per_seq = CONV_TM, SEQ // CONV_TM
    n_tiles = t // tm

    def body1(u_ref, h_ref, dxs_ref, dxs2_ref, dbm_ref, dcm_ref, w_ref, b_ref, dz_ref, dw_ref, db_ref, scr):
        i = pl.program_id(0)
        first = i % per_seq == 0
        scr[0:HALO, :] = jnp.where(first, 0.0, h_ref[...])
        scr[HALO:, :] = u_ref[...]
        acc = b_ref[...]
        for k in range(CONV_WIDTH):
            acc = acc + w_ref[k:k + 1, :] * scr[pl.ds(HALO - CONV_WIDTH + 1 + k, tm), :]
        sig = _sigmoid(acc)
        dy = jnp.concatenate([dxs_ref[...] + dxs2_ref[...], dbm_ref[...], dcm_ref[...]], axis=1)
        dz = dy * sig * (1.0 + acc * (1.0 - sig))
        dz_ref[...] = dz

        @pl.when(i == 0)
        def _():
            dw_ref[...] = jnp.zeros(dw_ref.shape, F32)
            db_ref[...] = jnp.zeros(db_ref.shape, F32)
        db_ref[...] += jnp.sum(dz, axis=0, keepdims=True)
        for k in range(CONV_WIDTH):
            dw_ref[k:k + 1, :] += jnp.sum(dz * scr[pl.ds(HALO - CONV_WIDTH + 1 + k, tm), :], axis=0, keepdims=True)

    dz, dw, db = pl.pallas_call(
        body1, name="conv_bwd_dz", grid=(n_tiles,),
        in_specs=[pl.BlockSpec((tm, D_CONV), lambda i: (i, 0)),
                  pl.BlockSpec((HALO, D_CONV), lambda i: (jnp.maximum(i * (tm // HALO) - 1, 0), 0)),
                  pl.BlockSpec((tm, D_SSD), lambda i: (i, 0)), pl.BlockSpec((tm, D_SSD), lambda i: (i, 0)),
                  pl.BlockSpec((tm, D_BC), lambda i: (i, 0)), pl.BlockSpec((tm, D_BC), lambda i: (i, 0)),
                  pl.BlockSpec((CONV_WIDTH, D_CONV), lambda i: (0, 0)), pl.BlockSpec((1, D_CONV), lambda i: (0, 0))],
        out_specs=[pl.BlockSpec((tm, D_CONV), lambda i: (i, 0)), pl.BlockSpec((CONV_WIDTH, D_CONV), lambda i: (0, 0)),
                   pl.BlockSpec((1, D_CONV), lambda i: (0, 0))],
        out_shape=[jax.ShapeDtypeStruct((t, D_CONV), F32), jax.ShapeDtypeStruct((CONV_WIDTH, D_CONV), F32),
                   jax.ShapeDtypeStruct((1, D_CONV), F32)],
        scratch_shapes=[pltpu.VMEM((tm + HALO, D_CONV), F32)],
        compiler_params=_cparams(("arbitrary",)),
    )(u, u, dxs_a, dxs_b, dbm, dcm, w, bias)

    def body2(dz_ref, n_ref, w_ref, du_ref, scr):
        last = pl.program_id(0) % per_seq == per_seq - 1
        scr[0:tm, :] = dz_ref[...]
        scr[tm:, :] = jnp.where(last, 0.0, n_ref[...])
        acc = jnp.zeros((tm, D_CONV), F32)
        for k in range(CONV_WIDTH):
            acc = acc + w_ref[k:k + 1, :] * scr[pl.ds(CONV_WIDTH - 1 - k, tm), :]
        du_ref[...] = acc.astype(du_ref.dtype)

    du = pl.pallas_call(
        body2, name="conv_bwd_du", grid=(n_tiles,),
        in_specs=[pl.BlockSpec((tm, D_CONV), lambda i: (i, 0)),
                  pl.BlockSpec((HALO, D_CONV), lambda i: (jnp.minimum((i + 1) * (tm // HALO), t // HALO - 1), 0)),
                  pl.BlockSpec((CONV_WIDTH, D_CONV), lambda i: (0, 0))],
        out_specs=pl.BlockSpec((tm, D_CONV), lambda i: (i, 0)),
        out_shape=jax.ShapeDtypeStruct((t, D_CONV), BF16),
        scratch_shapes=[pltpu.VMEM((tm + HALO, D_CONV), F32)],
        compiler_params=_cparams(("parallel",)),
    )(dz, dz, w)
    return du, dw, db


Q = SSD_CHUNK
N_PAIRS = D_SSD // LANES
HEADS_PER_GROUP = N_HEADS // SSD_GROUPS


def _rep(a, j):
    return jnp.broadcast_to(a[:, j:j + 1], a.shape)


def _dot_exact01(a, b, dn, a_is_01):
    x = b if a_is_01 else a
    hi = x.astype(BF16)
    mid = (x - hi.astype(F32)).astype(BF16)
    lo = (x - hi.astype(F32) - mid.astype(F32)).astype(BF16)
    z = a.astype(BF16) if a_is_01 else b.astype(BF16)
    out = None
    for term in (hi, mid, lo):
        d = _dot(z, term, dn) if a_is_01 else _dot(term, z, dn)
        out = d if out is None else out + d
    return out


def _pad_lanes(v, fill=0.0):
    row = jnp.pad(v.reshape(1, -1).astype(F32), ((0, 0), (0, LANES - v.size)), constant_values=fill)
    return row, row.reshape(LANES, 1)


def _ssd_common(dtr_ref, dtrt_ref, bias_r, bias_c, alog_r, alog_c):
    row = lax.broadcasted_iota(jnp.int32, (Q, Q), 0)
    col = lax.broadcasted_iota(jnp.int32, (Q, Q), 1)
    tril = row >= col
    lane = lax.broadcasted_iota(jnp.int32, (1, LANES), 1)
    a_r = jnp.where(lane < N_HEADS, -jnp.exp(alog_r[...]), 0.0)
    sub = lax.broadcasted_iota(jnp.int32, (LANES, 1), 0)
    a_c = jnp.where(sub < N_HEADS, -jnp.exp(alog_c[...]), 0.0)
    dt = _softplus(dtr_ref[...] + bias_r[...])
    cs = _dot_exact01(tril, dt * a_r, _NN, True)
    dtt = _softplus(dtrt_ref[...] + bias_c[...])
    cst = _dot_exact01(dtt * a_c, row <= col, _NN, False)
    return tril, lane, a_r, dt, cs, cst


def _ssd_specs(b, nc, rev):
    ci = (lambda c: nc - 1 - c) if rev else (lambda c: c)
    rows = lambda w: pl.BlockSpec((Q, w), lambda bb, c: (bb * nc + ci(c), 0))
    dtt = pl.BlockSpec((LANES, Q), lambda bb, c: (0, bb * nc + ci(c)))
    const = lambda s: pl.BlockSpec(s, lambda bb, c: (0,) * len(s))
    state = pl.BlockSpec((None, N_PAIRS, LANES, SSD_STATE), lambda bb, c: (bb * nc + ci(c), 0, 0, 0))
    return rows, dtt, const, state


def ssd_fwd(xs, bm, cm, dtraw, dt_bias, a_log, b):
    t = xs.shape[0]
    nc = SEQ // Q
    rows, dtt_spec, const, state = _ssd_specs(b, nc, False)
    bias_r, bias_c = _pad_lanes(dt_bias)
    alog_r, alog_c = _pad_lanes(a_log)

    def body(xs_ref, b_ref, c_ref, dtr_ref, dtrt_ref, br, bc, ar, ac, y_ref, hp_ref, h_scr):
        @pl.when(pl.program_id(1) == 0)
        def _():
            h_scr[...] = jnp.zeros(h_scr.shape, F32)
        tril, lane, _, dt, cs, cst = _ssd_common(dtr_ref, dtrt_ref, br, bc, ar, ac)
        sub = lax.broadcasted_iota(jnp.int32, (LANES, 1), 0)
        y_acc = [jnp.zeros((Q, LANES), F32) for _ in range(N_PAIRS)]
        h_old = [h_scr[p] for p in range(N_PAIRS)]
        h_new = [jnp.zeros((LANES, SSD_STATE), F32) for _ in range(N_PAIRS)]
        for g in range(SSD_GROUPS):
            bg = b_ref[:, g * SSD_STATE:(g + 1) * SSD_STATE].astype(BF16)
            cg = c_ref[:, g * SSD_STATE:(g + 1) * SSD_STATE].astype(BF16)
            cb = _dot(cg, bg, _NT)
            for j in range(g * HEADS_PER_GROUP, (g + 1) * HEADS_PER_GROUP):
                p, side = j // 2, j % 2
                m = (lane < HEAD_DIM) if side == 0 else (lane >= HEAD_DIM)
                ms = (sub < HEAD_DIM) if side == 0 else (sub >= HEAD_DIM)
                csj, dtj = _rep(cs, j), _rep(dt, j)
                lmat = jnp.exp(jnp.where(tril, csj - cst[j:j + 1, :], NEG))
                xdt = jnp.where(m, xs_ref[:, p * LANES:(p + 1) * LANES] * dtj, 0.0)
                hm = jnp.where(ms, h_old[p], 0.0)
                ydiag = _dot((cb * lmat).astype(BF16), xdt.astype(BF16), _NN)
                yoff = jnp.exp(csj) * _dot(cg, hm.astype(BF16), _NT)
                y_acc[p] = y_acc[p] + ydiag + yoff
                last = csj[Q - 1:Q, :]
                sj = _dot((xdt * jnp.exp(last - csj)).astype(BF16), bg, _TN)
                h_new[p] = h_new[p] + jnp.exp(last) * hm + sj
        for p in range(N_PAIRS):
            y_ref[:, p * LANES:(p + 1) * LANES] = y_acc[p]
            hp_ref[p] = h_old[p]
            h_scr[p] = h_new[p]

    return pl.pallas_call(
        body, name="ssd_fwd", grid=(b, nc),
        in_specs=[rows(D_SSD), rows(D_BC), rows(D_BC), rows(LANES), dtt_spec, const((1, LANES)), const((LANES, 1)),
                  const((1, LANES)), const((LANES, 1))],
        out_specs=[rows(D_SSD), state],
        out_shape=[jax.ShapeDtypeStruct((t, D_SSD), F32),
                   jax.ShapeDtypeStruct((b * nc, N_PAIRS, LANES, SSD_STATE), F32)],
        scratch_shapes=[pltpu.VMEM((N_PAIRS, LANES, SSD_STATE), F32)],
        compiler_params=_cparams(("parallel", "arbitrary")),
    )(xs, bm, cm, dtraw, dtraw.T, bias_r, bias_c, alog_r, alog_c)


def ssd_bwd(xs, bm, cm, dtraw, dt_bias, a_log, hprev, dy, b):
    t = xs.shape[0]
    nc = SEQ // Q
    rows, dtt_spec, const, state = _ssd_specs(b, nc, True)
    bias_r, bias_c = _pad_lanes(dt_bias)
    alog_r, alog_c = _pad_lanes(a_log)

    def body(xs_ref, b_ref, c_ref, dtr_ref, dtrt_ref, hp_ref, dy_ref, br, bc, ar, ac,
             dxs_ref, db_ref, dc_ref, ddt_ref, dbias_ref, dalog_ref, dh_scr):
        first = jnp.logical_and(pl.program_id(0) == 0, pl.program_id(1) == 0)

        @pl.when(pl.program_id(1) == 0)
        def _():
            dh_scr[...] = jnp.zeros(dh_scr.shape, F32)

        @pl.when(first)
        def _():
            dbias_ref[...] = jnp.zeros(dbias_ref.shape, F32)
            dalog_ref[...] = jnp.zeros(dalog_ref.shape, F32)
        tril, lane, a_r, dt, cs, cst = _ssd_common(dtr_ref, dtrt_ref, br, bc, ar, ac)
        sub = lax.broadcasted_iota(jnp.int32, (LANES, 1), 0)
        rowq = lax.broadcasted_iota(jnp.int32, (Q, 1), 0)
        triu = (lax.broadcasted_iota(jnp.int32, (Q, Q), 0) <= lax.broadcasted_iota(jnp.int32, (Q, Q), 1)).astype(F32)
        dxs_acc = [jnp.zeros((Q, LANES), F32) for _ in range(N_PAIRS)]
        dh_in = [dh_scr[p] for p in range(N_PAIRS)]
        h_in = [hp_ref[p] for p in range(N_PAIRS)]
        dh_out = [jnp.zeros((LANES, SSD_STATE), F32) for _ in range(N_PAIRS)]
        ddt = jnp.zeros((Q, LANES), F32)
        dalog = jnp.zeros((1, LANES), F32)
        for g in range(SSD_GROUPS):
            gs = slice(g * SSD_STATE, (g + 1) * SSD_STATE)
            bg, cg = b_ref[:, gs].astype(BF16), c_ref[:, gs].astype(BF16)
            cb = _dot(cg, bg, _NT)
            dcb = jnp.zeros((Q, Q), F32)
            dbg = jnp.zeros((Q, SSD_STATE), F32)
            dcg = jnp.zeros((Q, SSD_STATE), F32)
            for j in range(g * HEADS_PER_GROUP, (g + 1) * HEADS_PER_GROUP):
                p, side = j // 2, j % 2
                m = (lane < HEAD_DIM) if side == 0 else (lane >= HEAD_DIM)
                ms = (sub < HEAD_DIM) if side == 0 else (sub >= HEAD_DIM)
                csj, dtj = _rep(cs, j), _rep(dt, j)
                lmat = jnp.exp(jnp.where(tril, csj - cst[j:j + 1, :], NEG))
                x2 = jnp.where(m, xs_ref[:, p * LANES:(p + 1) * LANES], 0.0)
                xdt = x2 * dtj
                dym = jnp.where(m, dy_ref[:, p * LANES:(p + 1) * LANES], 0.0)
                hm = jnp.where(ms, h_in[p], 0.0)
                dhm = jnp.where(ms, dh_in[p], 0.0)
                ecs = jnp.exp(csj)
                last = csj[Q - 1:Q, :]
                decay = jnp.exp(last - csj)
                el = jnp.exp(last)
                gmat = cb * lmat
                dymb, xdtb = dym.astype(BF16), xdt.astype(BF16)
                dg = _dot(dymb, xdtb, _NT)
                dxdt = _dot(gmat.astype(BF16), dymb, _TN)
                dcb = dcb + dg * lmat
                ej = dg * gmat
                col_sums = jnp.broadcast_to(jnp.sum(ej, axis=0, keepdims=True), (Q, Q)).T
                dcs = jnp.sum(ej, axis=1, keepdims=True) - col_sums
                ch = _dot(cg, hm.astype(BF16), _NT)
                dye = dym * ecs
                dcs = dcs + jnp.sum(dye * ch, axis=1, keepdims=True)
                dcg = dcg + _dot(dye.astype(BF16), hm.astype(BF16), _NN)
                dhp = _dot(dye.astype(BF16), cg, _TN)
                wmat = _dot(bg, dhm.astype(BF16), _NT)
                xd = xdt * decay
                dxdt = dxdt + decay * wmat
                ddl = jnp.sum(xd * wmat, axis=1, keepdims=True)
                dlast = jnp.sum(ddl, axis=0, keepdims=True) + el * jnp.sum(jnp.sum(dhm * hm, axis=1, keepdims=True), axis=0, keepdims=True)
                dcs = dcs - ddl + jnp.where(rowq == Q - 1, dlast, 0.0)
                dbg = dbg + _dot(xd.astype(BF16), dhm.astype(BF16), _NN)
                dh_out[p] = dh_out[p] + el * dhm + dhp
                da = _dot_exact01(triu, dcs, _NN, True)
                aj = jnp.sum(jnp.where(lane == j, a_r, 0.0), axis=1, keepdims=True)
                ddtj = da * aj + jnp.sum(dxdt * x2, axis=1, keepdims=True)
                ddt = ddt + jnp.where(lane == j, ddtj, 0.0)
                dalog = dalog + jnp.where(lane == j, jnp.sum(da * dtj, axis=0, keepdims=True) * aj, 0.0)
                dxs_acc[p] = dxs_acc[p] + dxdt * dtj
            dcbb = dcb.astype(BF16)
            dc_ref[:, gs] = dcg + _dot(dcbb, bg, _NN)
            db_ref[:, gs] = dbg + _dot(dcbb, cg, _TN)
        for p in range(N_PAIRS):
            dxs_ref[:, p * LANES:(p + 1) * LANES] = dxs_acc[p]
            dh_scr[p] = dh_out[p]
        ddtraw = ddt * _sigmoid(dtr_ref[...] + br[...])
        ddt_ref[...] = ddtraw
        dbias_ref[...] += jnp.sum(ddtraw, axis=0, keepdims=True)
        dalog_ref[...] += dalog

    return pl.pallas_call(
        body, name="ssd_bwd", grid=(b, nc),
        in_specs=[rows(D_SSD), rows(D_BC), rows(D_BC), rows(LANES), dtt_spec, state, rows(D_SSD), const((1, LANES)),
                  const((LANES, 1)), const((1, LANES)), const((LANES, 1))],
        out_specs=[rows(D_SSD), rows(D_BC), rows(D_BC), rows(LANES), const((1, LANES)), const((1, LANES))],
        out_shape=[jax.ShapeDtypeStruct((t, D_SSD), F32), jax.ShapeDtypeStruct((t, D_BC), F32),
                   jax.ShapeDtypeStruct((t, D_BC), F32), jax.ShapeDtypeStruct((t, LANES), F32),
                   jax.ShapeDtypeStruct((1, LANES), F32), jax.ShapeDtypeStruct((1, LANES), F32)],
        scratch_shapes=[pltpu.VMEM((N_PAIRS, LANES, SSD_STATE), F32)],
        compiler_params=_cparams(("arbitrary", "arbitrary")),
    )(xs, bm, cm, dtraw, dtraw.T, hprev, dy, bias_r, bias_c, alog_r, alog_c)


def _split_w_in(w_in):
    w_dt = jnp.pad(w_in[:, D_QKVZ + D_CONV:], ((0, 0), (0, LANES - N_HEADS)))
    return w_in[:, :D_QKVZ], w_in[:, D_QKVZ:D_QKVZ + D_CONV], w_dt


def mixer_fwd(hb, p, cosv, sinv, b):
    t = hb.shape[0]
    w_a, w_b, w_c = _split_w_in(p['w_in'])
    qkvz = mm("in_qkvz", [(hb, w_a, 'nn')], D_QKVZ)
    xbc = mm("in_xbc", [(hb, w_b, 'nn')], D_CONV)
    dtraw = mm("in_dt", [(hb, w_c, 'nn')], LANES)
    mixed, *lses = attn_fwd(qkvz, cosv, sinv, b)
    attn = attn_norm_fwd(mixed, p['attn_norm_w'])
    xs, bm, cm = conv_fwd(xbc, p['conv_w'], p['conv_b'])
    y, hprev = ssd_fwd(xs, bm, cm, dtraw, p['dt_bias'], p['a_log'], b)
    dskip = jnp.repeat(p['d_skip'].reshape(-1), HEAD_DIM).reshape(1, D_SSD)
    yg, = rowwise("ssd_gate", _gate, [y, xs, Op(qkvz, D_SSD, 3)], [dskip, p['ssd_norm_w']], [(t, D_SSD, BF16)])
    mix = mm("out_proj", [(attn, p['w_out'][:D_ATTN], 'nn'), (yg, p['w_out'][D_ATTN:], 'nn')], D_MODEL)
    res = dict(hb=hb, qkvz=qkvz, xbc=xbc, dtraw=dtraw, mixed=mixed, lses=lses, attn=attn, xs=xs, bm=bm, cm=cm,
               y=y, hprev=hprev, dskip=dskip, yg=yg, cosv=cosv, sinv=sinv)
    return mix, res


def mixer_bwd(r, p, dmix, dh_resid, b):
    t = dmix.shape[0]
    w_a, w_b, w_c = _split_w_in(p['w_in'])
    w_out = p['w_out']
    dattn = mm("out_bwd_dattn", [(dmix, w_out[:D_ATTN], 'nt')], D_ATTN)
    dyg = mm("out_bwd_dyg", [(dmix, w_out[D_ATTN:], 'nt')], D_SSD)
    dw_out = jnp.concatenate([mm_tn("out_bwd_dw_a", r['attn'], dmix, BF16),
                              mm_tn("out_bwd_dw_y", r['yg'], dmix, BF16)], axis=0)

    def gate_bwd(dy_, y_, xs_, z_, ds_, w_):
        _, vjp = jax.vjp(_gate, y_, xs_, z_, ds_, w_)
        return vjp(dy_)

    dy, dxs_a, dz, ddskip, dssd_norm = rowwise(
        "ssd_gate_bwd", gate_bwd, [dyg, r['y'], r['xs'], Op(r['qkvz'], D_SSD, 3)], [r['dskip'], p['ssd_norm_w']],
        [(t, D_SSD, F32), (t, D_SSD, F32), (t, D_SSD, BF16)], accs=[(1, D_SSD), (1, D_SSD)])
    dxs_b, dbm, dcm, ddtraw, ddt_bias, da_log = ssd_bwd(r['xs'], r['bm'], r['cm'], r['dtraw'], p['dt_bias'], p['a_log'],
                                                        r['hprev'], dy, b)
    dxbc, dconv_w, dconv_b = conv_bwd(r['xbc'], p['conv_w'], p['conv_b'], dxs_a, dxs_b, dbm, dcm)
    dmixed, dattn_norm = attn_norm_bwd(dattn, r['mixed'], p['attn_norm_w'])
    dq, dk, dv = attn_bwd(r['qkvz'], r['cosv'], r['sinv'], dmixed, r['mixed'], r['lses'], b)
    wq, wk, wv, wz = (w_a[:, i * D_ATTN:(i + 1) * D_ATTN] for i in range(4))
    dh = mm("in_bwd_dh", [(dq, wq, 'nt'), (dk, wk, 'nt'), (dv, wv, 'nt'), (dz, wz, 'nt'), (dxbc, w_b, 'nt'),
                          (ddtraw, w_c, 'nt')], D_MODEL, add=dh_resid, tn=512)
    h = r['hb']
    dw_in = jnp.concatenate([mm_tn("in_bwd_dwq", h, dq, BF16), mm_tn("in_bwd_dwk", h, dk, BF16),
                             mm_tn("in_bwd_dwv", h, dv, BF16), mm_tn("in_bwd_dwz", h, dz, BF16),
                             mm_tn("in_bwd_dwx", h, dxbc, BF16), mm_tn("in_bwd_dwdt", h, ddtraw, BF16)[:, :N_HEADS]], axis=1)
    head_sum = lambda v: v.reshape(N_HEADS, HEAD_DIM).sum(axis=1).reshape(1, N_HEADS)
    grads = dict(w_in=dw_in, w_out=dw_out, conv_w=dconv_w, conv_b=dconv_b, dt_bias=ddt_bias[:, :N_HEADS],
                 a_log=da_log[:, :N_HEADS], d_skip=head_sum(ddskip), attn_norm_w=dattn_norm, ssd_norm_w=dssd_norm)
    return dh, grads


FFN_COL = ('ffn1_gate', 'ffn1_up', 'ffn2_gate', 'ffn2_up')
FFN_ROW = ('ffn1_down', 'ffn2_down')
CONV_W_COMM = (8, 2 * LANES)
SMALL = 'small'


def comm_shape(k, shapes):
    if k in FFN_COL:
        return (D_MODEL, FF_PAD)
    if k in FFN_ROW:
        return (FF_PAD, D_MODEL)
    if k == 'conv_w':
        return CONV_W_COMM
    if k == SMALL:
        n = sum(int(np.prod(shapes[r])) for r in REPLICATED)
        return (-(-n // (8 * LANES)) * 8, LANES)
    return tuple(shapes[k][1:])


def to_comm(k, vals, shapes):
    if k == SMALL:
        flat = jnp.concatenate([vals[r].reshape(-1) for r in REPLICATED])
        r_, c_ = comm_shape(k, shapes)
        return jnp.pad(flat, (0, r_ * c_ - flat.size)).reshape(r_, c_)
    a = vals[k].reshape(shapes[k][1:])
    r_, c_ = comm_shape(k, shapes)
    return jnp.pad(a, ((0, r_ - a.shape[0]), (0, c_ - a.shape[1])))


def from_comm(k, a, shapes):
    if k == SMALL:
        flat, out, off = a.reshape(-1), {}, 0
        for r in REPLICATED:
            n = int(np.prod(shapes[r]))
            out[r] = flat[off:off + n].reshape(shapes[r])
            off += n
        return out
    shp = shapes[k][1:]
    return {k: a[:shp[0], :shp[1]].reshape(shapes[k])}


def full_weight(k, g):
    if k in FFN_COL:
        return jnp.concatenate([g[p] for p in range(N_DEV)], axis=1)
    if k == 'conv_w':
        return jnp.transpose(g[:, :CONV_WIDTH, :D_CONV // N_DEV], (1, 0, 2)).reshape(CONV_WIDTH, D_CONV)
    return g.reshape(N_DEV * g.shape[1], g.shape[2])


def grad_shards(k, g):
    if k in FFN_COL:
        return jnp.stack([g[:, p * FF_PAD:(p + 1) * FF_PAD] for p in range(N_DEV)])
    if k == 'conv_w':
        s = jnp.transpose(g.reshape(CONV_WIDTH, N_DEV, D_CONV // N_DEV), (1, 0, 2))
        return jnp.pad(s, ((0, 0), (0, CONV_W_COMM[0] - CONV_WIDTH), (0, CONV_W_COMM[1] - D_CONV // N_DEV)))
    return g.reshape(N_DEV, g.shape[0] // N_DEV, g.shape[1])


def _flip(v, bit):
    return 1 - v if bit else v


N_PEER_COPIES = N_DEV - 1


def _comm_call(name, body, arrs, out_shape):
    n = len(arrs)
    return pl.pallas_call(
        functools.partial(body, n), name=name, out_shape=out_shape,
        in_specs=[pl.BlockSpec(memory_space=pl.ANY)] * n, out_specs=[pl.BlockSpec(memory_space=pl.ANY)] * n,
        scratch_shapes=[pltpu.SemaphoreType.DMA((n * N_PEER_COPIES,)), pltpu.SemaphoreType.DMA((n * N_PEER_COPIES,)),
                        pltpu.SemaphoreType.DMA((n,))],
    )(*arrs)


def all_gather(arrs):
    def body(n, *refs):
        x_refs, out_refs, (send_sems, recv_sems, local_sems) = refs[:n], refs[n:2 * n], refs[2 * n:]
        x, y, c = lax.axis_index("x"), lax.axis_index("y"), lax.axis_index("c")
        me, sibling = (x, y, c), (x, y, 1 - c)
        chips = [(1 - x, y), (x, 1 - y), (1 - x, 1 - y)]

        def copy(a, k, block, to, src=None):
            px, py, pc = block
            dst = out_refs[a].at[4 * px + 2 * py + pc]
            return pltpu.make_async_remote_copy(
                src_ref=dst if src is None else src, dst_ref=dst, send_sem=send_sems.at[a * N_PEER_COPIES + k],
                recv_sem=recv_sems.at[a * N_PEER_COPIES + k], device_id=to, device_id_type=MESH)

        mine = [pltpu.make_async_copy(x_refs[a], out_refs[a].at[4 * x + 2 * y + c], local_sems.at[a]) for a in range(n)]
        started = []
        for a in range(n):
            mine[a].start()
            first = [copy(a, 0, me, sibling, src=x_refs[a])]
            first += [copy(a, 1 + j, me, (*chip, c), src=x_refs[a]) for j, chip in enumerate(chips)]
            for cp in first:
                cp.start()
            started += first
        for j, chip in enumerate(chips):
            for a in range(n):
                copy(a, 1 + j, (*chip, c), me).wait_recv()
                cp = copy(a, 4 + j, (*chip, c), sibling)
                cp.start()
                started.append(cp)
        for a in range(n):
            copy(a, 0, sibling, me).wait_recv()
            for j, chip in enumerate(chips):
                copy(a, 4 + j, (*chip, 1 - c), me).wait_recv()
        for cp in started:
            cp.wait_send()
        for cp in mine:
            cp.wait()

    return _comm_call("all_gather_weights", body, arrs,
                      [jax.ShapeDtypeStruct((N_DEV,) + a.shape, a.dtype) for a in arrs])


def all_to_all(arrs):
    def body(n, *refs):
        s_refs, r_refs, (send_sems, recv_sems, local_sems) = refs[:n], refs[n:2 * n], refs[2 * n:]
        x, y, c = lax.axis_index("x"), lax.axis_index("y"), lax.axis_index("c")
        me = 4 * x + 2 * y + c

        def peer(k):
            return _flip(x, k & 4), _flip(y, k & 2), _flip(c, k & 1)

        def copy(a, k, landing):
            px, py, pc = peer(k)
            p = 4 * px + 2 * py + pc
            src, dst = (s_refs[a].at[me], r_refs[a].at[p]) if landing else (s_refs[a].at[p], r_refs[a].at[me])
            return pltpu.make_async_remote_copy(
                src_ref=src, dst_ref=dst, send_sem=send_sems.at[a * N_PEER_COPIES + k - 1],
                recv_sem=recv_sems.at[a * N_PEER_COPIES + k - 1], device_id=(px, py, pc), device_id_type=MESH)

        mine = [pltpu.make_async_copy(s_refs[a].at[me], r_refs[a].at[me], local_sems.at[a]) for a in range(n)]
        sends = [copy(a, k, False) for a in range(n) for k in range(1, N_DEV)]
        for cp in mine + sends:
            cp.start()
        for a in range(n):
            for k in range(1, N_DEV):
                copy(a, k, True).wait_recv()
        for cp in sends:
            cp.wait_send()
        for cp in mine:
            cp.wait()

    return _comm_call("all_to_all_grads", body, arrs, [jax.ShapeDtypeStruct(a.shape, a.dtype) for a in arrs])


def adamw(name, recv, w, m, v, tm):
    rows, cols = w.shape
    c1 = 1.0 / (1.0 - ADAM_B1 ** ADAM_STEP)
    c2 = 1.0 / (1.0 - ADAM_B2 ** ADAM_STEP)

    def fn(*a):
        g = a[0]
        for s in range(1, N_DEV):
            g = g + a[s]
        w_, m_, v_ = a[N_DEV:]
        m_ = ADAM_B1 * m_ + (1.0 - ADAM_B1) * g
        v_ = ADAM_B2 * v_ + (1.0 - ADAM_B2) * jnp.square(g)
        delta = -ADAM_LR * ((m_ * c1) / (jnp.sqrt(v_ * c2) + ADAM_EPS) + ADAM_WD * w_)
        return g, delta, m_, v_

    flat = recv.reshape(N_DEV * rows, cols)
    ins = [Op(flat, cols, 0, s * (rows // tm)) for s in range(N_DEV)] + [w, m, v]
    return rowwise(name, fn, ins, [], [(rows, cols, F32)] * 4, tm=tm)


ADAMW_TM = {'ffn1_gate': 256, 'ffn1_up': 256, 'ffn1_down': 128, 'w_in': 32, 'conv_w': 8, 'w_out': 64,
            'ffn2_gate': 256, 'ffn2_up': 256, 'ffn2_down': 128}


def kernel(x, positions, ln1_g, ln1_b, ffn1_gate, ffn1_up, ffn1_down, w_in, conv_w, conv_b, dt_bias, a_log, d_skip, attn_norm_w, ssd_norm_w, w_out, ln2_g, ln2_b, ffn2_gate, ffn2_up, ffn2_down, ln3_g, ln3_b, loss_target, m_ln1_g, m_ln1_b, m_ffn1_gate, m_ffn1_up, m_ffn1_down, m_w_in, m_conv_w, m_conv_b, m_dt_bias, m_a_log, m_d_skip, m_attn_norm_w, m_ssd_norm_w, m_w_out, m_ln2_g, m_ln2_b, m_ffn2_gate, m_ffn2_up, m_ffn2_down, m_ln3_g, m_ln3_b, v_ln1_g, v_ln1_b, v_ffn1_gate, v_ffn1_up, v_ffn1_down, v_w_in, v_conv_w, v_conv_b, v_dt_bias, v_a_log, v_d_skip, v_attn_norm_w, v_ssd_norm_w, v_w_out, v_ln2_g, v_ln2_b, v_ffn2_gate, v_ffn2_up, v_ffn2_down, v_ln3_g, v_ln3_b):
    args = dict(locals())
    wl = {k: args[k] for k in WEIGHTS}
    ml = {k: args["m_" + k] for k in WEIGHTS}
    vl = {k: args["v_" + k] for k in WEIGHTS}
    shapes = {k: wl[k].shape for k in WEIGHTS}
    b, s, dm = x.shape
    t = b * s

    w_comm = {k: to_comm(k, wl, shapes) for k in SHARDED + (SMALL,)}
    gathered = all_gather([w_comm[k] if k == 'conv_w' else w_comm[k].astype(BF16) for k in SHARDED])
    p = {k: full_weight(k, g) for k, g in zip(SHARDED, gathered)}
    for k in REPLICATED:
        p[k] = wl[k].reshape(1, -1)

    x2 = x.reshape(t, dm)
    cosv, sinv = rope_tables(positions)
    f1, res1 = ffn_fwd("ffn1", x2, p['ffn1_gate'], p['ffn1_up'], p['ffn1_down'])
    h1, h1b = resid_ln_fwd("ln1", 0.5, x2, f1, p['ln1_g'], p['ln1_b'])
    mix, resm = mixer_fwd(h1b, p, cosv, sinv, b)
    h2, h2b = resid_ln_fwd("ln2", 1.0, h1, mix, p['ln2_g'], p['ln2_b'])
    f2, res3 = ffn_fwd("ffn2", h2b, p['ffn2_gate'], p['ffn2_up'], p['ffn2_down'])

    small, full = {}, {}
    dh2_res, df2, small['ln3_g'], small['ln3_b'], sq = ln_loss_bwd("ln3_loss_bwd", h2, f2, loss_target.reshape(t, dm),
                                                                   p['ln3_g'], p['ln3_b'])
    loss = lax.psum(jnp.sum(sq) * (0.5 / dm), AXES)

    dh2, full['ffn2_gate'], full['ffn2_up'], full['ffn2_down'] = ffn_bwd("ffn2", res3, p['ffn2_gate'], p['ffn2_up'],
                                                                       p['ffn2_down'], df2, dh2_res)
    dh1_res, dmix, small['ln2_g'], small['ln2_b'] = resid_ln_bwd("ln2_bwd", 1.0, h1, mix, p['ln2_g'], p['ln2_b'], dh2)
    dh1, gm = mixer_bwd(resm, p, dmix, dh1_res, b)
    for k in ('w_in', 'w_out', 'conv_w'):
        full[k] = gm[k]
    for k in ('conv_b', 'dt_bias', 'a_log', 'd_skip', 'attn_norm_w', 'ssd_norm_w'):
        small[k] = gm[k]
    dx_res, df1, small['ln1_g'], small['ln1_b'] = resid_ln_bwd("ln1_bwd", 0.5, x2, f1, p['ln1_g'], p['ln1_b'], dh1)
    dx, full['ffn1_gate'], full['ffn1_up'], full['ffn1_down'] = ffn_bwd("ffn1", res1, p['ffn1_gate'], p['ffn1_up'],
                                                                      p['ffn1_down'], df1, dx_res)

    keys = SHARDED + (SMALL,)
    small_part = to_comm(SMALL, small, shapes)
    send = [grad_shards(k, full[k]) for k in SHARDED] + [jnp.broadcast_to(small_part[None], (N_DEV,) + small_part.shape)]
    recv = all_to_all(send)
    outs = [{}, {}, {}, {}]
    for k, r in zip(keys, recv):
        tm = ADAMW_TM.get(k, r.shape[1])
        res = adamw(f"adamw_{k}", r, w_comm[k], to_comm(k, ml, shapes), to_comm(k, vl, shapes), tm)
        for o, a in zip(outs, res):
            o.update(from_comm(k, a, shapes))
    return (loss, dx.reshape(b, s, dm), *[o[k] for o in outs for k in WEIGHTS])
```

```python
import functools
import math

import jax
import jax.numpy as jnp
import numpy as np
from jax import lax
from jax.experimental import pallas as pl
from jax.experimental.pallas import tpu as pltpu

F32, BF16 = jnp.float32, jnp.bfloat16
HI = lax.Precision.HIGHEST
MESH = pl.DeviceIdType.MESH
AXES = ("x", "y", "c")
N_DEV = 8

D_MODEL = 1024
SEQ = 2048
HEAD_DIM = 64
N_HEADS = 12
D_ATTN = N_HEADS * HEAD_DIM
DILATIONS = (1, 4, 16)
ATTN_BLOCK = 128
ROPE_THETA = 500000.0
ROPE_DIM = 16
D_SSD = 768
SSD_GROUPS = 4
SSD_STATE = 128
SSD_CHUNK = 128
D_BC = SSD_GROUPS * SSD_STATE
D_CONV = D_SSD + 2 * D_BC
CONV_WIDTH = 4
D_QKVZ = 3 * D_ATTN + D_SSD
D_IN_PROJ = D_QKVZ + D_CONV + N_HEADS
D_FF = 2816
ALPHA = 2.0 ** 0.25
LN_EPS = 1e-5
RMS_EPS = 1e-6
ADAM_LR, ADAM_B1, ADAM_B2, ADAM_EPS, ADAM_WD, ADAM_STEP = 0.001, 0.9, 0.999, 1e-08, 0.01, 10

LANES = 128
VMEM_LIMIT = 52 * 1024 * 1024
NEG = -1e30

WEIGHTS = ['ln1_g', 'ln1_b', 'ffn1_gate', 'ffn1_up', 'ffn1_down', 'w_in', 'conv_w', 'conv_b', 'dt_bias', 'a_log',
           'd_skip', 'attn_norm_w', 'ssd_norm_w', 'w_out', 'ln2_g', 'ln2_b', 'ffn2_gate', 'ffn2_up', 'ffn2_down',
           'ln3_g', 'ln3_b']
COL_SHARDED = ('ffn1_gate', 'ffn1_up', 'conv_w', 'ffn2_gate', 'ffn2_up')
ROW_SHARDED = ('ffn1_down', 'w_in', 'w_out', 'ffn2_down')
SHARDED = tuple(n for n in WEIGHTS if n in COL_SHARDED or n in ROW_SHARDED)
REPLICATED = tuple(n for n in WEIGHTS if n not in SHARDED)
FF_SHARD = D_FF // N_DEV
FF_PAD = -(-FF_SHARD // LANES) * LANES
D_FF_INT = N_DEV * FF_PAD


def _cparams(sem=None):
    return pltpu.CompilerParams(dimension_semantics=sem, vmem_limit_bytes=VMEM_LIMIT)


def _tile(n, prefs):
    for p in prefs:
        if n % p == 0:
            return p
    return n


class Op:
    def __init__(self, arr, bw=None, cb=0, ro=0):
        self.arr, self.bw, self.cb, self.ro = arr, (arr.shape[1] if bw is None else bw), cb, ro


def _op(a):
    return a if isinstance(a, Op) else Op(a)


def rowwise(name, fn, ins, consts, outs, accs=(), tm=256):
    ins = [_op(a) for a in ins]
    rows = outs[0][0]
    n_in, n_c, n_o, n_a = len(ins), len(consts), len(outs), len(accs)
    tm = min(tm, rows)
    assert rows % tm == 0, (name, rows, tm)

    def body(*refs):
        vals = [r[...].astype(F32) for r in refs[:n_in + n_c]]
        res = fn(*vals)
        res = res if isinstance(res, (tuple, list)) else (res,)
        o_refs = refs[n_in + n_c:n_in + n_c + n_o]
        a_refs = refs[n_in + n_c + n_o:]
        for r, v in zip(o_refs, res[:n_o]):
            r[...] = v.astype(r.dtype)
        if n_a:
            @pl.when(pl.program_id(0) == 0)
            def _():
                for r in a_refs:
                    r[...] = jnp.zeros(r.shape, r.dtype)
            for r, v in zip(a_refs, res[n_o:]):
                r[...] += v

    in_specs = [pl.BlockSpec((tm, o.bw), functools.partial(lambda i, o: (i + o.ro, o.cb), o=o)) for o in ins]
    in_specs += [pl.BlockSpec(c.shape, functools.partial(lambda i, nd: (0,) * nd, nd=c.ndim)) for c in consts]
    out_specs = [pl.BlockSpec((tm, w), lambda i: (i, 0)) for (_, w, _) in outs]
    out_specs += [pl.BlockSpec(s, functools.partial(lambda i, nd: (0,) * nd, nd=len(s))) for s in accs]
    out_shape = [jax.ShapeDtypeStruct((r, w), dt) for (r, w, dt) in outs]
    out_shape += [jax.ShapeDtypeStruct(s, F32) for s in accs]
    res = pl.pallas_call(
        body, name=name, grid=(rows // tm,), in_specs=in_specs, out_specs=out_specs, out_shape=out_shape,
        compiler_params=_cparams(("arbitrary",) if n_a else ("parallel",)),
    )(*[o.arr for o in ins], *consts)
    return res


MM_TM = 512
MM_TN = (1024, 896, 768, 512, 256, 128)
_NT = (((1,), (1,)), ((), ()))
_NN = (((1,), (0,)), ((), ()))
_TN = (((0,), (0,)), ((), ()))


def _dot(a, b, dn, precision=None):
    return lax.dot_general(a, b, dn, preferred_element_type=F32, precision=precision)


def _mm_specs(name, pairs, n_out, tm, tn):
    in_specs, args = [], []
    for a, b, mode in pairs:
        o = _op(a)
        in_specs.append(pl.BlockSpec((tm, o.bw), functools.partial(lambda j, i, o: (i, o.cb), o=o)))
        args.append(o.arr)
        if mode == 'nn':
            assert b.shape == (o.bw, n_out), (name, b.shape, o.bw, n_out)
            in_specs.append(pl.BlockSpec((o.bw, tn), lambda j, i: (0, j)))
        else:
            assert b.shape == (n_out, o.bw), (name, b.shape, o.bw, n_out)
            in_specs.append(pl.BlockSpec((tn, o.bw), lambda j, i: (j, 0)))
        args.append(b)
    return in_specs, args


def _mm_acc(refs, pairs):
    acc = None
    for k, (_, _, mode) in enumerate(pairs):
        d = _dot(refs[2 * k][...].astype(BF16), refs[2 * k + 1][...].astype(BF16), _NN if mode == 'nn' else _NT)
        acc = d if acc is None else acc + d
    return acc


def mm(name, pairs, n_out, add=None, out_dtype=F32, tm=MM_TM, tn=None):
    m = _op(pairs[0][0]).arr.shape[0]
    tn = tn or _tile(n_out, MM_TN)
    n_p = len(pairs)

    def body(*refs):
        acc = _mm_acc(refs, pairs)
        if add is not None:
            acc = acc + refs[2 * n_p][...]
        refs[-1][...] = acc.astype(refs[-1].dtype)

    in_specs, args = _mm_specs(name, pairs, n_out, tm, tn)
    tile = pl.BlockSpec((tm, tn), lambda j, i: (i, j))
    if add is not None:
        in_specs.append(tile)
        args.append(add)
    return pl.pallas_call(
        body, name=name, grid=(n_out // tn, m // tm), in_specs=in_specs, out_specs=tile,
        out_shape=jax.ShapeDtypeStruct((m, n_out), out_dtype),
        compiler_params=_cparams(("parallel", "parallel")),
    )(*args)


def mm_tn(name, a, b, out_dtype=F32, tt=1024):
    a, b = _op(a), _op(b)
    t = a.arr.shape[0]
    k, n = a.bw, b.bw
    tk = _tile(k, (512, 896, 768, 256, 128))
    tn = _tile(n, MM_TN)
    tt = min(tt, t)
    n_t = t // tt

    def body(a_ref, b_ref, o_ref, acc_ref):
        s = pl.program_id(2)
        d = _dot(a_ref[...].astype(BF16), b_ref[...].astype(BF16), _TN)

        @pl.when(s == 0)
        def _():
            acc_ref[...] = d

        @pl.when(s > 0)
        def _():
            acc_ref[...] += d

        @pl.when(s == n_t - 1)
        def _():
            o_ref[...] = acc_ref[...].astype(o_ref.dtype)

    return pl.pallas_call(
        body, name=name, grid=(k // tk, n // tn, n_t),
        in_specs=[pl.BlockSpec((tt, tk), functools.partial(lambda kk, nn, s, o: (s, o.cb * (o.bw // tk) + kk), o=a)),
                  pl.BlockSpec((tt, tn), functools.partial(lambda kk, nn, s, o: (s, o.cb * (o.bw // tn) + nn), o=b))],
        out_specs=pl.BlockSpec((tk, tn), lambda kk, nn, s: (kk, nn)),
        out_shape=jax.ShapeDtypeStruct((k, n), out_dtype),
        scratch_shapes=[pltpu.VMEM((tk, tn), F32)],
        compiler_params=_cparams(("parallel", "parallel", "arbitrary")),
    )(a.arr, b.arr)


def _sigmoid(x):
    return 1.0 / (1.0 + jnp.exp(-x))


def _silu(x):
    return x * _sigmoid(x)


def _softplus(x):
    return jnp.maximum(x, 0.0) + jnp.log(1.0 + jnp.exp(-jnp.abs(x)))


def _act(g, u):
    return _silu(g) * u


def _resid_ln(scale, h, branch, g, b):
    r = ALPHA * h + scale * branch
    mu = jnp.mean(r, axis=-1, keepdims=True)
    var = jnp.mean(jnp.square(r - mu), axis=-1, keepdims=True)
    return (r - mu) * lax.rsqrt(var + LN_EPS) * g + b


def _rms(t, w):
    return t * lax.rsqrt(jnp.mean(t * t, axis=-1, keepdims=True) + RMS_EPS) * w


def _branch_weights(l1, l2, l3):
    m = jnp.maximum(jnp.maximum(l1, l2), l3)
    e1, e2, e3 = jnp.exp(l1 - m), jnp.exp(l2 - m), jnp.exp(l3 - m)
    inv = 1.0 / (e1 + e2 + e3)
    return e1 * inv, e2 * inv, e3 * inv


def _gate(y, xs, z, dskip, w):
    return _rms((y + dskip * xs) * _silu(z), w)


def _rot(x):
    d = lax.broadcasted_iota(jnp.int32, x.shape, 1) % HEAD_DIM
    up = pltpu.roll(x, x.shape[1] - ROPE_DIM // 2, 1)
    down = jnp.where(d < ROPE_DIM, pltpu.roll(x, ROPE_DIM // 2, 1), 0.0)
    return jnp.where(d < ROPE_DIM // 2, up, down)


def ffn_gate_up(name, h, wg, wu):
    m, nf = h.shape[0], wg.shape[1]
    tn = _tile(nf, MM_TN)

    def body(h_ref, g_w, u_w, g_ref, u_ref, a_ref):
        hb = h_ref[...].astype(BF16)
        g = _dot(hb, g_w[...].astype(BF16), _NN)
        u = _dot(hb, u_w[...].astype(BF16), _NN)
        g_ref[...] = g.astype(g_ref.dtype)
        u_ref[...] = u.astype(u_ref.dtype)
        a_ref[...] = _act(g, u).astype(a_ref.dtype)

    in_specs, args = _mm_specs(name, [(h, wg, 'nn')], nf, MM_TM, tn)
    in_specs.append(in_specs[1])
    tile = pl.BlockSpec((MM_TM, tn), lambda j, i: (i, j))
    return pl.pallas_call(
        body, name=name, grid=(nf // tn, m // MM_TM), in_specs=in_specs, out_specs=[tile] * 3,
        out_shape=[jax.ShapeDtypeStruct((m, nf), BF16)] * 3, compiler_params=_cparams(("parallel", "parallel")),
    )(*args, wu)


def ffn_da_act(name, df, wd, g, u):
    m, nf = df.shape[0], wd.shape[0]
    tn = _tile(nf, MM_TN)

    def body(df_ref, w_ref, g_ref, u_ref, dg_ref, du_ref):
        da = _dot(df_ref[...].astype(BF16), w_ref[...].astype(BF16), _NT)
        _, vjp = jax.vjp(_act, g_ref[...].astype(F32), u_ref[...].astype(F32))
        dg, du = vjp(da)
        dg_ref[...] = dg.astype(dg_ref.dtype)
        du_ref[...] = du.astype(du_ref.dtype)

    in_specs, args = _mm_specs(name, [(df, wd, 'nt')], nf, MM_TM, tn)
    tile = pl.BlockSpec((MM_TM, tn), lambda j, i: (i, j))
    return pl.pallas_call(
        body, name=name, grid=(nf // tn, m // MM_TM), in_specs=in_specs + [tile, tile], out_specs=[tile] * 2,
        out_shape=[jax.ShapeDtypeStruct((m, nf), BF16)] * 2, compiler_params=_cparams(("parallel", "parallel")),
    )(*args, g, u)


def resid_ln_fwd(name, scale, h, branch, ln_g, ln_b):
    t = h.shape[0]

    def fn(*a):
        y = _resid_ln(scale, *a)
        return y, y

    return rowwise(name, fn, [h, branch], [ln_g, ln_b], [(t, D_MODEL, F32), (t, D_MODEL, BF16)], tm=512)


def ffn_fwd(tag, hb, wg, wu, wd):
    g, u, a = ffn_gate_up(f"{tag}_gate_up", hb, wg, wu)
    f = mm(f"{tag}_down", [(a, wd, 'nn')], D_MODEL)
    return f, (hb, g, u, a)


def ln_loss_bwd(name, h, branch, target, ln_g, ln_b):
    t, dm = h.shape

    def fn(h_, br_, tgt, g_, b_):
        y, vjp = jax.vjp(functools.partial(_resid_ln, 0.5), h_, br_, g_, b_)
        e = y - tgt
        return (*vjp(e * (1.0 / dm)), jnp.sum(e * e, axis=0, keepdims=True))

    return rowwise(name, fn, [h, branch, target], [ln_g, ln_b], [(t, dm, F32), (t, dm, F32)],
                   accs=[(1, dm), (1, dm), (1, dm)], tm=512)


def resid_ln_bwd(name, scale, h, branch, ln_g, ln_b, dout, extra=None):
    t = h.shape[0]

    def fn(h_, br_, do_, *rest):
        g_, b_ = rest[-2], rest[-1]
        _, vjp = jax.vjp(functools.partial(_resid_ln, scale), h_, br_, g_, b_)
        dh, dbr, dg, db = vjp(do_)
        if extra is not None:
            dh = dh + rest[0]
        return dh, dbr, dg, db

    ins = [h, branch, dout] + ([extra] if extra is not None else [])
    return rowwise(name, fn, ins, [ln_g, ln_b], [(t, D_MODEL, F32), (t, D_MODEL, F32)],
                   accs=[(1, D_MODEL), (1, D_MODEL)], tm=512)


def ffn_bwd(tag, res, wg, wu, wd, df, dh_resid):
    hb, g, u, a = res
    dg, du = ffn_da_act(f"{tag}_bwd_da_act", df, wd, g, u)
    dwd = mm_tn(f"{tag}_bwd_dwd", a, df, BF16)
    dh = mm(f"{tag}_bwd_dh", [(dg, wg, 'nt'), (du, wu, 'nt')], D_MODEL, add=dh_resid, tn=512)
    dwg = mm_tn(f"{tag}_bwd_dwg", hb, dg, BF16)
    dwu = mm_tn(f"{tag}_bwd_dwu", hb, du, BF16)
    return dh, dwg, dwu, dwd


def rope_tables(positions):
    inv_freq = ROPE_THETA ** (-jnp.arange(0, ROPE_DIM, 2, dtype=F32) / ROPE_DIM)
    ang = positions.reshape(-1, 1).astype(F32) * inv_freq
    c, s = jnp.cos(ang), jnp.sin(ang)
    t = ang.shape[0]
    cosv = jnp.concatenate([c, c, jnp.ones((t, HEAD_DIM - ROPE_DIM), F32)], axis=1)
    sinv = jnp.concatenate([-s, s, jnp.zeros((t, HEAD_DIM - ROPE_DIM), F32)], axis=1)
    return jnp.tile(cosv, (1, 2)), jnp.tile(sinv, (1, 2))


def _pair_masks():
    lane = lax.broadcasted_iota(jnp.int32, (1, LANES), 1)
    return (lane < HEAD_DIM, lane >= HEAD_DIM)


def _band_masks():
    row = lax.broadcasted_iota(jnp.int32, (ATTN_BLOCK, ATTN_BLOCK), 0)
    col = lax.broadcasted_iota(jnp.int32, (ATTN_BLOCK, ATTN_BLOCK), 1)
    return col >= row, col <= row


def _residue_blocks():
    out = []
    for g, d in enumerate(DILATIONS):
        for r in range(d):
            for i in range(SEQ // d // ATTN_BLOCK):
                rows = lambda j: pl.ds(r + j * ATTN_BLOCK * d, ATTN_BLOCK, stride=d) if d > 1 else pl.ds(j * ATTN_BLOCK, ATTN_BLOCK)
                out.append((g, rows(i), rows(i - 1) if i > 0 else None))
    return out


N_HEAD_PAIRS = D_ATTN // LANES
SCALE = HEAD_DIM ** -0.5
ATTN_GROUP = 4


def _block_operands(qr, kr, v_ref, cur, prev):
    prev_ok, cur_ok = _band_masks()
    if prev is None:
        return qr[cur, :], kr[cur, :].astype(BF16), v_ref[cur, :], cur_ok
    kcat = jnp.concatenate([kr[prev, :], kr[cur, :]], axis=0).astype(BF16)
    vcat = jnp.concatenate([v_ref[prev, :], v_ref[cur, :]], axis=0)
    return qr[cur, :], kcat, vcat, jnp.concatenate([prev_ok, cur_ok], axis=1)


def _attn_specs(b):
    col = lambda cb: pl.BlockSpec((SEQ, LANES), lambda bb, hp: (bb, cb + hp))
    tab = pl.BlockSpec((SEQ, LANES), lambda bb, hp: (bb, 0))
    return col, tab


def attn_fwd(qkvz, cosv, sinv, b):
    t = qkvz.shape[0]
    col, tab = _attn_specs(b)
    blocks = _residue_blocks()

    def body(q_ref, k_ref, v_ref, c_ref, s_ref, o_ref, l1_ref, l2_ref, l3_ref, qr, kr, o1, o2, o3):
        l_refs, o_scr = (l1_ref, l2_ref, l3_ref), (o1, o2, o3)
        c, s = c_ref[...], s_ref[...]
        q, k = q_ref[...], k_ref[...]
        qr[...] = q * c + _rot(q) * s
        kr[...] = k * c + _rot(k) * s
        masks = _pair_masks()
        for lo in range(0, len(blocks), ATTN_GROUP):
            chains = []
            for g, cur, prev in blocks[lo:lo + ATTN_GROUP]:
                q2, kcat, vcat, ok = _block_operands(qr, kr, v_ref, cur, prev)
                for m in masks:
                    qm = jnp.where(m, q2, 0.0).astype(BF16)
                    chains.append(dict(g=g, cur=cur, m=m, v=jnp.where(m, vcat, 0.0).astype(BF16),
                                       s=jnp.where(ok, _dot(qm, kcat, _NT) * SCALE, NEG)))
            for ch in chains:
                mx = jnp.max(ch['s'], axis=1, keepdims=True)
                p = jnp.exp(ch['s'] - mx)
                den = jnp.sum(p, axis=1, keepdims=True)
                ch.update(p=p.astype(BF16), inv=1.0 / den, lse=mx + jnp.log(den))
            for ch in chains:
                ch['o'] = _dot(ch['p'], ch['v'], _NN) * ch['inv']
            for c0, c1 in zip(chains[0::2], chains[1::2]):
                o_scr[c0['g']][c0['cur'], :] = c0['o'] + c1['o']
                l_refs[c0['g']][c0['cur'], :] = jnp.where(c0['m'], c0['lse'], c1['lse'])
        w1, w2, w3 = _branch_weights(l1_ref[...], l2_ref[...], l3_ref[...])
        o_ref[...] = w1 * o1[...] + w2 * o2[...] + w3 * o3[...]

    shp = jax.ShapeDtypeStruct((t, D_ATTN), F32)
    return pl.pallas_call(
        body, name="attn_fwd", grid=(b, N_HEAD_PAIRS),
        in_specs=[col(0), col(N_HEAD_PAIRS), col(2 * N_HEAD_PAIRS), tab, tab],
        out_specs=[col(0)] * 4, out_shape=[shp] * 4,
        scratch_shapes=[pltpu.VMEM((SEQ, LANES), F32)] * 5,
        compiler_params=_cparams(("parallel", "parallel")),
    )(qkvz, qkvz, qkvz, cosv, sinv)


def attn_bwd(qkvz, cosv, sinv, dmix, mixed, lses, b):
    t = qkvz.shape[0]
    col, tab = _attn_specs(b)
    blocks = _residue_blocks()
    hd = np.arange(LANES) // HEAD_DIM
    head_ones = jnp.asarray((hd[:, None] == hd[None, :]).astype(np.float32))

    def body(q_ref, k_ref, v_ref, c_ref, s_ref, dm_ref, mx_ref, l1_ref, l2_ref, l3_ref, ones_ref,
             dq_out, dk_out, dv_out, qr, kr, do1, do2, do3, dd1, dd2, dd3, dq_ref, dk_ref, dv_ref):
        l_refs, do_scr, dd_scr = (l1_ref, l2_ref, l3_ref), (do1, do2, do3), (dd1, dd2, dd3)
        c, s = c_ref[...], s_ref[...]
        q, k = q_ref[...], k_ref[...]
        qr[...] = q * c + _rot(q) * s
        kr[...] = k * c + _rot(k) * s
        dm = dm_ref[...]
        tot = _dot(dm * mx_ref[...], ones_ref[...], _NN, HI)
        for w, do_g, dd_g in zip(_branch_weights(l1_ref[...], l2_ref[...], l3_ref[...]), do_scr, dd_scr):
            do_g[...] = w * dm
            dd_g[...] = w * tot
        dq_ref[...] = jnp.zeros((SEQ, LANES), F32)
        dk_ref[...] = jnp.zeros((SEQ, LANES), F32)
        dv_ref[...] = jnp.zeros((SEQ, LANES), F32)
        masks = _pair_masks()
        for lo in range(0, len(blocks), ATTN_GROUP):
            chains = []
            for g, cur, prev in blocks[lo:lo + ATTN_GROUP]:
                q2, kcat, vcat, ok = _block_operands(qr, kr, v_ref, cur, prev)
                vcat = vcat.astype(BF16)
                do2_, l2, dd2_ = do_scr[g][cur, :], l_refs[g][cur, :], dd_scr[g][cur, :]
                l2s, dd2s = pltpu.roll(l2, HEAD_DIM, 1), pltpu.roll(dd2_, HEAD_DIM, 1)
                for m in masks:
                    qm = jnp.where(m, q2, 0.0).astype(BF16)
                    dom = jnp.where(m, do2_, 0.0).astype(BF16)
                    lrep, ddrep = jnp.where(m, l2, l2s), jnp.where(m, dd2_, dd2s)
                    if prev is not None:
                        lrep, ddrep = jnp.concatenate([lrep, lrep], axis=1), jnp.concatenate([ddrep, ddrep], axis=1)
                    chains.append(dict(cur=cur, prev=prev, qm=qm, dom=dom, km=jnp.where(m, kcat, 0), lrep=lrep, ddrep=ddrep,
                                       s=jnp.where(ok, _dot(qm, kcat, _NT) * SCALE, NEG), dp=_dot(dom, vcat, _NT)))
            for ch in chains:
                p = jnp.exp(ch['s'] - ch['lrep'])
                ch.update(p=p.astype(BF16), ds=(p * (ch['dp'] - ch['ddrep']) * SCALE).astype(BF16))
            for ch in chains:
                ch.update(dq=_dot(ch['ds'], ch['km'], _NN), dk=_dot(ch['ds'], ch['qm'], _TN), dv=_dot(ch['p'], ch['dom'], _TN))
            for c0, c1 in zip(chains[0::2], chains[1::2]):
                cur, prev = c0['cur'], c0['prev']
                dk, dv = c0['dk'] + c1['dk'], c0['dv'] + c1['dv']
                dq_ref[cur, :] += c0['dq'] + c1['dq']
                if prev is None:
                    dk_ref[cur, :] += dk
                    dv_ref[cur, :] += dv
                else:
                    dk_ref[prev, :] += dk[:ATTN_BLOCK]
                    dv_ref[prev, :] += dv[:ATTN_BLOCK]
                    dk_ref[cur, :] += dk[ATTN_BLOCK:]
                    dv_ref[cur, :] += dv[ATTN_BLOCK:]
        dq, dk = dq_ref[...], dk_ref[...]
        dq_out[...] = (dq * c + _rot(dq * s)).astype(dq_out.dtype)
        dk_out[...] = (dk * c + _rot(dk * s)).astype(dk_out.dtype)
        dv_out[...] = dv_ref[...].astype(dv_out.dtype)

    shp = jax.ShapeDtypeStruct((t, D_ATTN), BF16)
    return pl.pallas_call(
        body, name="attn_bwd", grid=(b, N_HEAD_PAIRS),
        in_specs=[col(0), col(N_HEAD_PAIRS), col(2 * N_HEAD_PAIRS), tab, tab, col(0), col(0), col(0), col(0), col(0),
                  pl.BlockSpec((LANES, LANES), lambda bb, hp: (0, 0))],
        out_specs=[col(0)] * 3, out_shape=[shp] * 3,
        scratch_shapes=[pltpu.VMEM((SEQ, LANES), F32)] * 11,
        compiler_params=_cparams(("parallel", "parallel")),
    )(qkvz, qkvz, qkvz, cosv, sinv, dmix, mixed, *lses, head_ones)


def attn_norm_fwd(mixed, norm_w):
    return rowwise("attn_norm", _rms, [mixed], [norm_w], [(mixed.shape[0], D_ATTN, BF16)])[0]


def attn_norm_bwd(dout, mixed, norm_w):
    def fn(dy, mx, w):
        _, vjp = jax.vjp(_rms, mx, w)
        return vjp(dy)

    return rowwise("attn_norm_bwd", fn, [dout, mixed], [norm_w], [(dout.shape[0], D_ATTN, F32)], accs=[(1, D_ATTN)])


CONV_TM = 256
HALO = 8


def conv_fwd(u, w, bias):
    t = u.shape[0]
    tm, per_seq = CONV_TM, SEQ // CONV_TM

    def body(u_ref, h_ref, w_ref, b_ref, xs_ref, bm_ref, cm_ref, scr):
        first = pl.program_id(0) % per_seq == 0
        scr[0:HALO, :] = jnp.where(first, 0.0, h_ref[...])
        scr[HALO:, :] = u_ref[...]
        acc = b_ref[...]
        for k in range(CONV_WIDTH):
            acc = acc + w_ref[k:k + 1, :] * scr[pl.ds(HALO - CONV_WIDTH + 1 + k, tm), :]
        y = _silu(acc)
        xs_ref[...] = y[:, :D_SSD]
        bm_ref[...] = y[:, D_SSD:D_SSD + D_BC]
        cm_ref[...] = y[:, D_SSD + D_BC:]

    return pl.pallas_call(
        body, name="conv_fwd", grid=(t // tm,),
        in_specs=[pl.BlockSpec((tm, D_CONV), lambda i: (i, 0)),
                  pl.BlockSpec((HALO, D_CONV), lambda i: (jnp.maximum(i * (tm // HALO) - 1, 0), 0)),
                  pl.BlockSpec((CONV_WIDTH, D_CONV), lambda i: (0, 0)), pl.BlockSpec((1, D_CONV), lambda i: (0, 0))],
        out_specs=[pl.BlockSpec((tm, D_SSD), lambda i: (i, 0)), pl.BlockSpec((tm, D_BC), lambda i: (i, 0)),
                   pl.BlockSpec((tm, D_BC), lambda i: (i, 0))],
        out_shape=[jax.ShapeDtypeStruct((t, D_SSD), F32), jax.ShapeDtypeStruct((t, D_BC), F32),
                   jax.ShapeDtypeStruct((t, D_BC), F32)],
        scratch_shapes=[pltpu.VMEM((tm + HALO, D_CONV), F32)],
        compiler_params=_cparams(("parallel",)),
    )(u, u, w, bias)


def conv_bwd(u, w, bias, dxs_a, dxs_b, dbm, dcm):
    t = u.shape[0]
    tm, per_seq = CONV_TM, SEQ // CONV_TM
    n_tiles = t // tm

    def body1(u_ref, h_ref, dxs_ref, dxs2_ref, dbm_ref, dcm_ref, w_ref, b_ref, dz_ref, dw_ref, db_ref, scr):
        i = pl.program_id(0)
        first = i % per_seq == 0
        scr[0:HALO, :] = jnp.where(first, 0.0, h_ref[...])
        scr[HALO:, :] = u_ref[...]
        acc = b_ref[...]
        for k in range(CONV_WIDTH):
            acc = acc + w_ref[k:k + 1, :] * scr[pl.ds(HALO - CONV_WIDTH + 1 + k, tm), :]
        sig = _sigmoid(acc)
        dy = jnp.concatenate([dxs_ref[...] + dxs2_ref[...], dbm_ref[...], dcm_ref[...]], axis=1)
        dz = dy * sig * (1.0 + acc * (1.0 - sig))
        dz_ref[...] = dz

        @pl.when(i == 0)
        def _():
            dw_ref[...] = jnp.zeros(dw_ref.shape, F32)
            db_ref[...] = jnp.zeros(db_ref.shape, F32)
        db_ref[...] += jnp.sum(dz, axis=0, keepdims=True)
        for k in range(CONV_WIDTH):
            dw_ref[k:k + 1, :] += jnp.sum(dz * scr[pl.ds(HALO - CONV_WIDTH + 1 + k, tm), :], axis=0, keepdims=True)

    dz, dw, db = pl.pallas_call(
        body1, name="conv_bwd_dz", grid=(n_tiles,),
        in_specs=[pl.BlockSpec((tm, D_CONV), lambda i: (i, 0)),
                  pl.BlockSpec((HALO, D_CONV), lambda i: (jnp.maximum(i * (tm // HALO) - 1, 0), 0)),
                  pl.BlockSpec((tm, D_SSD), lambda i: (i, 0)), pl.BlockSpec((tm, D_SSD), lambda i: (i, 0)),
                  pl.BlockSpec((tm, D_BC), lambda i: (i, 0)), pl.BlockSpec((tm, D_BC), lambda i: (i, 0)),
                  pl.BlockSpec((CONV_WIDTH, D_CONV), lambda i: (0, 0)), pl.BlockSpec((1, D_CONV), lambda i: (0, 0))],
        out_specs=[pl.BlockSpec((tm, D_CONV), lambda i: (i, 0)), pl.BlockSpec((CONV_WIDTH, D_CONV), lambda i: (0, 0)),
                   pl.BlockSpec((1, D_CONV), lambda i: (0, 0))],
        out_shape=[jax.ShapeDtypeStruct((t, D_CONV), F32), jax.ShapeDtypeStruct((CONV_WIDTH, D_CONV), F32),
                   jax.ShapeDtypeStruct((1, D_CONV), F32)],
        scratch_shapes=[pltpu.VMEM((tm + HALO, D_CONV), F32)],
        compiler_params=_cparams(("arbitrary",)),
    )(u, u, dxs_a, dxs_b, dbm, dcm, w, bias)

    def body2(dz_ref, n_ref, w_ref, du_ref, scr):
        last = pl.program_id(0) % per_seq == per_seq - 1
        scr[0:tm, :] = dz_ref[...]
        scr[tm:, :] = jnp.where(last, 0.0, n_ref[...])
        acc = jnp.zeros((tm, D_CONV), F32)
        for k in range(CONV_WIDTH):
            acc = acc + w_ref[k:k + 1, :] * scr[pl.ds(CONV_WIDTH - 1 - k, tm), :]
        du_ref[...] = acc.astype(du_ref.dtype)

    du = pl.pallas_call(
        body2, name="conv_bwd_du", grid=(n_tiles,),
        in_specs=[pl.BlockSpec((tm, D_CONV), lambda i: (i, 0)),
                  pl.BlockSpec((HALO, D_CONV), lambda i: (jnp.minimum((i + 1) * (tm // HALO), t // HALO - 1), 0)),
                  pl.BlockSpec((CONV_WIDTH, D_CONV), lambda i: (0, 0))],
        out_specs=pl.BlockSpec((tm, D_CONV), lambda i: (i, 0)),
        out_shape=jax.ShapeDtypeStruct((t, D_CONV), BF16),
        scratch_shapes=[pltpu.VMEM((tm + HALO, D_CONV), F32)],
        compiler_params=_cparams(("parallel",)),
    )(dz, dz, w)
    return du, dw, db


Q = SSD_CHUNK
N_PAIRS = D_SSD // LANES
HEADS_PER_GROUP = N_HEADS // SSD_GROUPS


def _rep(a, j):
    return jnp.broadcast_to(a[:, j:j + 1], a.shape)


def _dot_exact01(a, b, dn, a_is_01):
    x = b if a_is_01 else a
    hi = x.astype(BF16)
    mid = (x - hi.astype(F32)).astype(BF16)
    lo = (x - hi.astype(F32) - mid.astype(F32)).astype(BF16)
    z = a.astype(BF16) if a_is_01 else b.astype(BF16)
    out = None
    for term in (hi, mid, lo):
        d = _dot(z, term, dn) if a_is_01 else _dot(term, z, dn)
        out = d if out is None else out + d
    return out


def _pad_lanes(v, fill=0.0):
    row = jnp.pad(v.reshape(1, -1).astype(F32), ((0, 0), (0, LANES - v.size)), constant_values=fill)
    return row, row.reshape(LANES, 1)


def _ssd_common(dtr_ref, dtrt_ref, bias_r, bias_c, alog_r, alog_c):
    row = lax.broadcasted_iota(jnp.int32, (Q, Q), 0)
    col = lax.broadcasted_iota(jnp.int32, (Q, Q), 1)
    tril = row >= col
    lane = lax.broadcasted_iota(jnp.int32, (1, LANES), 1)
    a_r = jnp.where(lane < N_HEADS, -jnp.exp(alog_r[...]), 0.0)
    sub = lax.broadcasted_iota(jnp.int32, (LANES, 1), 0)
    a_c = jnp.where(sub < N_HEADS, -jnp.exp(alog_c[...]), 0.0)
    dt = _softplus(dtr_ref[...] + bias_r[...])
    cs = _dot_exact01(tril, dt * a_r, _NN, True)
    dtt = _softplus(dtrt_ref[...] + bias_c[...])
    cst = _dot_exact01(dtt * a_c, row <= col, _NN, False)
    return tril, lane, a_r, dt, cs, cst


def _ssd_specs(b, nc, rev):
    ci = (lambda c: nc - 1 - c) if rev else (lambda c: c)
    rows = lambda w: pl.BlockSpec((Q, w), lambda bb, c: (bb * nc + ci(c), 0))
    dtt = pl.BlockSpec((LANES, Q), lambda bb, c: (0, bb * nc + ci(c)))
    const = lambda s: pl.BlockSpec(s, lambda bb, c: (0,) * len(s))
    state = pl.BlockSpec((None, N_PAIRS, LANES, SSD_STATE), lambda bb, c: (bb * nc + ci(c), 0, 0, 0))
    return rows, dtt, const, state


def ssd_fwd(xs, bm, cm, dtraw, dt_bias, a_log, b):
    t = xs.shape[0]
    nc = SEQ // Q
    rows, dtt_spec, const, state = _ssd_specs(b, nc, False)
    bias_r, bias_c = _pad_lanes(dt_bias)
    alog_r, alog_c = _pad_lanes(a_log)

    def body(xs_ref, b_ref, c_ref, dtr_ref, dtrt_ref, br, bc, ar, ac, y_ref, hp_ref, h_scr):
        @pl.when(pl.program_id(1) == 0)
        def _():
            h_scr[...] = jnp.zeros(h_scr.shape, F32)
        tril, lane, _, dt, cs, cst = _ssd_common(dtr_ref, dtrt_ref, br, bc, ar, ac)
        sub = lax.broadcasted_iota(jnp.int32, (LANES, 1), 0)
        y_acc = [jnp.zeros((Q, LANES), F32) for _ in range(N_PAIRS)]
        h_old = [h_scr[p] for p in range(N_PAIRS)]
        h_new = [jnp.zeros((LANES, SSD_STATE), F32) for _ in range(N_PAIRS)]
        for g in range(SSD_GROUPS):
            bg = b_ref[:, g * SSD_STATE:(g + 1) * SSD_STATE].astype(BF16)
            cg = c_ref[:, g * SSD_STATE:(g + 1) * SSD_STATE].astype(BF16)
            cb = _dot(cg, bg, _NT)
            for j in range(g * HEADS_PER_GROUP, (g + 1) * HEADS_PER_GROUP):
                p, side = j // 2, j % 2
                m = (lane < HEAD_DIM) if side == 0 else (lane >= HEAD_DIM)
                ms = (sub < HEAD_DIM) if side == 0 else (sub >= HEAD_DIM)
                csj, dtj = _rep(cs, j), _rep(dt, j)
                lmat = jnp.exp(jnp.where(tril, csj - cst[j:j + 1, :], NEG))
                xdt = jnp.where(m, xs_ref[:, p * LANES:(p + 1) * LANES] * dtj, 0.0)
                hm = jnp.where(ms, h_old[p], 0.0)
                ydiag = _dot((cb * lmat).astype(BF16), xdt.astype(BF16), _NN)
                yoff = jnp.exp(csj) * _dot(cg, hm.astype(BF16), _NT)
                y_acc[p] = y_acc[p] + ydiag + yoff
                last = csj[Q - 1:Q, :]
                sj = _dot((xdt * jnp.exp(last - csj)).astype(BF16), bg, _TN)
                h_new[p] = h_new[p] + jnp.exp(last) * hm + sj
        for p in range(N_PAIRS):
            y_ref[:, p * LANES:(p + 1) * LANES] = y_acc[p]
            hp_ref[p] = h_old[p]
            h_scr[p] = h_new[p]

    return pl.pallas_call(
        body, name="ssd_fwd", grid=(b, nc),
        in_specs=[rows(D_SSD), rows(D_BC), rows(D_BC), rows(LANES), dtt_spec, const((1, LANES)), const((LANES, 1)),
                  const((1, LANES)), const((LANES, 1))],
        out_specs=[rows(D_SSD), state],
        out_shape=[jax.ShapeDtypeStruct((t, D_SSD), F32),
                   jax.ShapeDtypeStruct((b * nc, N_PAIRS, LANES, SSD_STATE), F32)],
        scratch_shapes=[pltpu.VMEM((N_PAIRS, LANES, SSD_STATE), F32)],
        compiler_params=_cparams(("parallel", "arbitrary")),
    )(xs, bm, cm, dtraw, dtraw.T, bias_r, bias_c, alog_r, alog_c)


def ssd_bwd(xs, bm, cm, dtraw, dt_bias, a_log, hprev, dy, b):
    t = xs.shape[0]
    nc = SEQ // Q
    rows, dtt_spec, const, state = _ssd_specs(b, nc, True)
    bias_r, bias_c = _pad_lanes(dt_bias)
    alog_r, alog_c = _pad_lanes(a_log)

    def body(xs_ref, b_ref, c_ref, dtr_ref, dtrt_ref, hp_ref, dy_ref, br, bc, ar, ac,
             dxs_ref, db_ref, dc_ref, ddt_ref, dbias_ref, dalog_ref, dh_scr):
        first = jnp.logical_and(pl.program_id(0) == 0, pl.program_id(1) == 0)

        @pl.when(pl.program_id(1) == 0)
        def _():
            dh_scr[...] = jnp.zeros(dh_scr.shape, F32)

        @pl.when(first)
        def _():
            dbias_ref[...] = jnp.zeros(dbias_ref.shape, F32)
            dalog_ref[...] = jnp.zeros(dalog_ref.shape, F32)
        tril, lane, a_r, dt, cs, cst = _ssd_common(dtr_ref, dtrt_ref, br, bc, ar, ac)
        sub = lax.broadcasted_iota(jnp.int32, (LANES, 1), 0)
        rowq = lax.broadcasted_iota(jnp.int32, (Q, 1), 0)
        triu = (lax.broadcasted_iota(jnp.int32, (Q, Q), 0) <= lax.broadcasted_iota(jnp.int32, (Q, Q), 1)).astype(F32)
        dxs_acc = [jnp.zeros((Q, LANES), F32) for _ in range(N_PAIRS)]
        dh_in = [dh_scr[p] for p in range(N_PAIRS)]
        h_in = [hp_ref[p] for p in range(N_PAIRS)]
        dh_out = [jnp.zeros((LANES, SSD_STATE), F32) for _ in range(N_PAIRS)]
        ddt = jnp.zeros((Q, LANES), F32)
        dalog = jnp.zeros((1, LANES), F32)
        for g in range(SSD_GROUPS):
            gs = slice(g * SSD_STATE, (g + 1) * SSD_STATE)
            bg, cg = b_ref[:, gs].astype(BF16), c_ref[:, gs].astype(BF16)
            cb = _dot(cg, bg, _NT)
            dcb = jnp.zeros((Q, Q), F32)
            dbg = jnp.zeros((Q, SSD_STATE), F32)
            dcg = jnp.zeros((Q, SSD_STATE), F32)
            for j in range(g * HEADS_PER_GROUP, (g + 1) * HEADS_PER_GROUP):
                p, side = j // 2, j % 2
                m = (lane < HEAD_DIM) if side == 0 else (lane >= HEAD_DIM)
                ms = (sub < HEAD_DIM) if side == 0 else (sub >= HEAD_DIM)
                csj, dtj = _rep(cs, j), _rep(dt, j)
                lmat = jnp.exp(jnp.where(tril, csj - cst[j:j + 1, :], NEG))
                x2 = jnp.where(m, xs_ref[:, p * LANES:(p + 1) * LANES], 0.0)
                xdt = x2 * dtj
                dym = jnp.where(m, dy_ref[:, p * LANES:(p + 1) * LANES], 0.0)
                hm = jnp.where(ms, h_in[p], 0.0)
                dhm = jnp.where(ms, dh_in[p], 0.0)
                ecs = jnp.exp(csj)
                last = csj[Q - 1:Q, :]
                decay = jnp.exp(last - csj)
                el = jnp.exp(last)
                gmat = cb * lmat
                dymb, xdtb = dym.astype(BF16), xdt.astype(BF16)
                dg = _dot(dymb, xdtb, _NT)
                dxdt = _dot(gmat.astype(BF16), dymb, _TN)
                dcb = dcb + dg * lmat
                ej = dg * gmat
                col_sums = jnp.broadcast_to(jnp.sum(ej, axis=0, keepdims=True), (Q, Q)).T
                dcs = jnp.sum(ej, axis=1, keepdims=True) - col_sums
                ch = _dot(cg, hm.astype(BF16), _NT)
                dye = dym * ecs
                dcs = dcs + jnp.sum(dye * ch, axis=1, keepdims=True)
                dcg = dcg + _dot(dye.astype(BF16), hm.astype(BF16), _NN)
                dhp = _dot(dye.astype(BF16), cg, _TN)
                wmat = _dot(bg, dhm.astype(BF16), _NT)
                xd = xdt * decay
                dxdt = dxdt + decay * wmat
                ddl = jnp.sum(xd * wmat, axis=1, keepdims=True)
                dlast = jnp.sum(ddl, axis=0, keepdims=True) + el * jnp.sum(jnp.sum(dhm * hm, axis=1, keepdims=True), axis=0, keepdims=True)
                dcs = dcs - ddl + jnp.where(rowq == Q - 1, dlast, 0.0)
                dbg = dbg + _dot(xd.astype(BF16), dhm.astype(BF16), _NN)
                dh_out[p] = dh_out[p] + el * dhm + dhp
                da = _dot_exact01(triu, dcs, _NN, True)
                aj = jnp.sum(jnp.where(lane == j, a_r, 0.0), axis=1, keepdims=True)
                ddtj = da * aj + jnp.sum(dxdt * x2, axis=1, keepdims=True)
                ddt = ddt + jnp.where(lane == j, ddtj, 0.0)
                dalog = dalog + jnp.where(lane == j, jnp.sum(da * dtj, axis=0, keepdims=True) * aj, 0.0)
                dxs_acc[p] = dxs_acc[p] + dxdt * dtj
            dcbb = dcb.astype(BF16)
            dc_ref[:, gs] = dcg + _dot(dcbb, bg, _NN)
            db_ref[:, gs] = dbg + _dot(dcbb, cg, _TN)
        for p in range(N_PAIRS):
            dxs_ref[:, p * LANES:(p + 1) * LANES] = dxs_acc[p]
            dh_scr[p] = dh_out[p]
        ddtraw = ddt * _sigmoid(dtr_ref[...] + br[...])
        ddt_ref[...] = ddtraw
        dbias_ref[...] += jnp.sum(ddtraw, axis=0, keepdims=True)
        dalog_ref[...] += dalog

    return pl.pallas_call(
        body, name="ssd_bwd", grid=(b, nc),
        in_specs=[rows(D_SSD), rows(D_BC), rows(D_BC), rows(LANES), dtt_spec, state, rows(D_SSD), const((1, LANES)),
                  const((LANES, 1)), const((1, LANES)), const((LANES, 1))],
        out_specs=[rows(D_SSD), rows(D_BC), rows(D_BC), rows(LANES), const((1, LANES)), const((1, LANES))],
        out_shape=[jax.ShapeDtypeStruct((t, D_SSD), F32), jax.ShapeDtypeStruct((t, D_BC), F32),
                   jax.ShapeDtypeStruct((t, D_BC), F32), jax.ShapeDtypeStruct((t, LANES), F32),
                   jax.ShapeDtypeStruct((1, LANES), F32), jax.ShapeDtypeStruct((1, LANES), F32)],
        scratch_shapes=[pltpu.VMEM((N_PAIRS, LANES, SSD_STATE), F32)],
        compiler_params=_cparams(("arbitrary", "arbitrary")),
    )(xs, bm, cm, dtraw, dtraw.T, hprev, dy, bias_r, bias_c, alog_r, alog_c)


def _split_w_in(w_in):
    w_dt = jnp.pad(w_in[:, D_QKVZ + D_CONV:], ((0, 0), (0, LANES - N_HEADS)))
    return w_in[:, :D_QKVZ], w_in[:, D_QKVZ:D_QKVZ + D_CONV], w_dt


def mixer_fwd(hb, p, cosv, sinv, b):
    t = hb.shape[0]
    w_a, w_b, w_c = _split_w_in(p['w_in'])
    qkvz = mm("in_qkvz", [(hb, w_a, 'nn')], D_QKVZ)
    xbc = mm("in_xbc", [(hb, w_b, 'nn')], D_CONV)
    dtraw = mm("in_dt", [(hb, w_c, 'nn')], LANES)
    mixed, *lses = attn_fwd(qkvz, cosv, sinv, b)
    attn = attn_norm_fwd(mixed, p['attn_norm_w'])
    xs, bm, cm = conv_fwd(xbc, p['conv_w'], p['conv_b'])
    y, hprev = ssd_fwd(xs, bm, cm, dtraw, p['dt_bias'], p['a_log'], b)
    dskip = jnp.repeat(p['d_skip'].reshape(-1), HEAD_DIM).reshape(1, D_SSD)
    yg, = rowwise("ssd_gate", _gate, [y, xs, Op(qkvz, D_SSD, 3)], [dskip, p['ssd_norm_w']], [(t, D_SSD, BF16)])
    mix = mm("out_proj", [(attn, p['w_out'][:D_ATTN], 'nn'), (yg, p['w_out'][D_ATTN:], 'nn')], D_MODEL)
    res = dict(hb=hb, qkvz=qkvz, xbc=xbc, dtraw=dtraw, mixed=mixed, lses=lses, attn=attn, xs=xs, bm=bm, cm=cm,
               y=y, hprev=hprev, dskip=dskip, yg=yg, cosv=cosv, sinv=sinv)
    return mix, res


def mixer_bwd(r, p, dmix, dh_resid, b):
    t = dmix.shape[0]
    w_a, w_b, w_c = _split_w_in(p['w_in'])
    w_out = p['w_out']
    dattn = mm("out_bwd_dattn", [(dmix, w_out[:D_ATTN], 'nt')], D_ATTN)
    dyg = mm("out_bwd_dyg", [(dmix, w_out[D_ATTN:], 'nt')], D_SSD)
    dw_out = jnp.concatenate([mm_tn("out_bwd_dw_a", r['attn'], dmix, BF16),
                              mm_tn("out_bwd_dw_y", r['yg'], dmix, BF16)], axis=0)

    def gate_bwd(dy_, y_, xs_, z_, ds_, w_):
        _, vjp = jax.vjp(_gate, y_, xs_, z_, ds_, w_)
        return vjp(dy_)

    dy, dxs_a, dz, ddskip, dssd_norm = rowwise(
        "ssd_gate_bwd", gate_bwd, [dyg, r['y'], r['xs'], Op(r['qkvz'], D_SSD, 3)], [r['dskip'], p['ssd_norm_w']],
        [(t, D_SSD, F32), (t, D_SSD, F32), (t, D_SSD, BF16)], accs=[(1, D_SSD), (1, D_SSD)])
    dxs_b, dbm, dcm, ddtraw, ddt_bias, da_log = ssd_bwd(r['xs'], r['bm'], r['cm'], r['dtraw'], p['dt_bias'], p['a_log'],
                                                        r['hprev'], dy, b)
    dxbc, dconv_w, dconv_b = conv_bwd(r['xbc'], p['conv_w'], p['conv_b'], dxs_a, dxs_b, dbm, dcm)
    dmixed, dattn_norm = attn_norm_bwd(dattn, r['mixed'], p['attn_norm_w'])
    dq, dk, dv = attn_bwd(r['qkvz'], r['cosv'], r['sinv'], dmixed, r['mixed'], r['lses'], b)
    wq, wk, wv, wz = (w_a[:, i * D_ATTN:(i + 1) * D_ATTN] for i in range(4))
    dh = mm("in_bwd_dh", [(dq, wq, 'nt'), (dk, wk, 'nt'), (dv, wv, 'nt'), (dz, wz, 'nt'), (dxbc, w_b, 'nt'),
                          (ddtraw, w_c, 'nt')], D_MODEL, add=dh_resid, tn=512)
    h = r['hb']
    dw_in = jnp.concatenate([mm_tn("in_bwd_dwq", h, dq, BF16), mm_tn("in_bwd_dwk", h, dk, BF16),
                             mm_tn("in_bwd_dwv", h, dv, BF16), mm_tn("in_bwd_dwz", h, dz, BF16),
                             mm_tn("in_bwd_dwx", h, dxbc, BF16), mm_tn("in_bwd_dwdt", h, ddtraw, BF16)[:, :N_HEADS]], axis=1)
    head_sum = lambda v: v.reshape(N_HEADS, HEAD_DIM).sum(axis=1).reshape(1, N_HEADS)
    grads = dict(w_in=dw_in, w_out=dw_out, conv_w=dconv_w, conv_b=dconv_b, dt_bias=ddt_bias[:, :N_HEADS],
                 a_log=da_log[:, :N_HEADS], d_skip=head_sum(ddskip), attn_norm_w=dattn_norm, ssd_norm_w=dssd_norm)
    return dh, grads


FFN1_KEYS = ('ffn1_gate', 'ffn1_up', 'ffn1_down')
FFN2_KEYS = ('ffn2_gate', 'ffn2_up', 'ffn2_down')
MIXER_KEYS = ('w_in', 'conv_w', 'w_out')
FFN_COL = ('ffn1_gate', 'ffn1_up', 'ffn2_gate', 'ffn2_up')
FFN_ROW = ('ffn1_down', 'ffn2_down')
CONV_W_COMM = (8, 2 * LANES)
SMALL = 'small'


def comm_shape(k, shapes):
    if k in FFN_COL:
        return (D_MODEL, FF_PAD)
    if k in FFN_ROW:
        return (FF_PAD, D_MODEL)
    if k == 'conv_w':
        return CONV_W_COMM
    if k == SMALL:
        n = sum(int(np.prod(shapes[r])) for r in REPLICATED)
        return (-(-n // (8 * LANES)) * 8, LANES)
    return tuple(shapes[k][1:])


def to_comm(k, vals, shapes):
    if k == SMALL:
        flat = jnp.concatenate([vals[r].reshape(-1) for r in REPLICATED])
        r_, c_ = comm_shape(k, shapes)
        return jnp.pad(flat, (0, r_ * c_ - flat.size)).reshape(r_, c_)
    a = vals[k].reshape(shapes[k][1:])
    r_, c_ = comm_shape(k, shapes)
    return jnp.pad(a, ((0, r_ - a.shape[0]), (0, c_ - a.shape[1])))


def from_comm(k, a, shapes):
    if k == SMALL:
        flat, out, off = a.reshape(-1), {}, 0
        for r in REPLICATED:
            n = int(np.prod(shapes[r]))
            out[r] = flat[off:off + n].reshape(shapes[r])
            off += n
        return out
    shp = shapes[k][1:]
    return {k: a[:shp[0], :shp[1]].reshape(shapes[k])}


def full_weight(k, g):
    if k in FFN_COL:
        return jnp.concatenate([g[p] for p in range(N_DEV)], axis=1)
    if k == 'conv_w':
        return jnp.transpose(g[:, :CONV_WIDTH, :D_CONV // N_DEV], (1, 0, 2)).reshape(CONV_WIDTH, D_CONV)
    return g.reshape(N_DEV * g.shape[1], g.shape[2])


def grad_shards(k, g):
    if k in FFN_COL:
        return jnp.stack([g[:, p * FF_PAD:(p + 1) * FF_PAD] for p in range(N_DEV)])
    if k == 'conv_w':
        s = jnp.transpose(g.reshape(CONV_WIDTH, N_DEV, D_CONV // N_DEV), (1, 0, 2))
        return jnp.pad(s, ((0, 0), (0, CONV_W_COMM[0] - CONV_WIDTH), (0, CONV_W_COMM[1] - D_CONV // N_DEV)))
    return g.reshape(N_DEV, g.shape[0] // N_DEV, g.shape[1])


def _flip(v, bit):
    return 1 - v if bit else v


N_PEER_COPIES = N_DEV - 1


def _comm_call(name, body, arrs, out_shape):
    n = len(arrs)
    return pl.pallas_call(
        functools.partial(body, n), name=name, out_shape=out_shape,
        in_specs=[pl.BlockSpec(memory_space=pl.ANY)] * n, out_specs=[pl.BlockSpec(memory_space=pl.ANY)] * n,
        scratch_shapes=[pltpu.SemaphoreType.DMA((n * N_PEER_COPIES,)), pltpu.SemaphoreType.DMA((n * N_PEER_COPIES,)),
                        pltpu.SemaphoreType.DMA((n,))],
    )(*arrs)


def all_gather(arrs):
    def body(n, *refs):
        x_refs, out_refs, (send_sems, recv_sems, local_sems) = refs[:n], refs[n:2 * n], refs[2 * n:]
        x, y, c = lax.axis_index("x"), lax.axis_index("y"), lax.axis_index("c")
        me, sibling = (x, y, c), (x, y, 1 - c)
        chips = [(1 - x, y), (x, 1 - y), (1 - x, 1 - y)]

        def copy(a, k, block, to, src=None):
            px, py, pc = block
            dst = out_refs[a].at[4 * px + 2 * py + pc]
            return pltpu.make_async_remote_copy(
                src_ref=dst if src is None else src, dst_ref=dst, send_sem=send_sems.at[a * N_PEER_COPIES + k],
                recv_sem=recv_sems.at[a * N_PEER_COPIES + k], device_id=to, device_id_type=MESH)

        mine = [pltpu.make_async_copy(x_refs[a], out_refs[a].at[4 * x + 2 * y + c], local_sems.at[a]) for a in range(n)]
        started = []
        for a in range(n):
            mine[a].start()
            first = [copy(a, 0, me, sibling, src=x_refs[a])]
            first += [copy(a, 1 + j, me, (*chip, c), src=x_refs[a]) for j, chip in enumerate(chips)]
            for cp in first:
                cp.start()
            started += first
        for j, chip in enumerate(chips):
            for a in range(n):
                copy(a, 1 + j, (*chip, c), me).wait_recv()
                cp = copy(a, 4 + j, (*chip, c), sibling)
                cp.start()
                started.append(cp)
        for a in range(n):
            copy(a, 0, sibling, me).wait_recv()
            for j, chip in enumerate(chips):
                copy(a, 4 + j, (*chip, 1 - c), me).wait_recv()
        for cp in started:
            cp.wait_send()
        for cp in mine:
            cp.wait()

    return _comm_call("all_gather_weights", body, arrs,
                      [jax.ShapeDtypeStruct((N_DEV,) + a.shape, a.dtype) for a in arrs])


def all_to_all(arrs):
    def body(n, *refs):
        s_refs, r_refs, (send_sems, recv_sems, local_sems) = refs[:n], refs[n:2 * n], refs[2 * n:]
        x, y, c = lax.axis_index("x"), lax.axis_index("y"), lax.axis_index("c")
        me = 4 * x + 2 * y + c

        def peer(k):
            return _flip(x, k & 4), _flip(y, k & 2), _flip(c, k & 1)

        def copy(a, k, landing):
            px, py, pc = peer(k)
            p = 4 * px + 2 * py + pc
            src, dst = (s_refs[a].at[me], r_refs[a].at[p]) if landing else (s_refs[a].at[p], r_refs[a].at[me])
            return pltpu.make_async_remote_copy(
                src_ref=src, dst_ref=dst, send_sem=send_sems.at[a * N_PEER_COPIES + k - 1],
                recv_sem=recv_sems.at[a * N_PEER_COPIES + k - 1], device_id=(px, py, pc), device_id_type=MESH)

        mine = [pltpu.make_async_copy(s_refs[a].at[me], r_refs[a].at[me], local_sems.at[a]) for a in range(n)]
        sends = [copy(a, k, False) for a in range(n) for k in range(1, N_DEV)]
        for cp in mine + sends:
            cp.start()
        for a in range(n):
            for k in range(1, N_DEV):
                copy(a, k, True).wait_recv()
        for cp in sends:
            cp.wait_send()
        for cp in mine:
            cp.wait()

    return _comm_call("all_to_all_grads", body, arrs, [jax.ShapeDtypeStruct(a.shape, a.dtype) for a in arrs])


_HBM = pl.BlockSpec(memory_space=pltpu.HBM)
_SEM = pl.BlockSpec(memory_space=pltpu.SEMAPHORE)
_EFFECT = pltpu.SideEffectType.DATAFLOW_SIDE_EFFECTING


def _peer(k):
    x, y, c = lax.axis_index("x"), lax.axis_index("y"), lax.axis_index("c")
    return _flip(x, k & 4), _flip(y, k & 2), _flip(c, k & 1)


def _my_index():
    return 4 * lax.axis_index("x") + 2 * lax.axis_index("y") + lax.axis_index("c")


def _split_copies(mode, n, src_refs, land_refs, send_sems, recv_sems):
    me = _my_index()
    out = []
    for a in range(n):
        for k in range(1, N_DEV):
            px, py, pc = _peer(k)
            src = src_refs[a].at[4 * px + 2 * py + pc] if mode == 'scatter' else src_refs[a]
            out.append(pltpu.make_async_remote_copy(
                src_ref=src, dst_ref=land_refs[a].at[me], send_sem=send_sems.at[a * N_PEER_COPIES + k - 1],
                recv_sem=recv_sems.at[a * N_PEER_COPIES + k - 1], device_id=(px, py, pc), device_id_type=MESH))
    return out


def exchange_start(name, mode, srcs):
    n = len(srcs)
    lands = [lax.empty(s.shape if mode == 'scatter' else (N_DEV,) + s.shape, s.dtype) for s in srcs]

    def body(*refs):
        src_refs, land_refs, send_sems, recv_sems = refs[:n], refs[n:2 * n], refs[2 * n], refs[2 * n + 1]
        for cp in _split_copies(mode, n, src_refs, land_refs, send_sems, recv_sems):
            cp.start()
        refs[-1][...] = jnp.zeros(refs[-1].shape, F32)

    sems = pltpu.SemaphoreType.DMA((n * N_PEER_COPIES,))
    res = pl.pallas_call(
        body, name=name,
        out_shape=(sems, sems, *[pltpu.HBM(a.shape, a.dtype) for a in srcs + lands], jax.ShapeDtypeStruct((8, LANES), F32)),
        in_specs=(_HBM,) * (2 * n), out_specs=(_SEM, _SEM, *(_HBM,) * (2 * n), pl.BlockSpec(memory_space=pltpu.VMEM)),
        input_output_aliases={i: 2 + i for i in range(2 * n)},
        compiler_params=pltpu.CompilerParams(has_side_effects=_EFFECT),
    )(*[pltpu.with_memory_space_constraint(a, pltpu.HBM) for a in srcs + lands])
    return (mode, n, res[:-1]), res[-1]


def exchange_wait(name, handles, after):
    mode, n, (send_sems, recv_sems, *bufs) = handles

    def body(*refs):
        src_refs, land_refs, s_sems, r_sems = refs[:n], refs[n:2 * n], refs[2 * n], refs[2 * n + 1]
        for cp in _split_copies(mode, n, src_refs, land_refs, s_sems, r_sems):
            cp.wait_send()
            cp.wait_recv()

    res = pl.pallas_call(
        body, name=name, out_shape=tuple(pltpu.HBM(a.shape, a.dtype) for a in bufs),
        in_specs=(*(_HBM,) * (2 * n), _SEM, _SEM, pl.BlockSpec(memory_space=pl.ANY)), out_specs=(_HBM,) * (2 * n),
        input_output_aliases={i: i for i in range(2 * n)},
        compiler_params=pltpu.CompilerParams(has_side_effects=_EFFECT),
    )(*bufs, send_sems, recv_sems, after)
    srcs, lands = res[:n], res[n:]
    me = _my_index()
    own = [lax.dynamic_index_in_dim(s, me, 0, keepdims=True) if mode == 'scatter' else s[None] for s in srcs]
    return [lax.dynamic_update_slice(l, o, (me, 0, 0)) for l, o in zip(lands, own)]


def adamw(name, recv, w, m, v, tm):
    rows, cols = w.shape
    c1 = 1.0 / (1.0 - ADAM_B1 ** ADAM_STEP)
    c2 = 1.0 / (1.0 - ADAM_B2 ** ADAM_STEP)

    def fn(*a):
        g = a[0]
        for s in range(1, N_DEV):
            g = g + a[s]
        w_, m_, v_ = a[N_DEV:]
        m_ = ADAM_B1 * m_ + (1.0 - ADAM_B1) * g
        v_ = ADAM_B2 * v_ + (1.0 - ADAM_B2) * jnp.square(g)
        delta = -ADAM_LR * ((m_ * c1) / (jnp.sqrt(v_ * c2) + ADAM_EPS) + ADAM_WD * w_)
        return g, delta, m_, v_

    flat = recv.reshape(N_DEV * rows, cols)
    ins = [Op(flat, cols, 0, s * (rows // tm)) for s in range(N_DEV)] + [w, m, v]
    return rowwise(name, fn, ins, [], [(rows, cols, F32)] * 4, tm=tm)


ADAMW_TM = {'ffn1_gate': 256, 'ffn1_up': 256, 'ffn1_down': 128, 'w_in': 32, 'conv_w': 8, 'w_out': 64,
            'ffn2_gate': 256, 'ffn2_up': 256, 'ffn2_down': 128}


def kernel(x, positions, ln1_g, ln1_b, ffn1_gate, ffn1_up, ffn1_down, w_in, conv_w, conv_b, dt_bias, a_log, d_skip, attn_norm_w, ssd_norm_w, w_out, ln2_g, ln2_b, ffn2_gate, ffn2_up, ffn2_down, ln3_g, ln3_b, loss_target, m_ln1_g, m_ln1_b, m_ffn1_gate, m_ffn1_up, m_ffn1_down, m_w_in, m_conv_w, m_conv_b, m_dt_bias, m_a_log, m_d_skip, m_attn_norm_w, m_ssd_norm_w, m_w_out, m_ln2_g, m_ln2_b, m_ffn2_gate, m_ffn2_up, m_ffn2_down, m_ln3_g, m_ln3_b, v_ln1_g, v_ln1_b, v_ffn1_gate, v_ffn1_up, v_ffn1_down, v_w_in, v_conv_w, v_conv_b, v_dt_bias, v_a_log, v_d_skip, v_attn_norm_w, v_ssd_norm_w, v_w_out, v_ln2_g, v_ln2_b, v_ffn2_gate, v_ffn2_up, v_ffn2_down, v_ln3_g, v_ln3_b):
    args = dict(locals())
    wl = {k: args[k] for k in WEIGHTS}
    ml = {k: args["m_" + k] for k in WEIGHTS}
    vl = {k: args["v_" + k] for k in WEIGHTS}
    shapes = {k: wl[k].shape for k in WEIGHTS}
    b, s, dm = x.shape
    t = b * s

    w_comm = {k: to_comm(k, wl, shapes) for k in SHARDED + (SMALL,)}
    sent = {k: w_comm[k] if k == 'conv_w' else w_comm[k].astype(BF16) for k in SHARDED}
    later = tuple(k for k in SHARDED if k not in FFN1_KEYS)
    p = {k: full_weight(k, g) for k, g in zip(FFN1_KEYS, all_gather([sent[k] for k in FFN1_KEYS]))}
    gather_later, token = exchange_start("gather_later_start", 'gather', [sent[k] for k in later])
    p['ffn1_gate'] = p['ffn1_gate'] + token[0, 0].astype(BF16)
    for k in REPLICATED:
        p[k] = wl[k].reshape(1, -1)

    x2 = x.reshape(t, dm)
    cosv, sinv = rope_tables(positions)
    f1, res1 = ffn_fwd("ffn1", x2, p['ffn1_gate'], p['ffn1_up'], p['ffn1_down'])
    h1, h1b = resid_ln_fwd("ln1", 0.5, x2, f1, p['ln1_g'], p['ln1_b'])
    for k, g in zip(later, exchange_wait("gather_later_wait", gather_later, h1b)):
        p[k] = full_weight(k, g)
    mix, resm = mixer_fwd(h1b, p, cosv, sinv, b)
    h2, h2b = resid_ln_fwd("ln2", 1.0, h1, mix, p['ln2_g'], p['ln2_b'])
    f2, res3 = ffn_fwd("ffn2", h2b, p['ffn2_gate'], p['ffn2_up'], p['ffn2_down'])

    small, full = {}, {}
    dh2_res, df2, small['ln3_g'], small['ln3_b'], sq = ln_loss_bwd("ln3_loss_bwd", h2, f2, loss_target.reshape(t, dm),
                                                                   p['ln3_g'], p['ln3_b'])
    loss = lax.psum(jnp.sum(sq) * (0.5 / dm), AXES)

    dh2, full['ffn2_gate'], full['ffn2_up'], full['ffn2_down'] = ffn_bwd("ffn2", res3, p['ffn2_gate'], p['ffn2_up'],
                                                                       p['ffn2_down'], df2, dh2_res)
    ffn2_exchange, token = exchange_start("grads_ffn2_start", 'scatter', [grad_shards(k, full[k]) for k in FFN2_KEYS])
    dh1_res, dmix, small['ln2_g'], small['ln2_b'] = resid_ln_bwd("ln2_bwd", 1.0, h1, mix, p['ln2_g'] + token[:1, :1],
                                                                 p['ln2_b'], dh2)
    dh1, gm = mixer_bwd(resm, p, dmix, dh1_res, b)
    for k in ('conv_b', 'dt_bias', 'a_log', 'd_skip', 'attn_norm_w', 'ssd_norm_w'):
        small[k] = gm[k]
    mixer_exchange, token = exchange_start("grads_mixer_start", 'scatter', [grad_shards(k, gm[k]) for k in MIXER_KEYS])
    dx_res, df1, small['ln1_g'], small['ln1_b'] = resid_ln_bwd("ln1_bwd", 0.5, x2, f1, p['ln1_g'] + token[:1, :1],
                                                               p['ln1_b'], dh1)
    dx, full['ffn1_gate'], full['ffn1_up'], full['ffn1_down'] = ffn_bwd("ffn1", res1, p['ffn1_gate'], p['ffn1_up'],
                                                                      p['ffn1_down'], df1, dx_res)
    small_part = to_comm(SMALL, small, shapes)
    recv = dict(zip(FFN1_KEYS + (SMALL,), all_to_all([grad_shards(k, full[k]) for k in FFN1_KEYS]
                                                     + [jnp.broadcast_to(small_part[None], (N_DEV,) + small_part.shape)])))
    recv.update(zip(FFN2_KEYS, exchange_wait("grads_ffn2_wait", ffn2_exchange, recv[SMALL])))
    recv.update(zip(MIXER_KEYS, exchange_wait("grads_mixer_wait", mixer_exchange, recv[SMALL])))
    outs = [{}, {}, {}, {}]
    for k, r in recv.items():
        tm = ADAMW_TM.get(k, r.shape[1])
        res = adamw(f"adamw_{k}", r, w_comm[k], to_comm(k, ml, shapes), to_comm(k, vl, shapes), tm)
        for o, a in zip(outs, res):
            o.update(from_comm(k, a, shapes))
    return (loss, dx.reshape(b, s, dm), *[o[k] for o in outs for k in WEIGHTS])
```

```python
import functools
import math

import jax
import jax.numpy as jnp
import numpy as np
from jax import lax
from jax.experimental import pallas as pl
from jax.experimental.pallas import tpu as pltpu

F32, BF16 = jnp.float32, jnp.bfloat16
HI = lax.Precision.HIGHEST
MESH = pl.DeviceIdType.MESH
AXES = ("x", "y", "c")
N_DEV = 8

D_MODEL = 1024
SEQ = 2048
HEAD_DIM = 64
N_HEADS = 12
D_ATTN = N_HEADS * HEAD_DIM
DILATIONS = (1, 4, 16)
ATTN_BLOCK = 128
ROPE_THETA = 500000.0
ROPE_DIM = 16
D_SSD = 768
SSD_GROUPS = 4
SSD_STATE = 128
SSD_CHUNK = 128
D_BC = SSD_GROUPS * SSD_STATE
D_CONV = D_SSD + 2 * D_BC
CONV_WIDTH = 4
D_QKVZ = 3 * D_ATTN + D_SSD
D_IN_PROJ = D_QKVZ + D_CONV + N_HEADS
D_FF = 2816
ALPHA = 2.0 ** 0.25
LN_EPS = 1e-5
RMS_EPS = 1e-6
ADAM_LR, ADAM_B1, ADAM_B2, ADAM_EPS, ADAM_WD, ADAM_STEP = 0.001, 0.9, 0.999, 1e-08, 0.01, 10

LANES = 128
VMEM_LIMIT = 52 * 1024 * 1024
NEG = -1e30

WEIGHTS = ['ln1_g', 'ln1_b', 'ffn1_gate', 'ffn1_up', 'ffn1_down', 'w_in', 'conv_w', 'conv_b', 'dt_bias', 'a_log',
           'd_skip', 'attn_norm_w', 'ssd_norm_w', 'w_out', 'ln2_g', 'ln2_b', 'ffn2_gate', 'ffn2_up', 'ffn2_down',
           'ln3_g', 'ln3_b']
COL_SHARDED = ('ffn1_gate', 'ffn1_up', 'conv_w', 'ffn2_gate', 'ffn2_up')
ROW_SHARDED = ('ffn1_down', 'w_in', 'w_out', 'ffn2_down')
SHARDED = tuple(n for n in WEIGHTS if n in COL_SHARDED or n in ROW_SHARDED)
REPLICATED = tuple(n for n in WEIGHTS if n not in SHARDED)
FF_SHARD = D_FF // N_DEV
FF_PAD = -(-FF_SHARD // LANES) * LANES
D_FF_INT = N_DEV * FF_PAD


def _cparams(sem=None):
    return pltpu.CompilerParams(dimension_semantics=sem, vmem_limit_bytes=VMEM_LIMIT)


def _tile(n, prefs):
    for p in prefs:
        if n % p == 0:
            return p
    return n


class Op:
    def __init__(self, arr, bw=None, cb=0, ro=0):
        self.arr, self.bw, self.cb, self.ro = arr, (arr.shape[1] if bw is None else bw), cb, ro


def _op(a):
    return a if isinstance(a, Op) else Op(a)


def rowwise(name, fn, ins, consts, outs, accs=(), tm=256):
    ins = [_op(a) for a in ins]
    rows = outs[0][0]
    n_in, n_c, n_o, n_a = len(ins), len(consts), len(outs), len(accs)
    tm = min(tm, rows)
    assert rows % tm == 0, (name, rows, tm)

    def body(*refs):
        vals = [r[...].astype(F32) for r in refs[:n_in + n_c]]
        res = fn(*vals)
        res = res if isinstance(res, (tuple, list)) else (res,)
        o_refs = refs[n_in + n_c:n_in + n_c + n_o]
        a_refs = refs[n_in + n_c + n_o:]
        for r, v in zip(o_refs, res[:n_o]):
            r[...] = v.astype(r.dtype)
        if n_a:
            @pl.when(pl.program_id(0) == 0)
            def _():
                for r in a_refs:
                    r[...] = jnp.zeros(r.shape, r.dtype)
            for r, v in zip(a_refs, res[n_o:]):
                r[...] += v

    in_specs = [pl.BlockSpec((tm, o.bw), functools.partial(lambda i, o: (i + o.ro, o.cb), o=o)) for o in ins]
    in_specs += [pl.BlockSpec(c.shape, functools.partial(lambda i, nd: (0,) * nd, nd=c.ndim)) for c in consts]
    out_specs = [pl.BlockSpec((tm, w), lambda i: (i, 0)) for (_, w, _) in outs]
    out_specs += [pl.BlockSpec(s, functools.partial(lambda i, nd: (0,) * nd, nd=len(s))) for s in accs]
    out_shape = [jax.ShapeDtypeStruct((r, w), dt) for (r, w, dt) in outs]
    out_shape += [jax.ShapeDtypeStruct(s, F32) for s in accs]
    res = pl.pallas_call(
        body, name=name, grid=(rows // tm,), in_specs=in_specs, out_specs=out_specs, out_shape=out_shape,
        compiler_params=_cparams(("arbitrary",) if n_a else ("parallel",)),
    )(*[o.arr for o in ins], *consts)
    return res


MM_TM = 512
MM_TN = (1024, 896, 768, 512, 256, 128)
_NT = (((1,), (1,)), ((), ()))
_NN = (((1,), (0,)), ((), ()))
_TN = (((0,), (0,)), ((), ()))


def _dot(a, b, dn, precision=None):
    return lax.dot_general(a, b, dn, preferred_element_type=F32, precision=precision)


def _mm_specs(name, pairs, n_out, tm, tn):
    in_specs, args = [], []
    for a, b, mode in pairs:
        o = _op(a)
        in_specs.append(pl.BlockSpec((tm, o.bw), functools.partial(lambda j, i, o: (i, o.cb), o=o)))
        args.append(o.arr)
        if mode == 'nn':
            assert b.shape == (o.bw, n_out), (name, b.shape, o.bw, n_out)
            in_specs.append(pl.BlockSpec((o.bw, tn), lambda j, i: (0, j)))
        else:
            assert b.shape == (n_out, o.bw), (name, b.shape, o.bw, n_out)
            in_specs.append(pl.BlockSpec((tn, o.bw), lambda j, i: (j, 0)))
        args.append(b)
    return in_specs, args


def _mm_acc(refs, pairs):
    acc = None
    for k, (_, _, mode) in enumerate(pairs):
        d = _dot(refs[2 * k][...].astype(BF16), refs[2 * k + 1][...].astype(BF16), _NN if mode == 'nn' else _NT)
        acc = d if acc is None else acc + d
    return acc


def mm(name, pairs, n_out, add=None, out_dtype=F32, tm=MM_TM, tn=None, after=None):
    m = _op(pairs[0][0]).arr.shape[0]
    tn = tn or _tile(n_out, MM_TN)
    n_p = len(pairs)

    def body(*refs):
        acc = _mm_acc(refs, pairs)
        if add is not None:
            acc = acc + refs[2 * n_p][...]
        refs[-1][...] = acc.astype(refs[-1].dtype)

    in_specs, args = _mm_specs(name, pairs, n_out, tm, tn)
    tile = pl.BlockSpec((tm, tn), lambda j, i: (i, j))
    if add is not None:
        in_specs.append(tile)
        args.append(add)
    if after is not None:
        in_specs.append(pl.BlockSpec(after.shape, lambda j, i: (0, 0)))
        args.append(after)
    return pl.pallas_call(
        body, name=name, grid=(n_out // tn, m // tm), in_specs=in_specs, out_specs=tile,
        out_shape=jax.ShapeDtypeStruct((m, n_out), out_dtype),
        compiler_params=_cparams(("parallel", "parallel")),
    )(*args)


def mm_tn(name, a, b, out_dtype=F32, tt=1024, after=None):
    a, b = _op(a), _op(b)
    t = a.arr.shape[0]
    k, n = a.bw, b.bw
    tk = _tile(k, (512, 896, 768, 256, 128))
    tn = _tile(n, MM_TN)
    tt = min(tt, t)
    n_t = t // tt
    order = [] if after is None else [after]

    def body(a_ref, b_ref, *rest):
        o_ref, acc_ref = rest[-2:]
        s = pl.program_id(2)
        d = _dot(a_ref[...].astype(BF16), b_ref[...].astype(BF16), _TN)

        @pl.when(s == 0)
        def _():
            acc_ref[...] = d

        @pl.when(s > 0)
        def _():
            acc_ref[...] += d

        @pl.when(s == n_t - 1)
        def _():
            o_ref[...] = acc_ref[...].astype(o_ref.dtype)

    return pl.pallas_call(
        body, name=name, grid=(k // tk, n // tn, n_t),
        in_specs=[pl.BlockSpec((tt, tk), functools.partial(lambda kk, nn, s, o: (s, o.cb * (o.bw // tk) + kk), o=a)),
                  pl.BlockSpec((tt, tn), functools.partial(lambda kk, nn, s, o: (s, o.cb * (o.bw // tn) + nn), o=b))]
        + [pl.BlockSpec(x.shape, lambda kk, nn, s: (0, 0)) for x in order],
        out_specs=pl.BlockSpec((tk, tn), lambda kk, nn, s: (kk, nn)),
        out_shape=jax.ShapeDtypeStruct((k, n), out_dtype),
        scratch_shapes=[pltpu.VMEM((tk, tn), F32)],
        compiler_params=_cparams(("parallel", "parallel", "arbitrary")),
    )(a.arr, b.arr, *order)


def _sigmoid(x):
    return 1.0 / (1.0 + jnp.exp(-x))


def _silu(x):
    return x * _sigmoid(x)


def _softplus(x):
    return jnp.maximum(x, 0.0) + jnp.log(1.0 + jnp.exp(-jnp.abs(x)))


def _act(g, u):
    return _silu(g) * u


def _resid_ln(scale, h, branch, g, b):
    r = ALPHA * h + scale * branch
    mu = jnp.mean(r, axis=-1, keepdims=True)
    var = jnp.mean(jnp.square(r - mu), axis=-1, keepdims=True)
    return (r - mu) * lax.rsqrt(var + LN_EPS) * g + b


def _rms(t, w):
    return t * lax.rsqrt(jnp.mean(t * t, axis=-1, keepdims=True) + RMS_EPS) * w


def _branch_weights(l1, l2, l3):
    m = jnp.maximum(jnp.maximum(l1, l2), l3)
    e1, e2, e3 = jnp.exp(l1 - m), jnp.exp(l2 - m), jnp.exp(l3 - m)
    inv = 1.0 / (e1 + e2 + e3)
    return e1 * inv, e2 * inv, e3 * inv


def _gate(y, xs, z, dskip, w):
    return _rms((y + dskip * xs) * _silu(z), w)


def _rot(x):
    d = lax.broadcasted_iota(jnp.int32, x.shape, 1) % HEAD_DIM
    up = pltpu.roll(x, x.shape[1] - ROPE_DIM // 2, 1)
    down = jnp.where(d < ROPE_DIM, pltpu.roll(x, ROPE_DIM // 2, 1), 0.0)
    return jnp.where(d < ROPE_DIM // 2, up, down)


def ffn_gate_up(name, h, wg, wu, after=()):
    m, nf = h.shape[0], wg.shape[1]
    tn = _tile(nf, MM_TN)

    def body(h_ref, g_w, u_w, *rest):
        g_ref, u_ref, a_ref = rest[-3:]
        hb = h_ref[...].astype(BF16)
        g = _dot(hb, g_w[...].astype(BF16), _NN)
        u = _dot(hb, u_w[...].astype(BF16), _NN)
        g_ref[...] = g.astype(g_ref.dtype)
        u_ref[...] = u.astype(u_ref.dtype)
        a_ref[...] = _act(g, u).astype(a_ref.dtype)

    in_specs, args = _mm_specs(name, [(h, wg, 'nn')], nf, MM_TM, tn)
    in_specs.append(in_specs[1])
    in_specs += [pl.BlockSpec(x.shape, lambda j, i: (0, 0)) for x in after]
    tile = pl.BlockSpec((MM_TM, tn), lambda j, i: (i, j))
    return pl.pallas_call(
        body, name=name, grid=(nf // tn, m // MM_TM), in_specs=in_specs, out_specs=[tile] * 3,
        out_shape=[jax.ShapeDtypeStruct((m, nf), BF16)] * 3, compiler_params=_cparams(("parallel", "parallel")),
    )(*args, wu, *after)


def ffn_da_act(name, df, wd, g, u):
    m, nf = df.shape[0], wd.shape[0]
    tn = _tile(nf, MM_TN)

    def body(df_ref, w_ref, g_ref, u_ref, dg_ref, du_ref):
        da = _dot(df_ref[...].astype(BF16), w_ref[...].astype(BF16), _NT)
        _, vjp = jax.vjp(_act, g_ref[...].astype(F32), u_ref[...].astype(F32))
        dg, du = vjp(da)
        dg_ref[...] = dg.astype(dg_ref.dtype)
        du_ref[...] = du.astype(du_ref.dtype)

    in_specs, args = _mm_specs(name, [(df, wd, 'nt')], nf, MM_TM, tn)
    tile = pl.BlockSpec((MM_TM, tn), lambda j, i: (i, j))
    return pl.pallas_call(
        body, name=name, grid=(nf // tn, m // MM_TM), in_specs=in_specs + [tile, tile], out_specs=[tile] * 2,
        out_shape=[jax.ShapeDtypeStruct((m, nf), BF16)] * 2, compiler_params=_cparams(("parallel", "parallel")),
    )(*args, g, u)


def resid_ln_fwd(name, scale, h, branch, ln_g, ln_b):
    t = h.shape[0]

    def fn(*a):
        y = _resid_ln(scale, *a)
        return y, y

    return rowwise(name, fn, [h, branch], [ln_g, ln_b], [(t, D_MODEL, F32), (t, D_MODEL, BF16)], tm=512)


def ffn_fwd(tag, hb, wg, wu, wd, after=()):
    g, u, a = ffn_gate_up(f"{tag}_gate_up", hb, wg, wu, after)
    f = mm(f"{tag}_down", [(a, wd, 'nn')], D_MODEL)
    return f, (hb, g, u, a)


def ln_loss_bwd(name, h, branch, target, ln_g, ln_b):
    t, dm = h.shape

    def fn(h_, br_, tgt, g_, b_):
        y, vjp = jax.vjp(functools.partial(_resid_ln, 0.5), h_, br_, g_, b_)
        e = y - tgt
        return (*vjp(e * (1.0 / dm)), jnp.sum(e * e, axis=0, keepdims=True))

    return rowwise(name, fn, [h, branch, target], [ln_g, ln_b], [(t, dm, F32), (t, dm, F32)],
                   accs=[(1, dm), (1, dm), (1, dm)], tm=512)


def resid_ln_bwd(name, scale, h, branch, ln_g, ln_b, dout, extra=None):
    t = h.shape[0]

    def fn(h_, br_, do_, *rest):
        g_, b_ = rest[-2], rest[-1]
        _, vjp = jax.vjp(functools.partial(_resid_ln, scale), h_, br_, g_, b_)
        dh, dbr, dg, db = vjp(do_)
        if extra is not None:
            dh = dh + rest[0]
        return dh, dbr, dg, db

    ins = [h, branch, dout] + ([extra] if extra is not None else [])
    return rowwise(name, fn, ins, [ln_g, ln_b], [(t, D_MODEL, F32), (t, D_MODEL, F32)],
                   accs=[(1, D_MODEL), (1, D_MODEL)], tm=512)


def ffn_bwd(tag, res, wg, wu, wd, df, dh_resid):
    hb, g, u, a = res
    dg, du = ffn_da_act(f"{tag}_bwd_da_act", df, wd, g, u)
    dwd = mm_tn(f"{tag}_bwd_dwd", a, df, BF16)
    dh = mm(f"{tag}_bwd_dh", [(dg, wg, 'nt'), (du, wu, 'nt')], D_MODEL, add=dh_resid, tn=512)
    dwg = mm_tn(f"{tag}_bwd_dwg", hb, dg, BF16)
    dwu = mm_tn(f"{tag}_bwd_dwu", hb, du, BF16)
    return dh, dwg, dwu, dwd


def rope_tables(positions):
    inv_freq = ROPE_THETA ** (-jnp.arange(0, ROPE_DIM, 2, dtype=F32) / ROPE_DIM)
    ang = positions.reshape(-1, 1).astype(F32) * inv_freq
    c, s = jnp.cos(ang), jnp.sin(ang)
    t = ang.shape[0]
    cosv = jnp.concatenate([c, c, jnp.ones((t, HEAD_DIM - ROPE_DIM), F32)], axis=1)
    sinv = jnp.concatenate([-s, s, jnp.zeros((t, HEAD_DIM - ROPE_DIM), F32)], axis=1)
    return jnp.tile(cosv, (1, 2)), jnp.tile(sinv, (1, 2))


def _pair_masks():
    lane = lax.broadcasted_iota(jnp.int32, (1, LANES), 1)
    return (lane < HEAD_DIM, lane >= HEAD_DIM)


def _band_masks():
    row = lax.broadcasted_iota(jnp.int32, (ATTN_BLOCK, ATTN_BLOCK), 0)
    col = lax.broadcasted_iota(jnp.int32, (ATTN_BLOCK, ATTN_BLOCK), 1)
    return col >= row, col <= row


def _residue_blocks():
    out = []
    for g, d in enumerate(DILATIONS):
        for r in range(d):
            for i in range(SEQ // d // ATTN_BLOCK):
                rows = lambda j: pl.ds(r + j * ATTN_BLOCK * d, ATTN_BLOCK, stride=d) if d > 1 else pl.ds(j * ATTN_BLOCK, ATTN_BLOCK)
                out.append((g, rows(i), rows(i - 1) if i > 0 else None))
    return out


N_HEAD_PAIRS = D_ATTN // LANES
SCALE = HEAD_DIM ** -0.5
ATTN_GROUP = 4


def _block_operands(qr, kr, v_ref, cur, prev):
    prev_ok, cur_ok = _band_masks()
    if prev is None:
        return qr[cur, :], kr[cur, :].astype(BF16), v_ref[cur, :], cur_ok
    kcat = jnp.concatenate([kr[prev, :], kr[cur, :]], axis=0).astype(BF16)
    vcat = jnp.concatenate([v_ref[prev, :], v_ref[cur, :]], axis=0)
    return qr[cur, :], kcat, vcat, jnp.concatenate([prev_ok, cur_ok], axis=1)


def _attn_specs(b):
    col = lambda cb: pl.BlockSpec((SEQ, LANES), lambda bb, hp: (bb, cb + hp))
    tab = pl.BlockSpec((SEQ, LANES), lambda bb, hp: (bb, 0))
    return col, tab


def attn_fwd(qkvz, cosv, sinv, b):
    t = qkvz.shape[0]
    col, tab = _attn_specs(b)
    blocks = _residue_blocks()

    def body(q_ref, k_ref, v_ref, c_ref, s_ref, o_ref, l1_ref, l2_ref, l3_ref, qr, kr, o1, o2, o3):
        l_refs, o_scr = (l1_ref, l2_ref, l3_ref), (o1, o2, o3)
        c, s = c_ref[...], s_ref[...]
        q, k = q_ref[...], k_ref[...]
        qr[...] = q * c + _rot(q) * s
        kr[...] = k * c + _rot(k) * s
        masks = _pair_masks()
        for lo in range(0, len(blocks), ATTN_GROUP):
            chains = []
            for g, cur, prev in blocks[lo:lo + ATTN_GROUP]:
                q2, kcat, vcat, ok = _block_operands(qr, kr, v_ref, cur, prev)
                for m in masks:
                    qm = jnp.where(m, q2, 0.0).astype(BF16)
                    chains.append(dict(g=g, cur=cur, m=m, v=jnp.where(m, vcat, 0.0).astype(BF16),
                                       s=jnp.where(ok, _dot(qm, kcat, _NT) * SCALE, NEG)))
            for ch in chains:
                mx = jnp.max(ch['s'], axis=1, keepdims=True)
                p = jnp.exp(ch['s'] - mx)
                den = jnp.sum(p, axis=1, keepdims=True)
                ch.update(p=p.astype(BF16), inv=1.0 / den, lse=mx + jnp.log(den))
            for ch in chains:
                ch['o'] = _dot(ch['p'], ch['v'], _NN) * ch['inv']
            for c0, c1 in zip(chains[0::2], chains[1::2]):
                o_scr[c0['g']][c0['cur'], :] = c0['o'] + c1['o']
                l_refs[c0['g']][c0['cur'], :] = jnp.where(c0['m'], c0['lse'], c1['lse'])
        w1, w2, w3 = _branch_weights(l1_ref[...], l2_ref[...], l3_ref[...])
        o_ref[...] = w1 * o1[...] + w2 * o2[...] + w3 * o3[...]

    shp = jax.ShapeDtypeStruct((t, D_ATTN), F32)
    return pl.pallas_call(
        body, name="attn_fwd", grid=(b, N_HEAD_PAIRS),
        in_specs=[col(0), col(N_HEAD_PAIRS), col(2 * N_HEAD_PAIRS), tab, tab],
        out_specs=[col(0)] * 4, out_shape=[shp] * 4,
        scratch_shapes=[pltpu.VMEM((SEQ, LANES), F32)] * 5,
        compiler_params=_cparams(("parallel", "parallel")),
    )(qkvz, qkvz, qkvz, cosv, sinv)


def attn_bwd(qkvz, cosv, sinv, dmix, mixed, lses, b):
    t = qkvz.shape[0]
    col, tab = _attn_specs(b)
    blocks = _residue_blocks()
    hd = np.arange(LANES) // HEAD_DIM
    head_ones = jnp.asarray((hd[:, None] == hd[None, :]).astype(np.float32))

    def body(q_ref, k_ref, v_ref, c_ref, s_ref, dm_ref, mx_ref, l1_ref, l2_ref, l3_ref, ones_ref,
             dq_out, dk_out, dv_out, qr, kr, do1, do2, do3, dd1, dd2, dd3, dq_ref, dk_ref, dv_ref):
        l_refs, do_scr, dd_scr = (l1_ref, l2_ref, l3_ref), (do1, do2, do3), (dd1, dd2, dd3)
        c, s = c_ref[...], s_ref[...]
        q, k = q_ref[...], k_ref[...]
        qr[...] = q * c + _rot(q) * s
        kr[...] = k * c + _rot(k) * s
        dm = dm_ref[...]
        tot = _dot(dm * mx_ref[...], ones_ref[...], _NN, HI)
        for w, do_g, dd_g in zip(_branch_weights(l1_ref[...], l2_ref[...], l3_ref[...]), do_scr, dd_scr):
            do_g[...] = w * dm
            dd_g[...] = w * tot
        dq_ref[...] = jnp.zeros((SEQ, LANES), F32)
        dk_ref[...] = jnp.zeros((SEQ, LANES), F32)
        dv_ref[...] = jnp.zeros((SEQ, LANES), F32)
        masks = _pair_masks()
        for lo in range(0, len(blocks), ATTN_GROUP):
            chains = []
            for g, cur, prev in blocks[lo:lo + ATTN_GROUP]:
                q2, kcat, vcat, ok = _block_operands(qr, kr, v_ref, cur, prev)
                vcat = vcat.astype(BF16)
                do2_, l2, dd2_ = do_scr[g][cur, :], l_refs[g][cur, :], dd_scr[g][cur, :]
                l2s, dd2s = pltpu.roll(l2, HEAD_DIM, 1), pltpu.roll(dd2_, HEAD_DIM, 1)
                for m in masks:
                    qm = jnp.where(m, q2, 0.0).astype(BF16)
                    dom = jnp.where(m, do2_, 0.0).astype(BF16)
                    lrep, ddrep = jnp.where(m, l2, l2s), jnp.where(m, dd2_, dd2s)
                    if prev is not None:
                        lrep, ddrep = jnp.concatenate([lrep, lrep], axis=1), jnp.concatenate([ddrep, ddrep], axis=1)
                    chains.append(dict(cur=cur, prev=prev, qm=qm, dom=dom, km=jnp.where(m, kcat, 0), lrep=lrep, ddrep=ddrep,
                                       s=jnp.where(ok, _dot(qm, kcat, _NT) * SCALE, NEG), dp=_dot(dom, vcat, _NT)))
            for ch in chains:
                p = jnp.exp(ch['s'] - ch['lrep'])
                ch.update(p=p.astype(BF16), ds=(p * (ch['dp'] - ch['ddrep']) * SCALE).astype(BF16))
            for ch in chains:
                ch.update(dq=_dot(ch['ds'], ch['km'], _NN), dk=_dot(ch['ds'], ch['qm'], _TN), dv=_dot(ch['p'], ch['dom'], _TN))
            for c0, c1 in zip(chains[0::2], chains[1::2]):
                cur, prev = c0['cur'], c0['prev']
                dk, dv = c0['dk'] + c1['dk'], c0['dv'] + c1['dv']
                dq_ref[cur, :] += c0['dq'] + c1['dq']
                if prev is None:
                    dk_ref[cur, :] += dk
                    dv_ref[cur, :] += dv
                else:
                    dk_ref[prev, :] += dk[:ATTN_BLOCK]
                    dv_ref[prev, :] += dv[:ATTN_BLOCK]
                    dk_ref[cur, :] += dk[ATTN_BLOCK:]
                    dv_ref[cur, :] += dv[ATTN_BLOCK:]
        dq, dk = dq_ref[...], dk_ref[...]
        dq_out[...] = (dq * c + _rot(dq * s)).astype(dq_out.dtype)
        dk_out[...] = (dk * c + _rot(dk * s)).astype(dk_out.dtype)
        dv_out[...] = dv_ref[...].astype(dv_out.dtype)

    shp = jax.ShapeDtypeStruct((t, D_ATTN), BF16)
    return pl.pallas_call(
        body, name="attn_bwd", grid=(b, N_HEAD_PAIRS),
        in_specs=[col(0), col(N_HEAD_PAIRS), col(2 * N_HEAD_PAIRS), tab, tab, col(0), col(0), col(0), col(0), col(0),
                  pl.BlockSpec((LANES, LANES), lambda bb, hp: (0, 0))],
        out_specs=[col(0)] * 3, out_shape=[shp] * 3,
        scratch_shapes=[pltpu.VMEM((SEQ, LANES), F32)] * 11,
        compiler_params=_cparams(("parallel", "parallel")),
    )(qkvz, qkvz, qkvz, cosv, sinv, dmix, mixed, *lses, head_ones)


def attn_norm_fwd(mixed, norm_w):
    return rowwise("attn_norm", _rms, [mixed], [norm_w], [(mixed.shape[0], D_ATTN, BF16)])[0]


def attn_norm_bwd(dout, mixed, norm_w):
    def fn(dy, mx, w):
        _, vjp = jax.vjp(_rms, mx, w)
        return vjp(dy)

    return rowwise("attn_norm_bwd", fn, [dout, mixed], [norm_w], [(dout.shape[0], D_ATTN, F32)], accs=[(1, D_ATTN)])


CONV_TM = 256
HALO = 8


def conv_fwd(u, w, bias):
    t = u.shape[0]
    tm, per_seq = CONV_TM, SEQ // CONV_TM

    def body(u_ref, h_ref, w_ref, b_ref, xs_ref, bm_ref, cm_ref, scr):
        first = pl.program_id(0) % per_seq == 0
        scr[0:HALO, :] = jnp.where(first, 0.0, h_ref[...])
        scr[HALO:, :] = u_ref[...]
        acc = b_ref[...]
        for k in range(CONV_WIDTH):
            acc = acc + w_ref[k:k + 1, :] * scr[pl.ds(HALO - CONV_WIDTH + 1 + k, tm), :]
        y = _silu(acc)
        xs_ref[...] = y[:, :D_SSD]
        bm_ref[...] = y[:, D_SSD:D_SSD + D_BC]
        cm_ref[...] = y[:, D_SSD + D_BC:]

    return pl.pallas_call(
        body, name="conv_fwd", grid=(t // tm,),
        in_specs=[pl.BlockSpec((tm, D_CONV), lambda i: (i, 0)),
                  pl.BlockSpec((HALO, D_CONV), lambda i: (jnp.maximum(i * (tm // HALO) - 1, 0), 0)),
                  pl.BlockSpec((CONV_WIDTH, D_CONV), lambda i: (0, 0)), pl.BlockSpec((1, D_CONV), lambda i: (0, 0))],
        out_specs=[pl.BlockSpec((tm, D_SSD), lambda i: (i, 0)), pl.BlockSpec((tm, D_BC), lambda i: (i, 0)),
                   pl.BlockSpec((tm, D_BC), lambda i: (i, 0))],
        out_shape=[jax.ShapeDtypeStruct((t, D_SSD), F32), jax.ShapeDtypeStruct((t, D_BC), F32),
                   jax.ShapeDtypeStruct((t, D_BC), F32)],
        scratch_shapes=[pltpu.VMEM((tm + HALO, D_CONV), F32)],
        compiler_params=_cparams(("parallel",)),
    )(u, u, w, bias)


def conv_bwd(u, w, bias, dxs_a, dxs_b, dbm, dcm):
    t = u.shape[0]
    tm, per_seq = CONV_TM, SEQ // CONV_TM
    n_tiles = t // tm

    def body1(u_ref, h_ref, dxs_ref, dxs2_ref, dbm_ref, dcm_ref, w_ref, b_ref, dz_ref, dw_ref, db_ref, scr):
        i = pl.program_id(0)
        first = i % per_seq == 0
        scr[0:HALO, :] = jnp.where(first, 0.0, h_ref[...])
        scr[HALO:, :] = u_ref[...]
        acc = b_ref[...]
        for k in range(CONV_WIDTH):
            acc = acc + w_ref[k:k + 1, :] * scr[pl.ds(HALO - CONV_WIDTH + 1 + k, tm), :]
        sig = _sigmoid(acc)
        dy = jnp.concatenate([dxs_ref[...] + dxs2_ref[...], dbm_ref[...], dcm_ref[...]], axis=1)
        dz = dy * sig * (1.0 + acc * (1.0 - sig))
        dz_ref[...] = dz

        @pl.when(i == 0)
        def _():
            dw_ref[...] = jnp.zeros(dw_ref.shape, F32)
            db_ref[...] = jnp.zeros(db_ref.shape, F32)
        db_ref[...] += jnp.sum(dz, axis=0, keepdims=True)
        for k in range(CONV_WIDTH):
            dw_ref[k:k + 1, :] += jnp.sum(dz * scr[pl.ds(HALO - CONV_WIDTH + 1 + k, tm), :], axis=0, keepdims=True)

    dz, dw, db = pl.pallas_call(
        body1, name="conv_bwd_dz", grid=(n_tiles,),
        in_specs=[pl.BlockSpec((tm, D_CONV), lambda i: (i, 0)),
                  pl.BlockSpec((HALO, D_CONV), lambda i: (jnp.maximum(i * (tm // HALO) - 1, 0), 0)),
                  pl.BlockSpec((tm, D_SSD), lambda i: (i, 0)), pl.BlockSpec((tm, D_SSD), lambda i: (i, 0)),
                  pl.BlockSpec((tm, D_BC), lambda i: (i, 0)), pl.BlockSpec((tm, D_BC), lambda i: (i, 0)),
                  pl.BlockSpec((CONV_WIDTH, D_CONV), lambda i: (0, 0)), pl.BlockSpec((1, D_CONV), lambda i: (0, 0))],
        out_specs=[pl.BlockSpec((tm, D_CONV), lambda i: (i, 0)), pl.BlockSpec((CONV_WIDTH, D_CONV), lambda i: (0, 0)),
                   pl.BlockSpec((1, D_CONV), lambda i: (0, 0))],
        out_shape=[jax.ShapeDtypeStruct((t, D_CONV), F32), jax.ShapeDtypeStruct((CONV_WIDTH, D_CONV), F32),
                   jax.ShapeDtypeStruct((1, D_CONV), F32)],
        scratch_shapes=[pltpu.VMEM((tm + HALO, D_CONV), F32)],
        compiler_params=_cparams(("arbitrary",)),
    )(u, u, dxs_a, dxs_b, dbm, dcm, w, bias)

    def body2(dz_ref, n_ref, w_ref, du_ref, scr):
        last = pl.program_id(0) % per_seq == per_seq - 1
        scr[0:tm, :] = dz_ref[...]
        scr[tm:, :] = jnp.where(last, 0.0, n_ref[...])
        acc = jnp.zeros((tm, D_CONV), F32)
        for k in range(CONV_WIDTH):
            acc = acc + w_ref[k:k + 1, :] * scr[pl.ds(CONV_WIDTH - 1 - k, tm), :]
        du_ref[...] = acc.astype(du_ref.dtype)

    du = pl.pallas_call(
        body2, name="conv_bwd_du", grid=(n_tiles,),
        in_specs=[pl.BlockSpec((tm, D_CONV), lambda i: (i, 0)),
                  pl.BlockSpec((HALO, D_CONV), lambda i: (jnp.minimum((i + 1) * (tm // HALO), t // HALO - 1), 0)),
                  pl.BlockSpec((CONV_WIDTH, D_CONV), lambda i: (0, 0))],
        out_specs=pl.BlockSpec((tm, D_CONV), lambda i: (i, 0)),
        out_shape=jax.ShapeDtypeStruct((t, D_CONV), BF16),
        scratch_shapes=[pltpu.VMEM((tm + HALO, D_CONV), F32)],
        compiler_params=_cparams(("parallel",)),
    )(dz, dz, w)
    return du, dw, db


Q = SSD_CHUNK
N_PAIRS = D_SSD // LANES
HEADS_PER_GROUP = N_HEADS // SSD_GROUPS


def _rep(a, j):
    return jnp.broadcast_to(a[:, j:j + 1], a.shape)


def _dot_exact01(a, b, dn, a_is_01):
    x = b if a_is_01 else a
    hi = x.astype(BF16)
    mid = (x - hi.astype(F32)).astype(BF16)
    lo = (x - hi.astype(F32) - mid.astype(F32)).astype(BF16)
    z = a.astype(BF16) if a_is_01 else b.astype(BF16)
    out = None
    for term in (hi, mid, lo):
        d = _dot(z, term, dn) if a_is_01 else _dot(term, z, dn)
        out = d if out is None else out + d
    return out


def _pad_lanes(v, fill=0.0):
    row = jnp.pad(v.reshape(1, -1).astype(F32), ((0, 0), (0, LANES - v.size)), constant_values=fill)
    return row, row.reshape(LANES, 1)


def _ssd_common(dtr_ref, dtrt_ref, bias_r, bias_c, alog_r, alog_c):
    row = lax.broadcasted_iota(jnp.int32, (Q, Q), 0)
    col = lax.broadcasted_iota(jnp.int32, (Q, Q), 1)
    tril = row >= col
    lane = lax.broadcasted_iota(jnp.int32, (1, LANES), 1)
    a_r = jnp.where(lane < N_HEADS, -jnp.exp(alog_r[...]), 0.0)
    sub = lax.broadcasted_iota(jnp.int32, (LANES, 1), 0)
    a_c = jnp.where(sub < N_HEADS, -jnp.exp(alog_c[...]), 0.0)
    dt = _softplus(dtr_ref[...] + bias_r[...])
    cs = _dot_exact01(tril, dt * a_r, _NN, True)
    dtt = _softplus(dtrt_ref[...] + bias_c[...])
    cst = _dot_exact01(dtt * a_c, row <= col, _NN, False)
    return tril, lane, a_r, dt, cs, cst


def _ssd_specs(b, nc, rev):
    ci = (lambda c: nc - 1 - c) if rev else (lambda c: c)
    rows = lambda w: pl.BlockSpec((Q, w), lambda bb, c: (bb * nc + ci(c), 0))
    dtt = pl.BlockSpec((LANES, Q), lambda bb, c: (0, bb * nc + ci(c)))
    const = lambda s: pl.BlockSpec(s, lambda bb, c: (0,) * len(s))
    state = pl.BlockSpec((None, N_PAIRS, LANES, SSD_STATE), lambda bb, c: (bb * nc + ci(c), 0, 0, 0))
    return rows, dtt, const, state


def ssd_fwd(xs, bm, cm, dtraw, dt_bias, a_log, b):
    t = xs.shape[0]
    nc = SEQ // Q
    rows, dtt_spec, const, state = _ssd_specs(b, nc, False)
    bias_r, bias_c = _pad_lanes(dt_bias)
    alog_r, alog_c = _pad_lanes(a_log)

    def body(xs_ref, b_ref, c_ref, dtr_ref, dtrt_ref, br, bc, ar, ac, y_ref, hp_ref, h_scr):
        @pl.when(pl.program_id(1) == 0)
        def _():
            h_scr[...] = jnp.zeros(h_scr.shape, F32)
        tril, lane, _, dt, cs, cst = _ssd_common(dtr_ref, dtrt_ref, br, bc, ar, ac)
        sub = lax.broadcasted_iota(jnp.int32, (LANES, 1), 0)
        y_acc = [jnp.zeros((Q, LANES), F32) for _ in range(N_PAIRS)]
        h_old = [h_scr[p] for p in range(N_PAIRS)]
        h_new = [jnp.zeros((LANES, SSD_STATE), F32) for _ in range(N_PAIRS)]
        for g in range(SSD_GROUPS):
            bg = b_ref[:, g * SSD_STATE:(g + 1) * SSD_STATE].astype(BF16)
            cg = c_ref[:, g * SSD_STATE:(g + 1) * SSD_STATE].astype(BF16)
            cb = _dot(cg, bg, _NT)
            for j in range(g * HEADS_PER_GROUP, (g + 1) * HEADS_PER_GROUP):
                p, side = j // 2, j % 2
                m = (lane < HEAD_DIM) if side == 0 else (lane >= HEAD_DIM)
                ms = (sub < HEAD_DIM) if side == 0 else (sub >= HEAD_DIM)
                csj, dtj = _rep(cs, j), _rep(dt, j)
                lmat = jnp.exp(jnp.where(tril, csj - cst[j:j + 1, :], NEG))
                xdt = jnp.where(m, xs_ref[:, p * LANES:(p + 1) * LANES] * dtj, 0.0)
                hm = jnp.where(ms, h_old[p], 0.0)
                ydiag = _dot((cb * lmat).astype(BF16), xdt.astype(BF16), _NN)
                yoff = jnp.exp(csj) * _dot(cg, hm.astype(BF16), _NT)
                y_acc[p] = y_acc[p] + ydiag + yoff
                last = csj[Q - 1:Q, :]
                sj = _dot((xdt * jnp.exp(last - csj)).astype(BF16), bg, _TN)
                h_new[p] = h_new[p] + jnp.exp(last) * hm + sj
        for p in range(N_PAIRS):
            y_ref[:, p * LANES:(p + 1) * LANES] = y_acc[p]
            hp_ref[p] = h_old[p]
            h_scr[p] = h_new[p]

    return pl.pallas_call(
        body, name="ssd_fwd", grid=(b, nc),
        in_specs=[rows(D_SSD), rows(D_BC), rows(D_BC), rows(LANES), dtt_spec, const((1, LANES)), const((LANES, 1)),
                  const((1, LANES)), const((LANES, 1))],
        out_specs=[rows(D_SSD), state],
        out_shape=[jax.ShapeDtypeStruct((t, D_SSD), F32),
                   jax.ShapeDtypeStruct((b * nc, N_PAIRS, LANES, SSD_STATE), F32)],
        scratch_shapes=[pltpu.VMEM((N_PAIRS, LANES, SSD_STATE), F32)],
        compiler_params=_cparams(("parallel", "arbitrary")),
    )(xs, bm, cm, dtraw, dtraw.T, bias_r, bias_c, alog_r, alog_c)


def ssd_bwd(xs, bm, cm, dtraw, dt_bias, a_log, hprev, dy, b):
    t = xs.shape[0]
    nc = SEQ // Q
    rows, dtt_spec, const, state = _ssd_specs(b, nc, True)
    bias_r, bias_c = _pad_lanes(dt_bias)
    alog_r, alog_c = _pad_lanes(a_log)

    def body(xs_ref, b_ref, c_ref, dtr_ref, dtrt_ref, hp_ref, dy_ref, br, bc, ar, ac,
             dxs_ref, db_ref, dc_ref, ddt_ref, dbias_ref, dalog_ref, dh_scr):
        first = jnp.logical_and(pl.program_id(0) == 0, pl.program_id(1) == 0)

        @pl.when(pl.program_id(1) == 0)
        def _():
            dh_scr[...] = jnp.zeros(dh_scr.shape, F32)

        @pl.when(first)
        def _():
            dbias_ref[...] = jnp.zeros(dbias_ref.shape, F32)
            dalog_ref[...] = jnp.zeros(dalog_ref.shape, F32)
        tril, lane, a_r, dt, cs, cst = _ssd_common(dtr_ref, dtrt_ref, br, bc, ar, ac)
        sub = lax.broadcasted_iota(jnp.int32, (LANES, 1), 0)
        rowq = lax.broadcasted_iota(jnp.int32, (Q, 1), 0)
        triu = (lax.broadcasted_iota(jnp.int32, (Q, Q), 0) <= lax.broadcasted_iota(jnp.int32, (Q, Q), 1)).astype(F32)
        dxs_acc = [jnp.zeros((Q, LANES), F32) for _ in range(N_PAIRS)]
        dh_in = [dh_scr[p] for p in range(N_PAIRS)]
        h_in = [hp_ref[p] for p in range(N_PAIRS)]
        dh_out = [jnp.zeros((LANES, SSD_STATE), F32) for _ in range(N_PAIRS)]
        ddt = jnp.zeros((Q, LANES), F32)
        dalog = jnp.zeros((1, LANES), F32)
        for g in range(SSD_GROUPS):
            gs = slice(g * SSD_STATE, (g + 1) * SSD_STATE)
            bg, cg = b_ref[:, gs].astype(BF16), c_ref[:, gs].astype(BF16)
            cb = _dot(cg, bg, _NT)
            dcb = jnp.zeros((Q, Q), F32)
            dbg = jnp.zeros((Q, SSD_STATE), F32)
            dcg = jnp.zeros((Q, SSD_STATE), F32)
            for j in range(g * HEADS_PER_GROUP, (g + 1) * HEADS_PER_GROUP):
                p, side = j // 2, j % 2
                m = (lane < HEAD_DIM) if side == 0 else (lane >= HEAD_DIM)
                ms = (sub < HEAD_DIM) if side == 0 else (sub >= HEAD_DIM)
                csj, dtj = _rep(cs, j), _rep(dt, j)
                lmat = jnp.exp(jnp.where(tril, csj - cst[j:j + 1, :], NEG))
                x2 = jnp.where(m, xs_ref[:, p * LANES:(p + 1) * LANES], 0.0)
                xdt = x2 * dtj
                dym = jnp.where(m, dy_ref[:, p * LANES:(p + 1) * LANES], 0.0)
                hm = jnp.where(ms, h_in[p], 0.0)
                dhm = jnp.where(ms, dh_in[p], 0.0)
                ecs = jnp.exp(csj)
                last = csj[Q - 1:Q, :]
                decay = jnp.exp(last - csj)
                el = jnp.exp(last)
                gmat = cb * lmat
                dymb, xdtb = dym.astype(BF16), xdt.astype(BF16)
                dg = _dot(dymb, xdtb, _NT)
                dxdt = _dot(gmat.astype(BF16), dymb, _TN)
                dcb = dcb + dg * lmat
                ej = dg * gmat
                col_sums = jnp.broadcast_to(jnp.sum(ej, axis=0, keepdims=True), (Q, Q)).T
                dcs = jnp.sum(ej, axis=1, keepdims=True) - col_sums
                ch = _dot(cg, hm.astype(BF16), _NT)
                dye = dym * ecs
                dcs = dcs + jnp.sum(dye * ch, axis=1, keepdims=True)
                dcg = dcg + _dot(dye.astype(BF16), hm.astype(BF16), _NN)
                dhp = _dot(dye.astype(BF16), cg, _TN)
                wmat = _dot(bg, dhm.astype(BF16), _NT)
                xd = xdt * decay
                dxdt = dxdt + decay * wmat
                ddl = jnp.sum(xd * wmat, axis=1, keepdims=True)
                dlast = jnp.sum(ddl, axis=0, keepdims=True) + el * jnp.sum(jnp.sum(dhm * hm, axis=1, keepdims=True), axis=0, keepdims=True)
                dcs = dcs - ddl + jnp.where(rowq == Q - 1, dlast, 0.0)
                dbg = dbg + _dot(xd.astype(BF16), dhm.astype(BF16), _NN)
                dh_out[p] = dh_out[p] + el * dhm + dhp
                da = _dot_exact01(triu, dcs, _NN, True)
                aj = jnp.sum(jnp.where(lane == j, a_r, 0.0), axis=1, keepdims=True)
                ddtj = da * aj + jnp.sum(dxdt * x2, axis=1, keepdims=True)
                ddt = ddt + jnp.where(lane == j, ddtj, 0.0)
                dalog = dalog + jnp.where(lane == j, jnp.sum(da * dtj, axis=0, keepdims=True) * aj, 0.0)
                dxs_acc[p] = dxs_acc[p] + dxdt * dtj
            dcbb = dcb.astype(BF16)
            dc_ref[:, gs] = dcg + _dot(dcbb, bg, _NN)
            db_ref[:, gs] = dbg + _dot(dcbb, cg, _TN)
        for p in range(N_PAIRS):
            dxs_ref[:, p * LANES:(p + 1) * LANES] = dxs_acc[p]
            dh_scr[p] = dh_out[p]
        ddtraw = ddt * _sigmoid(dtr_ref[...] + br[...])
        ddt_ref[...] = ddtraw
        dbias_ref[...] += jnp.sum(ddtraw, axis=0, keepdims=True)
        dalog_ref[...] += dalog

    return pl.pallas_call(
        body, name="ssd_bwd", grid=(b, nc),
        in_specs=[rows(D_SSD), rows(D_BC), rows(D_BC), rows(LANES), dtt_spec, state, rows(D_SSD), const((1, LANES)),
                  const((LANES, 1)), const((1, LANES)), const((LANES, 1))],
        out_specs=[rows(D_SSD), rows(D_BC), rows(D_BC), rows(LANES), const((1, LANES)), const((1, LANES))],
        out_shape=[jax.ShapeDtypeStruct((t, D_SSD), F32), jax.ShapeDtypeStruct((t, D_BC), F32),
                   jax.ShapeDtypeStruct((t, D_BC), F32), jax.ShapeDtypeStruct((t, LANES), F32),
                   jax.ShapeDtypeStruct((1, LANES), F32), jax.ShapeDtypeStruct((1, LANES), F32)],
        scratch_shapes=[pltpu.VMEM((N_PAIRS, LANES, SSD_STATE), F32)],
        compiler_params=_cparams(("arbitrary", "arbitrary")),
    )(xs, bm, cm, dtraw, dtraw.T, hprev, dy, bias_r, bias_c, alog_r, alog_c)


def _split_w_in(w_in):
    w_dt = jnp.pad(w_in[:, D_QKVZ + D_CONV:], ((0, 0), (0, LANES - N_HEADS)))
    return w_in[:, :D_QKVZ], w_in[:, D_QKVZ:D_QKVZ + D_CONV], w_dt


def mixer_fwd(hb, p, cosv, sinv, b):
    t = hb.shape[0]
    w_a, w_b, w_c = _split_w_in(p['w_in'])
    qkvz = mm("in_qkvz", [(hb, w_a, 'nn')], D_QKVZ)
    xbc = mm("in_xbc", [(hb, w_b, 'nn')], D_CONV)
    dtraw = mm("in_dt", [(hb, w_c, 'nn')], LANES)
    mixed, *lses = attn_fwd(qkvz, cosv, sinv, b)
    attn = attn_norm_fwd(mixed, p['attn_norm_w'])
    xs, bm, cm = conv_fwd(xbc, p['conv_w'], p['conv_b'])
    y, hprev = ssd_fwd(xs, bm, cm, dtraw, p['dt_bias'], p['a_log'], b)
    dskip = jnp.repeat(p['d_skip'].reshape(-1), HEAD_DIM).reshape(1, D_SSD)
    yg, = rowwise("ssd_gate", _gate, [y, xs, Op(qkvz, D_SSD, 3)], [dskip, p['ssd_norm_w']], [(t, D_SSD, BF16)])
    mix = mm("out_proj", [(attn, p['w_out'][:D_ATTN], 'nn'), (yg, p['w_out'][D_ATTN:], 'nn')], D_MODEL)
    res = dict(hb=hb, qkvz=qkvz, xbc=xbc, dtraw=dtraw, mixed=mixed, lses=lses, attn=attn, xs=xs, bm=bm, cm=cm,
               y=y, hprev=hprev, dskip=dskip, yg=yg, cosv=cosv, sinv=sinv)
    return mix, res


def mixer_bwd(r, p, dmix, dh_resid, b):
    t = dmix.shape[0]
    w_a, w_b, w_c = _split_w_in(p['w_in'])
    w_out = p['w_out']
    dattn = mm("out_bwd_dattn", [(dmix, w_out[:D_ATTN], 'nt')], D_ATTN)
    dyg = mm("out_bwd_dyg", [(dmix, w_out[D_ATTN:], 'nt')], D_SSD)
    dw_out = jnp.concatenate([mm_tn("out_bwd_dw_a", r['attn'], dmix, BF16),
                              mm_tn("out_bwd_dw_y", r['yg'], dmix, BF16)], axis=0)

    def gate_bwd(dy_, y_, xs_, z_, ds_, w_):
        _, vjp = jax.vjp(_gate, y_, xs_, z_, ds_, w_)
        return vjp(dy_)

    dy, dxs_a, dz, ddskip, dssd_norm = rowwise(
        "ssd_gate_bwd", gate_bwd, [dyg, r['y'], r['xs'], Op(r['qkvz'], D_SSD, 3)], [r['dskip'], p['ssd_norm_w']],
        [(t, D_SSD, F32), (t, D_SSD, F32), (t, D_SSD, BF16)], accs=[(1, D_SSD), (1, D_SSD)])
    dxs_b, dbm, dcm, ddtraw, ddt_bias, da_log = ssd_bwd(r['xs'], r['bm'], r['cm'], r['dtraw'], p['dt_bias'], p['a_log'],
                                                        r['hprev'], dy, b)
    dxbc, dconv_w, dconv_b = conv_bwd(r['xbc'], p['conv_w'], p['conv_b'], dxs_a, dxs_b, dbm, dcm)
    dmixed, dattn_norm = attn_norm_bwd(dattn, r['mixed'], p['attn_norm_w'])
    dq, dk, dv = attn_bwd(r['qkvz'], r['cosv'], r['sinv'], dmixed, r['mixed'], r['lses'], b)
    wq, wk, wv, wz = (w_a[:, i * D_ATTN:(i + 1) * D_ATTN] for i in range(4))
    dh = mm("in_bwd_dh", [(dq, wq, 'nt'), (dk, wk, 'nt'), (dv, wv, 'nt'), (dz, wz, 'nt'), (dxbc, w_b, 'nt'),
                          (ddtraw, w_c, 'nt')], D_MODEL, add=dh_resid, tn=512)
    h = r['hb']
    dw_in = jnp.concatenate([mm_tn("in_bwd_dwq", h, dq, BF16), mm_tn("in_bwd_dwk", h, dk, BF16),
                             mm_tn("in_bwd_dwv", h, dv, BF16), mm_tn("in_bwd_dwz", h, dz, BF16),
                             mm_tn("in_bwd_dwx", h, dxbc, BF16), mm_tn("in_bwd_dwdt", h, ddtraw, BF16)[:, :N_HEADS]], axis=1)
    head_sum = lambda v: v.reshape(N_HEADS, HEAD_DIM).sum(axis=1).reshape(1, N_HEADS)
    grads = dict(w_in=dw_in, w_out=dw_out, conv_w=dconv_w, conv_b=dconv_b, dt_bias=ddt_bias[:, :N_HEADS],
                 a_log=da_log[:, :N_HEADS], d_skip=head_sum(ddskip), attn_norm_w=dattn_norm, ssd_norm_w=dssd_norm)
    return dh, grads


FFN1_KEYS = ('ffn1_gate', 'ffn1_up', 'ffn1_down')
FFN2_KEYS = ('ffn2_gate', 'ffn2_up', 'ffn2_down')
MIXER_KEYS = ('w_in', 'conv_w', 'w_out')
FFN_COL = ('ffn1_gate', 'ffn1_up', 'ffn2_gate', 'ffn2_up')
FFN_ROW = ('ffn1_down', 'ffn2_down')
CONV_W_COMM = (8, 2 * LANES)
SMALL = 'small'


def comm_shape(k, shapes):
    if k in FFN_COL:
        return (D_MODEL, FF_PAD)
    if k in FFN_ROW:
        return (FF_PAD, D_MODEL)
    if k == 'conv_w':
        return CONV_W_COMM
    if k == SMALL:
        n = sum(int(np.prod(shapes[r])) for r in REPLICATED)
        return (-(-n // (8 * LANES)) * 8, LANES)
    return tuple(shapes[k][1:])


def to_comm(k, vals, shapes):
    if k == SMALL:
        flat = jnp.concatenate([vals[r].reshape(-1) for r in REPLICATED])
        r_, c_ = comm_shape(k, shapes)
        return jnp.pad(flat, (0, r_ * c_ - flat.size)).reshape(r_, c_)
    a = vals[k].reshape(shapes[k][1:])
    r_, c_ = comm_shape(k, shapes)
    return jnp.pad(a, ((0, r_ - a.shape[0]), (0, c_ - a.shape[1])))


def from_comm(k, a, shapes):
    if k == SMALL:
        flat, out, off = a.reshape(-1), {}, 0
        for r in REPLICATED:
            n = int(np.prod(shapes[r]))
            out[r] = flat[off:off + n].reshape(shapes[r])
            off += n
        return out
    shp = shapes[k][1:]
    return {k: a[:shp[0], :shp[1]].reshape(shapes[k])}


def full_weight(k, g):
    if k in FFN_COL:
        return jnp.concatenate([g[p] for p in range(N_DEV)], axis=1)
    if k == 'conv_w':
        return jnp.transpose(g[:, :CONV_WIDTH, :D_CONV // N_DEV], (1, 0, 2)).reshape(CONV_WIDTH, D_CONV)
    return g.reshape(N_DEV * g.shape[1], g.shape[2])


def grad_shards(k, g):
    if k in FFN_COL:
        return jnp.stack([g[:, p * FF_PAD:(p + 1) * FF_PAD] for p in range(N_DEV)])
    if k == 'conv_w':
        s = jnp.transpose(g.reshape(CONV_WIDTH, N_DEV, D_CONV // N_DEV), (1, 0, 2))
        return jnp.pad(s, ((0, 0), (0, CONV_W_COMM[0] - CONV_WIDTH), (0, CONV_W_COMM[1] - D_CONV // N_DEV)))
    return g.reshape(N_DEV, g.shape[0] // N_DEV, g.shape[1])


def _flip(v, bit):
    return 1 - v if bit else v


N_PEER_COPIES = N_DEV - 1


def _comm_call(name, body, arrs, out_shape):
    n = len(arrs)
    return pl.pallas_call(
        functools.partial(body, n), name=name, out_shape=out_shape,
        in_specs=[pl.BlockSpec(memory_space=pl.ANY)] * n, out_specs=[pl.BlockSpec(memory_space=pl.ANY)] * n,
        scratch_shapes=[pltpu.SemaphoreType.DMA((n * N_PEER_COPIES,)), pltpu.SemaphoreType.DMA((n * N_PEER_COPIES,)),
                        pltpu.SemaphoreType.DMA((n,))],
    )(*arrs)


def all_gather(arrs):
    def body(n, *refs):
        x_refs, out_refs, (send_sems, recv_sems, local_sems) = refs[:n], refs[n:2 * n], refs[2 * n:]
        x, y, c = lax.axis_index("x"), lax.axis_index("y"), lax.axis_index("c")
        me, sibling = (x, y, c), (x, y, 1 - c)
        chips = [(1 - x, y), (x, 1 - y), (1 - x, 1 - y)]

        def copy(a, k, block, to, src=None):
            px, py, pc = block
            dst = out_refs[a].at[4 * px + 2 * py + pc]
            return pltpu.make_async_remote_copy(
                src_ref=dst if src is None else src, dst_ref=dst, send_sem=send_sems.at[a * N_PEER_COPIES + k],
                recv_sem=recv_sems.at[a * N_PEER_COPIES + k], device_id=to, device_id_type=MESH)

        mine = [pltpu.make_async_copy(x_refs[a], out_refs[a].at[4 * x + 2 * y + c], local_sems.at[a]) for a in range(n)]
        started = []
        for a in range(n):
            mine[a].start()
            first = [copy(a, 0, me, sibling, src=x_refs[a])]
            first += [copy(a, 1 + j, me, (*chip, c), src=x_refs[a]) for j, chip in enumerate(chips)]
            for cp in first:
                cp.start()
            started += first
        for j, chip in enumerate(chips):
            for a in range(n):
                copy(a, 1 + j, (*chip, c), me).wait_recv()
                cp = copy(a, 4 + j, (*chip, c), sibling)
                cp.start()
                started.append(cp)
        for a in range(n):
            copy(a, 0, sibling, me).wait_recv()
            for j, chip in enumerate(chips):
                copy(a, 4 + j, (*chip, 1 - c), me).wait_recv()
        for cp in started:
            cp.wait_send()
        for cp in mine:
            cp.wait()

    return _comm_call("all_gather_weights", body, arrs,
                      [jax.ShapeDtypeStruct((N_DEV,) + a.shape, a.dtype) for a in arrs])


def all_to_all(arrs):
    def body(n, *refs):
        s_refs, r_refs, (send_sems, recv_sems, local_sems) = refs[:n], refs[n:2 * n], refs[2 * n:]
        x, y, c = lax.axis_index("x"), lax.axis_index("y"), lax.axis_index("c")
        me = 4 * x + 2 * y + c

        def peer(k):
            return _flip(x, k & 4), _flip(y, k & 2), _flip(c, k & 1)

        def copy(a, k, landing):
            px, py, pc = peer(k)
            p = 4 * px + 2 * py + pc
            src, dst = (s_refs[a].at[me], r_refs[a].at[p]) if landing else (s_refs[a].at[p], r_refs[a].at[me])
            return pltpu.make_async_remote_copy(
                src_ref=src, dst_ref=dst, send_sem=send_sems.at[a * N_PEER_COPIES + k - 1],
                recv_sem=recv_sems.at[a * N_PEER_COPIES + k - 1], device_id=(px, py, pc), device_id_type=MESH)

        mine = [pltpu.make_async_copy(s_refs[a].at[me], r_refs[a].at[me], local_sems.at[a]) for a in range(n)]
        sends = [copy(a, k, False) for a in range(n) for k in range(1, N_DEV)]
        for cp in mine + sends:
            cp.start()
        for a in range(n):
            for k in range(1, N_DEV):
                copy(a, k, True).wait_recv()
        for cp in sends:
            cp.wait_send()
        for cp in mine:
            cp.wait()

    return _comm_call("all_to_all_grads", body, arrs, [jax.ShapeDtypeStruct(a.shape, a.dtype) for a in arrs])


_HBM = pl.BlockSpec(memory_space=pltpu.HBM)
_SEM = pl.BlockSpec(memory_space=pltpu.SEMAPHORE)
_EFFECT = pltpu.SideEffectType.DATAFLOW_SIDE_EFFECTING


def _peer(k):
    x, y, c = lax.axis_index("x"), lax.axis_index("y"), lax.axis_index("c")
    return _flip(x, k & 4), _flip(y, k & 2), _flip(c, k & 1)


def _my_index():
    return 4 * lax.axis_index("x") + 2 * lax.axis_index("y") + lax.axis_index("c")


def _split_copies(mode, n, src_refs, land_refs, send_sems, recv_sems):
    me = _my_index()
    out = []
    for a in range(n):
        for k in range(1, N_DEV):
            px, py, pc = _peer(k)
            src = src_refs[a].at[4 * px + 2 * py + pc] if mode == 'scatter' else src_refs[a]
            out.append(pltpu.make_async_remote_copy(
                src_ref=src, dst_ref=land_refs[a].at[me], send_sem=send_sems.at[a * N_PEER_COPIES + k - 1],
                recv_sem=recv_sems.at[a * N_PEER_COPIES + k - 1], device_id=(px, py, pc), device_id_type=MESH))
    return out


def exchange_start(name, mode, srcs):
    n = len(srcs)
    lands = [lax.empty(s.shape if mode == 'scatter' else (N_DEV,) + s.shape, s.dtype) for s in srcs]

    def body(*refs):
        src_refs, land_refs, send_sems, recv_sems = refs[:n], refs[n:2 * n], refs[2 * n], refs[2 * n + 1]
        for cp in _split_copies(mode, n, src_refs, land_refs, send_sems, recv_sems):
            cp.start()
        refs[-1][...] = jnp.zeros(refs[-1].shape, F32)

    sems = pltpu.SemaphoreType.DMA((n * N_PEER_COPIES,))
    res = pl.pallas_call(
        body, name=name,
        out_shape=(sems, sems, *[pltpu.HBM(a.shape, a.dtype) for a in srcs + lands], jax.ShapeDtypeStruct((8, LANES), F32)),
        in_specs=(_HBM,) * (2 * n), out_specs=(_SEM, _SEM, *(_HBM,) * (2 * n), pl.BlockSpec(memory_space=pltpu.VMEM)),
        input_output_aliases={i: 2 + i for i in range(2 * n)},
        compiler_params=pltpu.CompilerParams(has_side_effects=_EFFECT),
    )(*[pltpu.with_memory_space_constraint(a, pltpu.HBM) for a in srcs + lands])
    return (mode, n, res[:-1]), res[-1]


def exchange_wait(name, handles, after):
    mode, n, (send_sems, recv_sems, *bufs) = handles

    def body(*refs):
        src_refs, land_refs, s_sems, r_sems = refs[:n], refs[n:2 * n], refs[2 * n], refs[2 * n + 1]
        for cp in _split_copies(mode, n, src_refs, land_refs, s_sems, r_sems):
            cp.wait_send()
            cp.wait_recv()

    res = pl.pallas_call(
        body, name=name, out_shape=tuple(pltpu.HBM(a.shape, a.dtype) for a in bufs),
        in_specs=(*(_HBM,) * (2 * n), _SEM, _SEM, pl.BlockSpec(memory_space=pl.ANY)), out_specs=(_HBM,) * (2 * n),
        input_output_aliases={i: i for i in range(2 * n)},
        compiler_params=pltpu.CompilerParams(has_side_effects=_EFFECT),
    )(*bufs, send_sems, recv_sems, after)
    srcs, lands = res[:n], res[n:]
    me = _my_index()
    own = [lax.dynamic_index_in_dim(s, me, 0, keepdims=True) if mode == 'scatter' else s[None] for s in srcs]
    return [lax.dynamic_update_slice(l, o, (me, 0, 0)) for l, o in zip(lands, own)]


def adamw(name, recv, w, m, v, tm):
    rows, cols = w.shape
    c1 = 1.0 / (1.0 - ADAM_B1 ** ADAM_STEP)
    c2 = 1.0 / (1.0 - ADAM_B2 ** ADAM_STEP)

    def fn(*a):
        g = a[0]
        for s in range(1, N_DEV):
            g = g + a[s]
        w_, m_, v_ = a[N_DEV:]
        m_ = ADAM_B1 * m_ + (1.0 - ADAM_B1) * g
        v_ = ADAM_B2 * v_ + (1.0 - ADAM_B2) * jnp.square(g)
        delta = -ADAM_LR * ((m_ * c1) / (jnp.sqrt(v_ * c2) + ADAM_EPS) + ADAM_WD * w_)
        return g, delta, m_, v_

    flat = recv.reshape(N_DEV * rows, cols)
    ins = [Op(flat, cols, 0, s * (rows // tm)) for s in range(N_DEV)] + [w, m, v]
    return rowwise(name, fn, ins, [], [(rows, cols, F32)] * 4, tm=tm)


ADAMW_TM = {'ffn1_gate': 256, 'ffn1_up': 256, 'ffn1_down': 128, 'w_in': 32, 'conv_w': 8, 'w_out': 64,
            'ffn2_gate': 256, 'ffn2_up': 256, 'ffn2_down': 128}


def kernel(x, positions, ln1_g, ln1_b, ffn1_gate, ffn1_up, ffn1_down, w_in, conv_w, conv_b, dt_bias, a_log, d_skip, attn_norm_w, ssd_norm_w, w_out, ln2_g, ln2_b, ffn2_gate, ffn2_up, ffn2_down, ln3_g, ln3_b, loss_target, m_ln1_g, m_ln1_b, m_ffn1_gate, m_ffn1_up, m_ffn1_down, m_w_in, m_conv_w, m_conv_b, m_dt_bias, m_a_log, m_d_skip, m_attn_norm_w, m_ssd_norm_w, m_w_out, m_ln2_g, m_ln2_b, m_ffn2_gate, m_ffn2_up, m_ffn2_down, m_ln3_g, m_ln3_b, v_ln1_g, v_ln1_b, v_ffn1_gate, v_ffn1_up, v_ffn1_down, v_w_in, v_conv_w, v_conv_b, v_dt_bias, v_a_log, v_d_skip, v_attn_norm_w, v_ssd_norm_w, v_w_out, v_ln2_g, v_ln2_b, v_ffn2_gate, v_ffn2_up, v_ffn2_down, v_ln3_g, v_ln3_b):
    args = dict(locals())
    wl = {k: args[k] for k in WEIGHTS}
    ml = {k: args["m_" + k] for k in WEIGHTS}
    vl = {k: args["v_" + k] for k in WEIGHTS}
    shapes = {k: wl[k].shape for k in WEIGHTS}
    b, s, dm = x.shape
    t = b * s

    w_comm = {k: to_comm(k, wl, shapes) for k in SHARDED + (SMALL,)}
    sent = {k: w_comm[k] if k == 'conv_w' else w_comm[k].astype(BF16) for k in SHARDED}
    p = {k: full_weight(k, g) for k, g in zip(FFN1_KEYS, all_gather([sent[k] for k in FFN1_KEYS]))}
    gather_mixer, token_m = exchange_start("gather_mixer_start", 'gather', [sent[k] for k in MIXER_KEYS])
    sent['ffn2_gate'] = sent['ffn2_gate'] + token_m[0, 0].astype(BF16)
    gather_ffn2, token_f = exchange_start("gather_ffn2_start", 'gather', [sent[k] for k in FFN2_KEYS])
    for k in REPLICATED:
        p[k] = wl[k].reshape(1, -1)

    x2 = x.reshape(t, dm)
    cosv, sinv = rope_tables(positions)
    f1, res1 = ffn_fwd("ffn1", x2, p['ffn1_gate'], p['ffn1_up'], p['ffn1_down'], after=(token_m, token_f))
    h1, h1b = resid_ln_fwd("ln1", 0.5, x2, f1, p['ln1_g'], p['ln1_b'])
    for k, g in zip(MIXER_KEYS, exchange_wait("gather_mixer_wait", gather_mixer, h1b)):
        p[k] = full_weight(k, g)
    mix, resm = mixer_fwd(h1b, p, cosv, sinv, b)
    h2, h2b = resid_ln_fwd("ln2", 1.0, h1, mix, p['ln2_g'], p['ln2_b'])
    for k, g in zip(FFN2_KEYS, exchange_wait("gather_ffn2_wait", gather_ffn2, h2b)):
        p[k] = full_weight(k, g)
    f2, res3 = ffn_fwd("ffn2", h2b, p['ffn2_gate'], p['ffn2_up'], p['ffn2_down'])

    small, full = {}, {}
    dh2_res, df2, small['ln3_g'], small['ln3_b'], sq = ln_loss_bwd("ln3_loss_bwd", h2, f2, loss_target.reshape(t, dm),
                                                                   p['ln3_g'], p['ln3_b'])
    loss = lax.psum(jnp.sum(sq) * (0.5 / dm), AXES)

    dh2, full['ffn2_gate'], full['ffn2_up'], full['ffn2_down'] = ffn_bwd("ffn2", res3, p['ffn2_gate'], p['ffn2_up'],
                                                                       p['ffn2_down'], df2, dh2_res)
    ffn2_exchange, token = exchange_start("grads_ffn2_start", 'scatter', [grad_shards(k, full[k]) for k in FFN2_KEYS])
    dh1_res, dmix, small['ln2_g'], small['ln2_b'] = resid_ln_bwd("ln2_bwd", 1.0, h1, mix, p['ln2_g'] + token[:1, :1],
                                                                 p['ln2_b'], dh2)
    dh1, gm = mixer_bwd(resm, p, dmix, dh1_res, b)
    for k in ('conv_b', 'dt_bias', 'a_log', 'd_skip', 'attn_norm_w', 'ssd_norm_w'):
        small[k] = gm[k]
    mixer_exchange, token = exchange_start("grads_mixer_start", 'scatter', [grad_shards(k, gm[k]) for k in MIXER_KEYS])
    dx_res, df1, small['ln1_g'], small['ln1_b'] = resid_ln_bwd("ln1_bwd", 0.5, x2, f1, p['ln1_g'] + token[:1, :1],
                                                               p['ln1_b'], dh1)
    hb, g, u, a = res1
    small_part = to_comm(SMALL, small, shapes)
    dg, du = ffn_da_act("ffn1_bwd_da_act", df1, p['ffn1_down'], g, u)
    dwd = mm_tn("ffn1_bwd_dwd", a, df1, BF16)
    down_exchange, token = exchange_start("grads_ffn1_down_start", 'scatter', [
        grad_shards('ffn1_down', dwd), jnp.broadcast_to(small_part[None], (N_DEV,) + small_part.shape)])
    dx = mm("ffn1_bwd_dh", [(dg, p['ffn1_gate'], 'nt'), (du, p['ffn1_up'], 'nt')], D_MODEL, add=dx_res, tn=512, after=token)
    dwg = mm_tn("ffn1_bwd_dwg", hb, dg, BF16)
    gate_exchange, token = exchange_start("grads_ffn1_gate_start", 'scatter', [grad_shards('ffn1_gate', dwg)])
    dwu = mm_tn("ffn1_bwd_dwu", hb, du, BF16, after=token)
    recv = dict(zip(('ffn1_up',), all_to_all([grad_shards('ffn1_up', dwu)])))
    for keys, name, ex in (((FFN2_KEYS), "grads_ffn2_wait", ffn2_exchange), (MIXER_KEYS, "grads_mixer_wait", mixer_exchange),
                           (('ffn1_down', SMALL), "grads_ffn1_down_wait", down_exchange),
                           (('ffn1_gate',), "grads_ffn1_gate_wait", gate_exchange)):
        recv.update(zip(keys, exchange_wait(name, ex, recv['ffn1_up'])))
    outs = [{}, {}, {}, {}]
    for k, r in recv.items():
        tm = ADAMW_TM.get(k, r.shape[1])
        res = adamw(f"adamw_{k}", r, w_comm[k], to_comm(k, ml, shapes), to_comm(k, vl, shapes), tm)
        for o, a in zip(outs, res):
            o.update(from_comm(k, a, shapes))
    return (loss, dx.reshape(b, s, dm), *[o[k] for o in outs for k in WEIGHTS])
```

```python
import functools
import math

import jax
import jax.numpy as jnp
import numpy as np
from jax import lax
from jax.experimental import pallas as pl
from jax.experimental.pallas import tpu as pltpu

F32, BF16 = jnp.float32, jnp.bfloat16
HI = lax.Precision.HIGHEST
MESH = pl.DeviceIdType.MESH
AXES = ("x", "y", "c")
N_DEV = 8

D_MODEL = 1024
SEQ = 2048
HEAD_DIM = 64
N_HEADS = 12
D_ATTN = N_HEADS * HEAD_DIM
DILATIONS = (1, 4, 16)
ATTN_BLOCK = 128
ROPE_THETA = 500000.0
ROPE_DIM = 16
D_SSD = 768
SSD_GROUPS = 4
SSD_STATE = 128
SSD_CHUNK = 128
D_BC = SSD_GROUPS * SSD_STATE
D_CONV = D_SSD + 2 * D_BC
CONV_WIDTH = 4
D_QKVZ = 3 * D_ATTN + D_SSD
D_IN_PROJ = D_QKVZ + D_CONV + N_HEADS
D_FF = 2816
ALPHA = 2.0 ** 0.25
LN_EPS = 1e-5
RMS_EPS = 1e-6
ADAM_LR, ADAM_B1, ADAM_B2, ADAM_EPS, ADAM_WD, ADAM_STEP = 0.001, 0.9, 0.999, 1e-08, 0.01, 10

LANES = 128
VMEM_LIMIT = 52 * 1024 * 1024
NEG = -1e30

WEIGHTS = ['ln1_g', 'ln1_b', 'ffn1_gate', 'ffn1_up', 'ffn1_down', 'w_in', 'conv_w', 'conv_b', 'dt_bias', 'a_log',
           'd_skip', 'attn_norm_w', 'ssd_norm_w', 'w_out', 'ln2_g', 'ln2_b', 'ffn2_gate', 'ffn2_up', 'ffn2_down',
           'ln3_g', 'ln3_b']
COL_SHARDED = ('ffn1_gate', 'ffn1_up', 'conv_w', 'ffn2_gate', 'ffn2_up')
ROW_SHARDED = ('ffn1_down', 'w_in', 'w_out', 'ffn2_down')
SHARDED = tuple(n for n in WEIGHTS if n in COL_SHARDED or n in ROW_SHARDED)
REPLICATED = tuple(n for n in WEIGHTS if n not in SHARDED)
FF_SHARD = D_FF // N_DEV
FF_PAD = -(-FF_SHARD // LANES) * LANES
D_FF_INT = N_DEV * FF_PAD


def _cparams(sem=None):
    return pltpu.CompilerParams(dimension_semantics=sem, vmem_limit_bytes=VMEM_LIMIT)


def _tile(n, prefs):
    for p in prefs:
        if n % p == 0:
            return p
    return n


class Op:
    def __init__(self, arr, bw=None, cb=0, ro=0):
        self.arr, self.bw, self.cb, self.ro = arr, (arr.shape[1] if bw is None else bw), cb, ro


def _op(a):
    return a if isinstance(a, Op) else Op(a)


def rowwise(name, fn, ins, consts, outs, accs=(), tm=256):
    ins = [_op(a) for a in ins]
    rows = outs[0][0]
    n_in, n_c, n_o, n_a = len(ins), len(consts), len(outs), len(accs)
    tm = min(tm, rows)
    assert rows % tm == 0, (name, rows, tm)

    def body(*refs):
        vals = [r[...].astype(F32) for r in refs[:n_in + n_c]]
        res = fn(*vals)
        res = res if isinstance(res, (tuple, list)) else (res,)
        o_refs = refs[n_in + n_c:n_in + n_c + n_o]
        a_refs = refs[n_in + n_c + n_o:]
        for r, v in zip(o_refs, res[:n_o]):
            r[...] = v.astype(r.dtype)
        if n_a:
            @pl.when(pl.program_id(0) == 0)
            def _():
                for r in a_refs:
                    r[...] = jnp.zeros(r.shape, r.dtype)
            for r, v in zip(a_refs, res[n_o:]):
                r[...] += v

    in_specs = [pl.BlockSpec((tm, o.bw), functools.partial(lambda i, o: (i + o.ro, o.cb), o=o)) for o in ins]
    in_specs += [pl.BlockSpec(c.shape, functools.partial(lambda i, nd: (0,) * nd, nd=c.ndim)) for c in consts]
    out_specs = [pl.BlockSpec((tm, w), lambda i: (i, 0)) for (_, w, _) in outs]
    out_specs += [pl.BlockSpec(s, functools.partial(lambda i, nd: (0,) * nd, nd=len(s))) for s in accs]
    out_shape = [jax.ShapeDtypeStruct((r, w), dt) for (r, w, dt) in outs]
    out_shape += [jax.ShapeDtypeStruct(s, F32) for s in accs]
    res = pl.pallas_call(
        body, name=name, grid=(rows // tm,), in_specs=in_specs, out_specs=out_specs, out_shape=out_shape,
        compiler_params=_cparams(("arbitrary",) if n_a else ("parallel",)),
    )(*[o.arr for o in ins], *consts)
    return res


MM_TM = 512
MM_TN = (1024, 896, 768, 512, 256, 128)
_NT = (((1,), (1,)), ((), ()))
_NN = (((1,), (0,)), ((), ()))
_TN = (((0,), (0,)), ((), ()))


def _dot(a, b, dn, precision=None):
    return lax.dot_general(a, b, dn, preferred_element_type=F32, precision=precision)


def _mm_specs(name, pairs, n_out, tm, tn):
    in_specs, args = [], []
    for a, b, mode in pairs:
        o = _op(a)
        in_specs.append(pl.BlockSpec((tm, o.bw), functools.partial(lambda j, i, o: (i, o.cb), o=o)))
        args.append(o.arr)
        if mode == 'nn':
            assert b.shape == (o.bw, n_out), (name, b.shape, o.bw, n_out)
            in_specs.append(pl.BlockSpec((o.bw, tn), lambda j, i: (0, j)))
        else:
            assert b.shape == (n_out, o.bw), (name, b.shape, o.bw, n_out)
            in_specs.append(pl.BlockSpec((tn, o.bw), lambda j, i: (j, 0)))
        args.append(b)
    return in_specs, args


def _mm_acc(refs, pairs):
    acc = None
    for k, (_, _, mode) in enumerate(pairs):
        d = _dot(refs[2 * k][...].astype(BF16), refs[2 * k + 1][...].astype(BF16), _NN if mode == 'nn' else _NT)
        acc = d if acc is None else acc + d
    return acc


def mm(name, pairs, n_out, add=None, out_dtype=F32, tm=MM_TM, tn=None, after=None):
    m = _op(pairs[0][0]).arr.shape[0]
    tn = tn or _tile(n_out, MM_TN)
    n_p = len(pairs)

    def body(*refs):
        acc = _mm_acc(refs, pairs)
        if add is not None:
            acc = acc + refs[2 * n_p][...]
        refs[-1][...] = acc.astype(refs[-1].dtype)

    in_specs, args = _mm_specs(name, pairs, n_out, tm, tn)
    tile = pl.BlockSpec((tm, tn), lambda j, i: (i, j))
    if add is not None:
        in_specs.append(tile)
        args.append(add)
    if after is not None:
        in_specs.append(pl.BlockSpec(memory_space=pl.ANY))
        args.append(after)
    return pl.pallas_call(
        body, name=name, grid=(n_out // tn, m // tm), in_specs=in_specs, out_specs=tile,
        out_shape=jax.ShapeDtypeStruct((m, n_out), out_dtype),
        compiler_params=_cparams(("parallel", "parallel")),
    )(*args)


def mm_tn(name, a, b, out_dtype=F32, tt=1024, after=None):
    a, b = _op(a), _op(b)
    t = a.arr.shape[0]
    k, n = a.bw, b.bw
    tk = _tile(k, (512, 896, 768, 256, 128))
    tn = _tile(n, MM_TN)
    tt = min(tt, t)
    n_t = t // tt
    order = [] if after is None else [after]

    def body(a_ref, b_ref, *rest):
        o_ref, acc_ref = rest[-2:]
        s = pl.program_id(2)
        d = _dot(a_ref[...].astype(BF16), b_ref[...].astype(BF16), _TN)

        @pl.when(s == 0)
        def _():
            acc_ref[...] = d

        @pl.when(s > 0)
        def _():
            acc_ref[...] += d

        @pl.when(s == n_t - 1)
        def _():
            o_ref[...] = acc_ref[...].astype(o_ref.dtype)

    return pl.pallas_call(
        body, name=name, grid=(k // tk, n // tn, n_t),
        in_specs=[pl.BlockSpec((tt, tk), functools.partial(lambda kk, nn, s, o: (s, o.cb * (o.bw // tk) + kk), o=a)),
                  pl.BlockSpec((tt, tn), functools.partial(lambda kk, nn, s, o: (s, o.cb * (o.bw // tn) + nn), o=b))]
        + [pl.BlockSpec(memory_space=pl.ANY) for _ in order],
        out_specs=pl.BlockSpec((tk, tn), lambda kk, nn, s: (kk, nn)),
        out_shape=jax.ShapeDtypeStruct((k, n), out_dtype),
        scratch_shapes=[pltpu.VMEM((tk, tn), F32)],
        compiler_params=_cparams(("parallel", "parallel", "arbitrary")),
    )(a.arr, b.arr, *order)


def _sigmoid(x):
    return 1.0 / (1.0 + jnp.exp(-x))


def _silu(x):
    return x * _sigmoid(x)


def _softplus(x):
    return jnp.maximum(x, 0.0) + jnp.log(1.0 + jnp.exp(-jnp.abs(x)))


def _act(g, u):
    return _silu(g) * u


def _resid_ln(scale, h, branch, g, b):
    r = ALPHA * h + scale * branch
    mu = jnp.mean(r, axis=-1, keepdims=True)
    var = jnp.mean(jnp.square(r - mu), axis=-1, keepdims=True)
    return (r - mu) * lax.rsqrt(var + LN_EPS) * g + b


def _rms(t, w):
    return t * lax.rsqrt(jnp.mean(t * t, axis=-1, keepdims=True) + RMS_EPS) * w


def _branch_weights(l1, l2, l3):
    m = jnp.maximum(jnp.maximum(l1, l2), l3)
    e1, e2, e3 = jnp.exp(l1 - m), jnp.exp(l2 - m), jnp.exp(l3 - m)
    inv = 1.0 / (e1 + e2 + e3)
    return e1 * inv, e2 * inv, e3 * inv


def _gate(y, xs, z, dskip, w):
    return _rms((y + dskip * xs) * _silu(z), w)


def _rot(x):
    d = lax.broadcasted_iota(jnp.int32, x.shape, 1) % HEAD_DIM
    up = pltpu.roll(x, x.shape[1] - ROPE_DIM // 2, 1)
    down = jnp.where(d < ROPE_DIM, pltpu.roll(x, ROPE_DIM // 2, 1), 0.0)
    return jnp.where(d < ROPE_DIM // 2, up, down)


def ffn_gate_up(name, h, wg, wu, after=()):
    m, nf = h.shape[0], wg.shape[1]
    tn = _tile(nf, MM_TN)

    def body(h_ref, g_w, u_w, *rest):
        g_ref, u_ref, a_ref = rest[-3:]
        hb = h_ref[...].astype(BF16)
        g = _dot(hb, g_w[...].astype(BF16), _NN)
        u = _dot(hb, u_w[...].astype(BF16), _NN)
        g_ref[...] = g.astype(g_ref.dtype)
        u_ref[...] = u.astype(u_ref.dtype)
        a_ref[...] = _act(g, u).astype(a_ref.dtype)

    in_specs, args = _mm_specs(name, [(h, wg, 'nn')], nf, MM_TM, tn)
    in_specs.append(in_specs[1])
    in_specs += [pl.BlockSpec(memory_space=pl.ANY) for _ in after]
    tile = pl.BlockSpec((MM_TM, tn), lambda j, i: (i, j))
    return pl.pallas_call(
        body, name=name, grid=(nf // tn, m // MM_TM), in_specs=in_specs, out_specs=[tile] * 3,
        out_shape=[jax.ShapeDtypeStruct((m, nf), BF16)] * 3, compiler_params=_cparams(("parallel", "parallel")),
    )(*args, wu, *after)


def ffn_da_act(name, df, wd, g, u):
    m, nf = df.shape[0], wd.shape[0]
    tn = _tile(nf, MM_TN)

    def body(df_ref, w_ref, g_ref, u_ref, dg_ref, du_ref):
        da = _dot(df_ref[...].astype(BF16), w_ref[...].astype(BF16), _NT)
        _, vjp = jax.vjp(_act, g_ref[...].astype(F32), u_ref[...].astype(F32))
        dg, du = vjp(da)
        dg_ref[...] = dg.astype(dg_ref.dtype)
        du_ref[...] = du.astype(du_ref.dtype)

    in_specs, args = _mm_specs(name, [(df, wd, 'nt')], nf, MM_TM, tn)
    tile = pl.BlockSpec((MM_TM, tn), lambda j, i: (i, j))
    return pl.pallas_call(
        body, name=name, grid=(nf // tn, m // MM_TM), in_specs=in_specs + [tile, tile], out_specs=[tile] * 2,
        out_shape=[jax.ShapeDtypeStruct((m, nf), BF16)] * 2, compiler_params=_cparams(("parallel", "parallel")),
    )(*args, g, u)


def resid_ln_fwd(name, scale, h, branch, ln_g, ln_b):
    t = h.shape[0]

    def fn(*a):
        y = _resid_ln(scale, *a)
        return y, y

    return rowwise(name, fn, [h, branch], [ln_g, ln_b], [(t, D_MODEL, F32), (t, D_MODEL, BF16)], tm=512)


def ffn_fwd(tag, hb, wg, wu, wd, after=()):
    g, u, a = ffn_gate_up(f"{tag}_gate_up", hb, wg, wu, after)
    f = mm(f"{tag}_down", [(a, wd, 'nn')], D_MODEL)
    return f, (hb, g, u, a)


def ln_loss_bwd(name, h, branch, target, ln_g, ln_b):
    t, dm = h.shape

    def fn(h_, br_, tgt, g_, b_):
        y, vjp = jax.vjp(functools.partial(_resid_ln, 0.5), h_, br_, g_, b_)
        e = y - tgt
        return (*vjp(e * (1.0 / dm)), jnp.sum(e * e, axis=0, keepdims=True))

    return rowwise(name, fn, [h, branch, target], [ln_g, ln_b], [(t, dm, F32), (t, dm, F32)],
                   accs=[(1, dm), (1, dm), (1, dm)], tm=512)


def resid_ln_bwd(name, scale, h, branch, ln_g, ln_b, dout, extra=None):
    t = h.shape[0]

    def fn(h_, br_, do_, *rest):
        g_, b_ = rest[-2], rest[-1]
        _, vjp = jax.vjp(functools.partial(_resid_ln, scale), h_, br_, g_, b_)
        dh, dbr, dg, db = vjp(do_)
        if extra is not None:
            dh = dh + rest[0]
        return dh, dbr, dg, db

    ins = [h, branch, dout] + ([extra] if extra is not None else [])
    return rowwise(name, fn, ins, [ln_g, ln_b], [(t, D_MODEL, F32), (t, D_MODEL, F32)],
                   accs=[(1, D_MODEL), (1, D_MODEL)], tm=512)


def ffn_bwd(tag, res, wg, wu, wd, df, dh_resid):
    hb, g, u, a = res
    dg, du = ffn_da_act(f"{tag}_bwd_da_act", df, wd, g, u)
    dwd = mm_tn(f"{tag}_bwd_dwd", a, df, BF16)
    dh = mm(f"{tag}_bwd_dh", [(dg, wg, 'nt'), (du, wu, 'nt')], D_MODEL, add=dh_resid, tn=512)
    dwg = mm_tn(f"{tag}_bwd_dwg", hb, dg, BF16)
    dwu = mm_tn(f"{tag}_bwd_dwu", hb, du, BF16)
    return dh, dwg, dwu, dwd


def rope_tables(positions):
    inv_freq = ROPE_THETA ** (-jnp.arange(0, ROPE_DIM, 2, dtype=F32) / ROPE_DIM)
    ang = positions.reshape(-1, 1).astype(F32) * inv_freq
    c, s = jnp.cos(ang), jnp.sin(ang)
    t = ang.shape[0]
    cosv = jnp.concatenate([c, c, jnp.ones((t, HEAD_DIM - ROPE_DIM), F32)], axis=1)
    sinv = jnp.concatenate([-s, s, jnp.zeros((t, HEAD_DIM - ROPE_DIM), F32)], axis=1)
    return jnp.tile(cosv, (1, 2)), jnp.tile(sinv, (1, 2))


def _pair_masks():
    lane = lax.broadcasted_iota(jnp.int32, (1, LANES), 1)
    return (lane < HEAD_DIM, lane >= HEAD_DIM)


def _band_masks():
    row = lax.broadcasted_iota(jnp.int32, (ATTN_BLOCK, ATTN_BLOCK), 0)
    col = lax.broadcasted_iota(jnp.int32, (ATTN_BLOCK, ATTN_BLOCK), 1)
    return col >= row, col <= row


def _residue_blocks():
    out = []
    for g, d in enumerate(DILATIONS):
        for r in range(d):
            for i in range(SEQ // d // ATTN_BLOCK):
                rows = lambda j: pl.ds(r + j * ATTN_BLOCK * d, ATTN_BLOCK, stride=d) if d > 1 else pl.ds(j * ATTN_BLOCK, ATTN_BLOCK)
                out.append((g, rows(i), rows(i - 1) if i > 0 else None))
    return out


N_HEAD_PAIRS = D_ATTN // LANES
SCALE = HEAD_DIM ** -0.5
ATTN_GROUP = 4


def _block_operands(qr, kr, v_ref, cur, prev):
    prev_ok, cur_ok = _band_masks()
    if prev is None:
        return qr[cur, :], kr[cur, :].astype(BF16), v_ref[cur, :], cur_ok
    kcat = jnp.concatenate([kr[prev, :], kr[cur, :]], axis=0).astype(BF16)
    vcat = jnp.concatenate([v_ref[prev, :], v_ref[cur, :]], axis=0)
    return qr[cur, :], kcat, vcat, jnp.concatenate([prev_ok, cur_ok], axis=1)


def _attn_specs(b):
    col = lambda cb: pl.BlockSpec((SEQ, LANES), lambda bb, hp: (bb, cb + hp))
    tab = pl.BlockSpec((SEQ, LANES), lambda bb, hp: (bb, 0))
    return col, tab


def attn_fwd(qkvz, cosv, sinv, b):
    t = qkvz.shape[0]
    col, tab = _attn_specs(b)
    blocks = _residue_blocks()

    def body(q_ref, k_ref, v_ref, c_ref, s_ref, o_ref, l1_ref, l2_ref, l3_ref, qr, kr, o1, o2, o3):
        l_refs, o_scr = (l1_ref, l2_ref, l3_ref), (o1, o2, o3)
        c, s = c_ref[...], s_ref[...]
        q, k = q_ref[...], k_ref[...]
        qr[...] = q * c + _rot(q) * s
        kr[...] = k * c + _rot(k) * s
        masks = _pair_masks()
        for lo in range(0, len(blocks), ATTN_GROUP):
            chains = []
            for g, cur, prev in blocks[lo:lo + ATTN_GROUP]:
                q2, kcat, vcat, ok = _block_operands(qr, kr, v_ref, cur, prev)
                for m in masks:
                    qm = jnp.where(m, q2, 0.0).astype(BF16)
                    chains.append(dict(g=g, cur=cur, m=m, v=jnp.where(m, vcat, 0.0).astype(BF16),
                                       s=jnp.where(ok, _dot(qm, kcat, _NT) * SCALE, NEG)))
            for ch in chains:
                mx = jnp.max(ch['s'], axis=1, keepdims=True)
                p = jnp.exp(ch['s'] - mx)
                den = jnp.sum(p, axis=1, keepdims=True)
                ch.update(p=p.astype(BF16), inv=1.0 / den, lse=mx + jnp.log(den))
            for ch in chains:
                ch['o'] = _dot(ch['p'], ch['v'], _NN) * ch['inv']
            for c0, c1 in zip(chains[0::2], chains[1::2]):
                o_scr[c0['g']][c0['cur'], :] = c0['o'] + c1['o']
                l_refs[c0['g']][c0['cur'], :] = jnp.where(c0['m'], c0['lse'], c1['lse'])
        w1, w2, w3 = _branch_weights(l1_ref[...], l2_ref[...], l3_ref[...])
        o_ref[...] = w1 * o1[...] + w2 * o2[...] + w3 * o3[...]

    shp = jax.ShapeDtypeStruct((t, D_ATTN), F32)
    return pl.pallas_call(
        body, name="attn_fwd", grid=(b, N_HEAD_PAIRS),
        in_specs=[col(0), col(N_HEAD_PAIRS), col(2 * N_HEAD_PAIRS), tab, tab],
        out_specs=[col(0)] * 4, out_shape=[shp] * 4,
        scratch_shapes=[pltpu.VMEM((SEQ, LANES), F32)] * 5,
        compiler_params=_cparams(("parallel", "parallel")),
    )(qkvz, qkvz, qkvz, cosv, sinv)


def attn_bwd(qkvz, cosv, sinv, dmix, mixed, lses, b):
    t = qkvz.shape[0]
    col, tab = _attn_specs(b)
    blocks = _residue_blocks()
    hd = np.arange(LANES) // HEAD_DIM
    head_ones = jnp.asarray((hd[:, None] == hd[None, :]).astype(np.float32))

    def body(q_ref, k_ref, v_ref, c_ref, s_ref, dm_ref, mx_ref, l1_ref, l2_ref, l3_ref, ones_ref,
             dq_out, dk_out, dv_out, qr, kr, do1, do2, do3, dd1, dd2, dd3, dq_ref, dk_ref, dv_ref):
        l_refs, do_scr, dd_scr = (l1_ref, l2_ref, l3_ref), (do1, do2, do3), (dd1, dd2, dd3)
        c, s = c_ref[...], s_ref[...]
        q, k = q_ref[...], k_ref[...]
        qr[...] = q * c + _rot(q) * s
        kr[...] = k * c + _rot(k) * s
        dm = dm_ref[...]
        tot = _dot(dm * mx_ref[...], ones_ref[...], _NN, HI)
        for w, do_g, dd_g in zip(_branch_weights(l1_ref[...], l2_ref[...], l3_ref[...]), do_scr, dd_scr):
            do_g[...] = w * dm
            dd_g[...] = w * tot
        dq_ref[...] = jnp.zeros((SEQ, LANES), F32)
        dk_ref[...] = jnp.zeros((SEQ, LANES), F32)
        dv_ref[...] = jnp.zeros((SEQ, LANES), F32)
        masks = _pair_masks()
        for lo in range(0, len(blocks), ATTN_GROUP):
            chains = []
            for g, cur, prev in blocks[lo:lo + ATTN_GROUP]:
                q2, kcat, vcat, ok = _block_operands(qr, kr, v_ref, cur, prev)
                vcat = vcat.astype(BF16)
                do2_, l2, dd2_ = do_scr[g][cur, :], l_refs[g][cur, :], dd_scr[g][cur, :]
                l2s, dd2s = pltpu.roll(l2, HEAD_DIM, 1), pltpu.roll(dd2_, HEAD_DIM, 1)
                for m in masks:
                    qm = jnp.where(m, q2, 0.0).astype(BF16)
                    dom = jnp.where(m, do2_, 0.0).astype(BF16)
                    lrep, ddrep = jnp.where(m, l2, l2s), jnp.where(m, dd2_, dd2s)
                    if prev is not None:
                        lrep, ddrep = jnp.concatenate([lrep, lrep], axis=1), jnp.concatenate([ddrep, ddrep], axis=1)
                    chains.append(dict(cur=cur, prev=prev, qm=qm, dom=dom, km=jnp.where(m, kcat, 0), lrep=lrep, ddrep=ddrep,
                                       s=jnp.where(ok, _dot(qm, kcat, _NT) * SCALE, NEG), dp=_dot(dom, vcat, _NT)))
            for ch in chains:
                p = jnp.exp(ch['s'] - ch['lrep'])
                ch.update(p=p.astype(BF16), ds=(p * (ch['dp'] - ch['ddrep']) * SCALE).astype(BF16))
            for ch in chains:
                ch.update(dq=_dot(ch['ds'], ch['km'], _NN), dk=_dot(ch['ds'], ch['qm'], _TN), dv=_dot(ch['p'], ch['dom'], _TN))
            for c0, c1 in zip(chains[0::2], chains[1::2]):
                cur, prev = c0['cur'], c0['prev']
                dk, dv = c0['dk'] + c1['dk'], c0['dv'] + c1['dv']
                dq_ref[cur, :] += c0['dq'] + c1['dq']
                if prev is None:
                    dk_ref[cur, :] += dk
                    dv_ref[cur, :] += dv
                else:
                    dk_ref[prev, :] += dk[:ATTN_BLOCK]
                    dv_ref[prev, :] += dv[:ATTN_BLOCK]
                    dk_ref[cur, :] += dk[ATTN_BLOCK:]
                    dv_ref[cur, :] += dv[ATTN_BLOCK:]
        dq, dk = dq_ref[...], dk_ref[...]
        dq_out[...] = (dq * c + _rot(dq * s)).astype(dq_out.dtype)
        dk_out[...] = (dk * c + _rot(dk * s)).astype(dk_out.dtype)
        dv_out[...] = dv_ref[...].astype(dv_out.dtype)

    shp = jax.ShapeDtypeStruct((t, D_ATTN), BF16)
    return pl.pallas_call(
        body, name="attn_bwd", grid=(b, N_HEAD_PAIRS),
        in_specs=[col(0), col(N_HEAD_PAIRS), col(2 * N_HEAD_PAIRS), tab, tab, col(0), col(0), col(0), col(0), col(0),
                  pl.BlockSpec((LANES, LANES), lambda bb, hp: (0, 0))],
        out_specs=[col(0)] * 3, out_shape=[shp] * 3,
        scratch_shapes=[pltpu.VMEM((SEQ, LANES), F32)] * 11,
        compiler_params=_cparams(("parallel", "parallel")),
    )(qkvz, qkvz, qkvz, cosv, sinv, dmix, mixed, *lses, head_ones)


def attn_norm_fwd(mixed, norm_w):
    return rowwise("attn_norm", _rms, [mixed], [norm_w], [(mixed.shape[0], D_ATTN, BF16)])[0]


def attn_norm_bwd(dout, mixed, norm_w):
    def fn(dy, mx, w):
        _, vjp = jax.vjp(_rms, mx, w)
        return vjp(dy)

    return rowwise("attn_norm_bwd", fn, [dout, mixed], [norm_w], [(dout.shape[0], D_ATTN, F32)], accs=[(1, D_ATTN)])


CONV_TM = 256
HALO = 8


def _conv_columns(refs):
    xs_ref, bm_ref, cm_ref = refs
    out = []
    for c in range(D_CONV // LANES):
        lo = c * LANES
        ref, base = (xs_ref, 0) if lo < D_SSD else (bm_ref, D_SSD) if lo < D_SSD + D_BC else (cm_ref, D_SSD + D_BC)
        out.append((slice(lo, lo + LANES), (ref, slice(lo - base, lo - base + LANES))))
    return out


def _conv_taps(scr, w_ref, cs, first_row, step, tm):
    acc = None
    for k in range(CONV_WIDTH):
        term = w_ref[k:k + 1, cs] * scr[pl.ds(first_row + step * k, tm), cs]
        acc = term if acc is None else acc + term
    return acc


def conv_fwd(u, w, bias):
    t = u.shape[0]
    tm, per_seq = CONV_TM, SEQ // CONV_TM

    def body(u_ref, h_ref, w_ref, b_ref, xs_ref, bm_ref, cm_ref, scr):
        first = pl.program_id(0) % per_seq == 0
        scr[0:HALO, :] = jnp.where(first, 0.0, h_ref[...])
        scr[HALO:, :] = u_ref[...]
        for cs, (o_ref, os_) in _conv_columns((xs_ref, bm_ref, cm_ref)):
            o_ref[:, os_] = _silu(_conv_taps(scr, w_ref, cs, HALO - CONV_WIDTH + 1, 1, tm) + b_ref[:, cs])

    return pl.pallas_call(
        body, name="conv_fwd", grid=(t // tm,),
        in_specs=[pl.BlockSpec((tm, D_CONV), lambda i: (i, 0)),
                  pl.BlockSpec((HALO, D_CONV), lambda i: (jnp.maximum(i * (tm // HALO) - 1, 0), 0)),
                  pl.BlockSpec((CONV_WIDTH, D_CONV), lambda i: (0, 0)), pl.BlockSpec((1, D_CONV), lambda i: (0, 0))],
        out_specs=[pl.BlockSpec((tm, D_SSD), lambda i: (i, 0)), pl.BlockSpec((tm, D_BC), lambda i: (i, 0)),
                   pl.BlockSpec((tm, D_BC), lambda i: (i, 0))],
        out_shape=[jax.ShapeDtypeStruct((t, D_SSD), F32), jax.ShapeDtypeStruct((t, D_BC), F32),
                   jax.ShapeDtypeStruct((t, D_BC), F32)],
        scratch_shapes=[pltpu.VMEM((tm + HALO, D_CONV), F32)],
        compiler_params=_cparams(("parallel",)),
    )(u, u, w, bias)


def conv_bwd(u, w, bias, dxs_a, dxs_b, dbm, dcm):
    t = u.shape[0]
    tm, per_seq = CONV_TM, SEQ // CONV_TM
    n_tiles = t // tm

    def body1(u_ref, h_ref, dxs_ref, dxs2_ref, dbm_ref, dcm_ref, w_ref, b_ref, dz_ref, dw_ref, db_ref, scr):
        i = pl.program_id(0)
        first = i % per_seq == 0
        scr[0:HALO, :] = jnp.where(first, 0.0, h_ref[...])
        scr[HALO:, :] = u_ref[...]

        @pl.when(i == 0)
        def _():
            dw_ref[...] = jnp.zeros(dw_ref.shape, F32)
            db_ref[...] = jnp.zeros(db_ref.shape, F32)
        for cs, (g_ref, gs) in _conv_columns((dxs_ref, dbm_ref, dcm_ref)):
            acc = _conv_taps(scr, w_ref, cs, HALO - CONV_WIDTH + 1, 1, tm) + b_ref[:, cs]
            sig = _sigmoid(acc)
            dy = g_ref[:, gs] + dxs2_ref[:, gs] if g_ref is dxs_ref else g_ref[:, gs]
            dz = dy * sig * (1.0 + acc * (1.0 - sig))
            dz_ref[:, cs] = dz
            db_ref[:, cs] += jnp.sum(dz, axis=0, keepdims=True)
            for k in range(CONV_WIDTH):
                dw_ref[k:k + 1, cs] += jnp.sum(dz * scr[pl.ds(HALO - CONV_WIDTH + 1 + k, tm), cs], axis=0, keepdims=True)

    dz, dw, db = pl.pallas_call(
        body1, name="conv_bwd_dz", grid=(n_tiles,),
        in_specs=[pl.BlockSpec((tm, D_CONV), lambda i: (i, 0)),
                  pl.BlockSpec((HALO, D_CONV), lambda i: (jnp.maximum(i * (tm // HALO) - 1, 0), 0)),
                  pl.BlockSpec((tm, D_SSD), lambda i: (i, 0)), pl.BlockSpec((tm, D_SSD), lambda i: (i, 0)),
                  pl.BlockSpec((tm, D_BC), lambda i: (i, 0)), pl.BlockSpec((tm, D_BC), lambda i: (i, 0)),
                  pl.BlockSpec((CONV_WIDTH, D_CONV), lambda i: (0, 0)), pl.BlockSpec((1, D_CONV), lambda i: (0, 0))],
        out_specs=[pl.BlockSpec((tm, D_CONV), lambda i: (i, 0)), pl.BlockSpec((CONV_WIDTH, D_CONV), lambda i: (0, 0)),
                   pl.BlockSpec((1, D_CONV), lambda i: (0, 0))],
        out_shape=[jax.ShapeDtypeStruct((t, D_CONV), F32), jax.ShapeDtypeStruct((CONV_WIDTH, D_CONV), F32),
                   jax.ShapeDtypeStruct((1, D_CONV), F32)],
        scratch_shapes=[pltpu.VMEM((tm + HALO, D_CONV), F32)],
        compiler_params=_cparams(("arbitrary",)),
    )(u, u, dxs_a, dxs_b, dbm, dcm, w, bias)

    def body2(dz_ref, n_ref, w_ref, du_ref, scr):
        last = pl.program_id(0) % per_seq == per_seq - 1
        scr[0:tm, :] = dz_ref[...]
        scr[tm:, :] = jnp.where(last, 0.0, n_ref[...])
        for c in range(D_CONV // LANES):
            cs = slice(c * LANES, (c + 1) * LANES)
            du_ref[:, cs] = _conv_taps(scr, w_ref, cs, CONV_WIDTH - 1, -1, tm).astype(du_ref.dtype)

    du = pl.pallas_call(
        body2, name="conv_bwd_du", grid=(n_tiles,),
        in_specs=[pl.BlockSpec((tm, D_CONV), lambda i: (i, 0)),
                  pl.BlockSpec((HALO, D_CONV), lambda i: (jnp.minimum((i + 1) * (tm // HALO), t // HALO - 1), 0)),
                  pl.BlockSpec((CONV_WIDTH, D_CONV), lambda i: (0, 0))],
        out_specs=pl.BlockSpec((tm, D_CONV), lambda i: (i, 0)),
        out_shape=jax.ShapeDtypeStruct((t, D_CONV), BF16),
        scratch_shapes=[pltpu.VMEM((tm + HALO, D_CONV), F32)],
        compiler_params=_cparams(("parallel",)),
    )(dz, dz, w)
    return du, dw, db


Q = SSD_CHUNK
N_PAIRS = D_SSD // LANES
HEADS_PER_GROUP = N_HEADS // SSD_GROUPS


def _rep(a, j):
    return jnp.broadcast_to(a[:, j:j + 1], a.shape)


def _dot_exact01(a, b, dn, a_is_01):
    x = b if a_is_01 else a
    hi = x.astype(BF16)
    mid = (x - hi.astype(F32)).astype(BF16)
    lo = (x - hi.astype(F32) - mid.astype(F32)).astype(BF16)
    z = a.astype(BF16) if a_is_01 else b.astype(BF16)
    out = None
    for term in (hi, mid, lo):
        d = _dot(z, term, dn) if a_is_01 else _dot(term, z, dn)
        out = d if out is None else out + d
    return out


def _pad_lanes(v, fill=0.0):
    row = jnp.pad(v.reshape(1, -1).astype(F32), ((0, 0), (0, LANES - v.size)), constant_values=fill)
    return row, row.reshape(LANES, 1)


def _ssd_common(dtr_ref, dtrt_ref, bias_r, bias_c, alog_r, alog_c):
    row = lax.broadcasted_iota(jnp.int32, (Q, Q), 0)
    col = lax.broadcasted_iota(jnp.int32, (Q, Q), 1)
    tril = row >= col
    lane = lax.broadcasted_iota(jnp.int32, (1, LANES), 1)
    a_r = jnp.where(lane < N_HEADS, -jnp.exp(alog_r[...]), 0.0)
    sub = lax.broadcasted_iota(jnp.int32, (LANES, 1), 0)
    a_c = jnp.where(sub < N_HEADS, -jnp.exp(alog_c[...]), 0.0)
    dt = _softplus(dtr_ref[...] + bias_r[...])
    cs = _dot_exact01(tril, dt * a_r, _NN, True)
    dtt = _softplus(dtrt_ref[...] + bias_c[...])
    cst = _dot_exact01(dtt * a_c, row <= col, _NN, False)
    return tril, lane, a_r, dt, cs, cst


def _ssd_specs(b, nc, rev):
    ci = (lambda c: nc - 1 - c) if rev else (lambda c: c)
    rows = lambda w: pl.BlockSpec((Q, w), lambda bb, c: (bb * nc + ci(c), 0))
    dtt = pl.BlockSpec((LANES, Q), lambda bb, c: (0, bb * nc + ci(c)))
    const = lambda s: pl.BlockSpec(s, lambda bb, c: (0,) * len(s))
    state = pl.BlockSpec((None, N_PAIRS, LANES, SSD_STATE), lambda bb, c: (bb * nc + ci(c), 0, 0, 0))
    return rows, dtt, const, state


def ssd_fwd(xs, bm, cm, dtraw, dt_bias, a_log, b):
    t = xs.shape[0]
    nc = SEQ // Q
    rows, dtt_spec, const, state = _ssd_specs(b, nc, False)
    bias_r, bias_c = _pad_lanes(dt_bias)
    alog_r, alog_c = _pad_lanes(a_log)

    def body(xs_ref, b_ref, c_ref, dtr_ref, dtrt_ref, br, bc, ar, ac, y_ref, hp_ref, h_scr):
        @pl.when(pl.program_id(1) == 0)
        def _():
            h_scr[...] = jnp.zeros(h_scr.shape, F32)
        tril, lane, _, dt, cs, cst = _ssd_common(dtr_ref, dtrt_ref, br, bc, ar, ac)
        sub = lax.broadcasted_iota(jnp.int32, (LANES, 1), 0)
        y_acc = [jnp.zeros((Q, LANES), F32) for _ in range(N_PAIRS)]
        h_old = [h_scr[p] for p in range(N_PAIRS)]
        h_new = [jnp.zeros((LANES, SSD_STATE), F32) for _ in range(N_PAIRS)]
        for g in range(SSD_GROUPS):
            bg = b_ref[:, g * SSD_STATE:(g + 1) * SSD_STATE].astype(BF16)
            cg = c_ref[:, g * SSD_STATE:(g + 1) * SSD_STATE].astype(BF16)
            cb = _dot(cg, bg, _NT)
            for j in range(g * HEADS_PER_GROUP, (g + 1) * HEADS_PER_GROUP):
                p, side = j // 2, j % 2
                m = (lane < HEAD_DIM) if side == 0 else (lane >= HEAD_DIM)
                ms = (sub < HEAD_DIM) if side == 0 else (sub >= HEAD_DIM)
                csj, dtj = _rep(cs, j), _rep(dt, j)
                lmat = jnp.exp(jnp.where(tril, csj - cst[j:j + 1, :], NEG))
                xdt = jnp.where(m, xs_ref[:, p * LANES:(p + 1) * LANES] * dtj, 0.0)
                hm = jnp.where(ms, h_old[p], 0.0)
                ydiag = _dot((cb * lmat).astype(BF16), xdt.astype(BF16), _NN)
                yoff = jnp.exp(csj) * _dot(cg, hm.astype(BF16), _NT)
                y_acc[p] = y_acc[p] + ydiag + yoff
                last = csj[Q - 1:Q, :]
                sj = _dot((xdt * jnp.exp(last - csj)).astype(BF16), bg, _TN)
                h_new[p] = h_new[p] + jnp.exp(last) * hm + sj
        for p in range(N_PAIRS):
            y_ref[:, p * LANES:(p + 1) * LANES] = y_acc[p]
            hp_ref[p] = h_old[p]
            h_scr[p] = h_new[p]

    return pl.pallas_call(
        body, name="ssd_fwd", grid=(b, nc),
        in_specs=[rows(D_SSD), rows(D_BC), rows(D_BC), rows(LANES), dtt_spec, const((1, LANES)), const((LANES, 1)),
                  const((1, LANES)), const((LANES, 1))],
        out_specs=[rows(D_SSD), state],
        out_shape=[jax.ShapeDtypeStruct((t, D_SSD), F32),
                   jax.ShapeDtypeStruct((b * nc, N_PAIRS, LANES, SSD_STATE), F32)],
        scratch_shapes=[pltpu.VMEM((N_PAIRS, LANES, SSD_STATE), F32)],
        compiler_params=_cparams(("parallel", "arbitrary")),
    )(xs, bm, cm, dtraw, dtraw.T, bias_r, bias_c, alog_r, alog_c)


def ssd_bwd(xs, bm, cm, dtraw, dt_bias, a_log, hprev, dy, b):
    t = xs.shape[0]
    nc = SEQ // Q
    rows, dtt_spec, const, state = _ssd_specs(b, nc, True)
    bias_r, bias_c = _pad_lanes(dt_bias)
    alog_r, alog_c = _pad_lanes(a_log)

    def body(xs_ref, b_ref, c_ref, dtr_ref, dtrt_ref, hp_ref, dy_ref, br, bc, ar, ac,
             dxs_ref, db_ref, dc_ref, ddt_ref, dbias_ref, dalog_ref, dh_scr):
        first = jnp.logical_and(pl.program_id(0) == 0, pl.program_id(1) == 0)

        @pl.when(pl.program_id(1) == 0)
        def _():
            dh_scr[...] = jnp.zeros(dh_scr.shape, F32)

        @pl.when(first)
        def _():
            dbias_ref[...] = jnp.zeros(dbias_ref.shape, F32)
            dalog_ref[...] = jnp.zeros(dalog_ref.shape, F32)
        tril, lane, a_r, dt, cs, cst = _ssd_common(dtr_ref, dtrt_ref, br, bc, ar, ac)
        sub = lax.broadcasted_iota(jnp.int32, (LANES, 1), 0)
        rowq = lax.broadcasted_iota(jnp.int32, (Q, 1), 0)
        triu = (lax.broadcasted_iota(jnp.int32, (Q, Q), 0) <= lax.broadcasted_iota(jnp.int32, (Q, Q), 1)).astype(F32)
        dxs_acc = [jnp.zeros((Q, LANES), F32) for _ in range(N_PAIRS)]
        dh_in = [dh_scr[p] for p in range(N_PAIRS)]
        h_in = [hp_ref[p] for p in range(N_PAIRS)]
        dh_out = [jnp.zeros((LANES, SSD_STATE), F32) for _ in range(N_PAIRS)]
        ddt = jnp.zeros((Q, LANES), F32)
        dalog = jnp.zeros((1, LANES), F32)
        for g in range(SSD_GROUPS):
            gs = slice(g * SSD_STATE, (g + 1) * SSD_STATE)
            bg, cg = b_ref[:, gs].astype(BF16), c_ref[:, gs].astype(BF16)
            cb = _dot(cg, bg, _NT)
            dcb = jnp.zeros((Q, Q), F32)
            dbg = jnp.zeros((Q, SSD_STATE), F32)
            dcg = jnp.zeros((Q, SSD_STATE), F32)
            for j in range(g * HEADS_PER_GROUP, (g + 1) * HEADS_PER_GROUP):
                p, side = j // 2, j % 2
                m = (lane < HEAD_DIM) if side == 0 else (lane >= HEAD_DIM)
                ms = (sub < HEAD_DIM) if side == 0 else (sub >= HEAD_DIM)
                csj, dtj = _rep(cs, j), _rep(dt, j)
                lmat = jnp.exp(jnp.where(tril, csj - cst[j:j + 1, :], NEG))
                x2 = jnp.where(m, xs_ref[:, p * LANES:(p + 1) * LANES], 0.0)
                xdt = x2 * dtj
                dym = jnp.where(m, dy_ref[:, p * LANES:(p + 1) * LANES], 0.0)
                hm = jnp.where(ms, h_in[p], 0.0)
                dhm = jnp.where(ms, dh_in[p], 0.0)
                ecs = jnp.exp(csj)
                last = csj[Q - 1:Q, :]
                decay = jnp.exp(last - csj)
                el = jnp.exp(last)
                gmat = cb * lmat
                dymb, xdtb = dym.astype(BF16), xdt.astype(BF16)
                dg = _dot(dymb, xdtb, _NT)
                dxdt = _dot(gmat.astype(BF16), dymb, _TN)
                dcb = dcb + dg * lmat
                ej = dg * gmat
                col_sums = jnp.broadcast_to(jnp.sum(ej, axis=0, keepdims=True), (Q, Q)).T
                dcs = jnp.sum(ej, axis=1, keepdims=True) - col_sums
                ch = _dot(cg, hm.astype(BF16), _NT)
                dye = dym * ecs
                dcs = dcs + jnp.sum(dye * ch, axis=1, keepdims=True)
                dcg = dcg + _dot(dye.astype(BF16), hm.astype(BF16), _NN)
                dhp = _dot(dye.astype(BF16), cg, _TN)
                wmat = _dot(bg, dhm.astype(BF16), _NT)
                xd = xdt * decay
                dxdt = dxdt + decay * wmat
                ddl = jnp.sum(xd * wmat, axis=1, keepdims=True)
                dlast = jnp.sum(ddl, axis=0, keepdims=True) + el * jnp.sum(jnp.sum(dhm * hm, axis=1, keepdims=True), axis=0, keepdims=True)
                dcs = dcs - ddl + jnp.where(rowq == Q - 1, dlast, 0.0)
                dbg = dbg + _dot(xd.astype(BF16), dhm.astype(BF16), _NN)
                dh_out[p] = dh_out[p] + el * dhm + dhp
                da = _dot_exact01(triu, dcs, _NN, True)
                aj = jnp.sum(jnp.where(lane == j, a_r, 0.0), axis=1, keepdims=True)
                ddtj = da * aj + jnp.sum(dxdt * x2, axis=1, keepdims=True)
                ddt = ddt + jnp.where(lane == j, ddtj, 0.0)
                dalog = dalog + jnp.where(lane == j, jnp.sum(da * dtj, axis=0, keepdims=True) * aj, 0.0)
                dxs_acc[p] = dxs_acc[p] + dxdt * dtj
            dcbb = dcb.astype(BF16)
            dc_ref[:, gs] = dcg + _dot(dcbb, bg, _NN)
            db_ref[:, gs] = dbg + _dot(dcbb, cg, _TN)
        for p in range(N_PAIRS):
            dxs_ref[:, p * LANES:(p + 1) * LANES] = dxs_acc[p]
            dh_scr[p] = dh_out[p]
        ddtraw = ddt * _sigmoid(dtr_ref[...] + br[...])
        ddt_ref[...] = ddtraw
        dbias_ref[...] += jnp.sum(ddtraw, axis=0, keepdims=True)
        dalog_ref[...] += dalog

    return pl.pallas_call(
        body, name="ssd_bwd", grid=(b, nc),
        in_specs=[rows(D_SSD), rows(D_BC), rows(D_BC), rows(LANES), dtt_spec, state, rows(D_SSD), const((1, LANES)),
                  const((LANES, 1)), const((1, LANES)), const((LANES, 1))],
        out_specs=[rows(D_SSD), rows(D_BC), rows(D_BC), rows(LANES), const((1, LANES)), const((1, LANES))],
        out_shape=[jax.ShapeDtypeStruct((t, D_SSD), F32), jax.ShapeDtypeStruct((t, D_BC), F32),
                   jax.ShapeDtypeStruct((t, D_BC), F32), jax.ShapeDtypeStruct((t, LANES), F32),
                   jax.ShapeDtypeStruct((1, LANES), F32), jax.ShapeDtypeStruct((1, LANES), F32)],
        scratch_shapes=[pltpu.VMEM((N_PAIRS, LANES, SSD_STATE), F32)],
        compiler_params=_cparams(("arbitrary", "arbitrary")),
    )(xs, bm, cm, dtraw, dtraw.T, hprev, dy, bias_r, bias_c, alog_r, alog_c)


def _split_w_in(w_in):
    w_dt = jnp.pad(w_in[:, D_QKVZ + D_CONV:], ((0, 0), (0, LANES - N_HEADS)))
    return w_in[:, :D_QKVZ], w_in[:, D_QKVZ:D_QKVZ + D_CONV], w_dt


def mixer_fwd(hb, p, cosv, sinv, b):
    t = hb.shape[0]
    w_a, w_b, w_c = _split_w_in(p['w_in'])
    qkvz = mm("in_qkvz", [(hb, w_a, 'nn')], D_QKVZ)
    xbc = mm("in_xbc", [(hb, w_b, 'nn')], D_CONV)
    dtraw = mm("in_dt", [(hb, w_c, 'nn')], LANES)
    mixed, *lses = attn_fwd(qkvz, cosv, sinv, b)
    attn = attn_norm_fwd(mixed, p['attn_norm_w'])
    xs, bm, cm = conv_fwd(xbc, p['conv_w'], p['conv_b'])
    y, hprev = ssd_fwd(xs, bm, cm, dtraw, p['dt_bias'], p['a_log'], b)
    dskip = jnp.repeat(p['d_skip'].reshape(-1), HEAD_DIM).reshape(1, D_SSD)
    yg, = rowwise("ssd_gate", _gate, [y, xs, Op(qkvz, D_SSD, 3)], [dskip, p['ssd_norm_w']], [(t, D_SSD, BF16)])
    mix = mm("out_proj", [(attn, p['w_out'][:D_ATTN], 'nn'), (yg, p['w_out'][D_ATTN:], 'nn')], D_MODEL)
    res = dict(hb=hb, qkvz=qkvz, xbc=xbc, dtraw=dtraw, mixed=mixed, lses=lses, attn=attn, xs=xs, bm=bm, cm=cm,
               y=y, hprev=hprev, dskip=dskip, yg=yg, cosv=cosv, sinv=sinv)
    return mix, res


def mixer_bwd(r, p, dmix, dh_resid, b):
    t = dmix.shape[0]
    w_a, w_b, w_c = _split_w_in(p['w_in'])
    w_out = p['w_out']
    dattn = mm("out_bwd_dattn", [(dmix, w_out[:D_ATTN], 'nt')], D_ATTN)
    dyg = mm("out_bwd_dyg", [(dmix, w_out[D_ATTN:], 'nt')], D_SSD)
    dw_out = jnp.concatenate([mm_tn("out_bwd_dw_a", r['attn'], dmix, BF16),
                              mm_tn("out_bwd_dw_y", r['yg'], dmix, BF16)], axis=0)

    def gate_bwd(dy_, y_, xs_, z_, ds_, w_):
        _, vjp = jax.vjp(_gate, y_, xs_, z_, ds_, w_)
        return vjp(dy_)

    dy, dxs_a, dz, ddskip, dssd_norm = rowwise(
        "ssd_gate_bwd", gate_bwd, [dyg, r['y'], r['xs'], Op(r['qkvz'], D_SSD, 3)], [r['dskip'], p['ssd_norm_w']],
        [(t, D_SSD, F32), (t, D_SSD, F32), (t, D_SSD, BF16)], accs=[(1, D_SSD), (1, D_SSD)])
    dxs_b, dbm, dcm, ddtraw, ddt_bias, da_log = ssd_bwd(r['xs'], r['bm'], r['cm'], r['dtraw'], p['dt_bias'], p['a_log'],
                                                        r['hprev'], dy, b)
    dxbc, dconv_w, dconv_b = conv_bwd(r['xbc'], p['conv_w'], p['conv_b'], dxs_a, dxs_b, dbm, dcm)
    dmixed, dattn_norm = attn_norm_bwd(dattn, r['mixed'], p['attn_norm_w'])
    dq, dk, dv = attn_bwd(r['qkvz'], r['cosv'], r['sinv'], dmixed, r['mixed'], r['lses'], b)
    wq, wk, wv, wz = (w_a[:, i * D_ATTN:(i + 1) * D_ATTN] for i in range(4))
    dh = mm("in_bwd_dh", [(dq, wq, 'nt'), (dk, wk, 'nt'), (dv, wv, 'nt'), (dz, wz, 'nt'), (dxbc, w_b, 'nt'),
                          (ddtraw, w_c, 'nt')], D_MODEL, add=dh_resid, tn=512)
    h = r['hb']
    dw_in = jnp.concatenate([mm_tn("in_bwd_dwq", h, dq, BF16), mm_tn("in_bwd_dwk", h, dk, BF16),
                             mm_tn("in_bwd_dwv", h, dv, BF16), mm_tn("in_bwd_dwz", h, dz, BF16),
                             mm_tn("in_bwd_dwx", h, dxbc, BF16), mm_tn("in_bwd_dwdt", h, ddtraw, BF16)[:, :N_HEADS]], axis=1)
    head_sum = lambda v: v.reshape(N_HEADS, HEAD_DIM).sum(axis=1).reshape(1, N_HEADS)
    grads = dict(w_in=dw_in, w_out=dw_out, conv_w=dconv_w, conv_b=dconv_b, dt_bias=ddt_bias[:, :N_HEADS],
                 a_log=da_log[:, :N_HEADS], d_skip=head_sum(ddskip), attn_norm_w=dattn_norm, ssd_norm_w=dssd_norm)
    return dh, grads


FFN1_KEYS = ('ffn1_gate', 'ffn1_up', 'ffn1_down')
FFN2_KEYS = ('ffn2_gate', 'ffn2_up', 'ffn2_down')
MIXER_KEYS = ('w_in', 'conv_w', 'w_out')
FFN_COL = ('ffn1_gate', 'ffn1_up', 'ffn2_gate', 'ffn2_up')
FFN_ROW = ('ffn1_down', 'ffn2_down')
CONV_W_COMM = (8, 2 * LANES)
SMALL = 'small'


def comm_shape(k, shapes):
    if k in FFN_COL:
        return (D_MODEL, FF_PAD)
    if k in FFN_ROW:
        return (FF_PAD, D_MODEL)
    if k == 'conv_w':
        return CONV_W_COMM
    if k == SMALL:
        n = sum(int(np.prod(shapes[r])) for r in REPLICATED)
        return (-(-n // (8 * LANES)) * 8, LANES)
    return tuple(shapes[k][1:])


def to_comm(k, vals, shapes):
    if k == SMALL:
        flat = jnp.concatenate([vals[r].reshape(-1) for r in REPLICATED])
        r_, c_ = comm_shape(k, shapes)
        return jnp.pad(flat, (0, r_ * c_ - flat.size)).reshape(r_, c_)
    a = vals[k].reshape(shapes[k][1:])
    r_, c_ = comm_shape(k, shapes)
    return jnp.pad(a, ((0, r_ - a.shape[0]), (0, c_ - a.shape[1])))


def from_comm(k, a, shapes):
    if k == SMALL:
        flat, out, off = a.reshape(-1), {}, 0
        for r in REPLICATED:
            n = int(np.prod(shapes[r]))
            out[r] = flat[off:off + n].reshape(shapes[r])
            off += n
        return out
    shp = shapes[k][1:]
    return {k: a[:shp[0], :shp[1]].reshape(shapes[k])}


def full_weight(k, g):
    if k in FFN_COL:
        return jnp.concatenate([g[p] for p in range(N_DEV)], axis=1)
    if k == 'conv_w':
        return jnp.transpose(g[:, :CONV_WIDTH, :D_CONV // N_DEV], (1, 0, 2)).reshape(CONV_WIDTH, D_CONV)
    return g.reshape(N_DEV * g.shape[1], g.shape[2])


def grad_shards(k, g):
    if k in FFN_COL:
        return jnp.stack([g[:, p * FF_PAD:(p + 1) * FF_PAD] for p in range(N_DEV)])
    if k == 'conv_w':
        s = jnp.transpose(g.reshape(CONV_WIDTH, N_DEV, D_CONV // N_DEV), (1, 0, 2))
        return jnp.pad(s, ((0, 0), (0, CONV_W_COMM[0] - CONV_WIDTH), (0, CONV_W_COMM[1] - D_CONV // N_DEV)))
    return g.reshape(N_DEV, g.shape[0] // N_DEV, g.shape[1])


def _flip(v, bit):
    return 1 - v if bit else v


N_PEER_COPIES = N_DEV - 1


def _comm_call(name, body, arrs, out_shape):
    n = len(arrs)
    return pl.pallas_call(
        functools.partial(body, n), name=name, out_shape=out_shape,
        in_specs=[pl.BlockSpec(memory_space=pl.ANY)] * n, out_specs=[pl.BlockSpec(memory_space=pl.ANY)] * n,
        scratch_shapes=[pltpu.SemaphoreType.DMA((n * N_PEER_COPIES,)), pltpu.SemaphoreType.DMA((n * N_PEER_COPIES,)),
                        pltpu.SemaphoreType.DMA((n,))],
    )(*arrs)


def all_gather(arrs):
    def body(n, *refs):
        x_refs, out_refs, (send_sems, recv_sems, local_sems) = refs[:n], refs[n:2 * n], refs[2 * n:]
        x, y, c = lax.axis_index("x"), lax.axis_index("y"), lax.axis_index("c")
        me, sibling = (x, y, c), (x, y, 1 - c)
        chips = [(1 - x, y), (x, 1 - y), (1 - x, 1 - y)]

        def copy(a, k, block, to, src=None):
            px, py, pc = block
            dst = out_refs[a].at[4 * px + 2 * py + pc]
            return pltpu.make_async_remote_copy(
                src_ref=dst if src is None else src, dst_ref=dst, send_sem=send_sems.at[a * N_PEER_COPIES + k],
                recv_sem=recv_sems.at[a * N_PEER_COPIES + k], device_id=to, device_id_type=MESH)

        mine = [pltpu.make_async_copy(x_refs[a], out_refs[a].at[4 * x + 2 * y + c], local_sems.at[a]) for a in range(n)]
        started = []
        for a in range(n):
            mine[a].start()
            first = [copy(a, 0, me, sibling, src=x_refs[a])]
            first += [copy(a, 1 + j, me, (*chip, c), src=x_refs[a]) for j, chip in enumerate(chips)]
            for cp in first:
                cp.start()
            started += first
        for j, chip in enumerate(chips):
            for a in range(n):
                copy(a, 1 + j, (*chip, c), me).wait_recv()
                cp = copy(a, 4 + j, (*chip, c), sibling)
                cp.start()
                started.append(cp)
        for a in range(n):
            copy(a, 0, sibling, me).wait_recv()
            for j, chip in enumerate(chips):
                copy(a, 4 + j, (*chip, 1 - c), me).wait_recv()
        for cp in started:
            cp.wait_send()
        for cp in mine:
            cp.wait()

    return _comm_call("all_gather_weights", body, arrs,
                      [jax.ShapeDtypeStruct((N_DEV,) + a.shape, a.dtype) for a in arrs])


def all_to_all(arrs):
    def body(n, *refs):
        s_refs, r_refs, (send_sems, recv_sems, local_sems) = refs[:n], refs[n:2 * n], refs[2 * n:]
        x, y, c = lax.axis_index("x"), lax.axis_index("y"), lax.axis_index("c")
        me = 4 * x + 2 * y + c

        def peer(k):
            return _flip(x, k & 4), _flip(y, k & 2), _flip(c, k & 1)

        def copy(a, k, landing):
            px, py, pc = peer(k)
            p = 4 * px + 2 * py + pc
            src, dst = (s_refs[a].at[me], r_refs[a].at[p]) if landing else (s_refs[a].at[p], r_refs[a].at[me])
            return pltpu.make_async_remote_copy(
                src_ref=src, dst_ref=dst, send_sem=send_sems.at[a * N_PEER_COPIES + k - 1],
                recv_sem=recv_sems.at[a * N_PEER_COPIES + k - 1], device_id=(px, py, pc), device_id_type=MESH)

        mine = [pltpu.make_async_copy(s_refs[a].at[me], r_refs[a].at[me], local_sems.at[a]) for a in range(n)]
        sends = [copy(a, k, False) for a in range(n) for k in range(1, N_DEV)]
        for cp in mine + sends:
            cp.start()
        for a in range(n):
            for k in range(1, N_DEV):
                copy(a, k, True).wait_recv()
        for cp in sends:
            cp.wait_send()
        for cp in mine:
            cp.wait()

    return _comm_call("all_to_all_grads", body, arrs, [jax.ShapeDtypeStruct(a.shape, a.dtype) for a in arrs])


_HBM = pl.BlockSpec(memory_space=pltpu.HBM)
_SEM = pl.BlockSpec(memory_space=pltpu.SEMAPHORE)
_EFFECT = pltpu.SideEffectType.DATAFLOW_SIDE_EFFECTING


def _peer(k):
    x, y, c = lax.axis_index("x"), lax.axis_index("y"), lax.axis_index("c")
    return _flip(x, k & 4), _flip(y, k & 2), _flip(c, k & 1)


def _my_index():
    return 4 * lax.axis_index("x") + 2 * lax.axis_index("y") + lax.axis_index("c")


def _split_copies(mode, n, src_refs, land_refs, send_sems, recv_sems):
    me = _my_index()
    out = []
    for a in range(n):
        for k in range(1, N_DEV):
            px, py, pc = _peer(k)
            src = src_refs[a].at[4 * px + 2 * py + pc] if mode == 'scatter' else src_refs[a]
            out.append(pltpu.make_async_remote_copy(
                src_ref=src, dst_ref=land_refs[a].at[me], send_sem=send_sems.at[a * N_PEER_COPIES + k - 1],
                recv_sem=recv_sems.at[a * N_PEER_COPIES + k - 1], device_id=(px, py, pc), device_id_type=MESH))
    return out


def exchange_start(name, mode, srcs):
    n = len(srcs)
    lands = [lax.empty(s.shape if mode == 'scatter' else (N_DEV,) + s.shape, s.dtype) for s in srcs]

    def body(*refs):
        src_refs, land_refs, send_sems, recv_sems = refs[:n], refs[n:2 * n], refs[2 * n], refs[2 * n + 1]
        for cp in _split_copies(mode, n, src_refs, land_refs, send_sems, recv_sems):
            cp.start()
        refs[-1][...] = jnp.zeros(refs[-1].shape, F32)

    sems = pltpu.SemaphoreType.DMA((n * N_PEER_COPIES,))
    res = pl.pallas_call(
        body, name=name,
        out_shape=(sems, sems, *[pltpu.HBM(a.shape, a.dtype) for a in srcs + lands], jax.ShapeDtypeStruct((8, LANES), F32)),
        in_specs=(_HBM,) * (2 * n), out_specs=(_SEM, _SEM, *(_HBM,) * (2 * n), pl.BlockSpec(memory_space=pltpu.VMEM)),
        input_output_aliases={i: 2 + i for i in range(2 * n)},
        compiler_params=pltpu.CompilerParams(has_side_effects=_EFFECT),
    )(*[pltpu.with_memory_space_constraint(a, pltpu.HBM) for a in srcs + lands])
    return (mode, n, res[:-1]), res[-1]


def exchange_wait(name, handles, after):
    mode, n, (send_sems, recv_sems, *bufs) = handles

    def body(*refs):
        src_refs, land_refs, s_sems, r_sems = refs[:n], refs[n:2 * n], refs[2 * n], refs[2 * n + 1]
        for cp in _split_copies(mode, n, src_refs, land_refs, s_sems, r_sems):
            cp.wait_send()
            cp.wait_recv()

    res = pl.pallas_call(
        body, name=name, out_shape=tuple(pltpu.HBM(a.shape, a.dtype) for a in bufs),
        in_specs=(*(_HBM,) * (2 * n), _SEM, _SEM, pl.BlockSpec(memory_space=pl.ANY)), out_specs=(_HBM,) * (2 * n),
        input_output_aliases={i: i for i in range(2 * n)},
        compiler_params=pltpu.CompilerParams(has_side_effects=_EFFECT),
    )(*bufs, send_sems, recv_sems, after)
    srcs, lands = res[:n], res[n:]
    me = _my_index()
    own = [lax.dynamic_index_in_dim(s, me, 0, keepdims=True) if mode == 'scatter' else s[None] for s in srcs]
    return [lax.dynamic_update_slice(l, o, (me, 0, 0)) for l, o in zip(lands, own)]


def adamw(name, recv, w, m, v, tm):
    rows, cols = w.shape
    c1 = 1.0 / (1.0 - ADAM_B1 ** ADAM_STEP)
    c2 = 1.0 / (1.0 - ADAM_B2 ** ADAM_STEP)

    def fn(*a):
        g = a[0]
        for s in range(1, N_DEV):
            g = g + a[s]
        w_, m_, v_ = a[N_DEV:]
        m_ = ADAM_B1 * m_ + (1.0 - ADAM_B1) * g
        v_ = ADAM_B2 * v_ + (1.0 - ADAM_B2) * jnp.square(g)
        delta = -ADAM_LR * ((m_ * c1) / (jnp.sqrt(v_ * c2) + ADAM_EPS) + ADAM_WD * w_)
        return g, delta, m_, v_

    flat = recv.reshape(N_DEV * rows, cols)
    ins = [Op(flat, cols, 0, s * (rows // tm)) for s in range(N_DEV)] + [w, m, v]
    return rowwise(name, fn, ins, [], [(rows, cols, F32)] * 4, tm=tm)


ADAMW_TM = {'ffn1_gate': 256, 'ffn1_up': 256, 'ffn1_down': 128, 'w_in': 32, 'conv_w': 8, 'w_out': 64,
            'ffn2_gate': 256, 'ffn2_up': 256, 'ffn2_down': 128}


def kernel(x, positions, ln1_g, ln1_b, ffn1_gate, ffn1_up, ffn1_down, w_in, conv_w, conv_b, dt_bias, a_log, d_skip, attn_norm_w, ssd_norm_w, w_out, ln2_g, ln2_b, ffn2_gate, ffn2_up, ffn2_down, ln3_g, ln3_b, loss_target, m_ln1_g, m_ln1_b, m_ffn1_gate, m_ffn1_up, m_ffn1_down, m_w_in, m_conv_w, m_conv_b, m_dt_bias, m_a_log, m_d_skip, m_attn_norm_w, m_ssd_norm_w, m_w_out, m_ln2_g, m_ln2_b, m_ffn2_gate, m_ffn2_up, m_ffn2_down, m_ln3_g, m_ln3_b, v_ln1_g, v_ln1_b, v_ffn1_gate, v_ffn1_up, v_ffn1_down, v_w_in, v_conv_w, v_conv_b, v_dt_bias, v_a_log, v_d_skip, v_attn_norm_w, v_ssd_norm_w, v_w_out, v_ln2_g, v_ln2_b, v_ffn2_gate, v_ffn2_up, v_ffn2_down, v_ln3_g, v_ln3_b):
    args = dict(locals())
    wl = {k: args[k] for k in WEIGHTS}
    ml = {k: args["m_" + k] for k in WEIGHTS}
    vl = {k: args["v_" + k] for k in WEIGHTS}
    shapes = {k: wl[k].shape for k in WEIGHTS}
    b, s, dm = x.shape
    t = b * s

    w_comm = {k: to_comm(k, wl, shapes) for k in SHARDED + (SMALL,)}
    sent = {k: w_comm[k] if k == 'conv_w' else w_comm[k].astype(BF16) for k in SHARDED}
    p = {k: full_weight(k, g) for k, g in zip(FFN1_KEYS, all_gather([sent[k] for k in FFN1_KEYS]))}
    gather_mixer, token_m = exchange_start("gather_mixer_start", 'gather', [sent[k] for k in MIXER_KEYS])
    sent['ffn2_gate'] = sent['ffn2_gate'] + token_m[0, 0].astype(BF16)
    gather_ffn2, token_f = exchange_start("gather_ffn2_start", 'gather', [sent[k] for k in FFN2_KEYS])
    for k in REPLICATED:
        p[k] = wl[k].reshape(1, -1)

    x2 = x.reshape(t, dm)
    cosv, sinv = rope_tables(positions)
    f1, res1 = ffn_fwd("ffn1", x2, p['ffn1_gate'], p['ffn1_up'], p['ffn1_down'], after=(token_m, token_f))
    h1, h1b = resid_ln_fwd("ln1", 0.5, x2, f1, p['ln1_g'], p['ln1_b'])
    for k, g in zip(MIXER_KEYS, exchange_wait("gather_mixer_wait", gather_mixer, h1b)):
        p[k] = full_weight(k, g)
    mix, resm = mixer_fwd(h1b, p, cosv, sinv, b)
    h2, h2b = resid_ln_fwd("ln2", 1.0, h1, mix, p['ln2_g'], p['ln2_b'])
    for k, g in zip(FFN2_KEYS, exchange_wait("gather_ffn2_wait", gather_ffn2, h2b)):
        p[k] = full_weight(k, g)
    f2, res3 = ffn_fwd("ffn2", h2b, p['ffn2_gate'], p['ffn2_up'], p['ffn2_down'])

    small, full = {}, {}
    dh2_res, df2, small['ln3_g'], small['ln3_b'], sq = ln_loss_bwd("ln3_loss_bwd", h2, f2, loss_target.reshape(t, dm),
                                                                   p['ln3_g'], p['ln3_b'])
    loss = lax.psum(jnp.sum(sq) * (0.5 / dm), AXES)

    dh2, full['ffn2_gate'], full['ffn2_up'], full['ffn2_down'] = ffn_bwd("ffn2", res3, p['ffn2_gate'], p['ffn2_up'],
                                                                       p['ffn2_down'], df2, dh2_res)
    ffn2_exchange, token = exchange_start("grads_ffn2_start", 'scatter', [grad_shards(k, full[k]) for k in FFN2_KEYS])
    dh1_res, dmix, small['ln2_g'], small['ln2_b'] = resid_ln_bwd("ln2_bwd", 1.0, h1, mix, p['ln2_g'] + token[:1, :1],
                                                                 p['ln2_b'], dh2)
    dh1, gm = mixer_bwd(resm, p, dmix, dh1_res, b)
    for k in ('conv_b', 'dt_bias', 'a_log', 'd_skip', 'attn_norm_w', 'ssd_norm_w'):
        small[k] = gm[k]
    mixer_exchange, token = exchange_start("grads_mixer_start", 'scatter', [grad_shards(k, gm[k]) for k in MIXER_KEYS])
    dx_res, df1, small['ln1_g'], small['ln1_b'] = resid_ln_bwd("ln1_bwd", 0.5, x2, f1, p['ln1_g'] + token[:1, :1],
                                                               p['ln1_b'], dh1)
    hb, g, u, a = res1
    small_part = to_comm(SMALL, small, shapes)
    dg, du = ffn_da_act("ffn1_bwd_da_act", df1, p['ffn1_down'], g, u)
    dwd = mm_tn("ffn1_bwd_dwd", a, df1, BF16, after=dg)
    down_exchange, token = exchange_start("grads_ffn1_down_start", 'scatter', [
        grad_shards('ffn1_down', dwd), jnp.broadcast_to(small_part[None], (N_DEV,) + small_part.shape)])
    dx = mm("ffn1_bwd_dh", [(dg, p['ffn1_gate'], 'nt'), (du, p['ffn1_up'], 'nt')], D_MODEL, add=dx_res, tn=512, after=token)
    dwg = mm_tn("ffn1_bwd_dwg", hb, dg, BF16, after=dx)
    gate_exchange, token = exchange_start("grads_ffn1_gate_start", 'scatter', [grad_shards('ffn1_gate', dwg)])
    dwu = mm_tn("ffn1_bwd_dwu", hb, du, BF16, after=token)
    recv = dict(zip(('ffn1_up',), all_to_all([grad_shards('ffn1_up', dwu)])))
    for keys, name, ex in (((FFN2_KEYS), "grads_ffn2_wait", ffn2_exchange), (MIXER_KEYS, "grads_mixer_wait", mixer_exchange),
                           (('ffn1_down', SMALL), "grads_ffn1_down_wait", down_exchange),
                           (('ffn1_gate',), "grads_ffn1_gate_wait", gate_exchange)):
        recv.update(zip(keys, exchange_wait(name, ex, recv['ffn1_up'])))
    outs = [{}, {}, {}, {}]
    for k, r in recv.items():
        tm = ADAMW_TM.get(k, r.shape[1])
        res = adamw(f"adamw_{k}", r, w_comm[k], to_comm(k, ml, shapes), to_comm(k, vl, shapes), tm)
        for o, a in zip(outs, res):
            o.update(from_comm(k, a, shapes))
    return (loss, dx.reshape(b, s, dm), *[o[k] for o in outs for k in WEIGHTS])
```

```python
import functools
import math

import jax
import jax.numpy as jnp
import numpy as np
from jax import lax
from jax.experimental import pallas as pl
from jax.experimental.pallas import tpu as pltpu

F32, BF16 = jnp.float32, jnp.bfloat16
HI = lax.Precision.HIGHEST
MESH = pl.DeviceIdType.MESH
AXES = ("x", "y", "c")
N_DEV = 8

D_MODEL = 1024
SEQ = 2048
HEAD_DIM = 64
N_HEADS = 12
D_ATTN = N_HEADS * HEAD_DIM
DILATIONS = (1, 4, 16)
ATTN_BLOCK = 128
ROPE_THETA = 500000.0
ROPE_DIM = 16
D_SSD = 768
SSD_GROUPS = 4
SSD_STATE = 128
SSD_CHUNK = 128
D_BC = SSD_GROUPS * SSD_STATE
D_CONV = D_SSD + 2 * D_BC
CONV_WIDTH = 4
D_QKVZ = 3 * D_ATTN + D_SSD
D_IN_PROJ = D_QKVZ + D_CONV + N_HEADS
D_FF = 2816
ALPHA = 2.0 ** 0.25
LN_EPS = 1e-5
RMS_EPS = 1e-6
ADAM_LR, ADAM_B1, ADAM_B2, ADAM_EPS, ADAM_WD, ADAM_STEP = 0.001, 0.9, 0.999, 1e-08, 0.01, 10

LANES = 128
VMEM_LIMIT = 52 * 1024 * 1024
NEG = -1e30

WEIGHTS = ['ln1_g', 'ln1_b', 'ffn1_gate', 'ffn1_up', 'ffn1_down', 'w_in', 'conv_w', 'conv_b', 'dt_bias', 'a_log',
           'd_skip', 'attn_norm_w', 'ssd_norm_w', 'w_out', 'ln2_g', 'ln2_b', 'ffn2_gate', 'ffn2_up', 'ffn2_down',
           'ln3_g', 'ln3_b']
COL_SHARDED = ('ffn1_gate', 'ffn1_up', 'conv_w', 'ffn2_gate', 'ffn2_up')
ROW_SHARDED = ('ffn1_down', 'w_in', 'w_out', 'ffn2_down')
SHARDED = tuple(n for n in WEIGHTS if n in COL_SHARDED or n in ROW_SHARDED)
REPLICATED = tuple(n for n in WEIGHTS if n not in SHARDED)
FF_SHARD = D_FF // N_DEV
FF_PAD = -(-FF_SHARD // LANES) * LANES
D_FF_INT = N_DEV * FF_PAD


def _cparams(sem=None):
    return pltpu.CompilerParams(dimension_semantics=sem, vmem_limit_bytes=VMEM_LIMIT)


def _tile(n, prefs):
    for p in prefs:
        if n % p == 0:
            return p
    return n


class Op:
    def __init__(self, arr, bw=None, cb=0, ro=0):
        self.arr, self.bw, self.cb, self.ro = arr, (arr.shape[1] if bw is None else bw), cb, ro


def _op(a):
    return a if isinstance(a, Op) else Op(a)


def rowwise(name, fn, ins, consts, outs, accs=(), tm=256):
    ins = [_op(a) for a in ins]
    rows = outs[0][0]
    n_in, n_c, n_o, n_a = len(ins), len(consts), len(outs), len(accs)
    tm = min(tm, rows)
    assert rows % tm == 0, (name, rows, tm)

    def body(*refs):
        vals = [r[...].astype(F32) for r in refs[:n_in + n_c]]
        res = fn(*vals)
        res = res if isinstance(res, (tuple, list)) else (res,)
        o_refs = refs[n_in + n_c:n_in + n_c + n_o]
        a_refs = refs[n_in + n_c + n_o:]
        for r, v in zip(o_refs, res[:n_o]):
            r[...] = v.astype(r.dtype)
        if n_a:
            @pl.when(pl.program_id(0) == 0)
            def _():
                for r in a_refs:
                    r[...] = jnp.zeros(r.shape, r.dtype)
            for r, v in zip(a_refs, res[n_o:]):
                r[...] += v

    in_specs = [pl.BlockSpec((tm, o.bw), functools.partial(lambda i, o: (i + o.ro, o.cb), o=o)) for o in ins]
    in_specs += [pl.BlockSpec(c.shape, functools.partial(lambda i, nd: (0,) * nd, nd=c.ndim)) for c in consts]
    out_specs = [pl.BlockSpec((tm, w), lambda i: (i, 0)) for (_, w, _) in outs]
    out_specs += [pl.BlockSpec(s, functools.partial(lambda i, nd: (0,) * nd, nd=len(s))) for s in accs]
    out_shape = [jax.ShapeDtypeStruct((r, w), dt) for (r, w, dt) in outs]
    out_shape += [jax.ShapeDtypeStruct(s, F32) for s in accs]
    res = pl.pallas_call(
        body, name=name, grid=(rows // tm,), in_specs=in_specs, out_specs=out_specs, out_shape=out_shape,
        compiler_params=_cparams(("arbitrary",) if n_a else ("parallel",)),
    )(*[o.arr for o in ins], *consts)
    return res


MM_TM = 512
MM_TN = (1024, 896, 768, 512, 256, 128)
_NT = (((1,), (1,)), ((), ()))
_NN = (((1,), (0,)), ((), ()))
_TN = (((0,), (0,)), ((), ()))


def _dot(a, b, dn, precision=None):
    return lax.dot_general(a, b, dn, preferred_element_type=F32, precision=precision)


def _mm_specs(name, pairs, n_out, tm, tn):
    in_specs, args = [], []
    for a, b, mode in pairs:
        o = _op(a)
        in_specs.append(pl.BlockSpec((tm, o.bw), functools.partial(lambda j, i, o: (i, o.cb), o=o)))
        args.append(o.arr)
        if mode == 'nn':
            assert b.shape == (o.bw, n_out), (name, b.shape, o.bw, n_out)
            in_specs.append(pl.BlockSpec((o.bw, tn), lambda j, i: (0, j)))
        else:
            assert b.shape == (n_out, o.bw), (name, b.shape, o.bw, n_out)
            in_specs.append(pl.BlockSpec((tn, o.bw), lambda j, i: (j, 0)))
        args.append(b)
    return in_specs, args


def _mm_acc(refs, pairs):
    acc = None
    for k, (_, _, mode) in enumerate(pairs):
        d = _dot(refs[2 * k][...].astype(BF16), refs[2 * k + 1][...].astype(BF16), _NN if mode == 'nn' else _NT)
        acc = d if acc is None else acc + d
    return acc


def mm(name, pairs, n_out, add=None, out_dtype=F32, tm=MM_TM, tn=None, after=None):
    m = _op(pairs[0][0]).arr.shape[0]
    tn = tn or _tile(n_out, MM_TN)
    n_p = len(pairs)

    def body(*refs):
        acc = _mm_acc(refs, pairs)
        if add is not None:
            acc = acc + refs[2 * n_p][...]
        refs[-1][...] = acc.astype(refs[-1].dtype)

    in_specs, args = _mm_specs(name, pairs, n_out, tm, tn)
    tile = pl.BlockSpec((tm, tn), lambda j, i: (i, j))
    if add is not None:
        in_specs.append(tile)
        args.append(add)
    if after is not None:
        in_specs.append(pl.BlockSpec(memory_space=pl.ANY))
        args.append(after)
    return pl.pallas_call(
        body, name=name, grid=(n_out // tn, m // tm), in_specs=in_specs, out_specs=tile,
        out_shape=jax.ShapeDtypeStruct((m, n_out), out_dtype),
        compiler_params=_cparams(("parallel", "parallel")),
    )(*args)


def mm_tn(name, a, b, out_dtype=F32, tt=1024, after=None):
    a, b = _op(a), _op(b)
    t = a.arr.shape[0]
    k, n = a.bw, b.bw
    tk = _tile(k, (512, 896, 768, 256, 128))
    tn = _tile(n, (3072, 1792) + MM_TN)
    tt = min(tt, t)
    n_t = t // tt
    order = [] if after is None else [after]

    def body(a_ref, b_ref, *rest):
        o_ref, acc_ref = rest[-2:]
        s = pl.program_id(2)
        d = _dot(a_ref[...].astype(BF16), b_ref[...].astype(BF16), _TN)

        @pl.when(s == 0)
        def _():
            acc_ref[...] = d

        @pl.when(s > 0)
        def _():
            acc_ref[...] += d

        @pl.when(s == n_t - 1)
        def _():
            o_ref[...] = acc_ref[...].astype(o_ref.dtype)

    return pl.pallas_call(
        body, name=name, grid=(k // tk, n // tn, n_t),
        in_specs=[pl.BlockSpec((tt, tk), functools.partial(lambda kk, nn, s, o: (s, o.cb * (o.bw // tk) + kk), o=a)),
                  pl.BlockSpec((tt, tn), functools.partial(lambda kk, nn, s, o: (s, o.cb * (o.bw // tn) + nn), o=b))]
        + [pl.BlockSpec(memory_space=pl.ANY) for _ in order],
        out_specs=pl.BlockSpec((tk, tn), lambda kk, nn, s: (kk, nn)),
        out_shape=jax.ShapeDtypeStruct((k, n), out_dtype),
        scratch_shapes=[pltpu.VMEM((tk, tn), F32)],
        compiler_params=_cparams(("parallel", "parallel", "arbitrary")),
    )(a.arr, b.arr, *order)


def _sigmoid(x):
    return 1.0 / (1.0 + jnp.exp(-x))


def _silu(x):
    return x * _sigmoid(x)


def _softplus(x):
    return jnp.maximum(x, 0.0) + jnp.log(1.0 + jnp.exp(-jnp.abs(x)))


def _act(g, u):
    return _silu(g) * u


def _resid_ln(scale, h, branch, g, b):
    r = ALPHA * h + scale * branch
    mu = jnp.mean(r, axis=-1, keepdims=True)
    var = jnp.mean(jnp.square(r - mu), axis=-1, keepdims=True)
    return (r - mu) * lax.rsqrt(var + LN_EPS) * g + b


def _rms(t, w):
    return t * lax.rsqrt(jnp.mean(t * t, axis=-1, keepdims=True) + RMS_EPS) * w


def _branch_weights(l1, l2, l3):
    m = jnp.maximum(jnp.maximum(l1, l2), l3)
    e1, e2, e3 = jnp.exp(l1 - m), jnp.exp(l2 - m), jnp.exp(l3 - m)
    inv = 1.0 / (e1 + e2 + e3)
    return e1 * inv, e2 * inv, e3 * inv


def _gate(y, xs, z, dskip, w):
    return _rms((y + dskip * xs) * _silu(z), w)


def _rot(x):
    d = lax.broadcasted_iota(jnp.int32, x.shape, 1) % HEAD_DIM
    up = pltpu.roll(x, x.shape[1] - ROPE_DIM // 2, 1)
    down = jnp.where(d < ROPE_DIM, pltpu.roll(x, ROPE_DIM // 2, 1), 0.0)
    return jnp.where(d < ROPE_DIM // 2, up, down)


def ffn_gate_up(name, h, wg, wu, after=()):
    m, nf = h.shape[0], wg.shape[1]
    tn = _tile(nf, MM_TN)

    def body(h_ref, g_w, u_w, *rest):
        g_ref, u_ref, a_ref = rest[-3:]
        hb = h_ref[...].astype(BF16)
        g = _dot(hb, g_w[...].astype(BF16), _NN)
        u = _dot(hb, u_w[...].astype(BF16), _NN)
        g_ref[...] = g.astype(g_ref.dtype)
        u_ref[...] = u.astype(u_ref.dtype)
        a_ref[...] = _act(g, u).astype(a_ref.dtype)

    in_specs, args = _mm_specs(name, [(h, wg, 'nn')], nf, MM_TM, tn)
    in_specs.append(in_specs[1])
    in_specs += [pl.BlockSpec(memory_space=pl.ANY) for _ in after]
    tile = pl.BlockSpec((MM_TM, tn), lambda j, i: (i, j))
    return pl.pallas_call(
        body, name=name, grid=(nf // tn, m // MM_TM), in_specs=in_specs, out_specs=[tile] * 3,
        out_shape=[jax.ShapeDtypeStruct((m, nf), BF16)] * 3, compiler_params=_cparams(("parallel", "parallel")),
    )(*args, wu, *after)


def ffn_da_act(name, df, wd, g, u):
    m, nf = df.shape[0], wd.shape[0]
    tn = _tile(nf, MM_TN)

    def body(df_ref, w_ref, g_ref, u_ref, dg_ref, du_ref):
        da = _dot(df_ref[...].astype(BF16), w_ref[...].astype(BF16), _NT)
        _, vjp = jax.vjp(_act, g_ref[...].astype(F32), u_ref[...].astype(F32))
        dg, du = vjp(da)
        dg_ref[...] = dg.astype(dg_ref.dtype)
        du_ref[...] = du.astype(du_ref.dtype)

    in_specs, args = _mm_specs(name, [(df, wd, 'nt')], nf, MM_TM, tn)
    tile = pl.BlockSpec((MM_TM, tn), lambda j, i: (i, j))
    return pl.pallas_call(
        body, name=name, grid=(nf // tn, m // MM_TM), in_specs=in_specs + [tile, tile], out_specs=[tile] * 2,
        out_shape=[jax.ShapeDtypeStruct((m, nf), BF16)] * 2, compiler_params=_cparams(("parallel", "parallel")),
    )(*args, g, u)


def resid_ln_fwd(name, scale, h, branch, ln_g, ln_b):
    t = h.shape[0]

    def fn(*a):
        y = _resid_ln(scale, *a)
        return y, y

    return rowwise(name, fn, [h, branch], [ln_g, ln_b], [(t, D_MODEL, F32), (t, D_MODEL, BF16)], tm=512)


def ffn_fwd(tag, hb, wg, wu, wd, after=()):
    g, u, a = ffn_gate_up(f"{tag}_gate_up", hb, wg, wu, after)
    f = mm(f"{tag}_down", [(a, wd, 'nn')], D_MODEL)
    return f, (hb, g, u, a)


def ln_loss_bwd(name, h, branch, target, ln_g, ln_b):
    t, dm = h.shape

    def fn(h_, br_, tgt, g_, b_):
        y, vjp = jax.vjp(functools.partial(_resid_ln, 0.5), h_, br_, g_, b_)
        e = y - tgt
        return (*vjp(e * (1.0 / dm)), jnp.sum(e * e, axis=0, keepdims=True))

    return rowwise(name, fn, [h, branch, target], [ln_g, ln_b], [(t, dm, F32), (t, dm, F32)],
                   accs=[(1, dm), (1, dm), (1, dm)], tm=512)


def resid_ln_bwd(name, scale, h, branch, ln_g, ln_b, dout, extra=None):
    t = h.shape[0]

    def fn(h_, br_, do_, *rest):
        g_, b_ = rest[-2], rest[-1]
        _, vjp = jax.vjp(functools.partial(_resid_ln, scale), h_, br_, g_, b_)
        dh, dbr, dg, db = vjp(do_)
        if extra is not None:
            dh = dh + rest[0]
        return dh, dbr, dg, db

    ins = [h, branch, dout] + ([extra] if extra is not None else [])
    return rowwise(name, fn, ins, [ln_g, ln_b], [(t, D_MODEL, F32), (t, D_MODEL, F32)],
                   accs=[(1, D_MODEL), (1, D_MODEL)], tm=512)


def ffn_bwd(tag, res, wg, wu, wd, df, dh_resid):
    hb, g, u, a = res
    dg, du = ffn_da_act(f"{tag}_bwd_da_act", df, wd, g, u)
    dwd = mm_tn(f"{tag}_bwd_dwd", a, df, BF16)
    dh = mm(f"{tag}_bwd_dh", [(dg, wg, 'nt'), (du, wu, 'nt')], D_MODEL, add=dh_resid, tn=512)
    dwg = mm_tn(f"{tag}_bwd_dwg", hb, dg, BF16)
    dwu = mm_tn(f"{tag}_bwd_dwu", hb, du, BF16)
    return dh, dwg, dwu, dwd


def rope_tables(positions):
    inv_freq = ROPE_THETA ** (-jnp.arange(0, ROPE_DIM, 2, dtype=F32) / ROPE_DIM)
    ang = positions.reshape(-1, 1).astype(F32) * inv_freq
    c, s = jnp.cos(ang), jnp.sin(ang)
    t = ang.shape[0]
    cosv = jnp.concatenate([c, c, jnp.ones((t, HEAD_DIM - ROPE_DIM), F32)], axis=1)
    sinv = jnp.concatenate([-s, s, jnp.zeros((t, HEAD_DIM - ROPE_DIM), F32)], axis=1)
    return jnp.tile(cosv, (1, 2)), jnp.tile(sinv, (1, 2))


def _pair_masks():
    lane = lax.broadcasted_iota(jnp.int32, (1, LANES), 1)
    return (lane < HEAD_DIM, lane >= HEAD_DIM)


def _band_masks():
    row = lax.broadcasted_iota(jnp.int32, (ATTN_BLOCK, ATTN_BLOCK), 0)
    col = lax.broadcasted_iota(jnp.int32, (ATTN_BLOCK, ATTN_BLOCK), 1)
    return col >= row, col <= row


def _residue_blocks():
    out = []
    for g, d in enumerate(DILATIONS):
        for r in range(d):
            for i in range(SEQ // d // ATTN_BLOCK):
                rows = lambda j: pl.ds(r + j * ATTN_BLOCK * d, ATTN_BLOCK, stride=d) if d > 1 else pl.ds(j * ATTN_BLOCK, ATTN_BLOCK)
                out.append((g, rows(i), rows(i - 1) if i > 0 else None))
    return out


N_HEAD_PAIRS = D_ATTN // LANES
SCALE = HEAD_DIM ** -0.5
ATTN_GROUP = 4


def _block_operands(qr, kr, v_ref, cur, prev):
    prev_ok, cur_ok = _band_masks()
    if prev is None:
        return qr[cur, :], kr[cur, :].astype(BF16), v_ref[cur, :], cur_ok
    kcat = jnp.concatenate([kr[prev, :], kr[cur, :]], axis=0).astype(BF16)
    vcat = jnp.concatenate([v_ref[prev, :], v_ref[cur, :]], axis=0)
    return qr[cur, :], kcat, vcat, jnp.concatenate([prev_ok, cur_ok], axis=1)


def _attn_specs(b):
    col = lambda cb: pl.BlockSpec((SEQ, LANES), lambda bb, hp: (bb, cb + hp))
    tab = pl.BlockSpec((SEQ, LANES), lambda bb, hp: (bb, 0))
    return col, tab


def attn_fwd(qkvz, cosv, sinv, b):
    t = qkvz.shape[0]
    col, tab = _attn_specs(b)
    blocks = _residue_blocks()

    def body(q_ref, k_ref, v_ref, c_ref, s_ref, o_ref, l1_ref, l2_ref, l3_ref, qr, kr, o1, o2, o3):
        l_refs, o_scr = (l1_ref, l2_ref, l3_ref), (o1, o2, o3)
        c, s = c_ref[...], s_ref[...]
        q, k = q_ref[...], k_ref[...]
        qr[...] = q * c + _rot(q) * s
        kr[...] = k * c + _rot(k) * s
        masks = _pair_masks()
        for lo in range(0, len(blocks), ATTN_GROUP):
            chains = []
            for g, cur, prev in blocks[lo:lo + ATTN_GROUP]:
                q2, kcat, vcat, ok = _block_operands(qr, kr, v_ref, cur, prev)
                for m in masks:
                    qm = jnp.where(m, q2, 0.0).astype(BF16)
                    chains.append(dict(g=g, cur=cur, m=m, v=jnp.where(m, vcat, 0.0).astype(BF16),
                                       s=jnp.where(ok, _dot(qm, kcat, _NT) * SCALE, NEG)))
            for ch in chains:
                mx = jnp.max(ch['s'], axis=1, keepdims=True)
                p = jnp.exp(ch['s'] - mx)
                den = jnp.sum(p, axis=1, keepdims=True)
                ch.update(p=p.astype(BF16), inv=1.0 / den, lse=mx + jnp.log(den))
            for ch in chains:
                ch['o'] = _dot(ch['p'], ch['v'], _NN) * ch['inv']
            for c0, c1 in zip(chains[0::2], chains[1::2]):
                o_scr[c0['g']][c0['cur'], :] = c0['o'] + c1['o']
                l_refs[c0['g']][c0['cur'], :] = jnp.where(c0['m'], c0['lse'], c1['lse'])
        w1, w2, w3 = _branch_weights(l1_ref[...], l2_ref[...], l3_ref[...])
        o_ref[...] = w1 * o1[...] + w2 * o2[...] + w3 * o3[...]

    shp = jax.ShapeDtypeStruct((t, D_ATTN), F32)
    return pl.pallas_call(
        body, name="attn_fwd", grid=(b, N_HEAD_PAIRS),
        in_specs=[col(0), col(N_HEAD_PAIRS), col(2 * N_HEAD_PAIRS), tab, tab],
        out_specs=[col(0)] * 4, out_shape=[shp] * 4,
        scratch_shapes=[pltpu.VMEM((SEQ, LANES), F32)] * 5,
        compiler_params=_cparams(("parallel", "parallel")),
    )(qkvz, qkvz, qkvz, cosv, sinv)


def attn_bwd(qkvz, cosv, sinv, dmix, mixed, lses, b):
    t = qkvz.shape[0]
    col, tab = _attn_specs(b)
    blocks = _residue_blocks()
    hd = np.arange(LANES) // HEAD_DIM
    head_ones = jnp.asarray((hd[:, None] == hd[None, :]).astype(np.float32))

    def body(q_ref, k_ref, v_ref, c_ref, s_ref, dm_ref, mx_ref, l1_ref, l2_ref, l3_ref, ones_ref,
             dq_out, dk_out, dv_out, qr, kr, do1, do2, do3, dd1, dd2, dd3, dq_ref, dk_ref, dv_ref):
        l_refs, do_scr, dd_scr = (l1_ref, l2_ref, l3_ref), (do1, do2, do3), (dd1, dd2, dd3)
        c, s = c_ref[...], s_ref[...]
        q, k = q_ref[...], k_ref[...]
        qr[...] = q * c + _rot(q) * s
        kr[...] = k * c + _rot(k) * s
        dm = dm_ref[...]
        tot = _dot(dm * mx_ref[...], ones_ref[...], _NN, HI)
        for w, do_g, dd_g in zip(_branch_weights(l1_ref[...], l2_ref[...], l3_ref[...]), do_scr, dd_scr):
            do_g[...] = w * dm
            dd_g[...] = w * tot
        dq_ref[...] = jnp.zeros((SEQ, LANES), F32)
        dk_ref[...] = jnp.zeros((SEQ, LANES), F32)
        dv_ref[...] = jnp.zeros((SEQ, LANES), F32)
        masks = _pair_masks()
        for lo in range(0, len(blocks), ATTN_GROUP):
            chains = []
            for g, cur, prev in blocks[lo:lo + ATTN_GROUP]:
                q2, kcat, vcat, ok = _block_operands(qr, kr, v_ref, cur, prev)
                vcat = vcat.astype(BF16)
                do2_, l2, dd2_ = do_scr[g][cur, :], l_refs[g][cur, :], dd_scr[g][cur, :]
                l2s, dd2s = pltpu.roll(l2, HEAD_DIM, 1), pltpu.roll(dd2_, HEAD_DIM, 1)
                for m in masks:
                    qm = jnp.where(m, q2, 0.0).astype(BF16)
                    dom = jnp.where(m, do2_, 0.0).astype(BF16)
                    lrep, ddrep = jnp.where(m, l2, l2s), jnp.where(m, dd2_, dd2s)
                    if prev is not None:
                        lrep, ddrep = jnp.concatenate([lrep, lrep], axis=1), jnp.concatenate([ddrep, ddrep], axis=1)
                    chains.append(dict(cur=cur, prev=prev, qm=qm, dom=dom, km=jnp.where(m, kcat, 0), lrep=lrep, ddrep=ddrep,
                                       s=jnp.where(ok, _dot(qm, kcat, _NT) * SCALE, NEG), dp=_dot(dom, vcat, _NT)))
            for ch in chains:
                p = jnp.exp(ch['s'] - ch['lrep'])
                ch.update(p=p.astype(BF16), ds=(p * (ch['dp'] - ch['ddrep']) * SCALE).astype(BF16))
            for ch in chains:
                ch.update(dq=_dot(ch['ds'], ch['km'], _NN), dk=_dot(ch['ds'], ch['qm'], _TN), dv=_dot(ch['p'], ch['dom'], _TN))
            for c0, c1 in zip(chains[0::2], chains[1::2]):
                cur, prev = c0['cur'], c0['prev']
                dk, dv = c0['dk'] + c1['dk'], c0['dv'] + c1['dv']
                dq_ref[cur, :] += c0['dq'] + c1['dq']
                if prev is None:
                    dk_ref[cur, :] += dk
                    dv_ref[cur, :] += dv
                else:
                    dk_ref[prev, :] += dk[:ATTN_BLOCK]
                    dv_ref[prev, :] += dv[:ATTN_BLOCK]
                    dk_ref[cur, :] += dk[ATTN_BLOCK:]
                    dv_ref[cur, :] += dv[ATTN_BLOCK:]
        dq, dk = dq_ref[...], dk_ref[...]
        dq_out[...] = (dq * c + _rot(dq * s)).astype(dq_out.dtype)
        dk_out[...] = (dk * c + _rot(dk * s)).astype(dk_out.dtype)
        dv_out[...] = dv_ref[...].astype(dv_out.dtype)

    shp = jax.ShapeDtypeStruct((t, D_ATTN), BF16)
    return pl.pallas_call(
        body, name="attn_bwd", grid=(b, N_HEAD_PAIRS),
        in_specs=[col(0), col(N_HEAD_PAIRS), col(2 * N_HEAD_PAIRS), tab, tab, col(0), col(0), col(0), col(0), col(0),
                  pl.BlockSpec((LANES, LANES), lambda bb, hp: (0, 0))],
        out_specs=[col(0)] * 3, out_shape=[shp] * 3,
        scratch_shapes=[pltpu.VMEM((SEQ, LANES), F32)] * 11,
        compiler_params=_cparams(("parallel", "parallel")),
    )(qkvz, qkvz, qkvz, cosv, sinv, dmix, mixed, *lses, head_ones)


def attn_norm_fwd(mixed, norm_w):
    return rowwise("attn_norm", _rms, [mixed], [norm_w], [(mixed.shape[0], D_ATTN, BF16)])[0]


def attn_norm_bwd(dout, mixed, norm_w):
    def fn(dy, mx, w):
        _, vjp = jax.vjp(_rms, mx, w)
        return vjp(dy)

    return rowwise("attn_norm_bwd", fn, [dout, mixed], [norm_w], [(dout.shape[0], D_ATTN, F32)], accs=[(1, D_ATTN)])


CONV_TM = 256
HALO = 8


def _conv_columns(refs):
    xs_ref, bm_ref, cm_ref = refs
    out = []
    for c in range(D_CONV // LANES):
        lo = c * LANES
        ref, base = (xs_ref, 0) if lo < D_SSD else (bm_ref, D_SSD) if lo < D_SSD + D_BC else (cm_ref, D_SSD + D_BC)
        out.append((slice(lo, lo + LANES), (ref, slice(lo - base, lo - base + LANES))))
    return out


def _conv_taps(scr, w_ref, cs, first_row, step, tm):
    acc = None
    for k in range(CONV_WIDTH):
        term = w_ref[k:k + 1, cs] * scr[pl.ds(first_row + step * k, tm), cs]
        acc = term if acc is None else acc + term
    return acc


def conv_fwd(u, w, bias):
    t = u.shape[0]
    tm, per_seq = CONV_TM, SEQ // CONV_TM

    def body(u_ref, h_ref, w_ref, b_ref, xs_ref, bm_ref, cm_ref, scr):
        first = pl.program_id(0) % per_seq == 0
        scr[0:HALO, :] = jnp.where(first, 0.0, h_ref[...])
        scr[HALO:, :] = u_ref[...]
        for cs, (o_ref, os_) in _conv_columns((xs_ref, bm_ref, cm_ref)):
            o_ref[:, os_] = _silu(_conv_taps(scr, w_ref, cs, HALO - CONV_WIDTH + 1, 1, tm) + b_ref[:, cs])

    return pl.pallas_call(
        body, name="conv_fwd", grid=(t // tm,),
        in_specs=[pl.BlockSpec((tm, D_CONV), lambda i: (i, 0)),
                  pl.BlockSpec((HALO, D_CONV), lambda i: (jnp.maximum(i * (tm // HALO) - 1, 0), 0)),
                  pl.BlockSpec((CONV_WIDTH, D_CONV), lambda i: (0, 0)), pl.BlockSpec((1, D_CONV), lambda i: (0, 0))],
        out_specs=[pl.BlockSpec((tm, D_SSD), lambda i: (i, 0)), pl.BlockSpec((tm, D_BC), lambda i: (i, 0)),
                   pl.BlockSpec((tm, D_BC), lambda i: (i, 0))],
        out_shape=[jax.ShapeDtypeStruct((t, D_SSD), F32), jax.ShapeDtypeStruct((t, D_BC), F32),
                   jax.ShapeDtypeStruct((t, D_BC), F32)],
        scratch_shapes=[pltpu.VMEM((tm + HALO, D_CONV), F32)],
        compiler_params=_cparams(("parallel",)),
    )(u, u, w, bias)


def conv_bwd(u, w, bias, dxs_a, dxs_b, dbm, dcm):
    t = u.shape[0]
    tm, per_seq = CONV_TM, SEQ // CONV_TM
    n_tiles = t // tm

    def body1(u_ref, h_ref, dxs_ref, dxs2_ref, dbm_ref, dcm_ref, w_ref, b_ref, dz_ref, dw_ref, db_ref, scr):
        i = pl.program_id(0)
        first = i % per_seq == 0
        scr[0:HALO, :] = jnp.where(first, 0.0, h_ref[...])
        scr[HALO:, :] = u_ref[...]

        @pl.when(i == 0)
        def _():
            dw_ref[...] = jnp.zeros(dw_ref.shape, F32)
            db_ref[...] = jnp.zeros(db_ref.shape, F32)
        for cs, (g_ref, gs) in _conv_columns((dxs_ref, dbm_ref, dcm_ref)):
            acc = _conv_taps(scr, w_ref, cs, HALO - CONV_WIDTH + 1, 1, tm) + b_ref[:, cs]
            sig = _sigmoid(acc)
            dy = g_ref[:, gs] + dxs2_ref[:, gs] if g_ref is dxs_ref else g_ref[:, gs]
            dz = dy * sig * (1.0 + acc * (1.0 - sig))
            dz_ref[:, cs] = dz
            db_ref[:, cs] += jnp.sum(dz, axis=0, keepdims=True)
            for k in range(CONV_WIDTH):
                dw_ref[k:k + 1, cs] += jnp.sum(dz * scr[pl.ds(HALO - CONV_WIDTH + 1 + k, tm), cs], axis=0, keepdims=True)

    dz, dw, db = pl.pallas_call(
        body1, name="conv_bwd_dz", grid=(n_tiles,),
        in_specs=[pl.BlockSpec((tm, D_CONV), lambda i: (i, 0)),
                  pl.BlockSpec((HALO, D_CONV), lambda i: (jnp.maximum(i * (tm // HALO) - 1, 0), 0)),
                  pl.BlockSpec((tm, D_SSD), lambda i: (i, 0)), pl.BlockSpec((tm, D_SSD), lambda i: (i, 0)),
                  pl.BlockSpec((tm, D_BC), lambda i: (i, 0)), pl.BlockSpec((tm, D_BC), lambda i: (i, 0)),
                  pl.BlockSpec((CONV_WIDTH, D_CONV), lambda i: (0, 0)), pl.BlockSpec((1, D_CONV), lambda i: (0, 0))],
        out_specs=[pl.BlockSpec((tm, D_CONV), lambda i: (i, 0)), pl.BlockSpec((CONV_WIDTH, D_CONV), lambda i: (0, 0)),
                   pl.BlockSpec((1, D_CONV), lambda i: (0, 0))],
        out_shape=[jax.ShapeDtypeStruct((t, D_CONV), F32), jax.ShapeDtypeStruct((CONV_WIDTH, D_CONV), F32),
                   jax.ShapeDtypeStruct((1, D_CONV), F32)],
        scratch_shapes=[pltpu.VMEM((tm + HALO, D_CONV), F32)],
        compiler_params=_cparams(("arbitrary",)),
    )(u, u, dxs_a, dxs_b, dbm, dcm, w, bias)

    def body2(dz_ref, n_ref, w_ref, du_ref, scr):
        last = pl.program_id(0) % per_seq == per_seq - 1
        scr[0:tm, :] = dz_ref[...]
        scr[tm:, :] = jnp.where(last, 0.0, n_ref[...])
        for c in range(D_CONV // LANES):
            cs = slice(c * LANES, (c + 1) * LANES)
            du_ref[:, cs] = _conv_taps(scr, w_ref, cs, CONV_WIDTH - 1, -1, tm).astype(du_ref.dtype)

    du = pl.pallas_call(
        body2, name="conv_bwd_du", grid=(n_tiles,),
        in_specs=[pl.BlockSpec((tm, D_CONV), lambda i: (i, 0)),
                  pl.BlockSpec((HALO, D_CONV), lambda i: (jnp.minimum((i + 1) * (tm // HALO), t // HALO - 1), 0)),
                  pl.BlockSpec((CONV_WIDTH, D_CONV), lambda i: (0, 0))],
        out_specs=pl.BlockSpec((tm, D_CONV), lambda i: (i, 0)),
        out_shape=jax.ShapeDtypeStruct((t, D_CONV), BF16),
        scratch_shapes=[pltpu.VMEM((tm + HALO, D_CONV), F32)],
        compiler_params=_cparams(("parallel",)),
    )(dz, dz, w)
    return du, dw, db


Q = SSD_CHUNK
N_PAIRS = D_SSD // LANES
HEADS_PER_GROUP = N_HEADS // SSD_GROUPS


def _rep(a, j):
    return jnp.broadcast_to(a[:, j:j + 1], a.shape)


def _dot_exact01(a, b, dn, a_is_01):
    x = b if a_is_01 else a
    hi = x.astype(BF16)
    mid = (x - hi.astype(F32)).astype(BF16)
    lo = (x - hi.astype(F32) - mid.astype(F32)).astype(BF16)
    z = a.astype(BF16) if a_is_01 else b.astype(BF16)
    out = None
    for term in (hi, mid, lo):
        d = _dot(z, term, dn) if a_is_01 else _dot(term, z, dn)
        out = d if out is None else out + d
    return out


def _pad_lanes(v, fill=0.0):
    row = jnp.pad(v.reshape(1, -1).astype(F32), ((0, 0), (0, LANES - v.size)), constant_values=fill)
    return row, row.reshape(LANES, 1)


def _ssd_common(dtr_ref, dtrt_ref, bias_r, bias_c, alog_r, alog_c):
    row = lax.broadcasted_iota(jnp.int32, (Q, Q), 0)
    col = lax.broadcasted_iota(jnp.int32, (Q, Q), 1)
    tril = row >= col
    lane = lax.broadcasted_iota(jnp.int32, (1, LANES), 1)
    a_r = jnp.where(lane < N_HEADS, -jnp.exp(alog_r[...]), 0.0)
    sub = lax.broadcasted_iota(jnp.int32, (LANES, 1), 0)
    a_c = jnp.where(sub < N_HEADS, -jnp.exp(alog_c[...]), 0.0)
    dt = _softplus(dtr_ref[...] + bias_r[...])
    cs = _dot_exact01(tril, dt * a_r, _NN, True)
    dtt = _softplus(dtrt_ref[...] + bias_c[...])
    cst = _dot_exact01(dtt * a_c, row <= col, _NN, False)
    return tril, lane, a_r, dt, cs, cst


def _ssd_specs(b, nc, rev):
    ci = (lambda c: nc - 1 - c) if rev else (lambda c: c)
    rows = lambda w: pl.BlockSpec((Q, w), lambda bb, c: (bb * nc + ci(c), 0))
    dtt = pl.BlockSpec((LANES, Q), lambda bb, c: (0, bb * nc + ci(c)))
    const = lambda s: pl.BlockSpec(s, lambda bb, c: (0,) * len(s))
    state = pl.BlockSpec((None, N_PAIRS, LANES, SSD_STATE), lambda bb, c: (bb * nc + ci(c), 0, 0, 0))
    return rows, dtt, const, state


def ssd_fwd(xs, bm, cm, dtraw, dt_bias, a_log, b):
    t = xs.shape[0]
    nc = SEQ // Q
    rows, dtt_spec, const, state = _ssd_specs(b, nc, False)
    bias_r, bias_c = _pad_lanes(dt_bias)
    alog_r, alog_c = _pad_lanes(a_log)

    def body(xs_ref, b_ref, c_ref, dtr_ref, dtrt_ref, br, bc, ar, ac, y_ref, hp_ref, h_scr):
        @pl.when(pl.program_id(1) == 0)
        def _():
            h_scr[...] = jnp.zeros(h_scr.shape, F32)
        tril, lane, _, dt, cs, cst = _ssd_common(dtr_ref, dtrt_ref, br, bc, ar, ac)
        sub = lax.broadcasted_iota(jnp.int32, (LANES, 1), 0)
        y_acc = [jnp.zeros((Q, LANES), F32) for _ in range(N_PAIRS)]
        h_old = [h_scr[p] for p in range(N_PAIRS)]
        h_new = [jnp.zeros((LANES, SSD_STATE), F32) for _ in range(N_PAIRS)]
        for g in range(SSD_GROUPS):
            bg = b_ref[:, g * SSD_STATE:(g + 1) * SSD_STATE].astype(BF16)
            cg = c_ref[:, g * SSD_STATE:(g + 1) * SSD_STATE].astype(BF16)
            cb = _dot(cg, bg, _NT)
            for j in range(g * HEADS_PER_GROUP, (g + 1) * HEADS_PER_GROUP):
                p, side = j // 2, j % 2
                m = (lane < HEAD_DIM) if side == 0 else (lane >= HEAD_DIM)
                ms = (sub < HEAD_DIM) if side == 0 else (sub >= HEAD_DIM)
                csj, dtj = _rep(cs, j), _rep(dt, j)
                lmat = jnp.exp(jnp.where(tril, csj - cst[j:j + 1, :], NEG))
                xdt = jnp.where(m, xs_ref[:, p * LANES:(p + 1) * LANES] * dtj, 0.0)
                hm = jnp.where(ms, h_old[p], 0.0)
                ydiag = _dot((cb * lmat).astype(BF16), xdt.astype(BF16), _NN)
                yoff = jnp.exp(csj) * _dot(cg, hm.astype(BF16), _NT)
                y_acc[p] = y_acc[p] + ydiag + yoff
                last = csj[Q - 1:Q, :]
                sj = _dot((xdt * jnp.exp(last - csj)).astype(BF16), bg, _TN)
                h_new[p] = h_new[p] + jnp.exp(last) * hm + sj
        for p in range(N_PAIRS):
            y_ref[:, p * LANES:(p + 1) * LANES] = y_acc[p]
            hp_ref[p] = h_old[p]
            h_scr[p] = h_new[p]

    return pl.pallas_call(
        body, name="ssd_fwd", grid=(b, nc),
        in_specs=[rows(D_SSD), rows(D_BC), rows(D_BC), rows(LANES), dtt_spec, const((1, LANES)), const((LANES, 1)),
                  const((1, LANES)), const((LANES, 1))],
        out_specs=[rows(D_SSD), state],
        out_shape=[jax.ShapeDtypeStruct((t, D_SSD), F32),
                   jax.ShapeDtypeStruct((b * nc, N_PAIRS, LANES, SSD_STATE), F32)],
        scratch_shapes=[pltpu.VMEM((N_PAIRS, LANES, SSD_STATE), F32)],
        compiler_params=_cparams(("parallel", "arbitrary")),
    )(xs, bm, cm, dtraw, dtraw.T, bias_r, bias_c, alog_r, alog_c)


def ssd_bwd(xs, bm, cm, dtraw, dt_bias, a_log, hprev, dy, b):
    t = xs.shape[0]
    nc = SEQ // Q
    rows, dtt_spec, const, state = _ssd_specs(b, nc, True)
    bias_r, bias_c = _pad_lanes(dt_bias)
    alog_r, alog_c = _pad_lanes(a_log)

    def body(xs_ref, b_ref, c_ref, dtr_ref, dtrt_ref, hp_ref, dy_ref, br, bc, ar, ac,
             dxs_ref, db_ref, dc_ref, ddt_ref, dbias_ref, dalog_ref, dh_scr):
        first = jnp.logical_and(pl.program_id(0) == 0, pl.program_id(1) == 0)

        @pl.when(pl.program_id(1) == 0)
        def _():
            dh_scr[...] = jnp.zeros(dh_scr.shape, F32)

        @pl.when(first)
        def _():
            dbias_ref[...] = jnp.zeros(dbias_ref.shape, F32)
            dalog_ref[...] = jnp.zeros(dalog_ref.shape, F32)
        tril, lane, a_r, dt, cs, cst = _ssd_common(dtr_ref, dtrt_ref, br, bc, ar, ac)
        sub = lax.broadcasted_iota(jnp.int32, (LANES, 1), 0)
        rowq = lax.broadcasted_iota(jnp.int32, (Q, 1), 0)
        triu = (lax.broadcasted_iota(jnp.int32, (Q, Q), 0) <= lax.broadcasted_iota(jnp.int32, (Q, Q), 1)).astype(F32)
        dxs_acc = [jnp.zeros((Q, LANES), F32) for _ in range(N_PAIRS)]
        dh_in = [dh_scr[p] for p in range(N_PAIRS)]
        h_in = [hp_ref[p] for p in range(N_PAIRS)]
        dh_out = [jnp.zeros((LANES, SSD_STATE), F32) for _ in range(N_PAIRS)]
        ddt = jnp.zeros((Q, LANES), F32)
        dalog = jnp.zeros((1, LANES), F32)
        for g in range(SSD_GROUPS):
            gs = slice(g * SSD_STATE, (g + 1) * SSD_STATE)
            bg, cg = b_ref[:, gs].astype(BF16), c_ref[:, gs].astype(BF16)
            cb = _dot(cg, bg, _NT)
            dcb = jnp.zeros((Q, Q), F32)
            dbg = jnp.zeros((Q, SSD_STATE), F32)
            dcg = jnp.zeros((Q, SSD_STATE), F32)
            for j in range(g * HEADS_PER_GROUP, (g + 1) * HEADS_PER_GROUP):
                p, side = j // 2, j % 2
                m = (lane < HEAD_DIM) if side == 0 else (lane >= HEAD_DIM)
                ms = (sub < HEAD_DIM) if side == 0 else (sub >= HEAD_DIM)
                csj, dtj = _rep(cs, j), _rep(dt, j)
                lmat = jnp.exp(jnp.where(tril, csj - cst[j:j + 1, :], NEG))
                x2 = jnp.where(m, xs_ref[:, p * LANES:(p + 1) * LANES], 0.0)
                xdt = x2 * dtj
                dym = jnp.where(m, dy_ref[:, p * LANES:(p + 1) * LANES], 0.0)
                hm = jnp.where(ms, h_in[p], 0.0)
                dhm = jnp.where(ms, dh_in[p], 0.0)
                ecs = jnp.exp(csj)
                last = csj[Q - 1:Q, :]
                decay = jnp.exp(last - csj)
                el = jnp.exp(last)
                gmat = cb * lmat
                dymb, xdtb = dym.astype(BF16), xdt.astype(BF16)
                dg = _dot(dymb, xdtb, _NT)
                dxdt = _dot(gmat.astype(BF16), dymb, _TN)
                dcb = dcb + dg * lmat
                ej = dg * gmat
                col_sums = jnp.broadcast_to(jnp.sum(ej, axis=0, keepdims=True), (Q, Q)).T
                dcs = jnp.sum(ej, axis=1, keepdims=True) - col_sums
                ch = _dot(cg, hm.astype(BF16), _NT)
                dye = dym * ecs
                dcs = dcs + jnp.sum(dye * ch, axis=1, keepdims=True)
                dcg = dcg + _dot(dye.astype(BF16), hm.astype(BF16), _NN)
                dhp = _dot(dye.astype(BF16), cg, _TN)
                wmat = _dot(bg, dhm.astype(BF16), _NT)
                xd = xdt * decay
                dxdt = dxdt + decay * wmat
                ddl = jnp.sum(xd * wmat, axis=1, keepdims=True)
                dlast = jnp.sum(ddl, axis=0, keepdims=True) + el * jnp.sum(jnp.sum(dhm * hm, axis=1, keepdims=True), axis=0, keepdims=True)
                dcs = dcs - ddl + jnp.where(rowq == Q - 1, dlast, 0.0)
                dbg = dbg + _dot(xd.astype(BF16), dhm.astype(BF16), _NN)
                dh_out[p] = dh_out[p] + el * dhm + dhp
                da = _dot_exact01(triu, dcs, _NN, True)
                aj = jnp.sum(jnp.where(lane == j, a_r, 0.0), axis=1, keepdims=True)
                ddtj = da * aj + jnp.sum(dxdt * x2, axis=1, keepdims=True)
                ddt = ddt + jnp.where(lane == j, ddtj, 0.0)
                dalog = dalog + jnp.where(lane == j, jnp.sum(da * dtj, axis=0, keepdims=True) * aj, 0.0)
                dxs_acc[p] = dxs_acc[p] + dxdt * dtj
            dcbb = dcb.astype(BF16)
            dc_ref[:, gs] = dcg + _dot(dcbb, bg, _NN)
            db_ref[:, gs] = dbg + _dot(dcbb, cg, _TN)
        for p in range(N_PAIRS):
            dxs_ref[:, p * LANES:(p + 1) * LANES] = dxs_acc[p]
            dh_scr[p] = dh_out[p]
        ddtraw = ddt * _sigmoid(dtr_ref[...] + br[...])
        ddt_ref[...] = ddtraw
        dbias_ref[...] += jnp.sum(ddtraw, axis=0, keepdims=True)
        dalog_ref[...] += dalog

    return pl.pallas_call(
        body, name="ssd_bwd", grid=(b, nc),
        in_specs=[rows(D_SSD), rows(D_BC), rows(D_BC), rows(LANES), dtt_spec, state, rows(D_SSD), const((1, LANES)),
                  const((LANES, 1)), const((1, LANES)), const((LANES, 1))],
        out_specs=[rows(D_SSD), rows(D_BC), rows(D_BC), rows(LANES), const((1, LANES)), const((1, LANES))],
        out_shape=[jax.ShapeDtypeStruct((t, D_SSD), F32), jax.ShapeDtypeStruct((t, D_BC), F32),
                   jax.ShapeDtypeStruct((t, D_BC), F32), jax.ShapeDtypeStruct((t, LANES), F32),
                   jax.ShapeDtypeStruct((1, LANES), F32), jax.ShapeDtypeStruct((1, LANES), F32)],
        scratch_shapes=[pltpu.VMEM((N_PAIRS, LANES, SSD_STATE), F32)],
        compiler_params=_cparams(("arbitrary", "arbitrary")),
    )(xs, bm, cm, dtraw, dtraw.T, hprev, dy, bias_r, bias_c, alog_r, alog_c)


def _split_w_in(w_in):
    w_dt = jnp.pad(w_in[:, D_QKVZ + D_CONV:], ((0, 0), (0, LANES - N_HEADS)))
    return w_in[:, :D_QKVZ], w_in[:, D_QKVZ:D_QKVZ + D_CONV], w_dt


def mixer_fwd(hb, p, cosv, sinv, b):
    t = hb.shape[0]
    w_a, w_b, w_c = _split_w_in(p['w_in'])
    qkvz = mm("in_qkvz", [(hb, w_a, 'nn')], D_QKVZ)
    xbc = mm("in_xbc", [(hb, w_b, 'nn')], D_CONV)
    dtraw = mm("in_dt", [(hb, w_c, 'nn')], LANES)
    mixed, *lses = attn_fwd(qkvz, cosv, sinv, b)
    attn = attn_norm_fwd(mixed, p['attn_norm_w'])
    xs, bm, cm = conv_fwd(xbc, p['conv_w'], p['conv_b'])
    y, hprev = ssd_fwd(xs, bm, cm, dtraw, p['dt_bias'], p['a_log'], b)
    dskip = jnp.repeat(p['d_skip'].reshape(-1), HEAD_DIM).reshape(1, D_SSD)
    yg, = rowwise("ssd_gate", _gate, [y, xs, Op(qkvz, D_SSD, 3)], [dskip, p['ssd_norm_w']], [(t, D_SSD, BF16)])
    mix = mm("out_proj", [(attn, p['w_out'][:D_ATTN], 'nn'), (yg, p['w_out'][D_ATTN:], 'nn')], D_MODEL)
    res = dict(hb=hb, qkvz=qkvz, xbc=xbc, dtraw=dtraw, mixed=mixed, lses=lses, attn=attn, xs=xs, bm=bm, cm=cm,
               y=y, hprev=hprev, dskip=dskip, yg=yg, cosv=cosv, sinv=sinv)
    return mix, res


def mixer_bwd(r, p, dmix, dh_resid, b):
    t = dmix.shape[0]
    w_a, w_b, w_c = _split_w_in(p['w_in'])
    w_out = p['w_out']
    dattn = mm("out_bwd_dattn", [(dmix, w_out[:D_ATTN], 'nt')], D_ATTN)
    dyg = mm("out_bwd_dyg", [(dmix, w_out[D_ATTN:], 'nt')], D_SSD)
    dw_out = jnp.concatenate([mm_tn("out_bwd_dw_a", r['attn'], dmix, BF16),
                              mm_tn("out_bwd_dw_y", r['yg'], dmix, BF16)], axis=0)

    def gate_bwd(dy_, y_, xs_, z_, ds_, w_):
        _, vjp = jax.vjp(_gate, y_, xs_, z_, ds_, w_)
        return vjp(dy_)

    dy, dxs_a, dz, ddskip, dssd_norm = rowwise(
        "ssd_gate_bwd", gate_bwd, [dyg, r['y'], r['xs'], Op(r['qkvz'], D_SSD, 3)], [r['dskip'], p['ssd_norm_w']],
        [(t, D_SSD, F32), (t, D_SSD, F32), (t, D_SSD, BF16)], accs=[(1, D_SSD), (1, D_SSD)])
    dxs_b, dbm, dcm, ddtraw, ddt_bias, da_log = ssd_bwd(r['xs'], r['bm'], r['cm'], r['dtraw'], p['dt_bias'], p['a_log'],
                                                        r['hprev'], dy, b)
    dxbc, dconv_w, dconv_b = conv_bwd(r['xbc'], p['conv_w'], p['conv_b'], dxs_a, dxs_b, dbm, dcm)
    dmixed, dattn_norm = attn_norm_bwd(dattn, r['mixed'], p['attn_norm_w'])
    dq, dk, dv = attn_bwd(r['qkvz'], r['cosv'], r['sinv'], dmixed, r['mixed'], r['lses'], b)
    wq, wk, wv, wz = (w_a[:, i * D_ATTN:(i + 1) * D_ATTN] for i in range(4))
    dh = mm("in_bwd_dh", [(dq, wq, 'nt'), (dk, wk, 'nt'), (dv, wv, 'nt'), (dz, wz, 'nt'), (dxbc, w_b, 'nt'),
                          (ddtraw, w_c, 'nt')], D_MODEL, add=dh_resid, tn=512)
    h = r['hb']
    dw_in = jnp.concatenate([mm_tn("in_bwd_dwq", h, dq, BF16), mm_tn("in_bwd_dwk", h, dk, BF16),
                             mm_tn("in_bwd_dwv", h, dv, BF16), mm_tn("in_bwd_dwz", h, dz, BF16),
                             mm_tn("in_bwd_dwx", h, dxbc, BF16), mm_tn("in_bwd_dwdt", h, ddtraw, BF16)[:, :N_HEADS]], axis=1)
    head_sum = lambda v: v.reshape(N_HEADS, HEAD_DIM).sum(axis=1).reshape(1, N_HEADS)
    grads = dict(w_in=dw_in, w_out=dw_out, conv_w=dconv_w, conv_b=dconv_b, dt_bias=ddt_bias[:, :N_HEADS],
                 a_log=da_log[:, :N_HEADS], d_skip=head_sum(ddskip), attn_norm_w=dattn_norm, ssd_norm_w=dssd_norm)
    return dh, grads


FFN1_KEYS = ('ffn1_gate', 'ffn1_up', 'ffn1_down')
FFN2_KEYS = ('ffn2_gate', 'ffn2_up', 'ffn2_down')
MIXER_KEYS = ('w_in', 'conv_w', 'w_out')
FFN_COL = ('ffn1_gate', 'ffn1_up', 'ffn2_gate', 'ffn2_up')
FFN_ROW = ('ffn1_down', 'ffn2_down')
CONV_W_COMM = (8, 2 * LANES)
SMALL = 'small'


def comm_shape(k, shapes):
    if k in FFN_COL:
        return (D_MODEL, FF_PAD)
    if k in FFN_ROW:
        return (FF_PAD, D_MODEL)
    if k == 'conv_w':
        return CONV_W_COMM
    return tuple(shapes[k][1:])


def to_comm(k, vals, shapes):
    a = vals[k].reshape(shapes[k][1:])
    r_, c_ = comm_shape(k, shapes)
    return jnp.pad(a, ((0, r_ - a.shape[0]), (0, c_ - a.shape[1])))


SMALL_ROWS, SMALL_COLS = 16, D_CONV


def pack_small(small):
    rows = [jnp.pad(small[r].reshape(1, -1), ((0, 0), (0, SMALL_COLS - small[r].size))) for r in REPLICATED]
    return jnp.concatenate(rows + [jnp.zeros((SMALL_ROWS - len(rows), SMALL_COLS), F32)], axis=0)


def full_weight(k, g):
    if k in FFN_COL:
        return jnp.concatenate([g[p] for p in range(N_DEV)], axis=1)
    if k == 'conv_w':
        return jnp.transpose(g[:, :CONV_WIDTH, :D_CONV // N_DEV], (1, 0, 2)).reshape(CONV_WIDTH, D_CONV)
    return g.reshape(N_DEV * g.shape[1], g.shape[2])


def grad_shards(k, g):
    if k in FFN_COL:
        return jnp.stack([g[:, p * FF_PAD:(p + 1) * FF_PAD] for p in range(N_DEV)])
    if k == 'conv_w':
        s = jnp.transpose(g.reshape(CONV_WIDTH, N_DEV, D_CONV // N_DEV), (1, 0, 2))
        return jnp.pad(s, ((0, 0), (0, CONV_W_COMM[0] - CONV_WIDTH), (0, CONV_W_COMM[1] - D_CONV // N_DEV)))
    return g.reshape(N_DEV, g.shape[0] // N_DEV, g.shape[1])


def _flip(v, bit):
    return 1 - v if bit else v


N_PEER_COPIES = N_DEV - 1


def _comm_call(name, body, arrs, out_shape):
    n = len(arrs)
    return pl.pallas_call(
        functools.partial(body, n), name=name, out_shape=out_shape,
        in_specs=[pl.BlockSpec(memory_space=pl.ANY)] * n, out_specs=[pl.BlockSpec(memory_space=pl.ANY)] * n,
        scratch_shapes=[pltpu.SemaphoreType.DMA((n * N_PEER_COPIES,)), pltpu.SemaphoreType.DMA((n * N_PEER_COPIES,)),
                        pltpu.SemaphoreType.DMA((n,))],
    )(*arrs)


def all_gather(arrs):
    def body(n, *refs):
        x_refs, out_refs, (send_sems, recv_sems, local_sems) = refs[:n], refs[n:2 * n], refs[2 * n:]
        x, y, c = lax.axis_index("x"), lax.axis_index("y"), lax.axis_index("c")
        me, sibling = (x, y, c), (x, y, 1 - c)
        chips = [(1 - x, y), (x, 1 - y), (1 - x, 1 - y)]

        def copy(a, k, block, to, src=None):
            px, py, pc = block
            dst = out_refs[a].at[4 * px + 2 * py + pc]
            return pltpu.make_async_remote_copy(
                src_ref=dst if src is None else src, dst_ref=dst, send_sem=send_sems.at[a * N_PEER_COPIES + k],
                recv_sem=recv_sems.at[a * N_PEER_COPIES + k], device_id=to, device_id_type=MESH)

        mine = [pltpu.make_async_copy(x_refs[a], out_refs[a].at[4 * x + 2 * y + c], local_sems.at[a]) for a in range(n)]
        started = []
        for a in range(n):
            mine[a].start()
            first = [copy(a, 0, me, sibling, src=x_refs[a])]
            first += [copy(a, 1 + j, me, (*chip, c), src=x_refs[a]) for j, chip in enumerate(chips)]
            for cp in first:
                cp.start()
            started += first
        for j, chip in enumerate(chips):
            for a in range(n):
                copy(a, 1 + j, (*chip, c), me).wait_recv()
                cp = copy(a, 4 + j, (*chip, c), sibling)
                cp.start()
                started.append(cp)
        for a in range(n):
            copy(a, 0, sibling, me).wait_recv()
            for j, chip in enumerate(chips):
                copy(a, 4 + j, (*chip, 1 - c), me).wait_recv()
        for cp in started:
            cp.wait_send()
        for cp in mine:
            cp.wait()

    return _comm_call("all_gather_weights", body, arrs,
                      [jax.ShapeDtypeStruct((N_DEV,) + a.shape, a.dtype) for a in arrs])


def all_to_all(arrs):
    def body(n, *refs):
        s_refs, r_refs, (send_sems, recv_sems, local_sems) = refs[:n], refs[n:2 * n], refs[2 * n:]
        x, y, c = lax.axis_index("x"), lax.axis_index("y"), lax.axis_index("c")
        me = 4 * x + 2 * y + c

        def peer(k):
            return _flip(x, k & 4), _flip(y, k & 2), _flip(c, k & 1)

        def copy(a, k, landing):
            px, py, pc = peer(k)
            p = 4 * px + 2 * py + pc
            src, dst = (s_refs[a].at[me], r_refs[a].at[p]) if landing else (s_refs[a].at[p], r_refs[a].at[me])
            return pltpu.make_async_remote_copy(
                src_ref=src, dst_ref=dst, send_sem=send_sems.at[a * N_PEER_COPIES + k - 1],
                recv_sem=recv_sems.at[a * N_PEER_COPIES + k - 1], device_id=(px, py, pc), device_id_type=MESH)

        mine = [pltpu.make_async_copy(s_refs[a].at[me], r_refs[a].at[me], local_sems.at[a]) for a in range(n)]
        sends = [copy(a, k, False) for a in range(n) for k in range(1, N_DEV)]
        for cp in mine + sends:
            cp.start()
        for a in range(n):
            for k in range(1, N_DEV):
                copy(a, k, True).wait_recv()
        for cp in sends:
            cp.wait_send()
        for cp in mine:
            cp.wait()

    return _comm_call("all_to_all_grads", body, arrs, [jax.ShapeDtypeStruct(a.shape, a.dtype) for a in arrs])


_HBM = pl.BlockSpec(memory_space=pltpu.HBM)
_SEM = pl.BlockSpec(memory_space=pltpu.SEMAPHORE)
_EFFECT = pltpu.SideEffectType.DATAFLOW_SIDE_EFFECTING


def _peer(k):
    x, y, c = lax.axis_index("x"), lax.axis_index("y"), lax.axis_index("c")
    return _flip(x, k & 4), _flip(y, k & 2), _flip(c, k & 1)


def _my_index():
    return 4 * lax.axis_index("x") + 2 * lax.axis_index("y") + lax.axis_index("c")


def _split_copies(mode, n, src_refs, land_refs, send_sems, recv_sems):
    me = _my_index()
    out = []
    for a in range(n):
        for k in range(1, N_DEV):
            px, py, pc = _peer(k)
            src = src_refs[a].at[4 * px + 2 * py + pc] if mode == 'scatter' else src_refs[a]
            out.append(pltpu.make_async_remote_copy(
                src_ref=src, dst_ref=land_refs[a].at[me], send_sem=send_sems.at[a * N_PEER_COPIES + k - 1],
                recv_sem=recv_sems.at[a * N_PEER_COPIES + k - 1], device_id=(px, py, pc), device_id_type=MESH))
    return out


def exchange_start(name, mode, srcs):
    n = len(srcs)
    lands = [lax.empty(s.shape if mode == 'scatter' else (N_DEV,) + s.shape, s.dtype) for s in srcs]

    def body(*refs):
        src_refs, land_refs, send_sems, recv_sems = refs[:n], refs[n:2 * n], refs[2 * n], refs[2 * n + 1]
        for cp in _split_copies(mode, n, src_refs, land_refs, send_sems, recv_sems):
            cp.start()
        refs[-1][...] = jnp.zeros(refs[-1].shape, F32)

    sems = pltpu.SemaphoreType.DMA((n * N_PEER_COPIES,))
    res = pl.pallas_call(
        body, name=name,
        out_shape=(sems, sems, *[pltpu.HBM(a.shape, a.dtype) for a in srcs + lands], jax.ShapeDtypeStruct((8, LANES), F32)),
        in_specs=(_HBM,) * (2 * n), out_specs=(_SEM, _SEM, *(_HBM,) * (2 * n), pl.BlockSpec(memory_space=pltpu.VMEM)),
        input_output_aliases={i: 2 + i for i in range(2 * n)},
        compiler_params=pltpu.CompilerParams(has_side_effects=_EFFECT),
    )(*[pltpu.with_memory_space_constraint(a, pltpu.HBM) for a in srcs + lands])
    return (mode, n, res[:-1]), res[-1]


def exchange_wait(name, handles, after):
    mode, n, (send_sems, recv_sems, *bufs) = handles

    def body(*refs):
        src_refs, land_refs, s_sems, r_sems = refs[:n], refs[n:2 * n], refs[2 * n], refs[2 * n + 1]
        for cp in _split_copies(mode, n, src_refs, land_refs, s_sems, r_sems):
            cp.wait_send()
            cp.wait_recv()

    res = pl.pallas_call(
        body, name=name, out_shape=tuple(pltpu.HBM(a.shape, a.dtype) for a in bufs),
        in_specs=(*(_HBM,) * (2 * n), _SEM, _SEM, pl.BlockSpec(memory_space=pl.ANY)), out_specs=(_HBM,) * (2 * n),
        input_output_aliases={i: i for i in range(2 * n)},
        compiler_params=pltpu.CompilerParams(has_side_effects=_EFFECT),
    )(*bufs, send_sems, recv_sems, after)
    srcs, lands = res[:n], res[n:]
    me = _my_index()
    own = [lax.dynamic_index_in_dim(s, me, 0, keepdims=True) if mode == 'scatter' else s[None] for s in srcs]
    return [lax.dynamic_update_slice(l, o, (me, 0, 0)) for l, o in zip(lands, own)]


def _adamw_math(g, w, m, v):
    c1 = 1.0 / (1.0 - ADAM_B1 ** ADAM_STEP)
    c2 = 1.0 / (1.0 - ADAM_B2 ** ADAM_STEP)
    m = ADAM_B1 * m + (1.0 - ADAM_B1) * g
    v = ADAM_B2 * v + (1.0 - ADAM_B2) * jnp.square(g)
    return g, -ADAM_LR * ((m * c1) / (jnp.sqrt(v * c2) + ADAM_EPS) + ADAM_WD * w), m, v


def adamw(name, recv, w, m, v, tm):
    rows, cols = w.shape
    tm = min(tm, rows)

    def body(*refs):
        g = refs[0][0:tm, 0:cols].astype(F32)
        for s in range(1, N_DEV):
            g = g + refs[s][0:tm, 0:cols].astype(F32)
        res = _adamw_math(g, *[r[...] for r in refs[N_DEV:N_DEV + 3]])
        for r, val in zip(refs[N_DEV + 3:], res):
            r[...] = val

    part = lambda s: pl.BlockSpec((None, recv.shape[1] if tm == rows else tm, recv.shape[2]), lambda i: (s, i, 0))
    tile = pl.BlockSpec((tm, cols), lambda i: (i, 0))
    return pl.pallas_call(
        body, name=name, grid=(rows // tm,), in_specs=[part(s) for s in range(N_DEV)] + [tile] * 3, out_specs=[tile] * 4,
        out_shape=[jax.ShapeDtypeStruct((rows, cols), F32)] * 4, compiler_params=_cparams(("parallel",)),
    )(*[recv] * N_DEV, w, m, v)


def adamw_small(recv, wl, ml, vl):
    n = len(REPLICATED)

    def body(recv_ref, *refs):
        g = recv_ref[0]
        for s in range(1, N_DEV):
            g = g + recv_ref[s]
        for r in range(n):
            w, m, v = (refs[j * n + r][...] for j in range(3))
            for j, val in enumerate(_adamw_math(g[r:r + 1, :w.shape[1]], w, m, v)):
                refs[(3 + j) * n + r][...] = val

    arrs = [d[k].reshape(1, -1) for d in (wl, ml, vl) for k in REPLICATED]
    res = pl.pallas_call(
        body, name="adamw_small", out_shape=[jax.ShapeDtypeStruct(a.shape, F32) for a in arrs[:n]] * 4,
    )(recv, *arrs)
    return [{k: res[j * n + r].reshape(wl[k].shape) for r, k in enumerate(REPLICATED)} for j in range(4)]


ADAMW_TM = {'ffn1_gate': 256, 'ffn1_up': 256, 'ffn2_gate': 256, 'ffn2_up': 256, 'w_in': 32}


def kernel(x, positions, ln1_g, ln1_b, ffn1_gate, ffn1_up, ffn1_down, w_in, conv_w, conv_b, dt_bias, a_log, d_skip, attn_norm_w, ssd_norm_w, w_out, ln2_g, ln2_b, ffn2_gate, ffn2_up, ffn2_down, ln3_g, ln3_b, loss_target, m_ln1_g, m_ln1_b, m_ffn1_gate, m_ffn1_up, m_ffn1_down, m_w_in, m_conv_w, m_conv_b, m_dt_bias, m_a_log, m_d_skip, m_attn_norm_w, m_ssd_norm_w, m_w_out, m_ln2_g, m_ln2_b, m_ffn2_gate, m_ffn2_up, m_ffn2_down, m_ln3_g, m_ln3_b, v_ln1_g, v_ln1_b, v_ffn1_gate, v_ffn1_up, v_ffn1_down, v_w_in, v_conv_w, v_conv_b, v_dt_bias, v_a_log, v_d_skip, v_attn_norm_w, v_ssd_norm_w, v_w_out, v_ln2_g, v_ln2_b, v_ffn2_gate, v_ffn2_up, v_ffn2_down, v_ln3_g, v_ln3_b):
    args = dict(locals())
    wl = {k: args[k] for k in WEIGHTS}
    ml = {k: args["m_" + k] for k in WEIGHTS}
    vl = {k: args["v_" + k] for k in WEIGHTS}
    shapes = {k: wl[k].shape for k in WEIGHTS}
    b, s, dm = x.shape
    t = b * s

    sent = {k: to_comm(k, wl, shapes).astype(F32 if k == 'conv_w' else BF16) for k in SHARDED}
    p = {k: full_weight(k, g) for k, g in zip(FFN1_KEYS, all_gather([sent[k] for k in FFN1_KEYS]))}
    gather_mixer, token_m = exchange_start("gather_mixer_start", 'gather', [sent[k] for k in MIXER_KEYS])
    sent['ffn2_gate'] = sent['ffn2_gate'] + token_m[0, 0].astype(BF16)
    gather_ffn2, token_f = exchange_start("gather_ffn2_start", 'gather', [sent[k] for k in FFN2_KEYS])
    for k in REPLICATED:
        p[k] = wl[k].reshape(1, -1)

    x2 = x.reshape(t, dm)
    cosv, sinv = rope_tables(positions)
    f1, res1 = ffn_fwd("ffn1", x2, p['ffn1_gate'], p['ffn1_up'], p['ffn1_down'], after=(token_m, token_f))
    h1, h1b = resid_ln_fwd("ln1", 0.5, x2, f1, p['ln1_g'], p['ln1_b'])
    for k, g in zip(MIXER_KEYS, exchange_wait("gather_mixer_wait", gather_mixer, h1b)):
        p[k] = full_weight(k, g)
    mix, resm = mixer_fwd(h1b, p, cosv, sinv, b)
    h2, h2b = resid_ln_fwd("ln2", 1.0, h1, mix, p['ln2_g'], p['ln2_b'])
    for k, g in zip(FFN2_KEYS, exchange_wait("gather_ffn2_wait", gather_ffn2, h2b)):
        p[k] = full_weight(k, g)
    f2, res3 = ffn_fwd("ffn2", h2b, p['ffn2_gate'], p['ffn2_up'], p['ffn2_down'])

    small, full = {}, {}
    dh2_res, df2, small['ln3_g'], small['ln3_b'], sq = ln_loss_bwd("ln3_loss_bwd", h2, f2, loss_target.reshape(t, dm),
                                                                   p['ln3_g'], p['ln3_b'])
    loss = lax.psum(jnp.sum(sq) * (0.5 / dm), AXES)

    dh2, full['ffn2_gate'], full['ffn2_up'], full['ffn2_down'] = ffn_bwd("ffn2", res3, p['ffn2_gate'], p['ffn2_up'],
                                                                       p['ffn2_down'], df2, dh2_res)
    ffn2_exchange, token = exchange_start("grads_ffn2_start", 'scatter', [grad_shards(k, full[k]) for k in FFN2_KEYS])
    dh1_res, dmix, small['ln2_g'], small['ln2_b'] = resid_ln_bwd("ln2_bwd", 1.0, h1, mix, p['ln2_g'] + token[:1, :1],
                                                                 p['ln2_b'], dh2)
    dh1, gm = mixer_bwd(resm, p, dmix, dh1_res, b)
    for k in ('conv_b', 'dt_bias', 'a_log', 'd_skip', 'attn_norm_w', 'ssd_norm_w'):
        small[k] = gm[k]
    mixer_exchange, token = exchange_start("grads_mixer_start", 'scatter', [grad_shards(k, gm[k]) for k in MIXER_KEYS])
    dx_res, df1, small['ln1_g'], small['ln1_b'] = resid_ln_bwd("ln1_bwd", 0.5, x2, f1, p['ln1_g'] + token[:1, :1],
                                                               p['ln1_b'], dh1)
    hb, g, u, a = res1
    small_part = pack_small(small)
    dg, du = ffn_da_act("ffn1_bwd_da_act", df1, p['ffn1_down'], g, u)
    dwd = mm_tn("ffn1_bwd_dwd", a, df1, BF16, after=dg)
    down_exchange, token = exchange_start("grads_ffn1_down_start", 'scatter', [
        grad_shards('ffn1_down', dwd), jnp.broadcast_to(small_part[None], (N_DEV,) + small_part.shape)])
    dx = mm("ffn1_bwd_dh", [(dg, p['ffn1_gate'], 'nt'), (du, p['ffn1_up'], 'nt')], D_MODEL, add=dx_res, tn=512, after=token)
    dwg = mm_tn("ffn1_bwd_dwg", hb, dg, BF16, after=dx)
    gate_exchange, token = exchange_start("grads_ffn1_gate_start", 'scatter', [grad_shards('ffn1_gate', dwg)])
    dwu = mm_tn("ffn1_bwd_dwu", hb, du, BF16, after=token)
    recv = dict(zip(('ffn1_up',), all_to_all([grad_shards('ffn1_up', dwu)])))
    for keys, name, ex in (((FFN2_KEYS), "grads_ffn2_wait", ffn2_exchange), (MIXER_KEYS, "grads_mixer_wait", mixer_exchange),
                           (('ffn1_down', SMALL), "grads_ffn1_down_wait", down_exchange),
                           (('ffn1_gate',), "grads_ffn1_gate_wait", gate_exchange)):
        recv.update(zip(keys, exchange_wait(name, ex, recv['ffn1_up'])))
    outs = adamw_small(recv.pop(SMALL), wl, ml, vl)
    for k, r in recv.items():
        shard = shapes[k][1:]
        res = adamw(f"adamw_{k}", r, *[d[k].reshape(shard) for d in (wl, ml, vl)], ADAMW_TM.get(k, shard[0]))
        for o, a in zip(outs, res):
            o[k] = a.reshape(shapes[k])
    return (loss, dx.reshape(b, s, dm), *[o[k] for o in outs for k in WEIGHTS])
```

```python
import functools
import math

import jax
import jax.numpy as jnp
import numpy as np
from jax import lax
from jax.experimental import pallas as pl
from jax.experimental.pallas import tpu as pltpu

F32, BF16 = jnp.float32, jnp.bfloat16
HI = lax.Precision.HIGHEST
MESH = pl.DeviceIdType.MESH
AXES = ("x", "y", "c")
N_DEV = 8

D_MODEL = 1024
SEQ = 2048
HEAD_DIM = 64
N_HEADS = 12
D_ATTN = N_HEADS * HEAD_DIM
DILATIONS = (1, 4, 16)
ATTN_BLOCK = 128
ROPE_THETA = 500000.0
ROPE_DIM = 16
D_SSD = 768
SSD_GROUPS = 4
SSD_STATE = 128
SSD_CHUNK = 128
D_BC = SSD_GROUPS * SSD_STATE
D_CONV = D_SSD + 2 * D_BC
CONV_WIDTH = 4
D_QKVZ = 3 * D_ATTN + D_SSD
D_IN_PROJ = D_QKVZ + D_CONV + N_HEADS
D_FF = 2816
ALPHA = 2.0 ** 0.25
LN_EPS = 1e-5
RMS_EPS = 1e-6
ADAM_LR, ADAM_B1, ADAM_B2, ADAM_EPS, ADAM_WD, ADAM_STEP = 0.001, 0.9, 0.999, 1e-08, 0.01, 10

LANES = 128
VMEM_LIMIT = 52 * 1024 * 1024
NEG = -1e30

WEIGHTS = ['ln1_g', 'ln1_b', 'ffn1_gate', 'ffn1_up', 'ffn1_down', 'w_in', 'conv_w', 'conv_b', 'dt_bias', 'a_log',
           'd_skip', 'attn_norm_w', 'ssd_norm_w', 'w_out', 'ln2_g', 'ln2_b', 'ffn2_gate', 'ffn2_up', 'ffn2_down',
           'ln3_g', 'ln3_b']
COL_SHARDED = ('ffn1_gate', 'ffn1_up', 'conv_w', 'ffn2_gate', 'ffn2_up')
ROW_SHARDED = ('ffn1_down', 'w_in', 'w_out', 'ffn2_down')
SHARDED = tuple(n for n in WEIGHTS if n in COL_SHARDED or n in ROW_SHARDED)
REPLICATED = tuple(n for n in WEIGHTS if n not in SHARDED)
FF_SHARD = D_FF // N_DEV
FF_PAD = -(-FF_SHARD // LANES) * LANES
D_FF_INT = N_DEV * FF_PAD


def _cparams(sem=None):
    return pltpu.CompilerParams(dimension_semantics=sem, vmem_limit_bytes=VMEM_LIMIT)


def _tile(n, prefs):
    for p in prefs:
        if n % p == 0:
            return p
    return n


class Op:
    def __init__(self, arr, bw=None, cb=0, ro=0):
        self.arr, self.bw, self.cb, self.ro = arr, (arr.shape[1] if bw is None else bw), cb, ro


def _op(a):
    return a if isinstance(a, Op) else Op(a)


def rowwise(name, fn, ins, consts, outs, accs=(), tm=256):
    ins = [_op(a) for a in ins]
    rows = outs[0][0]
    n_in, n_c, n_o, n_a = len(ins), len(consts), len(outs), len(accs)
    tm = min(tm, rows)
    assert rows % tm == 0, (name, rows, tm)

    def body(*refs):
        vals = [r[...].astype(F32) for r in refs[:n_in + n_c]]
        res = fn(*vals)
        res = res if isinstance(res, (tuple, list)) else (res,)
        o_refs = refs[n_in + n_c:n_in + n_c + n_o]
        a_refs = refs[n_in + n_c + n_o:]
        for r, v in zip(o_refs, res[:n_o]):
            r[...] = v.astype(r.dtype)
        if n_a:
            @pl.when(pl.program_id(0) == 0)
            def _():
                for r in a_refs:
                    r[...] = jnp.zeros(r.shape, r.dtype)
            for r, v in zip(a_refs, res[n_o:]):
                r[...] += v

    in_specs = [pl.BlockSpec((tm, o.bw), functools.partial(lambda i, o: (i + o.ro, o.cb), o=o)) for o in ins]
    in_specs += [pl.BlockSpec(c.shape, functools.partial(lambda i, nd: (0,) * nd, nd=c.ndim)) for c in consts]
    out_specs = [pl.BlockSpec((tm, w), lambda i: (i, 0)) for (_, w, _) in outs]
    out_specs += [pl.BlockSpec(s, functools.partial(lambda i, nd: (0,) * nd, nd=len(s))) for s in accs]
    out_shape = [jax.ShapeDtypeStruct((r, w), dt) for (r, w, dt) in outs]
    out_shape += [jax.ShapeDtypeStruct(s, F32) for s in accs]
    res = pl.pallas_call(
        body, name=name, grid=(rows // tm,), in_specs=in_specs, out_specs=out_specs, out_shape=out_shape,
        compiler_params=_cparams(("arbitrary",) if n_a else ("parallel",)),
    )(*[o.arr for o in ins], *consts)
    return res


MM_TM = 512
MM_TN = (1024, 896, 768, 512, 256, 128)
_NT = (((1,), (1,)), ((), ()))
_NN = (((1,), (0,)), ((), ()))
_TN = (((0,), (0,)), ((), ()))


def _dot(a, b, dn, precision=None):
    return lax.dot_general(a, b, dn, preferred_element_type=F32, precision=precision)


def _mm_specs(name, pairs, n_out, tm, tn):
    in_specs, args = [], []
    for a, b, mode in pairs:
        o = _op(a)
        in_specs.append(pl.BlockSpec((tm, o.bw), functools.partial(lambda j, i, o: (i, o.cb), o=o)))
        args.append(o.arr)
        if mode == 'nn':
            assert b.shape == (o.bw, n_out), (name, b.shape, o.bw, n_out)
            in_specs.append(pl.BlockSpec((o.bw, tn), lambda j, i: (0, j)))
        else:
            assert b.shape == (n_out, o.bw), (name, b.shape, o.bw, n_out)
            in_specs.append(pl.BlockSpec((tn, o.bw), lambda j, i: (j, 0)))
        args.append(b)
    return in_specs, args


def _mm_acc(refs, pairs):
    acc = None
    for k, (_, _, mode) in enumerate(pairs):
        d = _dot(refs[2 * k][...].astype(BF16), refs[2 * k + 1][...].astype(BF16), _NN if mode == 'nn' else _NT)
        acc = d if acc is None else acc + d
    return acc


def mm(name, pairs, n_out, add=None, out_dtype=F32, tm=MM_TM, tn=None, after=None):
    m = _op(pairs[0][0]).arr.shape[0]
    tn = tn or _tile(n_out, MM_TN)
    n_p = len(pairs)

    def body(*refs):
        acc = _mm_acc(refs, pairs)
        if add is not None:
            acc = acc + refs[2 * n_p][...]
        refs[-1][...] = acc.astype(refs[-1].dtype)

    in_specs, args = _mm_specs(name, pairs, n_out, tm, tn)
    tile = pl.BlockSpec((tm, tn), lambda j, i: (i, j))
    if add is not None:
        in_specs.append(tile)
        args.append(add)
    if after is not None:
        in_specs.append(pl.BlockSpec(memory_space=pl.ANY))
        args.append(after)
    return pl.pallas_call(
        body, name=name, grid=(n_out // tn, m // tm), in_specs=in_specs, out_specs=tile,
        out_shape=jax.ShapeDtypeStruct((m, n_out), out_dtype),
        compiler_params=_cparams(("parallel", "parallel")),
    )(*args)


def mm_tn(name, a, b, out_dtype=F32, tt=1024, after=None):
    a, b = _op(a), _op(b)
    t = a.arr.shape[0]
    k, n = a.bw, b.bw
    tk = _tile(k, (512, 896, 768, 256, 128))
    tn = _tile(n, (3072, 1792) + MM_TN)
    tt = min(tt, t)
    n_t = t // tt
    order = [] if after is None else [after]

    def body(a_ref, b_ref, *rest):
        o_ref, acc_ref = rest[-2:]
        s = pl.program_id(2)
        d = _dot(a_ref[...].astype(BF16), b_ref[...].astype(BF16), _TN)

        @pl.when(s == 0)
        def _():
            acc_ref[...] = d

        @pl.when(s > 0)
        def _():
            acc_ref[...] += d

        @pl.when(s == n_t - 1)
        def _():
            o_ref[...] = acc_ref[...].astype(o_ref.dtype)

    return pl.pallas_call(
        body, name=name, grid=(k // tk, n // tn, n_t),
        in_specs=[pl.BlockSpec((tt, tk), functools.partial(lambda kk, nn, s, o: (s, o.cb * (o.bw // tk) + kk), o=a)),
                  pl.BlockSpec((tt, tn), functools.partial(lambda kk, nn, s, o: (s, o.cb * (o.bw // tn) + nn), o=b))]
        + [pl.BlockSpec(memory_space=pl.ANY) for _ in order],
        out_specs=pl.BlockSpec((tk, tn), lambda kk, nn, s: (kk, nn)),
        out_shape=jax.ShapeDtypeStruct((k, n), out_dtype),
        scratch_shapes=[pltpu.VMEM((tk, tn), F32)],
        compiler_params=_cparams(("parallel", "parallel", "arbitrary")),
    )(a.arr, b.arr, *order)


def _sigmoid(x):
    return 1.0 / (1.0 + jnp.exp(-x))


def _silu(x):
    return x * _sigmoid(x)


def _softplus(x):
    return jnp.maximum(x, 0.0) + jnp.log(1.0 + jnp.exp(-jnp.abs(x)))


def _act(g, u):
    return _silu(g) * u


def _resid_ln(scale, h, branch, g, b):
    r = ALPHA * h + scale * branch
    mu = jnp.mean(r, axis=-1, keepdims=True)
    var = jnp.mean(jnp.square(r - mu), axis=-1, keepdims=True)
    return (r - mu) * lax.rsqrt(var + LN_EPS) * g + b


def _rms(t, w):
    return t * lax.rsqrt(jnp.mean(t * t, axis=-1, keepdims=True) + RMS_EPS) * w


def _branch_weights(l1, l2, l3):
    m = jnp.maximum(jnp.maximum(l1, l2), l3)
    e1, e2, e3 = jnp.exp(l1 - m), jnp.exp(l2 - m), jnp.exp(l3 - m)
    inv = 1.0 / (e1 + e2 + e3)
    return e1 * inv, e2 * inv, e3 * inv


def _gate(y, xs, z, dskip, w):
    return _rms((y + dskip * xs) * _silu(z), w)


def _rot(x):
    d = lax.broadcasted_iota(jnp.int32, x.shape, 1) % HEAD_DIM
    up = pltpu.roll(x, x.shape[1] - ROPE_DIM // 2, 1)
    down = jnp.where(d < ROPE_DIM, pltpu.roll(x, ROPE_DIM // 2, 1), 0.0)
    return jnp.where(d < ROPE_DIM // 2, up, down)


def ffn_gate_up(name, h, wg, wu, after=()):
    m, nf = h.shape[0], wg.shape[1]
    tn = _tile(nf, MM_TN)

    def body(h_ref, g_w, u_w, *rest):
        g_ref, u_ref, a_ref = rest[-3:]
        hb = h_ref[...].astype(BF16)
        g = _dot(hb, g_w[...].astype(BF16), _NN)
        u = _dot(hb, u_w[...].astype(BF16), _NN)
        g_ref[...] = g.astype(g_ref.dtype)
        u_ref[...] = u.astype(u_ref.dtype)
        a_ref[...] = _act(g, u).astype(a_ref.dtype)

    in_specs, args = _mm_specs(name, [(h, wg, 'nn')], nf, MM_TM, tn)
    in_specs.append(in_specs[1])
    in_specs += [pl.BlockSpec(memory_space=pl.ANY) for _ in after]
    tile = pl.BlockSpec((MM_TM, tn), lambda j, i: (i, j))
    return pl.pallas_call(
        body, name=name, grid=(nf // tn, m // MM_TM), in_specs=in_specs, out_specs=[tile] * 3,
        out_shape=[jax.ShapeDtypeStruct((m, nf), BF16)] * 3, compiler_params=_cparams(("parallel", "parallel")),
    )(*args, wu, *after)


def ffn_da_act(name, df, wd, g, u):
    m, nf = df.shape[0], wd.shape[0]
    tn = _tile(nf, MM_TN)

    def body(df_ref, w_ref, g_ref, u_ref, dg_ref, du_ref):
        da = _dot(df_ref[...].astype(BF16), w_ref[...].astype(BF16), _NT)
        _, vjp = jax.vjp(_act, g_ref[...].astype(F32), u_ref[...].astype(F32))
        dg, du = vjp(da)
        dg_ref[...] = dg.astype(dg_ref.dtype)
        du_ref[...] = du.astype(du_ref.dtype)

    in_specs, args = _mm_specs(name, [(df, wd, 'nt')], nf, MM_TM, tn)
    tile = pl.BlockSpec((MM_TM, tn), lambda j, i: (i, j))
    return pl.pallas_call(
        body, name=name, grid=(nf // tn, m // MM_TM), in_specs=in_specs + [tile, tile], out_specs=[tile] * 2,
        out_shape=[jax.ShapeDtypeStruct((m, nf), BF16)] * 2, compiler_params=_cparams(("parallel", "parallel")),
    )(*args, g, u)


def resid_ln_fwd(name, scale, h, branch, ln_g, ln_b):
    t = h.shape[0]

    def fn(*a):
        y = _resid_ln(scale, *a)
        return y, y

    return rowwise(name, fn, [h, branch], [ln_g, ln_b], [(t, D_MODEL, F32), (t, D_MODEL, BF16)], tm=512)


def ffn_fwd(tag, hb, wg, wu, wd, after=()):
    g, u, a = ffn_gate_up(f"{tag}_gate_up", hb, wg, wu, after)
    f = mm(f"{tag}_down", [(a, wd, 'nn')], D_MODEL)
    return f, (hb, g, u, a)


def ln_loss_bwd(name, h, branch, target, ln_g, ln_b):
    t, dm = h.shape

    def fn(h_, br_, tgt, g_, b_):
        y, vjp = jax.vjp(functools.partial(_resid_ln, 0.5), h_, br_, g_, b_)
        e = y - tgt
        return (*vjp(e * (1.0 / dm)), jnp.sum(e * e, axis=0, keepdims=True))

    return rowwise(name, fn, [h, branch, target], [ln_g, ln_b], [(t, dm, F32), (t, dm, F32)],
                   accs=[(1, dm), (1, dm), (1, dm)], tm=512)


def resid_ln_bwd(name, scale, h, branch, ln_g, ln_b, dout, extra=None):
    t = h.shape[0]

    def fn(h_, br_, do_, *rest):
        g_, b_ = rest[-2], rest[-1]
        _, vjp = jax.vjp(functools.partial(_resid_ln, scale), h_, br_, g_, b_)
        dh, dbr, dg, db = vjp(do_)
        if extra is not None:
            dh = dh + rest[0]
        return dh, dbr, dg, db

    ins = [h, branch, dout] + ([extra] if extra is not None else [])
    return rowwise(name, fn, ins, [ln_g, ln_b], [(t, D_MODEL, F32), (t, D_MODEL, F32)],
                   accs=[(1, D_MODEL), (1, D_MODEL)], tm=512)


def ffn_bwd(tag, res, wg, wu, wd, df, dh_resid):
    hb, g, u, a = res
    dg, du = ffn_da_act(f"{tag}_bwd_da_act", df, wd, g, u)
    dwd = mm_tn(f"{tag}_bwd_dwd", a, df, BF16)
    dh = mm(f"{tag}_bwd_dh", [(dg, wg, 'nt'), (du, wu, 'nt')], D_MODEL, add=dh_resid, tn=512)
    dwg = mm_tn(f"{tag}_bwd_dwg", hb, dg, BF16)
    dwu = mm_tn(f"{tag}_bwd_dwu", hb, du, BF16)
    return dh, dwg, dwu, dwd


def rope_tables(positions):
    inv_freq = ROPE_THETA ** (-jnp.arange(0, ROPE_DIM, 2, dtype=F32) / ROPE_DIM)
    ang = positions.reshape(-1, 1).astype(F32) * inv_freq
    c, s = jnp.cos(ang), jnp.sin(ang)
    t = ang.shape[0]
    cosv = jnp.concatenate([c, c, jnp.ones((t, HEAD_DIM - ROPE_DIM), F32)], axis=1)
    sinv = jnp.concatenate([-s, s, jnp.zeros((t, HEAD_DIM - ROPE_DIM), F32)], axis=1)
    return jnp.tile(cosv, (1, 2)), jnp.tile(sinv, (1, 2))


def _pair_masks():
    lane = lax.broadcasted_iota(jnp.int32, (1, LANES), 1)
    return (lane < HEAD_DIM, lane >= HEAD_DIM)


def _band_masks():
    row = lax.broadcasted_iota(jnp.int32, (ATTN_BLOCK, ATTN_BLOCK), 0)
    col = lax.broadcasted_iota(jnp.int32, (ATTN_BLOCK, ATTN_BLOCK), 1)
    return col >= row, col <= row


def _residue_blocks():
    out = []
    for g, d in enumerate(DILATIONS):
        for r in range(d):
            for i in range(SEQ // d // ATTN_BLOCK):
                rows = lambda j: pl.ds(r + j * ATTN_BLOCK * d, ATTN_BLOCK, stride=d) if d > 1 else pl.ds(j * ATTN_BLOCK, ATTN_BLOCK)
                out.append((g, rows(i), rows(i - 1) if i > 0 else None))
    return out


N_HEAD_PAIRS = D_ATTN // LANES
SCALE = HEAD_DIM ** -0.5
ATTN_GROUP = 4


def _block_operands(qr, kr, v_ref, cur, prev):
    prev_ok, cur_ok = _band_masks()
    if prev is None:
        return qr[cur, :], kr[cur, :].astype(BF16), v_ref[cur, :], cur_ok
    kcat = jnp.concatenate([kr[prev, :], kr[cur, :]], axis=0).astype(BF16)
    vcat = jnp.concatenate([v_ref[prev, :], v_ref[cur, :]], axis=0)
    return qr[cur, :], kcat, vcat, jnp.concatenate([prev_ok, cur_ok], axis=1)


def _attn_specs(b):
    col = lambda cb: pl.BlockSpec((SEQ, LANES), lambda bb, hp: (bb, cb + hp))
    tab = pl.BlockSpec((SEQ, LANES), lambda bb, hp: (bb, 0))
    return col, tab


def attn_fwd(qkvz, cosv, sinv, b):
    t = qkvz.shape[0]
    col, tab = _attn_specs(b)
    blocks = _residue_blocks()

    def body(q_ref, k_ref, v_ref, c_ref, s_ref, o_ref, l1_ref, l2_ref, l3_ref, qr, kr, o1, o2, o3):
        l_refs, o_scr = (l1_ref, l2_ref, l3_ref), (o1, o2, o3)
        c, s = c_ref[...], s_ref[...]
        q, k = q_ref[...], k_ref[...]
        qr[...] = q * c + _rot(q) * s
        kr[...] = k * c + _rot(k) * s
        masks = _pair_masks()
        for lo in range(0, len(blocks), ATTN_GROUP):
            chains = []
            for g, cur, prev in blocks[lo:lo + ATTN_GROUP]:
                q2, kcat, vcat, ok = _block_operands(qr, kr, v_ref, cur, prev)
                for m in masks:
                    qm = jnp.where(m, q2, 0.0).astype(BF16)
                    chains.append(dict(g=g, cur=cur, m=m, v=jnp.where(m, vcat, 0.0).astype(BF16),
                                       s=jnp.where(ok, _dot(qm, kcat, _NT) * SCALE, NEG)))
            for ch in chains:
                mx = jnp.max(ch['s'], axis=1, keepdims=True)
                p = jnp.exp(ch['s'] - mx)
                den = jnp.sum(p, axis=1, keepdims=True)
                ch.update(p=p.astype(BF16), inv=1.0 / den, lse=mx + jnp.log(den))
            for ch in chains:
                ch['o'] = _dot(ch['p'], ch['v'], _NN) * ch['inv']
            for c0, c1 in zip(chains[0::2], chains[1::2]):
                o_scr[c0['g']][c0['cur'], :] = c0['o'] + c1['o']
                l_refs[c0['g']][c0['cur'], :] = jnp.where(c0['m'], c0['lse'], c1['lse'])
        w1, w2, w3 = _branch_weights(l1_ref[...], l2_ref[...], l3_ref[...])
        o_ref[...] = w1 * o1[...] + w2 * o2[...] + w3 * o3[...]

    shp = jax.ShapeDtypeStruct((t, D_ATTN), F32)
    return pl.pallas_call(
        body, name="attn_fwd", grid=(b, N_HEAD_PAIRS),
        in_specs=[col(0), col(N_HEAD_PAIRS), col(2 * N_HEAD_PAIRS), tab, tab],
        out_specs=[col(0)] * 4, out_shape=[shp] * 4,
        scratch_shapes=[pltpu.VMEM((SEQ, LANES), F32)] * 5,
        compiler_params=_cparams(("parallel", "parallel")),
    )(qkvz, qkvz, qkvz, cosv, sinv)


def attn_bwd(qkvz, cosv, sinv, dmix, mixed, lses, b):
    t = qkvz.shape[0]
    col, tab = _attn_specs(b)
    blocks = _residue_blocks()
    hd = np.arange(LANES) // HEAD_DIM
    head_ones = jnp.asarray((hd[:, None] == hd[None, :]).astype(np.float32))

    def body(q_ref, k_ref, v_ref, c_ref, s_ref, dm_ref, mx_ref, l1_ref, l2_ref, l3_ref, ones_ref,
             dq_out, dk_out, dv_out, qr, kr, do1, do2, do3, dd1, dd2, dd3, dq_ref, dk_ref, dv_ref):
        l_refs, do_scr, dd_scr = (l1_ref, l2_ref, l3_ref), (do1, do2, do3), (dd1, dd2, dd3)
        c, s = c_ref[...], s_ref[...]
        q, k = q_ref[...], k_ref[...]
        qr[...] = q * c + _rot(q) * s
        kr[...] = k * c + _rot(k) * s
        dm = dm_ref[...]
        tot = _dot(dm * mx_ref[...], ones_ref[...], _NN, HI)
        for w, do_g, dd_g in zip(_branch_weights(l1_ref[...], l2_ref[...], l3_ref[...]), do_scr, dd_scr):
            do_g[...] = w * dm
            dd_g[...] = w * tot
        dq_ref[...] = jnp.zeros((SEQ, LANES), F32)
        dk_ref[...] = jnp.zeros((SEQ, LANES), F32)
        dv_ref[...] = jnp.zeros((SEQ, LANES), F32)
        masks = _pair_masks()
        for lo in range(0, len(blocks), ATTN_GROUP):
            chains = []
            for g, cur, prev in blocks[lo:lo + ATTN_GROUP]:
                q2, kcat, vcat, ok = _block_operands(qr, kr, v_ref, cur, prev)
                vcat = vcat.astype(BF16)
                do2_, l2, dd2_ = do_scr[g][cur, :], l_refs[g][cur, :], dd_scr[g][cur, :]
                l2s, dd2s = pltpu.roll(l2, HEAD_DIM, 1), pltpu.roll(dd2_, HEAD_DIM, 1)
                for m in masks:
                    qm = jnp.where(m, q2, 0.0).astype(BF16)
                    dom = jnp.where(m, do2_, 0.0).astype(BF16)
                    lrep, ddrep = jnp.where(m, l2, l2s), jnp.where(m, dd2_, dd2s)
                    if prev is not None:
                        lrep, ddrep = jnp.concatenate([lrep, lrep], axis=1), jnp.concatenate([ddrep, ddrep], axis=1)
                    chains.append(dict(cur=cur, prev=prev, qm=qm, dom=dom, km=jnp.where(m, kcat, 0), lrep=lrep, ddrep=ddrep,
                                       s=jnp.where(ok, _dot(qm, kcat, _NT) * SCALE, NEG), dp=_dot(dom, vcat, _NT)))
            for ch in chains:
                p = jnp.exp(ch['s'] - ch['lrep'])
                ch.update(p=p.astype(BF16), ds=(p * (ch['dp'] - ch['ddrep']) * SCALE).astype(BF16))
            for ch in chains:
                ch.update(dq=_dot(ch['ds'], ch['km'], _NN), dk=_dot(ch['ds'], ch['qm'], _TN), dv=_dot(ch['p'], ch['dom'], _TN))
            for c0, c1 in zip(chains[0::2], chains[1::2]):
                cur, prev = c0['cur'], c0['prev']
                dk, dv = c0['dk'] + c1['dk'], c0['dv'] + c1['dv']
                dq_ref[cur, :] += c0['dq'] + c1['dq']
                if prev is None:
                    dk_ref[cur, :] += dk
                    dv_ref[cur, :] += dv
                else:
                    dk_ref[prev, :] += dk[:ATTN_BLOCK]
                    dv_ref[prev, :] += dv[:ATTN_BLOCK]
                    dk_ref[cur, :] += dk[ATTN_BLOCK:]
                    dv_ref[cur, :] += dv[ATTN_BLOCK:]
        dq, dk = dq_ref[...], dk_ref[...]
        dq_out[...] = (dq * c + _rot(dq * s)).astype(dq_out.dtype)
        dk_out[...] = (dk * c + _rot(dk * s)).astype(dk_out.dtype)
        dv_out[...] = dv_ref[...].astype(dv_out.dtype)

    shp = jax.ShapeDtypeStruct((t, D_ATTN), BF16)
    return pl.pallas_call(
        body, name="attn_bwd", grid=(b, N_HEAD_PAIRS),
        in_specs=[col(0), col(N_HEAD_PAIRS), col(2 * N_HEAD_PAIRS), tab, tab, col(0), col(0), col(0), col(0), col(0),
                  pl.BlockSpec((LANES, LANES), lambda bb, hp: (0, 0))],
        out_specs=[col(0)] * 3, out_shape=[shp] * 3,
        scratch_shapes=[pltpu.VMEM((SEQ, LANES), F32)] * 11,
        compiler_params=_cparams(("parallel", "parallel")),
    )(qkvz, qkvz, qkvz, cosv, sinv, dmix, mixed, *lses, head_ones)


def attn_norm_fwd(mixed, norm_w):
    return rowwise("attn_norm", _rms, [mixed], [norm_w], [(mixed.shape[0], D_ATTN, BF16)])[0]


def attn_norm_bwd(dout, mixed, norm_w):
    def fn(dy, mx, w):
        _, vjp = jax.vjp(_rms, mx, w)
        return vjp(dy)

    return rowwise("attn_norm_bwd", fn, [dout, mixed], [norm_w], [(dout.shape[0], D_ATTN, F32)], accs=[(1, D_ATTN)])


CONV_TM = 256
HALO = 8


def _conv_columns(refs):
    xs_ref, bm_ref, cm_ref = refs
    out = []
    for c in range(D_CONV // LANES):
        lo = c * LANES
        ref, base = (xs_ref, 0) if lo < D_SSD else (bm_ref, D_SSD) if lo < D_SSD + D_BC else (cm_ref, D_SSD + D_BC)
        out.append((slice(lo, lo + LANES), (ref, slice(lo - base, lo - base + LANES))))
    return out


def _conv_taps(scr, w_ref, cs, first_row, step, tm):
    acc = None
    for k in range(CONV_WIDTH):
        term = w_ref[k:k + 1, cs] * scr[pl.ds(first_row + step * k, tm), cs]
        acc = term if acc is None else acc + term
    return acc


def conv_fwd(u, w, bias):
    t = u.shape[0]
    tm, per_seq = CONV_TM, SEQ // CONV_TM

    def body(u_ref, h_ref, w_ref, b_ref, xs_ref, bm_ref, cm_ref, scr):
        first = pl.program_id(0) % per_seq == 0
        scr[0:HALO, :] = jnp.where(first, 0.0, h_ref[...])
        scr[HALO:, :] = u_ref[...]
        for cs, (o_ref, os_) in _conv_columns((xs_ref, bm_ref, cm_ref)):
            o_ref[:, os_] = _silu(_conv_taps(scr, w_ref, cs, HALO - CONV_WIDTH + 1, 1, tm) + b_ref[:, cs])

    return pl.pallas_call(
        body, name="conv_fwd", grid=(t // tm,),
        in_specs=[pl.BlockSpec((tm, D_CONV), lambda i: (i, 0)),
                  pl.BlockSpec((HALO, D_CONV), lambda i: (jnp.maximum(i * (tm // HALO) - 1, 0), 0)),
                  pl.BlockSpec((CONV_WIDTH, D_CONV), lambda i: (0, 0)), pl.BlockSpec((1, D_CONV), lambda i: (0, 0))],
        out_specs=[pl.BlockSpec((tm, D_SSD), lambda i: (i, 0)), pl.BlockSpec((tm, D_BC), lambda i: (i, 0)),
                   pl.BlockSpec((tm, D_BC), lambda i: (i, 0))],
        out_shape=[jax.ShapeDtypeStruct((t, D_SSD), F32), jax.ShapeDtypeStruct((t, D_BC), F32),
                   jax.ShapeDtypeStruct((t, D_BC), F32)],
        scratch_shapes=[pltpu.VMEM((tm + HALO, D_CONV), F32)],
        compiler_params=_cparams(("parallel",)),
    )(u, u, w, bias)


def conv_bwd(u, w, bias, dxs_a, dxs_b, dbm, dcm):
    t = u.shape[0]
    tm, per_seq = CONV_TM, SEQ // CONV_TM
    n_tiles = t // tm

    def body1(u_ref, h_ref, dxs_ref, dxs2_ref, dbm_ref, dcm_ref, w_ref, b_ref, dz_ref, dw_ref, db_ref, scr):
        i = pl.program_id(0)
        first = i % per_seq == 0
        scr[0:HALO, :] = jnp.where(first, 0.0, h_ref[...])
        scr[HALO:, :] = u_ref[...]

        @pl.when(i == 0)
        def _():
            dw_ref[...] = jnp.zeros(dw_ref.shape, F32)
            db_ref[...] = jnp.zeros(db_ref.shape, F32)
        for cs, (g_ref, gs) in _conv_columns((dxs_ref, dbm_ref, dcm_ref)):
            acc = _conv_taps(scr, w_ref, cs, HALO - CONV_WIDTH + 1, 1, tm) + b_ref[:, cs]
            sig = _sigmoid(acc)
            dy = g_ref[:, gs] + dxs2_ref[:, gs] if g_ref is dxs_ref else g_ref[:, gs]
            dz = dy * sig * (1.0 + acc * (1.0 - sig))
            dz_ref[:, cs] = dz
            db_ref[:, cs] += jnp.sum(dz, axis=0, keepdims=True)
            for k in range(CONV_WIDTH):
                dw_ref[k:k + 1, cs] += jnp.sum(dz * scr[pl.ds(HALO - CONV_WIDTH + 1 + k, tm), cs], axis=0, keepdims=True)

    dz, dw, db = pl.pallas_call(
        body1, name="conv_bwd_dz", grid=(n_tiles,),
        in_specs=[pl.BlockSpec((tm, D_CONV), lambda i: (i, 0)),
                  pl.BlockSpec((HALO, D_CONV), lambda i: (jnp.maximum(i * (tm // HALO) - 1, 0), 0)),
                  pl.BlockSpec((tm, D_SSD), lambda i: (i, 0)), pl.BlockSpec((tm, D_SSD), lambda i: (i, 0)),
                  pl.BlockSpec((tm, D_BC), lambda i: (i, 0)), pl.BlockSpec((tm, D_BC), lambda i: (i, 0)),
                  pl.BlockSpec((CONV_WIDTH, D_CONV), lambda i: (0, 0)), pl.BlockSpec((1, D_CONV), lambda i: (0, 0))],
        out_specs=[pl.BlockSpec((tm, D_CONV), lambda i: (i, 0)), pl.BlockSpec((CONV_WIDTH, D_CONV), lambda i: (0, 0)),
                   pl.BlockSpec((1, D_CONV), lambda i: (0, 0))],
        out_shape=[jax.ShapeDtypeStruct((t, D_CONV), F32), jax.ShapeDtypeStruct((CONV_WIDTH, D_CONV), F32),
                   jax.ShapeDtypeStruct((1, D_CONV), F32)],
        scratch_shapes=[pltpu.VMEM((tm + HALO, D_CONV), F32)],
        compiler_params=_cparams(("arbitrary",)),
    )(u, u, dxs_a, dxs_b, dbm, dcm, w, bias)

    def body2(dz_ref, n_ref, w_ref, du_ref, scr):
        last = pl.program_id(0) % per_seq == per_seq - 1
        scr[0:tm, :] = dz_ref[...]
        scr[tm:, :] = jnp.where(last, 0.0, n_ref[...])
        for c in range(D_CONV // LANES):
            cs = slice(c * LANES, (c + 1) * LANES)
            du_ref[:, cs] = _conv_taps(scr, w_ref, cs, CONV_WIDTH - 1, -1, tm).astype(du_ref.dtype)

    du = pl.pallas_call(
        body2, name="conv_bwd_du", grid=(n_tiles,),
        in_specs=[pl.BlockSpec((tm, D_CONV), lambda i: (i, 0)),
                  pl.BlockSpec((HALO, D_CONV), lambda i: (jnp.minimum((i + 1) * (tm // HALO), t // HALO - 1), 0)),
                  pl.BlockSpec((CONV_WIDTH, D_CONV), lambda i: (0, 0))],
        out_specs=pl.BlockSpec((tm, D_CONV), lambda i: (i, 0)),
        out_shape=jax.ShapeDtypeStruct((t, D_CONV), BF16),
        scratch_shapes=[pltpu.VMEM((tm + HALO, D_CONV), F32)],
        compiler_params=_cparams(("parallel",)),
    )(dz, dz, w)
    return du, dw, db


Q = SSD_CHUNK
N_PAIRS = D_SSD // LANES
HEADS_PER_GROUP = N_HEADS // SSD_GROUPS


def _rep(a, j):
    return jnp.broadcast_to(a[:, j:j + 1], a.shape)


def _dot_exact01(a, b, dn, a_is_01):
    x = b if a_is_01 else a
    hi = x.astype(BF16)
    mid = (x - hi.astype(F32)).astype(BF16)
    lo = (x - hi.astype(F32) - mid.astype(F32)).astype(BF16)
    z = a.astype(BF16) if a_is_01 else b.astype(BF16)
    out = None
    for term in (hi, mid, lo):
        d = _dot(z, term, dn) if a_is_01 else _dot(term, z, dn)
        out = d if out is None else out + d
    return out


def _pad_lanes(v, fill=0.0):
    row = jnp.pad(v.reshape(1, -1).astype(F32), ((0, 0), (0, LANES - v.size)), constant_values=fill)
    return row, row.reshape(LANES, 1)


def _ssd_common(dtr_ref, dtrt_ref, bias_r, bias_c, alog_r, alog_c):
    row = lax.broadcasted_iota(jnp.int32, (Q, Q), 0)
    col = lax.broadcasted_iota(jnp.int32, (Q, Q), 1)
    tril = row >= col
    lane = lax.broadcasted_iota(jnp.int32, (1, LANES), 1)
    a_r = jnp.where(lane < N_HEADS, -jnp.exp(alog_r[...]), 0.0)
    sub = lax.broadcasted_iota(jnp.int32, (LANES, 1), 0)
    a_c = jnp.where(sub < N_HEADS, -jnp.exp(alog_c[...]), 0.0)
    dt = _softplus(dtr_ref[...] + bias_r[...])
    cs = _dot_exact01(tril, dt * a_r, _NN, True)
    dtt = _softplus(dtrt_ref[...] + bias_c[...])
    cst = _dot_exact01(dtt * a_c, row <= col, _NN, False)
    return tril, lane, a_r, dt, cs, cst


def _ssd_specs(b, nc, rev):
    ci = (lambda c: nc - 1 - c) if rev else (lambda c: c)
    rows = lambda w: pl.BlockSpec((Q, w), lambda bb, c: (bb * nc + ci(c), 0))
    dtt = pl.BlockSpec((LANES, Q), lambda bb, c: (0, bb * nc + ci(c)))
    const = lambda s: pl.BlockSpec(s, lambda bb, c: (0,) * len(s))
    state = pl.BlockSpec((None, N_PAIRS, LANES, SSD_STATE), lambda bb, c: (bb * nc + ci(c), 0, 0, 0))
    return rows, dtt, const, state


def ssd_fwd(xs, bm, cm, dtraw, dt_bias, a_log, b):
    t = xs.shape[0]
    nc = SEQ // Q
    rows, dtt_spec, const, state = _ssd_specs(b, nc, False)
    bias_r, bias_c = _pad_lanes(dt_bias)
    alog_r, alog_c = _pad_lanes(a_log)

    def body(xs_ref, b_ref, c_ref, dtr_ref, dtrt_ref, br, bc, ar, ac, y_ref, hp_ref, h_scr):
        @pl.when(pl.program_id(1) == 0)
        def _():
            h_scr[...] = jnp.zeros(h_scr.shape, F32)
        tril, lane, _, dt, cs, cst = _ssd_common(dtr_ref, dtrt_ref, br, bc, ar, ac)
        sub = lax.broadcasted_iota(jnp.int32, (LANES, 1), 0)
        y_acc = [jnp.zeros((Q, LANES), F32) for _ in range(N_PAIRS)]
        h_old = [h_scr[p] for p in range(N_PAIRS)]
        h_new = [jnp.zeros((LANES, SSD_STATE), F32) for _ in range(N_PAIRS)]
        for g in range(SSD_GROUPS):
            bg = b_ref[:, g * SSD_STATE:(g + 1) * SSD_STATE].astype(BF16)
            cg = c_ref[:, g * SSD_STATE:(g + 1) * SSD_STATE].astype(BF16)
            cb = _dot(cg, bg, _NT)
            for j in range(g * HEADS_PER_GROUP, (g + 1) * HEADS_PER_GROUP):
                p, side = j // 2, j % 2
                m = (lane < HEAD_DIM) if side == 0 else (lane >= HEAD_DIM)
                ms = (sub < HEAD_DIM) if side == 0 else (sub >= HEAD_DIM)
                csj, dtj = _rep(cs, j), _rep(dt, j)
                lmat = jnp.exp(jnp.where(tril, csj - cst[j:j + 1, :], NEG))
                xdt = jnp.where(m, xs_ref[:, p * LANES:(p + 1) * LANES] * dtj, 0.0)
                hm = jnp.where(ms, h_old[p], 0.0)
                ydiag = _dot((cb * lmat).astype(BF16), xdt.astype(BF16), _NN)
                yoff = jnp.exp(csj) * _dot(cg, hm.astype(BF16), _NT)
                y_acc[p] = y_acc[p] + ydiag + yoff
                last = csj[Q - 1:Q, :]
                sj = _dot((xdt * jnp.exp(last - csj)).astype(BF16), bg, _TN)
                h_new[p] = h_new[p] + jnp.exp(last) * hm + sj
        for p in range(N_PAIRS):
            y_ref[:, p * LANES:(p + 1) * LANES] = y_acc[p]
            hp_ref[p] = h_old[p]
            h_scr[p] = h_new[p]

    return pl.pallas_call(
        body, name="ssd_fwd", grid=(b, nc),
        in_specs=[rows(D_SSD), rows(D_BC), rows(D_BC), rows(LANES), dtt_spec, const((1, LANES)), const((LANES, 1)),
                  const((1, LANES)), const((LANES, 1))],
        out_specs=[rows(D_SSD), state],
        out_shape=[jax.ShapeDtypeStruct((t, D_SSD), F32),
                   jax.ShapeDtypeStruct((b * nc, N_PAIRS, LANES, SSD_STATE), F32)],
        scratch_shapes=[pltpu.VMEM((N_PAIRS, LANES, SSD_STATE), F32)],
        compiler_params=_cparams(("parallel", "arbitrary")),
    )(xs, bm, cm, dtraw, dtraw.T, bias_r, bias_c, alog_r, alog_c)


def ssd_bwd(xs, bm, cm, dtraw, dt_bias, a_log, hprev, dy, b):
    t = xs.shape[0]
    nc = SEQ // Q
    rows, dtt_spec, const, state = _ssd_specs(b, nc, True)
    bias_r, bias_c = _pad_lanes(dt_bias)
    alog_r, alog_c = _pad_lanes(a_log)

    def body(xs_ref, b_ref, c_ref, dtr_ref, dtrt_ref, hp_ref, dy_ref, br, bc, ar, ac,
             dxs_ref, db_ref, dc_ref, ddt_ref, dbias_ref, dalog_ref, dh_scr):
        first = jnp.logical_and(pl.program_id(0) == 0, pl.program_id(1) == 0)

        @pl.when(pl.program_id(1) == 0)
        def _():
            dh_scr[...] = jnp.zeros(dh_scr.shape, F32)

        @pl.when(first)
        def _():
            dbias_ref[...] = jnp.zeros(dbias_ref.shape, F32)
            dalog_ref[...] = jnp.zeros(dalog_ref.shape, F32)
        tril, lane, a_r, dt, cs, cst = _ssd_common(dtr_ref, dtrt_ref, br, bc, ar, ac)
        sub = lax.broadcasted_iota(jnp.int32, (LANES, 1), 0)
        rowq = lax.broadcasted_iota(jnp.int32, (Q, 1), 0)
        triu = (lax.broadcasted_iota(jnp.int32, (Q, Q), 0) <= lax.broadcasted_iota(jnp.int32, (Q, Q), 1)).astype(F32)
        dxs_acc = [jnp.zeros((Q, LANES), F32) for _ in range(N_PAIRS)]
        dh_in = [dh_scr[p] for p in range(N_PAIRS)]
        h_in = [hp_ref[p] for p in range(N_PAIRS)]
        dh_out = [jnp.zeros((LANES, SSD_STATE), F32) for _ in range(N_PAIRS)]
        ddt = jnp.zeros((Q, LANES), F32)
        dalog = jnp.zeros((1, LANES), F32)
        for g in range(SSD_GROUPS):
            gs = slice(g * SSD_STATE, (g + 1) * SSD_STATE)
            bg, cg = b_ref[:, gs].astype(BF16), c_ref[:, gs].astype(BF16)
            cb = _dot(cg, bg, _NT)
            dcb = jnp.zeros((Q, Q), F32)
            dbg = jnp.zeros((Q, SSD_STATE), F32)
            dcg = jnp.zeros((Q, SSD_STATE), F32)
            for j in range(g * HEADS_PER_GROUP, (g + 1) * HEADS_PER_GROUP):
                p, side = j // 2, j % 2
                m = (lane < HEAD_DIM) if side == 0 else (lane >= HEAD_DIM)
                ms = (sub < HEAD_DIM) if side == 0 else (sub >= HEAD_DIM)
                csj, dtj = _rep(cs, j), _rep(dt, j)
                lmat = jnp.exp(jnp.where(tril, csj - cst[j:j + 1, :], NEG))
                x2 = jnp.where(m, xs_ref[:, p * LANES:(p + 1) * LANES], 0.0)
                xdt = x2 * dtj
                dym = jnp.where(m, dy_ref[:, p * LANES:(p + 1) * LANES], 0.0)
                hm = jnp.where(ms, h_in[p], 0.0)
                dhm = jnp.where(ms, dh_in[p], 0.0)
                ecs = jnp.exp(csj)
                last = csj[Q - 1:Q, :]
                decay = jnp.exp(last - csj)
                el = jnp.exp(last)
                gmat = cb * lmat
                dymb, xdtb = dym.astype(BF16), xdt.astype(BF16)
                dg = _dot(dymb, xdtb, _NT)
                dxdt = _dot(gmat.astype(BF16), dymb, _TN)
                dcb = dcb + dg * lmat
                ej = dg * gmat
                col_sums = jnp.broadcast_to(jnp.sum(ej, axis=0, keepdims=True), (Q, Q)).T
                dcs = jnp.sum(ej, axis=1, keepdims=True) - col_sums
                ch = _dot(cg, hm.astype(BF16), _NT)
                dye = dym * ecs
                dcs = dcs + jnp.sum(dye * ch, axis=1, keepdims=True)
                dcg = dcg + _dot(dye.astype(BF16), hm.astype(BF16), _NN)
                dhp = _dot(dye.astype(BF16), cg, _TN)
                wmat = _dot(bg, dhm.astype(BF16), _NT)
                xd = xdt * decay
                dxdt = dxdt + decay * wmat
                ddl = jnp.sum(xd * wmat, axis=1, keepdims=True)
                dlast = jnp.sum(ddl, axis=0, keepdims=True) + el * jnp.sum(jnp.sum(dhm * hm, axis=1, keepdims=True), axis=0, keepdims=True)
                dcs = dcs - ddl + jnp.where(rowq == Q - 1, dlast, 0.0)
                dbg = dbg + _dot(xd.astype(BF16), dhm.astype(BF16), _NN)
                dh_out[p] = dh_out[p] + el * dhm + dhp
                da = _dot_exact01(triu, dcs, _NN, True)
                aj = jnp.sum(jnp.where(lane == j, a_r, 0.0), axis=1, keepdims=True)
                ddtj = da * aj + jnp.sum(dxdt * x2, axis=1, keepdims=True)
                ddt = ddt + jnp.where(lane == j, ddtj, 0.0)
                dalog = dalog + jnp.where(lane == j, jnp.sum(da * dtj, axis=0, keepdims=True) * aj, 0.0)
                dxs_acc[p] = dxs_acc[p] + dxdt * dtj
            dcbb = dcb.astype(BF16)
            dc_ref[:, gs] = dcg + _dot(dcbb, bg, _NN)
            db_ref[:, gs] = dbg + _dot(dcbb, cg, _TN)
        for p in range(N_PAIRS):
            dxs_ref[:, p * LANES:(p + 1) * LANES] = dxs_acc[p]
            dh_scr[p] = dh_out[p]
        ddtraw = ddt * _sigmoid(dtr_ref[...] + br[...])
        ddt_ref[...] = ddtraw
        dbias_ref[...] += jnp.sum(ddtraw, axis=0, keepdims=True)
        dalog_ref[...] += dalog

    return pl.pallas_call(
        body, name="ssd_bwd", grid=(b, nc),
        in_specs=[rows(D_SSD), rows(D_BC), rows(D_BC), rows(LANES), dtt_spec, state, rows(D_SSD), const((1, LANES)),
                  const((LANES, 1)), const((1, LANES)), const((LANES, 1))],
        out_specs=[rows(D_SSD), rows(D_BC), rows(D_BC), rows(LANES), const((1, LANES)), const((1, LANES))],
        out_shape=[jax.ShapeDtypeStruct((t, D_SSD), F32), jax.ShapeDtypeStruct((t, D_BC), F32),
                   jax.ShapeDtypeStruct((t, D_BC), F32), jax.ShapeDtypeStruct((t, LANES), F32),
                   jax.ShapeDtypeStruct((1, LANES), F32), jax.ShapeDtypeStruct((1, LANES), F32)],
        scratch_shapes=[pltpu.VMEM((N_PAIRS, LANES, SSD_STATE), F32)],
        compiler_params=_cparams(("arbitrary", "arbitrary")),
    )(xs, bm, cm, dtraw, dtraw.T, hprev, dy, bias_r, bias_c, alog_r, alog_c)


def _split_w_in(w_in):
    w_dt = jnp.pad(w_in[:, D_QKVZ + D_CONV:], ((0, 0), (0, LANES - N_HEADS)))
    return w_in[:, :D_QKVZ], w_in[:, D_QKVZ:D_QKVZ + D_CONV], w_dt


def mixer_fwd(hb, p, cosv, sinv, b):
    t = hb.shape[0]
    w_a, w_b, w_c = _split_w_in(p['w_in'])
    qkvz = mm("in_qkvz", [(hb, w_a, 'nn')], D_QKVZ)
    xbc = mm("in_xbc", [(hb, w_b, 'nn')], D_CONV)
    dtraw = mm("in_dt", [(hb, w_c, 'nn')], LANES)
    mixed, *lses = attn_fwd(qkvz, cosv, sinv, b)
    attn = attn_norm_fwd(mixed, p['attn_norm_w'])
    xs, bm, cm = conv_fwd(xbc, p['conv_w'], p['conv_b'])
    y, hprev = ssd_fwd(xs, bm, cm, dtraw, p['dt_bias'], p['a_log'], b)
    dskip = jnp.repeat(p['d_skip'].reshape(-1), HEAD_DIM).reshape(1, D_SSD)
    yg, = rowwise("ssd_gate", _gate, [y, xs, Op(qkvz, D_SSD, 3)], [dskip, p['ssd_norm_w']], [(t, D_SSD, BF16)])
    mix = mm("out_proj", [(attn, p['w_out'][:D_ATTN], 'nn'), (yg, p['w_out'][D_ATTN:], 'nn')], D_MODEL)
    res = dict(hb=hb, qkvz=qkvz, xbc=xbc, dtraw=dtraw, mixed=mixed, lses=lses, attn=attn, xs=xs, bm=bm, cm=cm,
               y=y, hprev=hprev, dskip=dskip, yg=yg, cosv=cosv, sinv=sinv)
    return mix, res


def mixer_bwd(r, p, dmix, dh_resid, b):
    t = dmix.shape[0]
    w_a, w_b, w_c = _split_w_in(p['w_in'])
    w_out = p['w_out']
    dattn = mm("out_bwd_dattn", [(dmix, w_out[:D_ATTN], 'nt')], D_ATTN)
    dyg = mm("out_bwd_dyg", [(dmix, w_out[D_ATTN:], 'nt')], D_SSD)
    dw_out = jnp.concatenate([mm_tn("out_bwd_dw_a", r['attn'], dmix, BF16),
                              mm_tn("out_bwd_dw_y", r['yg'], dmix, BF16)], axis=0)

    def gate_bwd(dy_, y_, xs_, z_, ds_, w_):
        _, vjp = jax.vjp(_gate, y_, xs_, z_, ds_, w_)
        return vjp(dy_)

    dy, dxs_a, dz, ddskip, dssd_norm = rowwise(
        "ssd_gate_bwd", gate_bwd, [dyg, r['y'], r['xs'], Op(r['qkvz'], D_SSD, 3)], [r['dskip'], p['ssd_norm_w']],
        [(t, D_SSD, F32), (t, D_SSD, F32), (t, D_SSD, BF16)], accs=[(1, D_SSD), (1, D_SSD)])
    dxs_b, dbm, dcm, ddtraw, ddt_bias, da_log = ssd_bwd(r['xs'], r['bm'], r['cm'], r['dtraw'], p['dt_bias'], p['a_log'],
                                                        r['hprev'], dy, b)
    dxbc, dconv_w, dconv_b = conv_bwd(r['xbc'], p['conv_w'], p['conv_b'], dxs_a, dxs_b, dbm, dcm)
    dmixed, dattn_norm = attn_norm_bwd(dattn, r['mixed'], p['attn_norm_w'])
    dq, dk, dv = attn_bwd(r['qkvz'], r['cosv'], r['sinv'], dmixed, r['mixed'], r['lses'], b)
    wq, wk, wv, wz = (w_a[:, i * D_ATTN:(i + 1) * D_ATTN] for i in range(4))
    dh = mm("in_bwd_dh", [(dq, wq, 'nt'), (dk, wk, 'nt'), (dv, wv, 'nt'), (dz, wz, 'nt'), (dxbc, w_b, 'nt'),
                          (ddtraw, w_c, 'nt')], D_MODEL, add=dh_resid, tn=512)
    h = r['hb']
    dw_in = jnp.concatenate([mm_tn("in_bwd_dwq", h, dq, BF16), mm_tn("in_bwd_dwk", h, dk, BF16),
                             mm_tn("in_bwd_dwv", h, dv, BF16), mm_tn("in_bwd_dwz", h, dz, BF16),
                             mm_tn("in_bwd_dwx", h, dxbc, BF16), mm_tn("in_bwd_dwdt", h, ddtraw, BF16)[:, :N_HEADS]], axis=1)
    head_sum = lambda v: v.reshape(N_HEADS, HEAD_DIM).sum(axis=1).reshape(1, N_HEADS)
    grads = dict(w_in=dw_in, w_out=dw_out, conv_w=dconv_w, conv_b=dconv_b, dt_bias=ddt_bias[:, :N_HEADS],
                 a_log=da_log[:, :N_HEADS], d_skip=head_sum(ddskip), attn_norm_w=dattn_norm, ssd_norm_w=dssd_norm)
    return dh, grads


FFN1_KEYS = ('ffn1_gate', 'ffn1_up', 'ffn1_down')
FFN2_KEYS = ('ffn2_gate', 'ffn2_up', 'ffn2_down')
MIXER_KEYS = ('w_in', 'conv_w', 'w_out')
FFN_COL = ('ffn1_gate', 'ffn1_up', 'ffn2_gate', 'ffn2_up')
FFN_ROW = ('ffn1_down', 'ffn2_down')
CONV_W_COMM = (8, 2 * LANES)
SMALL = 'small'


def comm_shape(k, shapes):
    if k in FFN_COL:
        return (D_MODEL, FF_PAD)
    if k in FFN_ROW:
        return (FF_PAD, D_MODEL)
    if k == 'conv_w':
        return CONV_W_COMM
    return tuple(shapes[k][1:])


def to_comm(k, vals, shapes):
    a = vals[k].reshape(shapes[k][1:])
    r_, c_ = comm_shape(k, shapes)
    return jnp.pad(a, ((0, r_ - a.shape[0]), (0, c_ - a.shape[1])))


SMALL_ROWS, SMALL_COLS = 16, D_CONV


def pack_small(small):
    rows = [jnp.pad(small[r].reshape(1, -1), ((0, 0), (0, SMALL_COLS - small[r].size))) for r in REPLICATED]
    return jnp.concatenate(rows + [jnp.zeros((SMALL_ROWS - len(rows), SMALL_COLS), F32)], axis=0)


def full_weight(k, g):
    if k in FFN_COL:
        return g
    if k == 'conv_w':
        return jnp.transpose(g[:, :CONV_WIDTH, :D_CONV // N_DEV], (1, 0, 2)).reshape(CONV_WIDTH, D_CONV)
    return g.reshape(N_DEV * g.shape[1], g.shape[2])


def grad_shards(k, g):
    if k in FFN_COL:
        return g
    if k == 'conv_w':
        s = jnp.transpose(g.reshape(CONV_WIDTH, N_DEV, D_CONV // N_DEV), (1, 0, 2))
        return jnp.pad(s, ((0, 0), (0, CONV_W_COMM[0] - CONV_WIDTH), (0, CONV_W_COMM[1] - D_CONV // N_DEV)))
    return g.reshape(N_DEV, g.shape[0] // N_DEV, g.shape[1])


def _flip(v, bit):
    return 1 - v if bit else v


N_PEER_COPIES = N_DEV - 1


def _comm_call(name, body, arrs, out_shape):
    n = len(arrs)
    return pl.pallas_call(
        functools.partial(body, n), name=name, out_shape=out_shape,
        in_specs=[pl.BlockSpec(memory_space=pl.ANY)] * n, out_specs=[pl.BlockSpec(memory_space=pl.ANY)] * n,
        scratch_shapes=[pltpu.SemaphoreType.DMA((n * N_PEER_COPIES,)), pltpu.SemaphoreType.DMA((n * N_PEER_COPIES,)),
                        pltpu.SemaphoreType.DMA((n,))],
    )(*arrs)


def _blk(ref, idx, by_cols):
    if not by_cols:
        return ref.at[idx]
    c = ref.shape[1] // N_DEV
    return ref.at[:, pl.ds(pl.multiple_of(idx * c, LANES), c)]


def _blocked_shape(a, by_cols):
    return (a.shape[0], N_DEV * a.shape[1]) if by_cols else (N_DEV,) + a.shape


def all_gather(arrs, by_cols):
    def body(n, *refs):
        x_refs, out_refs, (send_sems, recv_sems, local_sems) = refs[:n], refs[n:2 * n], refs[2 * n:]
        x, y, c = lax.axis_index("x"), lax.axis_index("y"), lax.axis_index("c")
        me, sibling = (x, y, c), (x, y, 1 - c)
        chips = [(1 - x, y), (x, 1 - y), (1 - x, 1 - y)]

        def copy(a, k, block, to, src=None):
            px, py, pc = block
            dst = _blk(out_refs[a], 4 * px + 2 * py + pc, by_cols[a])
            return pltpu.make_async_remote_copy(
                src_ref=dst if src is None else src, dst_ref=dst, send_sem=send_sems.at[a * N_PEER_COPIES + k],
                recv_sem=recv_sems.at[a * N_PEER_COPIES + k], device_id=to, device_id_type=MESH)

        mine = [pltpu.make_async_copy(x_refs[a], _blk(out_refs[a], 4 * x + 2 * y + c, by_cols[a]), local_sems.at[a])
                for a in range(n)]
        started = []
        for a in range(n):
            mine[a].start()
            first = [copy(a, 0, me, sibling, src=x_refs[a])]
            first += [copy(a, 1 + j, me, (*chip, c), src=x_refs[a]) for j, chip in enumerate(chips)]
            for cp in first:
                cp.start()
            started += first
        for j, chip in enumerate(chips):
            for a in range(n):
                copy(a, 1 + j, (*chip, c), me).wait_recv()
                cp = copy(a, 4 + j, (*chip, c), sibling)
                cp.start()
                started.append(cp)
        for a in range(n):
            copy(a, 0, sibling, me).wait_recv()
            for j, chip in enumerate(chips):
                copy(a, 4 + j, (*chip, 1 - c), me).wait_recv()
        for cp in started:
            cp.wait_send()
        for cp in mine:
            cp.wait()

    return _comm_call("all_gather_weights", body, arrs,
                      [jax.ShapeDtypeStruct(_blocked_shape(a, bc), a.dtype) for a, bc in zip(arrs, by_cols)])


def _landing_shape(a, by_cols):
    return (N_DEV, a.shape[0], a.shape[1] // N_DEV) if by_cols else a.shape


def all_to_all(arrs, by_cols):
    def body(n, *refs):
        s_refs, r_refs, (send_sems, recv_sems, local_sems) = refs[:n], refs[n:2 * n], refs[2 * n:]
        x, y, c = lax.axis_index("x"), lax.axis_index("y"), lax.axis_index("c")
        me = 4 * x + 2 * y + c

        def peer(k):
            return _flip(x, k & 4), _flip(y, k & 2), _flip(c, k & 1)

        def copy(a, k, landing):
            px, py, pc = peer(k)
            p = 4 * px + 2 * py + pc
            src, dst = (me, p) if landing else (p, me)
            return pltpu.make_async_remote_copy(
                src_ref=_blk(s_refs[a], src, by_cols[a]), dst_ref=r_refs[a].at[dst],
                send_sem=send_sems.at[a * N_PEER_COPIES + k - 1],
                recv_sem=recv_sems.at[a * N_PEER_COPIES + k - 1], device_id=(px, py, pc), device_id_type=MESH)

        mine = [pltpu.make_async_copy(_blk(s_refs[a], me, by_cols[a]), r_refs[a].at[me], local_sems.at[a]) for a in range(n)]
        sends = [copy(a, k, False) for a in range(n) for k in range(1, N_DEV)]
        for cp in mine + sends:
            cp.start()
        for a in range(n):
            for k in range(1, N_DEV):
                copy(a, k, True).wait_recv()
        for cp in sends:
            cp.wait_send()
        for cp in mine:
            cp.wait()

    return _comm_call("all_to_all_grads", body, arrs,
                      [jax.ShapeDtypeStruct(_landing_shape(a, bc), a.dtype) for a, bc in zip(arrs, by_cols)])


_HBM = pl.BlockSpec(memory_space=pltpu.HBM)
_SEM = pl.BlockSpec(memory_space=pltpu.SEMAPHORE)
_EFFECT = pltpu.SideEffectType.DATAFLOW_SIDE_EFFECTING


def _peer(k):
    x, y, c = lax.axis_index("x"), lax.axis_index("y"), lax.axis_index("c")
    return _flip(x, k & 4), _flip(y, k & 2), _flip(c, k & 1)


def _my_index():
    return 4 * lax.axis_index("x") + 2 * lax.axis_index("y") + lax.axis_index("c")


def _split_copies(mode, by_cols, src_refs, land_refs, send_sems, recv_sems):
    me = _my_index()
    out = []
    for a, bc in enumerate(by_cols):
        for k in range(1, N_DEV):
            px, py, pc = _peer(k)
            src = _blk(src_refs[a], 4 * px + 2 * py + pc, bc) if mode == 'scatter' else src_refs[a]
            dst = land_refs[a].at[me] if mode == 'scatter' else _blk(land_refs[a], me, bc)
            out.append(pltpu.make_async_remote_copy(
                src_ref=src, dst_ref=dst, send_sem=send_sems.at[a * N_PEER_COPIES + k - 1],
                recv_sem=recv_sems.at[a * N_PEER_COPIES + k - 1], device_id=(px, py, pc), device_id_type=MESH))
    return out


def exchange_start(name, mode, srcs, by_cols):
    n = len(srcs)
    lands = [lax.empty(_landing_shape(s, bc) if mode == 'scatter' else _blocked_shape(s, bc), s.dtype)
             for s, bc in zip(srcs, by_cols)]

    def body(*refs):
        src_refs, land_refs, send_sems, recv_sems = refs[:n], refs[n:2 * n], refs[2 * n], refs[2 * n + 1]
        for cp in _split_copies(mode, by_cols, src_refs, land_refs, send_sems, recv_sems):
            cp.start()
        refs[-1][...] = jnp.zeros(refs[-1].shape, F32)

    sems = pltpu.SemaphoreType.DMA((n * N_PEER_COPIES,))
    res = pl.pallas_call(
        body, name=name,
        out_shape=(sems, sems, *[pltpu.HBM(a.shape, a.dtype) for a in srcs + lands], jax.ShapeDtypeStruct((8, LANES), F32)),
        in_specs=(_HBM,) * (2 * n), out_specs=(_SEM, _SEM, *(_HBM,) * (2 * n), pl.BlockSpec(memory_space=pltpu.VMEM)),
        input_output_aliases={i: 2 + i for i in range(2 * n)},
        compiler_params=pltpu.CompilerParams(has_side_effects=_EFFECT),
    )(*[pltpu.with_memory_space_constraint(a, pltpu.HBM) for a in srcs + lands])
    return (mode, by_cols, res[:-1]), res[-1]


def exchange_wait(name, handles, after):
    mode, by_cols, (send_sems, recv_sems, *bufs) = handles
    n = len(by_cols)

    def body(*refs):
        src_refs, land_refs, s_sems, r_sems = refs[:n], refs[n:2 * n], refs[2 * n], refs[2 * n + 1]
        for cp in _split_copies(mode, by_cols, src_refs, land_refs, s_sems, r_sems):
            cp.wait_send()
            cp.wait_recv()

    res = pl.pallas_call(
        body, name=name, out_shape=tuple(pltpu.HBM(a.shape, a.dtype) for a in bufs),
        in_specs=(*(_HBM,) * (2 * n), _SEM, _SEM, pl.BlockSpec(memory_space=pl.ANY)), out_specs=(_HBM,) * (2 * n),
        input_output_aliases={i: i for i in range(2 * n)},
        compiler_params=pltpu.CompilerParams(has_side_effects=_EFFECT),
    )(*bufs, send_sems, recv_sems, after)
    me, out = _my_index(), []
    for src, land, bc in zip(res[:n], res[n:], by_cols):
        if mode == 'scatter':
            c = land.shape[2]
            own = lax.dynamic_slice(src, (0, me * c), (src.shape[0], c)) if bc else lax.dynamic_index_in_dim(src, me, 0, False)
            out.append(lax.dynamic_update_slice(land, own[None], (me, 0, 0)))
        elif bc:
            out.append(lax.dynamic_update_slice(land, src, (0, me * src.shape[1])))
        else:
            out.append(lax.dynamic_update_slice(land, src[None], (me, 0, 0)))
    return out


def _adamw_math(g, w, m, v):
    c1 = 1.0 / (1.0 - ADAM_B1 ** ADAM_STEP)
    c2 = 1.0 / (1.0 - ADAM_B2 ** ADAM_STEP)
    m = ADAM_B1 * m + (1.0 - ADAM_B1) * g
    v = ADAM_B2 * v + (1.0 - ADAM_B2) * jnp.square(g)
    return g, -ADAM_LR * ((m * c1) / (jnp.sqrt(v * c2) + ADAM_EPS) + ADAM_WD * w), m, v


def adamw(name, recv, w, m, v, tm):
    rows, cols = w.shape
    tm = min(tm, rows)

    def body(*refs):
        g = refs[0][0:tm, 0:cols].astype(F32)
        for s in range(1, N_DEV):
            g = g + refs[s][0:tm, 0:cols].astype(F32)
        res = _adamw_math(g, *[r[...] for r in refs[N_DEV:N_DEV + 3]])
        for r, val in zip(refs[N_DEV + 3:], res):
            r[...] = val

    part = lambda s: pl.BlockSpec((None, recv.shape[1] if tm == rows else tm, recv.shape[2]), lambda i: (s, i, 0))
    tile = pl.BlockSpec((tm, cols), lambda i: (i, 0))
    return pl.pallas_call(
        body, name=name, grid=(rows // tm,), in_specs=[part(s) for s in range(N_DEV)] + [tile] * 3, out_specs=[tile] * 4,
        out_shape=[jax.ShapeDtypeStruct((rows, cols), F32)] * 4, compiler_params=_cparams(("parallel",)),
    )(*[recv] * N_DEV, w, m, v)


def adamw_small(recv, wl, ml, vl):
    n = len(REPLICATED)

    def body(recv_ref, *refs):
        g = recv_ref[0]
        for s in range(1, N_DEV):
            g = g + recv_ref[s]
        for r in range(n):
            w, m, v = (refs[j * n + r][...] for j in range(3))
            for j, val in enumerate(_adamw_math(g[r:r + 1, :w.shape[1]], w, m, v)):
                refs[(3 + j) * n + r][...] = val

    arrs = [d[k].reshape(1, -1) for d in (wl, ml, vl) for k in REPLICATED]
    res = pl.pallas_call(
        body, name="adamw_small", out_shape=[jax.ShapeDtypeStruct(a.shape, F32) for a in arrs[:n]] * 4,
    )(recv, *arrs)
    return [{k: res[j * n + r].reshape(wl[k].shape) for r, k in enumerate(REPLICATED)} for j in range(4)]


ADAMW_TM = {'ffn1_gate': 256, 'ffn1_up': 256, 'ffn2_gate': 256, 'ffn2_up': 256, 'w_in': 32}


def kernel(x, positions, ln1_g, ln1_b, ffn1_gate, ffn1_up, ffn1_down, w_in, conv_w, conv_b, dt_bias, a_log, d_skip, attn_norm_w, ssd_norm_w, w_out, ln2_g, ln2_b, ffn2_gate, ffn2_up, ffn2_down, ln3_g, ln3_b, loss_target, m_ln1_g, m_ln1_b, m_ffn1_gate, m_ffn1_up, m_ffn1_down, m_w_in, m_conv_w, m_conv_b, m_dt_bias, m_a_log, m_d_skip, m_attn_norm_w, m_ssd_norm_w, m_w_out, m_ln2_g, m_ln2_b, m_ffn2_gate, m_ffn2_up, m_ffn2_down, m_ln3_g, m_ln3_b, v_ln1_g, v_ln1_b, v_ffn1_gate, v_ffn1_up, v_ffn1_down, v_w_in, v_conv_w, v_conv_b, v_dt_bias, v_a_log, v_d_skip, v_attn_norm_w, v_ssd_norm_w, v_w_out, v_ln2_g, v_ln2_b, v_ffn2_gate, v_ffn2_up, v_ffn2_down, v_ln3_g, v_ln3_b):
    args = dict(locals())
    wl = {k: args[k] for k in WEIGHTS}
    ml = {k: args["m_" + k] for k in WEIGHTS}
    vl = {k: args["v_" + k] for k in WEIGHTS}
    shapes = {k: wl[k].shape for k in WEIGHTS}
    b, s, dm = x.shape
    t = b * s

    sent = {k: to_comm(k, wl, shapes).astype(F32 if k == 'conv_w' else BF16) for k in SHARDED}
    by_cols = lambda keys: [k in FFN_COL for k in keys]
    p = {k: full_weight(k, g) for k, g in zip(FFN1_KEYS, all_gather([sent[k] for k in FFN1_KEYS], by_cols(FFN1_KEYS)))}
    gather_mixer, token_m = exchange_start("gather_mixer_start", 'gather', [sent[k] for k in MIXER_KEYS], by_cols(MIXER_KEYS))
    sent['ffn2_gate'] = sent['ffn2_gate'] + token_m[0, 0].astype(BF16)
    gather_ffn2, token_f = exchange_start("gather_ffn2_start", 'gather', [sent[k] for k in FFN2_KEYS], by_cols(FFN2_KEYS))
    for k in REPLICATED:
        p[k] = wl[k].reshape(1, -1)

    x2 = x.reshape(t, dm)
    cosv, sinv = rope_tables(positions)
    f1, res1 = ffn_fwd("ffn1", x2, p['ffn1_gate'], p['ffn1_up'], p['ffn1_down'], after=(token_m, token_f))
    h1, h1b = resid_ln_fwd("ln1", 0.5, x2, f1, p['ln1_g'], p['ln1_b'])
    for k, g in zip(MIXER_KEYS, exchange_wait("gather_mixer_wait", gather_mixer, h1b)):
        p[k] = full_weight(k, g)
    mix, resm = mixer_fwd(h1b, p, cosv, sinv, b)
    h2, h2b = resid_ln_fwd("ln2", 1.0, h1, mix, p['ln2_g'], p['ln2_b'])
    for k, g in zip(FFN2_KEYS, exchange_wait("gather_ffn2_wait", gather_ffn2, h2b)):
        p[k] = full_weight(k, g)
    f2, res3 = ffn_fwd("ffn2", h2b, p['ffn2_gate'], p['ffn2_up'], p['ffn2_down'])

    small, full = {}, {}
    dh2_res, df2, small['ln3_g'], small['ln3_b'], sq = ln_loss_bwd("ln3_loss_bwd", h2, f2, loss_target.reshape(t, dm),
                                                                   p['ln3_g'], p['ln3_b'])
    loss = lax.psum(jnp.sum(sq) * (0.5 / dm), AXES)

    dh2, full['ffn2_gate'], full['ffn2_up'], full['ffn2_down'] = ffn_bwd("ffn2", res3, p['ffn2_gate'], p['ffn2_up'],
                                                                       p['ffn2_down'], df2, dh2_res)
    ffn2_exchange, token = exchange_start("grads_ffn2_start", 'scatter', [grad_shards(k, full[k]) for k in FFN2_KEYS],
                                          by_cols(FFN2_KEYS))
    dh1_res, dmix, small['ln2_g'], small['ln2_b'] = resid_ln_bwd("ln2_bwd", 1.0, h1, mix, p['ln2_g'] + token[:1, :1],
                                                                 p['ln2_b'], dh2)
    dh1, gm = mixer_bwd(resm, p, dmix, dh1_res, b)
    for k in ('conv_b', 'dt_bias', 'a_log', 'd_skip', 'attn_norm_w', 'ssd_norm_w'):
        small[k] = gm[k]
    mixer_exchange, token = exchange_start("grads_mixer_start", 'scatter', [grad_shards(k, gm[k]) for k in MIXER_KEYS],
                                           by_cols(MIXER_KEYS))
    dx_res, df1, small['ln1_g'], small['ln1_b'] = resid_ln_bwd("ln1_bwd", 0.5, x2, f1, p['ln1_g'] + token[:1, :1],
                                                               p['ln1_b'], dh1)
    hb, g, u, a = res1
    small_part = pack_small(small)
    dg, du = ffn_da_act("ffn1_bwd_da_act", df1, p['ffn1_down'], g, u)
    dwd = mm_tn("ffn1_bwd_dwd", a, df1, BF16, after=dg)
    down_exchange, token = exchange_start("grads_ffn1_down_start", 'scatter', [
        grad_shards('ffn1_down', dwd), jnp.broadcast_to(small_part[None], (N_DEV,) + small_part.shape)], [False, False])
    dx = mm("ffn1_bwd_dh", [(dg, p['ffn1_gate'], 'nt'), (du, p['ffn1_up'], 'nt')], D_MODEL, add=dx_res, tn=512, after=token)
    dwg = mm_tn("ffn1_bwd_dwg", hb, dg, BF16, after=dx)
    gate_exchange, token = exchange_start("grads_ffn1_gate_start", 'scatter', [grad_shards('ffn1_gate', dwg)], [True])
    dwu = mm_tn("ffn1_bwd_dwu", hb, du, BF16, after=token)
    recv = dict(zip(('ffn1_up',), all_to_all([grad_shards('ffn1_up', dwu)], [True])))
    for keys, name, ex in (((FFN2_KEYS), "grads_ffn2_wait", ffn2_exchange), (MIXER_KEYS, "grads_mixer_wait", mixer_exchange),
                           (('ffn1_down', SMALL), "grads_ffn1_down_wait", down_exchange),
                           (('ffn1_gate',), "grads_ffn1_gate_wait", gate_exchange)):
        recv.update(zip(keys, exchange_wait(name, ex, recv['ffn1_up'])))
    outs = adamw_small(recv.pop(SMALL), wl, ml, vl)
    for k, r in recv.items():
        shard = shapes[k][1:]
        res = adamw(f"adamw_{k}", r, *[d[k].reshape(shard) for d in (wl, ml, vl)], ADAMW_TM.get(k, shard[0]))
        for o, a in zip(outs, res):
            o[k] = a.reshape(shapes[k])
    return (loss, dx.reshape(b, s, dm), *[o[k] for o in outs for k in WEIGHTS])
```

```python
import functools
import math

import jax
import jax.numpy as jnp
import numpy as np
from jax import lax
from jax.experimental import pallas as pl
from jax.experimental.pallas import tpu as pltpu

F32, BF16 = jnp.float32, jnp.bfloat16
HI = lax.Precision.HIGHEST
MESH = pl.DeviceIdType.MESH
AXES = ("x", "y", "c")
N_DEV = 8

D_MODEL = 1024
SEQ = 2048
HEAD_DIM = 64
N_HEADS = 12
D_ATTN = N_HEADS * HEAD_DIM
DILATIONS = (1, 4, 16)
ATTN_BLOCK = 128
ROPE_THETA = 500000.0
ROPE_DIM = 16
D_SSD = 768
SSD_GROUPS = 4
SSD_STATE = 128
SSD_CHUNK = 128
D_BC = SSD_GROUPS * SSD_STATE
D_CONV = D_SSD + 2 * D_BC
CONV_WIDTH = 4
D_QKVZ = 3 * D_ATTN + D_SSD
D_IN_PROJ = D_QKVZ + D_CONV + N_HEADS
D_FF = 2816
ALPHA = 2.0 ** 0.25
LN_EPS = 1e-5
RMS_EPS = 1e-6
ADAM_LR, ADAM_B1, ADAM_B2, ADAM_EPS, ADAM_WD, ADAM_STEP = 0.001, 0.9, 0.999, 1e-08, 0.01, 10

LANES = 128
VMEM_LIMIT = 52 * 1024 * 1024
NEG = -1e30

WEIGHTS = ['ln1_g', 'ln1_b', 'ffn1_gate', 'ffn1_up', 'ffn1_down', 'w_in', 'conv_w', 'conv_b', 'dt_bias', 'a_log',
           'd_skip', 'attn_norm_w', 'ssd_norm_w', 'w_out', 'ln2_g', 'ln2_b', 'ffn2_gate', 'ffn2_up', 'ffn2_down',
           'ln3_g', 'ln3_b']
COL_SHARDED = ('ffn1_gate', 'ffn1_up', 'conv_w', 'ffn2_gate', 'ffn2_up')
ROW_SHARDED = ('ffn1_down', 'w_in', 'w_out', 'ffn2_down')
SHARDED = tuple(n for n in WEIGHTS if n in COL_SHARDED or n in ROW_SHARDED)
REPLICATED = tuple(n for n in WEIGHTS if n not in SHARDED)
FF_SHARD = D_FF // N_DEV
FF_PAD = -(-FF_SHARD // LANES) * LANES
D_FF_INT = N_DEV * FF_PAD


def _cparams(sem=None):
    return pltpu.CompilerParams(dimension_semantics=sem, vmem_limit_bytes=VMEM_LIMIT)


def _tile(n, prefs):
    for p in prefs:
        if n % p == 0:
            return p
    return n


class Op:
    def __init__(self, arr, bw=None, cb=0, ro=0):
        self.arr, self.bw, self.cb, self.ro = arr, (arr.shape[1] if bw is None else bw), cb, ro


def _op(a):
    return a if isinstance(a, Op) else Op(a)


def rowwise(name, fn, ins, consts, outs, accs=(), tm=256):
    ins = [_op(a) for a in ins]
    rows = outs[0][0]
    n_in, n_c, n_o, n_a = len(ins), len(consts), len(outs), len(accs)
    tm = min(tm, rows)
    assert rows % tm == 0, (name, rows, tm)

    def body(*refs):
        vals = [r[...].astype(F32) for r in refs[:n_in + n_c]]
        res = fn(*vals)
        res = res if isinstance(res, (tuple, list)) else (res,)
        o_refs = refs[n_in + n_c:n_in + n_c + n_o]
        a_refs = refs[n_in + n_c + n_o:]
        for r, v in zip(o_refs, res[:n_o]):
            r[...] = v.astype(r.dtype)
        if n_a:
            @pl.when(pl.program_id(0) == 0)
            def _():
                for r in a_refs:
                    r[...] = jnp.zeros(r.shape, r.dtype)
            for r, v in zip(a_refs, res[n_o:]):
                r[...] += v

    in_specs = [pl.BlockSpec((tm, o.bw), functools.partial(lambda i, o: (i + o.ro, o.cb), o=o)) for o in ins]
    in_specs += [pl.BlockSpec(c.shape, functools.partial(lambda i, nd: (0,) * nd, nd=c.ndim)) for c in consts]
    out_specs = [pl.BlockSpec((tm, w), lambda i: (i, 0)) for (_, w, _) in outs]
    out_specs += [pl.BlockSpec(s, functools.partial(lambda i, nd: (0,) * nd, nd=len(s))) for s in accs]
    out_shape = [jax.ShapeDtypeStruct((r, w), dt) for (r, w, dt) in outs]
    out_shape += [jax.ShapeDtypeStruct(s, F32) for s in accs]
    res = pl.pallas_call(
        body, name=name, grid=(rows // tm,), in_specs=in_specs, out_specs=out_specs, out_shape=out_shape,
        compiler_params=_cparams(("arbitrary",) if n_a else ("parallel",)),
    )(*[o.arr for o in ins], *consts)
    return res


MM_TM = 512
MM_TN = (1024, 896, 768, 512, 256, 128)
_NT = (((1,), (1,)), ((), ()))
_NN = (((1,), (0,)), ((), ()))
_TN = (((0,), (0,)), ((), ()))


def _dot(a, b, dn, precision=None):
    return lax.dot_general(a, b, dn, preferred_element_type=F32, precision=precision)


def _mm_specs(name, pairs, n_out, tm, tn):
    in_specs, args = [], []
    for a, b, mode in pairs:
        o = _op(a)
        in_specs.append(pl.BlockSpec((tm, o.bw), functools.partial(lambda j, i, o: (i, o.cb), o=o)))
        args.append(o.arr)
        if mode == 'nn':
            assert b.shape == (o.bw, n_out), (name, b.shape, o.bw, n_out)
            in_specs.append(pl.BlockSpec((o.bw, tn), lambda j, i: (0, j)))
        else:
            assert b.shape == (n_out, o.bw), (name, b.shape, o.bw, n_out)
            in_specs.append(pl.BlockSpec((tn, o.bw), lambda j, i: (j, 0)))
        args.append(b)
    return in_specs, args


def _mm_acc(refs, pairs):
    acc = None
    for k, (_, _, mode) in enumerate(pairs):
        d = _dot(refs[2 * k][...].astype(BF16), refs[2 * k + 1][...].astype(BF16), _NN if mode == 'nn' else _NT)
        acc = d if acc is None else acc + d
    return acc


def mm(name, pairs, n_out, add=None, out_dtype=F32, tm=MM_TM, tn=None, after=None):
    m = _op(pairs[0][0]).arr.shape[0]
    tn = tn or _tile(n_out, MM_TN)
    n_p = len(pairs)

    def body(*refs):
        acc = _mm_acc(refs, pairs)
        if add is not None:
            acc = acc + refs[2 * n_p][...]
        refs[-1][...] = acc.astype(refs[-1].dtype)

    in_specs, args = _mm_specs(name, pairs, n_out, tm, tn)
    tile = pl.BlockSpec((tm, tn), lambda j, i: (i, j))
    if add is not None:
        in_specs.append(tile)
        args.append(add)
    if after is not None:
        in_specs.append(pl.BlockSpec(memory_space=pl.ANY))
        args.append(after)
    return pl.pallas_call(
        body, name=name, grid=(n_out // tn, m // tm), in_specs=in_specs, out_specs=tile,
        out_shape=jax.ShapeDtypeStruct((m, n_out), out_dtype),
        compiler_params=_cparams(("parallel", "parallel")),
    )(*args)


def mm_tn(name, a, b, out_dtype=F32, tt=1024, after=None):
    a, b = _op(a), _op(b)
    t = a.arr.shape[0]
    k, n = a.bw, b.bw
    tk = _tile(k, (512, 896, 768, 256, 128))
    tn = _tile(n, (3072, 1792) + MM_TN)
    tt = min(tt, t)
    n_t = t // tt
    order = [] if after is None else [after]

    def body(a_ref, b_ref, *rest):
        o_ref, acc_ref = rest[-2:]
        s = pl.program_id(2)
        d = _dot(a_ref[...].astype(BF16), b_ref[...].astype(BF16), _TN)

        @pl.when(s == 0)
        def _():
            acc_ref[...] = d

        @pl.when(s > 0)
        def _():
            acc_ref[...] += d

        @pl.when(s == n_t - 1)
        def _():
            o_ref[...] = acc_ref[...].astype(o_ref.dtype)

    return pl.pallas_call(
        body, name=name, grid=(k // tk, n // tn, n_t),
        in_specs=[pl.BlockSpec((tt, tk), functools.partial(lambda kk, nn, s, o: (s, o.cb * (o.bw // tk) + kk), o=a)),
                  pl.BlockSpec((tt, tn), functools.partial(lambda kk, nn, s, o: (s, o.cb * (o.bw // tn) + nn), o=b))]
        + [pl.BlockSpec(memory_space=pl.ANY) for _ in order],
        out_specs=pl.BlockSpec((tk, tn), lambda kk, nn, s: (kk, nn)),
        out_shape=jax.ShapeDtypeStruct((k, n), out_dtype),
        scratch_shapes=[pltpu.VMEM((tk, tn), F32)],
        compiler_params=_cparams(("parallel", "parallel", "arbitrary")),
    )(a.arr, b.arr, *order)


def _sigmoid(x):
    return 1.0 / (1.0 + jnp.exp(-x))


def _silu(x):
    return x * _sigmoid(x)


def _softplus(x):
    return jnp.maximum(x, 0.0) + jnp.log(1.0 + jnp.exp(-jnp.abs(x)))


def _act(g, u):
    return _silu(g) * u


def _resid_ln(scale, h, branch, g, b):
    r = ALPHA * h + scale * branch
    mu = jnp.mean(r, axis=-1, keepdims=True)
    var = jnp.mean(jnp.square(r - mu), axis=-1, keepdims=True)
    return (r - mu) * lax.rsqrt(var + LN_EPS) * g + b


def _rms(t, w):
    return t * lax.rsqrt(jnp.mean(t * t, axis=-1, keepdims=True) + RMS_EPS) * w


def _branch_weights(l1, l2, l3):
    m = jnp.maximum(jnp.maximum(l1, l2), l3)
    e1, e2, e3 = jnp.exp(l1 - m), jnp.exp(l2 - m), jnp.exp(l3 - m)
    inv = 1.0 / (e1 + e2 + e3)
    return e1 * inv, e2 * inv, e3 * inv


def _gate(y, xs, z, dskip, w):
    return _rms((y + dskip * xs) * _silu(z), w)


def _rot(x):
    d = lax.broadcasted_iota(jnp.int32, x.shape, 1) % HEAD_DIM
    up = pltpu.roll(x, x.shape[1] - ROPE_DIM // 2, 1)
    down = jnp.where(d < ROPE_DIM, pltpu.roll(x, ROPE_DIM // 2, 1), 0.0)
    return jnp.where(d < ROPE_DIM // 2, up, down)


def ffn_gate_up(name, h, wg, wu, after=()):
    m, nf = h.shape[0], wg.shape[1]
    tn = _tile(nf, MM_TN)

    def body(h_ref, g_w, u_w, *rest):
        g_ref, u_ref, a_ref = rest[-3:]
        hb = h_ref[...].astype(BF16)
        g = _dot(hb, g_w[...].astype(BF16), _NN)
        u = _dot(hb, u_w[...].astype(BF16), _NN)
        g_ref[...] = g.astype(g_ref.dtype)
        u_ref[...] = u.astype(u_ref.dtype)
        a_ref[...] = _act(g, u).astype(a_ref.dtype)

    in_specs, args = _mm_specs(name, [(h, wg, 'nn')], nf, MM_TM, tn)
    in_specs.append(in_specs[1])
    in_specs += [pl.BlockSpec(memory_space=pl.ANY) for _ in after]
    tile = pl.BlockSpec((MM_TM, tn), lambda j, i: (i, j))
    return pl.pallas_call(
        body, name=name, grid=(nf // tn, m // MM_TM), in_specs=in_specs, out_specs=[tile] * 3,
        out_shape=[jax.ShapeDtypeStruct((m, nf), BF16)] * 3, compiler_params=_cparams(("parallel", "parallel")),
    )(*args, wu, *after)


def ffn_da_act(name, df, wd, g, u):
    m, nf = df.shape[0], wd.shape[0]
    tn = _tile(nf, MM_TN)

    def body(df_ref, w_ref, g_ref, u_ref, dg_ref, du_ref):
        da = _dot(df_ref[...].astype(BF16), w_ref[...].astype(BF16), _NT)
        _, vjp = jax.vjp(_act, g_ref[...].astype(F32), u_ref[...].astype(F32))
        dg, du = vjp(da)
        dg_ref[...] = dg.astype(dg_ref.dtype)
        du_ref[...] = du.astype(du_ref.dtype)

    in_specs, args = _mm_specs(name, [(df, wd, 'nt')], nf, MM_TM, tn)
    tile = pl.BlockSpec((MM_TM, tn), lambda j, i: (i, j))
    return pl.pallas_call(
        body, name=name, grid=(nf // tn, m // MM_TM), in_specs=in_specs + [tile, tile], out_specs=[tile] * 2,
        out_shape=[jax.ShapeDtypeStruct((m, nf), BF16)] * 2, compiler_params=_cparams(("parallel", "parallel")),
    )(*args, g, u)


def resid_ln_fwd(name, scale, h, branch, ln_g, ln_b):
    t = h.shape[0]

    def fn(*a):
        y = _resid_ln(scale, *a)
        return y, y

    return rowwise(name, fn, [h, branch], [ln_g, ln_b], [(t, D_MODEL, F32), (t, D_MODEL, BF16)], tm=512)


def ffn_fwd(tag, hb, wg, wu, wd, after=()):
    g, u, a = ffn_gate_up(f"{tag}_gate_up", hb, wg, wu, after)
    f = mm(f"{tag}_down", [(a, wd, 'nn')], D_MODEL)
    return f, (hb, g, u, a)


def ln_loss_bwd(name, h, branch, target, ln_g, ln_b):
    t, dm = h.shape

    def fn(h_, br_, tgt, g_, b_):
        y, vjp = jax.vjp(functools.partial(_resid_ln, 0.5), h_, br_, g_, b_)
        e = y - tgt
        return (*vjp(e * (1.0 / dm)), jnp.sum(e * e, axis=0, keepdims=True))

    return rowwise(name, fn, [h, branch, target], [ln_g, ln_b], [(t, dm, F32), (t, dm, F32)],
                   accs=[(1, dm), (1, dm), (1, dm)], tm=512)


def resid_ln_bwd(name, scale, h, branch, ln_g, ln_b, dout, extra=None):
    t = h.shape[0]

    def fn(h_, br_, do_, *rest):
        g_, b_ = rest[-2], rest[-1]
        _, vjp = jax.vjp(functools.partial(_resid_ln, scale), h_, br_, g_, b_)
        dh, dbr, dg, db = vjp(do_)
        if extra is not None:
            dh = dh + rest[0]
        return dh, dbr, dg, db

    ins = [h, branch, dout] + ([extra] if extra is not None else [])
    return rowwise(name, fn, ins, [ln_g, ln_b], [(t, D_MODEL, F32), (t, D_MODEL, F32)],
                   accs=[(1, D_MODEL), (1, D_MODEL)], tm=512)


def ffn_bwd(tag, res, wg, wu, wd, df, dh_resid):
    hb, g, u, a = res
    dg, du = ffn_da_act(f"{tag}_bwd_da_act", df, wd, g, u)
    dwd = mm_tn(f"{tag}_bwd_dwd", a, df, BF16)
    dh = mm(f"{tag}_bwd_dh", [(dg, wg, 'nt'), (du, wu, 'nt')], D_MODEL, add=dh_resid, tn=512)
    dwg = mm_tn(f"{tag}_bwd_dwg", hb, dg, BF16)
    dwu = mm_tn(f"{tag}_bwd_dwu", hb, du, BF16)
    return dh, dwg, dwu, dwd


def rope_tables(positions):
    inv_freq = ROPE_THETA ** (-jnp.arange(0, ROPE_DIM, 2, dtype=F32) / ROPE_DIM)
    ang = positions.reshape(-1, 1).astype(F32) * inv_freq
    c, s = jnp.cos(ang), jnp.sin(ang)
    t = ang.shape[0]
    cosv = jnp.concatenate([c, c, jnp.ones((t, HEAD_DIM - ROPE_DIM), F32)], axis=1)
    sinv = jnp.concatenate([-s, s, jnp.zeros((t, HEAD_DIM - ROPE_DIM), F32)], axis=1)
    return jnp.tile(cosv, (1, 2)), jnp.tile(sinv, (1, 2))


def _pair_masks():
    lane = lax.broadcasted_iota(jnp.int32, (1, LANES), 1)
    return (lane < HEAD_DIM, lane >= HEAD_DIM)


def _band_masks():
    row = lax.broadcasted_iota(jnp.int32, (ATTN_BLOCK, ATTN_BLOCK), 0)
    col = lax.broadcasted_iota(jnp.int32, (ATTN_BLOCK, ATTN_BLOCK), 1)
    return col >= row, col <= row


def _residue_blocks():
    out = []
    for g, d in enumerate(DILATIONS):
        for r in range(d):
            for i in range(SEQ // d // ATTN_BLOCK):
                rows = lambda j: pl.ds(r + j * ATTN_BLOCK * d, ATTN_BLOCK, stride=d) if d > 1 else pl.ds(j * ATTN_BLOCK, ATTN_BLOCK)
                out.append((g, rows(i), rows(i - 1) if i > 0 else None))
    return out


N_HEAD_PAIRS = D_ATTN // LANES
SCALE = HEAD_DIM ** -0.5
ATTN_GROUP = 4


def _block_operands(qr, kr, v_ref, cur, prev):
    prev_ok, cur_ok = _band_masks()
    if prev is None:
        return qr[cur, :], kr[cur, :].astype(BF16), v_ref[cur, :], cur_ok
    kcat = jnp.concatenate([kr[prev, :], kr[cur, :]], axis=0).astype(BF16)
    vcat = jnp.concatenate([v_ref[prev, :], v_ref[cur, :]], axis=0)
    return qr[cur, :], kcat, vcat, jnp.concatenate([prev_ok, cur_ok], axis=1)


def _attn_specs(b):
    col = lambda cb: pl.BlockSpec((SEQ, LANES), lambda bb, hp: (bb, cb + hp))
    tab = pl.BlockSpec((SEQ, LANES), lambda bb, hp: (bb, 0))
    return col, tab


def attn_fwd(qkvz, cosv, sinv, b):
    t = qkvz.shape[0]
    col, tab = _attn_specs(b)
    blocks = _residue_blocks()

    def body(q_ref, k_ref, v_ref, c_ref, s_ref, o_ref, l1_ref, l2_ref, l3_ref, qr, kr, o1, o2, o3):
        l_refs, o_scr = (l1_ref, l2_ref, l3_ref), (o1, o2, o3)
        c, s = c_ref[...], s_ref[...]
        q, k = q_ref[...], k_ref[...]
        qr[...] = q * c + _rot(q) * s
        kr[...] = k * c + _rot(k) * s
        masks = _pair_masks()
        for lo in range(0, len(blocks), ATTN_GROUP):
            chains = []
            for g, cur, prev in blocks[lo:lo + ATTN_GROUP]:
                q2, kcat, vcat, ok = _block_operands(qr, kr, v_ref, cur, prev)
                for m in masks:
                    qm = jnp.where(m, q2, 0.0).astype(BF16)
                    chains.append(dict(g=g, cur=cur, m=m, v=jnp.where(m, vcat, 0.0).astype(BF16),
                                       s=jnp.where(ok, _dot(qm, kcat, _NT) * SCALE, NEG)))
            for ch in chains:
                mx = jnp.max(ch['s'], axis=1, keepdims=True)
                p = jnp.exp(ch['s'] - mx)
                den = jnp.sum(p, axis=1, keepdims=True)
                ch.update(p=p.astype(BF16), inv=1.0 / den, lse=mx + jnp.log(den))
            for ch in chains:
                ch['o'] = _dot(ch['p'], ch['v'], _NN) * ch['inv']
            for c0, c1 in zip(chains[0::2], chains[1::2]):
                o_scr[c0['g']][c0['cur'], :] = c0['o'] + c1['o']
                l_refs[c0['g']][c0['cur'], :] = jnp.where(c0['m'], c0['lse'], c1['lse'])
        w1, w2, w3 = _branch_weights(l1_ref[...], l2_ref[...], l3_ref[...])
        o_ref[...] = w1 * o1[...] + w2 * o2[...] + w3 * o3[...]

    shp = jax.ShapeDtypeStruct((t, D_ATTN), F32)
    return pl.pallas_call(
        body, name="attn_fwd", grid=(b, N_HEAD_PAIRS),
        in_specs=[col(0), col(N_HEAD_PAIRS), col(2 * N_HEAD_PAIRS), tab, tab],
        out_specs=[col(0)] * 4, out_shape=[shp] * 4,
        scratch_shapes=[pltpu.VMEM((SEQ, LANES), F32)] * 5,
        compiler_params=_cparams(("parallel", "parallel")),
    )(qkvz, qkvz, qkvz, cosv, sinv)


def attn_bwd(qkvz, cosv, sinv, dmix, mixed, lses, b):
    t = qkvz.shape[0]
    col, tab = _attn_specs(b)
    blocks = _residue_blocks()
    hd = np.arange(LANES) // HEAD_DIM
    head_ones = jnp.asarray((hd[:, None] == hd[None, :]).astype(np.float32))

    def body(q_ref, k_ref, v_ref, c_ref, s_ref, dm_ref, mx_ref, l1_ref, l2_ref, l3_ref, ones_ref,
             dq_out, dk_out, dv_out, qr, kr, do1, do2, do3, dd1, dd2, dd3, dq_ref, dk_ref, dv_ref):
        l_refs, do_scr, dd_scr = (l1_ref, l2_ref, l3_ref), (do1, do2, do3), (dd1, dd2, dd3)
        c, s = c_ref[...], s_ref[...]
        q, k = q_ref[...], k_ref[...]
        qr[...] = q * c + _rot(q) * s
        kr[...] = k * c + _rot(k) * s
        dm = dm_ref[...]
        tot = _dot(dm * mx_ref[...], ones_ref[...], _NN, HI)
        for w, do_g, dd_g in zip(_branch_weights(l1_ref[...], l2_ref[...], l3_ref[...]), do_scr, dd_scr):
            do_g[...] = w * dm
            dd_g[...] = w * tot
        dq_ref[...] = jnp.zeros((SEQ, LANES), F32)
        dk_ref[...] = jnp.zeros((SEQ, LANES), F32)
        dv_ref[...] = jnp.zeros((SEQ, LANES), F32)
        masks = _pair_masks()
        for lo in range(0, len(blocks), ATTN_GROUP):
            chains = []
            for g, cur, prev in blocks[lo:lo + ATTN_GROUP]:
                q2, kcat, vcat, ok = _block_operands(qr, kr, v_ref, cur, prev)
                vcat = vcat.astype(BF16)
                do2_, l2, dd2_ = do_scr[g][cur, :], l_refs[g][cur, :], dd_scr[g][cur, :]
                l2s, dd2s = pltpu.roll(l2, HEAD_DIM, 1), pltpu.roll(dd2_, HEAD_DIM, 1)
                for m in masks:
                    qm = jnp.where(m, q2, 0.0).astype(BF16)
                    dom = jnp.where(m, do2_, 0.0).astype(BF16)
                    lrep, ddrep = jnp.where(m, l2, l2s), jnp.where(m, dd2_, dd2s)
                    if prev is not None:
                        lrep, ddrep = jnp.concatenate([lrep, lrep], axis=1), jnp.concatenate([ddrep, ddrep], axis=1)
                    chains.append(dict(cur=cur, prev=prev, qm=qm, dom=dom, km=jnp.where(m, kcat, 0), lrep=lrep, ddrep=ddrep,
                                       s=jnp.where(ok, _dot(qm, kcat, _NT) * SCALE, NEG), dp=_dot(dom, vcat, _NT)))
            for ch in chains:
                p = jnp.exp(ch['s'] - ch['lrep'])
                ch.update(p=p.astype(BF16), ds=(p * (ch['dp'] - ch['ddrep']) * SCALE).astype(BF16))
            for ch in chains:
                ch.update(dq=_dot(ch['ds'], ch['km'], _NN), dk=_dot(ch['ds'], ch['qm'], _TN), dv=_dot(ch['p'], ch['dom'], _TN))
            for c0, c1 in zip(chains[0::2], chains[1::2]):
                cur, prev = c0['cur'], c0['prev']
                dk, dv = c0['dk'] + c1['dk'], c0['dv'] + c1['dv']
                dq_ref[cur, :] += c0['dq'] + c1['dq']
                if prev is None:
                    dk_ref[cur, :] += dk
                    dv_ref[cur, :] += dv
                else:
                    dk_ref[prev, :] += dk[:ATTN_BLOCK]
                    dv_ref[prev, :] += dv[:ATTN_BLOCK]
                    dk_ref[cur, :] += dk[ATTN_BLOCK:]
                    dv_ref[cur, :] += dv[ATTN_BLOCK:]
        dq, dk = dq_ref[...], dk_ref[...]
        dq_out[...] = (dq * c + _rot(dq * s)).astype(dq_out.dtype)
        dk_out[...] = (dk * c + _rot(dk * s)).astype(dk_out.dtype)
        dv_out[...] = dv_ref[...].astype(dv_out.dtype)

    shp = jax.ShapeDtypeStruct((t, D_ATTN), BF16)
    return pl.pallas_call(
        body, name="attn_bwd", grid=(b, N_HEAD_PAIRS),
        in_specs=[col(0), col(N_HEAD_PAIRS), col(2 * N_HEAD_PAIRS), tab, tab, col(0), col(0), col(0), col(0), col(0),
                  pl.BlockSpec((LANES, LANES), lambda bb, hp: (0, 0))],
        out_specs=[col(0)] * 3, out_shape=[shp] * 3,
        scratch_shapes=[pltpu.VMEM((SEQ, LANES), F32)] * 11,
        compiler_params=_cparams(("parallel", "parallel")),
    )(qkvz, qkvz, qkvz, cosv, sinv, dmix, mixed, *lses, head_ones)


def attn_norm_fwd(mixed, norm_w):
    return rowwise("attn_norm", _rms, [mixed], [norm_w], [(mixed.shape[0], D_ATTN, BF16)])[0]


def attn_norm_bwd(dout, mixed, norm_w):
    def fn(dy, mx, w):
        _, vjp = jax.vjp(_rms, mx, w)
        return vjp(dy)

    return rowwise("attn_norm_bwd", fn, [dout, mixed], [norm_w], [(dout.shape[0], D_ATTN, F32)], accs=[(1, D_ATTN)])


CONV_TM = 256
HALO = 8


def _conv_columns(refs):
    xs_ref, bm_ref, cm_ref = refs
    out = []
    for c in range(D_CONV // LANES):
        lo = c * LANES
        ref, base = (xs_ref, 0) if lo < D_SSD else (bm_ref, D_SSD) if lo < D_SSD + D_BC else (cm_ref, D_SSD + D_BC)
        out.append((slice(lo, lo + LANES), (ref, slice(lo - base, lo - base + LANES))))
    return out


def _conv_taps(scr, w_ref, cs, first_row, step, tm):
    acc = None
    for k in range(CONV_WIDTH):
        term = w_ref[k:k + 1, cs] * scr[pl.ds(first_row + step * k, tm), cs]
        acc = term if acc is None else acc + term
    return acc


def conv_fwd(u, w, bias):
    t = u.shape[0]
    tm, per_seq = CONV_TM, SEQ // CONV_TM

    def body(u_ref, h_ref, w_ref, b_ref, xs_ref, bm_ref, cm_ref, scr):
        first = pl.program_id(0) % per_seq == 0
        scr[0:HALO, :] = jnp.where(first, 0.0, h_ref[...])
        scr[HALO:, :] = u_ref[...]
        for cs, (o_ref, os_) in _conv_columns((xs_ref, bm_ref, cm_ref)):
            o_ref[:, os_] = _silu(_conv_taps(scr, w_ref, cs, HALO - CONV_WIDTH + 1, 1, tm) + b_ref[:, cs])

    return pl.pallas_call(
        body, name="conv_fwd", grid=(t // tm,),
        in_specs=[pl.BlockSpec((tm, D_CONV), lambda i: (i, 0)),
                  pl.BlockSpec((HALO, D_CONV), lambda i: (jnp.maximum(i * (tm // HALO) - 1, 0), 0)),
                  pl.BlockSpec((CONV_WIDTH, D_CONV), lambda i: (0, 0)), pl.BlockSpec((1, D_CONV), lambda i: (0, 0))],
        out_specs=[pl.BlockSpec((tm, D_SSD), lambda i: (i, 0)), pl.BlockSpec((tm, D_BC), lambda i: (i, 0)),
                   pl.BlockSpec((tm, D_BC), lambda i: (i, 0))],
        out_shape=[jax.ShapeDtypeStruct((t, D_SSD), F32), jax.ShapeDtypeStruct((t, D_BC), F32),
                   jax.ShapeDtypeStruct((t, D_BC), F32)],
        scratch_shapes=[pltpu.VMEM((tm + HALO, D_CONV), F32)],
        compiler_params=_cparams(("parallel",)),
    )(u, u, w, bias)


def conv_bwd(u, w, bias, dxs_a, dxs_b, dbm, dcm):
    t = u.shape[0]
    tm, per_seq = CONV_TM, SEQ // CONV_TM
    n_tiles = t // tm

    def body1(u_ref, h_ref, dxs_ref, dxs2_ref, dbm_ref, dcm_ref, w_ref, b_ref, dz_ref, dw_ref, db_ref, scr):
        i = pl.program_id(0)
        first = i % per_seq == 0
        scr[0:HALO, :] = jnp.where(first, 0.0, h_ref[...])
        scr[HALO:, :] = u_ref[...]

        @pl.when(i == 0)
        def _():
            dw_ref[...] = jnp.zeros(dw_ref.shape, F32)
            db_ref[...] = jnp.zeros(db_ref.shape, F32)
        for cs, (g_ref, gs) in _conv_columns((dxs_ref, dbm_ref, dcm_ref)):
            acc = _conv_taps(scr, w_ref, cs, HALO - CONV_WIDTH + 1, 1, tm) + b_ref[:, cs]
            sig = _sigmoid(acc)
            dy = g_ref[:, gs] + dxs2_ref[:, gs] if g_ref is dxs_ref else g_ref[:, gs]
            dz = dy * sig * (1.0 + acc * (1.0 - sig))
            dz_ref[:, cs] = dz
            db_ref[:, cs] += jnp.sum(dz, axis=0, keepdims=True)
            for k in range(CONV_WIDTH):
                dw_ref[k:k + 1, cs] += jnp.sum(dz * scr[pl.ds(HALO - CONV_WIDTH + 1 + k, tm), cs], axis=0, keepdims=True)

    dz, dw, db = pl.pallas_call(
        body1, name="conv_bwd_dz", grid=(n_tiles,),
        in_specs=[pl.BlockSpec((tm, D_CONV), lambda i: (i, 0)),
                  pl.BlockSpec((HALO, D_CONV), lambda i: (jnp.maximum(i * (tm // HALO) - 1, 0), 0)),
                  pl.BlockSpec((tm, D_SSD), lambda i: (i, 0)), pl.BlockSpec((tm, D_SSD), lambda i: (i, 0)),
                  pl.BlockSpec((tm, D_BC), lambda i: (i, 0)), pl.BlockSpec((tm, D_BC), lambda i: (i, 0)),
                  pl.BlockSpec((CONV_WIDTH, D_CONV), lambda i: (0, 0)), pl.BlockSpec((1, D_CONV), lambda i: (0, 0))],
        out_specs=[pl.BlockSpec((tm, D_CONV), lambda i: (i, 0)), pl.BlockSpec((CONV_WIDTH, D_CONV), lambda i: (0, 0)),
                   pl.BlockSpec((1, D_CONV), lambda i: (0, 0))],
        out_shape=[jax.ShapeDtypeStruct((t, D_CONV), F32), jax.ShapeDtypeStruct((CONV_WIDTH, D_CONV), F32),
                   jax.ShapeDtypeStruct((1, D_CONV), F32)],
        scratch_shapes=[pltpu.VMEM((tm + HALO, D_CONV), F32)],
        compiler_params=_cparams(("arbitrary",)),
    )(u, u, dxs_a, dxs_b, dbm, dcm, w, bias)

    def body2(dz_ref, n_ref, w_ref, du_ref, scr):
        last = pl.program_id(0) % per_seq == per_seq - 1
        scr[0:tm, :] = dz_ref[...]
        scr[tm:, :] = jnp.where(last, 0.0, n_ref[...])
        for c in range(D_CONV // LANES):
            cs = slice(c * LANES, (c + 1) * LANES)
            du_ref[:, cs] = _conv_taps(scr, w_ref, cs, CONV_WIDTH - 1, -1, tm).astype(du_ref.dtype)

    du = pl.pallas_call(
        body2, name="conv_bwd_du", grid=(n_tiles,),
        in_specs=[pl.BlockSpec((tm, D_CONV), lambda i: (i, 0)),
                  pl.BlockSpec((HALO, D_CONV), lambda i: (jnp.minimum((i + 1) * (tm // HALO), t // HALO - 1), 0)),
                  pl.BlockSpec((CONV_WIDTH, D_CONV), lambda i: (0, 0))],
        out_specs=pl.BlockSpec((tm, D_CONV), lambda i: (i, 0)),
        out_shape=jax.ShapeDtypeStruct((t, D_CONV), BF16),
        scratch_shapes=[pltpu.VMEM((tm + HALO, D_CONV), F32)],
        compiler_params=_cparams(("parallel",)),
    )(dz, dz, w)
    return du, dw, db


Q = SSD_CHUNK
N_PAIRS = D_SSD // LANES
HEADS_PER_GROUP = N_HEADS // SSD_GROUPS


def _rep(a, j):
    return jnp.broadcast_to(a[:, j:j + 1], a.shape)


def _dot_exact01(a, b, dn, a_is_01):
    x = b if a_is_01 else a
    hi = x.astype(BF16)
    mid = (x - hi.astype(F32)).astype(BF16)
    lo = (x - hi.astype(F32) - mid.astype(F32)).astype(BF16)
    z = a.astype(BF16) if a_is_01 else b.astype(BF16)
    out = None
    for term in (hi, mid, lo):
        d = _dot(z, term, dn) if a_is_01 else _dot(term, z, dn)
        out = d if out is None else out + d
    return out


def _pad_lanes(v, fill=0.0):
    row = jnp.pad(v.reshape(1, -1).astype(F32), ((0, 0), (0, LANES - v.size)), constant_values=fill)
    return row, row.reshape(LANES, 1)


def _ssd_common(dtr_ref, dtrt_ref, bias_r, bias_c, alog_r, alog_c):
    row = lax.broadcasted_iota(jnp.int32, (Q, Q), 0)
    col = lax.broadcasted_iota(jnp.int32, (Q, Q), 1)
    tril = row >= col
    lane = lax.broadcasted_iota(jnp.int32, (1, LANES), 1)
    a_r = jnp.where(lane < N_HEADS, -jnp.exp(alog_r[...]), 0.0)
    sub = lax.broadcasted_iota(jnp.int32, (LANES, 1), 0)
    a_c = jnp.where(sub < N_HEADS, -jnp.exp(alog_c[...]), 0.0)
    dt = _softplus(dtr_ref[...] + bias_r[...])
    cs = _dot_exact01(tril, dt * a_r, _NN, True)
    dtt = _softplus(dtrt_ref[...] + bias_c[...])
    cst = _dot_exact01(dtt * a_c, row <= col, _NN, False)
    return tril, lane, a_r, dt, cs, cst


def _ssd_specs(b, nc, rev):
    ci = (lambda c: nc - 1 - c) if rev else (lambda c: c)
    rows = lambda w: pl.BlockSpec((Q, w), lambda bb, c: (bb * nc + ci(c), 0))
    dtt = pl.BlockSpec((LANES, Q), lambda bb, c: (0, bb * nc + ci(c)))
    const = lambda s: pl.BlockSpec(s, lambda bb, c: (0,) * len(s))
    state = pl.BlockSpec((None, N_PAIRS, LANES, SSD_STATE), lambda bb, c: (bb * nc + ci(c), 0, 0, 0))
    return rows, dtt, const, state


def ssd_fwd(xs, bm, cm, dtraw, dt_bias, a_log, b):
    t = xs.shape[0]
    nc = SEQ // Q
    rows, dtt_spec, const, state = _ssd_specs(b, nc, False)
    bias_r, bias_c = _pad_lanes(dt_bias)
    alog_r, alog_c = _pad_lanes(a_log)

    def body(xs_ref, b_ref, c_ref, dtr_ref, dtrt_ref, br, bc, ar, ac, y_ref, hp_ref, h_scr):
        @pl.when(pl.program_id(1) == 0)
        def _():
            h_scr[...] = jnp.zeros(h_scr.shape, F32)
        tril, lane, _, dt, cs, cst = _ssd_common(dtr_ref, dtrt_ref, br, bc, ar, ac)
        sub = lax.broadcasted_iota(jnp.int32, (LANES, 1), 0)
        y_acc = [jnp.zeros((Q, LANES), F32) for _ in range(N_PAIRS)]
        h_old = [h_scr[p] for p in range(N_PAIRS)]
        h_new = [jnp.zeros((LANES, SSD_STATE), F32) for _ in range(N_PAIRS)]
        for g in range(SSD_GROUPS):
            bg = b_ref[:, g * SSD_STATE:(g + 1) * SSD_STATE].astype(BF16)
            cg = c_ref[:, g * SSD_STATE:(g + 1) * SSD_STATE].astype(BF16)
            cb = _dot(cg, bg, _NT)
            for j in range(g * HEADS_PER_GROUP, (g + 1) * HEADS_PER_GROUP):
                p, side = j // 2, j % 2
                m = (lane < HEAD_DIM) if side == 0 else (lane >= HEAD_DIM)
                ms = (sub < HEAD_DIM) if side == 0 else (sub >= HEAD_DIM)
                csj, dtj = _rep(cs, j), _rep(dt, j)
                lmat = jnp.exp(jnp.where(tril, csj - cst[j:j + 1, :], NEG))
                xdt = jnp.where(m, xs_ref[:, p * LANES:(p + 1) * LANES] * dtj, 0.0)
                hm = jnp.where(ms, h_old[p], 0.0)
                ydiag = _dot((cb * lmat).astype(BF16), xdt.astype(BF16), _NN)
                yoff = jnp.exp(csj) * _dot(cg, hm.astype(BF16), _NT)
                y_acc[p] = y_acc[p] + ydiag + yoff
                last = csj[Q - 1:Q, :]
                sj = _dot((xdt * jnp.exp(last - csj)).astype(BF16), bg, _TN)
                h_new[p] = h_new[p] + jnp.exp(last) * hm + sj
        for p in range(N_PAIRS):
            y_ref[:, p * LANES:(p + 1) * LANES] = y_acc[p]
            hp_ref[p] = h_old[p]
            h_scr[p] = h_new[p]

    return pl.pallas_call(
        body, name="ssd_fwd", grid=(b, nc),
        in_specs=[rows(D_SSD), rows(D_BC), rows(D_BC), rows(LANES), dtt_spec, const((1, LANES)), const((LANES, 1)),
                  const((1, LANES)), const((LANES, 1))],
        out_specs=[rows(D_SSD), state],
        out_shape=[jax.ShapeDtypeStruct((t, D_SSD), F32),
                   jax.ShapeDtypeStruct((b * nc, N_PAIRS, LANES, SSD_STATE), F32)],
        scratch_shapes=[pltpu.VMEM((N_PAIRS, LANES, SSD_STATE), F32)],
        compiler_params=_cparams(("parallel", "arbitrary")),
    )(xs, bm, cm, dtraw, dtraw.T, bias_r, bias_c, alog_r, alog_c)


def ssd_bwd(xs, bm, cm, dtraw, dt_bias, a_log, hprev, dy, b):
    t = xs.shape[0]
    nc = SEQ // Q
    rows, dtt_spec, const, state = _ssd_specs(b, nc, True)
    bias_r, bias_c = _pad_lanes(dt_bias)
    alog_r, alog_c = _pad_lanes(a_log)

    def body(xs_ref, b_ref, c_ref, dtr_ref, dtrt_ref, hp_ref, dy_ref, br, bc, ar, ac,
             dxs_ref, db_ref, dc_ref, ddt_ref, dbias_ref, dalog_ref, dh_scr):
        first = jnp.logical_and(pl.program_id(0) == 0, pl.program_id(1) == 0)

        @pl.when(pl.program_id(1) == 0)
        def _():
            dh_scr[...] = jnp.zeros(dh_scr.shape, F32)

        @pl.when(first)
        def _():
            dbias_ref[...] = jnp.zeros(dbias_ref.shape, F32)
            dalog_ref[...] = jnp.zeros(dalog_ref.shape, F32)
        tril, lane, a_r, dt, cs, cst = _ssd_common(dtr_ref, dtrt_ref, br, bc, ar, ac)
        sub = lax.broadcasted_iota(jnp.int32, (LANES, 1), 0)
        rowq = lax.broadcasted_iota(jnp.int32, (Q, 1), 0)
        triu = (lax.broadcasted_iota(jnp.int32, (Q, Q), 0) <= lax.broadcasted_iota(jnp.int32, (Q, Q), 1)).astype(F32)
        dxs_acc = [jnp.zeros((Q, LANES), F32) for _ in range(N_PAIRS)]
        dh_in = [dh_scr[p] for p in range(N_PAIRS)]
        h_in = [hp_ref[p] for p in range(N_PAIRS)]
        dh_out = [jnp.zeros((LANES, SSD_STATE), F32) for _ in range(N_PAIRS)]
        ddt = jnp.zeros((Q, LANES), F32)
        dalog = jnp.zeros((1, LANES), F32)
        for g in range(SSD_GROUPS):
            gs = slice(g * SSD_STATE, (g + 1) * SSD_STATE)
            bg, cg = b_ref[:, gs].astype(BF16), c_ref[:, gs].astype(BF16)
            cb = _dot(cg, bg, _NT)
            dcb = jnp.zeros((Q, Q), F32)
            dbg = jnp.zeros((Q, SSD_STATE), F32)
            dcg = jnp.zeros((Q, SSD_STATE), F32)
            for j in range(g * HEADS_PER_GROUP, (g + 1) * HEADS_PER_GROUP):
                p, side = j // 2, j % 2
                m = (lane < HEAD_DIM) if side == 0 else (lane >= HEAD_DIM)
                ms = (sub < HEAD_DIM) if side == 0 else (sub >= HEAD_DIM)
                csj, dtj = _rep(cs, j), _rep(dt, j)
                lmat = jnp.exp(jnp.where(tril, csj - cst[j:j + 1, :], NEG))
                x2 = jnp.where(m, xs_ref[:, p * LANES:(p + 1) * LANES], 0.0)
                xdt = x2 * dtj
                dym = jnp.where(m, dy_ref[:, p * LANES:(p + 1) * LANES], 0.0)
                hm = jnp.where(ms, h_in[p], 0.0)
                dhm = jnp.where(ms, dh_in[p], 0.0)
                ecs = jnp.exp(csj)
                last = csj[Q - 1:Q, :]
                decay = jnp.exp(last - csj)
                el = jnp.exp(last)
                gmat = cb * lmat
                dymb, xdtb = dym.astype(BF16), xdt.astype(BF16)
                dg = _dot(dymb, xdtb, _NT)
                dxdt = _dot(gmat.astype(BF16), dymb, _TN)
                dcb = dcb + dg * lmat
                ej = dg * gmat
                col_sums = jnp.broadcast_to(jnp.sum(ej, axis=0, keepdims=True), (Q, Q)).T
                dcs = jnp.sum(ej, axis=1, keepdims=True) - col_sums
                ch = _dot(cg, hm.astype(BF16), _NT)
                dye = dym * ecs
                dcs = dcs + jnp.sum(dye * ch, axis=1, keepdims=True)
                dcg = dcg + _dot(dye.astype(BF16), hm.astype(BF16), _NN)
                dhp = _dot(dye.astype(BF16), cg, _TN)
                wmat = _dot(bg, dhm.astype(BF16), _NT)
                xd = xdt * decay
                dxdt = dxdt + decay * wmat
                ddl = jnp.sum(xd * wmat, axis=1, keepdims=True)
                dlast = jnp.sum(ddl, axis=0, keepdims=True) + el * jnp.sum(jnp.sum(dhm * hm, axis=1, keepdims=True), axis=0, keepdims=True)
                dcs = dcs - ddl + jnp.where(rowq == Q - 1, dlast, 0.0)
                dbg = dbg + _dot(xd.astype(BF16), dhm.astype(BF16), _NN)
                dh_out[p] = dh_out[p] + el * dhm + dhp
                da = _dot_exact01(triu, dcs, _NN, True)
                aj = jnp.sum(jnp.where(lane == j, a_r, 0.0), axis=1, keepdims=True)
                ddtj = da * aj + jnp.sum(dxdt * x2, axis=1, keepdims=True)
                ddt = ddt + jnp.where(lane == j, ddtj, 0.0)
                dalog = dalog + jnp.where(lane == j, jnp.sum(da * dtj, axis=0, keepdims=True) * aj, 0.0)
                dxs_acc[p] = dxs_acc[p] + dxdt * dtj
            dcbb = dcb.astype(BF16)
            dc_ref[:, gs] = dcg + _dot(dcbb, bg, _NN)
            db_ref[:, gs] = dbg + _dot(dcbb, cg, _TN)
        for p in range(N_PAIRS):
            dxs_ref[:, p * LANES:(p + 1) * LANES] = dxs_acc[p]
            dh_scr[p] = dh_out[p]
        ddtraw = ddt * _sigmoid(dtr_ref[...] + br[...])
        ddt_ref[...] = ddtraw
        dbias_ref[...] += jnp.sum(ddtraw, axis=0, keepdims=True)
        dalog_ref[...] += dalog

    return pl.pallas_call(
        body, name="ssd_bwd", grid=(b, nc),
        in_specs=[rows(D_SSD), rows(D_BC), rows(D_BC), rows(LANES), dtt_spec, state, rows(D_SSD), const((1, LANES)),
                  const((LANES, 1)), const((1, LANES)), const((LANES, 1))],
        out_specs=[rows(D_SSD), rows(D_BC), rows(D_BC), rows(LANES), const((1, LANES)), const((1, LANES))],
        out_shape=[jax.ShapeDtypeStruct((t, D_SSD), F32), jax.ShapeDtypeStruct((t, D_BC), F32),
                   jax.ShapeDtypeStruct((t, D_BC), F32), jax.ShapeDtypeStruct((t, LANES), F32),
                   jax.ShapeDtypeStruct((1, LANES), F32), jax.ShapeDtypeStruct((1, LANES), F32)],
        scratch_shapes=[pltpu.VMEM((N_PAIRS, LANES, SSD_STATE), F32)],
        compiler_params=_cparams(("arbitrary", "arbitrary")),
    )(xs, bm, cm, dtraw, dtraw.T, hprev, dy, bias_r, bias_c, alog_r, alog_c)


def _split_w_in(w_in):
    w_dt = jnp.pad(w_in[:, D_QKVZ + D_CONV:], ((0, 0), (0, LANES - N_HEADS)))
    return w_in[:, :D_QKVZ], w_in[:, D_QKVZ:D_QKVZ + D_CONV], w_dt


def mixer_fwd(hb, p, cosv, sinv, b):
    t = hb.shape[0]
    w_a, w_b, w_c = _split_w_in(p['w_in'])
    qkvz = mm("in_qkvz", [(hb, w_a, 'nn')], D_QKVZ)
    xbc = mm("in_xbc", [(hb, w_b, 'nn')], D_CONV)
    dtraw = mm("in_dt", [(hb, w_c, 'nn')], LANES)
    mixed, *lses = attn_fwd(qkvz, cosv, sinv, b)
    attn = attn_norm_fwd(mixed, p['attn_norm_w'])
    xs, bm, cm = conv_fwd(xbc, p['conv_w'], p['conv_b'])
    y, hprev = ssd_fwd(xs, bm, cm, dtraw, p['dt_bias'], p['a_log'], b)
    dskip = jnp.repeat(p['d_skip'].reshape(-1), HEAD_DIM).reshape(1, D_SSD)
    yg, = rowwise("ssd_gate", _gate, [y, xs, Op(qkvz, D_SSD, 3)], [dskip, p['ssd_norm_w']], [(t, D_SSD, BF16)])
    mix = mm("out_proj", [(attn, p['w_out'][:D_ATTN], 'nn'), (yg, p['w_out'][D_ATTN:], 'nn')], D_MODEL)
    res = dict(hb=hb, qkvz=qkvz, xbc=xbc, dtraw=dtraw, mixed=mixed, lses=lses, attn=attn, xs=xs, bm=bm, cm=cm,
               y=y, hprev=hprev, dskip=dskip, yg=yg, cosv=cosv, sinv=sinv)
    return mix, res


def mixer_bwd(r, p, dmix, dh_resid, b):
    t = dmix.shape[0]
    w_a, w_b, w_c = _split_w_in(p['w_in'])
    w_out = p['w_out']
    dattn = mm("out_bwd_dattn", [(dmix, w_out[:D_ATTN], 'nt')], D_ATTN)
    dyg = mm("out_bwd_dyg", [(dmix, w_out[D_ATTN:], 'nt')], D_SSD)
    dw_out = jnp.concatenate([mm_tn("out_bwd_dw_a", r['attn'], dmix, BF16),
                              mm_tn("out_bwd_dw_y", r['yg'], dmix, BF16)], axis=0)

    def gate_bwd(dy_, y_, xs_, z_, ds_, w_):
        _, vjp = jax.vjp(_gate, y_, xs_, z_, ds_, w_)
        return vjp(dy_)

    dy, dxs_a, dz, ddskip, dssd_norm = rowwise(
        "ssd_gate_bwd", gate_bwd, [dyg, r['y'], r['xs'], Op(r['qkvz'], D_SSD, 3)], [r['dskip'], p['ssd_norm_w']],
        [(t, D_SSD, F32), (t, D_SSD, F32), (t, D_SSD, BF16)], accs=[(1, D_SSD), (1, D_SSD)])
    dxs_b, dbm, dcm, ddtraw, ddt_bias, da_log = ssd_bwd(r['xs'], r['bm'], r['cm'], r['dtraw'], p['dt_bias'], p['a_log'],
                                                        r['hprev'], dy, b)
    dxbc, dconv_w, dconv_b = conv_bwd(r['xbc'], p['conv_w'], p['conv_b'], dxs_a, dxs_b, dbm, dcm)
    dmixed, dattn_norm = attn_norm_bwd(dattn, r['mixed'], p['attn_norm_w'])
    dq, dk, dv = attn_bwd(r['qkvz'], r['cosv'], r['sinv'], dmixed, r['mixed'], r['lses'], b)
    wq, wk, wv, wz = (w_a[:, i * D_ATTN:(i + 1) * D_ATTN] for i in range(4))
    dh = mm("in_bwd_dh", [(dq, wq, 'nt'), (dk, wk, 'nt'), (dv, wv, 'nt'), (dz, wz, 'nt'), (dxbc, w_b, 'nt'),
                          (ddtraw, w_c, 'nt')], D_MODEL, add=dh_resid, tn=512)
    h = r['hb']
    dw_in = jnp.concatenate([mm_tn("in_bwd_dwq", h, dq, BF16), mm_tn("in_bwd_dwk", h, dk, BF16),
                             mm_tn("in_bwd_dwv", h, dv, BF16), mm_tn("in_bwd_dwz", h, dz, BF16),
                             mm_tn("in_bwd_dwx", h, dxbc, BF16), mm_tn("in_bwd_dwdt", h, ddtraw, BF16)[:, :N_HEADS]], axis=1)
    head_sum = lambda v: v.reshape(N_HEADS, HEAD_DIM).sum(axis=1).reshape(1, N_HEADS)
    grads = dict(w_in=dw_in, w_out=dw_out, conv_w=dconv_w, conv_b=dconv_b, dt_bias=ddt_bias[:, :N_HEADS],
                 a_log=da_log[:, :N_HEADS], d_skip=head_sum(ddskip), attn_norm_w=dattn_norm, ssd_norm_w=dssd_norm)
    return dh, grads


FFN1_KEYS = ('ffn1_gate', 'ffn1_up', 'ffn1_down')
FFN2_KEYS = ('ffn2_gate', 'ffn2_up', 'ffn2_down')
MIXER_KEYS = ('w_in', 'conv_w', 'w_out')
FFN_COL = ('ffn1_gate', 'ffn1_up', 'ffn2_gate', 'ffn2_up')
FFN_ROW = ('ffn1_down', 'ffn2_down')
CONV_W_COMM = (8, 2 * LANES)
SMALL = 'small'


def comm_shape(k, shapes):
    if k in FFN_COL:
        return (D_MODEL, FF_PAD)
    if k in FFN_ROW:
        return (FF_PAD, D_MODEL)
    if k == 'conv_w':
        return CONV_W_COMM
    return tuple(shapes[k][1:])


def to_comm(k, vals, shapes):
    a = vals[k].reshape(shapes[k][1:])
    r_, c_ = comm_shape(k, shapes)
    return jnp.pad(a, ((0, r_ - a.shape[0]), (0, c_ - a.shape[1])))


SMALL_ROWS, SMALL_COLS = 16, D_CONV


def pack_small(small):
    rows = [jnp.pad(small[r].reshape(1, -1), ((0, 0), (0, SMALL_COLS - small[r].size))) for r in REPLICATED]
    return jnp.concatenate(rows + [jnp.zeros((SMALL_ROWS - len(rows), SMALL_COLS), F32)], axis=0)


def full_weight(k, g):
    if k in FFN_COL:
        return g
    if k == 'conv_w':
        return jnp.transpose(g[:, :CONV_WIDTH, :D_CONV // N_DEV], (1, 0, 2)).reshape(CONV_WIDTH, D_CONV)
    return g.reshape(N_DEV * g.shape[1], g.shape[2])


def grad_shards(k, g):
    if k in FFN_COL:
        return g
    if k == 'conv_w':
        s = jnp.transpose(g.reshape(CONV_WIDTH, N_DEV, D_CONV // N_DEV), (1, 0, 2))
        return jnp.pad(s, ((0, 0), (0, CONV_W_COMM[0] - CONV_WIDTH), (0, CONV_W_COMM[1] - D_CONV // N_DEV)))
    return g.reshape(N_DEV, g.shape[0] // N_DEV, g.shape[1])


def _flip(v, bit):
    return 1 - v if bit else v


N_PEER_COPIES = N_DEV - 1


def _comm_call(name, body, arrs, out_shape):
    n = len(arrs)
    return pl.pallas_call(
        functools.partial(body, n), name=name, out_shape=out_shape,
        in_specs=[pl.BlockSpec(memory_space=pl.ANY)] * n, out_specs=[pl.BlockSpec(memory_space=pl.ANY)] * n,
        scratch_shapes=[pltpu.SemaphoreType.DMA((n * N_PEER_COPIES,)), pltpu.SemaphoreType.DMA((n * N_PEER_COPIES,)),
                        pltpu.SemaphoreType.DMA((n,))],
    )(*arrs)


def _blk(ref, idx, by_cols):
    if not by_cols:
        return ref.at[idx]
    c = ref.shape[1] // N_DEV
    return ref.at[:, pl.ds(pl.multiple_of(idx * c, LANES), c)]


def _blocked_shape(a, by_cols):
    return (a.shape[0], N_DEV * a.shape[1]) if by_cols else (N_DEV,) + a.shape


def all_gather(arrs, by_cols):
    def body(n, *refs):
        x_refs, out_refs, (send_sems, recv_sems, local_sems) = refs[:n], refs[n:2 * n], refs[2 * n:]
        x, y, c = lax.axis_index("x"), lax.axis_index("y"), lax.axis_index("c")
        me, sibling = (x, y, c), (x, y, 1 - c)
        chips = [(1 - x, y), (x, 1 - y), (1 - x, 1 - y)]

        def copy(a, k, block, to, src=None):
            px, py, pc = block
            dst = _blk(out_refs[a], 4 * px + 2 * py + pc, by_cols[a])
            return pltpu.make_async_remote_copy(
                src_ref=dst if src is None else src, dst_ref=dst, send_sem=send_sems.at[a * N_PEER_COPIES + k],
                recv_sem=recv_sems.at[a * N_PEER_COPIES + k], device_id=to, device_id_type=MESH)

        mine = [pltpu.make_async_copy(x_refs[a], _blk(out_refs[a], 4 * x + 2 * y + c, by_cols[a]), local_sems.at[a])
                for a in range(n)]
        started = []
        for a in range(n):
            mine[a].start()
            first = [copy(a, 0, me, sibling, src=x_refs[a])]
            first += [copy(a, 1 + j, me, (*chip, c), src=x_refs[a]) for j, chip in enumerate(chips)]
            for cp in first:
                cp.start()
            started += first
        for j, chip in enumerate(chips):
            for a in range(n):
                copy(a, 1 + j, (*chip, c), me).wait_recv()
                cp = copy(a, 4 + j, (*chip, c), sibling)
                cp.start()
                started.append(cp)
        for a in range(n):
            copy(a, 0, sibling, me).wait_recv()
            for j, chip in enumerate(chips):
                copy(a, 4 + j, (*chip, 1 - c), me).wait_recv()
        for cp in started:
            cp.wait_send()
        for cp in mine:
            cp.wait()

    return _comm_call("all_gather_weights", body, arrs,
                      [jax.ShapeDtypeStruct(_blocked_shape(a, bc), a.dtype) for a, bc in zip(arrs, by_cols)])


def blocks_to_cols(arrs):
    def body(*refs):
        for i, o in zip(refs[:len(arrs)], refs[len(arrs):]):
            o[...] = i[...]

    return pl.pallas_call(
        body, name="blocks_to_cols", grid=(N_DEV,),
        in_specs=[pl.BlockSpec((None,) + a.shape[1:], lambda p: (p, 0, 0)) for a in arrs],
        out_specs=[pl.BlockSpec(a.shape[1:], lambda p: (0, p)) for a in arrs],
        out_shape=[jax.ShapeDtypeStruct((a.shape[1], N_DEV * a.shape[2]), a.dtype) for a in arrs],
        compiler_params=_cparams(("parallel",)),
    )(*arrs)


def _landing_shape(a, by_cols):
    return (N_DEV, a.shape[0], a.shape[1] // N_DEV) if by_cols else a.shape


def all_to_all(arrs, by_cols):
    def body(n, *refs):
        s_refs, r_refs, (send_sems, recv_sems, local_sems) = refs[:n], refs[n:2 * n], refs[2 * n:]
        x, y, c = lax.axis_index("x"), lax.axis_index("y"), lax.axis_index("c")
        me = 4 * x + 2 * y + c

        def peer(k):
            return _flip(x, k & 4), _flip(y, k & 2), _flip(c, k & 1)

        def copy(a, k, landing):
            px, py, pc = peer(k)
            p = 4 * px + 2 * py + pc
            src, dst = (me, p) if landing else (p, me)
            return pltpu.make_async_remote_copy(
                src_ref=_blk(s_refs[a], src, by_cols[a]), dst_ref=r_refs[a].at[dst],
                send_sem=send_sems.at[a * N_PEER_COPIES + k - 1],
                recv_sem=recv_sems.at[a * N_PEER_COPIES + k - 1], device_id=(px, py, pc), device_id_type=MESH)

        mine = [pltpu.make_async_copy(_blk(s_refs[a], me, by_cols[a]), r_refs[a].at[me], local_sems.at[a]) for a in range(n)]
        sends = [copy(a, k, False) for a in range(n) for k in range(1, N_DEV)]
        for cp in mine + sends:
            cp.start()
        for a in range(n):
            for k in range(1, N_DEV):
                copy(a, k, True).wait_recv()
        for cp in sends:
            cp.wait_send()
        for cp in mine:
            cp.wait()

    return _comm_call("all_to_all_grads", body, arrs,
                      [jax.ShapeDtypeStruct(_landing_shape(a, bc), a.dtype) for a, bc in zip(arrs, by_cols)])


_HBM = pl.BlockSpec(memory_space=pltpu.HBM)
_SEM = pl.BlockSpec(memory_space=pltpu.SEMAPHORE)
_EFFECT = pltpu.SideEffectType.DATAFLOW_SIDE_EFFECTING


def _peer(k):
    x, y, c = lax.axis_index("x"), lax.axis_index("y"), lax.axis_index("c")
    return _flip(x, k & 4), _flip(y, k & 2), _flip(c, k & 1)


def _my_index():
    return 4 * lax.axis_index("x") + 2 * lax.axis_index("y") + lax.axis_index("c")


def _split_copies(mode, by_cols, src_refs, land_refs, send_sems, recv_sems):
    me = _my_index()
    out = []
    for a, bc in enumerate(by_cols):
        for k in range(1, N_DEV):
            px, py, pc = _peer(k)
            src = _blk(src_refs[a], 4 * px + 2 * py + pc, bc) if mode == 'scatter' else src_refs[a]
            dst = land_refs[a].at[me] if mode == 'scatter' else _blk(land_refs[a], me, bc)
            out.append(pltpu.make_async_remote_copy(
                src_ref=src, dst_ref=dst, send_sem=send_sems.at[a * N_PEER_COPIES + k - 1],
                recv_sem=recv_sems.at[a * N_PEER_COPIES + k - 1], device_id=(px, py, pc), device_id_type=MESH))
    return out


def exchange_start(name, mode, srcs, by_cols):
    n = len(srcs)
    lands = [lax.empty(_landing_shape(s, bc) if mode == 'scatter' else _blocked_shape(s, bc), s.dtype)
             for s, bc in zip(srcs, by_cols)]

    def body(*refs):
        src_refs, land_refs, send_sems, recv_sems = refs[:n], refs[n:2 * n], refs[2 * n], refs[2 * n + 1]
        for cp in _split_copies(mode, by_cols, src_refs, land_refs, send_sems, recv_sems):
            cp.start()
        refs[-1][...] = jnp.zeros(refs[-1].shape, F32)

    sems = pltpu.SemaphoreType.DMA((n * N_PEER_COPIES,))
    res = pl.pallas_call(
        body, name=name,
        out_shape=(sems, sems, *[pltpu.HBM(a.shape, a.dtype) for a in srcs + lands], jax.ShapeDtypeStruct((8, LANES), F32)),
        in_specs=(_HBM,) * (2 * n), out_specs=(_SEM, _SEM, *(_HBM,) * (2 * n), pl.BlockSpec(memory_space=pltpu.VMEM)),
        input_output_aliases={i: 2 + i for i in range(2 * n)},
        compiler_params=pltpu.CompilerParams(has_side_effects=_EFFECT),
    )(*[pltpu.with_memory_space_constraint(a, pltpu.HBM) for a in srcs + lands])
    return (mode, by_cols, res[:-1]), res[-1]


def exchange_wait(name, handles, after):
    mode, by_cols, (send_sems, recv_sems, *bufs) = handles
    n = len(by_cols)

    def body(*refs):
        src_refs, land_refs, s_sems, r_sems = refs[:n], refs[n:2 * n], refs[2 * n], refs[2 * n + 1]
        for cp in _split_copies(mode, by_cols, src_refs, land_refs, s_sems, r_sems):
            cp.wait_send()
            cp.wait_recv()

    res = pl.pallas_call(
        body, name=name, out_shape=tuple(pltpu.HBM(a.shape, a.dtype) for a in bufs),
        in_specs=(*(_HBM,) * (2 * n), _SEM, _SEM, pl.BlockSpec(memory_space=pl.ANY)), out_specs=(_HBM,) * (2 * n),
        input_output_aliases={i: i for i in range(2 * n)},
        compiler_params=pltpu.CompilerParams(has_side_effects=_EFFECT),
    )(*bufs, send_sems, recv_sems, after)
    me, out = _my_index(), []
    for src, land, bc in zip(res[:n], res[n:], by_cols):
        if mode == 'scatter':
            c = land.shape[2]
            own = lax.dynamic_slice(src, (0, me * c), (src.shape[0], c)) if bc else lax.dynamic_index_in_dim(src, me, 0, False)
            out.append(lax.dynamic_update_slice(land, own[None], (me, 0, 0)))
        elif bc:
            out.append(lax.dynamic_update_slice(land, src, (0, me * src.shape[1])))
        else:
            out.append(lax.dynamic_update_slice(land, src[None], (me, 0, 0)))
    return out


def _adamw_math(g, w, m, v):
    c1 = 1.0 / (1.0 - ADAM_B1 ** ADAM_STEP)
    c2 = 1.0 / (1.0 - ADAM_B2 ** ADAM_STEP)
    m = ADAM_B1 * m + (1.0 - ADAM_B1) * g
    v = ADAM_B2 * v + (1.0 - ADAM_B2) * jnp.square(g)
    return g, -ADAM_LR * ((m * c1) / (jnp.sqrt(v * c2) + ADAM_EPS) + ADAM_WD * w), m, v


def adamw(name, recv, w, m, v, tm):
    rows, cols = w.shape
    tm = min(tm, rows)

    def body(*refs):
        g = refs[0][0:tm, 0:cols].astype(F32)
        for s in range(1, N_DEV):
            g = g + refs[s][0:tm, 0:cols].astype(F32)
        res = _adamw_math(g, *[r[...] for r in refs[N_DEV:N_DEV + 3]])
        for r, val in zip(refs[N_DEV + 3:], res):
            r[...] = val

    part = lambda s: pl.BlockSpec((None, recv.shape[1] if tm == rows else tm, recv.shape[2]), lambda i: (s, i, 0))
    tile = pl.BlockSpec((tm, cols), lambda i: (i, 0))
    return pl.pallas_call(
        body, name=name, grid=(rows // tm,), in_specs=[part(s) for s in range(N_DEV)] + [tile] * 3, out_specs=[tile] * 4,
        out_shape=[jax.ShapeDtypeStruct((rows, cols), F32)] * 4, compiler_params=_cparams(("parallel",)),
    )(*[recv] * N_DEV, w, m, v)


def adamw_small(recv, wl, ml, vl):
    n = len(REPLICATED)

    def body(recv_ref, *refs):
        g = recv_ref[0]
        for s in range(1, N_DEV):
            g = g + recv_ref[s]
        for r in range(n):
            w, m, v = (refs[j * n + r][...] for j in range(3))
            for j, val in enumerate(_adamw_math(g[r:r + 1, :w.shape[1]], w, m, v)):
                refs[(3 + j) * n + r][...] = val

    arrs = [d[k].reshape(1, -1) for d in (wl, ml, vl) for k in REPLICATED]
    res = pl.pallas_call(
        body, name="adamw_small", out_shape=[jax.ShapeDtypeStruct(a.shape, F32) for a in arrs[:n]] * 4,
    )(recv, *arrs)
    return [{k: res[j * n + r].reshape(wl[k].shape) for r, k in enumerate(REPLICATED)} for j in range(4)]


ADAMW_TM = {'ffn1_gate': 256, 'ffn1_up': 256, 'ffn2_gate': 256, 'ffn2_up': 256, 'w_in': 32}


def kernel(x, positions, ln1_g, ln1_b, ffn1_gate, ffn1_up, ffn1_down, w_in, conv_w, conv_b, dt_bias, a_log, d_skip, attn_norm_w, ssd_norm_w, w_out, ln2_g, ln2_b, ffn2_gate, ffn2_up, ffn2_down, ln3_g, ln3_b, loss_target, m_ln1_g, m_ln1_b, m_ffn1_gate, m_ffn1_up, m_ffn1_down, m_w_in, m_conv_w, m_conv_b, m_dt_bias, m_a_log, m_d_skip, m_attn_norm_w, m_ssd_norm_w, m_w_out, m_ln2_g, m_ln2_b, m_ffn2_gate, m_ffn2_up, m_ffn2_down, m_ln3_g, m_ln3_b, v_ln1_g, v_ln1_b, v_ffn1_gate, v_ffn1_up, v_ffn1_down, v_w_in, v_conv_w, v_conv_b, v_dt_bias, v_a_log, v_d_skip, v_attn_norm_w, v_ssd_norm_w, v_w_out, v_ln2_g, v_ln2_b, v_ffn2_gate, v_ffn2_up, v_ffn2_down, v_ln3_g, v_ln3_b):
    args = dict(locals())
    wl = {k: args[k] for k in WEIGHTS}
    ml = {k: args["m_" + k] for k in WEIGHTS}
    vl = {k: args["v_" + k] for k in WEIGHTS}
    shapes = {k: wl[k].shape for k in WEIGHTS}
    b, s, dm = x.shape
    t = b * s

    sent = {k: to_comm(k, wl, shapes).astype(F32 if k == 'conv_w' else BF16) for k in SHARDED}
    by_cols = lambda keys: [k in FFN_COL for k in keys]
    gate, up, down = all_gather([sent[k] for k in FFN1_KEYS], [False] * 3)
    p = dict(zip(FFN1_KEYS, (*blocks_to_cols([gate, up]), full_weight('ffn1_down', down))))
    gather_mixer, token_m = exchange_start("gather_mixer_start", 'gather', [sent[k] for k in MIXER_KEYS], by_cols(MIXER_KEYS))
    sent['ffn2_gate'] = sent['ffn2_gate'] + token_m[0, 0].astype(BF16)
    gather_ffn2, token_f = exchange_start("gather_ffn2_start", 'gather', [sent[k] for k in FFN2_KEYS], by_cols(FFN2_KEYS))
    for k in REPLICATED:
        p[k] = wl[k].reshape(1, -1)

    x2 = x.reshape(t, dm)
    cosv, sinv = rope_tables(positions)
    f1, res1 = ffn_fwd("ffn1", x2, p['ffn1_gate'], p['ffn1_up'], p['ffn1_down'], after=(token_m, token_f))
    h1, h1b = resid_ln_fwd("ln1", 0.5, x2, f1, p['ln1_g'], p['ln1_b'])
    for k, g in zip(MIXER_KEYS, exchange_wait("gather_mixer_wait", gather_mixer, h1b)):
        p[k] = full_weight(k, g)
    mix, resm = mixer_fwd(h1b, p, cosv, sinv, b)
    h2, h2b = resid_ln_fwd("ln2", 1.0, h1, mix, p['ln2_g'], p['ln2_b'])
    for k, g in zip(FFN2_KEYS, exchange_wait("gather_ffn2_wait", gather_ffn2, h2b)):
        p[k] = full_weight(k, g)
    f2, res3 = ffn_fwd("ffn2", h2b, p['ffn2_gate'], p['ffn2_up'], p['ffn2_down'])

    small, full = {}, {}
    dh2_res, df2, small['ln3_g'], small['ln3_b'], sq = ln_loss_bwd("ln3_loss_bwd", h2, f2, loss_target.reshape(t, dm),
                                                                   p['ln3_g'], p['ln3_b'])
    loss = lax.psum(jnp.sum(sq) * (0.5 / dm), AXES)

    dh2, full['ffn2_gate'], full['ffn2_up'], full['ffn2_down'] = ffn_bwd("ffn2", res3, p['ffn2_gate'], p['ffn2_up'],
                                                                       p['ffn2_down'], df2, dh2_res)
    ffn2_exchange, token = exchange_start("grads_ffn2_start", 'scatter', [grad_shards(k, full[k]) for k in FFN2_KEYS],
                                          by_cols(FFN2_KEYS))
    dh1_res, dmix, small['ln2_g'], small['ln2_b'] = resid_ln_bwd("ln2_bwd", 1.0, h1, mix, p['ln2_g'] + token[:1, :1],
                                                                 p['ln2_b'], dh2)
    dh1, gm = mixer_bwd(resm, p, dmix, dh1_res, b)
    for k in ('conv_b', 'dt_bias', 'a_log', 'd_skip', 'attn_norm_w', 'ssd_norm_w'):
        small[k] = gm[k]
    mixer_exchange, token = exchange_start("grads_mixer_start", 'scatter', [grad_shards(k, gm[k]) for k in MIXER_KEYS],
                                           by_cols(MIXER_KEYS))
    dx_res, df1, small['ln1_g'], small['ln1_b'] = resid_ln_bwd("ln1_bwd", 0.5, x2, f1, p['ln1_g'] + token[:1, :1],
                                                               p['ln1_b'], dh1)
    hb, g, u, a = res1
    small_part = pack_small(small)
    dg, du = ffn_da_act("ffn1_bwd_da_act", df1, p['ffn1_down'], g, u)
    dwd = mm_tn("ffn1_bwd_dwd", a, df1, BF16, after=dg)
    down_exchange, token = exchange_start("grads_ffn1_down_start", 'scatter', [
        grad_shards('ffn1_down', dwd), jnp.broadcast_to(small_part[None], (N_DEV,) + small_part.shape)], [False, False])
    dx = mm("ffn1_bwd_dh", [(dg, p['ffn1_gate'], 'nt'), (du, p['ffn1_up'], 'nt')], D_MODEL, add=dx_res, tn=512, after=token)
    dwg = mm_tn("ffn1_bwd_dwg", hb, dg, BF16, after=dx)
    gate_exchange, token = exchange_start("grads_ffn1_gate_start", 'scatter', [grad_shards('ffn1_gate', dwg)], [True])
    dwu = mm_tn("ffn1_bwd_dwu", hb, du, BF16, after=token)
    recv = dict(zip(('ffn1_up',), all_to_all([grad_shards('ffn1_up', dwu)], [True])))
    for keys, name, ex in (((FFN2_KEYS), "grads_ffn2_wait", ffn2_exchange), (MIXER_KEYS, "grads_mixer_wait", mixer_exchange),
                           (('ffn1_down', SMALL), "grads_ffn1_down_wait", down_exchange),
                           (('ffn1_gate',), "grads_ffn1_gate_wait", gate_exchange)):
        recv.update(zip(keys, exchange_wait(name, ex, recv['ffn1_up'])))
    outs = adamw_small(recv.pop(SMALL), wl, ml, vl)
    for k, r in recv.items():
        shard = shapes[k][1:]
        res = adamw(f"adamw_{k}", r, *[d[k].reshape(shard) for d in (wl, ml, vl)], ADAMW_TM.get(k, shard[0]))
        for o, a in zip(outs, res):
            o[k] = a.reshape(shapes[k])
    return (loss, dx.reshape(b, s, dm), *[o[k] for o in outs for k in WEIGHTS])
```

```python
import functools
import math

import jax
import jax.numpy as jnp
import numpy as np
from jax import lax
from jax.experimental import pallas as pl
from jax.experimental.pallas import tpu as pltpu

F32, BF16 = jnp.float32, jnp.bfloat16
HI = lax.Precision.HIGHEST
MESH = pl.DeviceIdType.MESH
AXES = ("x", "y", "c")
N_DEV = 8

D_MODEL = 1024
SEQ = 2048
HEAD_DIM = 64
N_HEADS = 12
D_ATTN = N_HEADS * HEAD_DIM
DILATIONS = (1, 4, 16)
ATTN_BLOCK = 128
ROPE_THETA = 500000.0
ROPE_DIM = 16
D_SSD = 768
SSD_GROUPS = 4
SSD_STATE = 128
SSD_CHUNK = 128
D_BC = SSD_GROUPS * SSD_STATE
D_CONV = D_SSD + 2 * D_BC
CONV_WIDTH = 4
D_QKVZ = 3 * D_ATTN + D_SSD
D_IN_PROJ = D_QKVZ + D_CONV + N_HEADS
D_FF = 2816
ALPHA = 2.0 ** 0.25
LN_EPS = 1e-5
RMS_EPS = 1e-6
ADAM_LR, ADAM_B1, ADAM_B2, ADAM_EPS, ADAM_WD, ADAM_STEP = 0.001, 0.9, 0.999, 1e-08, 0.01, 10

LANES = 128
VMEM_LIMIT = 52 * 1024 * 1024
NEG = -1e30

WEIGHTS = ['ln1_g', 'ln1_b', 'ffn1_gate', 'ffn1_up', 'ffn1_down', 'w_in', 'conv_w', 'conv_b', 'dt_bias', 'a_log',
           'd_skip', 'attn_norm_w', 'ssd_norm_w', 'w_out', 'ln2_g', 'ln2_b', 'ffn2_gate', 'ffn2_up', 'ffn2_down',
           'ln3_g', 'ln3_b']
COL_SHARDED = ('ffn1_gate', 'ffn1_up', 'conv_w', 'ffn2_gate', 'ffn2_up')
ROW_SHARDED = ('ffn1_down', 'w_in', 'w_out', 'ffn2_down')
SHARDED = tuple(n for n in WEIGHTS if n in COL_SHARDED or n in ROW_SHARDED)
REPLICATED = tuple(n for n in WEIGHTS if n not in SHARDED)
FF_SHARD = D_FF // N_DEV
FF_PAD = -(-FF_SHARD // LANES) * LANES
D_FF_INT = N_DEV * FF_PAD


def _cparams(sem=None):
    return pltpu.CompilerParams(dimension_semantics=sem, vmem_limit_bytes=VMEM_LIMIT)


def _tile(n, prefs):
    for p in prefs:
        if n % p == 0:
            return p
    return n


class Op:
    def __init__(self, arr, bw=None, cb=0, ro=0):
        self.arr, self.bw, self.cb, self.ro = arr, (arr.shape[1] if bw is None else bw), cb, ro


def _op(a):
    return a if isinstance(a, Op) else Op(a)


def rowwise(name, fn, ins, consts, outs, accs=(), tm=256):
    ins = [_op(a) for a in ins]
    rows = outs[0][0]
    n_in, n_c, n_o, n_a = len(ins), len(consts), len(outs), len(accs)
    tm = min(tm, rows)
    assert rows % tm == 0, (name, rows, tm)

    def body(*refs):
        vals = [r[...].astype(F32) for r in refs[:n_in + n_c]]
        res = fn(*vals)
        res = res if isinstance(res, (tuple, list)) else (res,)
        o_refs = refs[n_in + n_c:n_in + n_c + n_o]
        a_refs = refs[n_in + n_c + n_o:]
        for r, v in zip(o_refs, res[:n_o]):
            r[...] = v.astype(r.dtype)
        if n_a:
            @pl.when(pl.program_id(0) == 0)
            def _():
                for r in a_refs:
                    r[...] = jnp.zeros(r.shape, r.dtype)
            for r, v in zip(a_refs, res[n_o:]):
                r[...] += v

    in_specs = [pl.BlockSpec((tm, o.bw), functools.partial(lambda i, o: (i + o.ro, o.cb), o=o)) for o in ins]
    in_specs += [pl.BlockSpec(c.shape, functools.partial(lambda i, nd: (0,) * nd, nd=c.ndim)) for c in consts]
    out_specs = [pl.BlockSpec((tm, w), lambda i: (i, 0)) for (_, w, _) in outs]
    out_specs += [pl.BlockSpec(s, functools.partial(lambda i, nd: (0,) * nd, nd=len(s))) for s in accs]
    out_shape = [jax.ShapeDtypeStruct((r, w), dt) for (r, w, dt) in outs]
    out_shape += [jax.ShapeDtypeStruct(s, F32) for s in accs]
    res = pl.pallas_call(
        body, name=name, grid=(rows // tm,), in_specs=in_specs, out_specs=out_specs, out_shape=out_shape,
        compiler_params=_cparams(("arbitrary",) if n_a else ("parallel",)),
    )(*[o.arr for o in ins], *consts)
    return res


MM_TM = 512
MM_TN = (1024, 896, 768, 512, 256, 128)
_NT = (((1,), (1,)), ((), ()))
_NN = (((1,), (0,)), ((), ()))
_TN = (((0,), (0,)), ((), ()))


def _dot(a, b, dn, precision=None):
    return lax.dot_general(a, b, dn, preferred_element_type=F32, precision=precision)


def _mm_specs(name, pairs, n_out, tm, tn):
    in_specs, args = [], []
    for a, b, mode in pairs:
        o = _op(a)
        in_specs.append(pl.BlockSpec((tm, o.bw), functools.partial(lambda j, i, o: (i, o.cb), o=o)))
        args.append(o.arr)
        if mode == 'nn':
            assert b.shape == (o.bw, n_out), (name, b.shape, o.bw, n_out)
            in_specs.append(pl.BlockSpec((o.bw, tn), lambda j, i: (0, j)))
        else:
            assert b.shape == (n_out, o.bw), (name, b.shape, o.bw, n_out)
            in_specs.append(pl.BlockSpec((tn, o.bw), lambda j, i: (j, 0)))
        args.append(b)
    return in_specs, args


def _mm_acc(refs, pairs):
    acc = None
    for k, (_, _, mode) in enumerate(pairs):
        d = _dot(refs[2 * k][...].astype(BF16), refs[2 * k + 1][...].astype(BF16), _NN if mode == 'nn' else _NT)
        acc = d if acc is None else acc + d
    return acc


def mm(name, pairs, n_out, add=None, out_dtype=F32, tm=MM_TM, tn=None, after=None):
    m = _op(pairs[0][0]).arr.shape[0]
    tn = tn or _tile(n_out, MM_TN)
    n_p = len(pairs)

    def body(*refs):
        acc = _mm_acc(refs, pairs)
        if add is not None:
            acc = acc + refs[2 * n_p][...]
        refs[-1][...] = acc.astype(refs[-1].dtype)

    in_specs, args = _mm_specs(name, pairs, n_out, tm, tn)
    tile = pl.BlockSpec((tm, tn), lambda j, i: (i, j))
    if add is not None:
        in_specs.append(tile)
        args.append(add)
    if after is not None:
        in_specs.append(pl.BlockSpec(memory_space=pl.ANY))
        args.append(after)
    return pl.pallas_call(
        body, name=name, grid=(n_out // tn, m // tm), in_specs=in_specs, out_specs=tile,
        out_shape=jax.ShapeDtypeStruct((m, n_out), out_dtype),
        compiler_params=_cparams(("parallel", "parallel")),
    )(*args)


def mm_tn(name, a, b, out_dtype=F32, tt=1024, after=None):
    a, b = _op(a), _op(b)
    t = a.arr.shape[0]
    k, n = a.bw, b.bw
    tk = _tile(k, (512, 896, 768, 256, 128))
    tn = _tile(n, (3072, 1792) + MM_TN)
    tt = min(tt, t)
    n_t = t // tt
    order = [] if after is None else [after]

    def body(a_ref, b_ref, *rest):
        o_ref, acc_ref = rest[-2:]
        s = pl.program_id(2)
        d = _dot(a_ref[...].astype(BF16), b_ref[...].astype(BF16), _TN)

        @pl.when(s == 0)
        def _():
            acc_ref[...] = d

        @pl.when(s > 0)
        def _():
            acc_ref[...] += d

        @pl.when(s == n_t - 1)
        def _():
            o_ref[...] = acc_ref[...].astype(o_ref.dtype)

    return pl.pallas_call(
        body, name=name, grid=(k // tk, n // tn, n_t),
        in_specs=[pl.BlockSpec((tt, tk), functools.partial(lambda kk, nn, s, o: (s, o.cb * (o.bw // tk) + kk), o=a)),
                  pl.BlockSpec((tt, tn), functools.partial(lambda kk, nn, s, o: (s, o.cb * (o.bw // tn) + nn), o=b))]
        + [pl.BlockSpec(memory_space=pl.ANY) for _ in order],
        out_specs=pl.BlockSpec((tk, tn), lambda kk, nn, s: (kk, nn)),
        out_shape=jax.ShapeDtypeStruct((k, n), out_dtype),
        scratch_shapes=[pltpu.VMEM((tk, tn), F32)],
        compiler_params=_cparams(("parallel", "parallel", "arbitrary")),
    )(a.arr, b.arr, *order)


def _sigmoid(x):
    return 1.0 / (1.0 + jnp.exp(-x))


def _silu(x):
    return x * _sigmoid(x)


def _softplus(x):
    return jnp.maximum(x, 0.0) + jnp.log(1.0 + jnp.exp(-jnp.abs(x)))


def _act(g, u):
    return _silu(g) * u


def _resid_ln(scale, h, branch, g, b):
    r = ALPHA * h + scale * branch
    mu = jnp.mean(r, axis=-1, keepdims=True)
    var = jnp.mean(jnp.square(r - mu), axis=-1, keepdims=True)
    return (r - mu) * lax.rsqrt(var + LN_EPS) * g + b


def _rms(t, w):
    return t * lax.rsqrt(jnp.mean(t * t, axis=-1, keepdims=True) + RMS_EPS) * w


def _branch_weights(l1, l2, l3):
    m = jnp.maximum(jnp.maximum(l1, l2), l3)
    e1, e2, e3 = jnp.exp(l1 - m), jnp.exp(l2 - m), jnp.exp(l3 - m)
    inv = 1.0 / (e1 + e2 + e3)
    return e1 * inv, e2 * inv, e3 * inv


def _gate(y, xs, z, dskip, w):
    return _rms((y + dskip * xs) * _silu(z), w)


def _rot(x):
    d = lax.broadcasted_iota(jnp.int32, x.shape, 1) % HEAD_DIM
    up = pltpu.roll(x, x.shape[1] - ROPE_DIM // 2, 1)
    down = jnp.where(d < ROPE_DIM, pltpu.roll(x, ROPE_DIM // 2, 1), 0.0)
    return jnp.where(d < ROPE_DIM // 2, up, down)


def ffn_gate_up(name, h, wg, wu, after=()):
    m, nf = h.shape[0], wg.shape[1]
    tn = _tile(nf, MM_TN)

    def body(h_ref, g_w, u_w, *rest):
        g_ref, u_ref, a_ref = rest[-3:]
        hb = h_ref[...].astype(BF16)
        g = _dot(hb, g_w[...].astype(BF16), _NN)
        u = _dot(hb, u_w[...].astype(BF16), _NN)
        g_ref[...] = g.astype(g_ref.dtype)
        u_ref[...] = u.astype(u_ref.dtype)
        a_ref[...] = _act(g, u).astype(a_ref.dtype)

    in_specs, args = _mm_specs(name, [(h, wg, 'nn')], nf, MM_TM, tn)
    in_specs.append(in_specs[1])
    in_specs += [pl.BlockSpec(memory_space=pl.ANY) for _ in after]
    tile = pl.BlockSpec((MM_TM, tn), lambda j, i: (i, j))
    return pl.pallas_call(
        body, name=name, grid=(nf // tn, m // MM_TM), in_specs=in_specs, out_specs=[tile] * 3,
        out_shape=[jax.ShapeDtypeStruct((m, nf), BF16)] * 3, compiler_params=_cparams(("parallel", "parallel")),
    )(*args, wu, *after)


def ffn_da_act(name, df, wd, g, u):
    m, nf = df.shape[0], wd.shape[0]
    tn = _tile(nf, MM_TN)

    def body(df_ref, w_ref, g_ref, u_ref, dg_ref, du_ref):
        da = _dot(df_ref[...].astype(BF16), w_ref[...].astype(BF16), _NT)
        _, vjp = jax.vjp(_act, g_ref[...].astype(F32), u_ref[...].astype(F32))
        dg, du = vjp(da)
        dg_ref[...] = dg.astype(dg_ref.dtype)
        du_ref[...] = du.astype(du_ref.dtype)

    in_specs, args = _mm_specs(name, [(df, wd, 'nt')], nf, MM_TM, tn)
    tile = pl.BlockSpec((MM_TM, tn), lambda j, i: (i, j))
    return pl.pallas_call(
        body, name=name, grid=(nf // tn, m // MM_TM), in_specs=in_specs + [tile, tile], out_specs=[tile] * 2,
        out_shape=[jax.ShapeDtypeStruct((m, nf), BF16)] * 2, compiler_params=_cparams(("parallel", "parallel")),
    )(*args, g, u)


def resid_ln_fwd(name, scale, h, branch, ln_g, ln_b):
    t = h.shape[0]

    def fn(*a):
        y = _resid_ln(scale, *a)
        return y, y

    return rowwise(name, fn, [h, branch], [ln_g, ln_b], [(t, D_MODEL, F32), (t, D_MODEL, BF16)], tm=512)


def ffn_fwd(tag, hb, wg, wu, wd, after=()):
    g, u, a = ffn_gate_up(f"{tag}_gate_up", hb, wg, wu, after)
    f = mm(f"{tag}_down", [(a, wd, 'nn')], D_MODEL)
    return f, (hb, g, u, a)


def ln_loss_bwd(name, h, branch, target, ln_g, ln_b):
    t, dm = h.shape

    def fn(h_, br_, tgt, g_, b_):
        y, vjp = jax.vjp(functools.partial(_resid_ln, 0.5), h_, br_, g_, b_)
        e = y - tgt
        return (*vjp(e * (1.0 / dm)), jnp.sum(e * e, axis=0, keepdims=True))

    return rowwise(name, fn, [h, branch, target], [ln_g, ln_b], [(t, dm, F32), (t, dm, F32)],
                   accs=[(1, dm), (1, dm), (1, dm)], tm=512)


def resid_ln_bwd(name, scale, h, branch, ln_g, ln_b, dout, extra=None):
    t = h.shape[0]

    def fn(h_, br_, do_, *rest):
        g_, b_ = rest[-2], rest[-1]
        _, vjp = jax.vjp(functools.partial(_resid_ln, scale), h_, br_, g_, b_)
        dh, dbr, dg, db = vjp(do_)
        if extra is not None:
            dh = dh + rest[0]
        return dh, dbr, dg, db

    ins = [h, branch, dout] + ([extra] if extra is not None else [])
    return rowwise(name, fn, ins, [ln_g, ln_b], [(t, D_MODEL, F32), (t, D_MODEL, F32)],
                   accs=[(1, D_MODEL), (1, D_MODEL)], tm=512)


def ffn_bwd(tag, res, wg, wu, wd, df, dh_resid):
    hb, g, u, a = res
    dg, du = ffn_da_act(f"{tag}_bwd_da_act", df, wd, g, u)
    dwd = mm_tn(f"{tag}_bwd_dwd", a, df, BF16)
    dh = mm(f"{tag}_bwd_dh", [(dg, wg, 'nt'), (du, wu, 'nt')], D_MODEL, add=dh_resid, tn=512)
    dwg = mm_tn(f"{tag}_bwd_dwg", hb, dg, BF16)
    dwu = mm_tn(f"{tag}_bwd_dwu", hb, du, BF16)
    return dh, dwg, dwu, dwd


def rope_tables(positions):
    inv_freq = ROPE_THETA ** (-jnp.arange(0, ROPE_DIM, 2, dtype=F32) / ROPE_DIM)
    ang = positions.reshape(-1, 1).astype(F32) * inv_freq
    c, s = jnp.cos(ang), jnp.sin(ang)
    t = ang.shape[0]
    cosv = jnp.concatenate([c, c, jnp.ones((t, HEAD_DIM - ROPE_DIM), F32)], axis=1)
    sinv = jnp.concatenate([-s, s, jnp.zeros((t, HEAD_DIM - ROPE_DIM), F32)], axis=1)
    return jnp.tile(cosv, (1, 2)), jnp.tile(sinv, (1, 2))


def _pair_masks():
    lane = lax.broadcasted_iota(jnp.int32, (1, LANES), 1)
    return (lane < HEAD_DIM, lane >= HEAD_DIM)


def _band_masks():
    row = lax.broadcasted_iota(jnp.int32, (ATTN_BLOCK, ATTN_BLOCK), 0)
    col = lax.broadcasted_iota(jnp.int32, (ATTN_BLOCK, ATTN_BLOCK), 1)
    return col >= row, col <= row


def _residue_blocks():
    out = []
    for g, d in enumerate(DILATIONS):
        for r in range(d):
            for i in range(SEQ // d // ATTN_BLOCK):
                rows = lambda j: pl.ds(r + j * ATTN_BLOCK * d, ATTN_BLOCK, stride=d) if d > 1 else pl.ds(j * ATTN_BLOCK, ATTN_BLOCK)
                out.append((g, rows(i), rows(i - 1) if i > 0 else None))
    return out


N_HEAD_PAIRS = D_ATTN // LANES
SCALE = HEAD_DIM ** -0.5
ATTN_GROUP = 4


def _block_operands(qr, kr, v_ref, cur, prev):
    prev_ok, cur_ok = _band_masks()
    if prev is None:
        return qr[cur, :], kr[cur, :].astype(BF16), v_ref[cur, :], cur_ok
    kcat = jnp.concatenate([kr[prev, :], kr[cur, :]], axis=0).astype(BF16)
    vcat = jnp.concatenate([v_ref[prev, :], v_ref[cur, :]], axis=0)
    return qr[cur, :], kcat, vcat, jnp.concatenate([prev_ok, cur_ok], axis=1)


def _attn_specs(b):
    col = lambda cb: pl.BlockSpec((SEQ, LANES), lambda bb, hp: (bb, cb + hp))
    tab = pl.BlockSpec((SEQ, LANES), lambda bb, hp: (bb, 0))
    return col, tab


def attn_fwd(qkvz, cosv, sinv, b):
    t = qkvz.shape[0]
    col, tab = _attn_specs(b)
    blocks = _residue_blocks()

    def body(q_ref, k_ref, v_ref, c_ref, s_ref, o_ref, l1_ref, l2_ref, l3_ref, qr, kr, o1, o2, o3):
        l_refs, o_scr = (l1_ref, l2_ref, l3_ref), (o1, o2, o3)
        c, s = c_ref[...], s_ref[...]
        q, k = q_ref[...], k_ref[...]
        qr[...] = q * c + _rot(q) * s
        kr[...] = k * c + _rot(k) * s
        masks = _pair_masks()
        for lo in range(0, len(blocks), ATTN_GROUP):
            chains = []
            for g, cur, prev in blocks[lo:lo + ATTN_GROUP]:
                q2, kcat, vcat, ok = _block_operands(qr, kr, v_ref, cur, prev)
                for m in masks:
                    qm = jnp.where(m, q2, 0.0).astype(BF16)
                    chains.append(dict(g=g, cur=cur, m=m, v=jnp.where(m, vcat, 0.0).astype(BF16),
                                       s=jnp.where(ok, _dot(qm, kcat, _NT) * SCALE, NEG)))
            for ch in chains:
                mx = jnp.max(ch['s'], axis=1, keepdims=True)
                p = jnp.exp(ch['s'] - mx)
                den = jnp.sum(p, axis=1, keepdims=True)
                ch.update(p=p.astype(BF16), inv=1.0 / den, lse=mx + jnp.log(den))
            for ch in chains:
                ch['o'] = _dot(ch['p'], ch['v'], _NN) * ch['inv']
            for c0, c1 in zip(chains[0::2], chains[1::2]):
                o_scr[c0['g']][c0['cur'], :] = c0['o'] + c1['o']
                l_refs[c0['g']][c0['cur'], :] = jnp.where(c0['m'], c0['lse'], c1['lse'])
        w1, w2, w3 = _branch_weights(l1_ref[...], l2_ref[...], l3_ref[...])
        o_ref[...] = w1 * o1[...] + w2 * o2[...] + w3 * o3[...]

    shp = jax.ShapeDtypeStruct((t, D_ATTN), F32)
    return pl.pallas_call(
        body, name="attn_fwd", grid=(b, N_HEAD_PAIRS),
        in_specs=[col(0), col(N_HEAD_PAIRS), col(2 * N_HEAD_PAIRS), tab, tab],
        out_specs=[col(0)] * 4, out_shape=[shp] * 4,
        scratch_shapes=[pltpu.VMEM((SEQ, LANES), F32)] * 5,
        compiler_params=_cparams(("parallel", "parallel")),
    )(qkvz, qkvz, qkvz, cosv, sinv)


def attn_bwd(qkvz, cosv, sinv, dmix, mixed, lses, b):
    t = qkvz.shape[0]
    col, tab = _attn_specs(b)
    blocks = _residue_blocks()
    hd = np.arange(LANES) // HEAD_DIM
    head_ones = jnp.asarray((hd[:, None] == hd[None, :]).astype(np.float32))

    def body(q_ref, k_ref, v_ref, c_ref, s_ref, dm_ref, mx_ref, l1_ref, l2_ref, l3_ref, ones_ref,
             dq_out, dk_out, dv_out, qr, kr, do1, do2, do3, dd1, dd2, dd3, dq_ref, dk_ref, dv_ref):
        l_refs, do_scr, dd_scr = (l1_ref, l2_ref, l3_ref), (do1, do2, do3), (dd1, dd2, dd3)
        c, s = c_ref[...], s_ref[...]
        q, k = q_ref[...], k_ref[...]
        qr[...] = q * c + _rot(q) * s
        kr[...] = k * c + _rot(k) * s
        dm = dm_ref[...]
        tot = _dot(dm * mx_ref[...], ones_ref[...], _NN, HI)
        for w, do_g, dd_g in zip(_branch_weights(l1_ref[...], l2_ref[...], l3_ref[...]), do_scr, dd_scr):
            do_g[...] = w * dm
            dd_g[...] = w * tot
        dq_ref[...] = jnp.zeros((SEQ, LANES), F32)
        dk_ref[...] = jnp.zeros((SEQ, LANES), F32)
        dv_ref[...] = jnp.zeros((SEQ, LANES), F32)
        masks = _pair_masks()
        for lo in range(0, len(blocks), ATTN_GROUP):
            chains = []
            for g, cur, prev in blocks[lo:lo + ATTN_GROUP]:
                q2, kcat, vcat, ok = _block_operands(qr, kr, v_ref, cur, prev)
                vcat = vcat.astype(BF16)
                do2_, l2, dd2_ = do_scr[g][cur, :], l_refs[g][cur, :], dd_scr[g][cur, :]
                l2s, dd2s = pltpu.roll(l2, HEAD_DIM, 1), pltpu.roll(dd2_, HEAD_DIM, 1)
                for m in masks:
                    qm = jnp.where(m, q2, 0.0).astype(BF16)
                    dom = jnp.where(m, do2_, 0.0).astype(BF16)
                    lrep, ddrep = jnp.where(m, l2, l2s), jnp.where(m, dd2_, dd2s)
                    if prev is not None:
                        lrep, ddrep = jnp.concatenate([lrep, lrep], axis=1), jnp.concatenate([ddrep, ddrep], axis=1)
                    chains.append(dict(cur=cur, prev=prev, qm=qm, dom=dom, km=jnp.where(m, kcat, 0), lrep=lrep, ddrep=ddrep,
                                       s=jnp.where(ok, _dot(qm, kcat, _NT) * SCALE, NEG), dp=_dot(dom, vcat, _NT)))
            for ch in chains:
                p = jnp.exp(ch['s'] - ch['lrep'])
                ch.update(p=p.astype(BF16), ds=(p * (ch['dp'] - ch['ddrep']) * SCALE).astype(BF16))
            for ch in chains:
                ch.update(dq=_dot(ch['ds'], ch['km'], _NN), dk=_dot(ch['ds'], ch['qm'], _TN), dv=_dot(ch['p'], ch['dom'], _TN))
            for c0, c1 in zip(chains[0::2], chains[1::2]):
                cur, prev = c0['cur'], c0['prev']
                dk, dv = c0['dk'] + c1['dk'], c0['dv'] + c1['dv']
                dq_ref[cur, :] += c0['dq'] + c1['dq']
                if prev is None:
                    dk_ref[cur, :] += dk
                    dv_ref[cur, :] += dv
                else:
                    dk_ref[prev, :] += dk[:ATTN_BLOCK]
                    dv_ref[prev, :] += dv[:ATTN_BLOCK]
                    dk_ref[cur, :] += dk[ATTN_BLOCK:]
                    dv_ref[cur, :] += dv[ATTN_BLOCK:]
        dq, dk = dq_ref[...], dk_ref[...]
        dq_out[...] = (dq * c + _rot(dq * s)).astype(dq_out.dtype)
        dk_out[...] = (dk * c + _rot(dk * s)).astype(dk_out.dtype)
        dv_out[...] = dv_ref[...].astype(dv_out.dtype)

    shp = jax.ShapeDtypeStruct((t, D_ATTN), BF16)
    return pl.pallas_call(
        body, name="attn_bwd", grid=(b, N_HEAD_PAIRS),
        in_specs=[col(0), col(N_HEAD_PAIRS), col(2 * N_HEAD_PAIRS), tab, tab, col(0), col(0), col(0), col(0), col(0),
                  pl.BlockSpec((LANES, LANES), lambda bb, hp: (0, 0))],
        out_specs=[col(0)] * 3, out_shape=[shp] * 3,
        scratch_shapes=[pltpu.VMEM((SEQ, LANES), F32)] * 11,
        compiler_params=_cparams(("parallel", "parallel")),
    )(qkvz, qkvz, qkvz, cosv, sinv, dmix, mixed, *lses, head_ones)


def attn_norm_fwd(mixed, norm_w):
    return rowwise("attn_norm", _rms, [mixed], [norm_w], [(mixed.shape[0], D_ATTN, BF16)])[0]


def attn_norm_bwd(dout, mixed, norm_w):
    def fn(dy, mx, w):
        _, vjp = jax.vjp(_rms, mx, w)
        return vjp(dy)

    return rowwise("attn_norm_bwd", fn, [dout, mixed], [norm_w], [(dout.shape[0], D_ATTN, F32)], accs=[(1, D_ATTN)])


CONV_TM = 256
HALO = 8


def _conv_columns(refs):
    xs_ref, bm_ref, cm_ref = refs
    out = []
    for c in range(D_CONV // LANES):
        lo = c * LANES
        ref, base = (xs_ref, 0) if lo < D_SSD else (bm_ref, D_SSD) if lo < D_SSD + D_BC else (cm_ref, D_SSD + D_BC)
        out.append((slice(lo, lo + LANES), (ref, slice(lo - base, lo - base + LANES))))
    return out


def _conv_taps(scr, w_ref, cs, first_row, step, tm):
    acc = None
    for k in range(CONV_WIDTH):
        term = w_ref[k:k + 1, cs] * scr[pl.ds(first_row + step * k, tm), cs]
        acc = term if acc is None else acc + term
    return acc


def conv_fwd(u, w, bias):
    t = u.shape[0]
    tm, per_seq = CONV_TM, SEQ // CONV_TM

    def body(u_ref, h_ref, w_ref, b_ref, xs_ref, bm_ref, cm_ref, scr):
        first = pl.program_id(0) % per_seq == 0
        scr[0:HALO, :] = jnp.where(first, 0.0, h_ref[...])
        scr[HALO:, :] = u_ref[...]
        for cs, (o_ref, os_) in _conv_columns((xs_ref, bm_ref, cm_ref)):
            o_ref[:, os_] = _silu(_conv_taps(scr, w_ref, cs, HALO - CONV_WIDTH + 1, 1, tm) + b_ref[:, cs])

    return pl.pallas_call(
        body, name="conv_fwd", grid=(t // tm,),
        in_specs=[pl.BlockSpec((tm, D_CONV), lambda i: (i, 0)),
                  pl.BlockSpec((HALO, D_CONV), lambda i: (jnp.maximum(i * (tm // HALO) - 1, 0), 0)),
                  pl.BlockSpec((CONV_WIDTH, D_CONV), lambda i: (0, 0)), pl.BlockSpec((1, D_CONV), lambda i: (0, 0))],
        out_specs=[pl.BlockSpec((tm, D_SSD), lambda i: (i, 0)), pl.BlockSpec((tm, D_BC), lambda i: (i, 0)),
                   pl.BlockSpec((tm, D_BC), lambda i: (i, 0))],
        out_shape=[jax.ShapeDtypeStruct((t, D_SSD), F32), jax.ShapeDtypeStruct((t, D_BC), F32),
                   jax.ShapeDtypeStruct((t, D_BC), F32)],
        scratch_shapes=[pltpu.VMEM((tm + HALO, D_CONV), F32)],
        compiler_params=_cparams(("parallel",)),
    )(u, u, w, bias)


def conv_bwd(u, w, bias, dxs_a, dxs_b, dbm, dcm):
    t = u.shape[0]
    tm, per_seq = CONV_TM, SEQ // CONV_TM
    n_tiles = t // tm

    def body1(u_ref, h_ref, dxs_ref, dxs2_ref, dbm_ref, dcm_ref, w_ref, b_ref, dz_ref, dw_ref, db_ref, scr):
        i = pl.program_id(0)
        first = i % per_seq == 0
        scr[0:HALO, :] = jnp.where(first, 0.0, h_ref[...])
        scr[HALO:, :] = u_ref[...]

        @pl.when(i == 0)
        def _():
            dw_ref[...] = jnp.zeros(dw_ref.shape, F32)
            db_ref[...] = jnp.zeros(db_ref.shape, F32)
        for cs, (g_ref, gs) in _conv_columns((dxs_ref, dbm_ref, dcm_ref)):
            acc = _conv_taps(scr, w_ref, cs, HALO - CONV_WIDTH + 1, 1, tm) + b_ref[:, cs]
            sig = _sigmoid(acc)
            dy = g_ref[:, gs] + dxs2_ref[:, gs] if g_ref is dxs_ref else g_ref[:, gs]
            dz = dy * sig * (1.0 + acc * (1.0 - sig))
            dz_ref[:, cs] = dz
            db_ref[:, cs] += jnp.sum(dz, axis=0, keepdims=True)
            for k in range(CONV_WIDTH):
                dw_ref[k:k + 1, cs] += jnp.sum(dz * scr[pl.ds(HALO - CONV_WIDTH + 1 + k, tm), cs], axis=0, keepdims=True)

    dz, dw, db = pl.pallas_call(
        body1, name="conv_bwd_dz", grid=(n_tiles,),
        in_specs=[pl.BlockSpec((tm, D_CONV), lambda i: (i, 0)),
                  pl.BlockSpec((HALO, D_CONV), lambda i: (jnp.maximum(i * (tm // HALO) - 1, 0), 0)),
                  pl.BlockSpec((tm, D_SSD), lambda i: (i, 0)), pl.BlockSpec((tm, D_SSD), lambda i: (i, 0)),
                  pl.BlockSpec((tm, D_BC), lambda i: (i, 0)), pl.BlockSpec((tm, D_BC), lambda i: (i, 0)),
                  pl.BlockSpec((CONV_WIDTH, D_CONV), lambda i: (0, 0)), pl.BlockSpec((1, D_CONV), lambda i: (0, 0))],
        out_specs=[pl.BlockSpec((tm, D_CONV), lambda i: (i, 0)), pl.BlockSpec((CONV_WIDTH, D_CONV), lambda i: (0, 0)),
                   pl.BlockSpec((1, D_CONV), lambda i: (0, 0))],
        out_shape=[jax.ShapeDtypeStruct((t, D_CONV), F32), jax.ShapeDtypeStruct((CONV_WIDTH, D_CONV), F32),
                   jax.ShapeDtypeStruct((1, D_CONV), F32)],
        scratch_shapes=[pltpu.VMEM((tm + HALO, D_CONV), F32)],
        compiler_params=_cparams(("arbitrary",)),
    )(u, u, dxs_a, dxs_b, dbm, dcm, w, bias)

    def body2(dz_ref, n_ref, w_ref, du_ref, scr):
        last = pl.program_id(0) % per_seq == per_seq - 1
        scr[0:tm, :] = dz_ref[...]
        scr[tm:, :] = jnp.where(last, 0.0, n_ref[...])
        for c in range(D_CONV // LANES):
            cs = slice(c * LANES, (c + 1) * LANES)
            du_ref[:, cs] = _conv_taps(scr, w_ref, cs, CONV_WIDTH - 1, -1, tm).astype(du_ref.dtype)

    du = pl.pallas_call(
        body2, name="conv_bwd_du", grid=(n_tiles,),
        in_specs=[pl.BlockSpec((tm, D_CONV), lambda i: (i, 0)),
                  pl.BlockSpec((HALO, D_CONV), lambda i: (jnp.minimum((i + 1) * (tm // HALO), t // HALO - 1), 0)),
                  pl.BlockSpec((CONV_WIDTH, D_CONV), lambda i: (0, 0))],
        out_specs=pl.BlockSpec((tm, D_CONV), lambda i: (i, 0)),
        out_shape=jax.ShapeDtypeStruct((t, D_CONV), BF16),
        scratch_shapes=[pltpu.VMEM((tm + HALO, D_CONV), F32)],
        compiler_params=_cparams(("parallel",)),
    )(dz, dz, w)
    return du, dw, db


Q = SSD_CHUNK
N_PAIRS = D_SSD // LANES
HEADS_PER_GROUP = N_HEADS // SSD_GROUPS


def _rep(a, j):
    return jnp.broadcast_to(a[:, j:j + 1], a.shape)


def _dot_exact01(a, b, dn, a_is_01):
    x = b if a_is_01 else a
    hi = x.astype(BF16)
    mid = (x - hi.astype(F32)).astype(BF16)
    lo = (x - hi.astype(F32) - mid.astype(F32)).astype(BF16)
    z = a.astype(BF16) if a_is_01 else b.astype(BF16)
    out = None
    for term in (hi, mid, lo):
        d = _dot(z, term, dn) if a_is_01 else _dot(term, z, dn)
        out = d if out is None else out + d
    return out


def _pad_lanes(v, fill=0.0):
    row = jnp.pad(v.reshape(1, -1).astype(F32), ((0, 0), (0, LANES - v.size)), constant_values=fill)
    return row, row.reshape(LANES, 1)


def _ssd_common(dtr_ref, dtrt_ref, bias_r, bias_c, alog_r, alog_c):
    row = lax.broadcasted_iota(jnp.int32, (Q, Q), 0)
    col = lax.broadcasted_iota(jnp.int32, (Q, Q), 1)
    tril = row >= col
    lane = lax.broadcasted_iota(jnp.int32, (1, LANES), 1)
    a_r = jnp.where(lane < N_HEADS, -jnp.exp(alog_r[...]), 0.0)
    sub = lax.broadcasted_iota(jnp.int32, (LANES, 1), 0)
    a_c = jnp.where(sub < N_HEADS, -jnp.exp(alog_c[...]), 0.0)
    dt = _softplus(dtr_ref[...] + bias_r[...])
    cs = _dot_exact01(tril, dt * a_r, _NN, True)
    dtt = _softplus(dtrt_ref[...] + bias_c[...])
    cst = _dot_exact01(dtt * a_c, row <= col, _NN, False)
    return tril, lane, a_r, dt, cs, cst


def _ssd_specs(b, nc, rev):
    ci = (lambda c: nc - 1 - c) if rev else (lambda c: c)
    rows = lambda w: pl.BlockSpec((Q, w), lambda bb, c: (bb * nc + ci(c), 0))
    dtt = pl.BlockSpec((LANES, Q), lambda bb, c: (0, bb * nc + ci(c)))
    const = lambda s: pl.BlockSpec(s, lambda bb, c: (0,) * len(s))
    state = pl.BlockSpec((None, N_PAIRS, LANES, SSD_STATE), lambda bb, c: (bb * nc + ci(c), 0, 0, 0))
    return rows, dtt, const, state


def ssd_fwd(xs, bm, cm, dtraw, dt_bias, a_log, b):
    t = xs.shape[0]
    nc = SEQ // Q
    rows, dtt_spec, const, state = _ssd_specs(b, nc, False)
    bias_r, bias_c = _pad_lanes(dt_bias)
    alog_r, alog_c = _pad_lanes(a_log)

    def body(xs_ref, b_ref, c_ref, dtr_ref, dtrt_ref, br, bc, ar, ac, y_ref, hp_ref, h_scr):
        @pl.when(pl.program_id(1) == 0)
        def _():
            h_scr[...] = jnp.zeros(h_scr.shape, F32)
        tril, lane, _, dt, cs, cst = _ssd_common(dtr_ref, dtrt_ref, br, bc, ar, ac)
        sub = lax.broadcasted_iota(jnp.int32, (LANES, 1), 0)
        y_acc = [jnp.zeros((Q, LANES), F32) for _ in range(N_PAIRS)]
        h_old = [h_scr[p] for p in range(N_PAIRS)]
        h_new = [jnp.zeros((LANES, SSD_STATE), F32) for _ in range(N_PAIRS)]
        for g in range(SSD_GROUPS):
            bg = b_ref[:, g * SSD_STATE:(g + 1) * SSD_STATE].astype(BF16)
            cg = c_ref[:, g * SSD_STATE:(g + 1) * SSD_STATE].astype(BF16)
            cb = _dot(cg, bg, _NT)
            for j in range(g * HEADS_PER_GROUP, (g + 1) * HEADS_PER_GROUP):
                p, side = j // 2, j % 2
                m = (lane < HEAD_DIM) if side == 0 else (lane >= HEAD_DIM)
                ms = (sub < HEAD_DIM) if side == 0 else (sub >= HEAD_DIM)
                csj, dtj = _rep(cs, j), _rep(dt, j)
                lmat = jnp.exp(jnp.where(tril, csj - cst[j:j + 1, :], NEG))
                xdt = jnp.where(m, xs_ref[:, p * LANES:(p + 1) * LANES] * dtj, 0.0)
                hm = jnp.where(ms, h_old[p], 0.0)
                ydiag = _dot((cb * lmat).astype(BF16), xdt.astype(BF16), _NN)
                yoff = jnp.exp(csj) * _dot(cg, hm.astype(BF16), _NT)
                y_acc[p] = y_acc[p] + ydiag + yoff
                last = csj[Q - 1:Q, :]
                sj = _dot((xdt * jnp.exp(last - csj)).astype(BF16), bg, _TN)
                h_new[p] = h_new[p] + jnp.exp(last) * hm + sj
        for p in range(N_PAIRS):
            y_ref[:, p * LANES:(p + 1) * LANES] = y_acc[p]
            hp_ref[p] = h_old[p]
            h_scr[p] = h_new[p]

    return pl.pallas_call(
        body, name="ssd_fwd", grid=(b, nc),
        in_specs=[rows(D_SSD), rows(D_BC), rows(D_BC), rows(LANES), dtt_spec, const((1, LANES)), const((LANES, 1)),
                  const((1, LANES)), const((LANES, 1))],
        out_specs=[rows(D_SSD), state],
        out_shape=[jax.ShapeDtypeStruct((t, D_SSD), F32),
                   jax.ShapeDtypeStruct((b * nc, N_PAIRS, LANES, SSD_STATE), F32)],
        scratch_shapes=[pltpu.VMEM((N_PAIRS, LANES, SSD_STATE), F32)],
        compiler_params=_cparams(("parallel", "arbitrary")),
    )(xs, bm, cm, dtraw, dtraw.T, bias_r, bias_c, alog_r, alog_c)


def ssd_bwd(xs, bm, cm, dtraw, dt_bias, a_log, hprev, dy, b):
    t = xs.shape[0]
    nc = SEQ // Q
    rows, dtt_spec, const, state = _ssd_specs(b, nc, True)
    bias_r, bias_c = _pad_lanes(dt_bias)
    alog_r, alog_c = _pad_lanes(a_log)

    def body(xs_ref, b_ref, c_ref, dtr_ref, dtrt_ref, hp_ref, dy_ref, br, bc, ar, ac,
             dxs_ref, db_ref, dc_ref, ddt_ref, dbias_ref, dalog_ref, dh_scr):
        first = jnp.logical_and(pl.program_id(0) == 0, pl.program_id(1) == 0)

        @pl.when(pl.program_id(1) == 0)
        def _():
            dh_scr[...] = jnp.zeros(dh_scr.shape, F32)

        @pl.when(first)
        def _():
            dbias_ref[...] = jnp.zeros(dbias_ref.shape, F32)
            dalog_ref[...] = jnp.zeros(dalog_ref.shape, F32)
        tril, lane, a_r, dt, cs, cst = _ssd_common(dtr_ref, dtrt_ref, br, bc, ar, ac)
        sub = lax.broadcasted_iota(jnp.int32, (LANES, 1), 0)
        rowq = lax.broadcasted_iota(jnp.int32, (Q, 1), 0)
        triu = (lax.broadcasted_iota(jnp.int32, (Q, Q), 0) <= lax.broadcasted_iota(jnp.int32, (Q, Q), 1)).astype(F32)
        dxs_acc = [jnp.zeros((Q, LANES), F32) for _ in range(N_PAIRS)]
        dh_in = [dh_scr[p] for p in range(N_PAIRS)]
        h_in = [hp_ref[p] for p in range(N_PAIRS)]
        dh_out = [jnp.zeros((LANES, SSD_STATE), F32) for _ in range(N_PAIRS)]
        ddt = jnp.zeros((Q, LANES), F32)
        dalog = jnp.zeros((1, LANES), F32)
        for g in range(SSD_GROUPS):
            gs = slice(g * SSD_STATE, (g + 1) * SSD_STATE)
            bg, cg = b_ref[:, gs].astype(BF16), c_ref[:, gs].astype(BF16)
            cb = _dot(cg, bg, _NT)
            dcb = jnp.zeros((Q, Q), F32)
            dbg = jnp.zeros((Q, SSD_STATE), F32)
            dcg = jnp.zeros((Q, SSD_STATE), F32)
            for j in range(g * HEADS_PER_GROUP, (g + 1) * HEADS_PER_GROUP):
                p, side = j // 2, j % 2
                m = (lane < HEAD_DIM) if side == 0 else (lane >= HEAD_DIM)
                ms = (sub < HEAD_DIM) if side == 0 else (sub >= HEAD_DIM)
                csj, dtj = _rep(cs, j), _rep(dt, j)
                lmat = jnp.exp(jnp.where(tril, csj - cst[j:j + 1, :], NEG))
                x2 = jnp.where(m, xs_ref[:, p * LANES:(p + 1) * LANES], 0.0)
                xdt = x2 * dtj
                dym = jnp.where(m, dy_ref[:, p * LANES:(p + 1) * LANES], 0.0)
                hm = jnp.where(ms, h_in[p], 0.0)
                dhm = jnp.where(ms, dh_in[p], 0.0)
                ecs = jnp.exp(csj)
                last = csj[Q - 1:Q, :]
                decay = jnp.exp(last - csj)
                el = jnp.exp(last)
                gmat = cb * lmat
                dymb, xdtb = dym.astype(BF16), xdt.astype(BF16)
                dg = _dot(dymb, xdtb, _NT)
                dxdt = _dot(gmat.astype(BF16), dymb, _TN)
                dcb = dcb + dg * lmat
                ej = dg * gmat
                col_sums = jnp.broadcast_to(jnp.sum(ej, axis=0, keepdims=True), (Q, Q)).T
                dcs = jnp.sum(ej, axis=1, keepdims=True) - col_sums
                ch = _dot(cg, hm.astype(BF16), _NT)
                dye = dym * ecs
                dcs = dcs + jnp.sum(dye * ch, axis=1, keepdims=True)
                dcg = dcg + _dot(dye.astype(BF16), hm.astype(BF16), _NN)
                dhp = _dot(dye.astype(BF16), cg, _TN)
                wmat = _dot(bg, dhm.astype(BF16), _NT)
                xd = xdt * decay
                dxdt = dxdt + decay * wmat
                ddl = jnp.sum(xd * wmat, axis=1, keepdims=True)
                dlast = jnp.sum(ddl, axis=0, keepdims=True) + el * jnp.sum(jnp.sum(dhm * hm, axis=1, keepdims=True), axis=0, keepdims=True)
                dcs = dcs - ddl + jnp.where(rowq == Q - 1, dlast, 0.0)
                dbg = dbg + _dot(xd.astype(BF16), dhm.astype(BF16), _NN)
                dh_out[p] = dh_out[p] + el * dhm + dhp
                da = _dot_exact01(triu, dcs, _NN, True)
                aj = jnp.sum(jnp.where(lane == j, a_r, 0.0), axis=1, keepdims=True)
                ddtj = da * aj + jnp.sum(dxdt * x2, axis=1, keepdims=True)
                ddt = ddt + jnp.where(lane == j, ddtj, 0.0)
                dalog = dalog + jnp.where(lane == j, jnp.sum(da * dtj, axis=0, keepdims=True) * aj, 0.0)
                dxs_acc[p] = dxs_acc[p] + dxdt * dtj
            dcbb = dcb.astype(BF16)
            dc_ref[:, gs] = dcg + _dot(dcbb, bg, _NN)
            db_ref[:, gs] = dbg + _dot(dcbb, cg, _TN)
        for p in range(N_PAIRS):
            dxs_ref[:, p * LANES:(p + 1) * LANES] = dxs_acc[p]
            dh_scr[p] = dh_out[p]
        ddtraw = ddt * _sigmoid(dtr_ref[...] + br[...])
        ddt_ref[...] = ddtraw
        dbias_ref[...] += jnp.sum(ddtraw, axis=0, keepdims=True)
        dalog_ref[...] += dalog

    return pl.pallas_call(
        body, name="ssd_bwd", grid=(b, nc),
        in_specs=[rows(D_SSD), rows(D_BC), rows(D_BC), rows(LANES), dtt_spec, state, rows(D_SSD), const((1, LANES)),
                  const((LANES, 1)), const((1, LANES)), const((LANES, 1))],
        out_specs=[rows(D_SSD), rows(D_BC), rows(D_BC), rows(LANES), const((1, LANES)), const((1, LANES))],
        out_shape=[jax.ShapeDtypeStruct((t, D_SSD), F32), jax.ShapeDtypeStruct((t, D_BC), F32),
                   jax.ShapeDtypeStruct((t, D_BC), F32), jax.ShapeDtypeStruct((t, LANES), F32),
                   jax.ShapeDtypeStruct((1, LANES), F32), jax.ShapeDtypeStruct((1, LANES), F32)],
        scratch_shapes=[pltpu.VMEM((N_PAIRS, LANES, SSD_STATE), F32)],
        compiler_params=_cparams(("arbitrary", "arbitrary")),
    )(xs, bm, cm, dtraw, dtraw.T, hprev, dy, bias_r, bias_c, alog_r, alog_c)


def _split_w_in(w_in):
    w_dt = jnp.pad(w_in[:, D_QKVZ + D_CONV:], ((0, 0), (0, LANES - N_HEADS)))
    return w_in[:, :D_QKVZ], w_in[:, D_QKVZ:D_QKVZ + D_CONV], w_dt


def mixer_fwd(hb, p, cosv, sinv, b):
    t = hb.shape[0]
    w_a, w_b, w_c = _split_w_in(p['w_in'])
    qkvz = mm("in_qkvz", [(hb, w_a, 'nn')], D_QKVZ)
    xbc = mm("in_xbc", [(hb, w_b, 'nn')], D_CONV)
    dtraw = mm("in_dt", [(hb, w_c, 'nn')], LANES)
    mixed, *lses = attn_fwd(qkvz, cosv, sinv, b)
    attn = attn_norm_fwd(mixed, p['attn_norm_w'])
    xs, bm, cm = conv_fwd(xbc, p['conv_w'], p['conv_b'])
    y, hprev = ssd_fwd(xs, bm, cm, dtraw, p['dt_bias'], p['a_log'], b)
    dskip = jnp.repeat(p['d_skip'].reshape(-1), HEAD_DIM).reshape(1, D_SSD)
    yg, = rowwise("ssd_gate", _gate, [y, xs, Op(qkvz, D_SSD, 3)], [dskip, p['ssd_norm_w']], [(t, D_SSD, BF16)])
    mix = mm("out_proj", [(attn, p['w_out'][:D_ATTN], 'nn'), (yg, p['w_out'][D_ATTN:], 'nn')], D_MODEL)
    res = dict(hb=hb, qkvz=qkvz, xbc=xbc, dtraw=dtraw, mixed=mixed, lses=lses, attn=attn, xs=xs, bm=bm, cm=cm,
               y=y, hprev=hprev, dskip=dskip, yg=yg, cosv=cosv, sinv=sinv)
    return mix, res


def mixer_bwd(r, p, dmix, dh_resid, b):
    t = dmix.shape[0]
    w_a, w_b, w_c = _split_w_in(p['w_in'])
    w_out = p['w_out']
    dattn = mm("out_bwd_dattn", [(dmix, w_out[:D_ATTN], 'nt')], D_ATTN)
    dyg = mm("out_bwd_dyg", [(dmix, w_out[D_ATTN:], 'nt')], D_SSD)
    dw_out = jnp.concatenate([mm_tn("out_bwd_dw_a", r['attn'], dmix, BF16),
                              mm_tn("out_bwd_dw_y", r['yg'], dmix, BF16)], axis=0)

    def gate_bwd(dy_, y_, xs_, z_, ds_, w_):
        _, vjp = jax.vjp(_gate, y_, xs_, z_, ds_, w_)
        return vjp(dy_)

    dy, dxs_a, dz, ddskip, dssd_norm = rowwise(
        "ssd_gate_bwd", gate_bwd, [dyg, r['y'], r['xs'], Op(r['qkvz'], D_SSD, 3)], [r['dskip'], p['ssd_norm_w']],
        [(t, D_SSD, F32), (t, D_SSD, F32), (t, D_SSD, BF16)], accs=[(1, D_SSD), (1, D_SSD)])
    dxs_b, dbm, dcm, ddtraw, ddt_bias, da_log = ssd_bwd(r['xs'], r['bm'], r['cm'], r['dtraw'], p['dt_bias'], p['a_log'],
                                                        r['hprev'], dy, b)
    dxbc, dconv_w, dconv_b = conv_bwd(r['xbc'], p['conv_w'], p['conv_b'], dxs_a, dxs_b, dbm, dcm)
    dmixed, dattn_norm = attn_norm_bwd(dattn, r['mixed'], p['attn_norm_w'])
    dq, dk, dv = attn_bwd(r['qkvz'], r['cosv'], r['sinv'], dmixed, r['mixed'], r['lses'], b)
    wq, wk, wv, wz = (w_a[:, i * D_ATTN:(i + 1) * D_ATTN] for i in range(4))
    dh = mm("in_bwd_dh", [(dq, wq, 'nt'), (dk, wk, 'nt'), (dv, wv, 'nt'), (dz, wz, 'nt'), (dxbc, w_b, 'nt'),
                          (ddtraw, w_c, 'nt')], D_MODEL, add=dh_resid, tn=512)
    h = r['hb']
    dw_in = jnp.concatenate([mm_tn("in_bwd_dwq", h, dq, BF16), mm_tn("in_bwd_dwk", h, dk, BF16),
                             mm_tn("in_bwd_dwv", h, dv, BF16), mm_tn("in_bwd_dwz", h, dz, BF16),
                             mm_tn("in_bwd_dwx", h, dxbc, BF16), mm_tn("in_bwd_dwdt", h, ddtraw, BF16)[:, :N_HEADS]], axis=1)
    head_sum = lambda v: v.reshape(N_HEADS, HEAD_DIM).sum(axis=1).reshape(1, N_HEADS)
    grads = dict(w_in=dw_in, w_out=dw_out, conv_w=dconv_w, conv_b=dconv_b, dt_bias=ddt_bias[:, :N_HEADS],
                 a_log=da_log[:, :N_HEADS], d_skip=head_sum(ddskip), attn_norm_w=dattn_norm, ssd_norm_w=dssd_norm)
    return dh, grads


FFN1_KEYS = ('ffn1_gate', 'ffn1_up', 'ffn1_down')
FFN2_KEYS = ('ffn2_gate', 'ffn2_up', 'ffn2_down')
MIXER_KEYS = ('w_in', 'conv_w', 'w_out')
FFN_COL = ('ffn1_gate', 'ffn1_up', 'ffn2_gate', 'ffn2_up')
FFN_ROW = ('ffn1_down', 'ffn2_down')
CONV_W_COMM = (8, 2 * LANES)
SMALL = 'small'


def comm_shape(k, shapes):
    if k in FFN_COL:
        return (D_MODEL, FF_PAD)
    if k in FFN_ROW:
        return (FF_PAD, D_MODEL)
    if k == 'conv_w':
        return CONV_W_COMM
    return tuple(shapes[k][1:])


def to_comm(k, vals, shapes):
    a = vals[k].reshape(shapes[k][1:])
    r_, c_ = comm_shape(k, shapes)
    return jnp.pad(a, ((0, r_ - a.shape[0]), (0, c_ - a.shape[1])))


SMALL_ROWS, SMALL_COLS = 16, D_CONV


def pack_small(small):
    rows = [jnp.pad(small[r].reshape(1, -1), ((0, 0), (0, SMALL_COLS - small[r].size))) for r in REPLICATED]
    return jnp.concatenate(rows + [jnp.zeros((SMALL_ROWS - len(rows), SMALL_COLS), F32)], axis=0)


def full_weight(k, g):
    if k in FFN_COL:
        return g
    if k == 'conv_w':
        return jnp.transpose(g[:, :CONV_WIDTH, :D_CONV // N_DEV], (1, 0, 2)).reshape(CONV_WIDTH, D_CONV)
    return g.reshape(N_DEV * g.shape[1], g.shape[2])


def grad_shards(k, g):
    if k in FFN_COL:
        return g
    if k == 'conv_w':
        s = jnp.transpose(g.reshape(CONV_WIDTH, N_DEV, D_CONV // N_DEV), (1, 0, 2))
        return jnp.pad(s, ((0, 0), (0, CONV_W_COMM[0] - CONV_WIDTH), (0, CONV_W_COMM[1] - D_CONV // N_DEV)))
    return g.reshape(N_DEV, g.shape[0] // N_DEV, g.shape[1])


def _flip(v, bit):
    return 1 - v if bit else v


N_PEER_COPIES = N_DEV - 1


def _comm_call(name, body, arrs, out_shape):
    n = len(arrs)
    return pl.pallas_call(
        functools.partial(body, n), name=name, out_shape=out_shape,
        in_specs=[pl.BlockSpec(memory_space=pl.ANY)] * n, out_specs=[pl.BlockSpec(memory_space=pl.ANY)] * n,
        scratch_shapes=[pltpu.SemaphoreType.DMA((n * N_PEER_COPIES,)), pltpu.SemaphoreType.DMA((n * N_PEER_COPIES,)),
                        pltpu.SemaphoreType.DMA((n,))],
    )(*arrs)


def _blk(ref, idx, by_cols):
    if not by_cols:
        return ref.at[idx]
    c = ref.shape[1] // N_DEV
    return ref.at[:, pl.ds(pl.multiple_of(idx * c, LANES), c)]


def _blocked_shape(a, by_cols):
    return (a.shape[0], N_DEV * a.shape[1]) if by_cols else (N_DEV,) + a.shape


def all_gather(arrs, by_cols):
    def body(n, *refs):
        x_refs, out_refs, (send_sems, recv_sems, local_sems) = refs[:n], refs[n:2 * n], refs[2 * n:]
        x, y, c = lax.axis_index("x"), lax.axis_index("y"), lax.axis_index("c")
        me, sibling = (x, y, c), (x, y, 1 - c)
        chips = [(1 - x, y), (x, 1 - y), (1 - x, 1 - y)]

        def copy(a, k, block, to, src=None):
            px, py, pc = block
            dst = _blk(out_refs[a], 4 * px + 2 * py + pc, by_cols[a])
            return pltpu.make_async_remote_copy(
                src_ref=dst if src is None else src, dst_ref=dst, send_sem=send_sems.at[a * N_PEER_COPIES + k],
                recv_sem=recv_sems.at[a * N_PEER_COPIES + k], device_id=to, device_id_type=MESH)

        mine = [pltpu.make_async_copy(x_refs[a], _blk(out_refs[a], 4 * x + 2 * y + c, by_cols[a]), local_sems.at[a])
                for a in range(n)]
        started = []
        for a in range(n):
            mine[a].start()
            first = [copy(a, 0, me, sibling, src=x_refs[a])]
            first += [copy(a, 1 + j, me, (*chip, c), src=x_refs[a]) for j, chip in enumerate(chips)]
            for cp in first:
                cp.start()
            started += first
        for j, chip in enumerate(chips):
            for a in range(n):
                copy(a, 1 + j, (*chip, c), me).wait_recv()
                cp = copy(a, 4 + j, (*chip, c), sibling)
                cp.start()
                started.append(cp)
        for a in range(n):
            copy(a, 0, sibling, me).wait_recv()
            for j, chip in enumerate(chips):
                copy(a, 4 + j, (*chip, 1 - c), me).wait_recv()
        for cp in started:
            cp.wait_send()
        for cp in mine:
            cp.wait()

    return _comm_call("all_gather_weights", body, arrs,
                      [jax.ShapeDtypeStruct(_blocked_shape(a, bc), a.dtype) for a, bc in zip(arrs, by_cols)])


def blocks_to_cols(arrs):
    def body(*refs):
        for i, o in zip(refs[:len(arrs)], refs[len(arrs):]):
            o[...] = i[...]

    return pl.pallas_call(
        body, name="blocks_to_cols", grid=(N_DEV,),
        in_specs=[pl.BlockSpec((None,) + a.shape[1:], lambda p: (p, 0, 0)) for a in arrs],
        out_specs=[pl.BlockSpec(a.shape[1:], lambda p: (0, p)) for a in arrs],
        out_shape=[jax.ShapeDtypeStruct((a.shape[1], N_DEV * a.shape[2]), a.dtype) for a in arrs],
        compiler_params=_cparams(("parallel",)),
    )(*arrs)


def _landing_shape(a, by_cols):
    return (N_DEV, a.shape[0], a.shape[1] // N_DEV) if by_cols else a.shape


def all_to_all(arrs, by_cols):
    def body(n, *refs):
        s_refs, r_refs, (send_sems, recv_sems, local_sems) = refs[:n], refs[n:2 * n], refs[2 * n:]
        x, y, c = lax.axis_index("x"), lax.axis_index("y"), lax.axis_index("c")
        me = 4 * x + 2 * y + c

        def peer(k):
            return _flip(x, k & 4), _flip(y, k & 2), _flip(c, k & 1)

        def copy(a, k, landing):
            px, py, pc = peer(k)
            p = 4 * px + 2 * py + pc
            src, dst = (me, p) if landing else (p, me)
            return pltpu.make_async_remote_copy(
                src_ref=_blk(s_refs[a], src, by_cols[a]), dst_ref=r_refs[a].at[dst],
                send_sem=send_sems.at[a * N_PEER_COPIES + k - 1],
                recv_sem=recv_sems.at[a * N_PEER_COPIES + k - 1], device_id=(px, py, pc), device_id_type=MESH)

        mine = [pltpu.make_async_copy(_blk(s_refs[a], me, by_cols[a]), r_refs[a].at[me], local_sems.at[a]) for a in range(n)]
        sends = [copy(a, k, False) for a in range(n) for k in range(1, N_DEV)]
        for cp in mine + sends:
            cp.start()
        for a in range(n):
            for k in range(1, N_DEV):
                copy(a, k, True).wait_recv()
        for cp in sends:
            cp.wait_send()
        for cp in mine:
            cp.wait()

    return _comm_call("all_to_all_grads", body, arrs,
                      [jax.ShapeDtypeStruct(_landing_shape(a, bc), a.dtype) for a, bc in zip(arrs, by_cols)])


_HBM = pl.BlockSpec(memory_space=pltpu.HBM)
_SEM = pl.BlockSpec(memory_space=pltpu.SEMAPHORE)
_EFFECT = pltpu.SideEffectType.DATAFLOW_SIDE_EFFECTING


def _peer(k):
    x, y, c = lax.axis_index("x"), lax.axis_index("y"), lax.axis_index("c")
    return _flip(x, k & 4), _flip(y, k & 2), _flip(c, k & 1)


def _my_index():
    return 4 * lax.axis_index("x") + 2 * lax.axis_index("y") + lax.axis_index("c")


def _split_copies(mode, by_cols, src_refs, land_refs, send_sems, recv_sems):
    me = _my_index()
    out = []
    for a, bc in enumerate(by_cols):
        for k in range(1, N_DEV):
            px, py, pc = _peer(k)
            src = _blk(src_refs[a], 4 * px + 2 * py + pc, bc) if mode == 'scatter' else src_refs[a]
            dst = land_refs[a].at[me] if mode == 'scatter' else _blk(land_refs[a], me, bc)
            out.append(pltpu.make_async_remote_copy(
                src_ref=src, dst_ref=dst, send_sem=send_sems.at[a * N_PEER_COPIES + k - 1],
                recv_sem=recv_sems.at[a * N_PEER_COPIES + k - 1], device_id=(px, py, pc), device_id_type=MESH))
    return out


def exchange_start(name, mode, srcs, by_cols):
    n = len(srcs)
    lands = [lax.empty(_landing_shape(s, bc) if mode == 'scatter' else _blocked_shape(s, bc), s.dtype)
             for s, bc in zip(srcs, by_cols)]

    def body(*refs):
        src_refs, land_refs, send_sems, recv_sems = refs[:n], refs[n:2 * n], refs[2 * n], refs[2 * n + 1]
        for cp in _split_copies(mode, by_cols, src_refs, land_refs, send_sems, recv_sems):
            cp.start()
        refs[-1][...] = jnp.zeros(refs[-1].shape, F32)

    sems = pltpu.SemaphoreType.DMA((n * N_PEER_COPIES,))
    res = pl.pallas_call(
        body, name=name,
        out_shape=(sems, sems, *[pltpu.HBM(a.shape, a.dtype) for a in srcs + lands], jax.ShapeDtypeStruct((8, LANES), F32)),
        in_specs=(_HBM,) * (2 * n), out_specs=(_SEM, _SEM, *(_HBM,) * (2 * n), pl.BlockSpec(memory_space=pltpu.VMEM)),
        input_output_aliases={i: 2 + i for i in range(2 * n)},
        compiler_params=pltpu.CompilerParams(has_side_effects=_EFFECT),
    )(*[pltpu.with_memory_space_constraint(a, pltpu.HBM) for a in srcs + lands])
    return (mode, by_cols, res[:-1]), res[-1]


def exchange_wait(name, handles, after):
    mode, by_cols, (send_sems, recv_sems, *bufs) = handles
    n = len(by_cols)

    def body(*refs):
        src_refs, land_refs, s_sems, r_sems = refs[:n], refs[n:2 * n], refs[2 * n], refs[2 * n + 1]
        for cp in _split_copies(mode, by_cols, src_refs, land_refs, s_sems, r_sems):
            cp.wait_send()
            cp.wait_recv()

    res = pl.pallas_call(
        body, name=name, out_shape=tuple(pltpu.HBM(a.shape, a.dtype) for a in bufs),
        in_specs=(*(_HBM,) * (2 * n), _SEM, _SEM, pl.BlockSpec(memory_space=pl.ANY)), out_specs=(_HBM,) * (2 * n),
        input_output_aliases={i: i for i in range(2 * n)},
        compiler_params=pltpu.CompilerParams(has_side_effects=_EFFECT),
    )(*bufs, send_sems, recv_sems, after)
    me, out = _my_index(), []
    for src, land, bc in zip(res[:n], res[n:], by_cols):
        if mode == 'scatter':
            c = land.shape[2]
            own = lax.dynamic_slice(src, (0, me * c), (src.shape[0], c)) if bc else lax.dynamic_index_in_dim(src, me, 0, False)
            out.append(lax.dynamic_update_slice(land, own[None], (me, 0, 0)))
        elif bc:
            out.append(lax.dynamic_update_slice(land, src, (0, me * src.shape[1])))
        else:
            out.append(lax.dynamic_update_slice(land, src[None], (me, 0, 0)))
    return out


def _adamw_math(g, w, m, v):
    c1 = 1.0 / (1.0 - ADAM_B1 ** ADAM_STEP)
    c2 = 1.0 / (1.0 - ADAM_B2 ** ADAM_STEP)
    m = ADAM_B1 * m + (1.0 - ADAM_B1) * g
    v = ADAM_B2 * v + (1.0 - ADAM_B2) * jnp.square(g)
    return g, -ADAM_LR * ((m * c1) / (jnp.sqrt(v * c2) + ADAM_EPS) + ADAM_WD * w), m, v


def adamw(name, recv, w, m, v, tm):
    rows, cols = w.shape
    tm = min(tm, rows)

    def body(*refs):
        g = refs[0][0:tm, 0:cols].astype(F32)
        for s in range(1, N_DEV):
            g = g + refs[s][0:tm, 0:cols].astype(F32)
        res = _adamw_math(g, *[r[...] for r in refs[N_DEV:N_DEV + 3]])
        for r, val in zip(refs[N_DEV + 3:], res):
            r[...] = val

    part = lambda s: pl.BlockSpec((None, recv.shape[1] if tm == rows else tm, recv.shape[2]), lambda i: (s, i, 0))
    tile = pl.BlockSpec((tm, cols), lambda i: (i, 0))
    return pl.pallas_call(
        body, name=name, grid=(rows // tm,), in_specs=[part(s) for s in range(N_DEV)] + [tile] * 3, out_specs=[tile] * 4,
        out_shape=[jax.ShapeDtypeStruct((rows, cols), F32)] * 4, compiler_params=_cparams(("parallel",)),
    )(*[recv] * N_DEV, w, m, v)


def adamw_small(recv, wl, ml, vl):
    n = len(REPLICATED)

    def body(recv_ref, *refs):
        g = recv_ref[0]
        for s in range(1, N_DEV):
            g = g + recv_ref[s]
        for r in range(n):
            w, m, v = (refs[j * n + r][...] for j in range(3))
            for j, val in enumerate(_adamw_math(g[r:r + 1, :w.shape[1]], w, m, v)):
                refs[(3 + j) * n + r][...] = val

    arrs = [d[k].reshape(1, -1) for d in (wl, ml, vl) for k in REPLICATED]
    res = pl.pallas_call(
        body, name="adamw_small", out_shape=[jax.ShapeDtypeStruct(a.shape, F32) for a in arrs[:n]] * 4,
    )(recv, *arrs)
    return [{k: res[j * n + r].reshape(wl[k].shape) for r, k in enumerate(REPLICATED)} for j in range(4)]


ADAMW_TM = {'ffn1_gate': 256, 'ffn1_up': 256, 'ffn2_gate': 256, 'ffn2_up': 256, 'w_in': 32}


def kernel(x, positions, ln1_g, ln1_b, ffn1_gate, ffn1_up, ffn1_down, w_in, conv_w, conv_b, dt_bias, a_log, d_skip, attn_norm_w, ssd_norm_w, w_out, ln2_g, ln2_b, ffn2_gate, ffn2_up, ffn2_down, ln3_g, ln3_b, loss_target, m_ln1_g, m_ln1_b, m_ffn1_gate, m_ffn1_up, m_ffn1_down, m_w_in, m_conv_w, m_conv_b, m_dt_bias, m_a_log, m_d_skip, m_attn_norm_w, m_ssd_norm_w, m_w_out, m_ln2_g, m_ln2_b, m_ffn2_gate, m_ffn2_up, m_ffn2_down, m_ln3_g, m_ln3_b, v_ln1_g, v_ln1_b, v_ffn1_gate, v_ffn1_up, v_ffn1_down, v_w_in, v_conv_w, v_conv_b, v_dt_bias, v_a_log, v_d_skip, v_attn_norm_w, v_ssd_norm_w, v_w_out, v_ln2_g, v_ln2_b, v_ffn2_gate, v_ffn2_up, v_ffn2_down, v_ln3_g, v_ln3_b):
    args = dict(locals())
    wl = {k: args[k] for k in WEIGHTS}
    ml = {k: args["m_" + k] for k in WEIGHTS}
    vl = {k: args["v_" + k] for k in WEIGHTS}
    shapes = {k: wl[k].shape for k in WEIGHTS}
    b, s, dm = x.shape
    t = b * s

    sent = {k: to_comm(k, wl, shapes).astype(F32 if k == 'conv_w' else BF16) for k in SHARDED}
    by_cols = lambda keys: [k in FFN_COL for k in keys]
    gate, up, down = all_gather([sent[k] for k in FFN1_KEYS], [False] * 3)
    (gate, up, down), sent = lax.optimization_barrier(((gate, up, down), sent))
    p = dict(zip(FFN1_KEYS, (*blocks_to_cols([gate, up]), full_weight('ffn1_down', down))))
    gather_mixer, token_m = exchange_start("gather_mixer_start", 'gather', [sent[k] for k in MIXER_KEYS], by_cols(MIXER_KEYS))
    sent['ffn2_gate'] = sent['ffn2_gate'] + token_m[0, 0].astype(BF16)
    gather_ffn2, token_f = exchange_start("gather_ffn2_start", 'gather', [sent[k] for k in FFN2_KEYS], by_cols(FFN2_KEYS))
    for k in REPLICATED:
        p[k] = wl[k].reshape(1, -1)

    x2 = x.reshape(t, dm)
    cosv, sinv = rope_tables(positions)
    f1, res1 = ffn_fwd("ffn1", x2, p['ffn1_gate'], p['ffn1_up'], p['ffn1_down'], after=(token_m, token_f))
    h1, h1b = resid_ln_fwd("ln1", 0.5, x2, f1, p['ln1_g'], p['ln1_b'])
    for k, g in zip(MIXER_KEYS, exchange_wait("gather_mixer_wait", gather_mixer, h1b)):
        p[k] = full_weight(k, g)
    mix, resm = mixer_fwd(h1b, p, cosv, sinv, b)
    h2, h2b = resid_ln_fwd("ln2", 1.0, h1, mix, p['ln2_g'], p['ln2_b'])
    for k, g in zip(FFN2_KEYS, exchange_wait("gather_ffn2_wait", gather_ffn2, h2b)):
        p[k] = full_weight(k, g)
    f2, res3 = ffn_fwd("ffn2", h2b, p['ffn2_gate'], p['ffn2_up'], p['ffn2_down'])

    small, full = {}, {}
    dh2_res, df2, small['ln3_g'], small['ln3_b'], sq = ln_loss_bwd("ln3_loss_bwd", h2, f2, loss_target.reshape(t, dm),
                                                                   p['ln3_g'], p['ln3_b'])
    loss = lax.psum(jnp.sum(sq) * (0.5 / dm), AXES)

    dh2, full['ffn2_gate'], full['ffn2_up'], full['ffn2_down'] = ffn_bwd("ffn2", res3, p['ffn2_gate'], p['ffn2_up'],
                                                                       p['ffn2_down'], df2, dh2_res)
    ffn2_exchange, token = exchange_start("grads_ffn2_start", 'scatter', [grad_shards(k, full[k]) for k in FFN2_KEYS],
                                          by_cols(FFN2_KEYS))
    dh1_res, dmix, small['ln2_g'], small['ln2_b'] = resid_ln_bwd("ln2_bwd", 1.0, h1, mix, p['ln2_g'] + token[:1, :1],
                                                                 p['ln2_b'], dh2)
    dh1, gm = mixer_bwd(resm, p, dmix, dh1_res, b)
    for k in ('conv_b', 'dt_bias', 'a_log', 'd_skip', 'attn_norm_w', 'ssd_norm_w'):
        small[k] = gm[k]
    mixer_exchange, token = exchange_start("grads_mixer_start", 'scatter', [grad_shards(k, gm[k]) for k in MIXER_KEYS],
                                           by_cols(MIXER_KEYS))
    dx_res, df1, small['ln1_g'], small['ln1_b'] = resid_ln_bwd("ln1_bwd", 0.5, x2, f1, p['ln1_g'] + token[:1, :1],
                                                               p['ln1_b'], dh1)
    hb, g, u, a = res1
    small_part = pack_small(small)
    dg, du = ffn_da_act("ffn1_bwd_da_act", df1, p['ffn1_down'], g, u)
    dwd = mm_tn("ffn1_bwd_dwd", a, df1, BF16, after=dg)
    down_exchange, token = exchange_start("grads_ffn1_down_start", 'scatter', [
        grad_shards('ffn1_down', dwd), jnp.broadcast_to(small_part[None], (N_DEV,) + small_part.shape)], [False, False])
    dx = mm("ffn1_bwd_dh", [(dg, p['ffn1_gate'], 'nt'), (du, p['ffn1_up'], 'nt')], D_MODEL, add=dx_res, tn=512, after=token)
    dwg = mm_tn("ffn1_bwd_dwg", hb, dg, BF16, after=dx)
    gate_exchange, token = exchange_start("grads_ffn1_gate_start", 'scatter', [grad_shards('ffn1_gate', dwg)], [True])
    dwu = mm_tn("ffn1_bwd_dwu", hb, du, BF16, after=token)
    recv = dict(zip(('ffn1_up',), all_to_all([grad_shards('ffn1_up', dwu)], [True])))
    for keys, name, ex in (((FFN2_KEYS), "grads_ffn2_wait", ffn2_exchange), (MIXER_KEYS, "grads_mixer_wait", mixer_exchange),
                           (('ffn1_down', SMALL), "grads_ffn1_down_wait", down_exchange),
                           (('ffn1_gate',), "grads_ffn1_gate_wait", gate_exchange)):
        recv.update(zip(keys, exchange_wait(name, ex, recv['ffn1_up'])))
    outs = adamw_small(recv.pop(SMALL), wl, ml, vl)
    for k, r in recv.items():
        shard = shapes[k][1:]
        res = adamw(f"adamw_{k}", r, *[d[k].reshape(shard) for d in (wl, ml, vl)], ADAMW_TM.get(k, shard[0]))
        for o, a in zip(outs, res):
            o[k] = a.reshape(shapes[k])
    return (loss, dx.reshape(b, s, dm), *[o[k] for o in outs for k in WEIGHTS])
```

```python
import functools
import math

import jax
import jax.numpy as jnp
import numpy as np
from jax import lax
from jax.experimental import pallas as pl
from jax.experimental.pallas import tpu as pltpu

F32, BF16 = jnp.float32, jnp.bfloat16
HI = lax.Precision.HIGHEST
MESH = pl.DeviceIdType.MESH
AXES = ("x", "y", "c")
N_DEV = 8

D_MODEL = 1024
SEQ = 2048
HEAD_DIM = 64
N_HEADS = 12
D_ATTN = N_HEADS * HEAD_DIM
DILATIONS = (1, 4, 16)
ATTN_BLOCK = 128
ROPE_THETA = 500000.0
ROPE_DIM = 16
D_SSD = 768
SSD_GROUPS = 4
SSD_STATE = 128
SSD_CHUNK = 128
D_BC = SSD_GROUPS * SSD_STATE
D_CONV = D_SSD + 2 * D_BC
CONV_WIDTH = 4
D_QKVZ = 3 * D_ATTN + D_SSD
D_IN_PROJ = D_QKVZ + D_CONV + N_HEADS
D_FF = 2816
ALPHA = 2.0 ** 0.25
LN_EPS = 1e-5
RMS_EPS = 1e-6
ADAM_LR, ADAM_B1, ADAM_B2, ADAM_EPS, ADAM_WD, ADAM_STEP = 0.001, 0.9, 0.999, 1e-08, 0.01, 10

LANES = 128
VMEM_LIMIT = 52 * 1024 * 1024
NEG = -1e30

WEIGHTS = ['ln1_g', 'ln1_b', 'ffn1_gate', 'ffn1_up', 'ffn1_down', 'w_in', 'conv_w', 'conv_b', 'dt_bias', 'a_log',
           'd_skip', 'attn_norm_w', 'ssd_norm_w', 'w_out', 'ln2_g', 'ln2_b', 'ffn2_gate', 'ffn2_up', 'ffn2_down',
           'ln3_g', 'ln3_b']
COL_SHARDED = ('ffn1_gate', 'ffn1_up', 'conv_w', 'ffn2_gate', 'ffn2_up')
ROW_SHARDED = ('ffn1_down', 'w_in', 'w_out', 'ffn2_down')
SHARDED = tuple(n for n in WEIGHTS if n in COL_SHARDED or n in ROW_SHARDED)
REPLICATED = tuple(n for n in WEIGHTS if n not in SHARDED)
FF_SHARD = D_FF // N_DEV
FF_PAD = -(-FF_SHARD // LANES) * LANES
D_FF_INT = N_DEV * FF_PAD


def _cparams(sem=None):
    return pltpu.CompilerParams(dimension_semantics=sem, vmem_limit_bytes=VMEM_LIMIT)


def _tile(n, prefs):
    for p in prefs:
        if n % p == 0:
            return p
    return n


class Op:
    def __init__(self, arr, bw=None, cb=0, ro=0):
        self.arr, self.bw, self.cb, self.ro = arr, (arr.shape[1] if bw is None else bw), cb, ro


def _op(a):
    return a if isinstance(a, Op) else Op(a)


def rowwise(name, fn, ins, consts, outs, accs=(), tm=256):
    ins = [_op(a) for a in ins]
    rows = outs[0][0]
    n_in, n_c, n_o, n_a = len(ins), len(consts), len(outs), len(accs)
    tm = min(tm, rows)
    assert rows % tm == 0, (name, rows, tm)

    def body(*refs):
        vals = [r[...].astype(F32) for r in refs[:n_in + n_c]]
        res = fn(*vals)
        res = res if isinstance(res, (tuple, list)) else (res,)
        o_refs = refs[n_in + n_c:n_in + n_c + n_o]
        a_refs = refs[n_in + n_c + n_o:]
        for r, v in zip(o_refs, res[:n_o]):
            r[...] = v.astype(r.dtype)
        if n_a:
            @pl.when(pl.program_id(0) == 0)
            def _():
                for r in a_refs:
                    r[...] = jnp.zeros(r.shape, r.dtype)
            for r, v in zip(a_refs, res[n_o:]):
                r[...] += v

    in_specs = [pl.BlockSpec((tm, o.bw), functools.partial(lambda i, o: (i + o.ro, o.cb), o=o)) for o in ins]
    in_specs += [pl.BlockSpec(c.shape, functools.partial(lambda i, nd: (0,) * nd, nd=c.ndim)) for c in consts]
    out_specs = [pl.BlockSpec((tm, w), lambda i: (i, 0)) for (_, w, _) in outs]
    out_specs += [pl.BlockSpec(s, functools.partial(lambda i, nd: (0,) * nd, nd=len(s))) for s in accs]
    out_shape = [jax.ShapeDtypeStruct((r, w), dt) for (r, w, dt) in outs]
    out_shape += [jax.ShapeDtypeStruct(s, F32) for s in accs]
    res = pl.pallas_call(
        body, name=name, grid=(rows // tm,), in_specs=in_specs, out_specs=out_specs, out_shape=out_shape,
        compiler_params=_cparams(("arbitrary",) if n_a else ("parallel",)),
    )(*[o.arr for o in ins], *consts)
    return res


MM_TM = 512
MM_TN = (1024, 896, 768, 512, 256, 128)
_NT = (((1,), (1,)), ((), ()))
_NN = (((1,), (0,)), ((), ()))
_TN = (((0,), (0,)), ((), ()))


def _dot(a, b, dn, precision=None):
    return lax.dot_general(a, b, dn, preferred_element_type=F32, precision=precision)


def _mm_specs(name, pairs, n_out, tm, tn):
    in_specs, args = [], []
    for a, b, mode in pairs:
        o = _op(a)
        in_specs.append(pl.BlockSpec((tm, o.bw), functools.partial(lambda j, i, o: (i, o.cb), o=o)))
        args.append(o.arr)
        if mode == 'nn':
            assert b.shape == (o.bw, n_out), (name, b.shape, o.bw, n_out)
            in_specs.append(pl.BlockSpec((o.bw, tn), lambda j, i: (0, j)))
        else:
            assert b.shape == (n_out, o.bw), (name, b.shape, o.bw, n_out)
            in_specs.append(pl.BlockSpec((tn, o.bw), lambda j, i: (j, 0)))
        args.append(b)
    return in_specs, args


def _mm_acc(refs, pairs):
    acc = None
    for k, (_, _, mode) in enumerate(pairs):
        d = _dot(refs[2 * k][...].astype(BF16), refs[2 * k + 1][...].astype(BF16), _NN if mode == 'nn' else _NT)
        acc = d if acc is None else acc + d
    return acc


def mm(name, pairs, n_out, add=None, out_dtype=F32, tm=MM_TM, tn=None, after=None):
    m = _op(pairs[0][0]).arr.shape[0]
    tn = tn or _tile(n_out, MM_TN)
    n_p = len(pairs)

    def body(*refs):
        acc = _mm_acc(refs, pairs)
        if add is not None:
            acc = acc + refs[2 * n_p][...]
        refs[-1][...] = acc.astype(refs[-1].dtype)

    in_specs, args = _mm_specs(name, pairs, n_out, tm, tn)
    tile = pl.BlockSpec((tm, tn), lambda j, i: (i, j))
    if add is not None:
        in_specs.append(tile)
        args.append(add)
    if after is not None:
        in_specs.append(pl.BlockSpec(memory_space=pl.ANY))
        args.append(after)
    return pl.pallas_call(
        body, name=name, grid=(n_out // tn, m // tm), in_specs=in_specs, out_specs=tile,
        out_shape=jax.ShapeDtypeStruct((m, n_out), out_dtype),
        compiler_params=_cparams(("parallel", "parallel")),
    )(*args)


def mm_tn(name, a, b, out_dtype=F32, tt=1024, after=None):
    a, b = _op(a), _op(b)
    t = a.arr.shape[0]
    k, n = a.bw, b.bw
    tk = _tile(k, (512, 896, 768, 256, 128))
    tn = _tile(n, (3072, 1792) + MM_TN)
    tt = min(tt, t)
    n_t = t // tt
    order = [] if after is None else [after]

    def body(a_ref, b_ref, *rest):
        o_ref, acc_ref = rest[-2:]
        s = pl.program_id(2)
        d = _dot(a_ref[...].astype(BF16), b_ref[...].astype(BF16), _TN)

        @pl.when(s == 0)
        def _():
            acc_ref[...] = d

        @pl.when(s > 0)
        def _():
            acc_ref[...] += d

        @pl.when(s == n_t - 1)
        def _():
            o_ref[...] = acc_ref[...].astype(o_ref.dtype)

    return pl.pallas_call(
        body, name=name, grid=(k // tk, n // tn, n_t),
        in_specs=[pl.BlockSpec((tt, tk), functools.partial(lambda kk, nn, s, o: (s, o.cb * (o.bw // tk) + kk), o=a)),
                  pl.BlockSpec((tt, tn), functools.partial(lambda kk, nn, s, o: (s, o.cb * (o.bw // tn) + nn), o=b))]
        + [pl.BlockSpec(memory_space=pl.ANY) for _ in order],
        out_specs=pl.BlockSpec((tk, tn), lambda kk, nn, s: (kk, nn)),
        out_shape=jax.ShapeDtypeStruct((k, n), out_dtype),
        scratch_shapes=[pltpu.VMEM((tk, tn), F32)],
        compiler_params=_cparams(("parallel", "parallel", "arbitrary")),
    )(a.arr, b.arr, *order)


def _sigmoid(x):
    return 1.0 / (1.0 + jnp.exp(-x))


def _silu(x):
    return x * _sigmoid(x)


def _softplus(x):
    return jnp.maximum(x, 0.0) + jnp.log(1.0 + jnp.exp(-jnp.abs(x)))


def _act(g, u):
    return _silu(g) * u


def _resid_ln(scale, h, branch, g, b):
    r = ALPHA * h + scale * branch
    mu = jnp.mean(r, axis=-1, keepdims=True)
    var = jnp.mean(jnp.square(r - mu), axis=-1, keepdims=True)
    return (r - mu) * lax.rsqrt(var + LN_EPS) * g + b


def _rms(t, w):
    return t * lax.rsqrt(jnp.mean(t * t, axis=-1, keepdims=True) + RMS_EPS) * w


def _branch_weights(l1, l2, l3):
    m = jnp.maximum(jnp.maximum(l1, l2), l3)
    e1, e2, e3 = jnp.exp(l1 - m), jnp.exp(l2 - m), jnp.exp(l3 - m)
    inv = 1.0 / (e1 + e2 + e3)
    return e1 * inv, e2 * inv, e3 * inv


def _gate(y, xs, z, dskip, w):
    return _rms((y + dskip * xs) * _silu(z), w)


def _rot(x):
    d = lax.broadcasted_iota(jnp.int32, x.shape, 1) % HEAD_DIM
    up = pltpu.roll(x, x.shape[1] - ROPE_DIM // 2, 1)
    down = jnp.where(d < ROPE_DIM, pltpu.roll(x, ROPE_DIM // 2, 1), 0.0)
    return jnp.where(d < ROPE_DIM // 2, up, down)


def ffn_gate_up(name, h, wg, wu, after=()):
    m, nf = h.shape[0], wg.shape[1]
    tn = _tile(nf, MM_TN)

    def body(h_ref, g_w, u_w, *rest):
        g_ref, u_ref, a_ref, at_ref = rest[-4:]
        hb = h_ref[...].astype(BF16)
        g = _dot(hb, g_w[...].astype(BF16), _NN)
        u = _dot(hb, u_w[...].astype(BF16), _NN)
        g_ref[...] = g.astype(g_ref.dtype)
        u_ref[...] = u.astype(u_ref.dtype)
        a = _act(g, u)
        a_ref[...] = a.astype(a_ref.dtype)
        at_ref[...] = a.T.astype(at_ref.dtype)

    in_specs, args = _mm_specs(name, [(h, wg, 'nn')], nf, MM_TM, tn)
    in_specs.append(in_specs[1])
    in_specs += [pl.BlockSpec(memory_space=pl.ANY) for _ in after]
    tile = pl.BlockSpec((MM_TM, tn), lambda j, i: (i, j))
    return pl.pallas_call(
        body, name=name, grid=(nf // tn, m // MM_TM), in_specs=in_specs,
        out_specs=[tile] * 3 + [pl.BlockSpec((tn, MM_TM), lambda j, i: (j, i))],
        out_shape=[jax.ShapeDtypeStruct((m, nf), BF16)] * 3 + [jax.ShapeDtypeStruct((nf, m), BF16)],
        compiler_params=_cparams(("parallel", "parallel")),
    )(*args, wu, *after)


def mm_acc(name, a, b, out_dtype=F32, tt=1024, after=None):
    k, t = a.shape
    n = b.shape[1]
    tk, tn, tt = _tile(k, (512, 256, 128)), _tile(n, MM_TN), min(tt, t)
    n_t = t // tt
    order = [] if after is None else [after]

    def body(a_ref, b_ref, *rest):
        o_ref, acc_ref = rest[-2:]
        s = pl.program_id(2)
        d = _dot(a_ref[...].astype(BF16), b_ref[...].astype(BF16), _NN)

        @pl.when(s == 0)
        def _():
            acc_ref[...] = d

        @pl.when(s > 0)
        def _():
            acc_ref[...] += d

        @pl.when(s == n_t - 1)
        def _():
            o_ref[...] = acc_ref[...].astype(o_ref.dtype)

    return pl.pallas_call(
        body, name=name, grid=(k // tk, n // tn, n_t),
        in_specs=[pl.BlockSpec((tk, tt), lambda kk, nn, s: (kk, s)), pl.BlockSpec((tt, tn), lambda kk, nn, s: (s, nn))]
        + [pl.BlockSpec(memory_space=pl.ANY) for _ in order],
        out_specs=pl.BlockSpec((tk, tn), lambda kk, nn, s: (kk, nn)),
        out_shape=jax.ShapeDtypeStruct((k, n), out_dtype), scratch_shapes=[pltpu.VMEM((tk, tn), F32)],
        compiler_params=_cparams(("parallel", "parallel", "arbitrary")),
    )(a, b, *order)


def ffn_da_act(name, df, wd, g, u):
    m, nf = df.shape[0], wd.shape[0]
    tn = _tile(nf, MM_TN)

    def body(df_ref, w_ref, g_ref, u_ref, dg_ref, du_ref):
        da = _dot(df_ref[...].astype(BF16), w_ref[...].astype(BF16), _NT)
        _, vjp = jax.vjp(_act, g_ref[...].astype(F32), u_ref[...].astype(F32))
        dg, du = vjp(da)
        dg_ref[...] = dg.astype(dg_ref.dtype)
        du_ref[...] = du.astype(du_ref.dtype)

    in_specs, args = _mm_specs(name, [(df, wd, 'nt')], nf, MM_TM, tn)
    tile = pl.BlockSpec((MM_TM, tn), lambda j, i: (i, j))
    return pl.pallas_call(
        body, name=name, grid=(nf // tn, m // MM_TM), in_specs=in_specs + [tile, tile], out_specs=[tile] * 2,
        out_shape=[jax.ShapeDtypeStruct((m, nf), BF16)] * 2, compiler_params=_cparams(("parallel", "parallel")),
    )(*args, g, u)


def resid_ln_fwd(name, scale, h, branch, ln_g, ln_b):
    t = h.shape[0]

    def fn(*a):
        y = _resid_ln(scale, *a)
        return y, y

    return rowwise(name, fn, [h, branch], [ln_g, ln_b], [(t, D_MODEL, F32), (t, D_MODEL, BF16)], tm=512)


def ffn_fwd(tag, hb, wg, wu, wd, after=()):
    g, u, a, at = ffn_gate_up(f"{tag}_gate_up", hb, wg, wu, after)
    f = mm(f"{tag}_down", [(a, wd, 'nn')], D_MODEL)
    return f, (hb, g, u, at)


def ln_loss_bwd(name, h, branch, target, ln_g, ln_b):
    t, dm = h.shape

    def fn(h_, br_, tgt, g_, b_):
        y, vjp = jax.vjp(functools.partial(_resid_ln, 0.5), h_, br_, g_, b_)
        e = y - tgt
        return (*vjp(e * (1.0 / dm)), jnp.sum(e * e, axis=0, keepdims=True))

    return rowwise(name, fn, [h, branch, target], [ln_g, ln_b], [(t, dm, F32), (t, dm, F32)],
                   accs=[(1, dm), (1, dm), (1, dm)], tm=512)


def resid_ln_bwd(name, scale, h, branch, ln_g, ln_b, dout, extra=None):
    t = h.shape[0]

    def fn(h_, br_, do_, *rest):
        g_, b_ = rest[-2], rest[-1]
        _, vjp = jax.vjp(functools.partial(_resid_ln, scale), h_, br_, g_, b_)
        dh, dbr, dg, db = vjp(do_)
        if extra is not None:
            dh = dh + rest[0]
        return dh, dbr, dg, db

    ins = [h, branch, dout] + ([extra] if extra is not None else [])
    return rowwise(name, fn, ins, [ln_g, ln_b], [(t, D_MODEL, F32), (t, D_MODEL, F32)],
                   accs=[(1, D_MODEL), (1, D_MODEL)], tm=512)


def ffn_bwd(tag, res, wg, wu, wd, df, dh_resid):
    hb, g, u, at = res
    dg, du = ffn_da_act(f"{tag}_bwd_da_act", df, wd, g, u)
    dwd = mm_acc(f"{tag}_bwd_dwd", at, df, BF16)
    dh = mm(f"{tag}_bwd_dh", [(dg, wg, 'nt'), (du, wu, 'nt')], D_MODEL, add=dh_resid, tn=512)
    dwg = mm_tn(f"{tag}_bwd_dwg", hb, dg, BF16)
    dwu = mm_tn(f"{tag}_bwd_dwu", hb, du, BF16)
    return dh, dwg, dwu, dwd


def rope_tables(positions):
    inv_freq = ROPE_THETA ** (-jnp.arange(0, ROPE_DIM, 2, dtype=F32) / ROPE_DIM)
    ang = positions.reshape(-1, 1).astype(F32) * inv_freq
    c, s = jnp.cos(ang), jnp.sin(ang)
    t = ang.shape[0]
    cosv = jnp.concatenate([c, c, jnp.ones((t, HEAD_DIM - ROPE_DIM), F32)], axis=1)
    sinv = jnp.concatenate([-s, s, jnp.zeros((t, HEAD_DIM - ROPE_DIM), F32)], axis=1)
    return jnp.tile(cosv, (1, 2)), jnp.tile(sinv, (1, 2))


def _pair_masks():
    lane = lax.broadcasted_iota(jnp.int32, (1, LANES), 1)
    return (lane < HEAD_DIM, lane >= HEAD_DIM)


def _band_masks():
    row = lax.broadcasted_iota(jnp.int32, (ATTN_BLOCK, ATTN_BLOCK), 0)
    col = lax.broadcasted_iota(jnp.int32, (ATTN_BLOCK, ATTN_BLOCK), 1)
    return col >= row, col <= row


def _residue_blocks():
    out = []
    for g, d in enumerate(DILATIONS):
        for r in range(d):
            for i in range(SEQ // d // ATTN_BLOCK):
                rows = lambda j: pl.ds(r + j * ATTN_BLOCK * d, ATTN_BLOCK, stride=d) if d > 1 else pl.ds(j * ATTN_BLOCK, ATTN_BLOCK)
                out.append((g, rows(i), rows(i - 1) if i > 0 else None))
    return out


N_HEAD_PAIRS = D_ATTN // LANES
SCALE = HEAD_DIM ** -0.5
ATTN_GROUP = 4


def _block_operands(qr, kr, v_ref, cur, prev):
    prev_ok, cur_ok = _band_masks()
    if prev is None:
        return qr[cur, :], kr[cur, :].astype(BF16), v_ref[cur, :], cur_ok
    kcat = jnp.concatenate([kr[prev, :], kr[cur, :]], axis=0).astype(BF16)
    vcat = jnp.concatenate([v_ref[prev, :], v_ref[cur, :]], axis=0)
    return qr[cur, :], kcat, vcat, jnp.concatenate([prev_ok, cur_ok], axis=1)


def _attn_specs(b):
    col = lambda cb: pl.BlockSpec((SEQ, LANES), lambda bb, hp: (bb, cb + hp))
    tab = pl.BlockSpec((SEQ, LANES), lambda bb, hp: (bb, 0))
    return col, tab


def attn_fwd(qkvz, cosv, sinv, b):
    t = qkvz.shape[0]
    col, tab = _attn_specs(b)
    blocks = _residue_blocks()

    def body(q_ref, k_ref, v_ref, c_ref, s_ref, o_ref, l1_ref, l2_ref, l3_ref, qr, kr, o1, o2, o3):
        l_refs, o_scr = (l1_ref, l2_ref, l3_ref), (o1, o2, o3)
        c, s = c_ref[...], s_ref[...]
        q, k = q_ref[...], k_ref[...]
        qr[...] = q * c + _rot(q) * s
        kr[...] = k * c + _rot(k) * s
        masks = _pair_masks()
        for lo in range(0, len(blocks), ATTN_GROUP):
            chains = []
            for g, cur, prev in blocks[lo:lo + ATTN_GROUP]:
                q2, kcat, vcat, ok = _block_operands(qr, kr, v_ref, cur, prev)
                for m in masks:
                    qm = jnp.where(m, q2, 0.0).astype(BF16)
                    chains.append(dict(g=g, cur=cur, m=m, v=jnp.where(m, vcat, 0.0).astype(BF16),
                                       s=jnp.where(ok, _dot(qm, kcat, _NT) * SCALE, NEG)))
            for ch in chains:
                mx = jnp.max(ch['s'], axis=1, keepdims=True)
                p = jnp.exp(ch['s'] - mx)
                den = jnp.sum(p, axis=1, keepdims=True)
                ch.update(p=p.astype(BF16), inv=1.0 / den, lse=mx + jnp.log(den))
            for ch in chains:
                ch['o'] = _dot(ch['p'], ch['v'], _NN) * ch['inv']
            for c0, c1 in zip(chains[0::2], chains[1::2]):
                o_scr[c0['g']][c0['cur'], :] = c0['o'] + c1['o']
                l_refs[c0['g']][c0['cur'], :] = jnp.where(c0['m'], c0['lse'], c1['lse'])
        w1, w2, w3 = _branch_weights(l1_ref[...], l2_ref[...], l3_ref[...])
        o_ref[...] = w1 * o1[...] + w2 * o2[...] + w3 * o3[...]

    shp = jax.ShapeDtypeStruct((t, D_ATTN), F32)
    return pl.pallas_call(
        body, name="attn_fwd", grid=(b, N_HEAD_PAIRS),
        in_specs=[col(0), col(N_HEAD_PAIRS), col(2 * N_HEAD_PAIRS), tab, tab],
        out_specs=[col(0)] * 4, out_shape=[shp] * 4,
        scratch_shapes=[pltpu.VMEM((SEQ, LANES), F32)] * 5,
        compiler_params=_cparams(("parallel", "parallel")),
    )(qkvz, qkvz, qkvz, cosv, sinv)


def attn_bwd(qkvz, cosv, sinv, dmix, mixed, lses, b):
    t = qkvz.shape[0]
    col, tab = _attn_specs(b)
    blocks = _residue_blocks()
    hd = np.arange(LANES) // HEAD_DIM
    head_ones = jnp.asarray((hd[:, None] == hd[None, :]).astype(np.float32))

    def body(q_ref, k_ref, v_ref, c_ref, s_ref, dm_ref, mx_ref, l1_ref, l2_ref, l3_ref, ones_ref,
             dq_out, dk_out, dv_out, qr, kr, do1, do2, do3, dd1, dd2, dd3, dq_ref, dk_ref, dv_ref):
        l_refs, do_scr, dd_scr = (l1_ref, l2_ref, l3_ref), (do1, do2, do3), (dd1, dd2, dd3)
        c, s = c_ref[...], s_ref[...]
        q, k = q_ref[...], k_ref[...]
        qr[...] = q * c + _rot(q) * s
        kr[...] = k * c + _rot(k) * s
        dm = dm_ref[...]
        tot = _dot(dm * mx_ref[...], ones_ref[...], _NN, HI)
        for w, do_g, dd_g in zip(_branch_weights(l1_ref[...], l2_ref[...], l3_ref[...]), do_scr, dd_scr):
            do_g[...] = w * dm
            dd_g[...] = w * tot
        dq_ref[...] = jnp.zeros((SEQ, LANES), F32)
        dk_ref[...] = jnp.zeros((SEQ, LANES), F32)
        dv_ref[...] = jnp.zeros((SEQ, LANES), F32)
        masks = _pair_masks()
        for lo in range(0, len(blocks), ATTN_GROUP):
            chains = []
            for g, cur, prev in blocks[lo:lo + ATTN_GROUP]:
                q2, kcat, vcat, ok = _block_operands(qr, kr, v_ref, cur, prev)
                vcat = vcat.astype(BF16)
                do2_, l2, dd2_ = do_scr[g][cur, :], l_refs[g][cur, :], dd_scr[g][cur, :]
                l2s, dd2s = pltpu.roll(l2, HEAD_DIM, 1), pltpu.roll(dd2_, HEAD_DIM, 1)
                for m in masks:
                    qm = jnp.where(m, q2, 0.0).astype(BF16)
                    dom = jnp.where(m, do2_, 0.0).astype(BF16)
                    lrep, ddrep = jnp.where(m, l2, l2s), jnp.where(m, dd2_, dd2s)
                    if prev is not None:
                        lrep, ddrep = jnp.concatenate([lrep, lrep], axis=1), jnp.concatenate([ddrep, ddrep], axis=1)
                    chains.append(dict(cur=cur, prev=prev, qm=qm, dom=dom, km=jnp.where(m, kcat, 0), lrep=lrep, ddrep=ddrep,
                                       s=jnp.where(ok, _dot(qm, kcat, _NT) * SCALE, NEG), dp=_dot(dom, vcat, _NT)))
            for ch in chains:
                p = jnp.exp(ch['s'] - ch['lrep'])
                ch.update(p=p.astype(BF16), ds=(p * (ch['dp'] - ch['ddrep']) * SCALE).astype(BF16))
            for ch in chains:
                ch.update(dq=_dot(ch['ds'], ch['km'], _NN), dk=_dot(ch['ds'], ch['qm'], _TN), dv=_dot(ch['p'], ch['dom'], _TN))
            for c0, c1 in zip(chains[0::2], chains[1::2]):
                cur, prev = c0['cur'], c0['prev']
                dk, dv = c0['dk'] + c1['dk'], c0['dv'] + c1['dv']
                dq_ref[cur, :] += c0['dq'] + c1['dq']
                if prev is None:
                    dk_ref[cur, :] += dk
                    dv_ref[cur, :] += dv
                else:
                    dk_ref[prev, :] += dk[:ATTN_BLOCK]
                    dv_ref[prev, :] += dv[:ATTN_BLOCK]
                    dk_ref[cur, :] += dk[ATTN_BLOCK:]
                    dv_ref[cur, :] += dv[ATTN_BLOCK:]
        dq, dk = dq_ref[...], dk_ref[...]
        dq_out[...] = (dq * c + _rot(dq * s)).astype(dq_out.dtype)
        dk_out[...] = (dk * c + _rot(dk * s)).astype(dk_out.dtype)
        dv_out[...] = dv_ref[...].astype(dv_out.dtype)

    shp = jax.ShapeDtypeStruct((t, D_ATTN), BF16)
    return pl.pallas_call(
        body, name="attn_bwd", grid=(b, N_HEAD_PAIRS),
        in_specs=[col(0), col(N_HEAD_PAIRS), col(2 * N_HEAD_PAIRS), tab, tab, col(0), col(0), col(0), col(0), col(0),
                  pl.BlockSpec((LANES, LANES), lambda bb, hp: (0, 0))],
        out_specs=[col(0)] * 3, out_shape=[shp] * 3,
        scratch_shapes=[pltpu.VMEM((SEQ, LANES), F32)] * 11,
        compiler_params=_cparams(("parallel", "parallel")),
    )(qkvz, qkvz, qkvz, cosv, sinv, dmix, mixed, *lses, head_ones)


def attn_norm_fwd(mixed, norm_w):
    return rowwise("attn_norm", _rms, [mixed], [norm_w], [(mixed.shape[0], D_ATTN, BF16)])[0]


def attn_norm_bwd(dout, mixed, norm_w):
    def fn(dy, mx, w):
        _, vjp = jax.vjp(_rms, mx, w)
        return vjp(dy)

    return rowwise("attn_norm_bwd", fn, [dout, mixed], [norm_w], [(dout.shape[0], D_ATTN, F32)], accs=[(1, D_ATTN)])


CONV_TM = 256
HALO = 8


def _conv_columns(refs):
    xs_ref, bm_ref, cm_ref = refs
    out = []
    for c in range(D_CONV // LANES):
        lo = c * LANES
        ref, base = (xs_ref, 0) if lo < D_SSD else (bm_ref, D_SSD) if lo < D_SSD + D_BC else (cm_ref, D_SSD + D_BC)
        out.append((slice(lo, lo + LANES), (ref, slice(lo - base, lo - base + LANES))))
    return out


def _conv_taps(scr, w_ref, cs, first_row, step, tm):
    acc = None
    for k in range(CONV_WIDTH):
        term = w_ref[k:k + 1, cs] * scr[pl.ds(first_row + step * k, tm), cs]
        acc = term if acc is None else acc + term
    return acc


def conv_fwd(u, w, bias):
    t = u.shape[0]
    tm, per_seq = CONV_TM, SEQ // CONV_TM

    def body(u_ref, h_ref, w_ref, b_ref, xs_ref, bm_ref, cm_ref, scr):
        first = pl.program_id(0) % per_seq == 0
        scr[0:HALO, :] = jnp.where(first, 0.0, h_ref[...])
        scr[HALO:, :] = u_ref[...]
        for cs, (o_ref, os_) in _conv_columns((xs_ref, bm_ref, cm_ref)):
            o_ref[:, os_] = _silu(_conv_taps(scr, w_ref, cs, HALO - CONV_WIDTH + 1, 1, tm) + b_ref[:, cs])

    return pl.pallas_call(
        body, name="conv_fwd", grid=(t // tm,),
        in_specs=[pl.BlockSpec((tm, D_CONV), lambda i: (i, 0)),
                  pl.BlockSpec((HALO, D_CONV), lambda i: (jnp.maximum(i * (tm // HALO) - 1, 0), 0)),
                  pl.BlockSpec((CONV_WIDTH, D_CONV), lambda i: (0, 0)), pl.BlockSpec((1, D_CONV), lambda i: (0, 0))],
        out_specs=[pl.BlockSpec((tm, D_SSD), lambda i: (i, 0)), pl.BlockSpec((tm, D_BC), lambda i: (i, 0)),
                   pl.BlockSpec((tm, D_BC), lambda i: (i, 0))],
        out_shape=[jax.ShapeDtypeStruct((t, D_SSD), F32), jax.ShapeDtypeStruct((t, D_BC), F32),
                   jax.ShapeDtypeStruct((t, D_BC), F32)],
        scratch_shapes=[pltpu.VMEM((tm + HALO, D_CONV), F32)],
        compiler_params=_cparams(("parallel",)),
    )(u, u, w, bias)


def conv_bwd(u, w, bias, dxs_a, dxs_b, dbm, dcm):
    t = u.shape[0]
    tm, per_seq = CONV_TM, SEQ // CONV_TM
    n_tiles = t // tm

    def body1(u_ref, h_ref, dxs_ref, dxs2_ref, dbm_ref, dcm_ref, w_ref, b_ref, dz_ref, dw_ref, db_ref, scr):
        i = pl.program_id(0)
        first = i % per_seq == 0
        scr[0:HALO, :] = jnp.where(first, 0.0, h_ref[...])
        scr[HALO:, :] = u_ref[...]

        @pl.when(i == 0)
        def _():
            dw_ref[...] = jnp.zeros(dw_ref.shape, F32)
            db_ref[...] = jnp.zeros(db_ref.shape, F32)
        for cs, (g_ref, gs) in _conv_columns((dxs_ref, dbm_ref, dcm_ref)):
            acc = _conv_taps(scr, w_ref, cs, HALO - CONV_WIDTH + 1, 1, tm) + b_ref[:, cs]
            sig = _sigmoid(acc)
            dy = g_ref[:, gs] + dxs2_ref[:, gs] if g_ref is dxs_ref else g_ref[:, gs]
            dz = dy * sig * (1.0 + acc * (1.0 - sig))
            dz_ref[:, cs] = dz
            db_ref[:, cs] += jnp.sum(dz, axis=0, keepdims=True)
            for k in range(CONV_WIDTH):
                dw_ref[k:k + 1, cs] += jnp.sum(dz * scr[pl.ds(HALO - CONV_WIDTH + 1 + k, tm), cs], axis=0, keepdims=True)

    dz, dw, db = pl.pallas_call(
        body1, name="conv_bwd_dz", grid=(n_tiles,),
        in_specs=[pl.BlockSpec((tm, D_CONV), lambda i: (i, 0)),
                  pl.BlockSpec((HALO, D_CONV), lambda i: (jnp.maximum(i * (tm // HALO) - 1, 0), 0)),
                  pl.BlockSpec((tm, D_SSD), lambda i: (i, 0)), pl.BlockSpec((tm, D_SSD), lambda i: (i, 0)),
                  pl.BlockSpec((tm, D_BC), lambda i: (i, 0)), pl.BlockSpec((tm, D_BC), lambda i: (i, 0)),
                  pl.BlockSpec((CONV_WIDTH, D_CONV), lambda i: (0, 0)), pl.BlockSpec((1, D_CONV), lambda i: (0, 0))],
        out_specs=[pl.BlockSpec((tm, D_CONV), lambda i: (i, 0)), pl.BlockSpec((CONV_WIDTH, D_CONV), lambda i: (0, 0)),
                   pl.BlockSpec((1, D_CONV), lambda i: (0, 0))],
        out_shape=[jax.ShapeDtypeStruct((t, D_CONV), F32), jax.ShapeDtypeStruct((CONV_WIDTH, D_CONV), F32),
                   jax.ShapeDtypeStruct((1, D_CONV), F32)],
        scratch_shapes=[pltpu.VMEM((tm + HALO, D_CONV), F32)],
        compiler_params=_cparams(("arbitrary",)),
    )(u, u, dxs_a, dxs_b, dbm, dcm, w, bias)

    def body2(dz_ref, n_ref, w_ref, du_ref, scr):
        last = pl.program_id(0) % per_seq == per_seq - 1
        scr[0:tm, :] = dz_ref[...]
        scr[tm:, :] = jnp.where(last, 0.0, n_ref[...])
        for c in range(D_CONV // LANES):
            cs = slice(c * LANES, (c + 1) * LANES)
            du_ref[:, cs] = _conv_taps(scr, w_ref, cs, CONV_WIDTH - 1, -1, tm).astype(du_ref.dtype)

    du = pl.pallas_call(
        body2, name="conv_bwd_du", grid=(n_tiles,),
        in_specs=[pl.BlockSpec((tm, D_CONV), lambda i: (i, 0)),
                  pl.BlockSpec((HALO, D_CONV), lambda i: (jnp.minimum((i + 1) * (tm // HALO), t // HALO - 1), 0)),
                  pl.BlockSpec((CONV_WIDTH, D_CONV), lambda i: (0, 0))],
        out_specs=pl.BlockSpec((tm, D_CONV), lambda i: (i, 0)),
        out_shape=jax.ShapeDtypeStruct((t, D_CONV), BF16),
        scratch_shapes=[pltpu.VMEM((tm + HALO, D_CONV), F32)],
        compiler_params=_cparams(("parallel",)),
    )(dz, dz, w)
    return du, dw, db


Q = SSD_CHUNK
N_PAIRS = D_SSD // LANES
HEADS_PER_GROUP = N_HEADS // SSD_GROUPS


def _rep(a, j):
    return jnp.broadcast_to(a[:, j:j + 1], a.shape)


def _dot_exact01(a, b, dn, a_is_01):
    x = b if a_is_01 else a
    hi = x.astype(BF16)
    mid = (x - hi.astype(F32)).astype(BF16)
    lo = (x - hi.astype(F32) - mid.astype(F32)).astype(BF16)
    z = a.astype(BF16) if a_is_01 else b.astype(BF16)
    out = None
    for term in (hi, mid, lo):
        d = _dot(z, term, dn) if a_is_01 else _dot(term, z, dn)
        out = d if out is None else out + d
    return out


def _pad_lanes(v, fill=0.0):
    row = jnp.pad(v.reshape(1, -1).astype(F32), ((0, 0), (0, LANES - v.size)), constant_values=fill)
    return row, row.reshape(LANES, 1)


def _ssd_common(dtr_ref, dtrt_ref, bias_r, bias_c, alog_r, alog_c):
    row = lax.broadcasted_iota(jnp.int32, (Q, Q), 0)
    col = lax.broadcasted_iota(jnp.int32, (Q, Q), 1)
    tril = row >= col
    lane = lax.broadcasted_iota(jnp.int32, (1, LANES), 1)
    a_r = jnp.where(lane < N_HEADS, -jnp.exp(alog_r[...]), 0.0)
    sub = lax.broadcasted_iota(jnp.int32, (LANES, 1), 0)
    a_c = jnp.where(sub < N_HEADS, -jnp.exp(alog_c[...]), 0.0)
    dt = _softplus(dtr_ref[...] + bias_r[...])
    cs = _dot_exact01(tril, dt * a_r, _NN, True)
    dtt = _softplus(dtrt_ref[...] + bias_c[...])
    cst = _dot_exact01(dtt * a_c, row <= col, _NN, False)
    return tril, lane, a_r, dt, cs, cst


def _ssd_specs(b, nc, rev):
    ci = (lambda c: nc - 1 - c) if rev else (lambda c: c)
    rows = lambda w: pl.BlockSpec((Q, w), lambda bb, c: (bb * nc + ci(c), 0))
    dtt = pl.BlockSpec((LANES, Q), lambda bb, c: (0, bb * nc + ci(c)))
    const = lambda s: pl.BlockSpec(s, lambda bb, c: (0,) * len(s))
    state = pl.BlockSpec((None, N_PAIRS, LANES, SSD_STATE), lambda bb, c: (bb * nc + ci(c), 0, 0, 0))
    return rows, dtt, const, state


def ssd_fwd(xs, bm, cm, dtraw, dt_bias, a_log, b):
    t = xs.shape[0]
    nc = SEQ // Q
    rows, dtt_spec, const, state = _ssd_specs(b, nc, False)
    bias_r, bias_c = _pad_lanes(dt_bias)
    alog_r, alog_c = _pad_lanes(a_log)

    def body(xs_ref, b_ref, c_ref, dtr_ref, dtrt_ref, br, bc, ar, ac, y_ref, hp_ref, h_scr):
        @pl.when(pl.program_id(1) == 0)
        def _():
            h_scr[...] = jnp.zeros(h_scr.shape, F32)
        tril, lane, _, dt, cs, cst = _ssd_common(dtr_ref, dtrt_ref, br, bc, ar, ac)
        sub = lax.broadcasted_iota(jnp.int32, (LANES, 1), 0)
        y_acc = [jnp.zeros((Q, LANES), F32) for _ in range(N_PAIRS)]
        h_old = [h_scr[p] for p in range(N_PAIRS)]
        h_new = [jnp.zeros((LANES, SSD_STATE), F32) for _ in range(N_PAIRS)]
        for g in range(SSD_GROUPS):
            bg = b_ref[:, g * SSD_STATE:(g + 1) * SSD_STATE].astype(BF16)
            cg = c_ref[:, g * SSD_STATE:(g + 1) * SSD_STATE].astype(BF16)
            cb = _dot(cg, bg, _NT)
            for j in range(g * HEADS_PER_GROUP, (g + 1) * HEADS_PER_GROUP):
                p, side = j // 2, j % 2
                m = (lane < HEAD_DIM) if side == 0 else (lane >= HEAD_DIM)
                ms = (sub < HEAD_DIM) if side == 0 else (sub >= HEAD_DIM)
                csj, dtj = _rep(cs, j), _rep(dt, j)
                lmat = jnp.exp(jnp.where(tril, csj - cst[j:j + 1, :], NEG))
                xdt = jnp.where(m, xs_ref[:, p * LANES:(p + 1) * LANES] * dtj, 0.0)
                hm = jnp.where(ms, h_old[p], 0.0)
                ydiag = _dot((cb * lmat).astype(BF16), xdt.astype(BF16), _NN)
                yoff = jnp.exp(csj) * _dot(cg, hm.astype(BF16), _NT)
                y_acc[p] = y_acc[p] + ydiag + yoff
                last = csj[Q - 1:Q, :]
                sj = _dot((xdt * jnp.exp(last - csj)).astype(BF16), bg, _TN)
                h_new[p] = h_new[p] + jnp.exp(last) * hm + sj
        for p in range(N_PAIRS):
            y_ref[:, p * LANES:(p + 1) * LANES] = y_acc[p]
            hp_ref[p] = h_old[p]
            h_scr[p] = h_new[p]

    return pl.pallas_call(
        body, name="ssd_fwd", grid=(b, nc),
        in_specs=[rows(D_SSD), rows(D_BC), rows(D_BC), rows(LANES), dtt_spec, const((1, LANES)), const((LANES, 1)),
                  const((1, LANES)), const((LANES, 1))],
        out_specs=[rows(D_SSD), state],
        out_shape=[jax.ShapeDtypeStruct((t, D_SSD), F32),
                   jax.ShapeDtypeStruct((b * nc, N_PAIRS, LANES, SSD_STATE), F32)],
        scratch_shapes=[pltpu.VMEM((N_PAIRS, LANES, SSD_STATE), F32)],
        compiler_params=_cparams(("parallel", "arbitrary")),
    )(xs, bm, cm, dtraw, dtraw.T, bias_r, bias_c, alog_r, alog_c)


def ssd_bwd(xs, bm, cm, dtraw, dt_bias, a_log, hprev, dy, b):
    t = xs.shape[0]
    nc = SEQ // Q
    rows, dtt_spec, const, state = _ssd_specs(b, nc, True)
    bias_r, bias_c = _pad_lanes(dt_bias)
    alog_r, alog_c = _pad_lanes(a_log)

    def body(xs_ref, b_ref, c_ref, dtr_ref, dtrt_ref, hp_ref, dy_ref, br, bc, ar, ac,
             dxs_ref, db_ref, dc_ref, ddt_ref, dbias_ref, dalog_ref, dh_scr):
        first = jnp.logical_and(pl.program_id(0) == 0, pl.program_id(1) == 0)

        @pl.when(pl.program_id(1) == 0)
        def _():
            dh_scr[...] = jnp.zeros(dh_scr.shape, F32)

        @pl.when(first)
        def _():
            dbias_ref[...] = jnp.zeros(dbias_ref.shape, F32)
            dalog_ref[...] = jnp.zeros(dalog_ref.shape, F32)
        tril, lane, a_r, dt, cs, cst = _ssd_common(dtr_ref, dtrt_ref, br, bc, ar, ac)
        sub = lax.broadcasted_iota(jnp.int32, (LANES, 1), 0)
        rowq = lax.broadcasted_iota(jnp.int32, (Q, 1), 0)
        triu = (lax.broadcasted_iota(jnp.int32, (Q, Q), 0) <= lax.broadcasted_iota(jnp.int32, (Q, Q), 1)).astype(F32)
        dxs_acc = [jnp.zeros((Q, LANES), F32) for _ in range(N_PAIRS)]
        dh_in = [dh_scr[p] for p in range(N_PAIRS)]
        h_in = [hp_ref[p] for p in range(N_PAIRS)]
        dh_out = [jnp.zeros((LANES, SSD_STATE), F32) for _ in range(N_PAIRS)]
        ddt = jnp.zeros((Q, LANES), F32)
        dalog = jnp.zeros((1, LANES), F32)
        for g in range(SSD_GROUPS):
            gs = slice(g * SSD_STATE, (g + 1) * SSD_STATE)
            bg, cg = b_ref[:, gs].astype(BF16), c_ref[:, gs].astype(BF16)
            cb = _dot(cg, bg, _NT)
            dcb = jnp.zeros((Q, Q), F32)
            dbg = jnp.zeros((Q, SSD_STATE), F32)
            dcg = jnp.zeros((Q, SSD_STATE), F32)
            for j in range(g * HEADS_PER_GROUP, (g + 1) * HEADS_PER_GROUP):
                p, side = j // 2, j % 2
                m = (lane < HEAD_DIM) if side == 0 else (lane >= HEAD_DIM)
                ms = (sub < HEAD_DIM) if side == 0 else (sub >= HEAD_DIM)
                csj, dtj = _rep(cs, j), _rep(dt, j)
                lmat = jnp.exp(jnp.where(tril, csj - cst[j:j + 1, :], NEG))
                x2 = jnp.where(m, xs_ref[:, p * LANES:(p + 1) * LANES], 0.0)
                xdt = x2 * dtj
                dym = jnp.where(m, dy_ref[:, p * LANES:(p + 1) * LANES], 0.0)
                hm = jnp.where(ms, h_in[p], 0.0)
                dhm = jnp.where(ms, dh_in[p], 0.0)
                ecs = jnp.exp(csj)
                last = csj[Q - 1:Q, :]
                decay = jnp.exp(last - csj)
                el = jnp.exp(last)
                gmat = cb * lmat
                dymb, xdtb = dym.astype(BF16), xdt.astype(BF16)
                dg = _dot(dymb, xdtb, _NT)
                dxdt = _dot(gmat.astype(BF16), dymb, _TN)
                dcb = dcb + dg * lmat
                ej = dg * gmat
                col_sums = jnp.broadcast_to(jnp.sum(ej, axis=0, keepdims=True), (Q, Q)).T
                dcs = jnp.sum(ej, axis=1, keepdims=True) - col_sums
                ch = _dot(cg, hm.astype(BF16), _NT)
                dye = dym * ecs
                dcs = dcs + jnp.sum(dye * ch, axis=1, keepdims=True)
                dcg = dcg + _dot(dye.astype(BF16), hm.astype(BF16), _NN)
                dhp = _dot(dye.astype(BF16), cg, _TN)
                wmat = _dot(bg, dhm.astype(BF16), _NT)
                xd = xdt * decay
                dxdt = dxdt + decay * wmat
                ddl = jnp.sum(xd * wmat, axis=1, keepdims=True)
                dlast = jnp.sum(ddl, axis=0, keepdims=True) + el * jnp.sum(jnp.sum(dhm * hm, axis=1, keepdims=True), axis=0, keepdims=True)
                dcs = dcs - ddl + jnp.where(rowq == Q - 1, dlast, 0.0)
                dbg = dbg + _dot(xd.astype(BF16), dhm.astype(BF16), _NN)
                dh_out[p] = dh_out[p] + el * dhm + dhp
                da = _dot_exact01(triu, dcs, _NN, True)
                aj = jnp.sum(jnp.where(lane == j, a_r, 0.0), axis=1, keepdims=True)
                ddtj = da * aj + jnp.sum(dxdt * x2, axis=1, keepdims=True)
                ddt = ddt + jnp.where(lane == j, ddtj, 0.0)
                dalog = dalog + jnp.where(lane == j, jnp.sum(da * dtj, axis=0, keepdims=True) * aj, 0.0)
                dxs_acc[p] = dxs_acc[p] + dxdt * dtj
            dcbb = dcb.astype(BF16)
            dc_ref[:, gs] = dcg + _dot(dcbb, bg, _NN)
            db_ref[:, gs] = dbg + _dot(dcbb, cg, _TN)
        for p in range(N_PAIRS):
            dxs_ref[:, p * LANES:(p + 1) * LANES] = dxs_acc[p]
            dh_scr[p] = dh_out[p]
        ddtraw = ddt * _sigmoid(dtr_ref[...] + br[...])
        ddt_ref[...] = ddtraw
        dbias_ref[...] += jnp.sum(ddtraw, axis=0, keepdims=True)
        dalog_ref[...] += dalog

    return pl.pallas_call(
        body, name="ssd_bwd", grid=(b, nc),
        in_specs=[rows(D_SSD), rows(D_BC), rows(D_BC), rows(LANES), dtt_spec, state, rows(D_SSD), const((1, LANES)),
                  const((LANES, 1)), const((1, LANES)), const((LANES, 1))],
        out_specs=[rows(D_SSD), rows(D_BC), rows(D_BC), rows(LANES), const((1, LANES)), const((1, LANES))],
        out_shape=[jax.ShapeDtypeStruct((t, D_SSD), F32), jax.ShapeDtypeStruct((t, D_BC), F32),
                   jax.ShapeDtypeStruct((t, D_BC), F32), jax.ShapeDtypeStruct((t, LANES), F32),
                   jax.ShapeDtypeStruct((1, LANES), F32), jax.ShapeDtypeStruct((1, LANES), F32)],
        scratch_shapes=[pltpu.VMEM((N_PAIRS, LANES, SSD_STATE), F32)],
        compiler_params=_cparams(("arbitrary", "arbitrary")),
    )(xs, bm, cm, dtraw, dtraw.T, hprev, dy, bias_r, bias_c, alog_r, alog_c)


def _split_w_in(w_in):
    w_dt = jnp.pad(w_in[:, D_QKVZ + D_CONV:], ((0, 0), (0, LANES - N_HEADS)))
    return w_in[:, :D_QKVZ], w_in[:, D_QKVZ:D_QKVZ + D_CONV], w_dt


def mixer_fwd(hb, p, cosv, sinv, b):
    t = hb.shape[0]
    w_a, w_b, w_c = _split_w_in(p['w_in'])
    qkvz = mm("in_qkvz", [(hb, w_a, 'nn')], D_QKVZ)
    xbc = mm("in_xbc", [(hb, w_b, 'nn')], D_CONV)
    dtraw = mm("in_dt", [(hb, w_c, 'nn')], LANES)
    mixed, *lses = attn_fwd(qkvz, cosv, sinv, b)
    attn = attn_norm_fwd(mixed, p['attn_norm_w'])
    xs, bm, cm = conv_fwd(xbc, p['conv_w'], p['conv_b'])
    y, hprev = ssd_fwd(xs, bm, cm, dtraw, p['dt_bias'], p['a_log'], b)
    dskip = jnp.repeat(p['d_skip'].reshape(-1), HEAD_DIM).reshape(1, D_SSD)
    yg, = rowwise("ssd_gate", _gate, [y, xs, Op(qkvz, D_SSD, 3)], [dskip, p['ssd_norm_w']], [(t, D_SSD, BF16)])
    mix = mm("out_proj", [(attn, p['w_out'][:D_ATTN], 'nn'), (yg, p['w_out'][D_ATTN:], 'nn')], D_MODEL)
    res = dict(hb=hb, qkvz=qkvz, xbc=xbc, dtraw=dtraw, mixed=mixed, lses=lses, attn=attn, xs=xs, bm=bm, cm=cm,
               y=y, hprev=hprev, dskip=dskip, yg=yg, cosv=cosv, sinv=sinv)
    return mix, res


def mixer_bwd(r, p, dmix, dh_resid, b):
    t = dmix.shape[0]
    w_a, w_b, w_c = _split_w_in(p['w_in'])
    w_out = p['w_out']
    dattn = mm("out_bwd_dattn", [(dmix, w_out[:D_ATTN], 'nt')], D_ATTN)
    dyg = mm("out_bwd_dyg", [(dmix, w_out[D_ATTN:], 'nt')], D_SSD)
    dw_out = jnp.concatenate([mm_tn("out_bwd_dw_a", r['attn'], dmix, BF16),
                              mm_tn("out_bwd_dw_y", r['yg'], dmix, BF16)], axis=0)

    def gate_bwd(dy_, y_, xs_, z_, ds_, w_):
        _, vjp = jax.vjp(_gate, y_, xs_, z_, ds_, w_)
        return vjp(dy_)

    dy, dxs_a, dz, ddskip, dssd_norm = rowwise(
        "ssd_gate_bwd", gate_bwd, [dyg, r['y'], r['xs'], Op(r['qkvz'], D_SSD, 3)], [r['dskip'], p['ssd_norm_w']],
        [(t, D_SSD, F32), (t, D_SSD, F32), (t, D_SSD, BF16)], accs=[(1, D_SSD), (1, D_SSD)])
    dxs_b, dbm, dcm, ddtraw, ddt_bias, da_log = ssd_bwd(r['xs'], r['bm'], r['cm'], r['dtraw'], p['dt_bias'], p['a_log'],
                                                        r['hprev'], dy, b)
    dxbc, dconv_w, dconv_b = conv_bwd(r['xbc'], p['conv_w'], p['conv_b'], dxs_a, dxs_b, dbm, dcm)
    dmixed, dattn_norm = attn_norm_bwd(dattn, r['mixed'], p['attn_norm_w'])
    dq, dk, dv = attn_bwd(r['qkvz'], r['cosv'], r['sinv'], dmixed, r['mixed'], r['lses'], b)
    wq, wk, wv, wz = (w_a[:, i * D_ATTN:(i + 1) * D_ATTN] for i in range(4))
    dh = mm("in_bwd_dh", [(dq, wq, 'nt'), (dk, wk, 'nt'), (dv, wv, 'nt'), (dz, wz, 'nt'), (dxbc, w_b, 'nt'),
                          (ddtraw, w_c, 'nt')], D_MODEL, add=dh_resid, tn=512)
    h = r['hb']
    dw_in = jnp.concatenate([mm_tn("in_bwd_dwq", h, dq, BF16), mm_tn("in_bwd_dwk", h, dk, BF16),
                             mm_tn("in_bwd_dwv", h, dv, BF16), mm_tn("in_bwd_dwz", h, dz, BF16),
                             mm_tn("in_bwd_dwx", h, dxbc, BF16), mm_tn("in_bwd_dwdt", h, ddtraw, BF16)[:, :N_HEADS]], axis=1)
    head_sum = lambda v: v.reshape(N_HEADS, HEAD_DIM).sum(axis=1).reshape(1, N_HEADS)
    grads = dict(w_in=dw_in, w_out=dw_out, conv_w=dconv_w, conv_b=dconv_b, dt_bias=ddt_bias[:, :N_HEADS],
                 a_log=da_log[:, :N_HEADS], d_skip=head_sum(ddskip), attn_norm_w=dattn_norm, ssd_norm_w=dssd_norm)
    return dh, grads


FFN1_KEYS = ('ffn1_gate', 'ffn1_up', 'ffn1_down')
FFN2_KEYS = ('ffn2_gate', 'ffn2_up', 'ffn2_down')
MIXER_KEYS = ('w_in', 'conv_w', 'w_out')
FFN_COL = ('ffn1_gate', 'ffn1_up', 'ffn2_gate', 'ffn2_up')
FFN_ROW = ('ffn1_down', 'ffn2_down')
CONV_W_COMM = (8, 2 * LANES)
SMALL = 'small'


def comm_shape(k, shapes):
    if k in FFN_COL:
        return (D_MODEL, FF_PAD)
    if k in FFN_ROW:
        return (FF_PAD, D_MODEL)
    if k == 'conv_w':
        return CONV_W_COMM
    return tuple(shapes[k][1:])


def to_comm(k, vals, shapes):
    a = vals[k].reshape(shapes[k][1:])
    r_, c_ = comm_shape(k, shapes)
    return jnp.pad(a, ((0, r_ - a.shape[0]), (0, c_ - a.shape[1])))


SMALL_ROWS, SMALL_COLS = 16, D_CONV


def pack_small(small):
    rows = [jnp.pad(small[r].reshape(1, -1), ((0, 0), (0, SMALL_COLS - small[r].size))) for r in REPLICATED]
    return jnp.concatenate(rows + [jnp.zeros((SMALL_ROWS - len(rows), SMALL_COLS), F32)], axis=0)


def full_weight(k, g):
    if k in FFN_COL:
        return g
    if k == 'conv_w':
        return jnp.transpose(g[:, :CONV_WIDTH, :D_CONV // N_DEV], (1, 0, 2)).reshape(CONV_WIDTH, D_CONV)
    return g.reshape(N_DEV * g.shape[1], g.shape[2])


def grad_shards(k, g):
    if k in FFN_COL:
        return g
    if k == 'conv_w':
        s = jnp.transpose(g.reshape(CONV_WIDTH, N_DEV, D_CONV // N_DEV), (1, 0, 2))
        return jnp.pad(s, ((0, 0), (0, CONV_W_COMM[0] - CONV_WIDTH), (0, CONV_W_COMM[1] - D_CONV // N_DEV)))
    return g.reshape(N_DEV, g.shape[0] // N_DEV, g.shape[1])


def _flip(v, bit):
    return 1 - v if bit else v


N_PEER_COPIES = N_DEV - 1


def _comm_call(name, body, arrs, out_shape):
    n = len(arrs)
    return pl.pallas_call(
        functools.partial(body, n), name=name, out_shape=out_shape,
        in_specs=[pl.BlockSpec(memory_space=pl.ANY)] * n, out_specs=[pl.BlockSpec(memory_space=pl.ANY)] * n,
        scratch_shapes=[pltpu.SemaphoreType.DMA((n * N_PEER_COPIES,)), pltpu.SemaphoreType.DMA((n * N_PEER_COPIES,)),
                        pltpu.SemaphoreType.DMA((n,))],
    )(*arrs)


def _blk(ref, idx, by_cols):
    if not by_cols:
        return ref.at[idx]
    c = ref.shape[1] // N_DEV
    return ref.at[:, pl.ds(pl.multiple_of(idx * c, LANES), c)]


def _blocked_shape(a, by_cols):
    return (a.shape[0], N_DEV * a.shape[1]) if by_cols else (N_DEV,) + a.shape


def all_gather(arrs, by_cols):
    def body(n, *refs):
        x_refs, out_refs, (send_sems, recv_sems, local_sems) = refs[:n], refs[n:2 * n], refs[2 * n:]
        x, y, c = lax.axis_index("x"), lax.axis_index("y"), lax.axis_index("c")
        me, sibling = (x, y, c), (x, y, 1 - c)
        chips = [(1 - x, y), (x, 1 - y), (1 - x, 1 - y)]

        def copy(a, k, block, to, src=None):
            px, py, pc = block
            dst = _blk(out_refs[a], 4 * px + 2 * py + pc, by_cols[a])
            return pltpu.make_async_remote_copy(
                src_ref=dst if src is None else src, dst_ref=dst, send_sem=send_sems.at[a * N_PEER_COPIES + k],
                recv_sem=recv_sems.at[a * N_PEER_COPIES + k], device_id=to, device_id_type=MESH)

        mine = [pltpu.make_async_copy(x_refs[a], _blk(out_refs[a], 4 * x + 2 * y + c, by_cols[a]), local_sems.at[a])
                for a in range(n)]
        started = []
        for a in range(n):
            mine[a].start()
            first = [copy(a, 0, me, sibling, src=x_refs[a])]
            first += [copy(a, 1 + j, me, (*chip, c), src=x_refs[a]) for j, chip in enumerate(chips)]
            for cp in first:
                cp.start()
            started += first
        for j, chip in enumerate(chips):
            for a in range(n):
                copy(a, 1 + j, (*chip, c), me).wait_recv()
                cp = copy(a, 4 + j, (*chip, c), sibling)
                cp.start()
                started.append(cp)
        for a in range(n):
            copy(a, 0, sibling, me).wait_recv()
            for j, chip in enumerate(chips):
                copy(a, 4 + j, (*chip, 1 - c), me).wait_recv()
        for cp in started:
            cp.wait_send()
        for cp in mine:
            cp.wait()

    return _comm_call("all_gather_weights", body, arrs,
                      [jax.ShapeDtypeStruct(_blocked_shape(a, bc), a.dtype) for a, bc in zip(arrs, by_cols)])


def blocks_to_cols(arrs):
    def body(*refs):
        for i, o in zip(refs[:len(arrs)], refs[len(arrs):]):
            o[...] = i[...]

    return pl.pallas_call(
        body, name="blocks_to_cols", grid=(N_DEV,),
        in_specs=[pl.BlockSpec((None,) + a.shape[1:], lambda p: (p, 0, 0)) for a in arrs],
        out_specs=[pl.BlockSpec(a.shape[1:], lambda p: (0, p)) for a in arrs],
        out_shape=[jax.ShapeDtypeStruct((a.shape[1], N_DEV * a.shape[2]), a.dtype) for a in arrs],
        compiler_params=_cparams(("parallel",)),
    )(*arrs)


def _landing_shape(a, by_cols):
    return (N_DEV, a.shape[0], a.shape[1] // N_DEV) if by_cols else a.shape


def all_to_all(arrs, by_cols):
    def body(n, *refs):
        s_refs, r_refs, (send_sems, recv_sems, local_sems) = refs[:n], refs[n:2 * n], refs[2 * n:]
        x, y, c = lax.axis_index("x"), lax.axis_index("y"), lax.axis_index("c")
        me = 4 * x + 2 * y + c

        def peer(k):
            return _flip(x, k & 4), _flip(y, k & 2), _flip(c, k & 1)

        def copy(a, k, landing):
            px, py, pc = peer(k)
            p = 4 * px + 2 * py + pc
            src, dst = (me, p) if landing else (p, me)
            return pltpu.make_async_remote_copy(
                src_ref=_blk(s_refs[a], src, by_cols[a]), dst_ref=r_refs[a].at[dst],
                send_sem=send_sems.at[a * N_PEER_COPIES + k - 1],
                recv_sem=recv_sems.at[a * N_PEER_COPIES + k - 1], device_id=(px, py, pc), device_id_type=MESH)

        mine = [pltpu.make_async_copy(_blk(s_refs[a], me, by_cols[a]), r_refs[a].at[me], local_sems.at[a]) for a in range(n)]
        sends = [copy(a, k, False) for a in range(n) for k in range(1, N_DEV)]
        for cp in mine + sends:
            cp.start()
        for a in range(n):
            for k in range(1, N_DEV):
                copy(a, k, True).wait_recv()
        for cp in sends:
            cp.wait_send()
        for cp in mine:
            cp.wait()

    return _comm_call("all_to_all_grads", body, arrs,
                      [jax.ShapeDtypeStruct(_landing_shape(a, bc), a.dtype) for a, bc in zip(arrs, by_cols)])


_HBM = pl.BlockSpec(memory_space=pltpu.HBM)
_SEM = pl.BlockSpec(memory_space=pltpu.SEMAPHORE)
_EFFECT = pltpu.SideEffectType.DATAFLOW_SIDE_EFFECTING


def _peer(k):
    x, y, c = lax.axis_index("x"), lax.axis_index("y"), lax.axis_index("c")
    return _flip(x, k & 4), _flip(y, k & 2), _flip(c, k & 1)


def _my_index():
    return 4 * lax.axis_index("x") + 2 * lax.axis_index("y") + lax.axis_index("c")


def _split_copies(mode, by_cols, src_refs, land_refs, send_sems, recv_sems):
    me = _my_index()
    out = []
    for a, bc in enumerate(by_cols):
        for k in range(1, N_DEV):
            px, py, pc = _peer(k)
            src = _blk(src_refs[a], 4 * px + 2 * py + pc, bc) if mode == 'scatter' else src_refs[a]
            dst = land_refs[a].at[me] if mode == 'scatter' else _blk(land_refs[a], me, bc)
            out.append(pltpu.make_async_remote_copy(
                src_ref=src, dst_ref=dst, send_sem=send_sems.at[a * N_PEER_COPIES + k - 1],
                recv_sem=recv_sems.at[a * N_PEER_COPIES + k - 1], device_id=(px, py, pc), device_id_type=MESH))
    return out


def exchange_start(name, mode, srcs, by_cols):
    n = len(srcs)
    lands = [lax.empty(_landing_shape(s, bc) if mode == 'scatter' else _blocked_shape(s, bc), s.dtype)
             for s, bc in zip(srcs, by_cols)]

    def body(*refs):
        src_refs, land_refs, send_sems, recv_sems = refs[:n], refs[n:2 * n], refs[2 * n], refs[2 * n + 1]
        for cp in _split_copies(mode, by_cols, src_refs, land_refs, send_sems, recv_sems):
            cp.start()
        refs[-1][...] = jnp.zeros(refs[-1].shape, F32)

    sems = pltpu.SemaphoreType.DMA((n * N_PEER_COPIES,))
    res = pl.pallas_call(
        body, name=name,
        out_shape=(sems, sems, *[pltpu.HBM(a.shape, a.dtype) for a in srcs + lands], jax.ShapeDtypeStruct((8, LANES), F32)),
        in_specs=(_HBM,) * (2 * n), out_specs=(_SEM, _SEM, *(_HBM,) * (2 * n), pl.BlockSpec(memory_space=pltpu.VMEM)),
        input_output_aliases={i: 2 + i for i in range(2 * n)},
        compiler_params=pltpu.CompilerParams(has_side_effects=_EFFECT),
    )(*[pltpu.with_memory_space_constraint(a, pltpu.HBM) for a in srcs + lands])
    return (mode, by_cols, res[:-1]), res[-1]


def exchange_wait(name, handles, after):
    mode, by_cols, (send_sems, recv_sems, *bufs) = handles
    n = len(by_cols)

    def body(*refs):
        src_refs, land_refs, s_sems, r_sems = refs[:n], refs[n:2 * n], refs[2 * n], refs[2 * n + 1]
        for cp in _split_copies(mode, by_cols, src_refs, land_refs, s_sems, r_sems):
            cp.wait_send()
            cp.wait_recv()

    res = pl.pallas_call(
        body, name=name, out_shape=tuple(pltpu.HBM(a.shape, a.dtype) for a in bufs),
        in_specs=(*(_HBM,) * (2 * n), _SEM, _SEM, pl.BlockSpec(memory_space=pl.ANY)), out_specs=(_HBM,) * (2 * n),
        input_output_aliases={i: i for i in range(2 * n)},
        compiler_params=pltpu.CompilerParams(has_side_effects=_EFFECT),
    )(*bufs, send_sems, recv_sems, after)
    me, out = _my_index(), []
    for src, land, bc in zip(res[:n], res[n:], by_cols):
        if mode == 'scatter':
            c = land.shape[2]
            own = lax.dynamic_slice(src, (0, me * c), (src.shape[0], c)) if bc else lax.dynamic_index_in_dim(src, me, 0, False)
            out.append(lax.dynamic_update_slice(land, own[None], (me, 0, 0)))
        elif bc:
            out.append(lax.dynamic_update_slice(land, src, (0, me * src.shape[1])))
        else:
            out.append(lax.dynamic_update_slice(land, src[None], (me, 0, 0)))
    return out


def _adamw_math(g, w, m, v):
    c1 = 1.0 / (1.0 - ADAM_B1 ** ADAM_STEP)
    c2 = 1.0 / (1.0 - ADAM_B2 ** ADAM_STEP)
    m = ADAM_B1 * m + (1.0 - ADAM_B1) * g
    v = ADAM_B2 * v + (1.0 - ADAM_B2) * jnp.square(g)
    return g, -ADAM_LR * ((m * c1) / (jnp.sqrt(v * c2) + ADAM_EPS) + ADAM_WD * w), m, v


def adamw(name, recv, w, m, v, tm):
    rows, cols = w.shape
    tm = min(tm, rows)

    def body(*refs):
        g = refs[0][0:tm, 0:cols].astype(F32)
        for s in range(1, N_DEV):
            g = g + refs[s][0:tm, 0:cols].astype(F32)
        res = _adamw_math(g, *[r[...] for r in refs[N_DEV:N_DEV + 3]])
        for r, val in zip(refs[N_DEV + 3:], res):
            r[...] = val

    part = lambda s: pl.BlockSpec((None, recv.shape[1] if tm == rows else tm, recv.shape[2]), lambda i: (s, i, 0))
    tile = pl.BlockSpec((tm, cols), lambda i: (i, 0))
    return pl.pallas_call(
        body, name=name, grid=(rows // tm,), in_specs=[part(s) for s in range(N_DEV)] + [tile] * 3, out_specs=[tile] * 4,
        out_shape=[jax.ShapeDtypeStruct((rows, cols), F32)] * 4, compiler_params=_cparams(("parallel",)),
    )(*[recv] * N_DEV, w, m, v)


def adamw_small(recv, wl, ml, vl):
    n = len(REPLICATED)

    def body(recv_ref, *refs):
        g = recv_ref[0]
        for s in range(1, N_DEV):
            g = g + recv_ref[s]
        for r in range(n):
            w, m, v = (refs[j * n + r][...] for j in range(3))
            for j, val in enumerate(_adamw_math(g[r:r + 1, :w.shape[1]], w, m, v)):
                refs[(3 + j) * n + r][...] = val

    arrs = [d[k].reshape(1, -1) for d in (wl, ml, vl) for k in REPLICATED]
    res = pl.pallas_call(
        body, name="adamw_small", out_shape=[jax.ShapeDtypeStruct(a.shape, F32) for a in arrs[:n]] * 4,
    )(recv, *arrs)
    return [{k: res[j * n + r].reshape(wl[k].shape) for r, k in enumerate(REPLICATED)} for j in range(4)]


ADAMW_TM = {'ffn1_gate': 256, 'ffn1_up': 256, 'ffn2_gate': 256, 'ffn2_up': 256, 'w_in': 32}


def kernel(x, positions, ln1_g, ln1_b, ffn1_gate, ffn1_up, ffn1_down, w_in, conv_w, conv_b, dt_bias, a_log, d_skip, attn_norm_w, ssd_norm_w, w_out, ln2_g, ln2_b, ffn2_gate, ffn2_up, ffn2_down, ln3_g, ln3_b, loss_target, m_ln1_g, m_ln1_b, m_ffn1_gate, m_ffn1_up, m_ffn1_down, m_w_in, m_conv_w, m_conv_b, m_dt_bias, m_a_log, m_d_skip, m_attn_norm_w, m_ssd_norm_w, m_w_out, m_ln2_g, m_ln2_b, m_ffn2_gate, m_ffn2_up, m_ffn2_down, m_ln3_g, m_ln3_b, v_ln1_g, v_ln1_b, v_ffn1_gate, v_ffn1_up, v_ffn1_down, v_w_in, v_conv_w, v_conv_b, v_dt_bias, v_a_log, v_d_skip, v_attn_norm_w, v_ssd_norm_w, v_w_out, v_ln2_g, v_ln2_b, v_ffn2_gate, v_ffn2_up, v_ffn2_down, v_ln3_g, v_ln3_b):
    args = dict(locals())
    wl = {k: args[k] for k in WEIGHTS}
    ml = {k: args["m_" + k] for k in WEIGHTS}
    vl = {k: args["v_" + k] for k in WEIGHTS}
    shapes = {k: wl[k].shape for k in WEIGHTS}
    b, s, dm = x.shape
    t = b * s

    sent = {k: to_comm(k, wl, shapes).astype(F32 if k == 'conv_w' else BF16) for k in SHARDED}
    by_cols = lambda keys: [k in FFN_COL for k in keys]
    gate, up = all_gather([sent['ffn1_gate'], sent['ffn1_up']], [False] * 2)
    (gate, up), sent = lax.optimization_barrier(((gate, up), sent))
    p = dict(zip(('ffn1_gate', 'ffn1_up'), blocks_to_cols([gate, up])))
    gather_down, token_d = exchange_start("gather_ffn1_down_start", 'gather', [sent['ffn1_down']], [False])
    sent['w_in'] = sent['w_in'] + token_d[0, 0].astype(BF16)
    gather_mixer, token_m = exchange_start("gather_mixer_start", 'gather', [sent[k] for k in MIXER_KEYS], by_cols(MIXER_KEYS))
    sent['ffn2_gate'] = sent['ffn2_gate'] + token_m[0, 0].astype(BF16)
    gather_ffn2, token_f = exchange_start("gather_ffn2_start", 'gather', [sent[k] for k in FFN2_KEYS], by_cols(FFN2_KEYS))
    for k in REPLICATED:
        p[k] = wl[k].reshape(1, -1)

    x2 = x.reshape(t, dm)
    cosv, sinv = rope_tables(positions)
    g1, u1, a1, at1 = ffn_gate_up("ffn1_gate_up", x2, p['ffn1_gate'], p['ffn1_up'], after=(token_d, token_m, token_f))
    p['ffn1_down'] = full_weight('ffn1_down', exchange_wait("gather_ffn1_down_wait", gather_down, a1)[0])
    f1, res1 = mm("ffn1_down", [(a1, p['ffn1_down'], 'nn')], D_MODEL), (x2, g1, u1, at1)
    h1, h1b = resid_ln_fwd("ln1", 0.5, x2, f1, p['ln1_g'], p['ln1_b'])
    for k, g in zip(MIXER_KEYS, exchange_wait("gather_mixer_wait", gather_mixer, h1b)):
        p[k] = full_weight(k, g)
    mix, resm = mixer_fwd(h1b, p, cosv, sinv, b)
    h2, h2b = resid_ln_fwd("ln2", 1.0, h1, mix, p['ln2_g'], p['ln2_b'])
    for k, g in zip(FFN2_KEYS, exchange_wait("gather_ffn2_wait", gather_ffn2, h2b)):
        p[k] = full_weight(k, g)
    f2, res3 = ffn_fwd("ffn2", h2b, p['ffn2_gate'], p['ffn2_up'], p['ffn2_down'])

    small, full = {}, {}
    dh2_res, df2, small['ln3_g'], small['ln3_b'], sq = ln_loss_bwd("ln3_loss_bwd", h2, f2, loss_target.reshape(t, dm),
                                                                   p['ln3_g'], p['ln3_b'])
    loss = lax.psum(jnp.sum(sq) * (0.5 / dm), AXES)

    dh2, full['ffn2_gate'], full['ffn2_up'], full['ffn2_down'] = ffn_bwd("ffn2", res3, p['ffn2_gate'], p['ffn2_up'],
                                                                       p['ffn2_down'], df2, dh2_res)
    ffn2_exchange, token = exchange_start("grads_ffn2_start", 'scatter', [grad_shards(k, full[k]) for k in FFN2_KEYS],
                                          by_cols(FFN2_KEYS))
    dh1_res, dmix, small['ln2_g'], small['ln2_b'] = resid_ln_bwd("ln2_bwd", 1.0, h1, mix, p['ln2_g'] + token[:1, :1],
                                                                 p['ln2_b'], dh2)
    dh1, gm = mixer_bwd(resm, p, dmix, dh1_res, b)
    for k in ('conv_b', 'dt_bias', 'a_log', 'd_skip', 'attn_norm_w', 'ssd_norm_w'):
        small[k] = gm[k]
    mixer_exchange, token = exchange_start("grads_mixer_start", 'scatter', [grad_shards(k, gm[k]) for k in MIXER_KEYS],
                                           by_cols(MIXER_KEYS))
    dx_res, df1, small['ln1_g'], small['ln1_b'] = resid_ln_bwd("ln1_bwd", 0.5, x2, f1, p['ln1_g'] + token[:1, :1],
                                                               p['ln1_b'], dh1)
    hb, g, u, at = res1
    small_part = pack_small(small)
    dg, du = ffn_da_act("ffn1_bwd_da_act", df1, p['ffn1_down'], g, u)
    dwd = mm_acc("ffn1_bwd_dwd", at, df1, BF16, after=dg)
    down_exchange, token = exchange_start("grads_ffn1_down_start", 'scatter', [
        grad_shards('ffn1_down', dwd), jnp.broadcast_to(small_part[None], (N_DEV,) + small_part.shape)], [False, False])
    dx = mm("ffn1_bwd_dh", [(dg, p['ffn1_gate'], 'nt'), (du, p['ffn1_up'], 'nt')], D_MODEL, add=dx_res, tn=512, after=token)
    dwg = mm_tn("ffn1_bwd_dwg", hb, dg, BF16, after=dx)
    gate_exchange, token = exchange_start("grads_ffn1_gate_start", 'scatter', [grad_shards('ffn1_gate', dwg)], [True])
    dwu = mm_tn("ffn1_bwd_dwu", hb, du, BF16, after=token)
    recv = dict(zip(('ffn1_up',), all_to_all([grad_shards('ffn1_up', dwu)], [True])))
    for keys, name, ex in (((FFN2_KEYS), "grads_ffn2_wait", ffn2_exchange), (MIXER_KEYS, "grads_mixer_wait", mixer_exchange),
                           (('ffn1_down', SMALL), "grads_ffn1_down_wait", down_exchange),
                           (('ffn1_gate',), "grads_ffn1_gate_wait", gate_exchange)):
        recv.update(zip(keys, exchange_wait(name, ex, recv['ffn1_up'])))
    outs = adamw_small(recv.pop(SMALL), wl, ml, vl)
    for k, r in recv.items():
        shard = shapes[k][1:]
        res = adamw(f"adamw_{k}", r, *[d[k].reshape(shard) for d in (wl, ml, vl)], ADAMW_TM.get(k, shard[0]))
        for o, a in zip(outs, res):
            o[k] = a.reshape(shapes[k])
    return (loss, dx.reshape(b, s, dm), *[o[k] for o in outs for k in WEIGHTS])
```

```python
import functools
import math

import jax
import jax.numpy as jnp
import numpy as np
from jax import lax
from jax.experimental import pallas as pl
from jax.experimental.pallas import tpu as pltpu

F32, BF16 = jnp.float32, jnp.bfloat16
HI = lax.Precision.HIGHEST
MESH = pl.DeviceIdType.MESH
AXES = ("x", "y", "c")
N_DEV = 8

D_MODEL = 1024
SEQ = 2048
HEAD_DIM = 64
N_HEADS = 12
D_ATTN = N_HEADS * HEAD_DIM
DILATIONS = (1, 4, 16)
ATTN_BLOCK = 128
ROPE_THETA = 500000.0
ROPE_DIM = 16
D_SSD = 768
SSD_GROUPS = 4
SSD_STATE = 128
SSD_CHUNK = 128
D_BC = SSD_GROUPS * SSD_STATE
D_CONV = D_SSD + 2 * D_BC
CONV_WIDTH = 4
D_QKVZ = 3 * D_ATTN + D_SSD
D_IN_PROJ = D_QKVZ + D_CONV + N_HEADS
D_FF = 2816
ALPHA = 2.0 ** 0.25
LN_EPS = 1e-5
RMS_EPS = 1e-6
ADAM_LR, ADAM_B1, ADAM_B2, ADAM_EPS, ADAM_WD, ADAM_STEP = 0.001, 0.9, 0.999, 1e-08, 0.01, 10

LANES = 128
VMEM_LIMIT = 52 * 1024 * 1024
NEG = -1e30

WEIGHTS = ['ln1_g', 'ln1_b', 'ffn1_gate', 'ffn1_up', 'ffn1_down', 'w_in', 'conv_w', 'conv_b', 'dt_bias', 'a_log',
           'd_skip', 'attn_norm_w', 'ssd_norm_w', 'w_out', 'ln2_g', 'ln2_b', 'ffn2_gate', 'ffn2_up', 'ffn2_down',
           'ln3_g', 'ln3_b']
COL_SHARDED = ('ffn1_gate', 'ffn1_up', 'conv_w', 'ffn2_gate', 'ffn2_up')
ROW_SHARDED = ('ffn1_down', 'w_in', 'w_out', 'ffn2_down')
SHARDED = tuple(n for n in WEIGHTS if n in COL_SHARDED or n in ROW_SHARDED)
REPLICATED = tuple(n for n in WEIGHTS if n not in SHARDED)
FF_SHARD = D_FF // N_DEV
FF_PAD = -(-FF_SHARD // LANES) * LANES
D_FF_INT = N_DEV * FF_PAD


def _cparams(sem=None):
    return pltpu.CompilerParams(dimension_semantics=sem, vmem_limit_bytes=VMEM_LIMIT)


def _tile(n, prefs):
    for p in prefs:
        if n % p == 0:
            return p
    return n


class Op:
    def __init__(self, arr, bw=None, cb=0, ro=0):
        self.arr, self.bw, self.cb, self.ro = arr, (arr.shape[1] if bw is None else bw), cb, ro


def _op(a):
    return a if isinstance(a, Op) else Op(a)


def rowwise(name, fn, ins, consts, outs, accs=(), tm=256):
    ins = [_op(a) for a in ins]
    rows = outs[0][0]
    n_in, n_c, n_o, n_a = len(ins), len(consts), len(outs), len(accs)
    tm = min(tm, rows)
    assert rows % tm == 0, (name, rows, tm)

    def body(*refs):
        vals = [r[...].astype(F32) for r in refs[:n_in + n_c]]
        res = fn(*vals)
        res = res if isinstance(res, (tuple, list)) else (res,)
        o_refs = refs[n_in + n_c:n_in + n_c + n_o]
        a_refs = refs[n_in + n_c + n_o:]
        for r, v in zip(o_refs, res[:n_o]):
            r[...] = v.astype(r.dtype)
        if n_a:
            @pl.when(pl.program_id(0) == 0)
            def _():
                for r in a_refs:
                    r[...] = jnp.zeros(r.shape, r.dtype)
            for r, v in zip(a_refs, res[n_o:]):
                r[...] += v

    in_specs = [pl.BlockSpec((tm, o.bw), functools.partial(lambda i, o: (i + o.ro, o.cb), o=o)) for o in ins]
    in_specs += [pl.BlockSpec(c.shape, functools.partial(lambda i, nd: (0,) * nd, nd=c.ndim)) for c in consts]
    out_specs = [pl.BlockSpec((tm, w), lambda i: (i, 0)) for (_, w, _) in outs]
    out_specs += [pl.BlockSpec(s, functools.partial(lambda i, nd: (0,) * nd, nd=len(s))) for s in accs]
    out_shape = [jax.ShapeDtypeStruct((r, w), dt) for (r, w, dt) in outs]
    out_shape += [jax.ShapeDtypeStruct(s, F32) for s in accs]
    res = pl.pallas_call(
        body, name=name, grid=(rows // tm,), in_specs=in_specs, out_specs=out_specs, out_shape=out_shape,
        compiler_params=_cparams(("arbitrary",) if n_a else ("parallel",)),
    )(*[o.arr for o in ins], *consts)
    return res


MM_TM = 512
MM_TN = (1024, 896, 768, 512, 256, 128)
_NT = (((1,), (1,)), ((), ()))
_NN = (((1,), (0,)), ((), ()))
_TN = (((0,), (0,)), ((), ()))


def _dot(a, b, dn, precision=None):
    return lax.dot_general(a, b, dn, preferred_element_type=F32, precision=precision)


def _mm_specs(name, pairs, n_out, tm, tn):
    in_specs, args = [], []
    for a, b, mode in pairs:
        o = _op(a)
        in_specs.append(pl.BlockSpec((tm, o.bw), functools.partial(lambda j, i, o: (i, o.cb), o=o)))
        args.append(o.arr)
        if mode == 'nn':
            assert b.shape == (o.bw, n_out), (name, b.shape, o.bw, n_out)
            in_specs.append(pl.BlockSpec((o.bw, tn), lambda j, i: (0, j)))
        else:
            assert b.shape == (n_out, o.bw), (name, b.shape, o.bw, n_out)
            in_specs.append(pl.BlockSpec((tn, o.bw), lambda j, i: (j, 0)))
        args.append(b)
    return in_specs, args


def _mm_acc(refs, pairs):
    acc = None
    for k, (_, _, mode) in enumerate(pairs):
        d = _dot(refs[2 * k][...].astype(BF16), refs[2 * k + 1][...].astype(BF16), _NN if mode == 'nn' else _NT)
        acc = d if acc is None else acc + d
    return acc


def mm(name, pairs, n_out, add=None, out_dtype=F32, tm=MM_TM, tn=None, after=None):
    m = _op(pairs[0][0]).arr.shape[0]
    tn = tn or _tile(n_out, MM_TN)
    n_p = len(pairs)

    def body(*refs):
        acc = _mm_acc(refs, pairs)
        if add is not None:
            acc = acc + refs[2 * n_p][...]
        refs[-1][...] = acc.astype(refs[-1].dtype)

    in_specs, args = _mm_specs(name, pairs, n_out, tm, tn)
    tile = pl.BlockSpec((tm, tn), lambda j, i: (i, j))
    if add is not None:
        in_specs.append(tile)
        args.append(add)
    if after is not None:
        in_specs.append(pl.BlockSpec(memory_space=pl.ANY))
        args.append(after)
    return pl.pallas_call(
        body, name=name, grid=(n_out // tn, m // tm), in_specs=in_specs, out_specs=tile,
        out_shape=jax.ShapeDtypeStruct((m, n_out), out_dtype),
        compiler_params=_cparams(("parallel", "parallel")),
    )(*args)


def mm_tn(name, a, b, out_dtype=F32, tt=1024, after=None):
    a, b = _op(a), _op(b)
    t = a.arr.shape[0]
    k, n = a.bw, b.bw
    tk = _tile(k, (512, 896, 768, 256, 128))
    tn = _tile(n, (3072, 1792) + MM_TN)
    tt = min(tt, t)
    n_t = t // tt
    order = [] if after is None else [after]

    def body(a_ref, b_ref, *rest):
        o_ref, acc_ref = rest[-2:]
        s = pl.program_id(2)
        d = _dot(a_ref[...].astype(BF16), b_ref[...].astype(BF16), _TN)

        @pl.when(s == 0)
        def _():
            acc_ref[...] = d

        @pl.when(s > 0)
        def _():
            acc_ref[...] += d

        @pl.when(s == n_t - 1)
        def _():
            o_ref[...] = acc_ref[...].astype(o_ref.dtype)

    return pl.pallas_call(
        body, name=name, grid=(k // tk, n // tn, n_t),
        in_specs=[pl.BlockSpec((tt, tk), functools.partial(lambda kk, nn, s, o: (s, o.cb * (o.bw // tk) + kk), o=a)),
                  pl.BlockSpec((tt, tn), functools.partial(lambda kk, nn, s, o: (s, o.cb * (o.bw // tn) + nn), o=b))]
        + [pl.BlockSpec(memory_space=pl.ANY) for _ in order],
        out_specs=pl.BlockSpec((tk, tn), lambda kk, nn, s: (kk, nn)),
        out_shape=jax.ShapeDtypeStruct((k, n), out_dtype),
        scratch_shapes=[pltpu.VMEM((tk, tn), F32)],
        compiler_params=_cparams(("parallel", "parallel", "arbitrary")),
    )(a.arr, b.arr, *order)


def _sigmoid(x):
    return 1.0 / (1.0 + jnp.exp(-x))


def _silu(x):
    return x * _sigmoid(x)


def _softplus(x):
    return jnp.maximum(x, 0.0) + jnp.log(1.0 + jnp.exp(-jnp.abs(x)))


def _act(g, u):
    return _silu(g) * u


def _resid_ln(scale, h, branch, g, b):
    r = ALPHA * h + scale * branch
    mu = jnp.mean(r, axis=-1, keepdims=True)
    var = jnp.mean(jnp.square(r - mu), axis=-1, keepdims=True)
    return (r - mu) * lax.rsqrt(var + LN_EPS) * g + b


def _rms(t, w):
    return t * lax.rsqrt(jnp.mean(t * t, axis=-1, keepdims=True) + RMS_EPS) * w


def _branch_weights(l1, l2, l3):
    m = jnp.maximum(jnp.maximum(l1, l2), l3)
    e1, e2, e3 = jnp.exp(l1 - m), jnp.exp(l2 - m), jnp.exp(l3 - m)
    inv = 1.0 / (e1 + e2 + e3)
    return e1 * inv, e2 * inv, e3 * inv


def _gate(y, xs, z, dskip, w):
    return _rms((y + dskip * xs) * _silu(z), w)


def _rot(x):
    d = lax.broadcasted_iota(jnp.int32, x.shape, 1) % HEAD_DIM
    up = pltpu.roll(x, x.shape[1] - ROPE_DIM // 2, 1)
    down = jnp.where(d < ROPE_DIM, pltpu.roll(x, ROPE_DIM // 2, 1), 0.0)
    return jnp.where(d < ROPE_DIM // 2, up, down)


def ffn_gate_up(name, h, wg, wu, after=()):
    m, nf = h.shape[0], wg.shape[1]
    tn = _tile(nf, MM_TN)

    def body(h_ref, g_w, u_w, *rest):
        g_ref, u_ref, a_ref, at_ref = rest[-4:]
        hb = h_ref[...].astype(BF16)
        g = _dot(hb, g_w[...].astype(BF16), _NN)
        u = _dot(hb, u_w[...].astype(BF16), _NN)
        g_ref[...] = g.astype(g_ref.dtype)
        u_ref[...] = u.astype(u_ref.dtype)
        a = _act(g, u)
        a_ref[...] = a.astype(a_ref.dtype)
        at_ref[...] = a.T.astype(at_ref.dtype)

    in_specs, args = _mm_specs(name, [(h, wg, 'nn')], nf, MM_TM, tn)
    in_specs.append(in_specs[1])
    in_specs += [pl.BlockSpec(memory_space=pl.ANY) for _ in after]
    tile = pl.BlockSpec((MM_TM, tn), lambda j, i: (i, j))
    return pl.pallas_call(
        body, name=name, grid=(nf // tn, m // MM_TM), in_specs=in_specs,
        out_specs=[tile] * 3 + [pl.BlockSpec((tn, MM_TM), lambda j, i: (j, i))],
        out_shape=[jax.ShapeDtypeStruct((m, nf), BF16)] * 3 + [jax.ShapeDtypeStruct((nf, m), BF16)],
        compiler_params=_cparams(("parallel", "parallel")),
    )(*args, wu, *after)


def mm_acc(name, a, b, out_dtype=F32, tt=1024, after=None):
    k, t = a.shape
    n = b.shape[1]
    tk, tn, tt = _tile(k, (1024, 512, 256, 128)), _tile(n, MM_TN), min(tt, t)
    n_t = t // tt
    order = [] if after is None else [after]

    def body(a_ref, b_ref, *rest):
        o_ref, acc_ref = rest[-2:]
        s = pl.program_id(2)
        d = _dot(a_ref[...].astype(BF16), b_ref[...].astype(BF16), _NN)

        @pl.when(s == 0)
        def _():
            acc_ref[...] = d

        @pl.when(s > 0)
        def _():
            acc_ref[...] += d

        @pl.when(s == n_t - 1)
        def _():
            o_ref[...] = acc_ref[...].astype(o_ref.dtype)

    return pl.pallas_call(
        body, name=name, grid=(k // tk, n // tn, n_t),
        in_specs=[pl.BlockSpec((tk, tt), lambda kk, nn, s: (kk, s)), pl.BlockSpec((tt, tn), lambda kk, nn, s: (s, nn))]
        + [pl.BlockSpec(memory_space=pl.ANY) for _ in order],
        out_specs=pl.BlockSpec((tk, tn), lambda kk, nn, s: (kk, nn)),
        out_shape=jax.ShapeDtypeStruct((k, n), out_dtype), scratch_shapes=[pltpu.VMEM((tk, tn), F32)],
        compiler_params=_cparams(("parallel", "parallel", "arbitrary")),
    )(a, b, *order)


def ffn_da_act(name, df, wd, g, u):
    m, nf = df.shape[0], wd.shape[0]
    tn = _tile(nf, MM_TN)

    def body(df_ref, w_ref, g_ref, u_ref, dg_ref, du_ref):
        da = _dot(df_ref[...].astype(BF16), w_ref[...].astype(BF16), _NT)
        _, vjp = jax.vjp(_act, g_ref[...].astype(F32), u_ref[...].astype(F32))
        dg, du = vjp(da)
        dg_ref[...] = dg.astype(dg_ref.dtype)
        du_ref[...] = du.astype(du_ref.dtype)

    in_specs, args = _mm_specs(name, [(df, wd, 'nt')], nf, MM_TM, tn)
    tile = pl.BlockSpec((MM_TM, tn), lambda j, i: (i, j))
    return pl.pallas_call(
        body, name=name, grid=(nf // tn, m // MM_TM), in_specs=in_specs + [tile, tile], out_specs=[tile] * 2,
        out_shape=[jax.ShapeDtypeStruct((m, nf), BF16)] * 2, compiler_params=_cparams(("parallel", "parallel")),
    )(*args, g, u)


def resid_ln_fwd(name, scale, h, branch, ln_g, ln_b):
    t = h.shape[0]

    def fn(*a):
        y = _resid_ln(scale, *a)
        return y, y

    return rowwise(name, fn, [h, branch], [ln_g, ln_b], [(t, D_MODEL, F32), (t, D_MODEL, BF16)], tm=512)


def ffn_fwd(tag, hb, wg, wu, wd, after=()):
    g, u, a, at = ffn_gate_up(f"{tag}_gate_up", hb, wg, wu, after)
    f = mm(f"{tag}_down", [(a, wd, 'nn')], D_MODEL)
    return f, (hb, g, u, at)


def ln_loss_bwd(name, h, branch, target, ln_g, ln_b):
    t, dm = h.shape

    def fn(h_, br_, tgt, g_, b_):
        y, vjp = jax.vjp(functools.partial(_resid_ln, 0.5), h_, br_, g_, b_)
        e = y - tgt
        return (*vjp(e * (1.0 / dm)), jnp.sum(e * e, axis=0, keepdims=True))

    return rowwise(name, fn, [h, branch, target], [ln_g, ln_b], [(t, dm, F32), (t, dm, BF16)],
                   accs=[(1, dm), (1, dm), (1, dm)], tm=512)


def resid_ln_bwd(name, scale, h, branch, ln_g, ln_b, dout, extra=None):
    t = h.shape[0]

    def fn(h_, br_, do_, *rest):
        g_, b_ = rest[-2], rest[-1]
        _, vjp = jax.vjp(functools.partial(_resid_ln, scale), h_, br_, g_, b_)
        dh, dbr, dg, db = vjp(do_)
        if extra is not None:
            dh = dh + rest[0]
        return dh, dbr, dg, db

    ins = [h, branch, dout] + ([extra] if extra is not None else [])
    return rowwise(name, fn, ins, [ln_g, ln_b], [(t, D_MODEL, F32), (t, D_MODEL, BF16)],
                   accs=[(1, D_MODEL), (1, D_MODEL)], tm=512)


def ffn_bwd(tag, res, wg, wu, wd, df, dh_resid):
    hb, g, u, at = res
    dg, du = ffn_da_act(f"{tag}_bwd_da_act", df, wd, g, u)
    dwd = mm_acc(f"{tag}_bwd_dwd", at, df, BF16)
    dh = mm(f"{tag}_bwd_dh", [(dg, wg, 'nt'), (du, wu, 'nt')], D_MODEL, add=dh_resid, tn=512)
    dwg = mm_tn(f"{tag}_bwd_dwg", hb, dg, BF16)
    dwu = mm_tn(f"{tag}_bwd_dwu", hb, du, BF16)
    return dh, dwg, dwu, dwd


def rope_tables(positions):
    inv_freq = ROPE_THETA ** (-jnp.arange(0, ROPE_DIM, 2, dtype=F32) / ROPE_DIM)
    ang = positions.reshape(-1, 1).astype(F32) * inv_freq
    c, s = jnp.cos(ang), jnp.sin(ang)
    t = ang.shape[0]
    cosv = jnp.concatenate([c, c, jnp.ones((t, HEAD_DIM - ROPE_DIM), F32)], axis=1)
    sinv = jnp.concatenate([-s, s, jnp.zeros((t, HEAD_DIM - ROPE_DIM), F32)], axis=1)
    return jnp.tile(cosv, (1, 2)), jnp.tile(sinv, (1, 2))


def _pair_masks():
    lane = lax.broadcasted_iota(jnp.int32, (1, LANES), 1)
    return (lane < HEAD_DIM, lane >= HEAD_DIM)


def _band_masks():
    row = lax.broadcasted_iota(jnp.int32, (ATTN_BLOCK, ATTN_BLOCK), 0)
    col = lax.broadcasted_iota(jnp.int32, (ATTN_BLOCK, ATTN_BLOCK), 1)
    return col >= row, col <= row


def _residue_blocks():
    out = []
    for g, d in enumerate(DILATIONS):
        for r in range(d):
            for i in range(SEQ // d // ATTN_BLOCK):
                rows = lambda j: pl.ds(r + j * ATTN_BLOCK * d, ATTN_BLOCK, stride=d) if d > 1 else pl.ds(j * ATTN_BLOCK, ATTN_BLOCK)
                out.append((g, rows(i), rows(i - 1) if i > 0 else None))
    return out


N_HEAD_PAIRS = D_ATTN // LANES
SCALE = HEAD_DIM ** -0.5
ATTN_GROUP = 4


def _block_operands(qr, kr, v_ref, cur, prev):
    prev_ok, cur_ok = _band_masks()
    if prev is None:
        return qr[cur, :], kr[cur, :].astype(BF16), v_ref[cur, :], cur_ok
    kcat = jnp.concatenate([kr[prev, :], kr[cur, :]], axis=0).astype(BF16)
    vcat = jnp.concatenate([v_ref[prev, :], v_ref[cur, :]], axis=0)
    return qr[cur, :], kcat, vcat, jnp.concatenate([prev_ok, cur_ok], axis=1)


def _attn_specs(b):
    col = lambda cb: pl.BlockSpec((SEQ, LANES), lambda bb, hp: (bb, cb + hp))
    tab = pl.BlockSpec((SEQ, LANES), lambda bb, hp: (bb, 0))
    return col, tab


def attn_fwd(qkvz, cosv, sinv, b):
    t = qkvz.shape[0]
    col, tab = _attn_specs(b)
    blocks = _residue_blocks()

    def body(q_ref, k_ref, v_ref, c_ref, s_ref, o_ref, l1_ref, l2_ref, l3_ref, qr, kr, o1, o2, o3):
        l_refs, o_scr = (l1_ref, l2_ref, l3_ref), (o1, o2, o3)
        c, s = c_ref[...], s_ref[...]
        q, k = q_ref[...], k_ref[...]
        qr[...] = q * c + _rot(q) * s
        kr[...] = k * c + _rot(k) * s
        masks = _pair_masks()
        for lo in range(0, len(blocks), ATTN_GROUP):
            chains = []
            for g, cur, prev in blocks[lo:lo + ATTN_GROUP]:
                q2, kcat, vcat, ok = _block_operands(qr, kr, v_ref, cur, prev)
                for m in masks:
                    qm = jnp.where(m, q2, 0.0).astype(BF16)
                    chains.append(dict(g=g, cur=cur, m=m, v=jnp.where(m, vcat, 0.0).astype(BF16),
                                       s=jnp.where(ok, _dot(qm, kcat, _NT) * SCALE, NEG)))
            for ch in chains:
                mx = jnp.max(ch['s'], axis=1, keepdims=True)
                p = jnp.exp(ch['s'] - mx)
                den = jnp.sum(p, axis=1, keepdims=True)
                ch.update(p=p.astype(BF16), inv=1.0 / den, lse=mx + jnp.log(den))
            for ch in chains:
                ch['o'] = _dot(ch['p'], ch['v'], _NN) * ch['inv']
            for c0, c1 in zip(chains[0::2], chains[1::2]):
                o_scr[c0['g']][c0['cur'], :] = c0['o'] + c1['o']
                l_refs[c0['g']][c0['cur'], :] = jnp.where(c0['m'], c0['lse'], c1['lse'])
        w1, w2, w3 = _branch_weights(l1_ref[...], l2_ref[...], l3_ref[...])
        o_ref[...] = w1 * o1[...] + w2 * o2[...] + w3 * o3[...]

    shp = jax.ShapeDtypeStruct((t, D_ATTN), F32)
    return pl.pallas_call(
        body, name="attn_fwd", grid=(b, N_HEAD_PAIRS),
        in_specs=[col(0), col(N_HEAD_PAIRS), col(2 * N_HEAD_PAIRS), tab, tab],
        out_specs=[col(0)] * 4, out_shape=[shp] * 4,
        scratch_shapes=[pltpu.VMEM((SEQ, LANES), F32)] * 5,
        compiler_params=_cparams(("parallel", "parallel")),
    )(qkvz, qkvz, qkvz, cosv, sinv)


def attn_bwd(qkvz, cosv, sinv, dmix, mixed, lses, b):
    t = qkvz.shape[0]
    col, tab = _attn_specs(b)
    blocks = _residue_blocks()
    hd = np.arange(LANES) // HEAD_DIM
    head_ones = jnp.asarray((hd[:, None] == hd[None, :]).astype(np.float32))

    def body(q_ref, k_ref, v_ref, c_ref, s_ref, dm_ref, mx_ref, l1_ref, l2_ref, l3_ref, ones_ref,
             dq_out, dk_out, dv_out, qr, kr, do1, do2, do3, dd1, dd2, dd3, dq_ref, dk_ref, dv_ref):
        l_refs, do_scr, dd_scr = (l1_ref, l2_ref, l3_ref), (do1, do2, do3), (dd1, dd2, dd3)
        c, s = c_ref[...], s_ref[...]
        q, k = q_ref[...], k_ref[...]
        qr[...] = q * c + _rot(q) * s
        kr[...] = k * c + _rot(k) * s
        dm = dm_ref[...]
        tot = _dot(dm * mx_ref[...], ones_ref[...], _NN, HI)
        for w, do_g, dd_g in zip(_branch_weights(l1_ref[...], l2_ref[...], l3_ref[...]), do_scr, dd_scr):
            do_g[...] = w * dm
            dd_g[...] = w * tot
        dq_ref[...] = jnp.zeros((SEQ, LANES), F32)
        dk_ref[...] = jnp.zeros((SEQ, LANES), F32)
        dv_ref[...] = jnp.zeros((SEQ, LANES), F32)
        masks = _pair_masks()
        for lo in range(0, len(blocks), ATTN_GROUP):
            chains = []
            for g, cur, prev in blocks[lo:lo + ATTN_GROUP]:
                q2, kcat, vcat, ok = _block_operands(qr, kr, v_ref, cur, prev)
                vcat = vcat.astype(BF16)
                do2_, l2, dd2_ = do_scr[g][cur, :], l_refs[g][cur, :], dd_scr[g][cur, :]
                l2s, dd2s = pltpu.roll(l2, HEAD_DIM, 1), pltpu.roll(dd2_, HEAD_DIM, 1)
                for m in masks:
                    qm = jnp.where(m, q2, 0.0).astype(BF16)
                    dom = jnp.where(m, do2_, 0.0).astype(BF16)
                    lrep, ddrep = jnp.where(m, l2, l2s), jnp.where(m, dd2_, dd2s)
                    if prev is not None:
                        lrep, ddrep = jnp.concatenate([lrep, lrep], axis=1), jnp.concatenate([ddrep, ddrep], axis=1)
                    chains.append(dict(cur=cur, prev=prev, qm=qm, dom=dom, km=jnp.where(m, kcat, 0), lrep=lrep, ddrep=ddrep,
                                       s=jnp.where(ok, _dot(qm, kcat, _NT) * SCALE, NEG), dp=_dot(dom, vcat, _NT)))
            for ch in chains:
                p = jnp.exp(ch['s'] - ch['lrep'])
                ch.update(p=p.astype(BF16), ds=(p * (ch['dp'] - ch['ddrep']) * SCALE).astype(BF16))
            for ch in chains:
                ch.update(dq=_dot(ch['ds'], ch['km'], _NN), dk=_dot(ch['ds'], ch['qm'], _TN), dv=_dot(ch['p'], ch['dom'], _TN))
            for c0, c1 in zip(chains[0::2], chains[1::2]):
                cur, prev = c0['cur'], c0['prev']
                dk, dv = c0['dk'] + c1['dk'], c0['dv'] + c1['dv']
                dq_ref[cur, :] += c0['dq'] + c1['dq']
                if prev is None:
                    dk_ref[cur, :] += dk
                    dv_ref[cur, :] += dv
                else:
                    dk_ref[prev, :] += dk[:ATTN_BLOCK]
                    dv_ref[prev, :] += dv[:ATTN_BLOCK]
                    dk_ref[cur, :] += dk[ATTN_BLOCK:]
                    dv_ref[cur, :] += dv[ATTN_BLOCK:]
        dq, dk = dq_ref[...], dk_ref[...]
        dq_out[...] = (dq * c + _rot(dq * s)).astype(dq_out.dtype)
        dk_out[...] = (dk * c + _rot(dk * s)).astype(dk_out.dtype)
        dv_out[...] = dv_ref[...].astype(dv_out.dtype)

    shp = jax.ShapeDtypeStruct((t, D_ATTN), BF16)
    return pl.pallas_call(
        body, name="attn_bwd", grid=(b, N_HEAD_PAIRS),
        in_specs=[col(0), col(N_HEAD_PAIRS), col(2 * N_HEAD_PAIRS), tab, tab, col(0), col(0), col(0), col(0), col(0),
                  pl.BlockSpec((LANES, LANES), lambda bb, hp: (0, 0))],
        out_specs=[col(0)] * 3, out_shape=[shp] * 3,
        scratch_shapes=[pltpu.VMEM((SEQ, LANES), F32)] * 11,
        compiler_params=_cparams(("parallel", "parallel")),
    )(qkvz, qkvz, qkvz, cosv, sinv, dmix, mixed, *lses, head_ones)


def attn_norm_fwd(mixed, norm_w):
    return rowwise("attn_norm", _rms, [mixed], [norm_w], [(mixed.shape[0], D_ATTN, BF16)])[0]


def attn_norm_bwd(dout, mixed, norm_w):
    def fn(dy, mx, w):
        _, vjp = jax.vjp(_rms, mx, w)
        return vjp(dy)

    return rowwise("attn_norm_bwd", fn, [dout, mixed], [norm_w], [(dout.shape[0], D_ATTN, F32)], accs=[(1, D_ATTN)])


CONV_TM = 256
HALO = 8


def _conv_columns(refs):
    xs_ref, bm_ref, cm_ref = refs
    out = []
    for c in range(D_CONV // LANES):
        lo = c * LANES
        ref, base = (xs_ref, 0) if lo < D_SSD else (bm_ref, D_SSD) if lo < D_SSD + D_BC else (cm_ref, D_SSD + D_BC)
        out.append((slice(lo, lo + LANES), (ref, slice(lo - base, lo - base + LANES))))
    return out


def _conv_taps(scr, w_ref, cs, first_row, step, tm):
    acc = None
    for k in range(CONV_WIDTH):
        term = w_ref[k:k + 1, cs] * scr[pl.ds(first_row + step * k, tm), cs]
        acc = term if acc is None else acc + term
    return acc


def conv_fwd(u, w, bias):
    t = u.shape[0]
    tm, per_seq = CONV_TM, SEQ // CONV_TM

    def body(u_ref, h_ref, w_ref, b_ref, xs_ref, bm_ref, cm_ref, scr):
        first = pl.program_id(0) % per_seq == 0
        scr[0:HALO, :] = jnp.where(first, 0.0, h_ref[...])
        scr[HALO:, :] = u_ref[...]
        for cs, (o_ref, os_) in _conv_columns((xs_ref, bm_ref, cm_ref)):
            o_ref[:, os_] = _silu(_conv_taps(scr, w_ref, cs, HALO - CONV_WIDTH + 1, 1, tm) + b_ref[:, cs])

    return pl.pallas_call(
        body, name="conv_fwd", grid=(t // tm,),
        in_specs=[pl.BlockSpec((tm, D_CONV), lambda i: (i, 0)),
                  pl.BlockSpec((HALO, D_CONV), lambda i: (jnp.maximum(i * (tm // HALO) - 1, 0), 0)),
                  pl.BlockSpec((CONV_WIDTH, D_CONV), lambda i: (0, 0)), pl.BlockSpec((1, D_CONV), lambda i: (0, 0))],
        out_specs=[pl.BlockSpec((tm, D_SSD), lambda i: (i, 0)), pl.BlockSpec((tm, D_BC), lambda i: (i, 0)),
                   pl.BlockSpec((tm, D_BC), lambda i: (i, 0))],
        out_shape=[jax.ShapeDtypeStruct((t, D_SSD), F32), jax.ShapeDtypeStruct((t, D_BC), F32),
                   jax.ShapeDtypeStruct((t, D_BC), F32)],
        scratch_shapes=[pltpu.VMEM((tm + HALO, D_CONV), F32)],
        compiler_params=_cparams(("parallel",)),
    )(u, u, w, bias)


def conv_bwd(u, w, bias, dxs_a, dxs_b, dbm, dcm):
    t = u.shape[0]
    tm, per_seq = CONV_TM, SEQ // CONV_TM
    n_tiles = t // tm

    def body1(u_ref, h_ref, dxs_ref, dxs2_ref, dbm_ref, dcm_ref, w_ref, b_ref, dz_ref, dw_ref, db_ref, scr):
        i = pl.program_id(0)
        first = i % per_seq == 0
        scr[0:HALO, :] = jnp.where(first, 0.0, h_ref[...])
        scr[HALO:, :] = u_ref[...]

        @pl.when(i == 0)
        def _():
            dw_ref[...] = jnp.zeros(dw_ref.shape, F32)
            db_ref[...] = jnp.zeros(db_ref.shape, F32)
        for cs, (g_ref, gs) in _conv_columns((dxs_ref, dbm_ref, dcm_ref)):
            acc = _conv_taps(scr, w_ref, cs, HALO - CONV_WIDTH + 1, 1, tm) + b_ref[:, cs]
            sig = _sigmoid(acc)
            dy = g_ref[:, gs] + dxs2_ref[:, gs] if g_ref is dxs_ref else g_ref[:, gs]
            dz = dy * sig * (1.0 + acc * (1.0 - sig))
            dz_ref[:, cs] = dz
            db_ref[:, cs] += jnp.sum(dz, axis=0, keepdims=True)
            for k in range(CONV_WIDTH):
                dw_ref[k:k + 1, cs] += jnp.sum(dz * scr[pl.ds(HALO - CONV_WIDTH + 1 + k, tm), cs], axis=0, keepdims=True)

    dz, dw, db = pl.pallas_call(
        body1, name="conv_bwd_dz", grid=(n_tiles,),
        in_specs=[pl.BlockSpec((tm, D_CONV), lambda i: (i, 0)),
                  pl.BlockSpec((HALO, D_CONV), lambda i: (jnp.maximum(i * (tm // HALO) - 1, 0), 0)),
                  pl.BlockSpec((tm, D_SSD), lambda i: (i, 0)), pl.BlockSpec((tm, D_SSD), lambda i: (i, 0)),
                  pl.BlockSpec((tm, D_BC), lambda i: (i, 0)), pl.BlockSpec((tm, D_BC), lambda i: (i, 0)),
                  pl.BlockSpec((CONV_WIDTH, D_CONV), lambda i: (0, 0)), pl.BlockSpec((1, D_CONV), lambda i: (0, 0))],
        out_specs=[pl.BlockSpec((tm, D_CONV), lambda i: (i, 0)), pl.BlockSpec((CONV_WIDTH, D_CONV), lambda i: (0, 0)),
                   pl.BlockSpec((1, D_CONV), lambda i: (0, 0))],
        out_shape=[jax.ShapeDtypeStruct((t, D_CONV), F32), jax.ShapeDtypeStruct((CONV_WIDTH, D_CONV), F32),
                   jax.ShapeDtypeStruct((1, D_CONV), F32)],
        scratch_shapes=[pltpu.VMEM((tm + HALO, D_CONV), F32)],
        compiler_params=_cparams(("arbitrary",)),
    )(u, u, dxs_a, dxs_b, dbm, dcm, w, bias)

    def body2(dz_ref, n_ref, w_ref, du_ref, scr):
        last = pl.program_id(0) % per_seq == per_seq - 1
        scr[0:tm, :] = dz_ref[...]
        scr[tm:, :] = jnp.where(last, 0.0, n_ref[...])
        for c in range(D_CONV // LANES):
            cs = slice(c * LANES, (c + 1) * LANES)
            du_ref[:, cs] = _conv_taps(scr, w_ref, cs, CONV_WIDTH - 1, -1, tm).astype(du_ref.dtype)

    du = pl.pallas_call(
        body2, name="conv_bwd_du", grid=(n_tiles,),
        in_specs=[pl.BlockSpec((tm, D_CONV), lambda i: (i, 0)),
                  pl.BlockSpec((HALO, D_CONV), lambda i: (jnp.minimum((i + 1) * (tm // HALO), t // HALO - 1), 0)),
                  pl.BlockSpec((CONV_WIDTH, D_CONV), lambda i: (0, 0))],
        out_specs=pl.BlockSpec((tm, D_CONV), lambda i: (i, 0)),
        out_shape=jax.ShapeDtypeStruct((t, D_CONV), BF16),
        scratch_shapes=[pltpu.VMEM((tm + HALO, D_CONV), F32)],
        compiler_params=_cparams(("parallel",)),
    )(dz, dz, w)
    return du, dw, db


Q = SSD_CHUNK
N_PAIRS = D_SSD // LANES
HEADS_PER_GROUP = N_HEADS // SSD_GROUPS


def _rep(a, j):
    return jnp.broadcast_to(a[:, j:j + 1], a.shape)


def _dot_exact01(a, b, dn, a_is_01):
    x = b if a_is_01 else a
    hi = x.astype(BF16)
    mid = (x - hi.astype(F32)).astype(BF16)
    lo = (x - hi.astype(F32) - mid.astype(F32)).astype(BF16)
    z = a.astype(BF16) if a_is_01 else b.astype(BF16)
    out = None
    for term in (hi, mid, lo):
        d = _dot(z, term, dn) if a_is_01 else _dot(term, z, dn)
        out = d if out is None else out + d
    return out


def _pad_lanes(v, fill=0.0):
    row = jnp.pad(v.reshape(1, -1).astype(F32), ((0, 0), (0, LANES - v.size)), constant_values=fill)
    return row, row.reshape(LANES, 1)


def _ssd_common(dtr_ref, dtrt_ref, bias_r, bias_c, alog_r, alog_c):
    row = lax.broadcasted_iota(jnp.int32, (Q, Q), 0)
    col = lax.broadcasted_iota(jnp.int32, (Q, Q), 1)
    tril = row >= col
    lane = lax.broadcasted_iota(jnp.int32, (1, LANES), 1)
    a_r = jnp.where(lane < N_HEADS, -jnp.exp(alog_r[...]), 0.0)
    sub = lax.broadcasted_iota(jnp.int32, (LANES, 1), 0)
    a_c = jnp.where(sub < N_HEADS, -jnp.exp(alog_c[...]), 0.0)
    dt = _softplus(dtr_ref[...] + bias_r[...])
    cs = _dot_exact01(tril, dt * a_r, _NN, True)
    dtt = _softplus(dtrt_ref[...] + bias_c[...])
    cst = _dot_exact01(dtt * a_c, row <= col, _NN, False)
    return tril, lane, a_r, dt, cs, cst


def _ssd_specs(b, nc, rev):
    ci = (lambda c: nc - 1 - c) if rev else (lambda c: c)
    rows = lambda w: pl.BlockSpec((Q, w), lambda bb, c: (bb * nc + ci(c), 0))
    dtt = pl.BlockSpec((LANES, Q), lambda bb, c: (0, bb * nc + ci(c)))
    const = lambda s: pl.BlockSpec(s, lambda bb, c: (0,) * len(s))
    state = pl.BlockSpec((None, N_PAIRS, LANES, SSD_STATE), lambda bb, c: (bb * nc + ci(c), 0, 0, 0))
    return rows, dtt, const, state


def ssd_fwd(xs, bm, cm, dtraw, dt_bias, a_log, b):
    t = xs.shape[0]
    nc = SEQ // Q
    rows, dtt_spec, const, state = _ssd_specs(b, nc, False)
    bias_r, bias_c = _pad_lanes(dt_bias)
    alog_r, alog_c = _pad_lanes(a_log)

    def body(xs_ref, b_ref, c_ref, dtr_ref, dtrt_ref, br, bc, ar, ac, y_ref, hp_ref, h_scr):
        @pl.when(pl.program_id(1) == 0)
        def _():
            h_scr[...] = jnp.zeros(h_scr.shape, F32)
        tril, lane, _, dt, cs, cst = _ssd_common(dtr_ref, dtrt_ref, br, bc, ar, ac)
        sub = lax.broadcasted_iota(jnp.int32, (LANES, 1), 0)
        y_acc = [jnp.zeros((Q, LANES), F32) for _ in range(N_PAIRS)]
        h_old = [h_scr[p] for p in range(N_PAIRS)]
        h_new = [jnp.zeros((LANES, SSD_STATE), F32) for _ in range(N_PAIRS)]
        for g in range(SSD_GROUPS):
            bg = b_ref[:, g * SSD_STATE:(g + 1) * SSD_STATE].astype(BF16)
            cg = c_ref[:, g * SSD_STATE:(g + 1) * SSD_STATE].astype(BF16)
            cb = _dot(cg, bg, _NT)
            for j in range(g * HEADS_PER_GROUP, (g + 1) * HEADS_PER_GROUP):
                p, side = j // 2, j % 2
                m = (lane < HEAD_DIM) if side == 0 else (lane >= HEAD_DIM)
                ms = (sub < HEAD_DIM) if side == 0 else (sub >= HEAD_DIM)
                csj, dtj = _rep(cs, j), _rep(dt, j)
                lmat = jnp.exp(jnp.where(tril, csj - cst[j:j + 1, :], NEG))
                xdt = jnp.where(m, xs_ref[:, p * LANES:(p + 1) * LANES] * dtj, 0.0)
                hm = jnp.where(ms, h_old[p], 0.0)
                ydiag = _dot((cb * lmat).astype(BF16), xdt.astype(BF16), _NN)
                yoff = jnp.exp(csj) * _dot(cg, hm.astype(BF16), _NT)
                y_acc[p] = y_acc[p] + ydiag + yoff
                last = csj[Q - 1:Q, :]
                sj = _dot((xdt * jnp.exp(last - csj)).astype(BF16), bg, _TN)
                h_new[p] = h_new[p] + jnp.exp(last) * hm + sj
        for p in range(N_PAIRS):
            y_ref[:, p * LANES:(p + 1) * LANES] = y_acc[p]
            hp_ref[p] = h_old[p]
            h_scr[p] = h_new[p]

    return pl.pallas_call(
        body, name="ssd_fwd", grid=(b, nc),
        in_specs=[rows(D_SSD), rows(D_BC), rows(D_BC), rows(LANES), dtt_spec, const((1, LANES)), const((LANES, 1)),
                  const((1, LANES)), const((LANES, 1))],
        out_specs=[rows(D_SSD), state],
        out_shape=[jax.ShapeDtypeStruct((t, D_SSD), F32),
                   jax.ShapeDtypeStruct((b * nc, N_PAIRS, LANES, SSD_STATE), F32)],
        scratch_shapes=[pltpu.VMEM((N_PAIRS, LANES, SSD_STATE), F32)],
        compiler_params=_cparams(("parallel", "arbitrary")),
    )(xs, bm, cm, dtraw, dtraw.T, bias_r, bias_c, alog_r, alog_c)


def ssd_bwd(xs, bm, cm, dtraw, dt_bias, a_log, hprev, dy, b):
    t = xs.shape[0]
    nc = SEQ // Q
    rows, dtt_spec, const, state = _ssd_specs(b, nc, True)
    bias_r, bias_c = _pad_lanes(dt_bias)
    alog_r, alog_c = _pad_lanes(a_log)

    def body(xs_ref, b_ref, c_ref, dtr_ref, dtrt_ref, hp_ref, dy_ref, br, bc, ar, ac,
             dxs_ref, db_ref, dc_ref, ddt_ref, dbias_ref, dalog_ref, dh_scr):
        first = jnp.logical_and(pl.program_id(0) == 0, pl.program_id(1) == 0)

        @pl.when(pl.program_id(1) == 0)
        def _():
            dh_scr[...] = jnp.zeros(dh_scr.shape, F32)

        @pl.when(first)
        def _():
            dbias_ref[...] = jnp.zeros(dbias_ref.shape, F32)
            dalog_ref[...] = jnp.zeros(dalog_ref.shape, F32)
        tril, lane, a_r, dt, cs, cst = _ssd_common(dtr_ref, dtrt_ref, br, bc, ar, ac)
        sub = lax.broadcasted_iota(jnp.int32, (LANES, 1), 0)
        rowq = lax.broadcasted_iota(jnp.int32, (Q, 1), 0)
        triu = (lax.broadcasted_iota(jnp.int32, (Q, Q), 0) <= lax.broadcasted_iota(jnp.int32, (Q, Q), 1)).astype(F32)
        dxs_acc = [jnp.zeros((Q, LANES), F32) for _ in range(N_PAIRS)]
        dh_in = [dh_scr[p] for p in range(N_PAIRS)]
        h_in = [hp_ref[p] for p in range(N_PAIRS)]
        dh_out = [jnp.zeros((LANES, SSD_STATE), F32) for _ in range(N_PAIRS)]
        ddt = jnp.zeros((Q, LANES), F32)
        dalog = jnp.zeros((1, LANES), F32)
        for g in range(SSD_GROUPS):
            gs = slice(g * SSD_STATE, (g + 1) * SSD_STATE)
            bg, cg = b_ref[:, gs].astype(BF16), c_ref[:, gs].astype(BF16)
            cb = _dot(cg, bg, _NT)
            dcb = jnp.zeros((Q, Q), F32)
            dbg = jnp.zeros((Q, SSD_STATE), F32)
            dcg = jnp.zeros((Q, SSD_STATE), F32)
            for j in range(g * HEADS_PER_GROUP, (g + 1) * HEADS_PER_GROUP):
                p, side = j // 2, j % 2
                m = (lane < HEAD_DIM) if side == 0 else (lane >= HEAD_DIM)
                ms = (sub < HEAD_DIM) if side == 0 else (sub >= HEAD_DIM)
                csj, dtj = _rep(cs, j), _rep(dt, j)
                lmat = jnp.exp(jnp.where(tril, csj - cst[j:j + 1, :], NEG))
                x2 = jnp.where(m, xs_ref[:, p * LANES:(p + 1) * LANES], 0.0)
                xdt = x2 * dtj
                dym = jnp.where(m, dy_ref[:, p * LANES:(p + 1) * LANES], 0.0)
                hm = jnp.where(ms, h_in[p], 0.0)
                dhm = jnp.where(ms, dh_in[p], 0.0)
                ecs = jnp.exp(csj)
                last = csj[Q - 1:Q, :]
                decay = jnp.exp(last - csj)
                el = jnp.exp(last)
                gmat = cb * lmat
                dymb, xdtb = dym.astype(BF16), xdt.astype(BF16)
                dg = _dot(dymb, xdtb, _NT)
                dxdt = _dot(gmat.astype(BF16), dymb, _TN)
                dcb = dcb + dg * lmat
                ej = dg * gmat
                col_sums = jnp.broadcast_to(jnp.sum(ej, axis=0, keepdims=True), (Q, Q)).T
                dcs = jnp.sum(ej, axis=1, keepdims=True) - col_sums
                ch = _dot(cg, hm.astype(BF16), _NT)
                dye = dym * ecs
                dcs = dcs + jnp.sum(dye * ch, axis=1, keepdims=True)
                dcg = dcg + _dot(dye.astype(BF16), hm.astype(BF16), _NN)
                dhp = _dot(dye.astype(BF16), cg, _TN)
                wmat = _dot(bg, dhm.astype(BF16), _NT)
                xd = xdt * decay
                dxdt = dxdt + decay * wmat
                ddl = jnp.sum(xd * wmat, axis=1, keepdims=True)
                dlast = jnp.sum(ddl, axis=0, keepdims=True) + el * jnp.sum(jnp.sum(dhm * hm, axis=1, keepdims=True), axis=0, keepdims=True)
                dcs = dcs - ddl + jnp.where(rowq == Q - 1, dlast, 0.0)
                dbg = dbg + _dot(xd.astype(BF16), dhm.astype(BF16), _NN)
                dh_out[p] = dh_out[p] + el * dhm + dhp
                da = _dot_exact01(triu, dcs, _NN, True)
                aj = jnp.sum(jnp.where(lane == j, a_r, 0.0), axis=1, keepdims=True)
                ddtj = da * aj + jnp.sum(dxdt * x2, axis=1, keepdims=True)
                ddt = ddt + jnp.where(lane == j, ddtj, 0.0)
                dalog = dalog + jnp.where(lane == j, jnp.sum(da * dtj, axis=0, keepdims=True) * aj, 0.0)
                dxs_acc[p] = dxs_acc[p] + dxdt * dtj
            dcbb = dcb.astype(BF16)
            dc_ref[:, gs] = dcg + _dot(dcbb, bg, _NN)
            db_ref[:, gs] = dbg + _dot(dcbb, cg, _TN)
        for p in range(N_PAIRS):
            dxs_ref[:, p * LANES:(p + 1) * LANES] = dxs_acc[p]
            dh_scr[p] = dh_out[p]
        ddtraw = ddt * _sigmoid(dtr_ref[...] + br[...])
        ddt_ref[...] = ddtraw
        dbias_ref[...] += jnp.sum(ddtraw, axis=0, keepdims=True)
        dalog_ref[...] += dalog

    return pl.pallas_call(
        body, name="ssd_bwd", grid=(b, nc),
        in_specs=[rows(D_SSD), rows(D_BC), rows(D_BC), rows(LANES), dtt_spec, state, rows(D_SSD), const((1, LANES)),
                  const((LANES, 1)), const((1, LANES)), const((LANES, 1))],
        out_specs=[rows(D_SSD), rows(D_BC), rows(D_BC), rows(LANES), const((1, LANES)), const((1, LANES))],
        out_shape=[jax.ShapeDtypeStruct((t, D_SSD), F32), jax.ShapeDtypeStruct((t, D_BC), F32),
                   jax.ShapeDtypeStruct((t, D_BC), F32), jax.ShapeDtypeStruct((t, LANES), F32),
                   jax.ShapeDtypeStruct((1, LANES), F32), jax.ShapeDtypeStruct((1, LANES), F32)],
        scratch_shapes=[pltpu.VMEM((N_PAIRS, LANES, SSD_STATE), F32)],
        compiler_params=_cparams(("arbitrary", "arbitrary")),
    )(xs, bm, cm, dtraw, dtraw.T, hprev, dy, bias_r, bias_c, alog_r, alog_c)


def _split_w_in(w_in):
    w_dt = jnp.pad(w_in[:, D_QKVZ + D_CONV:], ((0, 0), (0, LANES - N_HEADS)))
    return w_in[:, :D_QKVZ], w_in[:, D_QKVZ:D_QKVZ + D_CONV], w_dt


def mixer_fwd(hb, p, cosv, sinv, b):
    t = hb.shape[0]
    w_a, w_b, w_c = _split_w_in(p['w_in'])
    qkvz = mm("in_qkvz", [(hb, w_a, 'nn')], D_QKVZ)
    xbc = mm("in_xbc", [(hb, w_b, 'nn')], D_CONV)
    dtraw = mm("in_dt", [(hb, w_c, 'nn')], LANES)
    mixed, *lses = attn_fwd(qkvz, cosv, sinv, b)
    attn = attn_norm_fwd(mixed, p['attn_norm_w'])
    xs, bm, cm = conv_fwd(xbc, p['conv_w'], p['conv_b'])
    y, hprev = ssd_fwd(xs, bm, cm, dtraw, p['dt_bias'], p['a_log'], b)
    dskip = jnp.repeat(p['d_skip'].reshape(-1), HEAD_DIM).reshape(1, D_SSD)
    yg, = rowwise("ssd_gate", _gate, [y, xs, Op(qkvz, D_SSD, 3)], [dskip, p['ssd_norm_w']], [(t, D_SSD, BF16)])
    mix = mm("out_proj", [(attn, p['w_out'][:D_ATTN], 'nn'), (yg, p['w_out'][D_ATTN:], 'nn')], D_MODEL)
    res = dict(hb=hb, qkvz=qkvz, xbc=xbc, dtraw=dtraw, mixed=mixed, lses=lses, attn=attn, xs=xs, bm=bm, cm=cm,
               y=y, hprev=hprev, dskip=dskip, yg=yg, cosv=cosv, sinv=sinv)
    return mix, res


def mixer_bwd(r, p, dmix, dh_resid, b):
    t = dmix.shape[0]
    w_a, w_b, w_c = _split_w_in(p['w_in'])
    w_out = p['w_out']
    dattn = mm("out_bwd_dattn", [(dmix, w_out[:D_ATTN], 'nt')], D_ATTN)
    dyg = mm("out_bwd_dyg", [(dmix, w_out[D_ATTN:], 'nt')], D_SSD)
    dw_out = jnp.concatenate([mm_tn("out_bwd_dw_a", r['attn'], dmix, BF16),
                              mm_tn("out_bwd_dw_y", r['yg'], dmix, BF16)], axis=0)

    def gate_bwd(dy_, y_, xs_, z_, ds_, w_):
        _, vjp = jax.vjp(_gate, y_, xs_, z_, ds_, w_)
        return vjp(dy_)

    dy, dxs_a, dz, ddskip, dssd_norm = rowwise(
        "ssd_gate_bwd", gate_bwd, [dyg, r['y'], r['xs'], Op(r['qkvz'], D_SSD, 3)], [r['dskip'], p['ssd_norm_w']],
        [(t, D_SSD, F32), (t, D_SSD, F32), (t, D_SSD, BF16)], accs=[(1, D_SSD), (1, D_SSD)])
    dxs_b, dbm, dcm, ddtraw, ddt_bias, da_log = ssd_bwd(r['xs'], r['bm'], r['cm'], r['dtraw'], p['dt_bias'], p['a_log'],
                                                        r['hprev'], dy, b)
    dxbc, dconv_w, dconv_b = conv_bwd(r['xbc'], p['conv_w'], p['conv_b'], dxs_a, dxs_b, dbm, dcm)
    dmixed, dattn_norm = attn_norm_bwd(dattn, r['mixed'], p['attn_norm_w'])
    dq, dk, dv = attn_bwd(r['qkvz'], r['cosv'], r['sinv'], dmixed, r['mixed'], r['lses'], b)
    wq, wk, wv, wz = (w_a[:, i * D_ATTN:(i + 1) * D_ATTN] for i in range(4))
    dh = mm("in_bwd_dh", [(dq, wq, 'nt'), (dk, wk, 'nt'), (dv, wv, 'nt'), (dz, wz, 'nt'), (dxbc, w_b, 'nt'),
                          (ddtraw, w_c, 'nt')], D_MODEL, add=dh_resid, tn=512)
    h = r['hb']
    dw_in = jnp.concatenate([mm_tn("in_bwd_dwq", h, dq, BF16), mm_tn("in_bwd_dwk", h, dk, BF16),
                             mm_tn("in_bwd_dwv", h, dv, BF16), mm_tn("in_bwd_dwz", h, dz, BF16),
                             mm_tn("in_bwd_dwx", h, dxbc, BF16), mm_tn("in_bwd_dwdt", h, ddtraw, BF16)[:, :N_HEADS]], axis=1)
    head_sum = lambda v: v.reshape(N_HEADS, HEAD_DIM).sum(axis=1).reshape(1, N_HEADS)
    grads = dict(w_in=dw_in, w_out=dw_out, conv_w=dconv_w, conv_b=dconv_b, dt_bias=ddt_bias[:, :N_HEADS],
                 a_log=da_log[:, :N_HEADS], d_skip=head_sum(ddskip), attn_norm_w=dattn_norm, ssd_norm_w=dssd_norm)
    return dh, grads


FFN1_KEYS = ('ffn1_gate', 'ffn1_up', 'ffn1_down')
FFN2_KEYS = ('ffn2_gate', 'ffn2_up', 'ffn2_down')
MIXER_KEYS = ('w_in', 'conv_w', 'w_out')
FFN_COL = ('ffn1_gate', 'ffn1_up', 'ffn2_gate', 'ffn2_up')
FFN_ROW = ('ffn1_down', 'ffn2_down')
CONV_W_COMM = (8, 2 * LANES)
SMALL = 'small'


def comm_shape(k, shapes):
    if k in FFN_COL:
        return (D_MODEL, FF_PAD)
    if k in FFN_ROW:
        return (FF_PAD, D_MODEL)
    if k == 'conv_w':
        return CONV_W_COMM
    return tuple(shapes[k][1:])


def to_comm(k, vals, shapes):
    a = vals[k].reshape(shapes[k][1:])
    r_, c_ = comm_shape(k, shapes)
    return jnp.pad(a, ((0, r_ - a.shape[0]), (0, c_ - a.shape[1])))


SMALL_ROWS, SMALL_COLS = 16, D_CONV


def pack_small(small):
    rows = [jnp.pad(small[r].reshape(1, -1), ((0, 0), (0, SMALL_COLS - small[r].size))) for r in REPLICATED]
    return jnp.concatenate(rows + [jnp.zeros((SMALL_ROWS - len(rows), SMALL_COLS), F32)], axis=0)


def full_weight(k, g):
    if k in FFN_COL:
        return g
    if k == 'conv_w':
        return jnp.transpose(g[:, :CONV_WIDTH, :D_CONV // N_DEV], (1, 0, 2)).reshape(CONV_WIDTH, D_CONV)
    return g.reshape(N_DEV * g.shape[1], g.shape[2])


def grad_shards(k, g):
    if k in FFN_COL:
        return g
    if k == 'conv_w':
        s = jnp.transpose(g.reshape(CONV_WIDTH, N_DEV, D_CONV // N_DEV), (1, 0, 2))
        return jnp.pad(s, ((0, 0), (0, CONV_W_COMM[0] - CONV_WIDTH), (0, CONV_W_COMM[1] - D_CONV // N_DEV)))
    return g.reshape(N_DEV, g.shape[0] // N_DEV, g.shape[1])


def _flip(v, bit):
    return 1 - v if bit else v


N_PEER_COPIES = N_DEV - 1


def _comm_call(name, body, arrs, out_shape):
    n = len(arrs)
    return pl.pallas_call(
        functools.partial(body, n), name=name, out_shape=out_shape,
        in_specs=[pl.BlockSpec(memory_space=pl.ANY)] * n, out_specs=[pl.BlockSpec(memory_space=pl.ANY)] * n,
        scratch_shapes=[pltpu.SemaphoreType.DMA((n * N_PEER_COPIES,)), pltpu.SemaphoreType.DMA((n * N_PEER_COPIES,)),
                        pltpu.SemaphoreType.DMA((n,))],
    )(*arrs)


def _blk(ref, idx, by_cols):
    if not by_cols:
        return ref.at[idx]
    c = ref.shape[1] // N_DEV
    return ref.at[:, pl.ds(pl.multiple_of(idx * c, LANES), c)]


def _blocked_shape(a, by_cols):
    return (a.shape[0], N_DEV * a.shape[1]) if by_cols else (N_DEV,) + a.shape


def all_gather(arrs, by_cols):
    def body(n, *refs):
        x_refs, out_refs, (send_sems, recv_sems, local_sems) = refs[:n], refs[n:2 * n], refs[2 * n:]
        x, y, c = lax.axis_index("x"), lax.axis_index("y"), lax.axis_index("c")
        me, sibling = (x, y, c), (x, y, 1 - c)
        chips = [(1 - x, y), (x, 1 - y), (1 - x, 1 - y)]

        def copy(a, k, block, to, src=None):
            px, py, pc = block
            dst = _blk(out_refs[a], 4 * px + 2 * py + pc, by_cols[a])
            return pltpu.make_async_remote_copy(
                src_ref=dst if src is None else src, dst_ref=dst, send_sem=send_sems.at[a * N_PEER_COPIES + k],
                recv_sem=recv_sems.at[a * N_PEER_COPIES + k], device_id=to, device_id_type=MESH)

        mine = [pltpu.make_async_copy(x_refs[a], _blk(out_refs[a], 4 * x + 2 * y + c, by_cols[a]), local_sems.at[a])
                for a in range(n)]
        started = []
        for a in range(n):
            mine[a].start()
            first = [copy(a, 0, me, sibling, src=x_refs[a])]
            first += [copy(a, 1 + j, me, (*chip, c), src=x_refs[a]) for j, chip in enumerate(chips)]
            for cp in first:
                cp.start()
            started += first
        for j, chip in enumerate(chips):
            for a in range(n):
                copy(a, 1 + j, (*chip, c), me).wait_recv()
                cp = copy(a, 4 + j, (*chip, c), sibling)
                cp.start()
                started.append(cp)
        for a in range(n):
            copy(a, 0, sibling, me).wait_recv()
            for j, chip in enumerate(chips):
                copy(a, 4 + j, (*chip, 1 - c), me).wait_recv()
        for cp in started:
            cp.wait_send()
        for cp in mine:
            cp.wait()

    return _comm_call("all_gather_weights", body, arrs,
                      [jax.ShapeDtypeStruct(_blocked_shape(a, bc), a.dtype) for a, bc in zip(arrs, by_cols)])


def blocks_to_cols(arrs):
    def body(*refs):
        for i, o in zip(refs[:len(arrs)], refs[len(arrs):]):
            o[...] = i[...]

    return pl.pallas_call(
        body, name="blocks_to_cols", grid=(N_DEV,),
        in_specs=[pl.BlockSpec((None,) + a.shape[1:], lambda p: (p, 0, 0)) for a in arrs],
        out_specs=[pl.BlockSpec(a.shape[1:], lambda p: (0, p)) for a in arrs],
        out_shape=[jax.ShapeDtypeStruct((a.shape[1], N_DEV * a.shape[2]), a.dtype) for a in arrs],
        compiler_params=_cparams(("parallel",)),
    )(*arrs)


def _landing_shape(a, by_cols):
    return (N_DEV, a.shape[0], a.shape[1] // N_DEV) if by_cols else a.shape


def all_to_all(arrs, by_cols):
    def body(n, *refs):
        s_refs, r_refs, (send_sems, recv_sems, local_sems) = refs[:n], refs[n:2 * n], refs[2 * n:]
        x, y, c = lax.axis_index("x"), lax.axis_index("y"), lax.axis_index("c")
        me = 4 * x + 2 * y + c

        def peer(k):
            return _flip(x, k & 4), _flip(y, k & 2), _flip(c, k & 1)

        def copy(a, k, landing):
            px, py, pc = peer(k)
            p = 4 * px + 2 * py + pc
            src, dst = (me, p) if landing else (p, me)
            return pltpu.make_async_remote_copy(
                src_ref=_blk(s_refs[a], src, by_cols[a]), dst_ref=r_refs[a].at[dst],
                send_sem=send_sems.at[a * N_PEER_COPIES + k - 1],
                recv_sem=recv_sems.at[a * N_PEER_COPIES + k - 1], device_id=(px, py, pc), device_id_type=MESH)

        mine = [pltpu.make_async_copy(_blk(s_refs[a], me, by_cols[a]), r_refs[a].at[me], local_sems.at[a]) for a in range(n)]
        sends = [copy(a, k, False) for a in range(n) for k in range(1, N_DEV)]
        for cp in mine + sends:
            cp.start()
        for a in range(n):
            for k in range(1, N_DEV):
                copy(a, k, True).wait_recv()
        for cp in sends:
            cp.wait_send()
        for cp in mine:
            cp.wait()

    return _comm_call("all_to_all_grads", body, arrs,
                      [jax.ShapeDtypeStruct(_landing_shape(a, bc), a.dtype) for a, bc in zip(arrs, by_cols)])


_HBM = pl.BlockSpec(memory_space=pltpu.HBM)
_SEM = pl.BlockSpec(memory_space=pltpu.SEMAPHORE)
_EFFECT = pltpu.SideEffectType.DATAFLOW_SIDE_EFFECTING


def _peer(k):
    x, y, c = lax.axis_index("x"), lax.axis_index("y"), lax.axis_index("c")
    return _flip(x, k & 4), _flip(y, k & 2), _flip(c, k & 1)


def _my_index():
    return 4 * lax.axis_index("x") + 2 * lax.axis_index("y") + lax.axis_index("c")


def _split_copies(mode, by_cols, src_refs, land_refs, send_sems, recv_sems):
    me = _my_index()
    out = []
    for a, bc in enumerate(by_cols):
        for k in range(1, N_DEV):
            px, py, pc = _peer(k)
            src = _blk(src_refs[a], 4 * px + 2 * py + pc, bc) if mode == 'scatter' else src_refs[a]
            dst = land_refs[a].at[me] if mode == 'scatter' else _blk(land_refs[a], me, bc)
            out.append(pltpu.make_async_remote_copy(
                src_ref=src, dst_ref=dst, send_sem=send_sems.at[a * N_PEER_COPIES + k - 1],
                recv_sem=recv_sems.at[a * N_PEER_COPIES + k - 1], device_id=(px, py, pc), device_id_type=MESH))
    return out


def exchange_start(name, mode, srcs, by_cols):
    n = len(srcs)
    lands = [lax.empty(_landing_shape(s, bc) if mode == 'scatter' else _blocked_shape(s, bc), s.dtype)
             for s, bc in zip(srcs, by_cols)]

    def body(*refs):
        src_refs, land_refs, send_sems, recv_sems = refs[:n], refs[n:2 * n], refs[2 * n], refs[2 * n + 1]
        for cp in _split_copies(mode, by_cols, src_refs, land_refs, send_sems, recv_sems):
            cp.start()
        refs[-1][...] = jnp.zeros(refs[-1].shape, F32)

    sems = pltpu.SemaphoreType.DMA((n * N_PEER_COPIES,))
    res = pl.pallas_call(
        body, name=name,
        out_shape=(sems, sems, *[pltpu.HBM(a.shape, a.dtype) for a in srcs + lands], jax.ShapeDtypeStruct((8, LANES), F32)),
        in_specs=(_HBM,) * (2 * n), out_specs=(_SEM, _SEM, *(_HBM,) * (2 * n), pl.BlockSpec(memory_space=pltpu.VMEM)),
        input_output_aliases={i: 2 + i for i in range(2 * n)},
        compiler_params=pltpu.CompilerParams(has_side_effects=_EFFECT),
    )(*[pltpu.with_memory_space_constraint(a, pltpu.HBM) for a in srcs + lands])
    return (mode, by_cols, res[:-1]), res[-1]


def exchange_wait(name, handles, after):
    mode, by_cols, (send_sems, recv_sems, *bufs) = handles
    n = len(by_cols)

    def body(*refs):
        src_refs, land_refs, s_sems, r_sems = refs[:n], refs[n:2 * n], refs[2 * n], refs[2 * n + 1]
        for cp in _split_copies(mode, by_cols, src_refs, land_refs, s_sems, r_sems):
            cp.wait_send()
            cp.wait_recv()

    res = pl.pallas_call(
        body, name=name, out_shape=tuple(pltpu.HBM(a.shape, a.dtype) for a in bufs),
        in_specs=(*(_HBM,) * (2 * n), _SEM, _SEM, pl.BlockSpec(memory_space=pl.ANY)), out_specs=(_HBM,) * (2 * n),
        input_output_aliases={i: i for i in range(2 * n)},
        compiler_params=pltpu.CompilerParams(has_side_effects=_EFFECT),
    )(*bufs, send_sems, recv_sems, after)
    me, out = _my_index(), []
    for src, land, bc in zip(res[:n], res[n:], by_cols):
        if mode == 'scatter':
            c = land.shape[2]
            own = lax.dynamic_slice(src, (0, me * c), (src.shape[0], c)) if bc else lax.dynamic_index_in_dim(src, me, 0, False)
            out.append(lax.dynamic_update_slice(land, own[None], (me, 0, 0)))
        elif bc:
            out.append(lax.dynamic_update_slice(land, src, (0, me * src.shape[1])))
        else:
            out.append(lax.dynamic_update_slice(land, src[None], (me, 0, 0)))
    return out


def _adamw_math(g, w, m, v):
    c1 = 1.0 / (1.0 - ADAM_B1 ** ADAM_STEP)
    c2 = 1.0 / (1.0 - ADAM_B2 ** ADAM_STEP)
    m = ADAM_B1 * m + (1.0 - ADAM_B1) * g
    v = ADAM_B2 * v + (1.0 - ADAM_B2) * jnp.square(g)
    return g, -ADAM_LR * ((m * c1) / (jnp.sqrt(v * c2) + ADAM_EPS) + ADAM_WD * w), m, v


def adamw(name, recv, w, m, v, tm):
    rows, cols = w.shape
    tm = min(tm, rows)

    def body(*refs):
        g = refs[0][0:tm, 0:cols].astype(F32)
        for s in range(1, N_DEV):
            g = g + refs[s][0:tm, 0:cols].astype(F32)
        res = _adamw_math(g, *[r[...] for r in refs[N_DEV:N_DEV + 3]])
        for r, val in zip(refs[N_DEV + 3:], res):
            r[...] = val

    part = lambda s: pl.BlockSpec((None, recv.shape[1] if tm == rows else tm, recv.shape[2]), lambda i: (s, i, 0))
    tile = pl.BlockSpec((tm, cols), lambda i: (i, 0))
    return pl.pallas_call(
        body, name=name, grid=(rows // tm,), in_specs=[part(s) for s in range(N_DEV)] + [tile] * 3, out_specs=[tile] * 4,
        out_shape=[jax.ShapeDtypeStruct((rows, cols), F32)] * 4, compiler_params=_cparams(("parallel",)),
    )(*[recv] * N_DEV, w, m, v)


def adamw_small(recv, wl, ml, vl):
    n = len(REPLICATED)

    def body(recv_ref, *refs):
        g = recv_ref[0]
        for s in range(1, N_DEV):
            g = g + recv_ref[s]
        for r in range(n):
            w, m, v = (refs[j * n + r][...] for j in range(3))
            for j, val in enumerate(_adamw_math(g[r:r + 1, :w.shape[1]], w, m, v)):
                refs[(3 + j) * n + r][...] = val

    arrs = [d[k].reshape(1, -1) for d in (wl, ml, vl) for k in REPLICATED]
    res = pl.pallas_call(
        body, name="adamw_small", out_shape=[jax.ShapeDtypeStruct(a.shape, F32) for a in arrs[:n]] * 4,
    )(recv, *arrs)
    return [{k: res[j * n + r].reshape(wl[k].shape) for r, k in enumerate(REPLICATED)} for j in range(4)]


ADAMW_TM = {'ffn1_gate': 256, 'ffn1_up': 256, 'ffn2_gate': 256, 'ffn2_up': 256, 'w_in': 32}


def kernel(x, positions, ln1_g, ln1_b, ffn1_gate, ffn1_up, ffn1_down, w_in, conv_w, conv_b, dt_bias, a_log, d_skip, attn_norm_w, ssd_norm_w, w_out, ln2_g, ln2_b, ffn2_gate, ffn2_up, ffn2_down, ln3_g, ln3_b, loss_target, m_ln1_g, m_ln1_b, m_ffn1_gate, m_ffn1_up, m_ffn1_down, m_w_in, m_conv_w, m_conv_b, m_dt_bias, m_a_log, m_d_skip, m_attn_norm_w, m_ssd_norm_w, m_w_out, m_ln2_g, m_ln2_b, m_ffn2_gate, m_ffn2_up, m_ffn2_down, m_ln3_g, m_ln3_b, v_ln1_g, v_ln1_b, v_ffn1_gate, v_ffn1_up, v_ffn1_down, v_w_in, v_conv_w, v_conv_b, v_dt_bias, v_a_log, v_d_skip, v_attn_norm_w, v_ssd_norm_w, v_w_out, v_ln2_g, v_ln2_b, v_ffn2_gate, v_ffn2_up, v_ffn2_down, v_ln3_g, v_ln3_b):
    args = dict(locals())
    wl = {k: args[k] for k in WEIGHTS}
    ml = {k: args["m_" + k] for k in WEIGHTS}
    vl = {k: args["v_" + k] for k in WEIGHTS}
    shapes = {k: wl[k].shape for k in WEIGHTS}
    b, s, dm = x.shape
    t = b * s

    sent = {k: to_comm(k, wl, shapes).astype(F32 if k == 'conv_w' else BF16) for k in SHARDED}
    by_cols = lambda keys: [k in FFN_COL for k in keys]
    gate, up = all_gather([sent['ffn1_gate'], sent['ffn1_up']], [False] * 2)
    (gate, up), sent = lax.optimization_barrier(((gate, up), sent))
    p = dict(zip(('ffn1_gate', 'ffn1_up'), blocks_to_cols([gate, up])))
    gather_down, token_d = exchange_start("gather_ffn1_down_start", 'gather', [sent['ffn1_down']], [False])
    sent['w_in'] = sent['w_in'] + token_d[0, 0].astype(BF16)
    gather_mixer, token_m = exchange_start("gather_mixer_start", 'gather', [sent[k] for k in MIXER_KEYS], by_cols(MIXER_KEYS))
    sent['ffn2_gate'] = sent['ffn2_gate'] + token_m[0, 0].astype(BF16)
    gather_ffn2, token_f = exchange_start("gather_ffn2_start", 'gather', [sent[k] for k in FFN2_KEYS], by_cols(FFN2_KEYS))
    for k in REPLICATED:
        p[k] = wl[k].reshape(1, -1)

    x2 = x.reshape(t, dm)
    cosv, sinv = rope_tables(positions)
    g1, u1, a1, at1 = ffn_gate_up("ffn1_gate_up", x2, p['ffn1_gate'], p['ffn1_up'], after=(token_d, token_m, token_f))
    p['ffn1_down'] = full_weight('ffn1_down', exchange_wait("gather_ffn1_down_wait", gather_down, a1)[0])
    f1, res1 = mm("ffn1_down", [(a1, p['ffn1_down'], 'nn')], D_MODEL), (x2, g1, u1, at1)
    h1, h1b = resid_ln_fwd("ln1", 0.5, x2, f1, p['ln1_g'], p['ln1_b'])
    for k, g in zip(MIXER_KEYS, exchange_wait("gather_mixer_wait", gather_mixer, h1b)):
        p[k] = full_weight(k, g)
    mix, resm = mixer_fwd(h1b, p, cosv, sinv, b)
    h2, h2b = resid_ln_fwd("ln2", 1.0, h1, mix, p['ln2_g'], p['ln2_b'])
    for k, g in zip(FFN2_KEYS, exchange_wait("gather_ffn2_wait", gather_ffn2, h2b)):
        p[k] = full_weight(k, g)
    f2, res3 = ffn_fwd("ffn2", h2b, p['ffn2_gate'], p['ffn2_up'], p['ffn2_down'])

    small, full = {}, {}
    dh2_res, df2, small['ln3_g'], small['ln3_b'], sq = ln_loss_bwd("ln3_loss_bwd", h2, f2, loss_target.reshape(t, dm),
                                                                   p['ln3_g'], p['ln3_b'])
    loss = lax.psum(jnp.sum(sq) * (0.5 / dm), AXES)

    dh2, full['ffn2_gate'], full['ffn2_up'], full['ffn2_down'] = ffn_bwd("ffn2", res3, p['ffn2_gate'], p['ffn2_up'],
                                                                       p['ffn2_down'], df2, dh2_res)
    ffn2_exchange, token = exchange_start("grads_ffn2_start", 'scatter', [grad_shards(k, full[k]) for k in FFN2_KEYS],
                                          by_cols(FFN2_KEYS))
    dh1_res, dmix, small['ln2_g'], small['ln2_b'] = resid_ln_bwd("ln2_bwd", 1.0, h1, mix, p['ln2_g'] + token[:1, :1],
                                                                 p['ln2_b'], dh2)
    dh1, gm = mixer_bwd(resm, p, dmix, dh1_res, b)
    for k in ('conv_b', 'dt_bias', 'a_log', 'd_skip', 'attn_norm_w', 'ssd_norm_w'):
        small[k] = gm[k]
    mixer_exchange, token = exchange_start("grads_mixer_start", 'scatter', [grad_shards(k, gm[k]) for k in MIXER_KEYS],
                                           by_cols(MIXER_KEYS))
    dx_res, df1, small['ln1_g'], small['ln1_b'] = resid_ln_bwd("ln1_bwd", 0.5, x2, f1, p['ln1_g'] + token[:1, :1],
                                                               p['ln1_b'], dh1)
    hb, g, u, at = res1
    small_part = pack_small(small)
    dg, du = ffn_da_act("ffn1_bwd_da_act", df1, p['ffn1_down'], g, u)
    dwd = mm_acc("ffn1_bwd_dwd", at, df1, BF16, after=dg)
    down_exchange, token = exchange_start("grads_ffn1_down_start", 'scatter', [
        grad_shards('ffn1_down', dwd), jnp.broadcast_to(small_part[None], (N_DEV,) + small_part.shape)], [False, False])
    dx = mm("ffn1_bwd_dh", [(dg, p['ffn1_gate'], 'nt'), (du, p['ffn1_up'], 'nt')], D_MODEL, add=dx_res, tn=512, after=token)
    dwg = mm_tn("ffn1_bwd_dwg", hb, dg, BF16, after=dx)
    gate_exchange, token = exchange_start("grads_ffn1_gate_start", 'scatter', [grad_shards('ffn1_gate', dwg)], [True])
    dwu = mm_tn("ffn1_bwd_dwu", hb, du, BF16, after=token)
    recv = dict(zip(('ffn1_up',), all_to_all([grad_shards('ffn1_up', dwu)], [True])))
    for keys, name, ex in (((FFN2_KEYS), "grads_ffn2_wait", ffn2_exchange), (MIXER_KEYS, "grads_mixer_wait", mixer_exchange),
                           (('ffn1_down', SMALL), "grads_ffn1_down_wait", down_exchange),
                           (('ffn1_gate',), "grads_ffn1_gate_wait", gate_exchange)):
        recv.update(zip(keys, exchange_wait(name, ex, recv['ffn1_up'])))
    outs = adamw_small(recv.pop(SMALL), wl, ml, vl)
    for k, r in recv.items():
        shard = shapes[k][1:]
        res = adamw(f"adamw_{k}", r, *[d[k].reshape(shard) for d in (wl, ml, vl)], ADAMW_TM.get(k, shard[0]))
        for o, a in zip(outs, res):
            o[k] = a.reshape(shapes[k])
    return (loss, dx.reshape(b, s, dm), *[o[k] for o in outs for k in WEIGHTS])
```

```python
import functools
import math

import jax
import jax.numpy as jnp
import numpy as np
from jax import lax
from jax.experimental import pallas as pl
from jax.experimental.pallas import tpu as pltpu

F32, BF16 = jnp.float32, jnp.bfloat16
HI = lax.Precision.HIGHEST
MESH = pl.DeviceIdType.MESH
AXES = ("x", "y", "c")
N_DEV = 8

D_MODEL = 1024
SEQ = 2048
HEAD_DIM = 64
N_HEADS = 12
D_ATTN = N_HEADS * HEAD_DIM
DILATIONS = (1, 4, 16)
ATTN_BLOCK = 128
ROPE_THETA = 500000.0
ROPE_DIM = 16
D_SSD = 768
SSD_GROUPS = 4
SSD_STATE = 128
SSD_CHUNK = 128
D_BC = SSD_GROUPS * SSD_STATE
D_CONV = D_SSD + 2 * D_BC
CONV_WIDTH = 4
D_QKVZ = 3 * D_ATTN + D_SSD
D_IN_PROJ = D_QKVZ + D_CONV + N_HEADS
D_FF = 2816
ALPHA = 2.0 ** 0.25
LN_EPS = 1e-5
RMS_EPS = 1e-6
ADAM_LR, ADAM_B1, ADAM_B2, ADAM_EPS, ADAM_WD, ADAM_STEP = 0.001, 0.9, 0.999, 1e-08, 0.01, 10

LANES = 128
VMEM_LIMIT = 52 * 1024 * 1024
NEG = -1e30

WEIGHTS = ['ln1_g', 'ln1_b', 'ffn1_gate', 'ffn1_up', 'ffn1_down', 'w_in', 'conv_w', 'conv_b', 'dt_bias', 'a_log',
           'd_skip', 'attn_norm_w', 'ssd_norm_w', 'w_out', 'ln2_g', 'ln2_b', 'ffn2_gate', 'ffn2_up', 'ffn2_down',
           'ln3_g', 'ln3_b']
COL_SHARDED = ('ffn1_gate', 'ffn1_up', 'conv_w', 'ffn2_gate', 'ffn2_up')
ROW_SHARDED = ('ffn1_down', 'w_in', 'w_out', 'ffn2_down')
SHARDED = tuple(n for n in WEIGHTS if n in COL_SHARDED or n in ROW_SHARDED)
REPLICATED = tuple(n for n in WEIGHTS if n not in SHARDED)
FF_SHARD = D_FF // N_DEV
FF_PAD = -(-FF_SHARD // LANES) * LANES
D_FF_INT = N_DEV * FF_PAD


def _cparams(sem=None):
    return pltpu.CompilerParams(dimension_semantics=sem, vmem_limit_bytes=VMEM_LIMIT)


def _tile(n, prefs):
    for p in prefs:
        if n % p == 0:
            return p
    return n


class Op:
    def __init__(self, arr, bw=None, cb=0, ro=0):
        self.arr, self.bw, self.cb, self.ro = arr, (arr.shape[1] if bw is None else bw), cb, ro


def _op(a):
    return a if isinstance(a, Op) else Op(a)


def rowwise(name, fn, ins, consts, outs, accs=(), tm=256):
    ins = [_op(a) for a in ins]
    rows = outs[0][0]
    n_in, n_c, n_o, n_a = len(ins), len(consts), len(outs), len(accs)
    tm = min(tm, rows)
    assert rows % tm == 0, (name, rows, tm)

    def body(*refs):
        vals = [r[...].astype(F32) for r in refs[:n_in + n_c]]
        res = fn(*vals)
        res = res if isinstance(res, (tuple, list)) else (res,)
        o_refs = refs[n_in + n_c:n_in + n_c + n_o]
        a_refs = refs[n_in + n_c + n_o:]
        for r, v in zip(o_refs, res[:n_o]):
            r[...] = v.astype(r.dtype)
        if n_a:
            @pl.when(pl.program_id(0) == 0)
            def _():
                for r in a_refs:
                    r[...] = jnp.zeros(r.shape, r.dtype)
            for r, v in zip(a_refs, res[n_o:]):
                r[...] += v

    in_specs = [pl.BlockSpec((tm, o.bw), functools.partial(lambda i, o: (i + o.ro, o.cb), o=o)) for o in ins]
    in_specs += [pl.BlockSpec(c.shape, functools.partial(lambda i, nd: (0,) * nd, nd=c.ndim)) for c in consts]
    out_specs = [pl.BlockSpec((tm, w), lambda i: (i, 0)) for (_, w, _) in outs]
    out_specs += [pl.BlockSpec(s, functools.partial(lambda i, nd: (0,) * nd, nd=len(s))) for s in accs]
    out_shape = [jax.ShapeDtypeStruct((r, w), dt) for (r, w, dt) in outs]
    out_shape += [jax.ShapeDtypeStruct(s, F32) for s in accs]
    res = pl.pallas_call(
        body, name=name, grid=(rows // tm,), in_specs=in_specs, out_specs=out_specs, out_shape=out_shape,
        compiler_params=_cparams(("arbitrary",) if n_a else ("parallel",)),
    )(*[o.arr for o in ins], *consts)
    return res


MM_TM = 512
MM_TN = (1024, 896, 768, 512, 256, 128)
_NT = (((1,), (1,)), ((), ()))
_NN = (((1,), (0,)), ((), ()))
_TN = (((0,), (0,)), ((), ()))


def _dot(a, b, dn, precision=None):
    return lax.dot_general(a, b, dn, preferred_element_type=F32, precision=precision)


def _mm_specs(name, pairs, n_out, tm, tn):
    in_specs, args = [], []
    for a, b, mode in pairs:
        o = _op(a)
        in_specs.append(pl.BlockSpec((tm, o.bw), functools.partial(lambda j, i, o: (i, o.cb), o=o)))
        args.append(o.arr)
        if mode == 'nn':
            assert b.shape == (o.bw, n_out), (name, b.shape, o.bw, n_out)
            in_specs.append(pl.BlockSpec((o.bw, tn), lambda j, i: (0, j)))
        else:
            assert b.shape == (n_out, o.bw), (name, b.shape, o.bw, n_out)
            in_specs.append(pl.BlockSpec((tn, o.bw), lambda j, i: (j, 0)))
        args.append(b)
    return in_specs, args


def _mm_acc(refs, pairs):
    acc = None
    for k, (_, _, mode) in enumerate(pairs):
        d = _dot(refs[2 * k][...].astype(BF16), refs[2 * k + 1][...].astype(BF16), _NN if mode == 'nn' else _NT)
        acc = d if acc is None else acc + d
    return acc


def mm(name, pairs, n_out, add=None, out_dtype=F32, tm=MM_TM, tn=None, after=None):
    m = _op(pairs[0][0]).arr.shape[0]
    tn = tn or _tile(n_out, MM_TN)
    n_p = len(pairs)

    def body(*refs):
        acc = _mm_acc(refs, pairs)
        if add is not None:
            acc = acc + refs[2 * n_p][...]
        refs[-1][...] = acc.astype(refs[-1].dtype)

    in_specs, args = _mm_specs(name, pairs, n_out, tm, tn)
    tile = pl.BlockSpec((tm, tn), lambda j, i: (i, j))
    if add is not None:
        in_specs.append(tile)
        args.append(add)
    if after is not None:
        in_specs.append(pl.BlockSpec(memory_space=pl.ANY))
        args.append(after)
    return pl.pallas_call(
        body, name=name, grid=(n_out // tn, m // tm), in_specs=in_specs, out_specs=tile,
        out_shape=jax.ShapeDtypeStruct((m, n_out), out_dtype),
        compiler_params=_cparams(("parallel", "parallel")),
    )(*args)


def mm_tn(name, a, b, out_dtype=F32, tt=1024, after=None):
    a, b = _op(a), _op(b)
    t = a.arr.shape[0]
    k, n = a.bw, b.bw
    tk = _tile(k, (512, 896, 768, 256, 128))
    tn = _tile(n, (3072, 1792) + MM_TN)
    tt = min(tt, t)
    n_t = t // tt
    order = [] if after is None else [after]

    def body(a_ref, b_ref, *rest):
        o_ref, acc_ref = rest[-2:]
        s = pl.program_id(2)
        d = _dot(a_ref[...].astype(BF16), b_ref[...].astype(BF16), _TN)

        @pl.when(s == 0)
        def _():
            acc_ref[...] = d

        @pl.when(s > 0)
        def _():
            acc_ref[...] += d

        @pl.when(s == n_t - 1)
        def _():
            o_ref[...] = acc_ref[...].astype(o_ref.dtype)

    return pl.pallas_call(
        body, name=name, grid=(k // tk, n // tn, n_t),
        in_specs=[pl.BlockSpec((tt, tk), functools.partial(lambda kk, nn, s, o: (s, o.cb * (o.bw // tk) + kk), o=a)),
                  pl.BlockSpec((tt, tn), functools.partial(lambda kk, nn, s, o: (s, o.cb * (o.bw // tn) + nn), o=b))]
        + [pl.BlockSpec(memory_space=pl.ANY) for _ in order],
        out_specs=pl.BlockSpec((tk, tn), lambda kk, nn, s: (kk, nn)),
        out_shape=jax.ShapeDtypeStruct((k, n), out_dtype),
        scratch_shapes=[pltpu.VMEM((tk, tn), F32)],
        compiler_params=_cparams(("parallel", "parallel", "arbitrary")),
    )(a.arr, b.arr, *order)


def _sigmoid(x):
    return 1.0 / (1.0 + jnp.exp(-x))


def _silu(x):
    return x * _sigmoid(x)


def _softplus(x):
    return jnp.maximum(x, 0.0) + jnp.log(1.0 + jnp.exp(-jnp.abs(x)))


def _act(g, u):
    return _silu(g) * u


def _resid_ln(scale, h, branch, g, b):
    r = ALPHA * h + scale * branch
    mu = jnp.mean(r, axis=-1, keepdims=True)
    var = jnp.mean(jnp.square(r - mu), axis=-1, keepdims=True)
    return (r - mu) * lax.rsqrt(var + LN_EPS) * g + b


def _rms(t, w):
    return t * lax.rsqrt(jnp.mean(t * t, axis=-1, keepdims=True) + RMS_EPS) * w


def _branch_weights(l1, l2, l3):
    m = jnp.maximum(jnp.maximum(l1, l2), l3)
    e1, e2, e3 = jnp.exp(l1 - m), jnp.exp(l2 - m), jnp.exp(l3 - m)
    inv = 1.0 / (e1 + e2 + e3)
    return e1 * inv, e2 * inv, e3 * inv


def _gate(y, xs, z, dskip, w):
    return _rms((y + dskip * xs) * _silu(z), w)


def _rot(x):
    d = lax.broadcasted_iota(jnp.int32, x.shape, 1) % HEAD_DIM
    up = pltpu.roll(x, x.shape[1] - ROPE_DIM // 2, 1)
    down = jnp.where(d < ROPE_DIM, pltpu.roll(x, ROPE_DIM // 2, 1), 0.0)
    return jnp.where(d < ROPE_DIM // 2, up, down)


def ffn_gate_up(name, h, wg, wu, after=()):
    m, nf = h.shape[0], wg.shape[1]
    tn = _tile(nf, MM_TN)

    def body(h_ref, g_w, u_w, *rest):
        g_ref, u_ref, a_ref, at_ref = rest[-4:]
        hb = h_ref[...].astype(BF16)
        g = _dot(hb, g_w[...].astype(BF16), _NN)
        u = _dot(hb, u_w[...].astype(BF16), _NN)
        g_ref[...] = g.astype(g_ref.dtype)
        u_ref[...] = u.astype(u_ref.dtype)
        a = _act(g, u)
        a_ref[...] = a.astype(a_ref.dtype)
        at_ref[...] = a.T.astype(at_ref.dtype)

    in_specs, args = _mm_specs(name, [(h, wg, 'nn')], nf, MM_TM, tn)
    in_specs.append(in_specs[1])
    in_specs += [pl.BlockSpec(memory_space=pl.ANY) for _ in after]
    tile = pl.BlockSpec((MM_TM, tn), lambda j, i: (i, j))
    return pl.pallas_call(
        body, name=name, grid=(nf // tn, m // MM_TM), in_specs=in_specs,
        out_specs=[tile] * 3 + [pl.BlockSpec((tn, MM_TM), lambda j, i: (j, i))],
        out_shape=[jax.ShapeDtypeStruct((m, nf), BF16)] * 3 + [jax.ShapeDtypeStruct((nf, m), BF16)],
        compiler_params=_cparams(("parallel", "parallel")),
    )(*args, wu, *after)


def mm_acc(name, a, b, out_dtype=F32, tt=1024, after=None):
    k, t = a.shape
    n = b.shape[1]
    tk, tn, tt = _tile(k, (1024, 512, 256, 128)), _tile(n, MM_TN), min(tt, t)
    n_t = t // tt
    order = [] if after is None else [after]

    def body(a_ref, b_ref, *rest):
        o_ref, acc_ref = rest[-2:]
        s = pl.program_id(2)
        d = _dot(a_ref[...].astype(BF16), b_ref[...].astype(BF16), _NN)

        @pl.when(s == 0)
        def _():
            acc_ref[...] = d

        @pl.when(s > 0)
        def _():
            acc_ref[...] += d

        @pl.when(s == n_t - 1)
        def _():
            o_ref[...] = acc_ref[...].astype(o_ref.dtype)

    return pl.pallas_call(
        body, name=name, grid=(k // tk, n // tn, n_t),
        in_specs=[pl.BlockSpec((tk, tt), lambda kk, nn, s: (kk, s)), pl.BlockSpec((tt, tn), lambda kk, nn, s: (s, nn))]
        + [pl.BlockSpec(memory_space=pl.ANY) for _ in order],
        out_specs=pl.BlockSpec((tk, tn), lambda kk, nn, s: (kk, nn)),
        out_shape=jax.ShapeDtypeStruct((k, n), out_dtype), scratch_shapes=[pltpu.VMEM((tk, tn), F32)],
        compiler_params=_cparams(("parallel", "parallel", "arbitrary")),
    )(a, b, *order)


def ffn_da_act(name, df, wd, g, u):
    m, nf = df.shape[0], wd.shape[0]
    tn = _tile(nf, MM_TN)

    def body(df_ref, w_ref, g_ref, u_ref, dg_ref, du_ref):
        da = _dot(df_ref[...].astype(BF16), w_ref[...].astype(BF16), _NT)
        g, u = g_ref[...].astype(F32), u_ref[...].astype(F32)
        sig = _sigmoid(g)
        gs = g * sig
        dg_ref[...] = (da * u * (sig + gs * (1.0 - sig))).astype(dg_ref.dtype)
        du_ref[...] = (da * gs).astype(du_ref.dtype)

    in_specs, args = _mm_specs(name, [(df, wd, 'nt')], nf, MM_TM, tn)
    tile = pl.BlockSpec((MM_TM, tn), lambda j, i: (i, j))
    return pl.pallas_call(
        body, name=name, grid=(nf // tn, m // MM_TM), in_specs=in_specs + [tile, tile], out_specs=[tile] * 2,
        out_shape=[jax.ShapeDtypeStruct((m, nf), BF16)] * 2, compiler_params=_cparams(("parallel", "parallel")),
    )(*args, g, u)


def resid_ln_fwd(name, scale, h, branch, ln_g, ln_b):
    t = h.shape[0]

    def fn(*a):
        y = _resid_ln(scale, *a)
        return y, y

    return rowwise(name, fn, [h, branch], [ln_g, ln_b], [(t, D_MODEL, F32), (t, D_MODEL, BF16)], tm=512)


def ffn_fwd(tag, hb, wg, wu, wd, after=()):
    g, u, a, at = ffn_gate_up(f"{tag}_gate_up", hb, wg, wu, after)
    f = mm(f"{tag}_down", [(a, wd, 'nn')], D_MODEL)
    return f, (hb, g, u, at)


def ln_loss_bwd(name, h, branch, target, ln_g, ln_b):
    t, dm = h.shape

    def fn(h_, br_, tgt, g_, b_):
        y, vjp = jax.vjp(functools.partial(_resid_ln, 0.5), h_, br_, g_, b_)
        e = y - tgt
        return (*vjp(e * (1.0 / dm)), jnp.sum(e * e, axis=0, keepdims=True))

    return rowwise(name, fn, [h, branch, target], [ln_g, ln_b], [(t, dm, F32), (t, dm, BF16)],
                   accs=[(1, dm), (1, dm), (1, dm)], tm=512)


def resid_ln_bwd(name, scale, h, branch, ln_g, ln_b, dout, extra=None):
    t = h.shape[0]

    def fn(h_, br_, do_, *rest):
        g_, b_ = rest[-2], rest[-1]
        _, vjp = jax.vjp(functools.partial(_resid_ln, scale), h_, br_, g_, b_)
        dh, dbr, dg, db = vjp(do_)
        if extra is not None:
            dh = dh + rest[0]
        return dh, dbr, dg, db

    ins = [h, branch, dout] + ([extra] if extra is not None else [])
    return rowwise(name, fn, ins, [ln_g, ln_b], [(t, D_MODEL, F32), (t, D_MODEL, BF16)],
                   accs=[(1, D_MODEL), (1, D_MODEL)], tm=512)


def ffn_bwd(tag, res, wg, wu, wd, df, dh_resid):
    hb, g, u, at = res
    dg, du = ffn_da_act(f"{tag}_bwd_da_act", df, wd, g, u)
    dwd = mm_acc(f"{tag}_bwd_dwd", at, df, BF16)
    dh = mm(f"{tag}_bwd_dh", [(dg, wg, 'nt'), (du, wu, 'nt')], D_MODEL, add=dh_resid, tn=512)
    dwg = mm_tn(f"{tag}_bwd_dwg", hb, dg, BF16)
    dwu = mm_tn(f"{tag}_bwd_dwu", hb, du, BF16)
    return dh, dwg, dwu, dwd


def rope_tables(positions):
    inv_freq = ROPE_THETA ** (-jnp.arange(0, ROPE_DIM, 2, dtype=F32) / ROPE_DIM)
    ang = positions.reshape(-1, 1).astype(F32) * inv_freq
    c, s = jnp.cos(ang), jnp.sin(ang)
    t = ang.shape[0]
    cosv = jnp.concatenate([c, c, jnp.ones((t, HEAD_DIM - ROPE_DIM), F32)], axis=1)
    sinv = jnp.concatenate([-s, s, jnp.zeros((t, HEAD_DIM - ROPE_DIM), F32)], axis=1)
    return jnp.tile(cosv, (1, 2)), jnp.tile(sinv, (1, 2))


def _pair_masks():
    lane = lax.broadcasted_iota(jnp.int32, (1, LANES), 1)
    return (lane < HEAD_DIM, lane >= HEAD_DIM)


def _band_masks():
    row = lax.broadcasted_iota(jnp.int32, (ATTN_BLOCK, ATTN_BLOCK), 0)
    col = lax.broadcasted_iota(jnp.int32, (ATTN_BLOCK, ATTN_BLOCK), 1)
    return col >= row, col <= row


def _residue_blocks():
    out = []
    for g, d in enumerate(DILATIONS):
        for r in range(d):
            for i in range(SEQ // d // ATTN_BLOCK):
                rows = lambda j: pl.ds(r + j * ATTN_BLOCK * d, ATTN_BLOCK, stride=d) if d > 1 else pl.ds(j * ATTN_BLOCK, ATTN_BLOCK)
                out.append((g, rows(i), rows(i - 1) if i > 0 else None))
    return out


N_HEAD_PAIRS = D_ATTN // LANES
SCALE = HEAD_DIM ** -0.5
ATTN_GROUP = 4


def _block_operands(qr, kr, v_ref, cur, prev):
    prev_ok, cur_ok = _band_masks()
    if prev is None:
        return qr[cur, :], kr[cur, :].astype(BF16), v_ref[cur, :], cur_ok
    kcat = jnp.concatenate([kr[prev, :], kr[cur, :]], axis=0).astype(BF16)
    vcat = jnp.concatenate([v_ref[prev, :], v_ref[cur, :]], axis=0)
    return qr[cur, :], kcat, vcat, jnp.concatenate([prev_ok, cur_ok], axis=1)


def _attn_specs(b):
    col = lambda cb: pl.BlockSpec((SEQ, LANES), lambda bb, hp: (bb, cb + hp))
    tab = pl.BlockSpec((SEQ, LANES), lambda bb, hp: (bb, 0))
    return col, tab


def attn_fwd(qkvz, cosv, sinv, b):
    t = qkvz.shape[0]
    col, tab = _attn_specs(b)
    blocks = _residue_blocks()

    def body(q_ref, k_ref, v_ref, c_ref, s_ref, o_ref, l1_ref, l2_ref, l3_ref, qr, kr, o1, o2, o3):
        l_refs, o_scr = (l1_ref, l2_ref, l3_ref), (o1, o2, o3)
        c, s = c_ref[...], s_ref[...]
        q, k = q_ref[...], k_ref[...]
        qr[...] = q * c + _rot(q) * s
        kr[...] = k * c + _rot(k) * s
        masks = _pair_masks()
        for lo in range(0, len(blocks), ATTN_GROUP):
            chains = []
            for g, cur, prev in blocks[lo:lo + ATTN_GROUP]:
                q2, kcat, vcat, ok = _block_operands(qr, kr, v_ref, cur, prev)
                for m in masks:
                    qm = jnp.where(m, q2, 0.0).astype(BF16)
                    chains.append(dict(g=g, cur=cur, m=m, v=jnp.where(m, vcat, 0.0).astype(BF16),
                                       s=jnp.where(ok, _dot(qm, kcat, _NT) * SCALE, NEG)))
            for ch in chains:
                mx = jnp.max(ch['s'], axis=1, keepdims=True)
                p = jnp.exp(ch['s'] - mx)
                den = jnp.sum(p, axis=1, keepdims=True)
                ch.update(p=p.astype(BF16), inv=1.0 / den, lse=mx + jnp.log(den))
            for ch in chains:
                ch['o'] = _dot(ch['p'], ch['v'], _NN) * ch['inv']
            for c0, c1 in zip(chains[0::2], chains[1::2]):
                o_scr[c0['g']][c0['cur'], :] = c0['o'] + c1['o']
                l_refs[c0['g']][c0['cur'], :] = jnp.where(c0['m'], c0['lse'], c1['lse'])
        w1, w2, w3 = _branch_weights(l1_ref[...], l2_ref[...], l3_ref[...])
        o_ref[...] = w1 * o1[...] + w2 * o2[...] + w3 * o3[...]

    shp = jax.ShapeDtypeStruct((t, D_ATTN), F32)
    return pl.pallas_call(
        body, name="attn_fwd", grid=(b, N_HEAD_PAIRS),
        in_specs=[col(0), col(N_HEAD_PAIRS), col(2 * N_HEAD_PAIRS), tab, tab],
        out_specs=[col(0)] * 4, out_shape=[shp] * 4,
        scratch_shapes=[pltpu.VMEM((SEQ, LANES), F32)] * 5,
        compiler_params=_cparams(("parallel", "parallel")),
    )(qkvz, qkvz, qkvz, cosv, sinv)


def attn_bwd(qkvz, cosv, sinv, dmix, mixed, lses, b):
    t = qkvz.shape[0]
    col, tab = _attn_specs(b)
    blocks = _residue_blocks()
    hd = np.arange(LANES) // HEAD_DIM
    head_ones = jnp.asarray((hd[:, None] == hd[None, :]).astype(np.float32))

    def body(q_ref, k_ref, v_ref, c_ref, s_ref, dm_ref, mx_ref, l1_ref, l2_ref, l3_ref, ones_ref,
             dq_out, dk_out, dv_out, qr, kr, do1, do2, do3, dd1, dd2, dd3, dq_ref, dk_ref, dv_ref):
        l_refs, do_scr, dd_scr = (l1_ref, l2_ref, l3_ref), (do1, do2, do3), (dd1, dd2, dd3)
        c, s = c_ref[...], s_ref[...]
        q, k = q_ref[...], k_ref[...]
        qr[...] = q * c + _rot(q) * s
        kr[...] = k * c + _rot(k) * s
        dm = dm_ref[...]
        tot = _dot(dm * mx_ref[...], ones_ref[...], _NN, HI)
        for w, do_g, dd_g in zip(_branch_weights(l1_ref[...], l2_ref[...], l3_ref[...]), do_scr, dd_scr):
            do_g[...] = w * dm
            dd_g[...] = w * tot
        dq_ref[...] = jnp.zeros((SEQ, LANES), F32)
        dk_ref[...] = jnp.zeros((SEQ, LANES), F32)
        dv_ref[...] = jnp.zeros((SEQ, LANES), F32)
        masks = _pair_masks()
        for lo in range(0, len(blocks), ATTN_GROUP):
            chains = []
            for g, cur, prev in blocks[lo:lo + ATTN_GROUP]:
                q2, kcat, vcat, ok = _block_operands(qr, kr, v_ref, cur, prev)
                vcat = vcat.astype(BF16)
                do2_, l2, dd2_ = do_scr[g][cur, :], l_refs[g][cur, :], dd_scr[g][cur, :]
                l2s, dd2s = pltpu.roll(l2, HEAD_DIM, 1), pltpu.roll(dd2_, HEAD_DIM, 1)
                for m in masks:
                    qm = jnp.where(m, q2, 0.0).astype(BF16)
                    dom = jnp.where(m, do2_, 0.0).astype(BF16)
                    lrep, ddrep = jnp.where(m, l2, l2s), jnp.where(m, dd2_, dd2s)
                    if prev is not None:
                        lrep, ddrep = jnp.concatenate([lrep, lrep], axis=1), jnp.concatenate([ddrep, ddrep], axis=1)
                    chains.append(dict(cur=cur, prev=prev, qm=qm, dom=dom, km=jnp.where(m, kcat, 0), lrep=lrep, ddrep=ddrep,
                                       s=jnp.where(ok, _dot(qm, kcat, _NT) * SCALE, NEG), dp=_dot(dom, vcat, _NT)))
            for ch in chains:
                p = jnp.exp(ch['s'] - ch['lrep'])
                ch.update(p=p.astype(BF16), ds=(p * (ch['dp'] - ch['ddrep']) * SCALE).astype(BF16))
            for ch in chains:
                ch.update(dq=_dot(ch['ds'], ch['km'], _NN), dk=_dot(ch['ds'], ch['qm'], _TN), dv=_dot(ch['p'], ch['dom'], _TN))
            for c0, c1 in zip(chains[0::2], chains[1::2]):
                cur, prev = c0['cur'], c0['prev']
                dk, dv = c0['dk'] + c1['dk'], c0['dv'] + c1['dv']
                dq_ref[cur, :] += c0['dq'] + c1['dq']
                if prev is None:
                    dk_ref[cur, :] += dk
                    dv_ref[cur, :] += dv
                else:
                    dk_ref[prev, :] += dk[:ATTN_BLOCK]
                    dv_ref[prev, :] += dv[:ATTN_BLOCK]
                    dk_ref[cur, :] += dk[ATTN_BLOCK:]
                    dv_ref[cur, :] += dv[ATTN_BLOCK:]
        dq, dk = dq_ref[...], dk_ref[...]
        dq_out[...] = (dq * c + _rot(dq * s)).astype(dq_out.dtype)
        dk_out[...] = (dk * c + _rot(dk * s)).astype(dk_out.dtype)
        dv_out[...] = dv_ref[...].astype(dv_out.dtype)

    shp = jax.ShapeDtypeStruct((t, D_ATTN), BF16)
    return pl.pallas_call(
        body, name="attn_bwd", grid=(b, N_HEAD_PAIRS),
        in_specs=[col(0), col(N_HEAD_PAIRS), col(2 * N_HEAD_PAIRS), tab, tab, col(0), col(0), col(0), col(0), col(0),
                  pl.BlockSpec((LANES, LANES), lambda bb, hp: (0, 0))],
        out_specs=[col(0)] * 3, out_shape=[shp] * 3,
        scratch_shapes=[pltpu.VMEM((SEQ, LANES), F32)] * 11,
        compiler_params=_cparams(("parallel", "parallel")),
    )(qkvz, qkvz, qkvz, cosv, sinv, dmix, mixed, *lses, head_ones)


def attn_norm_fwd(mixed, norm_w):
    return rowwise("attn_norm", _rms, [mixed], [norm_w], [(mixed.shape[0], D_ATTN, BF16)])[0]


def attn_norm_bwd(dout, mixed, norm_w):
    def fn(dy, mx, w):
        _, vjp = jax.vjp(_rms, mx, w)
        return vjp(dy)

    return rowwise("attn_norm_bwd", fn, [dout, mixed], [norm_w], [(dout.shape[0], D_ATTN, F32)], accs=[(1, D_ATTN)])


CONV_TM = 256
HALO = 8


def _conv_columns(refs):
    xs_ref, bm_ref, cm_ref = refs
    out = []
    for c in range(D_CONV // LANES):
        lo = c * LANES
        ref, base = (xs_ref, 0) if lo < D_SSD else (bm_ref, D_SSD) if lo < D_SSD + D_BC else (cm_ref, D_SSD + D_BC)
        out.append((slice(lo, lo + LANES), (ref, slice(lo - base, lo - base + LANES))))
    return out


def _conv_taps(scr, w_ref, cs, first_row, step, tm):
    acc = None
    for k in range(CONV_WIDTH):
        term = w_ref[k:k + 1, cs] * scr[pl.ds(first_row + step * k, tm), cs]
        acc = term if acc is None else acc + term
    return acc


def conv_fwd(u, w, bias):
    t = u.shape[0]
    tm, per_seq = CONV_TM, SEQ // CONV_TM

    def body(u_ref, h_ref, w_ref, b_ref, xs_ref, bm_ref, cm_ref, scr):
        first = pl.program_id(0) % per_seq == 0
        scr[0:HALO, :] = jnp.where(first, 0.0, h_ref[...])
        scr[HALO:, :] = u_ref[...]
        for cs, (o_ref, os_) in _conv_columns((xs_ref, bm_ref, cm_ref)):
            o_ref[:, os_] = _silu(_conv_taps(scr, w_ref, cs, HALO - CONV_WIDTH + 1, 1, tm) + b_ref[:, cs])

    return pl.pallas_call(
        body, name="conv_fwd", grid=(t // tm,),
        in_specs=[pl.BlockSpec((tm, D_CONV), lambda i: (i, 0)),
                  pl.BlockSpec((HALO, D_CONV), lambda i: (jnp.maximum(i * (tm // HALO) - 1, 0), 0)),
                  pl.BlockSpec((CONV_WIDTH, D_CONV), lambda i: (0, 0)), pl.BlockSpec((1, D_CONV), lambda i: (0, 0))],
        out_specs=[pl.BlockSpec((tm, D_SSD), lambda i: (i, 0)), pl.BlockSpec((tm, D_BC), lambda i: (i, 0)),
                   pl.BlockSpec((tm, D_BC), lambda i: (i, 0))],
        out_shape=[jax.ShapeDtypeStruct((t, D_SSD), F32), jax.ShapeDtypeStruct((t, D_BC), F32),
                   jax.ShapeDtypeStruct((t, D_BC), F32)],
        scratch_shapes=[pltpu.VMEM((tm + HALO, D_CONV), F32)],
        compiler_params=_cparams(("parallel",)),
    )(u, u, w, bias)


def conv_bwd(u, w, bias, dxs_a, dxs_b, dbm, dcm):
    t = u.shape[0]
    tm, per_seq = CONV_TM, SEQ // CONV_TM
    n_tiles = t // tm

    def body1(u_ref, h_ref, dxs_ref, dxs2_ref, dbm_ref, dcm_ref, w_ref, b_ref, dz_ref, dw_ref, db_ref, scr):
        i = pl.program_id(0)
        first = i % per_seq == 0
        scr[0:HALO, :] = jnp.where(first, 0.0, h_ref[...])
        scr[HALO:, :] = u_ref[...]

        @pl.when(i == 0)
        def _():
            dw_ref[...] = jnp.zeros(dw_ref.shape, F32)
            db_ref[...] = jnp.zeros(db_ref.shape, F32)
        for cs, (g_ref, gs) in _conv_columns((dxs_ref, dbm_ref, dcm_ref)):
            acc = _conv_taps(scr, w_ref, cs, HALO - CONV_WIDTH + 1, 1, tm) + b_ref[:, cs]
            sig = _sigmoid(acc)
            dy = g_ref[:, gs] + dxs2_ref[:, gs] if g_ref is dxs_ref else g_ref[:, gs]
            dz = dy * sig * (1.0 + acc * (1.0 - sig))
            dz_ref[:, cs] = dz
            db_ref[:, cs] += jnp.sum(dz, axis=0, keepdims=True)
            for k in range(CONV_WIDTH):
                dw_ref[k:k + 1, cs] += jnp.sum(dz * scr[pl.ds(HALO - CONV_WIDTH + 1 + k, tm), cs], axis=0, keepdims=True)

    dz, dw, db = pl.pallas_call(
        body1, name="conv_bwd_dz", grid=(n_tiles,),
        in_specs=[pl.BlockSpec((tm, D_CONV), lambda i: (i, 0)),
                  pl.BlockSpec((HALO, D_CONV), lambda i: (jnp.maximum(i * (tm // HALO) - 1, 0), 0)),
                  pl.BlockSpec((tm, D_SSD), lambda i: (i, 0)), pl.BlockSpec((tm, D_SSD), lambda i: (i, 0)),
                  pl.BlockSpec((tm, D_BC), lambda i: (i, 0)), pl.BlockSpec((tm, D_BC), lambda i: (i, 0)),
                  pl.BlockSpec((CONV_WIDTH, D_CONV), lambda i: (0, 0)), pl.BlockSpec((1, D_CONV), lambda i: (0, 0))],
        out_specs=[pl.BlockSpec((tm, D_CONV), lambda i: (i, 0)), pl.BlockSpec((CONV_WIDTH, D_CONV), lambda i: (0, 0)),
                   pl.BlockSpec((1, D_CONV), lambda i: (0, 0))],
        out_shape=[jax.ShapeDtypeStruct((t, D_CONV), F32), jax.ShapeDtypeStruct((CONV_WIDTH, D_CONV), F32),
                   jax.ShapeDtypeStruct((1, D_CONV), F32)],
        scratch_shapes=[pltpu.VMEM((tm + HALO, D_CONV), F32)],
        compiler_params=_cparams(("arbitrary",)),
    )(u, u, dxs_a, dxs_b, dbm, dcm, w, bias)

    def body2(dz_ref, n_ref, w_ref, du_ref, scr):
        last = pl.program_id(0) % per_seq == per_seq - 1
        scr[0:tm, :] = dz_ref[...]
        scr[tm:, :] = jnp.where(last, 0.0, n_ref[...])
        for c in range(D_CONV // LANES):
            cs = slice(c * LANES, (c + 1) * LANES)
            du_ref[:, cs] = _conv_taps(scr, w_ref, cs, CONV_WIDTH - 1, -1, tm).astype(du_ref.dtype)

    du = pl.pallas_call(
        body2, name="conv_bwd_du", grid=(n_tiles,),
        in_specs=[pl.BlockSpec((tm, D_CONV), lambda i: (i, 0)),
                  pl.BlockSpec((HALO, D_CONV), lambda i: (jnp.minimum((i + 1) * (tm // HALO), t // HALO - 1), 0)),
                  pl.BlockSpec((CONV_WIDTH, D_CONV), lambda i: (0, 0))],
        out_specs=pl.BlockSpec((tm, D_CONV), lambda i: (i, 0)),
        out_shape=jax.ShapeDtypeStruct((t, D_CONV), BF16),
        scratch_shapes=[pltpu.VMEM((tm + HALO, D_CONV), F32)],
        compiler_params=_cparams(("parallel",)),
    )(dz, dz, w)
    return du, dw, db


Q = SSD_CHUNK
N_PAIRS = D_SSD // LANES
HEADS_PER_GROUP = N_HEADS // SSD_GROUPS


def _rep(a, j):
    return jnp.broadcast_to(a[:, j:j + 1], a.shape)


def _dot_exact01(a, b, dn, a_is_01):
    x = b if a_is_01 else a
    hi = x.astype(BF16)
    mid = (x - hi.astype(F32)).astype(BF16)
    lo = (x - hi.astype(F32) - mid.astype(F32)).astype(BF16)
    z = a.astype(BF16) if a_is_01 else b.astype(BF16)
    out = None
    for term in (hi, mid, lo):
        d = _dot(z, term, dn) if a_is_01 else _dot(term, z, dn)
        out = d if out is None else out + d
    return out


def _pad_lanes(v, fill=0.0):
    row = jnp.pad(v.reshape(1, -1).astype(F32), ((0, 0), (0, LANES - v.size)), constant_values=fill)
    return row, row.reshape(LANES, 1)


def _ssd_common(dtr_ref, dtrt_ref, bias_r, bias_c, alog_r, alog_c):
    row = lax.broadcasted_iota(jnp.int32, (Q, Q), 0)
    col = lax.broadcasted_iota(jnp.int32, (Q, Q), 1)
    tril = row >= col
    lane = lax.broadcasted_iota(jnp.int32, (1, LANES), 1)
    a_r = jnp.where(lane < N_HEADS, -jnp.exp(alog_r[...]), 0.0)
    sub = lax.broadcasted_iota(jnp.int32, (LANES, 1), 0)
    a_c = jnp.where(sub < N_HEADS, -jnp.exp(alog_c[...]), 0.0)
    dt = _softplus(dtr_ref[...] + bias_r[...])
    cs = _dot_exact01(tril, dt * a_r, _NN, True)
    dtt = _softplus(dtrt_ref[...] + bias_c[...])
    cst = _dot_exact01(dtt * a_c, row <= col, _NN, False)
    return tril, lane, a_r, dt, cs, cst


def _ssd_specs(b, nc, rev):
    ci = (lambda c: nc - 1 - c) if rev else (lambda c: c)
    rows = lambda w: pl.BlockSpec((Q, w), lambda bb, c: (bb * nc + ci(c), 0))
    dtt = pl.BlockSpec((LANES, Q), lambda bb, c: (0, bb * nc + ci(c)))
    const = lambda s: pl.BlockSpec(s, lambda bb, c: (0,) * len(s))
    state = pl.BlockSpec((None, N_PAIRS, LANES, SSD_STATE), lambda bb, c: (bb * nc + ci(c), 0, 0, 0))
    return rows, dtt, const, state


def ssd_fwd(xs, bm, cm, dtraw, dt_bias, a_log, b):
    t = xs.shape[0]
    nc = SEQ // Q
    rows, dtt_spec, const, state = _ssd_specs(b, nc, False)
    bias_r, bias_c = _pad_lanes(dt_bias)
    alog_r, alog_c = _pad_lanes(a_log)

    def body(xs_ref, b_ref, c_ref, dtr_ref, dtrt_ref, br, bc, ar, ac, y_ref, hp_ref, h_scr):
        @pl.when(pl.program_id(1) == 0)
        def _():
            h_scr[...] = jnp.zeros(h_scr.shape, F32)
        tril, lane, _, dt, cs, cst = _ssd_common(dtr_ref, dtrt_ref, br, bc, ar, ac)
        sub = lax.broadcasted_iota(jnp.int32, (LANES, 1), 0)
        y_acc = [jnp.zeros((Q, LANES), F32) for _ in range(N_PAIRS)]
        h_old = [h_scr[p] for p in range(N_PAIRS)]
        h_new = [jnp.zeros((LANES, SSD_STATE), F32) for _ in range(N_PAIRS)]
        for g in range(SSD_GROUPS):
            bg = b_ref[:, g * SSD_STATE:(g + 1) * SSD_STATE].astype(BF16)
            cg = c_ref[:, g * SSD_STATE:(g + 1) * SSD_STATE].astype(BF16)
            cb = _dot(cg, bg, _NT)
            heads = []
            for j in range(g * HEADS_PER_GROUP, (g + 1) * HEADS_PER_GROUP):
                p, side = j // 2, j % 2
                m = (lane < HEAD_DIM) if side == 0 else (lane >= HEAD_DIM)
                ms = (sub < HEAD_DIM) if side == 0 else (sub >= HEAD_DIM)
                csj, dtj = _rep(cs, j), _rep(dt, j)
                lmat = jnp.exp(jnp.where(tril, csj - cst[j:j + 1, :], NEG))
                xdt = jnp.where(m, xs_ref[:, p * LANES:(p + 1) * LANES] * dtj, 0.0)
                hm = jnp.where(ms, h_old[p], 0.0)
                last = csj[Q - 1:Q, :]
                heads.append(dict(p=p, hm=hm, ecs=jnp.exp(csj), el=jnp.exp(last), gmat=(cb * lmat).astype(BF16),
                                  xdt=xdt.astype(BF16), xd=(xdt * jnp.exp(last - csj)).astype(BF16)))
            for h in heads:
                h.update(ydiag=_dot(h['gmat'], h['xdt'], _NN), ch=_dot(cg, h['hm'].astype(BF16), _NT), sj=_dot(h['xd'], bg, _TN))
            for h in heads:
                y_acc[h['p']] = y_acc[h['p']] + h['ydiag'] + h['ecs'] * h['ch']
                h_new[h['p']] = h_new[h['p']] + h['el'] * h['hm'] + h['sj']
        for p in range(N_PAIRS):
            y_ref[:, p * LANES:(p + 1) * LANES] = y_acc[p]
            hp_ref[p] = h_old[p]
            h_scr[p] = h_new[p]

    return pl.pallas_call(
        body, name="ssd_fwd", grid=(b, nc),
        in_specs=[rows(D_SSD), rows(D_BC), rows(D_BC), rows(LANES), dtt_spec, const((1, LANES)), const((LANES, 1)),
                  const((1, LANES)), const((LANES, 1))],
        out_specs=[rows(D_SSD), state],
        out_shape=[jax.ShapeDtypeStruct((t, D_SSD), F32),
                   jax.ShapeDtypeStruct((b * nc, N_PAIRS, LANES, SSD_STATE), F32)],
        scratch_shapes=[pltpu.VMEM((N_PAIRS, LANES, SSD_STATE), F32)],
        compiler_params=_cparams(("parallel", "arbitrary")),
    )(xs, bm, cm, dtraw, dtraw.T, bias_r, bias_c, alog_r, alog_c)


def ssd_bwd(xs, bm, cm, dtraw, dt_bias, a_log, hprev, dy, b):
    t = xs.shape[0]
    nc = SEQ // Q
    rows, dtt_spec, const, state = _ssd_specs(b, nc, True)
    bias_r, bias_c = _pad_lanes(dt_bias)
    alog_r, alog_c = _pad_lanes(a_log)

    def body(xs_ref, b_ref, c_ref, dtr_ref, dtrt_ref, hp_ref, dy_ref, br, bc, ar, ac,
             dxs_ref, db_ref, dc_ref, ddt_ref, dbias_ref, dalog_ref, dh_scr):
        first = jnp.logical_and(pl.program_id(0) == 0, pl.program_id(1) == 0)

        @pl.when(pl.program_id(1) == 0)
        def _():
            dh_scr[...] = jnp.zeros(dh_scr.shape, F32)

        @pl.when(first)
        def _():
            dbias_ref[...] = jnp.zeros(dbias_ref.shape, F32)
            dalog_ref[...] = jnp.zeros(dalog_ref.shape, F32)
        tril, lane, a_r, dt, cs, cst = _ssd_common(dtr_ref, dtrt_ref, br, bc, ar, ac)
        sub = lax.broadcasted_iota(jnp.int32, (LANES, 1), 0)
        rowq = lax.broadcasted_iota(jnp.int32, (Q, 1), 0)
        triu = (lax.broadcasted_iota(jnp.int32, (Q, Q), 0) <= lax.broadcasted_iota(jnp.int32, (Q, Q), 1)).astype(F32)
        dxs_acc = [jnp.zeros((Q, LANES), F32) for _ in range(N_PAIRS)]
        dh_in = [dh_scr[p] for p in range(N_PAIRS)]
        h_in = [hp_ref[p] for p in range(N_PAIRS)]
        dh_out = [jnp.zeros((LANES, SSD_STATE), F32) for _ in range(N_PAIRS)]
        ddt = jnp.zeros((Q, LANES), F32)
        dalog = jnp.zeros((1, LANES), F32)
        for g in range(SSD_GROUPS):
            gs = slice(g * SSD_STATE, (g + 1) * SSD_STATE)
            bg, cg = b_ref[:, gs].astype(BF16), c_ref[:, gs].astype(BF16)
            cb = _dot(cg, bg, _NT)
            dcb = jnp.zeros((Q, Q), F32)
            dbg = jnp.zeros((Q, SSD_STATE), F32)
            dcg = jnp.zeros((Q, SSD_STATE), F32)
            heads = []
            for j in range(g * HEADS_PER_GROUP, (g + 1) * HEADS_PER_GROUP):
                p, side = j // 2, j % 2
                m = (lane < HEAD_DIM) if side == 0 else (lane >= HEAD_DIM)
                ms = (sub < HEAD_DIM) if side == 0 else (sub >= HEAD_DIM)
                csj, dtj = _rep(cs, j), _rep(dt, j)
                lmat = jnp.exp(jnp.where(tril, csj - cst[j:j + 1, :], NEG))
                x2 = jnp.where(m, xs_ref[:, p * LANES:(p + 1) * LANES], 0.0)
                xdt = x2 * dtj
                dym = jnp.where(m, dy_ref[:, p * LANES:(p + 1) * LANES], 0.0)
                hm = jnp.where(ms, h_in[p], 0.0)
                dhm = jnp.where(ms, dh_in[p], 0.0)
                last = csj[Q - 1:Q, :]
                decay = jnp.exp(last - csj)
                heads.append(dict(j=j, p=p, dtj=dtj, lmat=lmat, x2=x2, hm=hm, dhm=dhm, decay=decay, el=jnp.exp(last),
                                  gmat=cb * lmat, dym=dym.astype(BF16), xdt=xdt.astype(BF16), hmb=hm.astype(BF16),
                                  dhmb=dhm.astype(BF16), dye=dym * jnp.exp(csj), xd=xdt * decay))
            for h in heads:
                dyeb, xdb = h['dye'].astype(BF16), h['xd'].astype(BF16)
                h.update(dg=_dot(h['dym'], h['xdt'], _NT),
                         dxdt=_dot(h['gmat'].astype(BF16), h['dym'], _TN),
                         ch=_dot(cg, h['hmb'], _NT),
                         dcg=_dot(dyeb, h['hmb'], _NN), dhp=_dot(dyeb, cg, _TN),
                         wmat=_dot(bg, h['dhmb'], _NT),
                         dbg=_dot(xdb, h['dhmb'], _NN))
            for h in heads:
                ej = h['dg'] * h['gmat']
                col_sums = jnp.broadcast_to(jnp.sum(ej, axis=0, keepdims=True), (Q, Q)).T
                ddl = jnp.sum(h['xd'] * h['wmat'], axis=1, keepdims=True)
                dlast = jnp.sum(ddl, axis=0, keepdims=True) + h['el'] * jnp.sum(
                    jnp.sum(h['dhm'] * h['hm'], axis=1, keepdims=True), axis=0, keepdims=True)
                h['dcs'] = (jnp.sum(ej, axis=1, keepdims=True) - col_sums + jnp.sum(h['dye'] * h['ch'], axis=1, keepdims=True)
                            - ddl + jnp.where(rowq == Q - 1, dlast, 0.0))
                h['dxdt'] = h['dxdt'] + h['decay'] * h['wmat']
                dcb, dcg, dbg = dcb + h['dg'] * h['lmat'], dcg + h['dcg'], dbg + h['dbg']
                dh_out[h['p']] = dh_out[h['p']] + h['el'] * h['dhm'] + h['dhp']
            for h in heads:
                h['da'] = _dot_exact01(triu, h['dcs'], _NN, True)
            for h in heads:
                j, da = h['j'], h['da']
                aj = jnp.sum(jnp.where(lane == j, a_r, 0.0), axis=1, keepdims=True)
                ddtj = da * aj + jnp.sum(h['dxdt'] * h['x2'], axis=1, keepdims=True)
                ddt = ddt + jnp.where(lane == j, ddtj, 0.0)
                dalog = dalog + jnp.where(lane == j, jnp.sum(da * h['dtj'], axis=0, keepdims=True) * aj, 0.0)
                dxs_acc[h['p']] = dxs_acc[h['p']] + h['dxdt'] * h['dtj']
            dcbb = dcb.astype(BF16)
            dc_ref[:, gs] = dcg + _dot(dcbb, bg, _NN)
            db_ref[:, gs] = dbg + _dot(dcbb, cg, _TN)
        for p in range(N_PAIRS):
            dxs_ref[:, p * LANES:(p + 1) * LANES] = dxs_acc[p]
            dh_scr[p] = dh_out[p]
        ddtraw = ddt * _sigmoid(dtr_ref[...] + br[...])
        ddt_ref[...] = ddtraw
        dbias_ref[...] += jnp.sum(ddtraw, axis=0, keepdims=True)
        dalog_ref[...] += dalog

    return pl.pallas_call(
        body, name="ssd_bwd", grid=(b, nc),
        in_specs=[rows(D_SSD), rows(D_BC), rows(D_BC), rows(LANES), dtt_spec, state, rows(D_SSD), const((1, LANES)),
                  const((LANES, 1)), const((1, LANES)), const((LANES, 1))],
        out_specs=[rows(D_SSD), rows(D_BC), rows(D_BC), rows(LANES), const((1, LANES)), const((1, LANES))],
        out_shape=[jax.ShapeDtypeStruct((t, D_SSD), F32), jax.ShapeDtypeStruct((t, D_BC), F32),
                   jax.ShapeDtypeStruct((t, D_BC), F32), jax.ShapeDtypeStruct((t, LANES), F32),
                   jax.ShapeDtypeStruct((1, LANES), F32), jax.ShapeDtypeStruct((1, LANES), F32)],
        scratch_shapes=[pltpu.VMEM((N_PAIRS, LANES, SSD_STATE), F32)],
        compiler_params=_cparams(("arbitrary", "arbitrary")),
    )(xs, bm, cm, dtraw, dtraw.T, hprev, dy, bias_r, bias_c, alog_r, alog_c)


def _split_w_in(w_in):
    w_dt = jnp.pad(w_in[:, D_QKVZ + D_CONV:], ((0, 0), (0, LANES - N_HEADS)))
    return w_in[:, :D_QKVZ], w_in[:, D_QKVZ:D_QKVZ + D_CONV], w_dt


def mixer_fwd(hb, p, cosv, sinv, b):
    t = hb.shape[0]
    w_a, w_b, w_c = _split_w_in(p['w_in'])
    qkvz = mm("in_qkvz", [(hb, w_a, 'nn')], D_QKVZ)
    xbc = mm("in_xbc", [(hb, w_b, 'nn')], D_CONV)
    dtraw = mm("in_dt", [(hb, w_c, 'nn')], LANES)
    mixed, *lses = attn_fwd(qkvz, cosv, sinv, b)
    attn = attn_norm_fwd(mixed, p['attn_norm_w'])
    xs, bm, cm = conv_fwd(xbc, p['conv_w'], p['conv_b'])
    y, hprev = ssd_fwd(xs, bm, cm, dtraw, p['dt_bias'], p['a_log'], b)
    dskip = jnp.repeat(p['d_skip'].reshape(-1), HEAD_DIM).reshape(1, D_SSD)
    yg, = rowwise("ssd_gate", _gate, [y, xs, Op(qkvz, D_SSD, 3)], [dskip, p['ssd_norm_w']], [(t, D_SSD, BF16)])
    mix = mm("out_proj", [(attn, p['w_out'][:D_ATTN], 'nn'), (yg, p['w_out'][D_ATTN:], 'nn')], D_MODEL)
    res = dict(hb=hb, qkvz=qkvz, xbc=xbc, dtraw=dtraw, mixed=mixed, lses=lses, attn=attn, xs=xs, bm=bm, cm=cm,
               y=y, hprev=hprev, dskip=dskip, yg=yg, cosv=cosv, sinv=sinv)
    return mix, res


def mixer_bwd(r, p, dmix, dh_resid, b):
    t = dmix.shape[0]
    w_a, w_b, w_c = _split_w_in(p['w_in'])
    w_out = p['w_out']
    dattn = mm("out_bwd_dattn", [(dmix, w_out[:D_ATTN], 'nt')], D_ATTN)
    dyg = mm("out_bwd_dyg", [(dmix, w_out[D_ATTN:], 'nt')], D_SSD)
    dw_out = jnp.concatenate([mm_tn("out_bwd_dw_a", r['attn'], dmix, BF16),
                              mm_tn("out_bwd_dw_y", r['yg'], dmix, BF16)], axis=0)

    def gate_bwd(dy_, y_, xs_, z_, ds_, w_):
        _, vjp = jax.vjp(_gate, y_, xs_, z_, ds_, w_)
        return vjp(dy_)

    dy, dxs_a, dz, ddskip, dssd_norm = rowwise(
        "ssd_gate_bwd", gate_bwd, [dyg, r['y'], r['xs'], Op(r['qkvz'], D_SSD, 3)], [r['dskip'], p['ssd_norm_w']],
        [(t, D_SSD, F32), (t, D_SSD, F32), (t, D_SSD, BF16)], accs=[(1, D_SSD), (1, D_SSD)])
    dxs_b, dbm, dcm, ddtraw, ddt_bias, da_log = ssd_bwd(r['xs'], r['bm'], r['cm'], r['dtraw'], p['dt_bias'], p['a_log'],
                                                        r['hprev'], dy, b)
    dxbc, dconv_w, dconv_b = conv_bwd(r['xbc'], p['conv_w'], p['conv_b'], dxs_a, dxs_b, dbm, dcm)
    dmixed, dattn_norm = attn_norm_bwd(dattn, r['mixed'], p['attn_norm_w'])
    dq, dk, dv = attn_bwd(r['qkvz'], r['cosv'], r['sinv'], dmixed, r['mixed'], r['lses'], b)
    wq, wk, wv, wz = (w_a[:, i * D_ATTN:(i + 1) * D_ATTN] for i in range(4))
    dh = mm("in_bwd_dh", [(dq, wq, 'nt'), (dk, wk, 'nt'), (dv, wv, 'nt'), (dz, wz, 'nt'), (dxbc, w_b, 'nt'),
                          (ddtraw, w_c, 'nt')], D_MODEL, add=dh_resid, tn=512)
    h = r['hb']
    dw_in = jnp.concatenate([mm_tn("in_bwd_dwq", h, dq, BF16), mm_tn("in_bwd_dwk", h, dk, BF16),
                             mm_tn("in_bwd_dwv", h, dv, BF16), mm_tn("in_bwd_dwz", h, dz, BF16),
                             mm_tn("in_bwd_dwx", h, dxbc, BF16), mm_tn("in_bwd_dwdt", h, ddtraw, BF16)[:, :N_HEADS]], axis=1)
    head_sum = lambda v: v.reshape(N_HEADS, HEAD_DIM).sum(axis=1).reshape(1, N_HEADS)
    grads = dict(w_in=dw_in, w_out=dw_out, conv_w=dconv_w, conv_b=dconv_b, dt_bias=ddt_bias[:, :N_HEADS],
                 a_log=da_log[:, :N_HEADS], d_skip=head_sum(ddskip), attn_norm_w=dattn_norm, ssd_norm_w=dssd_norm)
    return dh, grads


FFN1_KEYS = ('ffn1_gate', 'ffn1_up', 'ffn1_down')
FFN2_KEYS = ('ffn2_gate', 'ffn2_up', 'ffn2_down')
MIXER_KEYS = ('w_in', 'conv_w', 'w_out')
FFN_COL = ('ffn1_gate', 'ffn1_up', 'ffn2_gate', 'ffn2_up')
FFN_ROW = ('ffn1_down', 'ffn2_down')
CONV_W_COMM = (8, 2 * LANES)
SMALL = 'small'


def comm_shape(k, shapes):
    if k in FFN_COL:
        return (D_MODEL, FF_PAD)
    if k in FFN_ROW:
        return (FF_PAD, D_MODEL)
    if k == 'conv_w':
        return CONV_W_COMM
    return tuple(shapes[k][1:])


def to_comm(k, vals, shapes):
    a = vals[k].reshape(shapes[k][1:])
    r_, c_ = comm_shape(k, shapes)
    return jnp.pad(a, ((0, r_ - a.shape[0]), (0, c_ - a.shape[1])))


SMALL_ROWS, SMALL_COLS = 16, D_CONV


def pack_small(small):
    rows = [jnp.pad(small[r].reshape(1, -1), ((0, 0), (0, SMALL_COLS - small[r].size))) for r in REPLICATED]
    return jnp.concatenate(rows + [jnp.zeros((SMALL_ROWS - len(rows), SMALL_COLS), F32)], axis=0)


def full_weight(k, g):
    if k in FFN_COL:
        return g
    if k == 'conv_w':
        return jnp.transpose(g[:, :CONV_WIDTH, :D_CONV // N_DEV], (1, 0, 2)).reshape(CONV_WIDTH, D_CONV)
    return g.reshape(N_DEV * g.shape[1], g.shape[2])


def grad_shards(k, g):
    if k in FFN_COL:
        return g
    if k == 'conv_w':
        s = jnp.transpose(g.reshape(CONV_WIDTH, N_DEV, D_CONV // N_DEV), (1, 0, 2))
        return jnp.pad(s, ((0, 0), (0, CONV_W_COMM[0] - CONV_WIDTH), (0, CONV_W_COMM[1] - D_CONV // N_DEV)))
    return g.reshape(N_DEV, g.shape[0] // N_DEV, g.shape[1])


def _flip(v, bit):
    return 1 - v if bit else v


N_PEER_COPIES = N_DEV - 1


def _comm_call(name, body, arrs, out_shape):
    n = len(arrs)
    return pl.pallas_call(
        functools.partial(body, n), name=name, out_shape=out_shape,
        in_specs=[pl.BlockSpec(memory_space=pl.ANY)] * n, out_specs=[pl.BlockSpec(memory_space=pl.ANY)] * n,
        scratch_shapes=[pltpu.SemaphoreType.DMA((n * N_PEER_COPIES,)), pltpu.SemaphoreType.DMA((n * N_PEER_COPIES,)),
                        pltpu.SemaphoreType.DMA((n,))],
    )(*arrs)


def _blk(ref, idx, by_cols):
    if not by_cols:
        return ref.at[idx]
    c = ref.shape[1] // N_DEV
    return ref.at[:, pl.ds(pl.multiple_of(idx * c, LANES), c)]


def _blocked_shape(a, by_cols):
    return (a.shape[0], N_DEV * a.shape[1]) if by_cols else (N_DEV,) + a.shape


def all_gather(arrs, by_cols):
    def body(n, *refs):
        x_refs, out_refs, (send_sems, recv_sems, local_sems) = refs[:n], refs[n:2 * n], refs[2 * n:]
        x, y, c = lax.axis_index("x"), lax.axis_index("y"), lax.axis_index("c")
        me, sibling = (x, y, c), (x, y, 1 - c)
        chips = [(1 - x, y), (x, 1 - y), (1 - x, 1 - y)]

        def copy(a, k, block, to, src=None):
            px, py, pc = block
            dst = _blk(out_refs[a], 4 * px + 2 * py + pc, by_cols[a])
            return pltpu.make_async_remote_copy(
                src_ref=dst if src is None else src, dst_ref=dst, send_sem=send_sems.at[a * N_PEER_COPIES + k],
                recv_sem=recv_sems.at[a * N_PEER_COPIES + k], device_id=to, device_id_type=MESH)

        mine = [pltpu.make_async_copy(x_refs[a], _blk(out_refs[a], 4 * x + 2 * y + c, by_cols[a]), local_sems.at[a])
                for a in range(n)]
        started = []
        for a in range(n):
            mine[a].start()
            first = [copy(a, 0, me, sibling, src=x_refs[a])]
            first += [copy(a, 1 + j, me, (*chip, c), src=x_refs[a]) for j, chip in enumerate(chips)]
            for cp in first:
                cp.start()
            started += first
        for j, chip in enumerate(chips):
            for a in range(n):
                copy(a, 1 + j, (*chip, c), me).wait_recv()
                cp = copy(a, 4 + j, (*chip, c), sibling)
                cp.start()
                started.append(cp)
        for a in range(n):
            copy(a, 0, sibling, me).wait_recv()
            for j, chip in enumerate(chips):
                copy(a, 4 + j, (*chip, 1 - c), me).wait_recv()
        for cp in started:
            cp.wait_send()
        for cp in mine:
            cp.wait()

    return _comm_call("all_gather_weights", body, arrs,
                      [jax.ShapeDtypeStruct(_blocked_shape(a, bc), a.dtype) for a, bc in zip(arrs, by_cols)])


def blocks_to_cols(arrs):
    def body(*refs):
        for i, o in zip(refs[:len(arrs)], refs[len(arrs):]):
            o[...] = i[...]

    return pl.pallas_call(
        body, name="blocks_to_cols", grid=(N_DEV,),
        in_specs=[pl.BlockSpec((None,) + a.shape[1:], lambda p: (p, 0, 0)) for a in arrs],
        out_specs=[pl.BlockSpec(a.shape[1:], lambda p: (0, p)) for a in arrs],
        out_shape=[jax.ShapeDtypeStruct((a.shape[1], N_DEV * a.shape[2]), a.dtype) for a in arrs],
        compiler_params=_cparams(("parallel",)),
    )(*arrs)


def _landing_shape(a, by_cols):
    return (N_DEV, a.shape[0], a.shape[1] // N_DEV) if by_cols else a.shape


def all_to_all(arrs, by_cols):
    def body(n, *refs):
        s_refs, r_refs, (send_sems, recv_sems, local_sems) = refs[:n], refs[n:2 * n], refs[2 * n:]
        x, y, c = lax.axis_index("x"), lax.axis_index("y"), lax.axis_index("c")
        me = 4 * x + 2 * y + c

        def peer(k):
            return _flip(x, k & 4), _flip(y, k & 2), _flip(c, k & 1)

        def copy(a, k, landing):
            px, py, pc = peer(k)
            p = 4 * px + 2 * py + pc
            src, dst = (me, p) if landing else (p, me)
            return pltpu.make_async_remote_copy(
                src_ref=_blk(s_refs[a], src, by_cols[a]), dst_ref=r_refs[a].at[dst],
                send_sem=send_sems.at[a * N_PEER_COPIES + k - 1],
                recv_sem=recv_sems.at[a * N_PEER_COPIES + k - 1], device_id=(px, py, pc), device_id_type=MESH)

        mine = [pltpu.make_async_copy(_blk(s_refs[a], me, by_cols[a]), r_refs[a].at[me], local_sems.at[a]) for a in range(n)]
        sends = [copy(a, k, False) for a in range(n) for k in range(1, N_DEV)]
        for cp in mine + sends:
            cp.start()
        for a in range(n):
            for k in range(1, N_DEV):
                copy(a, k, True).wait_recv()
        for cp in sends:
            cp.wait_send()
        for cp in mine:
            cp.wait()

    return _comm_call("all_to_all_grads", body, arrs,
                      [jax.ShapeDtypeStruct(_landing_shape(a, bc), a.dtype) for a, bc in zip(arrs, by_cols)])


_HBM = pl.BlockSpec(memory_space=pltpu.HBM)
_SEM = pl.BlockSpec(memory_space=pltpu.SEMAPHORE)
_EFFECT = pltpu.SideEffectType.DATAFLOW_SIDE_EFFECTING


def _peer(k):
    x, y, c = lax.axis_index("x"), lax.axis_index("y"), lax.axis_index("c")
    return _flip(x, k & 4), _flip(y, k & 2), _flip(c, k & 1)


def _my_index():
    return 4 * lax.axis_index("x") + 2 * lax.axis_index("y") + lax.axis_index("c")


def _split_copies(mode, by_cols, src_refs, land_refs, send_sems, recv_sems):
    me = _my_index()
    out = []
    for a, bc in enumerate(by_cols):
        for k in range(1, N_DEV):
            px, py, pc = _peer(k)
            src = _blk(src_refs[a], 4 * px + 2 * py + pc, bc) if mode == 'scatter' else src_refs[a]
            dst = land_refs[a].at[me] if mode == 'scatter' else _blk(land_refs[a], me, bc)
            out.append(pltpu.make_async_remote_copy(
                src_ref=src, dst_ref=dst, send_sem=send_sems.at[a * N_PEER_COPIES + k - 1],
                recv_sem=recv_sems.at[a * N_PEER_COPIES + k - 1], device_id=(px, py, pc), device_id_type=MESH))
    return out


def exchange_start(name, mode, srcs, by_cols):
    n = len(srcs)
    lands = [lax.empty(_landing_shape(s, bc) if mode == 'scatter' else _blocked_shape(s, bc), s.dtype)
             for s, bc in zip(srcs, by_cols)]

    def body(*refs):
        src_refs, land_refs, send_sems, recv_sems = refs[:n], refs[n:2 * n], refs[2 * n], refs[2 * n + 1]
        for cp in _split_copies(mode, by_cols, src_refs, land_refs, send_sems, recv_sems):
            cp.start()
        refs[-1][...] = jnp.zeros(refs[-1].shape, F32)

    sems = pltpu.SemaphoreType.DMA((n * N_PEER_COPIES,))
    res = pl.pallas_call(
        body, name=name,
        out_shape=(sems, sems, *[pltpu.HBM(a.shape, a.dtype) for a in srcs + lands], jax.ShapeDtypeStruct((8, LANES), F32)),
        in_specs=(_HBM,) * (2 * n), out_specs=(_SEM, _SEM, *(_HBM,) * (2 * n), pl.BlockSpec(memory_space=pltpu.VMEM)),
        input_output_aliases={i: 2 + i for i in range(2 * n)},
        compiler_params=pltpu.CompilerParams(has_side_effects=_EFFECT),
    )(*[pltpu.with_memory_space_constraint(a, pltpu.HBM) for a in srcs + lands])
    return (mode, by_cols, res[:-1]), res[-1]


def exchange_wait(name, handles, after):
    mode, by_cols, (send_sems, recv_sems, *bufs) = handles
    n = len(by_cols)

    def body(*refs):
        src_refs, land_refs, s_sems, r_sems = refs[:n], refs[n:2 * n], refs[2 * n], refs[2 * n + 1]
        for cp in _split_copies(mode, by_cols, src_refs, land_refs, s_sems, r_sems):
            cp.wait_send()
            cp.wait_recv()

    res = pl.pallas_call(
        body, name=name, out_shape=tuple(pltpu.HBM(a.shape, a.dtype) for a in bufs),
        in_specs=(*(_HBM,) * (2 * n), _SEM, _SEM, pl.BlockSpec(memory_space=pl.ANY)), out_specs=(_HBM,) * (2 * n),
        input_output_aliases={i: i for i in range(2 * n)},
        compiler_params=pltpu.CompilerParams(has_side_effects=_EFFECT),
    )(*bufs, send_sems, recv_sems, after)
    me, out = _my_index(), []
    for src, land, bc in zip(res[:n], res[n:], by_cols):
        if mode == 'scatter':
            c = land.shape[2]
            own = lax.dynamic_slice(src, (0, me * c), (src.shape[0], c)) if bc else lax.dynamic_index_in_dim(src, me, 0, False)
            out.append(lax.dynamic_update_slice(land, own[None], (me, 0, 0)))
        elif bc:
            out.append(lax.dynamic_update_slice(land, src, (0, me * src.shape[1])))
        else:
            out.append(lax.dynamic_update_slice(land, src[None], (me, 0, 0)))
    return out


def _adamw_math(g, w, m, v):
    c1 = 1.0 / (1.0 - ADAM_B1 ** ADAM_STEP)
    c2 = 1.0 / (1.0 - ADAM_B2 ** ADAM_STEP)
    m = ADAM_B1 * m + (1.0 - ADAM_B1) * g
    v = ADAM_B2 * v + (1.0 - ADAM_B2) * jnp.square(g)
    return g, -ADAM_LR * ((m * c1) / (jnp.sqrt(v * c2) + ADAM_EPS) + ADAM_WD * w), m, v


def adamw(name, recv, w, m, v, tm):
    rows, cols = w.shape
    tm = min(tm, rows)

    def body(*refs):
        g = refs[0][0:tm, 0:cols].astype(F32)
        for s in range(1, N_DEV):
            g = g + refs[s][0:tm, 0:cols].astype(F32)
        res = _adamw_math(g, *[r[...] for r in refs[N_DEV:N_DEV + 3]])
        for r, val in zip(refs[N_DEV + 3:], res):
            r[...] = val

    part = lambda s: pl.BlockSpec((None, recv.shape[1] if tm == rows else tm, recv.shape[2]), lambda i: (s, i, 0))
    tile = pl.BlockSpec((tm, cols), lambda i: (i, 0))
    return pl.pallas_call(
        body, name=name, grid=(rows // tm,), in_specs=[part(s) for s in range(N_DEV)] + [tile] * 3, out_specs=[tile] * 4,
        out_shape=[jax.ShapeDtypeStruct((rows, cols), F32)] * 4, compiler_params=_cparams(("parallel",)),
    )(*[recv] * N_DEV, w, m, v)


def adamw_small(recv, wl, ml, vl):
    n = len(REPLICATED)

    def body(recv_ref, *refs):
        g = recv_ref[0]
        for s in range(1, N_DEV):
            g = g + recv_ref[s]
        for r in range(n):
            w, m, v = (refs[j * n + r][...] for j in range(3))
            for j, val in enumerate(_adamw_math(g[r:r + 1, :w.shape[1]], w, m, v)):
                refs[(3 + j) * n + r][...] = val

    arrs = [d[k].reshape(1, -1) for d in (wl, ml, vl) for k in REPLICATED]
    res = pl.pallas_call(
        body, name="adamw_small", out_shape=[jax.ShapeDtypeStruct(a.shape, F32) for a in arrs[:n]] * 4,
    )(recv, *arrs)
    return [{k: res[j * n + r].reshape(wl[k].shape) for r, k in enumerate(REPLICATED)} for j in range(4)]


ADAMW_TM = {'ffn1_gate': 256, 'ffn1_up': 256, 'ffn2_gate': 256, 'ffn2_up': 256, 'w_in': 32}


def kernel(x, positions, ln1_g, ln1_b, ffn1_gate, ffn1_up, ffn1_down, w_in, conv_w, conv_b, dt_bias, a_log, d_skip, attn_norm_w, ssd_norm_w, w_out, ln2_g, ln2_b, ffn2_gate, ffn2_up, ffn2_down, ln3_g, ln3_b, loss_target, m_ln1_g, m_ln1_b, m_ffn1_gate, m_ffn1_up, m_ffn1_down, m_w_in, m_conv_w, m_conv_b, m_dt_bias, m_a_log, m_d_skip, m_attn_norm_w, m_ssd_norm_w, m_w_out, m_ln2_g, m_ln2_b, m_ffn2_gate, m_ffn2_up, m_ffn2_down, m_ln3_g, m_ln3_b, v_ln1_g, v_ln1_b, v_ffn1_gate, v_ffn1_up, v_ffn1_down, v_w_in, v_conv_w, v_conv_b, v_dt_bias, v_a_log, v_d_skip, v_attn_norm_w, v_ssd_norm_w, v_w_out, v_ln2_g, v_ln2_b, v_ffn2_gate, v_ffn2_up, v_ffn2_down, v_ln3_g, v_ln3_b):
    args = dict(locals())
    wl = {k: args[k] for k in WEIGHTS}
    ml = {k: args["m_" + k] for k in WEIGHTS}
    vl = {k: args["v_" + k] for k in WEIGHTS}
    shapes = {k: wl[k].shape for k in WEIGHTS}
    b, s, dm = x.shape
    t = b * s

    sent = {k: to_comm(k, wl, shapes).astype(F32 if k == 'conv_w' else BF16) for k in SHARDED}
    by_cols = lambda keys: [k in FFN_COL for k in keys]
    gate, up = all_gather([sent['ffn1_gate'], sent['ffn1_up']], [False] * 2)
    (gate, up), sent = lax.optimization_barrier(((gate, up), sent))
    p = dict(zip(('ffn1_gate', 'ffn1_up'), blocks_to_cols([gate, up])))
    gather_down, token_d = exchange_start("gather_ffn1_down_start", 'gather', [sent['ffn1_down']], [False])
    sent['w_in'] = sent['w_in'] + token_d[0, 0].astype(BF16)
    gather_mixer, token_m = exchange_start("gather_mixer_start", 'gather', [sent[k] for k in MIXER_KEYS], by_cols(MIXER_KEYS))
    sent['ffn2_gate'] = sent['ffn2_gate'] + token_m[0, 0].astype(BF16)
    gather_ffn2, token_f = exchange_start("gather_ffn2_start", 'gather', [sent[k] for k in FFN2_KEYS], by_cols(FFN2_KEYS))
    for k in REPLICATED:
        p[k] = wl[k].reshape(1, -1)

    x2 = x.reshape(t, dm)
    cosv, sinv = rope_tables(positions)
    g1, u1, a1, at1 = ffn_gate_up("ffn1_gate_up", x2, p['ffn1_gate'], p['ffn1_up'], after=(token_d, token_m, token_f))
    p['ffn1_down'] = full_weight('ffn1_down', exchange_wait("gather_ffn1_down_wait", gather_down, a1)[0])
    f1, res1 = mm("ffn1_down", [(a1, p['ffn1_down'], 'nn')], D_MODEL), (x2, g1, u1, at1)
    h1, h1b = resid_ln_fwd("ln1", 0.5, x2, f1, p['ln1_g'], p['ln1_b'])
    for k, g in zip(MIXER_KEYS, exchange_wait("gather_mixer_wait", gather_mixer, h1b)):
        p[k] = full_weight(k, g)
    mix, resm = mixer_fwd(h1b, p, cosv, sinv, b)
    h2, h2b = resid_ln_fwd("ln2", 1.0, h1, mix, p['ln2_g'], p['ln2_b'])
    for k, g in zip(FFN2_KEYS, exchange_wait("gather_ffn2_wait", gather_ffn2, h2b)):
        p[k] = full_weight(k, g)
    f2, res3 = ffn_fwd("ffn2", h2b, p['ffn2_gate'], p['ffn2_up'], p['ffn2_down'])

    small, full = {}, {}
    dh2_res, df2, small['ln3_g'], small['ln3_b'], sq = ln_loss_bwd("ln3_loss_bwd", h2, f2, loss_target.reshape(t, dm),
                                                                   p['ln3_g'], p['ln3_b'])
    loss = lax.psum(jnp.sum(sq) * (0.5 / dm), AXES)

    dh2, full['ffn2_gate'], full['ffn2_up'], full['ffn2_down'] = ffn_bwd("ffn2", res3, p['ffn2_gate'], p['ffn2_up'],
                                                                       p['ffn2_down'], df2, dh2_res)
    ffn2_exchange, token = exchange_start("grads_ffn2_start", 'scatter', [grad_shards(k, full[k]) for k in FFN2_KEYS],
                                          by_cols(FFN2_KEYS))
    dh1_res, dmix, small['ln2_g'], small['ln2_b'] = resid_ln_bwd("ln2_bwd", 1.0, h1, mix, p['ln2_g'] + token[:1, :1],
                                                                 p['ln2_b'], dh2)
    dh1, gm = mixer_bwd(resm, p, dmix, dh1_res, b)
    for k in ('conv_b', 'dt_bias', 'a_log', 'd_skip', 'attn_norm_w', 'ssd_norm_w'):
        small[k] = gm[k]
    mixer_exchange, token = exchange_start("grads_mixer_start", 'scatter', [grad_shards(k, gm[k]) for k in MIXER_KEYS],
                                           by_cols(MIXER_KEYS))
    dx_res, df1, small['ln1_g'], small['ln1_b'] = resid_ln_bwd("ln1_bwd", 0.5, x2, f1, p['ln1_g'] + token[:1, :1],
                                                               p['ln1_b'], dh1)
    hb, g, u, at = res1
    small_part = pack_small(small)
    dg, du = ffn_da_act("ffn1_bwd_da_act", df1, p['ffn1_down'], g, u)
    dwd = mm_acc("ffn1_bwd_dwd", at, df1, BF16, after=dg)
    down_exchange, token = exchange_start("grads_ffn1_down_start", 'scatter', [
        grad_shards('ffn1_down', dwd), jnp.broadcast_to(small_part[None], (N_DEV,) + small_part.shape)], [False, False])
    dx = mm("ffn1_bwd_dh", [(dg, p['ffn1_gate'], 'nt'), (du, p['ffn1_up'], 'nt')], D_MODEL, add=dx_res, tn=512, after=token)
    dwg = mm_tn("ffn1_bwd_dwg", hb, dg, BF16, after=dx)
    gate_exchange, token = exchange_start("grads_ffn1_gate_start", 'scatter', [grad_shards('ffn1_gate', dwg)], [True])
    dwu = mm_tn("ffn1_bwd_dwu", hb, du, BF16, after=token)
    recv = dict(zip(('ffn1_up',), all_to_all([grad_shards('ffn1_up', dwu)], [True])))
    for keys, name, ex in (((FFN2_KEYS), "grads_ffn2_wait", ffn2_exchange), (MIXER_KEYS, "grads_mixer_wait", mixer_exchange),
                           (('ffn1_down', SMALL), "grads_ffn1_down_wait", down_exchange),
                           (('ffn1_gate',), "grads_ffn1_gate_wait", gate_exchange)):
        recv.update(zip(keys, exchange_wait(name, ex, recv['ffn1_up'])))
    outs = adamw_small(recv.pop(SMALL), wl, ml, vl)
    for k, r in recv.items():
        shard = shapes[k][1:]
        res = adamw(f"adamw_{k}", r, *[d[k].reshape(shard) for d in (wl, ml, vl)], ADAMW_TM.get(k, shard[0]))
        for o, a in zip(outs, res):
            o[k] = a.reshape(shapes[k])
    return (loss, dx.reshape(b, s, dm), *[o[k] for o in outs for k in WEIGHTS])
```

```python
import functools
import math

import jax
import jax.numpy as jnp
import numpy as np
from jax import lax
from jax.experimental import pallas as pl
from jax.experimental.pallas import tpu as pltpu

F32, BF16 = jnp.float32, jnp.bfloat16
HI = lax.Precision.HIGHEST
MESH = pl.DeviceIdType.MESH
AXES = ("x", "y", "c")
N_DEV = 8

D_MODEL = 1024
SEQ = 2048
HEAD_DIM = 64
N_HEADS = 12
D_ATTN = N_HEADS * HEAD_DIM
DILATIONS = (1, 4, 16)
ATTN_BLOCK = 128
ROPE_THETA = 500000.0
ROPE_DIM = 16
D_SSD = 768
SSD_GROUPS = 4
SSD_STATE = 128
SSD_CHUNK = 128
D_BC = SSD_GROUPS * SSD_STATE
D_CONV = D_SSD + 2 * D_BC
CONV_WIDTH = 4
D_QKVZ = 3 * D_ATTN + D_SSD
D_IN_PROJ = D_QKVZ + D_CONV + N_HEADS
D_FF = 2816
ALPHA = 2.0 ** 0.25
LN_EPS = 1e-5
RMS_EPS = 1e-6
ADAM_LR, ADAM_B1, ADAM_B2, ADAM_EPS, ADAM_WD, ADAM_STEP = 0.001, 0.9, 0.999, 1e-08, 0.01, 10

LANES = 128
VMEM_LIMIT = 52 * 1024 * 1024
NEG = -1e30

WEIGHTS = ['ln1_g', 'ln1_b', 'ffn1_gate', 'ffn1_up', 'ffn1_down', 'w_in', 'conv_w', 'conv_b', 'dt_bias', 'a_log',
           'd_skip', 'attn_norm_w', 'ssd_norm_w', 'w_out', 'ln2_g', 'ln2_b', 'ffn2_gate', 'ffn2_up', 'ffn2_down',
           'ln3_g', 'ln3_b']
COL_SHARDED = ('ffn1_gate', 'ffn1_up', 'conv_w', 'ffn2_gate', 'ffn2_up')
ROW_SHARDED = ('ffn1_down', 'w_in', 'w_out', 'ffn2_down')
SHARDED = tuple(n for n in WEIGHTS if n in COL_SHARDED or n in ROW_SHARDED)
REPLICATED = tuple(n for n in WEIGHTS if n not in SHARDED)
FF_SHARD = D_FF // N_DEV
FF_PAD = -(-FF_SHARD // LANES) * LANES
D_FF_INT = N_DEV * FF_PAD


def _cparams(sem=None):
    return pltpu.CompilerParams(dimension_semantics=sem, vmem_limit_bytes=VMEM_LIMIT)


def _tile(n, prefs):
    for p in prefs:
        if n % p == 0:
            return p
    return n


class Op:
    def __init__(self, arr, bw=None, cb=0, ro=0):
        self.arr, self.bw, self.cb, self.ro = arr, (arr.shape[1] if bw is None else bw), cb, ro


def _op(a):
    return a if isinstance(a, Op) else Op(a)


def rowwise(name, fn, ins, consts, outs, accs=(), tm=256):
    ins = [_op(a) for a in ins]
    rows = outs[0][0]
    n_in, n_c, n_o, n_a = len(ins), len(consts), len(outs), len(accs)
    tm = min(tm, rows)
    assert rows % tm == 0, (name, rows, tm)

    def body(*refs):
        vals = [r[...].astype(F32) for r in refs[:n_in + n_c]]
        res = fn(*vals)
        res = res if isinstance(res, (tuple, list)) else (res,)
        o_refs = refs[n_in + n_c:n_in + n_c + n_o]
        a_refs = refs[n_in + n_c + n_o:]
        for r, v in zip(o_refs, res[:n_o]):
            r[...] = v.astype(r.dtype)
        if n_a:
            @pl.when(pl.program_id(0) == 0)
            def _():
                for r in a_refs:
                    r[...] = jnp.zeros(r.shape, r.dtype)
            for r, v in zip(a_refs, res[n_o:]):
                r[...] += v

    in_specs = [pl.BlockSpec((tm, o.bw), functools.partial(lambda i, o: (i + o.ro, o.cb), o=o)) for o in ins]
    in_specs += [pl.BlockSpec(c.shape, functools.partial(lambda i, nd: (0,) * nd, nd=c.ndim)) for c in consts]
    out_specs = [pl.BlockSpec((tm, w), lambda i: (i, 0)) for (_, w, _) in outs]
    out_specs += [pl.BlockSpec(s, functools.partial(lambda i, nd: (0,) * nd, nd=len(s))) for s in accs]
    out_shape = [jax.ShapeDtypeStruct((r, w), dt) for (r, w, dt) in outs]
    out_shape += [jax.ShapeDtypeStruct(s, F32) for s in accs]
    res = pl.pallas_call(
        body, name=name, grid=(rows // tm,), in_specs=in_specs, out_specs=out_specs, out_shape=out_shape,
        compiler_params=_cparams(("arbitrary",) if n_a else ("parallel",)),
    )(*[o.arr for o in ins], *consts)
    return res


MM_TM = 512
MM_TN = (1024, 896, 768, 512, 256, 128)
_NT = (((1,), (1,)), ((), ()))
_NN = (((1,), (0,)), ((), ()))
_TN = (((0,), (0,)), ((), ()))


def _dot(a, b, dn, precision=None):
    return lax.dot_general(a, b, dn, preferred_element_type=F32, precision=precision)


def _mm_specs(name, pairs, n_out, tm, tn):
    in_specs, args = [], []
    for a, b, mode in pairs:
        o = _op(a)
        in_specs.append(pl.BlockSpec((tm, o.bw), functools.partial(lambda j, i, o: (i, o.cb), o=o)))
        args.append(o.arr)
        if mode == 'nn':
            assert b.shape == (o.bw, n_out), (name, b.shape, o.bw, n_out)
            in_specs.append(pl.BlockSpec((o.bw, tn), lambda j, i: (0, j)))
        else:
            assert b.shape == (n_out, o.bw), (name, b.shape, o.bw, n_out)
            in_specs.append(pl.BlockSpec((tn, o.bw), lambda j, i: (j, 0)))
        args.append(b)
    return in_specs, args


def _mm_acc(refs, pairs):
    acc = None
    for k, (_, _, mode) in enumerate(pairs):
        d = _dot(refs[2 * k][...].astype(BF16), refs[2 * k + 1][...].astype(BF16), _NN if mode == 'nn' else _NT)
        acc = d if acc is None else acc + d
    return acc


def mm(name, pairs, n_out, add=None, out_dtype=F32, tm=MM_TM, tn=None, after=None):
    m = _op(pairs[0][0]).arr.shape[0]
    tn = tn or _tile(n_out, MM_TN)
    n_p = len(pairs)

    def body(*refs):
        acc = _mm_acc(refs, pairs)
        if add is not None:
            acc = acc + refs[2 * n_p][...]
        refs[-1][...] = acc.astype(refs[-1].dtype)

    in_specs, args = _mm_specs(name, pairs, n_out, tm, tn)
    tile = pl.BlockSpec((tm, tn), lambda j, i: (i, j))
    if add is not None:
        in_specs.append(tile)
        args.append(add)
    if after is not None:
        in_specs.append(pl.BlockSpec(memory_space=pl.ANY))
        args.append(after)
    return pl.pallas_call(
        body, name=name, grid=(n_out // tn, m // tm), in_specs=in_specs, out_specs=tile,
        out_shape=jax.ShapeDtypeStruct((m, n_out), out_dtype),
        compiler_params=_cparams(("parallel", "parallel")),
    )(*args)


def mm_tn(name, a, b, out_dtype=F32, tt=1024, after=None):
    a, b = _op(a), _op(b)
    t = a.arr.shape[0]
    k, n = a.bw, b.bw
    tk = _tile(k, (512, 896, 768, 256, 128))
    tn = _tile(n, (3072, 1792) + MM_TN)
    tt = min(tt, t)
    n_t = t // tt
    order = [] if after is None else [after]

    def body(a_ref, b_ref, *rest):
        o_ref, acc_ref = rest[-2:]
        s = pl.program_id(2)
        d = _dot(a_ref[...].astype(BF16), b_ref[...].astype(BF16), _TN)

        @pl.when(s == 0)
        def _():
            acc_ref[...] = d

        @pl.when(s > 0)
        def _():
            acc_ref[...] += d

        @pl.when(s == n_t - 1)
        def _():
            o_ref[...] = acc_ref[...].astype(o_ref.dtype)

    return pl.pallas_call(
        body, name=name, grid=(k // tk, n // tn, n_t),
        in_specs=[pl.BlockSpec((tt, tk), functools.partial(lambda kk, nn, s, o: (s, o.cb * (o.bw // tk) + kk), o=a)),
                  pl.BlockSpec((tt, tn), functools.partial(lambda kk, nn, s, o: (s, o.cb * (o.bw // tn) + nn), o=b))]
        + [pl.BlockSpec(memory_space=pl.ANY) for _ in order],
        out_specs=pl.BlockSpec((tk, tn), lambda kk, nn, s: (kk, nn)),
        out_shape=jax.ShapeDtypeStruct((k, n), out_dtype),
        scratch_shapes=[pltpu.VMEM((tk, tn), F32)],
        compiler_params=_cparams(("parallel", "parallel", "arbitrary")),
    )(a.arr, b.arr, *order)


def _sigmoid(x):
    return 1.0 / (1.0 + jnp.exp(-x))


def _silu(x):
    return x * _sigmoid(x)


def _softplus(x):
    return jnp.maximum(x, 0.0) + jnp.log(1.0 + jnp.exp(-jnp.abs(x)))


def _act(g, u):
    return _silu(g) * u


def _resid_ln(scale, h, branch, g, b):
    r = ALPHA * h + scale * branch
    mu = jnp.mean(r, axis=-1, keepdims=True)
    var = jnp.mean(jnp.square(r - mu), axis=-1, keepdims=True)
    return (r - mu) * lax.rsqrt(var + LN_EPS) * g + b


def _rms(t, w):
    return t * lax.rsqrt(jnp.mean(t * t, axis=-1, keepdims=True) + RMS_EPS) * w


def _branch_weights(l1, l2, l3):
    m = jnp.maximum(jnp.maximum(l1, l2), l3)
    e1, e2, e3 = jnp.exp(l1 - m), jnp.exp(l2 - m), jnp.exp(l3 - m)
    inv = 1.0 / (e1 + e2 + e3)
    return e1 * inv, e2 * inv, e3 * inv


def _gate(y, xs, z, dskip, w):
    return _rms((y + dskip * xs) * _silu(z), w)


def _rot(x):
    d = lax.broadcasted_iota(jnp.int32, x.shape, 1) % HEAD_DIM
    up = pltpu.roll(x, x.shape[1] - ROPE_DIM // 2, 1)
    down = jnp.where(d < ROPE_DIM, pltpu.roll(x, ROPE_DIM // 2, 1), 0.0)
    return jnp.where(d < ROPE_DIM // 2, up, down)


def ffn_gate_up(name, h, wg, wu, after=()):
    m, nf = h.shape[0], wg.shape[1]
    tn = _tile(nf, MM_TN)

    def body(h_ref, g_w, u_w, *rest):
        g_ref, u_ref, a_ref, at_ref = rest[-4:]
        hb = h_ref[...].astype(BF16)
        g = _dot(hb, g_w[...].astype(BF16), _NN)
        u = _dot(hb, u_w[...].astype(BF16), _NN)
        g_ref[...] = g.astype(g_ref.dtype)
        u_ref[...] = u.astype(u_ref.dtype)
        a = _act(g, u)
        a_ref[...] = a.astype(a_ref.dtype)
        at_ref[...] = a.T.astype(at_ref.dtype)

    in_specs, args = _mm_specs(name, [(h, wg, 'nn')], nf, MM_TM, tn)
    in_specs.append(in_specs[1])
    in_specs += [pl.BlockSpec(memory_space=pl.ANY) for _ in after]
    tile = pl.BlockSpec((MM_TM, tn), lambda j, i: (i, j))
    return pl.pallas_call(
        body, name=name, grid=(nf // tn, m // MM_TM), in_specs=in_specs,
        out_specs=[tile] * 3 + [pl.BlockSpec((tn, MM_TM), lambda j, i: (j, i))],
        out_shape=[jax.ShapeDtypeStruct((m, nf), BF16)] * 3 + [jax.ShapeDtypeStruct((nf, m), BF16)],
        compiler_params=_cparams(("parallel", "parallel")),
    )(*args, wu, *after)


def mm_tn_cat(name, a, bs, out_dtype=F32, tt=1024):
    t, k = a.shape
    widths = [b.shape[1] for b in bs]
    n, tk, tt = sum(widths), _tile(k, (512, 256, 128)), min(tt, t)
    n_t = t // tt

    def body(a_ref, *rest):
        b_refs, o_ref, acc_ref = rest[:len(bs)], rest[-2], rest[-1]
        s = pl.program_id(1)
        at = a_ref[...].astype(BF16)
        d = jnp.concatenate([_dot(at, b[...].astype(BF16), _TN) for b in b_refs], axis=1)

        @pl.when(s == 0)
        def _():
            acc_ref[...] = d

        @pl.when(s > 0)
        def _():
            acc_ref[...] += d

        @pl.when(s == n_t - 1)
        def _():
            o_ref[...] = acc_ref[...].astype(o_ref.dtype)

    return pl.pallas_call(
        body, name=name, grid=(k // tk, n_t),
        in_specs=[pl.BlockSpec((tt, tk), lambda kk, s: (s, kk))] + [pl.BlockSpec((tt, w), lambda kk, s: (s, 0)) for w in widths],
        out_specs=pl.BlockSpec((tk, n), lambda kk, s: (kk, 0)),
        out_shape=jax.ShapeDtypeStruct((k, n), out_dtype), scratch_shapes=[pltpu.VMEM((tk, n), F32)],
        compiler_params=_cparams(("parallel", "arbitrary")),
    )(a, *bs)


def mm_acc(name, a, b, out_dtype=F32, tt=1024, after=None):
    k, t = a.shape
    n = b.shape[1]
    tk, tn, tt = _tile(k, (1024, 512, 256, 128)), _tile(n, MM_TN), min(tt, t)
    n_t = t // tt
    order = [] if after is None else [after]

    def body(a_ref, b_ref, *rest):
        o_ref, acc_ref = rest[-2:]
        s = pl.program_id(2)
        d = _dot(a_ref[...].astype(BF16), b_ref[...].astype(BF16), _NN)

        @pl.when(s == 0)
        def _():
            acc_ref[...] = d

        @pl.when(s > 0)
        def _():
            acc_ref[...] += d

        @pl.when(s == n_t - 1)
        def _():
            o_ref[...] = acc_ref[...].astype(o_ref.dtype)

    return pl.pallas_call(
        body, name=name, grid=(k // tk, n // tn, n_t),
        in_specs=[pl.BlockSpec((tk, tt), lambda kk, nn, s: (kk, s)), pl.BlockSpec((tt, tn), lambda kk, nn, s: (s, nn))]
        + [pl.BlockSpec(memory_space=pl.ANY) for _ in order],
        out_specs=pl.BlockSpec((tk, tn), lambda kk, nn, s: (kk, nn)),
        out_shape=jax.ShapeDtypeStruct((k, n), out_dtype), scratch_shapes=[pltpu.VMEM((tk, tn), F32)],
        compiler_params=_cparams(("parallel", "parallel", "arbitrary")),
    )(a, b, *order)


def ffn_da_act(name, df, wd, g, u):
    m, nf = df.shape[0], wd.shape[0]
    tn = _tile(nf, MM_TN)

    def body(df_ref, w_ref, g_ref, u_ref, dg_ref, du_ref):
        da = _dot(df_ref[...].astype(BF16), w_ref[...].astype(BF16), _NT)
        g, u = g_ref[...].astype(F32), u_ref[...].astype(F32)
        sig = _sigmoid(g)
        gs = g * sig
        dg_ref[...] = (da * u * (sig + gs * (1.0 - sig))).astype(dg_ref.dtype)
        du_ref[...] = (da * gs).astype(du_ref.dtype)

    in_specs, args = _mm_specs(name, [(df, wd, 'nt')], nf, MM_TM, tn)
    tile = pl.BlockSpec((MM_TM, tn), lambda j, i: (i, j))
    return pl.pallas_call(
        body, name=name, grid=(nf // tn, m // MM_TM), in_specs=in_specs + [tile, tile], out_specs=[tile] * 2,
        out_shape=[jax.ShapeDtypeStruct((m, nf), BF16)] * 2, compiler_params=_cparams(("parallel", "parallel")),
    )(*args, g, u)


def resid_ln_fwd(name, scale, h, branch, ln_g, ln_b):
    t = h.shape[0]

    def fn(*a):
        y = _resid_ln(scale, *a)
        return y, y

    return rowwise(name, fn, [h, branch], [ln_g, ln_b], [(t, D_MODEL, F32), (t, D_MODEL, BF16)], tm=512)


def ffn_fwd(tag, hb, wg, wu, wd, after=()):
    g, u, a, at = ffn_gate_up(f"{tag}_gate_up", hb, wg, wu, after)
    f = mm(f"{tag}_down", [(a, wd, 'nn')], D_MODEL)
    return f, (hb, g, u, at)


def ln_loss_bwd(name, h, branch, target, ln_g, ln_b):
    t, dm = h.shape

    def fn(h_, br_, tgt, g_, b_):
        y, vjp = jax.vjp(functools.partial(_resid_ln, 0.5), h_, br_, g_, b_)
        e = y - tgt
        return (*vjp(e * (1.0 / dm)), jnp.sum(e * e, axis=0, keepdims=True))

    return rowwise(name, fn, [h, branch, target], [ln_g, ln_b], [(t, dm, F32), (t, dm, BF16)],
                   accs=[(1, dm), (1, dm), (1, dm)], tm=512)


def resid_ln_bwd(name, scale, h, branch, ln_g, ln_b, dout, extra=None):
    t = h.shape[0]

    def fn(h_, br_, do_, *rest):
        g_, b_ = rest[-2], rest[-1]
        _, vjp = jax.vjp(functools.partial(_resid_ln, scale), h_, br_, g_, b_)
        dh, dbr, dg, db = vjp(do_)
        if extra is not None:
            dh = dh + rest[0]
        return dh, dbr, dg, db

    ins = [h, branch, dout] + ([extra] if extra is not None else [])
    return rowwise(name, fn, ins, [ln_g, ln_b], [(t, D_MODEL, F32), (t, D_MODEL, BF16)],
                   accs=[(1, D_MODEL), (1, D_MODEL)], tm=512)


def ffn_bwd(tag, res, wg, wu, wd, df, dh_resid):
    hb, g, u, at = res
    dg, du = ffn_da_act(f"{tag}_bwd_da_act", df, wd, g, u)
    dwd = mm_acc(f"{tag}_bwd_dwd", at, df, BF16)
    dh = mm(f"{tag}_bwd_dh", [(dg, wg, 'nt'), (du, wu, 'nt')], D_MODEL, add=dh_resid, tn=512)
    dwg = mm_tn(f"{tag}_bwd_dwg", hb, dg, BF16)
    dwu = mm_tn(f"{tag}_bwd_dwu", hb, du, BF16)
    return dh, dwg, dwu, dwd


def rope_tables(positions):
    inv_freq = ROPE_THETA ** (-jnp.arange(0, ROPE_DIM, 2, dtype=F32) / ROPE_DIM)
    ang = positions.reshape(-1, 1).astype(F32) * inv_freq
    c, s = jnp.cos(ang), jnp.sin(ang)
    t = ang.shape[0]
    cosv = jnp.concatenate([c, c, jnp.ones((t, HEAD_DIM - ROPE_DIM), F32)], axis=1)
    sinv = jnp.concatenate([-s, s, jnp.zeros((t, HEAD_DIM - ROPE_DIM), F32)], axis=1)
    return jnp.tile(cosv, (1, 2)), jnp.tile(sinv, (1, 2))


def _pair_masks():
    lane = lax.broadcasted_iota(jnp.int32, (1, LANES), 1)
    return (lane < HEAD_DIM, lane >= HEAD_DIM)


def _band_masks():
    row = lax.broadcasted_iota(jnp.int32, (ATTN_BLOCK, ATTN_BLOCK), 0)
    col = lax.broadcasted_iota(jnp.int32, (ATTN_BLOCK, ATTN_BLOCK), 1)
    return col >= row, col <= row


def _residue_blocks():
    out = []
    for g, d in enumerate(DILATIONS):
        for r in range(d):
            for i in range(SEQ // d // ATTN_BLOCK):
                rows = lambda j: pl.ds(r + j * ATTN_BLOCK * d, ATTN_BLOCK, stride=d) if d > 1 else pl.ds(j * ATTN_BLOCK, ATTN_BLOCK)
                out.append((g, rows(i), rows(i - 1) if i > 0 else None))
    return out


N_HEAD_PAIRS = D_ATTN // LANES
SCALE = HEAD_DIM ** -0.5
ATTN_GROUP = 4


def _block_operands(qr, kr, v_ref, cur, prev):
    prev_ok, cur_ok = _band_masks()
    if prev is None:
        return qr[cur, :], kr[cur, :].astype(BF16), v_ref[cur, :], cur_ok
    kcat = jnp.concatenate([kr[prev, :], kr[cur, :]], axis=0).astype(BF16)
    vcat = jnp.concatenate([v_ref[prev, :], v_ref[cur, :]], axis=0)
    return qr[cur, :], kcat, vcat, jnp.concatenate([prev_ok, cur_ok], axis=1)


def _attn_specs(b):
    col = lambda cb: pl.BlockSpec((SEQ, LANES), lambda bb, hp: (bb, cb + hp))
    tab = pl.BlockSpec((SEQ, LANES), lambda bb, hp: (bb, 0))
    return col, tab


def attn_fwd(qkvz, cosv, sinv, b):
    t = qkvz.shape[0]
    col, tab = _attn_specs(b)
    blocks = _residue_blocks()

    def body(q_ref, k_ref, v_ref, c_ref, s_ref, o_ref, l1_ref, l2_ref, l3_ref, qr, kr, o1, o2, o3):
        l_refs, o_scr = (l1_ref, l2_ref, l3_ref), (o1, o2, o3)
        c, s = c_ref[...], s_ref[...]
        q, k = q_ref[...], k_ref[...]
        qr[...] = q * c + _rot(q) * s
        kr[...] = k * c + _rot(k) * s
        masks = _pair_masks()
        for lo in range(0, len(blocks), ATTN_GROUP):
            chains = []
            for g, cur, prev in blocks[lo:lo + ATTN_GROUP]:
                q2, kcat, vcat, ok = _block_operands(qr, kr, v_ref, cur, prev)
                for m in masks:
                    qm = jnp.where(m, q2, 0.0).astype(BF16)
                    chains.append(dict(g=g, cur=cur, m=m, v=jnp.where(m, vcat, 0.0).astype(BF16),
                                       s=jnp.where(ok, _dot(qm, kcat, _NT) * SCALE, NEG)))
            for ch in chains:
                mx = jnp.max(ch['s'], axis=1, keepdims=True)
                p = jnp.exp(ch['s'] - mx)
                den = jnp.sum(p, axis=1, keepdims=True)
                ch.update(p=p.astype(BF16), inv=1.0 / den, lse=mx + jnp.log(den))
            for ch in chains:
                ch['o'] = _dot(ch['p'], ch['v'], _NN) * ch['inv']
            for c0, c1 in zip(chains[0::2], chains[1::2]):
                o_scr[c0['g']][c0['cur'], :] = c0['o'] + c1['o']
                l_refs[c0['g']][c0['cur'], :] = jnp.where(c0['m'], c0['lse'], c1['lse'])
        w1, w2, w3 = _branch_weights(l1_ref[...], l2_ref[...], l3_ref[...])
        o_ref[...] = w1 * o1[...] + w2 * o2[...] + w3 * o3[...]

    shp = jax.ShapeDtypeStruct((t, D_ATTN), F32)
    return pl.pallas_call(
        body, name="attn_fwd", grid=(b, N_HEAD_PAIRS),
        in_specs=[col(0), col(N_HEAD_PAIRS), col(2 * N_HEAD_PAIRS), tab, tab],
        out_specs=[col(0)] * 4, out_shape=[shp] * 4,
        scratch_shapes=[pltpu.VMEM((SEQ, LANES), F32)] * 5,
        compiler_params=_cparams(("parallel", "parallel")),
    )(qkvz, qkvz, qkvz, cosv, sinv)


def attn_bwd(qkvz, cosv, sinv, dmix, mixed, lses, b):
    t = qkvz.shape[0]
    col, tab = _attn_specs(b)
    blocks = _residue_blocks()
    hd = np.arange(LANES) // HEAD_DIM
    head_ones = jnp.asarray((hd[:, None] == hd[None, :]).astype(np.float32))

    def body(q_ref, k_ref, v_ref, c_ref, s_ref, dm_ref, mx_ref, l1_ref, l2_ref, l3_ref, ones_ref,
             dq_out, dk_out, dv_out, qr, kr, do1, do2, do3, dd1, dd2, dd3, dq_ref, dk_ref, dv_ref):
        l_refs, do_scr, dd_scr = (l1_ref, l2_ref, l3_ref), (do1, do2, do3), (dd1, dd2, dd3)
        c, s = c_ref[...], s_ref[...]
        q, k = q_ref[...], k_ref[...]
        qr[...] = q * c + _rot(q) * s
        kr[...] = k * c + _rot(k) * s
        dm = dm_ref[...]
        tot = _dot(dm * mx_ref[...], ones_ref[...], _NN, HI)
        for w, do_g, dd_g in zip(_branch_weights(l1_ref[...], l2_ref[...], l3_ref[...]), do_scr, dd_scr):
            do_g[...] = w * dm
            dd_g[...] = w * tot
        dq_ref[...] = jnp.zeros((SEQ, LANES), F32)
        dk_ref[...] = jnp.zeros((SEQ, LANES), F32)
        dv_ref[...] = jnp.zeros((SEQ, LANES), F32)
        masks = _pair_masks()
        for lo in range(0, len(blocks), ATTN_GROUP):
            chains = []
            for g, cur, prev in blocks[lo:lo + ATTN_GROUP]:
                q2, kcat, vcat, ok = _block_operands(qr, kr, v_ref, cur, prev)
                vcat = vcat.astype(BF16)
                do2_, l2, dd2_ = do_scr[g][cur, :], l_refs[g][cur, :], dd_scr[g][cur, :]
                l2s, dd2s = pltpu.roll(l2, HEAD_DIM, 1), pltpu.roll(dd2_, HEAD_DIM, 1)
                for m in masks:
                    qm = jnp.where(m, q2, 0.0).astype(BF16)
                    dom = jnp.where(m, do2_, 0.0).astype(BF16)
                    lrep, ddrep = jnp.where(m, l2, l2s), jnp.where(m, dd2_, dd2s)
                    if prev is not None:
                        lrep, ddrep = jnp.concatenate([lrep, lrep], axis=1), jnp.concatenate([ddrep, ddrep], axis=1)
                    chains.append(dict(cur=cur, prev=prev, qm=qm, dom=dom, km=jnp.where(m, kcat, 0), lrep=lrep, ddrep=ddrep,
                                       s=jnp.where(ok, _dot(qm, kcat, _NT) * SCALE, NEG), dp=_dot(dom, vcat, _NT)))
            for ch in chains:
                p = jnp.exp(ch['s'] - ch['lrep'])
                ch.update(p=p.astype(BF16), ds=(p * (ch['dp'] - ch['ddrep']) * SCALE).astype(BF16))
            for ch in chains:
                ch.update(dq=_dot(ch['ds'], ch['km'], _NN), dk=_dot(ch['ds'], ch['qm'], _TN), dv=_dot(ch['p'], ch['dom'], _TN))
            for c0, c1 in zip(chains[0::2], chains[1::2]):
                cur, prev = c0['cur'], c0['prev']
                dk, dv = c0['dk'] + c1['dk'], c0['dv'] + c1['dv']
                dq_ref[cur, :] += c0['dq'] + c1['dq']
                if prev is None:
                    dk_ref[cur, :] += dk
                    dv_ref[cur, :] += dv
                else:
                    dk_ref[prev, :] += dk[:ATTN_BLOCK]
                    dv_ref[prev, :] += dv[:ATTN_BLOCK]
                    dk_ref[cur, :] += dk[ATTN_BLOCK:]
                    dv_ref[cur, :] += dv[ATTN_BLOCK:]
        dq, dk = dq_ref[...], dk_ref[...]
        dq_out[...] = (dq * c + _rot(dq * s)).astype(dq_out.dtype)
        dk_out[...] = (dk * c + _rot(dk * s)).astype(dk_out.dtype)
        dv_out[...] = dv_ref[...].astype(dv_out.dtype)

    shp = jax.ShapeDtypeStruct((t, D_ATTN), BF16)
    return pl.pallas_call(
        body, name="attn_bwd", grid=(b, N_HEAD_PAIRS),
        in_specs=[col(0), col(N_HEAD_PAIRS), col(2 * N_HEAD_PAIRS), tab, tab, col(0), col(0), col(0), col(0), col(0),
                  pl.BlockSpec((LANES, LANES), lambda bb, hp: (0, 0))],
        out_specs=[col(0)] * 3, out_shape=[shp] * 3,
        scratch_shapes=[pltpu.VMEM((SEQ, LANES), F32)] * 11,
        compiler_params=_cparams(("parallel", "parallel")),
    )(qkvz, qkvz, qkvz, cosv, sinv, dmix, mixed, *lses, head_ones)


def attn_norm_fwd(mixed, norm_w):
    return rowwise("attn_norm", _rms, [mixed], [norm_w], [(mixed.shape[0], D_ATTN, BF16)])[0]


def attn_norm_bwd(dout, mixed, norm_w):
    def fn(dy, mx, w):
        _, vjp = jax.vjp(_rms, mx, w)
        return vjp(dy)

    return rowwise("attn_norm_bwd", fn, [dout, mixed], [norm_w], [(dout.shape[0], D_ATTN, F32)], accs=[(1, D_ATTN)])


CONV_TM = 256
HALO = 8


def _conv_columns(refs):
    xs_ref, bm_ref, cm_ref = refs
    out = []
    for c in range(D_CONV // LANES):
        lo = c * LANES
        ref, base = (xs_ref, 0) if lo < D_SSD else (bm_ref, D_SSD) if lo < D_SSD + D_BC else (cm_ref, D_SSD + D_BC)
        out.append((slice(lo, lo + LANES), (ref, slice(lo - base, lo - base + LANES))))
    return out


def _conv_taps(scr, w_ref, cs, first_row, step, tm):
    acc = None
    for k in range(CONV_WIDTH):
        term = w_ref[k:k + 1, cs] * scr[pl.ds(first_row + step * k, tm), cs]
        acc = term if acc is None else acc + term
    return acc


def conv_fwd(u, w, bias):
    t = u.shape[0]
    tm, per_seq = CONV_TM, SEQ // CONV_TM

    def body(u_ref, h_ref, w_ref, b_ref, xs_ref, bm_ref, cm_ref, scr):
        first = pl.program_id(0) % per_seq == 0
        scr[0:HALO, :] = jnp.where(first, 0.0, h_ref[...])
        scr[HALO:, :] = u_ref[...]
        for cs, (o_ref, os_) in _conv_columns((xs_ref, bm_ref, cm_ref)):
            o_ref[:, os_] = _silu(_conv_taps(scr, w_ref, cs, HALO - CONV_WIDTH + 1, 1, tm) + b_ref[:, cs])

    return pl.pallas_call(
        body, name="conv_fwd", grid=(t // tm,),
        in_specs=[pl.BlockSpec((tm, D_CONV), lambda i: (i, 0)),
                  pl.BlockSpec((HALO, D_CONV), lambda i: (jnp.maximum(i * (tm // HALO) - 1, 0), 0)),
                  pl.BlockSpec((CONV_WIDTH, D_CONV), lambda i: (0, 0)), pl.BlockSpec((1, D_CONV), lambda i: (0, 0))],
        out_specs=[pl.BlockSpec((tm, D_SSD), lambda i: (i, 0)), pl.BlockSpec((tm, D_BC), lambda i: (i, 0)),
                   pl.BlockSpec((tm, D_BC), lambda i: (i, 0))],
        out_shape=[jax.ShapeDtypeStruct((t, D_SSD), F32), jax.ShapeDtypeStruct((t, D_BC), F32),
                   jax.ShapeDtypeStruct((t, D_BC), F32)],
        scratch_shapes=[pltpu.VMEM((tm + HALO, D_CONV), F32)],
        compiler_params=_cparams(("parallel",)),
    )(u, u, w, bias)


def conv_bwd(u, w, bias, dxs_a, dxs_b, dbm, dcm):
    t = u.shape[0]
    tm, per_seq = CONV_TM, SEQ // CONV_TM
    n_tiles = t // tm

    def body1(u_ref, h_ref, dxs_ref, dxs2_ref, dbm_ref, dcm_ref, w_ref, b_ref, dz_ref, dw_ref, db_ref, scr):
        i = pl.program_id(0)
        first = i % per_seq == 0
        scr[0:HALO, :] = jnp.where(first, 0.0, h_ref[...])
        scr[HALO:, :] = u_ref[...]

        @pl.when(i == 0)
        def _():
            dw_ref[...] = jnp.zeros(dw_ref.shape, F32)
            db_ref[...] = jnp.zeros(db_ref.shape, F32)
        for cs, (g_ref, gs) in _conv_columns((dxs_ref, dbm_ref, dcm_ref)):
            acc = _conv_taps(scr, w_ref, cs, HALO - CONV_WIDTH + 1, 1, tm) + b_ref[:, cs]
            sig = _sigmoid(acc)
            dy = g_ref[:, gs] + dxs2_ref[:, gs] if g_ref is dxs_ref else g_ref[:, gs]
            dz = dy * sig * (1.0 + acc * (1.0 - sig))
            dz_ref[:, cs] = dz
            db_ref[:, cs] += jnp.sum(dz, axis=0, keepdims=True)
            for k in range(CONV_WIDTH):
                dw_ref[k:k + 1, cs] += jnp.sum(dz * scr[pl.ds(HALO - CONV_WIDTH + 1 + k, tm), cs], axis=0, keepdims=True)

    dz, dw, db = pl.pallas_call(
        body1, name="conv_bwd_dz", grid=(n_tiles,),
        in_specs=[pl.BlockSpec((tm, D_CONV), lambda i: (i, 0)),
                  pl.BlockSpec((HALO, D_CONV), lambda i: (jnp.maximum(i * (tm // HALO) - 1, 0), 0)),
                  pl.BlockSpec((tm, D_SSD), lambda i: (i, 0)), pl.BlockSpec((tm, D_SSD), lambda i: (i, 0)),
                  pl.BlockSpec((tm, D_BC), lambda i: (i, 0)), pl.BlockSpec((tm, D_BC), lambda i: (i, 0)),
                  pl.BlockSpec((CONV_WIDTH, D_CONV), lambda i: (0, 0)), pl.BlockSpec((1, D_CONV), lambda i: (0, 0))],
        out_specs=[pl.BlockSpec((tm, D_CONV), lambda i: (i, 0)), pl.BlockSpec((CONV_WIDTH, D_CONV), lambda i: (0, 0)),
                   pl.BlockSpec((1, D_CONV), lambda i: (0, 0))],
        out_shape=[jax.ShapeDtypeStruct((t, D_CONV), F32), jax.ShapeDtypeStruct((CONV_WIDTH, D_CONV), F32),
                   jax.ShapeDtypeStruct((1, D_CONV), F32)],
        scratch_shapes=[pltpu.VMEM((tm + HALO, D_CONV), F32)],
        compiler_params=_cparams(("arbitrary",)),
    )(u, u, dxs_a, dxs_b, dbm, dcm, w, bias)

    def body2(dz_ref, n_ref, w_ref, du_ref, scr):
        last = pl.program_id(0) % per_seq == per_seq - 1
        scr[0:tm, :] = dz_ref[...]
        scr[tm:, :] = jnp.where(last, 0.0, n_ref[...])
        for c in range(D_CONV // LANES):
            cs = slice(c * LANES, (c + 1) * LANES)
            du_ref[:, cs] = _conv_taps(scr, w_ref, cs, CONV_WIDTH - 1, -1, tm).astype(du_ref.dtype)

    du = pl.pallas_call(
        body2, name="conv_bwd_du", grid=(n_tiles,),
        in_specs=[pl.BlockSpec((tm, D_CONV), lambda i: (i, 0)),
                  pl.BlockSpec((HALO, D_CONV), lambda i: (jnp.minimum((i + 1) * (tm // HALO), t // HALO - 1), 0)),
                  pl.BlockSpec((CONV_WIDTH, D_CONV), lambda i: (0, 0))],
        out_specs=pl.BlockSpec((tm, D_CONV), lambda i: (i, 0)),
        out_shape=jax.ShapeDtypeStruct((t, D_CONV), BF16),
        scratch_shapes=[pltpu.VMEM((tm + HALO, D_CONV), F32)],
        compiler_params=_cparams(("parallel",)),
    )(dz, dz, w)
    return du, dw, db


Q = SSD_CHUNK
N_PAIRS = D_SSD // LANES
HEADS_PER_GROUP = N_HEADS // SSD_GROUPS


def _rep(a, j):
    return jnp.broadcast_to(a[:, j:j + 1], a.shape)


def _dot_exact01(a, b, dn, a_is_01):
    x = b if a_is_01 else a
    hi = x.astype(BF16)
    mid = (x - hi.astype(F32)).astype(BF16)
    lo = (x - hi.astype(F32) - mid.astype(F32)).astype(BF16)
    z = a.astype(BF16) if a_is_01 else b.astype(BF16)
    out = None
    for term in (hi, mid, lo):
        d = _dot(z, term, dn) if a_is_01 else _dot(term, z, dn)
        out = d if out is None else out + d
    return out


def _pad_lanes(v, fill=0.0):
    row = jnp.pad(v.reshape(1, -1).astype(F32), ((0, 0), (0, LANES - v.size)), constant_values=fill)
    return row, row.reshape(LANES, 1)


def _ssd_common(dtr_ref, dtrt_ref, bias_r, bias_c, alog_r, alog_c):
    row = lax.broadcasted_iota(jnp.int32, (Q, Q), 0)
    col = lax.broadcasted_iota(jnp.int32, (Q, Q), 1)
    tril = row >= col
    lane = lax.broadcasted_iota(jnp.int32, (1, LANES), 1)
    a_r = jnp.where(lane < N_HEADS, -jnp.exp(alog_r[...]), 0.0)
    sub = lax.broadcasted_iota(jnp.int32, (LANES, 1), 0)
    a_c = jnp.where(sub < N_HEADS, -jnp.exp(alog_c[...]), 0.0)
    dt = _softplus(dtr_ref[...] + bias_r[...])
    cs = _dot_exact01(tril, dt * a_r, _NN, True)
    dtt = _softplus(dtrt_ref[...] + bias_c[...])
    cst = _dot_exact01(dtt * a_c, row <= col, _NN, False)
    return tril, lane, a_r, dt, cs, cst


def _ssd_specs(b, nc, rev):
    ci = (lambda c: nc - 1 - c) if rev else (lambda c: c)
    rows = lambda w: pl.BlockSpec((Q, w), lambda bb, c: (bb * nc + ci(c), 0))
    dtt = pl.BlockSpec((LANES, Q), lambda bb, c: (0, bb * nc + ci(c)))
    const = lambda s: pl.BlockSpec(s, lambda bb, c: (0,) * len(s))
    state = pl.BlockSpec((None, N_PAIRS, LANES, SSD_STATE), lambda bb, c: (bb * nc + ci(c), 0, 0, 0))
    return rows, dtt, const, state


def ssd_fwd(xs, bm, cm, dtraw, dt_bias, a_log, b):
    t = xs.shape[0]
    nc = SEQ // Q
    rows, dtt_spec, const, state = _ssd_specs(b, nc, False)
    bias_r, bias_c = _pad_lanes(dt_bias)
    alog_r, alog_c = _pad_lanes(a_log)

    def body(xs_ref, b_ref, c_ref, dtr_ref, dtrt_ref, br, bc, ar, ac, y_ref, hp_ref, h_scr):
        @pl.when(pl.program_id(1) == 0)
        def _():
            h_scr[...] = jnp.zeros(h_scr.shape, F32)
        tril, lane, _, dt, cs, cst = _ssd_common(dtr_ref, dtrt_ref, br, bc, ar, ac)
        sub = lax.broadcasted_iota(jnp.int32, (LANES, 1), 0)
        y_acc = [jnp.zeros((Q, LANES), F32) for _ in range(N_PAIRS)]
        h_old = [h_scr[p] for p in range(N_PAIRS)]
        h_new = [jnp.zeros((LANES, SSD_STATE), F32) for _ in range(N_PAIRS)]
        for g in range(SSD_GROUPS):
            bg = b_ref[:, g * SSD_STATE:(g + 1) * SSD_STATE].astype(BF16)
            cg = c_ref[:, g * SSD_STATE:(g + 1) * SSD_STATE].astype(BF16)
            cb = _dot(cg, bg, _NT)
            heads = []
            for j in range(g * HEADS_PER_GROUP, (g + 1) * HEADS_PER_GROUP):
                p, side = j // 2, j % 2
                m = (lane < HEAD_DIM) if side == 0 else (lane >= HEAD_DIM)
                ms = (sub < HEAD_DIM) if side == 0 else (sub >= HEAD_DIM)
                csj, dtj = _rep(cs, j), _rep(dt, j)
                lmat = jnp.exp(jnp.where(tril, csj - cst[j:j + 1, :], NEG))
                xdt = jnp.where(m, xs_ref[:, p * LANES:(p + 1) * LANES] * dtj, 0.0)
                hm = jnp.where(ms, h_old[p], 0.0)
                last = csj[Q - 1:Q, :]
                heads.append(dict(p=p, hm=hm, ecs=jnp.exp(csj), el=jnp.exp(last), gmat=(cb * lmat).astype(BF16),
                                  xdt=xdt.astype(BF16), xd=(xdt * jnp.exp(last - csj)).astype(BF16)))
            for h in heads:
                h.update(ydiag=_dot(h['gmat'], h['xdt'], _NN), ch=_dot(cg, h['hm'].astype(BF16), _NT), sj=_dot(h['xd'], bg, _TN))
            for h in heads:
                y_acc[h['p']] = y_acc[h['p']] + h['ydiag'] + h['ecs'] * h['ch']
                h_new[h['p']] = h_new[h['p']] + h['el'] * h['hm'] + h['sj']
        for p in range(N_PAIRS):
            y_ref[:, p * LANES:(p + 1) * LANES] = y_acc[p]
            hp_ref[p] = h_old[p]
            h_scr[p] = h_new[p]

    return pl.pallas_call(
        body, name="ssd_fwd", grid=(b, nc),
        in_specs=[rows(D_SSD), rows(D_BC), rows(D_BC), rows(LANES), dtt_spec, const((1, LANES)), const((LANES, 1)),
                  const((1, LANES)), const((LANES, 1))],
        out_specs=[rows(D_SSD), state],
        out_shape=[jax.ShapeDtypeStruct((t, D_SSD), F32),
                   jax.ShapeDtypeStruct((b * nc, N_PAIRS, LANES, SSD_STATE), F32)],
        scratch_shapes=[pltpu.VMEM((N_PAIRS, LANES, SSD_STATE), F32)],
        compiler_params=_cparams(("parallel", "arbitrary")),
    )(xs, bm, cm, dtraw, dtraw.T, bias_r, bias_c, alog_r, alog_c)


def ssd_bwd(xs, bm, cm, dtraw, dt_bias, a_log, hprev, dy, b):
    t = xs.shape[0]
    nc = SEQ // Q
    rows, dtt_spec, const, state = _ssd_specs(b, nc, True)
    bias_r, bias_c = _pad_lanes(dt_bias)
    alog_r, alog_c = _pad_lanes(a_log)

    def body(xs_ref, b_ref, c_ref, dtr_ref, dtrt_ref, hp_ref, dy_ref, br, bc, ar, ac,
             dxs_ref, db_ref, dc_ref, ddt_ref, dbias_ref, dalog_ref, dh_scr):
        first = jnp.logical_and(pl.program_id(0) == 0, pl.program_id(1) == 0)

        @pl.when(pl.program_id(1) == 0)
        def _():
            dh_scr[...] = jnp.zeros(dh_scr.shape, F32)

        @pl.when(first)
        def _():
            dbias_ref[...] = jnp.zeros(dbias_ref.shape, F32)
            dalog_ref[...] = jnp.zeros(dalog_ref.shape, F32)
        tril, lane, a_r, dt, cs, cst = _ssd_common(dtr_ref, dtrt_ref, br, bc, ar, ac)
        sub = lax.broadcasted_iota(jnp.int32, (LANES, 1), 0)
        rowq = lax.broadcasted_iota(jnp.int32, (Q, 1), 0)
        triu = (lax.broadcasted_iota(jnp.int32, (Q, Q), 0) <= lax.broadcasted_iota(jnp.int32, (Q, Q), 1)).astype(F32)
        dxs_acc = [jnp.zeros((Q, LANES), F32) for _ in range(N_PAIRS)]
        dh_in = [dh_scr[p] for p in range(N_PAIRS)]
        h_in = [hp_ref[p] for p in range(N_PAIRS)]
        dh_out = [jnp.zeros((LANES, SSD_STATE), F32) for _ in range(N_PAIRS)]
        ddt = jnp.zeros((Q, LANES), F32)
        dalog = jnp.zeros((1, LANES), F32)
        for g in range(SSD_GROUPS):
            gs = slice(g * SSD_STATE, (g + 1) * SSD_STATE)
            bg, cg = b_ref[:, gs].astype(BF16), c_ref[:, gs].astype(BF16)
            cb = _dot(cg, bg, _NT)
            dcb = jnp.zeros((Q, Q), F32)
            dbg = jnp.zeros((Q, SSD_STATE), F32)
            dcg = jnp.zeros((Q, SSD_STATE), F32)
            heads = []
            for j in range(g * HEADS_PER_GROUP, (g + 1) * HEADS_PER_GROUP):
                p, side = j // 2, j % 2
                m = (lane < HEAD_DIM) if side == 0 else (lane >= HEAD_DIM)
                ms = (sub < HEAD_DIM) if side == 0 else (sub >= HEAD_DIM)
                csj, dtj = _rep(cs, j), _rep(dt, j)
                lmat = jnp.exp(jnp.where(tril, csj - cst[j:j + 1, :], NEG))
                x2 = jnp.where(m, xs_ref[:, p * LANES:(p + 1) * LANES], 0.0)
                xdt = x2 * dtj
                dym = jnp.where(m, dy_ref[:, p * LANES:(p + 1) * LANES], 0.0)
                hm = jnp.where(ms, h_in[p], 0.0)
                dhm = jnp.where(ms, dh_in[p], 0.0)
                last = csj[Q - 1:Q, :]
                decay = jnp.exp(last - csj)
                heads.append(dict(j=j, p=p, dtj=dtj, lmat=lmat, x2=x2, hm=hm, dhm=dhm, decay=decay, el=jnp.exp(last),
                                  gmat=cb * lmat, dym=dym.astype(BF16), xdt=xdt.astype(BF16), hmb=hm.astype(BF16),
                                  dhmb=dhm.astype(BF16), dye=dym * jnp.exp(csj), xd=xdt * decay))
            for h in heads:
                dyeb, xdb = h['dye'].astype(BF16), h['xd'].astype(BF16)
                h.update(dg=_dot(h['dym'], h['xdt'], _NT),
                         dxdt=_dot(h['gmat'].astype(BF16), h['dym'], _TN),
                         ch=_dot(cg, h['hmb'], _NT),
                         dcg=_dot(dyeb, h['hmb'], _NN), dhp=_dot(dyeb, cg, _TN),
                         wmat=_dot(bg, h['dhmb'], _NT),
                         dbg=_dot(xdb, h['dhmb'], _NN))
            for h in heads:
                ej = h['dg'] * h['gmat']
                col_sums = jnp.broadcast_to(jnp.sum(ej, axis=0, keepdims=True), (Q, Q)).T
                ddl = jnp.sum(h['xd'] * h['wmat'], axis=1, keepdims=True)
                dlast = jnp.sum(ddl, axis=0, keepdims=True) + h['el'] * jnp.sum(
                    jnp.sum(h['dhm'] * h['hm'], axis=1, keepdims=True), axis=0, keepdims=True)
                h['dcs'] = (jnp.sum(ej, axis=1, keepdims=True) - col_sums + jnp.sum(h['dye'] * h['ch'], axis=1, keepdims=True)
                            - ddl + jnp.where(rowq == Q - 1, dlast, 0.0))
                h['dxdt'] = h['dxdt'] + h['decay'] * h['wmat']
                dcb, dcg, dbg = dcb + h['dg'] * h['lmat'], dcg + h['dcg'], dbg + h['dbg']
                dh_out[h['p']] = dh_out[h['p']] + h['el'] * h['dhm'] + h['dhp']
            for h in heads:
                h['da'] = _dot_exact01(triu, h['dcs'], _NN, True)
            for h in heads:
                j, da = h['j'], h['da']
                aj = jnp.sum(jnp.where(lane == j, a_r, 0.0), axis=1, keepdims=True)
                ddtj = da * aj + jnp.sum(h['dxdt'] * h['x2'], axis=1, keepdims=True)
                ddt = ddt + jnp.where(lane == j, ddtj, 0.0)
                dalog = dalog + jnp.where(lane == j, jnp.sum(da * h['dtj'], axis=0, keepdims=True) * aj, 0.0)
                dxs_acc[h['p']] = dxs_acc[h['p']] + h['dxdt'] * h['dtj']
            dcbb = dcb.astype(BF16)
            dc_ref[:, gs] = dcg + _dot(dcbb, bg, _NN)
            db_ref[:, gs] = dbg + _dot(dcbb, cg, _TN)
        for p in range(N_PAIRS):
            dxs_ref[:, p * LANES:(p + 1) * LANES] = dxs_acc[p]
            dh_scr[p] = dh_out[p]
        ddtraw = ddt * _sigmoid(dtr_ref[...] + br[...])
        ddt_ref[...] = ddtraw
        dbias_ref[...] += jnp.sum(ddtraw, axis=0, keepdims=True)
        dalog_ref[...] += dalog

    return pl.pallas_call(
        body, name="ssd_bwd", grid=(b, nc),
        in_specs=[rows(D_SSD), rows(D_BC), rows(D_BC), rows(LANES), dtt_spec, state, rows(D_SSD), const((1, LANES)),
                  const((LANES, 1)), const((1, LANES)), const((LANES, 1))],
        out_specs=[rows(D_SSD), rows(D_BC), rows(D_BC), rows(LANES), const((1, LANES)), const((1, LANES))],
        out_shape=[jax.ShapeDtypeStruct((t, D_SSD), F32), jax.ShapeDtypeStruct((t, D_BC), F32),
                   jax.ShapeDtypeStruct((t, D_BC), F32), jax.ShapeDtypeStruct((t, LANES), F32),
                   jax.ShapeDtypeStruct((1, LANES), F32), jax.ShapeDtypeStruct((1, LANES), F32)],
        scratch_shapes=[pltpu.VMEM((N_PAIRS, LANES, SSD_STATE), F32)],
        compiler_params=_cparams(("arbitrary", "arbitrary")),
    )(xs, bm, cm, dtraw, dtraw.T, hprev, dy, bias_r, bias_c, alog_r, alog_c)


def _split_w_in(w_in):
    w_dt = jnp.pad(w_in[:, D_QKVZ + D_CONV:], ((0, 0), (0, LANES - N_HEADS)))
    return w_in[:, :D_QKVZ], w_in[:, D_QKVZ:D_QKVZ + D_CONV], w_dt


def mixer_fwd(hb, p, cosv, sinv, b):
    t = hb.shape[0]
    w_a, w_b, w_c = _split_w_in(p['w_in'])
    qkvz = mm("in_qkvz", [(hb, w_a, 'nn')], D_QKVZ)
    xbc = mm("in_xbc", [(hb, w_b, 'nn')], D_CONV)
    dtraw = mm("in_dt", [(hb, w_c, 'nn')], LANES)
    mixed, *lses = attn_fwd(qkvz, cosv, sinv, b)
    attn = attn_norm_fwd(mixed, p['attn_norm_w'])
    xs, bm, cm = conv_fwd(xbc, p['conv_w'], p['conv_b'])
    y, hprev = ssd_fwd(xs, bm, cm, dtraw, p['dt_bias'], p['a_log'], b)
    dskip = jnp.repeat(p['d_skip'].reshape(-1), HEAD_DIM).reshape(1, D_SSD)
    yg, = rowwise("ssd_gate", _gate, [y, xs, Op(qkvz, D_SSD, 3)], [dskip, p['ssd_norm_w']], [(t, D_SSD, BF16)])
    mix = mm("out_proj", [(attn, p['w_out'][:D_ATTN], 'nn'), (yg, p['w_out'][D_ATTN:], 'nn')], D_MODEL)
    res = dict(hb=hb, qkvz=qkvz, xbc=xbc, dtraw=dtraw, mixed=mixed, lses=lses, attn=attn, xs=xs, bm=bm, cm=cm,
               y=y, hprev=hprev, dskip=dskip, yg=yg, cosv=cosv, sinv=sinv)
    return mix, res


def mixer_bwd(r, p, dmix, dh_resid, b):
    t = dmix.shape[0]
    w_a, w_b, w_c = _split_w_in(p['w_in'])
    w_out = p['w_out']
    dattn = mm("out_bwd_dattn", [(dmix, w_out[:D_ATTN], 'nt')], D_ATTN)
    dyg = mm("out_bwd_dyg", [(dmix, w_out[D_ATTN:], 'nt')], D_SSD)
    dw_out = jnp.concatenate([mm_tn("out_bwd_dw_a", r['attn'], dmix, BF16),
                              mm_tn("out_bwd_dw_y", r['yg'], dmix, BF16)], axis=0)

    def gate_bwd(dy_, y_, xs_, z_, ds_, w_):
        _, vjp = jax.vjp(_gate, y_, xs_, z_, ds_, w_)
        return vjp(dy_)

    dy, dxs_a, dz, ddskip, dssd_norm = rowwise(
        "ssd_gate_bwd", gate_bwd, [dyg, r['y'], r['xs'], Op(r['qkvz'], D_SSD, 3)], [r['dskip'], p['ssd_norm_w']],
        [(t, D_SSD, F32), (t, D_SSD, F32), (t, D_SSD, BF16)], accs=[(1, D_SSD), (1, D_SSD)])
    dxs_b, dbm, dcm, ddtraw, ddt_bias, da_log = ssd_bwd(r['xs'], r['bm'], r['cm'], r['dtraw'], p['dt_bias'], p['a_log'],
                                                        r['hprev'], dy, b)
    dxbc, dconv_w, dconv_b = conv_bwd(r['xbc'], p['conv_w'], p['conv_b'], dxs_a, dxs_b, dbm, dcm)
    dmixed, dattn_norm = attn_norm_bwd(dattn, r['mixed'], p['attn_norm_w'])
    dq, dk, dv = attn_bwd(r['qkvz'], r['cosv'], r['sinv'], dmixed, r['mixed'], r['lses'], b)
    wq, wk, wv, wz = (w_a[:, i * D_ATTN:(i + 1) * D_ATTN] for i in range(4))
    dh = mm("in_bwd_dh", [(dq, wq, 'nt'), (dk, wk, 'nt'), (dv, wv, 'nt'), (dz, wz, 'nt'), (dxbc, w_b, 'nt'),
                          (ddtraw, w_c, 'nt')], D_MODEL, add=dh_resid, tn=512)
    h = r['hb']
    dw_in = jnp.concatenate([mm_tn_cat("in_bwd_dw_qkvz", h, [dq, dk, dv, dz], BF16),
                             mm_tn_cat("in_bwd_dw_xbc_dt", h, [dxbc, ddtraw], BF16)[:, :D_CONV + N_HEADS]], axis=1)
    head_sum = lambda v: v.reshape(N_HEADS, HEAD_DIM).sum(axis=1).reshape(1, N_HEADS)
    grads = dict(w_in=dw_in, w_out=dw_out, conv_w=dconv_w, conv_b=dconv_b, dt_bias=ddt_bias[:, :N_HEADS],
                 a_log=da_log[:, :N_HEADS], d_skip=head_sum(ddskip), attn_norm_w=dattn_norm, ssd_norm_w=dssd_norm)
    return dh, grads


FFN1_KEYS = ('ffn1_gate', 'ffn1_up', 'ffn1_down')
FFN2_KEYS = ('ffn2_gate', 'ffn2_up', 'ffn2_down')
MIXER_KEYS = ('w_in', 'conv_w', 'w_out')
FFN_COL = ('ffn1_gate', 'ffn1_up', 'ffn2_gate', 'ffn2_up')
FFN_ROW = ('ffn1_down', 'ffn2_down')
CONV_W_COMM = (8, 2 * LANES)
SMALL = 'small'


def comm_shape(k, shapes):
    if k in FFN_COL:
        return (D_MODEL, FF_PAD)
    if k in FFN_ROW:
        return (FF_PAD, D_MODEL)
    if k == 'conv_w':
        return CONV_W_COMM
    return tuple(shapes[k][1:])


def to_comm(k, vals, shapes):
    a = vals[k].reshape(shapes[k][1:])
    r_, c_ = comm_shape(k, shapes)
    return jnp.pad(a, ((0, r_ - a.shape[0]), (0, c_ - a.shape[1])))


SMALL_ROWS, SMALL_COLS = 16, D_CONV


def pack_small(small):
    rows = [jnp.pad(small[r].reshape(1, -1), ((0, 0), (0, SMALL_COLS - small[r].size))) for r in REPLICATED]
    return jnp.concatenate(rows + [jnp.zeros((SMALL_ROWS - len(rows), SMALL_COLS), F32)], axis=0)


def full_weight(k, g):
    if k in FFN_COL:
        return g
    if k == 'conv_w':
        return jnp.transpose(g[:, :CONV_WIDTH, :D_CONV // N_DEV], (1, 0, 2)).reshape(CONV_WIDTH, D_CONV)
    return g.reshape(N_DEV * g.shape[1], g.shape[2])


def grad_shards(k, g):
    if k in FFN_COL:
        return g
    if k == 'conv_w':
        s = jnp.transpose(g.reshape(CONV_WIDTH, N_DEV, D_CONV // N_DEV), (1, 0, 2))
        return jnp.pad(s, ((0, 0), (0, CONV_W_COMM[0] - CONV_WIDTH), (0, CONV_W_COMM[1] - D_CONV // N_DEV)))
    return g.reshape(N_DEV, g.shape[0] // N_DEV, g.shape[1])


def _flip(v, bit):
    return 1 - v if bit else v


N_PEER_COPIES = N_DEV - 1


def _comm_call(name, body, arrs, out_shape):
    n = len(arrs)
    return pl.pallas_call(
        functools.partial(body, n), name=name, out_shape=out_shape,
        in_specs=[pl.BlockSpec(memory_space=pl.ANY)] * n, out_specs=[pl.BlockSpec(memory_space=pl.ANY)] * n,
        scratch_shapes=[pltpu.SemaphoreType.DMA((n * N_PEER_COPIES,)), pltpu.SemaphoreType.DMA((n * N_PEER_COPIES,)),
                        pltpu.SemaphoreType.DMA((n,))],
    )(*arrs)


def _blk(ref, idx, by_cols):
    if not by_cols:
        return ref.at[idx]
    c = ref.shape[1] // N_DEV
    return ref.at[:, pl.ds(pl.multiple_of(idx * c, LANES), c)]


def _blocked_shape(a, by_cols):
    return (a.shape[0], N_DEV * a.shape[1]) if by_cols else (N_DEV,) + a.shape


def all_gather(arrs, by_cols):
    def body(n, *refs):
        x_refs, out_refs, (send_sems, recv_sems, local_sems) = refs[:n], refs[n:2 * n], refs[2 * n:]
        x, y, c = lax.axis_index("x"), lax.axis_index("y"), lax.axis_index("c")
        me, sibling = (x, y, c), (x, y, 1 - c)
        chips = [(1 - x, y), (x, 1 - y), (1 - x, 1 - y)]

        def copy(a, k, block, to, src=None):
            px, py, pc = block
            dst = _blk(out_refs[a], 4 * px + 2 * py + pc, by_cols[a])
            return pltpu.make_async_remote_copy(
                src_ref=dst if src is None else src, dst_ref=dst, send_sem=send_sems.at[a * N_PEER_COPIES + k],
                recv_sem=recv_sems.at[a * N_PEER_COPIES + k], device_id=to, device_id_type=MESH)

        mine = [pltpu.make_async_copy(x_refs[a], _blk(out_refs[a], 4 * x + 2 * y + c, by_cols[a]), local_sems.at[a])
                for a in range(n)]
        started = []
        for a in range(n):
            mine[a].start()
            first = [copy(a, 0, me, sibling, src=x_refs[a])]
            first += [copy(a, 1 + j, me, (*chip, c), src=x_refs[a]) for j, chip in enumerate(chips)]
            for cp in first:
                cp.start()
            started += first
        for j, chip in enumerate(chips):
            for a in range(n):
                copy(a, 1 + j, (*chip, c), me).wait_recv()
                cp = copy(a, 4 + j, (*chip, c), sibling)
                cp.start()
                started.append(cp)
        for a in range(n):
            copy(a, 0, sibling, me).wait_recv()
            for j, chip in enumerate(chips):
                copy(a, 4 + j, (*chip, 1 - c), me).wait_recv()
        for cp in started:
            cp.wait_send()
        for cp in mine:
            cp.wait()

    return _comm_call("all_gather_weights", body, arrs,
                      [jax.ShapeDtypeStruct(_blocked_shape(a, bc), a.dtype) for a, bc in zip(arrs, by_cols)])


def blocks_to_cols(arrs):
    def body(*refs):
        for i, o in zip(refs[:len(arrs)], refs[len(arrs):]):
            o[...] = i[...]

    return pl.pallas_call(
        body, name="blocks_to_cols", grid=(N_DEV,),
        in_specs=[pl.BlockSpec((None,) + a.shape[1:], lambda p: (p, 0, 0)) for a in arrs],
        out_specs=[pl.BlockSpec(a.shape[1:], lambda p: (0, p)) for a in arrs],
        out_shape=[jax.ShapeDtypeStruct((a.shape[1], N_DEV * a.shape[2]), a.dtype) for a in arrs],
        compiler_params=_cparams(("parallel",)),
    )(*arrs)


def _landing_shape(a, by_cols):
    return (N_DEV, a.shape[0], a.shape[1] // N_DEV) if by_cols else a.shape


def all_to_all(arrs, by_cols):
    def body(n, *refs):
        s_refs, r_refs, (send_sems, recv_sems, local_sems) = refs[:n], refs[n:2 * n], refs[2 * n:]
        x, y, c = lax.axis_index("x"), lax.axis_index("y"), lax.axis_index("c")
        me = 4 * x + 2 * y + c

        def peer(k):
            return _flip(x, k & 4), _flip(y, k & 2), _flip(c, k & 1)

        def copy(a, k, landing):
            px, py, pc = peer(k)
            p = 4 * px + 2 * py + pc
            src, dst = (me, p) if landing else (p, me)
            return pltpu.make_async_remote_copy(
                src_ref=_blk(s_refs[a], src, by_cols[a]), dst_ref=r_refs[a].at[dst],
                send_sem=send_sems.at[a * N_PEER_COPIES + k - 1],
                recv_sem=recv_sems.at[a * N_PEER_COPIES + k - 1], device_id=(px, py, pc), device_id_type=MESH)

        mine = [pltpu.make_async_copy(_blk(s_refs[a], me, by_cols[a]), r_refs[a].at[me], local_sems.at[a]) for a in range(n)]
        sends = [copy(a, k, False) for a in range(n) for k in range(1, N_DEV)]
        for cp in mine + sends:
            cp.start()
        for a in range(n):
            for k in range(1, N_DEV):
                copy(a, k, True).wait_recv()
        for cp in sends:
            cp.wait_send()
        for cp in mine:
            cp.wait()

    return _comm_call("all_to_all_grads", body, arrs,
                      [jax.ShapeDtypeStruct(_landing_shape(a, bc), a.dtype) for a, bc in zip(arrs, by_cols)])


_HBM = pl.BlockSpec(memory_space=pltpu.HBM)
_SEM = pl.BlockSpec(memory_space=pltpu.SEMAPHORE)
_EFFECT = pltpu.SideEffectType.DATAFLOW_SIDE_EFFECTING


def _peer(k):
    x, y, c = lax.axis_index("x"), lax.axis_index("y"), lax.axis_index("c")
    return _flip(x, k & 4), _flip(y, k & 2), _flip(c, k & 1)


def _my_index():
    return 4 * lax.axis_index("x") + 2 * lax.axis_index("y") + lax.axis_index("c")


def _split_copies(mode, by_cols, src_refs, land_refs, send_sems, recv_sems):
    me = _my_index()
    out = []
    for a, bc in enumerate(by_cols):
        for k in range(1, N_DEV):
            px, py, pc = _peer(k)
            src = _blk(src_refs[a], 4 * px + 2 * py + pc, bc) if mode == 'scatter' else src_refs[a]
            dst = land_refs[a].at[me] if mode == 'scatter' else _blk(land_refs[a], me, bc)
            out.append(pltpu.make_async_remote_copy(
                src_ref=src, dst_ref=dst, send_sem=send_sems.at[a * N_PEER_COPIES + k - 1],
                recv_sem=recv_sems.at[a * N_PEER_COPIES + k - 1], device_id=(px, py, pc), device_id_type=MESH))
    return out


def exchange_start(name, mode, srcs, by_cols):
    n = len(srcs)
    lands = [lax.empty(_landing_shape(s, bc) if mode == 'scatter' else _blocked_shape(s, bc), s.dtype)
             for s, bc in zip(srcs, by_cols)]

    def body(*refs):
        src_refs, land_refs, send_sems, recv_sems = refs[:n], refs[n:2 * n], refs[2 * n], refs[2 * n + 1]
        for cp in _split_copies(mode, by_cols, src_refs, land_refs, send_sems, recv_sems):
            cp.start()
        refs[-1][...] = jnp.zeros(refs[-1].shape, F32)

    sems = pltpu.SemaphoreType.DMA((n * N_PEER_COPIES,))
    res = pl.pallas_call(
        body, name=name,
        out_shape=(sems, sems, *[pltpu.HBM(a.shape, a.dtype) for a in srcs + lands], jax.ShapeDtypeStruct((8, LANES), F32)),
        in_specs=(_HBM,) * (2 * n), out_specs=(_SEM, _SEM, *(_HBM,) * (2 * n), pl.BlockSpec(memory_space=pltpu.VMEM)),
        input_output_aliases={i: 2 + i for i in range(2 * n)},
        compiler_params=pltpu.CompilerParams(has_side_effects=_EFFECT),
    )(*[pltpu.with_memory_space_constraint(a, pltpu.HBM) for a in srcs + lands])
    return (mode, by_cols, res[:-1]), res[-1]


def exchange_wait(name, handles, after):
    mode, by_cols, (send_sems, recv_sems, *bufs) = handles
    n = len(by_cols)

    def body(*refs):
        src_refs, land_refs, s_sems, r_sems = refs[:n], refs[n:2 * n], refs[2 * n], refs[2 * n + 1]
        for cp in _split_copies(mode, by_cols, src_refs, land_refs, s_sems, r_sems):
            cp.wait_send()
            cp.wait_recv()

    res = pl.pallas_call(
        body, name=name, out_shape=tuple(pltpu.HBM(a.shape, a.dtype) for a in bufs),
        in_specs=(*(_HBM,) * (2 * n), _SEM, _SEM, pl.BlockSpec(memory_space=pl.ANY)), out_specs=(_HBM,) * (2 * n),
        input_output_aliases={i: i for i in range(2 * n)},
        compiler_params=pltpu.CompilerParams(has_side_effects=_EFFECT),
    )(*bufs, send_sems, recv_sems, after)
    me, out = _my_index(), []
    for src, land, bc in zip(res[:n], res[n:], by_cols):
        if mode == 'scatter':
            c = land.shape[2]
            own = lax.dynamic_slice(src, (0, me * c), (src.shape[0], c)) if bc else lax.dynamic_index_in_dim(src, me, 0, False)
            out.append(lax.dynamic_update_slice(land, own[None], (me, 0, 0)))
        elif bc:
            out.append(lax.dynamic_update_slice(land, src, (0, me * src.shape[1])))
        else:
            out.append(lax.dynamic_update_slice(land, src[None], (me, 0, 0)))
    return out


def _adamw_math(g, w, m, v):
    c1 = 1.0 / (1.0 - ADAM_B1 ** ADAM_STEP)
    c2 = 1.0 / (1.0 - ADAM_B2 ** ADAM_STEP)
    m = ADAM_B1 * m + (1.0 - ADAM_B1) * g
    v = ADAM_B2 * v + (1.0 - ADAM_B2) * jnp.square(g)
    return g, -ADAM_LR * ((m * c1) / (jnp.sqrt(v * c2) + ADAM_EPS) + ADAM_WD * w), m, v


def adamw(name, recv, w, m, v, tm):
    _, rows, cols = w.shape
    tm = min(tm, rows)

    def body(*refs):
        g = refs[0][0:tm, 0:cols].astype(F32)
        for s in range(1, N_DEV):
            g = g + refs[s][0:tm, 0:cols].astype(F32)
        res = _adamw_math(g, *[r[...] for r in refs[N_DEV:N_DEV + 3]])
        for r, val in zip(refs[N_DEV + 3:], res):
            r[...] = val

    part = lambda s: pl.BlockSpec((None, recv.shape[1] if tm == rows else tm, recv.shape[2]), lambda i: (s, i, 0))
    tile = pl.BlockSpec((None, tm, cols), lambda i: (0, i, 0))
    return pl.pallas_call(
        body, name=name, grid=(rows // tm,), in_specs=[part(s) for s in range(N_DEV)] + [tile] * 3, out_specs=[tile] * 4,
        out_shape=[jax.ShapeDtypeStruct((1, rows, cols), F32)] * 4, compiler_params=_cparams(("parallel",)),
    )(*[recv] * N_DEV, w, m, v)


def adamw_small(recv, wl, ml, vl):
    n = len(REPLICATED)

    def body(recv_ref, *refs):
        g = recv_ref[0]
        for s in range(1, N_DEV):
            g = g + recv_ref[s]
        for r in range(n):
            w, m, v = (refs[j * n + r][...] for j in range(3))
            for j, val in enumerate(_adamw_math(g[r:r + 1, :w.shape[1]], w, m, v)):
                refs[(3 + j) * n + r][...] = val

    arrs = [d[k].reshape(1, -1) for d in (wl, ml, vl) for k in REPLICATED]
    res = pl.pallas_call(
        body, name="adamw_small", out_shape=[jax.ShapeDtypeStruct(a.shape, F32) for a in arrs[:n]] * 4,
    )(recv, *arrs)
    return [{k: res[j * n + r].reshape(wl[k].shape) for r, k in enumerate(REPLICATED)} for j in range(4)]


ADAMW_TM = {'ffn1_gate': 256, 'ffn1_up': 256, 'ffn2_gate': 256, 'ffn2_up': 256, 'w_in': 32}


def kernel(x, positions, ln1_g, ln1_b, ffn1_gate, ffn1_up, ffn1_down, w_in, conv_w, conv_b, dt_bias, a_log, d_skip, attn_norm_w, ssd_norm_w, w_out, ln2_g, ln2_b, ffn2_gate, ffn2_up, ffn2_down, ln3_g, ln3_b, loss_target, m_ln1_g, m_ln1_b, m_ffn1_gate, m_ffn1_up, m_ffn1_down, m_w_in, m_conv_w, m_conv_b, m_dt_bias, m_a_log, m_d_skip, m_attn_norm_w, m_ssd_norm_w, m_w_out, m_ln2_g, m_ln2_b, m_ffn2_gate, m_ffn2_up, m_ffn2_down, m_ln3_g, m_ln3_b, v_ln1_g, v_ln1_b, v_ffn1_gate, v_ffn1_up, v_ffn1_down, v_w_in, v_conv_w, v_conv_b, v_dt_bias, v_a_log, v_d_skip, v_attn_norm_w, v_ssd_norm_w, v_w_out, v_ln2_g, v_ln2_b, v_ffn2_gate, v_ffn2_up, v_ffn2_down, v_ln3_g, v_ln3_b):
    args = dict(locals())
    wl = {k: args[k] for k in WEIGHTS}
    ml = {k: args["m_" + k] for k in WEIGHTS}
    vl = {k: args["v_" + k] for k in WEIGHTS}
    shapes = {k: wl[k].shape for k in WEIGHTS}
    b, s, dm = x.shape
    t = b * s

    sent = {k: to_comm(k, wl, shapes).astype(F32 if k == 'conv_w' else BF16) for k in SHARDED}
    by_cols = lambda keys: [k in FFN_COL for k in keys]
    gate, up = all_gather([sent['ffn1_gate'], sent['ffn1_up']], [False] * 2)
    (gate, up), sent = lax.optimization_barrier(((gate, up), sent))
    p = dict(zip(('ffn1_gate', 'ffn1_up'), blocks_to_cols([gate, up])))
    gather_down, token_d = exchange_start("gather_ffn1_down_start", 'gather', [sent['ffn1_down']], [False])
    sent['w_in'] = sent['w_in'] + token_d[0, 0].astype(BF16)
    gather_mixer, token_m = exchange_start("gather_mixer_start", 'gather', [sent[k] for k in MIXER_KEYS], by_cols(MIXER_KEYS))
    sent['ffn2_gate'] = sent['ffn2_gate'] + token_m[0, 0].astype(BF16)
    gather_ffn2, token_f = exchange_start("gather_ffn2_start", 'gather', [sent[k] for k in FFN2_KEYS], by_cols(FFN2_KEYS))
    for k in REPLICATED:
        p[k] = wl[k].reshape(1, -1)

    x2 = x.reshape(t, dm)
    cosv, sinv = rope_tables(positions)
    g1, u1, a1, at1 = ffn_gate_up("ffn1_gate_up", x2, p['ffn1_gate'], p['ffn1_up'], after=(token_d, token_m, token_f))
    p['ffn1_down'] = full_weight('ffn1_down', exchange_wait("gather_ffn1_down_wait", gather_down, a1)[0])
    f1, res1 = mm("ffn1_down", [(a1, p['ffn1_down'], 'nn')], D_MODEL), (x2, g1, u1, at1)
    h1, h1b = resid_ln_fwd("ln1", 0.5, x2, f1, p['ln1_g'], p['ln1_b'])
    for k, g in zip(MIXER_KEYS, exchange_wait("gather_mixer_wait", gather_mixer, h1b)):
        p[k] = full_weight(k, g)
    mix, resm = mixer_fwd(h1b, p, cosv, sinv, b)
    h2, h2b = resid_ln_fwd("ln2", 1.0, h1, mix, p['ln2_g'], p['ln2_b'])
    for k, g in zip(FFN2_KEYS, exchange_wait("gather_ffn2_wait", gather_ffn2, h2b)):
        p[k] = full_weight(k, g)
    f2, res3 = ffn_fwd("ffn2", h2b, p['ffn2_gate'], p['ffn2_up'], p['ffn2_down'])

    small, full = {}, {}
    dh2_res, df2, small['ln3_g'], small['ln3_b'], sq = ln_loss_bwd("ln3_loss_bwd", h2, f2, loss_target.reshape(t, dm),
                                                                   p['ln3_g'], p['ln3_b'])
    loss = lax.psum(jnp.sum(sq) * (0.5 / dm), AXES)

    dh2, full['ffn2_gate'], full['ffn2_up'], full['ffn2_down'] = ffn_bwd("ffn2", res3, p['ffn2_gate'], p['ffn2_up'],
                                                                       p['ffn2_down'], df2, dh2_res)
    ffn2_exchange, token = exchange_start("grads_ffn2_start", 'scatter', [grad_shards(k, full[k]) for k in FFN2_KEYS],
                                          by_cols(FFN2_KEYS))
    dh1_res, dmix, small['ln2_g'], small['ln2_b'] = resid_ln_bwd("ln2_bwd", 1.0, h1, mix, p['ln2_g'] + token[:1, :1],
                                                                 p['ln2_b'], dh2)
    dh1, gm = mixer_bwd(resm, p, dmix, dh1_res, b)
    for k in ('conv_b', 'dt_bias', 'a_log', 'd_skip', 'attn_norm_w', 'ssd_norm_w'):
        small[k] = gm[k]
    mixer_exchange, token = exchange_start("grads_mixer_start", 'scatter', [grad_shards(k, gm[k]) for k in MIXER_KEYS],
                                           by_cols(MIXER_KEYS))
    dx_res, df1, small['ln1_g'], small['ln1_b'] = resid_ln_bwd("ln1_bwd", 0.5, x2, f1, p['ln1_g'] + token[:1, :1],
                                                               p['ln1_b'], dh1)
    hb, g, u, at = res1
    small_part = pack_small(small)
    dg, du = ffn_da_act("ffn1_bwd_da_act", df1, p['ffn1_down'], g, u)
    dwd = mm_acc("ffn1_bwd_dwd", at, df1, BF16, after=dg)
    down_exchange, token = exchange_start("grads_ffn1_down_start", 'scatter', [
        grad_shards('ffn1_down', dwd), jnp.broadcast_to(small_part[None], (N_DEV,) + small_part.shape)], [False, False])
    dx = mm("ffn1_bwd_dh", [(dg, p['ffn1_gate'], 'nt'), (du, p['ffn1_up'], 'nt')], D_MODEL, add=dx_res, tn=512, after=token)
    dwg = mm_tn("ffn1_bwd_dwg", hb, dg, BF16, after=dx)
    gate_exchange, token = exchange_start("grads_ffn1_gate_start", 'scatter', [grad_shards('ffn1_gate', dwg)], [True])
    dwu = mm_tn("ffn1_bwd_dwu", hb, du, BF16, after=token)
    recv = dict(zip(('ffn1_up',), all_to_all([grad_shards('ffn1_up', dwu)], [True])))
    for keys, name, ex in (((FFN2_KEYS), "grads_ffn2_wait", ffn2_exchange), (MIXER_KEYS, "grads_mixer_wait", mixer_exchange),
                           (('ffn1_down', SMALL), "grads_ffn1_down_wait", down_exchange),
                           (('ffn1_gate',), "grads_ffn1_gate_wait", gate_exchange)):
        recv.update(zip(keys, exchange_wait(name, ex, recv['ffn1_up'])))
    outs = adamw_small(recv.pop(SMALL), wl, ml, vl)
    for k, r in recv.items():
        for o, a in zip(outs, adamw(f"adamw_{k}", r, wl[k], ml[k], vl[k], ADAMW_TM.get(k, shapes[k][1]))):
            o[k] = a
    return (loss, dx.reshape(b, s, dm), *[o[k] for o in outs for k in WEIGHTS])
```

```python
import functools
import math

import jax
import jax.numpy as jnp
import numpy as np
from jax import lax
from jax.experimental import pallas as pl
from jax.experimental.pallas import tpu as pltpu

F32, BF16 = jnp.float32, jnp.bfloat16
HI = lax.Precision.HIGHEST
MESH = pl.DeviceIdType.MESH
AXES = ("x", "y", "c")
N_DEV = 8

D_MODEL = 1024
SEQ = 2048
HEAD_DIM = 64
N_HEADS = 12
D_ATTN = N_HEADS * HEAD_DIM
DILATIONS = (1, 4, 16)
ATTN_BLOCK = 128
ROPE_THETA = 500000.0
ROPE_DIM = 16
D_SSD = 768
SSD_GROUPS = 4
SSD_STATE = 128
SSD_CHUNK = 128
D_BC = SSD_GROUPS * SSD_STATE
D_CONV = D_SSD + 2 * D_BC
CONV_WIDTH = 4
D_QKVZ = 3 * D_ATTN + D_SSD
D_IN_PROJ = D_QKVZ + D_CONV + N_HEADS
D_FF = 2816
ALPHA = 2.0 ** 0.25
LN_EPS = 1e-5
RMS_EPS = 1e-6
ADAM_LR, ADAM_B1, ADAM_B2, ADAM_EPS, ADAM_WD, ADAM_STEP = 0.001, 0.9, 0.999, 1e-08, 0.01, 10

LANES = 128
VMEM_LIMIT = 52 * 1024 * 1024
NEG = -1e30

WEIGHTS = ['ln1_g', 'ln1_b', 'ffn1_gate', 'ffn1_up', 'ffn1_down', 'w_in', 'conv_w', 'conv_b', 'dt_bias', 'a_log',
           'd_skip', 'attn_norm_w', 'ssd_norm_w', 'w_out', 'ln2_g', 'ln2_b', 'ffn2_gate', 'ffn2_up', 'ffn2_down',
           'ln3_g', 'ln3_b']
COL_SHARDED = ('ffn1_gate', 'ffn1_up', 'conv_w', 'ffn2_gate', 'ffn2_up')
ROW_SHARDED = ('ffn1_down', 'w_in', 'w_out', 'ffn2_down')
SHARDED = tuple(n for n in WEIGHTS if n in COL_SHARDED or n in ROW_SHARDED)
REPLICATED = tuple(n for n in WEIGHTS if n not in SHARDED)
FF_SHARD = D_FF // N_DEV
FF_PAD = -(-FF_SHARD // LANES) * LANES
D_FF_INT = N_DEV * FF_PAD


def _cparams(sem=None):
    return pltpu.CompilerParams(dimension_semantics=sem, vmem_limit_bytes=VMEM_LIMIT)


def _tile(n, prefs):
    for p in prefs:
        if n % p == 0:
            return p
    return n


class Op:
    def __init__(self, arr, bw=None, cb=0, ro=0):
        self.arr, self.bw, self.cb, self.ro = arr, (arr.shape[1] if bw is None else bw), cb, ro


def _op(a):
    return a if isinstance(a, Op) else Op(a)


def rowwise(name, fn, ins, consts, outs, accs=(), tm=256):
    ins = [_op(a) for a in ins]
    rows = outs[0][0]
    n_in, n_c, n_o, n_a = len(ins), len(consts), len(outs), len(accs)
    tm = min(tm, rows)
    assert rows % tm == 0, (name, rows, tm)

    def body(*refs):
        vals = [r[...].astype(F32) for r in refs[:n_in + n_c]]
        res = fn(*vals)
        res = res if isinstance(res, (tuple, list)) else (res,)
        o_refs = refs[n_in + n_c:n_in + n_c + n_o]
        a_refs = refs[n_in + n_c + n_o:]
        for r, v in zip(o_refs, res[:n_o]):
            r[...] = v.astype(r.dtype)
        if n_a:
            @pl.when(pl.program_id(0) == 0)
            def _():
                for r in a_refs:
                    r[...] = jnp.zeros(r.shape, r.dtype)
            for r, v in zip(a_refs, res[n_o:]):
                r[...] += v

    in_specs = [pl.BlockSpec((tm, o.bw), functools.partial(lambda i, o: (i + o.ro, o.cb), o=o)) for o in ins]
    in_specs += [pl.BlockSpec(c.shape, functools.partial(lambda i, nd: (0,) * nd, nd=c.ndim)) for c in consts]
    out_specs = [pl.BlockSpec((tm, w), lambda i: (i, 0)) for (_, w, _) in outs]
    out_specs += [pl.BlockSpec(s, functools.partial(lambda i, nd: (0,) * nd, nd=len(s))) for s in accs]
    out_shape = [jax.ShapeDtypeStruct((r, w), dt) for (r, w, dt) in outs]
    out_shape += [jax.ShapeDtypeStruct(s, F32) for s in accs]
    res = pl.pallas_call(
        body, name=name, grid=(rows // tm,), in_specs=in_specs, out_specs=out_specs, out_shape=out_shape,
        compiler_params=_cparams(("arbitrary",) if n_a else ("parallel",)),
    )(*[o.arr for o in ins], *consts)
    return res


MM_TM = 512
MM_TN = (1024, 896, 768, 512, 256, 128)
_NT = (((1,), (1,)), ((), ()))
_NN = (((1,), (0,)), ((), ()))
_TN = (((0,), (0,)), ((), ()))


def _dot(a, b, dn, precision=None):
    return lax.dot_general(a, b, dn, preferred_element_type=F32, precision=precision)


def _mm_specs(name, pairs, n_out, tm, tn):
    in_specs, args = [], []
    for a, b, mode in pairs:
        o = _op(a)
        in_specs.append(pl.BlockSpec((tm, o.bw), functools.partial(lambda j, i, o: (i, o.cb), o=o)))
        args.append(o.arr)
        if mode == 'nn':
            assert b.shape == (o.bw, n_out), (name, b.shape, o.bw, n_out)
            in_specs.append(pl.BlockSpec((o.bw, tn), lambda j, i: (0, j)))
        else:
            assert b.shape == (n_out, o.bw), (name, b.shape, o.bw, n_out)
            in_specs.append(pl.BlockSpec((tn, o.bw), lambda j, i: (j, 0)))
        args.append(b)
    return in_specs, args


def _mm_acc(refs, pairs):
    acc = None
    for k, (_, _, mode) in enumerate(pairs):
        d = _dot(refs[2 * k][...].astype(BF16), refs[2 * k + 1][...].astype(BF16), _NN if mode == 'nn' else _NT)
        acc = d if acc is None else acc + d
    return acc


def mm(name, pairs, n_out, add=None, out_dtype=F32, tm=MM_TM, tn=None, after=None):
    m = _op(pairs[0][0]).arr.shape[0]
    tn = tn or _tile(n_out, MM_TN)
    n_p = len(pairs)

    def body(*refs):
        acc = _mm_acc(refs, pairs)
        if add is not None:
            acc = acc + refs[2 * n_p][...]
        refs[-1][...] = acc.astype(refs[-1].dtype)

    in_specs, args = _mm_specs(name, pairs, n_out, tm, tn)
    tile = pl.BlockSpec((tm, tn), lambda j, i: (i, j))
    if add is not None:
        in_specs.append(tile)
        args.append(add)
    if after is not None:
        in_specs.append(pl.BlockSpec(memory_space=pl.ANY))
        args.append(after)
    return pl.pallas_call(
        body, name=name, grid=(n_out // tn, m // tm), in_specs=in_specs, out_specs=tile,
        out_shape=jax.ShapeDtypeStruct((m, n_out), out_dtype),
        compiler_params=_cparams(("parallel", "parallel")),
    )(*args)


def mm_tn(name, a, b, out_dtype=F32, tt=1024, after=None):
    a, b = _op(a), _op(b)
    t = a.arr.shape[0]
    k, n = a.bw, b.bw
    tk = _tile(k, (512, 896, 768, 256, 128))
    tn = _tile(n, (3072, 1792) + MM_TN)
    tt = min(tt, t)
    n_t = t // tt
    order = [] if after is None else [after]

    def body(a_ref, b_ref, *rest):
        o_ref, acc_ref = rest[-2:]
        s = pl.program_id(2)
        d = _dot(a_ref[...].astype(BF16), b_ref[...].astype(BF16), _TN)

        @pl.when(s == 0)
        def _():
            acc_ref[...] = d

        @pl.when(s > 0)
        def _():
            acc_ref[...] += d

        @pl.when(s == n_t - 1)
        def _():
            o_ref[...] = acc_ref[...].astype(o_ref.dtype)

    return pl.pallas_call(
        body, name=name, grid=(k // tk, n // tn, n_t),
        in_specs=[pl.BlockSpec((tt, tk), functools.partial(lambda kk, nn, s, o: (s, o.cb * (o.bw // tk) + kk), o=a)),
                  pl.BlockSpec((tt, tn), functools.partial(lambda kk, nn, s, o: (s, o.cb * (o.bw // tn) + nn), o=b))]
        + [pl.BlockSpec(memory_space=pl.ANY) for _ in order],
        out_specs=pl.BlockSpec((tk, tn), lambda kk, nn, s: (kk, nn)),
        out_shape=jax.ShapeDtypeStruct((k, n), out_dtype),
        scratch_shapes=[pltpu.VMEM((tk, tn), F32)],
        compiler_params=_cparams(("parallel", "parallel", "arbitrary")),
    )(a.arr, b.arr, *order)


def _sigmoid(x):
    return 1.0 / (1.0 + jnp.exp(-x))


def _silu(x):
    return x * _sigmoid(x)


def _softplus(x):
    return jnp.maximum(x, 0.0) + jnp.log(1.0 + jnp.exp(-jnp.abs(x)))


def _act(g, u):
    return _silu(g) * u


def _resid_ln(scale, h, branch, g, b):
    r = ALPHA * h + scale * branch
    mu = jnp.mean(r, axis=-1, keepdims=True)
    var = jnp.mean(jnp.square(r - mu), axis=-1, keepdims=True)
    return (r - mu) * lax.rsqrt(var + LN_EPS) * g + b


def _rms(t, w):
    return t * lax.rsqrt(jnp.mean(t * t, axis=-1, keepdims=True) + RMS_EPS) * w


def _branch_weights(l1, l2, l3):
    m = jnp.maximum(jnp.maximum(l1, l2), l3)
    e1, e2, e3 = jnp.exp(l1 - m), jnp.exp(l2 - m), jnp.exp(l3 - m)
    inv = 1.0 / (e1 + e2 + e3)
    return e1 * inv, e2 * inv, e3 * inv


def _gate(y, xs, z, dskip, w):
    return _rms((y + dskip * xs) * _silu(z), w)


def _rot(x):
    d = lax.broadcasted_iota(jnp.int32, x.shape, 1) % HEAD_DIM
    up = pltpu.roll(x, x.shape[1] - ROPE_DIM // 2, 1)
    down = jnp.where(d < ROPE_DIM, pltpu.roll(x, ROPE_DIM // 2, 1), 0.0)
    return jnp.where(d < ROPE_DIM // 2, up, down)


def ffn_gate_up(name, h, wg, wu, after=()):
    m, nf = h.shape[0], wg.shape[1]
    tn = _tile(nf, MM_TN)

    def body(h_ref, g_w, u_w, *rest):
        g_ref, u_ref, a_ref, at_ref = rest[-4:]
        hb = h_ref[...].astype(BF16)
        g = _dot(hb, g_w[...].astype(BF16), _NN)
        u = _dot(hb, u_w[...].astype(BF16), _NN)
        g_ref[...] = g.astype(g_ref.dtype)
        u_ref[...] = u.astype(u_ref.dtype)
        a = _act(g, u)
        a_ref[...] = a.astype(a_ref.dtype)
        at_ref[...] = a.T.astype(at_ref.dtype)

    in_specs, args = _mm_specs(name, [(h, wg, 'nn')], nf, MM_TM, tn)
    in_specs.append(in_specs[1])
    in_specs += [pl.BlockSpec(memory_space=pl.ANY) for _ in after]
    tile = pl.BlockSpec((MM_TM, tn), lambda j, i: (i, j))
    return pl.pallas_call(
        body, name=name, grid=(nf // tn, m // MM_TM), in_specs=in_specs,
        out_specs=[tile] * 3 + [pl.BlockSpec((tn, MM_TM), lambda j, i: (j, i))],
        out_shape=[jax.ShapeDtypeStruct((m, nf), BF16)] * 3 + [jax.ShapeDtypeStruct((nf, m), BF16)],
        compiler_params=_cparams(("parallel", "parallel")),
    )(*args, wu, *after)


def mm_tn_cat(name, a, bs, out_dtype=F32, tt=1024):
    t, k = a.shape
    widths = [b.shape[1] for b in bs]
    n, tk, tt = sum(widths), _tile(k, (512, 256, 128)), min(tt, t)
    n_t = t // tt

    def body(a_ref, *rest):
        b_refs, o_ref, acc_ref = rest[:len(bs)], rest[-2], rest[-1]
        s = pl.program_id(1)
        at = a_ref[...].astype(BF16)
        d = jnp.concatenate([_dot(at, b[...].astype(BF16), _TN) for b in b_refs], axis=1)

        @pl.when(s == 0)
        def _():
            acc_ref[...] = d

        @pl.when(s > 0)
        def _():
            acc_ref[...] += d

        @pl.when(s == n_t - 1)
        def _():
            o_ref[...] = acc_ref[...].astype(o_ref.dtype)

    return pl.pallas_call(
        body, name=name, grid=(k // tk, n_t),
        in_specs=[pl.BlockSpec((tt, tk), lambda kk, s: (s, kk))] + [pl.BlockSpec((tt, w), lambda kk, s: (s, 0)) for w in widths],
        out_specs=pl.BlockSpec((tk, n), lambda kk, s: (kk, 0)),
        out_shape=jax.ShapeDtypeStruct((k, n), out_dtype), scratch_shapes=[pltpu.VMEM((tk, n), F32)],
        compiler_params=_cparams(("parallel", "arbitrary")),
    )(a, *bs)


def mm_acc(name, a, b, out_dtype=F32, tt=1024, after=None):
    k, t = a.shape
    n = b.shape[1]
    tk, tn, tt = _tile(k, (1024, 512, 256, 128)), _tile(n, MM_TN), min(tt, t)
    n_t = t // tt
    order = [] if after is None else [after]

    def body(a_ref, b_ref, *rest):
        o_ref, acc_ref = rest[-2:]
        s = pl.program_id(2)
        d = _dot(a_ref[...].astype(BF16), b_ref[...].astype(BF16), _NN)

        @pl.when(s == 0)
        def _():
            acc_ref[...] = d

        @pl.when(s > 0)
        def _():
            acc_ref[...] += d

        @pl.when(s == n_t - 1)
        def _():
            o_ref[...] = acc_ref[...].astype(o_ref.dtype)

    return pl.pallas_call(
        body, name=name, grid=(k // tk, n // tn, n_t),
        in_specs=[pl.BlockSpec((tk, tt), lambda kk, nn, s: (kk, s)), pl.BlockSpec((tt, tn), lambda kk, nn, s: (s, nn))]
        + [pl.BlockSpec(memory_space=pl.ANY) for _ in order],
        out_specs=pl.BlockSpec((tk, tn), lambda kk, nn, s: (kk, nn)),
        out_shape=jax.ShapeDtypeStruct((k, n), out_dtype), scratch_shapes=[pltpu.VMEM((tk, tn), F32)],
        compiler_params=_cparams(("parallel", "parallel", "arbitrary")),
    )(a, b, *order)


def ffn_da_act(name, df, wd, g, u):
    m, nf = df.shape[0], wd.shape[0]
    tn = _tile(nf, MM_TN)

    def body(df_ref, w_ref, g_ref, u_ref, dg_ref, du_ref):
        da = _dot(df_ref[...].astype(BF16), w_ref[...].astype(BF16), _NT)
        g, u = g_ref[...].astype(F32), u_ref[...].astype(F32)
        sig = _sigmoid(g)
        gs = g * sig
        dg_ref[...] = (da * u * (sig + gs * (1.0 - sig))).astype(dg_ref.dtype)
        du_ref[...] = (da * gs).astype(du_ref.dtype)

    in_specs, args = _mm_specs(name, [(df, wd, 'nt')], nf, MM_TM, tn)
    tile = pl.BlockSpec((MM_TM, tn), lambda j, i: (i, j))
    return pl.pallas_call(
        body, name=name, grid=(nf // tn, m // MM_TM), in_specs=in_specs + [tile, tile], out_specs=[tile] * 2,
        out_shape=[jax.ShapeDtypeStruct((m, nf), BF16)] * 2, compiler_params=_cparams(("parallel", "parallel")),
    )(*args, g, u)


def resid_ln_fwd(name, scale, h, branch, ln_g, ln_b):
    t = h.shape[0]

    def fn(*a):
        y = _resid_ln(scale, *a)
        return y, y

    return rowwise(name, fn, [h, branch], [ln_g, ln_b], [(t, D_MODEL, F32), (t, D_MODEL, BF16)], tm=512)


def ffn_fwd(tag, hb, wg, wu, wd, after=()):
    g, u, a, at = ffn_gate_up(f"{tag}_gate_up", hb, wg, wu, after)
    f = mm(f"{tag}_down", [(a, wd, 'nn')], D_MODEL)
    return f, (hb, g, u, at)


def ln_loss_bwd(name, h, branch, target, ln_g, ln_b):
    t, dm = h.shape

    def fn(h_, br_, tgt, g_, b_):
        y, vjp = jax.vjp(functools.partial(_resid_ln, 0.5), h_, br_, g_, b_)
        e = y - tgt
        return (*vjp(e * (1.0 / dm)), jnp.sum(e * e, axis=0, keepdims=True))

    return rowwise(name, fn, [h, branch, target], [ln_g, ln_b], [(t, dm, F32), (t, dm, BF16)],
                   accs=[(1, dm), (1, dm), (1, dm)], tm=512)


def resid_ln_bwd(name, scale, h, branch, ln_g, ln_b, dout, extra=None):
    t = h.shape[0]

    def fn(h_, br_, do_, *rest):
        g_, b_ = rest[-2], rest[-1]
        _, vjp = jax.vjp(functools.partial(_resid_ln, scale), h_, br_, g_, b_)
        dh, dbr, dg, db = vjp(do_)
        if extra is not None:
            dh = dh + rest[0]
        return dh, dbr, dg, db

    ins = [h, branch, dout] + ([extra] if extra is not None else [])
    return rowwise(name, fn, ins, [ln_g, ln_b], [(t, D_MODEL, F32), (t, D_MODEL, BF16)],
                   accs=[(1, D_MODEL), (1, D_MODEL)], tm=512)


def ffn_bwd(tag, res, wg, wu, wd, df, dh_resid):
    hb, g, u, at = res
    dg, du = ffn_da_act(f"{tag}_bwd_da_act", df, wd, g, u)
    dwd = mm_acc(f"{tag}_bwd_dwd", at, df, BF16)
    dh = mm(f"{tag}_bwd_dh", [(dg, wg, 'nt'), (du, wu, 'nt')], D_MODEL, add=dh_resid, tn=512)
    dwg = mm_tn(f"{tag}_bwd_dwg", hb, dg, BF16)
    dwu = mm_tn(f"{tag}_bwd_dwu", hb, du, BF16)
    return dh, dwg, dwu, dwd


def rope_tables(positions):
    inv_freq = ROPE_THETA ** (-jnp.arange(0, ROPE_DIM, 2, dtype=F32) / ROPE_DIM)
    ang = positions.reshape(-1, 1).astype(F32) * inv_freq
    c, s = jnp.cos(ang), jnp.sin(ang)
    t = ang.shape[0]
    cosv = jnp.concatenate([c, c, jnp.ones((t, HEAD_DIM - ROPE_DIM), F32)], axis=1)
    sinv = jnp.concatenate([-s, s, jnp.zeros((t, HEAD_DIM - ROPE_DIM), F32)], axis=1)
    return jnp.tile(cosv, (1, 2)), jnp.tile(sinv, (1, 2))


def _pair_masks():
    lane = lax.broadcasted_iota(jnp.int32, (1, LANES), 1)
    return (lane < HEAD_DIM, lane >= HEAD_DIM)


def _band_masks():
    row = lax.broadcasted_iota(jnp.int32, (ATTN_BLOCK, ATTN_BLOCK), 0)
    col = lax.broadcasted_iota(jnp.int32, (ATTN_BLOCK, ATTN_BLOCK), 1)
    return col >= row, col <= row


def _residue_blocks():
    out = []
    for g, d in enumerate(DILATIONS):
        for r in range(d):
            for i in range(SEQ // d // ATTN_BLOCK):
                rows = lambda j: pl.ds(r + j * ATTN_BLOCK * d, ATTN_BLOCK, stride=d) if d > 1 else pl.ds(j * ATTN_BLOCK, ATTN_BLOCK)
                out.append((g, rows(i), rows(i - 1) if i > 0 else None))
    return out


N_HEAD_PAIRS = D_ATTN // LANES
SCALE = HEAD_DIM ** -0.5
ATTN_GROUP = 4


def _block_operands(qr, kr, v_ref, cur, prev):
    prev_ok, cur_ok = _band_masks()
    if prev is None:
        return qr[cur, :], kr[cur, :].astype(BF16), v_ref[cur, :], cur_ok
    kcat = jnp.concatenate([kr[prev, :], kr[cur, :]], axis=0).astype(BF16)
    vcat = jnp.concatenate([v_ref[prev, :], v_ref[cur, :]], axis=0)
    return qr[cur, :], kcat, vcat, jnp.concatenate([prev_ok, cur_ok], axis=1)


def _attn_specs(b):
    col = lambda cb: pl.BlockSpec((SEQ, LANES), lambda bb, hp: (bb, cb + hp))
    tab = pl.BlockSpec((SEQ, LANES), lambda bb, hp: (bb, 0))
    return col, tab


def attn_fwd(qkvz, cosv, sinv, b):
    t = qkvz.shape[0]
    col, tab = _attn_specs(b)
    blocks = _residue_blocks()

    def body(q_ref, k_ref, v_ref, c_ref, s_ref, o_ref, l1_ref, l2_ref, l3_ref, qr, kr, o1, o2, o3):
        l_refs, o_scr = (l1_ref, l2_ref, l3_ref), (o1, o2, o3)
        c, s = c_ref[...], s_ref[...]
        q, k = q_ref[...], k_ref[...]
        qr[...] = q * c + _rot(q) * s
        kr[...] = k * c + _rot(k) * s
        masks = _pair_masks()
        for lo in range(0, len(blocks), ATTN_GROUP):
            chains = []
            for g, cur, prev in blocks[lo:lo + ATTN_GROUP]:
                q2, kcat, vcat, ok = _block_operands(qr, kr, v_ref, cur, prev)
                for m in masks:
                    qm = jnp.where(m, q2, 0.0).astype(BF16)
                    chains.append(dict(g=g, cur=cur, m=m, v=jnp.where(m, vcat, 0.0).astype(BF16),
                                       s=jnp.where(ok, _dot(qm, kcat, _NT) * SCALE, NEG)))
            for ch in chains:
                mx = jnp.max(ch['s'], axis=1, keepdims=True)
                p = jnp.exp(ch['s'] - mx)
                den = jnp.sum(p, axis=1, keepdims=True)
                ch.update(p=p.astype(BF16), inv=1.0 / den, lse=mx + jnp.log(den))
            for ch in chains:
                ch['o'] = _dot(ch['p'], ch['v'], _NN) * ch['inv']
            for c0, c1 in zip(chains[0::2], chains[1::2]):
                o_scr[c0['g']][c0['cur'], :] = c0['o'] + c1['o']
                l_refs[c0['g']][c0['cur'], :] = jnp.where(c0['m'], c0['lse'], c1['lse'])
        w1, w2, w3 = _branch_weights(l1_ref[...], l2_ref[...], l3_ref[...])
        o_ref[...] = w1 * o1[...] + w2 * o2[...] + w3 * o3[...]

    shp = jax.ShapeDtypeStruct((t, D_ATTN), F32)
    return pl.pallas_call(
        body, name="attn_fwd", grid=(b, N_HEAD_PAIRS),
        in_specs=[col(0), col(N_HEAD_PAIRS), col(2 * N_HEAD_PAIRS), tab, tab],
        out_specs=[col(0)] * 4, out_shape=[shp] * 4,
        scratch_shapes=[pltpu.VMEM((SEQ, LANES), F32)] * 5,
        compiler_params=_cparams(("parallel", "parallel")),
    )(qkvz, qkvz, qkvz, cosv, sinv)


def attn_bwd(qkvz, cosv, sinv, dmix, mixed, lses, b):
    t = qkvz.shape[0]
    col, tab = _attn_specs(b)
    blocks = _residue_blocks()
    hd = np.arange(LANES) // HEAD_DIM
    head_ones = jnp.asarray((hd[:, None] == hd[None, :]).astype(np.float32))

    def body(q_ref, k_ref, v_ref, c_ref, s_ref, dm_ref, mx_ref, l1_ref, l2_ref, l3_ref, ones_ref,
             dq_out, dk_out, dv_out, qr, kr, do1, do2, do3, dd1, dd2, dd3, dq_ref, dk_ref, dv_ref):
        l_refs, do_scr, dd_scr = (l1_ref, l2_ref, l3_ref), (do1, do2, do3), (dd1, dd2, dd3)
        c, s = c_ref[...], s_ref[...]
        q, k = q_ref[...], k_ref[...]
        qr[...] = q * c + _rot(q) * s
        kr[...] = k * c + _rot(k) * s
        dm = dm_ref[...]
        tot = _dot(dm * mx_ref[...], ones_ref[...], _NN, HI)
        for w, do_g, dd_g in zip(_branch_weights(l1_ref[...], l2_ref[...], l3_ref[...]), do_scr, dd_scr):
            do_g[...] = w * dm
            dd_g[...] = w * tot
        dq_ref[...] = jnp.zeros((SEQ, LANES), F32)
        dk_ref[...] = jnp.zeros((SEQ, LANES), F32)
        dv_ref[...] = jnp.zeros((SEQ, LANES), F32)
        masks = _pair_masks()
        for lo in range(0, len(blocks), ATTN_GROUP):
            chains = []
            for g, cur, prev in blocks[lo:lo + ATTN_GROUP]:
                q2, kcat, vcat, ok = _block_operands(qr, kr, v_ref, cur, prev)
                vcat = vcat.astype(BF16)
                do2_, l2, dd2_ = do_scr[g][cur, :], l_refs[g][cur, :], dd_scr[g][cur, :]
                l2s, dd2s = pltpu.roll(l2, HEAD_DIM, 1), pltpu.roll(dd2_, HEAD_DIM, 1)
                for m in masks:
                    qm = jnp.where(m, q2, 0.0).astype(BF16)
                    dom = jnp.where(m, do2_, 0.0).astype(BF16)
                    lrep, ddrep = jnp.where(m, l2, l2s), jnp.where(m, dd2_, dd2s)
                    if prev is not None:
                        lrep, ddrep = jnp.concatenate([lrep, lrep], axis=1), jnp.concatenate([ddrep, ddrep], axis=1)
                    chains.append(dict(cur=cur, prev=prev, qm=qm, dom=dom, km=jnp.where(m, kcat, 0), lrep=lrep, ddrep=ddrep,
                                       s=jnp.where(ok, _dot(qm, kcat, _NT) * SCALE, NEG), dp=_dot(dom, vcat, _NT)))
            for ch in chains:
                p = jnp.exp(ch['s'] - ch['lrep'])
                ch.update(p=p.astype(BF16), ds=(p * (ch['dp'] - ch['ddrep']) * SCALE).astype(BF16))
            for ch in chains:
                ch.update(dq=_dot(ch['ds'], ch['km'], _NN), dk=_dot(ch['ds'], ch['qm'], _TN), dv=_dot(ch['p'], ch['dom'], _TN))
            for c0, c1 in zip(chains[0::2], chains[1::2]):
                cur, prev = c0['cur'], c0['prev']
                dk, dv = c0['dk'] + c1['dk'], c0['dv'] + c1['dv']
                dq_ref[cur, :] += c0['dq'] + c1['dq']
                if prev is None:
                    dk_ref[cur, :] += dk
                    dv_ref[cur, :] += dv
                else:
                    dk_ref[prev, :] += dk[:ATTN_BLOCK]
                    dv_ref[prev, :] += dv[:ATTN_BLOCK]
                    dk_ref[cur, :] += dk[ATTN_BLOCK:]
                    dv_ref[cur, :] += dv[ATTN_BLOCK:]
        dq, dk = dq_ref[...], dk_ref[...]
        dq_out[...] = (dq * c + _rot(dq * s)).astype(dq_out.dtype)
        dk_out[...] = (dk * c + _rot(dk * s)).astype(dk_out.dtype)
        dv_out[...] = dv_ref[...].astype(dv_out.dtype)

    shp = jax.ShapeDtypeStruct((t, D_ATTN), BF16)
    return pl.pallas_call(
        body, name="attn_bwd", grid=(b, N_HEAD_PAIRS),
        in_specs=[col(0), col(N_HEAD_PAIRS), col(2 * N_HEAD_PAIRS), tab, tab, col(0), col(0), col(0), col(0), col(0),
                  pl.BlockSpec((LANES, LANES), lambda bb, hp: (0, 0))],
        out_specs=[col(0)] * 3, out_shape=[shp] * 3,
        scratch_shapes=[pltpu.VMEM((SEQ, LANES), F32)] * 11,
        compiler_params=_cparams(("parallel", "parallel")),
    )(qkvz, qkvz, qkvz, cosv, sinv, dmix, mixed, *lses, head_ones)


def attn_norm_fwd(mixed, norm_w):
    return rowwise("attn_norm", _rms, [mixed], [norm_w], [(mixed.shape[0], D_ATTN, BF16)])[0]


def attn_norm_bwd(dout, mixed, norm_w):
    def fn(dy, mx, w):
        _, vjp = jax.vjp(_rms, mx, w)
        return vjp(dy)

    return rowwise("attn_norm_bwd", fn, [dout, mixed], [norm_w], [(dout.shape[0], D_ATTN, F32)], accs=[(1, D_ATTN)])


CONV_TM = 256
HALO = 8


def _conv_columns(refs):
    xs_ref, bm_ref, cm_ref = refs
    out = []
    for c in range(D_CONV // LANES):
        lo = c * LANES
        ref, base = (xs_ref, 0) if lo < D_SSD else (bm_ref, D_SSD) if lo < D_SSD + D_BC else (cm_ref, D_SSD + D_BC)
        out.append((slice(lo, lo + LANES), (ref, slice(lo - base, lo - base + LANES))))
    return out


def _conv_taps(scr, w_ref, cs, first_row, step, tm):
    acc = None
    for k in range(CONV_WIDTH):
        term = w_ref[k:k + 1, cs] * scr[pl.ds(first_row + step * k, tm), cs]
        acc = term if acc is None else acc + term
    return acc


def conv_fwd(u, w, bias):
    t = u.shape[0]
    tm, per_seq = CONV_TM, SEQ // CONV_TM

    def body(u_ref, h_ref, w_ref, b_ref, xs_ref, bm_ref, cm_ref, scr):
        first = pl.program_id(0) % per_seq == 0
        scr[0:HALO, :] = jnp.where(first, 0.0, h_ref[...])
        scr[HALO:, :] = u_ref[...]
        for cs, (o_ref, os_) in _conv_columns((xs_ref, bm_ref, cm_ref)):
            o_ref[:, os_] = _silu(_conv_taps(scr, w_ref, cs, HALO - CONV_WIDTH + 1, 1, tm) + b_ref[:, cs])

    return pl.pallas_call(
        body, name="conv_fwd", grid=(t // tm,),
        in_specs=[pl.BlockSpec((tm, D_CONV), lambda i: (i, 0)),
                  pl.BlockSpec((HALO, D_CONV), lambda i: (jnp.maximum(i * (tm // HALO) - 1, 0), 0)),
                  pl.BlockSpec((CONV_WIDTH, D_CONV), lambda i: (0, 0)), pl.BlockSpec((1, D_CONV), lambda i: (0, 0))],
        out_specs=[pl.BlockSpec((tm, D_SSD), lambda i: (i, 0)), pl.BlockSpec((tm, D_BC), lambda i: (i, 0)),
                   pl.BlockSpec((tm, D_BC), lambda i: (i, 0))],
        out_shape=[jax.ShapeDtypeStruct((t, D_SSD), F32), jax.ShapeDtypeStruct((t, D_BC), F32),
                   jax.ShapeDtypeStruct((t, D_BC), F32)],
        scratch_shapes=[pltpu.VMEM((tm + HALO, D_CONV), F32)],
        compiler_params=_cparams(("parallel",)),
    )(u, u, w, bias)


def conv_bwd(u, w, bias, dxs_a, dxs_b, dbm, dcm):
    t = u.shape[0]
    tm, per_seq = CONV_TM, SEQ // CONV_TM
    n_tiles = t // tm

    def body1(u_ref, h_ref, dxs_ref, dxs2_ref, dbm_ref, dcm_ref, w_ref, b_ref, dz_ref, dw_ref, db_ref, scr):
        i = pl.program_id(0)
        first = i % per_seq == 0
        scr[0:HALO, :] = jnp.where(first, 0.0, h_ref[...])
        scr[HALO:, :] = u_ref[...]

        @pl.when(i == 0)
        def _():
            dw_ref[...] = jnp.zeros(dw_ref.shape, F32)
            db_ref[...] = jnp.zeros(db_ref.shape, F32)
        for cs, (g_ref, gs) in _conv_columns((dxs_ref, dbm_ref, dcm_ref)):
            acc = _conv_taps(scr, w_ref, cs, HALO - CONV_WIDTH + 1, 1, tm) + b_ref[:, cs]
            sig = _sigmoid(acc)
            dy = g_ref[:, gs] + dxs2_ref[:, gs] if g_ref is dxs_ref else g_ref[:, gs]
            dz = dy * sig * (1.0 + acc * (1.0 - sig))
            dz_ref[:, cs] = dz
            db_ref[:, cs] += jnp.sum(dz, axis=0, keepdims=True)
            for k in range(CONV_WIDTH):
                dw_ref[k:k + 1, cs] += jnp.sum(dz * scr[pl.ds(HALO - CONV_WIDTH + 1 + k, tm), cs], axis=0, keepdims=True)

    dz, dw, db = pl.pallas_call(
        body1, name="conv_bwd_dz", grid=(n_tiles,),
        in_specs=[pl.BlockSpec((tm, D_CONV), lambda i: (i, 0)),
                  pl.BlockSpec((HALO, D_CONV), lambda i: (jnp.maximum(i * (tm // HALO) - 1, 0), 0)),
                  pl.BlockSpec((tm, D_SSD), lambda i: (i, 0)), pl.BlockSpec((tm, D_SSD), lambda i: (i, 0)),
                  pl.BlockSpec((tm, D_BC), lambda i: (i, 0)), pl.BlockSpec((tm, D_BC), lambda i: (i, 0)),
                  pl.BlockSpec((CONV_WIDTH, D_CONV), lambda i: (0, 0)), pl.BlockSpec((1, D_CONV), lambda i: (0, 0))],
        out_specs=[pl.BlockSpec((tm, D_CONV), lambda i: (i, 0)), pl.BlockSpec((CONV_WIDTH, D_CONV), lambda i: (0, 0)),
                   pl.BlockSpec((1, D_CONV), lambda i: (0, 0))],
        out_shape=[jax.ShapeDtypeStruct((t, D_CONV), F32), jax.ShapeDtypeStruct((CONV_WIDTH, D_CONV), F32),
                   jax.ShapeDtypeStruct((1, D_CONV), F32)],
        scratch_shapes=[pltpu.VMEM((tm + HALO, D_CONV), F32)],
        compiler_params=_cparams(("arbitrary",)),
    )(u, u, dxs_a, dxs_b, dbm, dcm, w, bias)

    def body2(dz_ref, n_ref, w_ref, du_ref, scr):
        last = pl.program_id(0) % per_seq == per_seq - 1
        scr[0:tm, :] = dz_ref[...]
        scr[tm:, :] = jnp.where(last, 0.0, n_ref[...])
        for c in range(D_CONV // LANES):
            cs = slice(c * LANES, (c + 1) * LANES)
            du_ref[:, cs] = _conv_taps(scr, w_ref, cs, CONV_WIDTH - 1, -1, tm).astype(du_ref.dtype)

    du = pl.pallas_call(
        body2, name="conv_bwd_du", grid=(n_tiles,),
        in_specs=[pl.BlockSpec((tm, D_CONV), lambda i: (i, 0)),
                  pl.BlockSpec((HALO, D_CONV), lambda i: (jnp.minimum((i + 1) * (tm // HALO), t // HALO - 1), 0)),
                  pl.BlockSpec((CONV_WIDTH, D_CONV), lambda i: (0, 0))],
        out_specs=pl.BlockSpec((tm, D_CONV), lambda i: (i, 0)),
        out_shape=jax.ShapeDtypeStruct((t, D_CONV), BF16),
        scratch_shapes=[pltpu.VMEM((tm + HALO, D_CONV), F32)],
        compiler_params=_cparams(("parallel",)),
    )(dz, dz, w)
    return du, dw, db


Q = SSD_CHUNK
N_PAIRS = D_SSD // LANES
HEADS_PER_GROUP = N_HEADS // SSD_GROUPS


def _rep(a, j):
    return jnp.broadcast_to(a[:, j:j + 1], a.shape)


def _dot_exact01(a, b, dn, a_is_01):
    x = b if a_is_01 else a
    hi = x.astype(BF16)
    mid = (x - hi.astype(F32)).astype(BF16)
    lo = (x - hi.astype(F32) - mid.astype(F32)).astype(BF16)
    z = a.astype(BF16) if a_is_01 else b.astype(BF16)
    out = None
    for term in (hi, mid, lo):
        d = _dot(z, term, dn) if a_is_01 else _dot(term, z, dn)
        out = d if out is None else out + d
    return out


def _pad_lanes(v, fill=0.0):
    row = jnp.pad(v.reshape(1, -1).astype(F32), ((0, 0), (0, LANES - v.size)), constant_values=fill)
    return row, row.reshape(LANES, 1)


def _ssd_common(dtr_ref, dtrt_ref, bias_r, bias_c, alog_r, alog_c):
    row = lax.broadcasted_iota(jnp.int32, (Q, Q), 0)
    col = lax.broadcasted_iota(jnp.int32, (Q, Q), 1)
    tril = row >= col
    lane = lax.broadcasted_iota(jnp.int32, (1, LANES), 1)
    a_r = jnp.where(lane < N_HEADS, -jnp.exp(alog_r[...]), 0.0)
    sub = lax.broadcasted_iota(jnp.int32, (LANES, 1), 0)
    a_c = jnp.where(sub < N_HEADS, -jnp.exp(alog_c[...]), 0.0)
    dt = _softplus(dtr_ref[...] + bias_r[...])
    cs = _dot_exact01(tril, dt * a_r, _NN, True)
    dtt = _softplus(dtrt_ref[...] + bias_c[...])
    cst = _dot_exact01(dtt * a_c, row <= col, _NN, False)
    return tril, lane, a_r, dt, cs, cst


def _ssd_specs(b, nc, rev):
    ci = (lambda c: nc - 1 - c) if rev else (lambda c: c)
    rows = lambda w: pl.BlockSpec((Q, w), lambda bb, c: (bb * nc + ci(c), 0))
    dtt = pl.BlockSpec((LANES, Q), lambda bb, c: (0, bb * nc + ci(c)))
    const = lambda s: pl.BlockSpec(s, lambda bb, c: (0,) * len(s))
    state = pl.BlockSpec((None, N_PAIRS, LANES, SSD_STATE), lambda bb, c: (bb * nc + ci(c), 0, 0, 0))
    return rows, dtt, const, state


def ssd_fwd(xs, bm, cm, dtraw, dt_bias, a_log, b):
    t = xs.shape[0]
    nc = SEQ // Q
    rows, dtt_spec, const, state = _ssd_specs(b, nc, False)
    bias_r, bias_c = _pad_lanes(dt_bias)
    alog_r, alog_c = _pad_lanes(a_log)

    def body(xs_ref, b_ref, c_ref, dtr_ref, dtrt_ref, br, bc, ar, ac, y_ref, hp_ref, h_scr):
        @pl.when(pl.program_id(1) == 0)
        def _():
            h_scr[...] = jnp.zeros(h_scr.shape, F32)
        tril, lane, _, dt, cs, cst = _ssd_common(dtr_ref, dtrt_ref, br, bc, ar, ac)
        sub = lax.broadcasted_iota(jnp.int32, (LANES, 1), 0)
        y_acc = [jnp.zeros((Q, LANES), F32) for _ in range(N_PAIRS)]
        h_old = [h_scr[p] for p in range(N_PAIRS)]
        h_new = [jnp.zeros((LANES, SSD_STATE), F32) for _ in range(N_PAIRS)]
        for g in range(SSD_GROUPS):
            bg = b_ref[:, g * SSD_STATE:(g + 1) * SSD_STATE].astype(BF16)
            cg = c_ref[:, g * SSD_STATE:(g + 1) * SSD_STATE].astype(BF16)
            cb = _dot(cg, bg, _NT)
            heads = []
            for j in range(g * HEADS_PER_GROUP, (g + 1) * HEADS_PER_GROUP):
                p, side = j // 2, j % 2
                m = (lane < HEAD_DIM) if side == 0 else (lane >= HEAD_DIM)
                ms = (sub < HEAD_DIM) if side == 0 else (sub >= HEAD_DIM)
                csj, dtj = _rep(cs, j), _rep(dt, j)
                lmat = jnp.exp(jnp.where(tril, csj - cst[j:j + 1, :], NEG))
                xdt = jnp.where(m, xs_ref[:, p * LANES:(p + 1) * LANES] * dtj, 0.0)
                hm = jnp.where(ms, h_old[p], 0.0)
                last = csj[Q - 1:Q, :]
                heads.append(dict(p=p, hm=hm, ecs=jnp.exp(csj), el=jnp.exp(last), gmat=(cb * lmat).astype(BF16),
                                  xdt=xdt.astype(BF16), xd=(xdt * jnp.exp(last - csj)).astype(BF16)))
            for h in heads:
                h.update(ydiag=_dot(h['gmat'], h['xdt'], _NN), ch=_dot(cg, h['hm'].astype(BF16), _NT), sj=_dot(h['xd'], bg, _TN))
            for h in heads:
                y_acc[h['p']] = y_acc[h['p']] + h['ydiag'] + h['ecs'] * h['ch']
                h_new[h['p']] = h_new[h['p']] + h['el'] * h['hm'] + h['sj']
        for p in range(N_PAIRS):
            y_ref[:, p * LANES:(p + 1) * LANES] = y_acc[p]
            hp_ref[p] = h_old[p]
            h_scr[p] = h_new[p]

    return pl.pallas_call(
        body, name="ssd_fwd", grid=(b, nc),
        in_specs=[rows(D_SSD), rows(D_BC), rows(D_BC), rows(LANES), dtt_spec, const((1, LANES)), const((LANES, 1)),
                  const((1, LANES)), const((LANES, 1))],
        out_specs=[rows(D_SSD), state],
        out_shape=[jax.ShapeDtypeStruct((t, D_SSD), F32),
                   jax.ShapeDtypeStruct((b * nc, N_PAIRS, LANES, SSD_STATE), F32)],
        scratch_shapes=[pltpu.VMEM((N_PAIRS, LANES, SSD_STATE), F32)],
        compiler_params=_cparams(("parallel", "arbitrary")),
    )(xs, bm, cm, dtraw, dtraw.T, bias_r, bias_c, alog_r, alog_c)


def ssd_bwd(xs, bm, cm, dtraw, dt_bias, a_log, hprev, dy, b):
    t = xs.shape[0]
    nc = SEQ // Q
    rows, dtt_spec, const, state = _ssd_specs(b, nc, True)
    bias_r, bias_c = _pad_lanes(dt_bias)
    alog_r, alog_c = _pad_lanes(a_log)

    def body(xs_ref, b_ref, c_ref, dtr_ref, dtrt_ref, hp_ref, dy_ref, br, bc, ar, ac,
             dxs_ref, db_ref, dc_ref, ddt_ref, dbias_ref, dalog_ref, dh_scr):
        first = jnp.logical_and(pl.program_id(0) == 0, pl.program_id(1) == 0)

        @pl.when(pl.program_id(1) == 0)
        def _():
            dh_scr[...] = jnp.zeros(dh_scr.shape, F32)

        @pl.when(first)
        def _():
            dbias_ref[...] = jnp.zeros(dbias_ref.shape, F32)
            dalog_ref[...] = jnp.zeros(dalog_ref.shape, F32)
        tril, lane, a_r, dt, cs, cst = _ssd_common(dtr_ref, dtrt_ref, br, bc, ar, ac)
        sub = lax.broadcasted_iota(jnp.int32, (LANES, 1), 0)
        rowq = lax.broadcasted_iota(jnp.int32, (Q, 1), 0)
        triu = (lax.broadcasted_iota(jnp.int32, (Q, Q), 0) <= lax.broadcasted_iota(jnp.int32, (Q, Q), 1)).astype(F32)
        dxs_acc = [jnp.zeros((Q, LANES), F32) for _ in range(N_PAIRS)]
        dh_in = [dh_scr[p] for p in range(N_PAIRS)]
        h_in = [hp_ref[p] for p in range(N_PAIRS)]
        dh_out = [jnp.zeros((LANES, SSD_STATE), F32) for _ in range(N_PAIRS)]
        ddt = jnp.zeros((Q, LANES), F32)
        dalog = jnp.zeros((1, LANES), F32)
        for g in range(SSD_GROUPS):
            gs = slice(g * SSD_STATE, (g + 1) * SSD_STATE)
            bg, cg = b_ref[:, gs].astype(BF16), c_ref[:, gs].astype(BF16)
            cb = _dot(cg, bg, _NT)
            dcb = jnp.zeros((Q, Q), F32)
            dbg = jnp.zeros((Q, SSD_STATE), F32)
            dcg = jnp.zeros((Q, SSD_STATE), F32)
            heads = []
            for j in range(g * HEADS_PER_GROUP, (g + 1) * HEADS_PER_GROUP):
                p, side = j // 2, j % 2
                m = (lane < HEAD_DIM) if side == 0 else (lane >= HEAD_DIM)
                ms = (sub < HEAD_DIM) if side == 0 else (sub >= HEAD_DIM)
                csj, dtj = _rep(cs, j), _rep(dt, j)
                lmat = jnp.exp(jnp.where(tril, csj - cst[j:j + 1, :], NEG))
                x2 = jnp.where(m, xs_ref[:, p * LANES:(p + 1) * LANES], 0.0)
                xdt = x2 * dtj
                dym = jnp.where(m, dy_ref[:, p * LANES:(p + 1) * LANES], 0.0)
                hm = jnp.where(ms, h_in[p], 0.0)
                dhm = jnp.where(ms, dh_in[p], 0.0)
                last = csj[Q - 1:Q, :]
                decay = jnp.exp(last - csj)
                heads.append(dict(j=j, p=p, dtj=dtj, lmat=lmat, x2=x2, hm=hm, dhm=dhm, decay=decay, el=jnp.exp(last),
                                  gmat=cb * lmat, dym=dym.astype(BF16), xdt=xdt.astype(BF16), hmb=hm.astype(BF16),
                                  dhmb=dhm.astype(BF16), dye=dym * jnp.exp(csj), xd=xdt * decay))
            for h in heads:
                dyeb, xdb = h['dye'].astype(BF16), h['xd'].astype(BF16)
                h.update(dg=_dot(h['dym'], h['xdt'], _NT),
                         dxdt=_dot(h['gmat'].astype(BF16), h['dym'], _TN),
                         ch=_dot(cg, h['hmb'], _NT),
                         dcg=_dot(dyeb, h['hmb'], _NN), dhp=_dot(dyeb, cg, _TN),
                         wmat=_dot(bg, h['dhmb'], _NT),
                         dbg=_dot(xdb, h['dhmb'], _NN))
            for h in heads:
                ej = h['dg'] * h['gmat']
                col_sums = jnp.broadcast_to(jnp.sum(ej, axis=0, keepdims=True), (Q, Q)).T
                ddl = jnp.sum(h['xd'] * h['wmat'], axis=1, keepdims=True)
                dlast = jnp.sum(ddl, axis=0, keepdims=True) + h['el'] * jnp.sum(
                    jnp.sum(h['dhm'] * h['hm'], axis=1, keepdims=True), axis=0, keepdims=True)
                h['dcs'] = (jnp.sum(ej, axis=1, keepdims=True) - col_sums + jnp.sum(h['dye'] * h['ch'], axis=1, keepdims=True)
                            - ddl + jnp.where(rowq == Q - 1, dlast, 0.0))
                h['dxdt'] = h['dxdt'] + h['decay'] * h['wmat']
                dcb, dcg, dbg = dcb + h['dg'] * h['lmat'], dcg + h['dcg'], dbg + h['dbg']
                dh_out[h['p']] = dh_out[h['p']] + h['el'] * h['dhm'] + h['dhp']
            for h in heads:
                h['da'] = _dot_exact01(triu, h['dcs'], _NN, True)
            for h in heads:
                j, da = h['j'], h['da']
                aj = jnp.sum(jnp.where(lane == j, a_r, 0.0), axis=1, keepdims=True)
                ddtj = da * aj + jnp.sum(h['dxdt'] * h['x2'], axis=1, keepdims=True)
                ddt = ddt + jnp.where(lane == j, ddtj, 0.0)
                dalog = dalog + jnp.where(lane == j, jnp.sum(da * h['dtj'], axis=0, keepdims=True) * aj, 0.0)
                dxs_acc[h['p']] = dxs_acc[h['p']] + h['dxdt'] * h['dtj']
            dcbb = dcb.astype(BF16)
            dc_ref[:, gs] = dcg + _dot(dcbb, bg, _NN)
            db_ref[:, gs] = dbg + _dot(dcbb, cg, _TN)
        for p in range(N_PAIRS):
            dxs_ref[:, p * LANES:(p + 1) * LANES] = dxs_acc[p]
            dh_scr[p] = dh_out[p]
        ddtraw = ddt * _sigmoid(dtr_ref[...] + br[...])
        ddt_ref[...] = ddtraw
        dbias_ref[...] += jnp.sum(ddtraw, axis=0, keepdims=True)
        dalog_ref[...] += dalog

    return pl.pallas_call(
        body, name="ssd_bwd", grid=(b, nc),
        in_specs=[rows(D_SSD), rows(D_BC), rows(D_BC), rows(LANES), dtt_spec, state, rows(D_SSD), const((1, LANES)),
                  const((LANES, 1)), const((1, LANES)), const((LANES, 1))],
        out_specs=[rows(D_SSD), rows(D_BC), rows(D_BC), rows(LANES), const((1, LANES)), const((1, LANES))],
        out_shape=[jax.ShapeDtypeStruct((t, D_SSD), F32), jax.ShapeDtypeStruct((t, D_BC), F32),
                   jax.ShapeDtypeStruct((t, D_BC), F32), jax.ShapeDtypeStruct((t, LANES), F32),
                   jax.ShapeDtypeStruct((1, LANES), F32), jax.ShapeDtypeStruct((1, LANES), F32)],
        scratch_shapes=[pltpu.VMEM((N_PAIRS, LANES, SSD_STATE), F32)],
        compiler_params=_cparams(("arbitrary", "arbitrary")),
    )(xs, bm, cm, dtraw, dtraw.T, hprev, dy, bias_r, bias_c, alog_r, alog_c)


def _split_w_in(w_in):
    w_dt = jnp.pad(w_in[:, D_QKVZ + D_CONV:], ((0, 0), (0, LANES - N_HEADS)))
    return w_in[:, :D_QKVZ], w_in[:, D_QKVZ:D_QKVZ + D_CONV], w_dt


def mixer_fwd(hb, p, cosv, sinv, b):
    t = hb.shape[0]
    w_a, w_b, w_c = _split_w_in(p['w_in'])
    qkvz = mm("in_qkvz", [(hb, w_a, 'nn')], D_QKVZ)
    xbc = mm("in_xbc", [(hb, w_b, 'nn')], D_CONV)
    dtraw = mm("in_dt", [(hb, w_c, 'nn')], LANES)
    mixed, *lses = attn_fwd(qkvz, cosv, sinv, b)
    attn = attn_norm_fwd(mixed, p['attn_norm_w'])
    xs, bm, cm = conv_fwd(xbc, p['conv_w'], p['conv_b'])
    y, hprev = ssd_fwd(xs, bm, cm, dtraw, p['dt_bias'], p['a_log'], b)
    dskip = jnp.repeat(p['d_skip'].reshape(-1), HEAD_DIM).reshape(1, D_SSD)
    yg, = rowwise("ssd_gate", _gate, [y, xs, Op(qkvz, D_SSD, 3)], [dskip, p['ssd_norm_w']], [(t, D_SSD, BF16)])
    mix = mm("out_proj", [(attn, p['w_out'][:D_ATTN], 'nn'), (yg, p['w_out'][D_ATTN:], 'nn')], D_MODEL)
    res = dict(hb=hb, qkvz=qkvz, xbc=xbc, dtraw=dtraw, mixed=mixed, lses=lses, attn=attn, xs=xs, bm=bm, cm=cm,
               y=y, hprev=hprev, dskip=dskip, yg=yg, cosv=cosv, sinv=sinv)
    return mix, res


def mixer_bwd(r, p, dmix, dh_resid, b):
    t = dmix.shape[0]
    w_a, w_b, w_c = _split_w_in(p['w_in'])
    w_out = p['w_out']
    dattn = mm("out_bwd_dattn", [(dmix, w_out[:D_ATTN], 'nt')], D_ATTN)
    dyg = mm("out_bwd_dyg", [(dmix, w_out[D_ATTN:], 'nt')], D_SSD)
    dw_out = jnp.concatenate([mm_tn("out_bwd_dw_a", r['attn'], dmix, BF16),
                              mm_tn("out_bwd_dw_y", r['yg'], dmix, BF16)], axis=0)

    def gate_bwd(dy_, y_, xs_, z_, ds_, w_):
        _, vjp = jax.vjp(_gate, y_, xs_, z_, ds_, w_)
        return vjp(dy_)

    dy, dxs_a, dz, ddskip, dssd_norm = rowwise(
        "ssd_gate_bwd", gate_bwd, [dyg, r['y'], r['xs'], Op(r['qkvz'], D_SSD, 3)], [r['dskip'], p['ssd_norm_w']],
        [(t, D_SSD, F32), (t, D_SSD, F32), (t, D_SSD, BF16)], accs=[(1, D_SSD), (1, D_SSD)])
    dxs_b, dbm, dcm, ddtraw, ddt_bias, da_log = ssd_bwd(r['xs'], r['bm'], r['cm'], r['dtraw'], p['dt_bias'], p['a_log'],
                                                        r['hprev'], dy, b)
    dxbc, dconv_w, dconv_b = conv_bwd(r['xbc'], p['conv_w'], p['conv_b'], dxs_a, dxs_b, dbm, dcm)
    dmixed, dattn_norm = attn_norm_bwd(dattn, r['mixed'], p['attn_norm_w'])
    dq, dk, dv = attn_bwd(r['qkvz'], r['cosv'], r['sinv'], dmixed, r['mixed'], r['lses'], b)
    wq, wk, wv, wz = (w_a[:, i * D_ATTN:(i + 1) * D_ATTN] for i in range(4))
    dh = mm("in_bwd_dh", [(dq, wq, 'nt'), (dk, wk, 'nt'), (dv, wv, 'nt'), (dz, wz, 'nt'), (dxbc, w_b, 'nt'),
                          (ddtraw, w_c, 'nt')], D_MODEL, add=dh_resid, tn=512)
    h = r['hb']
    dw_in = jnp.concatenate([mm_tn_cat("in_bwd_dw_qkvz", h, [dq, dk, dv, dz], BF16),
                             mm_tn_cat("in_bwd_dw_xbc_dt", h, [dxbc, ddtraw], BF16)[:, :D_CONV + N_HEADS]], axis=1)
    head_sum = lambda v: v.reshape(N_HEADS, HEAD_DIM).sum(axis=1).reshape(1, N_HEADS)
    grads = dict(w_in=dw_in, w_out=dw_out, conv_w=dconv_w, conv_b=dconv_b, dt_bias=ddt_bias[:, :N_HEADS],
                 a_log=da_log[:, :N_HEADS], d_skip=head_sum(ddskip), attn_norm_w=dattn_norm, ssd_norm_w=dssd_norm)
    return dh, grads


FFN1_KEYS = ('ffn1_gate', 'ffn1_up', 'ffn1_down')
FFN2_KEYS = ('ffn2_gate', 'ffn2_up', 'ffn2_down')
MIXER_KEYS = ('w_in', 'conv_w', 'w_out')
FFN_COL = ('ffn1_gate', 'ffn1_up', 'ffn2_gate', 'ffn2_up')
FFN_ROW = ('ffn1_down', 'ffn2_down')
CONV_W_COMM = (8, 2 * LANES)
SMALL = 'small'


def comm_shape(k, shapes):
    if k in FFN_COL:
        return (D_MODEL, FF_PAD)
    if k in FFN_ROW:
        return (FF_PAD, D_MODEL)
    if k == 'conv_w':
        return CONV_W_COMM
    return tuple(shapes[k][1:])


def to_comm(k, vals, shapes):
    a = vals[k].reshape(shapes[k][1:])
    r_, c_ = comm_shape(k, shapes)
    return jnp.pad(a, ((0, r_ - a.shape[0]), (0, c_ - a.shape[1])))


SMALL_ROWS, SMALL_COLS = 16, D_CONV


def pack_small(small):
    rows = [jnp.pad(small[r].reshape(1, -1), ((0, 0), (0, SMALL_COLS - small[r].size))) for r in REPLICATED]
    return jnp.concatenate(rows + [jnp.zeros((SMALL_ROWS - len(rows), SMALL_COLS), F32)], axis=0)


def full_weight(k, g):
    if k in FFN_COL:
        return g
    if k == 'conv_w':
        return jnp.transpose(g[:, :CONV_WIDTH, :D_CONV // N_DEV], (1, 0, 2)).reshape(CONV_WIDTH, D_CONV)
    return g.reshape(N_DEV * g.shape[1], g.shape[2])


def grad_shards(k, g):
    if k in FFN_COL:
        return g
    if k == 'conv_w':
        s = jnp.transpose(g.reshape(CONV_WIDTH, N_DEV, D_CONV // N_DEV), (1, 0, 2))
        return jnp.pad(s, ((0, 0), (0, CONV_W_COMM[0] - CONV_WIDTH), (0, CONV_W_COMM[1] - D_CONV // N_DEV)))
    return g.reshape(N_DEV, g.shape[0] // N_DEV, g.shape[1])


def _flip(v, bit):
    return 1 - v if bit else v


N_PEER_COPIES = N_DEV - 1


def _comm_call(name, body, arrs, out_shape):
    n = len(arrs)
    return pl.pallas_call(
        functools.partial(body, n), name=name, out_shape=out_shape,
        in_specs=[pl.BlockSpec(memory_space=pl.ANY)] * n, out_specs=[pl.BlockSpec(memory_space=pl.ANY)] * n,
        scratch_shapes=[pltpu.SemaphoreType.DMA((n * N_PEER_COPIES,)), pltpu.SemaphoreType.DMA((n * N_PEER_COPIES,)),
                        pltpu.SemaphoreType.DMA((n,))],
    )(*arrs)


def _blk(ref, idx, by_cols):
    if not by_cols:
        return ref.at[idx]
    c = ref.shape[1] // N_DEV
    return ref.at[:, pl.ds(pl.multiple_of(idx * c, LANES), c)]


def _blocked_shape(a, by_cols):
    return (a.shape[0], N_DEV * a.shape[1]) if by_cols else (N_DEV,) + a.shape


def all_gather(arrs, by_cols):
    def body(n, *refs):
        x_refs, out_refs, (send_sems, recv_sems, local_sems) = refs[:n], refs[n:2 * n], refs[2 * n:]
        x, y, c = lax.axis_index("x"), lax.axis_index("y"), lax.axis_index("c")
        me, sibling = (x, y, c), (x, y, 1 - c)
        chips = [(1 - x, y), (x, 1 - y), (1 - x, 1 - y)]

        def copy(a, k, block, to, src=None):
            px, py, pc = block
            dst = _blk(out_refs[a], 4 * px + 2 * py + pc, by_cols[a])
            return pltpu.make_async_remote_copy(
                src_ref=dst if src is None else src, dst_ref=dst, send_sem=send_sems.at[a * N_PEER_COPIES + k],
                recv_sem=recv_sems.at[a * N_PEER_COPIES + k], device_id=to, device_id_type=MESH)

        mine = [pltpu.make_async_copy(x_refs[a], _blk(out_refs[a], 4 * x + 2 * y + c, by_cols[a]), local_sems.at[a])
                for a in range(n)]
        started = []
        for a in range(n):
            mine[a].start()
            first = [copy(a, 0, me, sibling, src=x_refs[a])]
            first += [copy(a, 1 + j, me, (*chip, c), src=x_refs[a]) for j, chip in enumerate(chips)]
            for cp in first:
                cp.start()
            started += first
        for j, chip in enumerate(chips):
            for a in range(n):
                copy(a, 1 + j, (*chip, c), me).wait_recv()
                cp = copy(a, 4 + j, (*chip, c), sibling)
                cp.start()
                started.append(cp)
        for a in range(n):
            copy(a, 0, sibling, me).wait_recv()
            for j, chip in enumerate(chips):
                copy(a, 4 + j, (*chip, 1 - c), me).wait_recv()
        for cp in started:
            cp.wait_send()
        for cp in mine:
            cp.wait()

    return _comm_call("all_gather_weights", body, arrs,
                      [jax.ShapeDtypeStruct(_blocked_shape(a, bc), a.dtype) for a, bc in zip(arrs, by_cols)])


def blocks_to_cols(arrs):
    def body(*refs):
        for i, o in zip(refs[:len(arrs)], refs[len(arrs):]):
            o[...] = i[...]

    return pl.pallas_call(
        body, name="blocks_to_cols", grid=(N_DEV,),
        in_specs=[pl.BlockSpec((None,) + a.shape[1:], lambda p: (p, 0, 0)) for a in arrs],
        out_specs=[pl.BlockSpec(a.shape[1:], lambda p: (0, p)) for a in arrs],
        out_shape=[jax.ShapeDtypeStruct((a.shape[1], N_DEV * a.shape[2]), a.dtype) for a in arrs],
        compiler_params=_cparams(("parallel",)),
    )(*arrs)


def _landing_shape(a, by_cols):
    return (N_DEV, a.shape[0], a.shape[1] // N_DEV) if by_cols else a.shape


_HBM = pl.BlockSpec(memory_space=pltpu.HBM)
_SEM = pl.BlockSpec(memory_space=pltpu.SEMAPHORE)
_EFFECT = pltpu.SideEffectType.DATAFLOW_SIDE_EFFECTING


def _peer(k):
    x, y, c = lax.axis_index("x"), lax.axis_index("y"), lax.axis_index("c")
    return _flip(x, k & 4), _flip(y, k & 2), _flip(c, k & 1)


def _my_index():
    return 4 * lax.axis_index("x") + 2 * lax.axis_index("y") + lax.axis_index("c")


def _split_copies(mode, by_cols, src_refs, land_refs, send_sems, recv_sems):
    me = _my_index()
    out = []
    for a, bc in enumerate(by_cols):
        for k in range(1, N_DEV):
            px, py, pc = _peer(k)
            src = _blk(src_refs[a], 4 * px + 2 * py + pc, bc) if mode == 'scatter' else src_refs[a]
            dst = land_refs[a].at[me] if mode == 'scatter' else _blk(land_refs[a], me, bc)
            out.append(pltpu.make_async_remote_copy(
                src_ref=src, dst_ref=dst, send_sem=send_sems.at[a * N_PEER_COPIES + k - 1],
                recv_sem=recv_sems.at[a * N_PEER_COPIES + k - 1], device_id=(px, py, pc), device_id_type=MESH))
    return out


def exchange_start(name, mode, srcs, by_cols):
    n = len(srcs)
    lands = [lax.empty(_landing_shape(s, bc) if mode == 'scatter' else _blocked_shape(s, bc), s.dtype)
             for s, bc in zip(srcs, by_cols)]

    def body(*refs):
        src_refs, land_refs, send_sems, recv_sems = refs[:n], refs[n:2 * n], refs[2 * n], refs[2 * n + 1]
        for cp in _split_copies(mode, by_cols, src_refs, land_refs, send_sems, recv_sems):
            cp.start()
        refs[-1][...] = jnp.zeros(refs[-1].shape, F32)

    sems = pltpu.SemaphoreType.DMA((n * N_PEER_COPIES,))
    res = pl.pallas_call(
        body, name=name,
        out_shape=(sems, sems, *[pltpu.HBM(a.shape, a.dtype) for a in srcs + lands], jax.ShapeDtypeStruct((8, LANES), F32)),
        in_specs=(_HBM,) * (2 * n), out_specs=(_SEM, _SEM, *(_HBM,) * (2 * n), pl.BlockSpec(memory_space=pltpu.VMEM)),
        input_output_aliases={i: 2 + i for i in range(2 * n)},
        compiler_params=pltpu.CompilerParams(has_side_effects=_EFFECT),
    )(*[pltpu.with_memory_space_constraint(a, pltpu.HBM) for a in srcs + lands])
    return (mode, by_cols, res[:-1]), res[-1]


def exchange_wait(name, handles, after):
    mode, by_cols, (send_sems, recv_sems, *bufs) = handles
    n = len(by_cols)

    def body(*refs):
        src_refs, land_refs, s_sems, r_sems = refs[:n], refs[n:2 * n], refs[2 * n], refs[2 * n + 1]
        for cp in _split_copies(mode, by_cols, src_refs, land_refs, s_sems, r_sems):
            cp.wait_send()
            cp.wait_recv()

    res = pl.pallas_call(
        body, name=name, out_shape=tuple(pltpu.HBM(a.shape, a.dtype) for a in bufs),
        in_specs=(*(_HBM,) * (2 * n), _SEM, _SEM, pl.BlockSpec(memory_space=pl.ANY)), out_specs=(_HBM,) * (2 * n),
        input_output_aliases={i: i for i in range(2 * n)},
        compiler_params=pltpu.CompilerParams(has_side_effects=_EFFECT),
    )(*bufs, send_sems, recv_sems, after)
    me, out = _my_index(), []
    for src, land, bc in zip(res[:n], res[n:], by_cols):
        if mode == 'scatter':
            c = land.shape[2]
            own = lax.dynamic_slice(src, (0, me * c), (src.shape[0], c)) if bc else lax.dynamic_index_in_dim(src, me, 0, False)
            out.append(lax.dynamic_update_slice(land, own[None], (me, 0, 0)))
        elif bc:
            out.append(lax.dynamic_update_slice(land, src, (0, me * src.shape[1])))
        else:
            out.append(lax.dynamic_update_slice(land, src[None], (me, 0, 0)))
    return out


def _adamw_math(g, w, m, v):
    c1 = 1.0 / (1.0 - ADAM_B1 ** ADAM_STEP)
    c2 = 1.0 / (1.0 - ADAM_B2 ** ADAM_STEP)
    m = ADAM_B1 * m + (1.0 - ADAM_B1) * g
    v = ADAM_B2 * v + (1.0 - ADAM_B2) * jnp.square(g)
    return g, -ADAM_LR * ((m * c1) / (jnp.sqrt(v * c2) + ADAM_EPS) + ADAM_WD * w), m, v


def adamw(name, recv, w, m, v, tm):
    _, rows, cols = w.shape
    tm = min(tm, rows)

    def body(*refs):
        g = refs[0][0:tm, 0:cols].astype(F32)
        for s in range(1, N_DEV):
            g = g + refs[s][0:tm, 0:cols].astype(F32)
        res = _adamw_math(g, *[r[...] for r in refs[N_DEV:N_DEV + 3]])
        for r, val in zip(refs[N_DEV + 3:], res):
            r[...] = val

    part = lambda s: pl.BlockSpec((None, recv.shape[1] if tm == rows else tm, recv.shape[2]), lambda i: (s, i, 0))
    tile = pl.BlockSpec((None, tm, cols), lambda i: (0, i, 0))
    return pl.pallas_call(
        body, name=name, grid=(rows // tm,), in_specs=[part(s) for s in range(N_DEV)] + [tile] * 3, out_specs=[tile] * 4,
        out_shape=[jax.ShapeDtypeStruct((1, rows, cols), F32)] * 4, compiler_params=_cparams(("parallel",)),
    )(*[recv] * N_DEV, w, m, v)


def adamw_small(recv, wl, ml, vl):
    n = len(REPLICATED)

    def body(recv_ref, *refs):
        g = recv_ref[0]
        for s in range(1, N_DEV):
            g = g + recv_ref[s]
        for r in range(n):
            w, m, v = (refs[j * n + r][...] for j in range(3))
            for j, val in enumerate(_adamw_math(g[r:r + 1, :w.shape[1]], w, m, v)):
                refs[(3 + j) * n + r][...] = val

    arrs = [d[k].reshape(1, -1) for d in (wl, ml, vl) for k in REPLICATED]
    res = pl.pallas_call(
        body, name="adamw_small", out_shape=[jax.ShapeDtypeStruct(a.shape, F32) for a in arrs[:n]] * 4,
    )(recv, *arrs)
    return [{k: res[j * n + r].reshape(wl[k].shape) for r, k in enumerate(REPLICATED)} for j in range(4)]


ADAMW_TM = {'ffn1_gate': 256, 'ffn1_up': 256, 'ffn2_gate': 256, 'ffn2_up': 256, 'w_in': 32}


def kernel(x, positions, ln1_g, ln1_b, ffn1_gate, ffn1_up, ffn1_down, w_in, conv_w, conv_b, dt_bias, a_log, d_skip, attn_norm_w, ssd_norm_w, w_out, ln2_g, ln2_b, ffn2_gate, ffn2_up, ffn2_down, ln3_g, ln3_b, loss_target, m_ln1_g, m_ln1_b, m_ffn1_gate, m_ffn1_up, m_ffn1_down, m_w_in, m_conv_w, m_conv_b, m_dt_bias, m_a_log, m_d_skip, m_attn_norm_w, m_ssd_norm_w, m_w_out, m_ln2_g, m_ln2_b, m_ffn2_gate, m_ffn2_up, m_ffn2_down, m_ln3_g, m_ln3_b, v_ln1_g, v_ln1_b, v_ffn1_gate, v_ffn1_up, v_ffn1_down, v_w_in, v_conv_w, v_conv_b, v_dt_bias, v_a_log, v_d_skip, v_attn_norm_w, v_ssd_norm_w, v_w_out, v_ln2_g, v_ln2_b, v_ffn2_gate, v_ffn2_up, v_ffn2_down, v_ln3_g, v_ln3_b):
    args = dict(locals())
    wl = {k: args[k] for k in WEIGHTS}
    ml = {k: args["m_" + k] for k in WEIGHTS}
    vl = {k: args["v_" + k] for k in WEIGHTS}
    shapes = {k: wl[k].shape for k in WEIGHTS}
    b, s, dm = x.shape
    t = b * s

    sent = {k: to_comm(k, wl, shapes).astype(F32 if k == 'conv_w' else BF16) for k in SHARDED}
    by_cols = lambda keys: [k in FFN_COL for k in keys]
    gate, up = all_gather([sent['ffn1_gate'], sent['ffn1_up']], [False] * 2)
    (gate, up), sent = lax.optimization_barrier(((gate, up), sent))
    p = dict(zip(('ffn1_gate', 'ffn1_up'), blocks_to_cols([gate, up])))
    gather_down, token_d = exchange_start("gather_ffn1_down_start", 'gather', [sent['ffn1_down']], [False])
    sent['w_in'] = sent['w_in'] + token_d[0, 0].astype(BF16)
    gather_mixer, token_m = exchange_start("gather_mixer_start", 'gather', [sent[k] for k in MIXER_KEYS], by_cols(MIXER_KEYS))
    sent['ffn2_gate'] = sent['ffn2_gate'] + token_m[0, 0].astype(BF16)
    gather_ffn2, token_f = exchange_start("gather_ffn2_start", 'gather', [sent[k] for k in FFN2_KEYS], by_cols(FFN2_KEYS))
    for k in REPLICATED:
        p[k] = wl[k].reshape(1, -1)

    x2 = x.reshape(t, dm)
    cosv, sinv = rope_tables(positions)
    g1, u1, a1, at1 = ffn_gate_up("ffn1_gate_up", x2, p['ffn1_gate'], p['ffn1_up'], after=(token_d, token_m, token_f))
    p['ffn1_down'] = full_weight('ffn1_down', exchange_wait("gather_ffn1_down_wait", gather_down, a1)[0])
    f1, res1 = mm("ffn1_down", [(a1, p['ffn1_down'], 'nn')], D_MODEL), (x2, g1, u1, at1)
    h1, h1b = resid_ln_fwd("ln1", 0.5, x2, f1, p['ln1_g'], p['ln1_b'])
    for k, g in zip(MIXER_KEYS, exchange_wait("gather_mixer_wait", gather_mixer, h1b)):
        p[k] = full_weight(k, g)
    mix, resm = mixer_fwd(h1b, p, cosv, sinv, b)
    h2, h2b = resid_ln_fwd("ln2", 1.0, h1, mix, p['ln2_g'], p['ln2_b'])
    for k, g in zip(FFN2_KEYS, exchange_wait("gather_ffn2_wait", gather_ffn2, h2b)):
        p[k] = full_weight(k, g)
    f2, res3 = ffn_fwd("ffn2", h2b, p['ffn2_gate'], p['ffn2_up'], p['ffn2_down'])

    small, full = {}, {}
    dh2_res, df2, small['ln3_g'], small['ln3_b'], sq = ln_loss_bwd("ln3_loss_bwd", h2, f2, loss_target.reshape(t, dm),
                                                                   p['ln3_g'], p['ln3_b'])
    loss = lax.psum(jnp.sum(sq) * (0.5 / dm), AXES)

    dh2, full['ffn2_gate'], full['ffn2_up'], full['ffn2_down'] = ffn_bwd("ffn2", res3, p['ffn2_gate'], p['ffn2_up'],
                                                                       p['ffn2_down'], df2, dh2_res)
    ffn2_exchange, token = exchange_start("grads_ffn2_start", 'scatter', [grad_shards(k, full[k]) for k in FFN2_KEYS],
                                          by_cols(FFN2_KEYS))
    dh1_res, dmix, small['ln2_g'], small['ln2_b'] = resid_ln_bwd("ln2_bwd", 1.0, h1, mix, p['ln2_g'] + token[:1, :1],
                                                                 p['ln2_b'], dh2)
    dh1, gm = mixer_bwd(resm, p, dmix, dh1_res, b)
    for k in ('conv_b', 'dt_bias', 'a_log', 'd_skip', 'attn_norm_w', 'ssd_norm_w'):
        small[k] = gm[k]
    mixer_exchange, token = exchange_start("grads_mixer_start", 'scatter', [grad_shards(k, gm[k]) for k in MIXER_KEYS],
                                           by_cols(MIXER_KEYS))
    dx_res, df1, small['ln1_g'], small['ln1_b'] = resid_ln_bwd("ln1_bwd", 0.5, x2, f1, p['ln1_g'] + token[:1, :1],
                                                               p['ln1_b'], dh1)
    hb, g, u, at = res1
    small_part = pack_small(small)
    dg, du = ffn_da_act("ffn1_bwd_da_act", df1, p['ffn1_down'], g, u)
    dwd = mm_acc("ffn1_bwd_dwd", at, df1, BF16, after=dg)
    down_exchange, token = exchange_start("grads_ffn1_down_start", 'scatter', [
        grad_shards('ffn1_down', dwd), jnp.broadcast_to(small_part[None], (N_DEV,) + small_part.shape)], [False, False])
    dwg = mm_tn("ffn1_bwd_dwg", hb, dg, BF16, after=token)
    gate_exchange, token = exchange_start("grads_ffn1_gate_start", 'scatter', [grad_shards('ffn1_gate', dwg)], [True])
    dwu = mm_tn("ffn1_bwd_dwu", hb, du, BF16, after=token)
    up_exchange, token = exchange_start("grads_ffn1_up_start", 'scatter', [grad_shards('ffn1_up', dwu)], [True])
    dx = mm("ffn1_bwd_dh", [(dg, p['ffn1_gate'], 'nt'), (du, p['ffn1_up'], 'nt')], D_MODEL, add=dx_res, tn=512, after=token)
    recv = {}
    for keys, name, ex in (((FFN2_KEYS), "grads_ffn2_wait", ffn2_exchange), (MIXER_KEYS, "grads_mixer_wait", mixer_exchange),
                           (('ffn1_down', SMALL), "grads_ffn1_down_wait", down_exchange),
                           (('ffn1_gate',), "grads_ffn1_gate_wait", gate_exchange),
                           (('ffn1_up',), "grads_ffn1_up_wait", up_exchange)):
        recv.update(zip(keys, exchange_wait(name, ex, dx)))
    outs = adamw_small(recv.pop(SMALL), wl, ml, vl)
    for k, r in recv.items():
        for o, a in zip(outs, adamw(f"adamw_{k}", r, wl[k], ml[k], vl[k], ADAMW_TM.get(k, shapes[k][1]))):
            o[k] = a
    return (loss, dx.reshape(b, s, dm), *[o[k] for o in outs for k in WEIGHTS])
```

```python
import functools

import jax
import jax.numpy as jnp
import numpy as np
from jax import lax
from jax.experimental import pallas as pl
from jax.experimental.pallas import tpu as pltpu

F32, BF16 = jnp.float32, jnp.bfloat16
HI = lax.Precision.HIGHEST
MESH = pl.DeviceIdType.MESH
AXES = ("x", "y", "c")
N_DEV = 8

D_MODEL = 1024
SEQ = 2048
HEAD_DIM = 64
N_HEADS = 12
D_ATTN = N_HEADS * HEAD_DIM
DILATIONS = (1, 4, 16)
ATTN_BLOCK = 128
ROPE_THETA = 500000.0
ROPE_DIM = 16
D_SSD = 768
SSD_GROUPS = 4
SSD_STATE = 128
SSD_CHUNK = 128
D_BC = SSD_GROUPS * SSD_STATE
D_CONV = D_SSD + 2 * D_BC
CONV_WIDTH = 4
D_QKVZ = 3 * D_ATTN + D_SSD
D_FF = 2816
ALPHA = 2.0 ** 0.25
LN_EPS = 1e-5
RMS_EPS = 1e-6
ADAM_LR, ADAM_B1, ADAM_B2, ADAM_EPS, ADAM_WD, ADAM_STEP = 0.001, 0.9, 0.999, 1e-08, 0.01, 10

LANES = 128
VMEM_LIMIT = 52 * 1024 * 1024
NEG = -1e30

WEIGHTS = ['ln1_g', 'ln1_b', 'ffn1_gate', 'ffn1_up', 'ffn1_down', 'w_in', 'conv_w', 'conv_b', 'dt_bias', 'a_log',
           'd_skip', 'attn_norm_w', 'ssd_norm_w', 'w_out', 'ln2_g', 'ln2_b', 'ffn2_gate', 'ffn2_up', 'ffn2_down',
           'ln3_g', 'ln3_b']
COL_SHARDED = ('ffn1_gate', 'ffn1_up', 'conv_w', 'ffn2_gate', 'ffn2_up')
ROW_SHARDED = ('ffn1_down', 'w_in', 'w_out', 'ffn2_down')
SHARDED = tuple(n for n in WEIGHTS if n in COL_SHARDED or n in ROW_SHARDED)
REPLICATED = tuple(n for n in WEIGHTS if n not in SHARDED)
FF_SHARD = D_FF // N_DEV
FF_PAD = -(-FF_SHARD // LANES) * LANES


def _cparams(sem=None):
    return pltpu.CompilerParams(dimension_semantics=sem, vmem_limit_bytes=VMEM_LIMIT)


def _tile(n, prefs):
    for p in prefs:
        if n % p == 0:
            return p
    return n


class Op:
    def __init__(self, arr, bw=None, cb=0, ro=0):
        self.arr, self.bw, self.cb, self.ro = arr, (arr.shape[1] if bw is None else bw), cb, ro


def _op(a):
    return a if isinstance(a, Op) else Op(a)


def rowwise(name, fn, ins, consts, outs, accs=(), tm=256):
    ins = [_op(a) for a in ins]
    rows = outs[0][0]
    n_in, n_c, n_o, n_a = len(ins), len(consts), len(outs), len(accs)
    tm = min(tm, rows)
    assert rows % tm == 0, (name, rows, tm)

    def body(*refs):
        vals = [r[...].astype(F32) for r in refs[:n_in + n_c]]
        res = fn(*vals)
        res = res if isinstance(res, (tuple, list)) else (res,)
        o_refs = refs[n_in + n_c:n_in + n_c + n_o]
        a_refs = refs[n_in + n_c + n_o:]
        for r, v in zip(o_refs, res[:n_o]):
            r[...] = v.astype(r.dtype)
        if n_a:
            @pl.when(pl.program_id(0) == 0)
            def _():
                for r in a_refs:
                    r[...] = jnp.zeros(r.shape, r.dtype)
            for r, v in zip(a_refs, res[n_o:]):
                r[...] += v

    in_specs = [pl.BlockSpec((tm, o.bw), functools.partial(lambda i, o: (i + o.ro, o.cb), o=o)) for o in ins]
    in_specs += [pl.BlockSpec(c.shape, functools.partial(lambda i, nd: (0,) * nd, nd=c.ndim)) for c in consts]
    out_specs = [pl.BlockSpec((tm, w), lambda i: (i, 0)) for (_, w, _) in outs]
    out_specs += [pl.BlockSpec(s, functools.partial(lambda i, nd: (0,) * nd, nd=len(s))) for s in accs]
    out_shape = [jax.ShapeDtypeStruct((r, w), dt) for (r, w, dt) in outs]
    out_shape += [jax.ShapeDtypeStruct(s, F32) for s in accs]
    res = pl.pallas_call(
        body, name=name, grid=(rows // tm,), in_specs=in_specs, out_specs=out_specs, out_shape=out_shape,
        compiler_params=_cparams(("arbitrary",) if n_a else ("parallel",)),
    )(*[o.arr for o in ins], *consts)
    return res


MM_TM = 1024
MM_TN = (1024, 896, 768, 512, 256, 128)
_NT = (((1,), (1,)), ((), ()))
_NN = (((1,), (0,)), ((), ()))
_TN = (((0,), (0,)), ((), ()))


def _dot(a, b, dn, precision=None):
    return lax.dot_general(a, b, dn, preferred_element_type=F32, precision=precision)


def _mm_specs(name, pairs, n_out, tm, tn):
    in_specs, args = [], []
    for a, b, mode in pairs:
        o = _op(a)
        in_specs.append(pl.BlockSpec((tm, o.bw), functools.partial(lambda j, i, o: (i, o.cb), o=o)))
        args.append(o.arr)
        if mode == 'nn':
            assert b.shape == (o.bw, n_out), (name, b.shape, o.bw, n_out)
            in_specs.append(pl.BlockSpec((o.bw, tn), lambda j, i: (0, j)))
        else:
            assert b.shape == (n_out, o.bw), (name, b.shape, o.bw, n_out)
            in_specs.append(pl.BlockSpec((tn, o.bw), lambda j, i: (j, 0)))
        args.append(b)
    return in_specs, args


def _mm_acc(refs, pairs):
    acc = None
    for k, (_, _, mode) in enumerate(pairs):
        d = _dot(refs[2 * k][...].astype(BF16), refs[2 * k + 1][...].astype(BF16), _NN if mode == 'nn' else _NT)
        acc = d if acc is None else acc + d
    return acc


def mm(name, pairs, n_out, add=None, out_dtype=F32, tm=MM_TM, tn=None, after=None):
    m = _op(pairs[0][0]).arr.shape[0]
    tn = tn or _tile(n_out, MM_TN)
    n_p = len(pairs)

    def body(*refs):
        acc = _mm_acc(refs, pairs)
        if add is not None:
            acc = acc + refs[2 * n_p][...]
        refs[-1][...] = acc.astype(refs[-1].dtype)

    in_specs, args = _mm_specs(name, pairs, n_out, tm, tn)
    tile = pl.BlockSpec((tm, tn), lambda j, i: (i, j))
    if add is not None:
        in_specs.append(tile)
        args.append(add)
    if after is not None:
        in_specs.append(pl.BlockSpec(memory_space=pl.ANY))
        args.append(after)
    return pl.pallas_call(
        body, name=name, grid=(n_out // tn, m // tm), in_specs=in_specs, out_specs=tile,
        out_shape=jax.ShapeDtypeStruct((m, n_out), out_dtype),
        compiler_params=_cparams(("parallel", "parallel")),
    )(*args)


def mm_tn(name, a, b, out_dtype=F32, tt=1024, after=None):
    a, b = _op(a), _op(b)
    t = a.arr.shape[0]
    k, n = a.bw, b.bw
    tk = _tile(k, (512, 896, 768, 256, 128))
    tn = _tile(n, (3072, 1792) + MM_TN)
    tt = min(tt, t)
    n_t = t // tt
    order = [] if after is None else [after]

    def body(a_ref, b_ref, *rest):
        o_ref, acc_ref = rest[-2:]
        s = pl.program_id(2)
        d = _dot(a_ref[...].astype(BF16), b_ref[...].astype(BF16), _TN)

        @pl.when(s == 0)
        def _():
            acc_ref[...] = d

        @pl.when(s > 0)
        def _():
            acc_ref[...] += d

        @pl.when(s == n_t - 1)
        def _():
            o_ref[...] = acc_ref[...].astype(o_ref.dtype)

    return pl.pallas_call(
        body, name=name, grid=(k // tk, n // tn, n_t),
        in_specs=[pl.BlockSpec((tt, tk), functools.partial(lambda kk, nn, s, o: (s, o.cb * (o.bw // tk) + kk), o=a)),
                  pl.BlockSpec((tt, tn), functools.partial(lambda kk, nn, s, o: (s, o.cb * (o.bw // tn) + nn), o=b))]
        + [pl.BlockSpec(memory_space=pl.ANY) for _ in order],
        out_specs=pl.BlockSpec((tk, tn), lambda kk, nn, s: (kk, nn)),
        out_shape=jax.ShapeDtypeStruct((k, n), out_dtype),
        scratch_shapes=[pltpu.VMEM((tk, tn), F32)],
        compiler_params=_cparams(("parallel", "parallel", "arbitrary")),
    )(a.arr, b.arr, *order)


def _sigmoid(x):
    return 1.0 / (1.0 + jnp.exp(-x))


def _silu(x):
    return x * _sigmoid(x)


def _softplus(x):
    return jnp.maximum(x, 0.0) + jnp.log(1.0 + jnp.exp(-jnp.abs(x)))


def _act(g, u):
    return _silu(g) * u


def _resid_ln(scale, h, branch, g, b):
    r = ALPHA * h + scale * branch
    mu = jnp.mean(r, axis=-1, keepdims=True)
    var = jnp.mean(jnp.square(r - mu), axis=-1, keepdims=True)
    return (r - mu) * lax.rsqrt(var + LN_EPS) * g + b


def _rms(t, w):
    return t * lax.rsqrt(jnp.mean(t * t, axis=-1, keepdims=True) + RMS_EPS) * w


def _branch_weights(l1, l2, l3):
    m = jnp.maximum(jnp.maximum(l1, l2), l3)
    e1, e2, e3 = jnp.exp(l1 - m), jnp.exp(l2 - m), jnp.exp(l3 - m)
    inv = 1.0 / (e1 + e2 + e3)
    return e1 * inv, e2 * inv, e3 * inv


def _gate(y, xs, z, dskip, w):
    return _rms((y + dskip * xs) * _silu(z), w)


def _rot(x):
    d = lax.broadcasted_iota(jnp.int32, x.shape, 1) % HEAD_DIM
    up = pltpu.roll(x, x.shape[1] - ROPE_DIM // 2, 1)
    down = jnp.where(d < ROPE_DIM, pltpu.roll(x, ROPE_DIM // 2, 1), 0.0)
    return jnp.where(d < ROPE_DIM // 2, up, down)


def ffn_gate_up(name, h, wg, wu, after=()):
    m, nf = h.shape[0], wg.shape[1]
    tn = _tile(nf, MM_TN)

    def body(h_ref, g_w, u_w, *rest):
        g_ref, u_ref, a_ref, at_ref = rest[-4:]
        hb = h_ref[...].astype(BF16)
        g = _dot(hb, g_w[...].astype(BF16), _NN)
        u = _dot(hb, u_w[...].astype(BF16), _NN)
        g_ref[...] = g.astype(g_ref.dtype)
        u_ref[...] = u.astype(u_ref.dtype)
        a = _act(g, u)
        a_ref[...] = a.astype(a_ref.dtype)
        at_ref[...] = a.T.astype(at_ref.dtype)

    in_specs, args = _mm_specs(name, [(h, wg, 'nn')], nf, MM_TM, tn)
    in_specs.append(in_specs[1])
    in_specs += [pl.BlockSpec(memory_space=pl.ANY) for _ in after]
    tile = pl.BlockSpec((MM_TM, tn), lambda j, i: (i, j))
    return pl.pallas_call(
        body, name=name, grid=(nf // tn, m // MM_TM), in_specs=in_specs,
        out_specs=[tile] * 3 + [pl.BlockSpec((tn, MM_TM), lambda j, i: (j, i))],
        out_shape=[jax.ShapeDtypeStruct((m, nf), BF16)] * 3 + [jax.ShapeDtypeStruct((nf, m), BF16)],
        compiler_params=_cparams(("parallel", "parallel")),
    )(*args, wu, *after)


def mm_tn_cat(name, a, bs, out_dtype=F32, tt=1024):
    t, k = a.shape
    widths = [b.shape[1] for b in bs]
    n, tk, tt = sum(widths), _tile(k, (512, 256, 128)), min(tt, t)
    n_t = t // tt

    def body(a_ref, *rest):
        b_refs, o_ref, acc_ref = rest[:len(bs)], rest[-2], rest[-1]
        s = pl.program_id(1)
        at = a_ref[...].astype(BF16)
        d = jnp.concatenate([_dot(at, b[...].astype(BF16), _TN) for b in b_refs], axis=1)

        @pl.when(s == 0)
        def _():
            acc_ref[...] = d

        @pl.when(s > 0)
        def _():
            acc_ref[...] += d

        @pl.when(s == n_t - 1)
        def _():
            o_ref[...] = acc_ref[...].astype(o_ref.dtype)

    return pl.pallas_call(
        body, name=name, grid=(k // tk, n_t),
        in_specs=[pl.BlockSpec((tt, tk), lambda kk, s: (s, kk))] + [pl.BlockSpec((tt, w), lambda kk, s: (s, 0)) for w in widths],
        out_specs=pl.BlockSpec((tk, n), lambda kk, s: (kk, 0)),
        out_shape=jax.ShapeDtypeStruct((k, n), out_dtype), scratch_shapes=[pltpu.VMEM((tk, n), F32)],
        compiler_params=_cparams(("parallel", "arbitrary")),
    )(a, *bs)


def mm_acc(name, a, b, out_dtype=F32, tt=1024, after=None):
    k, t = a.shape
    n = b.shape[1]
    tk, tn, tt = _tile(k, (1024, 512, 256, 128)), _tile(n, MM_TN), min(tt, t)
    n_t = t // tt
    order = [] if after is None else [after]

    def body(a_ref, b_ref, *rest):
        o_ref, acc_ref = rest[-2:]
        s = pl.program_id(2)
        d = _dot(a_ref[...].astype(BF16), b_ref[...].astype(BF16), _NN)

        @pl.when(s == 0)
        def _():
            acc_ref[...] = d

        @pl.when(s > 0)
        def _():
            acc_ref[...] += d

        @pl.when(s == n_t - 1)
        def _():
            o_ref[...] = acc_ref[...].astype(o_ref.dtype)

    return pl.pallas_call(
        body, name=name, grid=(k // tk, n // tn, n_t),
        in_specs=[pl.BlockSpec((tk, tt), lambda kk, nn, s: (kk, s)), pl.BlockSpec((tt, tn), lambda kk, nn, s: (s, nn))]
        + [pl.BlockSpec(memory_space=pl.ANY) for _ in order],
        out_specs=pl.BlockSpec((tk, tn), lambda kk, nn, s: (kk, nn)),
        out_shape=jax.ShapeDtypeStruct((k, n), out_dtype), scratch_shapes=[pltpu.VMEM((tk, tn), F32)],
        compiler_params=_cparams(("parallel", "parallel", "arbitrary")),
    )(a, b, *order)


def ffn_da_act(name, df, wd, g, u):
    m, nf = df.shape[0], wd.shape[0]
    tn = _tile(nf, MM_TN)

    def body(df_ref, w_ref, g_ref, u_ref, dg_ref, du_ref):
        da = _dot(df_ref[...].astype(BF16), w_ref[...].astype(BF16), _NT)
        g, u = g_ref[...].astype(F32), u_ref[...].astype(F32)
        sig = _sigmoid(g)
        gs = g * sig
        dg_ref[...] = (da * u * (sig + gs * (1.0 - sig))).astype(dg_ref.dtype)
        du_ref[...] = (da * gs).astype(du_ref.dtype)

    in_specs, args = _mm_specs(name, [(df, wd, 'nt')], nf, MM_TM, tn)
    tile = pl.BlockSpec((MM_TM, tn), lambda j, i: (i, j))
    return pl.pallas_call(
        body, name=name, grid=(nf // tn, m // MM_TM), in_specs=in_specs + [tile, tile], out_specs=[tile] * 2,
        out_shape=[jax.ShapeDtypeStruct((m, nf), BF16)] * 2, compiler_params=_cparams(("parallel", "parallel")),
    )(*args, g, u)


def resid_ln_fwd(name, scale, h, branch, ln_g, ln_b):
    t = h.shape[0]

    def fn(*a):
        y = _resid_ln(scale, *a)
        return y, y

    return rowwise(name, fn, [h, branch], [ln_g, ln_b], [(t, D_MODEL, F32), (t, D_MODEL, BF16)], tm=512)


def ffn_fwd(tag, hb, wg, wu, wd, after=()):
    g, u, a, at = ffn_gate_up(f"{tag}_gate_up", hb, wg, wu, after)
    f = mm(f"{tag}_down", [(a, wd, 'nn')], D_MODEL, out_dtype=BF16)
    return f, (hb, g, u, at)


def ln_loss_bwd(name, h, branch, target, ln_g, ln_b):
    t, dm = h.shape

    def fn(h_, br_, tgt, g_, b_):
        y, vjp = jax.vjp(functools.partial(_resid_ln, 0.5), h_, br_, g_, b_)
        e = y - tgt
        return (*vjp(e * (1.0 / dm)), jnp.sum(e * e, axis=0, keepdims=True))

    return rowwise(name, fn, [h, branch, target], [ln_g, ln_b], [(t, dm, F32), (t, dm, BF16)],
                   accs=[(1, dm), (1, dm), (1, dm)], tm=512)


def resid_ln_bwd(name, scale, h, branch, ln_g, ln_b, dout, extra=None):
    t = h.shape[0]

    def fn(h_, br_, do_, *rest):
        g_, b_ = rest[-2], rest[-1]
        _, vjp = jax.vjp(functools.partial(_resid_ln, scale), h_, br_, g_, b_)
        dh, dbr, dg, db = vjp(do_)
        if extra is not None:
            dh = dh + rest[0]
        return dh, dbr, dg, db

    ins = [h, branch, dout] + ([extra] if extra is not None else [])
    return rowwise(name, fn, ins, [ln_g, ln_b], [(t, D_MODEL, F32), (t, D_MODEL, BF16)],
                   accs=[(1, D_MODEL), (1, D_MODEL)], tm=512)


def ffn_bwd(tag, res, wg, wu, wd, df, dh_resid):
    hb, g, u, at = res
    dg, du = ffn_da_act(f"{tag}_bwd_da_act", df, wd, g, u)
    dwd = mm_acc(f"{tag}_bwd_dwd", at, df, BF16)
    dh = mm(f"{tag}_bwd_dh", [(dg, wg, 'nt'), (du, wu, 'nt')], D_MODEL, add=dh_resid, tn=512)
    dwg = mm_tn(f"{tag}_bwd_dwg", hb, dg, BF16)
    dwu = mm_tn(f"{tag}_bwd_dwu", hb, du, BF16)
    return dh, dwg, dwu, dwd


def rope_tables(positions):
    inv_freq = ROPE_THETA ** (-jnp.arange(0, ROPE_DIM, 2, dtype=F32) / ROPE_DIM)
    ang = positions.reshape(-1, 1).astype(F32) * inv_freq
    c, s = jnp.cos(ang), jnp.sin(ang)
    t = ang.shape[0]
    cosv = jnp.concatenate([c, c, jnp.ones((t, HEAD_DIM - ROPE_DIM), F32)], axis=1)
    sinv = jnp.concatenate([-s, s, jnp.zeros((t, HEAD_DIM - ROPE_DIM), F32)], axis=1)
    return jnp.tile(cosv, (1, 2)), jnp.tile(sinv, (1, 2))


def _pair_masks():
    lane = lax.broadcasted_iota(jnp.int32, (1, LANES), 1)
    return (lane < HEAD_DIM, lane >= HEAD_DIM)


def _band_masks():
    row = lax.broadcasted_iota(jnp.int32, (ATTN_BLOCK, ATTN_BLOCK), 0)
    col = lax.broadcasted_iota(jnp.int32, (ATTN_BLOCK, ATTN_BLOCK), 1)
    return col >= row, col <= row


def _residue_blocks():
    out = []
    for g, d in enumerate(DILATIONS):
        for r in range(d):
            for i in range(SEQ // d // ATTN_BLOCK):
                rows = lambda j: pl.ds(r + j * ATTN_BLOCK * d, ATTN_BLOCK, stride=d) if d > 1 else pl.ds(j * ATTN_BLOCK, ATTN_BLOCK)
                out.append((g, rows(i), rows(i - 1) if i > 0 else None))
    return out


N_HEAD_PAIRS = D_ATTN // LANES
SCALE = HEAD_DIM ** -0.5
ATTN_GROUP = 4


def _block_operands(qr, kr, v_ref, cur, prev):
    prev_ok, cur_ok = _band_masks()
    if prev is None:
        return qr[cur, :], kr[cur, :].astype(BF16), v_ref[cur, :], cur_ok
    kcat = jnp.concatenate([kr[prev, :], kr[cur, :]], axis=0).astype(BF16)
    vcat = jnp.concatenate([v_ref[prev, :], v_ref[cur, :]], axis=0)
    return qr[cur, :], kcat, vcat, jnp.concatenate([prev_ok, cur_ok], axis=1)


def _attn_specs(b):
    col = lambda cb: pl.BlockSpec((SEQ, LANES), lambda bb, hp: (bb, cb + hp))
    tab = pl.BlockSpec((SEQ, LANES), lambda bb, hp: (bb, 0))
    return col, tab


def attn_fwd(qkvz, cosv, sinv, b):
    t = qkvz.shape[0]
    col, tab = _attn_specs(b)
    blocks = _residue_blocks()

    def body(q_ref, k_ref, v_ref, c_ref, s_ref, o_ref, l1_ref, l2_ref, l3_ref, qr, kr, o1, o2, o3):
        l_refs, o_scr = (l1_ref, l2_ref, l3_ref), (o1, o2, o3)
        c, s = c_ref[...], s_ref[...]
        q, k = q_ref[...], k_ref[...]
        qr[...] = q * c + _rot(q) * s
        kr[...] = k * c + _rot(k) * s
        masks = _pair_masks()
        for lo in range(0, len(blocks), ATTN_GROUP):
            chains = []
            for g, cur, prev in blocks[lo:lo + ATTN_GROUP]:
                q2, kcat, vcat, ok = _block_operands(qr, kr, v_ref, cur, prev)
                for m in masks:
                    qm = jnp.where(m, q2, 0.0).astype(BF16)
                    chains.append(dict(g=g, cur=cur, m=m, v=jnp.where(m, vcat, 0.0).astype(BF16),
                                       s=jnp.where(ok, _dot(qm, kcat, _NT) * SCALE, NEG)))
            for ch in chains:
                mx = jnp.max(ch['s'], axis=1, keepdims=True)
                p = jnp.exp(ch['s'] - mx)
                den = jnp.sum(p, axis=1, keepdims=True)
                ch.update(p=p.astype(BF16), inv=1.0 / den, lse=mx + jnp.log(den))
            for ch in chains:
                ch['o'] = _dot(ch['p'], ch['v'], _NN) * ch['inv']
            for c0, c1 in zip(chains[0::2], chains[1::2]):
                o_scr[c0['g']][c0['cur'], :] = c0['o'] + c1['o']
                l_refs[c0['g']][c0['cur'], :] = jnp.where(c0['m'], c0['lse'], c1['lse'])
        w1, w2, w3 = _branch_weights(l1_ref[...], l2_ref[...], l3_ref[...])
        o_ref[...] = w1 * o1[...] + w2 * o2[...] + w3 * o3[...]

    shp = jax.ShapeDtypeStruct((t, D_ATTN), F32)
    return pl.pallas_call(
        body, name="attn_fwd", grid=(b, N_HEAD_PAIRS),
        in_specs=[col(0), col(N_HEAD_PAIRS), col(2 * N_HEAD_PAIRS), tab, tab],
        out_specs=[col(0)] * 4, out_shape=[shp] * 4,
        scratch_shapes=[pltpu.VMEM((SEQ, LANES), F32)] * 5,
        compiler_params=_cparams(("parallel", "parallel")),
    )(qkvz, qkvz, qkvz, cosv, sinv)


def attn_bwd(qkvz, cosv, sinv, dmix, mixed, lses, b):
    t = qkvz.shape[0]
    col, tab = _attn_specs(b)
    blocks = _residue_blocks()
    hd = np.arange(LANES) // HEAD_DIM
    head_ones = jnp.asarray((hd[:, None] == hd[None, :]).astype(np.float32))

    def body(q_ref, k_ref, v_ref, c_ref, s_ref, dm_ref, mx_ref, l1_ref, l2_ref, l3_ref, ones_ref,
             dq_out, dk_out, dv_out, qr, kr, do1, do2, do3, dd1, dd2, dd3, dq_ref, dk_ref, dv_ref):
        l_refs, do_scr, dd_scr = (l1_ref, l2_ref, l3_ref), (do1, do2, do3), (dd1, dd2, dd3)
        c, s = c_ref[...], s_ref[...]
        q, k = q_ref[...], k_ref[...]
        qr[...] = q * c + _rot(q) * s
        kr[...] = k * c + _rot(k) * s
        dm = dm_ref[...]
        tot = _dot(dm * mx_ref[...], ones_ref[...], _NN, HI)
        for w, do_g, dd_g in zip(_branch_weights(l1_ref[...], l2_ref[...], l3_ref[...]), do_scr, dd_scr):
            do_g[...] = w * dm
            dd_g[...] = w * tot
        dq_ref[...] = jnp.zeros((SEQ, LANES), F32)
        dk_ref[...] = jnp.zeros((SEQ, LANES), F32)
        dv_ref[...] = jnp.zeros((SEQ, LANES), F32)
        masks = _pair_masks()
        for lo in range(0, len(blocks), ATTN_GROUP):
            chains = []
            for g, cur, prev in blocks[lo:lo + ATTN_GROUP]:
                q2, kcat, vcat, ok = _block_operands(qr, kr, v_ref, cur, prev)
                vcat = vcat.astype(BF16)
                do2_, l2, dd2_ = do_scr[g][cur, :], l_refs[g][cur, :], dd_scr[g][cur, :]
                l2s, dd2s = pltpu.roll(l2, HEAD_DIM, 1), pltpu.roll(dd2_, HEAD_DIM, 1)
                for m in masks:
                    qm = jnp.where(m, q2, 0.0).astype(BF16)
                    dom = jnp.where(m, do2_, 0.0).astype(BF16)
                    lrep, ddrep = jnp.where(m, l2, l2s), jnp.where(m, dd2_, dd2s)
                    if prev is not None:
                        lrep, ddrep = jnp.concatenate([lrep, lrep], axis=1), jnp.concatenate([ddrep, ddrep], axis=1)
                    chains.append(dict(cur=cur, prev=prev, qm=qm, dom=dom, km=jnp.where(m, kcat, 0), lrep=lrep, ddrep=ddrep,
                                       s=jnp.where(ok, _dot(qm, kcat, _NT) * SCALE, NEG), dp=_dot(dom, vcat, _NT)))
            for ch in chains:
                p = jnp.exp(ch['s'] - ch['lrep'])
                ch.update(p=p.astype(BF16), ds=(p * (ch['dp'] - ch['ddrep']) * SCALE).astype(BF16))
            for ch in chains:
                ch.update(dq=_dot(ch['ds'], ch['km'], _NN), dk=_dot(ch['ds'], ch['qm'], _TN), dv=_dot(ch['p'], ch['dom'], _TN))
            for c0, c1 in zip(chains[0::2], chains[1::2]):
                cur, prev = c0['cur'], c0['prev']
                dk, dv = c0['dk'] + c1['dk'], c0['dv'] + c1['dv']
                dq_ref[cur, :] += c0['dq'] + c1['dq']
                if prev is None:
                    dk_ref[cur, :] += dk
                    dv_ref[cur, :] += dv
                else:
                    dk_ref[prev, :] += dk[:ATTN_BLOCK]
                    dv_ref[prev, :] += dv[:ATTN_BLOCK]
                    dk_ref[cur, :] += dk[ATTN_BLOCK:]
                    dv_ref[cur, :] += dv[ATTN_BLOCK:]
        dq, dk = dq_ref[...], dk_ref[...]
        dq_out[...] = (dq * c + _rot(dq * s)).astype(dq_out.dtype)
        dk_out[...] = (dk * c + _rot(dk * s)).astype(dk_out.dtype)
        dv_out[...] = dv_ref[...].astype(dv_out.dtype)

    shp = jax.ShapeDtypeStruct((t, D_ATTN), BF16)
    return pl.pallas_call(
        body, name="attn_bwd", grid=(b, N_HEAD_PAIRS),
        in_specs=[col(0), col(N_HEAD_PAIRS), col(2 * N_HEAD_PAIRS), tab, tab, col(0), col(0), col(0), col(0), col(0),
                  pl.BlockSpec((LANES, LANES), lambda bb, hp: (0, 0))],
        out_specs=[col(0)] * 3, out_shape=[shp] * 3,
        scratch_shapes=[pltpu.VMEM((SEQ, LANES), F32)] * 11,
        compiler_params=_cparams(("parallel", "parallel")),
    )(qkvz, qkvz, qkvz, cosv, sinv, dmix, mixed, *lses, head_ones)


def attn_norm_fwd(mixed, norm_w):
    return rowwise("attn_norm", _rms, [mixed], [norm_w], [(mixed.shape[0], D_ATTN, BF16)])[0]


def attn_norm_bwd(dout, mixed, norm_w):
    def fn(dy, mx, w):
        _, vjp = jax.vjp(_rms, mx, w)
        return vjp(dy)

    return rowwise("attn_norm_bwd", fn, [dout, mixed], [norm_w], [(dout.shape[0], D_ATTN, F32)], accs=[(1, D_ATTN)])


CONV_TM = 256
HALO = 8


def _conv_columns(refs):
    xs_ref, bm_ref, cm_ref = refs
    out = []
    for c in range(D_CONV // LANES):
        lo = c * LANES
        ref, base = (xs_ref, 0) if lo < D_SSD else (bm_ref, D_SSD) if lo < D_SSD + D_BC else (cm_ref, D_SSD + D_BC)
        out.append((slice(lo, lo + LANES), (ref, slice(lo - base, lo - base + LANES))))
    return out


def _conv_taps(scr, w_ref, cs, first_row, step, tm):
    acc = None
    for k in range(CONV_WIDTH):
        term = w_ref[k:k + 1, cs] * scr[pl.ds(first_row + step * k, tm), cs]
        acc = term if acc is None else acc + term
    return acc


def conv_fwd(u, w, bias):
    t = u.shape[0]
    tm, per_seq = CONV_TM, SEQ // CONV_TM

    def body(u_ref, h_ref, w_ref, b_ref, xs_ref, bm_ref, cm_ref, scr):
        first = pl.program_id(0) % per_seq == 0
        scr[0:HALO, :] = jnp.where(first, 0.0, h_ref[...])
        scr[HALO:, :] = u_ref[...]
        for cs, (o_ref, os_) in _conv_columns((xs_ref, bm_ref, cm_ref)):
            o_ref[:, os_] = _silu(_conv_taps(scr, w_ref, cs, HALO - CONV_WIDTH + 1, 1, tm) + b_ref[:, cs])

    return pl.pallas_call(
        body, name="conv_fwd", grid=(t // tm,),
        in_specs=[pl.BlockSpec((tm, D_CONV), lambda i: (i, 0)),
                  pl.BlockSpec((HALO, D_CONV), lambda i: (jnp.maximum(i * (tm // HALO) - 1, 0), 0)),
                  pl.BlockSpec((CONV_WIDTH, D_CONV), lambda i: (0, 0)), pl.BlockSpec((1, D_CONV), lambda i: (0, 0))],
        out_specs=[pl.BlockSpec((tm, D_SSD), lambda i: (i, 0)), pl.BlockSpec((tm, D_BC), lambda i: (i, 0)),
                   pl.BlockSpec((tm, D_BC), lambda i: (i, 0))],
        out_shape=[jax.ShapeDtypeStruct((t, D_SSD), F32), jax.ShapeDtypeStruct((t, D_BC), F32),
                   jax.ShapeDtypeStruct((t, D_BC), F32)],
        scratch_shapes=[pltpu.VMEM((tm + HALO, D_CONV), F32)],
        compiler_params=_cparams(("parallel",)),
    )(u, u, w, bias)


def conv_bwd(u, w, bias, dxs_a, dxs_b, dbm, dcm):
    t = u.shape[0]
    tm, per_seq = CONV_TM, SEQ // CONV_TM
    n_tiles = t // tm

    def body1(u_ref, h_ref, dxs_ref, dxs2_ref, dbm_ref, dcm_ref, w_ref, b_ref, dz_ref, dw_ref, db_ref, scr):
        i = pl.program_id(0)
        first = i % per_seq == 0
        scr[0:HALO, :] = jnp.where(first, 0.0, h_ref[...])
        scr[HALO:, :] = u_ref[...]

        @pl.when(i == 0)
        def _():
            dw_ref[...] = jnp.zeros(dw_ref.shape, F32)
            db_ref[...] = jnp.zeros(db_ref.shape, F32)
        for cs, (g_ref, gs) in _conv_columns((dxs_ref, dbm_ref, dcm_ref)):
            acc = _conv_taps(scr, w_ref, cs, HALO - CONV_WIDTH + 1, 1, tm) + b_ref[:, cs]
            sig = _sigmoid(acc)
            dy = g_ref[:, gs] + dxs2_ref[:, gs] if g_ref is dxs_ref else g_ref[:, gs]
            dz = dy * sig * (1.0 + acc * (1.0 - sig))
            dz_ref[:, cs] = dz
            db_ref[:, cs] += jnp.sum(dz, axis=0, keepdims=True)
            for k in range(CONV_WIDTH):
                dw_ref[k:k + 1, cs] += jnp.sum(dz * scr[pl.ds(HALO - CONV_WIDTH + 1 + k, tm), cs], axis=0, keepdims=True)

    dz, dw, db = pl.pallas_call(
        body1, name="conv_bwd_dz", grid=(n_tiles,),
        in_specs=[pl.BlockSpec((tm, D_CONV), lambda i: (i, 0)),
                  pl.BlockSpec((HALO, D_CONV), lambda i: (jnp.maximum(i * (tm // HALO) - 1, 0), 0)),
                  pl.BlockSpec((tm, D_SSD), lambda i: (i, 0)), pl.BlockSpec((tm, D_SSD), lambda i: (i, 0)),
                  pl.BlockSpec((tm, D_BC), lambda i: (i, 0)), pl.BlockSpec((tm, D_BC), lambda i: (i, 0)),
                  pl.BlockSpec((CONV_WIDTH, D_CONV), lambda i: (0, 0)), pl.BlockSpec((1, D_CONV), lambda i: (0, 0))],
        out_specs=[pl.BlockSpec((tm, D_CONV), lambda i: (i, 0)), pl.BlockSpec((CONV_WIDTH, D_CONV), lambda i: (0, 0)),
                   pl.BlockSpec((1, D_CONV), lambda i: (0, 0))],
        out_shape=[jax.ShapeDtypeStruct((t, D_CONV), F32), jax.ShapeDtypeStruct((CONV_WIDTH, D_CONV), F32),
                   jax.ShapeDtypeStruct((1, D_CONV), F32)],
        scratch_shapes=[pltpu.VMEM((tm + HALO, D_CONV), F32)],
        compiler_params=_cparams(("arbitrary",)),
    )(u, u, dxs_a, dxs_b, dbm, dcm, w, bias)

    def body2(dz_ref, n_ref, w_ref, du_ref, scr):
        last = pl.program_id(0) % per_seq == per_seq - 1
        scr[0:tm, :] = dz_ref[...]
        scr[tm:, :] = jnp.where(last, 0.0, n_ref[...])
        for c in range(D_CONV // LANES):
            cs = slice(c * LANES, (c + 1) * LANES)
            du_ref[:, cs] = _conv_taps(scr, w_ref, cs, CONV_WIDTH - 1, -1, tm).astype(du_ref.dtype)

    du = pl.pallas_call(
        body2, name="conv_bwd_du", grid=(n_tiles,),
        in_specs=[pl.BlockSpec((tm, D_CONV), lambda i: (i, 0)),
                  pl.BlockSpec((HALO, D_CONV), lambda i: (jnp.minimum((i + 1) * (tm // HALO), t // HALO - 1), 0)),
                  pl.BlockSpec((CONV_WIDTH, D_CONV), lambda i: (0, 0))],
        out_specs=pl.BlockSpec((tm, D_CONV), lambda i: (i, 0)),
        out_shape=jax.ShapeDtypeStruct((t, D_CONV), BF16),
        scratch_shapes=[pltpu.VMEM((tm + HALO, D_CONV), F32)],
        compiler_params=_cparams(("parallel",)),
    )(dz, dz, w)
    return du, dw, db


Q = SSD_CHUNK
N_PAIRS = D_SSD // LANES
HEADS_PER_GROUP = N_HEADS // SSD_GROUPS


def _rep(a, j):
    return jnp.broadcast_to(a[:, j:j + 1], a.shape)


def _dot_exact01(a, b, dn, a_is_01):
    x = b if a_is_01 else a
    hi = x.astype(BF16)
    mid = (x - hi.astype(F32)).astype(BF16)
    lo = (x - hi.astype(F32) - mid.astype(F32)).astype(BF16)
    z = a.astype(BF16) if a_is_01 else b.astype(BF16)
    out = None
    for term in (hi, mid, lo):
        d = _dot(z, term, dn) if a_is_01 else _dot(term, z, dn)
        out = d if out is None else out + d
    return out


def _pad_lanes(v, fill=0.0):
    row = jnp.pad(v.reshape(1, -1).astype(F32), ((0, 0), (0, LANES - v.size)), constant_values=fill)
    return row, row.reshape(LANES, 1)


def _ssd_common(dtr_ref, dtrt_ref, bias_r, bias_c, alog_r, alog_c):
    row = lax.broadcasted_iota(jnp.int32, (Q, Q), 0)
    col = lax.broadcasted_iota(jnp.int32, (Q, Q), 1)
    tril = row >= col
    lane = lax.broadcasted_iota(jnp.int32, (1, LANES), 1)
    a_r = jnp.where(lane < N_HEADS, -jnp.exp(alog_r[...]), 0.0)
    sub = lax.broadcasted_iota(jnp.int32, (LANES, 1), 0)
    a_c = jnp.where(sub < N_HEADS, -jnp.exp(alog_c[...]), 0.0)
    dt = _softplus(dtr_ref[...] + bias_r[...])
    cs = _dot_exact01(tril, dt * a_r, _NN, True)
    dtt = _softplus(dtrt_ref[...] + bias_c[...])
    cst = _dot_exact01(dtt * a_c, row <= col, _NN, False)
    return tril, lane, a_r, dt, cs, cst


def _ssd_specs(b, nc, rev):
    ci = (lambda c: nc - 1 - c) if rev else (lambda c: c)
    rows = lambda w: pl.BlockSpec((Q, w), lambda bb, c: (bb * nc + ci(c), 0))
    dtt = pl.BlockSpec((LANES, Q), lambda bb, c: (0, bb * nc + ci(c)))
    const = lambda s: pl.BlockSpec(s, lambda bb, c: (0,) * len(s))
    state = pl.BlockSpec((None, N_PAIRS, LANES, SSD_STATE), lambda bb, c: (bb * nc + ci(c), 0, 0, 0))
    return rows, dtt, const, state


def ssd_fwd(xs, bm, cm, dtraw, dt_bias, a_log, b):
    t = xs.shape[0]
    nc = SEQ // Q
    rows, dtt_spec, const, state = _ssd_specs(b, nc, False)
    bias_r, bias_c = _pad_lanes(dt_bias)
    alog_r, alog_c = _pad_lanes(a_log)

    def body(xs_ref, b_ref, c_ref, dtr_ref, dtrt_ref, br, bc, ar, ac, y_ref, hp_ref, h_scr):
        @pl.when(pl.program_id(1) == 0)
        def _():
            h_scr[...] = jnp.zeros(h_scr.shape, F32)
        tril, lane, _, dt, cs, cst = _ssd_common(dtr_ref, dtrt_ref, br, bc, ar, ac)
        sub = lax.broadcasted_iota(jnp.int32, (LANES, 1), 0)
        y_acc = [jnp.zeros((Q, LANES), F32) for _ in range(N_PAIRS)]
        h_old = [h_scr[p] for p in range(N_PAIRS)]
        h_new = [jnp.zeros((LANES, SSD_STATE), F32) for _ in range(N_PAIRS)]
        for g in range(SSD_GROUPS):
            bg = b_ref[:, g * SSD_STATE:(g + 1) * SSD_STATE].astype(BF16)
            cg = c_ref[:, g * SSD_STATE:(g + 1) * SSD_STATE].astype(BF16)
            cb = _dot(cg, bg, _NT)
            heads = []
            for j in range(g * HEADS_PER_GROUP, (g + 1) * HEADS_PER_GROUP):
                p, side = j // 2, j % 2
                m = (lane < HEAD_DIM) if side == 0 else (lane >= HEAD_DIM)
                ms = (sub < HEAD_DIM) if side == 0 else (sub >= HEAD_DIM)
                csj, dtj = _rep(cs, j), _rep(dt, j)
                lmat = jnp.exp(jnp.where(tril, csj - cst[j:j + 1, :], NEG))
                xdt = jnp.where(m, xs_ref[:, p * LANES:(p + 1) * LANES] * dtj, 0.0)
                hm = jnp.where(ms, h_old[p], 0.0)
                last = csj[Q - 1:Q, :]
                heads.append(dict(p=p, hm=hm, ecs=jnp.exp(csj), el=jnp.exp(last), gmat=(cb * lmat).astype(BF16),
                                  xdt=xdt.astype(BF16), xd=(xdt * jnp.exp(last - csj)).astype(BF16)))
            for h in heads:
                h.update(ydiag=_dot(h['gmat'], h['xdt'], _NN), ch=_dot(cg, h['hm'].astype(BF16), _NT), sj=_dot(h['xd'], bg, _TN))
            for h in heads:
                y_acc[h['p']] = y_acc[h['p']] + h['ydiag'] + h['ecs'] * h['ch']
                h_new[h['p']] = h_new[h['p']] + h['el'] * h['hm'] + h['sj']
        for p in range(N_PAIRS):
            y_ref[:, p * LANES:(p + 1) * LANES] = y_acc[p]
            hp_ref[p] = h_old[p]
            h_scr[p] = h_new[p]

    return pl.pallas_call(
        body, name="ssd_fwd", grid=(b, nc),
        in_specs=[rows(D_SSD), rows(D_BC), rows(D_BC), rows(LANES), dtt_spec, const((1, LANES)), const((LANES, 1)),
                  const((1, LANES)), const((LANES, 1))],
        out_specs=[rows(D_SSD), state],
        out_shape=[jax.ShapeDtypeStruct((t, D_SSD), F32),
                   jax.ShapeDtypeStruct((b * nc, N_PAIRS, LANES, SSD_STATE), F32)],
        scratch_shapes=[pltpu.VMEM((N_PAIRS, LANES, SSD_STATE), F32)],
        compiler_params=_cparams(("parallel", "arbitrary")),
    )(xs, bm, cm, dtraw, dtraw.T, bias_r, bias_c, alog_r, alog_c)


def ssd_bwd(xs, bm, cm, dtraw, dt_bias, a_log, hprev, dy, b):
    t = xs.shape[0]
    nc = SEQ // Q
    rows, dtt_spec, const, state = _ssd_specs(b, nc, True)
    bias_r, bias_c = _pad_lanes(dt_bias)
    alog_r, alog_c = _pad_lanes(a_log)

    def body(xs_ref, b_ref, c_ref, dtr_ref, dtrt_ref, hp_ref, dy_ref, br, bc, ar, ac,
             dxs_ref, db_ref, dc_ref, ddt_ref, dbias_ref, dalog_ref, dh_scr):
        first = jnp.logical_and(pl.program_id(0) == 0, pl.program_id(1) == 0)

        @pl.when(pl.program_id(1) == 0)
        def _():
            dh_scr[...] = jnp.zeros(dh_scr.shape, F32)

        @pl.when(first)
        def _():
            dbias_ref[...] = jnp.zeros(dbias_ref.shape, F32)
            dalog_ref[...] = jnp.zeros(dalog_ref.shape, F32)
        tril, lane, a_r, dt, cs, cst = _ssd_common(dtr_ref, dtrt_ref, br, bc, ar, ac)
        sub = lax.broadcasted_iota(jnp.int32, (LANES, 1), 0)
        rowq = lax.broadcasted_iota(jnp.int32, (Q, 1), 0)
        triu = (lax.broadcasted_iota(jnp.int32, (Q, Q), 0) <= lax.broadcasted_iota(jnp.int32, (Q, Q), 1)).astype(F32)
        dxs_acc = [jnp.zeros((Q, LANES), F32) for _ in range(N_PAIRS)]
        dh_in = [dh_scr[p] for p in range(N_PAIRS)]
        h_in = [hp_ref[p] for p in range(N_PAIRS)]
        dh_out = [jnp.zeros((LANES, SSD_STATE), F32) for _ in range(N_PAIRS)]
        ddt = jnp.zeros((Q, LANES), F32)
        dalog = jnp.zeros((1, LANES), F32)
        for g in range(SSD_GROUPS):
            gs = slice(g * SSD_STATE, (g + 1) * SSD_STATE)
            bg, cg = b_ref[:, gs].astype(BF16), c_ref[:, gs].astype(BF16)
            cb = _dot(cg, bg, _NT)
            dcb = jnp.zeros((Q, Q), F32)
            dbg = jnp.zeros((Q, SSD_STATE), F32)
            dcg = jnp.zeros((Q, SSD_STATE), F32)
            heads = []
            for j in range(g * HEADS_PER_GROUP, (g + 1) * HEADS_PER_GROUP):
                p, side = j // 2, j % 2
                m = (lane < HEAD_DIM) if side == 0 else (lane >= HEAD_DIM)
                ms = (sub < HEAD_DIM) if side == 0 else (sub >= HEAD_DIM)
                csj, dtj = _rep(cs, j), _rep(dt, j)
                lmat = jnp.exp(jnp.where(tril, csj - cst[j:j + 1, :], NEG))
                x2 = jnp.where(m, xs_ref[:, p * LANES:(p + 1) * LANES], 0.0)
                xdt = x2 * dtj
                dym = jnp.where(m, dy_ref[:, p * LANES:(p + 1) * LANES], 0.0)
                hm = jnp.where(ms, h_in[p], 0.0)
                dhm = jnp.where(ms, dh_in[p], 0.0)
                last = csj[Q - 1:Q, :]
                decay = jnp.exp(last - csj)
                heads.append(dict(j=j, p=p, dtj=dtj, lmat=lmat, x2=x2, hm=hm, dhm=dhm, decay=decay, el=jnp.exp(last),
                                  gmat=cb * lmat, dym=dym.astype(BF16), xdt=xdt.astype(BF16), hmb=hm.astype(BF16),
                                  dhmb=dhm.astype(BF16), dye=dym * jnp.exp(csj), xd=xdt * decay))
            for h in heads:
                dyeb, xdb = h['dye'].astype(BF16), h['xd'].astype(BF16)
                h.update(dg=_dot(h['dym'], h['xdt'], _NT),
                         dxdt=_dot(h['gmat'].astype(BF16), h['dym'], _TN),
                         ch=_dot(cg, h['hmb'], _NT),
                         dcg=_dot(dyeb, h['hmb'], _NN), dhp=_dot(dyeb, cg, _TN),
                         wmat=_dot(bg, h['dhmb'], _NT),
                         dbg=_dot(xdb, h['dhmb'], _NN))
            for h in heads:
                ej = h['dg'] * h['gmat']
                col_sums = jnp.broadcast_to(jnp.sum(ej, axis=0, keepdims=True), (Q, Q)).T
                ddl = jnp.sum(h['xd'] * h['wmat'], axis=1, keepdims=True)
                dlast = jnp.sum(ddl, axis=0, keepdims=True) + h['el'] * jnp.sum(
                    jnp.sum(h['dhm'] * h['hm'], axis=1, keepdims=True), axis=0, keepdims=True)
                h['dcs'] = (jnp.sum(ej, axis=1, keepdims=True) - col_sums + jnp.sum(h['dye'] * h['ch'], axis=1, keepdims=True)
                            - ddl + jnp.where(rowq == Q - 1, dlast, 0.0))
                h['dxdt'] = h['dxdt'] + h['decay'] * h['wmat']
                dcb, dcg, dbg = dcb + h['dg'] * h['lmat'], dcg + h['dcg'], dbg + h['dbg']
                dh_out[h['p']] = dh_out[h['p']] + h['el'] * h['dhm'] + h['dhp']
            for h in heads:
                h['da'] = _dot_exact01(triu, h['dcs'], _NN, True)
            for h in heads:
                j, da = h['j'], h['da']
                aj = jnp.sum(jnp.where(lane == j, a_r, 0.0), axis=1, keepdims=True)
                ddtj = da * aj + jnp.sum(h['dxdt'] * h['x2'], axis=1, keepdims=True)
                ddt = ddt + jnp.where(lane == j, ddtj, 0.0)
                dalog = dalog + jnp.where(lane == j, jnp.sum(da * h['dtj'], axis=0, keepdims=True) * aj, 0.0)
                dxs_acc[h['p']] = dxs_acc[h['p']] + h['dxdt'] * h['dtj']
            dcbb = dcb.astype(BF16)
            dc_ref[:, gs] = dcg + _dot(dcbb, bg, _NN)
            db_ref[:, gs] = dbg + _dot(dcbb, cg, _TN)
        for p in range(N_PAIRS):
            dxs_ref[:, p * LANES:(p + 1) * LANES] = dxs_acc[p]
            dh_scr[p] = dh_out[p]
        ddtraw = ddt * _sigmoid(dtr_ref[...] + br[...])
        ddt_ref[...] = ddtraw
        dbias_ref[...] += jnp.sum(ddtraw, axis=0, keepdims=True)
        dalog_ref[...] += dalog

    return pl.pallas_call(
        body, name="ssd_bwd", grid=(b, nc),
        in_specs=[rows(D_SSD), rows(D_BC), rows(D_BC), rows(LANES), dtt_spec, state, rows(D_SSD), const((1, LANES)),
                  const((LANES, 1)), const((1, LANES)), const((LANES, 1))],
        out_specs=[rows(D_SSD), rows(D_BC), rows(D_BC), rows(LANES), const((1, LANES)), const((1, LANES))],
        out_shape=[jax.ShapeDtypeStruct((t, D_SSD), F32), jax.ShapeDtypeStruct((t, D_BC), F32),
                   jax.ShapeDtypeStruct((t, D_BC), F32), jax.ShapeDtypeStruct((t, LANES), F32),
                   jax.ShapeDtypeStruct((1, LANES), F32), jax.ShapeDtypeStruct((1, LANES), F32)],
        scratch_shapes=[pltpu.VMEM((N_PAIRS, LANES, SSD_STATE), F32)],
        compiler_params=_cparams(("arbitrary", "arbitrary")),
    )(xs, bm, cm, dtraw, dtraw.T, hprev, dy, bias_r, bias_c, alog_r, alog_c)


def _split_w_in(w_in):
    w_dt = jnp.pad(w_in[:, D_QKVZ + D_CONV:], ((0, 0), (0, LANES - N_HEADS)))
    return w_in[:, :D_QKVZ], w_in[:, D_QKVZ:D_QKVZ + D_CONV], w_dt


def mixer_fwd(hb, p, cosv, sinv, b):
    t = hb.shape[0]
    w_a, w_b, w_c = _split_w_in(p['w_in'])
    qkvz = mm("in_qkvz", [(hb, w_a, 'nn')], D_QKVZ)
    xbc = mm("in_xbc", [(hb, w_b, 'nn')], D_CONV)
    dtraw = mm("in_dt", [(hb, w_c, 'nn')], LANES)
    mixed, *lses = attn_fwd(qkvz, cosv, sinv, b)
    attn = attn_norm_fwd(mixed, p['attn_norm_w'])
    xs, bm, cm = conv_fwd(xbc, p['conv_w'], p['conv_b'])
    y, hprev = ssd_fwd(xs, bm, cm, dtraw, p['dt_bias'], p['a_log'], b)
    dskip = jnp.repeat(p['d_skip'].reshape(-1), HEAD_DIM).reshape(1, D_SSD)
    yg, = rowwise("ssd_gate", _gate, [y, xs, Op(qkvz, D_SSD, 3)], [dskip, p['ssd_norm_w']], [(t, D_SSD, BF16)])
    mix = mm("out_proj", [(attn, p['w_out'][:D_ATTN], 'nn'), (yg, p['w_out'][D_ATTN:], 'nn')], D_MODEL, out_dtype=BF16)
    res = dict(hb=hb, qkvz=qkvz, xbc=xbc, dtraw=dtraw, mixed=mixed, lses=lses, attn=attn, xs=xs, bm=bm, cm=cm,
               y=y, hprev=hprev, dskip=dskip, yg=yg, cosv=cosv, sinv=sinv)
    return mix, res


def mixer_bwd(r, p, dmix, dh_resid, b):
    t = dmix.shape[0]
    w_a, w_b, w_c = _split_w_in(p['w_in'])
    w_out = p['w_out']
    dattn = mm("out_bwd_dattn", [(dmix, w_out[:D_ATTN], 'nt')], D_ATTN)
    dyg = mm("out_bwd_dyg", [(dmix, w_out[D_ATTN:], 'nt')], D_SSD)
    dw_out = jnp.concatenate([mm_tn("out_bwd_dw_a", r['attn'], dmix, BF16),
                              mm_tn("out_bwd_dw_y", r['yg'], dmix, BF16)], axis=0)

    def gate_bwd(dy_, y_, xs_, z_, ds_, w_):
        _, vjp = jax.vjp(_gate, y_, xs_, z_, ds_, w_)
        return vjp(dy_)

    dy, dxs_a, dz, ddskip, dssd_norm = rowwise(
        "ssd_gate_bwd", gate_bwd, [dyg, r['y'], r['xs'], Op(r['qkvz'], D_SSD, 3)], [r['dskip'], p['ssd_norm_w']],
        [(t, D_SSD, F32), (t, D_SSD, F32), (t, D_SSD, BF16)], accs=[(1, D_SSD), (1, D_SSD)])
    dxs_b, dbm, dcm, ddtraw, ddt_bias, da_log = ssd_bwd(r['xs'], r['bm'], r['cm'], r['dtraw'], p['dt_bias'], p['a_log'],
                                                        r['hprev'], dy, b)
    dxbc, dconv_w, dconv_b = conv_bwd(r['xbc'], p['conv_w'], p['conv_b'], dxs_a, dxs_b, dbm, dcm)
    dmixed, dattn_norm = attn_norm_bwd(dattn, r['mixed'], p['attn_norm_w'])
    dq, dk, dv = attn_bwd(r['qkvz'], r['cosv'], r['sinv'], dmixed, r['mixed'], r['lses'], b)
    wq, wk, wv, wz = (w_a[:, i * D_ATTN:(i + 1) * D_ATTN] for i in range(4))
    dh = mm("in_bwd_dh", [(dq, wq, 'nt'), (dk, wk, 'nt'), (dv, wv, 'nt'), (dz, wz, 'nt'), (dxbc, w_b, 'nt'),
                          (ddtraw, w_c, 'nt')], D_MODEL, add=dh_resid, tn=512)
    h = r['hb']
    dw_in = jnp.concatenate([mm_tn_cat("in_bwd_dw_qkvz", h, [dq, dk, dv, dz], BF16),
                             mm_tn_cat("in_bwd_dw_xbc_dt", h, [dxbc, ddtraw], BF16)[:, :D_CONV + N_HEADS]], axis=1)
    head_sum = lambda v: v.reshape(N_HEADS, HEAD_DIM).sum(axis=1).reshape(1, N_HEADS)
    grads = dict(w_in=dw_in, w_out=dw_out, conv_w=dconv_w, conv_b=dconv_b, dt_bias=ddt_bias[:, :N_HEADS],
                 a_log=da_log[:, :N_HEADS], d_skip=head_sum(ddskip), attn_norm_w=dattn_norm, ssd_norm_w=dssd_norm)
    return dh, grads


FFN2_KEYS = ('ffn2_gate', 'ffn2_up', 'ffn2_down')
MIXER_KEYS = ('w_in', 'conv_w', 'w_out')
FFN_COL = ('ffn1_gate', 'ffn1_up', 'ffn2_gate', 'ffn2_up')
FFN_ROW = ('ffn1_down', 'ffn2_down')
CONV_W_COMM = (8, 2 * LANES)
SMALL = 'small'


def comm_shape(k, shapes):
    if k in FFN_COL:
        return (D_MODEL, FF_PAD)
    if k in FFN_ROW:
        return (FF_PAD, D_MODEL)
    if k == 'conv_w':
        return CONV_W_COMM
    return tuple(shapes[k][1:])


def to_comm(k, vals, shapes):
    a = vals[k].reshape(shapes[k][1:])
    r_, c_ = comm_shape(k, shapes)
    return jnp.pad(a, ((0, r_ - a.shape[0]), (0, c_ - a.shape[1])))


SMALL_ROWS, SMALL_COLS = 16, D_CONV


def pack_small(small):
    rows = [jnp.pad(small[r].reshape(1, -1), ((0, 0), (0, SMALL_COLS - small[r].size))) for r in REPLICATED]
    return jnp.concatenate(rows + [jnp.zeros((SMALL_ROWS - len(rows), SMALL_COLS), F32)], axis=0)


def full_weight(k, g):
    if k in FFN_COL:
        return g
    if k == 'conv_w':
        return jnp.transpose(g[:, :CONV_WIDTH, :D_CONV // N_DEV], (1, 0, 2)).reshape(CONV_WIDTH, D_CONV)
    return g.reshape(N_DEV * g.shape[1], g.shape[2])


def grad_shards(k, g):
    if k in FFN_COL:
        return g
    if k == 'conv_w':
        s = jnp.transpose(g.reshape(CONV_WIDTH, N_DEV, D_CONV // N_DEV), (1, 0, 2))
        return jnp.pad(s, ((0, 0), (0, CONV_W_COMM[0] - CONV_WIDTH), (0, CONV_W_COMM[1] - D_CONV // N_DEV)))
    return g.reshape(N_DEV, g.shape[0] // N_DEV, g.shape[1])


def _flip(v, bit):
    return 1 - v if bit else v


N_PEER_COPIES = N_DEV - 1


def _comm_call(name, body, arrs, out_shape):
    n = len(arrs)
    return pl.pallas_call(
        functools.partial(body, n), name=name, out_shape=out_shape,
        in_specs=[pl.BlockSpec(memory_space=pl.ANY)] * n, out_specs=[pl.BlockSpec(memory_space=pl.ANY)] * n,
        scratch_shapes=[pltpu.SemaphoreType.DMA((n * N_PEER_COPIES,)), pltpu.SemaphoreType.DMA((n * N_PEER_COPIES,)),
                        pltpu.SemaphoreType.DMA((n,))],
    )(*arrs)


def _blk(ref, idx, by_cols):
    if not by_cols:
        return ref.at[idx]
    c = ref.shape[1] // N_DEV
    return ref.at[:, pl.ds(pl.multiple_of(idx * c, LANES), c)]


def _blocked_shape(a, by_cols):
    return (a.shape[0], N_DEV * a.shape[1]) if by_cols else (N_DEV,) + a.shape


def all_gather(arrs, by_cols):
    def body(n, *refs):
        x_refs, out_refs, (send_sems, recv_sems, local_sems) = refs[:n], refs[n:2 * n], refs[2 * n:]
        x, y, c = lax.axis_index("x"), lax.axis_index("y"), lax.axis_index("c")
        me, sibling = (x, y, c), (x, y, 1 - c)
        chips = [(1 - x, y), (x, 1 - y), (1 - x, 1 - y)]

        def copy(a, k, block, to, src=None):
            px, py, pc = block
            dst = _blk(out_refs[a], 4 * px + 2 * py + pc, by_cols[a])
            return pltpu.make_async_remote_copy(
                src_ref=dst if src is None else src, dst_ref=dst, send_sem=send_sems.at[a * N_PEER_COPIES + k],
                recv_sem=recv_sems.at[a * N_PEER_COPIES + k], device_id=to, device_id_type=MESH)

        mine = [pltpu.make_async_copy(x_refs[a], _blk(out_refs[a], 4 * x + 2 * y + c, by_cols[a]), local_sems.at[a])
                for a in range(n)]
        started = []
        for a in range(n):
            mine[a].start()
            first = [copy(a, 0, me, sibling, src=x_refs[a])]
            first += [copy(a, 1 + j, me, (*chip, c), src=x_refs[a]) for j, chip in enumerate(chips)]
            for cp in first:
                cp.start()
            started += first
        for j, chip in enumerate(chips):
            for a in range(n):
                copy(a, 1 + j, (*chip, c), me).wait_recv()
                cp = copy(a, 4 + j, (*chip, c), sibling)
                cp.start()
                started.append(cp)
        for a in range(n):
            copy(a, 0, sibling, me).wait_recv()
            for j, chip in enumerate(chips):
                copy(a, 4 + j, (*chip, 1 - c), me).wait_recv()
        for cp in started:
            cp.wait_send()
        for cp in mine:
            cp.wait()

    return _comm_call("all_gather_weights", body, arrs,
                      [jax.ShapeDtypeStruct(_blocked_shape(a, bc), a.dtype) for a, bc in zip(arrs, by_cols)])


def blocks_to_cols(arrs):
    def body(*refs):
        for i, o in zip(refs[:len(arrs)], refs[len(arrs):]):
            o[...] = i[...]

    return pl.pallas_call(
        body, name="blocks_to_cols", grid=(N_DEV,),
        in_specs=[pl.BlockSpec((None,) + a.shape[1:], lambda p: (p, 0, 0)) for a in arrs],
        out_specs=[pl.BlockSpec(a.shape[1:], lambda p: (0, p)) for a in arrs],
        out_shape=[jax.ShapeDtypeStruct((a.shape[1], N_DEV * a.shape[2]), a.dtype) for a in arrs],
        compiler_params=_cparams(("parallel",)),
    )(*arrs)


def _landing_shape(a, by_cols):
    return (N_DEV, a.shape[0], a.shape[1] // N_DEV) if by_cols else a.shape


_HBM = pl.BlockSpec(memory_space=pltpu.HBM)
_SEM = pl.BlockSpec(memory_space=pltpu.SEMAPHORE)
_EFFECT = pltpu.SideEffectType.DATAFLOW_SIDE_EFFECTING


def _peer(k):
    x, y, c = lax.axis_index("x"), lax.axis_index("y"), lax.axis_index("c")
    return _flip(x, k & 4), _flip(y, k & 2), _flip(c, k & 1)


def _my_index():
    return 4 * lax.axis_index("x") + 2 * lax.axis_index("y") + lax.axis_index("c")


def _split_copies(mode, by_cols, src_refs, land_refs, send_sems, recv_sems):
    me = _my_index()
    out = []
    for a, bc in enumerate(by_cols):
        for k in range(1, N_DEV):
            px, py, pc = _peer(k)
            src = _blk(src_refs[a], 4 * px + 2 * py + pc, bc) if mode == 'scatter' else src_refs[a]
            dst = land_refs[a].at[me] if mode == 'scatter' else _blk(land_refs[a], me, bc)
            out.append(pltpu.make_async_remote_copy(
                src_ref=src, dst_ref=dst, send_sem=send_sems.at[a * N_PEER_COPIES + k - 1],
                recv_sem=recv_sems.at[a * N_PEER_COPIES + k - 1], device_id=(px, py, pc), device_id_type=MESH))
    return out


def exchange_start(name, mode, srcs, by_cols):
    n = len(srcs)
    lands = [lax.empty(_landing_shape(s, bc) if mode == 'scatter' else _blocked_shape(s, bc), s.dtype)
             for s, bc in zip(srcs, by_cols)]

    def body(*refs):
        src_refs, land_refs, send_sems, recv_sems = refs[:n], refs[n:2 * n], refs[2 * n], refs[2 * n + 1]
        for cp in _split_copies(mode, by_cols, src_refs, land_refs, send_sems, recv_sems):
            cp.start()
        refs[-1][...] = jnp.zeros(refs[-1].shape, F32)

    sems = pltpu.SemaphoreType.DMA((n * N_PEER_COPIES,))
    res = pl.pallas_call(
        body, name=name,
        out_shape=(sems, sems, *[pltpu.HBM(a.shape, a.dtype) for a in srcs + lands], jax.ShapeDtypeStruct((8, LANES), F32)),
        in_specs=(_HBM,) * (2 * n), out_specs=(_SEM, _SEM, *(_HBM,) * (2 * n), pl.BlockSpec(memory_space=pltpu.VMEM)),
        input_output_aliases={i: 2 + i for i in range(2 * n)},
        compiler_params=pltpu.CompilerParams(has_side_effects=_EFFECT),
    )(*[pltpu.with_memory_space_constraint(a, pltpu.HBM) for a in srcs + lands])
    return (mode, by_cols, res[:-1]), res[-1]


def exchange_wait(name, handles, after):
    mode, by_cols, (send_sems, recv_sems, *bufs) = handles
    n = len(by_cols)

    def body(*refs):
        src_refs, land_refs, s_sems, r_sems = refs[:n], refs[n:2 * n], refs[2 * n], refs[2 * n + 1]
        for cp in _split_copies(mode, by_cols, src_refs, land_refs, s_sems, r_sems):
            cp.wait_send()
            cp.wait_recv()

    res = pl.pallas_call(
        body, name=name, out_shape=tuple(pltpu.HBM(a.shape, a.dtype) for a in bufs),
        in_specs=(*(_HBM,) * (2 * n), _SEM, _SEM, pl.BlockSpec(memory_space=pl.ANY)), out_specs=(_HBM,) * (2 * n),
        input_output_aliases={i: i for i in range(2 * n)},
        compiler_params=pltpu.CompilerParams(has_side_effects=_EFFECT),
    )(*bufs, send_sems, recv_sems, after)
    me, out = _my_index(), []
    for src, land, bc in zip(res[:n], res[n:], by_cols):
        if mode == 'scatter':
            c = land.shape[2]
            own = lax.dynamic_slice(src, (0, me * c), (src.shape[0], c)) if bc else lax.dynamic_index_in_dim(src, me, 0, False)
            out.append(lax.dynamic_update_slice(land, own[None], (me, 0, 0)))
        elif bc:
            out.append(lax.dynamic_update_slice(land, src, (0, me * src.shape[1])))
        else:
            out.append(lax.dynamic_update_slice(land, src[None], (me, 0, 0)))
    return out


def _adamw_math(g, w, m, v):
    c1 = 1.0 / (1.0 - ADAM_B1 ** ADAM_STEP)
    c2 = 1.0 / (1.0 - ADAM_B2 ** ADAM_STEP)
    m = ADAM_B1 * m + (1.0 - ADAM_B1) * g
    v = ADAM_B2 * v + (1.0 - ADAM_B2) * jnp.square(g)
    return g, -ADAM_LR * ((m * c1) / (jnp.sqrt(v * c2) + ADAM_EPS) + ADAM_WD * w), m, v


def adamw(name, recv, w, m, v, tm):
    _, rows, cols = w.shape
    tm = min(tm, rows)

    def body(*refs):
        g = refs[0][0:tm, 0:cols].astype(F32)
        for s in range(1, N_DEV):
            g = g + refs[s][0:tm, 0:cols].astype(F32)
        res = _adamw_math(g, *[r[...] for r in refs[N_DEV:N_DEV + 3]])
        for r, val in zip(refs[N_DEV + 3:], res):
            r[...] = val

    part = lambda s: pl.BlockSpec((None, recv.shape[1] if tm == rows else tm, recv.shape[2]), lambda i: (s, i, 0))
    tile = pl.BlockSpec((None, tm, cols), lambda i: (0, i, 0))
    return pl.pallas_call(
        body, name=name, grid=(rows // tm,), in_specs=[part(s) for s in range(N_DEV)] + [tile] * 3, out_specs=[tile] * 4,
        out_shape=[jax.ShapeDtypeStruct((1, rows, cols), F32)] * 4, compiler_params=_cparams(("parallel",)),
    )(*[recv] * N_DEV, w, m, v)


def adamw_small(recv, wl, ml, vl):
    n = len(REPLICATED)

    def body(recv_ref, *refs):
        g = recv_ref[0]
        for s in range(1, N_DEV):
            g = g + recv_ref[s]
        for r in range(n):
            w, m, v = (refs[j * n + r][...] for j in range(3))
            for j, val in enumerate(_adamw_math(g[r:r + 1, :w.shape[1]], w, m, v)):
                refs[(3 + j) * n + r][...] = val

    arrs = [d[k].reshape(1, -1) for d in (wl, ml, vl) for k in REPLICATED]
    res = pl.pallas_call(
        body, name="adamw_small", out_shape=[jax.ShapeDtypeStruct(a.shape, F32) for a in arrs[:n]] * 4,
    )(recv, *arrs)
    return [{k: res[j * n + r].reshape(wl[k].shape) for r, k in enumerate(REPLICATED)} for j in range(4)]


ADAMW_TM = {'ffn1_gate': 256, 'ffn1_up': 256, 'ffn2_gate': 256, 'ffn2_up': 256, 'w_in': 32}


def kernel(x, positions, ln1_g, ln1_b, ffn1_gate, ffn1_up, ffn1_down, w_in, conv_w, conv_b, dt_bias, a_log, d_skip, attn_norm_w, ssd_norm_w, w_out, ln2_g, ln2_b, ffn2_gate, ffn2_up, ffn2_down, ln3_g, ln3_b, loss_target, m_ln1_g, m_ln1_b, m_ffn1_gate, m_ffn1_up, m_ffn1_down, m_w_in, m_conv_w, m_conv_b, m_dt_bias, m_a_log, m_d_skip, m_attn_norm_w, m_ssd_norm_w, m_w_out, m_ln2_g, m_ln2_b, m_ffn2_gate, m_ffn2_up, m_ffn2_down, m_ln3_g, m_ln3_b, v_ln1_g, v_ln1_b, v_ffn1_gate, v_ffn1_up, v_ffn1_down, v_w_in, v_conv_w, v_conv_b, v_dt_bias, v_a_log, v_d_skip, v_attn_norm_w, v_ssd_norm_w, v_w_out, v_ln2_g, v_ln2_b, v_ffn2_gate, v_ffn2_up, v_ffn2_down, v_ln3_g, v_ln3_b):
    args = dict(locals())
    wl = {k: args[k] for k in WEIGHTS}
    ml = {k: args["m_" + k] for k in WEIGHTS}
    vl = {k: args["v_" + k] for k in WEIGHTS}
    shapes = {k: wl[k].shape for k in WEIGHTS}
    b, s, dm = x.shape
    t = b * s

    sent = {k: to_comm(k, wl, shapes).astype(F32 if k == 'conv_w' else BF16) for k in SHARDED}
    by_cols = lambda keys: [k in FFN_COL for k in keys]
    gate, up = all_gather([sent['ffn1_gate'], sent['ffn1_up']], [False] * 2)
    (gate, up), sent = lax.optimization_barrier(((gate, up), sent))
    p = dict(zip(('ffn1_gate', 'ffn1_up'), blocks_to_cols([gate, up])))
    gather_down, token_d = exchange_start("gather_ffn1_down_start", 'gather', [sent['ffn1_down']], [False])
    sent['w_in'] = sent['w_in'] + token_d[0, 0].astype(BF16)
    gather_mixer, token_m = exchange_start("gather_mixer_start", 'gather', [sent[k] for k in MIXER_KEYS], by_cols(MIXER_KEYS))
    sent['ffn2_gate'] = sent['ffn2_gate'] + token_m[0, 0].astype(BF16)
    gather_ffn2, token_f = exchange_start("gather_ffn2_start", 'gather', [sent[k] for k in FFN2_KEYS], by_cols(FFN2_KEYS))
    for k in REPLICATED:
        p[k] = wl[k].reshape(1, -1)

    x2 = x.reshape(t, dm)
    cosv, sinv = rope_tables(positions)
    g1, u1, a1, at1 = ffn_gate_up("ffn1_gate_up", x2, p['ffn1_gate'], p['ffn1_up'], after=(token_d, token_m, token_f))
    p['ffn1_down'] = full_weight('ffn1_down', exchange_wait("gather_ffn1_down_wait", gather_down, a1)[0])
    f1, res1 = mm("ffn1_down", [(a1, p['ffn1_down'], 'nn')], D_MODEL, out_dtype=BF16), (x2, g1, u1, at1)
    h1, h1b = resid_ln_fwd("ln1", 0.5, x2, f1, p['ln1_g'], p['ln1_b'])
    for k, g in zip(MIXER_KEYS, exchange_wait("gather_mixer_wait", gather_mixer, h1b)):
        p[k] = full_weight(k, g)
    mix, resm = mixer_fwd(h1b, p, cosv, sinv, b)
    h2, h2b = resid_ln_fwd("ln2", 1.0, h1, mix, p['ln2_g'], p['ln2_b'])
    for k, g in zip(FFN2_KEYS, exchange_wait("gather_ffn2_wait", gather_ffn2, h2b)):
        p[k] = full_weight(k, g)
    f2, res3 = ffn_fwd("ffn2", h2b, p['ffn2_gate'], p['ffn2_up'], p['ffn2_down'])

    small, full = {}, {}
    dh2_res, df2, small['ln3_g'], small['ln3_b'], sq = ln_loss_bwd("ln3_loss_bwd", h2, f2, loss_target.reshape(t, dm),
                                                                   p['ln3_g'], p['ln3_b'])
    loss = lax.psum(jnp.sum(sq) * (0.5 / dm), AXES)

    dh2, full['ffn2_gate'], full['ffn2_up'], full['ffn2_down'] = ffn_bwd("ffn2", res3, p['ffn2_gate'], p['ffn2_up'],
                                                                       p['ffn2_down'], df2, dh2_res)
    ffn2_exchange, token = exchange_start("grads_ffn2_start", 'scatter', [grad_shards(k, full[k]) for k in FFN2_KEYS],
                                          by_cols(FFN2_KEYS))
    dh1_res, dmix, small['ln2_g'], small['ln2_b'] = resid_ln_bwd("ln2_bwd", 1.0, h1, mix, p['ln2_g'] + token[:1, :1],
                                                                 p['ln2_b'], dh2)
    dh1, gm = mixer_bwd(resm, p, dmix, dh1_res, b)
    for k in ('conv_b', 'dt_bias', 'a_log', 'd_skip', 'attn_norm_w', 'ssd_norm_w'):
        small[k] = gm[k]
    mixer_exchange, token = exchange_start("grads_mixer_start", 'scatter', [grad_shards(k, gm[k]) for k in MIXER_KEYS],
                                           by_cols(MIXER_KEYS))
    dx_res, df1, small['ln1_g'], small['ln1_b'] = resid_ln_bwd("ln1_bwd", 0.5, x2, f1, p['ln1_g'] + token[:1, :1],
                                                               p['ln1_b'], dh1)
    hb, g, u, at = res1
    small_part = pack_small(small)
    dg, du = ffn_da_act("ffn1_bwd_da_act", df1, p['ffn1_down'], g, u)
    dwd = mm_acc("ffn1_bwd_dwd", at, df1, BF16, after=dg)
    down_exchange, token = exchange_start("grads_ffn1_down_start", 'scatter', [
        grad_shards('ffn1_down', dwd), jnp.broadcast_to(small_part[None], (N_DEV,) + small_part.shape)], [False, False])
    dwg = mm_tn("ffn1_bwd_dwg", hb, dg, BF16, after=token)
    gate_exchange, token = exchange_start("grads_ffn1_gate_start", 'scatter', [grad_shards('ffn1_gate', dwg)], [True])
    dwu = mm_tn("ffn1_bwd_dwu", hb, du, BF16, after=token)
    up_exchange, token = exchange_start("grads_ffn1_up_start", 'scatter', [grad_shards('ffn1_up', dwu)], [True])
    dx = mm("ffn1_bwd_dh", [(dg, p['ffn1_gate'], 'nt'), (du, p['ffn1_up'], 'nt')], D_MODEL, add=dx_res, tn=512, after=token)
    recv = {}
    for keys, name, ex in (((FFN2_KEYS), "grads_ffn2_wait", ffn2_exchange), (MIXER_KEYS, "grads_mixer_wait", mixer_exchange),
                           (('ffn1_down', SMALL), "grads_ffn1_down_wait", down_exchange),
                           (('ffn1_gate',), "grads_ffn1_gate_wait", gate_exchange),
                           (('ffn1_up',), "grads_ffn1_up_wait", up_exchange)):
        recv.update(zip(keys, exchange_wait(name, ex, dx)))
    outs = adamw_small(recv.pop(SMALL), wl, ml, vl)
    for k, r in recv.items():
        for o, a in zip(outs, adamw(f"adamw_{k}", r, wl[k], ml[k], vl[k], ADAMW_TM.get(k, shapes[k][1]))):
            o[k] = a
    return (loss, dx.reshape(b, s, dm), *[o[k] for o in outs for k in WEIGHTS])
```

```python
import functools

import jax
import jax.numpy as jnp
import numpy as np
from jax import lax
from jax.experimental import pallas as pl
from jax.experimental.pallas import tpu as pltpu

F32, BF16 = jnp.float32, jnp.bfloat16
HI = lax.Precision.HIGHEST
MESH = pl.DeviceIdType.MESH
AXES = ("x", "y", "c")
N_DEV = 8

D_MODEL = 1024
SEQ = 2048
HEAD_DIM = 64
N_HEADS = 12
D_ATTN = N_HEADS * HEAD_DIM
DILATIONS = (1, 4, 16)
ATTN_BLOCK = 128
ROPE_THETA = 500000.0
ROPE_DIM = 16
D_SSD = 768
SSD_GROUPS = 4
SSD_STATE = 128
SSD_CHUNK = 128
D_BC = SSD_GROUPS * SSD_STATE
D_CONV = D_SSD + 2 * D_BC
CONV_WIDTH = 4
D_QKVZ = 3 * D_ATTN + D_SSD
D_FF = 2816
ALPHA = 2.0 ** 0.25
LN_EPS = 1e-5
RMS_EPS = 1e-6
ADAM_LR, ADAM_B1, ADAM_B2, ADAM_EPS, ADAM_WD, ADAM_STEP = 0.001, 0.9, 0.999, 1e-08, 0.01, 10

LANES = 128
VMEM_LIMIT = 52 * 1024 * 1024
NEG = -1e30

WEIGHTS = ['ln1_g', 'ln1_b', 'ffn1_gate', 'ffn1_up', 'ffn1_down', 'w_in', 'conv_w', 'conv_b', 'dt_bias', 'a_log',
           'd_skip', 'attn_norm_w', 'ssd_norm_w', 'w_out', 'ln2_g', 'ln2_b', 'ffn2_gate', 'ffn2_up', 'ffn2_down',
           'ln3_g', 'ln3_b']
COL_SHARDED = ('ffn1_gate', 'ffn1_up', 'conv_w', 'ffn2_gate', 'ffn2_up')
ROW_SHARDED = ('ffn1_down', 'w_in', 'w_out', 'ffn2_down')
SHARDED = tuple(n for n in WEIGHTS if n in COL_SHARDED or n in ROW_SHARDED)
REPLICATED = tuple(n for n in WEIGHTS if n not in SHARDED)
FF_SHARD = D_FF // N_DEV
FF_PAD = -(-FF_SHARD // LANES) * LANES


def _cparams(sem=None):
    return pltpu.CompilerParams(dimension_semantics=sem, vmem_limit_bytes=VMEM_LIMIT)


def _tile(n, prefs):
    for p in prefs:
        if n % p == 0:
            return p
    return n


class Op:
    def __init__(self, arr, bw=None, cb=0, ro=0):
        self.arr, self.bw, self.cb, self.ro = arr, (arr.shape[1] if bw is None else bw), cb, ro


def _op(a):
    return a if isinstance(a, Op) else Op(a)


def rowwise(name, fn, ins, consts, outs, accs=(), tm=256):
    ins = [_op(a) for a in ins]
    rows = outs[0][0]
    n_in, n_c, n_o, n_a = len(ins), len(consts), len(outs), len(accs)
    tm = min(tm, rows)
    assert rows % tm == 0, (name, rows, tm)

    def body(*refs):
        vals = [r[...].astype(F32) for r in refs[:n_in + n_c]]
        res = fn(*vals)
        res = res if isinstance(res, (tuple, list)) else (res,)
        o_refs = refs[n_in + n_c:n_in + n_c + n_o]
        a_refs = refs[n_in + n_c + n_o:]
        for r, v in zip(o_refs, res[:n_o]):
            r[...] = v.astype(r.dtype)
        if n_a:
            @pl.when(pl.program_id(0) == 0)
            def _():
                for r in a_refs:
                    r[...] = jnp.zeros(r.shape, r.dtype)
            for r, v in zip(a_refs, res[n_o:]):
                r[...] += v

    in_specs = [pl.BlockSpec((tm, o.bw), functools.partial(lambda i, o: (i + o.ro, o.cb), o=o)) for o in ins]
    in_specs += [pl.BlockSpec(c.shape, functools.partial(lambda i, nd: (0,) * nd, nd=c.ndim)) for c in consts]
    out_specs = [pl.BlockSpec((tm, w), lambda i: (i, 0)) for (_, w, _) in outs]
    out_specs += [pl.BlockSpec(s, functools.partial(lambda i, nd: (0,) * nd, nd=len(s))) for s in accs]
    out_shape = [jax.ShapeDtypeStruct((r, w), dt) for (r, w, dt) in outs]
    out_shape += [jax.ShapeDtypeStruct(s, F32) for s in accs]
    res = pl.pallas_call(
        body, name=name, grid=(rows // tm,), in_specs=in_specs, out_specs=out_specs, out_shape=out_shape,
        compiler_params=_cparams(("arbitrary",) if n_a else ("parallel",)),
    )(*[o.arr for o in ins], *consts)
    return res


MM_TM = 1024
MM_TN = (1024, 896, 768, 512, 256, 128)
_NT = (((1,), (1,)), ((), ()))
_NN = (((1,), (0,)), ((), ()))
_TN = (((0,), (0,)), ((), ()))


def _dot(a, b, dn, precision=None):
    return lax.dot_general(a, b, dn, preferred_element_type=F32, precision=precision)


def _mm_specs(name, pairs, n_out, tm, tn):
    in_specs, args = [], []
    for a, b, mode in pairs:
        o = _op(a)
        in_specs.append(pl.BlockSpec((tm, o.bw), functools.partial(lambda j, i, o: (i, o.cb), o=o)))
        args.append(o.arr)
        if mode == 'nn':
            assert b.shape == (o.bw, n_out), (name, b.shape, o.bw, n_out)
            in_specs.append(pl.BlockSpec((o.bw, tn), lambda j, i: (0, j)))
        else:
            assert b.shape == (n_out, o.bw), (name, b.shape, o.bw, n_out)
            in_specs.append(pl.BlockSpec((tn, o.bw), lambda j, i: (j, 0)))
        args.append(b)
    return in_specs, args


def _mm_acc(refs, pairs):
    acc = None
    for k, (_, _, mode) in enumerate(pairs):
        d = _dot(refs[2 * k][...].astype(BF16), refs[2 * k + 1][...].astype(BF16), _NN if mode == 'nn' else _NT)
        acc = d if acc is None else acc + d
    return acc


def mm(name, pairs, n_out, add=None, out_dtype=F32, tm=MM_TM, tn=None, after=None):
    m = _op(pairs[0][0]).arr.shape[0]
    tn = tn or _tile(n_out, MM_TN)
    n_p = len(pairs)

    def body(*refs):
        acc = _mm_acc(refs, pairs)
        if add is not None:
            acc = acc + refs[2 * n_p][...]
        refs[-1][...] = acc.astype(refs[-1].dtype)

    in_specs, args = _mm_specs(name, pairs, n_out, tm, tn)
    tile = pl.BlockSpec((tm, tn), lambda j, i: (i, j))
    if add is not None:
        in_specs.append(tile)
        args.append(add)
    if after is not None:
        in_specs.append(pl.BlockSpec(memory_space=pl.ANY))
        args.append(after)
    return pl.pallas_call(
        body, name=name, grid=(n_out // tn, m // tm), in_specs=in_specs, out_specs=tile,
        out_shape=jax.ShapeDtypeStruct((m, n_out), out_dtype),
        compiler_params=_cparams(("parallel", "parallel")),
    )(*args)


def mm_tn(name, a, b, out_dtype=F32, tt=1024, after=None):
    a, b = _op(a), _op(b)
    t = a.arr.shape[0]
    k, n = a.bw, b.bw
    tk = _tile(k, (512, 896, 768, 256, 128))
    tn = _tile(n, (3072, 1792) + MM_TN)
    tt = min(tt, t)
    n_t = t // tt
    order = [] if after is None else [after]

    def body(a_ref, b_ref, *rest):
        o_ref, acc_ref = rest[-2:]
        s = pl.program_id(2)
        d = _dot(a_ref[...].astype(BF16), b_ref[...].astype(BF16), _TN)

        @pl.when(s == 0)
        def _():
            acc_ref[...] = d

        @pl.when(s > 0)
        def _():
            acc_ref[...] += d

        @pl.when(s == n_t - 1)
        def _():
            o_ref[...] = acc_ref[...].astype(o_ref.dtype)

    return pl.pallas_call(
        body, name=name, grid=(k // tk, n // tn, n_t),
        in_specs=[pl.BlockSpec((tt, tk), functools.partial(lambda kk, nn, s, o: (s, o.cb * (o.bw // tk) + kk), o=a)),
                  pl.BlockSpec((tt, tn), functools.partial(lambda kk, nn, s, o: (s, o.cb * (o.bw // tn) + nn), o=b))]
        + [pl.BlockSpec(memory_space=pl.ANY) for _ in order],
        out_specs=pl.BlockSpec((tk, tn), lambda kk, nn, s: (kk, nn)),
        out_shape=jax.ShapeDtypeStruct((k, n), out_dtype),
        scratch_shapes=[pltpu.VMEM((tk, tn), F32)],
        compiler_params=_cparams(("parallel", "parallel", "arbitrary")),
    )(a.arr, b.arr, *order)


def _sigmoid(x):
    return 1.0 / (1.0 + jnp.exp(-x))


def _silu(x):
    return x * _sigmoid(x)


def _softplus(x):
    return jnp.maximum(x, 0.0) + jnp.log(1.0 + jnp.exp(-jnp.abs(x)))


def _resid_ln(scale, h, branch, g, b):
    r = ALPHA * h + scale * branch
    mu = jnp.mean(r, axis=-1, keepdims=True)
    var = jnp.mean(jnp.square(r - mu), axis=-1, keepdims=True)
    return (r - mu) * lax.rsqrt(var + LN_EPS) * g + b


def _rms(t, w):
    return t * lax.rsqrt(jnp.mean(t * t, axis=-1, keepdims=True) + RMS_EPS) * w


def _branch_weights(l1, l2, l3):
    m = jnp.maximum(jnp.maximum(l1, l2), l3)
    e1, e2, e3 = jnp.exp(l1 - m), jnp.exp(l2 - m), jnp.exp(l3 - m)
    inv = 1.0 / (e1 + e2 + e3)
    return e1 * inv, e2 * inv, e3 * inv


def _gate(y, xs, z, dskip, w):
    return _rms((y + dskip * xs) * _silu(z), w)


def _rot(x):
    d = lax.broadcasted_iota(jnp.int32, x.shape, 1) % HEAD_DIM
    up = pltpu.roll(x, x.shape[1] - ROPE_DIM // 2, 1)
    down = jnp.where(d < ROPE_DIM, pltpu.roll(x, ROPE_DIM // 2, 1), 0.0)
    return jnp.where(d < ROPE_DIM // 2, up, down)


def ffn_gate_up(name, h, wg, wu, after=()):
    m, nf = h.shape[0], wg.shape[1]
    tn = _tile(nf, MM_TN)

    def body(h_ref, g_w, u_w, *rest):
        du_ref, dg_ref, a_ref, at_ref = rest[-4:]
        hb = h_ref[...].astype(BF16)
        g = _dot(hb, g_w[...].astype(BF16), _NN)
        u = _dot(hb, u_w[...].astype(BF16), _NN)
        sig = _sigmoid(g)
        gs = g * sig
        du_ref[...] = gs.astype(du_ref.dtype)
        dg_ref[...] = (u * (sig + gs * (1.0 - sig))).astype(dg_ref.dtype)
        a = gs * u
        a_ref[...] = a.astype(a_ref.dtype)
        at_ref[...] = a.T.astype(at_ref.dtype)

    in_specs, args = _mm_specs(name, [(h, wg, 'nn')], nf, MM_TM, tn)
    in_specs.append(in_specs[1])
    in_specs += [pl.BlockSpec(memory_space=pl.ANY) for _ in after]
    tile = pl.BlockSpec((MM_TM, tn), lambda j, i: (i, j))
    return pl.pallas_call(
        body, name=name, grid=(nf // tn, m // MM_TM), in_specs=in_specs,
        out_specs=[tile] * 3 + [pl.BlockSpec((tn, MM_TM), lambda j, i: (j, i))],
        out_shape=[jax.ShapeDtypeStruct((m, nf), BF16)] * 3 + [jax.ShapeDtypeStruct((nf, m), BF16)],
        compiler_params=_cparams(("parallel", "parallel")),
    )(*args, wu, *after)


def mm_tn_cat(name, a, bs, out_dtype=F32, tt=1024):
    t, k = a.shape
    widths = [b.shape[1] for b in bs]
    n, tk, tt = sum(widths), _tile(k, (512, 256, 128)), min(tt, t)
    n_t = t // tt

    def body(a_ref, *rest):
        b_refs, o_ref, acc_ref = rest[:len(bs)], rest[-2], rest[-1]
        s = pl.program_id(1)
        at = a_ref[...].astype(BF16)
        d = jnp.concatenate([_dot(at, b[...].astype(BF16), _TN) for b in b_refs], axis=1)

        @pl.when(s == 0)
        def _():
            acc_ref[...] = d

        @pl.when(s > 0)
        def _():
            acc_ref[...] += d

        @pl.when(s == n_t - 1)
        def _():
            o_ref[...] = acc_ref[...].astype(o_ref.dtype)

    return pl.pallas_call(
        body, name=name, grid=(k // tk, n_t),
        in_specs=[pl.BlockSpec((tt, tk), lambda kk, s: (s, kk))] + [pl.BlockSpec((tt, w), lambda kk, s: (s, 0)) for w in widths],
        out_specs=pl.BlockSpec((tk, n), lambda kk, s: (kk, 0)),
        out_shape=jax.ShapeDtypeStruct((k, n), out_dtype), scratch_shapes=[pltpu.VMEM((tk, n), F32)],
        compiler_params=_cparams(("parallel", "arbitrary")),
    )(a, *bs)


def mm_acc(name, a, b, out_dtype=F32, tt=1024, after=None):
    k, t = a.shape
    n = b.shape[1]
    tk, tn, tt = _tile(k, (1024, 512, 256, 128)), _tile(n, MM_TN), min(tt, t)
    n_t = t // tt
    order = [] if after is None else [after]

    def body(a_ref, b_ref, *rest):
        o_ref, acc_ref = rest[-2:]
        s = pl.program_id(2)
        d = _dot(a_ref[...].astype(BF16), b_ref[...].astype(BF16), _NN)

        @pl.when(s == 0)
        def _():
            acc_ref[...] = d

        @pl.when(s > 0)
        def _():
            acc_ref[...] += d

        @pl.when(s == n_t - 1)
        def _():
            o_ref[...] = acc_ref[...].astype(o_ref.dtype)

    return pl.pallas_call(
        body, name=name, grid=(k // tk, n // tn, n_t),
        in_specs=[pl.BlockSpec((tk, tt), lambda kk, nn, s: (kk, s)), pl.BlockSpec((tt, tn), lambda kk, nn, s: (s, nn))]
        + [pl.BlockSpec(memory_space=pl.ANY) for _ in order],
        out_specs=pl.BlockSpec((tk, tn), lambda kk, nn, s: (kk, nn)),
        out_shape=jax.ShapeDtypeStruct((k, n), out_dtype), scratch_shapes=[pltpu.VMEM((tk, tn), F32)],
        compiler_params=_cparams(("parallel", "parallel", "arbitrary")),
    )(a, b, *order)


def ffn_da_act(name, df, wd, a_du, a_dg):
    m, nf = df.shape[0], wd.shape[0]
    tn = _tile(nf, MM_TN)

    def body(df_ref, w_ref, adu_ref, adg_ref, dg_ref, du_ref):
        da = _dot(df_ref[...].astype(BF16), w_ref[...].astype(BF16), _NT)
        dg_ref[...] = (da * adg_ref[...].astype(F32)).astype(dg_ref.dtype)
        du_ref[...] = (da * adu_ref[...].astype(F32)).astype(du_ref.dtype)

    in_specs, args = _mm_specs(name, [(df, wd, 'nt')], nf, MM_TM, tn)
    tile = pl.BlockSpec((MM_TM, tn), lambda j, i: (i, j))
    return pl.pallas_call(
        body, name=name, grid=(nf // tn, m // MM_TM), in_specs=in_specs + [tile, tile], out_specs=[tile] * 2,
        out_shape=[jax.ShapeDtypeStruct((m, nf), BF16)] * 2, compiler_params=_cparams(("parallel", "parallel")),
    )(*args, a_du, a_dg)


def resid_ln_fwd(name, scale, h, branch, ln_g, ln_b):
    t = h.shape[0]

    def fn(*a):
        y = _resid_ln(scale, *a)
        return y, y

    return rowwise(name, fn, [h, branch], [ln_g, ln_b], [(t, D_MODEL, F32), (t, D_MODEL, BF16)], tm=512)


def ffn_fwd(tag, hb, wg, wu, wd, after=()):
    g, u, a, at = ffn_gate_up(f"{tag}_gate_up", hb, wg, wu, after)
    f = mm(f"{tag}_down", [(a, wd, 'nn')], D_MODEL, out_dtype=BF16)
    return f, (hb, g, u, at)


def ln_loss_bwd(name, h, branch, target, ln_g, ln_b):
    t, dm = h.shape

    def fn(h_, br_, tgt, g_, b_):
        y, vjp = jax.vjp(functools.partial(_resid_ln, 0.5), h_, br_, g_, b_)
        e = y - tgt
        return (*vjp(e * (1.0 / dm)), jnp.sum(e * e, axis=0, keepdims=True))

    return rowwise(name, fn, [h, branch, target], [ln_g, ln_b], [(t, dm, F32), (t, dm, BF16)],
                   accs=[(1, dm), (1, dm), (1, dm)], tm=512)


def resid_ln_bwd(name, scale, h, branch, ln_g, ln_b, dout, extra=None):
    t = h.shape[0]

    def fn(h_, br_, do_, *rest):
        g_, b_ = rest[-2], rest[-1]
        _, vjp = jax.vjp(functools.partial(_resid_ln, scale), h_, br_, g_, b_)
        dh, dbr, dg, db = vjp(do_)
        if extra is not None:
            dh = dh + rest[0]
        return dh, dbr, dg, db

    ins = [h, branch, dout] + ([extra] if extra is not None else [])
    return rowwise(name, fn, ins, [ln_g, ln_b], [(t, D_MODEL, F32), (t, D_MODEL, BF16)],
                   accs=[(1, D_MODEL), (1, D_MODEL)], tm=512)


def ffn_bwd(tag, res, wg, wu, wd, df, dh_resid):
    hb, g, u, at = res
    dg, du = ffn_da_act(f"{tag}_bwd_da_act", df, wd, g, u)
    dwd = mm_acc(f"{tag}_bwd_dwd", at, df, BF16)
    dh = mm(f"{tag}_bwd_dh", [(dg, wg, 'nt'), (du, wu, 'nt')], D_MODEL, add=dh_resid, tn=512)
    dwg = mm_tn(f"{tag}_bwd_dwg", hb, dg, BF16)
    dwu = mm_tn(f"{tag}_bwd_dwu", hb, du, BF16)
    return dh, dwg, dwu, dwd


def rope_tables(positions):
    inv_freq = ROPE_THETA ** (-jnp.arange(0, ROPE_DIM, 2, dtype=F32) / ROPE_DIM)
    ang = positions.reshape(-1, 1).astype(F32) * inv_freq
    c, s = jnp.cos(ang), jnp.sin(ang)
    t = ang.shape[0]
    cosv = jnp.concatenate([c, c, jnp.ones((t, HEAD_DIM - ROPE_DIM), F32)], axis=1)
    sinv = jnp.concatenate([-s, s, jnp.zeros((t, HEAD_DIM - ROPE_DIM), F32)], axis=1)
    return jnp.tile(cosv, (1, 2)), jnp.tile(sinv, (1, 2))


def _pair_masks():
    lane = lax.broadcasted_iota(jnp.int32, (1, LANES), 1)
    return (lane < HEAD_DIM, lane >= HEAD_DIM)


def _band_masks():
    row = lax.broadcasted_iota(jnp.int32, (ATTN_BLOCK, ATTN_BLOCK), 0)
    col = lax.broadcasted_iota(jnp.int32, (ATTN_BLOCK, ATTN_BLOCK), 1)
    return col >= row, col <= row


def _residue_blocks():
    out = []
    for g, d in enumerate(DILATIONS):
        for r in range(d):
            for i in range(SEQ // d // ATTN_BLOCK):
                rows = lambda j: pl.ds(r + j * ATTN_BLOCK * d, ATTN_BLOCK, stride=d) if d > 1 else pl.ds(j * ATTN_BLOCK, ATTN_BLOCK)
                out.append((g, rows(i), rows(i - 1) if i > 0 else None))
    return out


N_HEAD_PAIRS = D_ATTN // LANES
SCALE = HEAD_DIM ** -0.5
ATTN_GROUP = 4


def _block_operands(qr, kr, v_ref, cur, prev):
    prev_ok, cur_ok = _band_masks()
    if prev is None:
        return qr[cur, :], kr[cur, :].astype(BF16), v_ref[cur, :], cur_ok
    kcat = jnp.concatenate([kr[prev, :], kr[cur, :]], axis=0).astype(BF16)
    vcat = jnp.concatenate([v_ref[prev, :], v_ref[cur, :]], axis=0)
    return qr[cur, :], kcat, vcat, jnp.concatenate([prev_ok, cur_ok], axis=1)


def _attn_specs(b):
    col = lambda cb: pl.BlockSpec((SEQ, LANES), lambda bb, hp: (bb, cb + hp))
    tab = pl.BlockSpec((SEQ, LANES), lambda bb, hp: (bb, 0))
    return col, tab


def attn_fwd(qkvz, cosv, sinv, b):
    t = qkvz.shape[0]
    col, tab = _attn_specs(b)
    blocks = _residue_blocks()

    def body(q_ref, k_ref, v_ref, c_ref, s_ref, o_ref, l1_ref, l2_ref, l3_ref, qr, kr, o1, o2, o3):
        l_refs, o_scr = (l1_ref, l2_ref, l3_ref), (o1, o2, o3)
        c, s = c_ref[...], s_ref[...]
        q, k = q_ref[...], k_ref[...]
        qr[...] = q * c + _rot(q) * s
        kr[...] = k * c + _rot(k) * s
        masks = _pair_masks()
        for lo in range(0, len(blocks), ATTN_GROUP):
            chains = []
            for g, cur, prev in blocks[lo:lo + ATTN_GROUP]:
                q2, kcat, vcat, ok = _block_operands(qr, kr, v_ref, cur, prev)
                for m in masks:
                    qm = jnp.where(m, q2, 0.0).astype(BF16)
                    chains.append(dict(g=g, cur=cur, m=m, v=jnp.where(m, vcat, 0.0).astype(BF16),
                                       s=jnp.where(ok, _dot(qm, kcat, _NT) * SCALE, NEG)))
            for ch in chains:
                mx = jnp.max(ch['s'], axis=1, keepdims=True)
                p = jnp.exp(ch['s'] - mx)
                den = jnp.sum(p, axis=1, keepdims=True)
                ch.update(p=p.astype(BF16), inv=1.0 / den, lse=mx + jnp.log(den))
            for ch in chains:
                ch['o'] = _dot(ch['p'], ch['v'], _NN) * ch['inv']
            for c0, c1 in zip(chains[0::2], chains[1::2]):
                o_scr[c0['g']][c0['cur'], :] = c0['o'] + c1['o']
                l_refs[c0['g']][c0['cur'], :] = jnp.where(c0['m'], c0['lse'], c1['lse'])
        w1, w2, w3 = _branch_weights(l1_ref[...], l2_ref[...], l3_ref[...])
        o_ref[...] = w1 * o1[...] + w2 * o2[...] + w3 * o3[...]

    shp = jax.ShapeDtypeStruct((t, D_ATTN), F32)
    return pl.pallas_call(
        body, name="attn_fwd", grid=(b, N_HEAD_PAIRS),
        in_specs=[col(0), col(N_HEAD_PAIRS), col(2 * N_HEAD_PAIRS), tab, tab],
        out_specs=[col(0)] * 4, out_shape=[shp] * 4,
        scratch_shapes=[pltpu.VMEM((SEQ, LANES), F32)] * 5,
        compiler_params=_cparams(("parallel", "parallel")),
    )(qkvz, qkvz, qkvz, cosv, sinv)


def attn_bwd(qkvz, cosv, sinv, dmix, mixed, lses, b):
    t = qkvz.shape[0]
    col, tab = _attn_specs(b)
    blocks = _residue_blocks()
    hd = np.arange(LANES) // HEAD_DIM
    head_ones = jnp.asarray((hd[:, None] == hd[None, :]).astype(np.float32))

    def body(q_ref, k_ref, v_ref, c_ref, s_ref, dm_ref, mx_ref, l1_ref, l2_ref, l3_ref, ones_ref,
             dq_out, dk_out, dv_out, qr, kr, do1, do2, do3, dd1, dd2, dd3, dq_ref, dk_ref, dv_ref):
        l_refs, do_scr, dd_scr = (l1_ref, l2_ref, l3_ref), (do1, do2, do3), (dd1, dd2, dd3)
        c, s = c_ref[...], s_ref[...]
        q, k = q_ref[...], k_ref[...]
        qr[...] = q * c + _rot(q) * s
        kr[...] = k * c + _rot(k) * s
        dm = dm_ref[...]
        tot = _dot(dm * mx_ref[...], ones_ref[...], _NN, HI)
        for w, do_g, dd_g in zip(_branch_weights(l1_ref[...], l2_ref[...], l3_ref[...]), do_scr, dd_scr):
            do_g[...] = w * dm
            dd_g[...] = w * tot
        dq_ref[...] = jnp.zeros((SEQ, LANES), F32)
        dk_ref[...] = jnp.zeros((SEQ, LANES), F32)
        dv_ref[...] = jnp.zeros((SEQ, LANES), F32)
        masks = _pair_masks()
        for lo in range(0, len(blocks), ATTN_GROUP):
            chains = []
            for g, cur, prev in blocks[lo:lo + ATTN_GROUP]:
                q2, kcat, vcat, ok = _block_operands(qr, kr, v_ref, cur, prev)
                vcat = vcat.astype(BF16)
                do2_, l2, dd2_ = do_scr[g][cur, :], l_refs[g][cur, :], dd_scr[g][cur, :]
                l2s, dd2s = pltpu.roll(l2, HEAD_DIM, 1), pltpu.roll(dd2_, HEAD_DIM, 1)
                for m in masks:
                    qm = jnp.where(m, q2, 0.0).astype(BF16)
                    dom = jnp.where(m, do2_, 0.0).astype(BF16)
                    lrep, ddrep = jnp.where(m, l2, l2s), jnp.where(m, dd2_, dd2s)
                    if prev is not None:
                        lrep, ddrep = jnp.concatenate([lrep, lrep], axis=1), jnp.concatenate([ddrep, ddrep], axis=1)
                    chains.append(dict(cur=cur, prev=prev, qm=qm, dom=dom, km=jnp.where(m, kcat, 0), lrep=lrep, ddrep=ddrep,
                                       s=jnp.where(ok, _dot(qm, kcat, _NT) * SCALE, NEG), dp=_dot(dom, vcat, _NT)))
            for ch in chains:
                p = jnp.exp(ch['s'] - ch['lrep'])
                ch.update(p=p.astype(BF16), ds=(p * (ch['dp'] - ch['ddrep']) * SCALE).astype(BF16))
            for ch in chains:
                ch.update(dq=_dot(ch['ds'], ch['km'], _NN), dk=_dot(ch['ds'], ch['qm'], _TN), dv=_dot(ch['p'], ch['dom'], _TN))
            for c0, c1 in zip(chains[0::2], chains[1::2]):
                cur, prev = c0['cur'], c0['prev']
                dk, dv = c0['dk'] + c1['dk'], c0['dv'] + c1['dv']
                dq_ref[cur, :] += c0['dq'] + c1['dq']
                if prev is None:
                    dk_ref[cur, :] += dk
                    dv_ref[cur, :] += dv
                else:
                    dk_ref[prev, :] += dk[:ATTN_BLOCK]
                    dv_ref[prev, :] += dv[:ATTN_BLOCK]
                    dk_ref[cur, :] += dk[ATTN_BLOCK:]
                    dv_ref[cur, :] += dv[ATTN_BLOCK:]
        dq, dk = dq_ref[...], dk_ref[...]
        dq_out[...] = (dq * c + _rot(dq * s)).astype(dq_out.dtype)
        dk_out[...] = (dk * c + _rot(dk * s)).astype(dk_out.dtype)
        dv_out[...] = dv_ref[...].astype(dv_out.dtype)

    shp = jax.ShapeDtypeStruct((t, D_ATTN), BF16)
    return pl.pallas_call(
        body, name="attn_bwd", grid=(b, N_HEAD_PAIRS),
        in_specs=[col(0), col(N_HEAD_PAIRS), col(2 * N_HEAD_PAIRS), tab, tab, col(0), col(0), col(0), col(0), col(0),
                  pl.BlockSpec((LANES, LANES), lambda bb, hp: (0, 0))],
        out_specs=[col(0)] * 3, out_shape=[shp] * 3,
        scratch_shapes=[pltpu.VMEM((SEQ, LANES), F32)] * 11,
        compiler_params=_cparams(("parallel", "parallel")),
    )(qkvz, qkvz, qkvz, cosv, sinv, dmix, mixed, *lses, head_ones)


def attn_norm_fwd(mixed, norm_w):
    return rowwise("attn_norm", _rms, [mixed], [norm_w], [(mixed.shape[0], D_ATTN, BF16)])[0]


def attn_norm_bwd(dout, mixed, norm_w):
    def fn(dy, mx, w):
        _, vjp = jax.vjp(_rms, mx, w)
        return vjp(dy)

    return rowwise("attn_norm_bwd", fn, [dout, mixed], [norm_w], [(dout.shape[0], D_ATTN, F32)], accs=[(1, D_ATTN)])


CONV_TM = 256
HALO = 8


def _conv_columns(refs):
    xs_ref, bm_ref, cm_ref = refs
    out = []
    for c in range(D_CONV // LANES):
        lo = c * LANES
        ref, base = (xs_ref, 0) if lo < D_SSD else (bm_ref, D_SSD) if lo < D_SSD + D_BC else (cm_ref, D_SSD + D_BC)
        out.append((slice(lo, lo + LANES), (ref, slice(lo - base, lo - base + LANES))))
    return out


def _conv_taps(scr, w_ref, cs, first_row, step, tm):
    acc = None
    for k in range(CONV_WIDTH):
        term = w_ref[k:k + 1, cs] * scr[pl.ds(first_row + step * k, tm), cs]
        acc = term if acc is None else acc + term
    return acc


def conv_fwd(u, w, bias):
    t = u.shape[0]
    tm, per_seq = CONV_TM, SEQ // CONV_TM

    def body(u_ref, h_ref, w_ref, b_ref, xs_ref, bm_ref, cm_ref, scr):
        first = pl.program_id(0) % per_seq == 0
        scr[0:HALO, :] = jnp.where(first, 0.0, h_ref[...])
        scr[HALO:, :] = u_ref[...]
        for cs, (o_ref, os_) in _conv_columns((xs_ref, bm_ref, cm_ref)):
            o_ref[:, os_] = _silu(_conv_taps(scr, w_ref, cs, HALO - CONV_WIDTH + 1, 1, tm) + b_ref[:, cs])

    return pl.pallas_call(
        body, name="conv_fwd", grid=(t // tm,),
        in_specs=[pl.BlockSpec((tm, D_CONV), lambda i: (i, 0)),
                  pl.BlockSpec((HALO, D_CONV), lambda i: (jnp.maximum(i * (tm // HALO) - 1, 0), 0)),
                  pl.BlockSpec((CONV_WIDTH, D_CONV), lambda i: (0, 0)), pl.BlockSpec((1, D_CONV), lambda i: (0, 0))],
        out_specs=[pl.BlockSpec((tm, D_SSD), lambda i: (i, 0)), pl.BlockSpec((tm, D_BC), lambda i: (i, 0)),
                   pl.BlockSpec((tm, D_BC), lambda i: (i, 0))],
        out_shape=[jax.ShapeDtypeStruct((t, D_SSD), F32), jax.ShapeDtypeStruct((t, D_BC), F32),
                   jax.ShapeDtypeStruct((t, D_BC), F32)],
        scratch_shapes=[pltpu.VMEM((tm + HALO, D_CONV), F32)],
        compiler_params=_cparams(("parallel",)),
    )(u, u, w, bias)


def conv_bwd(u, w, bias, dxs_a, dxs_b, dbm, dcm):
    t = u.shape[0]
    tm, per_seq = CONV_TM, SEQ // CONV_TM
    n_tiles = t // tm

    def body1(u_ref, h_ref, dxs_ref, dxs2_ref, dbm_ref, dcm_ref, w_ref, b_ref, dz_ref, dw_ref, db_ref, scr):
        i = pl.program_id(0)
        first = i % per_seq == 0
        scr[0:HALO, :] = jnp.where(first, 0.0, h_ref[...])
        scr[HALO:, :] = u_ref[...]

        @pl.when(i == 0)
        def _():
            dw_ref[...] = jnp.zeros(dw_ref.shape, F32)
            db_ref[...] = jnp.zeros(db_ref.shape, F32)
        for cs, (g_ref, gs) in _conv_columns((dxs_ref, dbm_ref, dcm_ref)):
            acc = _conv_taps(scr, w_ref, cs, HALO - CONV_WIDTH + 1, 1, tm) + b_ref[:, cs]
            sig = _sigmoid(acc)
            dy = g_ref[:, gs] + dxs2_ref[:, gs] if g_ref is dxs_ref else g_ref[:, gs]
            dz = dy * sig * (1.0 + acc * (1.0 - sig))
            dz_ref[:, cs] = dz
            db_ref[:, cs] += jnp.sum(dz, axis=0, keepdims=True)
            for k in range(CONV_WIDTH):
                dw_ref[k:k + 1, cs] += jnp.sum(dz * scr[pl.ds(HALO - CONV_WIDTH + 1 + k, tm), cs], axis=0, keepdims=True)

    dz, dw, db = pl.pallas_call(
        body1, name="conv_bwd_dz", grid=(n_tiles,),
        in_specs=[pl.BlockSpec((tm, D_CONV), lambda i: (i, 0)),
                  pl.BlockSpec((HALO, D_CONV), lambda i: (jnp.maximum(i * (tm // HALO) - 1, 0), 0)),
                  pl.BlockSpec((tm, D_SSD), lambda i: (i, 0)), pl.BlockSpec((tm, D_SSD), lambda i: (i, 0)),
                  pl.BlockSpec((tm, D_BC), lambda i: (i, 0)), pl.BlockSpec((tm, D_BC), lambda i: (i, 0)),
                  pl.BlockSpec((CONV_WIDTH, D_CONV), lambda i: (0, 0)), pl.BlockSpec((1, D_CONV), lambda i: (0, 0))],
        out_specs=[pl.BlockSpec((tm, D_CONV), lambda i: (i, 0)), pl.BlockSpec((CONV_WIDTH, D_CONV), lambda i: (0, 0)),
                   pl.BlockSpec((1, D_CONV), lambda i: (0, 0))],
        out_shape=[jax.ShapeDtypeStruct((t, D_CONV), F32), jax.ShapeDtypeStruct((CONV_WIDTH, D_CONV), F32),
                   jax.ShapeDtypeStruct((1, D_CONV), F32)],
        scratch_shapes=[pltpu.VMEM((tm + HALO, D_CONV), F32)],
        compiler_params=_cparams(("arbitrary",)),
    )(u, u, dxs_a, dxs_b, dbm, dcm, w, bias)

    def body2(dz_ref, n_ref, w_ref, du_ref, scr):
        last = pl.program_id(0) % per_seq == per_seq - 1
        scr[0:tm, :] = dz_ref[...]
        scr[tm:, :] = jnp.where(last, 0.0, n_ref[...])
        for c in range(D_CONV // LANES):
            cs = slice(c * LANES, (c + 1) * LANES)
            du_ref[:, cs] = _conv_taps(scr, w_ref, cs, CONV_WIDTH - 1, -1, tm).astype(du_ref.dtype)

    du = pl.pallas_call(
        body2, name="conv_bwd_du", grid=(n_tiles,),
        in_specs=[pl.BlockSpec((tm, D_CONV), lambda i: (i, 0)),
                  pl.BlockSpec((HALO, D_CONV), lambda i: (jnp.minimum((i + 1) * (tm // HALO), t // HALO - 1), 0)),
                  pl.BlockSpec((CONV_WIDTH, D_CONV), lambda i: (0, 0))],
        out_specs=pl.BlockSpec((tm, D_CONV), lambda i: (i, 0)),
        out_shape=jax.ShapeDtypeStruct((t, D_CONV), BF16),
        scratch_shapes=[pltpu.VMEM((tm + HALO, D_CONV), F32)],
        compiler_params=_cparams(("parallel",)),
    )(dz, dz, w)
    return du, dw, db


Q = SSD_CHUNK
N_PAIRS = D_SSD // LANES
HEADS_PER_GROUP = N_HEADS // SSD_GROUPS


def _rep(a, j):
    return jnp.broadcast_to(a[:, j:j + 1], a.shape)


def _dot_exact01(a, b, dn, a_is_01):
    x = b if a_is_01 else a
    hi = x.astype(BF16)
    mid = (x - hi.astype(F32)).astype(BF16)
    lo = (x - hi.astype(F32) - mid.astype(F32)).astype(BF16)
    z = a.astype(BF16) if a_is_01 else b.astype(BF16)
    out = None
    for term in (hi, mid, lo):
        d = _dot(z, term, dn) if a_is_01 else _dot(term, z, dn)
        out = d if out is None else out + d
    return out


def _pad_lanes(v, fill=0.0):
    row = jnp.pad(v.reshape(1, -1).astype(F32), ((0, 0), (0, LANES - v.size)), constant_values=fill)
    return row, row.reshape(LANES, 1)


def _ssd_common(dtr_ref, dtrt_ref, bias_r, bias_c, alog_r, alog_c):
    row = lax.broadcasted_iota(jnp.int32, (Q, Q), 0)
    col = lax.broadcasted_iota(jnp.int32, (Q, Q), 1)
    tril = row >= col
    lane = lax.broadcasted_iota(jnp.int32, (1, LANES), 1)
    a_r = jnp.where(lane < N_HEADS, -jnp.exp(alog_r[...]), 0.0)
    sub = lax.broadcasted_iota(jnp.int32, (LANES, 1), 0)
    a_c = jnp.where(sub < N_HEADS, -jnp.exp(alog_c[...]), 0.0)
    dt = _softplus(dtr_ref[...] + bias_r[...])
    cs = _dot_exact01(tril, dt * a_r, _NN, True)
    dtt = _softplus(dtrt_ref[...] + bias_c[...])
    cst = _dot_exact01(dtt * a_c, row <= col, _NN, False)
    return tril, lane, a_r, dt, cs, cst


def _ssd_specs(b, nc, rev):
    ci = (lambda c: nc - 1 - c) if rev else (lambda c: c)
    rows = lambda w: pl.BlockSpec((Q, w), lambda bb, c: (bb * nc + ci(c), 0))
    dtt = pl.BlockSpec((LANES, Q), lambda bb, c: (0, bb * nc + ci(c)))
    const = lambda s: pl.BlockSpec(s, lambda bb, c: (0,) * len(s))
    state = pl.BlockSpec((None, N_PAIRS, LANES, SSD_STATE), lambda bb, c: (bb * nc + ci(c), 0, 0, 0))
    return rows, dtt, const, state


def ssd_fwd(xs, bm, cm, dtraw, dt_bias, a_log, b):
    t = xs.shape[0]
    nc = SEQ // Q
    rows, dtt_spec, const, state = _ssd_specs(b, nc, False)
    bias_r, bias_c = _pad_lanes(dt_bias)
    alog_r, alog_c = _pad_lanes(a_log)

    def body(xs_ref, b_ref, c_ref, dtr_ref, dtrt_ref, br, bc, ar, ac, y_ref, hp_ref, h_scr):
        @pl.when(pl.program_id(1) == 0)
        def _():
            h_scr[...] = jnp.zeros(h_scr.shape, F32)
        tril, lane, _, dt, cs, cst = _ssd_common(dtr_ref, dtrt_ref, br, bc, ar, ac)
        sub = lax.broadcasted_iota(jnp.int32, (LANES, 1), 0)
        y_acc = [jnp.zeros((Q, LANES), F32) for _ in range(N_PAIRS)]
        h_old = [h_scr[p] for p in range(N_PAIRS)]
        h_new = [jnp.zeros((LANES, SSD_STATE), F32) for _ in range(N_PAIRS)]
        for g in range(SSD_GROUPS):
            bg = b_ref[:, g * SSD_STATE:(g + 1) * SSD_STATE].astype(BF16)
            cg = c_ref[:, g * SSD_STATE:(g + 1) * SSD_STATE].astype(BF16)
            cb = _dot(cg, bg, _NT)
            heads = []
            for j in range(g * HEADS_PER_GROUP, (g + 1) * HEADS_PER_GROUP):
                p, side = j // 2, j % 2
                m = (lane < HEAD_DIM) if side == 0 else (lane >= HEAD_DIM)
                ms = (sub < HEAD_DIM) if side == 0 else (sub >= HEAD_DIM)
                csj, dtj = _rep(cs, j), _rep(dt, j)
                lmat = jnp.exp(jnp.where(tril, csj - cst[j:j + 1, :], NEG))
                xdt = jnp.where(m, xs_ref[:, p * LANES:(p + 1) * LANES] * dtj, 0.0)
                hm = jnp.where(ms, h_old[p], 0.0)
                last = csj[Q - 1:Q, :]
                heads.append(dict(p=p, hm=hm, ecs=jnp.exp(csj), el=jnp.exp(last), gmat=(cb * lmat).astype(BF16),
                                  xdt=xdt.astype(BF16), xd=(xdt * jnp.exp(last - csj)).astype(BF16)))
            for h in heads:
                h.update(ydiag=_dot(h['gmat'], h['xdt'], _NN), ch=_dot(cg, h['hm'].astype(BF16), _NT), sj=_dot(h['xd'], bg, _TN))
            for h in heads:
                y_acc[h['p']] = y_acc[h['p']] + h['ydiag'] + h['ecs'] * h['ch']
                h_new[h['p']] = h_new[h['p']] + h['el'] * h['hm'] + h['sj']
        for p in range(N_PAIRS):
            y_ref[:, p * LANES:(p + 1) * LANES] = y_acc[p]
            hp_ref[p] = h_old[p]
            h_scr[p] = h_new[p]

    return pl.pallas_call(
        body, name="ssd_fwd", grid=(b, nc),
        in_specs=[rows(D_SSD), rows(D_BC), rows(D_BC), rows(LANES), dtt_spec, const((1, LANES)), const((LANES, 1)),
                  const((1, LANES)), const((LANES, 1))],
        out_specs=[rows(D_SSD), state],
        out_shape=[jax.ShapeDtypeStruct((t, D_SSD), F32),
                   jax.ShapeDtypeStruct((b * nc, N_PAIRS, LANES, SSD_STATE), F32)],
        scratch_shapes=[pltpu.VMEM((N_PAIRS, LANES, SSD_STATE), F32)],
        compiler_params=_cparams(("parallel", "arbitrary")),
    )(xs, bm, cm, dtraw, dtraw.T, bias_r, bias_c, alog_r, alog_c)


def ssd_bwd(xs, bm, cm, dtraw, dt_bias, a_log, hprev, dy, b):
    t = xs.shape[0]
    nc = SEQ // Q
    rows, dtt_spec, const, state = _ssd_specs(b, nc, True)
    bias_r, bias_c = _pad_lanes(dt_bias)
    alog_r, alog_c = _pad_lanes(a_log)

    def body(xs_ref, b_ref, c_ref, dtr_ref, dtrt_ref, hp_ref, dy_ref, br, bc, ar, ac,
             dxs_ref, db_ref, dc_ref, ddt_ref, dbias_ref, dalog_ref, dh_scr):
        first = jnp.logical_and(pl.program_id(0) == 0, pl.program_id(1) == 0)

        @pl.when(pl.program_id(1) == 0)
        def _():
            dh_scr[...] = jnp.zeros(dh_scr.shape, F32)

        @pl.when(first)
        def _():
            dbias_ref[...] = jnp.zeros(dbias_ref.shape, F32)
            dalog_ref[...] = jnp.zeros(dalog_ref.shape, F32)
        tril, lane, a_r, dt, cs, cst = _ssd_common(dtr_ref, dtrt_ref, br, bc, ar, ac)
        sub = lax.broadcasted_iota(jnp.int32, (LANES, 1), 0)
        rowq = lax.broadcasted_iota(jnp.int32, (Q, 1), 0)
        triu = (lax.broadcasted_iota(jnp.int32, (Q, Q), 0) <= lax.broadcasted_iota(jnp.int32, (Q, Q), 1)).astype(F32)
        dxs_acc = [jnp.zeros((Q, LANES), F32) for _ in range(N_PAIRS)]
        dh_in = [dh_scr[p] for p in range(N_PAIRS)]
        h_in = [hp_ref[p] for p in range(N_PAIRS)]
        dh_out = [jnp.zeros((LANES, SSD_STATE), F32) for _ in range(N_PAIRS)]
        ddt = jnp.zeros((Q, LANES), F32)
        dalog = jnp.zeros((1, LANES), F32)
        for g in range(SSD_GROUPS):
            gs = slice(g * SSD_STATE, (g + 1) * SSD_STATE)
            bg, cg = b_ref[:, gs].astype(BF16), c_ref[:, gs].astype(BF16)
            cb = _dot(cg, bg, _NT)
            dcb = jnp.zeros((Q, Q), F32)
            dbg = jnp.zeros((Q, SSD_STATE), F32)
            dcg = jnp.zeros((Q, SSD_STATE), F32)
            heads = []
            for j in range(g * HEADS_PER_GROUP, (g + 1) * HEADS_PER_GROUP):
                p, side = j // 2, j % 2
                m = (lane < HEAD_DIM) if side == 0 else (lane >= HEAD_DIM)
                ms = (sub < HEAD_DIM) if side == 0 else (sub >= HEAD_DIM)
                csj, dtj = _rep(cs, j), _rep(dt, j)
                lmat = jnp.exp(jnp.where(tril, csj - cst[j:j + 1, :], NEG))
                x2 = jnp.where(m, xs_ref[:, p * LANES:(p + 1) * LANES], 0.0)
                xdt = x2 * dtj
                dym = jnp.where(m, dy_ref[:, p * LANES:(p + 1) * LANES], 0.0)
                hm = jnp.where(ms, h_in[p], 0.0)
                dhm = jnp.where(ms, dh_in[p], 0.0)
                last = csj[Q - 1:Q, :]
                decay = jnp.exp(last - csj)
                heads.append(dict(j=j, p=p, dtj=dtj, lmat=lmat, x2=x2, hm=hm, dhm=dhm, decay=decay, el=jnp.exp(last),
                                  gmat=cb * lmat, dym=dym.astype(BF16), xdt=xdt.astype(BF16), hmb=hm.astype(BF16),
                                  dhmb=dhm.astype(BF16), dye=dym * jnp.exp(csj), xd=xdt * decay))
            for h in heads:
                dyeb, xdb = h['dye'].astype(BF16), h['xd'].astype(BF16)
                h.update(dg=_dot(h['dym'], h['xdt'], _NT),
                         dxdt=_dot(h['gmat'].astype(BF16), h['dym'], _TN),
                         ch=_dot(cg, h['hmb'], _NT),
                         dcg=_dot(dyeb, h['hmb'], _NN), dhp=_dot(dyeb, cg, _TN),
                         wmat=_dot(bg, h['dhmb'], _NT),
                         dbg=_dot(xdb, h['dhmb'], _NN))
            for h in heads:
                ej = h['dg'] * h['gmat']
                col_sums = jnp.broadcast_to(jnp.sum(ej, axis=0, keepdims=True), (Q, Q)).T
                ddl = jnp.sum(h['xd'] * h['wmat'], axis=1, keepdims=True)
                dlast = jnp.sum(ddl, axis=0, keepdims=True) + h['el'] * jnp.sum(
                    jnp.sum(h['dhm'] * h['hm'], axis=1, keepdims=True), axis=0, keepdims=True)
                h['dcs'] = (jnp.sum(ej, axis=1, keepdims=True) - col_sums + jnp.sum(h['dye'] * h['ch'], axis=1, keepdims=True)
                            - ddl + jnp.where(rowq == Q - 1, dlast, 0.0))
                h['dxdt'] = h['dxdt'] + h['decay'] * h['wmat']
                dcb, dcg, dbg = dcb + h['dg'] * h['lmat'], dcg + h['dcg'], dbg + h['dbg']
                dh_out[h['p']] = dh_out[h['p']] + h['el'] * h['dhm'] + h['dhp']
            for h in heads:
                h['da'] = _dot_exact01(triu, h['dcs'], _NN, True)
            for h in heads:
                j, da = h['j'], h['da']
                aj = jnp.sum(jnp.where(lane == j, a_r, 0.0), axis=1, keepdims=True)
                ddtj = da * aj + jnp.sum(h['dxdt'] * h['x2'], axis=1, keepdims=True)
                ddt = ddt + jnp.where(lane == j, ddtj, 0.0)
                dalog = dalog + jnp.where(lane == j, jnp.sum(da * h['dtj'], axis=0, keepdims=True) * aj, 0.0)
                dxs_acc[h['p']] = dxs_acc[h['p']] + h['dxdt'] * h['dtj']
            dcbb = dcb.astype(BF16)
            dc_ref[:, gs] = dcg + _dot(dcbb, bg, _NN)
            db_ref[:, gs] = dbg + _dot(dcbb, cg, _TN)
        for p in range(N_PAIRS):
            dxs_ref[:, p * LANES:(p + 1) * LANES] = dxs_acc[p]
            dh_scr[p] = dh_out[p]
        ddtraw = ddt * _sigmoid(dtr_ref[...] + br[...])
        ddt_ref[...] = ddtraw
        dbias_ref[...] += jnp.sum(ddtraw, axis=0, keepdims=True)
        dalog_ref[...] += dalog

    return pl.pallas_call(
        body, name="ssd_bwd", grid=(b, nc),
        in_specs=[rows(D_SSD), rows(D_BC), rows(D_BC), rows(LANES), dtt_spec, state, rows(D_SSD), const((1, LANES)),
                  const((LANES, 1)), const((1, LANES)), const((LANES, 1))],
        out_specs=[rows(D_SSD), rows(D_BC), rows(D_BC), rows(LANES), const((1, LANES)), const((1, LANES))],
        out_shape=[jax.ShapeDtypeStruct((t, D_SSD), F32), jax.ShapeDtypeStruct((t, D_BC), F32),
                   jax.ShapeDtypeStruct((t, D_BC), F32), jax.ShapeDtypeStruct((t, LANES), F32),
                   jax.ShapeDtypeStruct((1, LANES), F32), jax.ShapeDtypeStruct((1, LANES), F32)],
        scratch_shapes=[pltpu.VMEM((N_PAIRS, LANES, SSD_STATE), F32)],
        compiler_params=_cparams(("arbitrary", "arbitrary")),
    )(xs, bm, cm, dtraw, dtraw.T, hprev, dy, bias_r, bias_c, alog_r, alog_c)


def _split_w_in(w_in):
    w_dt = jnp.pad(w_in[:, D_QKVZ + D_CONV:], ((0, 0), (0, LANES - N_HEADS)))
    return w_in[:, :D_QKVZ], w_in[:, D_QKVZ:D_QKVZ + D_CONV], w_dt


def mixer_fwd(hb, p, cosv, sinv, b):
    t = hb.shape[0]
    w_a, w_b, w_c = _split_w_in(p['w_in'])
    qkvz = mm("in_qkvz", [(hb, w_a, 'nn')], D_QKVZ)
    xbc = mm("in_xbc", [(hb, w_b, 'nn')], D_CONV)
    dtraw = mm("in_dt", [(hb, w_c, 'nn')], LANES)
    mixed, *lses = attn_fwd(qkvz, cosv, sinv, b)
    attn = attn_norm_fwd(mixed, p['attn_norm_w'])
    xs, bm, cm = conv_fwd(xbc, p['conv_w'], p['conv_b'])
    y, hprev = ssd_fwd(xs, bm, cm, dtraw, p['dt_bias'], p['a_log'], b)
    dskip = jnp.repeat(p['d_skip'].reshape(-1), HEAD_DIM).reshape(1, D_SSD)
    yg, = rowwise("ssd_gate", _gate, [y, xs, Op(qkvz, D_SSD, 3)], [dskip, p['ssd_norm_w']], [(t, D_SSD, BF16)])
    mix = mm("out_proj", [(attn, p['w_out'][:D_ATTN], 'nn'), (yg, p['w_out'][D_ATTN:], 'nn')], D_MODEL, out_dtype=BF16)
    res = dict(hb=hb, qkvz=qkvz, xbc=xbc, dtraw=dtraw, mixed=mixed, lses=lses, attn=attn, xs=xs, bm=bm, cm=cm,
               y=y, hprev=hprev, dskip=dskip, yg=yg, cosv=cosv, sinv=sinv)
    return mix, res


def mixer_bwd(r, p, dmix, dh_resid, b):
    t = dmix.shape[0]
    w_a, w_b, w_c = _split_w_in(p['w_in'])
    w_out = p['w_out']
    dattn = mm("out_bwd_dattn", [(dmix, w_out[:D_ATTN], 'nt')], D_ATTN)
    dyg = mm("out_bwd_dyg", [(dmix, w_out[D_ATTN:], 'nt')], D_SSD)
    dw_out = jnp.concatenate([mm_tn("out_bwd_dw_a", r['attn'], dmix, BF16),
                              mm_tn("out_bwd_dw_y", r['yg'], dmix, BF16)], axis=0)

    def gate_bwd(dy_, y_, xs_, z_, ds_, w_):
        _, vjp = jax.vjp(_gate, y_, xs_, z_, ds_, w_)
        return vjp(dy_)

    dy, dxs_a, dz, ddskip, dssd_norm = rowwise(
        "ssd_gate_bwd", gate_bwd, [dyg, r['y'], r['xs'], Op(r['qkvz'], D_SSD, 3)], [r['dskip'], p['ssd_norm_w']],
        [(t, D_SSD, F32), (t, D_SSD, F32), (t, D_SSD, BF16)], accs=[(1, D_SSD), (1, D_SSD)])
    dxs_b, dbm, dcm, ddtraw, ddt_bias, da_log = ssd_bwd(r['xs'], r['bm'], r['cm'], r['dtraw'], p['dt_bias'], p['a_log'],
                                                        r['hprev'], dy, b)
    dxbc, dconv_w, dconv_b = conv_bwd(r['xbc'], p['conv_w'], p['conv_b'], dxs_a, dxs_b, dbm, dcm)
    dmixed, dattn_norm = attn_norm_bwd(dattn, r['mixed'], p['attn_norm_w'])
    dq, dk, dv = attn_bwd(r['qkvz'], r['cosv'], r['sinv'], dmixed, r['mixed'], r['lses'], b)
    wq, wk, wv, wz = (w_a[:, i * D_ATTN:(i + 1) * D_ATTN] for i in range(4))
    dh = mm("in_bwd_dh", [(dq, wq, 'nt'), (dk, wk, 'nt'), (dv, wv, 'nt'), (dz, wz, 'nt'), (dxbc, w_b, 'nt'),
                          (ddtraw, w_c, 'nt')], D_MODEL, add=dh_resid, tn=512)
    h = r['hb']
    dw_in = jnp.concatenate([mm_tn_cat("in_bwd_dw_qkvz", h, [dq, dk, dv, dz], BF16),
                             mm_tn_cat("in_bwd_dw_xbc_dt", h, [dxbc, ddtraw], BF16)[:, :D_CONV + N_HEADS]], axis=1)
    head_sum = lambda v: v.reshape(N_HEADS, HEAD_DIM).sum(axis=1).reshape(1, N_HEADS)
    grads = dict(w_in=dw_in, w_out=dw_out, conv_w=dconv_w, conv_b=dconv_b, dt_bias=ddt_bias[:, :N_HEADS],
                 a_log=da_log[:, :N_HEADS], d_skip=head_sum(ddskip), attn_norm_w=dattn_norm, ssd_norm_w=dssd_norm)
    return dh, grads


FFN2_KEYS = ('ffn2_gate', 'ffn2_up', 'ffn2_down')
MIXER_KEYS = ('w_in', 'conv_w', 'w_out')
FFN_COL = ('ffn1_gate', 'ffn1_up', 'ffn2_gate', 'ffn2_up')
FFN_ROW = ('ffn1_down', 'ffn2_down')
CONV_W_COMM = (8, 2 * LANES)
SMALL = 'small'


def comm_shape(k, shapes):
    if k in FFN_COL:
        return (D_MODEL, FF_PAD)
    if k in FFN_ROW:
        return (FF_PAD, D_MODEL)
    if k == 'conv_w':
        return CONV_W_COMM
    return tuple(shapes[k][1:])


def to_comm(k, vals, shapes):
    a = vals[k].reshape(shapes[k][1:])
    r_, c_ = comm_shape(k, shapes)
    return jnp.pad(a, ((0, r_ - a.shape[0]), (0, c_ - a.shape[1])))


SMALL_ROWS, SMALL_COLS = 16, D_CONV


def pack_small(small):
    rows = [jnp.pad(small[r].reshape(1, -1), ((0, 0), (0, SMALL_COLS - small[r].size))) for r in REPLICATED]
    return jnp.concatenate(rows + [jnp.zeros((SMALL_ROWS - len(rows), SMALL_COLS), F32)], axis=0)


def full_weight(k, g):
    if k in FFN_COL:
        return g
    if k == 'conv_w':
        return jnp.transpose(g[:, :CONV_WIDTH, :D_CONV // N_DEV], (1, 0, 2)).reshape(CONV_WIDTH, D_CONV)
    return g.reshape(N_DEV * g.shape[1], g.shape[2])


def grad_shards(k, g):
    if k in FFN_COL:
        return g
    if k == 'conv_w':
        s = jnp.transpose(g.reshape(CONV_WIDTH, N_DEV, D_CONV // N_DEV), (1, 0, 2))
        return jnp.pad(s, ((0, 0), (0, CONV_W_COMM[0] - CONV_WIDTH), (0, CONV_W_COMM[1] - D_CONV // N_DEV)))
    return g.reshape(N_DEV, g.shape[0] // N_DEV, g.shape[1])


def _flip(v, bit):
    return 1 - v if bit else v


N_PEER_COPIES = N_DEV - 1


def _comm_call(name, body, arrs, out_shape):
    n = len(arrs)
    return pl.pallas_call(
        functools.partial(body, n), name=name, out_shape=out_shape,
        in_specs=[pl.BlockSpec(memory_space=pl.ANY)] * n, out_specs=[pl.BlockSpec(memory_space=pl.ANY)] * n,
        scratch_shapes=[pltpu.SemaphoreType.DMA((n * N_PEER_COPIES,)), pltpu.SemaphoreType.DMA((n * N_PEER_COPIES,)),
                        pltpu.SemaphoreType.DMA((n,))],
    )(*arrs)


def _blk(ref, idx, by_cols):
    if not by_cols:
        return ref.at[idx]
    c = ref.shape[1] // N_DEV
    return ref.at[:, pl.ds(pl.multiple_of(idx * c, LANES), c)]


def _blocked_shape(a, by_cols):
    return (a.shape[0], N_DEV * a.shape[1]) if by_cols else (N_DEV,) + a.shape


def all_gather(arrs, by_cols):
    def body(n, *refs):
        x_refs, out_refs, (send_sems, recv_sems, local_sems) = refs[:n], refs[n:2 * n], refs[2 * n:]
        x, y, c = lax.axis_index("x"), lax.axis_index("y"), lax.axis_index("c")
        me, sibling = (x, y, c), (x, y, 1 - c)
        chips = [(1 - x, y), (x, 1 - y), (1 - x, 1 - y)]

        def copy(a, k, block, to, src=None):
            px, py, pc = block
            dst = _blk(out_refs[a], 4 * px + 2 * py + pc, by_cols[a])
            return pltpu.make_async_remote_copy(
                src_ref=dst if src is None else src, dst_ref=dst, send_sem=send_sems.at[a * N_PEER_COPIES + k],
                recv_sem=recv_sems.at[a * N_PEER_COPIES + k], device_id=to, device_id_type=MESH)

        mine = [pltpu.make_async_copy(x_refs[a], _blk(out_refs[a], 4 * x + 2 * y + c, by_cols[a]), local_sems.at[a])
                for a in range(n)]
        started = []
        for a in range(n):
            mine[a].start()
            first = [copy(a, 0, me, sibling, src=x_refs[a])]
            first += [copy(a, 1 + j, me, (*chip, c), src=x_refs[a]) for j, chip in enumerate(chips)]
            for cp in first:
                cp.start()
            started += first
        for j, chip in enumerate(chips):
            for a in range(n):
                copy(a, 1 + j, (*chip, c), me).wait_recv()
                cp = copy(a, 4 + j, (*chip, c), sibling)
                cp.start()
                started.append(cp)
        for a in range(n):
            copy(a, 0, sibling, me).wait_recv()
            for j, chip in enumerate(chips):
                copy(a, 4 + j, (*chip, 1 - c), me).wait_recv()
        for cp in started:
            cp.wait_send()
        for cp in mine:
            cp.wait()

    return _comm_call("all_gather_weights", body, arrs,
                      [jax.ShapeDtypeStruct(_blocked_shape(a, bc), a.dtype) for a, bc in zip(arrs, by_cols)])


def blocks_to_cols(arrs):
    def body(*refs):
        for i, o in zip(refs[:len(arrs)], refs[len(arrs):]):
            o[...] = i[...]

    return pl.pallas_call(
        body, name="blocks_to_cols", grid=(N_DEV,),
        in_specs=[pl.BlockSpec((None,) + a.shape[1:], lambda p: (p, 0, 0)) for a in arrs],
        out_specs=[pl.BlockSpec(a.shape[1:], lambda p: (0, p)) for a in arrs],
        out_shape=[jax.ShapeDtypeStruct((a.shape[1], N_DEV * a.shape[2]), a.dtype) for a in arrs],
        compiler_params=_cparams(("parallel",)),
    )(*arrs)


def _landing_shape(a, by_cols):
    return (N_DEV, a.shape[0], a.shape[1] // N_DEV) if by_cols else a.shape


_HBM = pl.BlockSpec(memory_space=pltpu.HBM)
_SEM = pl.BlockSpec(memory_space=pltpu.SEMAPHORE)
_EFFECT = pltpu.SideEffectType.DATAFLOW_SIDE_EFFECTING


def _peer(k):
    x, y, c = lax.axis_index("x"), lax.axis_index("y"), lax.axis_index("c")
    return _flip(x, k & 4), _flip(y, k & 2), _flip(c, k & 1)


def _my_index():
    return 4 * lax.axis_index("x") + 2 * lax.axis_index("y") + lax.axis_index("c")


def _split_copies(mode, by_cols, src_refs, land_refs, send_sems, recv_sems):
    me = _my_index()
    out = []
    for a, bc in enumerate(by_cols):
        for k in range(1, N_DEV):
            px, py, pc = _peer(k)
            src = _blk(src_refs[a], 4 * px + 2 * py + pc, bc) if mode == 'scatter' else src_refs[a]
            dst = land_refs[a].at[me] if mode == 'scatter' else _blk(land_refs[a], me, bc)
            out.append(pltpu.make_async_remote_copy(
                src_ref=src, dst_ref=dst, send_sem=send_sems.at[a * N_PEER_COPIES + k - 1],
                recv_sem=recv_sems.at[a * N_PEER_COPIES + k - 1], device_id=(px, py, pc), device_id_type=MESH))
    return out


def exchange_start(name, mode, srcs, by_cols):
    n = len(srcs)
    lands = [lax.empty(_landing_shape(s, bc) if mode == 'scatter' else _blocked_shape(s, bc), s.dtype)
             for s, bc in zip(srcs, by_cols)]

    def body(*refs):
        src_refs, land_refs, send_sems, recv_sems = refs[:n], refs[n:2 * n], refs[2 * n], refs[2 * n + 1]
        for cp in _split_copies(mode, by_cols, src_refs, land_refs, send_sems, recv_sems):
            cp.start()
        refs[-1][...] = jnp.zeros(refs[-1].shape, F32)

    sems = pltpu.SemaphoreType.DMA((n * N_PEER_COPIES,))
    res = pl.pallas_call(
        body, name=name,
        out_shape=(sems, sems, *[pltpu.HBM(a.shape, a.dtype) for a in srcs + lands], jax.ShapeDtypeStruct((8, LANES), F32)),
        in_specs=(_HBM,) * (2 * n), out_specs=(_SEM, _SEM, *(_HBM,) * (2 * n), pl.BlockSpec(memory_space=pltpu.VMEM)),
        input_output_aliases={i: 2 + i for i in range(2 * n)},
        compiler_params=pltpu.CompilerParams(has_side_effects=_EFFECT),
    )(*[pltpu.with_memory_space_constraint(a, pltpu.HBM) for a in srcs + lands])
    return (mode, by_cols, res[:-1]), res[-1]


def exchange_wait(name, handles, after):
    mode, by_cols, (send_sems, recv_sems, *bufs) = handles
    n = len(by_cols)

    def body(*refs):
        src_refs, land_refs, s_sems, r_sems = refs[:n], refs[n:2 * n], refs[2 * n], refs[2 * n + 1]
        for cp in _split_copies(mode, by_cols, src_refs, land_refs, s_sems, r_sems):
            cp.wait_send()
            cp.wait_recv()

    res = pl.pallas_call(
        body, name=name, out_shape=tuple(pltpu.HBM(a.shape, a.dtype) for a in bufs),
        in_specs=(*(_HBM,) * (2 * n), _SEM, _SEM, pl.BlockSpec(memory_space=pl.ANY)), out_specs=(_HBM,) * (2 * n),
        input_output_aliases={i: i for i in range(2 * n)},
        compiler_params=pltpu.CompilerParams(has_side_effects=_EFFECT),
    )(*bufs, send_sems, recv_sems, after)
    me, out = _my_index(), []
    for src, land, bc in zip(res[:n], res[n:], by_cols):
        if mode == 'scatter':
            c = land.shape[2]
            own = lax.dynamic_slice(src, (0, me * c), (src.shape[0], c)) if bc else lax.dynamic_index_in_dim(src, me, 0, False)
            out.append(lax.dynamic_update_slice(land, own[None], (me, 0, 0)))
        elif bc:
            out.append(lax.dynamic_update_slice(land, src, (0, me * src.shape[1])))
        else:
            out.append(lax.dynamic_update_slice(land, src[None], (me, 0, 0)))
    return out


def _adamw_math(g, w, m, v):
    c1 = 1.0 / (1.0 - ADAM_B1 ** ADAM_STEP)
    c2 = 1.0 / (1.0 - ADAM_B2 ** ADAM_STEP)
    m = ADAM_B1 * m + (1.0 - ADAM_B1) * g
    v = ADAM_B2 * v + (1.0 - ADAM_B2) * jnp.square(g)
    return g, -ADAM_LR * ((m * c1) / (jnp.sqrt(v * c2) + ADAM_EPS) + ADAM_WD * w), m, v


def adamw(name, recv, w, m, v, tm):
    _, rows, cols = w.shape
    tm = min(tm, rows)

    def body(*refs):
        g = refs[0][0:tm, 0:cols].astype(F32)
        for s in range(1, N_DEV):
            g = g + refs[s][0:tm, 0:cols].astype(F32)
        res = _adamw_math(g, *[r[...] for r in refs[N_DEV:N_DEV + 3]])
        for r, val in zip(refs[N_DEV + 3:], res):
            r[...] = val

    part = lambda s: pl.BlockSpec((None, recv.shape[1] if tm == rows else tm, recv.shape[2]), lambda i: (s, i, 0))
    tile = pl.BlockSpec((None, tm, cols), lambda i: (0, i, 0))
    return pl.pallas_call(
        body, name=name, grid=(rows // tm,), in_specs=[part(s) for s in range(N_DEV)] + [tile] * 3, out_specs=[tile] * 4,
        out_shape=[jax.ShapeDtypeStruct((1, rows, cols), F32)] * 4, compiler_params=_cparams(("parallel",)),
    )(*[recv] * N_DEV, w, m, v)


def adamw_small(recv, wl, ml, vl):
    n = len(REPLICATED)

    def body(recv_ref, *refs):
        g = recv_ref[0]
        for s in range(1, N_DEV):
            g = g + recv_ref[s]
        for r in range(n):
            w, m, v = (refs[j * n + r][...] for j in range(3))
            for j, val in enumerate(_adamw_math(g[r:r + 1, :w.shape[1]], w, m, v)):
                refs[(3 + j) * n + r][...] = val

    arrs = [d[k].reshape(1, -1) for d in (wl, ml, vl) for k in REPLICATED]
    res = pl.pallas_call(
        body, name="adamw_small", out_shape=[jax.ShapeDtypeStruct(a.shape, F32) for a in arrs[:n]] * 4,
    )(recv, *arrs)
    return [{k: res[j * n + r].reshape(wl[k].shape) for r, k in enumerate(REPLICATED)} for j in range(4)]


ADAMW_TM = {'ffn1_gate': 256, 'ffn1_up': 256, 'ffn2_gate': 256, 'ffn2_up': 256, 'w_in': 32}


def kernel(x, positions, ln1_g, ln1_b, ffn1_gate, ffn1_up, ffn1_down, w_in, conv_w, conv_b, dt_bias, a_log, d_skip, attn_norm_w, ssd_norm_w, w_out, ln2_g, ln2_b, ffn2_gate, ffn2_up, ffn2_down, ln3_g, ln3_b, loss_target, m_ln1_g, m_ln1_b, m_ffn1_gate, m_ffn1_up, m_ffn1_down, m_w_in, m_conv_w, m_conv_b, m_dt_bias, m_a_log, m_d_skip, m_attn_norm_w, m_ssd_norm_w, m_w_out, m_ln2_g, m_ln2_b, m_ffn2_gate, m_ffn2_up, m_ffn2_down, m_ln3_g, m_ln3_b, v_ln1_g, v_ln1_b, v_ffn1_gate, v_ffn1_up, v_ffn1_down, v_w_in, v_conv_w, v_conv_b, v_dt_bias, v_a_log, v_d_skip, v_attn_norm_w, v_ssd_norm_w, v_w_out, v_ln2_g, v_ln2_b, v_ffn2_gate, v_ffn2_up, v_ffn2_down, v_ln3_g, v_ln3_b):
    args = dict(locals())
    wl = {k: args[k] for k in WEIGHTS}
    ml = {k: args["m_" + k] for k in WEIGHTS}
    vl = {k: args["v_" + k] for k in WEIGHTS}
    shapes = {k: wl[k].shape for k in WEIGHTS}
    b, s, dm = x.shape
    t = b * s

    sent = {k: to_comm(k, wl, shapes).astype(F32 if k == 'conv_w' else BF16) for k in SHARDED}
    by_cols = lambda keys: [k in FFN_COL for k in keys]
    gate, up = all_gather([sent['ffn1_gate'], sent['ffn1_up']], [False] * 2)
    (gate, up), sent = lax.optimization_barrier(((gate, up), sent))
    p = dict(zip(('ffn1_gate', 'ffn1_up'), blocks_to_cols([gate, up])))
    gather_down, token_d = exchange_start("gather_ffn1_down_start", 'gather', [sent['ffn1_down']], [False])
    sent['w_in'] = sent['w_in'] + token_d[0, 0].astype(BF16)
    gather_mixer, token_m = exchange_start("gather_mixer_start", 'gather', [sent[k] for k in MIXER_KEYS], by_cols(MIXER_KEYS))
    sent['ffn2_gate'] = sent['ffn2_gate'] + token_m[0, 0].astype(BF16)
    gather_ffn2, token_f = exchange_start("gather_ffn2_start", 'gather', [sent[k] for k in FFN2_KEYS], by_cols(FFN2_KEYS))
    for k in REPLICATED:
        p[k] = wl[k].reshape(1, -1)

    x2 = x.reshape(t, dm)
    cosv, sinv = rope_tables(positions)
    g1, u1, a1, at1 = ffn_gate_up("ffn1_gate_up", x2, p['ffn1_gate'], p['ffn1_up'], after=(token_d, token_m, token_f))
    p['ffn1_down'] = full_weight('ffn1_down', exchange_wait("gather_ffn1_down_wait", gather_down, a1)[0])
    f1, res1 = mm("ffn1_down", [(a1, p['ffn1_down'], 'nn')], D_MODEL, out_dtype=BF16), (x2, g1, u1, at1)
    h1, h1b = resid_ln_fwd("ln1", 0.5, x2, f1, p['ln1_g'], p['ln1_b'])
    for k, g in zip(MIXER_KEYS, exchange_wait("gather_mixer_wait", gather_mixer, h1b)):
        p[k] = full_weight(k, g)
    mix, resm = mixer_fwd(h1b, p, cosv, sinv, b)
    h2, h2b = resid_ln_fwd("ln2", 1.0, h1, mix, p['ln2_g'], p['ln2_b'])
    for k, g in zip(FFN2_KEYS, exchange_wait("gather_ffn2_wait", gather_ffn2, h2b)):
        p[k] = full_weight(k, g)
    f2, res3 = ffn_fwd("ffn2", h2b, p['ffn2_gate'], p['ffn2_up'], p['ffn2_down'])

    small, full = {}, {}
    dh2_res, df2, small['ln3_g'], small['ln3_b'], sq = ln_loss_bwd("ln3_loss_bwd", h2, f2, loss_target.reshape(t, dm),
                                                                   p['ln3_g'], p['ln3_b'])
    loss = lax.psum(jnp.sum(sq) * (0.5 / dm), AXES)

    dh2, full['ffn2_gate'], full['ffn2_up'], full['ffn2_down'] = ffn_bwd("ffn2", res3, p['ffn2_gate'], p['ffn2_up'],
                                                                       p['ffn2_down'], df2, dh2_res)
    ffn2_exchange, token = exchange_start("grads_ffn2_start", 'scatter', [grad_shards(k, full[k]) for k in FFN2_KEYS],
                                          by_cols(FFN2_KEYS))
    dh1_res, dmix, small['ln2_g'], small['ln2_b'] = resid_ln_bwd("ln2_bwd", 1.0, h1, mix, p['ln2_g'] + token[:1, :1],
                                                                 p['ln2_b'], dh2)
    dh1, gm = mixer_bwd(resm, p, dmix, dh1_res, b)
    for k in ('conv_b', 'dt_bias', 'a_log', 'd_skip', 'attn_norm_w', 'ssd_norm_w'):
        small[k] = gm[k]
    mixer_exchange, token = exchange_start("grads_mixer_start", 'scatter', [grad_shards(k, gm[k]) for k in MIXER_KEYS],
                                           by_cols(MIXER_KEYS))
    dx_res, df1, small['ln1_g'], small['ln1_b'] = resid_ln_bwd("ln1_bwd", 0.5, x2, f1, p['ln1_g'] + token[:1, :1],
                                                               p['ln1_b'], dh1)
    hb, g, u, at = res1
    small_part = pack_small(small)
    dg, du = ffn_da_act("ffn1_bwd_da_act", df1, p['ffn1_down'], g, u)
    dwd = mm_acc("ffn1_bwd_dwd", at, df1, BF16, after=dg)
    down_exchange, token = exchange_start("grads_ffn1_down_start", 'scatter', [
        grad_shards('ffn1_down', dwd), jnp.broadcast_to(small_part[None], (N_DEV,) + small_part.shape)], [False, False])
    dwg = mm_tn("ffn1_bwd_dwg", hb, dg, BF16, after=token)
    gate_exchange, token = exchange_start("grads_ffn1_gate_start", 'scatter', [grad_shards('ffn1_gate', dwg)], [True])
    dwu = mm_tn("ffn1_bwd_dwu", hb, du, BF16, after=token)
    up_exchange, token = exchange_start("grads_ffn1_up_start", 'scatter', [grad_shards('ffn1_up', dwu)], [True])
    dx = mm("ffn1_bwd_dh", [(dg, p['ffn1_gate'], 'nt'), (du, p['ffn1_up'], 'nt')], D_MODEL, add=dx_res, tn=512, after=token)
    recv = {}
    for keys, name, ex in (((FFN2_KEYS), "grads_ffn2_wait", ffn2_exchange), (MIXER_KEYS, "grads_mixer_wait", mixer_exchange),
                           (('ffn1_down', SMALL), "grads_ffn1_down_wait", down_exchange),
                           (('ffn1_gate',), "grads_ffn1_gate_wait", gate_exchange),
                           (('ffn1_up',), "grads_ffn1_up_wait", up_exchange)):
        recv.update(zip(keys, exchange_wait(name, ex, dx)))
    outs = adamw_small(recv.pop(SMALL), wl, ml, vl)
    for k, r in recv.items():
        for o, a in zip(outs, adamw(f"adamw_{k}", r, wl[k], ml[k], vl[k], ADAMW_TM.get(k, shapes[k][1]))):
            o[k] = a
    return (loss, dx.reshape(b, s, dm), *[o[k] for o in outs for k in WEIGHTS])
```

```python
import functools

import jax
import jax.numpy as jnp
import numpy as np
from jax import lax
from jax.experimental import pallas as pl
from jax.experimental.pallas import tpu as pltpu

F32, BF16 = jnp.float32, jnp.bfloat16
HI = lax.Precision.HIGHEST
MESH = pl.DeviceIdType.MESH
AXES = ("x", "y", "c")
N_DEV = 8

D_MODEL = 1024
SEQ = 2048
HEAD_DIM = 64
N_HEADS = 12
D_ATTN = N_HEADS * HEAD_DIM
DILATIONS = (1, 4, 16)
ATTN_BLOCK = 128
ROPE_THETA = 500000.0
ROPE_DIM = 16
D_SSD = 768
SSD_GROUPS = 4
SSD_STATE = 128
SSD_CHUNK = 128
D_BC = SSD_GROUPS * SSD_STATE
D_CONV = D_SSD + 2 * D_BC
CONV_WIDTH = 4
D_QKVZ = 3 * D_ATTN + D_SSD
D_FF = 2816
ALPHA = 2.0 ** 0.25
LN_EPS = 1e-5
RMS_EPS = 1e-6
ADAM_LR, ADAM_B1, ADAM_B2, ADAM_EPS, ADAM_WD, ADAM_STEP = 0.001, 0.9, 0.999, 1e-08, 0.01, 10

LANES = 128
VMEM_LIMIT = 52 * 1024 * 1024
NEG = -1e30

WEIGHTS = ['ln1_g', 'ln1_b', 'ffn1_gate', 'ffn1_up', 'ffn1_down', 'w_in', 'conv_w', 'conv_b', 'dt_bias', 'a_log',
           'd_skip', 'attn_norm_w', 'ssd_norm_w', 'w_out', 'ln2_g', 'ln2_b', 'ffn2_gate', 'ffn2_up', 'ffn2_down',
           'ln3_g', 'ln3_b']
COL_SHARDED = ('ffn1_gate', 'ffn1_up', 'conv_w', 'ffn2_gate', 'ffn2_up')
ROW_SHARDED = ('ffn1_down', 'w_in', 'w_out', 'ffn2_down')
SHARDED = tuple(n for n in WEIGHTS if n in COL_SHARDED or n in ROW_SHARDED)
REPLICATED = tuple(n for n in WEIGHTS if n not in SHARDED)
FF_SHARD = D_FF // N_DEV
FF_PAD = -(-FF_SHARD // LANES) * LANES


def _cparams(sem=None):
    return pltpu.CompilerParams(dimension_semantics=sem, vmem_limit_bytes=VMEM_LIMIT)


def _tile(n, prefs):
    for p in prefs:
        if n % p == 0:
            return p
    return n


class Op:
    def __init__(self, arr, bw=None, cb=0, ro=0):
        self.arr, self.bw, self.cb, self.ro = arr, (arr.shape[1] if bw is None else bw), cb, ro


def _op(a):
    return a if isinstance(a, Op) else Op(a)


def rowwise(name, fn, ins, consts, outs, accs=(), tm=256):
    ins = [_op(a) for a in ins]
    rows = outs[0][0]
    n_in, n_c, n_o, n_a = len(ins), len(consts), len(outs), len(accs)
    tm = min(tm, rows)
    assert rows % tm == 0, (name, rows, tm)

    def body(*refs):
        vals = [r[...].astype(F32) for r in refs[:n_in + n_c]]
        res = fn(*vals)
        res = res if isinstance(res, (tuple, list)) else (res,)
        o_refs = refs[n_in + n_c:n_in + n_c + n_o]
        a_refs = refs[n_in + n_c + n_o:]
        for r, v in zip(o_refs, res[:n_o]):
            r[...] = v.astype(r.dtype)
        if n_a:
            @pl.when(pl.program_id(0) == 0)
            def _():
                for r in a_refs:
                    r[...] = jnp.zeros(r.shape, r.dtype)
            for r, v in zip(a_refs, res[n_o:]):
                r[...] += v

    in_specs = [pl.BlockSpec((tm, o.bw), functools.partial(lambda i, o: (i + o.ro, o.cb), o=o)) for o in ins]
    in_specs += [pl.BlockSpec(c.shape, functools.partial(lambda i, nd: (0,) * nd, nd=c.ndim)) for c in consts]
    out_specs = [pl.BlockSpec((tm, w), lambda i: (i, 0)) for (_, w, _) in outs]
    out_specs += [pl.BlockSpec(s, functools.partial(lambda i, nd: (0,) * nd, nd=len(s))) for s in accs]
    out_shape = [jax.ShapeDtypeStruct((r, w), dt) for (r, w, dt) in outs]
    out_shape += [jax.ShapeDtypeStruct(s, F32) for s in accs]
    res = pl.pallas_call(
        body, name=name, grid=(rows // tm,), in_specs=in_specs, out_specs=out_specs, out_shape=out_shape,
        compiler_params=_cparams(("arbitrary",) if n_a else ("parallel",)),
    )(*[o.arr for o in ins], *consts)
    return res


MM_TM = 1024
MM_TN = (1024, 896, 768, 512, 256, 128)
_NT = (((1,), (1,)), ((), ()))
_NN = (((1,), (0,)), ((), ()))
_TN = (((0,), (0,)), ((), ()))


def _dot(a, b, dn, precision=None):
    return lax.dot_general(a, b, dn, preferred_element_type=F32, precision=precision)


def _mm_specs(name, pairs, n_out, tm, tn):
    in_specs, args = [], []
    for a, b, mode in pairs:
        o = _op(a)
        in_specs.append(pl.BlockSpec((tm, o.bw), functools.partial(lambda j, i, o: (i, o.cb), o=o)))
        args.append(o.arr)
        if mode == 'nn':
            assert b.shape == (o.bw, n_out), (name, b.shape, o.bw, n_out)
            in_specs.append(pl.BlockSpec((o.bw, tn), lambda j, i: (0, j)))
        else:
            assert b.shape == (n_out, o.bw), (name, b.shape, o.bw, n_out)
            in_specs.append(pl.BlockSpec((tn, o.bw), lambda j, i: (j, 0)))
        args.append(b)
    return in_specs, args


def _mm_acc(refs, pairs):
    acc = None
    for k, (_, _, mode) in enumerate(pairs):
        d = _dot(refs[2 * k][...].astype(BF16), refs[2 * k + 1][...].astype(BF16), _NN if mode == 'nn' else _NT)
        acc = d if acc is None else acc + d
    return acc


def mm(name, pairs, n_out, add=None, out_dtype=F32, tm=MM_TM, tn=None, after=None):
    m = _op(pairs[0][0]).arr.shape[0]
    tn = tn or _tile(n_out, MM_TN)
    n_p = len(pairs)

    def body(*refs):
        acc = _mm_acc(refs, pairs)
        if add is not None:
            acc = acc + refs[2 * n_p][...]
        refs[-1][...] = acc.astype(refs[-1].dtype)

    in_specs, args = _mm_specs(name, pairs, n_out, tm, tn)
    tile = pl.BlockSpec((tm, tn), lambda j, i: (i, j))
    if add is not None:
        in_specs.append(tile)
        args.append(add)
    if after is not None:
        in_specs.append(pl.BlockSpec(memory_space=pl.ANY))
        args.append(after)
    return pl.pallas_call(
        body, name=name, grid=(n_out // tn, m // tm), in_specs=in_specs, out_specs=tile,
        out_shape=jax.ShapeDtypeStruct((m, n_out), out_dtype),
        compiler_params=_cparams(("parallel", "parallel")),
    )(*args)


def mm_tn(name, a, b, out_dtype=F32, tt=1024, after=None):
    a, b = _op(a), _op(b)
    t = a.arr.shape[0]
    k, n = a.bw, b.bw
    tk = _tile(k, (512, 896, 768, 256, 128))
    tn = _tile(n, (3072, 1792) + MM_TN)
    tt = min(tt, t)
    n_t = t // tt
    order = [] if after is None else [after]

    def body(a_ref, b_ref, *rest):
        o_ref, acc_ref = rest[-2:]
        s = pl.program_id(2)
        d = _dot(a_ref[...].astype(BF16), b_ref[...].astype(BF16), _TN)

        @pl.when(s == 0)
        def _():
            acc_ref[...] = d

        @pl.when(s > 0)
        def _():
            acc_ref[...] += d

        @pl.when(s == n_t - 1)
        def _():
            o_ref[...] = acc_ref[...].astype(o_ref.dtype)

    return pl.pallas_call(
        body, name=name, grid=(k // tk, n // tn, n_t),
        in_specs=[pl.BlockSpec((tt, tk), functools.partial(lambda kk, nn, s, o: (s, o.cb * (o.bw // tk) + kk), o=a)),
                  pl.BlockSpec((tt, tn), functools.partial(lambda kk, nn, s, o: (s, o.cb * (o.bw // tn) + nn), o=b))]
        + [pl.BlockSpec(memory_space=pl.ANY) for _ in order],
        out_specs=pl.BlockSpec((tk, tn), lambda kk, nn, s: (kk, nn)),
        out_shape=jax.ShapeDtypeStruct((k, n), out_dtype),
        scratch_shapes=[pltpu.VMEM((tk, tn), F32)],
        compiler_params=_cparams(("parallel", "parallel", "arbitrary")),
    )(a.arr, b.arr, *order)


def _sigmoid(x):
    return 1.0 / (1.0 + jnp.exp(-x))


def _silu(x):
    return x * _sigmoid(x)


def _softplus(x):
    return jnp.maximum(x, 0.0) + jnp.log(1.0 + jnp.exp(-jnp.abs(x)))


def _resid_ln(scale, h, branch, g, b):
    r = ALPHA * h + scale * branch
    mu = jnp.mean(r, axis=-1, keepdims=True)
    var = jnp.mean(jnp.square(r - mu), axis=-1, keepdims=True)
    return (r - mu) * lax.rsqrt(var + LN_EPS) * g + b


def _rms(t, w):
    return t * lax.rsqrt(jnp.mean(t * t, axis=-1, keepdims=True) + RMS_EPS) * w


def _branch_weights(l1, l2, l3):
    m = jnp.maximum(jnp.maximum(l1, l2), l3)
    e1, e2, e3 = jnp.exp(l1 - m), jnp.exp(l2 - m), jnp.exp(l3 - m)
    inv = 1.0 / (e1 + e2 + e3)
    return e1 * inv, e2 * inv, e3 * inv


def _gate(y, xs, z, dskip, w):
    return _rms((y + dskip * xs) * _silu(z), w)


def _rot(x):
    d = lax.broadcasted_iota(jnp.int32, x.shape, 1) % HEAD_DIM
    up = pltpu.roll(x, x.shape[1] - ROPE_DIM // 2, 1)
    down = jnp.where(d < ROPE_DIM, pltpu.roll(x, ROPE_DIM // 2, 1), 0.0)
    return jnp.where(d < ROPE_DIM // 2, up, down)


def ffn_gate_up(name, h, wg, wu, after=()):
    m, nf = h.shape[0], wg.shape[1]
    tn = _tile(nf, MM_TN)

    def body(h_ref, g_w, u_w, *rest):
        du_ref, dg_ref, a_ref, at_ref = rest[-4:]
        hb = h_ref[...].astype(BF16)
        g = _dot(hb, g_w[...].astype(BF16), _NN)
        u = _dot(hb, u_w[...].astype(BF16), _NN)
        sig = _sigmoid(g)
        gs = g * sig
        du_ref[...] = gs.astype(du_ref.dtype)
        dg_ref[...] = (u * (sig + gs * (1.0 - sig))).astype(dg_ref.dtype)
        a = gs * u
        a_ref[...] = a.astype(a_ref.dtype)
        at_ref[...] = a.T.astype(at_ref.dtype)

    in_specs, args = _mm_specs(name, [(h, wg, 'nn')], nf, MM_TM, tn)
    in_specs.append(in_specs[1])
    in_specs += [pl.BlockSpec(memory_space=pl.ANY) for _ in after]
    tile = pl.BlockSpec((MM_TM, tn), lambda j, i: (i, j))
    return pl.pallas_call(
        body, name=name, grid=(nf // tn, m // MM_TM), in_specs=in_specs,
        out_specs=[tile] * 3 + [pl.BlockSpec((tn, MM_TM), lambda j, i: (j, i))],
        out_shape=[jax.ShapeDtypeStruct((m, nf), BF16)] * 3 + [jax.ShapeDtypeStruct((nf, m), BF16)],
        compiler_params=_cparams(("parallel", "parallel")),
    )(*args, wu, *after)


def mm_tn_cat(name, a, bs, out_dtype=F32, tt=1024):
    t, k = a.shape
    widths = [b.shape[1] for b in bs]
    n, tk, tt = sum(widths), _tile(k, (512, 256, 128)), min(tt, t)
    n_t = t // tt

    def body(a_ref, *rest):
        b_refs, o_ref, acc_ref = rest[:len(bs)], rest[-2], rest[-1]
        s = pl.program_id(1)
        at = a_ref[...].astype(BF16)
        d = jnp.concatenate([_dot(at, b[...].astype(BF16), _TN) for b in b_refs], axis=1)

        @pl.when(s == 0)
        def _():
            acc_ref[...] = d

        @pl.when(s > 0)
        def _():
            acc_ref[...] += d

        @pl.when(s == n_t - 1)
        def _():
            o_ref[...] = acc_ref[...].astype(o_ref.dtype)

    return pl.pallas_call(
        body, name=name, grid=(k // tk, n_t),
        in_specs=[pl.BlockSpec((tt, tk), lambda kk, s: (s, kk))] + [pl.BlockSpec((tt, w), lambda kk, s: (s, 0)) for w in widths],
        out_specs=pl.BlockSpec((tk, n), lambda kk, s: (kk, 0)),
        out_shape=jax.ShapeDtypeStruct((k, n), out_dtype), scratch_shapes=[pltpu.VMEM((tk, n), F32)],
        compiler_params=_cparams(("parallel", "arbitrary")),
    )(a, *bs)


def mm_acc(name, a, b, out_dtype=F32, tt=1024, after=None):
    k, t = a.shape
    n = b.shape[1]
    tk, tn, tt = _tile(k, (1024, 512, 256, 128)), _tile(n, MM_TN), min(tt, t)
    n_t = t // tt
    order = [] if after is None else [after]

    def body(a_ref, b_ref, *rest):
        o_ref, acc_ref = rest[-2:]
        s = pl.program_id(2)
        d = _dot(a_ref[...].astype(BF16), b_ref[...].astype(BF16), _NN)

        @pl.when(s == 0)
        def _():
            acc_ref[...] = d

        @pl.when(s > 0)
        def _():
            acc_ref[...] += d

        @pl.when(s == n_t - 1)
        def _():
            o_ref[...] = acc_ref[...].astype(o_ref.dtype)

    return pl.pallas_call(
        body, name=name, grid=(k // tk, n // tn, n_t),
        in_specs=[pl.BlockSpec((tk, tt), lambda kk, nn, s: (kk, s)), pl.BlockSpec((tt, tn), lambda kk, nn, s: (s, nn))]
        + [pl.BlockSpec(memory_space=pl.ANY) for _ in order],
        out_specs=pl.BlockSpec((tk, tn), lambda kk, nn, s: (kk, nn)),
        out_shape=jax.ShapeDtypeStruct((k, n), out_dtype), scratch_shapes=[pltpu.VMEM((tk, tn), F32)],
        compiler_params=_cparams(("parallel", "parallel", "arbitrary")),
    )(a, b, *order)


def ffn_da_act(name, df, wd, a_du, a_dg):
    m, nf = df.shape[0], wd.shape[0]
    tn = _tile(nf, MM_TN)

    def body(df_ref, w_ref, adu_ref, adg_ref, dg_ref, du_ref):
        da = _dot(df_ref[...].astype(BF16), w_ref[...].astype(BF16), _NT)
        dg_ref[...] = (da * adg_ref[...].astype(F32)).astype(dg_ref.dtype)
        du_ref[...] = (da * adu_ref[...].astype(F32)).astype(du_ref.dtype)

    in_specs, args = _mm_specs(name, [(df, wd, 'nt')], nf, MM_TM, tn)
    tile = pl.BlockSpec((MM_TM, tn), lambda j, i: (i, j))
    return pl.pallas_call(
        body, name=name, grid=(nf // tn, m // MM_TM), in_specs=in_specs + [tile, tile], out_specs=[tile] * 2,
        out_shape=[jax.ShapeDtypeStruct((m, nf), BF16)] * 2, compiler_params=_cparams(("parallel", "parallel")),
    )(*args, a_du, a_dg)


def resid_ln_fwd(name, scale, h, branch, ln_g, ln_b):
    t = h.shape[0]

    def fn(*a):
        y = _resid_ln(scale, *a)
        return y, y

    return rowwise(name, fn, [h, branch], [ln_g, ln_b], [(t, D_MODEL, F32), (t, D_MODEL, BF16)], tm=512)


def ffn_fwd(tag, hb, wg, wu, wd, after=()):
    g, u, a, at = ffn_gate_up(f"{tag}_gate_up", hb, wg, wu, after)
    f = mm(f"{tag}_down", [(a, wd, 'nn')], D_MODEL, out_dtype=BF16)
    return f, (hb, g, u, at)


def ln_loss_bwd(name, h, branch, target, ln_g, ln_b):
    t, dm = h.shape

    def fn(h_, br_, tgt, g_, b_):
        y, vjp = jax.vjp(functools.partial(_resid_ln, 0.5), h_, br_, g_, b_)
        e = y - tgt
        return (*vjp(e * (1.0 / dm)), jnp.sum(e * e, axis=0, keepdims=True))

    return rowwise(name, fn, [h, branch, target], [ln_g, ln_b], [(t, dm, F32), (t, dm, BF16)],
                   accs=[(1, dm), (1, dm), (1, dm)], tm=512)


def resid_ln_bwd(name, scale, h, branch, ln_g, ln_b, dout, extra=None):
    t = h.shape[0]

    def fn(h_, br_, do_, *rest):
        g_, b_ = rest[-2], rest[-1]
        _, vjp = jax.vjp(functools.partial(_resid_ln, scale), h_, br_, g_, b_)
        dh, dbr, dg, db = vjp(do_)
        if extra is not None:
            dh = dh + rest[0]
        return dh, dbr, dg, db

    ins = [h, branch, dout] + ([extra] if extra is not None else [])
    return rowwise(name, fn, ins, [ln_g, ln_b], [(t, D_MODEL, F32), (t, D_MODEL, BF16)],
                   accs=[(1, D_MODEL), (1, D_MODEL)], tm=512)


def ffn_bwd(tag, res, wg, wu, wd, df, dh_resid):
    hb, g, u, at = res
    dg, du = ffn_da_act(f"{tag}_bwd_da_act", df, wd, g, u)
    dwd = mm_acc(f"{tag}_bwd_dwd", at, df, BF16)
    dh = mm(f"{tag}_bwd_dh", [(dg, wg, 'nt'), (du, wu, 'nt')], D_MODEL, add=dh_resid, tn=512)
    dwg = mm_tn(f"{tag}_bwd_dwg", hb, dg, BF16)
    dwu = mm_tn(f"{tag}_bwd_dwu", hb, du, BF16)
    return dh, dwg, dwu, dwd


def rope_tables(positions):
    inv_freq = ROPE_THETA ** (-jnp.arange(0, ROPE_DIM, 2, dtype=F32) / ROPE_DIM)
    ang = positions.reshape(-1, 1).astype(F32) * inv_freq
    c, s = jnp.cos(ang), jnp.sin(ang)
    t = ang.shape[0]
    cosv = jnp.concatenate([c, c, jnp.ones((t, HEAD_DIM - ROPE_DIM), F32)], axis=1)
    sinv = jnp.concatenate([-s, s, jnp.zeros((t, HEAD_DIM - ROPE_DIM), F32)], axis=1)
    return jnp.tile(cosv, (1, 2)), jnp.tile(sinv, (1, 2))


def _pair_masks():
    lane = lax.broadcasted_iota(jnp.int32, (1, LANES), 1)
    return (lane < HEAD_DIM, lane >= HEAD_DIM)


def _band_masks():
    row = lax.broadcasted_iota(jnp.int32, (ATTN_BLOCK, ATTN_BLOCK), 0)
    col = lax.broadcasted_iota(jnp.int32, (ATTN_BLOCK, ATTN_BLOCK), 1)
    return col >= row, col <= row


def _residue_blocks():
    out = []
    for g, d in enumerate(DILATIONS):
        for r in range(d):
            for i in range(SEQ // d // ATTN_BLOCK):
                rows = lambda j: pl.ds(r + j * ATTN_BLOCK * d, ATTN_BLOCK, stride=d) if d > 1 else pl.ds(j * ATTN_BLOCK, ATTN_BLOCK)
                out.append((g, rows(i), rows(i - 1) if i > 0 else None))
    return out


N_HEAD_PAIRS = D_ATTN // LANES
SCALE = HEAD_DIM ** -0.5
ATTN_GROUP = 4
ATTN_GROUP_BWD = 16


def _block_operands(qr, kr, v_ref, cur, prev):
    prev_ok, cur_ok = _band_masks()
    if prev is None:
        return qr[cur, :], kr[cur, :].astype(BF16), v_ref[cur, :], cur_ok
    kcat = jnp.concatenate([kr[prev, :], kr[cur, :]], axis=0).astype(BF16)
    vcat = jnp.concatenate([v_ref[prev, :], v_ref[cur, :]], axis=0)
    return qr[cur, :], kcat, vcat, jnp.concatenate([prev_ok, cur_ok], axis=1)


def _attn_specs(b):
    col = lambda cb: pl.BlockSpec((SEQ, LANES), lambda bb, hp: (bb, cb + hp))
    tab = pl.BlockSpec((SEQ, LANES), lambda bb, hp: (bb, 0))
    return col, tab


def attn_fwd(qkvz, cosv, sinv, b):
    t = qkvz.shape[0]
    col, tab = _attn_specs(b)
    blocks = _residue_blocks()

    def body(q_ref, k_ref, v_ref, c_ref, s_ref, o_ref, l1_ref, l2_ref, l3_ref, qr, kr, o1, o2, o3):
        l_refs, o_scr = (l1_ref, l2_ref, l3_ref), (o1, o2, o3)
        c, s = c_ref[...], s_ref[...]
        q, k = q_ref[...], k_ref[...]
        qr[...] = q * c + _rot(q) * s
        kr[...] = k * c + _rot(k) * s
        masks = _pair_masks()
        for lo in range(0, len(blocks), ATTN_GROUP):
            chains = []
            for g, cur, prev in blocks[lo:lo + ATTN_GROUP]:
                q2, kcat, vcat, ok = _block_operands(qr, kr, v_ref, cur, prev)
                for m in masks:
                    qm = jnp.where(m, q2, 0.0).astype(BF16)
                    chains.append(dict(g=g, cur=cur, m=m, v=jnp.where(m, vcat, 0.0).astype(BF16),
                                       s=jnp.where(ok, _dot(qm, kcat, _NT) * SCALE, NEG)))
            for ch in chains:
                mx = jnp.max(ch['s'], axis=1, keepdims=True)
                p = jnp.exp(ch['s'] - mx)
                den = jnp.sum(p, axis=1, keepdims=True)
                ch.update(p=p.astype(BF16), inv=1.0 / den, lse=mx + jnp.log(den))
            for ch in chains:
                ch['o'] = _dot(ch['p'], ch['v'], _NN) * ch['inv']
            for c0, c1 in zip(chains[0::2], chains[1::2]):
                o_scr[c0['g']][c0['cur'], :] = c0['o'] + c1['o']
                l_refs[c0['g']][c0['cur'], :] = jnp.where(c0['m'], c0['lse'], c1['lse'])
        w1, w2, w3 = _branch_weights(l1_ref[...], l2_ref[...], l3_ref[...])
        o_ref[...] = w1 * o1[...] + w2 * o2[...] + w3 * o3[...]

    shp = jax.ShapeDtypeStruct((t, D_ATTN), F32)
    return pl.pallas_call(
        body, name="attn_fwd", grid=(b, N_HEAD_PAIRS),
        in_specs=[col(0), col(N_HEAD_PAIRS), col(2 * N_HEAD_PAIRS), tab, tab],
        out_specs=[col(0)] * 4, out_shape=[shp] * 4,
        scratch_shapes=[pltpu.VMEM((SEQ, LANES), F32)] * 5,
        compiler_params=_cparams(("parallel", "parallel")),
    )(qkvz, qkvz, qkvz, cosv, sinv)


def attn_bwd(qkvz, cosv, sinv, dmix, mixed, lses, b):
    t = qkvz.shape[0]
    col, tab = _attn_specs(b)
    blocks = _residue_blocks()
    hd = np.arange(LANES) // HEAD_DIM
    head_ones = jnp.asarray((hd[:, None] == hd[None, :]).astype(np.float32))

    def body(q_ref, k_ref, v_ref, c_ref, s_ref, dm_ref, mx_ref, l1_ref, l2_ref, l3_ref, ones_ref,
             dq_out, dk_out, dv_out, qr, kr, do1, do2, do3, dd1, dd2, dd3, dq_ref, dk_ref, dv_ref):
        l_refs, do_scr, dd_scr = (l1_ref, l2_ref, l3_ref), (do1, do2, do3), (dd1, dd2, dd3)
        c, s = c_ref[...], s_ref[...]
        q, k = q_ref[...], k_ref[...]
        qr[...] = q * c + _rot(q) * s
        kr[...] = k * c + _rot(k) * s
        dm = dm_ref[...]
        tot = _dot(dm * mx_ref[...], ones_ref[...], _NN, HI)
        for w, do_g, dd_g in zip(_branch_weights(l1_ref[...], l2_ref[...], l3_ref[...]), do_scr, dd_scr):
            do_g[...] = w * dm
            dd_g[...] = w * tot
        dq_ref[...] = jnp.zeros((SEQ, LANES), F32)
        dk_ref[...] = jnp.zeros((SEQ, LANES), F32)
        dv_ref[...] = jnp.zeros((SEQ, LANES), F32)
        masks = _pair_masks()
        for lo in range(0, len(blocks), ATTN_GROUP_BWD):
            chains = []
            for g, cur, prev in blocks[lo:lo + ATTN_GROUP_BWD]:
                q2, kcat, vcat, ok = _block_operands(qr, kr, v_ref, cur, prev)
                vcat = vcat.astype(BF16)
                do2_, l2, dd2_ = do_scr[g][cur, :], l_refs[g][cur, :], dd_scr[g][cur, :]
                l2s, dd2s = pltpu.roll(l2, HEAD_DIM, 1), pltpu.roll(dd2_, HEAD_DIM, 1)
                for m in masks:
                    qm = jnp.where(m, q2, 0.0).astype(BF16)
                    dom = jnp.where(m, do2_, 0.0).astype(BF16)
                    lrep, ddrep = jnp.where(m, l2, l2s), jnp.where(m, dd2_, dd2s)
                    if prev is not None:
                        lrep, ddrep = jnp.concatenate([lrep, lrep], axis=1), jnp.concatenate([ddrep, ddrep], axis=1)
                    chains.append(dict(cur=cur, prev=prev, qm=qm, dom=dom, km=jnp.where(m, kcat, 0), lrep=lrep, ddrep=ddrep,
                                       s=jnp.where(ok, _dot(qm, kcat, _NT) * SCALE, NEG), dp=_dot(dom, vcat, _NT)))
            for ch in chains:
                p = jnp.exp(ch['s'] - ch['lrep'])
                ch.update(p=p.astype(BF16), ds=(p * (ch['dp'] - ch['ddrep']) * SCALE).astype(BF16))
            for ch in chains:
                ch.update(dq=_dot(ch['ds'], ch['km'], _NN), dk=_dot(ch['ds'], ch['qm'], _TN), dv=_dot(ch['p'], ch['dom'], _TN))
            for c0, c1 in zip(chains[0::2], chains[1::2]):
                cur, prev = c0['cur'], c0['prev']
                dk, dv = c0['dk'] + c1['dk'], c0['dv'] + c1['dv']
                dq_ref[cur, :] += c0['dq'] + c1['dq']
                if prev is None:
                    dk_ref[cur, :] += dk
                    dv_ref[cur, :] += dv
                else:
                    dk_ref[prev, :] += dk[:ATTN_BLOCK]
                    dv_ref[prev, :] += dv[:ATTN_BLOCK]
                    dk_ref[cur, :] += dk[ATTN_BLOCK:]
                    dv_ref[cur, :] += dv[ATTN_BLOCK:]
        dq, dk = dq_ref[...], dk_ref[...]
        dq_out[...] = (dq * c + _rot(dq * s)).astype(dq_out.dtype)
        dk_out[...] = (dk * c + _rot(dk * s)).astype(dk_out.dtype)
        dv_out[...] = dv_ref[...].astype(dv_out.dtype)

    shp = jax.ShapeDtypeStruct((t, D_ATTN), BF16)
    return pl.pallas_call(
        body, name="attn_bwd", grid=(b, N_HEAD_PAIRS),
        in_specs=[col(0), col(N_HEAD_PAIRS), col(2 * N_HEAD_PAIRS), tab, tab, col(0), col(0), col(0), col(0), col(0),
                  pl.BlockSpec((LANES, LANES), lambda bb, hp: (0, 0))],
        out_specs=[col(0)] * 3, out_shape=[shp] * 3,
        scratch_shapes=[pltpu.VMEM((SEQ, LANES), F32)] * 11,
        compiler_params=_cparams(("parallel", "parallel")),
    )(qkvz, qkvz, qkvz, cosv, sinv, dmix, mixed, *lses, head_ones)


def attn_norm_fwd(mixed, norm_w):
    return rowwise("attn_norm", _rms, [mixed], [norm_w], [(mixed.shape[0], D_ATTN, BF16)])[0]


def attn_norm_bwd(dout, mixed, norm_w):
    def fn(dy, mx, w):
        _, vjp = jax.vjp(_rms, mx, w)
        return vjp(dy)

    return rowwise("attn_norm_bwd", fn, [dout, mixed], [norm_w], [(dout.shape[0], D_ATTN, F32)], accs=[(1, D_ATTN)])


CONV_TM = 256
HALO = 8


def _conv_columns(refs):
    xs_ref, bm_ref, cm_ref = refs
    out = []
    for c in range(D_CONV // LANES):
        lo = c * LANES
        ref, base = (xs_ref, 0) if lo < D_SSD else (bm_ref, D_SSD) if lo < D_SSD + D_BC else (cm_ref, D_SSD + D_BC)
        out.append((slice(lo, lo + LANES), (ref, slice(lo - base, lo - base + LANES))))
    return out


def _conv_taps(scr, w_ref, cs, first_row, step, tm):
    acc = None
    for k in range(CONV_WIDTH):
        term = w_ref[k:k + 1, cs] * scr[pl.ds(first_row + step * k, tm), cs]
        acc = term if acc is None else acc + term
    return acc


def conv_fwd(u, w, bias):
    t = u.shape[0]
    tm, per_seq = CONV_TM, SEQ // CONV_TM

    def body(u_ref, h_ref, w_ref, b_ref, xs_ref, bm_ref, cm_ref, scr):
        first = pl.program_id(0) % per_seq == 0
        scr[0:HALO, :] = jnp.where(first, 0.0, h_ref[...])
        scr[HALO:, :] = u_ref[...]
        for cs, (o_ref, os_) in _conv_columns((xs_ref, bm_ref, cm_ref)):
            o_ref[:, os_] = _silu(_conv_taps(scr, w_ref, cs, HALO - CONV_WIDTH + 1, 1, tm) + b_ref[:, cs])

    return pl.pallas_call(
        body, name="conv_fwd", grid=(t // tm,),
        in_specs=[pl.BlockSpec((tm, D_CONV), lambda i: (i, 0)),
                  pl.BlockSpec((HALO, D_CONV), lambda i: (jnp.maximum(i * (tm // HALO) - 1, 0), 0)),
                  pl.BlockSpec((CONV_WIDTH, D_CONV), lambda i: (0, 0)), pl.BlockSpec((1, D_CONV), lambda i: (0, 0))],
        out_specs=[pl.BlockSpec((tm, D_SSD), lambda i: (i, 0)), pl.BlockSpec((tm, D_BC), lambda i: (i, 0)),
                   pl.BlockSpec((tm, D_BC), lambda i: (i, 0))],
        out_shape=[jax.ShapeDtypeStruct((t, D_SSD), F32), jax.ShapeDtypeStruct((t, D_BC), F32),
                   jax.ShapeDtypeStruct((t, D_BC), F32)],
        scratch_shapes=[pltpu.VMEM((tm + HALO, D_CONV), F32)],
        compiler_params=_cparams(("parallel",)),
    )(u, u, w, bias)


def conv_bwd(u, w, bias, dxs_a, dxs_b, dbm, dcm):
    t = u.shape[0]
    tm, per_seq = CONV_TM, SEQ // CONV_TM
    n_tiles = t // tm

    def body1(u_ref, h_ref, dxs_ref, dxs2_ref, dbm_ref, dcm_ref, w_ref, b_ref, dz_ref, dw_ref, db_ref, scr):
        i = pl.program_id(0)
        first = i % per_seq == 0
        scr[0:HALO, :] = jnp.where(first, 0.0, h_ref[...])
        scr[HALO:, :] = u_ref[...]

        @pl.when(i == 0)
        def _():
            dw_ref[...] = jnp.zeros(dw_ref.shape, F32)
            db_ref[...] = jnp.zeros(db_ref.shape, F32)
        for cs, (g_ref, gs) in _conv_columns((dxs_ref, dbm_ref, dcm_ref)):
            acc = _conv_taps(scr, w_ref, cs, HALO - CONV_WIDTH + 1, 1, tm) + b_ref[:, cs]
            sig = _sigmoid(acc)
            dy = g_ref[:, gs] + dxs2_ref[:, gs] if g_ref is dxs_ref else g_ref[:, gs]
            dz = dy * sig * (1.0 + acc * (1.0 - sig))
            dz_ref[:, cs] = dz
            db_ref[:, cs] += jnp.sum(dz, axis=0, keepdims=True)
            for k in range(CONV_WIDTH):
                dw_ref[k:k + 1, cs] += jnp.sum(dz * scr[pl.ds(HALO - CONV_WIDTH + 1 + k, tm), cs], axis=0, keepdims=True)

    dz, dw, db = pl.pallas_call(
        body1, name="conv_bwd_dz", grid=(n_tiles,),
        in_specs=[pl.BlockSpec((tm, D_CONV), lambda i: (i, 0)),
                  pl.BlockSpec((HALO, D_CONV), lambda i: (jnp.maximum(i * (tm // HALO) - 1, 0), 0)),
                  pl.BlockSpec((tm, D_SSD), lambda i: (i, 0)), pl.BlockSpec((tm, D_SSD), lambda i: (i, 0)),
                  pl.BlockSpec((tm, D_BC), lambda i: (i, 0)), pl.BlockSpec((tm, D_BC), lambda i: (i, 0)),
                  pl.BlockSpec((CONV_WIDTH, D_CONV), lambda i: (0, 0)), pl.BlockSpec((1, D_CONV), lambda i: (0, 0))],
        out_specs=[pl.BlockSpec((tm, D_CONV), lambda i: (i, 0)), pl.BlockSpec((CONV_WIDTH, D_CONV), lambda i: (0, 0)),
                   pl.BlockSpec((1, D_CONV), lambda i: (0, 0))],
        out_shape=[jax.ShapeDtypeStruct((t, D_CONV), F32), jax.ShapeDtypeStruct((CONV_WIDTH, D_CONV), F32),
                   jax.ShapeDtypeStruct((1, D_CONV), F32)],
        scratch_shapes=[pltpu.VMEM((tm + HALO, D_CONV), F32)],
        compiler_params=_cparams(("arbitrary",)),
    )(u, u, dxs_a, dxs_b, dbm, dcm, w, bias)

    def body2(dz_ref, n_ref, w_ref, du_ref, scr):
        last = pl.program_id(0) % per_seq == per_seq - 1
        scr[0:tm, :] = dz_ref[...]
        scr[tm:, :] = jnp.where(last, 0.0, n_ref[...])
        for c in range(D_CONV // LANES):
            cs = slice(c * LANES, (c + 1) * LANES)
            du_ref[:, cs] = _conv_taps(scr, w_ref, cs, CONV_WIDTH - 1, -1, tm).astype(du_ref.dtype)

    du = pl.pallas_call(
        body2, name="conv_bwd_du", grid=(n_tiles,),
        in_specs=[pl.BlockSpec((tm, D_CONV), lambda i: (i, 0)),
                  pl.BlockSpec((HALO, D_CONV), lambda i: (jnp.minimum((i + 1) * (tm // HALO), t // HALO - 1), 0)),
                  pl.BlockSpec((CONV_WIDTH, D_CONV), lambda i: (0, 0))],
        out_specs=pl.BlockSpec((tm, D_CONV), lambda i: (i, 0)),
        out_shape=jax.ShapeDtypeStruct((t, D_CONV), BF16),
        scratch_shapes=[pltpu.VMEM((tm + HALO, D_CONV), F32)],
        compiler_params=_cparams(("parallel",)),
    )(dz, dz, w)
    return du, dw, db


Q = SSD_CHUNK
N_PAIRS = D_SSD // LANES
HEADS_PER_GROUP = N_HEADS // SSD_GROUPS


def _rep(a, j):
    return jnp.broadcast_to(a[:, j:j + 1], a.shape)


def _dot_exact01(a, b, dn, a_is_01):
    x = b if a_is_01 else a
    hi = x.astype(BF16)
    mid = (x - hi.astype(F32)).astype(BF16)
    lo = (x - hi.astype(F32) - mid.astype(F32)).astype(BF16)
    z = a.astype(BF16) if a_is_01 else b.astype(BF16)
    out = None
    for term in (hi, mid, lo):
        d = _dot(z, term, dn) if a_is_01 else _dot(term, z, dn)
        out = d if out is None else out + d
    return out


def _pad_lanes(v, fill=0.0):
    row = jnp.pad(v.reshape(1, -1).astype(F32), ((0, 0), (0, LANES - v.size)), constant_values=fill)
    return row, row.reshape(LANES, 1)


def _ssd_common(dtr_ref, dtrt_ref, bias_r, bias_c, alog_r, alog_c):
    row = lax.broadcasted_iota(jnp.int32, (Q, Q), 0)
    col = lax.broadcasted_iota(jnp.int32, (Q, Q), 1)
    tril = row >= col
    lane = lax.broadcasted_iota(jnp.int32, (1, LANES), 1)
    a_r = jnp.where(lane < N_HEADS, -jnp.exp(alog_r[...]), 0.0)
    sub = lax.broadcasted_iota(jnp.int32, (LANES, 1), 0)
    a_c = jnp.where(sub < N_HEADS, -jnp.exp(alog_c[...]), 0.0)
    dt = _softplus(dtr_ref[...] + bias_r[...])
    cs = _dot_exact01(tril, dt * a_r, _NN, True)
    dtt = _softplus(dtrt_ref[...] + bias_c[...])
    cst = _dot_exact01(dtt * a_c, row <= col, _NN, False)
    return tril, lane, a_r, dt, cs, cst


def _ssd_specs(b, nc, rev):
    ci = (lambda c: nc - 1 - c) if rev else (lambda c: c)
    rows = lambda w: pl.BlockSpec((Q, w), lambda bb, c: (bb * nc + ci(c), 0))
    dtt = pl.BlockSpec((LANES, Q), lambda bb, c: (0, bb * nc + ci(c)))
    const = lambda s: pl.BlockSpec(s, lambda bb, c: (0,) * len(s))
    state = pl.BlockSpec((None, N_PAIRS, LANES, SSD_STATE), lambda bb, c: (bb * nc + ci(c), 0, 0, 0))
    return rows, dtt, const, state


def ssd_fwd(xs, bm, cm, dtraw, dt_bias, a_log, b):
    t = xs.shape[0]
    nc = SEQ // Q
    rows, dtt_spec, const, state = _ssd_specs(b, nc, False)
    bias_r, bias_c = _pad_lanes(dt_bias)
    alog_r, alog_c = _pad_lanes(a_log)

    def body(xs_ref, b_ref, c_ref, dtr_ref, dtrt_ref, br, bc, ar, ac, y_ref, hp_ref, h_scr):
        @pl.when(pl.program_id(1) == 0)
        def _():
            h_scr[...] = jnp.zeros(h_scr.shape, F32)
        tril, lane, _, dt, cs, cst = _ssd_common(dtr_ref, dtrt_ref, br, bc, ar, ac)
        sub = lax.broadcasted_iota(jnp.int32, (LANES, 1), 0)
        y_acc = [jnp.zeros((Q, LANES), F32) for _ in range(N_PAIRS)]
        h_old = [h_scr[p] for p in range(N_PAIRS)]
        h_new = [jnp.zeros((LANES, SSD_STATE), F32) for _ in range(N_PAIRS)]
        for g in range(SSD_GROUPS):
            bg = b_ref[:, g * SSD_STATE:(g + 1) * SSD_STATE].astype(BF16)
            cg = c_ref[:, g * SSD_STATE:(g + 1) * SSD_STATE].astype(BF16)
            cb = _dot(cg, bg, _NT)
            heads = []
            for j in range(g * HEADS_PER_GROUP, (g + 1) * HEADS_PER_GROUP):
                p, side = j // 2, j % 2
                m = (lane < HEAD_DIM) if side == 0 else (lane >= HEAD_DIM)
                ms = (sub < HEAD_DIM) if side == 0 else (sub >= HEAD_DIM)
                csj, dtj = _rep(cs, j), _rep(dt, j)
                lmat = jnp.exp(jnp.where(tril, csj - cst[j:j + 1, :], NEG))
                xdt = jnp.where(m, xs_ref[:, p * LANES:(p + 1) * LANES] * dtj, 0.0)
                hm = jnp.where(ms, h_old[p], 0.0)
                last = csj[Q - 1:Q, :]
                heads.append(dict(p=p, hm=hm, ecs=jnp.exp(csj), el=jnp.exp(last), gmat=(cb * lmat).astype(BF16),
                                  xdt=xdt.astype(BF16), xd=(xdt * jnp.exp(last - csj)).astype(BF16)))
            for h in heads:
                h.update(ydiag=_dot(h['gmat'], h['xdt'], _NN), ch=_dot(cg, h['hm'].astype(BF16), _NT), sj=_dot(h['xd'], bg, _TN))
            for h in heads:
                y_acc[h['p']] = y_acc[h['p']] + h['ydiag'] + h['ecs'] * h['ch']
                h_new[h['p']] = h_new[h['p']] + h['el'] * h['hm'] + h['sj']
        for p in range(N_PAIRS):
            y_ref[:, p * LANES:(p + 1) * LANES] = y_acc[p]
            hp_ref[p] = h_old[p]
            h_scr[p] = h_new[p]

    return pl.pallas_call(
        body, name="ssd_fwd", grid=(b, nc),
        in_specs=[rows(D_SSD), rows(D_BC), rows(D_BC), rows(LANES), dtt_spec, const((1, LANES)), const((LANES, 1)),
                  const((1, LANES)), const((LANES, 1))],
        out_specs=[rows(D_SSD), state],
        out_shape=[jax.ShapeDtypeStruct((t, D_SSD), F32),
                   jax.ShapeDtypeStruct((b * nc, N_PAIRS, LANES, SSD_STATE), F32)],
        scratch_shapes=[pltpu.VMEM((N_PAIRS, LANES, SSD_STATE), F32)],
        compiler_params=_cparams(("parallel", "arbitrary")),
    )(xs, bm, cm, dtraw, dtraw.T, bias_r, bias_c, alog_r, alog_c)


def ssd_bwd(xs, bm, cm, dtraw, dt_bias, a_log, hprev, dy, b):
    t = xs.shape[0]
    nc = SEQ // Q
    rows, dtt_spec, const, state = _ssd_specs(b, nc, True)
    bias_r, bias_c = _pad_lanes(dt_bias)
    alog_r, alog_c = _pad_lanes(a_log)

    def body(xs_ref, b_ref, c_ref, dtr_ref, dtrt_ref, hp_ref, dy_ref, br, bc, ar, ac,
             dxs_ref, db_ref, dc_ref, ddt_ref, dbias_ref, dalog_ref, dh_scr):
        first = jnp.logical_and(pl.program_id(0) == 0, pl.program_id(1) == 0)

        @pl.when(pl.program_id(1) == 0)
        def _():
            dh_scr[...] = jnp.zeros(dh_scr.shape, F32)

        @pl.when(first)
        def _():
            dbias_ref[...] = jnp.zeros(dbias_ref.shape, F32)
            dalog_ref[...] = jnp.zeros(dalog_ref.shape, F32)
        tril, lane, a_r, dt, cs, cst = _ssd_common(dtr_ref, dtrt_ref, br, bc, ar, ac)
        sub = lax.broadcasted_iota(jnp.int32, (LANES, 1), 0)
        rowq = lax.broadcasted_iota(jnp.int32, (Q, 1), 0)
        triu = (lax.broadcasted_iota(jnp.int32, (Q, Q), 0) <= lax.broadcasted_iota(jnp.int32, (Q, Q), 1)).astype(F32)
        dxs_acc = [jnp.zeros((Q, LANES), F32) for _ in range(N_PAIRS)]
        dh_in = [dh_scr[p] for p in range(N_PAIRS)]
        h_in = [hp_ref[p] for p in range(N_PAIRS)]
        dh_out = [jnp.zeros((LANES, SSD_STATE), F32) for _ in range(N_PAIRS)]
        ddt = jnp.zeros((Q, LANES), F32)
        dalog = jnp.zeros((1, LANES), F32)
        for g in range(SSD_GROUPS):
            gs = slice(g * SSD_STATE, (g + 1) * SSD_STATE)
            bg, cg = b_ref[:, gs].astype(BF16), c_ref[:, gs].astype(BF16)
            cb = _dot(cg, bg, _NT)
            dcb = jnp.zeros((Q, Q), F32)
            dbg = jnp.zeros((Q, SSD_STATE), F32)
            dcg = jnp.zeros((Q, SSD_STATE), F32)
            heads = []
            for j in range(g * HEADS_PER_GROUP, (g + 1) * HEADS_PER_GROUP):
                p, side = j // 2, j % 2
                m = (lane < HEAD_DIM) if side == 0 else (lane >= HEAD_DIM)
                ms = (sub < HEAD_DIM) if side == 0 else (sub >= HEAD_DIM)
                csj, dtj = _rep(cs, j), _rep(dt, j)
                lmat = jnp.exp(jnp.where(tril, csj - cst[j:j + 1, :], NEG))
                x2 = jnp.where(m, xs_ref[:, p * LANES:(p + 1) * LANES], 0.0)
                xdt = x2 * dtj
                dym = jnp.where(m, dy_ref[:, p * LANES:(p + 1) * LANES], 0.0)
                hm = jnp.where(ms, h_in[p], 0.0)
                dhm = jnp.where(ms, dh_in[p], 0.0)
                last = csj[Q - 1:Q, :]
                decay = jnp.exp(last - csj)
                heads.append(dict(j=j, p=p, dtj=dtj, lmat=lmat, x2=x2, hm=hm, dhm=dhm, decay=decay, el=jnp.exp(last),
                                  gmat=cb * lmat, dym=dym.astype(BF16), xdt=xdt.astype(BF16), hmb=hm.astype(BF16),
                                  dhmb=dhm.astype(BF16), dye=dym * jnp.exp(csj), xd=xdt * decay))
            for h in heads:
                dyeb, xdb = h['dye'].astype(BF16), h['xd'].astype(BF16)
                h.update(dg=_dot(h['dym'], h['xdt'], _NT),
                         dxdt=_dot(h['gmat'].astype(BF16), h['dym'], _TN),
                         ch=_dot(cg, h['hmb'], _NT),
                         dcg=_dot(dyeb, h['hmb'], _NN), dhp=_dot(dyeb, cg, _TN),
                         wmat=_dot(bg, h['dhmb'], _NT),
                         dbg=_dot(xdb, h['dhmb'], _NN))
            for h in heads:
                ej = h['dg'] * h['gmat']
                col_sums = jnp.broadcast_to(jnp.sum(ej, axis=0, keepdims=True), (Q, Q)).T
                ddl = jnp.sum(h['xd'] * h['wmat'], axis=1, keepdims=True)
                dlast = jnp.sum(ddl, axis=0, keepdims=True) + h['el'] * jnp.sum(
                    jnp.sum(h['dhm'] * h['hm'], axis=1, keepdims=True), axis=0, keepdims=True)
                h['dcs'] = (jnp.sum(ej, axis=1, keepdims=True) - col_sums + jnp.sum(h['dye'] * h['ch'], axis=1, keepdims=True)
                            - ddl + jnp.where(rowq == Q - 1, dlast, 0.0))
                h['dxdt'] = h['dxdt'] + h['decay'] * h['wmat']
                dcb, dcg, dbg = dcb + h['dg'] * h['lmat'], dcg + h['dcg'], dbg + h['dbg']
                dh_out[h['p']] = dh_out[h['p']] + h['el'] * h['dhm'] + h['dhp']
            for h in heads:
                h['da'] = _dot_exact01(triu, h['dcs'], _NN, True)
            for h in heads:
                j, da = h['j'], h['da']
                aj = jnp.sum(jnp.where(lane == j, a_r, 0.0), axis=1, keepdims=True)
                ddtj = da * aj + jnp.sum(h['dxdt'] * h['x2'], axis=1, keepdims=True)
                ddt = ddt + jnp.where(lane == j, ddtj, 0.0)
                dalog = dalog + jnp.where(lane == j, jnp.sum(da * h['dtj'], axis=0, keepdims=True) * aj, 0.0)
                dxs_acc[h['p']] = dxs_acc[h['p']] + h['dxdt'] * h['dtj']
            dcbb = dcb.astype(BF16)
            dc_ref[:, gs] = dcg + _dot(dcbb, bg, _NN)
            db_ref[:, gs] = dbg + _dot(dcbb, cg, _TN)
        for p in range(N_PAIRS):
            dxs_ref[:, p * LANES:(p + 1) * LANES] = dxs_acc[p]
            dh_scr[p] = dh_out[p]
        ddtraw = ddt * _sigmoid(dtr_ref[...] + br[...])
        ddt_ref[...] = ddtraw
        dbias_ref[...] += jnp.sum(ddtraw, axis=0, keepdims=True)
        dalog_ref[...] += dalog

    return pl.pallas_call(
        body, name="ssd_bwd", grid=(b, nc),
        in_specs=[rows(D_SSD), rows(D_BC), rows(D_BC), rows(LANES), dtt_spec, state, rows(D_SSD), const((1, LANES)),
                  const((LANES, 1)), const((1, LANES)), const((LANES, 1))],
        out_specs=[rows(D_SSD), rows(D_BC), rows(D_BC), rows(LANES), const((1, LANES)), const((1, LANES))],
        out_shape=[jax.ShapeDtypeStruct((t, D_SSD), F32), jax.ShapeDtypeStruct((t, D_BC), F32),
                   jax.ShapeDtypeStruct((t, D_BC), F32), jax.ShapeDtypeStruct((t, LANES), F32),
                   jax.ShapeDtypeStruct((1, LANES), F32), jax.ShapeDtypeStruct((1, LANES), F32)],
        scratch_shapes=[pltpu.VMEM((N_PAIRS, LANES, SSD_STATE), F32)],
        compiler_params=_cparams(("arbitrary", "arbitrary")),
    )(xs, bm, cm, dtraw, dtraw.T, hprev, dy, bias_r, bias_c, alog_r, alog_c)


def _split_w_in(w_in):
    w_dt = jnp.pad(w_in[:, D_QKVZ + D_CONV:], ((0, 0), (0, LANES - N_HEADS)))
    return w_in[:, :D_QKVZ], w_in[:, D_QKVZ:D_QKVZ + D_CONV], w_dt


def mixer_fwd(hb, p, cosv, sinv, b):
    t = hb.shape[0]
    w_a, w_b, w_c = _split_w_in(p['w_in'])
    qkvz = mm("in_qkvz", [(hb, w_a, 'nn')], D_QKVZ)
    xbc = mm("in_xbc", [(hb, w_b, 'nn')], D_CONV)
    dtraw = mm("in_dt", [(hb, w_c, 'nn')], LANES)
    mixed, *lses = attn_fwd(qkvz, cosv, sinv, b)
    attn = attn_norm_fwd(mixed, p['attn_norm_w'])
    xs, bm, cm = conv_fwd(xbc, p['conv_w'], p['conv_b'])
    y, hprev = ssd_fwd(xs, bm, cm, dtraw, p['dt_bias'], p['a_log'], b)
    dskip = jnp.repeat(p['d_skip'].reshape(-1), HEAD_DIM).reshape(1, D_SSD)
    yg, = rowwise("ssd_gate", _gate, [y, xs, Op(qkvz, D_SSD, 3)], [dskip, p['ssd_norm_w']], [(t, D_SSD, BF16)])
    mix = mm("out_proj", [(attn, p['w_out'][:D_ATTN], 'nn'), (yg, p['w_out'][D_ATTN:], 'nn')], D_MODEL, out_dtype=BF16)
    res = dict(hb=hb, qkvz=qkvz, xbc=xbc, dtraw=dtraw, mixed=mixed, lses=lses, attn=attn, xs=xs, bm=bm, cm=cm,
               y=y, hprev=hprev, dskip=dskip, yg=yg, cosv=cosv, sinv=sinv)
    return mix, res


def mixer_bwd(r, p, dmix, dh_resid, b):
    t = dmix.shape[0]
    w_a, w_b, w_c = _split_w_in(p['w_in'])
    w_out = p['w_out']
    dattn = mm("out_bwd_dattn", [(dmix, w_out[:D_ATTN], 'nt')], D_ATTN)
    dyg = mm("out_bwd_dyg", [(dmix, w_out[D_ATTN:], 'nt')], D_SSD)
    dw_out = jnp.concatenate([mm_tn("out_bwd_dw_a", r['attn'], dmix, BF16),
                              mm_tn("out_bwd_dw_y", r['yg'], dmix, BF16)], axis=0)

    def gate_bwd(dy_, y_, xs_, z_, ds_, w_):
        _, vjp = jax.vjp(_gate, y_, xs_, z_, ds_, w_)
        return vjp(dy_)

    dy, dxs_a, dz, ddskip, dssd_norm = rowwise(
        "ssd_gate_bwd", gate_bwd, [dyg, r['y'], r['xs'], Op(r['qkvz'], D_SSD, 3)], [r['dskip'], p['ssd_norm_w']],
        [(t, D_SSD, F32), (t, D_SSD, F32), (t, D_SSD, BF16)], accs=[(1, D_SSD), (1, D_SSD)])
    dxs_b, dbm, dcm, ddtraw, ddt_bias, da_log = ssd_bwd(r['xs'], r['bm'], r['cm'], r['dtraw'], p['dt_bias'], p['a_log'],
                                                        r['hprev'], dy, b)
    dxbc, dconv_w, dconv_b = conv_bwd(r['xbc'], p['conv_w'], p['conv_b'], dxs_a, dxs_b, dbm, dcm)
    dmixed, dattn_norm = attn_norm_bwd(dattn, r['mixed'], p['attn_norm_w'])
    dq, dk, dv = attn_bwd(r['qkvz'], r['cosv'], r['sinv'], dmixed, r['mixed'], r['lses'], b)
    wq, wk, wv, wz = (w_a[:, i * D_ATTN:(i + 1) * D_ATTN] for i in range(4))
    dh = mm("in_bwd_dh", [(dq, wq, 'nt'), (dk, wk, 'nt'), (dv, wv, 'nt'), (dz, wz, 'nt'), (dxbc, w_b, 'nt'),
                          (ddtraw, w_c, 'nt')], D_MODEL, add=dh_resid, tn=512)
    h = r['hb']
    dw_in = jnp.concatenate([mm_tn_cat("in_bwd_dw_qkvz", h, [dq, dk, dv, dz], BF16),
                             mm_tn_cat("in_bwd_dw_xbc_dt", h, [dxbc, ddtraw], BF16)[:, :D_CONV + N_HEADS]], axis=1)
    head_sum = lambda v: v.reshape(N_HEADS, HEAD_DIM).sum(axis=1).reshape(1, N_HEADS)
    grads = dict(w_in=dw_in, w_out=dw_out, conv_w=dconv_w, conv_b=dconv_b, dt_bias=ddt_bias[:, :N_HEADS],
                 a_log=da_log[:, :N_HEADS], d_skip=head_sum(ddskip), attn_norm_w=dattn_norm, ssd_norm_w=dssd_norm)
    return dh, grads


FFN2_KEYS = ('ffn2_gate', 'ffn2_up', 'ffn2_down')
MIXER_KEYS = ('w_in', 'conv_w', 'w_out')
FFN_COL = ('ffn1_gate', 'ffn1_up', 'ffn2_gate', 'ffn2_up')
FFN_ROW = ('ffn1_down', 'ffn2_down')
CONV_W_COMM = (8, 2 * LANES)
SMALL = 'small'


def comm_shape(k, shapes):
    if k in FFN_COL:
        return (D_MODEL, FF_PAD)
    if k in FFN_ROW:
        return (FF_PAD, D_MODEL)
    if k == 'conv_w':
        return CONV_W_COMM
    return tuple(shapes[k][1:])


def to_comm(k, vals, shapes):
    a = vals[k].reshape(shapes[k][1:])
    r_, c_ = comm_shape(k, shapes)
    return jnp.pad(a, ((0, r_ - a.shape[0]), (0, c_ - a.shape[1])))


SMALL_ROWS, SMALL_COLS = 16, D_CONV


def pack_small(small):
    rows = [jnp.pad(small[r].reshape(1, -1), ((0, 0), (0, SMALL_COLS - small[r].size))) for r in REPLICATED]
    return jnp.concatenate(rows + [jnp.zeros((SMALL_ROWS - len(rows), SMALL_COLS), F32)], axis=0)


def full_weight(k, g):
    if k in FFN_COL:
        return g
    if k == 'conv_w':
        return jnp.transpose(g[:, :CONV_WIDTH, :D_CONV // N_DEV], (1, 0, 2)).reshape(CONV_WIDTH, D_CONV)
    return g.reshape(N_DEV * g.shape[1], g.shape[2])


def grad_shards(k, g):
    if k in FFN_COL:
        return g
    if k == 'conv_w':
        s = jnp.transpose(g.reshape(CONV_WIDTH, N_DEV, D_CONV // N_DEV), (1, 0, 2))
        return jnp.pad(s, ((0, 0), (0, CONV_W_COMM[0] - CONV_WIDTH), (0, CONV_W_COMM[1] - D_CONV // N_DEV)))
    return g.reshape(N_DEV, g.shape[0] // N_DEV, g.shape[1])


def _flip(v, bit):
    return 1 - v if bit else v


N_PEER_COPIES = N_DEV - 1


def _comm_call(name, body, arrs, out_shape):
    n = len(arrs)
    return pl.pallas_call(
        functools.partial(body, n), name=name, out_shape=out_shape,
        in_specs=[pl.BlockSpec(memory_space=pl.ANY)] * n, out_specs=[pl.BlockSpec(memory_space=pl.ANY)] * n,
        scratch_shapes=[pltpu.SemaphoreType.DMA((n * N_PEER_COPIES,)), pltpu.SemaphoreType.DMA((n * N_PEER_COPIES,)),
                        pltpu.SemaphoreType.DMA((n,))],
    )(*arrs)


def _blk(ref, idx, by_cols):
    if not by_cols:
        return ref.at[idx]
    c = ref.shape[1] // N_DEV
    return ref.at[:, pl.ds(pl.multiple_of(idx * c, LANES), c)]


def _blocked_shape(a, by_cols):
    return (a.shape[0], N_DEV * a.shape[1]) if by_cols else (N_DEV,) + a.shape


def all_gather(arrs, by_cols):
    def body(n, *refs):
        x_refs, out_refs, (send_sems, recv_sems, local_sems) = refs[:n], refs[n:2 * n], refs[2 * n:]
        x, y, c = lax.axis_index("x"), lax.axis_index("y"), lax.axis_index("c")
        me, sibling = (x, y, c), (x, y, 1 - c)
        chips = [(1 - x, y), (x, 1 - y), (1 - x, 1 - y)]

        def copy(a, k, block, to, src=None):
            px, py, pc = block
            dst = _blk(out_refs[a], 4 * px + 2 * py + pc, by_cols[a])
            return pltpu.make_async_remote_copy(
                src_ref=dst if src is None else src, dst_ref=dst, send_sem=send_sems.at[a * N_PEER_COPIES + k],
                recv_sem=recv_sems.at[a * N_PEER_COPIES + k], device_id=to, device_id_type=MESH)

        mine = [pltpu.make_async_copy(x_refs[a], _blk(out_refs[a], 4 * x + 2 * y + c, by_cols[a]), local_sems.at[a])
                for a in range(n)]
        started = []
        for a in range(n):
            mine[a].start()
            first = [copy(a, 0, me, sibling, src=x_refs[a])]
            first += [copy(a, 1 + j, me, (*chip, c), src=x_refs[a]) for j, chip in enumerate(chips)]
            for cp in first:
                cp.start()
            started += first
        for j, chip in enumerate(chips):
            for a in range(n):
                copy(a, 1 + j, (*chip, c), me).wait_recv()
                cp = copy(a, 4 + j, (*chip, c), sibling)
                cp.start()
                started.append(cp)
        for a in range(n):
            copy(a, 0, sibling, me).wait_recv()
            for j, chip in enumerate(chips):
                copy(a, 4 + j, (*chip, 1 - c), me).wait_recv()
        for cp in started:
            cp.wait_send()
        for cp in mine:
            cp.wait()

    return _comm_call("all_gather_weights", body, arrs,
                      [jax.ShapeDtypeStruct(_blocked_shape(a, bc), a.dtype) for a, bc in zip(arrs, by_cols)])


def blocks_to_cols(arrs):
    def body(*refs):
        for i, o in zip(refs[:len(arrs)], refs[len(arrs):]):
            o[...] = i[...]

    return pl.pallas_call(
        body, name="blocks_to_cols", grid=(N_DEV,),
        in_specs=[pl.BlockSpec((None,) + a.shape[1:], lambda p: (p, 0, 0)) for a in arrs],
        out_specs=[pl.BlockSpec(a.shape[1:], lambda p: (0, p)) for a in arrs],
        out_shape=[jax.ShapeDtypeStruct((a.shape[1], N_DEV * a.shape[2]), a.dtype) for a in arrs],
        compiler_params=_cparams(("parallel",)),
    )(*arrs)


def _landing_shape(a, by_cols):
    return (N_DEV, a.shape[0], a.shape[1] // N_DEV) if by_cols else a.shape


_HBM = pl.BlockSpec(memory_space=pltpu.HBM)
_SEM = pl.BlockSpec(memory_space=pltpu.SEMAPHORE)
_EFFECT = pltpu.SideEffectType.DATAFLOW_SIDE_EFFECTING


def _peer(k):
    x, y, c = lax.axis_index("x"), lax.axis_index("y"), lax.axis_index("c")
    return _flip(x, k & 4), _flip(y, k & 2), _flip(c, k & 1)


def _my_index():
    return 4 * lax.axis_index("x") + 2 * lax.axis_index("y") + lax.axis_index("c")


def _split_copies(mode, by_cols, src_refs, land_refs, send_sems, recv_sems):
    me = _my_index()
    out = []
    for a, bc in enumerate(by_cols):
        for k in range(1, N_DEV):
            px, py, pc = _peer(k)
            src = _blk(src_refs[a], 4 * px + 2 * py + pc, bc) if mode == 'scatter' else src_refs[a]
            dst = land_refs[a].at[me] if mode == 'scatter' else _blk(land_refs[a], me, bc)
            out.append(pltpu.make_async_remote_copy(
                src_ref=src, dst_ref=dst, send_sem=send_sems.at[a * N_PEER_COPIES + k - 1],
                recv_sem=recv_sems.at[a * N_PEER_COPIES + k - 1], device_id=(px, py, pc), device_id_type=MESH))
    return out


def exchange_start(name, mode, srcs, by_cols):
    n = len(srcs)
    lands = [lax.empty(_landing_shape(s, bc) if mode == 'scatter' else _blocked_shape(s, bc), s.dtype)
             for s, bc in zip(srcs, by_cols)]

    def body(*refs):
        src_refs, land_refs, send_sems, recv_sems = refs[:n], refs[n:2 * n], refs[2 * n], refs[2 * n + 1]
        for cp in _split_copies(mode, by_cols, src_refs, land_refs, send_sems, recv_sems):
            cp.start()
        refs[-1][...] = jnp.zeros(refs[-1].shape, F32)

    sems = pltpu.SemaphoreType.DMA((n * N_PEER_COPIES,))
    res = pl.pallas_call(
        body, name=name,
        out_shape=(sems, sems, *[pltpu.HBM(a.shape, a.dtype) for a in srcs + lands], jax.ShapeDtypeStruct((8, LANES), F32)),
        in_specs=(_HBM,) * (2 * n), out_specs=(_SEM, _SEM, *(_HBM,) * (2 * n), pl.BlockSpec(memory_space=pltpu.VMEM)),
        input_output_aliases={i: 2 + i for i in range(2 * n)},
        compiler_params=pltpu.CompilerParams(has_side_effects=_EFFECT),
    )(*[pltpu.with_memory_space_constraint(a, pltpu.HBM) for a in srcs + lands])
    return (mode, by_cols, res[:-1]), res[-1]


def exchange_wait(name, handles, after):
    mode, by_cols, (send_sems, recv_sems, *bufs) = handles
    n = len(by_cols)

    def body(*refs):
        src_refs, land_refs, s_sems, r_sems = refs[:n], refs[n:2 * n], refs[2 * n], refs[2 * n + 1]
        for cp in _split_copies(mode, by_cols, src_refs, land_refs, s_sems, r_sems):
            cp.wait_send()
            cp.wait_recv()

    res = pl.pallas_call(
        body, name=name, out_shape=tuple(pltpu.HBM(a.shape, a.dtype) for a in bufs),
        in_specs=(*(_HBM,) * (2 * n), _SEM, _SEM, pl.BlockSpec(memory_space=pl.ANY)), out_specs=(_HBM,) * (2 * n),
        input_output_aliases={i: i for i in range(2 * n)},
        compiler_params=pltpu.CompilerParams(has_side_effects=_EFFECT),
    )(*bufs, send_sems, recv_sems, after)
    me, out = _my_index(), []
    for src, land, bc in zip(res[:n], res[n:], by_cols):
        if mode == 'scatter':
            c = land.shape[2]
            own = lax.dynamic_slice(src, (0, me * c), (src.shape[0], c)) if bc else lax.dynamic_index_in_dim(src, me, 0, False)
            out.append(lax.dynamic_update_slice(land, own[None], (me, 0, 0)))
        elif bc:
            out.append(lax.dynamic_update_slice(land, src, (0, me * src.shape[1])))
        else:
            out.append(lax.dynamic_update_slice(land, src[None], (me, 0, 0)))
    return out


def _adamw_math(g, w, m, v):
    c1 = 1.0 / (1.0 - ADAM_B1 ** ADAM_STEP)
    c2 = 1.0 / (1.0 - ADAM_B2 ** ADAM_STEP)
    m = ADAM_B1 * m + (1.0 - ADAM_B1) * g
    v = ADAM_B2 * v + (1.0 - ADAM_B2) * jnp.square(g)
    return g, -ADAM_LR * ((m * c1) / (jnp.sqrt(v * c2) + ADAM_EPS) + ADAM_WD * w), m, v


def adamw(name, recv, w, m, v, tm):
    _, rows, cols = w.shape
    tm = min(tm, rows)

    def body(*refs):
        g = refs[0][0:tm, 0:cols].astype(F32)
        for s in range(1, N_DEV):
            g = g + refs[s][0:tm, 0:cols].astype(F32)
        res = _adamw_math(g, *[r[...] for r in refs[N_DEV:N_DEV + 3]])
        for r, val in zip(refs[N_DEV + 3:], res):
            r[...] = val

    part = lambda s: pl.BlockSpec((None, recv.shape[1] if tm == rows else tm, recv.shape[2]), lambda i: (s, i, 0))
    tile = pl.BlockSpec((None, tm, cols), lambda i: (0, i, 0))
    return pl.pallas_call(
        body, name=name, grid=(rows // tm,), in_specs=[part(s) for s in range(N_DEV)] + [tile] * 3, out_specs=[tile] * 4,
        out_shape=[jax.ShapeDtypeStruct((1, rows, cols), F32)] * 4, compiler_params=_cparams(("parallel",)),
    )(*[recv] * N_DEV, w, m, v)


def adamw_small(recv, wl, ml, vl):
    n = len(REPLICATED)

    def body(recv_ref, *refs):
        g = recv_ref[0]
        for s in range(1, N_DEV):
            g = g + recv_ref[s]
        for r in range(n):
            w, m, v = (refs[j * n + r][...] for j in range(3))
            for j, val in enumerate(_adamw_math(g[r:r + 1, :w.shape[1]], w, m, v)):
                refs[(3 + j) * n + r][...] = val

    arrs = [d[k].reshape(1, -1) for d in (wl, ml, vl) for k in REPLICATED]
    res = pl.pallas_call(
        body, name="adamw_small", out_shape=[jax.ShapeDtypeStruct(a.shape, F32) for a in arrs[:n]] * 4,
    )(recv, *arrs)
    return [{k: res[j * n + r].reshape(wl[k].shape) for r, k in enumerate(REPLICATED)} for j in range(4)]


ADAMW_TM = {'ffn1_gate': 256, 'ffn1_up': 256, 'ffn2_gate': 256, 'ffn2_up': 256, 'w_in': 32}


def kernel(x, positions, ln1_g, ln1_b, ffn1_gate, ffn1_up, ffn1_down, w_in, conv_w, conv_b, dt_bias, a_log, d_skip, attn_norm_w, ssd_norm_w, w_out, ln2_g, ln2_b, ffn2_gate, ffn2_up, ffn2_down, ln3_g, ln3_b, loss_target, m_ln1_g, m_ln1_b, m_ffn1_gate, m_ffn1_up, m_ffn1_down, m_w_in, m_conv_w, m_conv_b, m_dt_bias, m_a_log, m_d_skip, m_attn_norm_w, m_ssd_norm_w, m_w_out, m_ln2_g, m_ln2_b, m_ffn2_gate, m_ffn2_up, m_ffn2_down, m_ln3_g, m_ln3_b, v_ln1_g, v_ln1_b, v_ffn1_gate, v_ffn1_up, v_ffn1_down, v_w_in, v_conv_w, v_conv_b, v_dt_bias, v_a_log, v_d_skip, v_attn_norm_w, v_ssd_norm_w, v_w_out, v_ln2_g, v_ln2_b, v_ffn2_gate, v_ffn2_up, v_ffn2_down, v_ln3_g, v_ln3_b):
    args = dict(locals())
    wl = {k: args[k] for k in WEIGHTS}
    ml = {k: args["m_" + k] for k in WEIGHTS}
    vl = {k: args["v_" + k] for k in WEIGHTS}
    shapes = {k: wl[k].shape for k in WEIGHTS}
    b, s, dm = x.shape
    t = b * s

    sent = {k: to_comm(k, wl, shapes).astype(F32 if k == 'conv_w' else BF16) for k in SHARDED}
    by_cols = lambda keys: [k in FFN_COL for k in keys]
    gate, up = all_gather([sent['ffn1_gate'], sent['ffn1_up']], [False] * 2)
    (gate, up), sent = lax.optimization_barrier(((gate, up), sent))
    p = dict(zip(('ffn1_gate', 'ffn1_up'), blocks_to_cols([gate, up])))
    gather_down, token_d = exchange_start("gather_ffn1_down_start", 'gather', [sent['ffn1_down']], [False])
    sent['w_in'] = sent['w_in'] + token_d[0, 0].astype(BF16)
    gather_mixer, token_m = exchange_start("gather_mixer_start", 'gather', [sent[k] for k in MIXER_KEYS], by_cols(MIXER_KEYS))
    sent['ffn2_gate'] = sent['ffn2_gate'] + token_m[0, 0].astype(BF16)
    gather_ffn2, token_f = exchange_start("gather_ffn2_start", 'gather', [sent[k] for k in FFN2_KEYS], by_cols(FFN2_KEYS))
    for k in REPLICATED:
        p[k] = wl[k].reshape(1, -1)

    x2 = x.reshape(t, dm)
    cosv, sinv = rope_tables(positions)
    g1, u1, a1, at1 = ffn_gate_up("ffn1_gate_up", x2, p['ffn1_gate'], p['ffn1_up'], after=(token_d, token_m, token_f))
    p['ffn1_down'] = full_weight('ffn1_down', exchange_wait("gather_ffn1_down_wait", gather_down, a1)[0])
    f1, res1 = mm("ffn1_down", [(a1, p['ffn1_down'], 'nn')], D_MODEL, out_dtype=BF16), (x2, g1, u1, at1)
    h1, h1b = resid_ln_fwd("ln1", 0.5, x2, f1, p['ln1_g'], p['ln1_b'])
    for k, g in zip(MIXER_KEYS, exchange_wait("gather_mixer_wait", gather_mixer, h1b)):
        p[k] = full_weight(k, g)
    mix, resm = mixer_fwd(h1b, p, cosv, sinv, b)
    h2, h2b = resid_ln_fwd("ln2", 1.0, h1, mix, p['ln2_g'], p['ln2_b'])
    for k, g in zip(FFN2_KEYS, exchange_wait("gather_ffn2_wait", gather_ffn2, h2b)):
        p[k] = full_weight(k, g)
    f2, res3 = ffn_fwd("ffn2", h2b, p['ffn2_gate'], p['ffn2_up'], p['ffn2_down'])

    small, full = {}, {}
    dh2_res, df2, small['ln3_g'], small['ln3_b'], sq = ln_loss_bwd("ln3_loss_bwd", h2, f2, loss_target.reshape(t, dm),
                                                                   p['ln3_g'], p['ln3_b'])
    loss = lax.psum(jnp.sum(sq) * (0.5 / dm), AXES)

    dh2, full['ffn2_gate'], full['ffn2_up'], full['ffn2_down'] = ffn_bwd("ffn2", res3, p['ffn2_gate'], p['ffn2_up'],
                                                                       p['ffn2_down'], df2, dh2_res)
    ffn2_exchange, token = exchange_start("grads_ffn2_start", 'scatter', [grad_shards(k, full[k]) for k in FFN2_KEYS],
                                          by_cols(FFN2_KEYS))
    dh1_res, dmix, small['ln2_g'], small['ln2_b'] = resid_ln_bwd("ln2_bwd", 1.0, h1, mix, p['ln2_g'] + token[:1, :1],
                                                                 p['ln2_b'], dh2)
    dh1, gm = mixer_bwd(resm, p, dmix, dh1_res, b)
    for k in ('conv_b', 'dt_bias', 'a_log', 'd_skip', 'attn_norm_w', 'ssd_norm_w'):
        small[k] = gm[k]
    mixer_exchange, token = exchange_start("grads_mixer_start", 'scatter', [grad_shards(k, gm[k]) for k in MIXER_KEYS],
                                           by_cols(MIXER_KEYS))
    dx_res, df1, small['ln1_g'], small['ln1_b'] = resid_ln_bwd("ln1_bwd", 0.5, x2, f1, p['ln1_g'] + token[:1, :1],
                                                               p['ln1_b'], dh1)
    hb, g, u, at = res1
    small_part = pack_small(small)
    dg, du = ffn_da_act("ffn1_bwd_da_act", df1, p['ffn1_down'], g, u)
    dwd = mm_acc("ffn1_bwd_dwd", at, df1, BF16, after=dg)
    down_exchange, token = exchange_start("grads_ffn1_down_start", 'scatter', [
        grad_shards('ffn1_down', dwd), jnp.broadcast_to(small_part[None], (N_DEV,) + small_part.shape)], [False, False])
    dwg = mm_tn("ffn1_bwd_dwg", hb, dg, BF16, after=token)
    gate_exchange, token = exchange_start("grads_ffn1_gate_start", 'scatter', [grad_shards('ffn1_gate', dwg)], [True])
    dwu = mm_tn("ffn1_bwd_dwu", hb, du, BF16, after=token)
    up_exchange, token = exchange_start("grads_ffn1_up_start", 'scatter', [grad_shards('ffn1_up', dwu)], [True])
    dx = mm("ffn1_bwd_dh", [(dg, p['ffn1_gate'], 'nt'), (du, p['ffn1_up'], 'nt')], D_MODEL, add=dx_res, tn=512, after=token)
    recv = {}
    for keys, name, ex in (((FFN2_KEYS), "grads_ffn2_wait", ffn2_exchange), (MIXER_KEYS, "grads_mixer_wait", mixer_exchange),
                           (('ffn1_down', SMALL), "grads_ffn1_down_wait", down_exchange),
                           (('ffn1_gate',), "grads_ffn1_gate_wait", gate_exchange),
                           (('ffn1_up',), "grads_ffn1_up_wait", up_exchange)):
        recv.update(zip(keys, exchange_wait(name, ex, dx)))
    outs = adamw_small(recv.pop(SMALL), wl, ml, vl)
    for k, r in recv.items():
        for o, a in zip(outs, adamw(f"adamw_{k}", r, wl[k], ml[k], vl[k], ADAMW_TM.get(k, shapes[k][1]))):
            o[k] = a
    return (loss, dx.reshape(b, s, dm), *[o[k] for o in outs for k in WEIGHTS])
```

```python
import functools

import jax
import jax.numpy as jnp
import numpy as np
from jax import lax
from jax.experimental import pallas as pl
from jax.experimental.pallas import tpu as pltpu

F32, BF16 = jnp.float32, jnp.bfloat16
HI = lax.Precision.HIGHEST
MESH = pl.DeviceIdType.MESH
AXES = ("x", "y", "c")
N_DEV = 8

D_MODEL = 1024
SEQ = 2048
HEAD_DIM = 64
N_HEADS = 12
D_ATTN = N_HEADS * HEAD_DIM
DILATIONS = (1, 4, 16)
ATTN_BLOCK = 128
ROPE_THETA = 500000.0
ROPE_DIM = 16
D_SSD = 768
SSD_GROUPS = 4
SSD_STATE = 128
SSD_CHUNK = 128
D_BC = SSD_GROUPS * SSD_STATE
D_CONV = D_SSD + 2 * D_BC
CONV_WIDTH = 4
D_QKVZ = 3 * D_ATTN + D_SSD
D_FF = 2816
ALPHA = 2.0 ** 0.25
LN_EPS = 1e-5
RMS_EPS = 1e-6
ADAM_LR, ADAM_B1, ADAM_B2, ADAM_EPS, ADAM_WD, ADAM_STEP = 0.001, 0.9, 0.999, 1e-08, 0.01, 10

LANES = 128
VMEM_LIMIT = 52 * 1024 * 1024
NEG = -1e30

WEIGHTS = ['ln1_g', 'ln1_b', 'ffn1_gate', 'ffn1_up', 'ffn1_down', 'w_in', 'conv_w', 'conv_b', 'dt_bias', 'a_log',
           'd_skip', 'attn_norm_w', 'ssd_norm_w', 'w_out', 'ln2_g', 'ln2_b', 'ffn2_gate', 'ffn2_up', 'ffn2_down',
           'ln3_g', 'ln3_b']
COL_SHARDED = ('ffn1_gate', 'ffn1_up', 'conv_w', 'ffn2_gate', 'ffn2_up')
ROW_SHARDED = ('ffn1_down', 'w_in', 'w_out', 'ffn2_down')
SHARDED = tuple(n for n in WEIGHTS if n in COL_SHARDED or n in ROW_SHARDED)
REPLICATED = tuple(n for n in WEIGHTS if n not in SHARDED)
FF_SHARD = D_FF // N_DEV
FF_PAD = -(-FF_SHARD // LANES) * LANES


def _cparams(sem=None):
    return pltpu.CompilerParams(dimension_semantics=sem, vmem_limit_bytes=VMEM_LIMIT)


def _tile(n, prefs):
    for p in prefs:
        if n % p == 0:
            return p
    return n


class Op:
    def __init__(self, arr, bw=None, cb=0, ro=0):
        self.arr, self.bw, self.cb, self.ro = arr, (arr.shape[1] if bw is None else bw), cb, ro


def _op(a):
    return a if isinstance(a, Op) else Op(a)


def rowwise(name, fn, ins, consts, outs, accs=(), tm=256):
    ins = [_op(a) for a in ins]
    rows = outs[0][0]
    n_in, n_c, n_o, n_a = len(ins), len(consts), len(outs), len(accs)
    tm = min(tm, rows)
    assert rows % tm == 0, (name, rows, tm)

    def body(*refs):
        vals = [r[...].astype(F32) for r in refs[:n_in + n_c]]
        res = fn(*vals)
        res = res if isinstance(res, (tuple, list)) else (res,)
        o_refs = refs[n_in + n_c:n_in + n_c + n_o]
        a_refs = refs[n_in + n_c + n_o:]
        for r, v in zip(o_refs, res[:n_o]):
            r[...] = v.astype(r.dtype)
        if n_a:
            @pl.when(pl.program_id(0) == 0)
            def _():
                for r in a_refs:
                    r[...] = jnp.zeros(r.shape, r.dtype)
            for r, v in zip(a_refs, res[n_o:]):
                r[...] += v

    in_specs = [pl.BlockSpec((tm, o.bw), functools.partial(lambda i, o: (i + o.ro, o.cb), o=o)) for o in ins]
    in_specs += [pl.BlockSpec(c.shape, functools.partial(lambda i, nd: (0,) * nd, nd=c.ndim)) for c in consts]
    out_specs = [pl.BlockSpec((tm, w), lambda i: (i, 0)) for (_, w, _) in outs]
    out_specs += [pl.BlockSpec(s, functools.partial(lambda i, nd: (0,) * nd, nd=len(s))) for s in accs]
    out_shape = [jax.ShapeDtypeStruct((r, w), dt) for (r, w, dt) in outs]
    out_shape += [jax.ShapeDtypeStruct(s, F32) for s in accs]
    res = pl.pallas_call(
        body, name=name, grid=(rows // tm,), in_specs=in_specs, out_specs=out_specs, out_shape=out_shape,
        compiler_params=_cparams(("arbitrary",) if n_a else ("parallel",)),
    )(*[o.arr for o in ins], *consts)
    return res


MM_TM = 1024
MM_TN = (1024, 896, 768, 512, 256, 128)
_NT = (((1,), (1,)), ((), ()))
_NN = (((1,), (0,)), ((), ()))
_TN = (((0,), (0,)), ((), ()))


def _dot(a, b, dn, precision=None):
    return lax.dot_general(a, b, dn, preferred_element_type=F32, precision=precision)


def _mm_specs(name, pairs, n_out, tm, tn):
    in_specs, args = [], []
    for a, b, mode in pairs:
        o = _op(a)
        in_specs.append(pl.BlockSpec((tm, o.bw), functools.partial(lambda j, i, o: (i, o.cb), o=o)))
        args.append(o.arr)
        if mode == 'nn':
            assert b.shape == (o.bw, n_out), (name, b.shape, o.bw, n_out)
            in_specs.append(pl.BlockSpec((o.bw, tn), lambda j, i: (0, j)))
        else:
            assert b.shape == (n_out, o.bw), (name, b.shape, o.bw, n_out)
            in_specs.append(pl.BlockSpec((tn, o.bw), lambda j, i: (j, 0)))
        args.append(b)
    return in_specs, args


def _mm_acc(refs, pairs):
    acc = None
    for k, (_, _, mode) in enumerate(pairs):
        d = _dot(refs[2 * k][...].astype(BF16), refs[2 * k + 1][...].astype(BF16), _NN if mode == 'nn' else _NT)
        acc = d if acc is None else acc + d
    return acc


def mm(name, pairs, n_out, add=None, out_dtype=F32, tm=MM_TM, tn=None, after=None):
    m = _op(pairs[0][0]).arr.shape[0]
    tn = tn or _tile(n_out, MM_TN)
    n_p = len(pairs)

    def body(*refs):
        acc = _mm_acc(refs, pairs)
        if add is not None:
            acc = acc + refs[2 * n_p][...]
        refs[-1][...] = acc.astype(refs[-1].dtype)

    in_specs, args = _mm_specs(name, pairs, n_out, tm, tn)
    tile = pl.BlockSpec((tm, tn), lambda j, i: (i, j))
    if add is not None:
        in_specs.append(tile)
        args.append(add)
    if after is not None:
        in_specs.append(pl.BlockSpec(memory_space=pl.ANY))
        args.append(after)
    return pl.pallas_call(
        body, name=name, grid=(n_out // tn, m // tm), in_specs=in_specs, out_specs=tile,
        out_shape=jax.ShapeDtypeStruct((m, n_out), out_dtype),
        compiler_params=_cparams(("parallel", "parallel")),
    )(*args)


def mm_tn(name, a, b, out_dtype=F32, tt=1024, after=None):
    a, b = _op(a), _op(b)
    t = a.arr.shape[0]
    k, n = a.bw, b.bw
    tk = _tile(k, (512, 896, 768, 256, 128))
    tn = _tile(n, (3072, 1792) + MM_TN)
    tt = min(tt, t)
    n_t = t // tt
    order = [] if after is None else [after]

    def body(a_ref, b_ref, *rest):
        o_ref, acc_ref = rest[-2:]
        s = pl.program_id(2)
        d = _dot(a_ref[...].astype(BF16), b_ref[...].astype(BF16), _TN)

        @pl.when(s == 0)
        def _():
            acc_ref[...] = d

        @pl.when(s > 0)
        def _():
            acc_ref[...] += d

        @pl.when(s == n_t - 1)
        def _():
            o_ref[...] = acc_ref[...].astype(o_ref.dtype)

    return pl.pallas_call(
        body, name=name, grid=(k // tk, n // tn, n_t),
        in_specs=[pl.BlockSpec((tt, tk), functools.partial(lambda kk, nn, s, o: (s, o.cb * (o.bw // tk) + kk), o=a)),
                  pl.BlockSpec((tt, tn), functools.partial(lambda kk, nn, s, o: (s, o.cb * (o.bw // tn) + nn), o=b))]
        + [pl.BlockSpec(memory_space=pl.ANY) for _ in order],
        out_specs=pl.BlockSpec((tk, tn), lambda kk, nn, s: (kk, nn)),
        out_shape=jax.ShapeDtypeStruct((k, n), out_dtype),
        scratch_shapes=[pltpu.VMEM((tk, tn), F32)],
        compiler_params=_cparams(("parallel", "parallel", "arbitrary")),
    )(a.arr, b.arr, *order)


def _sigmoid(x):
    return 1.0 / (1.0 + jnp.exp(-x))


def _silu(x):
    return x * _sigmoid(x)


def _softplus(x):
    return jnp.maximum(x, 0.0) + jnp.log(1.0 + jnp.exp(-jnp.abs(x)))


def _resid_ln(scale, h, branch, g, b):
    r = ALPHA * h + scale * branch
    mu = jnp.mean(r, axis=-1, keepdims=True)
    var = jnp.mean(jnp.square(r - mu), axis=-1, keepdims=True)
    return (r - mu) * lax.rsqrt(var + LN_EPS) * g + b


def _rms(t, w):
    return t * lax.rsqrt(jnp.mean(t * t, axis=-1, keepdims=True) + RMS_EPS) * w


def _branch_weights(l1, l2, l3):
    m = jnp.maximum(jnp.maximum(l1, l2), l3)
    e1, e2, e3 = jnp.exp(l1 - m), jnp.exp(l2 - m), jnp.exp(l3 - m)
    inv = 1.0 / (e1 + e2 + e3)
    return e1 * inv, e2 * inv, e3 * inv


def _gate(y, xs, z, dskip, w):
    return _rms((y + dskip * xs) * _silu(z), w)


def _rot(x):
    d = lax.broadcasted_iota(jnp.int32, x.shape, 1) % HEAD_DIM
    up = pltpu.roll(x, x.shape[1] - ROPE_DIM // 2, 1)
    down = jnp.where(d < ROPE_DIM, pltpu.roll(x, ROPE_DIM // 2, 1), 0.0)
    return jnp.where(d < ROPE_DIM // 2, up, down)


def ffn_gate_up(name, h, wg, wu, after=()):
    m, nf = h.shape[0], wg.shape[1]
    tn = _tile(nf, MM_TN)

    def body(h_ref, g_w, u_w, *rest):
        du_ref, dg_ref, a_ref, at_ref = rest[-4:]
        hb = h_ref[...].astype(BF16)
        g = _dot(hb, g_w[...].astype(BF16), _NN)
        u = _dot(hb, u_w[...].astype(BF16), _NN)
        sig = _sigmoid(g)
        gs = g * sig
        du_ref[...] = gs.astype(du_ref.dtype)
        dg_ref[...] = (u * (sig + gs * (1.0 - sig))).astype(dg_ref.dtype)
        a = gs * u
        a_ref[...] = a.astype(a_ref.dtype)
        at_ref[...] = a.T.astype(at_ref.dtype)

    in_specs, args = _mm_specs(name, [(h, wg, 'nn')], nf, MM_TM, tn)
    in_specs.append(in_specs[1])
    in_specs += [pl.BlockSpec(memory_space=pl.ANY) for _ in after]
    tile = pl.BlockSpec((MM_TM, tn), lambda j, i: (i, j))
    return pl.pallas_call(
        body, name=name, grid=(nf // tn, m // MM_TM), in_specs=in_specs,
        out_specs=[tile] * 3 + [pl.BlockSpec((tn, MM_TM), lambda j, i: (j, i))],
        out_shape=[jax.ShapeDtypeStruct((m, nf), BF16)] * 3 + [jax.ShapeDtypeStruct((nf, m), BF16)],
        compiler_params=_cparams(("parallel", "parallel")),
    )(*args, wu, *after)


def mm_tn_cat(name, a, bs, out_dtype=F32, tt=1024):
    t, k = a.shape
    widths = [b.shape[1] for b in bs]
    n, tk, tt = sum(widths), _tile(k, (512, 256, 128)), min(tt, t)
    n_t = t // tt

    def body(a_ref, *rest):
        b_refs, o_ref, acc_ref = rest[:len(bs)], rest[-2], rest[-1]
        s = pl.program_id(1)
        at = a_ref[...].astype(BF16)
        d = jnp.concatenate([_dot(at, b[...].astype(BF16), _TN) for b in b_refs], axis=1)

        @pl.when(s == 0)
        def _():
            acc_ref[...] = d

        @pl.when(s > 0)
        def _():
            acc_ref[...] += d

        @pl.when(s == n_t - 1)
        def _():
            o_ref[...] = acc_ref[...].astype(o_ref.dtype)

    return pl.pallas_call(
        body, name=name, grid=(k // tk, n_t),
        in_specs=[pl.BlockSpec((tt, tk), lambda kk, s: (s, kk))] + [pl.BlockSpec((tt, w), lambda kk, s: (s, 0)) for w in widths],
        out_specs=pl.BlockSpec((tk, n), lambda kk, s: (kk, 0)),
        out_shape=jax.ShapeDtypeStruct((k, n), out_dtype), scratch_shapes=[pltpu.VMEM((tk, n), F32)],
        compiler_params=_cparams(("parallel", "arbitrary")),
    )(a, *bs)


def mm_acc(name, a, b, out_dtype=F32, tt=1024, after=None):
    k, t = a.shape
    n = b.shape[1]
    tk, tn, tt = _tile(k, (1024, 512, 256, 128)), _tile(n, MM_TN), min(tt, t)
    n_t = t // tt
    order = [] if after is None else [after]

    def body(a_ref, b_ref, *rest):
        o_ref, acc_ref = rest[-2:]
        s = pl.program_id(2)
        d = _dot(a_ref[...].astype(BF16), b_ref[...].astype(BF16), _NN)

        @pl.when(s == 0)
        def _():
            acc_ref[...] = d

        @pl.when(s > 0)
        def _():
            acc_ref[...] += d

        @pl.when(s == n_t - 1)
        def _():
            o_ref[...] = acc_ref[...].astype(o_ref.dtype)

    return pl.pallas_call(
        body, name=name, grid=(k // tk, n // tn, n_t),
        in_specs=[pl.BlockSpec((tk, tt), lambda kk, nn, s: (kk, s)), pl.BlockSpec((tt, tn), lambda kk, nn, s: (s, nn))]
        + [pl.BlockSpec(memory_space=pl.ANY) for _ in order],
        out_specs=pl.BlockSpec((tk, tn), lambda kk, nn, s: (kk, nn)),
        out_shape=jax.ShapeDtypeStruct((k, n), out_dtype), scratch_shapes=[pltpu.VMEM((tk, tn), F32)],
        compiler_params=_cparams(("parallel", "parallel", "arbitrary")),
    )(a, b, *order)


def ffn_da_act(name, df, wd, a_du, a_dg):
    m, nf = df.shape[0], wd.shape[0]
    tn = _tile(nf, MM_TN)

    def body(df_ref, w_ref, adu_ref, adg_ref, dg_ref, du_ref):
        da = _dot(df_ref[...].astype(BF16), w_ref[...].astype(BF16), _NT)
        dg_ref[...] = (da * adg_ref[...].astype(F32)).astype(dg_ref.dtype)
        du_ref[...] = (da * adu_ref[...].astype(F32)).astype(du_ref.dtype)

    in_specs, args = _mm_specs(name, [(df, wd, 'nt')], nf, MM_TM, tn)
    tile = pl.BlockSpec((MM_TM, tn), lambda j, i: (i, j))
    return pl.pallas_call(
        body, name=name, grid=(nf // tn, m // MM_TM), in_specs=in_specs + [tile, tile], out_specs=[tile] * 2,
        out_shape=[jax.ShapeDtypeStruct((m, nf), BF16)] * 2, compiler_params=_cparams(("parallel", "parallel")),
    )(*args, a_du, a_dg)


def resid_ln_fwd(name, scale, h, branch, ln_g, ln_b):
    t = h.shape[0]

    def fn(*a):
        y = _resid_ln(scale, *a)
        return y, y

    return rowwise(name, fn, [h, branch], [ln_g, ln_b], [(t, D_MODEL, F32), (t, D_MODEL, BF16)], tm=512)


def ffn_fwd(tag, hb, wg, wu, wd, after=()):
    g, u, a, at = ffn_gate_up(f"{tag}_gate_up", hb, wg, wu, after)
    f = mm(f"{tag}_down", [(a, wd, 'nn')], D_MODEL, out_dtype=BF16)
    return f, (hb, g, u, at)


def ln_loss_bwd(name, h, branch, target, ln_g, ln_b):
    t, dm = h.shape

    def fn(h_, br_, tgt, g_, b_):
        y, vjp = jax.vjp(functools.partial(_resid_ln, 0.5), h_, br_, g_, b_)
        e = y - tgt
        return (*vjp(e * (1.0 / dm)), jnp.sum(e * e, axis=0, keepdims=True))

    return rowwise(name, fn, [h, branch, target], [ln_g, ln_b], [(t, dm, F32), (t, dm, BF16)],
                   accs=[(1, dm), (1, dm), (1, dm)], tm=512)


def resid_ln_bwd(name, scale, h, branch, ln_g, ln_b, dout, extra=None):
    t = h.shape[0]

    def fn(h_, br_, do_, *rest):
        g_, b_ = rest[-2], rest[-1]
        _, vjp = jax.vjp(functools.partial(_resid_ln, scale), h_, br_, g_, b_)
        dh, dbr, dg, db = vjp(do_)
        if extra is not None:
            dh = dh + rest[0]
        return dh, dbr, dg, db

    ins = [h, branch, dout] + ([extra] if extra is not None else [])
    return rowwise(name, fn, ins, [ln_g, ln_b], [(t, D_MODEL, F32), (t, D_MODEL, BF16)],
                   accs=[(1, D_MODEL), (1, D_MODEL)], tm=512)


def ffn_bwd(tag, res, wg, wu, wd, df, dh_resid):
    hb, g, u, at = res
    dg, du = ffn_da_act(f"{tag}_bwd_da_act", df, wd, g, u)
    dwd = mm_acc(f"{tag}_bwd_dwd", at, df, BF16)
    dh = mm(f"{tag}_bwd_dh", [(dg, wg, 'nt'), (du, wu, 'nt')], D_MODEL, add=dh_resid, tn=512)
    dwg = mm_tn(f"{tag}_bwd_dwg", hb, dg, BF16)
    dwu = mm_tn(f"{tag}_bwd_dwu", hb, du, BF16)
    return dh, dwg, dwu, dwd


def rope_tables(positions):
    inv_freq = ROPE_THETA ** (-jnp.arange(0, ROPE_DIM, 2, dtype=F32) / ROPE_DIM)
    ang = positions.reshape(-1, 1).astype(F32) * inv_freq
    c, s = jnp.cos(ang), jnp.sin(ang)
    t = ang.shape[0]
    cosv = jnp.concatenate([c, c, jnp.ones((t, HEAD_DIM - ROPE_DIM), F32)], axis=1)
    sinv = jnp.concatenate([-s, s, jnp.zeros((t, HEAD_DIM - ROPE_DIM), F32)], axis=1)
    return jnp.tile(cosv, (1, 2)), jnp.tile(sinv, (1, 2))


def _pair_masks():
    lane = lax.broadcasted_iota(jnp.int32, (1, LANES), 1)
    return (lane < HEAD_DIM, lane >= HEAD_DIM)


def _band_masks():
    row = lax.broadcasted_iota(jnp.int32, (ATTN_BLOCK, ATTN_BLOCK), 0)
    col = lax.broadcasted_iota(jnp.int32, (ATTN_BLOCK, ATTN_BLOCK), 1)
    return col >= row, col <= row


def _residue_blocks():
    out = []
    for g, d in enumerate(DILATIONS):
        for r in range(d):
            for i in range(SEQ // d // ATTN_BLOCK):
                rows = lambda j: pl.ds(r + j * ATTN_BLOCK * d, ATTN_BLOCK, stride=d) if d > 1 else pl.ds(j * ATTN_BLOCK, ATTN_BLOCK)
                out.append((g, rows(i), rows(i - 1) if i > 0 else None))
    return out


N_HEAD_PAIRS = D_ATTN // LANES
SCALE = HEAD_DIM ** -0.5
ATTN_GROUP = 4
ATTN_GROUP_BWD = 16


def _block_operands(qr, kr, v_ref, cur, prev):
    prev_ok, cur_ok = _band_masks()
    if prev is None:
        return qr[cur, :], kr[cur, :].astype(BF16), v_ref[cur, :], cur_ok
    kcat = jnp.concatenate([kr[prev, :], kr[cur, :]], axis=0).astype(BF16)
    vcat = jnp.concatenate([v_ref[prev, :], v_ref[cur, :]], axis=0)
    return qr[cur, :], kcat, vcat, jnp.concatenate([prev_ok, cur_ok], axis=1)


def _attn_specs(b):
    col = lambda cb: pl.BlockSpec((SEQ, LANES), lambda bb, hp: (bb, cb + hp))
    tab = pl.BlockSpec((SEQ, LANES), lambda bb, hp: (bb, 0))
    return col, tab


def attn_fwd(qkvz, cosv, sinv, b):
    t = qkvz.shape[0]
    col, tab = _attn_specs(b)
    blocks = _residue_blocks()

    def body(q_ref, k_ref, v_ref, c_ref, s_ref, o_ref, l1_ref, l2_ref, l3_ref, qr, kr, o1, o2, o3):
        l_refs, o_scr = (l1_ref, l2_ref, l3_ref), (o1, o2, o3)
        c, s = c_ref[...], s_ref[...]
        q, k = q_ref[...], k_ref[...]
        qr[...] = q * c + _rot(q) * s
        kr[...] = k * c + _rot(k) * s
        masks = _pair_masks()
        for lo in range(0, len(blocks), ATTN_GROUP):
            chains = []
            for g, cur, prev in blocks[lo:lo + ATTN_GROUP]:
                q2, kcat, vcat, ok = _block_operands(qr, kr, v_ref, cur, prev)
                for m in masks:
                    qm = jnp.where(m, q2, 0.0).astype(BF16)
                    chains.append(dict(g=g, cur=cur, m=m, v=jnp.where(m, vcat, 0.0).astype(BF16),
                                       s=jnp.where(ok, _dot(qm, kcat, _NT) * SCALE, NEG)))
            for ch in chains:
                mx = jnp.max(ch['s'], axis=1, keepdims=True)
                p = jnp.exp(ch['s'] - mx)
                den = jnp.sum(p, axis=1, keepdims=True)
                ch.update(p=p.astype(BF16), inv=1.0 / den, lse=mx + jnp.log(den))
            for ch in chains:
                ch['o'] = _dot(ch['p'], ch['v'], _NN) * ch['inv']
            for c0, c1 in zip(chains[0::2], chains[1::2]):
                o_scr[c0['g']][c0['cur'], :] = c0['o'] + c1['o']
                l_refs[c0['g']][c0['cur'], :] = jnp.where(c0['m'], c0['lse'], c1['lse'])
        w1, w2, w3 = _branch_weights(l1_ref[...], l2_ref[...], l3_ref[...])
        o_ref[...] = w1 * o1[...] + w2 * o2[...] + w3 * o3[...]

    shp = jax.ShapeDtypeStruct((t, D_ATTN), F32)
    return pl.pallas_call(
        body, name="attn_fwd", grid=(b, N_HEAD_PAIRS),
        in_specs=[col(0), col(N_HEAD_PAIRS), col(2 * N_HEAD_PAIRS), tab, tab],
        out_specs=[col(0)] * 4, out_shape=[shp] * 4,
        scratch_shapes=[pltpu.VMEM((SEQ, LANES), F32)] * 5,
        compiler_params=_cparams(("parallel", "parallel")),
    )(qkvz, qkvz, qkvz, cosv, sinv)


def attn_bwd(qkvz, cosv, sinv, dmix, mixed, lses, b):
    t = qkvz.shape[0]
    col, tab = _attn_specs(b)
    blocks = _residue_blocks()
    hd = np.arange(LANES) // HEAD_DIM
    head_ones = jnp.asarray((hd[:, None] == hd[None, :]).astype(np.float32))

    def body(q_ref, k_ref, v_ref, c_ref, s_ref, dm_ref, mx_ref, l1_ref, l2_ref, l3_ref, ones_ref,
             dq_out, dk_out, dv_out, qr, kr, do1, do2, do3, dd1, dd2, dd3, dq_ref, dk_ref, dv_ref):
        l_refs, do_scr, dd_scr = (l1_ref, l2_ref, l3_ref), (do1, do2, do3), (dd1, dd2, dd3)
        c, s = c_ref[...], s_ref[...]
        q, k = q_ref[...], k_ref[...]
        qr[...] = q * c + _rot(q) * s
        kr[...] = k * c + _rot(k) * s
        dm = dm_ref[...]
        tot = _dot(dm * mx_ref[...], ones_ref[...], _NN, HI)
        for w, do_g, dd_g in zip(_branch_weights(l1_ref[...], l2_ref[...], l3_ref[...]), do_scr, dd_scr):
            do_g[...] = w * dm
            dd_g[...] = w * tot
        dq_ref[...] = jnp.zeros((SEQ, LANES), F32)
        dk_ref[...] = jnp.zeros((SEQ, LANES), F32)
        dv_ref[...] = jnp.zeros((SEQ, LANES), F32)
        masks = _pair_masks()
        for lo in range(0, len(blocks), ATTN_GROUP_BWD):
            chains = []
            for g, cur, prev in blocks[lo:lo + ATTN_GROUP_BWD]:
                q2, kcat, vcat, ok = _block_operands(qr, kr, v_ref, cur, prev)
                vcat = vcat.astype(BF16)
                do2_, l2, dd2_ = do_scr[g][cur, :], l_refs[g][cur, :], dd_scr[g][cur, :]
                l2s, dd2s = pltpu.roll(l2, HEAD_DIM, 1), pltpu.roll(dd2_, HEAD_DIM, 1)
                for m in masks:
                    qm = jnp.where(m, q2, 0.0).astype(BF16)
                    dom = jnp.where(m, do2_, 0.0).astype(BF16)
                    lrep, ddrep = jnp.where(m, l2, l2s), jnp.where(m, dd2_, dd2s)
                    if prev is not None:
                        lrep, ddrep = jnp.concatenate([lrep, lrep], axis=1), jnp.concatenate([ddrep, ddrep], axis=1)
                    chains.append(dict(cur=cur, prev=prev, qm=qm, dom=dom, km=jnp.where(m, kcat, 0), lrep=lrep, ddrep=ddrep,
                                       s=jnp.where(ok, _dot(qm, kcat, _NT) * SCALE, NEG), dp=_dot(dom, vcat, _NT)))
            for ch in chains:
                p = jnp.exp(ch['s'] - ch['lrep'])
                ch.update(p=p.astype(BF16), ds=(p * (ch['dp'] - ch['ddrep']) * SCALE).astype(BF16))
            for ch in chains:
                ch.update(dq=_dot(ch['ds'], ch['km'], _NN), dk=_dot(ch['ds'], ch['qm'], _TN), dv=_dot(ch['p'], ch['dom'], _TN))
            for c0, c1 in zip(chains[0::2], chains[1::2]):
                cur, prev = c0['cur'], c0['prev']
                dk, dv = c0['dk'] + c1['dk'], c0['dv'] + c1['dv']
                dq_ref[cur, :] += c0['dq'] + c1['dq']
                if prev is None:
                    dk_ref[cur, :] += dk
                    dv_ref[cur, :] += dv
                else:
                    dk_ref[prev, :] += dk[:ATTN_BLOCK]
                    dv_ref[prev, :] += dv[:ATTN_BLOCK]
                    dk_ref[cur, :] += dk[ATTN_BLOCK:]
                    dv_ref[cur, :] += dv[ATTN_BLOCK:]
        dq, dk = dq_ref[...], dk_ref[...]
        dq_out[...] = (dq * c + _rot(dq * s)).astype(dq_out.dtype)
        dk_out[...] = (dk * c + _rot(dk * s)).astype(dk_out.dtype)
        dv_out[...] = dv_ref[...].astype(dv_out.dtype)

    shp = jax.ShapeDtypeStruct((t, D_ATTN), BF16)
    return pl.pallas_call(
        body, name="attn_bwd", grid=(b, N_HEAD_PAIRS),
        in_specs=[col(0), col(N_HEAD_PAIRS), col(2 * N_HEAD_PAIRS), tab, tab, col(0), col(0), col(0), col(0), col(0),
                  pl.BlockSpec((LANES, LANES), lambda bb, hp: (0, 0))],
        out_specs=[col(0)] * 3, out_shape=[shp] * 3,
        scratch_shapes=[pltpu.VMEM((SEQ, LANES), F32)] * 11,
        compiler_params=_cparams(("parallel", "parallel")),
    )(qkvz, qkvz, qkvz, cosv, sinv, dmix, mixed, *lses, head_ones)


def attn_norm_fwd(mixed, norm_w):
    return rowwise("attn_norm", _rms, [mixed], [norm_w], [(mixed.shape[0], D_ATTN, BF16)])[0]


def attn_norm_bwd(dout, mixed, norm_w):
    def fn(dy, mx, w):
        _, vjp = jax.vjp(_rms, mx, w)
        return vjp(dy)

    return rowwise("attn_norm_bwd", fn, [dout, mixed], [norm_w], [(dout.shape[0], D_ATTN, F32)], accs=[(1, D_ATTN)])


CONV_TM = 256
HALO = 8


def _conv_columns(refs):
    xs_ref, bm_ref, cm_ref = refs
    out = []
    for c in range(D_CONV // LANES):
        lo = c * LANES
        ref, base = (xs_ref, 0) if lo < D_SSD else (bm_ref, D_SSD) if lo < D_SSD + D_BC else (cm_ref, D_SSD + D_BC)
        out.append((slice(lo, lo + LANES), (ref, slice(lo - base, lo - base + LANES))))
    return out


def _conv_taps(scr, w_ref, cs, first_row, step, tm):
    acc = None
    for k in range(CONV_WIDTH):
        term = w_ref[k:k + 1, cs] * scr[pl.ds(first_row + step * k, tm), cs]
        acc = term if acc is None else acc + term
    return acc


def conv_fwd(u, w, bias):
    t = u.shape[0]
    tm, per_seq = CONV_TM, SEQ // CONV_TM

    def body(u_ref, h_ref, w_ref, b_ref, xs_ref, bm_ref, cm_ref, scr):
        first = pl.program_id(0) % per_seq == 0
        scr[0:HALO, :] = jnp.where(first, 0.0, h_ref[...])
        scr[HALO:, :] = u_ref[...]
        for cs, (o_ref, os_) in _conv_columns((xs_ref, bm_ref, cm_ref)):
            o_ref[:, os_] = _silu(_conv_taps(scr, w_ref, cs, HALO - CONV_WIDTH + 1, 1, tm) + b_ref[:, cs])

    return pl.pallas_call(
        body, name="conv_fwd", grid=(t // tm,),
        in_specs=[pl.BlockSpec((tm, D_CONV), lambda i: (i, 0)),
                  pl.BlockSpec((HALO, D_CONV), lambda i: (jnp.maximum(i * (tm // HALO) - 1, 0), 0)),
                  pl.BlockSpec((CONV_WIDTH, D_CONV), lambda i: (0, 0)), pl.BlockSpec((1, D_CONV), lambda i: (0, 0))],
        out_specs=[pl.BlockSpec((tm, D_SSD), lambda i: (i, 0)), pl.BlockSpec((tm, D_BC), lambda i: (i, 0)),
                   pl.BlockSpec((tm, D_BC), lambda i: (i, 0))],
        out_shape=[jax.ShapeDtypeStruct((t, D_SSD), F32), jax.ShapeDtypeStruct((t, D_BC), F32),
                   jax.ShapeDtypeStruct((t, D_BC), F32)],
        scratch_shapes=[pltpu.VMEM((tm + HALO, D_CONV), F32)],
        compiler_params=_cparams(("parallel",)),
    )(u, u, w, bias)


def conv_bwd(u, w, bias, dxs_a, dxs_b, dbm, dcm):
    t = u.shape[0]
    tm, per_seq = CONV_TM, SEQ // CONV_TM
    n_tiles = t // tm

    def body1(u_ref, h_ref, dxs_ref, dxs2_ref, dbm_ref, dcm_ref, w_ref, b_ref, dz_ref, dw_ref, db_ref, scr):
        i = pl.program_id(0)
        first = i % per_seq == 0
        scr[0:HALO, :] = jnp.where(first, 0.0, h_ref[...])
        scr[HALO:, :] = u_ref[...]

        @pl.when(i == 0)
        def _():
            dw_ref[...] = jnp.zeros(dw_ref.shape, F32)
            db_ref[...] = jnp.zeros(db_ref.shape, F32)
        for cs, (g_ref, gs) in _conv_columns((dxs_ref, dbm_ref, dcm_ref)):
            acc = _conv_taps(scr, w_ref, cs, HALO - CONV_WIDTH + 1, 1, tm) + b_ref[:, cs]
            sig = _sigmoid(acc)
            dy = g_ref[:, gs] + dxs2_ref[:, gs] if g_ref is dxs_ref else g_ref[:, gs]
            dz = dy * sig * (1.0 + acc * (1.0 - sig))
            dz_ref[:, cs] = dz
            db_ref[:, cs] += jnp.sum(dz, axis=0, keepdims=True)
            for k in range(CONV_WIDTH):
                dw_ref[k:k + 1, cs] += jnp.sum(dz * scr[pl.ds(HALO - CONV_WIDTH + 1 + k, tm), cs], axis=0, keepdims=True)

    dz, dw, db = pl.pallas_call(
        body1, name="conv_bwd_dz", grid=(n_tiles,),
        in_specs=[pl.BlockSpec((tm, D_CONV), lambda i: (i, 0)),
                  pl.BlockSpec((HALO, D_CONV), lambda i: (jnp.maximum(i * (tm // HALO) - 1, 0), 0)),
                  pl.BlockSpec((tm, D_SSD), lambda i: (i, 0)), pl.BlockSpec((tm, D_SSD), lambda i: (i, 0)),
                  pl.BlockSpec((tm, D_BC), lambda i: (i, 0)), pl.BlockSpec((tm, D_BC), lambda i: (i, 0)),
                  pl.BlockSpec((CONV_WIDTH, D_CONV), lambda i: (0, 0)), pl.BlockSpec((1, D_CONV), lambda i: (0, 0))],
        out_specs=[pl.BlockSpec((tm, D_CONV), lambda i: (i, 0)), pl.BlockSpec((CONV_WIDTH, D_CONV), lambda i: (0, 0)),
                   pl.BlockSpec((1, D_CONV), lambda i: (0, 0))],
        out_shape=[jax.ShapeDtypeStruct((t, D_CONV), F32), jax.ShapeDtypeStruct((CONV_WIDTH, D_CONV), F32),
                   jax.ShapeDtypeStruct((1, D_CONV), F32)],
        scratch_shapes=[pltpu.VMEM((tm + HALO, D_CONV), F32)],
        compiler_params=_cparams(("arbitrary",)),
    )(u, u, dxs_a, dxs_b, dbm, dcm, w, bias)

    def body2(dz_ref, n_ref, w_ref, du_ref, scr):
        last = pl.program_id(0) % per_seq == per_seq - 1
        scr[0:tm, :] = dz_ref[...]
        scr[tm:, :] = jnp.where(last, 0.0, n_ref[...])
        for c in range(D_CONV // LANES):
            cs = slice(c * LANES, (c + 1) * LANES)
            du_ref[:, cs] = _conv_taps(scr, w_ref, cs, CONV_WIDTH - 1, -1, tm).astype(du_ref.dtype)

    du = pl.pallas_call(
        body2, name="conv_bwd_du", grid=(n_tiles,),
        in_specs=[pl.BlockSpec((tm, D_CONV), lambda i: (i, 0)),
                  pl.BlockSpec((HALO, D_CONV), lambda i: (jnp.minimum((i + 1) * (tm // HALO), t // HALO - 1), 0)),
                  pl.BlockSpec((CONV_WIDTH, D_CONV), lambda i: (0, 0))],
        out_specs=pl.BlockSpec((tm, D_CONV), lambda i: (i, 0)),
        out_shape=jax.ShapeDtypeStruct((t, D_CONV), BF16),
        scratch_shapes=[pltpu.VMEM((tm + HALO, D_CONV), F32)],
        compiler_params=_cparams(("parallel",)),
    )(dz, dz, w)
    return du, dw, db


Q = SSD_CHUNK
N_PAIRS = D_SSD // LANES
HEADS_PER_GROUP = N_HEADS // SSD_GROUPS


def _rep(a, j):
    return jnp.broadcast_to(a[:, j:j + 1], a.shape)


def _dot_exact01(a, b, dn, a_is_01):
    x = b if a_is_01 else a
    hi = x.astype(BF16)
    mid = (x - hi.astype(F32)).astype(BF16)
    lo = (x - hi.astype(F32) - mid.astype(F32)).astype(BF16)
    z = a.astype(BF16) if a_is_01 else b.astype(BF16)
    out = None
    for term in (hi, mid, lo):
        d = _dot(z, term, dn) if a_is_01 else _dot(term, z, dn)
        out = d if out is None else out + d
    return out


def _pad_lanes(v, fill=0.0):
    row = jnp.pad(v.reshape(1, -1).astype(F32), ((0, 0), (0, LANES - v.size)), constant_values=fill)
    return row, row.reshape(LANES, 1)


def _ssd_common(dtr_ref, dtrt_ref, bias_r, bias_c, alog_r, alog_c):
    row = lax.broadcasted_iota(jnp.int32, (Q, Q), 0)
    col = lax.broadcasted_iota(jnp.int32, (Q, Q), 1)
    tril = row >= col
    lane = lax.broadcasted_iota(jnp.int32, (1, LANES), 1)
    a_r = jnp.where(lane < N_HEADS, -jnp.exp(alog_r[...]), 0.0)
    sub = lax.broadcasted_iota(jnp.int32, (LANES, 1), 0)
    a_c = jnp.where(sub < N_HEADS, -jnp.exp(alog_c[...]), 0.0)
    dt = _softplus(dtr_ref[...] + bias_r[...])
    cs = _dot_exact01(tril, dt * a_r, _NN, True)
    dtt = _softplus(dtrt_ref[...] + bias_c[...])
    cst = _dot_exact01(dtt * a_c, row <= col, _NN, False)
    return tril, lane, a_r, dt, cs, cst


def _ssd_specs(b, nc, rev):
    ci = (lambda c: nc - 1 - c) if rev else (lambda c: c)
    rows = lambda w: pl.BlockSpec((Q, w), lambda bb, c: (bb * nc + ci(c), 0))
    dtt = pl.BlockSpec((LANES, Q), lambda bb, c: (0, bb * nc + ci(c)))
    const = lambda s: pl.BlockSpec(s, lambda bb, c: (0,) * len(s))
    state = pl.BlockSpec((None, N_PAIRS, LANES, SSD_STATE), lambda bb, c: (bb * nc + ci(c), 0, 0, 0))
    return rows, dtt, const, state


def ssd_fwd(xs, bm, cm, dtraw, dt_bias, a_log, b):
    t = xs.shape[0]
    nc = SEQ // Q
    rows, dtt_spec, const, state = _ssd_specs(b, nc, False)
    bias_r, bias_c = _pad_lanes(dt_bias)
    alog_r, alog_c = _pad_lanes(a_log)

    def body(xs_ref, b_ref, c_ref, dtr_ref, dtrt_ref, br, bc, ar, ac, y_ref, hp_ref, h_scr):
        @pl.when(pl.program_id(1) == 0)
        def _():
            h_scr[...] = jnp.zeros(h_scr.shape, F32)
        tril, lane, _, dt, cs, cst = _ssd_common(dtr_ref, dtrt_ref, br, bc, ar, ac)
        sub = lax.broadcasted_iota(jnp.int32, (LANES, 1), 0)
        y_acc = [jnp.zeros((Q, LANES), F32) for _ in range(N_PAIRS)]
        h_old = [h_scr[p] for p in range(N_PAIRS)]
        h_new = [jnp.zeros((LANES, SSD_STATE), F32) for _ in range(N_PAIRS)]
        for g in range(SSD_GROUPS):
            bg = b_ref[:, g * SSD_STATE:(g + 1) * SSD_STATE].astype(BF16)
            cg = c_ref[:, g * SSD_STATE:(g + 1) * SSD_STATE].astype(BF16)
            cb = _dot(cg, bg, _NT)
            heads = []
            for j in range(g * HEADS_PER_GROUP, (g + 1) * HEADS_PER_GROUP):
                p, side = j // 2, j % 2
                m = (lane < HEAD_DIM) if side == 0 else (lane >= HEAD_DIM)
                ms = (sub < HEAD_DIM) if side == 0 else (sub >= HEAD_DIM)
                csj, dtj = _rep(cs, j), _rep(dt, j)
                lmat = jnp.exp(jnp.where(tril, csj - cst[j:j + 1, :], NEG))
                xdt = jnp.where(m, xs_ref[:, p * LANES:(p + 1) * LANES] * dtj, 0.0)
                hm = jnp.where(ms, h_old[p], 0.0)
                last = csj[Q - 1:Q, :]
                heads.append(dict(p=p, hm=hm, ecs=jnp.exp(csj), el=jnp.exp(last), gmat=(cb * lmat).astype(BF16),
                                  xdt=xdt.astype(BF16), xd=(xdt * jnp.exp(last - csj)).astype(BF16)))
            for h in heads:
                h.update(ydiag=_dot(h['gmat'], h['xdt'], _NN), ch=_dot(cg, h['hm'].astype(BF16), _NT), sj=_dot(h['xd'], bg, _TN))
            for h in heads:
                y_acc[h['p']] = y_acc[h['p']] + h['ydiag'] + h['ecs'] * h['ch']
                h_new[h['p']] = h_new[h['p']] + h['el'] * h['hm'] + h['sj']
        for p in range(N_PAIRS):
            y_ref[:, p * LANES:(p + 1) * LANES] = y_acc[p]
            hp_ref[p] = h_old[p]
            h_scr[p] = h_new[p]

    return pl.pallas_call(
        body, name="ssd_fwd", grid=(b, nc),
        in_specs=[rows(D_SSD), rows(D_BC), rows(D_BC), rows(LANES), dtt_spec, const((1, LANES)), const((LANES, 1)),
                  const((1, LANES)), const((LANES, 1))],
        out_specs=[rows(D_SSD), state],
        out_shape=[jax.ShapeDtypeStruct((t, D_SSD), F32),
                   jax.ShapeDtypeStruct((b * nc, N_PAIRS, LANES, SSD_STATE), F32)],
        scratch_shapes=[pltpu.VMEM((N_PAIRS, LANES, SSD_STATE), F32)],
        compiler_params=_cparams(("parallel", "arbitrary")),
    )(xs, bm, cm, dtraw, dtraw.T, bias_r, bias_c, alog_r, alog_c)


def ssd_bwd(xs, bm, cm, dtraw, dt_bias, a_log, hprev, dy, b):
    t = xs.shape[0]
    nc = SEQ // Q
    rows, dtt_spec, const, state = _ssd_specs(b, nc, True)
    bias_r, bias_c = _pad_lanes(dt_bias)
    alog_r, alog_c = _pad_lanes(a_log)

    def body(xs_ref, b_ref, c_ref, dtr_ref, dtrt_ref, hp_ref, dy_ref, br, bc, ar, ac,
             dxs_ref, db_ref, dc_ref, ddt_ref, dbias_ref, dalog_ref, dh_scr):
        first = jnp.logical_and(pl.program_id(0) == 0, pl.program_id(1) == 0)

        @pl.when(pl.program_id(1) == 0)
        def _():
            dh_scr[...] = jnp.zeros(dh_scr.shape, F32)

        @pl.when(first)
        def _():
            dbias_ref[...] = jnp.zeros(dbias_ref.shape, F32)
            dalog_ref[...] = jnp.zeros(dalog_ref.shape, F32)
        tril, lane, a_r, dt, cs, cst = _ssd_common(dtr_ref, dtrt_ref, br, bc, ar, ac)
        sub = lax.broadcasted_iota(jnp.int32, (LANES, 1), 0)
        rowq = lax.broadcasted_iota(jnp.int32, (Q, 1), 0)
        triu = (lax.broadcasted_iota(jnp.int32, (Q, Q), 0) <= lax.broadcasted_iota(jnp.int32, (Q, Q), 1)).astype(F32)
        dxs_acc = [jnp.zeros((Q, LANES), F32) for _ in range(N_PAIRS)]
        dh_in = [dh_scr[p] for p in range(N_PAIRS)]
        h_in = [hp_ref[p] for p in range(N_PAIRS)]
        dh_out = [jnp.zeros((LANES, SSD_STATE), F32) for _ in range(N_PAIRS)]
        ddt = jnp.zeros((Q, LANES), F32)
        dalog = jnp.zeros((1, LANES), F32)
        for g in range(SSD_GROUPS):
            gs = slice(g * SSD_STATE, (g + 1) * SSD_STATE)
            bg, cg = b_ref[:, gs].astype(BF16), c_ref[:, gs].astype(BF16)
            cb = _dot(cg, bg, _NT)
            dcb = jnp.zeros((Q, Q), F32)
            dbg = jnp.zeros((Q, SSD_STATE), F32)
            dcg = jnp.zeros((Q, SSD_STATE), F32)
            heads = []
            for j in range(g * HEADS_PER_GROUP, (g + 1) * HEADS_PER_GROUP):
                p, side = j // 2, j % 2
                m = (lane < HEAD_DIM) if side == 0 else (lane >= HEAD_DIM)
                ms = (sub < HEAD_DIM) if side == 0 else (sub >= HEAD_DIM)
                csj, dtj = _rep(cs, j), _rep(dt, j)
                lmat = jnp.exp(jnp.where(tril, csj - cst[j:j + 1, :], NEG))
                x2 = jnp.where(m, xs_ref[:, p * LANES:(p + 1) * LANES], 0.0)
                xdt = x2 * dtj
                dym = jnp.where(m, dy_ref[:, p * LANES:(p + 1) * LANES], 0.0)
                hm = jnp.where(ms, h_in[p], 0.0)
                dhm = jnp.where(ms, dh_in[p], 0.0)
                last = csj[Q - 1:Q, :]
                decay = jnp.exp(last - csj)
                heads.append(dict(j=j, p=p, dtj=dtj, lmat=lmat, x2=x2, hm=hm, dhm=dhm, decay=decay, el=jnp.exp(last),
                                  gmat=cb * lmat, dym=dym.astype(BF16), xdt=xdt.astype(BF16), hmb=hm.astype(BF16),
                                  dhmb=dhm.astype(BF16), dye=dym * jnp.exp(csj), xd=xdt * decay))
            for h in heads:
                dyeb, xdb = h['dye'].astype(BF16), h['xd'].astype(BF16)
                h.update(dg=_dot(h['dym'], h['xdt'], _NT),
                         dxdt=_dot(h['gmat'].astype(BF16), h['dym'], _TN),
                         ch=_dot(cg, h['hmb'], _NT),
                         dcg=_dot(dyeb, h['hmb'], _NN), dhp=_dot(dyeb, cg, _TN),
                         wmat=_dot(bg, h['dhmb'], _NT),
                         dbg=_dot(xdb, h['dhmb'], _NN))
            for h in heads:
                ej = h['dg'] * h['gmat']
                col_sums = jnp.broadcast_to(jnp.sum(ej, axis=0, keepdims=True), (Q, Q)).T
                dl = h['xd'] * h['wmat']
                total = lambda v: jnp.sum(jnp.sum(v, axis=0, keepdims=True), axis=1, keepdims=True)
                dlast = total(dl) + h['el'] * total(h['dhm'] * h['hm'])
                h['dcs'] = (jnp.sum(ej + h['dye'] * h['ch'] - dl, axis=1, keepdims=True) - col_sums
                            + jnp.where(rowq == Q - 1, dlast, 0.0))
                h['dxdt'] = h['dxdt'] + h['decay'] * h['wmat']
                dcb, dcg, dbg = dcb + h['dg'] * h['lmat'], dcg + h['dcg'], dbg + h['dbg']
                dh_out[h['p']] = dh_out[h['p']] + h['el'] * h['dhm'] + h['dhp']
            for h in heads:
                h['da'] = _dot_exact01(triu, h['dcs'], _NN, True)
            for h in heads:
                j, da = h['j'], h['da']
                aj = jnp.sum(jnp.where(lane == j, a_r, 0.0), axis=1, keepdims=True)
                ddtj = da * aj + jnp.sum(h['dxdt'] * h['x2'], axis=1, keepdims=True)
                ddt = ddt + jnp.where(lane == j, ddtj, 0.0)
                dalog = dalog + jnp.where(lane == j, jnp.sum(da * h['dtj'], axis=0, keepdims=True) * aj, 0.0)
                dxs_acc[h['p']] = dxs_acc[h['p']] + h['dxdt'] * h['dtj']
            dcbb = dcb.astype(BF16)
            dc_ref[:, gs] = dcg + _dot(dcbb, bg, _NN)
            db_ref[:, gs] = dbg + _dot(dcbb, cg, _TN)
        for p in range(N_PAIRS):
            dxs_ref[:, p * LANES:(p + 1) * LANES] = dxs_acc[p]
            dh_scr[p] = dh_out[p]
        ddtraw = ddt * _sigmoid(dtr_ref[...] + br[...])
        ddt_ref[...] = ddtraw
        dbias_ref[...] += jnp.sum(ddtraw, axis=0, keepdims=True)
        dalog_ref[...] += dalog

    return pl.pallas_call(
        body, name="ssd_bwd", grid=(b, nc),
        in_specs=[rows(D_SSD), rows(D_BC), rows(D_BC), rows(LANES), dtt_spec, state, rows(D_SSD), const((1, LANES)),
                  const((LANES, 1)), const((1, LANES)), const((LANES, 1))],
        out_specs=[rows(D_SSD), rows(D_BC), rows(D_BC), rows(LANES), const((1, LANES)), const((1, LANES))],
        out_shape=[jax.ShapeDtypeStruct((t, D_SSD), F32), jax.ShapeDtypeStruct((t, D_BC), F32),
                   jax.ShapeDtypeStruct((t, D_BC), F32), jax.ShapeDtypeStruct((t, LANES), F32),
                   jax.ShapeDtypeStruct((1, LANES), F32), jax.ShapeDtypeStruct((1, LANES), F32)],
        scratch_shapes=[pltpu.VMEM((N_PAIRS, LANES, SSD_STATE), F32)],
        compiler_params=_cparams(("arbitrary", "arbitrary")),
    )(xs, bm, cm, dtraw, dtraw.T, hprev, dy, bias_r, bias_c, alog_r, alog_c)


def _split_w_in(w_in):
    w_dt = jnp.pad(w_in[:, D_QKVZ + D_CONV:], ((0, 0), (0, LANES - N_HEADS)))
    return w_in[:, :D_QKVZ], w_in[:, D_QKVZ:D_QKVZ + D_CONV], w_dt


def mixer_fwd(hb, p, cosv, sinv, b):
    t = hb.shape[0]
    w_a, w_b, w_c = _split_w_in(p['w_in'])
    qkvz = mm("in_qkvz", [(hb, w_a, 'nn')], D_QKVZ)
    xbc = mm("in_xbc", [(hb, w_b, 'nn')], D_CONV)
    dtraw = mm("in_dt", [(hb, w_c, 'nn')], LANES)
    mixed, *lses = attn_fwd(qkvz, cosv, sinv, b)
    attn = attn_norm_fwd(mixed, p['attn_norm_w'])
    xs, bm, cm = conv_fwd(xbc, p['conv_w'], p['conv_b'])
    y, hprev = ssd_fwd(xs, bm, cm, dtraw, p['dt_bias'], p['a_log'], b)
    dskip = jnp.repeat(p['d_skip'].reshape(-1), HEAD_DIM).reshape(1, D_SSD)
    yg, = rowwise("ssd_gate", _gate, [y, xs, Op(qkvz, D_SSD, 3)], [dskip, p['ssd_norm_w']], [(t, D_SSD, BF16)])
    mix = mm("out_proj", [(attn, p['w_out'][:D_ATTN], 'nn'), (yg, p['w_out'][D_ATTN:], 'nn')], D_MODEL, out_dtype=BF16)
    res = dict(hb=hb, qkvz=qkvz, xbc=xbc, dtraw=dtraw, mixed=mixed, lses=lses, attn=attn, xs=xs, bm=bm, cm=cm,
               y=y, hprev=hprev, dskip=dskip, yg=yg, cosv=cosv, sinv=sinv)
    return mix, res


def mixer_bwd(r, p, dmix, dh_resid, b):
    t = dmix.shape[0]
    w_a, w_b, w_c = _split_w_in(p['w_in'])
    w_out = p['w_out']
    dattn = mm("out_bwd_dattn", [(dmix, w_out[:D_ATTN], 'nt')], D_ATTN)
    dyg = mm("out_bwd_dyg", [(dmix, w_out[D_ATTN:], 'nt')], D_SSD)
    dw_out = jnp.concatenate([mm_tn("out_bwd_dw_a", r['attn'], dmix, BF16),
                              mm_tn("out_bwd_dw_y", r['yg'], dmix, BF16)], axis=0)

    def gate_bwd(dy_, y_, xs_, z_, ds_, w_):
        _, vjp = jax.vjp(_gate, y_, xs_, z_, ds_, w_)
        return vjp(dy_)

    dy, dxs_a, dz, ddskip, dssd_norm = rowwise(
        "ssd_gate_bwd", gate_bwd, [dyg, r['y'], r['xs'], Op(r['qkvz'], D_SSD, 3)], [r['dskip'], p['ssd_norm_w']],
        [(t, D_SSD, F32), (t, D_SSD, F32), (t, D_SSD, BF16)], accs=[(1, D_SSD), (1, D_SSD)])
    dxs_b, dbm, dcm, ddtraw, ddt_bias, da_log = ssd_bwd(r['xs'], r['bm'], r['cm'], r['dtraw'], p['dt_bias'], p['a_log'],
                                                        r['hprev'], dy, b)
    dxbc, dconv_w, dconv_b = conv_bwd(r['xbc'], p['conv_w'], p['conv_b'], dxs_a, dxs_b, dbm, dcm)
    dmixed, dattn_norm = attn_norm_bwd(dattn, r['mixed'], p['attn_norm_w'])
    dq, dk, dv = attn_bwd(r['qkvz'], r['cosv'], r['sinv'], dmixed, r['mixed'], r['lses'], b)
    wq, wk, wv, wz = (w_a[:, i * D_ATTN:(i + 1) * D_ATTN] for i in range(4))
    dh = mm("in_bwd_dh", [(dq, wq, 'nt'), (dk, wk, 'nt'), (dv, wv, 'nt'), (dz, wz, 'nt'), (dxbc, w_b, 'nt'),
                          (ddtraw, w_c, 'nt')], D_MODEL, add=dh_resid, tn=512)
    h = r['hb']
    dw_in = jnp.concatenate([mm_tn_cat("in_bwd_dw_qkvz", h, [dq, dk, dv, dz], BF16),
                             mm_tn_cat("in_bwd_dw_xbc_dt", h, [dxbc, ddtraw], BF16)[:, :D_CONV + N_HEADS]], axis=1)
    head_sum = lambda v: v.reshape(N_HEADS, HEAD_DIM).sum(axis=1).reshape(1, N_HEADS)
    grads = dict(w_in=dw_in, w_out=dw_out, conv_w=dconv_w, conv_b=dconv_b, dt_bias=ddt_bias[:, :N_HEADS],
                 a_log=da_log[:, :N_HEADS], d_skip=head_sum(ddskip), attn_norm_w=dattn_norm, ssd_norm_w=dssd_norm)
    return dh, grads


FFN2_KEYS = ('ffn2_gate', 'ffn2_up', 'ffn2_down')
MIXER_KEYS = ('w_in', 'conv_w', 'w_out')
FFN_COL = ('ffn1_gate', 'ffn1_up', 'ffn2_gate', 'ffn2_up')
FFN_ROW = ('ffn1_down', 'ffn2_down')
CONV_W_COMM = (8, 2 * LANES)
SMALL = 'small'


def comm_shape(k, shapes):
    if k in FFN_COL:
        return (D_MODEL, FF_PAD)
    if k in FFN_ROW:
        return (FF_PAD, D_MODEL)
    if k == 'conv_w':
        return CONV_W_COMM
    return tuple(shapes[k][1:])


def to_comm(k, vals, shapes):
    a = vals[k].reshape(shapes[k][1:])
    r_, c_ = comm_shape(k, shapes)
    return jnp.pad(a, ((0, r_ - a.shape[0]), (0, c_ - a.shape[1])))


SMALL_ROWS, SMALL_COLS = 16, D_CONV


def pack_small(small):
    rows = [jnp.pad(small[r].reshape(1, -1), ((0, 0), (0, SMALL_COLS - small[r].size))) for r in REPLICATED]
    return jnp.concatenate(rows + [jnp.zeros((SMALL_ROWS - len(rows), SMALL_COLS), F32)], axis=0)


def full_weight(k, g):
    if k in FFN_COL:
        return g
    if k == 'conv_w':
        return jnp.transpose(g[:, :CONV_WIDTH, :D_CONV // N_DEV], (1, 0, 2)).reshape(CONV_WIDTH, D_CONV)
    return g.reshape(N_DEV * g.shape[1], g.shape[2])


def grad_shards(k, g):
    if k in FFN_COL:
        return g
    if k == 'conv_w':
        s = jnp.transpose(g.reshape(CONV_WIDTH, N_DEV, D_CONV // N_DEV), (1, 0, 2))
        return jnp.pad(s, ((0, 0), (0, CONV_W_COMM[0] - CONV_WIDTH), (0, CONV_W_COMM[1] - D_CONV // N_DEV)))
    return g.reshape(N_DEV, g.shape[0] // N_DEV, g.shape[1])


def _flip(v, bit):
    return 1 - v if bit else v


N_PEER_COPIES = N_DEV - 1


def _comm_call(name, body, arrs, out_shape):
    n = len(arrs)
    return pl.pallas_call(
        functools.partial(body, n), name=name, out_shape=out_shape,
        in_specs=[pl.BlockSpec(memory_space=pl.ANY)] * n, out_specs=[pl.BlockSpec(memory_space=pl.ANY)] * n,
        scratch_shapes=[pltpu.SemaphoreType.DMA((n * N_PEER_COPIES,)), pltpu.SemaphoreType.DMA((n * N_PEER_COPIES,)),
                        pltpu.SemaphoreType.DMA((n,))],
    )(*arrs)


def _blk(ref, idx, by_cols):
    if not by_cols:
        return ref.at[idx]
    c = ref.shape[1] // N_DEV
    return ref.at[:, pl.ds(pl.multiple_of(idx * c, LANES), c)]


def _blocked_shape(a, by_cols):
    return (a.shape[0], N_DEV * a.shape[1]) if by_cols else (N_DEV,) + a.shape


def all_gather(arrs, by_cols):
    def body(n, *refs):
        x_refs, out_refs, (send_sems, recv_sems, local_sems) = refs[:n], refs[n:2 * n], refs[2 * n:]
        x, y, c = lax.axis_index("x"), lax.axis_index("y"), lax.axis_index("c")
        me, sibling = (x, y, c), (x, y, 1 - c)
        chips = [(1 - x, y), (x, 1 - y), (1 - x, 1 - y)]

        def copy(a, k, block, to, src=None):
            px, py, pc = block
            dst = _blk(out_refs[a], 4 * px + 2 * py + pc, by_cols[a])
            return pltpu.make_async_remote_copy(
                src_ref=dst if src is None else src, dst_ref=dst, send_sem=send_sems.at[a * N_PEER_COPIES + k],
                recv_sem=recv_sems.at[a * N_PEER_COPIES + k], device_id=to, device_id_type=MESH)

        mine = [pltpu.make_async_copy(x_refs[a], _blk(out_refs[a], 4 * x + 2 * y + c, by_cols[a]), local_sems.at[a])
                for a in range(n)]
        started = []
        for a in range(n):
            mine[a].start()
            first = [copy(a, 0, me, sibling, src=x_refs[a])]
            first += [copy(a, 1 + j, me, (*chip, c), src=x_refs[a]) for j, chip in enumerate(chips)]
            for cp in first:
                cp.start()
            started += first
        for j, chip in enumerate(chips):
            for a in range(n):
                copy(a, 1 + j, (*chip, c), me).wait_recv()
                cp = copy(a, 4 + j, (*chip, c), sibling)
                cp.start()
                started.append(cp)
        for a in range(n):
            copy(a, 0, sibling, me).wait_recv()
            for j, chip in enumerate(chips):
                copy(a, 4 + j, (*chip, 1 - c), me).wait_recv()
        for cp in started:
            cp.wait_send()
        for cp in mine:
            cp.wait()

    return _comm_call("all_gather_weights", body, arrs,
                      [jax.ShapeDtypeStruct(_blocked_shape(a, bc), a.dtype) for a, bc in zip(arrs, by_cols)])


def blocks_to_cols(arrs):
    def body(*refs):
        for i, o in zip(refs[:len(arrs)], refs[len(arrs):]):
            o[...] = i[...]

    return pl.pallas_call(
        body, name="blocks_to_cols", grid=(N_DEV,),
        in_specs=[pl.BlockSpec((None,) + a.shape[1:], lambda p: (p, 0, 0)) for a in arrs],
        out_specs=[pl.BlockSpec(a.shape[1:], lambda p: (0, p)) for a in arrs],
        out_shape=[jax.ShapeDtypeStruct((a.shape[1], N_DEV * a.shape[2]), a.dtype) for a in arrs],
        compiler_params=_cparams(("parallel",)),
    )(*arrs)


def _landing_shape(a, by_cols):
    return (N_DEV, a.shape[0], a.shape[1] // N_DEV) if by_cols else a.shape


_HBM = pl.BlockSpec(memory_space=pltpu.HBM)
_SEM = pl.BlockSpec(memory_space=pltpu.SEMAPHORE)
_EFFECT = pltpu.SideEffectType.DATAFLOW_SIDE_EFFECTING


def _peer(k):
    x, y, c = lax.axis_index("x"), lax.axis_index("y"), lax.axis_index("c")
    return _flip(x, k & 4), _flip(y, k & 2), _flip(c, k & 1)


def _my_index():
    return 4 * lax.axis_index("x") + 2 * lax.axis_index("y") + lax.axis_index("c")


def _split_copies(mode, by_cols, src_refs, land_refs, send_sems, recv_sems):
    me = _my_index()
    out = []
    for a, bc in enumerate(by_cols):
        for k in range(1, N_DEV):
            px, py, pc = _peer(k)
            src = _blk(src_refs[a], 4 * px + 2 * py + pc, bc) if mode == 'scatter' else src_refs[a]
            dst = land_refs[a].at[me] if mode == 'scatter' else _blk(land_refs[a], me, bc)
            out.append(pltpu.make_async_remote_copy(
                src_ref=src, dst_ref=dst, send_sem=send_sems.at[a * N_PEER_COPIES + k - 1],
                recv_sem=recv_sems.at[a * N_PEER_COPIES + k - 1], device_id=(px, py, pc), device_id_type=MESH))
    return out


def exchange_start(name, mode, srcs, by_cols):
    n = len(srcs)
    lands = [lax.empty(_landing_shape(s, bc) if mode == 'scatter' else _blocked_shape(s, bc), s.dtype)
             for s, bc in zip(srcs, by_cols)]

    def body(*refs):
        src_refs, land_refs, send_sems, recv_sems = refs[:n], refs[n:2 * n], refs[2 * n], refs[2 * n + 1]
        for cp in _split_copies(mode, by_cols, src_refs, land_refs, send_sems, recv_sems):
            cp.start()
        refs[-1][...] = jnp.zeros(refs[-1].shape, F32)

    sems = pltpu.SemaphoreType.DMA((n * N_PEER_COPIES,))
    res = pl.pallas_call(
        body, name=name,
        out_shape=(sems, sems, *[pltpu.HBM(a.shape, a.dtype) for a in srcs + lands], jax.ShapeDtypeStruct((8, LANES), F32)),
        in_specs=(_HBM,) * (2 * n), out_specs=(_SEM, _SEM, *(_HBM,) * (2 * n), pl.BlockSpec(memory_space=pltpu.VMEM)),
        input_output_aliases={i: 2 + i for i in range(2 * n)},
        compiler_params=pltpu.CompilerParams(has_side_effects=_EFFECT),
    )(*[pltpu.with_memory_space_constraint(a, pltpu.HBM) for a in srcs + lands])
    return (mode, by_cols, res[:-1]), res[-1]


def exchange_wait(name, handles, after):
    mode, by_cols, (send_sems, recv_sems, *bufs) = handles
    n = len(by_cols)

    def body(*refs):
        src_refs, land_refs, s_sems, r_sems = refs[:n], refs[n:2 * n], refs[2 * n], refs[2 * n + 1]
        for cp in _split_copies(mode, by_cols, src_refs, land_refs, s_sems, r_sems):
            cp.wait_send()
            cp.wait_recv()

    res = pl.pallas_call(
        body, name=name, out_shape=tuple(pltpu.HBM(a.shape, a.dtype) for a in bufs),
        in_specs=(*(_HBM,) * (2 * n), _SEM, _SEM, pl.BlockSpec(memory_space=pl.ANY)), out_specs=(_HBM,) * (2 * n),
        input_output_aliases={i: i for i in range(2 * n)},
        compiler_params=pltpu.CompilerParams(has_side_effects=_EFFECT),
    )(*bufs, send_sems, recv_sems, after)
    me, out = _my_index(), []
    for src, land, bc in zip(res[:n], res[n:], by_cols):
        if mode == 'scatter':
            c = land.shape[2]
            own = lax.dynamic_slice(src, (0, me * c), (src.shape[0], c)) if bc else lax.dynamic_index_in_dim(src, me, 0, False)
            out.append(lax.dynamic_update_slice(land, own[None], (me, 0, 0)))
        elif bc:
            out.append(lax.dynamic_update_slice(land, src, (0, me * src.shape[1])))
        else:
            out.append(lax.dynamic_update_slice(land, src[None], (me, 0, 0)))
    return out


def _adamw_math(g, w, m, v):
    c1 = 1.0 / (1.0 - ADAM_B1 ** ADAM_STEP)
    c2 = 1.0 / (1.0 - ADAM_B2 ** ADAM_STEP)
    m = ADAM_B1 * m + (1.0 - ADAM_B1) * g
    v = ADAM_B2 * v + (1.0 - ADAM_B2) * jnp.square(g)
    return g, -ADAM_LR * ((m * c1) / (jnp.sqrt(v * c2) + ADAM_EPS) + ADAM_WD * w), m, v


def adamw(name, recv, w, m, v, tm):
    _, rows, cols = w.shape
    tm = min(tm, rows)

    def body(*refs):
        g = refs[0][0:tm, 0:cols].astype(F32)
        for s in range(1, N_DEV):
            g = g + refs[s][0:tm, 0:cols].astype(F32)
        res = _adamw_math(g, *[r[...] for r in refs[N_DEV:N_DEV + 3]])
        for r, val in zip(refs[N_DEV + 3:], res):
            r[...] = val

    part = lambda s: pl.BlockSpec((None, recv.shape[1] if tm == rows else tm, recv.shape[2]), lambda i: (s, i, 0))
    tile = pl.BlockSpec((None, tm, cols), lambda i: (0, i, 0))
    return pl.pallas_call(
        body, name=name, grid=(rows // tm,), in_specs=[part(s) for s in range(N_DEV)] + [tile] * 3, out_specs=[tile] * 4,
        out_shape=[jax.ShapeDtypeStruct((1, rows, cols), F32)] * 4, compiler_params=_cparams(("parallel",)),
    )(*[recv] * N_DEV, w, m, v)


def adamw_small(recv, wl, ml, vl):
    n = len(REPLICATED)

    def body(recv_ref, *refs):
        g = recv_ref[0]
        for s in range(1, N_DEV):
            g = g + recv_ref[s]
        for r in range(n):
            w, m, v = (refs[j * n + r][...] for j in range(3))
            for j, val in enumerate(_adamw_math(g[r:r + 1, :w.shape[1]], w, m, v)):
                refs[(3 + j) * n + r][...] = val

    arrs = [d[k].reshape(1, -1) for d in (wl, ml, vl) for k in REPLICATED]
    res = pl.pallas_call(
        body, name="adamw_small", out_shape=[jax.ShapeDtypeStruct(a.shape, F32) for a in arrs[:n]] * 4,
    )(recv, *arrs)
    return [{k: res[j * n + r].reshape(wl[k].shape) for r, k in enumerate(REPLICATED)} for j in range(4)]


ADAMW_TM = {'ffn1_gate': 256, 'ffn1_up': 256, 'ffn2_gate': 256, 'ffn2_up': 256, 'w_in': 32}


def kernel(x, positions, ln1_g, ln1_b, ffn1_gate, ffn1_up, ffn1_down, w_in, conv_w, conv_b, dt_bias, a_log, d_skip, attn_norm_w, ssd_norm_w, w_out, ln2_g, ln2_b, ffn2_gate, ffn2_up, ffn2_down, ln3_g, ln3_b, loss_target, m_ln1_g, m_ln1_b, m_ffn1_gate, m_ffn1_up, m_ffn1_down, m_w_in, m_conv_w, m_conv_b, m_dt_bias, m_a_log, m_d_skip, m_attn_norm_w, m_ssd_norm_w, m_w_out, m_ln2_g, m_ln2_b, m_ffn2_gate, m_ffn2_up, m_ffn2_down, m_ln3_g, m_ln3_b, v_ln1_g, v_ln1_b, v_ffn1_gate, v_ffn1_up, v_ffn1_down, v_w_in, v_conv_w, v_conv_b, v_dt_bias, v_a_log, v_d_skip, v_attn_norm_w, v_ssd_norm_w, v_w_out, v_ln2_g, v_ln2_b, v_ffn2_gate, v_ffn2_up, v_ffn2_down, v_ln3_g, v_ln3_b):
    args = dict(locals())
    wl = {k: args[k] for k in WEIGHTS}
    ml = {k: args["m_" + k] for k in WEIGHTS}
    vl = {k: args["v_" + k] for k in WEIGHTS}
    shapes = {k: wl[k].shape for k in WEIGHTS}
    b, s, dm = x.shape
    t = b * s

    sent = {k: to_comm(k, wl, shapes).astype(F32 if k == 'conv_w' else BF16) for k in SHARDED}
    by_cols = lambda keys: [k in FFN_COL for k in keys]
    gate, up = all_gather([sent['ffn1_gate'], sent['ffn1_up']], [False] * 2)
    (gate, up), sent = lax.optimization_barrier(((gate, up), sent))
    p = dict(zip(('ffn1_gate', 'ffn1_up'), blocks_to_cols([gate, up])))
    gather_down, token_d = exchange_start("gather_ffn1_down_start", 'gather', [sent['ffn1_down']], [False])
    sent['w_in'] = sent['w_in'] + token_d[0, 0].astype(BF16)
    gather_mixer, token_m = exchange_start("gather_mixer_start", 'gather', [sent[k] for k in MIXER_KEYS], by_cols(MIXER_KEYS))
    sent['ffn2_gate'] = sent['ffn2_gate'] + token_m[0, 0].astype(BF16)
    gather_ffn2, token_f = exchange_start("gather_ffn2_start", 'gather', [sent[k] for k in FFN2_KEYS], by_cols(FFN2_KEYS))
    for k in REPLICATED:
        p[k] = wl[k].reshape(1, -1)

    x2 = x.reshape(t, dm)
    cosv, sinv = rope_tables(positions)
    g1, u1, a1, at1 = ffn_gate_up("ffn1_gate_up", x2, p['ffn1_gate'], p['ffn1_up'], after=(token_d, token_m, token_f))
    p['ffn1_down'] = full_weight('ffn1_down', exchange_wait("gather_ffn1_down_wait", gather_down, a1)[0])
    f1, res1 = mm("ffn1_down", [(a1, p['ffn1_down'], 'nn')], D_MODEL, out_dtype=BF16), (x2, g1, u1, at1)
    h1, h1b = resid_ln_fwd("ln1", 0.5, x2, f1, p['ln1_g'], p['ln1_b'])
    for k, g in zip(MIXER_KEYS, exchange_wait("gather_mixer_wait", gather_mixer, h1b)):
        p[k] = full_weight(k, g)
    mix, resm = mixer_fwd(h1b, p, cosv, sinv, b)
    h2, h2b = resid_ln_fwd("ln2", 1.0, h1, mix, p['ln2_g'], p['ln2_b'])
    for k, g in zip(FFN2_KEYS, exchange_wait("gather_ffn2_wait", gather_ffn2, h2b)):
        p[k] = full_weight(k, g)
    f2, res3 = ffn_fwd("ffn2", h2b, p['ffn2_gate'], p['ffn2_up'], p['ffn2_down'])

    small, full = {}, {}
    dh2_res, df2, small['ln3_g'], small['ln3_b'], sq = ln_loss_bwd("ln3_loss_bwd", h2, f2, loss_target.reshape(t, dm),
                                                                   p['ln3_g'], p['ln3_b'])
    loss = lax.psum(jnp.sum(sq) * (0.5 / dm), AXES)

    dh2, full['ffn2_gate'], full['ffn2_up'], full['ffn2_down'] = ffn_bwd("ffn2", res3, p['ffn2_gate'], p['ffn2_up'],
                                                                       p['ffn2_down'], df2, dh2_res)
    ffn2_exchange, token = exchange_start("grads_ffn2_start", 'scatter', [grad_shards(k, full[k]) for k in FFN2_KEYS],
                                          by_cols(FFN2_KEYS))
    dh1_res, dmix, small['ln2_g'], small['ln2_b'] = resid_ln_bwd("ln2_bwd", 1.0, h1, mix, p['ln2_g'] + token[:1, :1],
                                                                 p['ln2_b'], dh2)
    dh1, gm = mixer_bwd(resm, p, dmix, dh1_res, b)
    for k in ('conv_b', 'dt_bias', 'a_log', 'd_skip', 'attn_norm_w', 'ssd_norm_w'):
        small[k] = gm[k]
    mixer_exchange, token = exchange_start("grads_mixer_start", 'scatter', [grad_shards(k, gm[k]) for k in MIXER_KEYS],
                                           by_cols(MIXER_KEYS))
    dx_res, df1, small['ln1_g'], small['ln1_b'] = resid_ln_bwd("ln1_bwd", 0.5, x2, f1, p['ln1_g'] + token[:1, :1],
                                                               p['ln1_b'], dh1)
    hb, g, u, at = res1
    small_part = pack_small(small)
    dg, du = ffn_da_act("ffn1_bwd_da_act", df1, p['ffn1_down'], g, u)
    dwd = mm_acc("ffn1_bwd_dwd", at, df1, BF16, after=dg)
    down_exchange, token = exchange_start("grads_ffn1_down_start", 'scatter', [
        grad_shards('ffn1_down', dwd), jnp.broadcast_to(small_part[None], (N_DEV,) + small_part.shape)], [False, False])
    dwg = mm_tn("ffn1_bwd_dwg", hb, dg, BF16, after=token)
    gate_exchange, token = exchange_start("grads_ffn1_gate_start", 'scatter', [grad_shards('ffn1_gate', dwg)], [True])
    dwu = mm_tn("ffn1_bwd_dwu", hb, du, BF16, after=token)
    up_exchange, token = exchange_start("grads_ffn1_up_start", 'scatter', [grad_shards('ffn1_up', dwu)], [True])
    dx = mm("ffn1_bwd_dh", [(dg, p['ffn1_gate'], 'nt'), (du, p['ffn1_up'], 'nt')], D_MODEL, add=dx_res, tn=512, after=token)
    recv = {}
    for keys, name, ex in (((FFN2_KEYS), "grads_ffn2_wait", ffn2_exchange), (MIXER_KEYS, "grads_mixer_wait", mixer_exchange),
                           (('ffn1_down', SMALL), "grads_ffn1_down_wait", down_exchange),
                           (('ffn1_gate',), "grads_ffn1_gate_wait", gate_exchange),
                           (('ffn1_up',), "grads_ffn1_up_wait", up_exchange)):
        recv.update(zip(keys, exchange_wait(name, ex, dx)))
    outs = adamw_small(recv.pop(SMALL), wl, ml, vl)
    for k, r in recv.items():
        for o, a in zip(outs, adamw(f"adamw_{k}", r, wl[k], ml[k], vl[k], ADAMW_TM.get(k, shapes[k][1]))):
            o[k] = a
    return (loss, dx.reshape(b, s, dm), *[o[k] for o in outs for k in WEIGHTS])
```

```python
import functools

import jax
import jax.numpy as jnp
import numpy as np
from jax import lax
from jax.experimental import pallas as pl
from jax.experimental.pallas import tpu as pltpu

F32, BF16 = jnp.float32, jnp.bfloat16
HI = lax.Precision.HIGHEST
MESH = pl.DeviceIdType.MESH
AXES = ("x", "y", "c")
N_DEV = 8

D_MODEL = 1024
SEQ = 2048
HEAD_DIM = 64
N_HEADS = 12
D_ATTN = N_HEADS * HEAD_DIM
DILATIONS = (1, 4, 16)
ATTN_BLOCK = 128
ROPE_THETA = 500000.0
ROPE_DIM = 16
D_SSD = 768
SSD_GROUPS = 4
SSD_STATE = 128
SSD_CHUNK = 128
D_BC = SSD_GROUPS * SSD_STATE
D_CONV = D_SSD + 2 * D_BC
CONV_WIDTH = 4
D_QKVZ = 3 * D_ATTN + D_SSD
D_FF = 2816
ALPHA = 2.0 ** 0.25
LN_EPS = 1e-5
RMS_EPS = 1e-6
ADAM_LR, ADAM_B1, ADAM_B2, ADAM_EPS, ADAM_WD, ADAM_STEP = 0.001, 0.9, 0.999, 1e-08, 0.01, 10

LANES = 128
VMEM_LIMIT = 52 * 1024 * 1024
NEG = -1e30

WEIGHTS = ['ln1_g', 'ln1_b', 'ffn1_gate', 'ffn1_up', 'ffn1_down', 'w_in', 'conv_w', 'conv_b', 'dt_bias', 'a_log',
           'd_skip', 'attn_norm_w', 'ssd_norm_w', 'w_out', 'ln2_g', 'ln2_b', 'ffn2_gate', 'ffn2_up', 'ffn2_down',
           'ln3_g', 'ln3_b']
COL_SHARDED = ('ffn1_gate', 'ffn1_up', 'conv_w', 'ffn2_gate', 'ffn2_up')
ROW_SHARDED = ('ffn1_down', 'w_in', 'w_out', 'ffn2_down')
SHARDED = tuple(n for n in WEIGHTS if n in COL_SHARDED or n in ROW_SHARDED)
REPLICATED = tuple(n for n in WEIGHTS if n not in SHARDED)
FF_SHARD = D_FF // N_DEV
FF_PAD = -(-FF_SHARD // LANES) * LANES


def _cparams(sem=None):
    return pltpu.CompilerParams(dimension_semantics=sem, vmem_limit_bytes=VMEM_LIMIT)


def _tile(n, prefs):
    for p in prefs:
        if n % p == 0:
            return p
    return n


class Op:
    def __init__(self, arr, bw=None, cb=0, ro=0):
        self.arr, self.bw, self.cb, self.ro = arr, (arr.shape[1] if bw is None else bw), cb, ro


def _op(a):
    return a if isinstance(a, Op) else Op(a)


def rowwise(name, fn, ins, consts, outs, accs=(), tm=512):
    ins = [_op(a) for a in ins]
    rows = outs[0][0]
    n_in, n_c, n_o, n_a = len(ins), len(consts), len(outs), len(accs)
    tm = min(tm, rows)
    assert rows % tm == 0, (name, rows, tm)

    def body(*refs):
        vals = [r[...].astype(F32) for r in refs[:n_in + n_c]]
        res = fn(*vals)
        res = res if isinstance(res, (tuple, list)) else (res,)
        o_refs = refs[n_in + n_c:n_in + n_c + n_o]
        a_refs = refs[n_in + n_c + n_o:]
        for r, v in zip(o_refs, res[:n_o]):
            r[...] = v.astype(r.dtype)
        if n_a:
            @pl.when(pl.program_id(0) == 0)
            def _():
                for r in a_refs:
                    r[...] = jnp.zeros(r.shape, r.dtype)
            for r, v in zip(a_refs, res[n_o:]):
                r[...] += v

    in_specs = [pl.BlockSpec((tm, o.bw), functools.partial(lambda i, o: (i + o.ro, o.cb), o=o)) for o in ins]
    in_specs += [pl.BlockSpec(c.shape, functools.partial(lambda i, nd: (0,) * nd, nd=c.ndim)) for c in consts]
    out_specs = [pl.BlockSpec((tm, w), lambda i: (i, 0)) for (_, w, _) in outs]
    out_specs += [pl.BlockSpec(s, functools.partial(lambda i, nd: (0,) * nd, nd=len(s))) for s in accs]
    out_shape = [jax.ShapeDtypeStruct((r, w), dt) for (r, w, dt) in outs]
    out_shape += [jax.ShapeDtypeStruct(s, F32) for s in accs]
    res = pl.pallas_call(
        body, name=name, grid=(rows // tm,), in_specs=in_specs, out_specs=out_specs, out_shape=out_shape,
        compiler_params=_cparams(("arbitrary",) if n_a else ("parallel",)),
    )(*[o.arr for o in ins], *consts)
    return res


MM_TM = 1024
MM_TN = (1024, 896, 768, 512, 256, 128)
_NT = (((1,), (1,)), ((), ()))
_NN = (((1,), (0,)), ((), ()))
_TN = (((0,), (0,)), ((), ()))


def _dot(a, b, dn, precision=None):
    return lax.dot_general(a, b, dn, preferred_element_type=F32, precision=precision)


def _mm_specs(name, pairs, n_out, tm, tn):
    in_specs, args = [], []
    for a, b, mode in pairs:
        o = _op(a)
        in_specs.append(pl.BlockSpec((tm, o.bw), functools.partial(lambda j, i, o: (i, o.cb), o=o)))
        args.append(o.arr)
        if mode == 'nn':
            assert b.shape == (o.bw, n_out), (name, b.shape, o.bw, n_out)
            in_specs.append(pl.BlockSpec((o.bw, tn), lambda j, i: (0, j)))
        else:
            assert b.shape == (n_out, o.bw), (name, b.shape, o.bw, n_out)
            in_specs.append(pl.BlockSpec((tn, o.bw), lambda j, i: (j, 0)))
        args.append(b)
    return in_specs, args


def _mm_acc(refs, pairs):
    acc = None
    for k, (_, _, mode) in enumerate(pairs):
        d = _dot(refs[2 * k][...].astype(BF16), refs[2 * k + 1][...].astype(BF16), _NN if mode == 'nn' else _NT)
        acc = d if acc is None else acc + d
    return acc


def mm(name, pairs, n_out, add=None, out_dtype=F32, tm=MM_TM, tn=None, after=None):
    m = _op(pairs[0][0]).arr.shape[0]
    tn = tn or _tile(n_out, MM_TN)
    n_p = len(pairs)

    def body(*refs):
        acc = _mm_acc(refs, pairs)
        if add is not None:
            acc = acc + refs[2 * n_p][...]
        refs[-1][...] = acc.astype(refs[-1].dtype)

    in_specs, args = _mm_specs(name, pairs, n_out, tm, tn)
    tile = pl.BlockSpec((tm, tn), lambda j, i: (i, j))
    if add is not None:
        in_specs.append(tile)
        args.append(add)
    if after is not None:
        in_specs.append(pl.BlockSpec(memory_space=pl.ANY))
        args.append(after)
    return pl.pallas_call(
        body, name=name, grid=(n_out // tn, m // tm), in_specs=in_specs, out_specs=tile,
        out_shape=jax.ShapeDtypeStruct((m, n_out), out_dtype),
        compiler_params=_cparams(("parallel", "parallel")),
    )(*args)


def mm_tn(name, a, b, out_dtype=F32, tt=1024, after=None):
    a, b = _op(a), _op(b)
    t = a.arr.shape[0]
    k, n = a.bw, b.bw
    tk = _tile(k, (512, 896, 768, 256, 128))
    tn = _tile(n, (3072, 1792) + MM_TN)
    tt = min(tt, t)
    n_t = t // tt
    order = [] if after is None else [after]

    def body(a_ref, b_ref, *rest):
        o_ref, acc_ref = rest[-2:]
        s = pl.program_id(2)
        d = _dot(a_ref[...].astype(BF16), b_ref[...].astype(BF16), _TN)

        @pl.when(s == 0)
        def _():
            acc_ref[...] = d

        @pl.when(s > 0)
        def _():
            acc_ref[...] += d

        @pl.when(s == n_t - 1)
        def _():
            o_ref[...] = acc_ref[...].astype(o_ref.dtype)

    return pl.pallas_call(
        body, name=name, grid=(k // tk, n // tn, n_t),
        in_specs=[pl.BlockSpec((tt, tk), functools.partial(lambda kk, nn, s, o: (s, o.cb * (o.bw // tk) + kk), o=a)),
                  pl.BlockSpec((tt, tn), functools.partial(lambda kk, nn, s, o: (s, o.cb * (o.bw // tn) + nn), o=b))]
        + [pl.BlockSpec(memory_space=pl.ANY) for _ in order],
        out_specs=pl.BlockSpec((tk, tn), lambda kk, nn, s: (kk, nn)),
        out_shape=jax.ShapeDtypeStruct((k, n), out_dtype),
        scratch_shapes=[pltpu.VMEM((tk, tn), F32)],
        compiler_params=_cparams(("parallel", "parallel", "arbitrary")),
    )(a.arr, b.arr, *order)


def _sigmoid(x):
    return 1.0 / (1.0 + jnp.exp(-x))


def _silu(x):
    return x * _sigmoid(x)


def _softplus(x):
    return jnp.maximum(x, 0.0) + jnp.log(1.0 + jnp.exp(-jnp.abs(x)))


def _resid_ln(scale, h, branch, g, b):
    r = ALPHA * h + scale * branch
    mu = jnp.mean(r, axis=-1, keepdims=True)
    var = jnp.mean(jnp.square(r - mu), axis=-1, keepdims=True)
    return (r - mu) * lax.rsqrt(var + LN_EPS) * g + b


def _rms(t, w):
    return t * lax.rsqrt(jnp.mean(t * t, axis=-1, keepdims=True) + RMS_EPS) * w


def _branch_weights(l1, l2, l3):
    m = jnp.maximum(jnp.maximum(l1, l2), l3)
    e1, e2, e3 = jnp.exp(l1 - m), jnp.exp(l2 - m), jnp.exp(l3 - m)
    inv = 1.0 / (e1 + e2 + e3)
    return e1 * inv, e2 * inv, e3 * inv


def _gate(y, xs, z, dskip, w):
    return _rms((y + dskip * xs) * _silu(z), w)


def _rot(x):
    d = lax.broadcasted_iota(jnp.int32, x.shape, 1) % HEAD_DIM
    up = pltpu.roll(x, x.shape[1] - ROPE_DIM // 2, 1)
    down = jnp.where(d < ROPE_DIM, pltpu.roll(x, ROPE_DIM // 2, 1), 0.0)
    return jnp.where(d < ROPE_DIM // 2, up, down)


def ffn_gate_up(name, h, wg, wu, after=()):
    m, nf = h.shape[0], wg.shape[1]
    tn = _tile(nf, MM_TN)

    def body(h_ref, g_w, u_w, *rest):
        du_ref, dg_ref, a_ref, at_ref = rest[-4:]
        hb = h_ref[...].astype(BF16)
        g = _dot(hb, g_w[...].astype(BF16), _NN)
        u = _dot(hb, u_w[...].astype(BF16), _NN)
        sig = _sigmoid(g)
        gs = g * sig
        du_ref[...] = gs.astype(du_ref.dtype)
        dg_ref[...] = (u * (sig + gs * (1.0 - sig))).astype(dg_ref.dtype)
        a = gs * u
        a_ref[...] = a.astype(a_ref.dtype)
        at_ref[...] = a.T.astype(at_ref.dtype)

    in_specs, args = _mm_specs(name, [(h, wg, 'nn')], nf, MM_TM, tn)
    in_specs.append(in_specs[1])
    in_specs += [pl.BlockSpec(memory_space=pl.ANY) for _ in after]
    tile = pl.BlockSpec((MM_TM, tn), lambda j, i: (i, j))
    return pl.pallas_call(
        body, name=name, grid=(nf // tn, m // MM_TM), in_specs=in_specs,
        out_specs=[tile] * 3 + [pl.BlockSpec((tn, MM_TM), lambda j, i: (j, i))],
        out_shape=[jax.ShapeDtypeStruct((m, nf), BF16)] * 3 + [jax.ShapeDtypeStruct((nf, m), BF16)],
        compiler_params=_cparams(("parallel", "parallel")),
    )(*args, wu, *after)


def mm_tn_cat(name, a, bs, out_dtype=F32, tt=1024):
    t, k = a.shape
    widths = [b.shape[1] for b in bs]
    n, tk, tt = sum(widths), _tile(k, (512, 256, 128)), min(tt, t)
    n_t = t // tt

    def body(a_ref, *rest):
        b_refs, o_ref, acc_ref = rest[:len(bs)], rest[-2], rest[-1]
        s = pl.program_id(1)
        at = a_ref[...].astype(BF16)
        d = jnp.concatenate([_dot(at, b[...].astype(BF16), _TN) for b in b_refs], axis=1)

        @pl.when(s == 0)
        def _():
            acc_ref[...] = d

        @pl.when(s > 0)
        def _():
            acc_ref[...] += d

        @pl.when(s == n_t - 1)
        def _():
            o_ref[...] = acc_ref[...].astype(o_ref.dtype)

    return pl.pallas_call(
        body, name=name, grid=(k // tk, n_t),
        in_specs=[pl.BlockSpec((tt, tk), lambda kk, s: (s, kk))] + [pl.BlockSpec((tt, w), lambda kk, s: (s, 0)) for w in widths],
        out_specs=pl.BlockSpec((tk, n), lambda kk, s: (kk, 0)),
        out_shape=jax.ShapeDtypeStruct((k, n), out_dtype), scratch_shapes=[pltpu.VMEM((tk, n), F32)],
        compiler_params=_cparams(("parallel", "arbitrary")),
    )(a, *bs)


def mm_acc(name, a, b, out_dtype=F32, tt=1024, after=None):
    k, t = a.shape
    n = b.shape[1]
    tk, tn, tt = _tile(k, (1024, 512, 256, 128)), _tile(n, MM_TN), min(tt, t)
    n_t = t // tt
    order = [] if after is None else [after]

    def body(a_ref, b_ref, *rest):
        o_ref, acc_ref = rest[-2:]
        s = pl.program_id(2)
        d = _dot(a_ref[...].astype(BF16), b_ref[...].astype(BF16), _NN)

        @pl.when(s == 0)
        def _():
            acc_ref[...] = d

        @pl.when(s > 0)
        def _():
            acc_ref[...] += d

        @pl.when(s == n_t - 1)
        def _():
            o_ref[...] = acc_ref[...].astype(o_ref.dtype)

    return pl.pallas_call(
        body, name=name, grid=(k // tk, n // tn, n_t),
        in_specs=[pl.BlockSpec((tk, tt), lambda kk, nn, s: (kk, s)), pl.BlockSpec((tt, tn), lambda kk, nn, s: (s, nn))]
        + [pl.BlockSpec(memory_space=pl.ANY) for _ in order],
        out_specs=pl.BlockSpec((tk, tn), lambda kk, nn, s: (kk, nn)),
        out_shape=jax.ShapeDtypeStruct((k, n), out_dtype), scratch_shapes=[pltpu.VMEM((tk, tn), F32)],
        compiler_params=_cparams(("parallel", "parallel", "arbitrary")),
    )(a, b, *order)


def ffn_da_act(name, df, wd, a_du, a_dg):
    m, nf = df.shape[0], wd.shape[0]
    tn = _tile(nf, MM_TN)

    def body(df_ref, w_ref, adu_ref, adg_ref, dg_ref, du_ref):
        da = _dot(df_ref[...].astype(BF16), w_ref[...].astype(BF16), _NT)
        dg_ref[...] = (da * adg_ref[...].astype(F32)).astype(dg_ref.dtype)
        du_ref[...] = (da * adu_ref[...].astype(F32)).astype(du_ref.dtype)

    in_specs, args = _mm_specs(name, [(df, wd, 'nt')], nf, MM_TM, tn)
    tile = pl.BlockSpec((MM_TM, tn), lambda j, i: (i, j))
    return pl.pallas_call(
        body, name=name, grid=(nf // tn, m // MM_TM), in_specs=in_specs + [tile, tile], out_specs=[tile] * 2,
        out_shape=[jax.ShapeDtypeStruct((m, nf), BF16)] * 2, compiler_params=_cparams(("parallel", "parallel")),
    )(*args, a_du, a_dg)


def resid_ln_fwd(name, scale, h, branch, ln_g, ln_b):
    t = h.shape[0]

    def fn(*a):
        y = _resid_ln(scale, *a)
        return y, y

    return rowwise(name, fn, [h, branch], [ln_g, ln_b], [(t, D_MODEL, F32), (t, D_MODEL, BF16)], tm=512)


def ffn_fwd(tag, hb, wg, wu, wd, after=()):
    g, u, a, at = ffn_gate_up(f"{tag}_gate_up", hb, wg, wu, after)
    f = mm(f"{tag}_down", [(a, wd, 'nn')], D_MODEL, out_dtype=BF16)
    return f, (hb, g, u, at)


def ln_loss_bwd(name, h, branch, target, ln_g, ln_b):
    t, dm = h.shape

    def fn(h_, br_, tgt, g_, b_):
        y, vjp = jax.vjp(functools.partial(_resid_ln, 0.5), h_, br_, g_, b_)
        e = y - tgt
        return (*vjp(e * (1.0 / dm)), jnp.sum(e * e, axis=0, keepdims=True))

    return rowwise(name, fn, [h, branch, target], [ln_g, ln_b], [(t, dm, F32), (t, dm, BF16)],
                   accs=[(1, dm), (1, dm), (1, dm)], tm=512)


def resid_ln_bwd(name, scale, h, branch, ln_g, ln_b, dout, extra=None):
    t = h.shape[0]

    def fn(h_, br_, do_, *rest):
        g_, b_ = rest[-2], rest[-1]
        _, vjp = jax.vjp(functools.partial(_resid_ln, scale), h_, br_, g_, b_)
        dh, dbr, dg, db = vjp(do_)
        if extra is not None:
            dh = dh + rest[0]
        return dh, dbr, dg, db

    ins = [h, branch, dout] + ([extra] if extra is not None else [])
    return rowwise(name, fn, ins, [ln_g, ln_b], [(t, D_MODEL, F32), (t, D_MODEL, BF16)],
                   accs=[(1, D_MODEL), (1, D_MODEL)], tm=512)


def ffn_bwd(tag, res, wg, wu, wd, df, dh_resid):
    hb, g, u, at = res
    dg, du = ffn_da_act(f"{tag}_bwd_da_act", df, wd, g, u)
    dwd = mm_acc(f"{tag}_bwd_dwd", at, df, BF16)
    dh = mm(f"{tag}_bwd_dh", [(dg, wg, 'nt'), (du, wu, 'nt')], D_MODEL, add=dh_resid, tn=512)
    dwg = mm_tn(f"{tag}_bwd_dwg", hb, dg, BF16)
    dwu = mm_tn(f"{tag}_bwd_dwu", hb, du, BF16)
    return dh, dwg, dwu, dwd


def rope_tables(positions):
    inv_freq = ROPE_THETA ** (-jnp.arange(0, ROPE_DIM, 2, dtype=F32) / ROPE_DIM)
    ang = positions.reshape(-1, 1).astype(F32) * inv_freq
    c, s = jnp.cos(ang), jnp.sin(ang)
    t = ang.shape[0]
    cosv = jnp.concatenate([c, c, jnp.ones((t, HEAD_DIM - ROPE_DIM), F32)], axis=1)
    sinv = jnp.concatenate([-s, s, jnp.zeros((t, HEAD_DIM - ROPE_DIM), F32)], axis=1)
    return jnp.tile(cosv, (1, 2)), jnp.tile(sinv, (1, 2))


def _pair_masks():
    lane = lax.broadcasted_iota(jnp.int32, (1, LANES), 1)
    return (lane < HEAD_DIM, lane >= HEAD_DIM)


def _band_masks():
    row = lax.broadcasted_iota(jnp.int32, (ATTN_BLOCK, ATTN_BLOCK), 0)
    col = lax.broadcasted_iota(jnp.int32, (ATTN_BLOCK, ATTN_BLOCK), 1)
    return col >= row, col <= row


def _residue_blocks():
    out = []
    for g, d in enumerate(DILATIONS):
        for r in range(d):
            for i in range(SEQ // d // ATTN_BLOCK):
                rows = lambda j: pl.ds(r + j * ATTN_BLOCK * d, ATTN_BLOCK, stride=d) if d > 1 else pl.ds(j * ATTN_BLOCK, ATTN_BLOCK)
                out.append((g, rows(i), rows(i - 1) if i > 0 else None))
    return out


N_HEAD_PAIRS = D_ATTN // LANES
SCALE = HEAD_DIM ** -0.5
ATTN_GROUP = 4
ATTN_GROUP_BWD = 16


def _block_operands(qr, kr, v_ref, cur, prev):
    prev_ok, cur_ok = _band_masks()
    if prev is None:
        return qr[cur, :], kr[cur, :].astype(BF16), v_ref[cur, :], cur_ok
    kcat = jnp.concatenate([kr[prev, :], kr[cur, :]], axis=0).astype(BF16)
    vcat = jnp.concatenate([v_ref[prev, :], v_ref[cur, :]], axis=0)
    return qr[cur, :], kcat, vcat, jnp.concatenate([prev_ok, cur_ok], axis=1)


def _attn_specs(b):
    col = lambda cb: pl.BlockSpec((SEQ, LANES), lambda bb, hp: (bb, cb + hp))
    tab = pl.BlockSpec((SEQ, LANES), lambda bb, hp: (bb, 0))
    return col, tab


def attn_fwd(qkvz, cosv, sinv, b):
    t = qkvz.shape[0]
    col, tab = _attn_specs(b)
    blocks = _residue_blocks()

    def body(q_ref, k_ref, v_ref, c_ref, s_ref, o_ref, l1_ref, l2_ref, l3_ref, qr, kr, o1, o2, o3):
        l_refs, o_scr = (l1_ref, l2_ref, l3_ref), (o1, o2, o3)
        c, s = c_ref[...], s_ref[...]
        q, k = q_ref[...], k_ref[...]
        qr[...] = q * c + _rot(q) * s
        kr[...] = k * c + _rot(k) * s
        masks = _pair_masks()
        for lo in range(0, len(blocks), ATTN_GROUP):
            chains = []
            for g, cur, prev in blocks[lo:lo + ATTN_GROUP]:
                q2, kcat, vcat, ok = _block_operands(qr, kr, v_ref, cur, prev)
                for m in masks:
                    qm = jnp.where(m, q2, 0.0).astype(BF16)
                    chains.append(dict(g=g, cur=cur, m=m, v=jnp.where(m, vcat, 0.0).astype(BF16),
                                       s=jnp.where(ok, _dot(qm, kcat, _NT) * SCALE, NEG)))
            for ch in chains:
                mx = jnp.max(ch['s'], axis=1, keepdims=True)
                p = jnp.exp(ch['s'] - mx)
                den = jnp.sum(p, axis=1, keepdims=True)
                ch.update(p=p.astype(BF16), inv=1.0 / den, lse=mx + jnp.log(den))
            for ch in chains:
                ch['o'] = _dot(ch['p'], ch['v'], _NN) * ch['inv']
            for c0, c1 in zip(chains[0::2], chains[1::2]):
                o_scr[c0['g']][c0['cur'], :] = c0['o'] + c1['o']
                l_refs[c0['g']][c0['cur'], :] = jnp.where(c0['m'], c0['lse'], c1['lse'])
        w1, w2, w3 = _branch_weights(l1_ref[...], l2_ref[...], l3_ref[...])
        o_ref[...] = w1 * o1[...] + w2 * o2[...] + w3 * o3[...]

    shp = jax.ShapeDtypeStruct((t, D_ATTN), F32)
    return pl.pallas_call(
        body, name="attn_fwd", grid=(b, N_HEAD_PAIRS),
        in_specs=[col(0), col(N_HEAD_PAIRS), col(2 * N_HEAD_PAIRS), tab, tab],
        out_specs=[col(0)] * 4, out_shape=[shp] * 4,
        scratch_shapes=[pltpu.VMEM((SEQ, LANES), F32)] * 5,
        compiler_params=_cparams(("parallel", "parallel")),
    )(qkvz, qkvz, qkvz, cosv, sinv)


def attn_bwd(qkvz, cosv, sinv, dmix, mixed, lses, b):
    t = qkvz.shape[0]
    col, tab = _attn_specs(b)
    blocks = _residue_blocks()
    hd = np.arange(LANES) // HEAD_DIM
    head_ones = jnp.asarray((hd[:, None] == hd[None, :]).astype(np.float32))

    def body(q_ref, k_ref, v_ref, c_ref, s_ref, dm_ref, mx_ref, l1_ref, l2_ref, l3_ref, ones_ref,
             dq_out, dk_out, dv_out, qr, kr, do1, do2, do3, dd1, dd2, dd3, dq_ref, dk_ref, dv_ref):
        l_refs, do_scr, dd_scr = (l1_ref, l2_ref, l3_ref), (do1, do2, do3), (dd1, dd2, dd3)
        c, s = c_ref[...], s_ref[...]
        q, k = q_ref[...], k_ref[...]
        qr[...] = q * c + _rot(q) * s
        kr[...] = k * c + _rot(k) * s
        dm = dm_ref[...]
        tot = _dot(dm * mx_ref[...], ones_ref[...], _NN, HI)
        for w, do_g, dd_g in zip(_branch_weights(l1_ref[...], l2_ref[...], l3_ref[...]), do_scr, dd_scr):
            do_g[...] = w * dm
            dd_g[...] = w * tot
        dq_ref[...] = jnp.zeros((SEQ, LANES), F32)
        dk_ref[...] = jnp.zeros((SEQ, LANES), F32)
        dv_ref[...] = jnp.zeros((SEQ, LANES), F32)
        masks = _pair_masks()
        for lo in range(0, len(blocks), ATTN_GROUP_BWD):
            chains = []
            for g, cur, prev in blocks[lo:lo + ATTN_GROUP_BWD]:
                q2, kcat, vcat, ok = _block_operands(qr, kr, v_ref, cur, prev)
                vcat = vcat.astype(BF16)
                do2_, l2, dd2_ = do_scr[g][cur, :], l_refs[g][cur, :], dd_scr[g][cur, :]
                l2s, dd2s = pltpu.roll(l2, HEAD_DIM, 1), pltpu.roll(dd2_, HEAD_DIM, 1)
                for m in masks:
                    qm = jnp.where(m, q2, 0.0).astype(BF16)
                    dom = jnp.where(m, do2_, 0.0).astype(BF16)
                    lrep, ddrep = jnp.where(m, l2, l2s), jnp.where(m, dd2_, dd2s)
                    if prev is not None:
                        lrep, ddrep = jnp.concatenate([lrep, lrep], axis=1), jnp.concatenate([ddrep, ddrep], axis=1)
                    chains.append(dict(cur=cur, prev=prev, qm=qm, dom=dom, km=jnp.where(m, kcat, 0), lrep=lrep, ddrep=ddrep,
                                       s=jnp.where(ok, _dot(qm, kcat, _NT) * SCALE, NEG), dp=_dot(dom, vcat, _NT)))
            for ch in chains:
                p = jnp.exp(ch['s'] - ch['lrep'])
                ch.update(p=p.astype(BF16), ds=(p * (ch['dp'] - ch['ddrep']) * SCALE).astype(BF16))
            for ch in chains:
                ch.update(dq=_dot(ch['ds'], ch['km'], _NN), dk=_dot(ch['ds'], ch['qm'], _TN), dv=_dot(ch['p'], ch['dom'], _TN))
            for c0, c1 in zip(chains[0::2], chains[1::2]):
                cur, prev = c0['cur'], c0['prev']
                dk, dv = c0['dk'] + c1['dk'], c0['dv'] + c1['dv']
                dq_ref[cur, :] += c0['dq'] + c1['dq']
                if prev is None:
                    dk_ref[cur, :] += dk
                    dv_ref[cur, :] += dv
                else:
                    dk_ref[prev, :] += dk[:ATTN_BLOCK]
                    dv_ref[prev, :] += dv[:ATTN_BLOCK]
                    dk_ref[cur, :] += dk[ATTN_BLOCK:]
                    dv_ref[cur, :] += dv[ATTN_BLOCK:]
        dq, dk = dq_ref[...], dk_ref[...]
        dq_out[...] = (dq * c + _rot(dq * s)).astype(dq_out.dtype)
        dk_out[...] = (dk * c + _rot(dk * s)).astype(dk_out.dtype)
        dv_out[...] = dv_ref[...].astype(dv_out.dtype)

    shp = jax.ShapeDtypeStruct((t, D_ATTN), BF16)
    return pl.pallas_call(
        body, name="attn_bwd", grid=(b, N_HEAD_PAIRS),
        in_specs=[col(0), col(N_HEAD_PAIRS), col(2 * N_HEAD_PAIRS), tab, tab, col(0), col(0), col(0), col(0), col(0),
                  pl.BlockSpec((LANES, LANES), lambda bb, hp: (0, 0))],
        out_specs=[col(0)] * 3, out_shape=[shp] * 3,
        scratch_shapes=[pltpu.VMEM((SEQ, LANES), F32)] * 11,
        compiler_params=_cparams(("parallel", "parallel")),
    )(qkvz, qkvz, qkvz, cosv, sinv, dmix, mixed, *lses, head_ones)


def attn_norm_fwd(mixed, norm_w):
    return rowwise("attn_norm", _rms, [mixed], [norm_w], [(mixed.shape[0], D_ATTN, BF16)])[0]


def attn_norm_bwd(dout, mixed, norm_w):
    def fn(dy, mx, w):
        _, vjp = jax.vjp(_rms, mx, w)
        return vjp(dy)

    return rowwise("attn_norm_bwd", fn, [dout, mixed], [norm_w], [(dout.shape[0], D_ATTN, F32)], accs=[(1, D_ATTN)])


CONV_TM = 512
HALO = 8


def _conv_columns(refs):
    xs_ref, bm_ref, cm_ref = refs
    out = []
    for c in range(D_CONV // LANES):
        lo = c * LANES
        ref, base = (xs_ref, 0) if lo < D_SSD else (bm_ref, D_SSD) if lo < D_SSD + D_BC else (cm_ref, D_SSD + D_BC)
        out.append((slice(lo, lo + LANES), (ref, slice(lo - base, lo - base + LANES))))
    return out


def _conv_taps(scr, w_ref, cs, first_row, step, tm):
    acc = None
    for k in range(CONV_WIDTH):
        term = w_ref[k:k + 1, cs] * scr[pl.ds(first_row + step * k, tm), cs]
        acc = term if acc is None else acc + term
    return acc


def conv_fwd(u, w, bias):
    t = u.shape[0]
    tm, per_seq = CONV_TM, SEQ // CONV_TM

    def body(u_ref, h_ref, w_ref, b_ref, xs_ref, bm_ref, cm_ref, scr):
        first = pl.program_id(0) % per_seq == 0
        scr[0:HALO, :] = jnp.where(first, 0.0, h_ref[...])
        scr[HALO:, :] = u_ref[...]
        for cs, (o_ref, os_) in _conv_columns((xs_ref, bm_ref, cm_ref)):
            o_ref[:, os_] = _silu(_conv_taps(scr, w_ref, cs, HALO - CONV_WIDTH + 1, 1, tm) + b_ref[:, cs])

    return pl.pallas_call(
        body, name="conv_fwd", grid=(t // tm,),
        in_specs=[pl.BlockSpec((tm, D_CONV), lambda i: (i, 0)),
                  pl.BlockSpec((HALO, D_CONV), lambda i: (jnp.maximum(i * (tm // HALO) - 1, 0), 0)),
                  pl.BlockSpec((CONV_WIDTH, D_CONV), lambda i: (0, 0)), pl.BlockSpec((1, D_CONV), lambda i: (0, 0))],
        out_specs=[pl.BlockSpec((tm, D_SSD), lambda i: (i, 0)), pl.BlockSpec((tm, D_BC), lambda i: (i, 0)),
                   pl.BlockSpec((tm, D_BC), lambda i: (i, 0))],
        out_shape=[jax.ShapeDtypeStruct((t, D_SSD), F32), jax.ShapeDtypeStruct((t, D_BC), F32),
                   jax.ShapeDtypeStruct((t, D_BC), F32)],
        scratch_shapes=[pltpu.VMEM((tm + HALO, D_CONV), F32)],
        compiler_params=_cparams(("parallel",)),
    )(u, u, w, bias)


def conv_bwd(u, w, bias, dxs_a, dxs_b, dbm, dcm):
    t = u.shape[0]
    tm, per_seq = CONV_TM, SEQ // CONV_TM
    n_tiles = t // tm

    def body1(u_ref, h_ref, dxs_ref, dxs2_ref, dbm_ref, dcm_ref, w_ref, b_ref, dz_ref, dw_ref, db_ref, scr):
        i = pl.program_id(0)
        first = i % per_seq == 0
        scr[0:HALO, :] = jnp.where(first, 0.0, h_ref[...])
        scr[HALO:, :] = u_ref[...]

        @pl.when(i == 0)
        def _():
            dw_ref[...] = jnp.zeros(dw_ref.shape, F32)
            db_ref[...] = jnp.zeros(db_ref.shape, F32)
        for cs, (g_ref, gs) in _conv_columns((dxs_ref, dbm_ref, dcm_ref)):
            acc = _conv_taps(scr, w_ref, cs, HALO - CONV_WIDTH + 1, 1, tm) + b_ref[:, cs]
            sig = _sigmoid(acc)
            dy = g_ref[:, gs] + dxs2_ref[:, gs] if g_ref is dxs_ref else g_ref[:, gs]
            dz = dy * sig * (1.0 + acc * (1.0 - sig))
            dz_ref[:, cs] = dz
            db_ref[:, cs] += jnp.sum(dz, axis=0, keepdims=True)
            for k in range(CONV_WIDTH):
                dw_ref[k:k + 1, cs] += jnp.sum(dz * scr[pl.ds(HALO - CONV_WIDTH + 1 + k, tm), cs], axis=0, keepdims=True)

    dz, dw, db = pl.pallas_call(
        body1, name="conv_bwd_dz", grid=(n_tiles,),
        in_specs=[pl.BlockSpec((tm, D_CONV), lambda i: (i, 0)),
                  pl.BlockSpec((HALO, D_CONV), lambda i: (jnp.maximum(i * (tm // HALO) - 1, 0), 0)),
                  pl.BlockSpec((tm, D_SSD), lambda i: (i, 0)), pl.BlockSpec((tm, D_SSD), lambda i: (i, 0)),
                  pl.BlockSpec((tm, D_BC), lambda i: (i, 0)), pl.BlockSpec((tm, D_BC), lambda i: (i, 0)),
                  pl.BlockSpec((CONV_WIDTH, D_CONV), lambda i: (0, 0)), pl.BlockSpec((1, D_CONV), lambda i: (0, 0))],
        out_specs=[pl.BlockSpec((tm, D_CONV), lambda i: (i, 0)), pl.BlockSpec((CONV_WIDTH, D_CONV), lambda i: (0, 0)),
                   pl.BlockSpec((1, D_CONV), lambda i: (0, 0))],
        out_shape=[jax.ShapeDtypeStruct((t, D_CONV), F32), jax.ShapeDtypeStruct((CONV_WIDTH, D_CONV), F32),
                   jax.ShapeDtypeStruct((1, D_CONV), F32)],
        scratch_shapes=[pltpu.VMEM((tm + HALO, D_CONV), F32)],
        compiler_params=_cparams(("arbitrary",)),
    )(u, u, dxs_a, dxs_b, dbm, dcm, w, bias)

    def body2(dz_ref, n_ref, w_ref, du_ref, scr):
        last = pl.program_id(0) % per_seq == per_seq - 1
        scr[0:tm, :] = dz_ref[...]
        scr[tm:, :] = jnp.where(last, 0.0, n_ref[...])
        for c in range(D_CONV // LANES):
            cs = slice(c * LANES, (c + 1) * LANES)
            du_ref[:, cs] = _conv_taps(scr, w_ref, cs, CONV_WIDTH - 1, -1, tm).astype(du_ref.dtype)

    du = pl.pallas_call(
        body2, name="conv_bwd_du", grid=(n_tiles,),
        in_specs=[pl.BlockSpec((tm, D_CONV), lambda i: (i, 0)),
                  pl.BlockSpec((HALO, D_CONV), lambda i: (jnp.minimum((i + 1) * (tm // HALO), t // HALO - 1), 0)),
                  pl.BlockSpec((CONV_WIDTH, D_CONV), lambda i: (0, 0))],
        out_specs=pl.BlockSpec((tm, D_CONV), lambda i: (i, 0)),
        out_shape=jax.ShapeDtypeStruct((t, D_CONV), BF16),
        scratch_shapes=[pltpu.VMEM((tm + HALO, D_CONV), F32)],
        compiler_params=_cparams(("parallel",)),
    )(dz, dz, w)
    return du, dw, db


Q = SSD_CHUNK
N_PAIRS = D_SSD // LANES
HEADS_PER_GROUP = N_HEADS // SSD_GROUPS


def _rep(a, j):
    return jnp.broadcast_to(a[:, j:j + 1], a.shape)


def _dot_exact01(a, b, dn, a_is_01):
    x = b if a_is_01 else a
    hi = x.astype(BF16)
    mid = (x - hi.astype(F32)).astype(BF16)
    lo = (x - hi.astype(F32) - mid.astype(F32)).astype(BF16)
    z = a.astype(BF16) if a_is_01 else b.astype(BF16)
    out = None
    for term in (hi, mid, lo):
        d = _dot(z, term, dn) if a_is_01 else _dot(term, z, dn)
        out = d if out is None else out + d
    return out


def _pad_lanes(v, fill=0.0):
    row = jnp.pad(v.reshape(1, -1).astype(F32), ((0, 0), (0, LANES - v.size)), constant_values=fill)
    return row, row.reshape(LANES, 1)


def _ssd_common(dtr_ref, dtrt_ref, bias_r, bias_c, alog_r, alog_c):
    row = lax.broadcasted_iota(jnp.int32, (Q, Q), 0)
    col = lax.broadcasted_iota(jnp.int32, (Q, Q), 1)
    tril = row >= col
    lane = lax.broadcasted_iota(jnp.int32, (1, LANES), 1)
    a_r = jnp.where(lane < N_HEADS, -jnp.exp(alog_r[...]), 0.0)
    sub = lax.broadcasted_iota(jnp.int32, (LANES, 1), 0)
    a_c = jnp.where(sub < N_HEADS, -jnp.exp(alog_c[...]), 0.0)
    dt = _softplus(dtr_ref[...] + bias_r[...])
    cs = _dot_exact01(tril, dt * a_r, _NN, True)
    dtt = _softplus(dtrt_ref[...] + bias_c[...])
    cst = _dot_exact01(dtt * a_c, row <= col, _NN, False)
    return tril, lane, a_r, dt, cs, cst


def _ssd_specs(b, nc, rev):
    ci = (lambda c: nc - 1 - c) if rev else (lambda c: c)
    rows = lambda w: pl.BlockSpec((Q, w), lambda bb, c: (bb * nc + ci(c), 0))
    dtt = pl.BlockSpec((LANES, Q), lambda bb, c: (0, bb * nc + ci(c)))
    const = lambda s: pl.BlockSpec(s, lambda bb, c: (0,) * len(s))
    state = pl.BlockSpec((None, N_PAIRS, LANES, SSD_STATE), lambda bb, c: (bb * nc + ci(c), 0, 0, 0))
    return rows, dtt, const, state


def ssd_fwd(xs, bm, cm, dtraw, dt_bias, a_log, b):
    t = xs.shape[0]
    nc = SEQ // Q
    rows, dtt_spec, const, state = _ssd_specs(b, nc, False)
    bias_r, bias_c = _pad_lanes(dt_bias)
    alog_r, alog_c = _pad_lanes(a_log)

    def body(xs_ref, b_ref, c_ref, dtr_ref, dtrt_ref, br, bc, ar, ac, y_ref, hp_ref, h_scr):
        @pl.when(pl.program_id(1) == 0)
        def _():
            h_scr[...] = jnp.zeros(h_scr.shape, F32)
        tril, lane, _, dt, cs, cst = _ssd_common(dtr_ref, dtrt_ref, br, bc, ar, ac)
        sub = lax.broadcasted_iota(jnp.int32, (LANES, 1), 0)
        y_acc = [jnp.zeros((Q, LANES), F32) for _ in range(N_PAIRS)]
        h_old = [h_scr[p] for p in range(N_PAIRS)]
        h_new = [jnp.zeros((LANES, SSD_STATE), F32) for _ in range(N_PAIRS)]
        for g in range(SSD_GROUPS):
            bg = b_ref[:, g * SSD_STATE:(g + 1) * SSD_STATE].astype(BF16)
            cg = c_ref[:, g * SSD_STATE:(g + 1) * SSD_STATE].astype(BF16)
            cb = _dot(cg, bg, _NT)
            heads = []
            for j in range(g * HEADS_PER_GROUP, (g + 1) * HEADS_PER_GROUP):
                p, side = j // 2, j % 2
                m = (lane < HEAD_DIM) if side == 0 else (lane >= HEAD_DIM)
                ms = (sub < HEAD_DIM) if side == 0 else (sub >= HEAD_DIM)
                csj, dtj = _rep(cs, j), _rep(dt, j)
                lmat = jnp.exp(jnp.where(tril, csj - cst[j:j + 1, :], NEG))
                xdt = jnp.where(m, xs_ref[:, p * LANES:(p + 1) * LANES] * dtj, 0.0)
                hm = jnp.where(ms, h_old[p], 0.0)
                last = csj[Q - 1:Q, :]
                heads.append(dict(p=p, hm=hm, ecs=jnp.exp(csj), el=jnp.exp(last), gmat=(cb * lmat).astype(BF16),
                                  xdt=xdt.astype(BF16), xd=(xdt * jnp.exp(last - csj)).astype(BF16)))
            for h in heads:
                h.update(ydiag=_dot(h['gmat'], h['xdt'], _NN), ch=_dot(cg, h['hm'].astype(BF16), _NT), sj=_dot(h['xd'], bg, _TN))
            for h in heads:
                y_acc[h['p']] = y_acc[h['p']] + h['ydiag'] + h['ecs'] * h['ch']
                h_new[h['p']] = h_new[h['p']] + h['el'] * h['hm'] + h['sj']
        for p in range(N_PAIRS):
            y_ref[:, p * LANES:(p + 1) * LANES] = y_acc[p]
            hp_ref[p] = h_old[p]
            h_scr[p] = h_new[p]

    return pl.pallas_call(
        body, name="ssd_fwd", grid=(b, nc),
        in_specs=[rows(D_SSD), rows(D_BC), rows(D_BC), rows(LANES), dtt_spec, const((1, LANES)), const((LANES, 1)),
                  const((1, LANES)), const((LANES, 1))],
        out_specs=[rows(D_SSD), state],
        out_shape=[jax.ShapeDtypeStruct((t, D_SSD), F32),
                   jax.ShapeDtypeStruct((b * nc, N_PAIRS, LANES, SSD_STATE), F32)],
        scratch_shapes=[pltpu.VMEM((N_PAIRS, LANES, SSD_STATE), F32)],
        compiler_params=_cparams(("parallel", "arbitrary")),
    )(xs, bm, cm, dtraw, dtraw.T, bias_r, bias_c, alog_r, alog_c)


def ssd_bwd(xs, bm, cm, dtraw, dt_bias, a_log, hprev, dy, b):
    t = xs.shape[0]
    nc = SEQ // Q
    rows, dtt_spec, const, state = _ssd_specs(b, nc, True)
    bias_r, bias_c = _pad_lanes(dt_bias)
    alog_r, alog_c = _pad_lanes(a_log)

    def body(xs_ref, b_ref, c_ref, dtr_ref, dtrt_ref, hp_ref, dy_ref, br, bc, ar, ac,
             dxs_ref, db_ref, dc_ref, ddt_ref, dbias_ref, dalog_ref, dh_scr):
        first = jnp.logical_and(pl.program_id(0) == 0, pl.program_id(1) == 0)

        @pl.when(pl.program_id(1) == 0)
        def _():
            dh_scr[...] = jnp.zeros(dh_scr.shape, F32)

        @pl.when(first)
        def _():
            dbias_ref[...] = jnp.zeros(dbias_ref.shape, F32)
            dalog_ref[...] = jnp.zeros(dalog_ref.shape, F32)
        tril, lane, a_r, dt, cs, cst = _ssd_common(dtr_ref, dtrt_ref, br, bc, ar, ac)
        sub = lax.broadcasted_iota(jnp.int32, (LANES, 1), 0)
        rowq = lax.broadcasted_iota(jnp.int32, (Q, 1), 0)
        triu = (lax.broadcasted_iota(jnp.int32, (Q, Q), 0) <= lax.broadcasted_iota(jnp.int32, (Q, Q), 1)).astype(F32)
        dxs_acc = [jnp.zeros((Q, LANES), F32) for _ in range(N_PAIRS)]
        dh_in = [dh_scr[p] for p in range(N_PAIRS)]
        h_in = [hp_ref[p] for p in range(N_PAIRS)]
        dh_out = [jnp.zeros((LANES, SSD_STATE), F32) for _ in range(N_PAIRS)]
        ddt = jnp.zeros((Q, LANES), F32)
        dalog = jnp.zeros((1, LANES), F32)
        for g in range(SSD_GROUPS):
            gs = slice(g * SSD_STATE, (g + 1) * SSD_STATE)
            bg, cg = b_ref[:, gs].astype(BF16), c_ref[:, gs].astype(BF16)
            cb = _dot(cg, bg, _NT)
            dcb = jnp.zeros((Q, Q), F32)
            dbg = jnp.zeros((Q, SSD_STATE), F32)
            dcg = jnp.zeros((Q, SSD_STATE), F32)
            heads = []
            for j in range(g * HEADS_PER_GROUP, (g + 1) * HEADS_PER_GROUP):
                p, side = j // 2, j % 2
                m = (lane < HEAD_DIM) if side == 0 else (lane >= HEAD_DIM)
                ms = (sub < HEAD_DIM) if side == 0 else (sub >= HEAD_DIM)
                csj, dtj = _rep(cs, j), _rep(dt, j)
                lmat = jnp.exp(jnp.where(tril, csj - cst[j:j + 1, :], NEG))
                x2 = jnp.where(m, xs_ref[:, p * LANES:(p + 1) * LANES], 0.0)
                xdt = x2 * dtj
                dym = jnp.where(m, dy_ref[:, p * LANES:(p + 1) * LANES], 0.0)
                hm = jnp.where(ms, h_in[p], 0.0)
                dhm = jnp.where(ms, dh_in[p], 0.0)
                last = csj[Q - 1:Q, :]
                decay = jnp.exp(last - csj)
                heads.append(dict(j=j, p=p, dtj=dtj, lmat=lmat, x2=x2, hm=hm, dhm=dhm, decay=decay, el=jnp.exp(last),
                                  gmat=cb * lmat, dym=dym.astype(BF16), xdt=xdt.astype(BF16), hmb=hm.astype(BF16),
                                  dhmb=dhm.astype(BF16), dye=dym * jnp.exp(csj), xd=xdt * decay))
            for h in heads:
                dyeb, xdb = h['dye'].astype(BF16), h['xd'].astype(BF16)
                h.update(dg=_dot(h['dym'], h['xdt'], _NT),
                         dxdt=_dot(h['gmat'].astype(BF16), h['dym'], _TN),
                         ch=_dot(cg, h['hmb'], _NT),
                         dcg=_dot(dyeb, h['hmb'], _NN), dhp=_dot(dyeb, cg, _TN),
                         wmat=_dot(bg, h['dhmb'], _NT),
                         dbg=_dot(xdb, h['dhmb'], _NN))
            for h in heads:
                ej = h['dg'] * h['gmat']
                col_sums = jnp.broadcast_to(jnp.sum(ej, axis=0, keepdims=True), (Q, Q)).T
                dl = h['xd'] * h['wmat']
                total = lambda v: jnp.sum(jnp.sum(v, axis=0, keepdims=True), axis=1, keepdims=True)
                dlast = total(dl) + h['el'] * total(h['dhm'] * h['hm'])
                h['dcs'] = (jnp.sum(ej + h['dye'] * h['ch'] - dl, axis=1, keepdims=True) - col_sums
                            + jnp.where(rowq == Q - 1, dlast, 0.0))
                h['dxdt'] = h['dxdt'] + h['decay'] * h['wmat']
                dcb, dcg, dbg = dcb + h['dg'] * h['lmat'], dcg + h['dcg'], dbg + h['dbg']
                dh_out[h['p']] = dh_out[h['p']] + h['el'] * h['dhm'] + h['dhp']
            for h in heads:
                h['da'] = _dot_exact01(triu, h['dcs'], _NN, True)
            for h in heads:
                j, da = h['j'], h['da']
                aj = jnp.sum(jnp.where(lane == j, a_r, 0.0), axis=1, keepdims=True)
                ddtj = da * aj + jnp.sum(h['dxdt'] * h['x2'], axis=1, keepdims=True)
                ddt = ddt + jnp.where(lane == j, ddtj, 0.0)
                dalog = dalog + jnp.where(lane == j, jnp.sum(da * h['dtj'], axis=0, keepdims=True) * aj, 0.0)
                dxs_acc[h['p']] = dxs_acc[h['p']] + h['dxdt'] * h['dtj']
            dcbb = dcb.astype(BF16)
            dc_ref[:, gs] = dcg + _dot(dcbb, bg, _NN)
            db_ref[:, gs] = dbg + _dot(dcbb, cg, _TN)
        for p in range(N_PAIRS):
            dxs_ref[:, p * LANES:(p + 1) * LANES] = dxs_acc[p]
            dh_scr[p] = dh_out[p]
        ddtraw = ddt * _sigmoid(dtr_ref[...] + br[...])
        ddt_ref[...] = ddtraw
        dbias_ref[...] += jnp.sum(ddtraw, axis=0, keepdims=True)
        dalog_ref[...] += dalog

    return pl.pallas_call(
        body, name="ssd_bwd", grid=(b, nc),
        in_specs=[rows(D_SSD), rows(D_BC), rows(D_BC), rows(LANES), dtt_spec, state, rows(D_SSD), const((1, LANES)),
                  const((LANES, 1)), const((1, LANES)), const((LANES, 1))],
        out_specs=[rows(D_SSD), rows(D_BC), rows(D_BC), rows(LANES), const((1, LANES)), const((1, LANES))],
        out_shape=[jax.ShapeDtypeStruct((t, D_SSD), F32), jax.ShapeDtypeStruct((t, D_BC), F32),
                   jax.ShapeDtypeStruct((t, D_BC), F32), jax.ShapeDtypeStruct((t, LANES), F32),
                   jax.ShapeDtypeStruct((1, LANES), F32), jax.ShapeDtypeStruct((1, LANES), F32)],
        scratch_shapes=[pltpu.VMEM((N_PAIRS, LANES, SSD_STATE), F32)],
        compiler_params=_cparams(("arbitrary", "arbitrary")),
    )(xs, bm, cm, dtraw, dtraw.T, hprev, dy, bias_r, bias_c, alog_r, alog_c)


def _split_w_in(w_in):
    w_dt = jnp.pad(w_in[:, D_QKVZ + D_CONV:], ((0, 0), (0, LANES - N_HEADS)))
    return w_in[:, :D_QKVZ], w_in[:, D_QKVZ:D_QKVZ + D_CONV], w_dt


def mixer_fwd(hb, p, cosv, sinv, b):
    t = hb.shape[0]
    w_a, w_b, w_c = _split_w_in(p['w_in'])
    qkvz = mm("in_qkvz", [(hb, w_a, 'nn')], D_QKVZ)
    xbc = mm("in_xbc", [(hb, w_b, 'nn')], D_CONV)
    dtraw = mm("in_dt", [(hb, w_c, 'nn')], LANES)
    mixed, *lses = attn_fwd(qkvz, cosv, sinv, b)
    attn = attn_norm_fwd(mixed, p['attn_norm_w'])
    xs, bm, cm = conv_fwd(xbc, p['conv_w'], p['conv_b'])
    y, hprev = ssd_fwd(xs, bm, cm, dtraw, p['dt_bias'], p['a_log'], b)
    dskip = jnp.repeat(p['d_skip'].reshape(-1), HEAD_DIM).reshape(1, D_SSD)
    yg, = rowwise("ssd_gate", _gate, [y, xs, Op(qkvz, D_SSD, 3)], [dskip, p['ssd_norm_w']], [(t, D_SSD, BF16)])
    mix = mm("out_proj", [(attn, p['w_out'][:D_ATTN], 'nn'), (yg, p['w_out'][D_ATTN:], 'nn')], D_MODEL, out_dtype=BF16)
    res = dict(hb=hb, qkvz=qkvz, xbc=xbc, dtraw=dtraw, mixed=mixed, lses=lses, attn=attn, xs=xs, bm=bm, cm=cm,
               y=y, hprev=hprev, dskip=dskip, yg=yg, cosv=cosv, sinv=sinv)
    return mix, res


def mixer_bwd(r, p, dmix, dh_resid, b):
    t = dmix.shape[0]
    w_a, w_b, w_c = _split_w_in(p['w_in'])
    w_out = p['w_out']
    dattn = mm("out_bwd_dattn", [(dmix, w_out[:D_ATTN], 'nt')], D_ATTN)
    dyg = mm("out_bwd_dyg", [(dmix, w_out[D_ATTN:], 'nt')], D_SSD)
    dw_out = jnp.concatenate([mm_tn("out_bwd_dw_a", r['attn'], dmix, BF16),
                              mm_tn("out_bwd_dw_y", r['yg'], dmix, BF16)], axis=0)

    def gate_bwd(dy_, y_, xs_, z_, ds_, w_):
        _, vjp = jax.vjp(_gate, y_, xs_, z_, ds_, w_)
        return vjp(dy_)

    dy, dxs_a, dz, ddskip, dssd_norm = rowwise(
        "ssd_gate_bwd", gate_bwd, [dyg, r['y'], r['xs'], Op(r['qkvz'], D_SSD, 3)], [r['dskip'], p['ssd_norm_w']],
        [(t, D_SSD, F32), (t, D_SSD, F32), (t, D_SSD, BF16)], accs=[(1, D_SSD), (1, D_SSD)])
    dxs_b, dbm, dcm, ddtraw, ddt_bias, da_log = ssd_bwd(r['xs'], r['bm'], r['cm'], r['dtraw'], p['dt_bias'], p['a_log'],
                                                        r['hprev'], dy, b)
    dxbc, dconv_w, dconv_b = conv_bwd(r['xbc'], p['conv_w'], p['conv_b'], dxs_a, dxs_b, dbm, dcm)
    dmixed, dattn_norm = attn_norm_bwd(dattn, r['mixed'], p['attn_norm_w'])
    dq, dk, dv = attn_bwd(r['qkvz'], r['cosv'], r['sinv'], dmixed, r['mixed'], r['lses'], b)
    wq, wk, wv, wz = (w_a[:, i * D_ATTN:(i + 1) * D_ATTN] for i in range(4))
    dh = mm("in_bwd_dh", [(dq, wq, 'nt'), (dk, wk, 'nt'), (dv, wv, 'nt'), (dz, wz, 'nt'), (dxbc, w_b, 'nt'),
                          (ddtraw, w_c, 'nt')], D_MODEL, add=dh_resid, tn=512)
    h = r['hb']
    dw_in = jnp.concatenate([mm_tn_cat("in_bwd_dw_qkvz", h, [dq, dk, dv, dz], BF16),
                             mm_tn_cat("in_bwd_dw_xbc_dt", h, [dxbc, ddtraw], BF16)[:, :D_CONV + N_HEADS]], axis=1)
    head_sum = lambda v: v.reshape(N_HEADS, HEAD_DIM).sum(axis=1).reshape(1, N_HEADS)
    grads = dict(w_in=dw_in, w_out=dw_out, conv_w=dconv_w, conv_b=dconv_b, dt_bias=ddt_bias[:, :N_HEADS],
                 a_log=da_log[:, :N_HEADS], d_skip=head_sum(ddskip), attn_norm_w=dattn_norm, ssd_norm_w=dssd_norm)
    return dh, grads


FFN2_KEYS = ('ffn2_gate', 'ffn2_up', 'ffn2_down')
MIXER_KEYS = ('w_in', 'conv_w', 'w_out')
FFN_COL = ('ffn1_gate', 'ffn1_up', 'ffn2_gate', 'ffn2_up')
FFN_ROW = ('ffn1_down', 'ffn2_down')
CONV_W_COMM = (8, 2 * LANES)
SMALL = 'small'


def comm_shape(k, shapes):
    if k in FFN_COL:
        return (D_MODEL, FF_PAD)
    if k in FFN_ROW:
        return (FF_PAD, D_MODEL)
    if k == 'conv_w':
        return CONV_W_COMM
    return tuple(shapes[k][1:])


def to_comm(k, vals, shapes):
    a = vals[k].reshape(shapes[k][1:])
    r_, c_ = comm_shape(k, shapes)
    return jnp.pad(a, ((0, r_ - a.shape[0]), (0, c_ - a.shape[1])))


SMALL_ROWS, SMALL_COLS = 16, D_CONV


def pack_small(small):
    rows = [jnp.pad(small[r].reshape(1, -1), ((0, 0), (0, SMALL_COLS - small[r].size))) for r in REPLICATED]
    return jnp.concatenate(rows + [jnp.zeros((SMALL_ROWS - len(rows), SMALL_COLS), F32)], axis=0)


def full_weight(k, g):
    if k in FFN_COL:
        return g
    if k == 'conv_w':
        return jnp.transpose(g[:, :CONV_WIDTH, :D_CONV // N_DEV], (1, 0, 2)).reshape(CONV_WIDTH, D_CONV)
    return g.reshape(N_DEV * g.shape[1], g.shape[2])


def grad_shards(k, g):
    if k in FFN_COL:
        return g
    if k == 'conv_w':
        s = jnp.transpose(g.reshape(CONV_WIDTH, N_DEV, D_CONV // N_DEV), (1, 0, 2))
        return jnp.pad(s, ((0, 0), (0, CONV_W_COMM[0] - CONV_WIDTH), (0, CONV_W_COMM[1] - D_CONV // N_DEV)))
    return g.reshape(N_DEV, g.shape[0] // N_DEV, g.shape[1])


def _flip(v, bit):
    return 1 - v if bit else v


N_PEER_COPIES = N_DEV - 1


def _comm_call(name, body, arrs, out_shape):
    n = len(arrs)
    return pl.pallas_call(
        functools.partial(body, n), name=name, out_shape=out_shape,
        in_specs=[pl.BlockSpec(memory_space=pl.ANY)] * n, out_specs=[pl.BlockSpec(memory_space=pl.ANY)] * n,
        scratch_shapes=[pltpu.SemaphoreType.DMA((n * N_PEER_COPIES,)), pltpu.SemaphoreType.DMA((n * N_PEER_COPIES,)),
                        pltpu.SemaphoreType.DMA((n,))],
    )(*arrs)


def _blk(ref, idx, by_cols):
    if not by_cols:
        return ref.at[idx]
    c = ref.shape[1] // N_DEV
    return ref.at[:, pl.ds(pl.multiple_of(idx * c, LANES), c)]


def _blocked_shape(a, by_cols):
    return (a.shape[0], N_DEV * a.shape[1]) if by_cols else (N_DEV,) + a.shape


def all_gather(arrs, by_cols):
    def body(n, *refs):
        x_refs, out_refs, (send_sems, recv_sems, local_sems) = refs[:n], refs[n:2 * n], refs[2 * n:]
        x, y, c = lax.axis_index("x"), lax.axis_index("y"), lax.axis_index("c")
        me, sibling = (x, y, c), (x, y, 1 - c)
        chips = [(1 - x, y), (x, 1 - y), (1 - x, 1 - y)]

        def copy(a, k, block, to, src=None):
            px, py, pc = block
            dst = _blk(out_refs[a], 4 * px + 2 * py + pc, by_cols[a])
            return pltpu.make_async_remote_copy(
                src_ref=dst if src is None else src, dst_ref=dst, send_sem=send_sems.at[a * N_PEER_COPIES + k],
                recv_sem=recv_sems.at[a * N_PEER_COPIES + k], device_id=to, device_id_type=MESH)

        mine = [pltpu.make_async_copy(x_refs[a], _blk(out_refs[a], 4 * x + 2 * y + c, by_cols[a]), local_sems.at[a])
                for a in range(n)]
        started = []
        for a in range(n):
            mine[a].start()
            first = [copy(a, 0, me, sibling, src=x_refs[a])]
            first += [copy(a, 1 + j, me, (*chip, c), src=x_refs[a]) for j, chip in enumerate(chips)]
            for cp in first:
                cp.start()
            started += first
        for j, chip in enumerate(chips):
            for a in range(n):
                copy(a, 1 + j, (*chip, c), me).wait_recv()
                cp = copy(a, 4 + j, (*chip, c), sibling)
                cp.start()
                started.append(cp)
        for a in range(n):
            copy(a, 0, sibling, me).wait_recv()
            for j, chip in enumerate(chips):
                copy(a, 4 + j, (*chip, 1 - c), me).wait_recv()
        for cp in started:
            cp.wait_send()
        for cp in mine:
            cp.wait()

    return _comm_call("all_gather_weights", body, arrs,
                      [jax.ShapeDtypeStruct(_blocked_shape(a, bc), a.dtype) for a, bc in zip(arrs, by_cols)])


def blocks_to_cols(arrs):
    def body(*refs):
        for i, o in zip(refs[:len(arrs)], refs[len(arrs):]):
            o[...] = i[...]

    return pl.pallas_call(
        body, name="blocks_to_cols", grid=(N_DEV,),
        in_specs=[pl.BlockSpec((None,) + a.shape[1:], lambda p: (p, 0, 0)) for a in arrs],
        out_specs=[pl.BlockSpec(a.shape[1:], lambda p: (0, p)) for a in arrs],
        out_shape=[jax.ShapeDtypeStruct((a.shape[1], N_DEV * a.shape[2]), a.dtype) for a in arrs],
        compiler_params=_cparams(("parallel",)),
    )(*arrs)


def _landing_shape(a, by_cols):
    return (N_DEV, a.shape[0], a.shape[1] // N_DEV) if by_cols else a.shape


_HBM = pl.BlockSpec(memory_space=pltpu.HBM)
_SEM = pl.BlockSpec(memory_space=pltpu.SEMAPHORE)
_EFFECT = pltpu.SideEffectType.DATAFLOW_SIDE_EFFECTING


def _peer(k):
    x, y, c = lax.axis_index("x"), lax.axis_index("y"), lax.axis_index("c")
    return _flip(x, k & 4), _flip(y, k & 2), _flip(c, k & 1)


def _my_index():
    return 4 * lax.axis_index("x") + 2 * lax.axis_index("y") + lax.axis_index("c")


def _split_copies(mode, by_cols, src_refs, land_refs, send_sems, recv_sems):
    me = _my_index()
    out = []
    for a, bc in enumerate(by_cols):
        for k in range(1, N_DEV):
            px, py, pc = _peer(k)
            src = _blk(src_refs[a], 4 * px + 2 * py + pc, bc) if mode == 'scatter' else src_refs[a]
            dst = land_refs[a].at[me] if mode == 'scatter' else _blk(land_refs[a], me, bc)
            out.append(pltpu.make_async_remote_copy(
                src_ref=src, dst_ref=dst, send_sem=send_sems.at[a * N_PEER_COPIES + k - 1],
                recv_sem=recv_sems.at[a * N_PEER_COPIES + k - 1], device_id=(px, py, pc), device_id_type=MESH))
    return out


def exchange_start(name, mode, srcs, by_cols):
    n = len(srcs)
    lands = [lax.empty(_landing_shape(s, bc) if mode == 'scatter' else _blocked_shape(s, bc), s.dtype)
             for s, bc in zip(srcs, by_cols)]

    def body(*refs):
        src_refs, land_refs, send_sems, recv_sems = refs[:n], refs[n:2 * n], refs[2 * n], refs[2 * n + 1]
        for cp in _split_copies(mode, by_cols, src_refs, land_refs, send_sems, recv_sems):
            cp.start()
        refs[-1][...] = jnp.zeros(refs[-1].shape, F32)

    sems = pltpu.SemaphoreType.DMA((n * N_PEER_COPIES,))
    res = pl.pallas_call(
        body, name=name,
        out_shape=(sems, sems, *[pltpu.HBM(a.shape, a.dtype) for a in srcs + lands], jax.ShapeDtypeStruct((8, LANES), F32)),
        in_specs=(_HBM,) * (2 * n), out_specs=(_SEM, _SEM, *(_HBM,) * (2 * n), pl.BlockSpec(memory_space=pltpu.VMEM)),
        input_output_aliases={i: 2 + i for i in range(2 * n)},
        compiler_params=pltpu.CompilerParams(has_side_effects=_EFFECT),
    )(*[pltpu.with_memory_space_constraint(a, pltpu.HBM) for a in srcs + lands])
    return (mode, by_cols, res[:-1]), res[-1]


def exchange_wait(name, handles, after):
    mode, by_cols, (send_sems, recv_sems, *bufs) = handles
    n = len(by_cols)

    def body(*refs):
        src_refs, land_refs, s_sems, r_sems = refs[:n], refs[n:2 * n], refs[2 * n], refs[2 * n + 1]
        for cp in _split_copies(mode, by_cols, src_refs, land_refs, s_sems, r_sems):
            cp.wait_send()
            cp.wait_recv()

    res = pl.pallas_call(
        body, name=name, out_shape=tuple(pltpu.HBM(a.shape, a.dtype) for a in bufs),
        in_specs=(*(_HBM,) * (2 * n), _SEM, _SEM, pl.BlockSpec(memory_space=pl.ANY)), out_specs=(_HBM,) * (2 * n),
        input_output_aliases={i: i for i in range(2 * n)},
        compiler_params=pltpu.CompilerParams(has_side_effects=_EFFECT),
    )(*bufs, send_sems, recv_sems, after)
    me, out = _my_index(), []
    for src, land, bc in zip(res[:n], res[n:], by_cols):
        if mode == 'scatter':
            c = land.shape[2]
            own = lax.dynamic_slice(src, (0, me * c), (src.shape[0], c)) if bc else lax.dynamic_index_in_dim(src, me, 0, False)
            out.append(lax.dynamic_update_slice(land, own[None], (me, 0, 0)))
        elif bc:
            out.append(lax.dynamic_update_slice(land, src, (0, me * src.shape[1])))
        else:
            out.append(lax.dynamic_update_slice(land, src[None], (me, 0, 0)))
    return out


def _adamw_math(g, w, m, v):
    c1 = 1.0 / (1.0 - ADAM_B1 ** ADAM_STEP)
    c2 = 1.0 / (1.0 - ADAM_B2 ** ADAM_STEP)
    m = ADAM_B1 * m + (1.0 - ADAM_B1) * g
    v = ADAM_B2 * v + (1.0 - ADAM_B2) * jnp.square(g)
    return g, -ADAM_LR * ((m * c1) / (jnp.sqrt(v * c2) + ADAM_EPS) + ADAM_WD * w), m, v


def adamw(name, recv, w, m, v, tm):
    _, rows, cols = w.shape
    tm = min(tm, rows)

    def body(*refs):
        g = refs[0][0:tm, 0:cols].astype(F32)
        for s in range(1, N_DEV):
            g = g + refs[s][0:tm, 0:cols].astype(F32)
        res = _adamw_math(g, *[r[...] for r in refs[N_DEV:N_DEV + 3]])
        for r, val in zip(refs[N_DEV + 3:], res):
            r[...] = val

    part = lambda s: pl.BlockSpec((None, recv.shape[1] if tm == rows else tm, recv.shape[2]), lambda i: (s, i, 0))
    tile = pl.BlockSpec((None, tm, cols), lambda i: (0, i, 0))
    return pl.pallas_call(
        body, name=name, grid=(rows // tm,), in_specs=[part(s) for s in range(N_DEV)] + [tile] * 3, out_specs=[tile] * 4,
        out_shape=[jax.ShapeDtypeStruct((1, rows, cols), F32)] * 4, compiler_params=_cparams(("parallel",)),
    )(*[recv] * N_DEV, w, m, v)


def adamw_small(recv, wl, ml, vl):
    n = len(REPLICATED)

    def body(recv_ref, *refs):
        g = recv_ref[0]
        for s in range(1, N_DEV):
            g = g + recv_ref[s]
        for r in range(n):
            w, m, v = (refs[j * n + r][...] for j in range(3))
            for j, val in enumerate(_adamw_math(g[r:r + 1, :w.shape[1]], w, m, v)):
                refs[(3 + j) * n + r][...] = val

    arrs = [d[k].reshape(1, -1) for d in (wl, ml, vl) for k in REPLICATED]
    res = pl.pallas_call(
        body, name="adamw_small", out_shape=[jax.ShapeDtypeStruct(a.shape, F32) for a in arrs[:n]] * 4,
    )(recv, *arrs)
    return [{k: res[j * n + r].reshape(wl[k].shape) for r, k in enumerate(REPLICATED)} for j in range(4)]


ADAMW_TM = {'ffn1_gate': 256, 'ffn1_up': 256, 'ffn2_gate': 256, 'ffn2_up': 256, 'w_in': 32}


def kernel(x, positions, ln1_g, ln1_b, ffn1_gate, ffn1_up, ffn1_down, w_in, conv_w, conv_b, dt_bias, a_log, d_skip, attn_norm_w, ssd_norm_w, w_out, ln2_g, ln2_b, ffn2_gate, ffn2_up, ffn2_down, ln3_g, ln3_b, loss_target, m_ln1_g, m_ln1_b, m_ffn1_gate, m_ffn1_up, m_ffn1_down, m_w_in, m_conv_w, m_conv_b, m_dt_bias, m_a_log, m_d_skip, m_attn_norm_w, m_ssd_norm_w, m_w_out, m_ln2_g, m_ln2_b, m_ffn2_gate, m_ffn2_up, m_ffn2_down, m_ln3_g, m_ln3_b, v_ln1_g, v_ln1_b, v_ffn1_gate, v_ffn1_up, v_ffn1_down, v_w_in, v_conv_w, v_conv_b, v_dt_bias, v_a_log, v_d_skip, v_attn_norm_w, v_ssd_norm_w, v_w_out, v_ln2_g, v_ln2_b, v_ffn2_gate, v_ffn2_up, v_ffn2_down, v_ln3_g, v_ln3_b):
    args = dict(locals())
    wl = {k: args[k] for k in WEIGHTS}
    ml = {k: args["m_" + k] for k in WEIGHTS}
    vl = {k: args["v_" + k] for k in WEIGHTS}
    shapes = {k: wl[k].shape for k in WEIGHTS}
    b, s, dm = x.shape
    t = b * s

    sent = {k: to_comm(k, wl, shapes).astype(F32 if k == 'conv_w' else BF16) for k in SHARDED}
    by_cols = lambda keys: [k in FFN_COL for k in keys]
    gate, up = all_gather([sent['ffn1_gate'], sent['ffn1_up']], [False] * 2)
    (gate, up), sent = lax.optimization_barrier(((gate, up), sent))
    p = dict(zip(('ffn1_gate', 'ffn1_up'), blocks_to_cols([gate, up])))
    gather_down, token_d = exchange_start("gather_ffn1_down_start", 'gather', [sent['ffn1_down']], [False])
    sent['w_in'] = sent['w_in'] + token_d[0, 0].astype(BF16)
    gather_mixer, token_m = exchange_start("gather_mixer_start", 'gather', [sent[k] for k in MIXER_KEYS], by_cols(MIXER_KEYS))
    sent['ffn2_gate'] = sent['ffn2_gate'] + token_m[0, 0].astype(BF16)
    gather_ffn2, token_f = exchange_start("gather_ffn2_start", 'gather', [sent[k] for k in FFN2_KEYS], by_cols(FFN2_KEYS))
    for k in REPLICATED:
        p[k] = wl[k].reshape(1, -1)

    x2 = x.reshape(t, dm)
    cosv, sinv = rope_tables(positions)
    g1, u1, a1, at1 = ffn_gate_up("ffn1_gate_up", x2, p['ffn1_gate'], p['ffn1_up'], after=(token_d, token_m, token_f))
    p['ffn1_down'] = full_weight('ffn1_down', exchange_wait("gather_ffn1_down_wait", gather_down, a1)[0])
    f1, res1 = mm("ffn1_down", [(a1, p['ffn1_down'], 'nn')], D_MODEL, out_dtype=BF16), (x2, g1, u1, at1)
    h1, h1b = resid_ln_fwd("ln1", 0.5, x2, f1, p['ln1_g'], p['ln1_b'])
    for k, g in zip(MIXER_KEYS, exchange_wait("gather_mixer_wait", gather_mixer, h1b)):
        p[k] = full_weight(k, g)
    mix, resm = mixer_fwd(h1b, p, cosv, sinv, b)
    h2, h2b = resid_ln_fwd("ln2", 1.0, h1, mix, p['ln2_g'], p['ln2_b'])
    for k, g in zip(FFN2_KEYS, exchange_wait("gather_ffn2_wait", gather_ffn2, h2b)):
        p[k] = full_weight(k, g)
    f2, res3 = ffn_fwd("ffn2", h2b, p['ffn2_gate'], p['ffn2_up'], p['ffn2_down'])

    small, full = {}, {}
    dh2_res, df2, small['ln3_g'], small['ln3_b'], sq = ln_loss_bwd("ln3_loss_bwd", h2, f2, loss_target.reshape(t, dm),
                                                                   p['ln3_g'], p['ln3_b'])
    loss = lax.psum(jnp.sum(sq) * (0.5 / dm), AXES)

    dh2, full['ffn2_gate'], full['ffn2_up'], full['ffn2_down'] = ffn_bwd("ffn2", res3, p['ffn2_gate'], p['ffn2_up'],
                                                                       p['ffn2_down'], df2, dh2_res)
    ffn2_exchange, token = exchange_start("grads_ffn2_start", 'scatter', [grad_shards(k, full[k]) for k in FFN2_KEYS],
                                          by_cols(FFN2_KEYS))
    dh1_res, dmix, small['ln2_g'], small['ln2_b'] = resid_ln_bwd("ln2_bwd", 1.0, h1, mix, p['ln2_g'] + token[:1, :1],
                                                                 p['ln2_b'], dh2)
    dh1, gm = mixer_bwd(resm, p, dmix, dh1_res, b)
    for k in ('conv_b', 'dt_bias', 'a_log', 'd_skip', 'attn_norm_w', 'ssd_norm_w'):
        small[k] = gm[k]
    mixer_exchange, token = exchange_start("grads_mixer_start", 'scatter', [grad_shards(k, gm[k]) for k in MIXER_KEYS],
                                           by_cols(MIXER_KEYS))
    dx_res, df1, small['ln1_g'], small['ln1_b'] = resid_ln_bwd("ln1_bwd", 0.5, x2, f1, p['ln1_g'] + token[:1, :1],
                                                               p['ln1_b'], dh1)
    hb, g, u, at = res1
    small_part = pack_small(small)
    dg, du = ffn_da_act("ffn1_bwd_da_act", df1, p['ffn1_down'], g, u)
    dwd = mm_acc("ffn1_bwd_dwd", at, df1, BF16, after=dg)
    down_exchange, token = exchange_start("grads_ffn1_down_start", 'scatter', [
        grad_shards('ffn1_down', dwd), jnp.broadcast_to(small_part[None], (N_DEV,) + small_part.shape)], [False, False])
    dwg = mm_tn("ffn1_bwd_dwg", hb, dg, BF16, after=token)
    gate_exchange, token = exchange_start("grads_ffn1_gate_start", 'scatter', [grad_shards('ffn1_gate', dwg)], [True])
    dwu = mm_tn("ffn1_bwd_dwu", hb, du, BF16, after=token)
    up_exchange, token = exchange_start("grads_ffn1_up_start", 'scatter', [grad_shards('ffn1_up', dwu)], [True])
    dx = mm("ffn1_bwd_dh", [(dg, p['ffn1_gate'], 'nt'), (du, p['ffn1_up'], 'nt')], D_MODEL, add=dx_res, tn=512, after=token)
    recv = {}
    for keys, name, ex in (((FFN2_KEYS), "grads_ffn2_wait", ffn2_exchange), (MIXER_KEYS, "grads_mixer_wait", mixer_exchange),
                           (('ffn1_down', SMALL), "grads_ffn1_down_wait", down_exchange),
                           (('ffn1_gate',), "grads_ffn1_gate_wait", gate_exchange),
                           (('ffn1_up',), "grads_ffn1_up_wait", up_exchange)):
        recv.update(zip(keys, exchange_wait(name, ex, dx)))
    outs = adamw_small(recv.pop(SMALL), wl, ml, vl)
    for k, r in recv.items():
        for o, a in zip(outs, adamw(f"adamw_{k}", r, wl[k], ml[k], vl[k], ADAMW_TM.get(k, shapes[k][1]))):
            o[k] = a
    return (loss, dx.reshape(b, s, dm), *[o[k] for o in outs for k in WEIGHTS])
```

```python
import functools

import jax
import jax.numpy as jnp
import numpy as np
from jax import lax
from jax.experimental import pallas as pl
from jax.experimental.pallas import tpu as pltpu

F32, BF16 = jnp.float32, jnp.bfloat16
HI = lax.Precision.HIGHEST
MESH = pl.DeviceIdType.MESH
AXES = ("x", "y", "c")
N_DEV = 8

D_MODEL = 1024
SEQ = 2048
HEAD_DIM = 64
N_HEADS = 12
D_ATTN = N_HEADS * HEAD_DIM
DILATIONS = (1, 4, 16)
ATTN_BLOCK = 128
ROPE_THETA = 500000.0
ROPE_DIM = 16
D_SSD = 768
SSD_GROUPS = 4
SSD_STATE = 128
SSD_CHUNK = 128
D_BC = SSD_GROUPS * SSD_STATE
D_CONV = D_SSD + 2 * D_BC
CONV_WIDTH = 4
D_QKVZ = 3 * D_ATTN + D_SSD
D_FF = 2816
ALPHA = 2.0 ** 0.25
LN_EPS = 1e-5
RMS_EPS = 1e-6
ADAM_LR, ADAM_B1, ADAM_B2, ADAM_EPS, ADAM_WD, ADAM_STEP = 0.001, 0.9, 0.999, 1e-08, 0.01, 10

LANES = 128
VMEM_LIMIT = 52 * 1024 * 1024
NEG = -1e30

WEIGHTS = ['ln1_g', 'ln1_b', 'ffn1_gate', 'ffn1_up', 'ffn1_down', 'w_in', 'conv_w', 'conv_b', 'dt_bias', 'a_log',
           'd_skip', 'attn_norm_w', 'ssd_norm_w', 'w_out', 'ln2_g', 'ln2_b', 'ffn2_gate', 'ffn2_up', 'ffn2_down',
           'ln3_g', 'ln3_b']
COL_SHARDED = ('ffn1_gate', 'ffn1_up', 'conv_w', 'ffn2_gate', 'ffn2_up')
ROW_SHARDED = ('ffn1_down', 'w_in', 'w_out', 'ffn2_down')
SHARDED = tuple(n for n in WEIGHTS if n in COL_SHARDED or n in ROW_SHARDED)
REPLICATED = tuple(n for n in WEIGHTS if n not in SHARDED)
FF_SHARD = D_FF // N_DEV
FF_PAD = -(-FF_SHARD // LANES) * LANES


def _cparams(sem=None):
    return pltpu.CompilerParams(dimension_semantics=sem, vmem_limit_bytes=VMEM_LIMIT)


def _tile(n, prefs):
    for p in prefs:
        if n % p == 0:
            return p
    return n


class Op:
    def __init__(self, arr, bw=None, cb=0, ro=0):
        self.arr, self.bw, self.cb, self.ro = arr, (arr.shape[1] if bw is None else bw), cb, ro


def _op(a):
    return a if isinstance(a, Op) else Op(a)


def rowwise(name, fn, ins, consts, outs, accs=(), tm=512):
    ins = [_op(a) for a in ins]
    rows = outs[0][0]
    n_in, n_c, n_o, n_a = len(ins), len(consts), len(outs), len(accs)
    tm = min(tm, rows)
    assert rows % tm == 0, (name, rows, tm)

    def body(*refs):
        vals = [r[...].astype(F32) for r in refs[:n_in + n_c]]
        res = fn(*vals)
        res = res if isinstance(res, (tuple, list)) else (res,)
        o_refs = refs[n_in + n_c:n_in + n_c + n_o]
        a_refs = refs[n_in + n_c + n_o:]
        for r, v in zip(o_refs, res[:n_o]):
            r[...] = v.astype(r.dtype)
        if n_a:
            @pl.when(pl.program_id(0) == 0)
            def _():
                for r in a_refs:
                    r[...] = jnp.zeros(r.shape, r.dtype)
            for r, v in zip(a_refs, res[n_o:]):
                r[...] += v

    in_specs = [pl.BlockSpec((tm, o.bw), functools.partial(lambda i, o: (i + o.ro, o.cb), o=o)) for o in ins]
    in_specs += [pl.BlockSpec(c.shape, functools.partial(lambda i, nd: (0,) * nd, nd=c.ndim)) for c in consts]
    out_specs = [pl.BlockSpec((tm, w), lambda i: (i, 0)) for (_, w, _) in outs]
    out_specs += [pl.BlockSpec(s, functools.partial(lambda i, nd: (0,) * nd, nd=len(s))) for s in accs]
    out_shape = [jax.ShapeDtypeStruct((r, w), dt) for (r, w, dt) in outs]
    out_shape += [jax.ShapeDtypeStruct(s, F32) for s in accs]
    res = pl.pallas_call(
        body, name=name, grid=(rows // tm,), in_specs=in_specs, out_specs=out_specs, out_shape=out_shape,
        compiler_params=_cparams(("arbitrary",) if n_a else ("parallel",)),
    )(*[o.arr for o in ins], *consts)
    return res


MM_TM = 1024
MM_TN = (1024, 896, 768, 512, 256, 128)
_NT = (((1,), (1,)), ((), ()))
_NN = (((1,), (0,)), ((), ()))
_TN = (((0,), (0,)), ((), ()))


def _dot(a, b, dn, precision=None):
    return lax.dot_general(a, b, dn, preferred_element_type=F32, precision=precision)


def _mm_specs(name, pairs, n_out, tm, tn):
    in_specs, args = [], []
    for a, b, mode in pairs:
        o = _op(a)
        in_specs.append(pl.BlockSpec((tm, o.bw), functools.partial(lambda j, i, o: (i, o.cb), o=o)))
        args.append(o.arr)
        if mode == 'nn':
            assert b.shape == (o.bw, n_out), (name, b.shape, o.bw, n_out)
            in_specs.append(pl.BlockSpec((o.bw, tn), lambda j, i: (0, j)))
        else:
            assert b.shape == (n_out, o.bw), (name, b.shape, o.bw, n_out)
            in_specs.append(pl.BlockSpec((tn, o.bw), lambda j, i: (j, 0)))
        args.append(b)
    return in_specs, args


def _mm_acc(refs, pairs):
    acc = None
    for k, (_, _, mode) in enumerate(pairs):
        d = _dot(refs[2 * k][...].astype(BF16), refs[2 * k + 1][...].astype(BF16), _NN if mode == 'nn' else _NT)
        acc = d if acc is None else acc + d
    return acc


def mm(name, pairs, n_out, add=None, out_dtype=F32, tm=MM_TM, tn=None, after=None):
    m = _op(pairs[0][0]).arr.shape[0]
    tn = tn or _tile(n_out, MM_TN)
    n_p = len(pairs)

    def body(*refs):
        acc = _mm_acc(refs, pairs)
        if add is not None:
            acc = acc + refs[2 * n_p][...]
        refs[-1][...] = acc.astype(refs[-1].dtype)

    in_specs, args = _mm_specs(name, pairs, n_out, tm, tn)
    tile = pl.BlockSpec((tm, tn), lambda j, i: (i, j))
    if add is not None:
        in_specs.append(tile)
        args.append(add)
    if after is not None:
        in_specs.append(pl.BlockSpec(memory_space=pl.ANY))
        args.append(after)
    return pl.pallas_call(
        body, name=name, grid=(n_out // tn, m // tm), in_specs=in_specs, out_specs=tile,
        out_shape=jax.ShapeDtypeStruct((m, n_out), out_dtype),
        compiler_params=_cparams(("parallel", "parallel")),
    )(*args)


def mm_tn(name, a, b, out_dtype=F32, tt=1024, after=None):
    a, b = _op(a), _op(b)
    t = a.arr.shape[0]
    k, n = a.bw, b.bw
    tk = _tile(k, (512, 896, 768, 256, 128))
    tn = _tile(n, (3072, 1792) + MM_TN)
    tt = min(tt, t)
    n_t = t // tt
    order = [] if after is None else [after]

    def body(a_ref, b_ref, *rest):
        o_ref, acc_ref = rest[-2:]
        s = pl.program_id(2)
        d = _dot(a_ref[...].astype(BF16), b_ref[...].astype(BF16), _TN)

        @pl.when(s == 0)
        def _():
            acc_ref[...] = d

        @pl.when(s > 0)
        def _():
            acc_ref[...] += d

        @pl.when(s == n_t - 1)
        def _():
            o_ref[...] = acc_ref[...].astype(o_ref.dtype)

    return pl.pallas_call(
        body, name=name, grid=(k // tk, n // tn, n_t),
        in_specs=[pl.BlockSpec((tt, tk), functools.partial(lambda kk, nn, s, o: (s, o.cb * (o.bw // tk) + kk), o=a)),
                  pl.BlockSpec((tt, tn), functools.partial(lambda kk, nn, s, o: (s, o.cb * (o.bw // tn) + nn), o=b))]
        + [pl.BlockSpec(memory_space=pl.ANY) for _ in order],
        out_specs=pl.BlockSpec((tk, tn), lambda kk, nn, s: (kk, nn)),
        out_shape=jax.ShapeDtypeStruct((k, n), out_dtype),
        scratch_shapes=[pltpu.VMEM((tk, tn), F32)],
        compiler_params=_cparams(("parallel", "parallel", "arbitrary")),
    )(a.arr, b.arr, *order)


def _sigmoid(x):
    return 1.0 / (1.0 + jnp.exp(-x))


def _silu(x):
    return x * _sigmoid(x)


def _softplus(x):
    return jnp.maximum(x, 0.0) + jnp.log(1.0 + jnp.exp(-jnp.abs(x)))


def _resid_ln(scale, h, branch, g, b):
    r = ALPHA * h + scale * branch
    mu = jnp.mean(r, axis=-1, keepdims=True)
    var = jnp.mean(jnp.square(r - mu), axis=-1, keepdims=True)
    return (r - mu) * lax.rsqrt(var + LN_EPS) * g + b


def _rms(t, w):
    return t * lax.rsqrt(jnp.mean(t * t, axis=-1, keepdims=True) + RMS_EPS) * w


def _branch_weights(l1, l2, l3):
    m = jnp.maximum(jnp.maximum(l1, l2), l3)
    e1, e2, e3 = jnp.exp(l1 - m), jnp.exp(l2 - m), jnp.exp(l3 - m)
    inv = 1.0 / (e1 + e2 + e3)
    return e1 * inv, e2 * inv, e3 * inv


def _gate(y, xs, z, dskip, w):
    return _rms((y + dskip * xs) * _silu(z), w)


def _rot(x):
    d = lax.broadcasted_iota(jnp.int32, x.shape, 1) % HEAD_DIM
    up = pltpu.roll(x, x.shape[1] - ROPE_DIM // 2, 1)
    down = jnp.where(d < ROPE_DIM, pltpu.roll(x, ROPE_DIM // 2, 1), 0.0)
    return jnp.where(d < ROPE_DIM // 2, up, down)


def ffn_gate_up(name, h, wg, wu, after=()):
    m, nf = h.shape[0], wg.shape[1]
    tn = _tile(nf, MM_TN)

    def body(h_ref, g_w, u_w, *rest):
        du_ref, dg_ref, a_ref, at_ref = rest[-4:]
        hb = h_ref[...].astype(BF16)
        g = _dot(hb, g_w[...].astype(BF16), _NN)
        u = _dot(hb, u_w[...].astype(BF16), _NN)
        sig = _sigmoid(g)
        gs = g * sig
        du_ref[...] = gs.astype(du_ref.dtype)
        dg_ref[...] = (u * (sig + gs * (1.0 - sig))).astype(dg_ref.dtype)
        a = gs * u
        a_ref[...] = a.astype(a_ref.dtype)
        at_ref[...] = a.T.astype(at_ref.dtype)

    in_specs, args = _mm_specs(name, [(h, wg, 'nn')], nf, MM_TM, tn)
    in_specs.append(in_specs[1])
    in_specs += [pl.BlockSpec(memory_space=pl.ANY) for _ in after]
    tile = pl.BlockSpec((MM_TM, tn), lambda j, i: (i, j))
    return pl.pallas_call(
        body, name=name, grid=(nf // tn, m // MM_TM), in_specs=in_specs,
        out_specs=[tile] * 3 + [pl.BlockSpec((tn, MM_TM), lambda j, i: (j, i))],
        out_shape=[jax.ShapeDtypeStruct((m, nf), BF16)] * 3 + [jax.ShapeDtypeStruct((nf, m), BF16)],
        compiler_params=_cparams(("parallel", "parallel")),
    )(*args, wu, *after)


def mm_tn_cat(name, a, bs, out_dtype=F32, tt=1024):
    t, k = a.shape
    widths = [b.shape[1] for b in bs]
    n, tk, tt = sum(widths), _tile(k, (512, 256, 128)), min(tt, t)
    n_t = t // tt

    def body(a_ref, *rest):
        b_refs, o_ref, acc_ref = rest[:len(bs)], rest[-2], rest[-1]
        s = pl.program_id(1)
        at = a_ref[...].astype(BF16)
        d = jnp.concatenate([_dot(at, b[...].astype(BF16), _TN) for b in b_refs], axis=1)

        @pl.when(s == 0)
        def _():
            acc_ref[...] = d

        @pl.when(s > 0)
        def _():
            acc_ref[...] += d

        @pl.when(s == n_t - 1)
        def _():
            o_ref[...] = acc_ref[...].astype(o_ref.dtype)

    return pl.pallas_call(
        body, name=name, grid=(k // tk, n_t),
        in_specs=[pl.BlockSpec((tt, tk), lambda kk, s: (s, kk))] + [pl.BlockSpec((tt, w), lambda kk, s: (s, 0)) for w in widths],
        out_specs=pl.BlockSpec((tk, n), lambda kk, s: (kk, 0)),
        out_shape=jax.ShapeDtypeStruct((k, n), out_dtype), scratch_shapes=[pltpu.VMEM((tk, n), F32)],
        compiler_params=_cparams(("parallel", "arbitrary")),
    )(a, *bs)


def mm_acc(name, a, b, out_dtype=F32, tt=1024, after=None):
    k, t = a.shape
    n = b.shape[1]
    tk, tn, tt = _tile(k, (1024, 512, 256, 128)), _tile(n, MM_TN), min(tt, t)
    n_t = t // tt
    order = [] if after is None else [after]

    def body(a_ref, b_ref, *rest):
        o_ref, acc_ref = rest[-2:]
        s = pl.program_id(2)
        d = _dot(a_ref[...].astype(BF16), b_ref[...].astype(BF16), _NN)

        @pl.when(s == 0)
        def _():
            acc_ref[...] = d

        @pl.when(s > 0)
        def _():
            acc_ref[...] += d

        @pl.when(s == n_t - 1)
        def _():
            o_ref[...] = acc_ref[...].astype(o_ref.dtype)

    return pl.pallas_call(
        body, name=name, grid=(k // tk, n // tn, n_t),
        in_specs=[pl.BlockSpec((tk, tt), lambda kk, nn, s: (kk, s)), pl.BlockSpec((tt, tn), lambda kk, nn, s: (s, nn))]
        + [pl.BlockSpec(memory_space=pl.ANY) for _ in order],
        out_specs=pl.BlockSpec((tk, tn), lambda kk, nn, s: (kk, nn)),
        out_shape=jax.ShapeDtypeStruct((k, n), out_dtype), scratch_shapes=[pltpu.VMEM((tk, tn), F32)],
        compiler_params=_cparams(("parallel", "parallel", "arbitrary")),
    )(a, b, *order)


def ffn_da_act(name, df, wd, a_du, a_dg):
    m, nf = df.shape[0], wd.shape[0]
    tn = _tile(nf, MM_TN)

    def body(df_ref, w_ref, adu_ref, adg_ref, dg_ref, du_ref):
        da = _dot(df_ref[...].astype(BF16), w_ref[...].astype(BF16), _NT)
        dg_ref[...] = (da * adg_ref[...].astype(F32)).astype(dg_ref.dtype)
        du_ref[...] = (da * adu_ref[...].astype(F32)).astype(du_ref.dtype)

    in_specs, args = _mm_specs(name, [(df, wd, 'nt')], nf, MM_TM, tn)
    tile = pl.BlockSpec((MM_TM, tn), lambda j, i: (i, j))
    return pl.pallas_call(
        body, name=name, grid=(nf // tn, m // MM_TM), in_specs=in_specs + [tile, tile], out_specs=[tile] * 2,
        out_shape=[jax.ShapeDtypeStruct((m, nf), BF16)] * 2, compiler_params=_cparams(("parallel", "parallel")),
    )(*args, a_du, a_dg)


def resid_ln_fwd(name, scale, h, branch, ln_g, ln_b):
    t = h.shape[0]

    def fn(*a):
        y = _resid_ln(scale, *a)
        return y, y

    return rowwise(name, fn, [h, branch], [ln_g, ln_b], [(t, D_MODEL, F32), (t, D_MODEL, BF16)], tm=512)


def ffn_fwd(tag, hb, wg, wu, wd, after=()):
    g, u, a, at = ffn_gate_up(f"{tag}_gate_up", hb, wg, wu, after)
    f = mm(f"{tag}_down", [(a, wd, 'nn')], D_MODEL, out_dtype=BF16)
    return f, (hb, g, u, at)


def ln_loss_bwd(name, h, branch, target, ln_g, ln_b):
    t, dm = h.shape

    def fn(h_, br_, tgt, g_, b_):
        y, vjp = jax.vjp(functools.partial(_resid_ln, 0.5), h_, br_, g_, b_)
        e = y - tgt
        return (*vjp(e * (1.0 / dm)), jnp.sum(e * e, axis=0, keepdims=True))

    return rowwise(name, fn, [h, branch, target], [ln_g, ln_b], [(t, dm, F32), (t, dm, BF16)],
                   accs=[(1, dm), (1, dm), (1, dm)], tm=512)


def resid_ln_bwd(name, scale, h, branch, ln_g, ln_b, dout, extra=None):
    t = h.shape[0]

    def fn(h_, br_, do_, *rest):
        g_, b_ = rest[-2], rest[-1]
        _, vjp = jax.vjp(functools.partial(_resid_ln, scale), h_, br_, g_, b_)
        dh, dbr, dg, db = vjp(do_)
        if extra is not None:
            dh = dh + rest[0]
        return dh, dbr, dg, db

    ins = [h, branch, dout] + ([extra] if extra is not None else [])
    return rowwise(name, fn, ins, [ln_g, ln_b], [(t, D_MODEL, F32), (t, D_MODEL, BF16)],
                   accs=[(1, D_MODEL), (1, D_MODEL)], tm=512)


def ffn_bwd(tag, res, wg, wu, wd, df, dh_resid):
    hb, g, u, at = res
    dg, du = ffn_da_act(f"{tag}_bwd_da_act", df, wd, g, u)
    dwd = mm_acc(f"{tag}_bwd_dwd", at, df, BF16)
    dh = mm(f"{tag}_bwd_dh", [(dg, wg, 'nt'), (du, wu, 'nt')], D_MODEL, add=dh_resid, tn=512)
    dwg = mm_tn(f"{tag}_bwd_dwg", hb, dg, BF16)
    dwu = mm_tn(f"{tag}_bwd_dwu", hb, du, BF16)
    return dh, dwg, dwu, dwd


def rope_tables(positions):
    inv_freq = ROPE_THETA ** (-jnp.arange(0, ROPE_DIM, 2, dtype=F32) / ROPE_DIM)
    ang = positions.reshape(-1, 1).astype(F32) * inv_freq
    c, s = jnp.cos(ang), jnp.sin(ang)
    t = ang.shape[0]
    cosv = jnp.concatenate([c, c, jnp.ones((t, HEAD_DIM - ROPE_DIM), F32)], axis=1)
    sinv = jnp.concatenate([-s, s, jnp.zeros((t, HEAD_DIM - ROPE_DIM), F32)], axis=1)
    return jnp.tile(cosv, (1, 2)), jnp.tile(sinv, (1, 2))


def _pair_masks():
    lane = lax.broadcasted_iota(jnp.int32, (1, LANES), 1)
    return (lane < HEAD_DIM, lane >= HEAD_DIM)


def _band_masks():
    row = lax.broadcasted_iota(jnp.int32, (ATTN_BLOCK, ATTN_BLOCK), 0)
    col = lax.broadcasted_iota(jnp.int32, (ATTN_BLOCK, ATTN_BLOCK), 1)
    return col >= row, col <= row


def _residue_blocks():
    out = []
    for g, d in enumerate(DILATIONS):
        for r in range(d):
            for i in range(SEQ // d // ATTN_BLOCK):
                rows = lambda j: pl.ds(r + j * ATTN_BLOCK * d, ATTN_BLOCK, stride=d) if d > 1 else pl.ds(j * ATTN_BLOCK, ATTN_BLOCK)
                out.append((g, rows(i), rows(i - 1) if i > 0 else None))
    return out


N_HEAD_PAIRS = D_ATTN // LANES
SCALE = HEAD_DIM ** -0.5
ATTN_GROUP = 4
ATTN_GROUP_BWD = 16


def _block_operands(qr, kr, v_ref, cur, prev):
    prev_ok, cur_ok = _band_masks()
    if prev is None:
        return qr[cur, :], kr[cur, :].astype(BF16), v_ref[cur, :], cur_ok
    kcat = jnp.concatenate([kr[prev, :], kr[cur, :]], axis=0).astype(BF16)
    vcat = jnp.concatenate([v_ref[prev, :], v_ref[cur, :]], axis=0)
    return qr[cur, :], kcat, vcat, jnp.concatenate([prev_ok, cur_ok], axis=1)


def _attn_specs(b):
    col = lambda cb: pl.BlockSpec((SEQ, LANES), lambda bb, hp: (bb, cb + hp))
    tab = pl.BlockSpec((SEQ, LANES), lambda bb, hp: (bb, 0))
    return col, tab


def attn_fwd(qkvz, cosv, sinv, b):
    t = qkvz.shape[0]
    col, tab = _attn_specs(b)
    blocks = _residue_blocks()

    def body(q_ref, k_ref, v_ref, c_ref, s_ref, o_ref, l1_ref, l2_ref, l3_ref, qr, kr, o1, o2, o3):
        l_refs, o_scr = (l1_ref, l2_ref, l3_ref), (o1, o2, o3)
        c, s = c_ref[...], s_ref[...]
        q, k = q_ref[...], k_ref[...]
        qr[...] = q * c + _rot(q) * s
        kr[...] = k * c + _rot(k) * s
        masks = _pair_masks()
        for lo in range(0, len(blocks), ATTN_GROUP):
            chains = []
            for g, cur, prev in blocks[lo:lo + ATTN_GROUP]:
                q2, kcat, vcat, ok = _block_operands(qr, kr, v_ref, cur, prev)
                for m in masks:
                    qm = jnp.where(m, q2, 0.0).astype(BF16)
                    chains.append(dict(g=g, cur=cur, m=m, v=jnp.where(m, vcat, 0.0).astype(BF16),
                                       s=jnp.where(ok, _dot(qm, kcat, _NT) * SCALE, NEG)))
            for ch in chains:
                mx = jnp.max(ch['s'], axis=1, keepdims=True)
                p = jnp.exp(ch['s'] - mx)
                den = jnp.sum(p, axis=1, keepdims=True)
                ch.update(p=p.astype(BF16), inv=1.0 / den, lse=mx + jnp.log(den))
            for ch in chains:
                ch['o'] = _dot(ch['p'], ch['v'], _NN) * ch['inv']
            for c0, c1 in zip(chains[0::2], chains[1::2]):
                o_scr[c0['g']][c0['cur'], :] = c0['o'] + c1['o']
                l_refs[c0['g']][c0['cur'], :] = jnp.where(c0['m'], c0['lse'], c1['lse'])
        w1, w2, w3 = _branch_weights(l1_ref[...], l2_ref[...], l3_ref[...])
        o_ref[...] = w1 * o1[...] + w2 * o2[...] + w3 * o3[...]

    shp = jax.ShapeDtypeStruct((t, D_ATTN), F32)
    return pl.pallas_call(
        body, name="attn_fwd", grid=(b, N_HEAD_PAIRS),
        in_specs=[col(0), col(N_HEAD_PAIRS), col(2 * N_HEAD_PAIRS), tab, tab],
        out_specs=[col(0)] * 4, out_shape=[shp] * 4,
        scratch_shapes=[pltpu.VMEM((SEQ, LANES), F32)] * 5,
        compiler_params=_cparams(("parallel", "parallel")),
    )(qkvz, qkvz, qkvz, cosv, sinv)


def attn_bwd(qkvz, cosv, sinv, dmix, mixed, lses, b):
    t = qkvz.shape[0]
    col, tab = _attn_specs(b)
    blocks = _residue_blocks()
    hd = np.arange(LANES) // HEAD_DIM
    head_ones = jnp.asarray((hd[:, None] == hd[None, :]).astype(np.float32))

    def body(q_ref, k_ref, v_ref, c_ref, s_ref, dm_ref, mx_ref, l1_ref, l2_ref, l3_ref, ones_ref,
             dq_out, dk_out, dv_out, qr, kr, do1, do2, do3, dd1, dd2, dd3, dq_ref, dk_ref, dv_ref):
        l_refs, do_scr, dd_scr = (l1_ref, l2_ref, l3_ref), (do1, do2, do3), (dd1, dd2, dd3)
        c, s = c_ref[...], s_ref[...]
        q, k = q_ref[...], k_ref[...]
        qr[...] = q * c + _rot(q) * s
        kr[...] = k * c + _rot(k) * s
        dm = dm_ref[...]
        tot = _dot(dm * mx_ref[...], ones_ref[...], _NN, HI)
        for w, do_g, dd_g in zip(_branch_weights(l1_ref[...], l2_ref[...], l3_ref[...]), do_scr, dd_scr):
            do_g[...] = w * dm
            dd_g[...] = w * tot
        dq_ref[...] = jnp.zeros((SEQ, LANES), F32)
        dk_ref[...] = jnp.zeros((SEQ, LANES), F32)
        dv_ref[...] = jnp.zeros((SEQ, LANES), F32)
        masks = _pair_masks()
        for lo in range(0, len(blocks), ATTN_GROUP_BWD):
            chains = []
            for g, cur, prev in blocks[lo:lo + ATTN_GROUP_BWD]:
                q2, kcat, vcat, ok = _block_operands(qr, kr, v_ref, cur, prev)
                vcat = vcat.astype(BF16)
                do2_, l2, dd2_ = do_scr[g][cur, :], l_refs[g][cur, :], dd_scr[g][cur, :]
                l2s, dd2s = pltpu.roll(l2, HEAD_DIM, 1), pltpu.roll(dd2_, HEAD_DIM, 1)
                for m in masks:
                    qm = jnp.where(m, q2, 0.0).astype(BF16)
                    dom = jnp.where(m, do2_, 0.0).astype(BF16)
                    lrep, ddrep = jnp.where(m, l2, l2s), jnp.where(m, dd2_, dd2s)
                    if prev is not None:
                        lrep, ddrep = jnp.concatenate([lrep, lrep], axis=1), jnp.concatenate([ddrep, ddrep], axis=1)
                    chains.append(dict(cur=cur, prev=prev, qm=qm, dom=dom, km=jnp.where(m, kcat, 0), lrep=lrep, ddrep=ddrep,
                                       s=jnp.where(ok, _dot(qm, kcat, _NT) * SCALE, NEG), dp=_dot(dom, vcat, _NT)))
            for ch in chains:
                p = jnp.exp(ch['s'] - ch['lrep'])
                ch.update(p=p.astype(BF16), ds=(p * (ch['dp'] - ch['ddrep']) * SCALE).astype(BF16))
            for ch in chains:
                ch.update(dq=_dot(ch['ds'], ch['km'], _NN), dk=_dot(ch['ds'], ch['qm'], _TN), dv=_dot(ch['p'], ch['dom'], _TN))
            for c0, c1 in zip(chains[0::2], chains[1::2]):
                cur, prev = c0['cur'], c0['prev']
                dk, dv = c0['dk'] + c1['dk'], c0['dv'] + c1['dv']
                dq_ref[cur, :] += c0['dq'] + c1['dq']
                if prev is None:
                    dk_ref[cur, :] += dk
                    dv_ref[cur, :] += dv
                else:
                    dk_ref[prev, :] += dk[:ATTN_BLOCK]
                    dv_ref[prev, :] += dv[:ATTN_BLOCK]
                    dk_ref[cur, :] += dk[ATTN_BLOCK:]
                    dv_ref[cur, :] += dv[ATTN_BLOCK:]
        dq, dk = dq_ref[...], dk_ref[...]
        dq_out[...] = (dq * c + _rot(dq * s)).astype(dq_out.dtype)
        dk_out[...] = (dk * c + _rot(dk * s)).astype(dk_out.dtype)
        dv_out[...] = dv_ref[...].astype(dv_out.dtype)

    shp = jax.ShapeDtypeStruct((t, D_ATTN), BF16)
    return pl.pallas_call(
        body, name="attn_bwd", grid=(b, N_HEAD_PAIRS),
        in_specs=[col(0), col(N_HEAD_PAIRS), col(2 * N_HEAD_PAIRS), tab, tab, col(0), col(0), col(0), col(0), col(0),
                  pl.BlockSpec((LANES, LANES), lambda bb, hp: (0, 0))],
        out_specs=[col(0)] * 3, out_shape=[shp] * 3,
        scratch_shapes=[pltpu.VMEM((SEQ, LANES), F32)] * 11,
        compiler_params=_cparams(("parallel", "parallel")),
    )(qkvz, qkvz, qkvz, cosv, sinv, dmix, mixed, *lses, head_ones)


def attn_norm_fwd(mixed, norm_w):
    return rowwise("attn_norm", _rms, [mixed], [norm_w], [(mixed.shape[0], D_ATTN, BF16)])[0]


def attn_norm_bwd(dout, mixed, norm_w):
    def fn(dy, mx, w):
        _, vjp = jax.vjp(_rms, mx, w)
        return vjp(dy)

    return rowwise("attn_norm_bwd", fn, [dout, mixed], [norm_w], [(dout.shape[0], D_ATTN, F32)], accs=[(1, D_ATTN)])


CONV_TM = 512
HALO = 8


def _conv_columns(refs):
    xs_ref, bm_ref, cm_ref = refs
    out = []
    for c in range(D_CONV // LANES):
        lo = c * LANES
        ref, base = (xs_ref, 0) if lo < D_SSD else (bm_ref, D_SSD) if lo < D_SSD + D_BC else (cm_ref, D_SSD + D_BC)
        out.append((slice(lo, lo + LANES), (ref, slice(lo - base, lo - base + LANES))))
    return out


def _conv_taps(scr, w_ref, cs, first_row, step, tm):
    acc = None
    for k in range(CONV_WIDTH):
        term = w_ref[k:k + 1, cs] * scr[pl.ds(first_row + step * k, tm), cs]
        acc = term if acc is None else acc + term
    return acc


def conv_fwd(u, w, bias):
    t = u.shape[0]
    tm, per_seq = CONV_TM, SEQ // CONV_TM

    def body(u_ref, h_ref, w_ref, b_ref, xs_ref, bm_ref, cm_ref, scr):
        first = pl.program_id(0) % per_seq == 0
        scr[0:HALO, :] = jnp.where(first, 0.0, h_ref[...])
        scr[HALO:, :] = u_ref[...]
        for cs, (o_ref, os_) in _conv_columns((xs_ref, bm_ref, cm_ref)):
            o_ref[:, os_] = _silu(_conv_taps(scr, w_ref, cs, HALO - CONV_WIDTH + 1, 1, tm) + b_ref[:, cs])

    return pl.pallas_call(
        body, name="conv_fwd", grid=(t // tm,),
        in_specs=[pl.BlockSpec((tm, D_CONV), lambda i: (i, 0)),
                  pl.BlockSpec((HALO, D_CONV), lambda i: (jnp.maximum(i * (tm // HALO) - 1, 0), 0)),
                  pl.BlockSpec((CONV_WIDTH, D_CONV), lambda i: (0, 0)), pl.BlockSpec((1, D_CONV), lambda i: (0, 0))],
        out_specs=[pl.BlockSpec((tm, D_SSD), lambda i: (i, 0)), pl.BlockSpec((tm, D_BC), lambda i: (i, 0)),
                   pl.BlockSpec((tm, D_BC), lambda i: (i, 0))],
        out_shape=[jax.ShapeDtypeStruct((t, D_SSD), F32), jax.ShapeDtypeStruct((t, D_BC), F32),
                   jax.ShapeDtypeStruct((t, D_BC), F32)],
        scratch_shapes=[pltpu.VMEM((tm + HALO, D_CONV), F32)],
        compiler_params=_cparams(("parallel",)),
    )(u, u, w, bias)


def conv_bwd(u, w, bias, dxs_a, dxs_b, dbm, dcm):
    t = u.shape[0]
    tm, per_seq = CONV_TM, SEQ // CONV_TM
    n_tiles = t // tm

    def body1(u_ref, h_ref, dxs_ref, dxs2_ref, dbm_ref, dcm_ref, w_ref, b_ref, dz_ref, dw_ref, db_ref, scr):
        i = pl.program_id(0)
        first = i % per_seq == 0
        scr[0:HALO, :] = jnp.where(first, 0.0, h_ref[...])
        scr[HALO:, :] = u_ref[...]

        @pl.when(i == 0)
        def _():
            dw_ref[...] = jnp.zeros(dw_ref.shape, F32)
            db_ref[...] = jnp.zeros(db_ref.shape, F32)
        for cs, (g_ref, gs) in _conv_columns((dxs_ref, dbm_ref, dcm_ref)):
            acc = _conv_taps(scr, w_ref, cs, HALO - CONV_WIDTH + 1, 1, tm) + b_ref[:, cs]
            sig = _sigmoid(acc)
            dy = g_ref[:, gs] + dxs2_ref[:, gs] if g_ref is dxs_ref else g_ref[:, gs]
            dz = dy * sig * (1.0 + acc * (1.0 - sig))
            dz_ref[:, cs] = dz
            db_ref[:, cs] += jnp.sum(dz, axis=0, keepdims=True)
            for k in range(CONV_WIDTH):
                dw_ref[k:k + 1, cs] += jnp.sum(dz * scr[pl.ds(HALO - CONV_WIDTH + 1 + k, tm), cs], axis=0, keepdims=True)

    dz, dw, db = pl.pallas_call(
        body1, name="conv_bwd_dz", grid=(n_tiles,),
        in_specs=[pl.BlockSpec((tm, D_CONV), lambda i: (i, 0)),
                  pl.BlockSpec((HALO, D_CONV), lambda i: (jnp.maximum(i * (tm // HALO) - 1, 0), 0)),
                  pl.BlockSpec((tm, D_SSD), lambda i: (i, 0)), pl.BlockSpec((tm, D_SSD), lambda i: (i, 0)),
                  pl.BlockSpec((tm, D_BC), lambda i: (i, 0)), pl.BlockSpec((tm, D_BC), lambda i: (i, 0)),
                  pl.BlockSpec((CONV_WIDTH, D_CONV), lambda i: (0, 0)), pl.BlockSpec((1, D_CONV), lambda i: (0, 0))],
        out_specs=[pl.BlockSpec((tm, D_CONV), lambda i: (i, 0)), pl.BlockSpec((CONV_WIDTH, D_CONV), lambda i: (0, 0)),
                   pl.BlockSpec((1, D_CONV), lambda i: (0, 0))],
        out_shape=[jax.ShapeDtypeStruct((t, D_CONV), F32), jax.ShapeDtypeStruct((CONV_WIDTH, D_CONV), F32),
                   jax.ShapeDtypeStruct((1, D_CONV), F32)],
        scratch_shapes=[pltpu.VMEM((tm + HALO, D_CONV), F32)],
        compiler_params=_cparams(("arbitrary",)),
    )(u, u, dxs_a, dxs_b, dbm, dcm, w, bias)

    def body2(dz_ref, n_ref, w_ref, du_ref, scr):
        last = pl.program_id(0) % per_seq == per_seq - 1
        scr[0:tm, :] = dz_ref[...]
        scr[tm:, :] = jnp.where(last, 0.0, n_ref[...])
        for c in range(D_CONV // LANES):
            cs = slice(c * LANES, (c + 1) * LANES)
            du_ref[:, cs] = _conv_taps(scr, w_ref, cs, CONV_WIDTH - 1, -1, tm).astype(du_ref.dtype)

    du = pl.pallas_call(
        body2, name="conv_bwd_du", grid=(n_tiles,),
        in_specs=[pl.BlockSpec((tm, D_CONV), lambda i: (i, 0)),
                  pl.BlockSpec((HALO, D_CONV), lambda i: (jnp.minimum((i + 1) * (tm // HALO), t // HALO - 1), 0)),
                  pl.BlockSpec((CONV_WIDTH, D_CONV), lambda i: (0, 0))],
        out_specs=pl.BlockSpec((tm, D_CONV), lambda i: (i, 0)),
        out_shape=jax.ShapeDtypeStruct((t, D_CONV), BF16),
        scratch_shapes=[pltpu.VMEM((tm + HALO, D_CONV), F32)],
        compiler_params=_cparams(("parallel",)),
    )(dz, dz, w)
    return du, dw, db


Q = SSD_CHUNK
N_PAIRS = D_SSD // LANES
HEADS_PER_GROUP = N_HEADS // SSD_GROUPS


def _rep(a, j):
    return jnp.broadcast_to(a[:, j:j + 1], a.shape)


def _dot_exact01(a, b, dn, a_is_01):
    x = b if a_is_01 else a
    hi = x.astype(BF16)
    mid = (x - hi.astype(F32)).astype(BF16)
    lo = (x - hi.astype(F32) - mid.astype(F32)).astype(BF16)
    z = a.astype(BF16) if a_is_01 else b.astype(BF16)
    out = None
    for term in (hi, mid, lo):
        d = _dot(z, term, dn) if a_is_01 else _dot(term, z, dn)
        out = d if out is None else out + d
    return out


def _pad_lanes(v, fill=0.0):
    row = jnp.pad(v.reshape(1, -1).astype(F32), ((0, 0), (0, LANES - v.size)), constant_values=fill)
    return row, row.reshape(LANES, 1)


def _ssd_common(dtr_ref, dtrt_ref, bias_r, bias_c, alog_r, alog_c):
    row = lax.broadcasted_iota(jnp.int32, (Q, Q), 0)
    col = lax.broadcasted_iota(jnp.int32, (Q, Q), 1)
    tril = row >= col
    lane = lax.broadcasted_iota(jnp.int32, (1, LANES), 1)
    a_r = jnp.where(lane < N_HEADS, -jnp.exp(alog_r[...]), 0.0)
    sub = lax.broadcasted_iota(jnp.int32, (LANES, 1), 0)
    a_c = jnp.where(sub < N_HEADS, -jnp.exp(alog_c[...]), 0.0)
    dt = _softplus(dtr_ref[...] + bias_r[...])
    cs = _dot_exact01(tril, dt * a_r, _NN, True)
    dtt = _softplus(dtrt_ref[...] + bias_c[...])
    cst = _dot_exact01(dtt * a_c, row <= col, _NN, False)
    return tril, lane, a_r, dt, cs, cst


def _ssd_specs(b, nc, rev):
    ci = (lambda c: nc - 1 - c) if rev else (lambda c: c)
    rows = lambda w: pl.BlockSpec((Q, w), lambda bb, c: (bb * nc + ci(c), 0))
    dtt = pl.BlockSpec((LANES, Q), lambda bb, c: (0, bb * nc + ci(c)))
    const = lambda s: pl.BlockSpec(s, lambda bb, c: (0,) * len(s))
    state = pl.BlockSpec((None, N_PAIRS, LANES, SSD_STATE), lambda bb, c: (bb * nc + ci(c), 0, 0, 0))
    return rows, dtt, const, state


def ssd_fwd(xs, bm, cm, dtraw, dt_bias, a_log, b):
    t = xs.shape[0]
    nc = SEQ // Q
    rows, dtt_spec, const, state = _ssd_specs(b, nc, False)
    bias_r, bias_c = _pad_lanes(dt_bias)
    alog_r, alog_c = _pad_lanes(a_log)

    def body(xs_ref, b_ref, c_ref, dtr_ref, dtrt_ref, br, bc, ar, ac, y_ref, hp_ref, h_scr):
        @pl.when(pl.program_id(1) == 0)
        def _():
            h_scr[...] = jnp.zeros(h_scr.shape, F32)
        tril, lane, _, dt, cs, cst = _ssd_common(dtr_ref, dtrt_ref, br, bc, ar, ac)
        sub = lax.broadcasted_iota(jnp.int32, (LANES, 1), 0)
        y_acc = [jnp.zeros((Q, LANES), F32) for _ in range(N_PAIRS)]
        h_old = [h_scr[p] for p in range(N_PAIRS)]
        h_new = [jnp.zeros((LANES, SSD_STATE), F32) for _ in range(N_PAIRS)]
        for g in range(SSD_GROUPS):
            bg = b_ref[:, g * SSD_STATE:(g + 1) * SSD_STATE].astype(BF16)
            cg = c_ref[:, g * SSD_STATE:(g + 1) * SSD_STATE].astype(BF16)
            cb = _dot(cg, bg, _NT)
            heads = []
            for j in range(g * HEADS_PER_GROUP, (g + 1) * HEADS_PER_GROUP):
                p, side = j // 2, j % 2
                m = (lane < HEAD_DIM) if side == 0 else (lane >= HEAD_DIM)
                ms = (sub < HEAD_DIM) if side == 0 else (sub >= HEAD_DIM)
                csj, dtj = _rep(cs, j), _rep(dt, j)
                lmat = jnp.exp(jnp.where(tril, csj - cst[j:j + 1, :], NEG))
                xdt = jnp.where(m, xs_ref[:, p * LANES:(p + 1) * LANES] * dtj, 0.0)
                hm = jnp.where(ms, h_old[p], 0.0)
                last = csj[Q - 1:Q, :]
                heads.append(dict(p=p, hm=hm, ecs=jnp.exp(csj), el=jnp.exp(last), gmat=(cb * lmat).astype(BF16),
                                  xdt=xdt.astype(BF16), xd=(xdt * jnp.exp(last - csj)).astype(BF16)))
            for h in heads:
                h.update(ydiag=_dot(h['gmat'], h['xdt'], _NN), ch=_dot(cg, h['hm'].astype(BF16), _NT), sj=_dot(h['xd'], bg, _TN))
            for h in heads:
                y_acc[h['p']] = y_acc[h['p']] + h['ydiag'] + h['ecs'] * h['ch']
                h_new[h['p']] = h_new[h['p']] + h['el'] * h['hm'] + h['sj']
        for p in range(N_PAIRS):
            y_ref[:, p * LANES:(p + 1) * LANES] = y_acc[p]
            hp_ref[p] = h_old[p]
            h_scr[p] = h_new[p]

    return pl.pallas_call(
        body, name="ssd_fwd", grid=(b, nc),
        in_specs=[rows(D_SSD), rows(D_BC), rows(D_BC), rows(LANES), dtt_spec, const((1, LANES)), const((LANES, 1)),
                  const((1, LANES)), const((LANES, 1))],
        out_specs=[rows(D_SSD), state],
        out_shape=[jax.ShapeDtypeStruct((t, D_SSD), F32),
                   jax.ShapeDtypeStruct((b * nc, N_PAIRS, LANES, SSD_STATE), F32)],
        scratch_shapes=[pltpu.VMEM((N_PAIRS, LANES, SSD_STATE), F32)],
        compiler_params=_cparams(("parallel", "arbitrary")),
    )(xs, bm, cm, dtraw, dtraw.T, bias_r, bias_c, alog_r, alog_c)


def ssd_bwd(xs, bm, cm, dtraw, dt_bias, a_log, hprev, dy, b):
    t = xs.shape[0]
    nc = SEQ // Q
    rows, dtt_spec, const, state = _ssd_specs(b, nc, True)
    bias_r, bias_c = _pad_lanes(dt_bias)
    alog_r, alog_c = _pad_lanes(a_log)

    def body(xs_ref, b_ref, c_ref, dtr_ref, dtrt_ref, hp_ref, dy_ref, br, bc, ar, ac,
             dxs_ref, db_ref, dc_ref, ddt_ref, dbias_ref, dalog_ref, dh_scr):
        first = jnp.logical_and(pl.program_id(0) == 0, pl.program_id(1) == 0)

        @pl.when(pl.program_id(1) == 0)
        def _():
            dh_scr[...] = jnp.zeros(dh_scr.shape, F32)

        @pl.when(first)
        def _():
            dbias_ref[...] = jnp.zeros(dbias_ref.shape, F32)
            dalog_ref[...] = jnp.zeros(dalog_ref.shape, F32)
        tril, lane, a_r, dt, cs, cst = _ssd_common(dtr_ref, dtrt_ref, br, bc, ar, ac)
        sub = lax.broadcasted_iota(jnp.int32, (LANES, 1), 0)
        rowq = lax.broadcasted_iota(jnp.int32, (Q, 1), 0)
        triu = (lax.broadcasted_iota(jnp.int32, (Q, Q), 0) <= lax.broadcasted_iota(jnp.int32, (Q, Q), 1)).astype(F32)
        dxs_acc = [jnp.zeros((Q, LANES), F32) for _ in range(N_PAIRS)]
        dh_in = [dh_scr[p] for p in range(N_PAIRS)]
        h_in = [hp_ref[p] for p in range(N_PAIRS)]
        dh_out = [jnp.zeros((LANES, SSD_STATE), F32) for _ in range(N_PAIRS)]
        ddt = jnp.zeros((Q, LANES), F32)
        dalog = jnp.zeros((1, LANES), F32)
        for g in range(SSD_GROUPS):
            gs = slice(g * SSD_STATE, (g + 1) * SSD_STATE)
            bg, cg = b_ref[:, gs].astype(BF16), c_ref[:, gs].astype(BF16)
            cb = _dot(cg, bg, _NT)
            dcb = jnp.zeros((Q, Q), F32)
            dbg = jnp.zeros((Q, SSD_STATE), F32)
            dcg = jnp.zeros((Q, SSD_STATE), F32)
            heads = []
            for j in range(g * HEADS_PER_GROUP, (g + 1) * HEADS_PER_GROUP):
                p, side = j // 2, j % 2
                m = (lane < HEAD_DIM) if side == 0 else (lane >= HEAD_DIM)
                ms = (sub < HEAD_DIM) if side == 0 else (sub >= HEAD_DIM)
                csj, dtj = _rep(cs, j), _rep(dt, j)
                lmat = jnp.exp(jnp.where(tril, csj - cst[j:j + 1, :], NEG))
                x2 = jnp.where(m, xs_ref[:, p * LANES:(p + 1) * LANES], 0.0)
                xdt = x2 * dtj
                dym = jnp.where(m, dy_ref[:, p * LANES:(p + 1) * LANES], 0.0)
                hm = jnp.where(ms, h_in[p], 0.0)
                dhm = jnp.where(ms, dh_in[p], 0.0)
                last = csj[Q - 1:Q, :]
                decay = jnp.exp(last - csj)
                heads.append(dict(j=j, p=p, dtj=dtj, lmat=lmat, x2=x2, hm=hm, dhm=dhm, decay=decay, el=jnp.exp(last),
                                  gmat=cb * lmat, dym=dym.astype(BF16), xdt=xdt.astype(BF16), hmb=hm.astype(BF16),
                                  dhmb=dhm.astype(BF16), dye=dym * jnp.exp(csj), xd=xdt * decay))
            for h in heads:
                dyeb, xdb = h['dye'].astype(BF16), h['xd'].astype(BF16)
                h.update(dg=_dot(h['dym'], h['xdt'], _NT),
                         dxdt=_dot(h['gmat'].astype(BF16), h['dym'], _TN),
                         ch=_dot(cg, h['hmb'], _NT),
                         dcg=_dot(dyeb, h['hmb'], _NN), dhp=_dot(dyeb, cg, _TN),
                         wmat=_dot(bg, h['dhmb'], _NT),
                         dbg=_dot(xdb, h['dhmb'], _NN))
            for h in heads:
                ej = h['dg'] * h['gmat']
                col_sums = jnp.broadcast_to(jnp.sum(ej, axis=0, keepdims=True), (Q, Q)).T
                dl = h['xd'] * h['wmat']
                total = lambda v: jnp.sum(jnp.sum(v, axis=0, keepdims=True), axis=1, keepdims=True)
                dlast = total(dl) + h['el'] * total(h['dhm'] * h['hm'])
                h['dcs'] = (jnp.sum(ej + h['dye'] * h['ch'] - dl, axis=1, keepdims=True) - col_sums
                            + jnp.where(rowq == Q - 1, dlast, 0.0))
                h['dxdt'] = h['dxdt'] + h['decay'] * h['wmat']
                dcb, dcg, dbg = dcb + h['dg'] * h['lmat'], dcg + h['dcg'], dbg + h['dbg']
                dh_out[h['p']] = dh_out[h['p']] + h['el'] * h['dhm'] + h['dhp']
            for h in heads:
                h['da'] = _dot_exact01(triu, h['dcs'], _NN, True)
            for h in heads:
                j, da = h['j'], h['da']
                aj = jnp.sum(jnp.where(lane == j, a_r, 0.0), axis=1, keepdims=True)
                ddtj = da * aj + jnp.sum(h['dxdt'] * h['x2'], axis=1, keepdims=True)
                ddt = ddt + jnp.where(lane == j, ddtj, 0.0)
                dalog = dalog + jnp.where(lane == j, jnp.sum(da * h['dtj'], axis=0, keepdims=True) * aj, 0.0)
                dxs_acc[h['p']] = dxs_acc[h['p']] + h['dxdt'] * h['dtj']
            dcbb = dcb.astype(BF16)
            dc_ref[:, gs] = dcg + _dot(dcbb, bg, _NN)
            db_ref[:, gs] = dbg + _dot(dcbb, cg, _TN)
        for p in range(N_PAIRS):
            dxs_ref[:, p * LANES:(p + 1) * LANES] = dxs_acc[p]
            dh_scr[p] = dh_out[p]
        ddtraw = ddt * _sigmoid(dtr_ref[...] + br[...])
        ddt_ref[...] = ddtraw
        dbias_ref[...] += jnp.sum(ddtraw, axis=0, keepdims=True)
        dalog_ref[...] += dalog

    return pl.pallas_call(
        body, name="ssd_bwd", grid=(b, nc),
        in_specs=[rows(D_SSD), rows(D_BC), rows(D_BC), rows(LANES), dtt_spec, state, rows(D_SSD), const((1, LANES)),
                  const((LANES, 1)), const((1, LANES)), const((LANES, 1))],
        out_specs=[rows(D_SSD), rows(D_BC), rows(D_BC), rows(LANES), const((1, LANES)), const((1, LANES))],
        out_shape=[jax.ShapeDtypeStruct((t, D_SSD), F32), jax.ShapeDtypeStruct((t, D_BC), F32),
                   jax.ShapeDtypeStruct((t, D_BC), F32), jax.ShapeDtypeStruct((t, LANES), F32),
                   jax.ShapeDtypeStruct((1, LANES), F32), jax.ShapeDtypeStruct((1, LANES), F32)],
        scratch_shapes=[pltpu.VMEM((N_PAIRS, LANES, SSD_STATE), F32)],
        compiler_params=_cparams(("arbitrary", "arbitrary")),
    )(xs, bm, cm, dtraw, dtraw.T, hprev, dy, bias_r, bias_c, alog_r, alog_c)


def _split_w_in(w_in):
    w_dt = jnp.pad(w_in[:, D_QKVZ + D_CONV:], ((0, 0), (0, LANES - N_HEADS)))
    return w_in[:, :D_QKVZ], w_in[:, D_QKVZ:D_QKVZ + D_CONV], w_dt


def mixer_fwd(hb, p, cosv, sinv, b):
    t = hb.shape[0]
    w_a, w_b, w_c = _split_w_in(p['w_in'])
    qkvz = mm("in_qkvz", [(hb, w_a, 'nn')], D_QKVZ)
    xbc = mm("in_xbc", [(hb, w_b, 'nn')], D_CONV)
    dtraw = mm("in_dt", [(hb, w_c, 'nn')], LANES)
    mixed, *lses = attn_fwd(qkvz, cosv, sinv, b)
    attn = attn_norm_fwd(mixed, p['attn_norm_w'])
    xs, bm, cm = conv_fwd(xbc, p['conv_w'], p['conv_b'])
    y, hprev = ssd_fwd(xs, bm, cm, dtraw, p['dt_bias'], p['a_log'], b)
    dskip = jnp.repeat(p['d_skip'].reshape(-1), HEAD_DIM).reshape(1, D_SSD)
    yg, = rowwise("ssd_gate", _gate, [y, xs, Op(qkvz, D_SSD, 3)], [dskip, p['ssd_norm_w']], [(t, D_SSD, BF16)])
    mix = mm("out_proj", [(attn, p['w_out'][:D_ATTN], 'nn'), (yg, p['w_out'][D_ATTN:], 'nn')], D_MODEL, out_dtype=BF16)
    res = dict(hb=hb, qkvz=qkvz, xbc=xbc, dtraw=dtraw, mixed=mixed, lses=lses, attn=attn, xs=xs, bm=bm, cm=cm,
               y=y, hprev=hprev, dskip=dskip, yg=yg, cosv=cosv, sinv=sinv)
    return mix, res


def mixer_bwd(r, p, dmix, dh_resid, b):
    t = dmix.shape[0]
    w_a, w_b, w_c = _split_w_in(p['w_in'])
    w_out = p['w_out']
    dattn = mm("out_bwd_dattn", [(dmix, w_out[:D_ATTN], 'nt')], D_ATTN)
    dyg = mm("out_bwd_dyg", [(dmix, w_out[D_ATTN:], 'nt')], D_SSD)
    dw_out = jnp.concatenate([mm_tn("out_bwd_dw_a", r['attn'], dmix, BF16),
                              mm_tn("out_bwd_dw_y", r['yg'], dmix, BF16)], axis=0)

    def gate_bwd(dy_, y_, xs_, z_, ds_, w_):
        _, vjp = jax.vjp(_gate, y_, xs_, z_, ds_, w_)
        return vjp(dy_)

    dy, dxs_a, dz, ddskip, dssd_norm = rowwise(
        "ssd_gate_bwd", gate_bwd, [dyg, r['y'], r['xs'], Op(r['qkvz'], D_SSD, 3)], [r['dskip'], p['ssd_norm_w']],
        [(t, D_SSD, F32), (t, D_SSD, F32), (t, D_SSD, BF16)], accs=[(1, D_SSD), (1, D_SSD)])
    dxs_b, dbm, dcm, ddtraw, ddt_bias, da_log = ssd_bwd(r['xs'], r['bm'], r['cm'], r['dtraw'], p['dt_bias'], p['a_log'],
                                                        r['hprev'], dy, b)
    dxbc, dconv_w, dconv_b = conv_bwd(r['xbc'], p['conv_w'], p['conv_b'], dxs_a, dxs_b, dbm, dcm)
    dmixed, dattn_norm = attn_norm_bwd(dattn, r['mixed'], p['attn_norm_w'])
    dq, dk, dv = attn_bwd(r['qkvz'], r['cosv'], r['sinv'], dmixed, r['mixed'], r['lses'], b)
    wq, wk, wv, wz = (w_a[:, i * D_ATTN:(i + 1) * D_ATTN] for i in range(4))
    dh = mm("in_bwd_dh", [(dq, wq, 'nt'), (dk, wk, 'nt'), (dv, wv, 'nt'), (dz, wz, 'nt'), (dxbc, w_b, 'nt'),
                          (ddtraw, w_c, 'nt')], D_MODEL, add=dh_resid, tn=512)
    h = r['hb']
    dw_in = jnp.concatenate([mm_tn_cat("in_bwd_dw_qkvz", h, [dq, dk, dv, dz], BF16),
                             mm_tn_cat("in_bwd_dw_xbc_dt", h, [dxbc, ddtraw], BF16)[:, :D_CONV + N_HEADS]], axis=1)
    head_sum = lambda v: v.reshape(N_HEADS, HEAD_DIM).sum(axis=1).reshape(1, N_HEADS)
    grads = dict(w_in=dw_in, w_out=dw_out, conv_w=dconv_w, conv_b=dconv_b, dt_bias=ddt_bias[:, :N_HEADS],
                 a_log=da_log[:, :N_HEADS], d_skip=head_sum(ddskip), attn_norm_w=dattn_norm, ssd_norm_w=dssd_norm)
    return dh, grads


FFN2_KEYS = ('ffn2_gate', 'ffn2_up', 'ffn2_down')
MIXER_KEYS = ('w_in', 'conv_w', 'w_out')
FFN_COL = ('ffn1_gate', 'ffn1_up', 'ffn2_gate', 'ffn2_up')
FFN_ROW = ('ffn1_down', 'ffn2_down')
CONV_W_COMM = (8, 2 * LANES)
SMALL = 'small'


def comm_shape(k, shapes):
    if k in FFN_COL:
        return (D_MODEL, FF_PAD)
    if k in FFN_ROW:
        return (FF_PAD, D_MODEL)
    if k == 'conv_w':
        return CONV_W_COMM
    return tuple(shapes[k][1:])


def to_comm(k, vals, shapes):
    a = vals[k].reshape(shapes[k][1:])
    r_, c_ = comm_shape(k, shapes)
    return jnp.pad(a, ((0, r_ - a.shape[0]), (0, c_ - a.shape[1])))


SMALL_ROWS, SMALL_COLS = 16, D_CONV


def pack_small(small):
    rows = [jnp.pad(small[r].reshape(1, -1), ((0, 0), (0, SMALL_COLS - small[r].size))) for r in REPLICATED]
    return jnp.concatenate(rows + [jnp.zeros((SMALL_ROWS - len(rows), SMALL_COLS), F32)], axis=0)


def full_weight(k, g):
    if k in FFN_COL:
        return g
    if k == 'conv_w':
        return jnp.transpose(g[:, :CONV_WIDTH, :D_CONV // N_DEV], (1, 0, 2)).reshape(CONV_WIDTH, D_CONV)
    return g.reshape(N_DEV * g.shape[1], g.shape[2])


def grad_shards(k, g):
    if k in FFN_COL:
        return g
    if k == 'conv_w':
        s = jnp.transpose(g.reshape(CONV_WIDTH, N_DEV, D_CONV // N_DEV), (1, 0, 2))
        return jnp.pad(s, ((0, 0), (0, CONV_W_COMM[0] - CONV_WIDTH), (0, CONV_W_COMM[1] - D_CONV // N_DEV)))
    return g.reshape(N_DEV, g.shape[0] // N_DEV, g.shape[1])


def _flip(v, bit):
    return 1 - v if bit else v


N_PEER_COPIES = N_DEV - 1


def _comm_call(name, body, arrs, out_shape):
    n = len(arrs)
    return pl.pallas_call(
        functools.partial(body, n), name=name, out_shape=out_shape,
        in_specs=[pl.BlockSpec(memory_space=pl.ANY)] * n, out_specs=[pl.BlockSpec(memory_space=pl.ANY)] * n,
        scratch_shapes=[pltpu.SemaphoreType.DMA((n * N_PEER_COPIES,)), pltpu.SemaphoreType.DMA((n * N_PEER_COPIES,)),
                        pltpu.SemaphoreType.DMA((n,))],
    )(*arrs)


def _blk(ref, idx, by_cols):
    if not by_cols:
        return ref.at[idx]
    c = ref.shape[1] // N_DEV
    return ref.at[:, pl.ds(pl.multiple_of(idx * c, LANES), c)]


def _blocked_shape(a, by_cols):
    return (a.shape[0], N_DEV * a.shape[1]) if by_cols else (N_DEV,) + a.shape


def all_gather(arrs, by_cols):
    def body(n, *refs):
        x_refs, out_refs, (send_sems, recv_sems, local_sems) = refs[:n], refs[n:2 * n], refs[2 * n:]
        x, y, c = lax.axis_index("x"), lax.axis_index("y"), lax.axis_index("c")
        me, sibling = (x, y, c), (x, y, 1 - c)
        chips = [(1 - x, y), (x, 1 - y), (1 - x, 1 - y)]

        def copy(a, k, block, to, src=None):
            px, py, pc = block
            dst = _blk(out_refs[a], 4 * px + 2 * py + pc, by_cols[a])
            return pltpu.make_async_remote_copy(
                src_ref=dst if src is None else src, dst_ref=dst, send_sem=send_sems.at[a * N_PEER_COPIES + k],
                recv_sem=recv_sems.at[a * N_PEER_COPIES + k], device_id=to, device_id_type=MESH)

        mine = [pltpu.make_async_copy(x_refs[a], _blk(out_refs[a], 4 * x + 2 * y + c, by_cols[a]), local_sems.at[a])
                for a in range(n)]
        started = []
        for a in range(n):
            mine[a].start()
            first = [copy(a, 0, me, sibling, src=x_refs[a])]
            first += [copy(a, 1 + j, me, (*chip, c), src=x_refs[a]) for j, chip in enumerate(chips)]
            for cp in first:
                cp.start()
            started += first
        for j, chip in enumerate(chips):
            for a in range(n):
                copy(a, 1 + j, (*chip, c), me).wait_recv()
                cp = copy(a, 4 + j, (*chip, c), sibling)
                cp.start()
                started.append(cp)
        for a in range(n):
            copy(a, 0, sibling, me).wait_recv()
            for j, chip in enumerate(chips):
                copy(a, 4 + j, (*chip, 1 - c), me).wait_recv()
        for cp in started:
            cp.wait_send()
        for cp in mine:
            cp.wait()

    return _comm_call("all_gather_weights", body, arrs,
                      [jax.ShapeDtypeStruct(_blocked_shape(a, bc), a.dtype) for a, bc in zip(arrs, by_cols)])


def _landing_shape(a, by_cols):
    return (N_DEV, a.shape[0], a.shape[1] // N_DEV) if by_cols else a.shape


_HBM = pl.BlockSpec(memory_space=pltpu.HBM)
_SEM = pl.BlockSpec(memory_space=pltpu.SEMAPHORE)
_EFFECT = pltpu.SideEffectType.DATAFLOW_SIDE_EFFECTING


def _peer(k):
    x, y, c = lax.axis_index("x"), lax.axis_index("y"), lax.axis_index("c")
    return _flip(x, k & 4), _flip(y, k & 2), _flip(c, k & 1)


def _my_index():
    return 4 * lax.axis_index("x") + 2 * lax.axis_index("y") + lax.axis_index("c")


def _split_copies(mode, by_cols, src_refs, land_refs, send_sems, recv_sems):
    me = _my_index()
    out = []
    for a, bc in enumerate(by_cols):
        for k in range(1, N_DEV):
            px, py, pc = _peer(k)
            src = _blk(src_refs[a], 4 * px + 2 * py + pc, bc) if mode == 'scatter' else src_refs[a]
            dst = land_refs[a].at[me] if mode == 'scatter' else _blk(land_refs[a], me, bc)
            out.append(pltpu.make_async_remote_copy(
                src_ref=src, dst_ref=dst, send_sem=send_sems.at[a * N_PEER_COPIES + k - 1],
                recv_sem=recv_sems.at[a * N_PEER_COPIES + k - 1], device_id=(px, py, pc), device_id_type=MESH))
    return out


def exchange_start(name, mode, srcs, by_cols):
    n = len(srcs)
    lands = [lax.empty(_landing_shape(s, bc) if mode == 'scatter' else _blocked_shape(s, bc), s.dtype)
             for s, bc in zip(srcs, by_cols)]

    def body(*refs):
        src_refs, land_refs, send_sems, recv_sems = refs[:n], refs[n:2 * n], refs[2 * n], refs[2 * n + 1]
        for cp in _split_copies(mode, by_cols, src_refs, land_refs, send_sems, recv_sems):
            cp.start()
        refs[-1][...] = jnp.zeros(refs[-1].shape, F32)

    sems = pltpu.SemaphoreType.DMA((n * N_PEER_COPIES,))
    res = pl.pallas_call(
        body, name=name,
        out_shape=(sems, sems, *[pltpu.HBM(a.shape, a.dtype) for a in srcs + lands], jax.ShapeDtypeStruct((8, LANES), F32)),
        in_specs=(_HBM,) * (2 * n), out_specs=(_SEM, _SEM, *(_HBM,) * (2 * n), pl.BlockSpec(memory_space=pltpu.VMEM)),
        input_output_aliases={i: 2 + i for i in range(2 * n)},
        compiler_params=pltpu.CompilerParams(has_side_effects=_EFFECT),
    )(*[pltpu.with_memory_space_constraint(a, pltpu.HBM) for a in srcs + lands])
    return (mode, by_cols, res[:-1]), res[-1]


def exchange_wait(name, handles, after):
    mode, by_cols, (send_sems, recv_sems, *bufs) = handles
    n = len(by_cols)

    def body(*refs):
        src_refs, land_refs, s_sems, r_sems = refs[:n], refs[n:2 * n], refs[2 * n], refs[2 * n + 1]
        for cp in _split_copies(mode, by_cols, src_refs, land_refs, s_sems, r_sems):
            cp.wait_send()
            cp.wait_recv()

    res = pl.pallas_call(
        body, name=name, out_shape=tuple(pltpu.HBM(a.shape, a.dtype) for a in bufs),
        in_specs=(*(_HBM,) * (2 * n), _SEM, _SEM, pl.BlockSpec(memory_space=pl.ANY)), out_specs=(_HBM,) * (2 * n),
        input_output_aliases={i: i for i in range(2 * n)},
        compiler_params=pltpu.CompilerParams(has_side_effects=_EFFECT),
    )(*bufs, send_sems, recv_sems, after)
    me, out = _my_index(), []
    for src, land, bc in zip(res[:n], res[n:], by_cols):
        if mode == 'scatter':
            c = land.shape[2]
            own = lax.dynamic_slice(src, (0, me * c), (src.shape[0], c)) if bc else lax.dynamic_index_in_dim(src, me, 0, False)
            out.append(lax.dynamic_update_slice(land, own[None], (me, 0, 0)))
        elif bc:
            out.append(lax.dynamic_update_slice(land, src, (0, me * src.shape[1])))
        else:
            out.append(lax.dynamic_update_slice(land, src[None], (me, 0, 0)))
    return out


def _adamw_math(g, w, m, v):
    c1 = 1.0 / (1.0 - ADAM_B1 ** ADAM_STEP)
    c2 = 1.0 / (1.0 - ADAM_B2 ** ADAM_STEP)
    m = ADAM_B1 * m + (1.0 - ADAM_B1) * g
    v = ADAM_B2 * v + (1.0 - ADAM_B2) * jnp.square(g)
    return g, -ADAM_LR * ((m * c1) / (jnp.sqrt(v * c2) + ADAM_EPS) + ADAM_WD * w), m, v


def adamw(name, recv, w, m, v, tm):
    _, rows, cols = w.shape
    tm = min(tm, rows)

    def body(*refs):
        g = refs[0][0:tm, 0:cols].astype(F32)
        for s in range(1, N_DEV):
            g = g + refs[s][0:tm, 0:cols].astype(F32)
        res = _adamw_math(g, *[r[...] for r in refs[N_DEV:N_DEV + 3]])
        for r, val in zip(refs[N_DEV + 3:], res):
            r[...] = val

    part = lambda s: pl.BlockSpec((None, recv.shape[1] if tm == rows else tm, recv.shape[2]), lambda i: (s, i, 0))
    tile = pl.BlockSpec((None, tm, cols), lambda i: (0, i, 0))
    return pl.pallas_call(
        body, name=name, grid=(rows // tm,), in_specs=[part(s) for s in range(N_DEV)] + [tile] * 3, out_specs=[tile] * 4,
        out_shape=[jax.ShapeDtypeStruct((1, rows, cols), F32)] * 4, compiler_params=_cparams(("parallel",)),
    )(*[recv] * N_DEV, w, m, v)


def adamw_small(recv, wl, ml, vl):
    n = len(REPLICATED)

    def body(recv_ref, *refs):
        g = recv_ref[0]
        for s in range(1, N_DEV):
            g = g + recv_ref[s]
        for r in range(n):
            w, m, v = (refs[j * n + r][...] for j in range(3))
            for j, val in enumerate(_adamw_math(g[r:r + 1, :w.shape[1]], w, m, v)):
                refs[(3 + j) * n + r][...] = val

    arrs = [d[k].reshape(1, -1) for d in (wl, ml, vl) for k in REPLICATED]
    res = pl.pallas_call(
        body, name="adamw_small", out_shape=[jax.ShapeDtypeStruct(a.shape, F32) for a in arrs[:n]] * 4,
    )(recv, *arrs)
    return [{k: res[j * n + r].reshape(wl[k].shape) for r, k in enumerate(REPLICATED)} for j in range(4)]


ADAMW_TM = {'ffn1_gate': 256, 'ffn1_up': 256, 'ffn2_gate': 256, 'ffn2_up': 256, 'w_in': 32}


def kernel(x, positions, ln1_g, ln1_b, ffn1_gate, ffn1_up, ffn1_down, w_in, conv_w, conv_b, dt_bias, a_log, d_skip, attn_norm_w, ssd_norm_w, w_out, ln2_g, ln2_b, ffn2_gate, ffn2_up, ffn2_down, ln3_g, ln3_b, loss_target, m_ln1_g, m_ln1_b, m_ffn1_gate, m_ffn1_up, m_ffn1_down, m_w_in, m_conv_w, m_conv_b, m_dt_bias, m_a_log, m_d_skip, m_attn_norm_w, m_ssd_norm_w, m_w_out, m_ln2_g, m_ln2_b, m_ffn2_gate, m_ffn2_up, m_ffn2_down, m_ln3_g, m_ln3_b, v_ln1_g, v_ln1_b, v_ffn1_gate, v_ffn1_up, v_ffn1_down, v_w_in, v_conv_w, v_conv_b, v_dt_bias, v_a_log, v_d_skip, v_attn_norm_w, v_ssd_norm_w, v_w_out, v_ln2_g, v_ln2_b, v_ffn2_gate, v_ffn2_up, v_ffn2_down, v_ln3_g, v_ln3_b):
    args = dict(locals())
    wl = {k: args[k] for k in WEIGHTS}
    ml = {k: args["m_" + k] for k in WEIGHTS}
    vl = {k: args["v_" + k] for k in WEIGHTS}
    shapes = {k: wl[k].shape for k in WEIGHTS}
    b, s, dm = x.shape
    t = b * s

    sent = {k: to_comm(k, wl, shapes).astype(F32 if k == 'conv_w' else BF16) for k in SHARDED}
    by_cols = lambda keys: [k in FFN_COL for k in keys]
    gate, up = all_gather([sent['ffn1_gate'], sent['ffn1_up']], [True] * 2)
    (gate, up), sent = lax.optimization_barrier(((gate, up), sent))
    p = {'ffn1_gate': gate, 'ffn1_up': up}
    gather_down, token_d = exchange_start("gather_ffn1_down_start", 'gather', [sent['ffn1_down']], [False])
    sent['w_in'] = sent['w_in'] + token_d[0, 0].astype(BF16)
    gather_mixer, token_m = exchange_start("gather_mixer_start", 'gather', [sent[k] for k in MIXER_KEYS], by_cols(MIXER_KEYS))
    sent['ffn2_gate'] = sent['ffn2_gate'] + token_m[0, 0].astype(BF16)
    gather_ffn2, token_f = exchange_start("gather_ffn2_start", 'gather', [sent[k] for k in FFN2_KEYS], by_cols(FFN2_KEYS))
    for k in REPLICATED:
        p[k] = wl[k].reshape(1, -1)

    x2 = x.reshape(t, dm)
    cosv, sinv = rope_tables(positions)
    g1, u1, a1, at1 = ffn_gate_up("ffn1_gate_up", x2, p['ffn1_gate'], p['ffn1_up'], after=(token_d, token_m, token_f))
    p['ffn1_down'] = full_weight('ffn1_down', exchange_wait("gather_ffn1_down_wait", gather_down, a1)[0])
    f1, res1 = mm("ffn1_down", [(a1, p['ffn1_down'], 'nn')], D_MODEL, out_dtype=BF16), (x2, g1, u1, at1)
    h1, h1b = resid_ln_fwd("ln1", 0.5, x2, f1, p['ln1_g'], p['ln1_b'])
    for k, g in zip(MIXER_KEYS, exchange_wait("gather_mixer_wait", gather_mixer, h1b)):
        p[k] = full_weight(k, g)
    mix, resm = mixer_fwd(h1b, p, cosv, sinv, b)
    h2, h2b = resid_ln_fwd("ln2", 1.0, h1, mix, p['ln2_g'], p['ln2_b'])
    for k, g in zip(FFN2_KEYS, exchange_wait("gather_ffn2_wait", gather_ffn2, h2b)):
        p[k] = full_weight(k, g)
    f2, res3 = ffn_fwd("ffn2", h2b, p['ffn2_gate'], p['ffn2_up'], p['ffn2_down'])

    small, full = {}, {}
    dh2_res, df2, small['ln3_g'], small['ln3_b'], sq = ln_loss_bwd("ln3_loss_bwd", h2, f2, loss_target.reshape(t, dm),
                                                                   p['ln3_g'], p['ln3_b'])
    loss = lax.psum(jnp.sum(sq) * (0.5 / dm), AXES)

    dh2, full['ffn2_gate'], full['ffn2_up'], full['ffn2_down'] = ffn_bwd("ffn2", res3, p['ffn2_gate'], p['ffn2_up'],
                                                                       p['ffn2_down'], df2, dh2_res)
    ffn2_exchange, token = exchange_start("grads_ffn2_start", 'scatter', [grad_shards(k, full[k]) for k in FFN2_KEYS],
                                          by_cols(FFN2_KEYS))
    dh1_res, dmix, small['ln2_g'], small['ln2_b'] = resid_ln_bwd("ln2_bwd", 1.0, h1, mix, p['ln2_g'] + token[:1, :1],
                                                                 p['ln2_b'], dh2)
    dh1, gm = mixer_bwd(resm, p, dmix, dh1_res, b)
    for k in ('conv_b', 'dt_bias', 'a_log', 'd_skip', 'attn_norm_w', 'ssd_norm_w'):
        small[k] = gm[k]
    mixer_exchange, token = exchange_start("grads_mixer_start", 'scatter', [grad_shards(k, gm[k]) for k in MIXER_KEYS],
                                           by_cols(MIXER_KEYS))
    dx_res, df1, small['ln1_g'], small['ln1_b'] = resid_ln_bwd("ln1_bwd", 0.5, x2, f1, p['ln1_g'] + token[:1, :1],
                                                               p['ln1_b'], dh1)
    hb, g, u, at = res1
    small_part = pack_small(small)
    dg, du = ffn_da_act("ffn1_bwd_da_act", df1, p['ffn1_down'], g, u)
    dwd = mm_acc("ffn1_bwd_dwd", at, df1, BF16, after=dg)
    down_exchange, token = exchange_start("grads_ffn1_down_start", 'scatter', [
        grad_shards('ffn1_down', dwd), jnp.broadcast_to(small_part[None], (N_DEV,) + small_part.shape)], [False, False])
    dwg = mm_tn("ffn1_bwd_dwg", hb, dg, BF16, after=token)
    gate_exchange, token = exchange_start("grads_ffn1_gate_start", 'scatter', [grad_shards('ffn1_gate', dwg)], [True])
    dwu = mm_tn("ffn1_bwd_dwu", hb, du, BF16, after=token)
    up_exchange, token = exchange_start("grads_ffn1_up_start", 'scatter', [grad_shards('ffn1_up', dwu)], [True])
    dx = mm("ffn1_bwd_dh", [(dg, p['ffn1_gate'], 'nt'), (du, p['ffn1_up'], 'nt')], D_MODEL, add=dx_res, tn=512, after=token)
    recv = {}
    for keys, name, ex in (((FFN2_KEYS), "grads_ffn2_wait", ffn2_exchange), (MIXER_KEYS, "grads_mixer_wait", mixer_exchange),
                           (('ffn1_down', SMALL), "grads_ffn1_down_wait", down_exchange),
                           (('ffn1_gate',), "grads_ffn1_gate_wait", gate_exchange),
                           (('ffn1_up',), "grads_ffn1_up_wait", up_exchange)):
        recv.update(zip(keys, exchange_wait(name, ex, dx)))
    outs = adamw_small(recv.pop(SMALL), wl, ml, vl)
    for k, r in recv.items():
        for o, a in zip(outs, adamw(f"adamw_{k}", r, wl[k], ml[k], vl[k], ADAMW_TM.get(k, shapes[k][1]))):
            o[k] = a
    return (loss, dx.reshape(b, s, dm), *[o[k] for o in outs for k in WEIGHTS])
```

```python
import functools

import jax
import jax.numpy as jnp
import numpy as np
from jax import lax
from jax.experimental import pallas as pl
from jax.experimental.pallas import tpu as pltpu

F32, BF16 = jnp.float32, jnp.bfloat16
HI = lax.Precision.HIGHEST
MESH = pl.DeviceIdType.MESH
AXES = ("x", "y", "c")
N_DEV = 8

D_MODEL = 1024
SEQ = 2048
HEAD_DIM = 64
N_HEADS = 12
D_ATTN = N_HEADS * HEAD_DIM
DILATIONS = (1, 4, 16)
ATTN_BLOCK = 128
ROPE_THETA = 500000.0
ROPE_DIM = 16
D_SSD = 768
SSD_GROUPS = 4
SSD_STATE = 128
SSD_CHUNK = 128
D_BC = SSD_GROUPS * SSD_STATE
D_CONV = D_SSD + 2 * D_BC
CONV_WIDTH = 4
D_QKVZ = 3 * D_ATTN + D_SSD
D_FF = 2816
ALPHA = 2.0 ** 0.25
LN_EPS = 1e-5
RMS_EPS = 1e-6
ADAM_LR, ADAM_B1, ADAM_B2, ADAM_EPS, ADAM_WD, ADAM_STEP = 0.001, 0.9, 0.999, 1e-08, 0.01, 10

LANES = 128
VMEM_LIMIT = 52 * 1024 * 1024
NEG = -1e30

WEIGHTS = ['ln1_g', 'ln1_b', 'ffn1_gate', 'ffn1_up', 'ffn1_down', 'w_in', 'conv_w', 'conv_b', 'dt_bias', 'a_log',
           'd_skip', 'attn_norm_w', 'ssd_norm_w', 'w_out', 'ln2_g', 'ln2_b', 'ffn2_gate', 'ffn2_up', 'ffn2_down',
           'ln3_g', 'ln3_b']
COL_SHARDED = ('ffn1_gate', 'ffn1_up', 'conv_w', 'ffn2_gate', 'ffn2_up')
ROW_SHARDED = ('ffn1_down', 'w_in', 'w_out', 'ffn2_down')
SHARDED = tuple(n for n in WEIGHTS if n in COL_SHARDED or n in ROW_SHARDED)
REPLICATED = tuple(n for n in WEIGHTS if n not in SHARDED)
FF_SHARD = D_FF // N_DEV
FF_PAD = -(-FF_SHARD // LANES) * LANES


def _cparams(sem=None):
    return pltpu.CompilerParams(dimension_semantics=sem, vmem_limit_bytes=VMEM_LIMIT)


def _tile(n, prefs):
    for p in prefs:
        if n % p == 0:
            return p
    return n


class Op:
    def __init__(self, arr, bw=None, cb=0, ro=0):
        self.arr, self.bw, self.cb, self.ro = arr, (arr.shape[1] if bw is None else bw), cb, ro


def _op(a):
    return a if isinstance(a, Op) else Op(a)


def rowwise(name, fn, ins, consts, outs, accs=(), tm=512):
    ins = [_op(a) for a in ins]
    rows = outs[0][0]
    n_in, n_c, n_o, n_a = len(ins), len(consts), len(outs), len(accs)
    tm = min(tm, rows)
    assert rows % tm == 0, (name, rows, tm)

    def body(*refs):
        vals = [r[...].astype(F32) for r in refs[:n_in + n_c]]
        res = fn(*vals)
        res = res if isinstance(res, (tuple, list)) else (res,)
        o_refs = refs[n_in + n_c:n_in + n_c + n_o]
        a_refs = refs[n_in + n_c + n_o:]
        for r, v in zip(o_refs, res[:n_o]):
            r[...] = v.astype(r.dtype)
        if n_a:
            @pl.when(pl.program_id(0) == 0)
            def _():
                for r in a_refs:
                    r[...] = jnp.zeros(r.shape, r.dtype)
            for r, v in zip(a_refs, res[n_o:]):
                r[...] += v

    in_specs = [pl.BlockSpec((tm, o.bw), functools.partial(lambda i, o: (i + o.ro, o.cb), o=o)) for o in ins]
    in_specs += [pl.BlockSpec(c.shape, functools.partial(lambda i, nd: (0,) * nd, nd=c.ndim)) for c in consts]
    out_specs = [pl.BlockSpec((tm, w), lambda i: (i, 0)) for (_, w, _) in outs]
    out_specs += [pl.BlockSpec(s, functools.partial(lambda i, nd: (0,) * nd, nd=len(s))) for s in accs]
    out_shape = [jax.ShapeDtypeStruct((r, w), dt) for (r, w, dt) in outs]
    out_shape += [jax.ShapeDtypeStruct(s, F32) for s in accs]
    res = pl.pallas_call(
        body, name=name, grid=(rows // tm,), in_specs=in_specs, out_specs=out_specs, out_shape=out_shape,
        compiler_params=_cparams(("arbitrary",) if n_a else ("parallel",)),
    )(*[o.arr for o in ins], *consts)
    return res


MM_TM = 1024
MM_TN = (1024, 896, 768, 512, 256, 128)
_NT = (((1,), (1,)), ((), ()))
_NN = (((1,), (0,)), ((), ()))
_TN = (((0,), (0,)), ((), ()))


def _dot(a, b, dn, precision=None):
    return lax.dot_general(a, b, dn, preferred_element_type=F32, precision=precision)


def _mm_specs(name, pairs, n_out, tm, tn):
    in_specs, args = [], []
    for a, b, mode in pairs:
        o = _op(a)
        in_specs.append(pl.BlockSpec((tm, o.bw), functools.partial(lambda j, i, o: (i, o.cb), o=o)))
        args.append(o.arr)
        if mode == 'nn':
            assert b.shape == (o.bw, n_out), (name, b.shape, o.bw, n_out)
            in_specs.append(pl.BlockSpec((o.bw, tn), lambda j, i: (0, j)))
        else:
            assert b.shape == (n_out, o.bw), (name, b.shape, o.bw, n_out)
            in_specs.append(pl.BlockSpec((tn, o.bw), lambda j, i: (j, 0)))
        args.append(b)
    return in_specs, args


def _mm_acc(refs, pairs):
    acc = None
    for k, (_, _, mode) in enumerate(pairs):
        d = _dot(refs[2 * k][...].astype(BF16), refs[2 * k + 1][...].astype(BF16), _NN if mode == 'nn' else _NT)
        acc = d if acc is None else acc + d
    return acc


def mm(name, pairs, n_out, add=None, out_dtype=F32, tm=MM_TM, tn=None, after=None):
    m = _op(pairs[0][0]).arr.shape[0]
    tn = tn or _tile(n_out, MM_TN)
    n_p = len(pairs)

    def body(*refs):
        acc = _mm_acc(refs, pairs)
        if add is not None:
            acc = acc + refs[2 * n_p][...]
        refs[-1][...] = acc.astype(refs[-1].dtype)

    in_specs, args = _mm_specs(name, pairs, n_out, tm, tn)
    tile = pl.BlockSpec((tm, tn), lambda j, i: (i, j))
    if add is not None:
        in_specs.append(tile)
        args.append(add)
    if after is not None:
        in_specs.append(pl.BlockSpec(memory_space=pl.ANY))
        args.append(after)
    return pl.pallas_call(
        body, name=name, grid=(n_out // tn, m // tm), in_specs=in_specs, out_specs=tile,
        out_shape=jax.ShapeDtypeStruct((m, n_out), out_dtype),
        compiler_params=_cparams(("parallel", "parallel")),
    )(*args)


def mm_tn(name, a, b, out_dtype=F32, tt=1024, after=None):
    a, b = _op(a), _op(b)
    t = a.arr.shape[0]
    k, n = a.bw, b.bw
    tk = _tile(k, (512, 896, 768, 256, 128))
    tn = _tile(n, (3072, 1792) + MM_TN)
    tt = min(tt, t)
    n_t = t // tt
    order = [] if after is None else [after]

    def body(a_ref, b_ref, *rest):
        o_ref, acc_ref = rest[-2:]
        s = pl.program_id(2)
        d = _dot(a_ref[...].astype(BF16), b_ref[...].astype(BF16), _TN)

        @pl.when(s == 0)
        def _():
            acc_ref[...] = d

        @pl.when(s > 0)
        def _():
            acc_ref[...] += d

        @pl.when(s == n_t - 1)
        def _():
            o_ref[...] = acc_ref[...].astype(o_ref.dtype)

    return pl.pallas_call(
        body, name=name, grid=(k // tk, n // tn, n_t),
        in_specs=[pl.BlockSpec((tt, tk), functools.partial(lambda kk, nn, s, o: (s, o.cb * (o.bw // tk) + kk), o=a)),
                  pl.BlockSpec((tt, tn), functools.partial(lambda kk, nn, s, o: (s, o.cb * (o.bw // tn) + nn), o=b))]
        + [pl.BlockSpec(memory_space=pl.ANY) for _ in order],
        out_specs=pl.BlockSpec((tk, tn), lambda kk, nn, s: (kk, nn)),
        out_shape=jax.ShapeDtypeStruct((k, n), out_dtype),
        scratch_shapes=[pltpu.VMEM((tk, tn), F32)],
        compiler_params=_cparams(("parallel", "parallel", "arbitrary")),
    )(a.arr, b.arr, *order)


def _sigmoid(x):
    return 1.0 / (1.0 + jnp.exp(-x))


def _silu(x):
    return x * _sigmoid(x)


def _softplus(x):
    return jnp.maximum(x, 0.0) + jnp.log(1.0 + jnp.exp(-jnp.abs(x)))


def _resid_ln(scale, h, branch, g, b):
    r = ALPHA * h + scale * branch
    mu = jnp.mean(r, axis=-1, keepdims=True)
    var = jnp.mean(jnp.square(r - mu), axis=-1, keepdims=True)
    return (r - mu) * lax.rsqrt(var + LN_EPS) * g + b


def _rms(t, w):
    return t * lax.rsqrt(jnp.mean(t * t, axis=-1, keepdims=True) + RMS_EPS) * w


def _branch_weights(l1, l2, l3):
    m = jnp.maximum(jnp.maximum(l1, l2), l3)
    e1, e2, e3 = jnp.exp(l1 - m), jnp.exp(l2 - m), jnp.exp(l3 - m)
    inv = 1.0 / (e1 + e2 + e3)
    return e1 * inv, e2 * inv, e3 * inv


def _gate(y, xs, z, dskip, w):
    return _rms((y + dskip * xs) * _silu(z), w)


def _rot(x):
    d = lax.broadcasted_iota(jnp.int32, x.shape, 1) % HEAD_DIM
    up = pltpu.roll(x, x.shape[1] - ROPE_DIM // 2, 1)
    down = jnp.where(d < ROPE_DIM, pltpu.roll(x, ROPE_DIM // 2, 1), 0.0)
    return jnp.where(d < ROPE_DIM // 2, up, down)


def ffn_gate_up(name, h, wg, wu, after=()):
    m, nf = h.shape[0], wg.shape[1]
    tn = _tile(nf, MM_TN)

    def body(h_ref, g_w, u_w, *rest):
        du_ref, dg_ref, a_ref, at_ref = rest[-4:]
        hb = h_ref[...].astype(BF16)
        g = _dot(hb, g_w[...].astype(BF16), _NN)
        u = _dot(hb, u_w[...].astype(BF16), _NN)
        sig = _sigmoid(g)
        gs = g * sig
        du_ref[...] = gs.astype(du_ref.dtype)
        dg_ref[...] = (u * (sig + gs * (1.0 - sig))).astype(dg_ref.dtype)
        a = gs * u
        a_ref[...] = a.astype(a_ref.dtype)
        at_ref[...] = a.T.astype(at_ref.dtype)

    in_specs, args = _mm_specs(name, [(h, wg, 'nn')], nf, MM_TM, tn)
    in_specs.append(in_specs[1])
    in_specs += [pl.BlockSpec(memory_space=pl.ANY) for _ in after]
    tile = pl.BlockSpec((MM_TM, tn), lambda j, i: (i, j))
    return pl.pallas_call(
        body, name=name, grid=(nf // tn, m // MM_TM), in_specs=in_specs,
        out_specs=[tile] * 3 + [pl.BlockSpec((tn, MM_TM), lambda j, i: (j, i))],
        out_shape=[jax.ShapeDtypeStruct((m, nf), BF16)] * 3 + [jax.ShapeDtypeStruct((nf, m), BF16)],
        compiler_params=_cparams(("parallel", "parallel")),
    )(*args, wu, *after)


def mm_tn_cat(name, a, bs, out_dtype=F32, tt=1024):
    t, k = a.shape
    widths = [b.shape[1] for b in bs]
    n, tk, tt = sum(widths), _tile(k, (512, 256, 128)), min(tt, t)
    n_t = t // tt

    def body(a_ref, *rest):
        b_refs, o_ref, acc_ref = rest[:len(bs)], rest[-2], rest[-1]
        s = pl.program_id(1)
        at = a_ref[...].astype(BF16)
        d = jnp.concatenate([_dot(at, b[...].astype(BF16), _TN) for b in b_refs], axis=1)

        @pl.when(s == 0)
        def _():
            acc_ref[...] = d

        @pl.when(s > 0)
        def _():
            acc_ref[...] += d

        @pl.when(s == n_t - 1)
        def _():
            o_ref[...] = acc_ref[...].astype(o_ref.dtype)

    return pl.pallas_call(
        body, name=name, grid=(k // tk, n_t),
        in_specs=[pl.BlockSpec((tt, tk), lambda kk, s: (s, kk))] + [pl.BlockSpec((tt, w), lambda kk, s: (s, 0)) for w in widths],
        out_specs=pl.BlockSpec((tk, n), lambda kk, s: (kk, 0)),
        out_shape=jax.ShapeDtypeStruct((k, n), out_dtype), scratch_shapes=[pltpu.VMEM((tk, n), F32)],
        compiler_params=_cparams(("parallel", "arbitrary")),
    )(a, *bs)


def mm_acc(name, a, b, out_dtype=F32, tt=1024, after=None):
    k, t = a.shape
    n = b.shape[1]
    tk, tn, tt = _tile(k, (1024, 512, 256, 128)), _tile(n, MM_TN), min(tt, t)
    n_t = t // tt
    order = [] if after is None else [after]

    def body(a_ref, b_ref, *rest):
        o_ref, acc_ref = rest[-2:]
        s = pl.program_id(2)
        d = _dot(a_ref[...].astype(BF16), b_ref[...].astype(BF16), _NN)

        @pl.when(s == 0)
        def _():
            acc_ref[...] = d

        @pl.when(s > 0)
        def _():
            acc_ref[...] += d

        @pl.when(s == n_t - 1)
        def _():
            o_ref[...] = acc_ref[...].astype(o_ref.dtype)

    return pl.pallas_call(
        body, name=name, grid=(k // tk, n // tn, n_t),
        in_specs=[pl.BlockSpec((tk, tt), lambda kk, nn, s: (kk, s)), pl.BlockSpec((tt, tn), lambda kk, nn, s: (s, nn))]
        + [pl.BlockSpec(memory_space=pl.ANY) for _ in order],
        out_specs=pl.BlockSpec((tk, tn), lambda kk, nn, s: (kk, nn)),
        out_shape=jax.ShapeDtypeStruct((k, n), out_dtype), scratch_shapes=[pltpu.VMEM((tk, tn), F32)],
        compiler_params=_cparams(("parallel", "parallel", "arbitrary")),
    )(a, b, *order)


def ffn_da_act(name, df, wd, a_du, a_dg):
    m, nf = df.shape[0], wd.shape[0]
    tn = _tile(nf, MM_TN)

    def body(df_ref, w_ref, adu_ref, adg_ref, dg_ref, du_ref):
        da = _dot(df_ref[...].astype(BF16), w_ref[...].astype(BF16), _NT)
        dg_ref[...] = (da * adg_ref[...].astype(F32)).astype(dg_ref.dtype)
        du_ref[...] = (da * adu_ref[...].astype(F32)).astype(du_ref.dtype)

    in_specs, args = _mm_specs(name, [(df, wd, 'nt')], nf, MM_TM, tn)
    tile = pl.BlockSpec((MM_TM, tn), lambda j, i: (i, j))
    return pl.pallas_call(
        body, name=name, grid=(nf // tn, m // MM_TM), in_specs=in_specs + [tile, tile], out_specs=[tile] * 2,
        out_shape=[jax.ShapeDtypeStruct((m, nf), BF16)] * 2, compiler_params=_cparams(("parallel", "parallel")),
    )(*args, a_du, a_dg)


def resid_ln_fwd(name, scale, h, branch, ln_g, ln_b):
    t = h.shape[0]

    def fn(*a):
        y = _resid_ln(scale, *a)
        return y, y

    return rowwise(name, fn, [h, branch], [ln_g, ln_b], [(t, D_MODEL, F32), (t, D_MODEL, BF16)], tm=1024)


def ffn_fwd(tag, hb, wg, wu, wd, after=()):
    g, u, a, at = ffn_gate_up(f"{tag}_gate_up", hb, wg, wu, after)
    f = mm(f"{tag}_down", [(a, wd, 'nn')], D_MODEL, out_dtype=BF16)
    return f, (hb, g, u, at)


def ln_loss_bwd(name, h, branch, target, ln_g, ln_b):
    t, dm = h.shape

    def fn(h_, br_, tgt, g_, b_):
        y, vjp = jax.vjp(functools.partial(_resid_ln, 0.5), h_, br_, g_, b_)
        e = y - tgt
        return (*vjp(e * (1.0 / dm)), jnp.sum(e * e, axis=0, keepdims=True))

    return rowwise(name, fn, [h, branch, target], [ln_g, ln_b], [(t, dm, F32), (t, dm, BF16)],
                   accs=[(1, dm), (1, dm), (1, dm)], tm=1024)


def resid_ln_bwd(name, scale, h, branch, ln_g, ln_b, dout, extra=None):
    t = h.shape[0]

    def fn(h_, br_, do_, *rest):
        g_, b_ = rest[-2], rest[-1]
        _, vjp = jax.vjp(functools.partial(_resid_ln, scale), h_, br_, g_, b_)
        dh, dbr, dg, db = vjp(do_)
        if extra is not None:
            dh = dh + rest[0]
        return dh, dbr, dg, db

    ins = [h, branch, dout] + ([extra] if extra is not None else [])
    return rowwise(name, fn, ins, [ln_g, ln_b], [(t, D_MODEL, F32), (t, D_MODEL, BF16)],
                   accs=[(1, D_MODEL), (1, D_MODEL)], tm=1024)


def ffn_bwd(tag, res, wg, wu, wd, df, dh_resid):
    hb, g, u, at = res
    dg, du = ffn_da_act(f"{tag}_bwd_da_act", df, wd, g, u)
    dwd = mm_acc(f"{tag}_bwd_dwd", at, df, BF16)
    dh = mm(f"{tag}_bwd_dh", [(dg, wg, 'nt'), (du, wu, 'nt')], D_MODEL, add=dh_resid, tn=512)
    dwg = mm_tn(f"{tag}_bwd_dwg", hb, dg, BF16)
    dwu = mm_tn(f"{tag}_bwd_dwu", hb, du, BF16)
    return dh, dwg, dwu, dwd


def rope_tables(positions):
    inv_freq = ROPE_THETA ** (-jnp.arange(0, ROPE_DIM, 2, dtype=F32) / ROPE_DIM)
    ang = positions.reshape(-1, 1).astype(F32) * inv_freq
    c, s = jnp.cos(ang), jnp.sin(ang)
    t = ang.shape[0]
    cosv = jnp.concatenate([c, c, jnp.ones((t, HEAD_DIM - ROPE_DIM), F32)], axis=1)
    sinv = jnp.concatenate([-s, s, jnp.zeros((t, HEAD_DIM - ROPE_DIM), F32)], axis=1)
    return jnp.tile(cosv, (1, 2)), jnp.tile(sinv, (1, 2))


def _pair_masks():
    lane = lax.broadcasted_iota(jnp.int32, (1, LANES), 1)
    return (lane < HEAD_DIM, lane >= HEAD_DIM)


def _band_masks():
    row = lax.broadcasted_iota(jnp.int32, (ATTN_BLOCK, ATTN_BLOCK), 0)
    col = lax.broadcasted_iota(jnp.int32, (ATTN_BLOCK, ATTN_BLOCK), 1)
    return col >= row, col <= row


def _residue_blocks():
    out = []
    for g, d in enumerate(DILATIONS):
        for r in range(d):
            for i in range(SEQ // d // ATTN_BLOCK):
                rows = lambda j: pl.ds(r + j * ATTN_BLOCK * d, ATTN_BLOCK, stride=d) if d > 1 else pl.ds(j * ATTN_BLOCK, ATTN_BLOCK)
                out.append((g, rows(i), rows(i - 1) if i > 0 else None))
    return out


N_HEAD_PAIRS = D_ATTN // LANES
SCALE = HEAD_DIM ** -0.5
ATTN_GROUP = 4
ATTN_GROUP_BWD = 16


def _block_operands(qr, kr, v_ref, cur, prev):
    prev_ok, cur_ok = _band_masks()
    if prev is None:
        return qr[cur, :], kr[cur, :].astype(BF16), v_ref[cur, :], cur_ok
    kcat = jnp.concatenate([kr[prev, :], kr[cur, :]], axis=0).astype(BF16)
    vcat = jnp.concatenate([v_ref[prev, :], v_ref[cur, :]], axis=0)
    return qr[cur, :], kcat, vcat, jnp.concatenate([prev_ok, cur_ok], axis=1)


def _attn_specs(b):
    col = lambda cb: pl.BlockSpec((SEQ, LANES), lambda bb, hp: (bb, cb + hp))
    tab = pl.BlockSpec((SEQ, LANES), lambda bb, hp: (bb, 0))
    return col, tab


def attn_fwd(qkvz, cosv, sinv, b):
    t = qkvz.shape[0]
    col, tab = _attn_specs(b)
    blocks = _residue_blocks()

    def body(q_ref, k_ref, v_ref, c_ref, s_ref, o_ref, l1_ref, l2_ref, l3_ref, qr, kr, o1, o2, o3):
        l_refs, o_scr = (l1_ref, l2_ref, l3_ref), (o1, o2, o3)
        c, s = c_ref[...], s_ref[...]
        q, k = q_ref[...], k_ref[...]
        qr[...] = q * c + _rot(q) * s
        kr[...] = k * c + _rot(k) * s
        masks = _pair_masks()
        for lo in range(0, len(blocks), ATTN_GROUP):
            chains = []
            for g, cur, prev in blocks[lo:lo + ATTN_GROUP]:
                q2, kcat, vcat, ok = _block_operands(qr, kr, v_ref, cur, prev)
                for m in masks:
                    qm = jnp.where(m, q2, 0.0).astype(BF16)
                    chains.append(dict(g=g, cur=cur, m=m, v=jnp.where(m, vcat, 0.0).astype(BF16),
                                       s=jnp.where(ok, _dot(qm, kcat, _NT) * SCALE, NEG)))
            for ch in chains:
                mx = jnp.max(ch['s'], axis=1, keepdims=True)
                p = jnp.exp(ch['s'] - mx)
                den = jnp.sum(p, axis=1, keepdims=True)
                ch.update(p=p.astype(BF16), inv=1.0 / den, lse=mx + jnp.log(den))
            for ch in chains:
                ch['o'] = _dot(ch['p'], ch['v'], _NN) * ch['inv']
            for c0, c1 in zip(chains[0::2], chains[1::2]):
                o_scr[c0['g']][c0['cur'], :] = c0['o'] + c1['o']
                l_refs[c0['g']][c0['cur'], :] = jnp.where(c0['m'], c0['lse'], c1['lse'])
        w1, w2, w3 = _branch_weights(l1_ref[...], l2_ref[...], l3_ref[...])
        o_ref[...] = w1 * o1[...] + w2 * o2[...] + w3 * o3[...]

    shp = jax.ShapeDtypeStruct((t, D_ATTN), F32)
    return pl.pallas_call(
        body, name="attn_fwd", grid=(b, N_HEAD_PAIRS),
        in_specs=[col(0), col(N_HEAD_PAIRS), col(2 * N_HEAD_PAIRS), tab, tab],
        out_specs=[col(0)] * 4, out_shape=[shp] * 4,
        scratch_shapes=[pltpu.VMEM((SEQ, LANES), F32)] * 5,
        compiler_params=_cparams(("parallel", "parallel")),
    )(qkvz, qkvz, qkvz, cosv, sinv)


def attn_bwd(qkvz, cosv, sinv, dmix, mixed, lses, b):
    t = qkvz.shape[0]
    col, tab = _attn_specs(b)
    blocks = _residue_blocks()
    hd = np.arange(LANES) // HEAD_DIM
    head_ones = jnp.asarray((hd[:, None] == hd[None, :]).astype(np.float32))

    def body(q_ref, k_ref, v_ref, c_ref, s_ref, dm_ref, mx_ref, l1_ref, l2_ref, l3_ref, ones_ref,
             dq_out, dk_out, dv_out, qr, kr, do1, do2, do3, dd1, dd2, dd3, dq_ref, dk_ref, dv_ref):
        l_refs, do_scr, dd_scr = (l1_ref, l2_ref, l3_ref), (do1, do2, do3), (dd1, dd2, dd3)
        c, s = c_ref[...], s_ref[...]
        q, k = q_ref[...], k_ref[...]
        qr[...] = q * c + _rot(q) * s
        kr[...] = k * c + _rot(k) * s
        dm = dm_ref[...]
        tot = _dot(dm * mx_ref[...], ones_ref[...], _NN, HI)
        for w, do_g, dd_g in zip(_branch_weights(l1_ref[...], l2_ref[...], l3_ref[...]), do_scr, dd_scr):
            do_g[...] = w * dm
            dd_g[...] = w * tot
        dq_ref[...] = jnp.zeros((SEQ, LANES), F32)
        dk_ref[...] = jnp.zeros((SEQ, LANES), F32)
        dv_ref[...] = jnp.zeros((SEQ, LANES), F32)
        masks = _pair_masks()
        for lo in range(0, len(blocks), ATTN_GROUP_BWD):
            chains = []
            for g, cur, prev in blocks[lo:lo + ATTN_GROUP_BWD]:
                q2, kcat, vcat, ok = _block_operands(qr, kr, v_ref, cur, prev)
                vcat = vcat.astype(BF16)
                do2_, l2, dd2_ = do_scr[g][cur, :], l_refs[g][cur, :], dd_scr[g][cur, :]
                l2s, dd2s = pltpu.roll(l2, HEAD_DIM, 1), pltpu.roll(dd2_, HEAD_DIM, 1)
                for m in masks:
                    qm = jnp.where(m, q2, 0.0).astype(BF16)
                    dom = jnp.where(m, do2_, 0.0).astype(BF16)
                    lrep, ddrep = jnp.where(m, l2, l2s), jnp.where(m, dd2_, dd2s)
                    if prev is not None:
                        lrep, ddrep = jnp.concatenate([lrep, lrep], axis=1), jnp.concatenate([ddrep, ddrep], axis=1)
                    chains.append(dict(cur=cur, prev=prev, qm=qm, dom=dom, km=jnp.where(m, kcat, 0), lrep=lrep, ddrep=ddrep,
                                       s=jnp.where(ok, _dot(qm, kcat, _NT) * SCALE, NEG), dp=_dot(dom, vcat, _NT)))
            for ch in chains:
                p = jnp.exp(ch['s'] - ch['lrep'])
                ch.update(p=p.astype(BF16), ds=(p * (ch['dp'] - ch['ddrep']) * SCALE).astype(BF16))
            for ch in chains:
                ch.update(dq=_dot(ch['ds'], ch['km'], _NN), dk=_dot(ch['ds'], ch['qm'], _TN), dv=_dot(ch['p'], ch['dom'], _TN))
            for c0, c1 in zip(chains[0::2], chains[1::2]):
                cur, prev = c0['cur'], c0['prev']
                dk, dv = c0['dk'] + c1['dk'], c0['dv'] + c1['dv']
                dq_ref[cur, :] += c0['dq'] + c1['dq']
                if prev is None:
                    dk_ref[cur, :] += dk
                    dv_ref[cur, :] += dv
                else:
                    dk_ref[prev, :] += dk[:ATTN_BLOCK]
                    dv_ref[prev, :] += dv[:ATTN_BLOCK]
                    dk_ref[cur, :] += dk[ATTN_BLOCK:]
                    dv_ref[cur, :] += dv[ATTN_BLOCK:]
        dq, dk = dq_ref[...], dk_ref[...]
        dq_out[...] = (dq * c + _rot(dq * s)).astype(dq_out.dtype)
        dk_out[...] = (dk * c + _rot(dk * s)).astype(dk_out.dtype)
        dv_out[...] = dv_ref[...].astype(dv_out.dtype)

    shp = jax.ShapeDtypeStruct((t, D_ATTN), BF16)
    return pl.pallas_call(
        body, name="attn_bwd", grid=(b, N_HEAD_PAIRS),
        in_specs=[col(0), col(N_HEAD_PAIRS), col(2 * N_HEAD_PAIRS), tab, tab, col(0), col(0), col(0), col(0), col(0),
                  pl.BlockSpec((LANES, LANES), lambda bb, hp: (0, 0))],
        out_specs=[col(0)] * 3, out_shape=[shp] * 3,
        scratch_shapes=[pltpu.VMEM((SEQ, LANES), F32)] * 11,
        compiler_params=_cparams(("parallel", "parallel")),
    )(qkvz, qkvz, qkvz, cosv, sinv, dmix, mixed, *lses, head_ones)


def attn_norm_fwd(mixed, norm_w):
    return rowwise("attn_norm", _rms, [mixed], [norm_w], [(mixed.shape[0], D_ATTN, BF16)])[0]


def attn_norm_bwd(dout, mixed, norm_w):
    def fn(dy, mx, w):
        _, vjp = jax.vjp(_rms, mx, w)
        return vjp(dy)

    return rowwise("attn_norm_bwd", fn, [dout, mixed], [norm_w], [(dout.shape[0], D_ATTN, F32)], accs=[(1, D_ATTN)])


CONV_TM = 512
HALO = 8


def _conv_columns(refs):
    xs_ref, bm_ref, cm_ref = refs
    out = []
    for c in range(D_CONV // LANES):
        lo = c * LANES
        ref, base = (xs_ref, 0) if lo < D_SSD else (bm_ref, D_SSD) if lo < D_SSD + D_BC else (cm_ref, D_SSD + D_BC)
        out.append((slice(lo, lo + LANES), (ref, slice(lo - base, lo - base + LANES))))
    return out


def _conv_taps(scr, w_ref, cs, first_row, step, tm):
    acc = None
    for k in range(CONV_WIDTH):
        term = w_ref[k:k + 1, cs] * scr[pl.ds(first_row + step * k, tm), cs]
        acc = term if acc is None else acc + term
    return acc


def conv_fwd(u, w, bias):
    t = u.shape[0]
    tm, per_seq = CONV_TM, SEQ // CONV_TM

    def body(u_ref, h_ref, w_ref, b_ref, xs_ref, bm_ref, cm_ref, scr):
        first = pl.program_id(0) % per_seq == 0
        scr[0:HALO, :] = jnp.where(first, 0.0, h_ref[...])
        scr[HALO:, :] = u_ref[...]
        for cs, (o_ref, os_) in _conv_columns((xs_ref, bm_ref, cm_ref)):
            o_ref[:, os_] = _silu(_conv_taps(scr, w_ref, cs, HALO - CONV_WIDTH + 1, 1, tm) + b_ref[:, cs])

    return pl.pallas_call(
        body, name="conv_fwd", grid=(t // tm,),
        in_specs=[pl.BlockSpec((tm, D_CONV), lambda i: (i, 0)),
                  pl.BlockSpec((HALO, D_CONV), lambda i: (jnp.maximum(i * (tm // HALO) - 1, 0), 0)),
                  pl.BlockSpec((CONV_WIDTH, D_CONV), lambda i: (0, 0)), pl.BlockSpec((1, D_CONV), lambda i: (0, 0))],
        out_specs=[pl.BlockSpec((tm, D_SSD), lambda i: (i, 0)), pl.BlockSpec((tm, D_BC), lambda i: (i, 0)),
                   pl.BlockSpec((tm, D_BC), lambda i: (i, 0))],
        out_shape=[jax.ShapeDtypeStruct((t, D_SSD), F32), jax.ShapeDtypeStruct((t, D_BC), F32),
                   jax.ShapeDtypeStruct((t, D_BC), F32)],
        scratch_shapes=[pltpu.VMEM((tm + HALO, D_CONV), F32)],
        compiler_params=_cparams(("parallel",)),
    )(u, u, w, bias)


def conv_bwd(u, w, bias, dxs_a, dxs_b, dbm, dcm):
    t = u.shape[0]
    tm, per_seq = CONV_TM, SEQ // CONV_TM
    n_tiles = t // tm

    def body1(u_ref, h_ref, dxs_ref, dxs2_ref, dbm_ref, dcm_ref, w_ref, b_ref, dz_ref, dw_ref, db_ref, scr):
        i = pl.program_id(0)
        first = i % per_seq == 0
        scr[0:HALO, :] = jnp.where(first, 0.0, h_ref[...])
        scr[HALO:, :] = u_ref[...]

        @pl.when(i == 0)
        def _():
            dw_ref[...] = jnp.zeros(dw_ref.shape, F32)
            db_ref[...] = jnp.zeros(db_ref.shape, F32)
        for cs, (g_ref, gs) in _conv_columns((dxs_ref, dbm_ref, dcm_ref)):
            acc = _conv_taps(scr, w_ref, cs, HALO - CONV_WIDTH + 1, 1, tm) + b_ref[:, cs]
            sig = _sigmoid(acc)
            dy = g_ref[:, gs] + dxs2_ref[:, gs] if g_ref is dxs_ref else g_ref[:, gs]
            dz = dy * sig * (1.0 + acc * (1.0 - sig))
            dz_ref[:, cs] = dz
            db_ref[:, cs] += jnp.sum(dz, axis=0, keepdims=True)
            for k in range(CONV_WIDTH):
                dw_ref[k:k + 1, cs] += jnp.sum(dz * scr[pl.ds(HALO - CONV_WIDTH + 1 + k, tm), cs], axis=0, keepdims=True)

    dz, dw, db = pl.pallas_call(
        body1, name="conv_bwd_dz", grid=(n_tiles,),
        in_specs=[pl.BlockSpec((tm, D_CONV), lambda i: (i, 0)),
                  pl.BlockSpec((HALO, D_CONV), lambda i: (jnp.maximum(i * (tm // HALO) - 1, 0), 0)),
                  pl.BlockSpec((tm, D_SSD), lambda i: (i, 0)), pl.BlockSpec((tm, D_SSD), lambda i: (i, 0)),
                  pl.BlockSpec((tm, D_BC), lambda i: (i, 0)), pl.BlockSpec((tm, D_BC), lambda i: (i, 0)),
                  pl.BlockSpec((CONV_WIDTH, D_CONV), lambda i: (0, 0)), pl.BlockSpec((1, D_CONV), lambda i: (0, 0))],
        out_specs=[pl.BlockSpec((tm, D_CONV), lambda i: (i, 0)), pl.BlockSpec((CONV_WIDTH, D_CONV), lambda i: (0, 0)),
                   pl.BlockSpec((1, D_CONV), lambda i: (0, 0))],
        out_shape=[jax.ShapeDtypeStruct((t, D_CONV), F32), jax.ShapeDtypeStruct((CONV_WIDTH, D_CONV), F32),
                   jax.ShapeDtypeStruct((1, D_CONV), F32)],
        scratch_shapes=[pltpu.VMEM((tm + HALO, D_CONV), F32)],
        compiler_params=_cparams(("arbitrary",)),
    )(u, u, dxs_a, dxs_b, dbm, dcm, w, bias)

    def body2(dz_ref, n_ref, w_ref, du_ref, scr):
        last = pl.program_id(0) % per_seq == per_seq - 1
        scr[0:tm, :] = dz_ref[...]
        scr[tm:, :] = jnp.where(last, 0.0, n_ref[...])
        for c in range(D_CONV // LANES):
            cs = slice(c * LANES, (c + 1) * LANES)
            du_ref[:, cs] = _conv_taps(scr, w_ref, cs, CONV_WIDTH - 1, -1, tm).astype(du_ref.dtype)

    du = pl.pallas_call(
        body2, name="conv_bwd_du", grid=(n_tiles,),
        in_specs=[pl.BlockSpec((tm, D_CONV), lambda i: (i, 0)),
                  pl.BlockSpec((HALO, D_CONV), lambda i: (jnp.minimum((i + 1) * (tm // HALO), t // HALO - 1), 0)),
                  pl.BlockSpec((CONV_WIDTH, D_CONV), lambda i: (0, 0))],
        out_specs=pl.BlockSpec((tm, D_CONV), lambda i: (i, 0)),
        out_shape=jax.ShapeDtypeStruct((t, D_CONV), BF16),
        scratch_shapes=[pltpu.VMEM((tm + HALO, D_CONV), F32)],
        compiler_params=_cparams(("parallel",)),
    )(dz, dz, w)
    return du, dw, db


Q = SSD_CHUNK
N_PAIRS = D_SSD // LANES
HEADS_PER_GROUP = N_HEADS // SSD_GROUPS


def _rep(a, j):
    return jnp.broadcast_to(a[:, j:j + 1], a.shape)


def _dot_exact01(a, b, dn, a_is_01):
    x = b if a_is_01 else a
    hi = x.astype(BF16)
    mid = (x - hi.astype(F32)).astype(BF16)
    lo = (x - hi.astype(F32) - mid.astype(F32)).astype(BF16)
    z = a.astype(BF16) if a_is_01 else b.astype(BF16)
    out = None
    for term in (hi, mid, lo):
        d = _dot(z, term, dn) if a_is_01 else _dot(term, z, dn)
        out = d if out is None else out + d
    return out


def _pad_lanes(v, fill=0.0):
    row = jnp.pad(v.reshape(1, -1).astype(F32), ((0, 0), (0, LANES - v.size)), constant_values=fill)
    return row, row.reshape(LANES, 1)


def _ssd_common(dtr_ref, dtrt_ref, bias_r, bias_c, alog_r, alog_c):
    row = lax.broadcasted_iota(jnp.int32, (Q, Q), 0)
    col = lax.broadcasted_iota(jnp.int32, (Q, Q), 1)
    tril = row >= col
    lane = lax.broadcasted_iota(jnp.int32, (1, LANES), 1)
    a_r = jnp.where(lane < N_HEADS, -jnp.exp(alog_r[...]), 0.0)
    sub = lax.broadcasted_iota(jnp.int32, (LANES, 1), 0)
    a_c = jnp.where(sub < N_HEADS, -jnp.exp(alog_c[...]), 0.0)
    dt = _softplus(dtr_ref[...] + bias_r[...])
    cs = _dot_exact01(tril, dt * a_r, _NN, True)
    dtt = _softplus(dtrt_ref[...] + bias_c[...])
    cst = _dot_exact01(dtt * a_c, row <= col, _NN, False)
    return tril, lane, a_r, dt, cs, cst


def _ssd_specs(b, nc, rev):
    ci = (lambda c: nc - 1 - c) if rev else (lambda c: c)
    rows = lambda w: pl.BlockSpec((Q, w), lambda bb, c: (bb * nc + ci(c), 0))
    dtt = pl.BlockSpec((LANES, Q), lambda bb, c: (0, bb * nc + ci(c)))
    const = lambda s: pl.BlockSpec(s, lambda bb, c: (0,) * len(s))
    state = pl.BlockSpec((None, N_PAIRS, LANES, SSD_STATE), lambda bb, c: (bb * nc + ci(c), 0, 0, 0))
    return rows, dtt, const, state


def ssd_fwd(xs, bm, cm, dtraw, dt_bias, a_log, b):
    t = xs.shape[0]
    nc = SEQ // Q
    rows, dtt_spec, const, state = _ssd_specs(b, nc, False)
    bias_r, bias_c = _pad_lanes(dt_bias)
    alog_r, alog_c = _pad_lanes(a_log)

    def body(xs_ref, b_ref, c_ref, dtr_ref, dtrt_ref, br, bc, ar, ac, y_ref, hp_ref, h_scr):
        @pl.when(pl.program_id(1) == 0)
        def _():
            h_scr[...] = jnp.zeros(h_scr.shape, F32)
        tril, lane, _, dt, cs, cst = _ssd_common(dtr_ref, dtrt_ref, br, bc, ar, ac)
        sub = lax.broadcasted_iota(jnp.int32, (LANES, 1), 0)
        y_acc = [jnp.zeros((Q, LANES), F32) for _ in range(N_PAIRS)]
        h_old = [h_scr[p] for p in range(N_PAIRS)]
        h_new = [jnp.zeros((LANES, SSD_STATE), F32) for _ in range(N_PAIRS)]
        for g in range(SSD_GROUPS):
            bg = b_ref[:, g * SSD_STATE:(g + 1) * SSD_STATE].astype(BF16)
            cg = c_ref[:, g * SSD_STATE:(g + 1) * SSD_STATE].astype(BF16)
            cb = _dot(cg, bg, _NT)
            heads = []
            for j in range(g * HEADS_PER_GROUP, (g + 1) * HEADS_PER_GROUP):
                p, side = j // 2, j % 2
                m = (lane < HEAD_DIM) if side == 0 else (lane >= HEAD_DIM)
                ms = (sub < HEAD_DIM) if side == 0 else (sub >= HEAD_DIM)
                csj, dtj = _rep(cs, j), _rep(dt, j)
                lmat = jnp.exp(jnp.where(tril, csj - cst[j:j + 1, :], NEG))
                xdt = jnp.where(m, xs_ref[:, p * LANES:(p + 1) * LANES] * dtj, 0.0)
                hm = jnp.where(ms, h_old[p], 0.0)
                last = csj[Q - 1:Q, :]
                heads.append(dict(p=p, hm=hm, ecs=jnp.exp(csj), el=jnp.exp(last), gmat=(cb * lmat).astype(BF16),
                                  xdt=xdt.astype(BF16), xd=(xdt * jnp.exp(last - csj)).astype(BF16)))
            for h in heads:
                h.update(ydiag=_dot(h['gmat'], h['xdt'], _NN), ch=_dot(cg, h['hm'].astype(BF16), _NT), sj=_dot(h['xd'], bg, _TN))
            for h in heads:
                y_acc[h['p']] = y_acc[h['p']] + h['ydiag'] + h['ecs'] * h['ch']
                h_new[h['p']] = h_new[h['p']] + h['el'] * h['hm'] + h['sj']
        for p in range(N_PAIRS):
            y_ref[:, p * LANES:(p + 1) * LANES] = y_acc[p]
            hp_ref[p] = h_old[p]
            h_scr[p] = h_new[p]

    return pl.pallas_call(
        body, name="ssd_fwd", grid=(b, nc),
        in_specs=[rows(D_SSD), rows(D_BC), rows(D_BC), rows(LANES), dtt_spec, const((1, LANES)), const((LANES, 1)),
                  const((1, LANES)), const((LANES, 1))],
        out_specs=[rows(D_SSD), state],
        out_shape=[jax.ShapeDtypeStruct((t, D_SSD), F32),
                   jax.ShapeDtypeStruct((b * nc, N_PAIRS, LANES, SSD_STATE), F32)],
        scratch_shapes=[pltpu.VMEM((N_PAIRS, LANES, SSD_STATE), F32)],
        compiler_params=_cparams(("parallel", "arbitrary")),
    )(xs, bm, cm, dtraw, dtraw.T, bias_r, bias_c, alog_r, alog_c)


def ssd_bwd(xs, bm, cm, dtraw, dt_bias, a_log, hprev, dy, b):
    t = xs.shape[0]
    nc = SEQ // Q
    rows, dtt_spec, const, state = _ssd_specs(b, nc, True)
    bias_r, bias_c = _pad_lanes(dt_bias)
    alog_r, alog_c = _pad_lanes(a_log)

    def body(xs_ref, b_ref, c_ref, dtr_ref, dtrt_ref, hp_ref, dy_ref, br, bc, ar, ac,
             dxs_ref, db_ref, dc_ref, ddt_ref, dbias_ref, dalog_ref, dh_scr):
        first = jnp.logical_and(pl.program_id(0) == 0, pl.program_id(1) == 0)

        @pl.when(pl.program_id(1) == 0)
        def _():
            dh_scr[...] = jnp.zeros(dh_scr.shape, F32)

        @pl.when(first)
        def _():
            dbias_ref[...] = jnp.zeros(dbias_ref.shape, F32)
            dalog_ref[...] = jnp.zeros(dalog_ref.shape, F32)
        tril, lane, a_r, dt, cs, cst = _ssd_common(dtr_ref, dtrt_ref, br, bc, ar, ac)
        sub = lax.broadcasted_iota(jnp.int32, (LANES, 1), 0)
        rowq = lax.broadcasted_iota(jnp.int32, (Q, 1), 0)
        triu = (lax.broadcasted_iota(jnp.int32, (Q, Q), 0) <= lax.broadcasted_iota(jnp.int32, (Q, Q), 1)).astype(F32)
        dxs_acc = [jnp.zeros((Q, LANES), F32) for _ in range(N_PAIRS)]
        dh_in = [dh_scr[p] for p in range(N_PAIRS)]
        h_in = [hp_ref[p] for p in range(N_PAIRS)]
        dh_out = [jnp.zeros((LANES, SSD_STATE), F32) for _ in range(N_PAIRS)]
        ddt = jnp.zeros((Q, LANES), F32)
        dalog = jnp.zeros((1, LANES), F32)
        for g in range(SSD_GROUPS):
            gs = slice(g * SSD_STATE, (g + 1) * SSD_STATE)
            bg, cg = b_ref[:, gs].astype(BF16), c_ref[:, gs].astype(BF16)
            cb = _dot(cg, bg, _NT)
            dcb = jnp.zeros((Q, Q), F32)
            dbg = jnp.zeros((Q, SSD_STATE), F32)
            dcg = jnp.zeros((Q, SSD_STATE), F32)
            heads = []
            for j in range(g * HEADS_PER_GROUP, (g + 1) * HEADS_PER_GROUP):
                p, side = j // 2, j % 2
                m = (lane < HEAD_DIM) if side == 0 else (lane >= HEAD_DIM)
                ms = (sub < HEAD_DIM) if side == 0 else (sub >= HEAD_DIM)
                csj, dtj = _rep(cs, j), _rep(dt, j)
                lmat = jnp.exp(jnp.where(tril, csj - cst[j:j + 1, :], NEG))
                x2 = jnp.where(m, xs_ref[:, p * LANES:(p + 1) * LANES], 0.0)
                xdt = x2 * dtj
                dym = jnp.where(m, dy_ref[:, p * LANES:(p + 1) * LANES], 0.0)
                hm = jnp.where(ms, h_in[p], 0.0)
                dhm = jnp.where(ms, dh_in[p], 0.0)
                last = csj[Q - 1:Q, :]
                decay = jnp.exp(last - csj)
                heads.append(dict(j=j, p=p, dtj=dtj, lmat=lmat, x2=x2, hm=hm, dhm=dhm, decay=decay, el=jnp.exp(last),
                                  gmat=cb * lmat, dym=dym.astype(BF16), xdt=xdt.astype(BF16), hmb=hm.astype(BF16),
                                  dhmb=dhm.astype(BF16), dye=dym * jnp.exp(csj), xd=xdt * decay))
            for h in heads:
                dyeb, xdb = h['dye'].astype(BF16), h['xd'].astype(BF16)
                h.update(dg=_dot(h['dym'], h['xdt'], _NT),
                         dxdt=_dot(h['gmat'].astype(BF16), h['dym'], _TN),
                         ch=_dot(cg, h['hmb'], _NT),
                         dcg=_dot(dyeb, h['hmb'], _NN), dhp=_dot(dyeb, cg, _TN),
                         wmat=_dot(bg, h['dhmb'], _NT),
                         dbg=_dot(xdb, h['dhmb'], _NN))
            for h in heads:
                ej = h['dg'] * h['gmat']
                col_sums = jnp.broadcast_to(jnp.sum(ej, axis=0, keepdims=True), (Q, Q)).T
                dl = h['xd'] * h['wmat']
                total = lambda v: jnp.sum(jnp.sum(v, axis=0, keepdims=True), axis=1, keepdims=True)
                dlast = total(dl) + h['el'] * total(h['dhm'] * h['hm'])
                h['dcs'] = (jnp.sum(ej + h['dye'] * h['ch'] - dl, axis=1, keepdims=True) - col_sums
                            + jnp.where(rowq == Q - 1, dlast, 0.0))
                h['dxdt'] = h['dxdt'] + h['decay'] * h['wmat']
                dcb, dcg, dbg = dcb + h['dg'] * h['lmat'], dcg + h['dcg'], dbg + h['dbg']
                dh_out[h['p']] = dh_out[h['p']] + h['el'] * h['dhm'] + h['dhp']
            for h in heads:
                h['da'] = _dot_exact01(triu, h['dcs'], _NN, True)
            for h in heads:
                j, da = h['j'], h['da']
                aj = jnp.sum(jnp.where(lane == j, a_r, 0.0), axis=1, keepdims=True)
                ddtj = da * aj + jnp.sum(h['dxdt'] * h['x2'], axis=1, keepdims=True)
                ddt = ddt + jnp.where(lane == j, ddtj, 0.0)
                dalog = dalog + jnp.where(lane == j, jnp.sum(da * h['dtj'], axis=0, keepdims=True) * aj, 0.0)
                dxs_acc[h['p']] = dxs_acc[h['p']] + h['dxdt'] * h['dtj']
            dcbb = dcb.astype(BF16)
            dc_ref[:, gs] = dcg + _dot(dcbb, bg, _NN)
            db_ref[:, gs] = dbg + _dot(dcbb, cg, _TN)
        for p in range(N_PAIRS):
            dxs_ref[:, p * LANES:(p + 1) * LANES] = dxs_acc[p]
            dh_scr[p] = dh_out[p]
        ddtraw = ddt * _sigmoid(dtr_ref[...] + br[...])
        ddt_ref[...] = ddtraw
        dbias_ref[...] += jnp.sum(ddtraw, axis=0, keepdims=True)
        dalog_ref[...] += dalog

    return pl.pallas_call(
        body, name="ssd_bwd", grid=(b, nc),
        in_specs=[rows(D_SSD), rows(D_BC), rows(D_BC), rows(LANES), dtt_spec, state, rows(D_SSD), const((1, LANES)),
                  const((LANES, 1)), const((1, LANES)), const((LANES, 1))],
        out_specs=[rows(D_SSD), rows(D_BC), rows(D_BC), rows(LANES), const((1, LANES)), const((1, LANES))],
        out_shape=[jax.ShapeDtypeStruct((t, D_SSD), F32), jax.ShapeDtypeStruct((t, D_BC), F32),
                   jax.ShapeDtypeStruct((t, D_BC), F32), jax.ShapeDtypeStruct((t, LANES), F32),
                   jax.ShapeDtypeStruct((1, LANES), F32), jax.ShapeDtypeStruct((1, LANES), F32)],
        scratch_shapes=[pltpu.VMEM((N_PAIRS, LANES, SSD_STATE), F32)],
        compiler_params=_cparams(("arbitrary", "arbitrary")),
    )(xs, bm, cm, dtraw, dtraw.T, hprev, dy, bias_r, bias_c, alog_r, alog_c)


def _split_w_in(w_in):
    w_dt = jnp.pad(w_in[:, D_QKVZ + D_CONV:], ((0, 0), (0, LANES - N_HEADS)))
    return w_in[:, :D_QKVZ], w_in[:, D_QKVZ:D_QKVZ + D_CONV], w_dt


def mixer_fwd(hb, p, cosv, sinv, b):
    t = hb.shape[0]
    w_a, w_b, w_c = _split_w_in(p['w_in'])
    qkvz = mm("in_qkvz", [(hb, w_a, 'nn')], D_QKVZ)
    xbc = mm("in_xbc", [(hb, w_b, 'nn')], D_CONV)
    dtraw = mm("in_dt", [(hb, w_c, 'nn')], LANES)
    mixed, *lses = attn_fwd(qkvz, cosv, sinv, b)
    attn = attn_norm_fwd(mixed, p['attn_norm_w'])
    xs, bm, cm = conv_fwd(xbc, p['conv_w'], p['conv_b'])
    y, hprev = ssd_fwd(xs, bm, cm, dtraw, p['dt_bias'], p['a_log'], b)
    dskip = jnp.repeat(p['d_skip'].reshape(-1), HEAD_DIM).reshape(1, D_SSD)
    yg, = rowwise("ssd_gate", _gate, [y, xs, Op(qkvz, D_SSD, 3)], [dskip, p['ssd_norm_w']], [(t, D_SSD, BF16)])
    mix = mm("out_proj", [(attn, p['w_out'][:D_ATTN], 'nn'), (yg, p['w_out'][D_ATTN:], 'nn')], D_MODEL, out_dtype=BF16)
    res = dict(hb=hb, qkvz=qkvz, xbc=xbc, dtraw=dtraw, mixed=mixed, lses=lses, attn=attn, xs=xs, bm=bm, cm=cm,
               y=y, hprev=hprev, dskip=dskip, yg=yg, cosv=cosv, sinv=sinv)
    return mix, res


def mixer_bwd(r, p, dmix, dh_resid, b):
    t = dmix.shape[0]
    w_a, w_b, w_c = _split_w_in(p['w_in'])
    w_out = p['w_out']
    dattn = mm("out_bwd_dattn", [(dmix, w_out[:D_ATTN], 'nt')], D_ATTN)
    dyg = mm("out_bwd_dyg", [(dmix, w_out[D_ATTN:], 'nt')], D_SSD)
    dw_out = jnp.concatenate([mm_tn("out_bwd_dw_a", r['attn'], dmix, BF16),
                              mm_tn("out_bwd_dw_y", r['yg'], dmix, BF16)], axis=0)

    def gate_bwd(dy_, y_, xs_, z_, ds_, w_):
        _, vjp = jax.vjp(_gate, y_, xs_, z_, ds_, w_)
        return vjp(dy_)

    dy, dxs_a, dz, ddskip, dssd_norm = rowwise(
        "ssd_gate_bwd", gate_bwd, [dyg, r['y'], r['xs'], Op(r['qkvz'], D_SSD, 3)], [r['dskip'], p['ssd_norm_w']],
        [(t, D_SSD, F32), (t, D_SSD, F32), (t, D_SSD, BF16)], accs=[(1, D_SSD), (1, D_SSD)])
    dxs_b, dbm, dcm, ddtraw, ddt_bias, da_log = ssd_bwd(r['xs'], r['bm'], r['cm'], r['dtraw'], p['dt_bias'], p['a_log'],
                                                        r['hprev'], dy, b)
    dxbc, dconv_w, dconv_b = conv_bwd(r['xbc'], p['conv_w'], p['conv_b'], dxs_a, dxs_b, dbm, dcm)
    dmixed, dattn_norm = attn_norm_bwd(dattn, r['mixed'], p['attn_norm_w'])
    dq, dk, dv = attn_bwd(r['qkvz'], r['cosv'], r['sinv'], dmixed, r['mixed'], r['lses'], b)
    wq, wk, wv, wz = (w_a[:, i * D_ATTN:(i + 1) * D_ATTN] for i in range(4))
    dh = mm("in_bwd_dh", [(dq, wq, 'nt'), (dk, wk, 'nt'), (dv, wv, 'nt'), (dz, wz, 'nt'), (dxbc, w_b, 'nt'),
                          (ddtraw, w_c, 'nt')], D_MODEL, add=dh_resid, tn=512)
    h = r['hb']
    dw_in = jnp.concatenate([mm_tn_cat("in_bwd_dw_qkvz", h, [dq, dk, dv, dz], BF16),
                             mm_tn_cat("in_bwd_dw_xbc_dt", h, [dxbc, ddtraw], BF16)[:, :D_CONV + N_HEADS]], axis=1)
    head_sum = lambda v: v.reshape(N_HEADS, HEAD_DIM).sum(axis=1).reshape(1, N_HEADS)
    grads = dict(w_in=dw_in, w_out=dw_out, conv_w=dconv_w, conv_b=dconv_b, dt_bias=ddt_bias[:, :N_HEADS],
                 a_log=da_log[:, :N_HEADS], d_skip=head_sum(ddskip), attn_norm_w=dattn_norm, ssd_norm_w=dssd_norm)
    return dh, grads


FFN2_KEYS = ('ffn2_gate', 'ffn2_up', 'ffn2_down')
MIXER_KEYS = ('w_in', 'conv_w', 'w_out')
FFN_COL = ('ffn1_gate', 'ffn1_up', 'ffn2_gate', 'ffn2_up')
FFN_ROW = ('ffn1_down', 'ffn2_down')
CONV_W_COMM = (8, 2 * LANES)
SMALL = 'small'


def comm_shape(k, shapes):
    if k in FFN_COL:
        return (D_MODEL, FF_PAD)
    if k in FFN_ROW:
        return (FF_PAD, D_MODEL)
    if k == 'conv_w':
        return CONV_W_COMM
    return tuple(shapes[k][1:])


def to_comm(k, vals, shapes):
    a = vals[k].reshape(shapes[k][1:])
    r_, c_ = comm_shape(k, shapes)
    return jnp.pad(a, ((0, r_ - a.shape[0]), (0, c_ - a.shape[1])))


SMALL_ROWS, SMALL_COLS = 16, D_CONV


def pack_small(small):
    rows = [jnp.pad(small[r].reshape(1, -1), ((0, 0), (0, SMALL_COLS - small[r].size))) for r in REPLICATED]
    return jnp.concatenate(rows + [jnp.zeros((SMALL_ROWS - len(rows), SMALL_COLS), F32)], axis=0)


def full_weight(k, g):
    if k in FFN_COL:
        return g
    if k == 'conv_w':
        return jnp.transpose(g[:, :CONV_WIDTH, :D_CONV // N_DEV], (1, 0, 2)).reshape(CONV_WIDTH, D_CONV)
    return g.reshape(N_DEV * g.shape[1], g.shape[2])


def grad_shards(k, g):
    if k in FFN_COL:
        return g
    if k == 'conv_w':
        s = jnp.transpose(g.reshape(CONV_WIDTH, N_DEV, D_CONV // N_DEV), (1, 0, 2))
        return jnp.pad(s, ((0, 0), (0, CONV_W_COMM[0] - CONV_WIDTH), (0, CONV_W_COMM[1] - D_CONV // N_DEV)))
    return g.reshape(N_DEV, g.shape[0] // N_DEV, g.shape[1])


def _flip(v, bit):
    return 1 - v if bit else v


N_PEER_COPIES = N_DEV - 1


def _comm_call(name, body, arrs, out_shape):
    n = len(arrs)
    return pl.pallas_call(
        functools.partial(body, n), name=name, out_shape=out_shape,
        in_specs=[pl.BlockSpec(memory_space=pl.ANY)] * n, out_specs=[pl.BlockSpec(memory_space=pl.ANY)] * n,
        scratch_shapes=[pltpu.SemaphoreType.DMA((n * N_PEER_COPIES,)), pltpu.SemaphoreType.DMA((n * N_PEER_COPIES,)),
                        pltpu.SemaphoreType.DMA((n,))],
    )(*arrs)


def _blk(ref, idx, by_cols):
    if not by_cols:
        return ref.at[idx]
    c = ref.shape[1] // N_DEV
    return ref.at[:, pl.ds(pl.multiple_of(idx * c, LANES), c)]


def _blocked_shape(a, by_cols):
    return (a.shape[0], N_DEV * a.shape[1]) if by_cols else (N_DEV,) + a.shape


def all_gather(arrs, by_cols):
    def body(n, *refs):
        x_refs, out_refs, (send_sems, recv_sems, local_sems) = refs[:n], refs[n:2 * n], refs[2 * n:]
        x, y, c = lax.axis_index("x"), lax.axis_index("y"), lax.axis_index("c")
        me, sibling = (x, y, c), (x, y, 1 - c)
        chips = [(1 - x, y), (x, 1 - y), (1 - x, 1 - y)]

        def copy(a, k, block, to, src=None):
            px, py, pc = block
            dst = _blk(out_refs[a], 4 * px + 2 * py + pc, by_cols[a])
            return pltpu.make_async_remote_copy(
                src_ref=dst if src is None else src, dst_ref=dst, send_sem=send_sems.at[a * N_PEER_COPIES + k],
                recv_sem=recv_sems.at[a * N_PEER_COPIES + k], device_id=to, device_id_type=MESH)

        mine = [pltpu.make_async_copy(x_refs[a], _blk(out_refs[a], 4 * x + 2 * y + c, by_cols[a]), local_sems.at[a])
                for a in range(n)]
        started = []
        for a in range(n):
            mine[a].start()
            first = [copy(a, 0, me, sibling, src=x_refs[a])]
            first += [copy(a, 1 + j, me, (*chip, c), src=x_refs[a]) for j, chip in enumerate(chips)]
            for cp in first:
                cp.start()
            started += first
        for j, chip in enumerate(chips):
            for a in range(n):
                copy(a, 1 + j, (*chip, c), me).wait_recv()
                cp = copy(a, 4 + j, (*chip, c), sibling)
                cp.start()
                started.append(cp)
        for a in range(n):
            copy(a, 0, sibling, me).wait_recv()
            for j, chip in enumerate(chips):
                copy(a, 4 + j, (*chip, 1 - c), me).wait_recv()
        for cp in started:
            cp.wait_send()
        for cp in mine:
            cp.wait()

    return _comm_call("all_gather_weights", body, arrs,
                      [jax.ShapeDtypeStruct(_blocked_shape(a, bc), a.dtype) for a, bc in zip(arrs, by_cols)])


def _landing_shape(a, by_cols):
    return (N_DEV, a.shape[0], a.shape[1] // N_DEV) if by_cols else a.shape


_HBM = pl.BlockSpec(memory_space=pltpu.HBM)
_SEM = pl.BlockSpec(memory_space=pltpu.SEMAPHORE)
_EFFECT = pltpu.SideEffectType.DATAFLOW_SIDE_EFFECTING


def _peer(k):
    x, y, c = lax.axis_index("x"), lax.axis_index("y"), lax.axis_index("c")
    return _flip(x, k & 4), _flip(y, k & 2), _flip(c, k & 1)


def _my_index():
    return 4 * lax.axis_index("x") + 2 * lax.axis_index("y") + lax.axis_index("c")


def _split_copies(mode, by_cols, src_refs, land_refs, send_sems, recv_sems):
    me = _my_index()
    out = []
    for a, bc in enumerate(by_cols):
        for k in range(1, N_DEV):
            px, py, pc = _peer(k)
            src = _blk(src_refs[a], 4 * px + 2 * py + pc, bc) if mode == 'scatter' else src_refs[a]
            dst = land_refs[a].at[me] if mode == 'scatter' else _blk(land_refs[a], me, bc)
            out.append(pltpu.make_async_remote_copy(
                src_ref=src, dst_ref=dst, send_sem=send_sems.at[a * N_PEER_COPIES + k - 1],
                recv_sem=recv_sems.at[a * N_PEER_COPIES + k - 1], device_id=(px, py, pc), device_id_type=MESH))
    return out


def exchange_start(name, mode, srcs, by_cols):
    n = len(srcs)
    lands = [lax.empty(_landing_shape(s, bc) if mode == 'scatter' else _blocked_shape(s, bc), s.dtype)
             for s, bc in zip(srcs, by_cols)]

    def body(*refs):
        src_refs, land_refs, send_sems, recv_sems = refs[:n], refs[n:2 * n], refs[2 * n], refs[2 * n + 1]
        for cp in _split_copies(mode, by_cols, src_refs, land_refs, send_sems, recv_sems):
            cp.start()
        refs[-1][...] = jnp.zeros(refs[-1].shape, F32)

    sems = pltpu.SemaphoreType.DMA((n * N_PEER_COPIES,))
    res = pl.pallas_call(
        body, name=name,
        out_shape=(sems, sems, *[pltpu.HBM(a.shape, a.dtype) for a in srcs + lands], jax.ShapeDtypeStruct((8, LANES), F32)),
        in_specs=(_HBM,) * (2 * n), out_specs=(_SEM, _SEM, *(_HBM,) * (2 * n), pl.BlockSpec(memory_space=pltpu.VMEM)),
        input_output_aliases={i: 2 + i for i in range(2 * n)},
        compiler_params=pltpu.CompilerParams(has_side_effects=_EFFECT),
    )(*[pltpu.with_memory_space_constraint(a, pltpu.HBM) for a in srcs + lands])
    return (mode, by_cols, res[:-1]), res[-1]


def exchange_wait(name, handles, after):
    mode, by_cols, (send_sems, recv_sems, *bufs) = handles
    n = len(by_cols)

    def body(*refs):
        src_refs, land_refs, s_sems, r_sems = refs[:n], refs[n:2 * n], refs[2 * n], refs[2 * n + 1]
        for cp in _split_copies(mode, by_cols, src_refs, land_refs, s_sems, r_sems):
            cp.wait_send()
            cp.wait_recv()

    res = pl.pallas_call(
        body, name=name, out_shape=tuple(pltpu.HBM(a.shape, a.dtype) for a in bufs),
        in_specs=(*(_HBM,) * (2 * n), _SEM, _SEM, pl.BlockSpec(memory_space=pl.ANY)), out_specs=(_HBM,) * (2 * n),
        input_output_aliases={i: i for i in range(2 * n)},
        compiler_params=pltpu.CompilerParams(has_side_effects=_EFFECT),
    )(*bufs, send_sems, recv_sems, after)
    me, out = _my_index(), []
    for src, land, bc in zip(res[:n], res[n:], by_cols):
        if mode == 'scatter':
            c = land.shape[2]
            own = lax.dynamic_slice(src, (0, me * c), (src.shape[0], c)) if bc else lax.dynamic_index_in_dim(src, me, 0, False)
            out.append(lax.dynamic_update_slice(land, own[None], (me, 0, 0)))
        elif bc:
            out.append(lax.dynamic_update_slice(land, src, (0, me * src.shape[1])))
        else:
            out.append(lax.dynamic_update_slice(land, src[None], (me, 0, 0)))
    return out


def _adamw_math(g, w, m, v):
    c1 = 1.0 / (1.0 - ADAM_B1 ** ADAM_STEP)
    c2 = 1.0 / (1.0 - ADAM_B2 ** ADAM_STEP)
    m = ADAM_B1 * m + (1.0 - ADAM_B1) * g
    v = ADAM_B2 * v + (1.0 - ADAM_B2) * jnp.square(g)
    return g, -ADAM_LR * ((m * c1) / (jnp.sqrt(v * c2) + ADAM_EPS) + ADAM_WD * w), m, v


def adamw(name, recv, w, m, v, tm):
    _, rows, cols = w.shape
    tm = min(tm, rows)

    def body(*refs):
        g = refs[0][0:tm, 0:cols].astype(F32)
        for s in range(1, N_DEV):
            g = g + refs[s][0:tm, 0:cols].astype(F32)
        res = _adamw_math(g, *[r[...] for r in refs[N_DEV:N_DEV + 3]])
        for r, val in zip(refs[N_DEV + 3:], res):
            r[...] = val

    part = lambda s: pl.BlockSpec((None, recv.shape[1] if tm == rows else tm, recv.shape[2]), lambda i: (s, i, 0))
    tile = pl.BlockSpec((None, tm, cols), lambda i: (0, i, 0))
    return pl.pallas_call(
        body, name=name, grid=(rows // tm,), in_specs=[part(s) for s in range(N_DEV)] + [tile] * 3, out_specs=[tile] * 4,
        out_shape=[jax.ShapeDtypeStruct((1, rows, cols), F32)] * 4, compiler_params=_cparams(("parallel",)),
    )(*[recv] * N_DEV, w, m, v)


def adamw_small(recv, wl, ml, vl):
    n = len(REPLICATED)

    def body(recv_ref, *refs):
        g = recv_ref[0]
        for s in range(1, N_DEV):
            g = g + recv_ref[s]
        for r in range(n):
            w, m, v = (refs[j * n + r][...] for j in range(3))
            for j, val in enumerate(_adamw_math(g[r:r + 1, :w.shape[1]], w, m, v)):
                refs[(3 + j) * n + r][...] = val

    arrs = [d[k].reshape(1, -1) for d in (wl, ml, vl) for k in REPLICATED]
    res = pl.pallas_call(
        body, name="adamw_small", out_shape=[jax.ShapeDtypeStruct(a.shape, F32) for a in arrs[:n]] * 4,
    )(recv, *arrs)
    return [{k: res[j * n + r].reshape(wl[k].shape) for r, k in enumerate(REPLICATED)} for j in range(4)]


ADAMW_TM = {'ffn1_gate': 256, 'ffn1_up': 256, 'ffn2_gate': 256, 'ffn2_up': 256, 'w_in': 32}


def kernel(x, positions, ln1_g, ln1_b, ffn1_gate, ffn1_up, ffn1_down, w_in, conv_w, conv_b, dt_bias, a_log, d_skip, attn_norm_w, ssd_norm_w, w_out, ln2_g, ln2_b, ffn2_gate, ffn2_up, ffn2_down, ln3_g, ln3_b, loss_target, m_ln1_g, m_ln1_b, m_ffn1_gate, m_ffn1_up, m_ffn1_down, m_w_in, m_conv_w, m_conv_b, m_dt_bias, m_a_log, m_d_skip, m_attn_norm_w, m_ssd_norm_w, m_w_out, m_ln2_g, m_ln2_b, m_ffn2_gate, m_ffn2_up, m_ffn2_down, m_ln3_g, m_ln3_b, v_ln1_g, v_ln1_b, v_ffn1_gate, v_ffn1_up, v_ffn1_down, v_w_in, v_conv_w, v_conv_b, v_dt_bias, v_a_log, v_d_skip, v_attn_norm_w, v_ssd_norm_w, v_w_out, v_ln2_g, v_ln2_b, v_ffn2_gate, v_ffn2_up, v_ffn2_down, v_ln3_g, v_ln3_b):
    args = dict(locals())
    wl = {k: args[k] for k in WEIGHTS}
    ml = {k: args["m_" + k] for k in WEIGHTS}
    vl = {k: args["v_" + k] for k in WEIGHTS}
    shapes = {k: wl[k].shape for k in WEIGHTS}
    b, s, dm = x.shape
    t = b * s

    sent = {k: to_comm(k, wl, shapes).astype(F32 if k == 'conv_w' else BF16) for k in SHARDED}
    by_cols = lambda keys: [k in FFN_COL for k in keys]
    gate, up = all_gather([sent['ffn1_gate'], sent['ffn1_up']], [True] * 2)
    (gate, up), sent = lax.optimization_barrier(((gate, up), sent))
    p = {'ffn1_gate': gate, 'ffn1_up': up}
    gather_down, token_d = exchange_start("gather_ffn1_down_start", 'gather', [sent['ffn1_down']], [False])
    sent['w_in'] = sent['w_in'] + token_d[0, 0].astype(BF16)
    gather_mixer, token_m = exchange_start("gather_mixer_start", 'gather', [sent[k] for k in MIXER_KEYS], by_cols(MIXER_KEYS))
    sent['ffn2_gate'] = sent['ffn2_gate'] + token_m[0, 0].astype(BF16)
    gather_ffn2, token_f = exchange_start("gather_ffn2_start", 'gather', [sent[k] for k in FFN2_KEYS], by_cols(FFN2_KEYS))
    for k in REPLICATED:
        p[k] = wl[k].reshape(1, -1)

    x2 = x.reshape(t, dm)
    cosv, sinv = rope_tables(positions)
    g1, u1, a1, at1 = ffn_gate_up("ffn1_gate_up", x2, p['ffn1_gate'], p['ffn1_up'], after=(token_d, token_m, token_f))
    p['ffn1_down'] = full_weight('ffn1_down', exchange_wait("gather_ffn1_down_wait", gather_down, a1)[0])
    f1, res1 = mm("ffn1_down", [(a1, p['ffn1_down'], 'nn')], D_MODEL, out_dtype=BF16), (x2, g1, u1, at1)
    h1, h1b = resid_ln_fwd("ln1", 0.5, x2, f1, p['ln1_g'], p['ln1_b'])
    for k, g in zip(MIXER_KEYS, exchange_wait("gather_mixer_wait", gather_mixer, h1b)):
        p[k] = full_weight(k, g)
    mix, resm = mixer_fwd(h1b, p, cosv, sinv, b)
    h2, h2b = resid_ln_fwd("ln2", 1.0, h1, mix, p['ln2_g'], p['ln2_b'])
    for k, g in zip(FFN2_KEYS, exchange_wait("gather_ffn2_wait", gather_ffn2, h2b)):
        p[k] = full_weight(k, g)
    f2, res3 = ffn_fwd("ffn2", h2b, p['ffn2_gate'], p['ffn2_up'], p['ffn2_down'])

    small, full = {}, {}
    dh2_res, df2, small['ln3_g'], small['ln3_b'], sq = ln_loss_bwd("ln3_loss_bwd", h2, f2, loss_target.reshape(t, dm),
                                                                   p['ln3_g'], p['ln3_b'])
    loss = lax.psum(jnp.sum(sq) * (0.5 / dm), AXES)

    dh2, full['ffn2_gate'], full['ffn2_up'], full['ffn2_down'] = ffn_bwd("ffn2", res3, p['ffn2_gate'], p['ffn2_up'],
                                                                       p['ffn2_down'], df2, dh2_res)
    ffn2_exchange, token = exchange_start("grads_ffn2_start", 'scatter', [grad_shards(k, full[k]) for k in FFN2_KEYS],
                                          by_cols(FFN2_KEYS))
    dh1_res, dmix, small['ln2_g'], small['ln2_b'] = resid_ln_bwd("ln2_bwd", 1.0, h1, mix, p['ln2_g'] + token[:1, :1],
                                                                 p['ln2_b'], dh2)
    dh1, gm = mixer_bwd(resm, p, dmix, dh1_res, b)
    for k in ('conv_b', 'dt_bias', 'a_log', 'd_skip', 'attn_norm_w', 'ssd_norm_w'):
        small[k] = gm[k]
    mixer_exchange, token = exchange_start("grads_mixer_start", 'scatter', [grad_shards(k, gm[k]) for k in MIXER_KEYS],
                                           by_cols(MIXER_KEYS))
    dx_res, df1, small['ln1_g'], small['ln1_b'] = resid_ln_bwd("ln1_bwd", 0.5, x2, f1, p['ln1_g'] + token[:1, :1],
                                                               p['ln1_b'], dh1)
    hb, g, u, at = res1
    small_part = pack_small(small)
    dg, du = ffn_da_act("ffn1_bwd_da_act", df1, p['ffn1_down'], g, u)
    dwd = mm_acc("ffn1_bwd_dwd", at, df1, BF16, after=dg)
    down_exchange, token = exchange_start("grads_ffn1_down_start", 'scatter', [
        grad_shards('ffn1_down', dwd), jnp.broadcast_to(small_part[None], (N_DEV,) + small_part.shape)], [False, False])
    dwg = mm_tn("ffn1_bwd_dwg", hb, dg, BF16, after=token)
    gate_exchange, token = exchange_start("grads_ffn1_gate_start", 'scatter', [grad_shards('ffn1_gate', dwg)], [True])
    dwu = mm_tn("ffn1_bwd_dwu", hb, du, BF16, after=token)
    up_exchange, token = exchange_start("grads_ffn1_up_start", 'scatter', [grad_shards('ffn1_up', dwu)], [True])
    dx = mm("ffn1_bwd_dh", [(dg, p['ffn1_gate'], 'nt'), (du, p['ffn1_up'], 'nt')], D_MODEL, add=dx_res, tn=512, after=token)
    recv = {}
    for keys, name, ex in (((FFN2_KEYS), "grads_ffn2_wait", ffn2_exchange), (MIXER_KEYS, "grads_mixer_wait", mixer_exchange),
                           (('ffn1_down', SMALL), "grads_ffn1_down_wait", down_exchange),
                           (('ffn1_gate',), "grads_ffn1_gate_wait", gate_exchange),
                           (('ffn1_up',), "grads_ffn1_up_wait", up_exchange)):
        recv.update(zip(keys, exchange_wait(name, ex, dx)))
    outs = adamw_small(recv.pop(SMALL), wl, ml, vl)
    for k, r in recv.items():
        for o, a in zip(outs, adamw(f"adamw_{k}", r, wl[k], ml[k], vl[k], ADAMW_TM.get(k, shapes[k][1]))):
            o[k] = a
    return (loss, dx.reshape(b, s, dm), *[o[k] for o in outs for k in WEIGHTS])
```

```python
import functools

import jax
import jax.numpy as jnp
import numpy as np
from jax import lax
from jax.experimental import pallas as pl
from jax.experimental.pallas import tpu as pltpu

F32, BF16 = jnp.float32, jnp.bfloat16
HI = lax.Precision.HIGHEST
MESH = pl.DeviceIdType.MESH
AXES = ("x", "y", "c")
N_DEV = 8

D_MODEL = 1024
SEQ = 2048
HEAD_DIM = 64
N_HEADS = 12
D_ATTN = N_HEADS * HEAD_DIM
DILATIONS = (1, 4, 16)
ATTN_BLOCK = 128
ROPE_THETA = 500000.0
ROPE_DIM = 16
D_SSD = 768
SSD_GROUPS = 4
SSD_STATE = 128
SSD_CHUNK = 128
D_BC = SSD_GROUPS * SSD_STATE
D_CONV = D_SSD + 2 * D_BC
CONV_WIDTH = 4
D_QKVZ = 3 * D_ATTN + D_SSD
D_FF = 2816
ALPHA = 2.0 ** 0.25
LN_EPS = 1e-5
RMS_EPS = 1e-6
ADAM_LR, ADAM_B1, ADAM_B2, ADAM_EPS, ADAM_WD, ADAM_STEP = 0.001, 0.9, 0.999, 1e-08, 0.01, 10

LANES = 128
VMEM_LIMIT = 52 * 1024 * 1024
NEG = -1e30

WEIGHTS = ['ln1_g', 'ln1_b', 'ffn1_gate', 'ffn1_up', 'ffn1_down', 'w_in', 'conv_w', 'conv_b', 'dt_bias', 'a_log',
           'd_skip', 'attn_norm_w', 'ssd_norm_w', 'w_out', 'ln2_g', 'ln2_b', 'ffn2_gate', 'ffn2_up', 'ffn2_down',
           'ln3_g', 'ln3_b']
COL_SHARDED = ('ffn1_gate', 'ffn1_up', 'conv_w', 'ffn2_gate', 'ffn2_up')
ROW_SHARDED = ('ffn1_down', 'w_in', 'w_out', 'ffn2_down')
SHARDED = tuple(n for n in WEIGHTS if n in COL_SHARDED or n in ROW_SHARDED)
REPLICATED = tuple(n for n in WEIGHTS if n not in SHARDED)
FF_SHARD = D_FF // N_DEV
FF_PAD = -(-FF_SHARD // LANES) * LANES


def _cparams(sem=None):
    return pltpu.CompilerParams(dimension_semantics=sem, vmem_limit_bytes=VMEM_LIMIT)


def _tile(n, prefs):
    for p in prefs:
        if n % p == 0:
            return p
    return n


class Op:
    def __init__(self, arr, bw=None, cb=0, ro=0):
        self.arr, self.bw, self.cb, self.ro = arr, (arr.shape[1] if bw is None else bw), cb, ro


def _op(a):
    return a if isinstance(a, Op) else Op(a)


def rowwise(name, fn, ins, consts, outs, accs=(), tm=512):
    ins = [_op(a) for a in ins]
    rows = outs[0][0]
    n_in, n_c, n_o, n_a = len(ins), len(consts), len(outs), len(accs)
    tm = min(tm, rows)
    assert rows % tm == 0, (name, rows, tm)

    def body(*refs):
        vals = [r[...].astype(F32) for r in refs[:n_in + n_c]]
        res = fn(*vals)
        res = res if isinstance(res, (tuple, list)) else (res,)
        o_refs = refs[n_in + n_c:n_in + n_c + n_o]
        a_refs = refs[n_in + n_c + n_o:]
        for r, v in zip(o_refs, res[:n_o]):
            r[...] = v.astype(r.dtype)
        if n_a:
            @pl.when(pl.program_id(0) == 0)
            def _():
                for r in a_refs:
                    r[...] = jnp.zeros(r.shape, r.dtype)
            for r, v in zip(a_refs, res[n_o:]):
                r[...] += v

    in_specs = [pl.BlockSpec((tm, o.bw), functools.partial(lambda i, o: (i + o.ro, o.cb), o=o)) for o in ins]
    in_specs += [pl.BlockSpec(c.shape, functools.partial(lambda i, nd: (0,) * nd, nd=c.ndim)) for c in consts]
    out_specs = [pl.BlockSpec((tm, w), lambda i: (i, 0)) for (_, w, _) in outs]
    out_specs += [pl.BlockSpec(s, functools.partial(lambda i, nd: (0,) * nd, nd=len(s))) for s in accs]
    out_shape = [jax.ShapeDtypeStruct((r, w), dt) for (r, w, dt) in outs]
    out_shape += [jax.ShapeDtypeStruct(s, F32) for s in accs]
    res = pl.pallas_call(
        body, name=name, grid=(rows // tm,), in_specs=in_specs, out_specs=out_specs, out_shape=out_shape,
        compiler_params=_cparams(("arbitrary",) if n_a else ("parallel",)),
    )(*[o.arr for o in ins], *consts)
    return res


MM_TM = 1024
MM_TN = (1024, 896, 768, 512, 256, 128)
_NT = (((1,), (1,)), ((), ()))
_NN = (((1,), (0,)), ((), ()))
_TN = (((0,), (0,)), ((), ()))


def _dot(a, b, dn, precision=None):
    return lax.dot_general(a, b, dn, preferred_element_type=F32, precision=precision)


def _mm_specs(name, pairs, n_out, tm, tn):
    in_specs, args = [], []
    for a, b, mode in pairs:
        o = _op(a)
        in_specs.append(pl.BlockSpec((tm, o.bw), functools.partial(lambda j, i, o: (i, o.cb), o=o)))
        args.append(o.arr)
        if mode == 'nn':
            assert b.shape == (o.bw, n_out), (name, b.shape, o.bw, n_out)
            in_specs.append(pl.BlockSpec((o.bw, tn), lambda j, i: (0, j)))
        else:
            assert b.shape == (n_out, o.bw), (name, b.shape, o.bw, n_out)
            in_specs.append(pl.BlockSpec((tn, o.bw), lambda j, i: (j, 0)))
        args.append(b)
    return in_specs, args


def _mm_acc(refs, pairs):
    acc = None
    for k, (_, _, mode) in enumerate(pairs):
        d = _dot(refs[2 * k][...].astype(BF16), refs[2 * k + 1][...].astype(BF16), _NN if mode == 'nn' else _NT)
        acc = d if acc is None else acc + d
    return acc


def mm(name, pairs, n_out, add=None, out_dtype=F32, tm=MM_TM, tn=None, after=None):
    m = _op(pairs[0][0]).arr.shape[0]
    tn = tn or _tile(n_out, MM_TN)
    n_p = len(pairs)

    def body(*refs):
        acc = _mm_acc(refs, pairs)
        if add is not None:
            acc = acc + refs[2 * n_p][...]
        refs[-1][...] = acc.astype(refs[-1].dtype)

    in_specs, args = _mm_specs(name, pairs, n_out, tm, tn)
    tile = pl.BlockSpec((tm, tn), lambda j, i: (i, j))
    if add is not None:
        in_specs.append(tile)
        args.append(add)
    if after is not None:
        in_specs.append(pl.BlockSpec(memory_space=pl.ANY))
        args.append(after)
    return pl.pallas_call(
        body, name=name, grid=(n_out // tn, m // tm), in_specs=in_specs, out_specs=tile,
        out_shape=jax.ShapeDtypeStruct((m, n_out), out_dtype),
        compiler_params=_cparams(("parallel", "parallel")),
    )(*args)


def mm_tn(name, a, b, out_dtype=F32, tt=2048, after=None):
    a, b = _op(a), _op(b)
    t = a.arr.shape[0]
    k, n = a.bw, b.bw
    tk = _tile(k, (512, 896, 768, 256, 128))
    tn = _tile(n, (3072, 1792) + MM_TN)
    tt = min(tt, t)
    n_t = t // tt
    order = [] if after is None else [after]

    def body(a_ref, b_ref, *rest):
        o_ref, acc_ref = rest[-2:]
        s = pl.program_id(2)
        d = _dot(a_ref[...].astype(BF16), b_ref[...].astype(BF16), _TN)

        @pl.when(s == 0)
        def _():
            acc_ref[...] = d

        @pl.when(s > 0)
        def _():
            acc_ref[...] += d

        @pl.when(s == n_t - 1)
        def _():
            o_ref[...] = acc_ref[...].astype(o_ref.dtype)

    return pl.pallas_call(
        body, name=name, grid=(k // tk, n // tn, n_t),
        in_specs=[pl.BlockSpec((tt, tk), functools.partial(lambda kk, nn, s, o: (s, o.cb * (o.bw // tk) + kk), o=a)),
                  pl.BlockSpec((tt, tn), functools.partial(lambda kk, nn, s, o: (s, o.cb * (o.bw // tn) + nn), o=b))]
        + [pl.BlockSpec(memory_space=pl.ANY) for _ in order],
        out_specs=pl.BlockSpec((tk, tn), lambda kk, nn, s: (kk, nn)),
        out_shape=jax.ShapeDtypeStruct((k, n), out_dtype),
        scratch_shapes=[pltpu.VMEM((tk, tn), F32)],
        compiler_params=_cparams(("parallel", "parallel", "arbitrary")),
    )(a.arr, b.arr, *order)


def _sigmoid(x):
    return 1.0 / (1.0 + jnp.exp(-x))


def _silu(x):
    return x * _sigmoid(x)


def _softplus(x):
    return jnp.maximum(x, 0.0) + jnp.log(1.0 + jnp.exp(-jnp.abs(x)))


def _resid_ln(scale, h, branch, g, b):
    r = ALPHA * h + scale * branch
    mu = jnp.mean(r, axis=-1, keepdims=True)
    var = jnp.mean(jnp.square(r - mu), axis=-1, keepdims=True)
    return (r - mu) * lax.rsqrt(var + LN_EPS) * g + b


def _rms(t, w):
    return t * lax.rsqrt(jnp.mean(t * t, axis=-1, keepdims=True) + RMS_EPS) * w


def _branch_weights(l1, l2, l3):
    m = jnp.maximum(jnp.maximum(l1, l2), l3)
    e1, e2, e3 = jnp.exp(l1 - m), jnp.exp(l2 - m), jnp.exp(l3 - m)
    inv = 1.0 / (e1 + e2 + e3)
    return e1 * inv, e2 * inv, e3 * inv


def _gate(y, xs, z, dskip, w):
    return _rms((y + dskip * xs) * _silu(z), w)


def _rot(x):
    d = lax.broadcasted_iota(jnp.int32, x.shape, 1) % HEAD_DIM
    up = pltpu.roll(x, x.shape[1] - ROPE_DIM // 2, 1)
    down = jnp.where(d < ROPE_DIM, pltpu.roll(x, ROPE_DIM // 2, 1), 0.0)
    return jnp.where(d < ROPE_DIM // 2, up, down)


def ffn_gate_up(name, h, wg, wu, after=()):
    m, nf = h.shape[0], wg.shape[1]
    tn = _tile(nf, MM_TN)

    def body(h_ref, g_w, u_w, *rest):
        du_ref, dg_ref, a_ref, at_ref = rest[-4:]
        hb = h_ref[...].astype(BF16)
        g = _dot(hb, g_w[...].astype(BF16), _NN)
        u = _dot(hb, u_w[...].astype(BF16), _NN)
        sig = _sigmoid(g)
        gs = g * sig
        du_ref[...] = gs.astype(du_ref.dtype)
        dg_ref[...] = (u * (sig + gs * (1.0 - sig))).astype(dg_ref.dtype)
        a = gs * u
        a_ref[...] = a.astype(a_ref.dtype)
        at_ref[...] = a.T.astype(at_ref.dtype)

    in_specs, args = _mm_specs(name, [(h, wg, 'nn')], nf, MM_TM, tn)
    in_specs.append(in_specs[1])
    in_specs += [pl.BlockSpec(memory_space=pl.ANY) for _ in after]
    tile = pl.BlockSpec((MM_TM, tn), lambda j, i: (i, j))
    return pl.pallas_call(
        body, name=name, grid=(nf // tn, m // MM_TM), in_specs=in_specs,
        out_specs=[tile] * 3 + [pl.BlockSpec((tn, MM_TM), lambda j, i: (j, i))],
        out_shape=[jax.ShapeDtypeStruct((m, nf), BF16)] * 3 + [jax.ShapeDtypeStruct((nf, m), BF16)],
        compiler_params=_cparams(("parallel", "parallel")),
    )(*args, wu, *after)


def mm_tn_cat(name, a, bs, out_dtype=F32, tt=2048):
    t, k = a.shape
    widths = [b.shape[1] for b in bs]
    n, tk, tt = sum(widths), _tile(k, (512, 256, 128)), min(tt, t)
    n_t = t // tt

    def body(a_ref, *rest):
        b_refs, o_ref, acc_ref = rest[:len(bs)], rest[-2], rest[-1]
        s = pl.program_id(1)
        at = a_ref[...].astype(BF16)
        d = jnp.concatenate([_dot(at, b[...].astype(BF16), _TN) for b in b_refs], axis=1)

        @pl.when(s == 0)
        def _():
            acc_ref[...] = d

        @pl.when(s > 0)
        def _():
            acc_ref[...] += d

        @pl.when(s == n_t - 1)
        def _():
            o_ref[...] = acc_ref[...].astype(o_ref.dtype)

    return pl.pallas_call(
        body, name=name, grid=(k // tk, n_t),
        in_specs=[pl.BlockSpec((tt, tk), lambda kk, s: (s, kk))] + [pl.BlockSpec((tt, w), lambda kk, s: (s, 0)) for w in widths],
        out_specs=pl.BlockSpec((tk, n), lambda kk, s: (kk, 0)),
        out_shape=jax.ShapeDtypeStruct((k, n), out_dtype), scratch_shapes=[pltpu.VMEM((tk, n), F32)],
        compiler_params=_cparams(("parallel", "arbitrary")),
    )(a, *bs)


def mm_acc(name, a, b, out_dtype=F32, tt=1024, after=None):
    k, t = a.shape
    n = b.shape[1]
    tk, tn, tt = _tile(k, (1024, 512, 256, 128)), _tile(n, MM_TN), min(tt, t)
    n_t = t // tt
    order = [] if after is None else [after]

    def body(a_ref, b_ref, *rest):
        o_ref, acc_ref = rest[-2:]
        s = pl.program_id(2)
        d = _dot(a_ref[...].astype(BF16), b_ref[...].astype(BF16), _NN)

        @pl.when(s == 0)
        def _():
            acc_ref[...] = d

        @pl.when(s > 0)
        def _():
            acc_ref[...] += d

        @pl.when(s == n_t - 1)
        def _():
            o_ref[...] = acc_ref[...].astype(o_ref.dtype)

    return pl.pallas_call(
        body, name=name, grid=(k // tk, n // tn, n_t),
        in_specs=[pl.BlockSpec((tk, tt), lambda kk, nn, s: (kk, s)), pl.BlockSpec((tt, tn), lambda kk, nn, s: (s, nn))]
        + [pl.BlockSpec(memory_space=pl.ANY) for _ in order],
        out_specs=pl.BlockSpec((tk, tn), lambda kk, nn, s: (kk, nn)),
        out_shape=jax.ShapeDtypeStruct((k, n), out_dtype), scratch_shapes=[pltpu.VMEM((tk, tn), F32)],
        compiler_params=_cparams(("parallel", "parallel", "arbitrary")),
    )(a, b, *order)


def ffn_da_act(name, df, wd, a_du, a_dg):
    m, nf = df.shape[0], wd.shape[0]
    tn = _tile(nf, MM_TN)

    def body(df_ref, w_ref, adu_ref, adg_ref, dg_ref, du_ref):
        da = _dot(df_ref[...].astype(BF16), w_ref[...].astype(BF16), _NT)
        dg_ref[...] = (da * adg_ref[...].astype(F32)).astype(dg_ref.dtype)
        du_ref[...] = (da * adu_ref[...].astype(F32)).astype(du_ref.dtype)

    in_specs, args = _mm_specs(name, [(df, wd, 'nt')], nf, MM_TM, tn)
    tile = pl.BlockSpec((MM_TM, tn), lambda j, i: (i, j))
    return pl.pallas_call(
        body, name=name, grid=(nf // tn, m // MM_TM), in_specs=in_specs + [tile, tile], out_specs=[tile] * 2,
        out_shape=[jax.ShapeDtypeStruct((m, nf), BF16)] * 2, compiler_params=_cparams(("parallel", "parallel")),
    )(*args, a_du, a_dg)


def resid_ln_fwd(name, scale, h, branch, ln_g, ln_b):
    t = h.shape[0]

    def fn(*a):
        y = _resid_ln(scale, *a)
        return y, y

    return rowwise(name, fn, [h, branch], [ln_g, ln_b], [(t, D_MODEL, F32), (t, D_MODEL, BF16)], tm=1024)


def ffn_fwd(tag, hb, wg, wu, wd, after=()):
    g, u, a, at = ffn_gate_up(f"{tag}_gate_up", hb, wg, wu, after)
    f = mm(f"{tag}_down", [(a, wd, 'nn')], D_MODEL, out_dtype=BF16)
    return f, (hb, g, u, at)


def ln_loss_bwd(name, h, branch, target, ln_g, ln_b):
    t, dm = h.shape

    def fn(h_, br_, tgt, g_, b_):
        y, vjp = jax.vjp(functools.partial(_resid_ln, 0.5), h_, br_, g_, b_)
        e = y - tgt
        return (*vjp(e * (1.0 / dm)), jnp.sum(e * e, axis=0, keepdims=True))

    return rowwise(name, fn, [h, branch, target], [ln_g, ln_b], [(t, dm, F32), (t, dm, BF16)],
                   accs=[(1, dm), (1, dm), (1, dm)], tm=1024)


def resid_ln_bwd(name, scale, h, branch, ln_g, ln_b, dout, extra=None):
    t = h.shape[0]

    def fn(h_, br_, do_, *rest):
        g_, b_ = rest[-2], rest[-1]
        _, vjp = jax.vjp(functools.partial(_resid_ln, scale), h_, br_, g_, b_)
        dh, dbr, dg, db = vjp(do_)
        if extra is not None:
            dh = dh + rest[0]
        return dh, dbr, dg, db

    ins = [h, branch, dout] + ([extra] if extra is not None else [])
    return rowwise(name, fn, ins, [ln_g, ln_b], [(t, D_MODEL, F32), (t, D_MODEL, BF16)],
                   accs=[(1, D_MODEL), (1, D_MODEL)], tm=1024)


def ffn_bwd(tag, res, wg, wu, wd, df, dh_resid):
    hb, g, u, at = res
    dg, du = ffn_da_act(f"{tag}_bwd_da_act", df, wd, g, u)
    dwd = mm_acc(f"{tag}_bwd_dwd", at, df, BF16)
    dh = mm(f"{tag}_bwd_dh", [(dg, wg, 'nt'), (du, wu, 'nt')], D_MODEL, add=dh_resid, tn=512)
    dwg = mm_tn(f"{tag}_bwd_dwg", hb, dg, BF16)
    dwu = mm_tn(f"{tag}_bwd_dwu", hb, du, BF16)
    return dh, dwg, dwu, dwd


def rope_tables(positions):
    inv_freq = ROPE_THETA ** (-jnp.arange(0, ROPE_DIM, 2, dtype=F32) / ROPE_DIM)
    ang = positions.reshape(-1, 1).astype(F32) * inv_freq
    c, s = jnp.cos(ang), jnp.sin(ang)
    t = ang.shape[0]
    cosv = jnp.concatenate([c, c, jnp.ones((t, HEAD_DIM - ROPE_DIM), F32)], axis=1)
    sinv = jnp.concatenate([-s, s, jnp.zeros((t, HEAD_DIM - ROPE_DIM), F32)], axis=1)
    return jnp.tile(cosv, (1, 2)), jnp.tile(sinv, (1, 2))


def _pair_masks():
    lane = lax.broadcasted_iota(jnp.int32, (1, LANES), 1)
    return (lane < HEAD_DIM, lane >= HEAD_DIM)


def _band_masks():
    row = lax.broadcasted_iota(jnp.int32, (ATTN_BLOCK, ATTN_BLOCK), 0)
    col = lax.broadcasted_iota(jnp.int32, (ATTN_BLOCK, ATTN_BLOCK), 1)
    return col >= row, col <= row


def _residue_blocks():
    out = []
    for g, d in enumerate(DILATIONS):
        for r in range(d):
            for i in range(SEQ // d // ATTN_BLOCK):
                rows = lambda j: pl.ds(r + j * ATTN_BLOCK * d, ATTN_BLOCK, stride=d) if d > 1 else pl.ds(j * ATTN_BLOCK, ATTN_BLOCK)
                out.append((g, rows(i), rows(i - 1) if i > 0 else None))
    return out


N_HEAD_PAIRS = D_ATTN // LANES
SCALE = HEAD_DIM ** -0.5
ATTN_GROUP = 4
ATTN_GROUP_BWD = 16


def _block_operands(qr, kr, v_ref, cur, prev):
    prev_ok, cur_ok = _band_masks()
    if prev is None:
        return qr[cur, :], kr[cur, :].astype(BF16), v_ref[cur, :], cur_ok
    kcat = jnp.concatenate([kr[prev, :], kr[cur, :]], axis=0).astype(BF16)
    vcat = jnp.concatenate([v_ref[prev, :], v_ref[cur, :]], axis=0)
    return qr[cur, :], kcat, vcat, jnp.concatenate([prev_ok, cur_ok], axis=1)


def _attn_specs(b):
    col = lambda cb: pl.BlockSpec((SEQ, LANES), lambda bb, hp: (bb, cb + hp))
    tab = pl.BlockSpec((SEQ, LANES), lambda bb, hp: (bb, 0))
    return col, tab


def attn_fwd(qkvz, cosv, sinv, b):
    t = qkvz.shape[0]
    col, tab = _attn_specs(b)
    blocks = _residue_blocks()

    def body(q_ref, k_ref, v_ref, c_ref, s_ref, o_ref, l1_ref, l2_ref, l3_ref, qr, kr, o1, o2, o3):
        l_refs, o_scr = (l1_ref, l2_ref, l3_ref), (o1, o2, o3)
        c, s = c_ref[...], s_ref[...]
        q, k = q_ref[...], k_ref[...]
        qr[...] = q * c + _rot(q) * s
        kr[...] = k * c + _rot(k) * s
        masks = _pair_masks()
        for lo in range(0, len(blocks), ATTN_GROUP):
            chains = []
            for g, cur, prev in blocks[lo:lo + ATTN_GROUP]:
                q2, kcat, vcat, ok = _block_operands(qr, kr, v_ref, cur, prev)
                for m in masks:
                    qm = jnp.where(m, q2, 0.0).astype(BF16)
                    chains.append(dict(g=g, cur=cur, m=m, v=jnp.where(m, vcat, 0.0).astype(BF16),
                                       s=jnp.where(ok, _dot(qm, kcat, _NT) * SCALE, NEG)))
            for ch in chains:
                mx = jnp.max(ch['s'], axis=1, keepdims=True)
                p = jnp.exp(ch['s'] - mx)
                den = jnp.sum(p, axis=1, keepdims=True)
                ch.update(p=p.astype(BF16), inv=1.0 / den, lse=mx + jnp.log(den))
            for ch in chains:
                ch['o'] = _dot(ch['p'], ch['v'], _NN) * ch['inv']
            for c0, c1 in zip(chains[0::2], chains[1::2]):
                o_scr[c0['g']][c0['cur'], :] = c0['o'] + c1['o']
                l_refs[c0['g']][c0['cur'], :] = jnp.where(c0['m'], c0['lse'], c1['lse'])
        w1, w2, w3 = _branch_weights(l1_ref[...], l2_ref[...], l3_ref[...])
        o_ref[...] = w1 * o1[...] + w2 * o2[...] + w3 * o3[...]

    shp = jax.ShapeDtypeStruct((t, D_ATTN), F32)
    return pl.pallas_call(
        body, name="attn_fwd", grid=(b, N_HEAD_PAIRS),
        in_specs=[col(0), col(N_HEAD_PAIRS), col(2 * N_HEAD_PAIRS), tab, tab],
        out_specs=[col(0)] * 4, out_shape=[shp] * 4,
        scratch_shapes=[pltpu.VMEM((SEQ, LANES), F32)] * 5,
        compiler_params=_cparams(("parallel", "parallel")),
    )(qkvz, qkvz, qkvz, cosv, sinv)


def attn_bwd(qkvz, cosv, sinv, dmix, mixed, lses, b):
    t = qkvz.shape[0]
    col, tab = _attn_specs(b)
    blocks = _residue_blocks()
    hd = np.arange(LANES) // HEAD_DIM
    head_ones = jnp.asarray((hd[:, None] == hd[None, :]).astype(np.float32))

    def body(q_ref, k_ref, v_ref, c_ref, s_ref, dm_ref, mx_ref, l1_ref, l2_ref, l3_ref, ones_ref,
             dq_out, dk_out, dv_out, qr, kr, do1, do2, do3, dd1, dd2, dd3, dq_ref, dk_ref, dv_ref):
        l_refs, do_scr, dd_scr = (l1_ref, l2_ref, l3_ref), (do1, do2, do3), (dd1, dd2, dd3)
        c, s = c_ref[...], s_ref[...]
        q, k = q_ref[...], k_ref[...]
        qr[...] = q * c + _rot(q) * s
        kr[...] = k * c + _rot(k) * s
        dm = dm_ref[...]
        tot = _dot(dm * mx_ref[...], ones_ref[...], _NN, HI)
        for w, do_g, dd_g in zip(_branch_weights(l1_ref[...], l2_ref[...], l3_ref[...]), do_scr, dd_scr):
            do_g[...] = w * dm
            dd_g[...] = w * tot
        dq_ref[...] = jnp.zeros((SEQ, LANES), F32)
        dk_ref[...] = jnp.zeros((SEQ, LANES), F32)
        dv_ref[...] = jnp.zeros((SEQ, LANES), F32)
        masks = _pair_masks()
        for lo in range(0, len(blocks), ATTN_GROUP_BWD):
            chains = []
            for g, cur, prev in blocks[lo:lo + ATTN_GROUP_BWD]:
                q2, kcat, vcat, ok = _block_operands(qr, kr, v_ref, cur, prev)
                vcat = vcat.astype(BF16)
                do2_, l2, dd2_ = do_scr[g][cur, :], l_refs[g][cur, :], dd_scr[g][cur, :]
                l2s, dd2s = pltpu.roll(l2, HEAD_DIM, 1), pltpu.roll(dd2_, HEAD_DIM, 1)
                for m in masks:
                    qm = jnp.where(m, q2, 0.0).astype(BF16)
                    dom = jnp.where(m, do2_, 0.0).astype(BF16)
                    lrep, ddrep = jnp.where(m, l2, l2s), jnp.where(m, dd2_, dd2s)
                    if prev is not None:
                        lrep, ddrep = jnp.concatenate([lrep, lrep], axis=1), jnp.concatenate([ddrep, ddrep], axis=1)
                    chains.append(dict(cur=cur, prev=prev, qm=qm, dom=dom, km=jnp.where(m, kcat, 0), lrep=lrep, ddrep=ddrep,
                                       s=jnp.where(ok, _dot(qm, kcat, _NT) * SCALE, NEG), dp=_dot(dom, vcat, _NT)))
            for ch in chains:
                p = jnp.exp(ch['s'] - ch['lrep'])
                ch.update(p=p.astype(BF16), ds=(p * (ch['dp'] - ch['ddrep']) * SCALE).astype(BF16))
            for ch in chains:
                ch.update(dq=_dot(ch['ds'], ch['km'], _NN), dk=_dot(ch['ds'], ch['qm'], _TN), dv=_dot(ch['p'], ch['dom'], _TN))
            for c0, c1 in zip(chains[0::2], chains[1::2]):
                cur, prev = c0['cur'], c0['prev']
                dk, dv = c0['dk'] + c1['dk'], c0['dv'] + c1['dv']
                dq_ref[cur, :] += c0['dq'] + c1['dq']
                if prev is None:
                    dk_ref[cur, :] += dk
                    dv_ref[cur, :] += dv
                else:
                    dk_ref[prev, :] += dk[:ATTN_BLOCK]
                    dv_ref[prev, :] += dv[:ATTN_BLOCK]
                    dk_ref[cur, :] += dk[ATTN_BLOCK:]
                    dv_ref[cur, :] += dv[ATTN_BLOCK:]
        dq, dk = dq_ref[...], dk_ref[...]
        dq_out[...] = (dq * c + _rot(dq * s)).astype(dq_out.dtype)
        dk_out[...] = (dk * c + _rot(dk * s)).astype(dk_out.dtype)
        dv_out[...] = dv_ref[...].astype(dv_out.dtype)

    shp = jax.ShapeDtypeStruct((t, D_ATTN), BF16)
    return pl.pallas_call(
        body, name="attn_bwd", grid=(b, N_HEAD_PAIRS),
        in_specs=[col(0), col(N_HEAD_PAIRS), col(2 * N_HEAD_PAIRS), tab, tab, col(0), col(0), col(0), col(0), col(0),
                  pl.BlockSpec((LANES, LANES), lambda bb, hp: (0, 0))],
        out_specs=[col(0)] * 3, out_shape=[shp] * 3,
        scratch_shapes=[pltpu.VMEM((SEQ, LANES), F32)] * 11,
        compiler_params=_cparams(("parallel", "parallel")),
    )(qkvz, qkvz, qkvz, cosv, sinv, dmix, mixed, *lses, head_ones)


def attn_norm_fwd(mixed, norm_w):
    return rowwise("attn_norm", _rms, [mixed], [norm_w], [(mixed.shape[0], D_ATTN, BF16)])[0]


def attn_norm_bwd(dout, mixed, norm_w):
    def fn(dy, mx, w):
        _, vjp = jax.vjp(_rms, mx, w)
        return vjp(dy)

    return rowwise("attn_norm_bwd", fn, [dout, mixed], [norm_w], [(dout.shape[0], D_ATTN, F32)], accs=[(1, D_ATTN)])


CONV_TM = 512
HALO = 8


def _conv_columns(refs):
    xs_ref, bm_ref, cm_ref = refs
    out = []
    for c in range(D_CONV // LANES):
        lo = c * LANES
        ref, base = (xs_ref, 0) if lo < D_SSD else (bm_ref, D_SSD) if lo < D_SSD + D_BC else (cm_ref, D_SSD + D_BC)
        out.append((slice(lo, lo + LANES), (ref, slice(lo - base, lo - base + LANES))))
    return out


def _conv_taps(scr, w_ref, cs, first_row, step, tm):
    acc = None
    for k in range(CONV_WIDTH):
        term = w_ref[k:k + 1, cs] * scr[pl.ds(first_row + step * k, tm), cs]
        acc = term if acc is None else acc + term
    return acc


def conv_fwd(u, w, bias):
    t = u.shape[0]
    tm, per_seq = CONV_TM, SEQ // CONV_TM

    def body(u_ref, h_ref, w_ref, b_ref, xs_ref, bm_ref, cm_ref, scr):
        first = pl.program_id(0) % per_seq == 0
        scr[0:HALO, :] = jnp.where(first, 0.0, h_ref[...])
        scr[HALO:, :] = u_ref[...]
        for cs, (o_ref, os_) in _conv_columns((xs_ref, bm_ref, cm_ref)):
            o_ref[:, os_] = _silu(_conv_taps(scr, w_ref, cs, HALO - CONV_WIDTH + 1, 1, tm) + b_ref[:, cs])

    return pl.pallas_call(
        body, name="conv_fwd", grid=(t // tm,),
        in_specs=[pl.BlockSpec((tm, D_CONV), lambda i: (i, 0)),
                  pl.BlockSpec((HALO, D_CONV), lambda i: (jnp.maximum(i * (tm // HALO) - 1, 0), 0)),
                  pl.BlockSpec((CONV_WIDTH, D_CONV), lambda i: (0, 0)), pl.BlockSpec((1, D_CONV), lambda i: (0, 0))],
        out_specs=[pl.BlockSpec((tm, D_SSD), lambda i: (i, 0)), pl.BlockSpec((tm, D_BC), lambda i: (i, 0)),
                   pl.BlockSpec((tm, D_BC), lambda i: (i, 0))],
        out_shape=[jax.ShapeDtypeStruct((t, D_SSD), F32), jax.ShapeDtypeStruct((t, D_BC), F32),
                   jax.ShapeDtypeStruct((t, D_BC), F32)],
        scratch_shapes=[pltpu.VMEM((tm + HALO, D_CONV), F32)],
        compiler_params=_cparams(("parallel",)),
    )(u, u, w, bias)


def conv_bwd(u, w, bias, dxs_a, dxs_b, dbm, dcm):
    t = u.shape[0]
    tm, per_seq = CONV_TM, SEQ // CONV_TM
    n_tiles = t // tm

    def body1(u_ref, h_ref, dxs_ref, dxs2_ref, dbm_ref, dcm_ref, w_ref, b_ref, dz_ref, dw_ref, db_ref, scr):
        i = pl.program_id(0)
        first = i % per_seq == 0
        scr[0:HALO, :] = jnp.where(first, 0.0, h_ref[...])
        scr[HALO:, :] = u_ref[...]

        @pl.when(i == 0)
        def _():
            dw_ref[...] = jnp.zeros(dw_ref.shape, F32)
            db_ref[...] = jnp.zeros(db_ref.shape, F32)
        for cs, (g_ref, gs) in _conv_columns((dxs_ref, dbm_ref, dcm_ref)):
            acc = _conv_taps(scr, w_ref, cs, HALO - CONV_WIDTH + 1, 1, tm) + b_ref[:, cs]
            sig = _sigmoid(acc)
            dy = g_ref[:, gs] + dxs2_ref[:, gs] if g_ref is dxs_ref else g_ref[:, gs]
            dz = dy * sig * (1.0 + acc * (1.0 - sig))
            dz_ref[:, cs] = dz
            db_ref[:, cs] += jnp.sum(dz, axis=0, keepdims=True)
            for k in range(CONV_WIDTH):
                dw_ref[k:k + 1, cs] += jnp.sum(dz * scr[pl.ds(HALO - CONV_WIDTH + 1 + k, tm), cs], axis=0, keepdims=True)

    dz, dw, db = pl.pallas_call(
        body1, name="conv_bwd_dz", grid=(n_tiles,),
        in_specs=[pl.BlockSpec((tm, D_CONV), lambda i: (i, 0)),
                  pl.BlockSpec((HALO, D_CONV), lambda i: (jnp.maximum(i * (tm // HALO) - 1, 0), 0)),
                  pl.BlockSpec((tm, D_SSD), lambda i: (i, 0)), pl.BlockSpec((tm, D_SSD), lambda i: (i, 0)),
                  pl.BlockSpec((tm, D_BC), lambda i: (i, 0)), pl.BlockSpec((tm, D_BC), lambda i: (i, 0)),
                  pl.BlockSpec((CONV_WIDTH, D_CONV), lambda i: (0, 0)), pl.BlockSpec((1, D_CONV), lambda i: (0, 0))],
        out_specs=[pl.BlockSpec((tm, D_CONV), lambda i: (i, 0)), pl.BlockSpec((CONV_WIDTH, D_CONV), lambda i: (0, 0)),
                   pl.BlockSpec((1, D_CONV), lambda i: (0, 0))],
        out_shape=[jax.ShapeDtypeStruct((t, D_CONV), F32), jax.ShapeDtypeStruct((CONV_WIDTH, D_CONV), F32),
                   jax.ShapeDtypeStruct((1, D_CONV), F32)],
        scratch_shapes=[pltpu.VMEM((tm + HALO, D_CONV), F32)],
        compiler_params=_cparams(("arbitrary",)),
    )(u, u, dxs_a, dxs_b, dbm, dcm, w, bias)

    def body2(dz_ref, n_ref, w_ref, du_ref, scr):
        last = pl.program_id(0) % per_seq == per_seq - 1
        scr[0:tm, :] = dz_ref[...]
        scr[tm:, :] = jnp.where(last, 0.0, n_ref[...])
        for c in range(D_CONV // LANES):
            cs = slice(c * LANES, (c + 1) * LANES)
            du_ref[:, cs] = _conv_taps(scr, w_ref, cs, CONV_WIDTH - 1, -1, tm).astype(du_ref.dtype)

    du = pl.pallas_call(
        body2, name="conv_bwd_du", grid=(n_tiles,),
        in_specs=[pl.BlockSpec((tm, D_CONV), lambda i: (i, 0)),
                  pl.BlockSpec((HALO, D_CONV), lambda i: (jnp.minimum((i + 1) * (tm // HALO), t // HALO - 1), 0)),
                  pl.BlockSpec((CONV_WIDTH, D_CONV), lambda i: (0, 0))],
        out_specs=pl.BlockSpec((tm, D_CONV), lambda i: (i, 0)),
        out_shape=jax.ShapeDtypeStruct((t, D_CONV), BF16),
        scratch_shapes=[pltpu.VMEM((tm + HALO, D_CONV), F32)],
        compiler_params=_cparams(("parallel",)),
    )(dz, dz, w)
    return du, dw, db


Q = SSD_CHUNK
N_PAIRS = D_SSD // LANES
HEADS_PER_GROUP = N_HEADS // SSD_GROUPS


def _rep(a, j):
    return jnp.broadcast_to(a[:, j:j + 1], a.shape)


def _dot_exact01(a, b, dn, a_is_01):
    x = b if a_is_01 else a
    hi = x.astype(BF16)
    mid = (x - hi.astype(F32)).astype(BF16)
    lo = (x - hi.astype(F32) - mid.astype(F32)).astype(BF16)
    z = a.astype(BF16) if a_is_01 else b.astype(BF16)
    out = None
    for term in (hi, mid, lo):
        d = _dot(z, term, dn) if a_is_01 else _dot(term, z, dn)
        out = d if out is None else out + d
    return out


def _pad_lanes(v, fill=0.0):
    row = jnp.pad(v.reshape(1, -1).astype(F32), ((0, 0), (0, LANES - v.size)), constant_values=fill)
    return row, row.reshape(LANES, 1)


def _ssd_common(dtr_ref, dtrt_ref, bias_r, bias_c, alog_r, alog_c):
    row = lax.broadcasted_iota(jnp.int32, (Q, Q), 0)
    col = lax.broadcasted_iota(jnp.int32, (Q, Q), 1)
    tril = row >= col
    lane = lax.broadcasted_iota(jnp.int32, (1, LANES), 1)
    a_r = jnp.where(lane < N_HEADS, -jnp.exp(alog_r[...]), 0.0)
    sub = lax.broadcasted_iota(jnp.int32, (LANES, 1), 0)
    a_c = jnp.where(sub < N_HEADS, -jnp.exp(alog_c[...]), 0.0)
    dt = _softplus(dtr_ref[...] + bias_r[...])
    cs = _dot_exact01(tril, dt * a_r, _NN, True)
    dtt = _softplus(dtrt_ref[...] + bias_c[...])
    cst = _dot_exact01(dtt * a_c, row <= col, _NN, False)
    return tril, lane, a_r, dt, cs, cst


def _ssd_specs(b, nc, rev):
    ci = (lambda c: nc - 1 - c) if rev else (lambda c: c)
    rows = lambda w: pl.BlockSpec((Q, w), lambda bb, c: (bb * nc + ci(c), 0))
    dtt = pl.BlockSpec((LANES, Q), lambda bb, c: (0, bb * nc + ci(c)))
    const = lambda s: pl.BlockSpec(s, lambda bb, c: (0,) * len(s))
    state = pl.BlockSpec((None, N_PAIRS, LANES, SSD_STATE), lambda bb, c: (bb * nc + ci(c), 0, 0, 0))
    return rows, dtt, const, state


def ssd_fwd(xs, bm, cm, dtraw, dt_bias, a_log, b):
    t = xs.shape[0]
    nc = SEQ // Q
    rows, dtt_spec, const, state = _ssd_specs(b, nc, False)
    bias_r, bias_c = _pad_lanes(dt_bias)
    alog_r, alog_c = _pad_lanes(a_log)

    def body(xs_ref, b_ref, c_ref, dtr_ref, dtrt_ref, br, bc, ar, ac, y_ref, hp_ref, h_scr):
        @pl.when(pl.program_id(1) == 0)
        def _():
            h_scr[...] = jnp.zeros(h_scr.shape, F32)
        tril, lane, _, dt, cs, cst = _ssd_common(dtr_ref, dtrt_ref, br, bc, ar, ac)
        sub = lax.broadcasted_iota(jnp.int32, (LANES, 1), 0)
        y_acc = [jnp.zeros((Q, LANES), F32) for _ in range(N_PAIRS)]
        h_old = [h_scr[p] for p in range(N_PAIRS)]
        h_new = [jnp.zeros((LANES, SSD_STATE), F32) for _ in range(N_PAIRS)]
        for g in range(SSD_GROUPS):
            bg = b_ref[:, g * SSD_STATE:(g + 1) * SSD_STATE].astype(BF16)
            cg = c_ref[:, g * SSD_STATE:(g + 1) * SSD_STATE].astype(BF16)
            cb = _dot(cg, bg, _NT)
            heads = []
            for j in range(g * HEADS_PER_GROUP, (g + 1) * HEADS_PER_GROUP):
                p, side = j // 2, j % 2
                m = (lane < HEAD_DIM) if side == 0 else (lane >= HEAD_DIM)
                ms = (sub < HEAD_DIM) if side == 0 else (sub >= HEAD_DIM)
                csj, dtj = _rep(cs, j), _rep(dt, j)
                lmat = jnp.exp(jnp.where(tril, csj - cst[j:j + 1, :], NEG))
                xdt = jnp.where(m, xs_ref[:, p * LANES:(p + 1) * LANES] * dtj, 0.0)
                hm = jnp.where(ms, h_old[p], 0.0)
                last = csj[Q - 1:Q, :]
                heads.append(dict(p=p, hm=hm, ecs=jnp.exp(csj), el=jnp.exp(last), gmat=(cb * lmat).astype(BF16),
                                  xdt=xdt.astype(BF16), xd=(xdt * jnp.exp(last - csj)).astype(BF16)))
            for h in heads:
                h.update(ydiag=_dot(h['gmat'], h['xdt'], _NN), ch=_dot(cg, h['hm'].astype(BF16), _NT), sj=_dot(h['xd'], bg, _TN))
            for h in heads:
                y_acc[h['p']] = y_acc[h['p']] + h['ydiag'] + h['ecs'] * h['ch']
                h_new[h['p']] = h_new[h['p']] + h['el'] * h['hm'] + h['sj']
        for p in range(N_PAIRS):
            y_ref[:, p * LANES:(p + 1) * LANES] = y_acc[p]
            hp_ref[p] = h_old[p]
            h_scr[p] = h_new[p]

    return pl.pallas_call(
        body, name="ssd_fwd", grid=(b, nc),
        in_specs=[rows(D_SSD), rows(D_BC), rows(D_BC), rows(LANES), dtt_spec, const((1, LANES)), const((LANES, 1)),
                  const((1, LANES)), const((LANES, 1))],
        out_specs=[rows(D_SSD), state],
        out_shape=[jax.ShapeDtypeStruct((t, D_SSD), F32),
                   jax.ShapeDtypeStruct((b * nc, N_PAIRS, LANES, SSD_STATE), F32)],
        scratch_shapes=[pltpu.VMEM((N_PAIRS, LANES, SSD_STATE), F32)],
        compiler_params=_cparams(("parallel", "arbitrary")),
    )(xs, bm, cm, dtraw, dtraw.T, bias_r, bias_c, alog_r, alog_c)


def ssd_bwd(xs, bm, cm, dtraw, dt_bias, a_log, hprev, dy, b):
    t = xs.shape[0]
    nc = SEQ // Q
    rows, dtt_spec, const, state = _ssd_specs(b, nc, True)
    bias_r, bias_c = _pad_lanes(dt_bias)
    alog_r, alog_c = _pad_lanes(a_log)

    def body(xs_ref, b_ref, c_ref, dtr_ref, dtrt_ref, hp_ref, dy_ref, br, bc, ar, ac,
             dxs_ref, db_ref, dc_ref, ddt_ref, dbias_ref, dalog_ref, dh_scr):
        first = jnp.logical_and(pl.program_id(0) == 0, pl.program_id(1) == 0)

        @pl.when(pl.program_id(1) == 0)
        def _():
            dh_scr[...] = jnp.zeros(dh_scr.shape, F32)

        @pl.when(first)
        def _():
            dbias_ref[...] = jnp.zeros(dbias_ref.shape, F32)
            dalog_ref[...] = jnp.zeros(dalog_ref.shape, F32)
        tril, lane, a_r, dt, cs, cst = _ssd_common(dtr_ref, dtrt_ref, br, bc, ar, ac)
        sub = lax.broadcasted_iota(jnp.int32, (LANES, 1), 0)
        rowq = lax.broadcasted_iota(jnp.int32, (Q, 1), 0)
        triu = (lax.broadcasted_iota(jnp.int32, (Q, Q), 0) <= lax.broadcasted_iota(jnp.int32, (Q, Q), 1)).astype(F32)
        dxs_acc = [jnp.zeros((Q, LANES), F32) for _ in range(N_PAIRS)]
        dh_in = [dh_scr[p] for p in range(N_PAIRS)]
        h_in = [hp_ref[p] for p in range(N_PAIRS)]
        dh_out = [jnp.zeros((LANES, SSD_STATE), F32) for _ in range(N_PAIRS)]
        ddt = jnp.zeros((Q, LANES), F32)
        dalog = jnp.zeros((1, LANES), F32)
        for g in range(SSD_GROUPS):
            gs = slice(g * SSD_STATE, (g + 1) * SSD_STATE)
            bg, cg = b_ref[:, gs].astype(BF16), c_ref[:, gs].astype(BF16)
            cb = _dot(cg, bg, _NT)
            dcb = jnp.zeros((Q, Q), F32)
            dbg = jnp.zeros((Q, SSD_STATE), F32)
            dcg = jnp.zeros((Q, SSD_STATE), F32)
            heads = []
            for j in range(g * HEADS_PER_GROUP, (g + 1) * HEADS_PER_GROUP):
                p, side = j // 2, j % 2
                m = (lane < HEAD_DIM) if side == 0 else (lane >= HEAD_DIM)
                ms = (sub < HEAD_DIM) if side == 0 else (sub >= HEAD_DIM)
                csj, dtj = _rep(cs, j), _rep(dt, j)
                lmat = jnp.exp(jnp.where(tril, csj - cst[j:j + 1, :], NEG))
                x2 = jnp.where(m, xs_ref[:, p * LANES:(p + 1) * LANES], 0.0)
                xdt = x2 * dtj
                dym = jnp.where(m, dy_ref[:, p * LANES:(p + 1) * LANES], 0.0)
                hm = jnp.where(ms, h_in[p], 0.0)
                dhm = jnp.where(ms, dh_in[p], 0.0)
                last = csj[Q - 1:Q, :]
                decay = jnp.exp(last - csj)
                heads.append(dict(j=j, p=p, dtj=dtj, lmat=lmat, x2=x2, hm=hm, dhm=dhm, decay=decay, el=jnp.exp(last),
                                  gmat=cb * lmat, dym=dym.astype(BF16), xdt=xdt.astype(BF16), hmb=hm.astype(BF16),
                                  dhmb=dhm.astype(BF16), dye=dym * jnp.exp(csj), xd=xdt * decay))
            for h in heads:
                dyeb, xdb = h['dye'].astype(BF16), h['xd'].astype(BF16)
                h.update(dg=_dot(h['dym'], h['xdt'], _NT),
                         dxdt=_dot(h['gmat'].astype(BF16), h['dym'], _TN),
                         ch=_dot(cg, h['hmb'], _NT),
                         dcg=_dot(dyeb, h['hmb'], _NN), dhp=_dot(dyeb, cg, _TN),
                         wmat=_dot(bg, h['dhmb'], _NT),
                         dbg=_dot(xdb, h['dhmb'], _NN))
            for h in heads:
                ej = h['dg'] * h['gmat']
                col_sums = jnp.broadcast_to(jnp.sum(ej, axis=0, keepdims=True), (Q, Q)).T
                dl = h['xd'] * h['wmat']
                total = lambda v: jnp.sum(jnp.sum(v, axis=0, keepdims=True), axis=1, keepdims=True)
                dlast = total(dl) + h['el'] * total(h['dhm'] * h['hm'])
                h['dcs'] = (jnp.sum(ej + h['dye'] * h['ch'] - dl, axis=1, keepdims=True) - col_sums
                            + jnp.where(rowq == Q - 1, dlast, 0.0))
                h['dxdt'] = h['dxdt'] + h['decay'] * h['wmat']
                dcb, dcg, dbg = dcb + h['dg'] * h['lmat'], dcg + h['dcg'], dbg + h['dbg']
                dh_out[h['p']] = dh_out[h['p']] + h['el'] * h['dhm'] + h['dhp']
            for h in heads:
                h['da'] = _dot_exact01(triu, h['dcs'], _NN, True)
            for h in heads:
                j, da = h['j'], h['da']
                aj = jnp.sum(jnp.where(lane == j, a_r, 0.0), axis=1, keepdims=True)
                ddtj = da * aj + jnp.sum(h['dxdt'] * h['x2'], axis=1, keepdims=True)
                ddt = ddt + jnp.where(lane == j, ddtj, 0.0)
                dalog = dalog + jnp.where(lane == j, jnp.sum(da * h['dtj'], axis=0, keepdims=True) * aj, 0.0)
                dxs_acc[h['p']] = dxs_acc[h['p']] + h['dxdt'] * h['dtj']
            dcbb = dcb.astype(BF16)
            dc_ref[:, gs] = dcg + _dot(dcbb, bg, _NN)
            db_ref[:, gs] = dbg + _dot(dcbb, cg, _TN)
        for p in range(N_PAIRS):
            dxs_ref[:, p * LANES:(p + 1) * LANES] = dxs_acc[p]
            dh_scr[p] = dh_out[p]
        ddtraw = ddt * _sigmoid(dtr_ref[...] + br[...])
        ddt_ref[...] = ddtraw
        dbias_ref[...] += jnp.sum(ddtraw, axis=0, keepdims=True)
        dalog_ref[...] += dalog

    return pl.pallas_call(
        body, name="ssd_bwd", grid=(b, nc),
        in_specs=[rows(D_SSD), rows(D_BC), rows(D_BC), rows(LANES), dtt_spec, state, rows(D_SSD), const((1, LANES)),
                  const((LANES, 1)), const((1, LANES)), const((LANES, 1))],
        out_specs=[rows(D_SSD), rows(D_BC), rows(D_BC), rows(LANES), const((1, LANES)), const((1, LANES))],
        out_shape=[jax.ShapeDtypeStruct((t, D_SSD), F32), jax.ShapeDtypeStruct((t, D_BC), F32),
                   jax.ShapeDtypeStruct((t, D_BC), F32), jax.ShapeDtypeStruct((t, LANES), F32),
                   jax.ShapeDtypeStruct((1, LANES), F32), jax.ShapeDtypeStruct((1, LANES), F32)],
        scratch_shapes=[pltpu.VMEM((N_PAIRS, LANES, SSD_STATE), F32)],
        compiler_params=_cparams(("arbitrary", "arbitrary")),
    )(xs, bm, cm, dtraw, dtraw.T, hprev, dy, bias_r, bias_c, alog_r, alog_c)


def _split_w_in(w_in):
    w_dt = jnp.pad(w_in[:, D_QKVZ + D_CONV:], ((0, 0), (0, LANES - N_HEADS)))
    return w_in[:, :D_QKVZ], w_in[:, D_QKVZ:D_QKVZ + D_CONV], w_dt


def mixer_fwd(hb, p, cosv, sinv, b):
    t = hb.shape[0]
    w_a, w_b, w_c = _split_w_in(p['w_in'])
    qkvz = mm("in_qkvz", [(hb, w_a, 'nn')], D_QKVZ)
    xbc = mm("in_xbc", [(hb, w_b, 'nn')], D_CONV)
    dtraw = mm("in_dt", [(hb, w_c, 'nn')], LANES)
    mixed, *lses = attn_fwd(qkvz, cosv, sinv, b)
    attn = attn_norm_fwd(mixed, p['attn_norm_w'])
    xs, bm, cm = conv_fwd(xbc, p['conv_w'], p['conv_b'])
    y, hprev = ssd_fwd(xs, bm, cm, dtraw, p['dt_bias'], p['a_log'], b)
    dskip = jnp.repeat(p['d_skip'].reshape(-1), HEAD_DIM).reshape(1, D_SSD)
    yg, = rowwise("ssd_gate", _gate, [y, xs, Op(qkvz, D_SSD, 3)], [dskip, p['ssd_norm_w']], [(t, D_SSD, BF16)])
    mix = mm("out_proj", [(attn, p['w_out'][:D_ATTN], 'nn'), (yg, p['w_out'][D_ATTN:], 'nn')], D_MODEL, out_dtype=BF16)
    res = dict(hb=hb, qkvz=qkvz, xbc=xbc, dtraw=dtraw, mixed=mixed, lses=lses, attn=attn, xs=xs, bm=bm, cm=cm,
               y=y, hprev=hprev, dskip=dskip, yg=yg, cosv=cosv, sinv=sinv)
    return mix, res


def mixer_bwd(r, p, dmix, dh_resid, b):
    t = dmix.shape[0]
    w_a, w_b, w_c = _split_w_in(p['w_in'])
    w_out = p['w_out']
    dattn = mm("out_bwd_dattn", [(dmix, w_out[:D_ATTN], 'nt')], D_ATTN)
    dyg = mm("out_bwd_dyg", [(dmix, w_out[D_ATTN:], 'nt')], D_SSD)
    dw_out = jnp.concatenate([mm_tn("out_bwd_dw_a", r['attn'], dmix, BF16),
                              mm_tn("out_bwd_dw_y", r['yg'], dmix, BF16)], axis=0)

    def gate_bwd(dy_, y_, xs_, z_, ds_, w_):
        _, vjp = jax.vjp(_gate, y_, xs_, z_, ds_, w_)
        return vjp(dy_)

    dy, dxs_a, dz, ddskip, dssd_norm = rowwise(
        "ssd_gate_bwd", gate_bwd, [dyg, r['y'], r['xs'], Op(r['qkvz'], D_SSD, 3)], [r['dskip'], p['ssd_norm_w']],
        [(t, D_SSD, F32), (t, D_SSD, F32), (t, D_SSD, BF16)], accs=[(1, D_SSD), (1, D_SSD)])
    dxs_b, dbm, dcm, ddtraw, ddt_bias, da_log = ssd_bwd(r['xs'], r['bm'], r['cm'], r['dtraw'], p['dt_bias'], p['a_log'],
                                                        r['hprev'], dy, b)
    dxbc, dconv_w, dconv_b = conv_bwd(r['xbc'], p['conv_w'], p['conv_b'], dxs_a, dxs_b, dbm, dcm)
    dmixed, dattn_norm = attn_norm_bwd(dattn, r['mixed'], p['attn_norm_w'])
    dq, dk, dv = attn_bwd(r['qkvz'], r['cosv'], r['sinv'], dmixed, r['mixed'], r['lses'], b)
    wq, wk, wv, wz = (w_a[:, i * D_ATTN:(i + 1) * D_ATTN] for i in range(4))
    dh = mm("in_bwd_dh", [(dq, wq, 'nt'), (dk, wk, 'nt'), (dv, wv, 'nt'), (dz, wz, 'nt'), (dxbc, w_b, 'nt'),
                          (ddtraw, w_c, 'nt')], D_MODEL, add=dh_resid, tn=512)
    h = r['hb']
    dw_in = jnp.concatenate([mm_tn_cat("in_bwd_dw_qkvz", h, [dq, dk, dv, dz], BF16),
                             mm_tn_cat("in_bwd_dw_xbc_dt", h, [dxbc, ddtraw], BF16)[:, :D_CONV + N_HEADS]], axis=1)
    head_sum = lambda v: v.reshape(N_HEADS, HEAD_DIM).sum(axis=1).reshape(1, N_HEADS)
    grads = dict(w_in=dw_in, w_out=dw_out, conv_w=dconv_w, conv_b=dconv_b, dt_bias=ddt_bias[:, :N_HEADS],
                 a_log=da_log[:, :N_HEADS], d_skip=head_sum(ddskip), attn_norm_w=dattn_norm, ssd_norm_w=dssd_norm)
    return dh, grads


FFN2_KEYS = ('ffn2_gate', 'ffn2_up', 'ffn2_down')
MIXER_KEYS = ('w_in', 'conv_w', 'w_out')
FFN_COL = ('ffn1_gate', 'ffn1_up', 'ffn2_gate', 'ffn2_up')
FFN_ROW = ('ffn1_down', 'ffn2_down')
CONV_W_COMM = (8, 2 * LANES)
SMALL = 'small'


def comm_shape(k, shapes):
    if k in FFN_COL:
        return (D_MODEL, FF_PAD)
    if k in FFN_ROW:
        return (FF_PAD, D_MODEL)
    if k == 'conv_w':
        return CONV_W_COMM
    return tuple(shapes[k][1:])


def to_comm(k, vals, shapes):
    a = vals[k].reshape(shapes[k][1:])
    r_, c_ = comm_shape(k, shapes)
    return jnp.pad(a, ((0, r_ - a.shape[0]), (0, c_ - a.shape[1])))


SMALL_ROWS, SMALL_COLS = 16, D_CONV


def pack_small(small):
    rows = [jnp.pad(small[r].reshape(1, -1), ((0, 0), (0, SMALL_COLS - small[r].size))) for r in REPLICATED]
    return jnp.concatenate(rows + [jnp.zeros((SMALL_ROWS - len(rows), SMALL_COLS), F32)], axis=0)


def full_weight(k, g):
    if k in FFN_COL:
        return g
    if k == 'conv_w':
        return jnp.transpose(g[:, :CONV_WIDTH, :D_CONV // N_DEV], (1, 0, 2)).reshape(CONV_WIDTH, D_CONV)
    return g.reshape(N_DEV * g.shape[1], g.shape[2])


def grad_shards(k, g):
    if k in FFN_COL:
        return g
    if k == 'conv_w':
        s = jnp.transpose(g.reshape(CONV_WIDTH, N_DEV, D_CONV // N_DEV), (1, 0, 2))
        return jnp.pad(s, ((0, 0), (0, CONV_W_COMM[0] - CONV_WIDTH), (0, CONV_W_COMM[1] - D_CONV // N_DEV)))
    return g.reshape(N_DEV, g.shape[0] // N_DEV, g.shape[1])


def _flip(v, bit):
    return 1 - v if bit else v


N_PEER_COPIES = N_DEV - 1


def _comm_call(name, body, arrs, out_shape):
    n = len(arrs)
    return pl.pallas_call(
        functools.partial(body, n), name=name, out_shape=out_shape,
        in_specs=[pl.BlockSpec(memory_space=pl.ANY)] * n, out_specs=[pl.BlockSpec(memory_space=pl.ANY)] * n,
        scratch_shapes=[pltpu.SemaphoreType.DMA((n * N_PEER_COPIES,)), pltpu.SemaphoreType.DMA((n * N_PEER_COPIES,)),
                        pltpu.SemaphoreType.DMA((n,))],
    )(*arrs)


def _blk(ref, idx, by_cols):
    if not by_cols:
        return ref.at[idx]
    c = ref.shape[1] // N_DEV
    return ref.at[:, pl.ds(pl.multiple_of(idx * c, LANES), c)]


def _blocked_shape(a, by_cols):
    return (a.shape[0], N_DEV * a.shape[1]) if by_cols else (N_DEV,) + a.shape


def all_gather(arrs, by_cols):
    def body(n, *refs):
        x_refs, out_refs, (send_sems, recv_sems, local_sems) = refs[:n], refs[n:2 * n], refs[2 * n:]
        x, y, c = lax.axis_index("x"), lax.axis_index("y"), lax.axis_index("c")
        me, sibling = (x, y, c), (x, y, 1 - c)
        chips = [(1 - x, y), (x, 1 - y), (1 - x, 1 - y)]

        def copy(a, k, block, to, src=None):
            px, py, pc = block
            dst = _blk(out_refs[a], 4 * px + 2 * py + pc, by_cols[a])
            return pltpu.make_async_remote_copy(
                src_ref=dst if src is None else src, dst_ref=dst, send_sem=send_sems.at[a * N_PEER_COPIES + k],
                recv_sem=recv_sems.at[a * N_PEER_COPIES + k], device_id=to, device_id_type=MESH)

        mine = [pltpu.make_async_copy(x_refs[a], _blk(out_refs[a], 4 * x + 2 * y + c, by_cols[a]), local_sems.at[a])
                for a in range(n)]
        started = []
        for a in range(n):
            mine[a].start()
            first = [copy(a, 0, me, sibling, src=x_refs[a])]
            first += [copy(a, 1 + j, me, (*chip, c), src=x_refs[a]) for j, chip in enumerate(chips)]
            for cp in first:
                cp.start()
            started += first
        for j, chip in enumerate(chips):
            for a in range(n):
                copy(a, 1 + j, (*chip, c), me).wait_recv()
                cp = copy(a, 4 + j, (*chip, c), sibling)
                cp.start()
                started.append(cp)
        for a in range(n):
            copy(a, 0, sibling, me).wait_recv()
            for j, chip in enumerate(chips):
                copy(a, 4 + j, (*chip, 1 - c), me).wait_recv()
        for cp in started:
            cp.wait_send()
        for cp in mine:
            cp.wait()

    return _comm_call("all_gather_weights", body, arrs,
                      [jax.ShapeDtypeStruct(_blocked_shape(a, bc), a.dtype) for a, bc in zip(arrs, by_cols)])


def _landing_shape(a, by_cols):
    return (N_DEV, a.shape[0], a.shape[1] // N_DEV) if by_cols else a.shape


_HBM = pl.BlockSpec(memory_space=pltpu.HBM)
_SEM = pl.BlockSpec(memory_space=pltpu.SEMAPHORE)
_EFFECT = pltpu.SideEffectType.DATAFLOW_SIDE_EFFECTING


def _peer(k):
    x, y, c = lax.axis_index("x"), lax.axis_index("y"), lax.axis_index("c")
    return _flip(x, k & 4), _flip(y, k & 2), _flip(c, k & 1)


def _my_index():
    return 4 * lax.axis_index("x") + 2 * lax.axis_index("y") + lax.axis_index("c")


def _split_copies(mode, by_cols, src_refs, land_refs, send_sems, recv_sems):
    me = _my_index()
    out = []
    for a, bc in enumerate(by_cols):
        for k in range(1, N_DEV):
            px, py, pc = _peer(k)
            src = _blk(src_refs[a], 4 * px + 2 * py + pc, bc) if mode == 'scatter' else src_refs[a]
            dst = land_refs[a].at[me] if mode == 'scatter' else _blk(land_refs[a], me, bc)
            out.append(pltpu.make_async_remote_copy(
                src_ref=src, dst_ref=dst, send_sem=send_sems.at[a * N_PEER_COPIES + k - 1],
                recv_sem=recv_sems.at[a * N_PEER_COPIES + k - 1], device_id=(px, py, pc), device_id_type=MESH))
    return out


def exchange_start(name, mode, srcs, by_cols):
    n = len(srcs)
    lands = [lax.empty(_landing_shape(s, bc) if mode == 'scatter' else _blocked_shape(s, bc), s.dtype)
             for s, bc in zip(srcs, by_cols)]

    def body(*refs):
        src_refs, land_refs, send_sems, recv_sems = refs[:n], refs[n:2 * n], refs[2 * n], refs[2 * n + 1]
        for cp in _split_copies(mode, by_cols, src_refs, land_refs, send_sems, recv_sems):
            cp.start()
        refs[-1][...] = jnp.zeros(refs[-1].shape, F32)

    sems = pltpu.SemaphoreType.DMA((n * N_PEER_COPIES,))
    res = pl.pallas_call(
        body, name=name,
        out_shape=(sems, sems, *[pltpu.HBM(a.shape, a.dtype) for a in srcs + lands], jax.ShapeDtypeStruct((8, LANES), F32)),
        in_specs=(_HBM,) * (2 * n), out_specs=(_SEM, _SEM, *(_HBM,) * (2 * n), pl.BlockSpec(memory_space=pltpu.VMEM)),
        input_output_aliases={i: 2 + i for i in range(2 * n)},
        compiler_params=pltpu.CompilerParams(has_side_effects=_EFFECT),
    )(*[pltpu.with_memory_space_constraint(a, pltpu.HBM) for a in srcs + lands])
    return (mode, by_cols, res[:-1]), res[-1]


def exchange_wait(name, handles, after):
    mode, by_cols, (send_sems, recv_sems, *bufs) = handles
    n = len(by_cols)

    def body(*refs):
        src_refs, land_refs, s_sems, r_sems = refs[:n], refs[n:2 * n], refs[2 * n], refs[2 * n + 1]
        for cp in _split_copies(mode, by_cols, src_refs, land_refs, s_sems, r_sems):
            cp.wait_send()
            cp.wait_recv()

    res = pl.pallas_call(
        body, name=name, out_shape=tuple(pltpu.HBM(a.shape, a.dtype) for a in bufs),
        in_specs=(*(_HBM,) * (2 * n), _SEM, _SEM, pl.BlockSpec(memory_space=pl.ANY)), out_specs=(_HBM,) * (2 * n),
        input_output_aliases={i: i for i in range(2 * n)},
        compiler_params=pltpu.CompilerParams(has_side_effects=_EFFECT),
    )(*bufs, send_sems, recv_sems, after)
    me, out = _my_index(), []
    for src, land, bc in zip(res[:n], res[n:], by_cols):
        if mode == 'scatter':
            c = land.shape[2]
            own = lax.dynamic_slice(src, (0, me * c), (src.shape[0], c)) if bc else lax.dynamic_index_in_dim(src, me, 0, False)
            out.append(lax.dynamic_update_slice(land, own[None], (me, 0, 0)))
        elif bc:
            out.append(lax.dynamic_update_slice(land, src, (0, me * src.shape[1])))
        else:
            out.append(lax.dynamic_update_slice(land, src[None], (me, 0, 0)))
    return out


def _adamw_math(g, w, m, v):
    c1 = 1.0 / (1.0 - ADAM_B1 ** ADAM_STEP)
    c2 = 1.0 / (1.0 - ADAM_B2 ** ADAM_STEP)
    m = ADAM_B1 * m + (1.0 - ADAM_B1) * g
    v = ADAM_B2 * v + (1.0 - ADAM_B2) * jnp.square(g)
    return g, -ADAM_LR * ((m * c1) / (jnp.sqrt(v * c2) + ADAM_EPS) + ADAM_WD * w), m, v


def adamw(name, recv, w, m, v, tm):
    _, rows, cols = w.shape
    tm = min(tm, rows)

    def body(*refs):
        g = refs[0][0:tm, 0:cols].astype(F32)
        for s in range(1, N_DEV):
            g = g + refs[s][0:tm, 0:cols].astype(F32)
        res = _adamw_math(g, *[r[...] for r in refs[N_DEV:N_DEV + 3]])
        for r, val in zip(refs[N_DEV + 3:], res):
            r[...] = val

    part = lambda s: pl.BlockSpec((None, recv.shape[1] if tm == rows else tm, recv.shape[2]), lambda i: (s, i, 0))
    tile = pl.BlockSpec((None, tm, cols), lambda i: (0, i, 0))
    return pl.pallas_call(
        body, name=name, grid=(rows // tm,), in_specs=[part(s) for s in range(N_DEV)] + [tile] * 3, out_specs=[tile] * 4,
        out_shape=[jax.ShapeDtypeStruct((1, rows, cols), F32)] * 4, compiler_params=_cparams(("parallel",)),
    )(*[recv] * N_DEV, w, m, v)


def adamw_small(recv, wl, ml, vl):
    n = len(REPLICATED)

    def body(recv_ref, *refs):
        g = recv_ref[0]
        for s in range(1, N_DEV):
            g = g + recv_ref[s]
        for r in range(n):
            w, m, v = (refs[j * n + r][...] for j in range(3))
            for j, val in enumerate(_adamw_math(g[r:r + 1, :w.shape[1]], w, m, v)):
                refs[(3 + j) * n + r][...] = val

    arrs = [d[k].reshape(1, -1) for d in (wl, ml, vl) for k in REPLICATED]
    res = pl.pallas_call(
        body, name="adamw_small", out_shape=[jax.ShapeDtypeStruct(a.shape, F32) for a in arrs[:n]] * 4,
    )(recv, *arrs)
    return [{k: res[j * n + r].reshape(wl[k].shape) for r, k in enumerate(REPLICATED)} for j in range(4)]


ADAMW_TM = {'ffn1_gate': 256, 'ffn1_up': 256, 'ffn2_gate': 256, 'ffn2_up': 256, 'w_in': 32}


def kernel(x, positions, ln1_g, ln1_b, ffn1_gate, ffn1_up, ffn1_down, w_in, conv_w, conv_b, dt_bias, a_log, d_skip, attn_norm_w, ssd_norm_w, w_out, ln2_g, ln2_b, ffn2_gate, ffn2_up, ffn2_down, ln3_g, ln3_b, loss_target, m_ln1_g, m_ln1_b, m_ffn1_gate, m_ffn1_up, m_ffn1_down, m_w_in, m_conv_w, m_conv_b, m_dt_bias, m_a_log, m_d_skip, m_attn_norm_w, m_ssd_norm_w, m_w_out, m_ln2_g, m_ln2_b, m_ffn2_gate, m_ffn2_up, m_ffn2_down, m_ln3_g, m_ln3_b, v_ln1_g, v_ln1_b, v_ffn1_gate, v_ffn1_up, v_ffn1_down, v_w_in, v_conv_w, v_conv_b, v_dt_bias, v_a_log, v_d_skip, v_attn_norm_w, v_ssd_norm_w, v_w_out, v_ln2_g, v_ln2_b, v_ffn2_gate, v_ffn2_up, v_ffn2_down, v_ln3_g, v_ln3_b):
    args = dict(locals())
    wl = {k: args[k] for k in WEIGHTS}
    ml = {k: args["m_" + k] for k in WEIGHTS}
    vl = {k: args["v_" + k] for k in WEIGHTS}
    shapes = {k: wl[k].shape for k in WEIGHTS}
    b, s, dm = x.shape
    t = b * s

    sent = {k: to_comm(k, wl, shapes).astype(F32 if k == 'conv_w' else BF16) for k in SHARDED}
    by_cols = lambda keys: [k in FFN_COL for k in keys]
    gate, up = all_gather([sent['ffn1_gate'], sent['ffn1_up']], [True] * 2)
    (gate, up), sent = lax.optimization_barrier(((gate, up), sent))
    p = {'ffn1_gate': gate, 'ffn1_up': up}
    gather_down, token_d = exchange_start("gather_ffn1_down_start", 'gather', [sent['ffn1_down']], [False])
    sent['w_in'] = sent['w_in'] + token_d[0, 0].astype(BF16)
    gather_mixer, token_m = exchange_start("gather_mixer_start", 'gather', [sent[k] for k in MIXER_KEYS], by_cols(MIXER_KEYS))
    sent['ffn2_gate'] = sent['ffn2_gate'] + token_m[0, 0].astype(BF16)
    gather_ffn2, token_f = exchange_start("gather_ffn2_start", 'gather', [sent[k] for k in FFN2_KEYS], by_cols(FFN2_KEYS))
    for k in REPLICATED:
        p[k] = wl[k].reshape(1, -1)

    x2 = x.reshape(t, dm)
    cosv, sinv = rope_tables(positions)
    g1, u1, a1, at1 = ffn_gate_up("ffn1_gate_up", x2, p['ffn1_gate'], p['ffn1_up'], after=(token_d, token_m, token_f))
    p['ffn1_down'] = full_weight('ffn1_down', exchange_wait("gather_ffn1_down_wait", gather_down, a1)[0])
    f1, res1 = mm("ffn1_down", [(a1, p['ffn1_down'], 'nn')], D_MODEL, out_dtype=BF16), (x2, g1, u1, at1)
    h1, h1b = resid_ln_fwd("ln1", 0.5, x2, f1, p['ln1_g'], p['ln1_b'])
    for k, g in zip(MIXER_KEYS, exchange_wait("gather_mixer_wait", gather_mixer, h1b)):
        p[k] = full_weight(k, g)
    mix, resm = mixer_fwd(h1b, p, cosv, sinv, b)
    h2, h2b = resid_ln_fwd("ln2", 1.0, h1, mix, p['ln2_g'], p['ln2_b'])
    for k, g in zip(FFN2_KEYS, exchange_wait("gather_ffn2_wait", gather_ffn2, h2b)):
        p[k] = full_weight(k, g)
    f2, res3 = ffn_fwd("ffn2", h2b, p['ffn2_gate'], p['ffn2_up'], p['ffn2_down'])

    small, full = {}, {}
    dh2_res, df2, small['ln3_g'], small['ln3_b'], sq = ln_loss_bwd("ln3_loss_bwd", h2, f2, loss_target.reshape(t, dm),
                                                                   p['ln3_g'], p['ln3_b'])
    loss = lax.psum(jnp.sum(sq) * (0.5 / dm), AXES)

    dh2, full['ffn2_gate'], full['ffn2_up'], full['ffn2_down'] = ffn_bwd("ffn2", res3, p['ffn2_gate'], p['ffn2_up'],
                                                                       p['ffn2_down'], df2, dh2_res)
    ffn2_exchange, token = exchange_start("grads_ffn2_start", 'scatter', [grad_shards(k, full[k]) for k in FFN2_KEYS],
                                          by_cols(FFN2_KEYS))
    dh1_res, dmix, small['ln2_g'], small['ln2_b'] = resid_ln_bwd("ln2_bwd", 1.0, h1, mix, p['ln2_g'] + token[:1, :1],
                                                                 p['ln2_b'], dh2)
    dh1, gm = mixer_bwd(resm, p, dmix, dh1_res, b)
    for k in ('conv_b', 'dt_bias', 'a_log', 'd_skip', 'attn_norm_w', 'ssd_norm_w'):
        small[k] = gm[k]
    mixer_exchange, token = exchange_start("grads_mixer_start", 'scatter', [grad_shards(k, gm[k]) for k in MIXER_KEYS],
                                           by_cols(MIXER_KEYS))
    dx_res, df1, small['ln1_g'], small['ln1_b'] = resid_ln_bwd("ln1_bwd", 0.5, x2, f1, p['ln1_g'] + token[:1, :1],
                                                               p['ln1_b'], dh1)
    hb, g, u, at = res1
    small_part = pack_small(small)
    dg, du = ffn_da_act("ffn1_bwd_da_act", df1, p['ffn1_down'], g, u)
    dwd = mm_acc("ffn1_bwd_dwd", at, df1, BF16, after=dg)
    down_exchange, token = exchange_start("grads_ffn1_down_start", 'scatter', [
        grad_shards('ffn1_down', dwd), jnp.broadcast_to(small_part[None], (N_DEV,) + small_part.shape)], [False, False])
    dwg = mm_tn("ffn1_bwd_dwg", hb, dg, BF16, after=token)
    gate_exchange, token = exchange_start("grads_ffn1_gate_start", 'scatter', [grad_shards('ffn1_gate', dwg)], [True])
    dwu = mm_tn("ffn1_bwd_dwu", hb, du, BF16, after=token)
    up_exchange, token = exchange_start("grads_ffn1_up_start", 'scatter', [grad_shards('ffn1_up', dwu)], [True])
    dx = mm("ffn1_bwd_dh", [(dg, p['ffn1_gate'], 'nt'), (du, p['ffn1_up'], 'nt')], D_MODEL, add=dx_res, tn=512, after=token)
    recv = {}
    for keys, name, ex in (((FFN2_KEYS), "grads_ffn2_wait", ffn2_exchange), (MIXER_KEYS, "grads_mixer_wait", mixer_exchange),
                           (('ffn1_down', SMALL), "grads_ffn1_down_wait", down_exchange),
                           (('ffn1_gate',), "grads_ffn1_gate_wait", gate_exchange),
                           (('ffn1_up',), "grads_ffn1_up_wait", up_exchange)):
        recv.update(zip(keys, exchange_wait(name, ex, dx)))
    outs = adamw_small(recv.pop(SMALL), wl, ml, vl)
    for k, r in recv.items():
        for o, a in zip(outs, adamw(f"adamw_{k}", r, wl[k], ml[k], vl[k], ADAMW_TM.get(k, shapes[k][1]))):
            o[k] = a
    return (loss, dx.reshape(b, s, dm), *[o[k] for o in outs for k in WEIGHTS])
```

```python
import functools

import jax
import jax.numpy as jnp
import numpy as np
from jax import lax
from jax.experimental import pallas as pl
from jax.experimental.pallas import tpu as pltpu

F32, BF16 = jnp.float32, jnp.bfloat16
HI = lax.Precision.HIGHEST
MESH = pl.DeviceIdType.MESH
AXES = ("x", "y", "c")
N_DEV = 8

D_MODEL = 1024
SEQ = 2048
HEAD_DIM = 64
N_HEADS = 12
D_ATTN = N_HEADS * HEAD_DIM
DILATIONS = (1, 4, 16)
ATTN_BLOCK = 128
ROPE_THETA = 500000.0
ROPE_DIM = 16
D_SSD = 768
SSD_GROUPS = 4
SSD_STATE = 128
SSD_CHUNK = 128
D_BC = SSD_GROUPS * SSD_STATE
D_CONV = D_SSD + 2 * D_BC
CONV_WIDTH = 4
D_QKVZ = 3 * D_ATTN + D_SSD
D_FF = 2816
ALPHA = 2.0 ** 0.25
LN_EPS = 1e-5
RMS_EPS = 1e-6
ADAM_LR, ADAM_B1, ADAM_B2, ADAM_EPS, ADAM_WD, ADAM_STEP = 0.001, 0.9, 0.999, 1e-08, 0.01, 10

LANES = 128
VMEM_LIMIT = 52 * 1024 * 1024
NEG = -1e30

WEIGHTS = ['ln1_g', 'ln1_b', 'ffn1_gate', 'ffn1_up', 'ffn1_down', 'w_in', 'conv_w', 'conv_b', 'dt_bias', 'a_log',
           'd_skip', 'attn_norm_w', 'ssd_norm_w', 'w_out', 'ln2_g', 'ln2_b', 'ffn2_gate', 'ffn2_up', 'ffn2_down',
           'ln3_g', 'ln3_b']
COL_SHARDED = ('ffn1_gate', 'ffn1_up', 'conv_w', 'ffn2_gate', 'ffn2_up')
ROW_SHARDED = ('ffn1_down', 'w_in', 'w_out', 'ffn2_down')
SHARDED = tuple(n for n in WEIGHTS if n in COL_SHARDED or n in ROW_SHARDED)
REPLICATED = tuple(n for n in WEIGHTS if n not in SHARDED)
FF_SHARD = D_FF // N_DEV
FF_PAD = -(-FF_SHARD // LANES) * LANES


def _cparams(sem=None):
    return pltpu.CompilerParams(dimension_semantics=sem, vmem_limit_bytes=VMEM_LIMIT)


def _tile(n, prefs):
    for p in prefs:
        if n % p == 0:
            return p
    return n


class Op:
    def __init__(self, arr, bw=None, cb=0, ro=0):
        self.arr, self.bw, self.cb, self.ro = arr, (arr.shape[1] if bw is None else bw), cb, ro


def _op(a):
    return a if isinstance(a, Op) else Op(a)


def rowwise(name, fn, ins, consts, outs, accs=(), tm=512):
    ins = [_op(a) for a in ins]
    rows = outs[0][0]
    n_in, n_c, n_o, n_a = len(ins), len(consts), len(outs), len(accs)
    tm = min(tm, rows)
    assert rows % tm == 0, (name, rows, tm)

    def body(*refs):
        vals = [r[...].astype(F32) for r in refs[:n_in + n_c]]
        res = fn(*vals)
        res = res if isinstance(res, (tuple, list)) else (res,)
        o_refs = refs[n_in + n_c:n_in + n_c + n_o]
        a_refs = refs[n_in + n_c + n_o:]
        for r, v in zip(o_refs, res[:n_o]):
            r[...] = v.astype(r.dtype)
        if n_a:
            @pl.when(pl.program_id(0) == 0)
            def _():
                for r in a_refs:
                    r[...] = jnp.zeros(r.shape, r.dtype)
            for r, v in zip(a_refs, res[n_o:]):
                r[...] += v

    in_specs = [pl.BlockSpec((tm, o.bw), functools.partial(lambda i, o: (i + o.ro, o.cb), o=o)) for o in ins]
    in_specs += [pl.BlockSpec(c.shape, functools.partial(lambda i, nd: (0,) * nd, nd=c.ndim)) for c in consts]
    out_specs = [pl.BlockSpec((tm, w), lambda i: (i, 0)) for (_, w, _) in outs]
    out_specs += [pl.BlockSpec(s, functools.partial(lambda i, nd: (0,) * nd, nd=len(s))) for s in accs]
    out_shape = [jax.ShapeDtypeStruct((r, w), dt) for (r, w, dt) in outs]
    out_shape += [jax.ShapeDtypeStruct(s, F32) for s in accs]
    res = pl.pallas_call(
        body, name=name, grid=(rows // tm,), in_specs=in_specs, out_specs=out_specs, out_shape=out_shape,
        compiler_params=_cparams(("arbitrary",) if n_a else ("parallel",)),
    )(*[o.arr for o in ins], *consts)
    return res


MM_TM = 1024
MM_TN = (1024, 896, 768, 512, 256, 128)
_NT = (((1,), (1,)), ((), ()))
_NN = (((1,), (0,)), ((), ()))
_TN = (((0,), (0,)), ((), ()))


def _dot(a, b, dn, precision=None):
    return lax.dot_general(a, b, dn, preferred_element_type=F32, precision=precision)


def _mm_specs(name, pairs, n_out, tm, tn):
    in_specs, args = [], []
    for a, b, mode in pairs:
        o = _op(a)
        in_specs.append(pl.BlockSpec((tm, o.bw), functools.partial(lambda j, i, o: (i, o.cb), o=o)))
        args.append(o.arr)
        if mode == 'nn':
            assert b.shape == (o.bw, n_out), (name, b.shape, o.bw, n_out)
            in_specs.append(pl.BlockSpec((o.bw, tn), lambda j, i: (0, j)))
        else:
            assert b.shape == (n_out, o.bw), (name, b.shape, o.bw, n_out)
            in_specs.append(pl.BlockSpec((tn, o.bw), lambda j, i: (j, 0)))
        args.append(b)
    return in_specs, args


def _mm_acc(refs, pairs):
    acc = None
    for k, (_, _, mode) in enumerate(pairs):
        d = _dot(refs[2 * k][...].astype(BF16), refs[2 * k + 1][...].astype(BF16), _NN if mode == 'nn' else _NT)
        acc = d if acc is None else acc + d
    return acc


def mm(name, pairs, n_out, add=None, out_dtype=F32, tm=MM_TM, tn=None, after=None):
    m = _op(pairs[0][0]).arr.shape[0]
    tn = tn or _tile(n_out, MM_TN)
    n_p = len(pairs)

    def body(*refs):
        acc = _mm_acc(refs, pairs)
        if add is not None:
            acc = acc + refs[2 * n_p][...]
        refs[-1][...] = acc.astype(refs[-1].dtype)

    in_specs, args = _mm_specs(name, pairs, n_out, tm, tn)
    tile = pl.BlockSpec((tm, tn), lambda j, i: (i, j))
    if add is not None:
        in_specs.append(tile)
        args.append(add)
    if after is not None:
        in_specs.append(pl.BlockSpec(memory_space=pl.ANY))
        args.append(after)
    return pl.pallas_call(
        body, name=name, grid=(n_out // tn, m // tm), in_specs=in_specs, out_specs=tile,
        out_shape=jax.ShapeDtypeStruct((m, n_out), out_dtype),
        compiler_params=_cparams(("parallel", "parallel")),
    )(*args)


def mm_tn(name, a, b, out_dtype=F32, tt=2048, after=None):
    a, b = _op(a), _op(b)
    t = a.arr.shape[0]
    k, n = a.bw, b.bw
    tk = _tile(k, (512, 896, 768, 256, 128))
    tn = _tile(n, (3072, 1792) + MM_TN)
    tt = min(tt, t)
    n_t = t // tt
    order = [] if after is None else [after]

    def body(a_ref, b_ref, *rest):
        o_ref, acc_ref = rest[-2:]
        s = pl.program_id(2)
        d = _dot(a_ref[...].astype(BF16), b_ref[...].astype(BF16), _TN)

        @pl.when(s == 0)
        def _():
            acc_ref[...] = d

        @pl.when(s > 0)
        def _():
            acc_ref[...] += d

        @pl.when(s == n_t - 1)
        def _():
            o_ref[...] = acc_ref[...].astype(o_ref.dtype)

    return pl.pallas_call(
        body, name=name, grid=(k // tk, n // tn, n_t),
        in_specs=[pl.BlockSpec((tt, tk), functools.partial(lambda kk, nn, s, o: (s, o.cb * (o.bw // tk) + kk), o=a)),
                  pl.BlockSpec((tt, tn), functools.partial(lambda kk, nn, s, o: (s, o.cb * (o.bw // tn) + nn), o=b))]
        + [pl.BlockSpec(memory_space=pl.ANY) for _ in order],
        out_specs=pl.BlockSpec((tk, tn), lambda kk, nn, s: (kk, nn)),
        out_shape=jax.ShapeDtypeStruct((k, n), out_dtype),
        scratch_shapes=[pltpu.VMEM((tk, tn), F32)],
        compiler_params=_cparams(("parallel", "parallel", "arbitrary")),
    )(a.arr, b.arr, *order)


def _sigmoid(x):
    return 1.0 / (1.0 + jnp.exp(-x))


def _silu(x):
    return x * _sigmoid(x)


def _softplus(x):
    return jnp.maximum(x, 0.0) + jnp.log(1.0 + jnp.exp(-jnp.abs(x)))


def _resid_ln(scale, h, branch, g, b):
    r = ALPHA * h + scale * branch
    mu = jnp.mean(r, axis=-1, keepdims=True)
    var = jnp.mean(jnp.square(r - mu), axis=-1, keepdims=True)
    return (r - mu) * lax.rsqrt(var + LN_EPS) * g + b


def _rms(t, w):
    return t * lax.rsqrt(jnp.mean(t * t, axis=-1, keepdims=True) + RMS_EPS) * w


def _branch_weights(l1, l2, l3):
    m = jnp.maximum(jnp.maximum(l1, l2), l3)
    e1, e2, e3 = jnp.exp(l1 - m), jnp.exp(l2 - m), jnp.exp(l3 - m)
    inv = 1.0 / (e1 + e2 + e3)
    return e1 * inv, e2 * inv, e3 * inv


def _gate(y, xs, z, dskip, w):
    return _rms((y + dskip * xs) * _silu(z), w)


def _rot(x):
    d = lax.broadcasted_iota(jnp.int32, x.shape, 1) % HEAD_DIM
    up = pltpu.roll(x, x.shape[1] - ROPE_DIM // 2, 1)
    down = jnp.where(d < ROPE_DIM, pltpu.roll(x, ROPE_DIM // 2, 1), 0.0)
    return jnp.where(d < ROPE_DIM // 2, up, down)


def ffn_gate_up(name, h, wg, wu, after=()):
    m, nf = h.shape[0], wg.shape[1]
    tn = _tile(nf, MM_TN)

    def body(h_ref, g_w, u_w, *rest):
        du_ref, dg_ref, a_ref, at_ref = rest[-4:]
        hb = h_ref[...].astype(BF16)
        g = _dot(hb, g_w[...].astype(BF16), _NN)
        u = _dot(hb, u_w[...].astype(BF16), _NN)
        sig = _sigmoid(g)
        gs = g * sig
        du_ref[...] = gs.astype(du_ref.dtype)
        dg_ref[...] = (u * (sig + gs * (1.0 - sig))).astype(dg_ref.dtype)
        a = gs * u
        a_ref[...] = a.astype(a_ref.dtype)
        at_ref[...] = a.T.astype(at_ref.dtype)

    in_specs, args = _mm_specs(name, [(h, wg, 'nn')], nf, MM_TM, tn)
    in_specs.append(in_specs[1])
    in_specs += [pl.BlockSpec(memory_space=pl.ANY) for _ in after]
    tile = pl.BlockSpec((MM_TM, tn), lambda j, i: (i, j))
    return pl.pallas_call(
        body, name=name, grid=(nf // tn, m // MM_TM), in_specs=in_specs,
        out_specs=[tile] * 3 + [pl.BlockSpec((tn, MM_TM), lambda j, i: (j, i))],
        out_shape=[jax.ShapeDtypeStruct((m, nf), BF16)] * 3 + [jax.ShapeDtypeStruct((nf, m), BF16)],
        compiler_params=_cparams(("parallel", "parallel")),
    )(*args, wu, *after)


def mm_tn_cat(name, a, bs, out_dtype=F32, tt=2048):
    t, k = a.shape
    widths = [b.shape[1] for b in bs]
    n, tk, tt = sum(widths), _tile(k, (512, 256, 128)), min(tt, t)
    n_t = t // tt

    def body(a_ref, *rest):
        b_refs, o_ref, acc_ref = rest[:len(bs)], rest[-2], rest[-1]
        s = pl.program_id(1)
        at = a_ref[...].astype(BF16)
        d = jnp.concatenate([_dot(at, b[...].astype(BF16), _TN) for b in b_refs], axis=1)

        @pl.when(s == 0)
        def _():
            acc_ref[...] = d

        @pl.when(s > 0)
        def _():
            acc_ref[...] += d

        @pl.when(s == n_t - 1)
        def _():
            o_ref[...] = acc_ref[...].astype(o_ref.dtype)

    return pl.pallas_call(
        body, name=name, grid=(k // tk, n_t),
        in_specs=[pl.BlockSpec((tt, tk), lambda kk, s: (s, kk))] + [pl.BlockSpec((tt, w), lambda kk, s: (s, 0)) for w in widths],
        out_specs=pl.BlockSpec((tk, n), lambda kk, s: (kk, 0)),
        out_shape=jax.ShapeDtypeStruct((k, n), out_dtype), scratch_shapes=[pltpu.VMEM((tk, n), F32)],
        compiler_params=_cparams(("parallel", "arbitrary")),
    )(a, *bs)


def mm_acc(name, a, b, out_dtype=F32, tt=2048, after=None):
    k, t = a.shape
    n = b.shape[1]
    tk, tn, tt = _tile(k, (1024, 512, 256, 128)), _tile(n, MM_TN), min(tt, t)
    n_t = t // tt
    order = [] if after is None else [after]

    def body(a_ref, b_ref, *rest):
        o_ref, acc_ref = rest[-2:]
        s = pl.program_id(2)
        d = _dot(a_ref[...].astype(BF16), b_ref[...].astype(BF16), _NN)

        @pl.when(s == 0)
        def _():
            acc_ref[...] = d

        @pl.when(s > 0)
        def _():
            acc_ref[...] += d

        @pl.when(s == n_t - 1)
        def _():
            o_ref[...] = acc_ref[...].astype(o_ref.dtype)

    return pl.pallas_call(
        body, name=name, grid=(k // tk, n // tn, n_t),
        in_specs=[pl.BlockSpec((tk, tt), lambda kk, nn, s: (kk, s)), pl.BlockSpec((tt, tn), lambda kk, nn, s: (s, nn))]
        + [pl.BlockSpec(memory_space=pl.ANY) for _ in order],
        out_specs=pl.BlockSpec((tk, tn), lambda kk, nn, s: (kk, nn)),
        out_shape=jax.ShapeDtypeStruct((k, n), out_dtype), scratch_shapes=[pltpu.VMEM((tk, tn), F32)],
        compiler_params=_cparams(("parallel", "parallel", "arbitrary")),
    )(a, b, *order)


def ffn_da_act(name, df, wd, a_du, a_dg):
    m, nf = df.shape[0], wd.shape[0]
    tn = _tile(nf, MM_TN)

    def body(df_ref, w_ref, adu_ref, adg_ref, dg_ref, du_ref):
        da = _dot(df_ref[...].astype(BF16), w_ref[...].astype(BF16), _NT)
        dg_ref[...] = (da * adg_ref[...].astype(F32)).astype(dg_ref.dtype)
        du_ref[...] = (da * adu_ref[...].astype(F32)).astype(du_ref.dtype)

    in_specs, args = _mm_specs(name, [(df, wd, 'nt')], nf, MM_TM, tn)
    tile = pl.BlockSpec((MM_TM, tn), lambda j, i: (i, j))
    return pl.pallas_call(
        body, name=name, grid=(nf // tn, m // MM_TM), in_specs=in_specs + [tile, tile], out_specs=[tile] * 2,
        out_shape=[jax.ShapeDtypeStruct((m, nf), BF16)] * 2, compiler_params=_cparams(("parallel", "parallel")),
    )(*args, a_du, a_dg)


def resid_ln_fwd(name, scale, h, branch, ln_g, ln_b):
    t = h.shape[0]

    def fn(*a):
        y = _resid_ln(scale, *a)
        return y, y

    return rowwise(name, fn, [h, branch], [ln_g, ln_b], [(t, D_MODEL, F32), (t, D_MODEL, BF16)], tm=1024)


def ffn_fwd(tag, hb, wg, wu, wd, after=()):
    g, u, a, at = ffn_gate_up(f"{tag}_gate_up", hb, wg, wu, after)
    f = mm(f"{tag}_down", [(a, wd, 'nn')], D_MODEL, out_dtype=BF16)
    return f, (hb, g, u, at)


def ln_loss_bwd(name, h, branch, target, ln_g, ln_b):
    t, dm = h.shape

    def fn(h_, br_, tgt, g_, b_):
        y, vjp = jax.vjp(functools.partial(_resid_ln, 0.5), h_, br_, g_, b_)
        e = y - tgt
        return (*vjp(e * (1.0 / dm)), jnp.sum(e * e, axis=0, keepdims=True))

    return rowwise(name, fn, [h, branch, target], [ln_g, ln_b], [(t, dm, F32), (t, dm, BF16)],
                   accs=[(1, dm), (1, dm), (1, dm)], tm=1024)


def resid_ln_bwd(name, scale, h, branch, ln_g, ln_b, dout, extra=None):
    t = h.shape[0]

    def fn(h_, br_, do_, *rest):
        g_, b_ = rest[-2], rest[-1]
        _, vjp = jax.vjp(functools.partial(_resid_ln, scale), h_, br_, g_, b_)
        dh, dbr, dg, db = vjp(do_)
        if extra is not None:
            dh = dh + rest[0]
        return dh, dbr, dg, db

    ins = [h, branch, dout] + ([extra] if extra is not None else [])
    return rowwise(name, fn, ins, [ln_g, ln_b], [(t, D_MODEL, F32), (t, D_MODEL, BF16)],
                   accs=[(1, D_MODEL), (1, D_MODEL)], tm=1024)


def ffn_bwd(tag, res, wg, wu, wd, df, dh_resid):
    hb, g, u, at = res
    dg, du = ffn_da_act(f"{tag}_bwd_da_act", df, wd, g, u)
    dwd = mm_acc(f"{tag}_bwd_dwd", at, df, BF16)
    dh = mm(f"{tag}_bwd_dh", [(dg, wg, 'nt'), (du, wu, 'nt')], D_MODEL, add=dh_resid, tn=512)
    dwg = mm_tn(f"{tag}_bwd_dwg", hb, dg, BF16)
    dwu = mm_tn(f"{tag}_bwd_dwu", hb, du, BF16)
    return dh, dwg, dwu, dwd


def rope_tables(positions):
    inv_freq = ROPE_THETA ** (-jnp.arange(0, ROPE_DIM, 2, dtype=F32) / ROPE_DIM)
    ang = positions.reshape(-1, 1).astype(F32) * inv_freq
    c, s = jnp.cos(ang), jnp.sin(ang)
    t = ang.shape[0]
    cosv = jnp.concatenate([c, c, jnp.ones((t, HEAD_DIM - ROPE_DIM), F32)], axis=1)
    sinv = jnp.concatenate([-s, s, jnp.zeros((t, HEAD_DIM - ROPE_DIM), F32)], axis=1)
    return jnp.tile(cosv, (1, 2)), jnp.tile(sinv, (1, 2))


def _pair_masks():
    lane = lax.broadcasted_iota(jnp.int32, (1, LANES), 1)
    return (lane < HEAD_DIM, lane >= HEAD_DIM)


def _band_masks():
    row = lax.broadcasted_iota(jnp.int32, (ATTN_BLOCK, ATTN_BLOCK), 0)
    col = lax.broadcasted_iota(jnp.int32, (ATTN_BLOCK, ATTN_BLOCK), 1)
    return col >= row, col <= row


def _residue_blocks():
    out = []
    for g, d in enumerate(DILATIONS):
        for r in range(d):
            for i in range(SEQ // d // ATTN_BLOCK):
                rows = lambda j: pl.ds(r + j * ATTN_BLOCK * d, ATTN_BLOCK, stride=d) if d > 1 else pl.ds(j * ATTN_BLOCK, ATTN_BLOCK)
                out.append((g, rows(i), rows(i - 1) if i > 0 else None))
    return out


N_HEAD_PAIRS = D_ATTN // LANES
SCALE = HEAD_DIM ** -0.5
ATTN_GROUP = 4
ATTN_GROUP_BWD = 16


def _block_operands(qr, kr, v_ref, cur, prev):
    prev_ok, cur_ok = _band_masks()
    if prev is None:
        return qr[cur, :], kr[cur, :].astype(BF16), v_ref[cur, :], cur_ok
    kcat = jnp.concatenate([kr[prev, :], kr[cur, :]], axis=0).astype(BF16)
    vcat = jnp.concatenate([v_ref[prev, :], v_ref[cur, :]], axis=0)
    return qr[cur, :], kcat, vcat, jnp.concatenate([prev_ok, cur_ok], axis=1)


def _attn_specs(b):
    col = lambda cb: pl.BlockSpec((SEQ, LANES), lambda bb, hp: (bb, cb + hp))
    tab = pl.BlockSpec((SEQ, LANES), lambda bb, hp: (bb, 0))
    return col, tab


def attn_fwd(qkvz, cosv, sinv, b):
    t = qkvz.shape[0]
    col, tab = _attn_specs(b)
    blocks = _residue_blocks()

    def body(q_ref, k_ref, v_ref, c_ref, s_ref, o_ref, l1_ref, l2_ref, l3_ref, qr, kr, o1, o2, o3):
        l_refs, o_scr = (l1_ref, l2_ref, l3_ref), (o1, o2, o3)
        c, s = c_ref[...], s_ref[...]
        q, k = q_ref[...], k_ref[...]
        qr[...] = q * c + _rot(q) * s
        kr[...] = k * c + _rot(k) * s
        masks = _pair_masks()
        for lo in range(0, len(blocks), ATTN_GROUP):
            chains = []
            for g, cur, prev in blocks[lo:lo + ATTN_GROUP]:
                q2, kcat, vcat, ok = _block_operands(qr, kr, v_ref, cur, prev)
                for m in masks:
                    qm = jnp.where(m, q2, 0.0).astype(BF16)
                    chains.append(dict(g=g, cur=cur, m=m, v=jnp.where(m, vcat, 0.0).astype(BF16),
                                       s=jnp.where(ok, _dot(qm, kcat, _NT) * SCALE, NEG)))
            for ch in chains:
                mx = jnp.max(ch['s'], axis=1, keepdims=True)
                p = jnp.exp(ch['s'] - mx)
                den = jnp.sum(p, axis=1, keepdims=True)
                ch.update(p=p.astype(BF16), inv=1.0 / den, lse=mx + jnp.log(den))
            for ch in chains:
                ch['o'] = _dot(ch['p'], ch['v'], _NN) * ch['inv']
            for c0, c1 in zip(chains[0::2], chains[1::2]):
                o_scr[c0['g']][c0['cur'], :] = c0['o'] + c1['o']
                l_refs[c0['g']][c0['cur'], :] = jnp.where(c0['m'], c0['lse'], c1['lse'])
        w1, w2, w3 = _branch_weights(l1_ref[...], l2_ref[...], l3_ref[...])
        o_ref[...] = w1 * o1[...] + w2 * o2[...] + w3 * o3[...]

    shp = jax.ShapeDtypeStruct((t, D_ATTN), F32)
    return pl.pallas_call(
        body, name="attn_fwd", grid=(b, N_HEAD_PAIRS),
        in_specs=[col(0), col(N_HEAD_PAIRS), col(2 * N_HEAD_PAIRS), tab, tab],
        out_specs=[col(0)] * 4, out_shape=[shp] * 4,
        scratch_shapes=[pltpu.VMEM((SEQ, LANES), F32)] * 5,
        compiler_params=_cparams(("parallel", "parallel")),
    )(qkvz, qkvz, qkvz, cosv, sinv)


def attn_bwd(qkvz, cosv, sinv, dmix, mixed, lses, b):
    t = qkvz.shape[0]
    col, tab = _attn_specs(b)
    blocks = _residue_blocks()
    hd = np.arange(LANES) // HEAD_DIM
    head_ones = jnp.asarray((hd[:, None] == hd[None, :]).astype(np.float32))

    def body(q_ref, k_ref, v_ref, c_ref, s_ref, dm_ref, mx_ref, l1_ref, l2_ref, l3_ref, ones_ref,
             dq_out, dk_out, dv_out, qr, kr, do1, do2, do3, dd1, dd2, dd3, dq_ref, dk_ref, dv_ref):
        l_refs, do_scr, dd_scr = (l1_ref, l2_ref, l3_ref), (do1, do2, do3), (dd1, dd2, dd3)
        c, s = c_ref[...], s_ref[...]
        q, k = q_ref[...], k_ref[...]
        qr[...] = q * c + _rot(q) * s
        kr[...] = k * c + _rot(k) * s
        dm = dm_ref[...]
        tot = _dot(dm * mx_ref[...], ones_ref[...], _NN, HI)
        for w, do_g, dd_g in zip(_branch_weights(l1_ref[...], l2_ref[...], l3_ref[...]), do_scr, dd_scr):
            do_g[...] = w * dm
            dd_g[...] = w * tot
        dq_ref[...] = jnp.zeros((SEQ, LANES), F32)
        dk_ref[...] = jnp.zeros((SEQ, LANES), F32)
        dv_ref[...] = jnp.zeros((SEQ, LANES), F32)
        masks = _pair_masks()
        for lo in range(0, len(blocks), ATTN_GROUP_BWD):
            chains = []
            for g, cur, prev in blocks[lo:lo + ATTN_GROUP_BWD]:
                q2, kcat, vcat, ok = _block_operands(qr, kr, v_ref, cur, prev)
                vcat = vcat.astype(BF16)
                do2_, l2, dd2_ = do_scr[g][cur, :], l_refs[g][cur, :], dd_scr[g][cur, :]
                l2s, dd2s = pltpu.roll(l2, HEAD_DIM, 1), pltpu.roll(dd2_, HEAD_DIM, 1)
                for m in masks:
                    qm = jnp.where(m, q2, 0.0).astype(BF16)
                    dom = jnp.where(m, do2_, 0.0).astype(BF16)
                    lrep, ddrep = jnp.where(m, l2, l2s), jnp.where(m, dd2_, dd2s)
                    if prev is not None:
                        lrep, ddrep = jnp.concatenate([lrep, lrep], axis=1), jnp.concatenate([ddrep, ddrep], axis=1)
                    chains.append(dict(cur=cur, prev=prev, qm=qm, dom=dom, km=jnp.where(m, kcat, 0), lrep=lrep, ddrep=ddrep,
                                       s=jnp.where(ok, _dot(qm, kcat, _NT) * SCALE, NEG), dp=_dot(dom, vcat, _NT)))
            for ch in chains:
                p = jnp.exp(ch['s'] - ch['lrep'])
                ch.update(p=p.astype(BF16), ds=(p * (ch['dp'] - ch['ddrep']) * SCALE).astype(BF16))
            for ch in chains:
                ch.update(dq=_dot(ch['ds'], ch['km'], _NN), dk=_dot(ch['ds'], ch['qm'], _TN), dv=_dot(ch['p'], ch['dom'], _TN))
            for c0, c1 in zip(chains[0::2], chains[1::2]):
                cur, prev = c0['cur'], c0['prev']
                dk, dv = c0['dk'] + c1['dk'], c0['dv'] + c1['dv']
                dq_ref[cur, :] += c0['dq'] + c1['dq']
                if prev is None:
                    dk_ref[cur, :] += dk
                    dv_ref[cur, :] += dv
                else:
                    dk_ref[prev, :] += dk[:ATTN_BLOCK]
                    dv_ref[prev, :] += dv[:ATTN_BLOCK]
                    dk_ref[cur, :] += dk[ATTN_BLOCK:]
                    dv_ref[cur, :] += dv[ATTN_BLOCK:]
        dq, dk = dq_ref[...], dk_ref[...]
        dq_out[...] = (dq * c + _rot(dq * s)).astype(dq_out.dtype)
        dk_out[...] = (dk * c + _rot(dk * s)).astype(dk_out.dtype)
        dv_out[...] = dv_ref[...].astype(dv_out.dtype)

    shp = jax.ShapeDtypeStruct((t, D_ATTN), BF16)
    return pl.pallas_call(
        body, name="attn_bwd", grid=(b, N_HEAD_PAIRS),
        in_specs=[col(0), col(N_HEAD_PAIRS), col(2 * N_HEAD_PAIRS), tab, tab, col(0), col(0), col(0), col(0), col(0),
                  pl.BlockSpec((LANES, LANES), lambda bb, hp: (0, 0))],
        out_specs=[col(0)] * 3, out_shape=[shp] * 3,
        scratch_shapes=[pltpu.VMEM((SEQ, LANES), F32)] * 11,
        compiler_params=_cparams(("parallel", "parallel")),
    )(qkvz, qkvz, qkvz, cosv, sinv, dmix, mixed, *lses, head_ones)


def attn_norm_fwd(mixed, norm_w):
    return rowwise("attn_norm", _rms, [mixed], [norm_w], [(mixed.shape[0], D_ATTN, BF16)])[0]


def attn_norm_bwd(dout, mixed, norm_w):
    def fn(dy, mx, w):
        _, vjp = jax.vjp(_rms, mx, w)
        return vjp(dy)

    return rowwise("attn_norm_bwd", fn, [dout, mixed], [norm_w], [(dout.shape[0], D_ATTN, F32)], accs=[(1, D_ATTN)])


CONV_TM = 512
HALO = 8


def _conv_columns(refs):
    xs_ref, bm_ref, cm_ref = refs
    out = []
    for c in range(D_CONV // LANES):
        lo = c * LANES
        ref, base = (xs_ref, 0) if lo < D_SSD else (bm_ref, D_SSD) if lo < D_SSD + D_BC else (cm_ref, D_SSD + D_BC)
        out.append((slice(lo, lo + LANES), (ref, slice(lo - base, lo - base + LANES))))
    return out


def _conv_taps(scr, w_ref, cs, first_row, step, tm):
    acc = None
    for k in range(CONV_WIDTH):
        term = w_ref[k:k + 1, cs] * scr[pl.ds(first_row + step * k, tm), cs]
        acc = term if acc is None else acc + term
    return acc


def conv_fwd(u, w, bias):
    t = u.shape[0]
    tm, per_seq = CONV_TM, SEQ // CONV_TM

    def body(u_ref, h_ref, w_ref, b_ref, xs_ref, bm_ref, cm_ref, scr):
        first = pl.program_id(0) % per_seq == 0
        scr[0:HALO, :] = jnp.where(first, 0.0, h_ref[...])
        scr[HALO:, :] = u_ref[...]
        for cs, (o_ref, os_) in _conv_columns((xs_ref, bm_ref, cm_ref)):
            o_ref[:, os_] = _silu(_conv_taps(scr, w_ref, cs, HALO - CONV_WIDTH + 1, 1, tm) + b_ref[:, cs])

    return pl.pallas_call(
        body, name="conv_fwd", grid=(t // tm,),
        in_specs=[pl.BlockSpec((tm, D_CONV), lambda i: (i, 0)),
                  pl.BlockSpec((HALO, D_CONV), lambda i: (jnp.maximum(i * (tm // HALO) - 1, 0), 0)),
                  pl.BlockSpec((CONV_WIDTH, D_CONV), lambda i: (0, 0)), pl.BlockSpec((1, D_CONV), lambda i: (0, 0))],
        out_specs=[pl.BlockSpec((tm, D_SSD), lambda i: (i, 0)), pl.BlockSpec((tm, D_BC), lambda i: (i, 0)),
                   pl.BlockSpec((tm, D_BC), lambda i: (i, 0))],
        out_shape=[jax.ShapeDtypeStruct((t, D_SSD), F32), jax.ShapeDtypeStruct((t, D_BC), F32),
                   jax.ShapeDtypeStruct((t, D_BC), F32)],
        scratch_shapes=[pltpu.VMEM((tm + HALO, D_CONV), F32)],
        compiler_params=_cparams(("parallel",)),
    )(u, u, w, bias)


def conv_bwd(u, w, bias, dxs_a, dxs_b, dbm, dcm):
    t = u.shape[0]
    tm, per_seq = CONV_TM, SEQ // CONV_TM
    n_tiles = t // tm

    def body1(u_ref, h_ref, dxs_ref, dxs2_ref, dbm_ref, dcm_ref, w_ref, b_ref, dz_ref, dw_ref, db_ref, scr):
        i = pl.program_id(0)
        first = i % per_seq == 0
        scr[0:HALO, :] = jnp.where(first, 0.0, h_ref[...])
        scr[HALO:, :] = u_ref[...]

        @pl.when(i == 0)
        def _():
            dw_ref[...] = jnp.zeros(dw_ref.shape, F32)
            db_ref[...] = jnp.zeros(db_ref.shape, F32)
        for cs, (g_ref, gs) in _conv_columns((dxs_ref, dbm_ref, dcm_ref)):
            acc = _conv_taps(scr, w_ref, cs, HALO - CONV_WIDTH + 1, 1, tm) + b_ref[:, cs]
            sig = _sigmoid(acc)
            dy = g_ref[:, gs] + dxs2_ref[:, gs] if g_ref is dxs_ref else g_ref[:, gs]
            dz = dy * sig * (1.0 + acc * (1.0 - sig))
            dz_ref[:, cs] = dz
            db_ref[:, cs] += jnp.sum(dz, axis=0, keepdims=True)
            for k in range(CONV_WIDTH):
                dw_ref[k:k + 1, cs] += jnp.sum(dz * scr[pl.ds(HALO - CONV_WIDTH + 1 + k, tm), cs], axis=0, keepdims=True)

    dz, dw, db = pl.pallas_call(
        body1, name="conv_bwd_dz", grid=(n_tiles,),
        in_specs=[pl.BlockSpec((tm, D_CONV), lambda i: (i, 0)),
                  pl.BlockSpec((HALO, D_CONV), lambda i: (jnp.maximum(i * (tm // HALO) - 1, 0), 0)),
                  pl.BlockSpec((tm, D_SSD), lambda i: (i, 0)), pl.BlockSpec((tm, D_SSD), lambda i: (i, 0)),
                  pl.BlockSpec((tm, D_BC), lambda i: (i, 0)), pl.BlockSpec((tm, D_BC), lambda i: (i, 0)),
                  pl.BlockSpec((CONV_WIDTH, D_CONV), lambda i: (0, 0)), pl.BlockSpec((1, D_CONV), lambda i: (0, 0))],
        out_specs=[pl.BlockSpec((tm, D_CONV), lambda i: (i, 0)), pl.BlockSpec((CONV_WIDTH, D_CONV), lambda i: (0, 0)),
                   pl.BlockSpec((1, D_CONV), lambda i: (0, 0))],
        out_shape=[jax.ShapeDtypeStruct((t, D_CONV), F32), jax.ShapeDtypeStruct((CONV_WIDTH, D_CONV), F32),
                   jax.ShapeDtypeStruct((1, D_CONV), F32)],
        scratch_shapes=[pltpu.VMEM((tm + HALO, D_CONV), F32)],
        compiler_params=_cparams(("arbitrary",)),
    )(u, u, dxs_a, dxs_b, dbm, dcm, w, bias)

    def body2(dz_ref, n_ref, w_ref, du_ref, scr):
        last = pl.program_id(0) % per_seq == per_seq - 1
        scr[0:tm, :] = dz_ref[...]
        scr[tm:, :] = jnp.where(last, 0.0, n_ref[...])
        for c in range(D_CONV // LANES):
            cs = slice(c * LANES, (c + 1) * LANES)
            du_ref[:, cs] = _conv_taps(scr, w_ref, cs, CONV_WIDTH - 1, -1, tm).astype(du_ref.dtype)

    du = pl.pallas_call(
        body2, name="conv_bwd_du", grid=(n_tiles,),
        in_specs=[pl.BlockSpec((tm, D_CONV), lambda i: (i, 0)),
                  pl.BlockSpec((HALO, D_CONV), lambda i: (jnp.minimum((i + 1) * (tm // HALO), t // HALO - 1), 0)),
                  pl.BlockSpec((CONV_WIDTH, D_CONV), lambda i: (0, 0))],
        out_specs=pl.BlockSpec((tm, D_CONV), lambda i: (i, 0)),
        out_shape=jax.ShapeDtypeStruct((t, D_CONV), BF16),
        scratch_shapes=[pltpu.VMEM((tm + HALO, D_CONV), F32)],
        compiler_params=_cparams(("parallel",)),
    )(dz, dz, w)
    return du, dw, db


Q = SSD_CHUNK
N_PAIRS = D_SSD // LANES
HEADS_PER_GROUP = N_HEADS // SSD_GROUPS


def _rep(a, j):
    return jnp.broadcast_to(a[:, j:j + 1], a.shape)


def _dot_exact01(a, b, dn, a_is_01):
    x = b if a_is_01 else a
    hi = x.astype(BF16)
    mid = (x - hi.astype(F32)).astype(BF16)
    lo = (x - hi.astype(F32) - mid.astype(F32)).astype(BF16)
    z = a.astype(BF16) if a_is_01 else b.astype(BF16)
    out = None
    for term in (hi, mid, lo):
        d = _dot(z, term, dn) if a_is_01 else _dot(term, z, dn)
        out = d if out is None else out + d
    return out


def _pad_lanes(v, fill=0.0):
    row = jnp.pad(v.reshape(1, -1).astype(F32), ((0, 0), (0, LANES - v.size)), constant_values=fill)
    return row, row.reshape(LANES, 1)


def _ssd_common(dtr_ref, dtrt_ref, bias_r, bias_c, alog_r, alog_c):
    row = lax.broadcasted_iota(jnp.int32, (Q, Q), 0)
    col = lax.broadcasted_iota(jnp.int32, (Q, Q), 1)
    tril = row >= col
    lane = lax.broadcasted_iota(jnp.int32, (1, LANES), 1)
    a_r = jnp.where(lane < N_HEADS, -jnp.exp(alog_r[...]), 0.0)
    sub = lax.broadcasted_iota(jnp.int32, (LANES, 1), 0)
    a_c = jnp.where(sub < N_HEADS, -jnp.exp(alog_c[...]), 0.0)
    dt = _softplus(dtr_ref[...] + bias_r[...])
    cs = _dot_exact01(tril, dt * a_r, _NN, True)
    dtt = _softplus(dtrt_ref[...] + bias_c[...])
    cst = _dot_exact01(dtt * a_c, row <= col, _NN, False)
    return tril, lane, a_r, dt, cs, cst


def _ssd_specs(b, nc, rev):
    ci = (lambda c: nc - 1 - c) if rev else (lambda c: c)
    rows = lambda w: pl.BlockSpec((Q, w), lambda bb, c: (bb * nc + ci(c), 0))
    dtt = pl.BlockSpec((LANES, Q), lambda bb, c: (0, bb * nc + ci(c)))
    const = lambda s: pl.BlockSpec(s, lambda bb, c: (0,) * len(s))
    state = pl.BlockSpec((None, N_PAIRS, LANES, SSD_STATE), lambda bb, c: (bb * nc + ci(c), 0, 0, 0))
    return rows, dtt, const, state


def ssd_fwd(xs, bm, cm, dtraw, dt_bias, a_log, b):
    t = xs.shape[0]
    nc = SEQ // Q
    rows, dtt_spec, const, state = _ssd_specs(b, nc, False)
    bias_r, bias_c = _pad_lanes(dt_bias)
    alog_r, alog_c = _pad_lanes(a_log)

    def body(xs_ref, b_ref, c_ref, dtr_ref, dtrt_ref, br, bc, ar, ac, y_ref, hp_ref, h_scr):
        @pl.when(pl.program_id(1) == 0)
        def _():
            h_scr[...] = jnp.zeros(h_scr.shape, F32)
        tril, lane, _, dt, cs, cst = _ssd_common(dtr_ref, dtrt_ref, br, bc, ar, ac)
        sub = lax.broadcasted_iota(jnp.int32, (LANES, 1), 0)
        y_acc = [jnp.zeros((Q, LANES), F32) for _ in range(N_PAIRS)]
        h_old = [h_scr[p] for p in range(N_PAIRS)]
        h_new = [jnp.zeros((LANES, SSD_STATE), F32) for _ in range(N_PAIRS)]
        for g in range(SSD_GROUPS):
            bg = b_ref[:, g * SSD_STATE:(g + 1) * SSD_STATE].astype(BF16)
            cg = c_ref[:, g * SSD_STATE:(g + 1) * SSD_STATE].astype(BF16)
            cb = _dot(cg, bg, _NT)
            heads = []
            for j in range(g * HEADS_PER_GROUP, (g + 1) * HEADS_PER_GROUP):
                p, side = j // 2, j % 2
                m = (lane < HEAD_DIM) if side == 0 else (lane >= HEAD_DIM)
                ms = (sub < HEAD_DIM) if side == 0 else (sub >= HEAD_DIM)
                csj, dtj = _rep(cs, j), _rep(dt, j)
                lmat = jnp.exp(jnp.where(tril, csj - cst[j:j + 1, :], NEG))
                xdt = jnp.where(m, xs_ref[:, p * LANES:(p + 1) * LANES] * dtj, 0.0)
                hm = jnp.where(ms, h_old[p], 0.0)
                last = csj[Q - 1:Q, :]
                heads.append(dict(p=p, hm=hm, ecs=jnp.exp(csj), el=jnp.exp(last), gmat=(cb * lmat).astype(BF16),
                                  xdt=xdt.astype(BF16), xd=(xdt * jnp.exp(last - csj)).astype(BF16)))
            for h in heads:
                h.update(ydiag=_dot(h['gmat'], h['xdt'], _NN), ch=_dot(cg, h['hm'].astype(BF16), _NT), sj=_dot(h['xd'], bg, _TN))
            for h in heads:
                y_acc[h['p']] = y_acc[h['p']] + h['ydiag'] + h['ecs'] * h['ch']
                h_new[h['p']] = h_new[h['p']] + h['el'] * h['hm'] + h['sj']
        for p in range(N_PAIRS):
            y_ref[:, p * LANES:(p + 1) * LANES] = y_acc[p]
            hp_ref[p] = h_old[p]
            h_scr[p] = h_new[p]

    return pl.pallas_call(
        body, name="ssd_fwd", grid=(b, nc),
        in_specs=[rows(D_SSD), rows(D_BC), rows(D_BC), rows(LANES), dtt_spec, const((1, LANES)), const((LANES, 1)),
                  const((1, LANES)), const((LANES, 1))],
        out_specs=[rows(D_SSD), state],
        out_shape=[jax.ShapeDtypeStruct((t, D_SSD), F32),
                   jax.ShapeDtypeStruct((b * nc, N_PAIRS, LANES, SSD_STATE), F32)],
        scratch_shapes=[pltpu.VMEM((N_PAIRS, LANES, SSD_STATE), F32)],
        compiler_params=_cparams(("parallel", "arbitrary")),
    )(xs, bm, cm, dtraw, dtraw.T, bias_r, bias_c, alog_r, alog_c)


def ssd_bwd(xs, bm, cm, dtraw, dt_bias, a_log, hprev, dy, b):
    t = xs.shape[0]
    nc = SEQ // Q
    rows, dtt_spec, const, state = _ssd_specs(b, nc, True)
    bias_r, bias_c = _pad_lanes(dt_bias)
    alog_r, alog_c = _pad_lanes(a_log)

    def body(xs_ref, b_ref, c_ref, dtr_ref, dtrt_ref, hp_ref, dy_ref, br, bc, ar, ac,
             dxs_ref, db_ref, dc_ref, ddt_ref, dbias_ref, dalog_ref, dh_scr):
        first = jnp.logical_and(pl.program_id(0) == 0, pl.program_id(1) == 0)

        @pl.when(pl.program_id(1) == 0)
        def _():
            dh_scr[...] = jnp.zeros(dh_scr.shape, F32)

        @pl.when(first)
        def _():
            dbias_ref[...] = jnp.zeros(dbias_ref.shape, F32)
            dalog_ref[...] = jnp.zeros(dalog_ref.shape, F32)
        tril, lane, a_r, dt, cs, cst = _ssd_common(dtr_ref, dtrt_ref, br, bc, ar, ac)
        sub = lax.broadcasted_iota(jnp.int32, (LANES, 1), 0)
        rowq = lax.broadcasted_iota(jnp.int32, (Q, 1), 0)
        triu = (lax.broadcasted_iota(jnp.int32, (Q, Q), 0) <= lax.broadcasted_iota(jnp.int32, (Q, Q), 1)).astype(F32)
        dxs_acc = [jnp.zeros((Q, LANES), F32) for _ in range(N_PAIRS)]
        dh_in = [dh_scr[p] for p in range(N_PAIRS)]
        h_in = [hp_ref[p] for p in range(N_PAIRS)]
        dh_out = [jnp.zeros((LANES, SSD_STATE), F32) for _ in range(N_PAIRS)]
        ddt = jnp.zeros((Q, LANES), F32)
        dalog = jnp.zeros((1, LANES), F32)
        for g in range(SSD_GROUPS):
            gs = slice(g * SSD_STATE, (g + 1) * SSD_STATE)
            bg, cg = b_ref[:, gs].astype(BF16), c_ref[:, gs].astype(BF16)
            cb = _dot(cg, bg, _NT)
            dcb = jnp.zeros((Q, Q), F32)
            dbg = jnp.zeros((Q, SSD_STATE), F32)
            dcg = jnp.zeros((Q, SSD_STATE), F32)
            heads = []
            for j in range(g * HEADS_PER_GROUP, (g + 1) * HEADS_PER_GROUP):
                p, side = j // 2, j % 2
                m = (lane < HEAD_DIM) if side == 0 else (lane >= HEAD_DIM)
                ms = (sub < HEAD_DIM) if side == 0 else (sub >= HEAD_DIM)
                csj, dtj = _rep(cs, j), _rep(dt, j)
                lmat = jnp.exp(jnp.where(tril, csj - cst[j:j + 1, :], NEG))
                x2 = jnp.where(m, xs_ref[:, p * LANES:(p + 1) * LANES], 0.0)
                xdt = x2 * dtj
                dym = jnp.where(m, dy_ref[:, p * LANES:(p + 1) * LANES], 0.0)
                hm = jnp.where(ms, h_in[p], 0.0)
                dhm = jnp.where(ms, dh_in[p], 0.0)
                last = csj[Q - 1:Q, :]
                decay = jnp.exp(last - csj)
                heads.append(dict(j=j, p=p, dtj=dtj, lmat=lmat, x2=x2, hm=hm, dhm=dhm, decay=decay, el=jnp.exp(last),
                                  gmat=cb * lmat, dym=dym.astype(BF16), xdt=xdt.astype(BF16), hmb=hm.astype(BF16),
                                  dhmb=dhm.astype(BF16), dye=dym * jnp.exp(csj), xd=xdt * decay))
            for h in heads:
                dyeb, xdb = h['dye'].astype(BF16), h['xd'].astype(BF16)
                h.update(dg=_dot(h['dym'], h['xdt'], _NT),
                         dxdt=_dot(h['gmat'].astype(BF16), h['dym'], _TN),
                         ch=_dot(cg, h['hmb'], _NT),
                         dcg=_dot(dyeb, h['hmb'], _NN), dhp=_dot(dyeb, cg, _TN),
                         wmat=_dot(bg, h['dhmb'], _NT),
                         dbg=_dot(xdb, h['dhmb'], _NN))
            for h in heads:
                ej = h['dg'] * h['gmat']
                col_sums = jnp.broadcast_to(jnp.sum(ej, axis=0, keepdims=True), (Q, Q)).T
                dl = h['xd'] * h['wmat']
                total = lambda v: jnp.sum(jnp.sum(v, axis=0, keepdims=True), axis=1, keepdims=True)
                dlast = total(dl) + h['el'] * total(h['dhm'] * h['hm'])
                h['dcs'] = (jnp.sum(ej + h['dye'] * h['ch'] - dl, axis=1, keepdims=True) - col_sums
                            + jnp.where(rowq == Q - 1, dlast, 0.0))
                h['dxdt'] = h['dxdt'] + h['decay'] * h['wmat']
                dcb, dcg, dbg = dcb + h['dg'] * h['lmat'], dcg + h['dcg'], dbg + h['dbg']
                dh_out[h['p']] = dh_out[h['p']] + h['el'] * h['dhm'] + h['dhp']
            for h in heads:
                h['da'] = _dot_exact01(triu, h['dcs'], _NN, True)
            for h in heads:
                j, da = h['j'], h['da']
                aj = jnp.sum(jnp.where(lane == j, a_r, 0.0), axis=1, keepdims=True)
                ddtj = da * aj + jnp.sum(h['dxdt'] * h['x2'], axis=1, keepdims=True)
                ddt = ddt + jnp.where(lane == j, ddtj, 0.0)
                dalog = dalog + jnp.where(lane == j, jnp.sum(da * h['dtj'], axis=0, keepdims=True) * aj, 0.0)
                dxs_acc[h['p']] = dxs_acc[h['p']] + h['dxdt'] * h['dtj']
            dcbb = dcb.astype(BF16)
            dc_ref[:, gs] = dcg + _dot(dcbb, bg, _NN)
            db_ref[:, gs] = dbg + _dot(dcbb, cg, _TN)
        for p in range(N_PAIRS):
            dxs_ref[:, p * LANES:(p + 1) * LANES] = dxs_acc[p]
            dh_scr[p] = dh_out[p]
        ddtraw = ddt * _sigmoid(dtr_ref[...] + br[...])
        ddt_ref[...] = ddtraw
        dbias_ref[...] += jnp.sum(ddtraw, axis=0, keepdims=True)
        dalog_ref[...] += dalog

    return pl.pallas_call(
        body, name="ssd_bwd", grid=(b, nc),
        in_specs=[rows(D_SSD), rows(D_BC), rows(D_BC), rows(LANES), dtt_spec, state, rows(D_SSD), const((1, LANES)),
                  const((LANES, 1)), const((1, LANES)), const((LANES, 1))],
        out_specs=[rows(D_SSD), rows(D_BC), rows(D_BC), rows(LANES), const((1, LANES)), const((1, LANES))],
        out_shape=[jax.ShapeDtypeStruct((t, D_SSD), F32), jax.ShapeDtypeStruct((t, D_BC), F32),
                   jax.ShapeDtypeStruct((t, D_BC), F32), jax.ShapeDtypeStruct((t, LANES), F32),
                   jax.ShapeDtypeStruct((1, LANES), F32), jax.ShapeDtypeStruct((1, LANES), F32)],
        scratch_shapes=[pltpu.VMEM((N_PAIRS, LANES, SSD_STATE), F32)],
        compiler_params=_cparams(("arbitrary", "arbitrary")),
    )(xs, bm, cm, dtraw, dtraw.T, hprev, dy, bias_r, bias_c, alog_r, alog_c)


def _split_w_in(w_in):
    w_dt = jnp.pad(w_in[:, D_QKVZ + D_CONV:], ((0, 0), (0, LANES - N_HEADS)))
    return w_in[:, :D_QKVZ], w_in[:, D_QKVZ:D_QKVZ + D_CONV], w_dt


def mixer_fwd(hb, p, cosv, sinv, b):
    t = hb.shape[0]
    w_a, w_b, w_c = _split_w_in(p['w_in'])
    qkvz = mm("in_qkvz", [(hb, w_a, 'nn')], D_QKVZ)
    xbc = mm("in_xbc", [(hb, w_b, 'nn')], D_CONV)
    dtraw = mm("in_dt", [(hb, w_c, 'nn')], LANES)
    mixed, *lses = attn_fwd(qkvz, cosv, sinv, b)
    attn = attn_norm_fwd(mixed, p['attn_norm_w'])
    xs, bm, cm = conv_fwd(xbc, p['conv_w'], p['conv_b'])
    y, hprev = ssd_fwd(xs, bm, cm, dtraw, p['dt_bias'], p['a_log'], b)
    dskip = jnp.repeat(p['d_skip'].reshape(-1), HEAD_DIM).reshape(1, D_SSD)
    yg, = rowwise("ssd_gate", _gate, [y, xs, Op(qkvz, D_SSD, 3)], [dskip, p['ssd_norm_w']], [(t, D_SSD, BF16)])
    mix = mm("out_proj", [(attn, p['w_out'][:D_ATTN], 'nn'), (yg, p['w_out'][D_ATTN:], 'nn')], D_MODEL, out_dtype=BF16)
    res = dict(hb=hb, qkvz=qkvz, xbc=xbc, dtraw=dtraw, mixed=mixed, lses=lses, attn=attn, xs=xs, bm=bm, cm=cm,
               y=y, hprev=hprev, dskip=dskip, yg=yg, cosv=cosv, sinv=sinv)
    return mix, res


def mixer_bwd(r, p, dmix, dh_resid, b):
    t = dmix.shape[0]
    w_a, w_b, w_c = _split_w_in(p['w_in'])
    w_out = p['w_out']
    dattn = mm("out_bwd_dattn", [(dmix, w_out[:D_ATTN], 'nt')], D_ATTN)
    dyg = mm("out_bwd_dyg", [(dmix, w_out[D_ATTN:], 'nt')], D_SSD)
    dw_out = jnp.concatenate([mm_tn("out_bwd_dw_a", r['attn'], dmix, BF16),
                              mm_tn("out_bwd_dw_y", r['yg'], dmix, BF16)], axis=0)

    def gate_bwd(dy_, y_, xs_, z_, ds_, w_):
        _, vjp = jax.vjp(_gate, y_, xs_, z_, ds_, w_)
        return vjp(dy_)

    dy, dxs_a, dz, ddskip, dssd_norm = rowwise(
        "ssd_gate_bwd", gate_bwd, [dyg, r['y'], r['xs'], Op(r['qkvz'], D_SSD, 3)], [r['dskip'], p['ssd_norm_w']],
        [(t, D_SSD, F32), (t, D_SSD, F32), (t, D_SSD, BF16)], accs=[(1, D_SSD), (1, D_SSD)])
    dxs_b, dbm, dcm, ddtraw, ddt_bias, da_log = ssd_bwd(r['xs'], r['bm'], r['cm'], r['dtraw'], p['dt_bias'], p['a_log'],
                                                        r['hprev'], dy, b)
    dxbc, dconv_w, dconv_b = conv_bwd(r['xbc'], p['conv_w'], p['conv_b'], dxs_a, dxs_b, dbm, dcm)
    dmixed, dattn_norm = attn_norm_bwd(dattn, r['mixed'], p['attn_norm_w'])
    dq, dk, dv = attn_bwd(r['qkvz'], r['cosv'], r['sinv'], dmixed, r['mixed'], r['lses'], b)
    wq, wk, wv, wz = (w_a[:, i * D_ATTN:(i + 1) * D_ATTN] for i in range(4))
    dh = mm("in_bwd_dh", [(dq, wq, 'nt'), (dk, wk, 'nt'), (dv, wv, 'nt'), (dz, wz, 'nt'), (dxbc, w_b, 'nt'),
                          (ddtraw, w_c, 'nt')], D_MODEL, add=dh_resid, tn=512)
    h = r['hb']
    dw_in = jnp.concatenate([mm_tn_cat("in_bwd_dw_qkvz", h, [dq, dk, dv, dz], BF16),
                             mm_tn_cat("in_bwd_dw_xbc_dt", h, [dxbc, ddtraw], BF16)[:, :D_CONV + N_HEADS]], axis=1)
    head_sum = lambda v: v.reshape(N_HEADS, HEAD_DIM).sum(axis=1).reshape(1, N_HEADS)
    grads = dict(w_in=dw_in, w_out=dw_out, conv_w=dconv_w, conv_b=dconv_b, dt_bias=ddt_bias[:, :N_HEADS],
                 a_log=da_log[:, :N_HEADS], d_skip=head_sum(ddskip), attn_norm_w=dattn_norm, ssd_norm_w=dssd_norm)
    return dh, grads


FFN2_KEYS = ('ffn2_gate', 'ffn2_up', 'ffn2_down')
MIXER_KEYS = ('w_in', 'conv_w', 'w_out')
FFN_COL = ('ffn1_gate', 'ffn1_up', 'ffn2_gate', 'ffn2_up')
FFN_ROW = ('ffn1_down', 'ffn2_down')
CONV_W_COMM = (8, 2 * LANES)
SMALL = 'small'


def comm_shape(k, shapes):
    if k in FFN_COL:
        return (D_MODEL, FF_PAD)
    if k in FFN_ROW:
        return (FF_PAD, D_MODEL)
    if k == 'conv_w':
        return CONV_W_COMM
    return tuple(shapes[k][1:])


def to_comm(k, vals, shapes):
    a = vals[k].reshape(shapes[k][1:])
    r_, c_ = comm_shape(k, shapes)
    return jnp.pad(a, ((0, r_ - a.shape[0]), (0, c_ - a.shape[1])))


SMALL_ROWS, SMALL_COLS = 16, D_CONV


def pack_small(small):
    rows = [jnp.pad(small[r].reshape(1, -1), ((0, 0), (0, SMALL_COLS - small[r].size))) for r in REPLICATED]
    return jnp.concatenate(rows + [jnp.zeros((SMALL_ROWS - len(rows), SMALL_COLS), F32)], axis=0)


def full_weight(k, g):
    if k in FFN_COL:
        return g
    if k == 'conv_w':
        return jnp.transpose(g[:, :CONV_WIDTH, :D_CONV // N_DEV], (1, 0, 2)).reshape(CONV_WIDTH, D_CONV)
    return g.reshape(N_DEV * g.shape[1], g.shape[2])


def grad_shards(k, g):
    if k in FFN_COL:
        return g
    if k == 'conv_w':
        s = jnp.transpose(g.reshape(CONV_WIDTH, N_DEV, D_CONV // N_DEV), (1, 0, 2))
        return jnp.pad(s, ((0, 0), (0, CONV_W_COMM[0] - CONV_WIDTH), (0, CONV_W_COMM[1] - D_CONV // N_DEV)))
    return g.reshape(N_DEV, g.shape[0] // N_DEV, g.shape[1])


def _flip(v, bit):
    return 1 - v if bit else v


N_PEER_COPIES = N_DEV - 1


def _comm_call(name, body, arrs, out_shape):
    n = len(arrs)
    return pl.pallas_call(
        functools.partial(body, n), name=name, out_shape=out_shape,
        in_specs=[pl.BlockSpec(memory_space=pl.ANY)] * n, out_specs=[pl.BlockSpec(memory_space=pl.ANY)] * n,
        scratch_shapes=[pltpu.SemaphoreType.DMA((n * N_PEER_COPIES,)), pltpu.SemaphoreType.DMA((n * N_PEER_COPIES,)),
                        pltpu.SemaphoreType.DMA((n,))],
    )(*arrs)


def _blk(ref, idx, by_cols):
    if not by_cols:
        return ref.at[idx]
    c = ref.shape[1] // N_DEV
    return ref.at[:, pl.ds(pl.multiple_of(idx * c, LANES), c)]


def _blocked_shape(a, by_cols):
    return (a.shape[0], N_DEV * a.shape[1]) if by_cols else (N_DEV,) + a.shape


def all_gather(arrs, by_cols):
    def body(n, *refs):
        x_refs, out_refs, (send_sems, recv_sems, local_sems) = refs[:n], refs[n:2 * n], refs[2 * n:]
        x, y, c = lax.axis_index("x"), lax.axis_index("y"), lax.axis_index("c")
        me, sibling = (x, y, c), (x, y, 1 - c)
        chips = [(1 - x, y), (x, 1 - y), (1 - x, 1 - y)]

        def copy(a, k, block, to, src=None):
            px, py, pc = block
            dst = _blk(out_refs[a], 4 * px + 2 * py + pc, by_cols[a])
            return pltpu.make_async_remote_copy(
                src_ref=dst if src is None else src, dst_ref=dst, send_sem=send_sems.at[a * N_PEER_COPIES + k],
                recv_sem=recv_sems.at[a * N_PEER_COPIES + k], device_id=to, device_id_type=MESH)

        mine = [pltpu.make_async_copy(x_refs[a], _blk(out_refs[a], 4 * x + 2 * y + c, by_cols[a]), local_sems.at[a])
                for a in range(n)]
        started = []
        for a in range(n):
            mine[a].start()
            first = [copy(a, 0, me, sibling, src=x_refs[a])]
            first += [copy(a, 1 + j, me, (*chip, c), src=x_refs[a]) for j, chip in enumerate(chips)]
            for cp in first:
                cp.start()
            started += first
        for j, chip in enumerate(chips):
            for a in range(n):
                copy(a, 1 + j, (*chip, c), me).wait_recv()
                cp = copy(a, 4 + j, (*chip, c), sibling)
                cp.start()
                started.append(cp)
        for a in range(n):
            copy(a, 0, sibling, me).wait_recv()
            for j, chip in enumerate(chips):
                copy(a, 4 + j, (*chip, 1 - c), me).wait_recv()
        for cp in started:
            cp.wait_send()
        for cp in mine:
            cp.wait()

    return _comm_call("all_gather_weights", body, arrs,
                      [jax.ShapeDtypeStruct(_blocked_shape(a, bc), a.dtype) for a, bc in zip(arrs, by_cols)])


def _landing_shape(a, by_cols):
    return (N_DEV, a.shape[0], a.shape[1] // N_DEV) if by_cols else a.shape


_HBM = pl.BlockSpec(memory_space=pltpu.HBM)
_SEM = pl.BlockSpec(memory_space=pltpu.SEMAPHORE)
_EFFECT = pltpu.SideEffectType.DATAFLOW_SIDE_EFFECTING


def _peer(k):
    x, y, c = lax.axis_index("x"), lax.axis_index("y"), lax.axis_index("c")
    return _flip(x, k & 4), _flip(y, k & 2), _flip(c, k & 1)


def _my_index():
    return 4 * lax.axis_index("x") + 2 * lax.axis_index("y") + lax.axis_index("c")


def _split_copies(mode, by_cols, src_refs, land_refs, send_sems, recv_sems):
    me = _my_index()
    out = []
    for a, bc in enumerate(by_cols):
        for k in range(1, N_DEV):
            px, py, pc = _peer(k)
            src = _blk(src_refs[a], 4 * px + 2 * py + pc, bc) if mode == 'scatter' else src_refs[a]
            dst = land_refs[a].at[me] if mode == 'scatter' else _blk(land_refs[a], me, bc)
            out.append(pltpu.make_async_remote_copy(
                src_ref=src, dst_ref=dst, send_sem=send_sems.at[a * N_PEER_COPIES + k - 1],
                recv_sem=recv_sems.at[a * N_PEER_COPIES + k - 1], device_id=(px, py, pc), device_id_type=MESH))
    return out


def exchange_start(name, mode, srcs, by_cols):
    n = len(srcs)
    lands = [lax.empty(_landing_shape(s, bc) if mode == 'scatter' else _blocked_shape(s, bc), s.dtype)
             for s, bc in zip(srcs, by_cols)]

    def body(*refs):
        src_refs, land_refs, send_sems, recv_sems = refs[:n], refs[n:2 * n], refs[2 * n], refs[2 * n + 1]
        for cp in _split_copies(mode, by_cols, src_refs, land_refs, send_sems, recv_sems):
            cp.start()
        refs[-1][...] = jnp.zeros(refs[-1].shape, F32)

    sems = pltpu.SemaphoreType.DMA((n * N_PEER_COPIES,))
    res = pl.pallas_call(
        body, name=name,
        out_shape=(sems, sems, *[pltpu.HBM(a.shape, a.dtype) for a in srcs + lands], jax.ShapeDtypeStruct((8, LANES), F32)),
        in_specs=(_HBM,) * (2 * n), out_specs=(_SEM, _SEM, *(_HBM,) * (2 * n), pl.BlockSpec(memory_space=pltpu.VMEM)),
        input_output_aliases={i: 2 + i for i in range(2 * n)},
        compiler_params=pltpu.CompilerParams(has_side_effects=_EFFECT),
    )(*[pltpu.with_memory_space_constraint(a, pltpu.HBM) for a in srcs + lands])
    return (mode, by_cols, res[:-1]), res[-1]


def exchange_wait(name, handles, after):
    mode, by_cols, (send_sems, recv_sems, *bufs) = handles
    n = len(by_cols)

    def body(*refs):
        src_refs, land_refs, s_sems, r_sems = refs[:n], refs[n:2 * n], refs[2 * n], refs[2 * n + 1]
        for cp in _split_copies(mode, by_cols, src_refs, land_refs, s_sems, r_sems):
            cp.wait_send()
            cp.wait_recv()

    res = pl.pallas_call(
        body, name=name, out_shape=tuple(pltpu.HBM(a.shape, a.dtype) for a in bufs),
        in_specs=(*(_HBM,) * (2 * n), _SEM, _SEM, pl.BlockSpec(memory_space=pl.ANY)), out_specs=(_HBM,) * (2 * n),
        input_output_aliases={i: i for i in range(2 * n)},
        compiler_params=pltpu.CompilerParams(has_side_effects=_EFFECT),
    )(*bufs, send_sems, recv_sems, after)
    me, out = _my_index(), []
    for src, land, bc in zip(res[:n], res[n:], by_cols):
        if mode == 'scatter':
            c = land.shape[2]
            own = lax.dynamic_slice(src, (0, me * c), (src.shape[0], c)) if bc else lax.dynamic_index_in_dim(src, me, 0, False)
            out.append(lax.dynamic_update_slice(land, own[None], (me, 0, 0)))
        elif bc:
            out.append(lax.dynamic_update_slice(land, src, (0, me * src.shape[1])))
        else:
            out.append(lax.dynamic_update_slice(land, src[None], (me, 0, 0)))
    return out


def _adamw_math(g, w, m, v):
    c1 = 1.0 / (1.0 - ADAM_B1 ** ADAM_STEP)
    c2 = 1.0 / (1.0 - ADAM_B2 ** ADAM_STEP)
    m = ADAM_B1 * m + (1.0 - ADAM_B1) * g
    v = ADAM_B2 * v + (1.0 - ADAM_B2) * jnp.square(g)
    return g, -ADAM_LR * ((m * c1) / (jnp.sqrt(v * c2) + ADAM_EPS) + ADAM_WD * w), m, v


def adamw(name, recv, w, m, v, tm):
    _, rows, cols = w.shape
    tm = min(tm, rows)

    def body(*refs):
        g = refs[0][0:tm, 0:cols].astype(F32)
        for s in range(1, N_DEV):
            g = g + refs[s][0:tm, 0:cols].astype(F32)
        res = _adamw_math(g, *[r[...] for r in refs[N_DEV:N_DEV + 3]])
        for r, val in zip(refs[N_DEV + 3:], res):
            r[...] = val

    part = lambda s: pl.BlockSpec((None, recv.shape[1] if tm == rows else tm, recv.shape[2]), lambda i: (s, i, 0))
    tile = pl.BlockSpec((None, tm, cols), lambda i: (0, i, 0))
    return pl.pallas_call(
        body, name=name, grid=(rows // tm,), in_specs=[part(s) for s in range(N_DEV)] + [tile] * 3, out_specs=[tile] * 4,
        out_shape=[jax.ShapeDtypeStruct((1, rows, cols), F32)] * 4, compiler_params=_cparams(("parallel",)),
    )(*[recv] * N_DEV, w, m, v)


def adamw_small(recv, wl, ml, vl):
    n = len(REPLICATED)

    def body(recv_ref, *refs):
        g = recv_ref[0]
        for s in range(1, N_DEV):
            g = g + recv_ref[s]
        for r in range(n):
            w, m, v = (refs[j * n + r][...] for j in range(3))
            for j, val in enumerate(_adamw_math(g[r:r + 1, :w.shape[1]], w, m, v)):
                refs[(3 + j) * n + r][...] = val

    arrs = [d[k].reshape(1, -1) for d in (wl, ml, vl) for k in REPLICATED]
    res = pl.pallas_call(
        body, name="adamw_small", out_shape=[jax.ShapeDtypeStruct(a.shape, F32) for a in arrs[:n]] * 4,
    )(recv, *arrs)
    return [{k: res[j * n + r].reshape(wl[k].shape) for r, k in enumerate(REPLICATED)} for j in range(4)]


ADAMW_TM = {'ffn1_gate': 256, 'ffn1_up': 256, 'ffn2_gate': 256, 'ffn2_up': 256, 'w_in': 32}


def kernel(x, positions, ln1_g, ln1_b, ffn1_gate, ffn1_up, ffn1_down, w_in, conv_w, conv_b, dt_bias, a_log, d_skip, attn_norm_w, ssd_norm_w, w_out, ln2_g, ln2_b, ffn2_gate, ffn2_up, ffn2_down, ln3_g, ln3_b, loss_target, m_ln1_g, m_ln1_b, m_ffn1_gate, m_ffn1_up, m_ffn1_down, m_w_in, m_conv_w, m_conv_b, m_dt_bias, m_a_log, m_d_skip, m_attn_norm_w, m_ssd_norm_w, m_w_out, m_ln2_g, m_ln2_b, m_ffn2_gate, m_ffn2_up, m_ffn2_down, m_ln3_g, m_ln3_b, v_ln1_g, v_ln1_b, v_ffn1_gate, v_ffn1_up, v_ffn1_down, v_w_in, v_conv_w, v_conv_b, v_dt_bias, v_a_log, v_d_skip, v_attn_norm_w, v_ssd_norm_w, v_w_out, v_ln2_g, v_ln2_b, v_ffn2_gate, v_ffn2_up, v_ffn2_down, v_ln3_g, v_ln3_b):
    args = dict(locals())
    wl = {k: args[k] for k in WEIGHTS}
    ml = {k: args["m_" + k] for k in WEIGHTS}
    vl = {k: args["v_" + k] for k in WEIGHTS}
    shapes = {k: wl[k].shape for k in WEIGHTS}
    b, s, dm = x.shape
    t = b * s

    sent = {k: to_comm(k, wl, shapes).astype(F32 if k == 'conv_w' else BF16) for k in SHARDED}
    by_cols = lambda keys: [k in FFN_COL for k in keys]
    gate, up = all_gather([sent['ffn1_gate'], sent['ffn1_up']], [True] * 2)
    (gate, up), sent = lax.optimization_barrier(((gate, up), sent))
    p = {'ffn1_gate': gate, 'ffn1_up': up}
    gather_down, token_d = exchange_start("gather_ffn1_down_start", 'gather', [sent['ffn1_down']], [False])
    sent['w_in'] = sent['w_in'] + token_d[0, 0].astype(BF16)
    gather_mixer, token_m = exchange_start("gather_mixer_start", 'gather', [sent[k] for k in MIXER_KEYS], by_cols(MIXER_KEYS))
    sent['ffn2_gate'] = sent['ffn2_gate'] + token_m[0, 0].astype(BF16)
    gather_ffn2, token_f = exchange_start("gather_ffn2_start", 'gather', [sent[k] for k in FFN2_KEYS], by_cols(FFN2_KEYS))
    for k in REPLICATED:
        p[k] = wl[k].reshape(1, -1)

    x2 = x.reshape(t, dm)
    cosv, sinv = rope_tables(positions)
    g1, u1, a1, at1 = ffn_gate_up("ffn1_gate_up", x2, p['ffn1_gate'], p['ffn1_up'], after=(token_d, token_m, token_f))
    p['ffn1_down'] = full_weight('ffn1_down', exchange_wait("gather_ffn1_down_wait", gather_down, a1)[0])
    f1, res1 = mm("ffn1_down", [(a1, p['ffn1_down'], 'nn')], D_MODEL, out_dtype=BF16), (x2, g1, u1, at1)
    h1, h1b = resid_ln_fwd("ln1", 0.5, x2, f1, p['ln1_g'], p['ln1_b'])
    for k, g in zip(MIXER_KEYS, exchange_wait("gather_mixer_wait", gather_mixer, h1b)):
        p[k] = full_weight(k, g)
    mix, resm = mixer_fwd(h1b, p, cosv, sinv, b)
    h2, h2b = resid_ln_fwd("ln2", 1.0, h1, mix, p['ln2_g'], p['ln2_b'])
    for k, g in zip(FFN2_KEYS, exchange_wait("gather_ffn2_wait", gather_ffn2, h2b)):
        p[k] = full_weight(k, g)
    f2, res3 = ffn_fwd("ffn2", h2b, p['ffn2_gate'], p['ffn2_up'], p['ffn2_down'])

    small, full = {}, {}
    dh2_res, df2, small['ln3_g'], small['ln3_b'], sq = ln_loss_bwd("ln3_loss_bwd", h2, f2, loss_target.reshape(t, dm),
                                                                   p['ln3_g'], p['ln3_b'])
    loss = lax.psum(jnp.sum(sq) * (0.5 / dm), AXES)

    dh2, full['ffn2_gate'], full['ffn2_up'], full['ffn2_down'] = ffn_bwd("ffn2", res3, p['ffn2_gate'], p['ffn2_up'],
                                                                       p['ffn2_down'], df2, dh2_res)
    ffn2_exchange, token = exchange_start("grads_ffn2_start", 'scatter', [grad_shards(k, full[k]) for k in FFN2_KEYS],
                                          by_cols(FFN2_KEYS))
    dh1_res, dmix, small['ln2_g'], small['ln2_b'] = resid_ln_bwd("ln2_bwd", 1.0, h1, mix, p['ln2_g'] + token[:1, :1],
                                                                 p['ln2_b'], dh2)
    dh1, gm = mixer_bwd(resm, p, dmix, dh1_res, b)
    for k in ('conv_b', 'dt_bias', 'a_log', 'd_skip', 'attn_norm_w', 'ssd_norm_w'):
        small[k] = gm[k]
    mixer_exchange, token = exchange_start("grads_mixer_start", 'scatter', [grad_shards(k, gm[k]) for k in MIXER_KEYS],
                                           by_cols(MIXER_KEYS))
    dx_res, df1, small['ln1_g'], small['ln1_b'] = resid_ln_bwd("ln1_bwd", 0.5, x2, f1, p['ln1_g'] + token[:1, :1],
                                                               p['ln1_b'], dh1)
    hb, g, u, at = res1
    small_part = pack_small(small)
    dg, du = ffn_da_act("ffn1_bwd_da_act", df1, p['ffn1_down'], g, u)
    dwd = mm_acc("ffn1_bwd_dwd", at, df1, BF16, after=dg)
    down_exchange, token = exchange_start("grads_ffn1_down_start", 'scatter', [
        grad_shards('ffn1_down', dwd), jnp.broadcast_to(small_part[None], (N_DEV,) + small_part.shape)], [False, False])
    dwg = mm_tn("ffn1_bwd_dwg", hb, dg, BF16, after=token)
    gate_exchange, token = exchange_start("grads_ffn1_gate_start", 'scatter', [grad_shards('ffn1_gate', dwg)], [True])
    dwu = mm_tn("ffn1_bwd_dwu", hb, du, BF16, after=token)
    up_exchange, token = exchange_start("grads_ffn1_up_start", 'scatter', [grad_shards('ffn1_up', dwu)], [True])
    dx = mm("ffn1_bwd_dh", [(dg, p['ffn1_gate'], 'nt'), (du, p['ffn1_up'], 'nt')], D_MODEL, add=dx_res, tn=512, after=token)
    recv = {}
    for keys, name, ex in (((FFN2_KEYS), "grads_ffn2_wait", ffn2_exchange), (MIXER_KEYS, "grads_mixer_wait", mixer_exchange),
                           (('ffn1_down', SMALL), "grads_ffn1_down_wait", down_exchange),
                           (('ffn1_gate',), "grads_ffn1_gate_wait", gate_exchange),
                           (('ffn1_up',), "grads_ffn1_up_wait", up_exchange)):
        recv.update(zip(keys, exchange_wait(name, ex, dx)))
    outs = adamw_small(recv.pop(SMALL), wl, ml, vl)
    for k, r in recv.items():
        for o, a in zip(outs, adamw(f"adamw_{k}", r, wl[k], ml[k], vl[k], ADAMW_TM.get(k, shapes[k][1]))):
            o[k] = a
    return (loss, dx.reshape(b, s, dm), *[o[k] for o in outs for k in WEIGHTS])
```
